```python
import math
import jax, jax.numpy as jnp
from jax import lax
import numpy as np

D_MODEL = 2048
BATCH = 8
SEQ = 2048
DEPTH = 1

MIX_WIDTH = D_MODEL
SSM_WIDTH = MIX_WIDTH // 2
SSM_GROUP = 16
SSM_GROUPS = SSM_WIDTH // SSM_GROUP
SSM_STATE = 64
DT_MIN = 1e-3
DT_MAX = 1e-1
QK_NOPE_DIM = 128
QK_ROPE_DIM = 64
V_HEAD_DIM = 128
MLA_WIDTH = MIX_WIDTH - SSM_WIDTH
MLA_HEADS = MLA_WIDTH // V_HEAD_DIM
Q_LORA_RANK = D_MODEL // 4
KV_LORA_RANK = D_MODEL // 8
ROPE_THETA = 10000.0
Q_BLOCK = 128
IN_WIDTH = SSM_WIDTH + Q_LORA_RANK + KV_LORA_RANK + QK_ROPE_DIM
D_FF = ((8 * D_MODEL // 3 + 255) // 256) * 256
CONV_WIDTH = 3
RMS_EPS = 1e-6

kernel_name = 'hybrid_s5_mla_convffn_layer'


def _rmsnorm(x, w):
    xf = x.astype(jnp.float32)
    y = xf * lax.rsqrt(jnp.mean(xf * xf, axis=-1, keepdims=True) + RMS_EPS)
    return (y * w.astype(jnp.float32)).astype(x.dtype)


def _rope_tables(positions, dtype):
    inv_freq = ROPE_THETA ** (-jnp.arange(0, QK_ROPE_DIM, 2, dtype=jnp.float32) / QK_ROPE_DIM)
    ang = positions.astype(jnp.float32)[..., None] * inv_freq
    return jnp.cos(ang).astype(dtype), jnp.sin(ang).astype(dtype)


def _rope(x, cos, sin):
    x1, x2 = jnp.split(x, 2, axis=-1)
    return jnp.concatenate([x1 * cos - x2 * sin, x1 * sin + x2 * cos], axis=-1)


def _ssm_combine(left, right):
    ar1, ai1, br1, bi1 = left
    ar2, ai2, br2, bi2 = right
    ar = ar2 * ar1 - ai2 * ai1
    ai = ar2 * ai1 + ai2 * ar1
    br = ar2 * br1 - ai2 * bi1 + br2
    bi = ar2 * bi1 + ai2 * br1 + bi2
    return ar, ai, br, bi


def _s5_group(u, lam_re, lam_im, log_dt, b_re, b_im, c_re, c_im, d_skip, w_glu, b_glu):
    bsz, seq, _ = u.shape
    ug = u.astype(jnp.float32).reshape(bsz, seq, SSM_GROUPS, SSM_GROUP)
    lr = lam_re.astype(jnp.float32)
    li = lam_im.astype(jnp.float32)
    dt = jnp.exp(log_dt.astype(jnp.float32))[:, None]
    mag = jnp.exp(lr * dt)
    abar_re = mag * jnp.cos(li * dt)
    abar_im = mag * jnp.sin(li * dt)
    nr, ni = abar_re - 1.0, abar_im
    den = lr * lr + li * li
    zr = (nr * lr + ni * li) / den
    zi = (ni * lr - nr * li) / den
    bre = b_re.astype(jnp.float32)
    bim = b_im.astype(jnp.float32)
    bbar_re = zr[..., None] * bre - zi[..., None] * bim
    bbar_im = zr[..., None] * bim + zi[..., None] * bre
    bu_re = jnp.einsum('blgh,gph->lbgp', ug, bbar_re)
    bu_im = jnp.einsum('blgh,gph->lbgp', ug, bbar_im)
    a_re = jnp.broadcast_to(abar_re, (seq, 1, SSM_GROUPS, SSM_STATE))
    a_im = jnp.broadcast_to(abar_im, (seq, 1, SSM_GROUPS, SSM_STATE))
    _, _, s_re, s_im = lax.associative_scan(_ssm_combine, (a_re, a_im, bu_re, bu_im), axis=0)
    y = (jnp.einsum('lbgp,ghp->blgh', s_re, c_re.astype(jnp.float32))
         - jnp.einsum('lbgp,ghp->blgh', s_im, c_im.astype(jnp.float32))
         + d_skip.astype(jnp.float32).reshape(SSM_GROUPS, SSM_GROUP) * ug)
    y = jax.nn.gelu(y.reshape(bsz, seq, SSM_WIDTH)).astype(u.dtype)
    return y * jax.nn.sigmoid(y @ w_glu + b_glu)


def _mla_group(c_q, c_kv, k_pe, positions, q_norm_w, w_uq, kv_norm_w, w_ukv):
    bsz, seq, _ = c_q.shape
    q = (_rmsnorm(c_q, q_norm_w) @ w_uq).reshape(bsz, seq, MLA_HEADS, QK_NOPE_DIM + QK_ROPE_DIM)
    q_nope, q_pe = q[..., :QK_NOPE_DIM], q[..., QK_NOPE_DIM:]
    kv = (_rmsnorm(c_kv, kv_norm_w) @ w_ukv).reshape(bsz, seq, MLA_HEADS, QK_NOPE_DIM + V_HEAD_DIM)
    k_nope, v = kv[..., :QK_NOPE_DIM], kv[..., QK_NOPE_DIM:]
    cos, sin = _rope_tables(positions, q.dtype)
    q_pe = _rope(q_pe, cos[:, :, None, :], sin[:, :, None, :])
    k_pe = _rope(k_pe, cos, sin)
    scale = (QK_NOPE_DIM + QK_ROPE_DIM) ** -0.5
    neg = jnp.finfo(jnp.float32).min
    outs = []
    for blk in range(seq // Q_BLOCK):
        q0 = blk * Q_BLOCK
        kend = q0 + Q_BLOCK
        s = (jnp.einsum('bqhd,bkhd->bhqk', q_nope[:, q0:kend], k_nope[:, :kend])
             + jnp.einsum('bqhr,bkr->bhqk', q_pe[:, q0:kend], k_pe[:, :kend]))
        s = s.astype(jnp.float32) * scale
        causal = jnp.arange(kend)[None, :] <= (q0 + jnp.arange(Q_BLOCK))[:, None]
        s = jnp.where(causal, s, neg)
        p = jax.nn.softmax(s, axis=-1).astype(v.dtype)
        outs.append(jnp.einsum('bhqk,bkhd->bqhd', p, v[:, :kend]))
    o = jnp.concatenate(outs, axis=1)
    return o.reshape(bsz, seq, MLA_WIDTH)


def _conv_ffn(h, w_up, conv_w, conv_b, w_down):
    a = h @ w_up
    a = lax.conv_general_dilated(a, conv_w[:, None, :], window_strides=(1,),
                                 padding=[(CONV_WIDTH - 1, 0)],
                                 dimension_numbers=('NWC', 'WIO', 'NWC'),
                                 feature_group_count=2 * D_FF) + conv_b
    gate, val = jnp.split(a, 2, axis=-1)
    return (jax.nn.silu(gate) * val) @ w_down


def _fwd_setup_inputs(seed: int = 0) -> dict:
    key = jax.random.key(seed)
    ks = jax.random.split(key, 32)
    f32 = jnp.float32

    def nrm(k, shape, scale):
        return jax.random.normal(k, (DEPTH,) + shape, f32) * scale

    def gain(k, n):
        return 1.0 + 0.02 * jax.random.normal(k, (DEPTH, n), f32)

    x = jax.random.normal(ks[0], (BATCH, SEQ, D_MODEL), f32)
    offs = jax.random.randint(ks[1], (BATCH, 1), 0, 1024, dtype=jnp.int32)
    positions = offs + jnp.arange(SEQ, dtype=jnp.int32)[None, :]
    lam_re = -0.5 + 0.01 * jax.random.normal(ks[4], (DEPTH, SSM_GROUPS, SSM_STATE), f32)
    lam_im = (math.pi * jnp.arange(SSM_STATE, dtype=f32))[None, None, :] + 0.01 * jax.random.normal(ks[5], (DEPTH, SSM_GROUPS, SSM_STATE), f32)
    log_dt = jax.random.uniform(ks[6], (DEPTH, SSM_GROUPS), f32, math.log(DT_MIN), math.log(DT_MAX))
    return {
        'x': x,
        'positions': positions,
        'attn_norm_w': gain(ks[2], D_MODEL),
        'w_in': nrm(ks[3], (D_MODEL, IN_WIDTH), D_MODEL ** -0.5),
        'ssm_lambda_re': lam_re,
        'ssm_lambda_im': lam_im,
        'ssm_log_dt': log_dt,
        'ssm_b_re': nrm(ks[7], (SSM_GROUPS, SSM_STATE, SSM_GROUP), (2 * SSM_GROUP) ** -0.5),
        'ssm_b_im': nrm(ks[8], (SSM_GROUPS, SSM_STATE, SSM_GROUP), (2 * SSM_GROUP) ** -0.5),
        'ssm_c_re': nrm(ks[9], (SSM_GROUPS, SSM_GROUP, SSM_STATE), (2 * SSM_STATE) ** -0.5),
        'ssm_c_im': nrm(ks[10], (SSM_GROUPS, SSM_GROUP, SSM_STATE), (2 * SSM_STATE) ** -0.5),
        'ssm_d': nrm(ks[11], (SSM_WIDTH,), 1.0),
        'ssm_w_glu': nrm(ks[12], (SSM_WIDTH, SSM_WIDTH), SSM_WIDTH ** -0.5),
        'ssm_b_glu': nrm(ks[13], (SSM_WIDTH,), 0.01),
        'mla_q_norm_w': gain(ks[14], Q_LORA_RANK),
        'mla_w_uq': nrm(ks[15], (Q_LORA_RANK, MLA_HEADS * (QK_NOPE_DIM + QK_ROPE_DIM)), Q_LORA_RANK ** -0.5),
        'mla_kv_norm_w': gain(ks[16], KV_LORA_RANK),
        'mla_w_ukv': nrm(ks[17], (KV_LORA_RANK, MLA_HEADS * (QK_NOPE_DIM + V_HEAD_DIM)), KV_LORA_RANK ** -0.5),
        'ssm_out_norm_w': gain(ks[18], SSM_WIDTH),
        'mla_out_norm_w': gain(ks[19], MLA_WIDTH),
        'w_out': nrm(ks[20], (MIX_WIDTH, D_MODEL), MIX_WIDTH ** -0.5),
        'ffn_norm_w': gain(ks[21], D_MODEL),
        'ffn_w_up': nrm(ks[22], (D_MODEL, 2 * D_FF), D_MODEL ** -0.5),
        'ffn_conv_w': nrm(ks[23], (CONV_WIDTH, 2 * D_FF), CONV_WIDTH ** -0.5),
        'ffn_conv_b': nrm(ks[24], (2 * D_FF,), 0.01),
        'ffn_w_down': nrm(ks[25], (D_FF, D_MODEL), D_FF ** -0.5),
        'final_norm_w': 1.0 + 0.02 * jax.random.normal(ks[26], (D_MODEL,), f32),
    }


def _fwd_reference(x, positions, attn_norm_w, w_in, ssm_lambda_re, ssm_lambda_im, ssm_log_dt,
              ssm_b_re, ssm_b_im, ssm_c_re, ssm_c_im, ssm_d, ssm_w_glu, ssm_b_glu,
              mla_q_norm_w, mla_w_uq, mla_kv_norm_w, mla_w_ukv, ssm_out_norm_w,
              mla_out_norm_w, w_out, ffn_norm_w, ffn_w_up, ffn_conv_w, ffn_conv_b,
              ffn_w_down, final_norm_w):
    split_at = [SSM_WIDTH, SSM_WIDTH + Q_LORA_RANK, SSM_WIDTH + Q_LORA_RANK + KV_LORA_RANK]
    h = x
    for l in range(DEPTH):
        hn = _rmsnorm(h, attn_norm_w[l])
        proj = hn @ w_in[l]
        u, c_q, c_kv, k_pe = jnp.split(proj, split_at, axis=-1)
        y_ssm = _s5_group(u, ssm_lambda_re[l], ssm_lambda_im[l], ssm_log_dt[l],
                          ssm_b_re[l], ssm_b_im[l], ssm_c_re[l], ssm_c_im[l],
                          ssm_d[l], ssm_w_glu[l], ssm_b_glu[l])
        y_mla = _mla_group(c_q, c_kv, k_pe, positions, mla_q_norm_w[l], mla_w_uq[l],
                           mla_kv_norm_w[l], mla_w_ukv[l])
        y = jnp.concatenate([_rmsnorm(y_ssm, ssm_out_norm_w[l]),
                             _rmsnorm(y_mla, mla_out_norm_w[l])], axis=-1)
        h = h + y @ w_out[l]
        h = h + _conv_ffn(_rmsnorm(h, ffn_norm_w[l]), ffn_w_up[l], ffn_conv_w[l],
                          ffn_conv_b[l], ffn_w_down[l])
    return _rmsnorm(h, final_norm_w)


import jax as _jax
import jax.numpy as _jnp

TWIN_FORMAT = 'train_step'
FWD_PARAMS = ['x', 'positions', 'attn_norm_w', 'w_in', 'ssm_lambda_re', 'ssm_lambda_im', 'ssm_log_dt', 'ssm_b_re', 'ssm_b_im', 'ssm_c_re', 'ssm_c_im', 'ssm_d', 'ssm_w_glu', 'ssm_b_glu', 'mla_q_norm_w', 'mla_w_uq', 'mla_kv_norm_w', 'mla_w_ukv', 'ssm_out_norm_w', 'mla_out_norm_w', 'w_out', 'ffn_norm_w', 'ffn_w_up', 'ffn_conv_w', 'ffn_conv_b', 'ffn_w_down', 'final_norm_w']
TWIN_WEIGHTS = ['attn_norm_w', 'w_in', 'ssm_lambda_re', 'ssm_lambda_im', 'ssm_log_dt', 'ssm_b_re', 'ssm_b_im', 'ssm_c_re', 'ssm_c_im', 'ssm_d', 'ssm_w_glu', 'ssm_b_glu', 'mla_q_norm_w', 'mla_w_uq', 'mla_kv_norm_w', 'mla_w_ukv', 'ssm_out_norm_w', 'mla_out_norm_w', 'w_out', 'ffn_norm_w', 'ffn_w_up', 'ffn_conv_w', 'ffn_conv_b', 'ffn_w_down', 'final_norm_w']
TWIN_DIFF_INPUT = 'x'
TWIN_INPUTS = ['x', 'positions', 'attn_norm_w', 'w_in', 'ssm_lambda_re', 'ssm_lambda_im', 'ssm_log_dt', 'ssm_b_re', 'ssm_b_im', 'ssm_c_re', 'ssm_c_im', 'ssm_d', 'ssm_w_glu', 'ssm_b_glu', 'mla_q_norm_w', 'mla_w_uq', 'mla_kv_norm_w', 'mla_w_ukv', 'ssm_out_norm_w', 'mla_out_norm_w', 'w_out', 'ffn_norm_w', 'ffn_w_up', 'ffn_conv_w', 'ffn_conv_b', 'ffn_w_down', 'final_norm_w', 'loss_target', 'm_attn_norm_w', 'm_w_in', 'm_ssm_lambda_re', 'm_ssm_lambda_im', 'm_ssm_log_dt', 'm_ssm_b_re', 'm_ssm_b_im', 'm_ssm_c_re', 'm_ssm_c_im', 'm_ssm_d', 'm_ssm_w_glu', 'm_ssm_b_glu', 'm_mla_q_norm_w', 'm_mla_w_uq', 'm_mla_kv_norm_w', 'm_mla_w_ukv', 'm_ssm_out_norm_w', 'm_mla_out_norm_w', 'm_w_out', 'm_ffn_norm_w', 'm_ffn_w_up', 'm_ffn_conv_w', 'm_ffn_conv_b', 'm_ffn_w_down', 'm_final_norm_w', 'v_attn_norm_w', 'v_w_in', 'v_ssm_lambda_re', 'v_ssm_lambda_im', 'v_ssm_log_dt', 'v_ssm_b_re', 'v_ssm_b_im', 'v_ssm_c_re', 'v_ssm_c_im', 'v_ssm_d', 'v_ssm_w_glu', 'v_ssm_b_glu', 'v_mla_q_norm_w', 'v_mla_w_uq', 'v_mla_kv_norm_w', 'v_mla_w_ukv', 'v_ssm_out_norm_w', 'v_mla_out_norm_w', 'v_w_out', 'v_ffn_norm_w', 'v_ffn_w_up', 'v_ffn_conv_w', 'v_ffn_conv_b', 'v_ffn_w_down', 'v_final_norm_w']
TWIN_OUTPUTS = ['loss', 'grad_x', 'grad_attn_norm_w', 'grad_w_in', 'grad_ssm_lambda_re', 'grad_ssm_lambda_im', 'grad_ssm_log_dt', 'grad_ssm_b_re', 'grad_ssm_b_im', 'grad_ssm_c_re', 'grad_ssm_c_im', 'grad_ssm_d', 'grad_ssm_w_glu', 'grad_ssm_b_glu', 'grad_mla_q_norm_w', 'grad_mla_w_uq', 'grad_mla_kv_norm_w', 'grad_mla_w_ukv', 'grad_ssm_out_norm_w', 'grad_mla_out_norm_w', 'grad_w_out', 'grad_ffn_norm_w', 'grad_ffn_w_up', 'grad_ffn_conv_w', 'grad_ffn_conv_b', 'grad_ffn_w_down', 'grad_final_norm_w', 'delta_attn_norm_w', 'delta_w_in', 'delta_ssm_lambda_re', 'delta_ssm_lambda_im', 'delta_ssm_log_dt', 'delta_ssm_b_re', 'delta_ssm_b_im', 'delta_ssm_c_re', 'delta_ssm_c_im', 'delta_ssm_d', 'delta_ssm_w_glu', 'delta_ssm_b_glu', 'delta_mla_q_norm_w', 'delta_mla_w_uq', 'delta_mla_kv_norm_w', 'delta_mla_w_ukv', 'delta_ssm_out_norm_w', 'delta_mla_out_norm_w', 'delta_w_out', 'delta_ffn_norm_w', 'delta_ffn_w_up', 'delta_ffn_conv_w', 'delta_ffn_conv_b', 'delta_ffn_w_down', 'delta_final_norm_w', 'new_m_attn_norm_w', 'new_m_w_in', 'new_m_ssm_lambda_re', 'new_m_ssm_lambda_im', 'new_m_ssm_log_dt', 'new_m_ssm_b_re', 'new_m_ssm_b_im', 'new_m_ssm_c_re', 'new_m_ssm_c_im', 'new_m_ssm_d', 'new_m_ssm_w_glu', 'new_m_ssm_b_glu', 'new_m_mla_q_norm_w', 'new_m_mla_w_uq', 'new_m_mla_kv_norm_w', 'new_m_mla_w_ukv', 'new_m_ssm_out_norm_w', 'new_m_mla_out_norm_w', 'new_m_w_out', 'new_m_ffn_norm_w', 'new_m_ffn_w_up', 'new_m_ffn_conv_w', 'new_m_ffn_conv_b', 'new_m_ffn_w_down', 'new_m_final_norm_w', 'new_v_attn_norm_w', 'new_v_w_in', 'new_v_ssm_lambda_re', 'new_v_ssm_lambda_im', 'new_v_ssm_log_dt', 'new_v_ssm_b_re', 'new_v_ssm_b_im', 'new_v_ssm_c_re', 'new_v_ssm_c_im', 'new_v_ssm_d', 'new_v_ssm_w_glu', 'new_v_ssm_b_glu', 'new_v_mla_q_norm_w', 'new_v_mla_w_uq', 'new_v_mla_kv_norm_w', 'new_v_mla_w_ukv', 'new_v_ssm_out_norm_w', 'new_v_mla_out_norm_w', 'new_v_w_out', 'new_v_ffn_norm_w', 'new_v_ffn_w_up', 'new_v_ffn_conv_w', 'new_v_ffn_conv_b', 'new_v_ffn_w_down', 'new_v_final_norm_w']
TWIN_LEAF_KINDS = {'loss': 'loss', 'grad_x': 'grad_x', 'grad_attn_norm_w': 'grad_w', 'grad_w_in': 'grad_w', 'grad_ssm_lambda_re': 'grad_w', 'grad_ssm_lambda_im': 'grad_w', 'grad_ssm_log_dt': 'grad_w', 'grad_ssm_b_re': 'grad_w', 'grad_ssm_b_im': 'grad_w', 'grad_ssm_c_re': 'grad_w', 'grad_ssm_c_im': 'grad_w', 'grad_ssm_d': 'grad_w', 'grad_ssm_w_glu': 'grad_w', 'grad_ssm_b_glu': 'grad_w', 'grad_mla_q_norm_w': 'grad_w', 'grad_mla_w_uq': 'grad_w', 'grad_mla_kv_norm_w': 'grad_w', 'grad_mla_w_ukv': 'grad_w', 'grad_ssm_out_norm_w': 'grad_w', 'grad_mla_out_norm_w': 'grad_w', 'grad_w_out': 'grad_w', 'grad_ffn_norm_w': 'grad_w', 'grad_ffn_w_up': 'grad_w', 'grad_ffn_conv_w': 'grad_w', 'grad_ffn_conv_b': 'grad_w', 'grad_ffn_w_down': 'grad_w', 'grad_final_norm_w': 'grad_w', 'delta_attn_norm_w': 'delta_w', 'delta_w_in': 'delta_w', 'delta_ssm_lambda_re': 'delta_w', 'delta_ssm_lambda_im': 'delta_w', 'delta_ssm_log_dt': 'delta_w', 'delta_ssm_b_re': 'delta_w', 'delta_ssm_b_im': 'delta_w', 'delta_ssm_c_re': 'delta_w', 'delta_ssm_c_im': 'delta_w', 'delta_ssm_d': 'delta_w', 'delta_ssm_w_glu': 'delta_w', 'delta_ssm_b_glu': 'delta_w', 'delta_mla_q_norm_w': 'delta_w', 'delta_mla_w_uq': 'delta_w', 'delta_mla_kv_norm_w': 'delta_w', 'delta_mla_w_ukv': 'delta_w', 'delta_ssm_out_norm_w': 'delta_w', 'delta_mla_out_norm_w': 'delta_w', 'delta_w_out': 'delta_w', 'delta_ffn_norm_w': 'delta_w', 'delta_ffn_w_up': 'delta_w', 'delta_ffn_conv_w': 'delta_w', 'delta_ffn_conv_b': 'delta_w', 'delta_ffn_w_down': 'delta_w', 'delta_final_norm_w': 'delta_w', 'new_m_attn_norm_w': 'new_m', 'new_m_w_in': 'new_m', 'new_m_ssm_lambda_re': 'new_m', 'new_m_ssm_lambda_im': 'new_m', 'new_m_ssm_log_dt': 'new_m', 'new_m_ssm_b_re': 'new_m', 'new_m_ssm_b_im': 'new_m', 'new_m_ssm_c_re': 'new_m', 'new_m_ssm_c_im': 'new_m', 'new_m_ssm_d': 'new_m', 'new_m_ssm_w_glu': 'new_m', 'new_m_ssm_b_glu': 'new_m', 'new_m_mla_q_norm_w': 'new_m', 'new_m_mla_w_uq': 'new_m', 'new_m_mla_kv_norm_w': 'new_m', 'new_m_mla_w_ukv': 'new_m', 'new_m_ssm_out_norm_w': 'new_m', 'new_m_mla_out_norm_w': 'new_m', 'new_m_w_out': 'new_m', 'new_m_ffn_norm_w': 'new_m', 'new_m_ffn_w_up': 'new_m', 'new_m_ffn_conv_w': 'new_m', 'new_m_ffn_conv_b': 'new_m', 'new_m_ffn_w_down': 'new_m', 'new_m_final_norm_w': 'new_m', 'new_v_attn_norm_w': 'new_v', 'new_v_w_in': 'new_v', 'new_v_ssm_lambda_re': 'new_v', 'new_v_ssm_lambda_im': 'new_v', 'new_v_ssm_log_dt': 'new_v', 'new_v_ssm_b_re': 'new_v', 'new_v_ssm_b_im': 'new_v', 'new_v_ssm_c_re': 'new_v', 'new_v_ssm_c_im': 'new_v', 'new_v_ssm_d': 'new_v', 'new_v_ssm_w_glu': 'new_v', 'new_v_ssm_b_glu': 'new_v', 'new_v_mla_q_norm_w': 'new_v', 'new_v_mla_w_uq': 'new_v', 'new_v_mla_kv_norm_w': 'new_v', 'new_v_mla_w_ukv': 'new_v', 'new_v_ssm_out_norm_w': 'new_v', 'new_v_mla_out_norm_w': 'new_v', 'new_v_w_out': 'new_v', 'new_v_ffn_norm_w': 'new_v', 'new_v_ffn_w_up': 'new_v', 'new_v_ffn_conv_w': 'new_v', 'new_v_ffn_conv_b': 'new_v', 'new_v_ffn_w_down': 'new_v', 'new_v_final_norm_w': 'new_v'}


def _forward(args):
    return _fwd_reference(*[args[k] for k in FWD_PARAMS])


def _output_shape():
    out = _jax.eval_shape(lambda: _forward(_fwd_setup_inputs(0)))
    return out.shape, out.dtype

N_MICROBATCH = 1
ADAM_LR = 0.001
ADAM_B1 = 0.9
ADAM_B2 = 0.999
ADAM_EPS = 1e-08
ADAM_WD = 0.01
ADAM_STEP = 10
PER_EXAMPLE_BATCH_AXIS = {'x': 0, 'positions': 0, 'loss_target': 0}
SHARED_INPUTS = []
_WEIGHT_DTYPES = {'attn_norm_w': _jnp.float32, 'w_in': _jnp.float32, 'ssm_lambda_re': _jnp.float32, 'ssm_lambda_im': _jnp.float32, 'ssm_log_dt': _jnp.float32, 'ssm_b_re': _jnp.float32, 'ssm_b_im': _jnp.float32, 'ssm_c_re': _jnp.float32, 'ssm_c_im': _jnp.float32, 'ssm_d': _jnp.float32, 'ssm_w_glu': _jnp.float32, 'ssm_b_glu': _jnp.float32, 'mla_q_norm_w': _jnp.float32, 'mla_w_uq': _jnp.float32, 'mla_kv_norm_w': _jnp.float32, 'mla_w_ukv': _jnp.float32, 'ssm_out_norm_w': _jnp.float32, 'mla_out_norm_w': _jnp.float32, 'w_out': _jnp.float32, 'ffn_norm_w': _jnp.float32, 'ffn_w_up': _jnp.float32, 'ffn_conv_w': _jnp.float32, 'ffn_conv_b': _jnp.float32, 'ffn_w_down': _jnp.float32, 'final_norm_w': _jnp.float32}
MOMENT_SCALE = {'attn_norm_w': 6.473288e-02, 'w_in': 6.915649e-02, 'ssm_lambda_re': 2.270050e-03, 'ssm_lambda_im': 2.649025e-03, 'ssm_log_dt': 2.495059e+00, 'ssm_b_re': 1.634341e-03, 'ssm_b_im': 1.658852e-03, 'ssm_c_re': 3.239219e-03, 'ssm_c_im': 3.283425e-03, 'ssm_d': 5.364586e-02, 'ssm_w_glu': 1.422623e-02, 'ssm_b_glu': 2.277446e-02, 'mla_q_norm_w': 7.089274e-02, 'mla_w_uq': 3.999904e-02, 'mla_kv_norm_w': 1.618976e-01, 'mla_w_ukv': 4.663686e-02, 'ssm_out_norm_w': 5.490191e-02, 'mla_out_norm_w': 5.062263e-02, 'w_out': 5.004621e-02, 'ffn_norm_w': 3.558215e-02, 'ffn_w_up': 1.540771e-02, 'ffn_conv_w': 1.564158e-02, 'ffn_conv_b': 1.620507e-02, 'ffn_w_down': 2.513951e-02, 'final_norm_w': 8.018657e+00}


def _to_microbatches(a, axis):
    t = _jnp.moveaxis(a, axis, 0)
    t = t.reshape((N_MICROBATCH, t.shape[0] // N_MICROBATCH) + t.shape[1:])
    return _jnp.moveaxis(t, 1, axis + 1)


def setup_inputs(seed: int = 0) -> dict:
    inp = _fwd_setup_inputs(seed)
    key = _jax.random.fold_in(_jax.random.key(seed), 7919)
    shape, _ = _output_shape()
    out = dict(inp)
    out["loss_target"] = _jax.random.normal(_jax.random.fold_in(key, 0), shape, _jnp.float32)
    for i, name in enumerate(TWIN_WEIGHTS):
        w = inp[name].astype(_jnp.float32)
        if MOMENT_SCALE is None:
            s = _jnp.sqrt(_jnp.mean(_jnp.square(w)) + 1e-30)
        else:
            s = MOMENT_SCALE[name]
        km, kv = _jax.random.split(_jax.random.fold_in(key, i + 1))
        out[name] = w
        out["m_" + name] = s * _jax.random.normal(km, w.shape, _jnp.float32)
        out["v_" + name] = (s * s) * _jax.random.uniform(kv, w.shape, _jnp.float32, 0.5, 1.5)
    if N_MICROBATCH > 1:
        for name, axis in PER_EXAMPLE_BATCH_AXIS.items():
            out[name] = _to_microbatches(out[name], axis)
    return {'x': out['x'], 'positions': out['positions'], 'attn_norm_w': out['attn_norm_w'], 'w_in': out['w_in'], 'ssm_lambda_re': out['ssm_lambda_re'], 'ssm_lambda_im': out['ssm_lambda_im'], 'ssm_log_dt': out['ssm_log_dt'], 'ssm_b_re': out['ssm_b_re'], 'ssm_b_im': out['ssm_b_im'], 'ssm_c_re': out['ssm_c_re'], 'ssm_c_im': out['ssm_c_im'], 'ssm_d': out['ssm_d'], 'ssm_w_glu': out['ssm_w_glu'], 'ssm_b_glu': out['ssm_b_glu'], 'mla_q_norm_w': out['mla_q_norm_w'], 'mla_w_uq': out['mla_w_uq'], 'mla_kv_norm_w': out['mla_kv_norm_w'], 'mla_w_ukv': out['mla_w_ukv'], 'ssm_out_norm_w': out['ssm_out_norm_w'], 'mla_out_norm_w': out['mla_out_norm_w'], 'w_out': out['w_out'], 'ffn_norm_w': out['ffn_norm_w'], 'ffn_w_up': out['ffn_w_up'], 'ffn_conv_w': out['ffn_conv_w'], 'ffn_conv_b': out['ffn_conv_b'], 'ffn_w_down': out['ffn_w_down'], 'final_norm_w': out['final_norm_w'], 'loss_target': out['loss_target'], 'm_attn_norm_w': out['m_attn_norm_w'], 'm_w_in': out['m_w_in'], 'm_ssm_lambda_re': out['m_ssm_lambda_re'], 'm_ssm_lambda_im': out['m_ssm_lambda_im'], 'm_ssm_log_dt': out['m_ssm_log_dt'], 'm_ssm_b_re': out['m_ssm_b_re'], 'm_ssm_b_im': out['m_ssm_b_im'], 'm_ssm_c_re': out['m_ssm_c_re'], 'm_ssm_c_im': out['m_ssm_c_im'], 'm_ssm_d': out['m_ssm_d'], 'm_ssm_w_glu': out['m_ssm_w_glu'], 'm_ssm_b_glu': out['m_ssm_b_glu'], 'm_mla_q_norm_w': out['m_mla_q_norm_w'], 'm_mla_w_uq': out['m_mla_w_uq'], 'm_mla_kv_norm_w': out['m_mla_kv_norm_w'], 'm_mla_w_ukv': out['m_mla_w_ukv'], 'm_ssm_out_norm_w': out['m_ssm_out_norm_w'], 'm_mla_out_norm_w': out['m_mla_out_norm_w'], 'm_w_out': out['m_w_out'], 'm_ffn_norm_w': out['m_ffn_norm_w'], 'm_ffn_w_up': out['m_ffn_w_up'], 'm_ffn_conv_w': out['m_ffn_conv_w'], 'm_ffn_conv_b': out['m_ffn_conv_b'], 'm_ffn_w_down': out['m_ffn_w_down'], 'm_final_norm_w': out['m_final_norm_w'], 'v_attn_norm_w': out['v_attn_norm_w'], 'v_w_in': out['v_w_in'], 'v_ssm_lambda_re': out['v_ssm_lambda_re'], 'v_ssm_lambda_im': out['v_ssm_lambda_im'], 'v_ssm_log_dt': out['v_ssm_log_dt'], 'v_ssm_b_re': out['v_ssm_b_re'], 'v_ssm_b_im': out['v_ssm_b_im'], 'v_ssm_c_re': out['v_ssm_c_re'], 'v_ssm_c_im': out['v_ssm_c_im'], 'v_ssm_d': out['v_ssm_d'], 'v_ssm_w_glu': out['v_ssm_w_glu'], 'v_ssm_b_glu': out['v_ssm_b_glu'], 'v_mla_q_norm_w': out['v_mla_q_norm_w'], 'v_mla_w_uq': out['v_mla_w_uq'], 'v_mla_kv_norm_w': out['v_mla_kv_norm_w'], 'v_mla_w_ukv': out['v_mla_w_ukv'], 'v_ssm_out_norm_w': out['v_ssm_out_norm_w'], 'v_mla_out_norm_w': out['v_mla_out_norm_w'], 'v_w_out': out['v_w_out'], 'v_ffn_norm_w': out['v_ffn_norm_w'], 'v_ffn_w_up': out['v_ffn_w_up'], 'v_ffn_conv_w': out['v_ffn_conv_w'], 'v_ffn_conv_b': out['v_ffn_conv_b'], 'v_ffn_w_down': out['v_ffn_w_down'], 'v_final_norm_w': out['v_final_norm_w']}


def _loss(weights, diff, rest, loss_target):
    with _jax.named_scope("forward"):
        args = {**rest, TWIN_DIFF_INPUT: diff, **{k: w.astype(_WEIGHT_DTYPES[k]) for k, w in weights.items()}}
        y = _forward(args)
    with _jax.named_scope("loss_head"):
        err = _jnp.square(y.astype(_jnp.float32) - loss_target)
        return 0.5 * _jnp.sum(_jnp.mean(err, axis=-1)) if err.ndim else 0.5 * err


def _adamw(w, g, m, v):
    m = ADAM_B1 * m + (1.0 - ADAM_B1) * g
    v = ADAM_B2 * v + (1.0 - ADAM_B2) * _jnp.square(g)
    m_hat = m / (1.0 - ADAM_B1 ** ADAM_STEP)
    v_hat = v / (1.0 - ADAM_B2 ** ADAM_STEP)
    delta = -ADAM_LR * (m_hat / (_jnp.sqrt(v_hat) + ADAM_EPS) + ADAM_WD * w)
    return delta, m, v


def reference(x, positions, attn_norm_w, w_in, ssm_lambda_re, ssm_lambda_im, ssm_log_dt, ssm_b_re, ssm_b_im, ssm_c_re, ssm_c_im, ssm_d, ssm_w_glu, ssm_b_glu, mla_q_norm_w, mla_w_uq, mla_kv_norm_w, mla_w_ukv, ssm_out_norm_w, mla_out_norm_w, w_out, ffn_norm_w, ffn_w_up, ffn_conv_w, ffn_conv_b, ffn_w_down, final_norm_w, loss_target, m_attn_norm_w, m_w_in, m_ssm_lambda_re, m_ssm_lambda_im, m_ssm_log_dt, m_ssm_b_re, m_ssm_b_im, m_ssm_c_re, m_ssm_c_im, m_ssm_d, m_ssm_w_glu, m_ssm_b_glu, m_mla_q_norm_w, m_mla_w_uq, m_mla_kv_norm_w, m_mla_w_ukv, m_ssm_out_norm_w, m_mla_out_norm_w, m_w_out, m_ffn_norm_w, m_ffn_w_up, m_ffn_conv_w, m_ffn_conv_b, m_ffn_w_down, m_final_norm_w, v_attn_norm_w, v_w_in, v_ssm_lambda_re, v_ssm_lambda_im, v_ssm_log_dt, v_ssm_b_re, v_ssm_b_im, v_ssm_c_re, v_ssm_c_im, v_ssm_d, v_ssm_w_glu, v_ssm_b_glu, v_mla_q_norm_w, v_mla_w_uq, v_mla_kv_norm_w, v_mla_w_ukv, v_ssm_out_norm_w, v_mla_out_norm_w, v_w_out, v_ffn_norm_w, v_ffn_w_up, v_ffn_conv_w, v_ffn_conv_b, v_ffn_w_down, v_final_norm_w):
    given = dict(x=x, positions=positions, attn_norm_w=attn_norm_w, w_in=w_in, ssm_lambda_re=ssm_lambda_re, ssm_lambda_im=ssm_lambda_im, ssm_log_dt=ssm_log_dt, ssm_b_re=ssm_b_re, ssm_b_im=ssm_b_im, ssm_c_re=ssm_c_re, ssm_c_im=ssm_c_im, ssm_d=ssm_d, ssm_w_glu=ssm_w_glu, ssm_b_glu=ssm_b_glu, mla_q_norm_w=mla_q_norm_w, mla_w_uq=mla_w_uq, mla_kv_norm_w=mla_kv_norm_w, mla_w_ukv=mla_w_ukv, ssm_out_norm_w=ssm_out_norm_w, mla_out_norm_w=mla_out_norm_w, w_out=w_out, ffn_norm_w=ffn_norm_w, ffn_w_up=ffn_w_up, ffn_conv_w=ffn_conv_w, ffn_conv_b=ffn_conv_b, ffn_w_down=ffn_w_down, final_norm_w=final_norm_w, loss_target=loss_target, m_attn_norm_w=m_attn_norm_w, m_w_in=m_w_in, m_ssm_lambda_re=m_ssm_lambda_re, m_ssm_lambda_im=m_ssm_lambda_im, m_ssm_log_dt=m_ssm_log_dt, m_ssm_b_re=m_ssm_b_re, m_ssm_b_im=m_ssm_b_im, m_ssm_c_re=m_ssm_c_re, m_ssm_c_im=m_ssm_c_im, m_ssm_d=m_ssm_d, m_ssm_w_glu=m_ssm_w_glu, m_ssm_b_glu=m_ssm_b_glu, m_mla_q_norm_w=m_mla_q_norm_w, m_mla_w_uq=m_mla_w_uq, m_mla_kv_norm_w=m_mla_kv_norm_w, m_mla_w_ukv=m_mla_w_ukv, m_ssm_out_norm_w=m_ssm_out_norm_w, m_mla_out_norm_w=m_mla_out_norm_w, m_w_out=m_w_out, m_ffn_norm_w=m_ffn_norm_w, m_ffn_w_up=m_ffn_w_up, m_ffn_conv_w=m_ffn_conv_w, m_ffn_conv_b=m_ffn_conv_b, m_ffn_w_down=m_ffn_w_down, m_final_norm_w=m_final_norm_w, v_attn_norm_w=v_attn_norm_w, v_w_in=v_w_in, v_ssm_lambda_re=v_ssm_lambda_re, v_ssm_lambda_im=v_ssm_lambda_im, v_ssm_log_dt=v_ssm_log_dt, v_ssm_b_re=v_ssm_b_re, v_ssm_b_im=v_ssm_b_im, v_ssm_c_re=v_ssm_c_re, v_ssm_c_im=v_ssm_c_im, v_ssm_d=v_ssm_d, v_ssm_w_glu=v_ssm_w_glu, v_ssm_b_glu=v_ssm_b_glu, v_mla_q_norm_w=v_mla_q_norm_w, v_mla_w_uq=v_mla_w_uq, v_mla_kv_norm_w=v_mla_kv_norm_w, v_mla_w_ukv=v_mla_w_ukv, v_ssm_out_norm_w=v_ssm_out_norm_w, v_mla_out_norm_w=v_mla_out_norm_w, v_w_out=v_w_out, v_ffn_norm_w=v_ffn_norm_w, v_ffn_w_up=v_ffn_w_up, v_ffn_conv_w=v_ffn_conv_w, v_ffn_conv_b=v_ffn_conv_b, v_ffn_w_down=v_ffn_w_down, v_final_norm_w=v_final_norm_w)
    weights = {n: given[n] for n in TWIN_WEIGHTS}
    shared = {n: given[n] for n in SHARED_INPUTS}
    per_example = {n: given[n] for n in ['x', 'positions']}
    grad_fn = _jax.value_and_grad(_loss, argnums=(0, 1))

    def one_microbatch(ex, loss_target):
        ex = dict(ex)
        diff = ex.pop(TWIN_DIFF_INPUT)
        return grad_fn(weights, diff, {**shared, **ex}, loss_target)

    if N_MICROBATCH == 1:
        loss, (grad_w, grad_x) = one_microbatch(per_example, given["loss_target"])
    else:
        def body(carry, xs):
            loss_sum, grad_sum = carry
            l_k, (gw_k, gx_k) = one_microbatch(xs[0], xs[1])
            with _jax.named_scope("update"):
                return (loss_sum + l_k, _jax.tree.map(_jnp.add, grad_sum, gw_k)), gx_k

        init = (_jnp.zeros((), _jnp.float32), _jax.tree.map(_jnp.zeros_like, weights))
        (loss, grad_w), grad_x = _jax.lax.scan(body, init, (per_example, given["loss_target"]))
    with _jax.named_scope("update"):
        delta_w, new_m, new_v = {}, {}, {}
        for n in TWIN_WEIGHTS:
            delta_w[n], new_m[n], new_v[n] = _adamw(weights[n], grad_w[n], given["m_" + n], given["v_" + n])
    return (loss, grad_x, *[grad_w[n] for n in TWIN_WEIGHTS], *[delta_w[n] for n in TWIN_WEIGHTS],
            *[new_m[n] for n in TWIN_WEIGHTS], *[new_v[n] for n in TWIN_WEIGHTS])
```

```python
import functools
import math

import jax
import jax.numpy as jnp
from jax import lax
from jax.experimental import pallas as pl
from jax.experimental.pallas import tpu as pltpu

F32 = jnp.float32
BF16 = jnp.bfloat16

SSM_GROUP = 16
SSM_STATE = 64
QK_NOPE_DIM = 128
QK_ROPE_DIM = 64
V_HEAD_DIM = 128
ROPE_THETA = 10000.0
RMS_EPS = 1e-6
ADAM_LR, ADAM_B1, ADAM_B2, ADAM_EPS, ADAM_WD, ADAM_STEP = 0.001, 0.9, 0.999, 1e-08, 0.01, 10

LANES = 128
SUBLANES = 8
VMEM_LIMIT_BYTES = 56 * 1024 * 1024

GROUPS_PER_BATCH = LANES // SSM_GROUP
STATE_PER_BATCH = GROUPS_PER_BATCH * SSM_STATE
HEAD_SLOT = 2 * LANES
NEG_INF = -1e30
ATTN_BLOCK = 512

N_CHIPS = 4
N_CORES = 2


def _tile(n, pref, align=LANES):
    if n <= pref:
        return n
    t = (pref // align) * align
    while t >= align:
        if n % t == 0:
            return t
        t -= align
    return n


def _params(sem):
    return pltpu.CompilerParams(dimension_semantics=sem, vmem_limit_bytes=VMEM_LIMIT_BYTES)


def _dot(a, b, dims):
    return lax.dot_general(a, b, (dims, ((), ())), preferred_element_type=F32)


def _dot_nn(a, b):
    return _dot(a, b, ((1,), (0,)))


def _dot_nt(a, b):
    return _dot(a, b, ((1,), (1,)))


def _dot_tn(a, b):
    return _dot(a, b, ((0,), (0,)))


def _matmul(a, b, *, mode, name, tm=512, tn=1024, tk=2048, bias=None, add=None, out_dtype=F32,
            out_blocks=None, a_split=False, b_split=False):
    if a_split:
        assert mode == "nt"
        a_shape = (a.shape[1], 2 * a.shape[2])
    else:
        a_shape = a.shape
    if b_split:
        assert mode == "tn"
        b_shape = (b.shape[1], 2 * b.shape[2])
    else:
        b_shape = b.shape
    if mode == "nn":
        (m, k), (k2, n) = a_shape, b_shape
    elif mode == "nt":
        (m, k), (n, k2) = a_shape, b_shape
    else:
        (k, m), (k2, n) = a_shape, b_shape
    assert k == k2, (a.shape, b.shape, mode)
    tm, tn, tk = _tile(m, tm, SUBLANES), _tile(n, tn), _tile(k, tk)
    nk = k // tk
    a_spec = {"nn": pl.BlockSpec((tm, tk), lambda i, j, kk: (i, kk)),
              "nt": pl.BlockSpec((tm, tk), lambda i, j, kk: (i, kk)),
              "tn": pl.BlockSpec((tk, tm), lambda i, j, kk: (kk, i))}[mode]
    b_spec = {"nn": pl.BlockSpec((tk, tn), lambda i, j, kk: (kk, j)),
              "nt": pl.BlockSpec((tn, tk), lambda i, j, kk: (j, kk)),
              "tn": pl.BlockSpec((tk, tn), lambda i, j, kk: (kk, j))}[mode]
    if a_split:
        kb = a.shape[2] // tk
        assert a.shape[2] % tk == 0
        a_spec = pl.BlockSpec((None, tm, tk), lambda i, j, kk: (kk // kb, i, kk % kb))
    if b_split:
        nb = b.shape[2] // tn
        assert b.shape[2] % tn == 0
        b_spec = pl.BlockSpec((None, tk, tn), lambda i, j, kk: (j // nb, kk, j % nb))
    dot = {"nn": _dot_nn, "nt": _dot_nt, "tn": _dot_tn}[mode]
    in_specs, operands = [a_spec, b_spec], [a, b]
    if bias is not None:
        in_specs.append(pl.BlockSpec((1, tn), lambda i, j, kk: (0, j)))
        operands.append(bias)
    if add is not None:
        in_specs.append(pl.BlockSpec((tm, tn), lambda i, j, kk: (i, j)))
        operands.append(add)

    def body(*refs):
        a_ref, b_ref = refs[0], refs[1]
        rest = list(refs[2:])
        bias_ref = rest.pop(0) if bias is not None else None
        add_ref = rest.pop(0) if add is not None else None
        o_ref, acc_ref = rest

        def finish(acc):
            if bias_ref is not None:
                acc = acc + bias_ref[...]
            if add_ref is not None:
                acc = acc + add_ref[...]
            o_ref[...] = acc.astype(o_ref.dtype)

        part = dot(a_ref[...].astype(BF16), b_ref[...].astype(BF16))
        if nk == 1:
            finish(part)
        else:
            kk = pl.program_id(2)

            @pl.when(kk == 0)
            def _():
                acc_ref[...] = part

            @pl.when(jnp.logical_and(kk > 0, kk < nk - 1))
            def _():
                acc_ref[...] += part

            @pl.when(kk == nk - 1)
            def _():
                finish(acc_ref[...] + part)

    if out_blocks is None:
        out_shape = jax.ShapeDtypeStruct((m, n), out_dtype)
        out_spec = pl.BlockSpec((tm, tn), lambda i, j, kk: (i, j))
    else:
        shape, block, index_map = out_blocks(tm, tn)
        out_shape = jax.ShapeDtypeStruct(shape, out_dtype)
        out_spec = pl.BlockSpec(block, index_map)
    acc_shape = (tm, tn) if nk > 1 else (SUBLANES, LANES)
    return pl.pallas_call(
        body, name=name, grid=(m // tm, n // tn, nk), in_specs=in_specs, out_specs=out_spec, out_shape=out_shape,
        scratch_shapes=[pltpu.VMEM(acc_shape, F32)],
        compiler_params=_params(("parallel", "parallel", "arbitrary")),
    )(*operands)


def _wgrad_blocks(rows, cols, row_sharded):
    if row_sharded:
        sr, sc = rows // N_CHIPS, cols // N_CORES
    else:
        sr, sc = rows // N_CORES, cols // N_CHIPS

    def make(tm, tn):
        assert sr % tm == 0 and sc % tn == 0, (rows, cols, tm, tn)
        rb, cb = sr // tm, sc // tn
        if row_sharded:
            def index_map(i, j, kk):
                return (j // cb, i // rb, i % rb, j % cb)
        else:
            def index_map(i, j, kk):
                return (i // rb, j // cb, i % rb, j % cb)
        return (N_CORES, N_CHIPS, sr, sc), (None, None, tm, tn), index_map

    return make, (sr, sc)


def _rms_rows(x):
    return lax.rsqrt(jnp.mean(x * x, axis=-1, keepdims=True) + RMS_EPS)


def _rmsnorm_fwd(x, w, *, name, width=None, col=0, out_dtype=BF16, tr=256):
    rows = x.shape[0]
    width = x.shape[1] if width is None else width
    tr = _tile(rows, tr, SUBLANES)

    def body(x_ref, w_ref, o_ref):
        xv = x_ref[...]
        o_ref[...] = (xv * _rms_rows(xv) * w_ref[...]).astype(o_ref.dtype)

    return pl.pallas_call(
        body, name=name, grid=(rows // tr,),
        in_specs=[pl.BlockSpec((tr, width), lambda i: (i, col)), pl.BlockSpec((1, width), lambda i: (0, 0))],
        out_specs=pl.BlockSpec((tr, width), lambda i: (i, 0)),
        out_shape=jax.ShapeDtypeStruct((rows, width), out_dtype),
        compiler_params=_params(("parallel",)),
    )(x, w)


def _rmsnorm_bwd_rows(xv, w, dy):
    r = _rms_rows(xv)
    n = xv * r
    dn = dy * w
    dx = r * (dn - n * jnp.mean(dn * n, axis=-1, keepdims=True))
    return dx, dy * n


def _rmsnorm_bwd(x, w, dy, *, name, width=None, col=0, dy_col=0, add=None, tr=256):
    rows = x.shape[0]
    width = x.shape[1] if width is None else width
    tr = _tile(rows, tr, SUBLANES)
    in_specs = [pl.BlockSpec((tr, width), lambda i: (i, col)), pl.BlockSpec((1, width), lambda i: (0, 0)),
                pl.BlockSpec((tr, width), lambda i: (i, dy_col))]
    operands = [x, w, dy]
    if add is not None:
        in_specs.append(pl.BlockSpec((tr, width), lambda i: (i, 0)))
        operands.append(add)

    def body(*refs):
        x_ref, w_ref, dy_ref = refs[:3]
        add_ref = refs[3] if add is not None else None
        dx_ref, dw_ref = refs[-2:]
        dx, dwp = _rmsnorm_bwd_rows(x_ref[...], w_ref[...], dy_ref[...])
        if add_ref is not None:
            dx = dx + add_ref[...]
        dx_ref[...] = dx
        part = jnp.sum(dwp, axis=0, keepdims=True)

        @pl.when(pl.program_id(0) == 0)
        def _():
            dw_ref[...] = part

        @pl.when(pl.program_id(0) > 0)
        def _():
            dw_ref[...] += part

    return pl.pallas_call(
        body, name=name, grid=(rows // tr,), in_specs=in_specs,
        out_specs=[pl.BlockSpec((tr, width), lambda i: (i, 0)), pl.BlockSpec((1, width), lambda i: (0, 0))],
        out_shape=[jax.ShapeDtypeStruct((rows, width), F32), jax.ShapeDtypeStruct((1, width), F32)],
        compiler_params=_params(("arbitrary",)),
    )(*operands)


def _final_norm_loss(h, w, target, *, tr=256):
    rows, d = h.shape
    tr = _tile(rows, tr, SUBLANES)

    def body(h_ref, w_ref, t_ref, loss_ref, dh_ref, dw_ref):
        hv, wv = h_ref[...], w_ref[...]
        r = _rms_rows(hv)
        n = hv * r
        err = n * wv - t_ref[...]
        d_out = err * (1.0 / d)
        dn = d_out * wv
        dh_ref[...] = r * (dn - n * jnp.mean(dn * n, axis=-1, keepdims=True))
        dw_part = jnp.sum(d_out * n, axis=0, keepdims=True)
        loss_part = jnp.full((SUBLANES, LANES), 0.5 / d, F32) * jnp.sum(err * err)

        @pl.when(pl.program_id(0) == 0)
        def _():
            dw_ref[...] = dw_part
            loss_ref[...] = loss_part

        @pl.when(pl.program_id(0) > 0)
        def _():
            dw_ref[...] += dw_part
            loss_ref[...] += loss_part

    return pl.pallas_call(
        body, name="final_norm_loss", grid=(rows // tr,),
        in_specs=[pl.BlockSpec((tr, d), lambda i: (i, 0)), pl.BlockSpec((1, d), lambda i: (0, 0)),
                  pl.BlockSpec((tr, d), lambda i: (i, 0))],
        out_specs=[pl.BlockSpec((SUBLANES, LANES), lambda i: (0, 0)), pl.BlockSpec((tr, d), lambda i: (i, 0)),
                   pl.BlockSpec((1, d), lambda i: (0, 0))],
        out_shape=[jax.ShapeDtypeStruct((SUBLANES, LANES), F32), jax.ShapeDtypeStruct((rows, d), F32),
                   jax.ShapeDtypeStruct((1, d), F32)],
        compiler_params=_params(("arbitrary",)),
    )(h, w, target)


def _cmul(ar, ai, br, bi):
    return ar * br - ai * bi, ar * bi + ai * br


def _expand_matrix(groups, reps):
    row = lax.broadcasted_iota(jnp.int32, (groups, groups * reps), 0)
    colg = lax.broadcasted_iota(jnp.int32, (groups, groups * reps), 1) // reps
    return (row == colg).astype(F32)


def _dot_exact(a, b, dims):
    return lax.dot_general(a, b, (dims, ((), ())), preferred_element_type=F32, precision=lax.Precision.HIGHEST)


def _s5_discretize(lr, li, dt):
    mag = jnp.exp(lr * dt)
    th = li * dt
    ar, ai = mag * jnp.cos(th), mag * jnp.sin(th)
    nr, ni = ar - 1.0, ai
    den = lr * lr + li * li
    zr = (nr * lr + ni * li) / den
    zi = (ni * lr - nr * li) / den
    return mag, ar, ai, nr, ni, den, zr, zi


def _s5_params(lam_re, lam_im, log_dt, b_re, b_im):
    g, p = lam_re.shape
    ph = b_re.shape[1]

    def body(lr_ref, li_ref, ldt_ref, br_ref, bi_ref, ar_ref, ai_ref, bbr_ref, bbi_ref):
        dt = jnp.exp(ldt_ref[...])
        _, ar, ai, _, _, _, zr, zi = _s5_discretize(lr_ref[...], li_ref[...], dt)
        ar_ref[...] = ar
        ai_ref[...] = ai
        e = _expand_matrix(p, ph // p)
        zr_x = _dot_exact(zr, e, ((1,), (0,)))
        zi_x = _dot_exact(zi, e, ((1,), (0,)))
        bre, bim = br_ref[...], bi_ref[...]
        bbr_ref[...] = zr_x * bre - zi_x * bim
        bbi_ref[...] = zr_x * bim + zi_x * bre

    return pl.pallas_call(
        body, name="s5_params",
        out_shape=[jax.ShapeDtypeStruct((g, p), F32)] * 2 + [jax.ShapeDtypeStruct((g, ph), F32)] * 2,
    )(lam_re, lam_im, log_dt, b_re, b_im)


def _s5_params_bwd(lam_re, lam_im, log_dt, b_re, b_im, d_ar, d_ai, d_bbr, d_bbi):
    g, p = lam_re.shape
    ph = b_re.shape[1]

    def body(lr_ref, li_ref, ldt_ref, br_ref, bi_ref, dar_ref, dai_ref, dbr_ref, dbi_ref,
             dlr_ref, dli_ref, dldt_ref, dbre_ref, dbim_ref):
        lr, li = lr_ref[...], li_ref[...]
        dt = jnp.exp(ldt_ref[...])
        mag, ar, ai, nr, ni, den, zr, zi = _s5_discretize(lr, li, dt)
        e = _expand_matrix(p, ph // p)
        zr_x = _dot_exact(zr, e, ((1,), (0,)))
        zi_x = _dot_exact(zi, e, ((1,), (0,)))
        bre, bim, dbr, dbi = br_ref[...], bi_ref[...], dbr_ref[...], dbi_ref[...]
        dbre_ref[...] = zr_x * dbr + zi_x * dbi
        dbim_ref[...] = zr_x * dbi - zi_x * dbr
        dzr = _dot_exact(bre * dbr + bim * dbi, e, ((1,), (1,)))
        dzi = _dot_exact(bre * dbi - bim * dbr, e, ((1,), (1,)))
        inv = 1.0 / den
        d_nr = (dzr * lr - dzi * li) * inv
        d_ni = (dzr * li + dzi * lr) * inv
        d_den = -(dzr * zr + dzi * zi) * inv
        d_lr = (dzr * nr + dzi * ni) * inv + 2.0 * lr * d_den
        d_li = (dzr * ni - dzi * nr) * inv + 2.0 * li * d_den
        t_ar = dar_ref[...] + d_nr
        t_ai = dai_ref[...] + d_ni
        d_lrdt = t_ar * ar + t_ai * ai
        d_th = t_ai * ar - t_ar * ai
        dlr_ref[...] = d_lr + d_lrdt * dt
        dli_ref[...] = d_li + d_th * dt
        dldt_ref[...] = jnp.sum(d_lrdt * lr + d_th * li, axis=1, keepdims=True) * dt

    return pl.pallas_call(
        body, name="s5_params_bwd",
        out_shape=[jax.ShapeDtypeStruct((g, p), F32)] * 2 + [jax.ShapeDtypeStruct((g, 1), F32)]
        + [jax.ShapeDtypeStruct((g, ph), F32)] * 2,
    )(lam_re, lam_im, log_dt, b_re, b_im, d_ar, d_ai, d_bbr, d_bbi)


def _powers(ar, ai, count):
    out = [(ar, ai)]
    for _ in range(count - 1):
        out.append(_cmul(out[-1][0], out[-1][1], ar, ai))
    return out


def _scan_coefs(ar, ai, reverse):
    w = ar.shape[-1]
    pw = _powers(ar, ai, SUBLANES)
    row = lax.broadcasted_iota(jnp.int32, (SUBLANES, w), 0)
    steps = []
    d = 1
    while d < SUBLANES:
        keep = (row < SUBLANES - d) if reverse else (row >= d)
        pr, pi = pw[d - 1]
        steps.append((d, jnp.where(keep, pr, 0.0), jnp.where(keep, pi, 0.0)))
        d *= 2
    cr = jnp.zeros((SUBLANES, w), F32)
    ci = jnp.zeros((SUBLANES, w), F32)
    for t in range(SUBLANES):
        pr, pi = pw[SUBLANES - 1 - t] if reverse else pw[t]
        cr = jnp.where(row == t, pr, cr)
        ci = jnp.where(row == t, pi, ci)
    return steps, cr, ci


def _scan_tile(xr, xi, carry_r, carry_i, coefs, reverse):
    steps, cr, ci = coefs
    for d, mr, mi in steps:
        shift = SUBLANES - d if reverse else d
        sr, si = pltpu.roll(xr, shift, 0), pltpu.roll(xi, shift, 0)
        pr, pi = _cmul(mr, mi, sr, si)
        xr, xi = xr + pr, xi + pi
    pr, pi = _cmul(cr, ci, carry_r, carry_i)
    return xr + pr, xi + pi


def _gelu(x):
    c = math.sqrt(2.0 / math.pi)
    return 0.5 * x * (1.0 + jnp.tanh(c * (x + 0.044715 * x * x * x)))


def _gelu_grad(x):
    c = math.sqrt(2.0 / math.pi)
    t = jnp.tanh(c * (x + 0.044715 * x * x * x))
    return 0.5 * (1.0 + t) + 0.5 * x * (1.0 - t * t) * c * (1.0 + 3.0 * 0.044715 * x * x)


def _s5_fwd(proj, wb, wc, d_skip, abar):
    rows = proj.shape[0]
    nb = wb.shape[0]
    s2 = 2 * STATE_PER_BATCH
    st = STATE_PER_BATCH
    chunk = _tile(rows, 512, SUBLANES)

    def body(u_ref, wb_ref, wc_ref, d_ref, a_ref, s_ref, y_ref, yg_ref):
        for c0 in range(0, rows, chunk):
            s_ref[pl.ds(c0, chunk), :] = _dot_nn(u_ref[pl.ds(c0, chunk), :].astype(BF16), wb_ref[...])
        av = a_ref[...]
        coefs = _scan_coefs(av[:, :st], av[:, st:], reverse=False)

        def tile(b, carry):
            r0 = pl.multiple_of(b * SUBLANES, SUBLANES)
            xr, xi = _scan_tile(s_ref[pl.ds(r0, SUBLANES), :st], s_ref[pl.ds(r0, SUBLANES), st:], carry[0], carry[1],
                                coefs, False)
            s_ref[pl.ds(r0, SUBLANES), :st] = xr
            s_ref[pl.ds(r0, SUBLANES), st:] = xi
            return xr[SUBLANES - 1:, :], xi[SUBLANES - 1:, :]

        zero = jnp.zeros((1, st), F32)
        lax.fori_loop(0, rows // SUBLANES, tile, (zero, zero))
        for c0 in range(0, rows, chunk):
            y = _dot_nn(s_ref[pl.ds(c0, chunk), :].astype(BF16), wc_ref[...]) + d_ref[...] * u_ref[pl.ds(c0, chunk), :]
            y_ref[pl.ds(c0, chunk), :] = y
            yg_ref[pl.ds(c0, chunk), :] = _gelu(y).astype(BF16)

    return pl.pallas_call(
        body, name="s5_fwd", grid=(nb,),
        in_specs=[pl.BlockSpec((rows, LANES), lambda j: (0, j)), pl.BlockSpec((None, LANES, s2), lambda j: (j, 0, 0)),
                  pl.BlockSpec((None, s2, LANES), lambda j: (j, 0, 0)), pl.BlockSpec((1, LANES), lambda j: (0, j)),
                  pl.BlockSpec((None, 1, s2), lambda j: (j, 0, 0))],
        out_specs=[pl.BlockSpec((rows, s2), lambda j: (0, j)), pl.BlockSpec((rows, LANES), lambda j: (0, j)),
                   pl.BlockSpec((rows, LANES), lambda j: (0, j))],
        out_shape=[jax.ShapeDtypeStruct((rows, nb * s2), F32), jax.ShapeDtypeStruct((rows, nb * LANES), F32),
                   jax.ShapeDtypeStruct((rows, nb * LANES), BF16)],
        compiler_params=_params(("parallel",)),
    )(proj, wb, wc, d_skip, abar)


def _s5_bwd(proj, states, y_pre, dyg_a, dyg_b, wb, wc, d_skip, abar):
    rows = proj.shape[0]
    nb = wb.shape[0]
    s2 = 2 * STATE_PER_BATCH
    st = STATE_PER_BATCH
    chunk = _tile(rows, 512, SUBLANES)
    n_tiles = rows // SUBLANES

    def body(u_ref, s_ref, y_ref, ga_ref, gb_ref, wb_ref, wc_ref, d_ref, a_ref,
             du_ref, dwb_ref, dwc_ref, da_ref, dd_ref, ds_ref, dy_ref):
        dy_ref[...] = (ga_ref[...] + gb_ref[...]) * _gelu_grad(y_ref[...])
        dd_ref[...] = jnp.sum(dy_ref[...] * u_ref[...], axis=0, keepdims=True)
        for c0 in range(0, rows, chunk):
            ds_ref[pl.ds(c0, chunk), :] = _dot_nt(dy_ref[pl.ds(c0, chunk), :].astype(BF16), wc_ref[...])
        dwc_ref[...] = _dot_tn(s_ref[...].astype(BF16), dy_ref[...].astype(BF16))
        av = a_ref[...]
        coefs = _scan_coefs(av[:, :st], -av[:, st:], reverse=True)
        row = lax.broadcasted_iota(jnp.int32, (SUBLANES, st), 0)

        def tile(k, carry):
            cr, ci, acc_r, acc_i = carry
            b = n_tiles - 1 - k
            r0 = pl.multiple_of(b * SUBLANES, SUBLANES)
            rp = pl.multiple_of(jnp.maximum(b - 1, 0) * SUBLANES, SUBLANES)
            xr, xi = _scan_tile(ds_ref[pl.ds(r0, SUBLANES), :st], ds_ref[pl.ds(r0, SUBLANES), st:], cr, ci, coefs, True)
            ds_ref[pl.ds(r0, SUBLANES), :st] = xr
            ds_ref[pl.ds(r0, SUBLANES), st:] = xi
            first = jnp.where(b > 0, 1.0, 0.0)
            pr = jnp.where(row == 0, pltpu.roll(s_ref[pl.ds(rp, SUBLANES), :st], 1, 0) * first,
                           pltpu.roll(s_ref[pl.ds(r0, SUBLANES), :st], 1, 0))
            pi = jnp.where(row == 0, pltpu.roll(s_ref[pl.ds(rp, SUBLANES), st:], 1, 0) * first,
                           pltpu.roll(s_ref[pl.ds(r0, SUBLANES), st:], 1, 0))
            acc_r = acc_r + pr * xr + pi * xi
            acc_i = acc_i + pr * xi - pi * xr
            return xr[:1, :], xi[:1, :], acc_r, acc_i

        zero = jnp.zeros((1, st), F32)
        zacc = jnp.zeros((SUBLANES, st), F32)
        _, _, acc_r, acc_i = lax.fori_loop(0, n_tiles, tile, (zero, zero, zacc, zacc))
        da_ref[:, :st] = jnp.sum(acc_r, axis=0, keepdims=True)
        da_ref[:, st:] = jnp.sum(acc_i, axis=0, keepdims=True)
        for c0 in range(0, rows, chunk):
            du_ref[pl.ds(c0, chunk), :] = (_dot_nt(ds_ref[pl.ds(c0, chunk), :].astype(BF16), wb_ref[...])
                                           + d_ref[...] * dy_ref[pl.ds(c0, chunk), :])
        dwb_ref[...] = _dot_tn(u_ref[...].astype(BF16), ds_ref[...].astype(BF16))

    col = pl.BlockSpec((rows, LANES), lambda j: (0, j))
    return pl.pallas_call(
        body, name="s5_bwd", grid=(nb,),
        in_specs=[col, pl.BlockSpec((rows, s2), lambda j: (0, j)), col, col, col,
                  pl.BlockSpec((None, LANES, s2), lambda j: (j, 0, 0)), pl.BlockSpec((None, s2, LANES), lambda j: (j, 0, 0)),
                  pl.BlockSpec((1, LANES), lambda j: (0, j)), pl.BlockSpec((None, 1, s2), lambda j: (j, 0, 0))],
        out_specs=[col, pl.BlockSpec((None, LANES, s2), lambda j: (j, 0, 0)),
                   pl.BlockSpec((None, s2, LANES), lambda j: (j, 0, 0)), pl.BlockSpec((None, 1, s2), lambda j: (j, 0, 0)),
                   pl.BlockSpec((1, LANES), lambda j: (0, j))],
        out_shape=[jax.ShapeDtypeStruct((rows, nb * LANES), F32), jax.ShapeDtypeStruct((nb, LANES, s2), F32),
                   jax.ShapeDtypeStruct((nb, s2, LANES), F32), jax.ShapeDtypeStruct((nb, 1, s2), F32),
                   jax.ShapeDtypeStruct((1, nb * LANES), F32)],
        scratch_shapes=[pltpu.VMEM((rows, s2), F32), pltpu.VMEM((rows, LANES), F32)],
        compiler_params=_params(("parallel",)),
    )(proj, states, y_pre, dyg_a, dyg_b, wb, wc, d_skip, abar)


def _glu_norm_fwd(y_pre, z, w, *, tr=256):
    rows, width = y_pre.shape
    tr = _tile(rows, tr, SUBLANES)

    def body(y_ref, z_ref, w_ref, o_ref):
        v = _gelu(y_ref[...]) * jax.nn.sigmoid(z_ref[...])
        o_ref[...] = (v * _rms_rows(v) * w_ref[...]).astype(o_ref.dtype)

    blk = pl.BlockSpec((tr, width), lambda i: (i, 0))
    return pl.pallas_call(
        body, name="glu_norm_fwd", grid=(rows // tr,),
        in_specs=[blk, blk, pl.BlockSpec((1, width), lambda i: (0, 0))], out_specs=blk,
        out_shape=jax.ShapeDtypeStruct((rows, width), BF16), compiler_params=_params(("parallel",)),
    )(y_pre, z, w)


def _glu_norm_bwd(y_pre, z, w, dycat, *, tr=256):
    rows, width = y_pre.shape
    tr = _tile(rows, tr, SUBLANES)

    def body(y_ref, z_ref, w_ref, dy_ref, dz_ref, dg_ref, dw_ref, db_ref):
        yg = _gelu(y_ref[...])
        sg = jax.nn.sigmoid(z_ref[...])
        dv, dwp = _rmsnorm_bwd_rows(yg * sg, w_ref[...], dy_ref[...])
        dz = dv * yg * sg * (1.0 - sg)
        dz_ref[...] = dz
        dg_ref[...] = dv * sg
        dw_part = jnp.sum(dwp, axis=0, keepdims=True)
        db_part = jnp.sum(dz, axis=0, keepdims=True)

        @pl.when(pl.program_id(0) == 0)
        def _():
            dw_ref[...] = dw_part
            db_ref[...] = db_part

        @pl.when(pl.program_id(0) > 0)
        def _():
            dw_ref[...] += dw_part
            db_ref[...] += db_part

    blk = pl.BlockSpec((tr, width), lambda i: (i, 0))
    vec = pl.BlockSpec((1, width), lambda i: (0, 0))
    return pl.pallas_call(
        body, name="glu_norm_bwd", grid=(rows // tr,), in_specs=[blk, blk, vec, blk], out_specs=[blk, blk, vec, vec],
        out_shape=[jax.ShapeDtypeStruct((rows, width), F32)] * 2 + [jax.ShapeDtypeStruct((1, width), F32)] * 2,
        compiler_params=_params(("arbitrary",)),
    )(y_pre, z, w, dycat)


def _rope_tables(pos, freq, sign):
    rows = pos.shape[0]

    def body(p_ref, f_ref, s_ref, cos_ref, sin_ref):
        ang = p_ref[...] * f_ref[...]
        cos_ref[...] = jnp.cos(ang)
        sin_ref[...] = jnp.sin(ang) * s_ref[...]

    return pl.pallas_call(body, name="rope_tables", out_shape=[jax.ShapeDtypeStruct((rows, LANES), F32)] * 2)(pos, freq, sign)


def _rope(x, cos, sin_signed):
    lane = lax.broadcasted_iota(jnp.int32, x.shape, 1)
    half = QK_ROPE_DIM // 2
    swapped = jnp.where(lane < half, pltpu.roll(x, LANES - half, 1), pltpu.roll(x, half, 1))
    return x * cos + swapped * sin_signed


def _attn_prep(q, kv, proj, kpe_col, cos, sin, *, tr=256):
    rows = q.shape[0]
    heads = q.shape[1] // HEAD_SLOT
    tr = _tile(rows, tr, SUBLANES)

    def body(q_ref, kv_ref, kpe_ref, cos_ref, sin_ref, qc_ref, kc_ref, v_ref):
        c, s = cos_ref[...], sin_ref[...]
        qc_ref[:, :LANES] = q_ref[:, :LANES].astype(BF16)
        qc_ref[:, LANES:] = _rope(q_ref[:, LANES:], c, s).astype(BF16)
        kc_ref[:, :LANES] = kv_ref[:, :LANES].astype(BF16)
        kc_ref[:, LANES:] = _rope(kpe_ref[...], c, s).astype(BF16)
        v_ref[...] = kv_ref[:, LANES:].astype(BF16)

    slot = pl.BlockSpec((tr, HEAD_SLOT), lambda i, h: (i, h))
    tab = pl.BlockSpec((tr, LANES), lambda i, h: (i, 0))
    return pl.pallas_call(
        body, name="attn_prep", grid=(rows // tr, heads),
        in_specs=[slot, slot, pl.BlockSpec((tr, LANES), lambda i, h: (i, kpe_col)), tab, tab],
        out_specs=[slot, slot, pl.BlockSpec((tr, LANES), lambda i, h: (i, h))],
        out_shape=[jax.ShapeDtypeStruct((rows, heads * HEAD_SLOT), BF16)] * 2
        + [jax.ShapeDtypeStruct((rows, heads * LANES), BF16)],
        compiler_params=_params(("parallel", "parallel")),
    )(q, kv, proj, cos, sin)


def _causal(i, j, tq, tk):
    qpos = i * tq + lax.broadcasted_iota(jnp.int32, (tq, tk), 0)
    kpos = j * tk + lax.broadcasted_iota(jnp.int32, (tq, tk), 1)
    return kpos <= qpos


def _attn_fwd(qc, kc, vb, *, scale, tq=512):
    rows = qc.shape[0]
    heads = qc.shape[1] // HEAD_SLOT
    tq = _tile(rows, tq, SUBLANES)
    tk = tq

    def body(q_ref, k_ref, v_ref, o_ref, lse_ref):
        i = pl.program_id(1)
        q = q_ref[...]

        def step(j, carry):
            m, l, acc = carry
            k0 = pl.multiple_of(j * tk, tk)
            s = _dot_nt(q, k_ref[pl.ds(k0, tk), :]) * scale
            s = jnp.where(_causal(i, j, tq, tk), s, NEG_INF)
            m_new = jnp.maximum(m, jnp.max(s, axis=-1, keepdims=True))
            p = jnp.exp(s - m_new)
            alpha = jnp.exp(m - m_new)
            l = alpha * l + jnp.sum(p, axis=-1, keepdims=True)
            acc = alpha * acc + _dot_nn(p.astype(BF16), v_ref[pl.ds(k0, tk), :])
            return m_new, l, acc

        init = (jnp.full((tq, 1), NEG_INF, F32), jnp.zeros((tq, 1), F32), jnp.zeros((tq, LANES), F32))
        m, l, acc = lax.fori_loop(0, i + 1, step, init)
        o_ref[...] = acc / l
        lse_ref[...] = jnp.broadcast_to(m + jnp.log(l), (tq, LANES))

    return pl.pallas_call(
        body, name="attn_fwd", grid=(heads, rows // tq),
        in_specs=[pl.BlockSpec((tq, HEAD_SLOT), lambda h, i: (i, h)), pl.BlockSpec((rows, HEAD_SLOT), lambda h, i: (0, h)),
                  pl.BlockSpec((rows, LANES), lambda h, i: (0, h))],
        out_specs=[pl.BlockSpec((tq, LANES), lambda h, i: (i, h))] * 2,
        out_shape=[jax.ShapeDtypeStruct((rows, heads * LANES), F32)] * 2,
        compiler_params=_params(("parallel", "parallel")),
    )(qc, kc, vb)


def _attn_bwd_q(qc, kc, vb, o, do, lse, cos, sin, *, scale, tq=512):
    rows = qc.shape[0]
    heads = qc.shape[1] // HEAD_SLOT
    tq = _tile(rows, tq, SUBLANES)
    tk = tq

    def body(q_ref, k_ref, v_ref, o_ref, do_ref, lse_ref, cos_ref, sin_ref, dq_ref, delta_ref):
        i = pl.program_id(1)
        q = q_ref[...]
        dov = do_ref[...]
        delta = jnp.sum(dov * o_ref[...], axis=-1, keepdims=True)
        delta_ref[...] = jnp.broadcast_to(delta, (tq, LANES))
        dob = dov.astype(BF16)
        lse_col = lse_ref[:, :1]

        def step(j, dq):
            k0 = pl.multiple_of(j * tk, tk)
            kb = k_ref[pl.ds(k0, tk), :]
            s = _dot_nt(q, kb) * scale
            p = jnp.where(_causal(i, j, tq, tk), jnp.exp(s - lse_col), 0.0)
            dp = _dot_nt(dob, v_ref[pl.ds(k0, tk), :])
            ds = p * (dp - delta)
            return dq + _dot_nn(ds.astype(BF16), kb)

        dq = lax.fori_loop(0, i + 1, step, jnp.zeros((tq, HEAD_SLOT), F32)) * scale
        dq_ref[:, :LANES] = dq[:, :LANES]
        dq_ref[:, LANES:] = _rope(dq[:, LANES:], cos_ref[...], -sin_ref[...])

    qblk = pl.BlockSpec((tq, HEAD_SLOT), lambda h, i: (i, h))
    vblk = pl.BlockSpec((tq, LANES), lambda h, i: (i, h))
    tab = pl.BlockSpec((tq, LANES), lambda h, i: (i, 0))
    return pl.pallas_call(
        body, name="attn_bwd_q", grid=(heads, rows // tq),
        in_specs=[qblk, pl.BlockSpec((rows, HEAD_SLOT), lambda h, i: (0, h)), pl.BlockSpec((rows, LANES), lambda h, i: (0, h)),
                  vblk, vblk, vblk, tab, tab],
        out_specs=[qblk, vblk],
        out_shape=[jax.ShapeDtypeStruct((rows, heads * HEAD_SLOT), F32), jax.ShapeDtypeStruct((rows, heads * LANES), F32)],
        compiler_params=_params(("parallel", "parallel")),
    )(qc, kc, vb, o, do, lse, cos, sin)


def _attn_bwd_kv(qc, kc, vb, do, lse, delta, cos, sin, *, scale, tk=512):
    rows = qc.shape[0]
    heads = qc.shape[1] // HEAD_SLOT
    tk = _tile(rows, tk, SUBLANES)
    tq = tk
    nq = rows // tq

    def body(q_ref, k_ref, v_ref, do_ref, lse_ref, delta_ref, cos_ref, sin_ref, dkv_ref, dkpe_ref):
        j, h = pl.program_id(0), pl.program_id(1)
        kb, vv = k_ref[...], v_ref[...]

        def step(i, carry):
            dk, dv = carry
            q0 = pl.multiple_of(i * tq, tq)
            qb = q_ref[pl.ds(q0, tq), :]
            dob = do_ref[pl.ds(q0, tq), :].astype(BF16)
            s = _dot_nt(qb, kb) * scale
            p = jnp.where(_causal(i, j, tq, tk), jnp.exp(s - lse_ref[pl.ds(q0, tq), :1]), 0.0)
            dv = dv + _dot_tn(p.astype(BF16), dob)
            ds = p * (_dot_nt(dob, vv) - delta_ref[pl.ds(q0, tq), :1])
            dk = dk + _dot_tn(ds.astype(BF16), qb)
            return dk, dv

        dk, dv = lax.fori_loop(j, nq, step, (jnp.zeros((tk, HEAD_SLOT), F32), jnp.zeros((tk, LANES), F32)))
        dkv_ref[:, :LANES] = dk[:, :LANES] * scale
        dkv_ref[:, LANES:] = dv
        part = dk[:, LANES:] * scale

        @pl.when(h == 0)
        def _():
            dkpe_ref[...] = part

        @pl.when(h > 0)
        def _():
            dkpe_ref[...] += part

        @pl.when(h == heads - 1)
        def _():
            dkpe_ref[...] = _rope(dkpe_ref[...], cos_ref[...], -sin_ref[...])

    full_q = pl.BlockSpec((rows, HEAD_SLOT), lambda j, h: (0, h))
    full_v = pl.BlockSpec((rows, LANES), lambda j, h: (0, h))
    tab = pl.BlockSpec((tk, LANES), lambda j, h: (j, 0))
    return pl.pallas_call(
        body, name="attn_bwd_kv", grid=(rows // tk, heads),
        in_specs=[full_q, pl.BlockSpec((tk, HEAD_SLOT), lambda j, h: (j, h)), pl.BlockSpec((tk, LANES), lambda j, h: (j, h)),
                  full_v, full_v, full_v, tab, tab],
        out_specs=[pl.BlockSpec((tk, HEAD_SLOT), lambda j, h: (j, h)), pl.BlockSpec((tk, LANES), lambda j, h: (j, 0))],
        out_shape=[jax.ShapeDtypeStruct((rows, heads * HEAD_SLOT), F32), jax.ShapeDtypeStruct((rows, LANES), F32)],
        compiler_params=_params(("parallel", "arbitrary")),
    )(qc, kc, vb, do, lse, delta, cos, sin)


def _shift_down(x, d):
    row = lax.broadcasted_iota(jnp.int32, x.shape, 0)
    return jnp.where(row >= d, pltpu.roll(x, d, 0), 0.0)


def _shift_up(x, d):
    rows = x.shape[0]
    row = lax.broadcasted_iota(jnp.int32, x.shape, 0)
    return jnp.where(row < rows - d, pltpu.roll(x, rows - d, 0), 0.0)


def _conv3(a, w, b):
    return w[2:3, :] * a + w[1:2, :] * _shift_down(a, 1) + w[0:1, :] * _shift_down(a, 2) + b


def _conv_gate_fwd(a, conv_w, conv_b, *, tc=256):
    rows, f2 = a.shape
    f = f2 // 2
    tc = _tile(f, tc)
    nc = f // tc

    def body(ag_ref, av_ref, wg_ref, wv_ref, bg_ref, bv_ref, o_ref):
        gate = _conv3(ag_ref[...], wg_ref[...], bg_ref[...])
        val = _conv3(av_ref[...], wv_ref[...], bv_ref[...])
        o_ref[...] = (gate * jax.nn.sigmoid(gate) * val).astype(o_ref.dtype)

    return pl.pallas_call(
        body, name="conv_gate_fwd", grid=(nc,),
        in_specs=[pl.BlockSpec((rows, tc), lambda j: (0, j)), pl.BlockSpec((rows, tc), lambda j: (0, j + nc)),
                  pl.BlockSpec((SUBLANES, tc), lambda j: (0, j)), pl.BlockSpec((SUBLANES, tc), lambda j: (0, j + nc)),
                  pl.BlockSpec((1, tc), lambda j: (0, j)), pl.BlockSpec((1, tc), lambda j: (0, j + nc))],
        out_specs=pl.BlockSpec((rows, tc), lambda j: (0, j)),
        out_shape=jax.ShapeDtypeStruct((rows, f), BF16), compiler_params=_params(("parallel",)),
    )(a, a, conv_w, conv_w, conv_b, conv_b)


def _conv_gate_bwd(a, conv_w, conv_b, dg, *, tc=256):
    rows, f2 = a.shape
    f = f2 // 2
    tc = _tile(f, tc)
    nc = f // tc

    def conv_bwd(a_val, w, d_out):
        da = w[2:3, :] * d_out + w[1:2, :] * _shift_up(d_out, 1) + w[0:1, :] * _shift_up(d_out, 2)
        db = jnp.sum(d_out, axis=0, keepdims=True)
        row = lax.broadcasted_iota(jnp.int32, (SUBLANES, a_val.shape[1]), 0)
        dw = jnp.zeros((SUBLANES, a_val.shape[1]), F32)
        for tap in range(3):
            t = jnp.sum(d_out * (_shift_down(a_val, 2 - tap) if tap < 2 else a_val), axis=0, keepdims=True)
            dw = jnp.where(row == tap, t, dw)
        return da, dw, db

    def body(ag_ref, av_ref, wg_ref, wv_ref, bg_ref, bv_ref, dg_ref, da_ref, dw_ref, db_ref):
        ag, av, wg, wv = ag_ref[...], av_ref[...], wg_ref[...], wv_ref[...]
        gate = _conv3(ag, wg, bg_ref[...])
        val = _conv3(av, wv, bv_ref[...])
        sg = jax.nn.sigmoid(gate)
        dgv = dg_ref[...]
        d_gate = dgv * val * sg * (1.0 + gate * (1.0 - sg))
        d_val = dgv * gate * sg
        for half, (a_val, w, d_out) in enumerate(((ag, wg, d_gate), (av, wv, d_val))):
            da, dw, db = conv_bwd(a_val, w, d_out)
            da_ref[half] = da.astype(da_ref.dtype)
            dw_ref[half] = dw
            db_ref[half] = db

    lo = lambda j: (0, j)
    hi = lambda j: (0, j + nc)
    both = lambda j: (0, 0, j)
    return pl.pallas_call(
        body, name="conv_gate_bwd", grid=(nc,),
        in_specs=[pl.BlockSpec((rows, tc), lo), pl.BlockSpec((rows, tc), hi), pl.BlockSpec((SUBLANES, tc), lo),
                  pl.BlockSpec((SUBLANES, tc), hi), pl.BlockSpec((1, tc), lo), pl.BlockSpec((1, tc), hi),
                  pl.BlockSpec((rows, tc), lo)],
        out_specs=[pl.BlockSpec((2, rows, tc), both), pl.BlockSpec((2, SUBLANES, tc), both), pl.BlockSpec((2, 1, tc), both)],
        out_shape=[jax.ShapeDtypeStruct((2, rows, f), BF16), jax.ShapeDtypeStruct((2, SUBLANES, f), F32),
                   jax.ShapeDtypeStruct((2, 1, f), F32)],
        compiler_params=_params(("parallel",)),
    )(a, a, conv_w, conv_w, conv_b, conv_b, dg)


def _wgrad(a, b, rows, cols, row_sharded, name, **kw):
    make, (sr, sc) = _wgrad_blocks(rows, cols, row_sharded)
    tm = kw.pop("tm", _tile(sr, 512))
    tn = kw.pop("tn", _tile(sc, 1024))
    return _matmul(a, b, mode="tn", name=name, tm=tm, tn=tn, out_blocks=make, **kw)


def _block_diag(x):
    nb, g, r, c = x.shape
    eye = jnp.eye(g, dtype=x.dtype)
    return (x[:, :, :, None, :] * eye[None, :, None, :, None]).reshape(nb, g * r, g * c)


def _block_diag_part(x, r, c):
    nb = x.shape[0]
    g = GROUPS_PER_BATCH
    eye = jnp.eye(g, dtype=x.dtype)
    return jnp.sum(x.reshape(nb, g, r, g, c) * eye[None, :, None, :, None], axis=3)


def _local_step(x, posf, target, w):
    rows, d = x.shape
    width = w["ssm_d"].shape[1]
    qr, kvr = w["mla_q_norm_w"].shape[1], w["mla_kv_norm_w"].shape[1]
    heads = w["mla_w_ukv"].shape[1] // HEAD_SLOT
    f2 = w["ffn_w_up"].shape[1]
    inp = w["w_in"].shape[1]
    groups = width // SSM_GROUP
    nb = groups // GROUPS_PER_BATCH
    scale = (QK_NOPE_DIM + QK_ROPE_DIM) ** -0.5
    g = {}

    hn = _rmsnorm_fwd(x, w["attn_norm_w"], name="attn_norm")
    proj = _matmul(hn, w["w_in"], mode="nn", name="in_proj")

    ar, ai, bbr, bbi = _s5_params(w["ssm_lambda_re"], w["ssm_lambda_im"], w["ssm_log_dt"], w["ssm_b_re"], w["ssm_b_im"])

    def b_band(bb):
        return _block_diag(bb.reshape(nb, GROUPS_PER_BATCH, SSM_STATE, SSM_GROUP).transpose(0, 1, 3, 2))

    def c_band(c):
        return _block_diag(c.reshape(nb, GROUPS_PER_BATCH, SSM_GROUP, SSM_STATE).transpose(0, 1, 3, 2))

    wb = jnp.concatenate([b_band(bbr), b_band(bbi)], axis=2).astype(BF16)
    wc = jnp.concatenate([c_band(w["ssm_c_re"]), -c_band(w["ssm_c_im"])], axis=1).astype(BF16)
    abar = jnp.concatenate([ar.reshape(nb, 1, STATE_PER_BATCH), ai.reshape(nb, 1, STATE_PER_BATCH)], axis=2)
    states, y_pre, yg = _s5_fwd(proj, wb, wc, w["ssm_d"], abar)
    z = _matmul(yg, w["ssm_w_glu"], mode="nn", name="glu_proj", bias=w["ssm_b_glu"])
    ys = _glu_norm_fwd(y_pre, z, w["ssm_out_norm_w"])

    q_col, kv_col, kpe_col = width // qr, (width + qr) // kvr, (width + qr + kvr) // LANES
    assert width % qr == 0 and (width + qr) % kvr == 0
    qn = _rmsnorm_fwd(proj, w["mla_q_norm_w"], name="q_norm", width=qr, col=q_col)
    kvn = _rmsnorm_fwd(proj, w["mla_kv_norm_w"], name="kv_norm", width=kvr, col=kv_col)
    q = _matmul(qn, w["mla_w_uq"], mode="nn", name="q_proj")
    kv = _matmul(kvn, w["mla_w_ukv"], mode="nn", name="kv_proj")
    half = QK_ROPE_DIM // 2
    inv_freq = ROPE_THETA ** (-jnp.arange(0, QK_ROPE_DIM, 2, dtype=F32) / QK_ROPE_DIM)
    zeros = jnp.zeros((LANES - QK_ROPE_DIM,), F32)
    freq = jnp.concatenate([inv_freq, inv_freq, zeros]).reshape(1, LANES)
    sign = jnp.concatenate([-jnp.ones((half,), F32), jnp.ones((half,), F32), zeros]).reshape(1, LANES)
    cos, sin = _rope_tables(posf, freq, sign)
    qc, kc, vb = _attn_prep(q, kv, proj, kpe_col, cos, sin)
    o, lse = _attn_fwd(qc, kc, vb, scale=scale, tq=ATTN_BLOCK)
    ym = _rmsnorm_fwd(o, w["mla_out_norm_w"], name="mla_out_norm")
    ycat = jnp.concatenate([ys, ym], axis=1)
    h1 = _matmul(ycat, w["w_out"], mode="nn", name="out_proj", add=x)

    hn2 = _rmsnorm_fwd(h1, w["ffn_norm_w"], name="ffn_norm")
    a = _matmul(hn2, w["ffn_w_up"], mode="nn", name="ffn_up")
    gated = _conv_gate_fwd(a, w["ffn_conv_w"], w["ffn_conv_b"])
    h2 = _matmul(gated, w["ffn_w_down"], mode="nn", name="ffn_down", add=h1, tk=2816)
    loss_tile, dh2, g["final_norm_w"] = _final_norm_loss(h2, w["final_norm_w"], target)

    dgated = _matmul(dh2, w["ffn_w_down"], mode="nt", name="ffn_down_dx")
    g["ffn_w_down"] = _wgrad(gated, dh2, f2 // 2, d, True, "ffn_down_dw", tm=f2 // 2 // N_CHIPS, tn=512)
    da, dcw, dcb = _conv_gate_bwd(a, w["ffn_conv_w"], w["ffn_conv_b"], dgated)
    g["ffn_conv_w"] = jnp.concatenate([dcw[0, :3], dcw[1, :3]], axis=1)
    g["ffn_conv_b"] = jnp.concatenate([dcb[0], dcb[1]], axis=1)
    dhn2 = _matmul(da, w["ffn_w_up"], mode="nt", name="ffn_up_dx", a_split=True, tk=_tile(f2 // 2, 2816))
    g["ffn_w_up"] = _wgrad(hn2, da, d, f2, False, "ffn_up_dw", b_split=True)
    dh1, g["ffn_norm_w"] = _rmsnorm_bwd(h1, w["ffn_norm_w"], dhn2, name="ffn_norm_bwd", add=dh2)

    dycat = _matmul(dh1, w["w_out"], mode="nt", name="out_proj_dx")
    g["w_out"] = _wgrad(ycat, dh1, 2 * width, d, True, "out_proj_dw")

    do, g["mla_out_norm_w"] = _rmsnorm_bwd(o, w["mla_out_norm_w"], dycat, name="mla_out_norm_bwd", width=width, dy_col=1)
    dq, delta = _attn_bwd_q(qc, kc, vb, o, do, lse, cos, sin, scale=scale, tq=ATTN_BLOCK)
    dkv, dkpe = _attn_bwd_kv(qc, kc, vb, do, lse, delta, cos, sin, scale=scale, tk=ATTN_BLOCK)
    g["mla_w_uq"] = _wgrad(qn, dq, qr, heads * HEAD_SLOT, False, "q_proj_dw")
    dqn = _matmul(dq, w["mla_w_uq"], mode="nt", name="q_proj_dx")
    dcq, g["mla_q_norm_w"] = _rmsnorm_bwd(proj, w["mla_q_norm_w"], dqn, name="q_norm_bwd", width=qr, col=q_col)
    g["mla_w_ukv"] = _wgrad(kvn, dkv, kvr, heads * HEAD_SLOT, False, "kv_proj_dw")
    dkvn = _matmul(dkv, w["mla_w_ukv"], mode="nt", name="kv_proj_dx")
    dckv, g["mla_kv_norm_w"] = _rmsnorm_bwd(proj, w["mla_kv_norm_w"], dkvn, name="kv_norm_bwd", width=kvr, col=kv_col)

    dz, dyg_a, g["ssm_out_norm_w"], g["ssm_b_glu"] = _glu_norm_bwd(y_pre, z, w["ssm_out_norm_w"], dycat)
    dyg_b = _matmul(dz, w["ssm_w_glu"], mode="nt", name="glu_proj_dx")
    g["ssm_w_glu"] = _wgrad(yg, dz, width, width, True, "glu_proj_dw")
    du, dwb, dwc, dabar, g["ssm_d"] = _s5_bwd(proj, states, y_pre, dyg_a, dyg_b, wb, wc, w["ssm_d"], abar)

    def b_unband(x):
        return _block_diag_part(x, SSM_GROUP, SSM_STATE).transpose(0, 1, 3, 2).reshape(groups, SSM_STATE * SSM_GROUP)

    def c_unband(x):
        return _block_diag_part(x, SSM_STATE, SSM_GROUP).transpose(0, 1, 3, 2).reshape(groups, SSM_GROUP, SSM_STATE)

    st = STATE_PER_BATCH
    g["ssm_c_re"] = c_unband(dwc[:, :st, :])
    g["ssm_c_im"] = -c_unband(dwc[:, st:, :])
    d_ar = dabar[:, 0, :st].reshape(groups, SSM_STATE)
    d_ai = dabar[:, 0, st:].reshape(groups, SSM_STATE)
    (g["ssm_lambda_re"], g["ssm_lambda_im"], g["ssm_log_dt"], g["ssm_b_re"], g["ssm_b_im"]) = _s5_params_bwd(
        w["ssm_lambda_re"], w["ssm_lambda_im"], w["ssm_log_dt"], w["ssm_b_re"], w["ssm_b_im"], d_ar, d_ai,
        b_unband(dwb[:, :, :st]), b_unband(dwb[:, :, st:]))

    pad = jnp.zeros((rows, inp - (width + qr + kvr + LANES)), F32)
    dproj = jnp.concatenate([du, dcq, dckv, dkpe, pad], axis=1)
    g["w_in"] = _wgrad(hn, dproj, d, inp, True, "in_proj_dw")
    dhn = _matmul(dproj, w["w_in"], mode="nt", name="in_proj_dx")
    dx, g["attn_norm_w"] = _rmsnorm_bwd(x, w["attn_norm_w"], dhn, name="attn_norm_bwd", add=dh1)
    return loss_tile, dx, g


ANY = pl.BlockSpec(memory_space=pl.ANY)
MESH = pl.DeviceIdType.MESH


def _mesh_pos():
    return lax.axis_index("x"), lax.axis_index("y"), lax.axis_index("c")


def _other_chips(x, y):
    return [(1 - x, y), (x, 1 - y), (1 - x, 1 - y)]


def _remote(src, dst, send_sems, recv_sems, k, to):
    return pltpu.make_async_remote_copy(src_ref=src, dst_ref=dst, send_sem=send_sems.at[k], recv_sem=recv_sems.at[k],
                                        device_id=to, device_id_type=MESH)


def _gather_weights(shards):
    n = len(shards)
    out_shape = []
    for arr, row_sharded, _ in shards:
        rs, cs = arr.shape
        out_shape.append(jax.ShapeDtypeStruct((N_CHIPS * rs, cs) if row_sharded else (rs, N_CHIPS * cs), arr.dtype))
    n_remote = sum(3 if direct else 6 for _, _, direct in shards)

    def body(*refs):
        ins, outs = refs[:n], refs[n:2 * n]
        send_sems, recv_sems, local_sems = refs[2 * n:]
        x, y, c = _mesh_pos()
        sibling = (x, y, 1 - c)
        chips = _other_chips(x, y)
        me = 2 * x + y

        def window(t, piece, half):
            (rs, cs), row_sharded = shards[t][0].shape, shards[t][1]
            hr = rs // 2
            if row_sharded:
                if half is None:
                    return outs[t].at[pl.ds(piece * rs, rs), :]
                return outs[t].at[pl.ds(piece * rs + half * hr, hr), :]
            if half is None:
                return outs[t].at[:, pl.ds(piece * cs, cs)]
            return outs[t].at[pl.ds(half * hr, hr), pl.ds(piece * cs, cs)]

        local, first, passed, base = [], [], [], []
        k = 0
        for t, (arr, _, direct) in enumerate(shards):
            base.append(k)
            cp = pltpu.make_async_copy(ins[t], window(t, me, None), local_sems.at[t])
            cp.start()
            local.append(cp)
            hr = arr.shape[0] // 2
            for j, chip in enumerate(chips):
                if direct:
                    cp = _remote(ins[t], window(t, me, None), send_sems, recv_sems, k + j, (*chip, c))
                else:
                    cp = _remote(ins[t].at[pl.ds(c * hr, hr), :], window(t, me, c), send_sems, recv_sems, k + j, (*chip, c))
                cp.start()
                first.append(cp)
            k += 3 if direct else 6
        for t, (arr, _, direct) in enumerate(shards):
            for j, (px, py) in enumerate(chips):
                piece = 2 * px + py
                if direct:
                    _remote(ins[t], window(t, piece, None), send_sems, recv_sems, base[t] + j, sibling).wait_recv()
                    continue
                arrived = window(t, piece, c)
                _remote(arrived, arrived, send_sems, recv_sems, base[t] + j, sibling).wait_recv()
                cp = _remote(arrived, arrived, send_sems, recv_sems, base[t] + 3 + j, sibling)
                cp.start()
                passed.append(cp)
        for t, (arr, _, direct) in enumerate(shards):
            if direct:
                continue
            for j, (px, py) in enumerate(chips):
                other = window(t, 2 * px + py, 1 - c)
                _remote(other, other, send_sems, recv_sems, base[t] + 3 + j, sibling).wait_recv()
        for cp in first + passed:
            cp.wait_send()
        for cp in local:
            cp.wait()

    return pl.pallas_call(
        body, name="gather_weights", in_specs=[ANY] * n, out_specs=[ANY] * n, out_shape=out_shape,
        scratch_shapes=[pltpu.SemaphoreType.DMA((n_remote,)), pltpu.SemaphoreType.DMA((n_remote,)),
                        pltpu.SemaphoreType.DMA((n,))],
    )(*[arr for arr, _, _ in shards])


def _exchange(name, arrays, out_shapes, plan, n_copies):
    n = len(arrays)

    def body(*refs):
        ins, outs = refs[:n], refs[n:n + len(out_shapes)]
        send_sems, recv_sems, local_sems = refs[n + len(out_shapes):]
        local, sends, recvs = plan(ins, outs, send_sems, recv_sems, local_sems)
        for cp in local + sends:
            cp.start()
        for cp in recvs:
            cp.wait_recv()
        for cp in sends:
            cp.wait_send()
        for cp in local:
            cp.wait()

    return pl.pallas_call(
        body, name=name, in_specs=[ANY] * n, out_specs=[ANY] * len(out_shapes), out_shape=out_shapes,
        scratch_shapes=[pltpu.SemaphoreType.DMA((n_copies,)), pltpu.SemaphoreType.DMA((n_copies,)),
                        pltpu.SemaphoreType.DMA((max(n, 1),))],
    )(*arrays)


def _swap_other_half(grads):
    def plan(ins, outs, send_sems, recv_sems, local_sems):
        x, y, c = _mesh_pos()
        sends = [_remote(ins[t].at[1 - c], outs[t], send_sems, recv_sems, t, (x, y, 1 - c)) for t in range(len(ins))]
        return [], sends, sends

    shapes = [jax.ShapeDtypeStruct(g.shape[1:], g.dtype) for g in grads]
    return _exchange("grad_swap_halves", grads, shapes, plan, len(grads))


def _scatter_pieces(sums):
    def plan(ins, outs, send_sems, recv_sems, local_sems):
        x, y, c = _mesh_pos()
        sends = []
        for t in range(len(ins)):
            for j, (px, py) in enumerate(_other_chips(x, y)):
                sends.append(_remote(ins[t].at[2 * px + py], outs[t].at[j], send_sems, recv_sems, 3 * t + j, (px, py, c)))
        return [], sends, sends

    shapes = [jax.ShapeDtypeStruct((3,) + s.shape[1:], s.dtype) for s in sums]
    return _exchange("grad_scatter_pieces", sums, shapes, plan, 3 * len(sums))


def _join_halves(halves):
    def plan(ins, outs, send_sems, recv_sems, local_sems):
        x, y, c = _mesh_pos()
        local = [pltpu.make_async_copy(ins[t], outs[t].at[c], local_sems.at[t]) for t in range(len(ins))]
        sends = [_remote(ins[t], outs[t].at[c], send_sems, recv_sems, t, (x, y, 1 - c)) for t in range(len(ins))]
        recvs = [_remote(ins[t], outs[t].at[1 - c], send_sems, recv_sems, t, (x, y, 1 - c)) for t in range(len(ins))]
        return local, sends, recvs

    shapes = [jax.ShapeDtypeStruct((2,) + h.shape, h.dtype) for h in halves]
    return _exchange("grad_join_halves", halves, shapes, plan, len(halves))


def _all_to_all_small(packed):
    def plan(ins, outs, send_sems, recv_sems, local_sems):
        x, y, c = _mesh_pos()
        me = 4 * x + 2 * y + c
        local = [pltpu.make_async_copy(ins[0], outs[0].at[me], local_sems.at[0])]
        sends, recvs = [], []
        for mask in range(1, 8):
            fx, fy, fc = (mask >> 2) & 1, (mask >> 1) & 1, mask & 1
            px, py, pc = x ^ fx, y ^ fy, c ^ fc
            sends.append(_remote(ins[0], outs[0].at[me], send_sems, recv_sems, mask - 1, (px, py, pc)))
            recvs.append(_remote(ins[0], outs[0].at[4 * px + 2 * py + pc], send_sems, recv_sems, mask - 1, (px, py, pc)))
        return local, sends, recvs

    shape = jax.ShapeDtypeStruct((N_CHIPS * N_CORES,) + packed.shape, packed.dtype)
    return _exchange("small_grads_all_to_all", [packed], [shape], plan, 7)[0]


def _add_other_half(g4, got, c_idx, name):
    _, pieces, sr, sc = g4.shape
    tr = _tile(sr, 256, SUBLANES)

    def body(c_ref, a_ref, b_ref, o_ref):
        o_ref[...] = a_ref[...] + b_ref[...]

    blk = pl.BlockSpec((None, tr, sc), lambda p, i, c_ref: (p, i, 0))
    return pl.pallas_call(
        body, name=name, out_shape=jax.ShapeDtypeStruct((pieces, sr, sc), F32),
        grid_spec=pltpu.PrefetchScalarGridSpec(
            num_scalar_prefetch=1, grid=(pieces, sr // tr),
            in_specs=[pl.BlockSpec((None, None, tr, sc), lambda p, i, c_ref: (c_ref[0], p, i, 0)), blk], out_specs=blk),
        compiler_params=_params(("parallel", "parallel")),
    )(c_idx, g4, got)


def _add_pieces(sums, got, piece_idx, name):
    _, sr, sc = sums.shape
    tr = _tile(sr, 256, SUBLANES)

    def body(p_ref, a_ref, b_ref, o_ref):
        o_ref[...] = ((a_ref[...] + b_ref[0]) + b_ref[1]) + b_ref[2]

    return pl.pallas_call(
        body, name=name, out_shape=jax.ShapeDtypeStruct((sr, sc), F32),
        grid_spec=pltpu.PrefetchScalarGridSpec(
            num_scalar_prefetch=1, grid=(sr // tr,),
            in_specs=[pl.BlockSpec((None, tr, sc), lambda i, p_ref: (p_ref[0], i, 0)),
                      pl.BlockSpec((3, tr, sc), lambda i, p_ref: (0, i, 0))],
            out_specs=pl.BlockSpec((tr, sc), lambda i, p_ref: (i, 0))),
        compiler_params=_params(("parallel",)),
    )(piece_idx, sums, got)


def _sum_slots(slots):
    n, rows, lanes = slots.shape
    tr = _tile(rows, 512, SUBLANES)

    def body(s_ref, o_ref):
        acc = s_ref[0]
        for k in range(1, n):
            acc = acc + s_ref[k]
        o_ref[...] = acc

    return pl.pallas_call(
        body, name="small_grads_sum", grid=(rows // tr,),
        in_specs=[pl.BlockSpec((n, tr, lanes), lambda i: (0, i, 0))], out_specs=pl.BlockSpec((tr, lanes), lambda i: (i, 0)),
        out_shape=jax.ShapeDtypeStruct((rows, lanes), F32), compiler_params=_params(("parallel",)),
    )(slots)


def _adamw(w, g, m, v, name):
    rows, cols = w.shape
    tr = _tile(rows, max(SUBLANES, (1 << 19) // max(cols, 1) // SUBLANES * SUBLANES), SUBLANES)

    def body(w_ref, g_ref, m_ref, v_ref, d_ref, nm_ref, nv_ref):
        gv = g_ref[...]
        nm = ADAM_B1 * m_ref[...] + (1.0 - ADAM_B1) * gv
        nv = ADAM_B2 * v_ref[...] + (1.0 - ADAM_B2) * (gv * gv)
        m_hat = nm / (1.0 - ADAM_B1 ** ADAM_STEP)
        v_hat = nv / (1.0 - ADAM_B2 ** ADAM_STEP)
        d_ref[...] = -ADAM_LR * (m_hat / (jnp.sqrt(v_hat) + ADAM_EPS) + ADAM_WD * w_ref[...])
        nm_ref[...] = nm
        nv_ref[...] = nv

    blk = pl.BlockSpec((tr, cols), lambda i: (i, 0))
    return pl.pallas_call(
        body, name=name, grid=(rows // tr,), in_specs=[blk] * 4, out_specs=[blk] * 3,
        out_shape=[jax.ShapeDtypeStruct((rows, cols), F32)] * 3, compiler_params=_params(("parallel",)),
    )(w, g, m, v)


WEIGHTS = ['attn_norm_w', 'w_in', 'ssm_lambda_re', 'ssm_lambda_im', 'ssm_log_dt', 'ssm_b_re', 'ssm_b_im', 'ssm_c_re',
           'ssm_c_im', 'ssm_d', 'ssm_w_glu', 'ssm_b_glu', 'mla_q_norm_w', 'mla_w_uq', 'mla_kv_norm_w', 'mla_w_ukv',
           'ssm_out_norm_w', 'mla_out_norm_w', 'w_out', 'ffn_norm_w', 'ffn_w_up', 'ffn_conv_w', 'ffn_conv_b',
           'ffn_w_down', 'final_norm_w']
SHARDED = {'w_in': True, 'ssm_w_glu': True, 'mla_w_uq': False, 'mla_w_ukv': False, 'w_out': True, 'ffn_w_up': False,
           'ffn_w_down': True}
SMALL = [n for n in WEIGHTS if n not in SHARDED and n != 'ffn_conv_w']
ROPE_PAD = HEAD_SLOT - QK_NOPE_DIM - QK_ROPE_DIM


def _pad_heads(w_uq, heads):
    qr = w_uq.shape[0]
    w3 = w_uq.reshape(qr, heads, QK_NOPE_DIM + QK_ROPE_DIM)
    return jnp.concatenate([w3, jnp.zeros((qr, heads, ROPE_PAD), w_uq.dtype)], axis=2).reshape(qr, heads * HEAD_SLOT)


def _unpad_heads(g_uq, heads):
    qr = g_uq.shape[0]
    return g_uq.reshape(qr, heads, HEAD_SLOT)[:, :, :QK_NOPE_DIM + QK_ROPE_DIM].reshape(qr, -1)


def _step(args):
    x, positions, target = args["x"][0], args["positions"], args["loss_target"][0]
    rows = x.shape[0]
    p = {n: args[n] for n in WEIGHTS}
    xi, yi, ci = _mesh_pos()
    piece = 2 * xi + yi

    w_in = p["w_in"][0]
    in_width = w_in.shape[1]
    in_pad = (-in_width) % (2 * LANES)
    heads_here = p["mla_w_uq"].shape[2] // (QK_NOPE_DIM + QK_ROPE_DIM)
    shards = {
        "w_in": jnp.pad(w_in, ((0, 0), (0, in_pad))).astype(BF16),
        "ssm_w_glu": p["ssm_w_glu"][0].astype(BF16),
        "mla_w_uq": _pad_heads(p["mla_w_uq"][0], heads_here).astype(BF16),
        "mla_w_ukv": p["mla_w_ukv"][0].astype(BF16),
        "w_out": p["w_out"][0].astype(BF16),
        "ffn_w_up": p["ffn_w_up"][0].astype(BF16),
        "ffn_w_down": p["ffn_w_down"][0].astype(BF16),
    }
    conv_w = jnp.pad(p["ffn_conv_w"][0], ((0, SUBLANES - p["ffn_conv_w"].shape[1]), (0, 0)))
    order = list(SHARDED)
    gathered = _gather_weights([(shards[n], SHARDED[n], False) for n in order] + [(conv_w, False, True)])
    w = dict(zip(order + ["ffn_conv_w"], gathered))
    groups = p["ssm_lambda_re"].shape[1]
    w.update({
        "attn_norm_w": p["attn_norm_w"], "ssm_lambda_re": p["ssm_lambda_re"][0], "ssm_lambda_im": p["ssm_lambda_im"][0],
        "ssm_log_dt": p["ssm_log_dt"].reshape(groups, 1), "ssm_b_re": p["ssm_b_re"].reshape(groups, -1),
        "ssm_b_im": p["ssm_b_im"].reshape(groups, -1), "ssm_c_re": p["ssm_c_re"][0], "ssm_c_im": p["ssm_c_im"][0],
        "ssm_d": p["ssm_d"], "ssm_b_glu": p["ssm_b_glu"], "mla_q_norm_w": p["mla_q_norm_w"],
        "mla_kv_norm_w": p["mla_kv_norm_w"], "ssm_out_norm_w": p["ssm_out_norm_w"], "mla_out_norm_w": p["mla_out_norm_w"],
        "ffn_norm_w": p["ffn_norm_w"], "ffn_conv_b": p["ffn_conv_b"], "final_norm_w": p["final_norm_w"].reshape(1, -1),
    })

    loss_tile, dx, g = _local_step(x, positions.reshape(rows, 1).astype(F32), target, w)
    loss = lax.psum(loss_tile[0, 0], ("x", "y", "c"))

    c_idx, piece_idx = ci.reshape(1).astype(jnp.int32), piece.reshape(1).astype(jnp.int32)
    got = _swap_other_half([g[n] for n in order])
    sums = [_add_other_half(g[n], got[t], c_idx, "grad_add_half_" + n) for t, n in enumerate(order)]
    got = _scatter_pieces(sums)
    halves = [_add_pieces(sums[t], got[t], piece_idx, "grad_add_pieces_" + n) for t, n in enumerate(order)]
    joined = _join_halves(halves)
    grads = {}
    for t, n in enumerate(order):
        j = joined[t]
        grads[n] = jnp.concatenate([j[0], j[1]], axis=1) if SHARDED[n] else j.reshape(2 * j.shape[1], j.shape[2])
    grads["w_in"] = grads["w_in"][:, :in_width]
    grads["mla_w_uq"] = _unpad_heads(grads["mla_w_uq"], heads_here)

    flat = [g[n].reshape(-1) for n in SMALL] + [g["ffn_conv_w"].reshape(-1)]
    sizes = [f.shape[0] for f in flat]
    total = sum(sizes)
    tile_elems = SUBLANES * LANES
    padded = -(-total // tile_elems) * tile_elems

    def pack(parts):
        parts = list(parts)
        have = sum(q.shape[0] for q in parts)
        return jnp.concatenate(parts + [jnp.zeros((padded - have,), F32)]).reshape(padded // LANES, LANES)

    small_sum = _sum_slots(_all_to_all_small(pack(flat)))
    flat_sum = small_sum.reshape(-1)
    offs = [0]
    for s in sizes:
        offs.append(offs[-1] + s)
    for k, n in enumerate(SMALL):
        grads[n] = flat_sum[offs[k]:offs[k + 1]].reshape(p[n].shape)
    taps, cols_here = p["ffn_conv_w"].shape[1], p["ffn_conv_w"].shape[2]
    conv_full = flat_sum[offs[len(SMALL)]:offs[len(SMALL) + 1]].reshape(taps, N_CHIPS * cols_here)
    grads["ffn_conv_w"] = lax.dynamic_slice_in_dim(conv_full, piece * cols_here, cols_here, axis=1)

    delta, new_m, new_v = {}, {}, {}
    for n in list(SHARDED) + ["ffn_conv_w"]:
        shape = p[n].shape
        d2, m2, v2 = _adamw(p[n].reshape(shape[1:]), grads[n], args["m_" + n].reshape(shape[1:]),
                            args["v_" + n].reshape(shape[1:]), "adamw_" + n)
        grads[n] = grads[n].reshape(shape)
        delta[n], new_m[n], new_v[n] = d2.reshape(shape), m2.reshape(shape), v2.reshape(shape)
    d2, m2, v2 = _adamw(pack(p[n].reshape(-1) for n in SMALL), small_sum, pack(args["m_" + n].reshape(-1) for n in SMALL),
                        pack(args["v_" + n].reshape(-1) for n in SMALL), "adamw_small")
    for k, n in enumerate(SMALL):
        for src, dst in ((d2, delta), (m2, new_m), (v2, new_v)):
            dst[n] = src.reshape(-1)[offs[k]:offs[k + 1]].reshape(p[n].shape)

    return (loss, dx[None], *[grads[n] for n in WEIGHTS], *[delta[n] for n in WEIGHTS],
            *[new_m[n] for n in WEIGHTS], *[new_v[n] for n in WEIGHTS])


def kernel(x, positions, attn_norm_w, w_in, ssm_lambda_re, ssm_lambda_im, ssm_log_dt, ssm_b_re, ssm_b_im, ssm_c_re, ssm_c_im, ssm_d, ssm_w_glu, ssm_b_glu, mla_q_norm_w, mla_w_uq, mla_kv_norm_w, mla_w_ukv, ssm_out_norm_w, mla_out_norm_w, w_out, ffn_norm_w, ffn_w_up, ffn_conv_w, ffn_conv_b, ffn_w_down, final_norm_w, loss_target, m_attn_norm_w, m_w_in, m_ssm_lambda_re, m_ssm_lambda_im, m_ssm_log_dt, m_ssm_b_re, m_ssm_b_im, m_ssm_c_re, m_ssm_c_im, m_ssm_d, m_ssm_w_glu, m_ssm_b_glu, m_mla_q_norm_w, m_mla_w_uq, m_mla_kv_norm_w, m_mla_w_ukv, m_ssm_out_norm_w, m_mla_out_norm_w, m_w_out, m_ffn_norm_w, m_ffn_w_up, m_ffn_conv_w, m_ffn_conv_b, m_ffn_w_down, m_final_norm_w, v_attn_norm_w, v_w_in, v_ssm_lambda_re, v_ssm_lambda_im, v_ssm_log_dt, v_ssm_b_re, v_ssm_b_im, v_ssm_c_re, v_ssm_c_im, v_ssm_d, v_ssm_w_glu, v_ssm_b_glu, v_mla_q_norm_w, v_mla_w_uq, v_mla_kv_norm_w, v_mla_w_ukv, v_ssm_out_norm_w, v_mla_out_norm_w, v_w_out, v_ffn_norm_w, v_ffn_w_up, v_ffn_conv_w, v_ffn_conv_b, v_ffn_w_down, v_final_norm_w):
    return _step(dict(locals()))
```

```python
import functools
import math

import jax
import jax.numpy as jnp
from jax import lax
from jax.experimental import pallas as pl
from jax.experimental.pallas import tpu as pltpu

F32 = jnp.float32
BF16 = jnp.bfloat16

SSM_GROUP = 16
SSM_STATE = 64
QK_NOPE_DIM = 128
QK_ROPE_DIM = 64
V_HEAD_DIM = 128
ROPE_THETA = 10000.0
RMS_EPS = 1e-6
ADAM_LR, ADAM_B1, ADAM_B2, ADAM_EPS, ADAM_WD, ADAM_STEP = 0.001, 0.9, 0.999, 1e-08, 0.01, 10

LANES = 128
SUBLANES = 8
VMEM_LIMIT_BYTES = 56 * 1024 * 1024

GROUPS_PER_BATCH = LANES // SSM_GROUP
STATE_PER_BATCH = GROUPS_PER_BATCH * SSM_STATE
HEAD_SLOT = 2 * LANES
NEG_INF = -1e30
ATTN_BLOCK = 512

N_CHIPS = 4
N_CORES = 2


def _tile(n, pref, align=LANES):
    if n <= pref:
        return n
    t = (pref // align) * align
    while t >= align:
        if n % t == 0:
            return t
        t -= align
    return n


def _params(sem):
    return pltpu.CompilerParams(dimension_semantics=sem, vmem_limit_bytes=VMEM_LIMIT_BYTES)


def _dot(a, b, dims):
    return lax.dot_general(a, b, (dims, ((), ())), preferred_element_type=F32)


def _dot_nn(a, b):
    return _dot(a, b, ((1,), (0,)))


def _dot_nt(a, b):
    return _dot(a, b, ((1,), (1,)))


def _dot_tn(a, b):
    return _dot(a, b, ((0,), (0,)))


def _matmul(a, b, *, mode, name, tm=512, tn=1024, tk=2048, bias=None, add=None, out_dtype=F32,
            out_blocks=None, a_split=False, b_split=False):
    if a_split:
        assert mode == "nt"
        a_shape = (a.shape[1], 2 * a.shape[2])
    else:
        a_shape = a.shape
    if b_split:
        assert mode == "tn"
        b_shape = (b.shape[1], 2 * b.shape[2])
    else:
        b_shape = b.shape
    if mode == "nn":
        (m, k), (k2, n) = a_shape, b_shape
    elif mode == "nt":
        (m, k), (n, k2) = a_shape, b_shape
    else:
        (k, m), (k2, n) = a_shape, b_shape
    assert k == k2, (a.shape, b.shape, mode)
    tm, tn, tk = _tile(m, tm, SUBLANES), _tile(n, tn), _tile(k, tk)
    nk = k // tk
    a_spec = {"nn": pl.BlockSpec((tm, tk), lambda i, j, kk: (i, kk)),
              "nt": pl.BlockSpec((tm, tk), lambda i, j, kk: (i, kk)),
              "tn": pl.BlockSpec((tk, tm), lambda i, j, kk: (kk, i))}[mode]
    b_spec = {"nn": pl.BlockSpec((tk, tn), lambda i, j, kk: (kk, j)),
              "nt": pl.BlockSpec((tn, tk), lambda i, j, kk: (j, kk)),
              "tn": pl.BlockSpec((tk, tn), lambda i, j, kk: (kk, j))}[mode]
    if a_split:
        kb = a.shape[2] // tk
        assert a.shape[2] % tk == 0
        a_spec = pl.BlockSpec((None, tm, tk), lambda i, j, kk: (kk // kb, i, kk % kb))
    if b_split:
        nb = b.shape[2] // tn
        assert b.shape[2] % tn == 0
        b_spec = pl.BlockSpec((None, tk, tn), lambda i, j, kk: (j // nb, kk, j % nb))
    dot = {"nn": _dot_nn, "nt": _dot_nt, "tn": _dot_tn}[mode]
    in_specs, operands = [a_spec, b_spec], [a, b]
    if bias is not None:
        in_specs.append(pl.BlockSpec((1, tn), lambda i, j, kk: (0, j)))
        operands.append(bias)
    if add is not None:
        in_specs.append(pl.BlockSpec((tm, tn), lambda i, j, kk: (i, j)))
        operands.append(add)

    def body(*refs):
        a_ref, b_ref = refs[0], refs[1]
        rest = list(refs[2:])
        bias_ref = rest.pop(0) if bias is not None else None
        add_ref = rest.pop(0) if add is not None else None
        o_ref, acc_ref = rest

        def finish(acc):
            if bias_ref is not None:
                acc = acc + bias_ref[...]
            if add_ref is not None:
                acc = acc + add_ref[...]
            o_ref[...] = acc.astype(o_ref.dtype)

        part = dot(a_ref[...].astype(BF16), b_ref[...].astype(BF16))
        if nk == 1:
            finish(part)
        else:
            kk = pl.program_id(2)

            @pl.when(kk == 0)
            def _():
                acc_ref[...] = part

            @pl.when(jnp.logical_and(kk > 0, kk < nk - 1))
            def _():
                acc_ref[...] += part

            @pl.when(kk == nk - 1)
            def _():
                finish(acc_ref[...] + part)

    if out_blocks is None:
        out_shape = jax.ShapeDtypeStruct((m, n), out_dtype)
        out_spec = pl.BlockSpec((tm, tn), lambda i, j, kk: (i, j))
    else:
        shape, block, index_map = out_blocks(tm, tn)
        out_shape = jax.ShapeDtypeStruct(shape, out_dtype)
        out_spec = pl.BlockSpec(block, index_map)
    acc_shape = (tm, tn) if nk > 1 else (SUBLANES, LANES)
    return pl.pallas_call(
        body, name=name, grid=(m // tm, n // tn, nk), in_specs=in_specs, out_specs=out_spec, out_shape=out_shape,
        scratch_shapes=[pltpu.VMEM(acc_shape, F32)],
        compiler_params=_params(("parallel", "parallel", "arbitrary")),
    )(*operands)


def _wgrad_blocks(rows, cols, row_sharded):
    if row_sharded:
        sr, sc = rows // N_CHIPS, cols // N_CORES
    else:
        sr, sc = rows // N_CORES, cols // N_CHIPS

    def make(tm, tn):
        assert sr % tm == 0 and sc % tn == 0, (rows, cols, tm, tn)
        rb, cb = sr // tm, sc // tn
        if row_sharded:
            def index_map(i, j, kk):
                return (j // cb, i // rb, i % rb, j % cb)
        else:
            def index_map(i, j, kk):
                return (i // rb, j // cb, i % rb, j % cb)
        return (N_CORES, N_CHIPS, sr, sc), (None, None, tm, tn), index_map

    return make, (sr, sc)


def _rms_rows(x):
    return lax.rsqrt(jnp.mean(x * x, axis=-1, keepdims=True) + RMS_EPS)


def _rmsnorm_fwd(x, w, *, name, width=None, col=0, out_dtype=BF16, tr=256):
    rows = x.shape[0]
    width = x.shape[1] if width is None else width
    tr = _tile(rows, tr, SUBLANES)

    def body(x_ref, w_ref, o_ref):
        xv = x_ref[...]
        o_ref[...] = (xv * _rms_rows(xv) * w_ref[...]).astype(o_ref.dtype)

    return pl.pallas_call(
        body, name=name, grid=(rows // tr,),
        in_specs=[pl.BlockSpec((tr, width), lambda i: (i, col)), pl.BlockSpec((1, width), lambda i: (0, 0))],
        out_specs=pl.BlockSpec((tr, width), lambda i: (i, 0)),
        out_shape=jax.ShapeDtypeStruct((rows, width), out_dtype),
        compiler_params=_params(("parallel",)),
    )(x, w)


def _rmsnorm_bwd_rows(xv, w, dy):
    r = _rms_rows(xv)
    n = xv * r
    dn = dy * w
    dx = r * (dn - n * jnp.mean(dn * n, axis=-1, keepdims=True))
    return dx, dy * n


def _rmsnorm_bwd(x, w, dy, *, name, width=None, col=0, dy_col=0, add=None, tr=256):
    rows = x.shape[0]
    width = x.shape[1] if width is None else width
    tr = _tile(rows, tr, SUBLANES)
    in_specs = [pl.BlockSpec((tr, width), lambda i: (i, col)), pl.BlockSpec((1, width), lambda i: (0, 0)),
                pl.BlockSpec((tr, width), lambda i: (i, dy_col))]
    operands = [x, w, dy]
    if add is not None:
        in_specs.append(pl.BlockSpec((tr, width), lambda i: (i, 0)))
        operands.append(add)

    def body(*refs):
        x_ref, w_ref, dy_ref = refs[:3]
        add_ref = refs[3] if add is not None else None
        dx_ref, dw_ref = refs[-2:]
        dx, dwp = _rmsnorm_bwd_rows(x_ref[...], w_ref[...], dy_ref[...])
        if add_ref is not None:
            dx = dx + add_ref[...]
        dx_ref[...] = dx
        part = jnp.sum(dwp, axis=0, keepdims=True)

        @pl.when(pl.program_id(0) == 0)
        def _():
            dw_ref[...] = part

        @pl.when(pl.program_id(0) > 0)
        def _():
            dw_ref[...] += part

    return pl.pallas_call(
        body, name=name, grid=(rows // tr,), in_specs=in_specs,
        out_specs=[pl.BlockSpec((tr, width), lambda i: (i, 0)), pl.BlockSpec((1, width), lambda i: (0, 0))],
        out_shape=[jax.ShapeDtypeStruct((rows, width), F32), jax.ShapeDtypeStruct((1, width), F32)],
        compiler_params=_params(("arbitrary",)),
    )(*operands)


def _final_norm_loss(h, w, target, *, tr=256):
    rows, d = h.shape
    tr = _tile(rows, tr, SUBLANES)

    def body(h_ref, w_ref, t_ref, loss_ref, dh_ref, dw_ref):
        hv, wv = h_ref[...], w_ref[...]
        r = _rms_rows(hv)
        n = hv * r
        err = n * wv - t_ref[...]
        d_out = err * (1.0 / d)
        dn = d_out * wv
        dh_ref[...] = r * (dn - n * jnp.mean(dn * n, axis=-1, keepdims=True))
        dw_part = jnp.sum(d_out * n, axis=0, keepdims=True)
        loss_part = jnp.full((SUBLANES, LANES), 0.5 / d, F32) * jnp.sum(err * err)

        @pl.when(pl.program_id(0) == 0)
        def _():
            dw_ref[...] = dw_part
            loss_ref[...] = loss_part

        @pl.when(pl.program_id(0) > 0)
        def _():
            dw_ref[...] += dw_part
            loss_ref[...] += loss_part

    return pl.pallas_call(
        body, name="final_norm_loss", grid=(rows // tr,),
        in_specs=[pl.BlockSpec((tr, d), lambda i: (i, 0)), pl.BlockSpec((1, d), lambda i: (0, 0)),
                  pl.BlockSpec((tr, d), lambda i: (i, 0))],
        out_specs=[pl.BlockSpec((SUBLANES, LANES), lambda i: (0, 0)), pl.BlockSpec((tr, d), lambda i: (i, 0)),
                   pl.BlockSpec((1, d), lambda i: (0, 0))],
        out_shape=[jax.ShapeDtypeStruct((SUBLANES, LANES), F32), jax.ShapeDtypeStruct((rows, d), F32),
                   jax.ShapeDtypeStruct((1, d), F32)],
        compiler_params=_params(("arbitrary",)),
    )(h, w, target)


def _cmul(ar, ai, br, bi):
    return ar * br - ai * bi, ar * bi + ai * br


def _expand_matrix(groups, reps):
    row = lax.broadcasted_iota(jnp.int32, (groups, groups * reps), 0)
    colg = lax.broadcasted_iota(jnp.int32, (groups, groups * reps), 1) // reps
    return (row == colg).astype(F32)


def _dot_exact(a, b, dims):
    return lax.dot_general(a, b, (dims, ((), ())), preferred_element_type=F32, precision=lax.Precision.HIGHEST)


def _s5_discretize(lr, li, dt):
    mag = jnp.exp(lr * dt)
    th = li * dt
    ar, ai = mag * jnp.cos(th), mag * jnp.sin(th)
    nr, ni = ar - 1.0, ai
    den = lr * lr + li * li
    zr = (nr * lr + ni * li) / den
    zi = (ni * lr - nr * li) / den
    return mag, ar, ai, nr, ni, den, zr, zi


def _s5_params(lam_re, lam_im, log_dt, b_re, b_im):
    g, p = lam_re.shape
    ph = b_re.shape[1]

    def body(lr_ref, li_ref, ldt_ref, br_ref, bi_ref, ar_ref, ai_ref, bbr_ref, bbi_ref):
        dt = jnp.exp(ldt_ref[...])
        _, ar, ai, _, _, _, zr, zi = _s5_discretize(lr_ref[...], li_ref[...], dt)
        ar_ref[...] = ar
        ai_ref[...] = ai
        e = _expand_matrix(p, ph // p)
        zr_x = _dot_exact(zr, e, ((1,), (0,)))
        zi_x = _dot_exact(zi, e, ((1,), (0,)))
        bre, bim = br_ref[...], bi_ref[...]
        bbr_ref[...] = zr_x * bre - zi_x * bim
        bbi_ref[...] = zr_x * bim + zi_x * bre

    return pl.pallas_call(
        body, name="s5_params",
        out_shape=[jax.ShapeDtypeStruct((g, p), F32)] * 2 + [jax.ShapeDtypeStruct((g, ph), F32)] * 2,
    )(lam_re, lam_im, log_dt, b_re, b_im)


def _s5_params_bwd(lam_re, lam_im, log_dt, b_re, b_im, d_ar, d_ai, d_bbr, d_bbi):
    g, p = lam_re.shape
    ph = b_re.shape[1]

    def body(lr_ref, li_ref, ldt_ref, br_ref, bi_ref, dar_ref, dai_ref, dbr_ref, dbi_ref,
             dlr_ref, dli_ref, dldt_ref, dbre_ref, dbim_ref):
        lr, li = lr_ref[...], li_ref[...]
        dt = jnp.exp(ldt_ref[...])
        mag, ar, ai, nr, ni, den, zr, zi = _s5_discretize(lr, li, dt)
        e = _expand_matrix(p, ph // p)
        zr_x = _dot_exact(zr, e, ((1,), (0,)))
        zi_x = _dot_exact(zi, e, ((1,), (0,)))
        bre, bim, dbr, dbi = br_ref[...], bi_ref[...], dbr_ref[...], dbi_ref[...]
        dbre_ref[...] = zr_x * dbr + zi_x * dbi
        dbim_ref[...] = zr_x * dbi - zi_x * dbr
        dzr = _dot_exact(bre * dbr + bim * dbi, e, ((1,), (1,)))
        dzi = _dot_exact(bre * dbi - bim * dbr, e, ((1,), (1,)))
        inv = 1.0 / den
        d_nr = (dzr * lr - dzi * li) * inv
        d_ni = (dzr * li + dzi * lr) * inv
        d_den = -(dzr * zr + dzi * zi) * inv
        d_lr = (dzr * nr + dzi * ni) * inv + 2.0 * lr * d_den
        d_li = (dzr * ni - dzi * nr) * inv + 2.0 * li * d_den
        t_ar = dar_ref[...] + d_nr
        t_ai = dai_ref[...] + d_ni
        d_lrdt = t_ar * ar + t_ai * ai
        d_th = t_ai * ar - t_ar * ai
        dlr_ref[...] = d_lr + d_lrdt * dt
        dli_ref[...] = d_li + d_th * dt
        dldt_ref[...] = jnp.sum(d_lrdt * lr + d_th * li, axis=1, keepdims=True) * dt

    return pl.pallas_call(
        body, name="s5_params_bwd",
        out_shape=[jax.ShapeDtypeStruct((g, p), F32)] * 2 + [jax.ShapeDtypeStruct((g, 1), F32)]
        + [jax.ShapeDtypeStruct((g, ph), F32)] * 2,
    )(lam_re, lam_im, log_dt, b_re, b_im, d_ar, d_ai, d_bbr, d_bbi)


def _powers(ar, ai, count):
    out = [(ar, ai)]
    for _ in range(count - 1):
        out.append(_cmul(out[-1][0], out[-1][1], ar, ai))
    return out


def _scan_coefs(ar, ai, reverse):
    w = ar.shape[-1]
    pw = _powers(ar, ai, SUBLANES)
    row = lax.broadcasted_iota(jnp.int32, (SUBLANES, w), 0)
    steps = []
    d = 1
    while d < SUBLANES:
        keep = (row < SUBLANES - d) if reverse else (row >= d)
        pr, pi = pw[d - 1]
        steps.append((d, jnp.where(keep, pr, 0.0), jnp.where(keep, pi, 0.0)))
        d *= 2
    cr = jnp.zeros((SUBLANES, w), F32)
    ci = jnp.zeros((SUBLANES, w), F32)
    for t in range(SUBLANES):
        pr, pi = pw[SUBLANES - 1 - t] if reverse else pw[t]
        cr = jnp.where(row == t, pr, cr)
        ci = jnp.where(row == t, pi, ci)
    return steps, cr, ci


def _scan_tile(xr, xi, carry_r, carry_i, coefs, reverse):
    steps, cr, ci = coefs
    for d, mr, mi in steps:
        shift = SUBLANES - d if reverse else d
        sr, si = pltpu.roll(xr, shift, 0), pltpu.roll(xi, shift, 0)
        pr, pi = _cmul(mr, mi, sr, si)
        xr, xi = xr + pr, xi + pi
    pr, pi = _cmul(cr, ci, carry_r, carry_i)
    return xr + pr, xi + pi


def _gelu(x):
    c = math.sqrt(2.0 / math.pi)
    return 0.5 * x * (1.0 + jnp.tanh(c * (x + 0.044715 * x * x * x)))


def _gelu_grad(x):
    c = math.sqrt(2.0 / math.pi)
    t = jnp.tanh(c * (x + 0.044715 * x * x * x))
    return 0.5 * (1.0 + t) + 0.5 * x * (1.0 - t * t) * c * (1.0 + 3.0 * 0.044715 * x * x)


def _s5_fwd(proj, wb, wc, d_skip, abar):
    rows = proj.shape[0]
    nb = wb.shape[0]
    s2 = 2 * STATE_PER_BATCH
    st = STATE_PER_BATCH
    chunk = _tile(rows, 512, SUBLANES)

    def body(u_ref, wb_ref, wc_ref, d_ref, a_ref, s_ref, y_ref, yg_ref):
        for c0 in range(0, rows, chunk):
            s_ref[pl.ds(c0, chunk), :] = _dot_nn(u_ref[pl.ds(c0, chunk), :].astype(BF16), wb_ref[...])
        av = a_ref[...]
        coefs = _scan_coefs(av[:, :st], av[:, st:], reverse=False)

        def tile(b, carry):
            r0 = pl.multiple_of(b * SUBLANES, SUBLANES)
            xr, xi = _scan_tile(s_ref[pl.ds(r0, SUBLANES), :st], s_ref[pl.ds(r0, SUBLANES), st:], carry[0], carry[1],
                                coefs, False)
            s_ref[pl.ds(r0, SUBLANES), :st] = xr
            s_ref[pl.ds(r0, SUBLANES), st:] = xi
            return xr[SUBLANES - 1:, :], xi[SUBLANES - 1:, :]

        zero = jnp.zeros((1, st), F32)
        lax.fori_loop(0, rows // SUBLANES, tile, (zero, zero))
        for c0 in range(0, rows, chunk):
            y = _dot_nn(s_ref[pl.ds(c0, chunk), :].astype(BF16), wc_ref[...]) + d_ref[...] * u_ref[pl.ds(c0, chunk), :]
            y_ref[pl.ds(c0, chunk), :] = y
            yg_ref[pl.ds(c0, chunk), :] = _gelu(y).astype(BF16)

    return pl.pallas_call(
        body, name="s5_fwd", grid=(nb,),
        in_specs=[pl.BlockSpec((rows, LANES), lambda j: (0, j)), pl.BlockSpec((None, LANES, s2), lambda j: (j, 0, 0)),
                  pl.BlockSpec((None, s2, LANES), lambda j: (j, 0, 0)), pl.BlockSpec((1, LANES), lambda j: (0, j)),
                  pl.BlockSpec((None, 1, s2), lambda j: (j, 0, 0))],
        out_specs=[pl.BlockSpec((rows, s2), lambda j: (0, j)), pl.BlockSpec((rows, LANES), lambda j: (0, j)),
                   pl.BlockSpec((rows, LANES), lambda j: (0, j))],
        out_shape=[jax.ShapeDtypeStruct((rows, nb * s2), F32), jax.ShapeDtypeStruct((rows, nb * LANES), F32),
                   jax.ShapeDtypeStruct((rows, nb * LANES), BF16)],
        compiler_params=_params(("parallel",)),
    )(proj, wb, wc, d_skip, abar)


def _s5_bwd(proj, states, y_pre, dyg_a, dyg_b, wb, wc, d_skip, abar):
    rows = proj.shape[0]
    nb = wb.shape[0]
    s2 = 2 * STATE_PER_BATCH
    st = STATE_PER_BATCH
    chunk = _tile(rows, 512, SUBLANES)
    n_tiles = rows // SUBLANES

    def body(u_ref, s_ref, y_ref, ga_ref, gb_ref, wb_ref, wc_ref, d_ref, a_ref,
             du_ref, dwb_ref, dwc_ref, da_ref, dd_ref, ds_ref, dy_ref):
        dy_ref[...] = (ga_ref[...] + gb_ref[...]) * _gelu_grad(y_ref[...])
        dd_ref[...] = jnp.sum(dy_ref[...] * u_ref[...], axis=0, keepdims=True)
        for c0 in range(0, rows, chunk):
            ds_ref[pl.ds(c0, chunk), :] = _dot_nt(dy_ref[pl.ds(c0, chunk), :].astype(BF16), wc_ref[...])
        dwc_ref[...] = _dot_tn(s_ref[...].astype(BF16), dy_ref[...].astype(BF16))
        av = a_ref[...]
        coefs = _scan_coefs(av[:, :st], -av[:, st:], reverse=True)
        row = lax.broadcasted_iota(jnp.int32, (SUBLANES, st), 0)

        def tile(k, carry):
            cr, ci, acc_r, acc_i = carry
            b = n_tiles - 1 - k
            r0 = pl.multiple_of(b * SUBLANES, SUBLANES)
            rp = pl.multiple_of(jnp.maximum(b - 1, 0) * SUBLANES, SUBLANES)
            xr, xi = _scan_tile(ds_ref[pl.ds(r0, SUBLANES), :st], ds_ref[pl.ds(r0, SUBLANES), st:], cr, ci, coefs, True)
            ds_ref[pl.ds(r0, SUBLANES), :st] = xr
            ds_ref[pl.ds(r0, SUBLANES), st:] = xi
            first = jnp.where(b > 0, 1.0, 0.0)
            pr = jnp.where(row == 0, pltpu.roll(s_ref[pl.ds(rp, SUBLANES), :st], 1, 0) * first,
                           pltpu.roll(s_ref[pl.ds(r0, SUBLANES), :st], 1, 0))
            pi = jnp.where(row == 0, pltpu.roll(s_ref[pl.ds(rp, SUBLANES), st:], 1, 0) * first,
                           pltpu.roll(s_ref[pl.ds(r0, SUBLANES), st:], 1, 0))
            acc_r = acc_r + pr * xr + pi * xi
            acc_i = acc_i + pr * xi - pi * xr
            return xr[:1, :], xi[:1, :], acc_r, acc_i

        zero = jnp.zeros((1, st), F32)
        zacc = jnp.zeros((SUBLANES, st), F32)
        _, _, acc_r, acc_i = lax.fori_loop(0, n_tiles, tile, (zero, zero, zacc, zacc))
        da_ref[:, :st] = jnp.sum(acc_r, axis=0, keepdims=True)
        da_ref[:, st:] = jnp.sum(acc_i, axis=0, keepdims=True)
        for c0 in range(0, rows, chunk):
            du_ref[pl.ds(c0, chunk), :] = (_dot_nt(ds_ref[pl.ds(c0, chunk), :].astype(BF16), wb_ref[...])
                                           + d_ref[...] * dy_ref[pl.ds(c0, chunk), :])
        dwb_ref[...] = _dot_tn(u_ref[...].astype(BF16), ds_ref[...].astype(BF16))

    col = pl.BlockSpec((rows, LANES), lambda j: (0, j))
    return pl.pallas_call(
        body, name="s5_bwd", grid=(nb,),
        in_specs=[col, pl.BlockSpec((rows, s2), lambda j: (0, j)), col, col, col,
                  pl.BlockSpec((None, LANES, s2), lambda j: (j, 0, 0)), pl.BlockSpec((None, s2, LANES), lambda j: (j, 0, 0)),
                  pl.BlockSpec((1, LANES), lambda j: (0, j)), pl.BlockSpec((None, 1, s2), lambda j: (j, 0, 0))],
        out_specs=[col, pl.BlockSpec((None, LANES, s2), lambda j: (j, 0, 0)),
                   pl.BlockSpec((None, s2, LANES), lambda j: (j, 0, 0)), pl.BlockSpec((None, 1, s2), lambda j: (j, 0, 0)),
                   pl.BlockSpec((1, LANES), lambda j: (0, j))],
        out_shape=[jax.ShapeDtypeStruct((rows, nb * LANES), F32), jax.ShapeDtypeStruct((nb, LANES, s2), F32),
                   jax.ShapeDtypeStruct((nb, s2, LANES), F32), jax.ShapeDtypeStruct((nb, 1, s2), F32),
                   jax.ShapeDtypeStruct((1, nb * LANES), F32)],
        scratch_shapes=[pltpu.VMEM((rows, s2), F32), pltpu.VMEM((rows, LANES), F32)],
        compiler_params=_params(("parallel",)),
    )(proj, states, y_pre, dyg_a, dyg_b, wb, wc, d_skip, abar)


def _glu_norm_fwd(y_pre, z, w, *, tr=256):
    rows, width = y_pre.shape
    tr = _tile(rows, tr, SUBLANES)

    def body(y_ref, z_ref, w_ref, o_ref):
        v = _gelu(y_ref[...]) * jax.nn.sigmoid(z_ref[...])
        o_ref[...] = (v * _rms_rows(v) * w_ref[...]).astype(o_ref.dtype)

    blk = pl.BlockSpec((tr, width), lambda i: (i, 0))
    return pl.pallas_call(
        body, name="glu_norm_fwd", grid=(rows // tr,),
        in_specs=[blk, blk, pl.BlockSpec((1, width), lambda i: (0, 0))], out_specs=blk,
        out_shape=jax.ShapeDtypeStruct((rows, width), BF16), compiler_params=_params(("parallel",)),
    )(y_pre, z, w)


def _glu_norm_bwd(y_pre, z, w, dycat, *, tr=256):
    rows, width = y_pre.shape
    tr = _tile(rows, tr, SUBLANES)

    def body(y_ref, z_ref, w_ref, dy_ref, dz_ref, dg_ref, dw_ref, db_ref):
        yg = _gelu(y_ref[...])
        sg = jax.nn.sigmoid(z_ref[...])
        dv, dwp = _rmsnorm_bwd_rows(yg * sg, w_ref[...], dy_ref[...])
        dz = dv * yg * sg * (1.0 - sg)
        dz_ref[...] = dz
        dg_ref[...] = dv * sg
        dw_part = jnp.sum(dwp, axis=0, keepdims=True)
        db_part = jnp.sum(dz, axis=0, keepdims=True)

        @pl.when(pl.program_id(0) == 0)
        def _():
            dw_ref[...] = dw_part
            db_ref[...] = db_part

        @pl.when(pl.program_id(0) > 0)
        def _():
            dw_ref[...] += dw_part
            db_ref[...] += db_part

    blk = pl.BlockSpec((tr, width), lambda i: (i, 0))
    vec = pl.BlockSpec((1, width), lambda i: (0, 0))
    return pl.pallas_call(
        body, name="glu_norm_bwd", grid=(rows // tr,), in_specs=[blk, blk, vec, blk], out_specs=[blk, blk, vec, vec],
        out_shape=[jax.ShapeDtypeStruct((rows, width), F32)] * 2 + [jax.ShapeDtypeStruct((1, width), F32)] * 2,
        compiler_params=_params(("arbitrary",)),
    )(y_pre, z, w, dycat)


def _rope_tables(pos, freq, sign):
    rows = pos.shape[0]

    def body(p_ref, f_ref, s_ref, cos_ref, sin_ref):
        ang = p_ref[...] * f_ref[...]
        cos_ref[...] = jnp.cos(ang)
        sin_ref[...] = jnp.sin(ang) * s_ref[...]

    return pl.pallas_call(body, name="rope_tables", out_shape=[jax.ShapeDtypeStruct((rows, LANES), F32)] * 2)(pos, freq, sign)


def _rope(x, cos, sin_signed):
    lane = lax.broadcasted_iota(jnp.int32, x.shape, 1)
    half = QK_ROPE_DIM // 2
    swapped = jnp.where(lane < half, pltpu.roll(x, LANES - half, 1), pltpu.roll(x, half, 1))
    return x * cos + swapped * sin_signed


def _attn_prep(q, kv, proj, kpe_col, cos, sin, *, tr=256):
    rows = q.shape[0]
    heads = q.shape[1] // HEAD_SLOT
    tr = _tile(rows, tr, SUBLANES)

    def body(q_ref, kv_ref, kpe_ref, cos_ref, sin_ref, qc_ref, kc_ref, v_ref):
        c, s = cos_ref[...], sin_ref[...]
        qc_ref[:, :LANES] = q_ref[:, :LANES].astype(BF16)
        qc_ref[:, LANES:] = _rope(q_ref[:, LANES:], c, s).astype(BF16)
        kc_ref[:, :LANES] = kv_ref[:, :LANES].astype(BF16)
        kc_ref[:, LANES:] = _rope(kpe_ref[...], c, s).astype(BF16)
        v_ref[...] = kv_ref[:, LANES:].astype(BF16)

    slot = pl.BlockSpec((tr, HEAD_SLOT), lambda i, h: (i, h))
    tab = pl.BlockSpec((tr, LANES), lambda i, h: (i, 0))
    return pl.pallas_call(
        body, name="attn_prep", grid=(rows // tr, heads),
        in_specs=[slot, slot, pl.BlockSpec((tr, LANES), lambda i, h: (i, kpe_col)), tab, tab],
        out_specs=[slot, slot, pl.BlockSpec((tr, LANES), lambda i, h: (i, h))],
        out_shape=[jax.ShapeDtypeStruct((rows, heads * HEAD_SLOT), BF16)] * 2
        + [jax.ShapeDtypeStruct((rows, heads * LANES), BF16)],
        compiler_params=_params(("parallel", "parallel")),
    )(q, kv, proj, cos, sin)


def _causal(i, j, tq, tk):
    qpos = i * tq + lax.broadcasted_iota(jnp.int32, (tq, tk), 0)
    kpos = j * tk + lax.broadcasted_iota(jnp.int32, (tq, tk), 1)
    return kpos <= qpos


def _attn_fwd(qc, kc, vb, *, scale, tq=512):
    rows = qc.shape[0]
    heads = qc.shape[1] // HEAD_SLOT
    tq = _tile(rows, tq, SUBLANES)
    tk = tq

    def body(q_ref, k_ref, v_ref, o_ref, lse_ref):
        i = pl.program_id(1)
        q = q_ref[...]

        def step(j, carry):
            m, l, acc = carry
            k0 = pl.multiple_of(j * tk, tk)
            s = _dot_nt(q, k_ref[pl.ds(k0, tk), :]) * scale
            s = jnp.where(_causal(i, j, tq, tk), s, NEG_INF)
            m_new = jnp.maximum(m, jnp.max(s, axis=-1, keepdims=True))
            p = jnp.exp(s - m_new)
            alpha = jnp.exp(m - m_new)
            l = alpha * l + jnp.sum(p, axis=-1, keepdims=True)
            acc = alpha * acc + _dot_nn(p.astype(BF16), v_ref[pl.ds(k0, tk), :])
            return m_new, l, acc

        init = (jnp.full((tq, 1), NEG_INF, F32), jnp.zeros((tq, 1), F32), jnp.zeros((tq, LANES), F32))
        m, l, acc = lax.fori_loop(0, i + 1, step, init)
        o_ref[...] = acc / l
        lse_ref[...] = jnp.broadcast_to(m + jnp.log(l), (tq, LANES))

    return pl.pallas_call(
        body, name="attn_fwd", grid=(heads, rows // tq),
        in_specs=[pl.BlockSpec((tq, HEAD_SLOT), lambda h, i: (i, h)), pl.BlockSpec((rows, HEAD_SLOT), lambda h, i: (0, h)),
                  pl.BlockSpec((rows, LANES), lambda h, i: (0, h))],
        out_specs=[pl.BlockSpec((tq, LANES), lambda h, i: (i, h))] * 2,
        out_shape=[jax.ShapeDtypeStruct((rows, heads * LANES), F32)] * 2,
        compiler_params=_params(("parallel", "parallel")),
    )(qc, kc, vb)


def _attn_bwd_q(qc, kc, vb, o, do, lse, cos, sin, *, scale, tq=512):
    rows = qc.shape[0]
    heads = qc.shape[1] // HEAD_SLOT
    tq = _tile(rows, tq, SUBLANES)
    tk = tq

    def body(q_ref, k_ref, v_ref, o_ref, do_ref, lse_ref, cos_ref, sin_ref, dq_ref, delta_ref):
        i = pl.program_id(1)
        q = q_ref[...]
        dov = do_ref[...]
        delta = jnp.sum(dov * o_ref[...], axis=-1, keepdims=True)
        delta_ref[...] = jnp.broadcast_to(delta, (tq, LANES))
        dob = dov.astype(BF16)
        lse_col = lse_ref[:, :1]

        def step(j, dq):
            k0 = pl.multiple_of(j * tk, tk)
            kb = k_ref[pl.ds(k0, tk), :]
            s = _dot_nt(q, kb) * scale
            p = jnp.where(_causal(i, j, tq, tk), jnp.exp(s - lse_col), 0.0)
            dp = _dot_nt(dob, v_ref[pl.ds(k0, tk), :])
            ds = p * (dp - delta)
            return dq + _dot_nn(ds.astype(BF16), kb)

        dq = lax.fori_loop(0, i + 1, step, jnp.zeros((tq, HEAD_SLOT), F32)) * scale
        dq_ref[:, :LANES] = dq[:, :LANES]
        dq_ref[:, LANES:] = _rope(dq[:, LANES:], cos_ref[...], -sin_ref[...])

    qblk = pl.BlockSpec((tq, HEAD_SLOT), lambda h, i: (i, h))
    vblk = pl.BlockSpec((tq, LANES), lambda h, i: (i, h))
    tab = pl.BlockSpec((tq, LANES), lambda h, i: (i, 0))
    return pl.pallas_call(
        body, name="attn_bwd_q", grid=(heads, rows // tq),
        in_specs=[qblk, pl.BlockSpec((rows, HEAD_SLOT), lambda h, i: (0, h)), pl.BlockSpec((rows, LANES), lambda h, i: (0, h)),
                  vblk, vblk, vblk, tab, tab],
        out_specs=[qblk, vblk],
        out_shape=[jax.ShapeDtypeStruct((rows, heads * HEAD_SLOT), F32), jax.ShapeDtypeStruct((rows, heads * LANES), F32)],
        compiler_params=_params(("parallel", "parallel")),
    )(qc, kc, vb, o, do, lse, cos, sin)


def _attn_bwd_kv(qc, kc, vb, do, lse, delta, cos, sin, *, scale, tk=512):
    rows = qc.shape[0]
    heads = qc.shape[1] // HEAD_SLOT
    tk = _tile(rows, tk, SUBLANES)
    tq = tk
    nq = rows // tq

    def body(q_ref, k_ref, v_ref, do_ref, lse_ref, delta_ref, cos_ref, sin_ref, dkv_ref, dkpe_ref):
        j, h = pl.program_id(0), pl.program_id(1)
        kb, vv = k_ref[...], v_ref[...]

        def step(i, carry):
            dk, dv = carry
            q0 = pl.multiple_of(i * tq, tq)
            qb = q_ref[pl.ds(q0, tq), :]
            dob = do_ref[pl.ds(q0, tq), :].astype(BF16)
            s = _dot_nt(qb, kb) * scale
            p = jnp.where(_causal(i, j, tq, tk), jnp.exp(s - lse_ref[pl.ds(q0, tq), :1]), 0.0)
            dv = dv + _dot_tn(p.astype(BF16), dob)
            ds = p * (_dot_nt(dob, vv) - delta_ref[pl.ds(q0, tq), :1])
            dk = dk + _dot_tn(ds.astype(BF16), qb)
            return dk, dv

        dk, dv = lax.fori_loop(j, nq, step, (jnp.zeros((tk, HEAD_SLOT), F32), jnp.zeros((tk, LANES), F32)))
        dkv_ref[:, :LANES] = dk[:, :LANES] * scale
        dkv_ref[:, LANES:] = dv
        part = dk[:, LANES:] * scale

        @pl.when(h == 0)
        def _():
            dkpe_ref[...] = part

        @pl.when(h > 0)
        def _():
            dkpe_ref[...] += part

        @pl.when(h == heads - 1)
        def _():
            dkpe_ref[...] = _rope(dkpe_ref[...], cos_ref[...], -sin_ref[...])

    full_q = pl.BlockSpec((rows, HEAD_SLOT), lambda j, h: (0, h))
    full_v = pl.BlockSpec((rows, LANES), lambda j, h: (0, h))
    tab = pl.BlockSpec((tk, LANES), lambda j, h: (j, 0))
    return pl.pallas_call(
        body, name="attn_bwd_kv", grid=(rows // tk, heads),
        in_specs=[full_q, pl.BlockSpec((tk, HEAD_SLOT), lambda j, h: (j, h)), pl.BlockSpec((tk, LANES), lambda j, h: (j, h)),
                  full_v, full_v, full_v, tab, tab],
        out_specs=[pl.BlockSpec((tk, HEAD_SLOT), lambda j, h: (j, h)), pl.BlockSpec((tk, LANES), lambda j, h: (j, 0))],
        out_shape=[jax.ShapeDtypeStruct((rows, heads * HEAD_SLOT), F32), jax.ShapeDtypeStruct((rows, LANES), F32)],
        compiler_params=_params(("parallel", "arbitrary")),
    )(qc, kc, vb, do, lse, delta, cos, sin)


def _shift_down(x, d):
    row = lax.broadcasted_iota(jnp.int32, x.shape, 0)
    return jnp.where(row >= d, pltpu.roll(x, d, 0), 0.0)


def _shift_up(x, d):
    rows = x.shape[0]
    row = lax.broadcasted_iota(jnp.int32, x.shape, 0)
    return jnp.where(row < rows - d, pltpu.roll(x, rows - d, 0), 0.0)


def _conv3(a, w, b):
    return w[2:3, :] * a + w[1:2, :] * _shift_down(a, 1) + w[0:1, :] * _shift_down(a, 2) + b


def _conv_gate_fwd(a, conv_w, conv_b, *, tc=256):
    rows, f2 = a.shape
    f = f2 // 2
    tc = _tile(f, tc)
    nc = f // tc

    def body(ag_ref, av_ref, wg_ref, wv_ref, bg_ref, bv_ref, o_ref):
        gate = _conv3(ag_ref[...], wg_ref[...], bg_ref[...])
        val = _conv3(av_ref[...], wv_ref[...], bv_ref[...])
        o_ref[...] = (gate * jax.nn.sigmoid(gate) * val).astype(o_ref.dtype)

    return pl.pallas_call(
        body, name="conv_gate_fwd", grid=(nc,),
        in_specs=[pl.BlockSpec((rows, tc), lambda j: (0, j)), pl.BlockSpec((rows, tc), lambda j: (0, j + nc)),
                  pl.BlockSpec((SUBLANES, tc), lambda j: (0, j)), pl.BlockSpec((SUBLANES, tc), lambda j: (0, j + nc)),
                  pl.BlockSpec((1, tc), lambda j: (0, j)), pl.BlockSpec((1, tc), lambda j: (0, j + nc))],
        out_specs=pl.BlockSpec((rows, tc), lambda j: (0, j)),
        out_shape=jax.ShapeDtypeStruct((rows, f), BF16), compiler_params=_params(("parallel",)),
    )(a, a, conv_w, conv_w, conv_b, conv_b)


def _conv_gate_bwd(a, conv_w, conv_b, dg, *, tc=256):
    rows, f2 = a.shape
    f = f2 // 2
    tc = _tile(f, tc)
    nc = f // tc

    def conv_bwd(a_val, w, d_out):
        da = w[2:3, :] * d_out + w[1:2, :] * _shift_up(d_out, 1) + w[0:1, :] * _shift_up(d_out, 2)
        db = jnp.sum(d_out, axis=0, keepdims=True)
        row = lax.broadcasted_iota(jnp.int32, (SUBLANES, a_val.shape[1]), 0)
        dw = jnp.zeros((SUBLANES, a_val.shape[1]), F32)
        for tap in range(3):
            t = jnp.sum(d_out * (_shift_down(a_val, 2 - tap) if tap < 2 else a_val), axis=0, keepdims=True)
            dw = jnp.where(row == tap, t, dw)
        return da, dw, db

    def body(ag_ref, av_ref, wg_ref, wv_ref, bg_ref, bv_ref, dg_ref, da_ref, dw_ref, db_ref):
        ag, av, wg, wv = ag_ref[...], av_ref[...], wg_ref[...], wv_ref[...]
        gate = _conv3(ag, wg, bg_ref[...])
        val = _conv3(av, wv, bv_ref[...])
        sg = jax.nn.sigmoid(gate)
        dgv = dg_ref[...]
        d_gate = dgv * val * sg * (1.0 + gate * (1.0 - sg))
        d_val = dgv * gate * sg
        for half, (a_val, w, d_out) in enumerate(((ag, wg, d_gate), (av, wv, d_val))):
            da, dw, db = conv_bwd(a_val, w, d_out)
            da_ref[half] = da.astype(da_ref.dtype)
            dw_ref[half] = dw
            db_ref[half] = db

    lo = lambda j: (0, j)
    hi = lambda j: (0, j + nc)
    both = lambda j: (0, 0, j)
    return pl.pallas_call(
        body, name="conv_gate_bwd", grid=(nc,),
        in_specs=[pl.BlockSpec((rows, tc), lo), pl.BlockSpec((rows, tc), hi), pl.BlockSpec((SUBLANES, tc), lo),
                  pl.BlockSpec((SUBLANES, tc), hi), pl.BlockSpec((1, tc), lo), pl.BlockSpec((1, tc), hi),
                  pl.BlockSpec((rows, tc), lo)],
        out_specs=[pl.BlockSpec((2, rows, tc), both), pl.BlockSpec((2, SUBLANES, tc), both), pl.BlockSpec((2, 1, tc), both)],
        out_shape=[jax.ShapeDtypeStruct((2, rows, f), BF16), jax.ShapeDtypeStruct((2, SUBLANES, f), F32),
                   jax.ShapeDtypeStruct((2, 1, f), F32)],
        compiler_params=_params(("parallel",)),
    )(a, a, conv_w, conv_w, conv_b, conv_b, dg)


def _wgrad(a, b, rows, cols, row_sharded, name, **kw):
    make, (sr, sc) = _wgrad_blocks(rows, cols, row_sharded)
    tm = kw.pop("tm", _tile(sr, 512))
    tn = kw.pop("tn", _tile(sc, 1024))
    return _matmul(a, b, mode="tn", name=name, tm=tm, tn=tn, out_blocks=make, **kw)


def _block_diag(x):
    nb, g, r, c = x.shape
    eye = jnp.eye(g, dtype=x.dtype)
    return (x[:, :, :, None, :] * eye[None, :, None, :, None]).reshape(nb, g * r, g * c)


def _block_diag_part(x, r, c):
    nb = x.shape[0]
    g = GROUPS_PER_BATCH
    eye = jnp.eye(g, dtype=x.dtype)
    return jnp.sum(x.reshape(nb, g, r, g, c) * eye[None, :, None, :, None], axis=3)


def _local_step(x, posf, target, w):
    rows, d = x.shape
    width = w["ssm_d"].shape[1]
    qr, kvr = w["mla_q_norm_w"].shape[1], w["mla_kv_norm_w"].shape[1]
    heads = w["mla_w_ukv"].shape[1] // HEAD_SLOT
    f2 = w["ffn_w_up"].shape[1]
    inp = w["w_in"].shape[1]
    groups = width // SSM_GROUP
    nb = groups // GROUPS_PER_BATCH
    scale = (QK_NOPE_DIM + QK_ROPE_DIM) ** -0.5
    g = {}

    hn = _rmsnorm_fwd(x, w["attn_norm_w"], name="attn_norm")
    proj = _matmul(hn, w["w_in"], mode="nn", name="in_proj")

    ar, ai, bbr, bbi = _s5_params(w["ssm_lambda_re"], w["ssm_lambda_im"], w["ssm_log_dt"], w["ssm_b_re"], w["ssm_b_im"])

    def b_band(bb):
        return _block_diag(bb.reshape(nb, GROUPS_PER_BATCH, SSM_STATE, SSM_GROUP).transpose(0, 1, 3, 2))

    def c_band(c):
        return _block_diag(c.reshape(nb, GROUPS_PER_BATCH, SSM_GROUP, SSM_STATE).transpose(0, 1, 3, 2))

    wb = jnp.concatenate([b_band(bbr), b_band(bbi)], axis=2).astype(BF16)
    wc = jnp.concatenate([c_band(w["ssm_c_re"]), -c_band(w["ssm_c_im"])], axis=1).astype(BF16)
    abar = jnp.concatenate([ar.reshape(nb, 1, STATE_PER_BATCH), ai.reshape(nb, 1, STATE_PER_BATCH)], axis=2)
    states, y_pre, yg = _s5_fwd(proj, wb, wc, w["ssm_d"], abar)
    z = _matmul(yg, w["ssm_w_glu"], mode="nn", name="glu_proj", bias=w["ssm_b_glu"])
    ys = _glu_norm_fwd(y_pre, z, w["ssm_out_norm_w"])

    q_col, kv_col, kpe_col = width // qr, (width + qr) // kvr, (width + qr + kvr) // LANES
    assert width % qr == 0 and (width + qr) % kvr == 0
    qn = _rmsnorm_fwd(proj, w["mla_q_norm_w"], name="q_norm", width=qr, col=q_col)
    kvn = _rmsnorm_fwd(proj, w["mla_kv_norm_w"], name="kv_norm", width=kvr, col=kv_col)
    q = _matmul(qn, w["mla_w_uq"], mode="nn", name="q_proj")
    kv = _matmul(kvn, w["mla_w_ukv"], mode="nn", name="kv_proj")
    half = QK_ROPE_DIM // 2
    inv_freq = ROPE_THETA ** (-jnp.arange(0, QK_ROPE_DIM, 2, dtype=F32) / QK_ROPE_DIM)
    zeros = jnp.zeros((LANES - QK_ROPE_DIM,), F32)
    freq = jnp.concatenate([inv_freq, inv_freq, zeros]).reshape(1, LANES)
    sign = jnp.concatenate([-jnp.ones((half,), F32), jnp.ones((half,), F32), zeros]).reshape(1, LANES)
    cos, sin = _rope_tables(posf, freq, sign)
    qc, kc, vb = _attn_prep(q, kv, proj, kpe_col, cos, sin)
    o, lse = _attn_fwd(qc, kc, vb, scale=scale, tq=ATTN_BLOCK)
    ym = _rmsnorm_fwd(o, w["mla_out_norm_w"], name="mla_out_norm")
    ycat = jnp.concatenate([ys, ym], axis=1)
    h1 = _matmul(ycat, w["w_out"], mode="nn", name="out_proj", add=x)

    hn2 = _rmsnorm_fwd(h1, w["ffn_norm_w"], name="ffn_norm")
    a = _matmul(hn2, w["ffn_w_up"], mode="nn", name="ffn_up")
    gated = _conv_gate_fwd(a, w["ffn_conv_w"], w["ffn_conv_b"])
    h2 = _matmul(gated, w["ffn_w_down"], mode="nn", name="ffn_down", add=h1, tk=2816)
    loss_tile, dh2, g["final_norm_w"] = _final_norm_loss(h2, w["final_norm_w"], target)

    dgated = _matmul(dh2, w["ffn_w_down"], mode="nt", name="ffn_down_dx")
    g["ffn_w_down"] = _wgrad(gated, dh2, f2 // 2, d, True, "ffn_down_dw", tm=f2 // 2 // N_CHIPS, tn=512)
    da, dcw, dcb = _conv_gate_bwd(a, w["ffn_conv_w"], w["ffn_conv_b"], dgated)
    g["ffn_conv_w"] = jnp.concatenate([dcw[0, :3], dcw[1, :3]], axis=1)
    g["ffn_conv_b"] = jnp.concatenate([dcb[0], dcb[1]], axis=1)
    dhn2 = _matmul(da, w["ffn_w_up"], mode="nt", name="ffn_up_dx", a_split=True, tk=_tile(f2 // 2, 2816))
    g["ffn_w_up"] = _wgrad(hn2, da, d, f2, False, "ffn_up_dw", b_split=True)
    dh1, g["ffn_norm_w"] = _rmsnorm_bwd(h1, w["ffn_norm_w"], dhn2, name="ffn_norm_bwd", add=dh2)

    dycat = _matmul(dh1, w["w_out"], mode="nt", name="out_proj_dx")
    g["w_out"] = _wgrad(ycat, dh1, 2 * width, d, True, "out_proj_dw")

    do, g["mla_out_norm_w"] = _rmsnorm_bwd(o, w["mla_out_norm_w"], dycat, name="mla_out_norm_bwd", width=width, dy_col=1)
    dq, delta = _attn_bwd_q(qc, kc, vb, o, do, lse, cos, sin, scale=scale, tq=ATTN_BLOCK)
    dkv, dkpe = _attn_bwd_kv(qc, kc, vb, do, lse, delta, cos, sin, scale=scale, tk=ATTN_BLOCK)
    g["mla_w_uq"] = _wgrad(qn, dq, qr, heads * HEAD_SLOT, False, "q_proj_dw")
    dqn = _matmul(dq, w["mla_w_uq"], mode="nt", name="q_proj_dx")
    dcq, g["mla_q_norm_w"] = _rmsnorm_bwd(proj, w["mla_q_norm_w"], dqn, name="q_norm_bwd", width=qr, col=q_col)
    g["mla_w_ukv"] = _wgrad(kvn, dkv, kvr, heads * HEAD_SLOT, False, "kv_proj_dw")
    dkvn = _matmul(dkv, w["mla_w_ukv"], mode="nt", name="kv_proj_dx")
    dckv, g["mla_kv_norm_w"] = _rmsnorm_bwd(proj, w["mla_kv_norm_w"], dkvn, name="kv_norm_bwd", width=kvr, col=kv_col)

    dz, dyg_a, g["ssm_out_norm_w"], g["ssm_b_glu"] = _glu_norm_bwd(y_pre, z, w["ssm_out_norm_w"], dycat)
    dyg_b = _matmul(dz, w["ssm_w_glu"], mode="nt", name="glu_proj_dx")
    g["ssm_w_glu"] = _wgrad(yg, dz, width, width, True, "glu_proj_dw")
    du, dwb, dwc, dabar, g["ssm_d"] = _s5_bwd(proj, states, y_pre, dyg_a, dyg_b, wb, wc, w["ssm_d"], abar)

    def b_unband(x):
        return _block_diag_part(x, SSM_GROUP, SSM_STATE).transpose(0, 1, 3, 2).reshape(groups, SSM_STATE * SSM_GROUP)

    def c_unband(x):
        return _block_diag_part(x, SSM_STATE, SSM_GROUP).transpose(0, 1, 3, 2).reshape(groups, SSM_GROUP, SSM_STATE)

    st = STATE_PER_BATCH
    g["ssm_c_re"] = c_unband(dwc[:, :st, :])
    g["ssm_c_im"] = -c_unband(dwc[:, st:, :])
    d_ar = dabar[:, 0, :st].reshape(groups, SSM_STATE)
    d_ai = dabar[:, 0, st:].reshape(groups, SSM_STATE)
    (g["ssm_lambda_re"], g["ssm_lambda_im"], g["ssm_log_dt"], g["ssm_b_re"], g["ssm_b_im"]) = _s5_params_bwd(
        w["ssm_lambda_re"], w["ssm_lambda_im"], w["ssm_log_dt"], w["ssm_b_re"], w["ssm_b_im"], d_ar, d_ai,
        b_unband(dwb[:, :, :st]), b_unband(dwb[:, :, st:]))

    pad = jnp.zeros((rows, inp - (width + qr + kvr + LANES)), F32)
    dproj = jnp.concatenate([du, dcq, dckv, dkpe, pad], axis=1)
    g["w_in"] = _wgrad(hn, dproj, d, inp, True, "in_proj_dw")
    dhn = _matmul(dproj, w["w_in"], mode="nt", name="in_proj_dx")
    dx, g["attn_norm_w"] = _rmsnorm_bwd(x, w["attn_norm_w"], dhn, name="attn_norm_bwd", add=dh1)
    return loss_tile, dx, g


ANY = pl.BlockSpec(memory_space=pl.ANY)
MESH = pl.DeviceIdType.MESH


def _mesh_pos():
    return lax.axis_index("x"), lax.axis_index("y"), lax.axis_index("c")


def _other_chips(x, y):
    return [(1 - x, y), (x, 1 - y), (1 - x, 1 - y)]


def _remote(src, dst, send_sems, recv_sems, k, to):
    return pltpu.make_async_remote_copy(src_ref=src, dst_ref=dst, send_sem=send_sems.at[k], recv_sem=recv_sems.at[k],
                                        device_id=to, device_id_type=MESH)


def _place_shard(shard, piece_idx, row_sharded, name, out_dtype=BF16, pieces=N_CHIPS):
    rs, cs = shard.shape
    tr = _tile(rs, 256, 2 * SUBLANES)
    rb = rs // tr

    def body(p_ref, x_ref, o_ref):
        o_ref[...] = x_ref[...].astype(o_ref.dtype)

    if row_sharded:
        out_shape, out_map = (pieces * rs, cs), (lambda i, p_ref: (p_ref[0] * rb + i, 0))
    else:
        out_shape, out_map = (rs, pieces * cs), (lambda i, p_ref: (i, p_ref[0]))
    return pl.pallas_call(
        body, name=name, out_shape=jax.ShapeDtypeStruct(out_shape, out_dtype),
        grid_spec=pltpu.PrefetchScalarGridSpec(
            num_scalar_prefetch=1, grid=(rb,), in_specs=[pl.BlockSpec((tr, cs), lambda i, p_ref: (i, 0))],
            out_specs=pl.BlockSpec((tr, cs), out_map)),
        compiler_params=_params(("parallel",)),
    )(piece_idx, shard)


def _gather_weights(placed):
    n = len(placed)
    shard_shape = []
    for arr, row_sharded, _ in placed:
        r, c = arr.shape
        shard_shape.append((r // N_CHIPS, c) if row_sharded else (r, c // N_CHIPS))
    n_remote = sum(3 if direct else 6 for _, _, direct in placed)

    def body(*refs):
        outs = refs[n:2 * n]
        send_sems, recv_sems = refs[2 * n:]
        x, y, c = _mesh_pos()
        sibling = (x, y, 1 - c)
        chips = _other_chips(x, y)
        me = 2 * x + y

        def window(t, piece, half):
            (rs, cs), row_sharded = shard_shape[t], placed[t][1]
            hr = rs // 2
            if row_sharded:
                if half is None:
                    return outs[t].at[pl.ds(piece * rs, rs), :]
                return outs[t].at[pl.ds(piece * rs + half * hr, hr), :]
            if half is None:
                return outs[t].at[:, pl.ds(piece * cs, cs)]
            return outs[t].at[pl.ds(half * hr, hr), pl.ds(piece * cs, cs)]

        first, passed, base = [], [], []
        k = 0
        for t, (_, _, direct) in enumerate(placed):
            base.append(k)
            mine = window(t, me, None if direct else c)
            for j, chip in enumerate(chips):
                cp = _remote(mine, mine, send_sems, recv_sems, k + j, (*chip, c))
                cp.start()
                first.append(cp)
            k += 3 if direct else 6
        for t, (_, _, direct) in enumerate(placed):
            for j, (px, py) in enumerate(chips):
                arrived = window(t, 2 * px + py, None if direct else c)
                _remote(arrived, arrived, send_sems, recv_sems, base[t] + j, sibling).wait_recv()
                if not direct:
                    cp = _remote(arrived, arrived, send_sems, recv_sems, base[t] + 3 + j, sibling)
                    cp.start()
                    passed.append(cp)
        for t, (_, _, direct) in enumerate(placed):
            if direct:
                continue
            for j, (px, py) in enumerate(chips):
                other = window(t, 2 * px + py, 1 - c)
                _remote(other, other, send_sems, recv_sems, base[t] + 3 + j, sibling).wait_recv()
        for cp in first + passed:
            cp.wait_send()

    return pl.pallas_call(
        body, name="gather_weights", in_specs=[ANY] * n, out_specs=[ANY] * n,
        out_shape=[jax.ShapeDtypeStruct(arr.shape, arr.dtype) for arr, _, _ in placed],
        input_output_aliases={t: t for t in range(n)},
        scratch_shapes=[pltpu.SemaphoreType.DMA((n_remote,)), pltpu.SemaphoreType.DMA((n_remote,))],
    )(*[arr for arr, _, _ in placed])


def _exchange(name, arrays, out_shapes, plan, n_copies, in_place=False):
    n = len(arrays)

    def body(*refs):
        ins, outs = refs[:n], refs[n:n + len(out_shapes)]
        send_sems, recv_sems = refs[n + len(out_shapes):]
        sends, recvs = plan(ins, outs, send_sems, recv_sems)
        for cp in sends:
            cp.start()
        for cp in recvs:
            cp.wait_recv()
        for cp in sends:
            cp.wait_send()

    return pl.pallas_call(
        body, name=name, in_specs=[ANY] * n, out_specs=[ANY] * len(out_shapes), out_shape=out_shapes,
        input_output_aliases={t: t for t in range(n)} if in_place else {},
        scratch_shapes=[pltpu.SemaphoreType.DMA((n_copies,)), pltpu.SemaphoreType.DMA((n_copies,))],
    )(*arrays)


def _swap_other_half(grads):
    def plan(ins, outs, send_sems, recv_sems):
        x, y, c = _mesh_pos()
        sends = [_remote(ins[t].at[1 - c], outs[t], send_sems, recv_sems, t, (x, y, 1 - c)) for t in range(len(ins))]
        return sends, sends

    shapes = [jax.ShapeDtypeStruct(g.shape[1:], g.dtype) for g in grads]
    return _exchange("grad_swap_halves", grads, shapes, plan, len(grads))


def _scatter_pieces(sums):
    def plan(ins, outs, send_sems, recv_sems):
        x, y, c = _mesh_pos()
        sends = []
        for t in range(len(ins)):
            for j, (px, py) in enumerate(_other_chips(x, y)):
                sends.append(_remote(ins[t].at[2 * px + py], outs[t].at[j], send_sems, recv_sems, 3 * t + j, (px, py, c)))
        return sends, sends

    shapes = [jax.ShapeDtypeStruct((3,) + s.shape[1:], s.dtype) for s in sums]
    return _exchange("grad_scatter_pieces", sums, shapes, plan, 3 * len(sums))


def _join_halves(halves):
    def plan(ins, outs, send_sems, recv_sems):
        x, y, c = _mesh_pos()
        sends = [_remote(outs[t].at[c], outs[t].at[c], send_sems, recv_sems, t, (x, y, 1 - c)) for t in range(len(ins))]
        recvs = [_remote(outs[t].at[1 - c], outs[t].at[1 - c], send_sems, recv_sems, t, (x, y, 1 - c))
                 for t in range(len(ins))]
        return sends, recvs

    shapes = [jax.ShapeDtypeStruct(h.shape, h.dtype) for h in halves]
    return _exchange("grad_join_halves", halves, shapes, plan, len(halves), in_place=True)


def _all_to_all_small(slots):
    def plan(ins, outs, send_sems, recv_sems):
        x, y, c = _mesh_pos()
        mine = outs[0].at[4 * x + 2 * y + c]
        sends, recvs = [], []
        for mask in range(1, 8):
            px, py, pc = x ^ ((mask >> 2) & 1), y ^ ((mask >> 1) & 1), c ^ (mask & 1)
            theirs = outs[0].at[4 * px + 2 * py + pc]
            sends.append(_remote(mine, mine, send_sems, recv_sems, mask - 1, (px, py, pc)))
            recvs.append(_remote(theirs, theirs, send_sems, recv_sems, mask - 1, (px, py, pc)))
        return sends, recvs

    shape = jax.ShapeDtypeStruct(slots.shape, slots.dtype)
    return _exchange("small_grads_all_to_all", [slots], [shape], plan, 7, in_place=True)[0]


def _add_other_half(g4, got, where, name):
    _, pieces, sr, sc = g4.shape
    tr = _tile(sr, 256, 2 * SUBLANES)

    def body(w_ref, a_ref, b_ref, o_ref):
        o_ref[...] = (a_ref[...] + b_ref[...]).astype(o_ref.dtype)

    blk = pl.BlockSpec((None, tr, sc), lambda p, i, w_ref: (p, i, 0))
    return pl.pallas_call(
        body, name=name, out_shape=jax.ShapeDtypeStruct((pieces, sr, sc), BF16),
        grid_spec=pltpu.PrefetchScalarGridSpec(
            num_scalar_prefetch=1, grid=(pieces, sr // tr),
            in_specs=[pl.BlockSpec((None, None, tr, sc), lambda p, i, w_ref: (w_ref[0], p, i, 0)), blk], out_specs=blk),
        compiler_params=_params(("parallel", "parallel")),
    )(where, g4, got)


def _add_pieces(g4, got_half, got_pieces, where, name):
    _, _, sr, sc = g4.shape
    tr = _tile(sr, 256, 2 * SUBLANES)

    def body(w_ref, a_ref, b_ref, r_ref, o_ref):
        acc = a_ref[...] + b_ref[...]
        for j in range(3):
            acc = acc + r_ref[j].astype(F32)
        o_ref[...] = acc

    return pl.pallas_call(
        body, name=name, out_shape=jax.ShapeDtypeStruct((N_CORES, sr, sc), F32),
        grid_spec=pltpu.PrefetchScalarGridSpec(
            num_scalar_prefetch=1, grid=(sr // tr,),
            in_specs=[pl.BlockSpec((None, None, tr, sc), lambda i, w_ref: (w_ref[0], w_ref[1], i, 0)),
                      pl.BlockSpec((None, tr, sc), lambda i, w_ref: (w_ref[1], i, 0)),
                      pl.BlockSpec((3, tr, sc), lambda i, w_ref: (0, i, 0))],
            out_specs=pl.BlockSpec((None, tr, sc), lambda i, w_ref: (w_ref[0], i, 0))),
        compiler_params=_params(("parallel",)),
    )(where, g4, got_half, got_pieces)


def _sum_slots(slots):
    n, rows, lanes = slots.shape
    tr = _tile(rows, 512, SUBLANES)

    def body(s_ref, o_ref):
        acc = s_ref[0]
        for k in range(1, n):
            acc = acc + s_ref[k]
        o_ref[...] = acc

    return pl.pallas_call(
        body, name="small_grads_sum", grid=(rows // tr,),
        in_specs=[pl.BlockSpec((n, tr, lanes), lambda i: (0, i, 0))], out_specs=pl.BlockSpec((tr, lanes), lambda i: (i, 0)),
        out_shape=jax.ShapeDtypeStruct((rows, lanes), F32), compiler_params=_params(("parallel",)),
    )(slots)


def _adamw(w, g, m, v, name):
    rows, cols = w.shape
    tr = _tile(rows, max(SUBLANES, (1 << 19) // max(cols, 1) // SUBLANES * SUBLANES), SUBLANES)

    def body(w_ref, g_ref, m_ref, v_ref, d_ref, nm_ref, nv_ref):
        gv = g_ref[...]
        nm = ADAM_B1 * m_ref[...] + (1.0 - ADAM_B1) * gv
        nv = ADAM_B2 * v_ref[...] + (1.0 - ADAM_B2) * (gv * gv)
        m_hat = nm / (1.0 - ADAM_B1 ** ADAM_STEP)
        v_hat = nv / (1.0 - ADAM_B2 ** ADAM_STEP)
        d_ref[...] = -ADAM_LR * (m_hat / (jnp.sqrt(v_hat) + ADAM_EPS) + ADAM_WD * w_ref[...])
        nm_ref[...] = nm
        nv_ref[...] = nv

    blk = pl.BlockSpec((tr, cols), lambda i: (i, 0))
    return pl.pallas_call(
        body, name=name, grid=(rows // tr,), in_specs=[blk] * 4, out_specs=[blk] * 3,
        out_shape=[jax.ShapeDtypeStruct((rows, cols), F32)] * 3, compiler_params=_params(("parallel",)),
    )(w, g, m, v)


WEIGHTS = ['attn_norm_w', 'w_in', 'ssm_lambda_re', 'ssm_lambda_im', 'ssm_log_dt', 'ssm_b_re', 'ssm_b_im', 'ssm_c_re',
           'ssm_c_im', 'ssm_d', 'ssm_w_glu', 'ssm_b_glu', 'mla_q_norm_w', 'mla_w_uq', 'mla_kv_norm_w', 'mla_w_ukv',
           'ssm_out_norm_w', 'mla_out_norm_w', 'w_out', 'ffn_norm_w', 'ffn_w_up', 'ffn_conv_w', 'ffn_conv_b',
           'ffn_w_down', 'final_norm_w']
SHARDED = {'w_in': True, 'ssm_w_glu': True, 'mla_w_uq': False, 'mla_w_ukv': False, 'w_out': True, 'ffn_w_up': False,
           'ffn_w_down': True}
SMALL = [n for n in WEIGHTS if n not in SHARDED and n != 'ffn_conv_w']
ROPE_PAD = HEAD_SLOT - QK_NOPE_DIM - QK_ROPE_DIM


def _pad_heads(w_uq, heads):
    qr = w_uq.shape[0]
    w3 = w_uq.reshape(qr, heads, QK_NOPE_DIM + QK_ROPE_DIM)
    return jnp.concatenate([w3, jnp.zeros((qr, heads, ROPE_PAD), w_uq.dtype)], axis=2).reshape(qr, heads * HEAD_SLOT)


def _unpad_heads(g_uq, heads):
    qr = g_uq.shape[0]
    return g_uq.reshape(qr, heads, HEAD_SLOT)[:, :, :QK_NOPE_DIM + QK_ROPE_DIM].reshape(qr, -1)


def _step(args):
    x, positions, target = args["x"][0], args["positions"], args["loss_target"][0]
    rows = x.shape[0]
    p = {n: args[n] for n in WEIGHTS}
    xi, yi, ci = _mesh_pos()
    piece = 2 * xi + yi

    w_in = p["w_in"][0]
    in_width = w_in.shape[1]
    in_pad = (-in_width) % (2 * LANES)
    heads_here = p["mla_w_uq"].shape[2] // (QK_NOPE_DIM + QK_ROPE_DIM)
    shards = {
        "w_in": jnp.pad(w_in, ((0, 0), (0, in_pad))),
        "ssm_w_glu": p["ssm_w_glu"][0],
        "mla_w_uq": _pad_heads(p["mla_w_uq"][0], heads_here),
        "mla_w_ukv": p["mla_w_ukv"][0],
        "w_out": p["w_out"][0],
        "ffn_w_up": p["ffn_w_up"][0],
        "ffn_w_down": p["ffn_w_down"][0],
    }
    conv_w = jnp.pad(p["ffn_conv_w"][0], ((0, SUBLANES - p["ffn_conv_w"].shape[1]), (0, 0)))
    order = list(SHARDED)
    piece_idx = piece.reshape(1).astype(jnp.int32)
    placed = [(_place_shard(shards[n], piece_idx, SHARDED[n], "place_" + n), SHARDED[n], False) for n in order]
    placed.append((_place_shard(conv_w, piece_idx, False, "place_ffn_conv_w", out_dtype=F32), False, True))
    w = dict(zip(order + ["ffn_conv_w"], _gather_weights(placed)))
    groups = p["ssm_lambda_re"].shape[1]
    w.update({
        "attn_norm_w": p["attn_norm_w"], "ssm_lambda_re": p["ssm_lambda_re"][0], "ssm_lambda_im": p["ssm_lambda_im"][0],
        "ssm_log_dt": p["ssm_log_dt"].reshape(groups, 1), "ssm_b_re": p["ssm_b_re"].reshape(groups, -1),
        "ssm_b_im": p["ssm_b_im"].reshape(groups, -1), "ssm_c_re": p["ssm_c_re"][0], "ssm_c_im": p["ssm_c_im"][0],
        "ssm_d": p["ssm_d"], "ssm_b_glu": p["ssm_b_glu"], "mla_q_norm_w": p["mla_q_norm_w"],
        "mla_kv_norm_w": p["mla_kv_norm_w"], "ssm_out_norm_w": p["ssm_out_norm_w"], "mla_out_norm_w": p["mla_out_norm_w"],
        "ffn_norm_w": p["ffn_norm_w"], "ffn_conv_b": p["ffn_conv_b"], "final_norm_w": p["final_norm_w"].reshape(1, -1),
    })

    loss_tile, dx, g = _local_step(x, positions.reshape(rows, 1).astype(F32), target, w)
    loss = lax.psum(loss_tile[0, 0], ("x", "y", "c"))

    where = jnp.stack([ci, piece]).astype(jnp.int32)
    got_half = _swap_other_half([g[n] for n in order])
    sums = [_add_other_half(g[n], got_half[t], where, "grad_add_half_" + n) for t, n in enumerate(order)]
    got_pieces = _scatter_pieces(sums)
    halves = [_add_pieces(g[n], got_half[t], got_pieces[t], where, "grad_add_pieces_" + n) for t, n in enumerate(order)]
    joined = _join_halves(halves)
    grads = {}
    for t, n in enumerate(order):
        j = joined[t]
        grads[n] = jnp.concatenate([j[0], j[1]], axis=1) if SHARDED[n] else j.reshape(2 * j.shape[1], j.shape[2])
    grads["w_in"] = grads["w_in"][:, :in_width]
    grads["mla_w_uq"] = _unpad_heads(grads["mla_w_uq"], heads_here)

    flat = [g[n].reshape(-1) for n in SMALL] + [g["ffn_conv_w"].reshape(-1)]
    sizes = [f.shape[0] for f in flat]
    total = sum(sizes)
    tile_elems = SUBLANES * LANES
    padded = -(-total // tile_elems) * tile_elems

    def pack(parts):
        parts = list(parts)
        have = sum(q.shape[0] for q in parts)
        return jnp.concatenate(parts + [jnp.zeros((padded - have,), F32)]).reshape(padded // LANES, LANES)

    me_idx = (4 * xi + 2 * yi + ci).reshape(1).astype(jnp.int32)
    slots = _place_shard(pack(flat), me_idx, True, "place_small_grads", out_dtype=F32, pieces=N_CHIPS * N_CORES)
    small_sum = _sum_slots(_all_to_all_small(slots.reshape(N_CHIPS * N_CORES, padded // LANES, LANES)))
    flat_sum = small_sum.reshape(-1)
    offs = [0]
    for s in sizes:
        offs.append(offs[-1] + s)
    for k, n in enumerate(SMALL):
        grads[n] = flat_sum[offs[k]:offs[k + 1]].reshape(p[n].shape)
    taps, cols_here = p["ffn_conv_w"].shape[1], p["ffn_conv_w"].shape[2]
    conv_full = flat_sum[offs[len(SMALL)]:offs[len(SMALL) + 1]].reshape(taps, N_CHIPS * cols_here)
    grads["ffn_conv_w"] = lax.dynamic_slice_in_dim(conv_full, piece * cols_here, cols_here, axis=1)

    delta, new_m, new_v = {}, {}, {}
    for n in list(SHARDED) + ["ffn_conv_w"]:
        shape = p[n].shape
        d2, m2, v2 = _adamw(p[n].reshape(shape[1:]), grads[n], args["m_" + n].reshape(shape[1:]),
                            args["v_" + n].reshape(shape[1:]), "adamw_" + n)
        grads[n] = grads[n].reshape(shape)
        delta[n], new_m[n], new_v[n] = d2.reshape(shape), m2.reshape(shape), v2.reshape(shape)
    d2, m2, v2 = _adamw(pack(p[n].reshape(-1) for n in SMALL), small_sum, pack(args["m_" + n].reshape(-1) for n in SMALL),
                        pack(args["v_" + n].reshape(-1) for n in SMALL), "adamw_small")
    for k, n in enumerate(SMALL):
        for src, dst in ((d2, delta), (m2, new_m), (v2, new_v)):
            dst[n] = src.reshape(-1)[offs[k]:offs[k + 1]].reshape(p[n].shape)

    return (loss, dx[None], *[grads[n] for n in WEIGHTS], *[delta[n] for n in WEIGHTS],
            *[new_m[n] for n in WEIGHTS], *[new_v[n] for n in WEIGHTS])


def kernel(x, positions, attn_norm_w, w_in, ssm_lambda_re, ssm_lambda_im, ssm_log_dt, ssm_b_re, ssm_b_im, ssm_c_re, ssm_c_im, ssm_d, ssm_w_glu, ssm_b_glu, mla_q_norm_w, mla_w_uq, mla_kv_norm_w, mla_w_ukv, ssm_out_norm_w, mla_out_norm_w, w_out, ffn_norm_w, ffn_w_up, ffn_conv_w, ffn_conv_b, ffn_w_down, final_norm_w, loss_target, m_attn_norm_w, m_w_in, m_ssm_lambda_re, m_ssm_lambda_im, m_ssm_log_dt, m_ssm_b_re, m_ssm_b_im, m_ssm_c_re, m_ssm_c_im, m_ssm_d, m_ssm_w_glu, m_ssm_b_glu, m_mla_q_norm_w, m_mla_w_uq, m_mla_kv_norm_w, m_mla_w_ukv, m_ssm_out_norm_w, m_mla_out_norm_w, m_w_out, m_ffn_norm_w, m_ffn_w_up, m_ffn_conv_w, m_ffn_conv_b, m_ffn_w_down, m_final_norm_w, v_attn_norm_w, v_w_in, v_ssm_lambda_re, v_ssm_lambda_im, v_ssm_log_dt, v_ssm_b_re, v_ssm_b_im, v_ssm_c_re, v_ssm_c_im, v_ssm_d, v_ssm_w_glu, v_ssm_b_glu, v_mla_q_norm_w, v_mla_w_uq, v_mla_kv_norm_w, v_mla_w_ukv, v_ssm_out_norm_w, v_mla_out_norm_w, v_w_out, v_ffn_norm_w, v_ffn_w_up, v_ffn_conv_w, v_ffn_conv_b, v_ffn_w_down, v_final_norm_w):
    return _step(dict(locals()))
```

```python
import functools
import math

import jax
import jax.numpy as jnp
from jax import lax
from jax.experimental import pallas as pl
from jax.experimental.pallas import tpu as pltpu

F32 = jnp.float32
BF16 = jnp.bfloat16

SSM_GROUP = 16
SSM_STATE = 64
QK_NOPE_DIM = 128
QK_ROPE_DIM = 64
V_HEAD_DIM = 128
ROPE_THETA = 10000.0
RMS_EPS = 1e-6
ADAM_LR, ADAM_B1, ADAM_B2, ADAM_EPS, ADAM_WD, ADAM_STEP = 0.001, 0.9, 0.999, 1e-08, 0.01, 10

LANES = 128
SUBLANES = 8
VMEM_LIMIT_BYTES = 56 * 1024 * 1024

GROUPS_PER_BATCH = LANES // SSM_GROUP
STATE_PER_BATCH = GROUPS_PER_BATCH * SSM_STATE
HEAD_SLOT = 2 * LANES
NEG_INF = -1e30
ATTN_BLOCK = 512

N_CHIPS = 4
N_CORES = 2


def _tile(n, pref, align=LANES):
    if n <= pref:
        return n
    t = (pref // align) * align
    while t >= align:
        if n % t == 0:
            return t
        t -= align
    return n


def _params(sem):
    return pltpu.CompilerParams(dimension_semantics=sem, vmem_limit_bytes=VMEM_LIMIT_BYTES)


def _dot(a, b, dims):
    return lax.dot_general(a, b, (dims, ((), ())), preferred_element_type=F32)


def _dot_nn(a, b):
    return _dot(a, b, ((1,), (0,)))


def _dot_nt(a, b):
    return _dot(a, b, ((1,), (1,)))


def _dot_tn(a, b):
    return _dot(a, b, ((0,), (0,)))


def _matmul(a, b, *, mode, name, tm=512, tn=1024, tk=2048, bias=None, add=None, out_dtype=F32,
            out_blocks=None, a_split=False, b_split=False):
    if a_split:
        assert mode == "nt"
        a_shape = (a.shape[1], 2 * a.shape[2])
    else:
        a_shape = a.shape
    if b_split:
        assert mode == "tn"
        b_shape = (b.shape[1], 2 * b.shape[2])
    else:
        b_shape = b.shape
    if mode == "nn":
        (m, k), (k2, n) = a_shape, b_shape
    elif mode == "nt":
        (m, k), (n, k2) = a_shape, b_shape
    else:
        (k, m), (k2, n) = a_shape, b_shape
    assert k == k2, (a.shape, b.shape, mode)
    tm, tn, tk = _tile(m, tm, SUBLANES), _tile(n, tn), _tile(k, tk)
    nk = k // tk
    a_spec = {"nn": pl.BlockSpec((tm, tk), lambda i, j, kk: (i, kk)),
              "nt": pl.BlockSpec((tm, tk), lambda i, j, kk: (i, kk)),
              "tn": pl.BlockSpec((tk, tm), lambda i, j, kk: (kk, i))}[mode]
    b_spec = {"nn": pl.BlockSpec((tk, tn), lambda i, j, kk: (kk, j)),
              "nt": pl.BlockSpec((tn, tk), lambda i, j, kk: (j, kk)),
              "tn": pl.BlockSpec((tk, tn), lambda i, j, kk: (kk, j))}[mode]
    if a_split:
        kb = a.shape[2] // tk
        assert a.shape[2] % tk == 0
        a_spec = pl.BlockSpec((None, tm, tk), lambda i, j, kk: (kk // kb, i, kk % kb))
    if b_split:
        nb = b.shape[2] // tn
        assert b.shape[2] % tn == 0
        b_spec = pl.BlockSpec((None, tk, tn), lambda i, j, kk: (j // nb, kk, j % nb))
    dot = {"nn": _dot_nn, "nt": _dot_nt, "tn": _dot_tn}[mode]
    in_specs, operands = [a_spec, b_spec], [a, b]
    if bias is not None:
        in_specs.append(pl.BlockSpec((1, tn), lambda i, j, kk: (0, j)))
        operands.append(bias)
    if add is not None:
        in_specs.append(pl.BlockSpec((tm, tn), lambda i, j, kk: (i, j)))
        operands.append(add)

    def body(*refs):
        a_ref, b_ref = refs[0], refs[1]
        rest = list(refs[2:])
        bias_ref = rest.pop(0) if bias is not None else None
        add_ref = rest.pop(0) if add is not None else None
        o_ref, acc_ref = rest

        def finish(acc):
            if bias_ref is not None:
                acc = acc + bias_ref[...]
            if add_ref is not None:
                acc = acc + add_ref[...]
            o_ref[...] = acc.astype(o_ref.dtype)

        part = dot(a_ref[...].astype(BF16), b_ref[...].astype(BF16))
        if nk == 1:
            finish(part)
        else:
            kk = pl.program_id(2)

            @pl.when(kk == 0)
            def _():
                acc_ref[...] = part

            @pl.when(jnp.logical_and(kk > 0, kk < nk - 1))
            def _():
                acc_ref[...] += part

            @pl.when(kk == nk - 1)
            def _():
                finish(acc_ref[...] + part)

    if out_blocks is None:
        out_shape = jax.ShapeDtypeStruct((m, n), out_dtype)
        out_spec = pl.BlockSpec((tm, tn), lambda i, j, kk: (i, j))
    else:
        shape, block, index_map = out_blocks(tm, tn)
        out_shape = jax.ShapeDtypeStruct(shape, out_dtype)
        out_spec = pl.BlockSpec(block, index_map)
    acc_shape = (tm, tn) if nk > 1 else (SUBLANES, LANES)
    return pl.pallas_call(
        body, name=name, grid=(m // tm, n // tn, nk), in_specs=in_specs, out_specs=out_spec, out_shape=out_shape,
        scratch_shapes=[pltpu.VMEM(acc_shape, F32)],
        compiler_params=_params(("parallel", "parallel", "arbitrary")),
    )(*operands)


def _wgrad_blocks(rows, cols, row_sharded):
    if row_sharded:
        sr, sc = rows // N_CHIPS, cols // N_CORES
    else:
        sr, sc = rows // N_CORES, cols // N_CHIPS

    def make(tm, tn):
        assert sr % tm == 0 and sc % tn == 0, (rows, cols, tm, tn)
        rb, cb = sr // tm, sc // tn
        if row_sharded:
            def index_map(i, j, kk):
                return (j // cb, i // rb, i % rb, j % cb)
        else:
            def index_map(i, j, kk):
                return (i // rb, j // cb, i % rb, j % cb)
        return (N_CORES, N_CHIPS, sr, sc), (None, None, tm, tn), index_map

    return make, (sr, sc)


def _rms_rows(x):
    return lax.rsqrt(jnp.mean(x * x, axis=-1, keepdims=True) + RMS_EPS)


def _rmsnorm_fwd(x, w, *, name, width=None, col=0, out_dtype=BF16, tr=256):
    rows = x.shape[0]
    width = x.shape[1] if width is None else width
    tr = _tile(rows, tr, SUBLANES)

    def body(x_ref, w_ref, o_ref):
        xv = x_ref[...]
        o_ref[...] = (xv * _rms_rows(xv) * w_ref[...]).astype(o_ref.dtype)

    return pl.pallas_call(
        body, name=name, grid=(rows // tr,),
        in_specs=[pl.BlockSpec((tr, width), lambda i: (i, col)), pl.BlockSpec((1, width), lambda i: (0, 0))],
        out_specs=pl.BlockSpec((tr, width), lambda i: (i, 0)),
        out_shape=jax.ShapeDtypeStruct((rows, width), out_dtype),
        compiler_params=_params(("parallel",)),
    )(x, w)


def _rmsnorm_bwd_rows(xv, w, dy):
    r = _rms_rows(xv)
    n = xv * r
    dn = dy * w
    dx = r * (dn - n * jnp.mean(dn * n, axis=-1, keepdims=True))
    return dx, dy * n


def _rmsnorm_bwd(x, w, dy, *, name, width=None, col=0, dy_col=0, add=None, tr=256):
    rows = x.shape[0]
    width = x.shape[1] if width is None else width
    tr = _tile(rows, tr, SUBLANES)
    in_specs = [pl.BlockSpec((tr, width), lambda i: (i, col)), pl.BlockSpec((1, width), lambda i: (0, 0)),
                pl.BlockSpec((tr, width), lambda i: (i, dy_col))]
    operands = [x, w, dy]
    if add is not None:
        in_specs.append(pl.BlockSpec((tr, width), lambda i: (i, 0)))
        operands.append(add)

    def body(*refs):
        x_ref, w_ref, dy_ref = refs[:3]
        add_ref = refs[3] if add is not None else None
        dx_ref, dw_ref = refs[-2:]
        dx, dwp = _rmsnorm_bwd_rows(x_ref[...], w_ref[...], dy_ref[...])
        if add_ref is not None:
            dx = dx + add_ref[...]
        dx_ref[...] = dx
        part = jnp.sum(dwp, axis=0, keepdims=True)

        @pl.when(pl.program_id(0) == 0)
        def _():
            dw_ref[...] = part

        @pl.when(pl.program_id(0) > 0)
        def _():
            dw_ref[...] += part

    return pl.pallas_call(
        body, name=name, grid=(rows // tr,), in_specs=in_specs,
        out_specs=[pl.BlockSpec((tr, width), lambda i: (i, 0)), pl.BlockSpec((1, width), lambda i: (0, 0))],
        out_shape=[jax.ShapeDtypeStruct((rows, width), F32), jax.ShapeDtypeStruct((1, width), F32)],
        compiler_params=_params(("arbitrary",)),
    )(*operands)


def _final_norm_loss(h, w, target, *, tr=256):
    rows, d = h.shape
    tr = _tile(rows, tr, SUBLANES)

    def body(h_ref, w_ref, t_ref, loss_ref, dh_ref, dw_ref):
        hv, wv = h_ref[...], w_ref[...]
        r = _rms_rows(hv)
        n = hv * r
        err = n * wv - t_ref[...]
        d_out = err * (1.0 / d)
        dn = d_out * wv
        dh_ref[...] = r * (dn - n * jnp.mean(dn * n, axis=-1, keepdims=True))
        dw_part = jnp.sum(d_out * n, axis=0, keepdims=True)
        loss_part = jnp.full((SUBLANES, LANES), 0.5 / d, F32) * jnp.sum(err * err)

        @pl.when(pl.program_id(0) == 0)
        def _():
            dw_ref[...] = dw_part
            loss_ref[...] = loss_part

        @pl.when(pl.program_id(0) > 0)
        def _():
            dw_ref[...] += dw_part
            loss_ref[...] += loss_part

    return pl.pallas_call(
        body, name="final_norm_loss", grid=(rows // tr,),
        in_specs=[pl.BlockSpec((tr, d), lambda i: (i, 0)), pl.BlockSpec((1, d), lambda i: (0, 0)),
                  pl.BlockSpec((tr, d), lambda i: (i, 0))],
        out_specs=[pl.BlockSpec((SUBLANES, LANES), lambda i: (0, 0)), pl.BlockSpec((tr, d), lambda i: (i, 0)),
                   pl.BlockSpec((1, d), lambda i: (0, 0))],
        out_shape=[jax.ShapeDtypeStruct((SUBLANES, LANES), F32), jax.ShapeDtypeStruct((rows, d), F32),
                   jax.ShapeDtypeStruct((1, d), F32)],
        compiler_params=_params(("arbitrary",)),
    )(h, w, target)


def _cmul(ar, ai, br, bi):
    return ar * br - ai * bi, ar * bi + ai * br


def _expand_matrix(groups, reps):
    row = lax.broadcasted_iota(jnp.int32, (groups, groups * reps), 0)
    colg = lax.broadcasted_iota(jnp.int32, (groups, groups * reps), 1) // reps
    return (row == colg).astype(F32)


def _dot_exact(a, b, dims):
    return lax.dot_general(a, b, (dims, ((), ())), preferred_element_type=F32, precision=lax.Precision.HIGHEST)


def _s5_discretize(lr, li, dt):
    mag = jnp.exp(lr * dt)
    th = li * dt
    ar, ai = mag * jnp.cos(th), mag * jnp.sin(th)
    nr, ni = ar - 1.0, ai
    den = lr * lr + li * li
    zr = (nr * lr + ni * li) / den
    zi = (ni * lr - nr * li) / den
    return mag, ar, ai, nr, ni, den, zr, zi


def _s5_params(lam_re, lam_im, log_dt, b_re, b_im):
    g, p = lam_re.shape
    ph = b_re.shape[1]

    def body(lr_ref, li_ref, ldt_ref, br_ref, bi_ref, ar_ref, ai_ref, bbr_ref, bbi_ref):
        dt = jnp.exp(ldt_ref[...])
        _, ar, ai, _, _, _, zr, zi = _s5_discretize(lr_ref[...], li_ref[...], dt)
        ar_ref[...] = ar
        ai_ref[...] = ai
        e = _expand_matrix(p, ph // p)
        zr_x = _dot_exact(zr, e, ((1,), (0,)))
        zi_x = _dot_exact(zi, e, ((1,), (0,)))
        bre, bim = br_ref[...], bi_ref[...]
        bbr_ref[...] = zr_x * bre - zi_x * bim
        bbi_ref[...] = zr_x * bim + zi_x * bre

    return pl.pallas_call(
        body, name="s5_params",
        out_shape=[jax.ShapeDtypeStruct((g, p), F32)] * 2 + [jax.ShapeDtypeStruct((g, ph), F32)] * 2,
    )(lam_re, lam_im, log_dt, b_re, b_im)


def _s5_params_bwd(lam_re, lam_im, log_dt, b_re, b_im, d_ar, d_ai, d_bbr, d_bbi):
    g, p = lam_re.shape
    ph = b_re.shape[1]

    def body(lr_ref, li_ref, ldt_ref, br_ref, bi_ref, dar_ref, dai_ref, dbr_ref, dbi_ref,
             dlr_ref, dli_ref, dldt_ref, dbre_ref, dbim_ref):
        lr, li = lr_ref[...], li_ref[...]
        dt = jnp.exp(ldt_ref[...])
        mag, ar, ai, nr, ni, den, zr, zi = _s5_discretize(lr, li, dt)
        e = _expand_matrix(p, ph // p)
        zr_x = _dot_exact(zr, e, ((1,), (0,)))
        zi_x = _dot_exact(zi, e, ((1,), (0,)))
        bre, bim, dbr, dbi = br_ref[...], bi_ref[...], dbr_ref[...], dbi_ref[...]
        dbre_ref[...] = zr_x * dbr + zi_x * dbi
        dbim_ref[...] = zr_x * dbi - zi_x * dbr
        dzr = _dot_exact(bre * dbr + bim * dbi, e, ((1,), (1,)))
        dzi = _dot_exact(bre * dbi - bim * dbr, e, ((1,), (1,)))
        inv = 1.0 / den
        d_nr = (dzr * lr - dzi * li) * inv
        d_ni = (dzr * li + dzi * lr) * inv
        d_den = -(dzr * zr + dzi * zi) * inv
        d_lr = (dzr * nr + dzi * ni) * inv + 2.0 * lr * d_den
        d_li = (dzr * ni - dzi * nr) * inv + 2.0 * li * d_den
        t_ar = dar_ref[...] + d_nr
        t_ai = dai_ref[...] + d_ni
        d_lrdt = t_ar * ar + t_ai * ai
        d_th = t_ai * ar - t_ar * ai
        dlr_ref[...] = d_lr + d_lrdt * dt
        dli_ref[...] = d_li + d_th * dt
        dldt_ref[...] = jnp.sum(d_lrdt * lr + d_th * li, axis=1, keepdims=True) * dt

    return pl.pallas_call(
        body, name="s5_params_bwd",
        out_shape=[jax.ShapeDtypeStruct((g, p), F32)] * 2 + [jax.ShapeDtypeStruct((g, 1), F32)]
        + [jax.ShapeDtypeStruct((g, ph), F32)] * 2,
    )(lam_re, lam_im, log_dt, b_re, b_im, d_ar, d_ai, d_bbr, d_bbi)


def _powers(ar, ai, count):
    out = [(ar, ai)]
    for _ in range(count - 1):
        out.append(_cmul(out[-1][0], out[-1][1], ar, ai))
    return out


def _scan_coefs(ar, ai, reverse):
    w = ar.shape[-1]
    pw = _powers(ar, ai, SUBLANES)
    row = lax.broadcasted_iota(jnp.int32, (SUBLANES, w), 0)
    steps = []
    d = 1
    while d < SUBLANES:
        keep = (row < SUBLANES - d) if reverse else (row >= d)
        pr, pi = pw[d - 1]
        steps.append((d, jnp.where(keep, pr, 0.0), jnp.where(keep, pi, 0.0)))
        d *= 2
    cr = jnp.zeros((SUBLANES, w), F32)
    ci = jnp.zeros((SUBLANES, w), F32)
    for t in range(SUBLANES):
        pr, pi = pw[SUBLANES - 1 - t] if reverse else pw[t]
        cr = jnp.where(row == t, pr, cr)
        ci = jnp.where(row == t, pi, ci)
    return steps, cr, ci


def _scan_tile(xr, xi, carry_r, carry_i, coefs, reverse):
    steps, cr, ci = coefs
    for d, mr, mi in steps:
        shift = SUBLANES - d if reverse else d
        sr, si = pltpu.roll(xr, shift, 0), pltpu.roll(xi, shift, 0)
        pr, pi = _cmul(mr, mi, sr, si)
        xr, xi = xr + pr, xi + pi
    pr, pi = _cmul(cr, ci, carry_r, carry_i)
    return xr + pr, xi + pi


def _gelu(x):
    c = math.sqrt(2.0 / math.pi)
    return 0.5 * x * (1.0 + jnp.tanh(c * (x + 0.044715 * x * x * x)))


def _gelu_grad(x):
    c = math.sqrt(2.0 / math.pi)
    t = jnp.tanh(c * (x + 0.044715 * x * x * x))
    return 0.5 * (1.0 + t) + 0.5 * x * (1.0 - t * t) * c * (1.0 + 3.0 * 0.044715 * x * x)


def _s5_fwd(proj, wb, wc, d_skip, abar):
    rows = proj.shape[0]
    nb = wb.shape[0]
    s2 = 2 * STATE_PER_BATCH
    st = STATE_PER_BATCH
    chunk = _tile(rows, 512, SUBLANES)

    def body(u_ref, wb_ref, wc_ref, d_ref, a_ref, s_ref, y_ref, yg_ref):
        for c0 in range(0, rows, chunk):
            s_ref[pl.ds(c0, chunk), :] = _dot_nn(u_ref[pl.ds(c0, chunk), :].astype(BF16), wb_ref[...])
        av = a_ref[...]
        coefs = _scan_coefs(av[:, :st], av[:, st:], reverse=False)

        def tile(b, carry):
            r0 = pl.multiple_of(b * SUBLANES, SUBLANES)
            xr, xi = _scan_tile(s_ref[pl.ds(r0, SUBLANES), :st], s_ref[pl.ds(r0, SUBLANES), st:], carry[0], carry[1],
                                coefs, False)
            s_ref[pl.ds(r0, SUBLANES), :st] = xr
            s_ref[pl.ds(r0, SUBLANES), st:] = xi
            return xr[SUBLANES - 1:, :], xi[SUBLANES - 1:, :]

        zero = jnp.zeros((1, st), F32)
        lax.fori_loop(0, rows // SUBLANES, tile, (zero, zero))
        for c0 in range(0, rows, chunk):
            y = _dot_nn(s_ref[pl.ds(c0, chunk), :].astype(BF16), wc_ref[...]) + d_ref[...] * u_ref[pl.ds(c0, chunk), :]
            y_ref[pl.ds(c0, chunk), :] = y
            yg_ref[pl.ds(c0, chunk), :] = _gelu(y).astype(BF16)

    return pl.pallas_call(
        body, name="s5_fwd", grid=(nb,),
        in_specs=[pl.BlockSpec((rows, LANES), lambda j: (0, j)), pl.BlockSpec((None, LANES, s2), lambda j: (j, 0, 0)),
                  pl.BlockSpec((None, s2, LANES), lambda j: (j, 0, 0)), pl.BlockSpec((1, LANES), lambda j: (0, j)),
                  pl.BlockSpec((None, 1, s2), lambda j: (j, 0, 0))],
        out_specs=[pl.BlockSpec((rows, s2), lambda j: (0, j)), pl.BlockSpec((rows, LANES), lambda j: (0, j)),
                   pl.BlockSpec((rows, LANES), lambda j: (0, j))],
        out_shape=[jax.ShapeDtypeStruct((rows, nb * s2), F32), jax.ShapeDtypeStruct((rows, nb * LANES), F32),
                   jax.ShapeDtypeStruct((rows, nb * LANES), BF16)],
        compiler_params=_params(("parallel",)),
    )(proj, wb, wc, d_skip, abar)


def _s5_bwd(proj, states, y_pre, dyg_a, dyg_b, wb, wc, d_skip, abar):
    rows = proj.shape[0]
    nb = wb.shape[0]
    s2 = 2 * STATE_PER_BATCH
    st = STATE_PER_BATCH
    chunk = _tile(rows, 512, SUBLANES)
    n_tiles = rows // SUBLANES

    def body(u_ref, s_ref, y_ref, ga_ref, gb_ref, wb_ref, wc_ref, d_ref, a_ref,
             du_ref, dwb_ref, dwc_ref, da_ref, dd_ref, ds_ref, dy_ref):
        dy_ref[...] = (ga_ref[...] + gb_ref[...]) * _gelu_grad(y_ref[...])
        dd_ref[...] = jnp.sum(dy_ref[...] * u_ref[...], axis=0, keepdims=True)
        for c0 in range(0, rows, chunk):
            ds_ref[pl.ds(c0, chunk), :] = _dot_nt(dy_ref[pl.ds(c0, chunk), :].astype(BF16), wc_ref[...])
        dwc_ref[...] = _dot_tn(s_ref[...].astype(BF16), dy_ref[...].astype(BF16))
        av = a_ref[...]
        coefs = _scan_coefs(av[:, :st], -av[:, st:], reverse=True)
        row = lax.broadcasted_iota(jnp.int32, (SUBLANES, st), 0)

        def tile(k, carry):
            cr, ci, acc_r, acc_i = carry
            b = n_tiles - 1 - k
            r0 = pl.multiple_of(b * SUBLANES, SUBLANES)
            rp = pl.multiple_of(jnp.maximum(b - 1, 0) * SUBLANES, SUBLANES)
            xr, xi = _scan_tile(ds_ref[pl.ds(r0, SUBLANES), :st], ds_ref[pl.ds(r0, SUBLANES), st:], cr, ci, coefs, True)
            ds_ref[pl.ds(r0, SUBLANES), :st] = xr
            ds_ref[pl.ds(r0, SUBLANES), st:] = xi
            first = jnp.where(b > 0, 1.0, 0.0)
            pr = jnp.where(row == 0, pltpu.roll(s_ref[pl.ds(rp, SUBLANES), :st], 1, 0) * first,
                           pltpu.roll(s_ref[pl.ds(r0, SUBLANES), :st], 1, 0))
            pi = jnp.where(row == 0, pltpu.roll(s_ref[pl.ds(rp, SUBLANES), st:], 1, 0) * first,
                           pltpu.roll(s_ref[pl.ds(r0, SUBLANES), st:], 1, 0))
            acc_r = acc_r + pr * xr + pi * xi
            acc_i = acc_i + pr * xi - pi * xr
            return xr[:1, :], xi[:1, :], acc_r, acc_i

        zero = jnp.zeros((1, st), F32)
        zacc = jnp.zeros((SUBLANES, st), F32)
        _, _, acc_r, acc_i = lax.fori_loop(0, n_tiles, tile, (zero, zero, zacc, zacc))
        da_ref[:, :st] = jnp.sum(acc_r, axis=0, keepdims=True)
        da_ref[:, st:] = jnp.sum(acc_i, axis=0, keepdims=True)
        for c0 in range(0, rows, chunk):
            du_ref[pl.ds(c0, chunk), :] = (_dot_nt(ds_ref[pl.ds(c0, chunk), :].astype(BF16), wb_ref[...])
                                           + d_ref[...] * dy_ref[pl.ds(c0, chunk), :])
        dwb_ref[...] = _dot_tn(u_ref[...].astype(BF16), ds_ref[...].astype(BF16))

    col = pl.BlockSpec((rows, LANES), lambda j: (0, j))
    return pl.pallas_call(
        body, name="s5_bwd", grid=(nb,),
        in_specs=[col, pl.BlockSpec((rows, s2), lambda j: (0, j)), col, col, col,
                  pl.BlockSpec((None, LANES, s2), lambda j: (j, 0, 0)), pl.BlockSpec((None, s2, LANES), lambda j: (j, 0, 0)),
                  pl.BlockSpec((1, LANES), lambda j: (0, j)), pl.BlockSpec((None, 1, s2), lambda j: (j, 0, 0))],
        out_specs=[col, pl.BlockSpec((None, LANES, s2), lambda j: (j, 0, 0)),
                   pl.BlockSpec((None, s2, LANES), lambda j: (j, 0, 0)), pl.BlockSpec((None, 1, s2), lambda j: (j, 0, 0)),
                   pl.BlockSpec((1, LANES), lambda j: (0, j))],
        out_shape=[jax.ShapeDtypeStruct((rows, nb * LANES), F32), jax.ShapeDtypeStruct((nb, LANES, s2), F32),
                   jax.ShapeDtypeStruct((nb, s2, LANES), F32), jax.ShapeDtypeStruct((nb, 1, s2), F32),
                   jax.ShapeDtypeStruct((1, nb * LANES), F32)],
        scratch_shapes=[pltpu.VMEM((rows, s2), F32), pltpu.VMEM((rows, LANES), F32)],
        compiler_params=_params(("parallel",)),
    )(proj, states, y_pre, dyg_a, dyg_b, wb, wc, d_skip, abar)


def _glu_norm_fwd(y_pre, z, w, *, tr=256):
    rows, width = y_pre.shape
    tr = _tile(rows, tr, SUBLANES)

    def body(y_ref, z_ref, w_ref, o_ref):
        v = _gelu(y_ref[...]) * jax.nn.sigmoid(z_ref[...])
        o_ref[...] = (v * _rms_rows(v) * w_ref[...]).astype(o_ref.dtype)

    blk = pl.BlockSpec((tr, width), lambda i: (i, 0))
    return pl.pallas_call(
        body, name="glu_norm_fwd", grid=(rows // tr,),
        in_specs=[blk, blk, pl.BlockSpec((1, width), lambda i: (0, 0))], out_specs=blk,
        out_shape=jax.ShapeDtypeStruct((rows, width), BF16), compiler_params=_params(("parallel",)),
    )(y_pre, z, w)


def _glu_norm_bwd(y_pre, z, w, dycat, *, tr=256):
    rows, width = y_pre.shape
    tr = _tile(rows, tr, SUBLANES)

    def body(y_ref, z_ref, w_ref, dy_ref, dz_ref, dg_ref, dw_ref, db_ref):
        yg = _gelu(y_ref[...])
        sg = jax.nn.sigmoid(z_ref[...])
        dv, dwp = _rmsnorm_bwd_rows(yg * sg, w_ref[...], dy_ref[...])
        dz = dv * yg * sg * (1.0 - sg)
        dz_ref[...] = dz
        dg_ref[...] = dv * sg
        dw_part = jnp.sum(dwp, axis=0, keepdims=True)
        db_part = jnp.sum(dz, axis=0, keepdims=True)

        @pl.when(pl.program_id(0) == 0)
        def _():
            dw_ref[...] = dw_part
            db_ref[...] = db_part

        @pl.when(pl.program_id(0) > 0)
        def _():
            dw_ref[...] += dw_part
            db_ref[...] += db_part

    blk = pl.BlockSpec((tr, width), lambda i: (i, 0))
    vec = pl.BlockSpec((1, width), lambda i: (0, 0))
    return pl.pallas_call(
        body, name="glu_norm_bwd", grid=(rows // tr,), in_specs=[blk, blk, vec, blk], out_specs=[blk, blk, vec, vec],
        out_shape=[jax.ShapeDtypeStruct((rows, width), F32)] * 2 + [jax.ShapeDtypeStruct((1, width), F32)] * 2,
        compiler_params=_params(("arbitrary",)),
    )(y_pre, z, w, dycat)


def _rope_tables(pos, freq, sign):
    rows = pos.shape[0]

    def body(p_ref, f_ref, s_ref, cos_ref, sin_ref):
        ang = p_ref[...] * f_ref[...]
        cos_ref[...] = jnp.cos(ang)
        sin_ref[...] = jnp.sin(ang) * s_ref[...]

    return pl.pallas_call(body, name="rope_tables", out_shape=[jax.ShapeDtypeStruct((rows, LANES), F32)] * 2)(pos, freq, sign)


def _rope(x, cos, sin_signed):
    lane = lax.broadcasted_iota(jnp.int32, x.shape, 1)
    half = QK_ROPE_DIM // 2
    swapped = jnp.where(lane < half, pltpu.roll(x, LANES - half, 1), pltpu.roll(x, half, 1))
    return x * cos + swapped * sin_signed


def _attn_prep(q, kv, proj, kpe_col, cos, sin, *, tr=256):
    rows = q.shape[0]
    heads = q.shape[1] // HEAD_SLOT
    tr = _tile(rows, tr, SUBLANES)

    def body(q_ref, kv_ref, kpe_ref, cos_ref, sin_ref, qc_ref, kc_ref, v_ref):
        c, s = cos_ref[...], sin_ref[...]
        qc_ref[:, :LANES] = q_ref[:, :LANES].astype(BF16)
        qc_ref[:, LANES:] = _rope(q_ref[:, LANES:], c, s).astype(BF16)
        kc_ref[:, :LANES] = kv_ref[:, :LANES].astype(BF16)
        kc_ref[:, LANES:] = _rope(kpe_ref[...], c, s).astype(BF16)
        v_ref[...] = kv_ref[:, LANES:].astype(BF16)

    slot = pl.BlockSpec((tr, HEAD_SLOT), lambda i, h: (i, h))
    tab = pl.BlockSpec((tr, LANES), lambda i, h: (i, 0))
    return pl.pallas_call(
        body, name="attn_prep", grid=(rows // tr, heads),
        in_specs=[slot, slot, pl.BlockSpec((tr, LANES), lambda i, h: (i, kpe_col)), tab, tab],
        out_specs=[slot, slot, pl.BlockSpec((tr, LANES), lambda i, h: (i, h))],
        out_shape=[jax.ShapeDtypeStruct((rows, heads * HEAD_SLOT), BF16)] * 2
        + [jax.ShapeDtypeStruct((rows, heads * LANES), BF16)],
        compiler_params=_params(("parallel", "parallel")),
    )(q, kv, proj, cos, sin)


def _causal(i, j, tq, tk):
    qpos = i * tq + lax.broadcasted_iota(jnp.int32, (tq, tk), 0)
    kpos = j * tk + lax.broadcasted_iota(jnp.int32, (tq, tk), 1)
    return kpos <= qpos


def _attn_fwd(qc, kc, vb, *, scale, tq=512):
    rows = qc.shape[0]
    heads = qc.shape[1] // HEAD_SLOT
    tq = _tile(rows, tq, SUBLANES)
    tk = tq

    def body(q_ref, k_ref, v_ref, o_ref, lse_ref):
        i = pl.program_id(1)
        q = q_ref[...]

        def step(j, carry):
            m, l, acc = carry
            k0 = pl.multiple_of(j * tk, tk)
            s = _dot_nt(q, k_ref[pl.ds(k0, tk), :]) * scale
            s = jnp.where(_causal(i, j, tq, tk), s, NEG_INF)
            m_new = jnp.maximum(m, jnp.max(s, axis=-1, keepdims=True))
            p = jnp.exp(s - m_new)
            alpha = jnp.exp(m - m_new)
            l = alpha * l + jnp.sum(p, axis=-1, keepdims=True)
            acc = alpha * acc + _dot_nn(p.astype(BF16), v_ref[pl.ds(k0, tk), :])
            return m_new, l, acc

        init = (jnp.full((tq, 1), NEG_INF, F32), jnp.zeros((tq, 1), F32), jnp.zeros((tq, LANES), F32))
        m, l, acc = lax.fori_loop(0, i + 1, step, init)
        o_ref[...] = acc / l
        lse_ref[...] = jnp.broadcast_to(m + jnp.log(l), (tq, LANES))

    return pl.pallas_call(
        body, name="attn_fwd", grid=(heads, rows // tq),
        in_specs=[pl.BlockSpec((tq, HEAD_SLOT), lambda h, i: (i, h)), pl.BlockSpec((rows, HEAD_SLOT), lambda h, i: (0, h)),
                  pl.BlockSpec((rows, LANES), lambda h, i: (0, h))],
        out_specs=[pl.BlockSpec((tq, LANES), lambda h, i: (i, h))] * 2,
        out_shape=[jax.ShapeDtypeStruct((rows, heads * LANES), F32)] * 2,
        compiler_params=_params(("parallel", "parallel")),
    )(qc, kc, vb)


def _attn_bwd_q(qc, kc, vb, o, do, lse, cos, sin, *, scale, tq=512):
    rows = qc.shape[0]
    heads = qc.shape[1] // HEAD_SLOT
    tq = _tile(rows, tq, SUBLANES)
    tk = tq

    def body(q_ref, k_ref, v_ref, o_ref, do_ref, lse_ref, cos_ref, sin_ref, dq_ref, delta_ref):
        i = pl.program_id(1)
        q = q_ref[...]
        dov = do_ref[...]
        delta = jnp.sum(dov * o_ref[...], axis=-1, keepdims=True)
        delta_ref[...] = jnp.broadcast_to(delta, (tq, LANES))
        dob = dov.astype(BF16)
        lse_col = lse_ref[:, :1]

        def step(j, dq):
            k0 = pl.multiple_of(j * tk, tk)
            kb = k_ref[pl.ds(k0, tk), :]
            s = _dot_nt(q, kb) * scale
            p = jnp.where(_causal(i, j, tq, tk), jnp.exp(s - lse_col), 0.0)
            dp = _dot_nt(dob, v_ref[pl.ds(k0, tk), :])
            ds = p * (dp - delta)
            return dq + _dot_nn(ds.astype(BF16), kb)

        dq = lax.fori_loop(0, i + 1, step, jnp.zeros((tq, HEAD_SLOT), F32)) * scale
        dq_ref[:, :LANES] = dq[:, :LANES]
        dq_ref[:, LANES:] = _rope(dq[:, LANES:], cos_ref[...], -sin_ref[...])

    qblk = pl.BlockSpec((tq, HEAD_SLOT), lambda h, i: (i, h))
    vblk = pl.BlockSpec((tq, LANES), lambda h, i: (i, h))
    tab = pl.BlockSpec((tq, LANES), lambda h, i: (i, 0))
    return pl.pallas_call(
        body, name="attn_bwd_q", grid=(heads, rows // tq),
        in_specs=[qblk, pl.BlockSpec((rows, HEAD_SLOT), lambda h, i: (0, h)), pl.BlockSpec((rows, LANES), lambda h, i: (0, h)),
                  vblk, vblk, vblk, tab, tab],
        out_specs=[qblk, vblk],
        out_shape=[jax.ShapeDtypeStruct((rows, heads * HEAD_SLOT), F32), jax.ShapeDtypeStruct((rows, heads * LANES), F32)],
        compiler_params=_params(("parallel", "parallel")),
    )(qc, kc, vb, o, do, lse, cos, sin)


def _attn_bwd_kv(qc, kc, vb, do, lse, delta, cos, sin, *, scale, tk=512):
    rows = qc.shape[0]
    heads = qc.shape[1] // HEAD_SLOT
    tk = _tile(rows, tk, SUBLANES)
    tq = tk
    nq = rows // tq

    def body(q_ref, k_ref, v_ref, do_ref, lse_ref, delta_ref, cos_ref, sin_ref, dkv_ref, dkpe_ref):
        j, h = pl.program_id(0), pl.program_id(1)
        kb, vv = k_ref[...], v_ref[...]

        def step(i, carry):
            dk, dv = carry
            q0 = pl.multiple_of(i * tq, tq)
            qb = q_ref[pl.ds(q0, tq), :]
            dob = do_ref[pl.ds(q0, tq), :].astype(BF16)
            s = _dot_nt(qb, kb) * scale
            p = jnp.where(_causal(i, j, tq, tk), jnp.exp(s - lse_ref[pl.ds(q0, tq), :1]), 0.0)
            dv = dv + _dot_tn(p.astype(BF16), dob)
            ds = p * (_dot_nt(dob, vv) - delta_ref[pl.ds(q0, tq), :1])
            dk = dk + _dot_tn(ds.astype(BF16), qb)
            return dk, dv

        dk, dv = lax.fori_loop(j, nq, step, (jnp.zeros((tk, HEAD_SLOT), F32), jnp.zeros((tk, LANES), F32)))
        dkv_ref[:, :LANES] = dk[:, :LANES] * scale
        dkv_ref[:, LANES:] = dv
        part = dk[:, LANES:] * scale

        @pl.when(h == 0)
        def _():
            dkpe_ref[...] = part

        @pl.when(h > 0)
        def _():
            dkpe_ref[...] += part

        @pl.when(h == heads - 1)
        def _():
            dkpe_ref[...] = _rope(dkpe_ref[...], cos_ref[...], -sin_ref[...])

    full_q = pl.BlockSpec((rows, HEAD_SLOT), lambda j, h: (0, h))
    full_v = pl.BlockSpec((rows, LANES), lambda j, h: (0, h))
    tab = pl.BlockSpec((tk, LANES), lambda j, h: (j, 0))
    return pl.pallas_call(
        body, name="attn_bwd_kv", grid=(rows // tk, heads),
        in_specs=[full_q, pl.BlockSpec((tk, HEAD_SLOT), lambda j, h: (j, h)), pl.BlockSpec((tk, LANES), lambda j, h: (j, h)),
                  full_v, full_v, full_v, tab, tab],
        out_specs=[pl.BlockSpec((tk, HEAD_SLOT), lambda j, h: (j, h)), pl.BlockSpec((tk, LANES), lambda j, h: (j, 0))],
        out_shape=[jax.ShapeDtypeStruct((rows, heads * HEAD_SLOT), F32), jax.ShapeDtypeStruct((rows, LANES), F32)],
        compiler_params=_params(("parallel", "arbitrary")),
    )(qc, kc, vb, do, lse, delta, cos, sin)


def _shift_down(x, d):
    row = lax.broadcasted_iota(jnp.int32, x.shape, 0)
    return jnp.where(row >= d, pltpu.roll(x, d, 0), 0.0)


def _shift_up(x, d):
    rows = x.shape[0]
    row = lax.broadcasted_iota(jnp.int32, x.shape, 0)
    return jnp.where(row < rows - d, pltpu.roll(x, rows - d, 0), 0.0)


def _conv3(a, w, b):
    return w[2:3, :] * a + w[1:2, :] * _shift_down(a, 1) + w[0:1, :] * _shift_down(a, 2) + b


def _conv_gate_fwd(a, conv_w, conv_b, *, tc=256):
    rows, f2 = a.shape
    f = f2 // 2
    tc = _tile(f, tc)
    nc = f // tc

    def body(ag_ref, av_ref, wg_ref, wv_ref, bg_ref, bv_ref, o_ref):
        gate = _conv3(ag_ref[...], wg_ref[...], bg_ref[...])
        val = _conv3(av_ref[...], wv_ref[...], bv_ref[...])
        o_ref[...] = (gate * jax.nn.sigmoid(gate) * val).astype(o_ref.dtype)

    return pl.pallas_call(
        body, name="conv_gate_fwd", grid=(nc,),
        in_specs=[pl.BlockSpec((rows, tc), lambda j: (0, j)), pl.BlockSpec((rows, tc), lambda j: (0, j + nc)),
                  pl.BlockSpec((SUBLANES, tc), lambda j: (0, j)), pl.BlockSpec((SUBLANES, tc), lambda j: (0, j + nc)),
                  pl.BlockSpec((1, tc), lambda j: (0, j)), pl.BlockSpec((1, tc), lambda j: (0, j + nc))],
        out_specs=pl.BlockSpec((rows, tc), lambda j: (0, j)),
        out_shape=jax.ShapeDtypeStruct((rows, f), BF16), compiler_params=_params(("parallel",)),
    )(a, a, conv_w, conv_w, conv_b, conv_b)


def _conv_gate_bwd(a, conv_w, conv_b, dg, *, tc=256):
    rows, f2 = a.shape
    f = f2 // 2
    tc = _tile(f, tc)
    nc = f // tc

    def conv_bwd(a_val, w, d_out):
        da = w[2:3, :] * d_out + w[1:2, :] * _shift_up(d_out, 1) + w[0:1, :] * _shift_up(d_out, 2)
        db = jnp.sum(d_out, axis=0, keepdims=True)
        row = lax.broadcasted_iota(jnp.int32, (SUBLANES, a_val.shape[1]), 0)
        dw = jnp.zeros((SUBLANES, a_val.shape[1]), F32)
        for tap in range(3):
            t = jnp.sum(d_out * (_shift_down(a_val, 2 - tap) if tap < 2 else a_val), axis=0, keepdims=True)
            dw = jnp.where(row == tap, t, dw)
        return da, dw, db

    def body(ag_ref, av_ref, wg_ref, wv_ref, bg_ref, bv_ref, dg_ref, da_ref, dw_ref, db_ref):
        ag, av, wg, wv = ag_ref[...], av_ref[...], wg_ref[...], wv_ref[...]
        gate = _conv3(ag, wg, bg_ref[...])
        val = _conv3(av, wv, bv_ref[...])
        sg = jax.nn.sigmoid(gate)
        dgv = dg_ref[...]
        d_gate = dgv * val * sg * (1.0 + gate * (1.0 - sg))
        d_val = dgv * gate * sg
        for half, (a_val, w, d_out) in enumerate(((ag, wg, d_gate), (av, wv, d_val))):
            da, dw, db = conv_bwd(a_val, w, d_out)
            da_ref[half] = da.astype(da_ref.dtype)
            dw_ref[half] = dw
            db_ref[half] = db

    lo = lambda j: (0, j)
    hi = lambda j: (0, j + nc)
    both = lambda j: (0, 0, j)
    return pl.pallas_call(
        body, name="conv_gate_bwd", grid=(nc,),
        in_specs=[pl.BlockSpec((rows, tc), lo), pl.BlockSpec((rows, tc), hi), pl.BlockSpec((SUBLANES, tc), lo),
                  pl.BlockSpec((SUBLANES, tc), hi), pl.BlockSpec((1, tc), lo), pl.BlockSpec((1, tc), hi),
                  pl.BlockSpec((rows, tc), lo)],
        out_specs=[pl.BlockSpec((2, rows, tc), both), pl.BlockSpec((2, SUBLANES, tc), both), pl.BlockSpec((2, 1, tc), both)],
        out_shape=[jax.ShapeDtypeStruct((2, rows, f), BF16), jax.ShapeDtypeStruct((2, SUBLANES, f), F32),
                   jax.ShapeDtypeStruct((2, 1, f), F32)],
        compiler_params=_params(("parallel",)),
    )(a, a, conv_w, conv_w, conv_b, conv_b, dg)


def _wgrad(a, b, rows, cols, row_sharded, name, **kw):
    make, (sr, sc) = _wgrad_blocks(rows, cols, row_sharded)
    tm = kw.pop("tm", _tile(sr, 512))
    tn = kw.pop("tn", _tile(sc, 1024))
    return _matmul(a, b, mode="tn", name=name, tm=tm, tn=tn, out_blocks=make, **kw)


def _block_diag(x):
    nb, g, r, c = x.shape
    eye = jnp.eye(g, dtype=x.dtype)
    return (x[:, :, :, None, :] * eye[None, :, None, :, None]).reshape(nb, g * r, g * c)


def _block_diag_part(x, r, c):
    nb = x.shape[0]
    g = GROUPS_PER_BATCH
    eye = jnp.eye(g, dtype=x.dtype)
    return jnp.sum(x.reshape(nb, g, r, g, c) * eye[None, :, None, :, None], axis=3)


class _NoExchange:
    def __init__(self, ffn):
        self.ffn = ffn

    def ffn_weights(self, after):
        return self.ffn

    def ffn_grads(self, g_down, g_up):
        return None

    def ffn_backward_done(self, after):
        pass


def _local_step(x, posf, target, w, hooks):
    rows, d = x.shape
    width = w["ssm_d"].shape[1]
    qr, kvr = w["mla_q_norm_w"].shape[1], w["mla_kv_norm_w"].shape[1]
    heads = w["mla_w_ukv"].shape[1] // HEAD_SLOT
    f2 = w["ffn_conv_b"].shape[1]
    inp = w["w_in"].shape[1]
    groups = width // SSM_GROUP
    nb = groups // GROUPS_PER_BATCH
    scale = (QK_NOPE_DIM + QK_ROPE_DIM) ** -0.5
    g = {}

    hn = _rmsnorm_fwd(x, w["attn_norm_w"], name="attn_norm")
    proj = _matmul(hn, w["w_in"], mode="nn", name="in_proj")

    ar, ai, bbr, bbi = _s5_params(w["ssm_lambda_re"], w["ssm_lambda_im"], w["ssm_log_dt"], w["ssm_b_re"], w["ssm_b_im"])

    def b_band(bb):
        return _block_diag(bb.reshape(nb, GROUPS_PER_BATCH, SSM_STATE, SSM_GROUP).transpose(0, 1, 3, 2))

    def c_band(c):
        return _block_diag(c.reshape(nb, GROUPS_PER_BATCH, SSM_GROUP, SSM_STATE).transpose(0, 1, 3, 2))

    wb = jnp.concatenate([b_band(bbr), b_band(bbi)], axis=2).astype(BF16)
    wc = jnp.concatenate([c_band(w["ssm_c_re"]), -c_band(w["ssm_c_im"])], axis=1).astype(BF16)
    abar = jnp.concatenate([ar.reshape(nb, 1, STATE_PER_BATCH), ai.reshape(nb, 1, STATE_PER_BATCH)], axis=2)
    states, y_pre, yg = _s5_fwd(proj, wb, wc, w["ssm_d"], abar)
    z = _matmul(yg, w["ssm_w_glu"], mode="nn", name="glu_proj", bias=w["ssm_b_glu"])
    ys = _glu_norm_fwd(y_pre, z, w["ssm_out_norm_w"])

    q_col, kv_col, kpe_col = width // qr, (width + qr) // kvr, (width + qr + kvr) // LANES
    assert width % qr == 0 and (width + qr) % kvr == 0
    qn = _rmsnorm_fwd(proj, w["mla_q_norm_w"], name="q_norm", width=qr, col=q_col)
    kvn = _rmsnorm_fwd(proj, w["mla_kv_norm_w"], name="kv_norm", width=kvr, col=kv_col)
    q = _matmul(qn, w["mla_w_uq"], mode="nn", name="q_proj")
    kv = _matmul(kvn, w["mla_w_ukv"], mode="nn", name="kv_proj")
    half = QK_ROPE_DIM // 2
    inv_freq = ROPE_THETA ** (-jnp.arange(0, QK_ROPE_DIM, 2, dtype=F32) / QK_ROPE_DIM)
    zeros = jnp.zeros((LANES - QK_ROPE_DIM,), F32)
    freq = jnp.concatenate([inv_freq, inv_freq, zeros]).reshape(1, LANES)
    sign = jnp.concatenate([-jnp.ones((half,), F32), jnp.ones((half,), F32), zeros]).reshape(1, LANES)
    cos, sin = _rope_tables(posf, freq, sign)
    qc, kc, vb = _attn_prep(q, kv, proj, kpe_col, cos, sin)
    o, lse = _attn_fwd(qc, kc, vb, scale=scale, tq=ATTN_BLOCK)
    ym = _rmsnorm_fwd(o, w["mla_out_norm_w"], name="mla_out_norm")
    ycat = jnp.concatenate([ys, ym], axis=1)
    h1 = _matmul(ycat, w["w_out"], mode="nn", name="out_proj", add=x)

    hn2 = _rmsnorm_fwd(h1, w["ffn_norm_w"], name="ffn_norm")
    ffn = hooks.ffn_weights(hn2)
    a = _matmul(hn2, ffn["ffn_w_up"], mode="nn", name="ffn_up")
    gated = _conv_gate_fwd(a, ffn["ffn_conv_w"], w["ffn_conv_b"])
    h2 = _matmul(gated, ffn["ffn_w_down"], mode="nn", name="ffn_down", add=h1, tk=2816)
    loss_tile, dh2, g["final_norm_w"] = _final_norm_loss(h2, w["final_norm_w"], target)

    dgated = _matmul(dh2, ffn["ffn_w_down"], mode="nt", name="ffn_down_dx")
    g["ffn_w_down"] = _wgrad(gated, dh2, f2 // 2, d, True, "ffn_down_dw", tm=f2 // 2 // N_CHIPS, tn=512)
    da, dcw, dcb = _conv_gate_bwd(a, ffn["ffn_conv_w"], w["ffn_conv_b"], dgated)
    g["ffn_conv_w"] = jnp.concatenate([dcw[0, :3], dcw[1, :3]], axis=1)
    g["ffn_conv_b"] = jnp.concatenate([dcb[0], dcb[1]], axis=1)
    g["ffn_w_up"] = _wgrad(hn2, da, d, f2, False, "ffn_up_dw", b_split=True)
    started = hooks.ffn_grads(g["ffn_w_down"], g["ffn_w_up"])
    ffn_norm_w = w["ffn_norm_w"] if started is None else w["ffn_norm_w"] + started[:1, :1]
    dhn2 = _matmul(da, ffn["ffn_w_up"], mode="nt", name="ffn_up_dx", a_split=True, tk=_tile(f2 // 2, 2816))
    dh1, g["ffn_norm_w"] = _rmsnorm_bwd(h1, ffn_norm_w, dhn2, name="ffn_norm_bwd", add=dh2)

    dycat = _matmul(dh1, w["w_out"], mode="nt", name="out_proj_dx")
    g["w_out"] = _wgrad(ycat, dh1, 2 * width, d, True, "out_proj_dw")
    hooks.ffn_backward_done(dycat)

    do, g["mla_out_norm_w"] = _rmsnorm_bwd(o, w["mla_out_norm_w"], dycat, name="mla_out_norm_bwd", width=width, dy_col=1)
    dq, delta = _attn_bwd_q(qc, kc, vb, o, do, lse, cos, sin, scale=scale, tq=ATTN_BLOCK)
    dkv, dkpe = _attn_bwd_kv(qc, kc, vb, do, lse, delta, cos, sin, scale=scale, tk=ATTN_BLOCK)
    g["mla_w_uq"] = _wgrad(qn, dq, qr, heads * HEAD_SLOT, False, "q_proj_dw")
    dqn = _matmul(dq, w["mla_w_uq"], mode="nt", name="q_proj_dx")
    dcq, g["mla_q_norm_w"] = _rmsnorm_bwd(proj, w["mla_q_norm_w"], dqn, name="q_norm_bwd", width=qr, col=q_col)
    g["mla_w_ukv"] = _wgrad(kvn, dkv, kvr, heads * HEAD_SLOT, False, "kv_proj_dw")
    dkvn = _matmul(dkv, w["mla_w_ukv"], mode="nt", name="kv_proj_dx")
    dckv, g["mla_kv_norm_w"] = _rmsnorm_bwd(proj, w["mla_kv_norm_w"], dkvn, name="kv_norm_bwd", width=kvr, col=kv_col)

    dz, dyg_a, g["ssm_out_norm_w"], g["ssm_b_glu"] = _glu_norm_bwd(y_pre, z, w["ssm_out_norm_w"], dycat)
    dyg_b = _matmul(dz, w["ssm_w_glu"], mode="nt", name="glu_proj_dx")
    g["ssm_w_glu"] = _wgrad(yg, dz, width, width, True, "glu_proj_dw")
    du, dwb, dwc, dabar, g["ssm_d"] = _s5_bwd(proj, states, y_pre, dyg_a, dyg_b, wb, wc, w["ssm_d"], abar)

    def b_unband(x):
        return _block_diag_part(x, SSM_GROUP, SSM_STATE).transpose(0, 1, 3, 2).reshape(groups, SSM_STATE * SSM_GROUP)

    def c_unband(x):
        return _block_diag_part(x, SSM_STATE, SSM_GROUP).transpose(0, 1, 3, 2).reshape(groups, SSM_GROUP, SSM_STATE)

    st = STATE_PER_BATCH
    g["ssm_c_re"] = c_unband(dwc[:, :st, :])
    g["ssm_c_im"] = -c_unband(dwc[:, st:, :])
    d_ar = dabar[:, 0, :st].reshape(groups, SSM_STATE)
    d_ai = dabar[:, 0, st:].reshape(groups, SSM_STATE)
    (g["ssm_lambda_re"], g["ssm_lambda_im"], g["ssm_log_dt"], g["ssm_b_re"], g["ssm_b_im"]) = _s5_params_bwd(
        w["ssm_lambda_re"], w["ssm_lambda_im"], w["ssm_log_dt"], w["ssm_b_re"], w["ssm_b_im"], d_ar, d_ai,
        b_unband(dwb[:, :, :st]), b_unband(dwb[:, :, st:]))

    pad = jnp.zeros((rows, inp - (width + qr + kvr + LANES)), F32)
    dproj = jnp.concatenate([du, dcq, dckv, dkpe, pad], axis=1)
    g["w_in"] = _wgrad(hn, dproj, d, inp, True, "in_proj_dw")
    dhn = _matmul(dproj, w["w_in"], mode="nt", name="in_proj_dx")
    dx, g["attn_norm_w"] = _rmsnorm_bwd(x, w["attn_norm_w"], dhn, name="attn_norm_bwd", add=dh1)
    return loss_tile, dx, g


ANY = pl.BlockSpec(memory_space=pl.ANY)
MESH = pl.DeviceIdType.MESH


def _mesh_pos():
    return lax.axis_index("x"), lax.axis_index("y"), lax.axis_index("c")


def _other_chips(x, y):
    return [(1 - x, y), (x, 1 - y), (1 - x, 1 - y)]


def _remote(src, dst, send_sems, recv_sems, k, to):
    return pltpu.make_async_remote_copy(src_ref=src, dst_ref=dst, send_sem=send_sems.at[k], recv_sem=recv_sems.at[k],
                                        device_id=to, device_id_type=MESH)


def _place_shard(shard, piece_idx, row_sharded, name, out_dtype=BF16, pieces=N_CHIPS):
    rs, cs = shard.shape
    tr = _tile(rs, 256, 2 * SUBLANES)
    rb = rs // tr

    def body(p_ref, x_ref, o_ref):
        o_ref[...] = x_ref[...].astype(o_ref.dtype)

    if row_sharded:
        out_shape, out_map = (pieces * rs, cs), (lambda i, p_ref: (p_ref[0] * rb + i, 0))
    else:
        out_shape, out_map = (rs, pieces * cs), (lambda i, p_ref: (i, p_ref[0]))
    return pl.pallas_call(
        body, name=name, out_shape=jax.ShapeDtypeStruct(out_shape, out_dtype),
        grid_spec=pltpu.PrefetchScalarGridSpec(
            num_scalar_prefetch=1, grid=(rb,), in_specs=[pl.BlockSpec((tr, cs), lambda i, p_ref: (i, 0))],
            out_specs=pl.BlockSpec((tr, cs), out_map)),
        compiler_params=_params(("parallel",)),
    )(piece_idx, shard)


def _gather_weights(placed, name):
    n = len(placed)
    meta = [(row_sharded, direct) for _, row_sharded, direct in placed]
    over_ici, over_d2d = _gather_plans(meta)
    forwarded = [t for t, (_, direct) in enumerate(meta) if not direct]

    def body(*refs):
        outs = refs[n:2 * n]
        send_sems, recv_sems, pass_send_sems, pass_recv_sems = refs[2 * n:]
        first, arrivals = over_ici(outs, send_sems, recv_sems)
        passed, passed_arrivals = over_d2d([outs[t] for t in forwarded], pass_send_sems, pass_recv_sems)
        for cp in first:
            cp.start()
        for t in range(n):
            for j in range(3):
                arrivals[3 * t + j].wait_recv()
                if t in forwarded:
                    passed[3 * forwarded.index(t) + j].start()
        for cp in passed_arrivals:
            cp.wait_recv()
        for cp in first + passed:
            cp.wait_send()

    return pl.pallas_call(
        body, name=name, in_specs=[ANY] * n, out_specs=[ANY] * n,
        out_shape=[jax.ShapeDtypeStruct(arr.shape, arr.dtype) for arr, _, _ in placed],
        input_output_aliases={t: t for t in range(n)},
        scratch_shapes=[pltpu.SemaphoreType.DMA((3 * n,)), pltpu.SemaphoreType.DMA((3 * n,)),
                        pltpu.SemaphoreType.DMA((3 * len(forwarded),)), pltpu.SemaphoreType.DMA((3 * len(forwarded),))],
    )(*[arr for arr, _, _ in placed])


def _gather_plans(meta):
    def window(ref, row_sharded, piece, half):
        r, cc = ref.shape
        if row_sharded:
            rs = r // N_CHIPS
            if half is None:
                return ref.at[pl.ds(piece * rs, rs), :]
            return ref.at[pl.ds(piece * rs + half * (rs // 2), rs // 2), :]
        cs = cc // N_CHIPS
        if half is None:
            return ref.at[:, pl.ds(piece * cs, cs)]
        return ref.at[pl.ds(half * (r // 2), r // 2), pl.ds(piece * cs, cs)]

    def over_ici(refs, send_sems, recv_sems):
        x, y, c = _mesh_pos()
        sends, recvs = [], []
        for t, (row_sharded, direct) in enumerate(meta):
            mine = window(refs[t], row_sharded, 2 * x + y, None if direct else c)
            for j, (px, py) in enumerate(_other_chips(x, y)):
                theirs = window(refs[t], row_sharded, 2 * px + py, None if direct else c)
                sends.append(_remote(mine, mine, send_sems, recv_sems, 3 * t + j, (px, py, c)))
                recvs.append(_remote(theirs, theirs, send_sems, recv_sems, 3 * t + j, (px, py, c)))
        return sends, recvs

    def over_d2d(refs, send_sems, recv_sems):
        x, y, c = _mesh_pos()
        sends, recvs = [], []
        rows = [row_sharded for row_sharded, direct in meta if not direct]
        for t, row_sharded in enumerate(rows):
            for j, (px, py) in enumerate(_other_chips(x, y)):
                got = window(refs[t], row_sharded, 2 * px + py, c)
                other = window(refs[t], row_sharded, 2 * px + py, 1 - c)
                sends.append(_remote(got, got, send_sems, recv_sems, 3 * t + j, (x, y, 1 - c)))
                recvs.append(_remote(other, other, send_sems, recv_sems, 3 * t + j, (x, y, 1 - c)))
        return sends, recvs

    return over_ici, over_d2d


HBM = pl.BlockSpec(memory_space=pltpu.HBM)
SEMAPHORES = pl.BlockSpec(memory_space=pltpu.SEMAPHORE)
DATAFLOW = pltpu.SideEffectType.DATAFLOW_SIDE_EFFECTING


def _start_copies(name, arrays, plan, n_copies):
    n = len(arrays)

    def body(*refs):
        sends, _ = plan(refs[:n], refs[n], refs[n + 1])
        for cp in sends:
            cp.start()
        token = refs[2 * n + 2]
        token[...] = jnp.zeros_like(token)

    out = pl.pallas_call(
        body, name=name,
        out_shape=(pltpu.SemaphoreType.DMA((n_copies,)), pltpu.SemaphoreType.DMA((n_copies,)),
                   *[pltpu.HBM(a.shape, a.dtype) for a in arrays], jax.ShapeDtypeStruct((SUBLANES, LANES), F32)),
        in_specs=[HBM] * n, out_specs=(SEMAPHORES, SEMAPHORES, *[HBM] * n, pl.BlockSpec(memory_space=pltpu.VMEM)),
        input_output_aliases={t: t + 2 for t in range(n)},
        compiler_params=pltpu.CompilerParams(has_side_effects=DATAFLOW),
    )(*[pltpu.with_memory_space_constraint(a, pltpu.HBM) for a in arrays])
    return out[0], out[1], list(out[2:2 + n]), out[2 + n]


def _wait_copies(name, started, plan, after):
    send_sems, recv_sems, arrays, _ = started
    n = len(arrays)

    def body(*refs):
        sends, recvs = plan(refs[:n], refs[n], refs[n + 1])
        for cp in sends:
            cp.wait_send()
        for cp in recvs:
            cp.wait_recv()

    out = pl.pallas_call(
        body, name=name, out_shape=[pltpu.HBM(a.shape, a.dtype) for a in arrays],
        in_specs=[HBM] * n + [SEMAPHORES, SEMAPHORES, ANY], out_specs=[HBM] * n,
        input_output_aliases={t: t for t in range(n)},
        compiler_params=pltpu.CompilerParams(has_side_effects=DATAFLOW),
    )(*arrays, send_sems, recv_sems, after)
    return list(out)


def _exchange(name, arrays, out_shapes, plan, n_copies, in_place=False):
    n = len(arrays)

    def body(*refs):
        ins, outs = refs[:n], refs[n:n + len(out_shapes)]
        send_sems, recv_sems = refs[n + len(out_shapes):]
        sends, recvs = plan(ins, outs, send_sems, recv_sems)
        for cp in sends:
            cp.start()
        for cp in recvs:
            cp.wait_recv()
        for cp in sends:
            cp.wait_send()

    return pl.pallas_call(
        body, name=name, in_specs=[ANY] * n, out_specs=[ANY] * len(out_shapes), out_shape=out_shapes,
        input_output_aliases={t: t for t in range(n)} if in_place else {},
        scratch_shapes=[pltpu.SemaphoreType.DMA((n_copies,)), pltpu.SemaphoreType.DMA((n_copies,))],
    )(*arrays)


def _swap_plan(n):
    def plan(refs, send_sems, recv_sems):
        x, y, c = _mesh_pos()
        sends = [_remote(refs[t].at[1 - c], refs[n + t], send_sems, recv_sems, t, (x, y, 1 - c)) for t in range(n)]
        return sends, sends

    return plan


def _scatter_plan(n):
    def plan(refs, send_sems, recv_sems):
        x, y, c = _mesh_pos()
        sends = []
        for t in range(n):
            for j, (px, py) in enumerate(_other_chips(x, y)):
                sends.append(_remote(refs[t].at[2 * px + py], refs[n + t].at[j], send_sems, recv_sems, 3 * t + j, (px, py, c)))
        return sends, sends

    return plan


def _swap_shapes(grads):
    return [jax.ShapeDtypeStruct(g.shape[1:], g.dtype) for g in grads]


def _scatter_shapes(sums):
    return [jax.ShapeDtypeStruct((3,) + s.shape[1:], s.dtype) for s in sums]


def _swap_other_half(grads, name):
    plan = _swap_plan(len(grads))
    return _exchange(name, grads, _swap_shapes(grads), lambda ins, outs, s, r: plan(list(ins) + list(outs), s, r), len(grads))


def _scatter_pieces(sums, name):
    plan = _scatter_plan(len(sums))
    return _exchange(name, sums, _scatter_shapes(sums), lambda ins, outs, s, r: plan(list(ins) + list(outs), s, r),
                     3 * len(sums))


def _join_halves(halves):
    def plan(ins, outs, send_sems, recv_sems):
        x, y, c = _mesh_pos()
        sends = [_remote(outs[t].at[c], outs[t].at[c], send_sems, recv_sems, t, (x, y, 1 - c)) for t in range(len(ins))]
        recvs = [_remote(outs[t].at[1 - c], outs[t].at[1 - c], send_sems, recv_sems, t, (x, y, 1 - c))
                 for t in range(len(ins))]
        return sends, recvs

    shapes = [jax.ShapeDtypeStruct(h.shape, h.dtype) for h in halves]
    return _exchange("grad_join_halves", halves, shapes, plan, len(halves), in_place=True)


def _all_to_all_small(slots):
    def plan(ins, outs, send_sems, recv_sems):
        x, y, c = _mesh_pos()
        mine = outs[0].at[4 * x + 2 * y + c]
        sends, recvs = [], []
        for mask in range(1, 8):
            px, py, pc = x ^ ((mask >> 2) & 1), y ^ ((mask >> 1) & 1), c ^ (mask & 1)
            theirs = outs[0].at[4 * px + 2 * py + pc]
            sends.append(_remote(mine, mine, send_sems, recv_sems, mask - 1, (px, py, pc)))
            recvs.append(_remote(theirs, theirs, send_sems, recv_sems, mask - 1, (px, py, pc)))
        return sends, recvs

    shape = jax.ShapeDtypeStruct(slots.shape, slots.dtype)
    return _exchange("small_grads_all_to_all", [slots], [shape], plan, 7, in_place=True)[0]


def _add_other_half(g4, got, where, name):
    _, pieces, sr, sc = g4.shape
    tr = _tile(sr, 256, 2 * SUBLANES)

    def body(w_ref, a_ref, b_ref, o_ref):
        o_ref[...] = (a_ref[...] + b_ref[...]).astype(o_ref.dtype)

    blk = pl.BlockSpec((None, tr, sc), lambda p, i, w_ref: (p, i, 0))
    return pl.pallas_call(
        body, name=name, out_shape=jax.ShapeDtypeStruct((pieces, sr, sc), BF16),
        grid_spec=pltpu.PrefetchScalarGridSpec(
            num_scalar_prefetch=1, grid=(pieces, sr // tr),
            in_specs=[pl.BlockSpec((None, None, tr, sc), lambda p, i, w_ref: (w_ref[0], p, i, 0)), blk], out_specs=blk),
        compiler_params=_params(("parallel", "parallel")),
    )(where, g4, got)


def _add_pieces(g4, got_half, got_pieces, where, name):
    _, _, sr, sc = g4.shape
    tr = _tile(sr, 256, 2 * SUBLANES)

    def body(w_ref, a_ref, b_ref, r_ref, o_ref):
        acc = a_ref[...] + b_ref[...]
        for j in range(3):
            acc = acc + r_ref[j].astype(F32)
        o_ref[...] = acc

    return pl.pallas_call(
        body, name=name, out_shape=jax.ShapeDtypeStruct((N_CORES, sr, sc), F32),
        grid_spec=pltpu.PrefetchScalarGridSpec(
            num_scalar_prefetch=1, grid=(sr // tr,),
            in_specs=[pl.BlockSpec((None, None, tr, sc), lambda i, w_ref: (w_ref[0], w_ref[1], i, 0)),
                      pl.BlockSpec((None, tr, sc), lambda i, w_ref: (w_ref[1], i, 0)),
                      pl.BlockSpec((3, tr, sc), lambda i, w_ref: (0, i, 0))],
            out_specs=pl.BlockSpec((None, tr, sc), lambda i, w_ref: (w_ref[0], i, 0))),
        compiler_params=_params(("parallel",)),
    )(where, g4, got_half, got_pieces)


def _sum_slots(slots):
    n, rows, lanes = slots.shape
    tr = _tile(rows, 512, SUBLANES)

    def body(s_ref, o_ref):
        acc = s_ref[0]
        for k in range(1, n):
            acc = acc + s_ref[k]
        o_ref[...] = acc

    return pl.pallas_call(
        body, name="small_grads_sum", grid=(rows // tr,),
        in_specs=[pl.BlockSpec((n, tr, lanes), lambda i: (0, i, 0))], out_specs=pl.BlockSpec((tr, lanes), lambda i: (i, 0)),
        out_shape=jax.ShapeDtypeStruct((rows, lanes), F32), compiler_params=_params(("parallel",)),
    )(slots)


def _adamw(w, g, m, v, name):
    rows, cols = w.shape
    tr = _tile(rows, max(SUBLANES, (1 << 19) // max(cols, 1) // SUBLANES * SUBLANES), SUBLANES)

    def body(w_ref, g_ref, m_ref, v_ref, d_ref, nm_ref, nv_ref):
        gv = g_ref[...]
        nm = ADAM_B1 * m_ref[...] + (1.0 - ADAM_B1) * gv
        nv = ADAM_B2 * v_ref[...] + (1.0 - ADAM_B2) * (gv * gv)
        m_hat = nm / (1.0 - ADAM_B1 ** ADAM_STEP)
        v_hat = nv / (1.0 - ADAM_B2 ** ADAM_STEP)
        d_ref[...] = -ADAM_LR * (m_hat / (jnp.sqrt(v_hat) + ADAM_EPS) + ADAM_WD * w_ref[...])
        nm_ref[...] = nm
        nv_ref[...] = nv

    blk = pl.BlockSpec((tr, cols), lambda i: (i, 0))
    return pl.pallas_call(
        body, name=name, grid=(rows // tr,), in_specs=[blk] * 4, out_specs=[blk] * 3,
        out_shape=[jax.ShapeDtypeStruct((rows, cols), F32)] * 3, compiler_params=_params(("parallel",)),
    )(w, g, m, v)


WEIGHTS = ['attn_norm_w', 'w_in', 'ssm_lambda_re', 'ssm_lambda_im', 'ssm_log_dt', 'ssm_b_re', 'ssm_b_im', 'ssm_c_re',
           'ssm_c_im', 'ssm_d', 'ssm_w_glu', 'ssm_b_glu', 'mla_q_norm_w', 'mla_w_uq', 'mla_kv_norm_w', 'mla_w_ukv',
           'ssm_out_norm_w', 'mla_out_norm_w', 'w_out', 'ffn_norm_w', 'ffn_w_up', 'ffn_conv_w', 'ffn_conv_b',
           'ffn_w_down', 'final_norm_w']
SHARDED = {'w_in': True, 'ssm_w_glu': True, 'mla_w_uq': False, 'mla_w_ukv': False, 'w_out': True, 'ffn_w_up': False,
           'ffn_w_down': True}
SMALL = [n for n in WEIGHTS if n not in SHARDED and n != 'ffn_conv_w']
ROPE_PAD = HEAD_SLOT - QK_NOPE_DIM - QK_ROPE_DIM


def _pad_heads(w_uq, heads):
    qr = w_uq.shape[0]
    w3 = w_uq.reshape(qr, heads, QK_NOPE_DIM + QK_ROPE_DIM)
    return jnp.concatenate([w3, jnp.zeros((qr, heads, ROPE_PAD), w_uq.dtype)], axis=2).reshape(qr, heads * HEAD_SLOT)


def _unpad_heads(g_uq, heads):
    qr = g_uq.shape[0]
    return g_uq.reshape(qr, heads, HEAD_SLOT)[:, :, :QK_NOPE_DIM + QK_ROPE_DIM].reshape(qr, -1)


FFN = ['ffn_w_up', 'ffn_w_down']
FFN_GATHER = FFN + ['ffn_conv_w']
FFN_GATHER_META = [(SHARDED[n], False) for n in FFN] + [(False, True)]


class _Overlapped:
    def __init__(self, placed, where):
        self.where = where
        self.over_ici, self.over_d2d = _gather_plans(FFN_GATHER_META)
        self.gather = _start_copies("gather_ffn_start", placed, self.over_ici, 3 * len(placed))

    def ffn_weights(self, after):
        arrived = _wait_copies("gather_ffn_wait", self.gather, self.over_ici, after)
        n = len(FFN)
        shapes = [jax.ShapeDtypeStruct(a.shape, a.dtype) for a in arrived[:n]]
        passed = _exchange("gather_ffn_pass", arrived[:n], shapes, lambda ins, outs, s, r: self.over_d2d(outs, s, r),
                           3 * n, in_place=True)
        return dict(zip(FFN_GATHER, list(passed) + arrived[n:]))

    def ffn_grads(self, g_down, g_up):
        grads = [g_up, g_down]
        lands = [lax.empty(s.shape, s.dtype) for s in _swap_shapes(grads)]
        self.swap = _start_copies("grad_ffn_swap_start", grads + lands, _swap_plan(len(grads)), len(grads))
        return self.swap[3]

    def ffn_backward_done(self, after):
        n = len(FFN)
        out = _wait_copies("grad_ffn_swap_wait", self.swap, _swap_plan(n), after)
        self.grads, self.got_half = out[:n], out[n:]
        sums = [_add_other_half(self.grads[t], self.got_half[t], self.where, "grad_add_half_" + name)
                for t, name in enumerate(FFN)]
        lands = [lax.empty(s.shape, s.dtype) for s in _scatter_shapes(sums)]
        self.scatter = _start_copies("grad_ffn_scatter_start", sums + lands, _scatter_plan(n), 3 * n)

    def ffn_reduced(self, after):
        n = len(FFN)
        got_pieces = _wait_copies("grad_ffn_scatter_wait", self.scatter, _scatter_plan(n), after)[n:]
        return [_add_pieces(self.grads[t], self.got_half[t], got_pieces[t], self.where, "grad_add_pieces_" + name)
                for t, name in enumerate(FFN)]


def _step(args):
    x, positions, target = args["x"][0], args["positions"], args["loss_target"][0]
    rows = x.shape[0]
    p = {n: args[n] for n in WEIGHTS}
    xi, yi, ci = _mesh_pos()
    piece = 2 * xi + yi

    w_in = p["w_in"][0]
    in_width = w_in.shape[1]
    in_pad = (-in_width) % (2 * LANES)
    heads_here = p["mla_w_uq"].shape[2] // (QK_NOPE_DIM + QK_ROPE_DIM)
    shards = {
        "w_in": jnp.pad(w_in, ((0, 0), (0, in_pad))),
        "ssm_w_glu": p["ssm_w_glu"][0],
        "mla_w_uq": _pad_heads(p["mla_w_uq"][0], heads_here),
        "mla_w_ukv": p["mla_w_ukv"][0],
        "w_out": p["w_out"][0],
        "ffn_w_up": p["ffn_w_up"][0],
        "ffn_w_down": p["ffn_w_down"][0],
    }
    conv_w = jnp.pad(p["ffn_conv_w"][0], ((0, SUBLANES - p["ffn_conv_w"].shape[1]), (0, 0)))
    order = list(SHARDED)
    piece_idx = piece.reshape(1).astype(jnp.int32)
    placed = {n: _place_shard(shards[n], piece_idx, SHARDED[n], "place_" + n) for n in order}
    placed["ffn_conv_w"] = _place_shard(conv_w, piece_idx, False, "place_ffn_conv_w", out_dtype=F32)
    mixer = [n for n in order if n not in FFN]
    w = dict(zip(mixer, _gather_weights([(placed[n], SHARDED[n], False) for n in mixer], "gather_mixer_weights")))
    where = jnp.stack([ci, piece]).astype(jnp.int32)
    hooks = _Overlapped([placed[n] for n in FFN_GATHER], where)
    groups = p["ssm_lambda_re"].shape[1]
    w.update({
        "attn_norm_w": p["attn_norm_w"], "ssm_lambda_re": p["ssm_lambda_re"][0], "ssm_lambda_im": p["ssm_lambda_im"][0],
        "ssm_log_dt": p["ssm_log_dt"].reshape(groups, 1), "ssm_b_re": p["ssm_b_re"].reshape(groups, -1),
        "ssm_b_im": p["ssm_b_im"].reshape(groups, -1), "ssm_c_re": p["ssm_c_re"][0], "ssm_c_im": p["ssm_c_im"][0],
        "ssm_d": p["ssm_d"], "ssm_b_glu": p["ssm_b_glu"], "mla_q_norm_w": p["mla_q_norm_w"],
        "mla_kv_norm_w": p["mla_kv_norm_w"], "ssm_out_norm_w": p["ssm_out_norm_w"], "mla_out_norm_w": p["mla_out_norm_w"],
        "ffn_norm_w": p["ffn_norm_w"], "ffn_conv_b": p["ffn_conv_b"], "final_norm_w": p["final_norm_w"].reshape(1, -1),
    })

    loss_tile, dx, g = _local_step(x, positions.reshape(rows, 1).astype(F32), target, w, hooks)
    loss = lax.psum(loss_tile[0, 0], ("x", "y", "c"))

    g_mixer = [g[n] for n in mixer]
    got_half = _swap_other_half(g_mixer, "grad_swap_halves")
    sums = [_add_other_half(g_mixer[t], got_half[t], where, "grad_add_half_" + n) for t, n in enumerate(mixer)]
    got_pieces = _scatter_pieces(sums, "grad_scatter_pieces")
    halves = {n: _add_pieces(g_mixer[t], got_half[t], got_pieces[t], where, "grad_add_pieces_" + n)
              for t, n in enumerate(mixer)}
    halves.update(zip(FFN, hooks.ffn_reduced(dx)))
    joined = _join_halves([halves[n] for n in order])
    grads = {}
    for t, n in enumerate(order):
        j = joined[t]
        grads[n] = jnp.concatenate([j[0], j[1]], axis=1) if SHARDED[n] else j.reshape(2 * j.shape[1], j.shape[2])
    grads["w_in"] = grads["w_in"][:, :in_width]
    grads["mla_w_uq"] = _unpad_heads(grads["mla_w_uq"], heads_here)

    flat = [g[n].reshape(-1) for n in SMALL] + [g["ffn_conv_w"].reshape(-1)]
    sizes = [f.shape[0] for f in flat]
    total = sum(sizes)
    tile_elems = SUBLANES * LANES
    padded = -(-total // tile_elems) * tile_elems

    def pack(parts):
        parts = list(parts)
        have = sum(q.shape[0] for q in parts)
        return jnp.concatenate(parts + [jnp.zeros((padded - have,), F32)]).reshape(padded // LANES, LANES)

    me_idx = (4 * xi + 2 * yi + ci).reshape(1).astype(jnp.int32)
    slots = _place_shard(pack(flat), me_idx, True, "place_small_grads", out_dtype=F32, pieces=N_CHIPS * N_CORES)
    small_sum = _sum_slots(_all_to_all_small(slots.reshape(N_CHIPS * N_CORES, padded // LANES, LANES)))
    flat_sum = small_sum.reshape(-1)
    offs = [0]
    for s in sizes:
        offs.append(offs[-1] + s)
    for k, n in enumerate(SMALL):
        grads[n] = flat_sum[offs[k]:offs[k + 1]].reshape(p[n].shape)
    taps, cols_here = p["ffn_conv_w"].shape[1], p["ffn_conv_w"].shape[2]
    conv_full = flat_sum[offs[len(SMALL)]:offs[len(SMALL) + 1]].reshape(taps, N_CHIPS * cols_here)
    grads["ffn_conv_w"] = lax.dynamic_slice_in_dim(conv_full, piece * cols_here, cols_here, axis=1)

    delta, new_m, new_v = {}, {}, {}
    for n in list(SHARDED) + ["ffn_conv_w"]:
        shape = p[n].shape
        d2, m2, v2 = _adamw(p[n].reshape(shape[1:]), grads[n], args["m_" + n].reshape(shape[1:]),
                            args["v_" + n].reshape(shape[1:]), "adamw_" + n)
        grads[n] = grads[n].reshape(shape)
        delta[n], new_m[n], new_v[n] = d2.reshape(shape), m2.reshape(shape), v2.reshape(shape)
    d2, m2, v2 = _adamw(pack(p[n].reshape(-1) for n in SMALL), small_sum, pack(args["m_" + n].reshape(-1) for n in SMALL),
                        pack(args["v_" + n].reshape(-1) for n in SMALL), "adamw_small")
    for k, n in enumerate(SMALL):
        for src, dst in ((d2, delta), (m2, new_m), (v2, new_v)):
            dst[n] = src.reshape(-1)[offs[k]:offs[k + 1]].reshape(p[n].shape)

    return (loss, dx[None], *[grads[n] for n in WEIGHTS], *[delta[n] for n in WEIGHTS],
            *[new_m[n] for n in WEIGHTS], *[new_v[n] for n in WEIGHTS])


def kernel(x, positions, attn_norm_w, w_in, ssm_lambda_re, ssm_lambda_im, ssm_log_dt, ssm_b_re, ssm_b_im, ssm_c_re, ssm_c_im, ssm_d, ssm_w_glu, ssm_b_glu, mla_q_norm_w, mla_w_uq, mla_kv_norm_w, mla_w_ukv, ssm_out_norm_w, mla_out_norm_w, w_out, ffn_norm_w, ffn_w_up, ffn_conv_w, ffn_conv_b, ffn_w_down, final_norm_w, loss_target, m_attn_norm_w, m_w_in, m_ssm_lambda_re, m_ssm_lambda_im, m_ssm_log_dt, m_ssm_b_re, m_ssm_b_im, m_ssm_c_re, m_ssm_c_im, m_ssm_d, m_ssm_w_glu, m_ssm_b_glu, m_mla_q_norm_w, m_mla_w_uq, m_mla_kv_norm_w, m_mla_w_ukv, m_ssm_out_norm_w, m_mla_out_norm_w, m_w_out, m_ffn_norm_w, m_ffn_w_up, m_ffn_conv_w, m_ffn_conv_b, m_ffn_w_down, m_final_norm_w, v_attn_norm_w, v_w_in, v_ssm_lambda_re, v_ssm_lambda_im, v_ssm_log_dt, v_ssm_b_re, v_ssm_b_im, v_ssm_c_re, v_ssm_c_im, v_ssm_d, v_ssm_w_glu, v_ssm_b_glu, v_mla_q_norm_w, v_mla_w_uq, v_mla_kv_norm_w, v_mla_w_ukv, v_ssm_out_norm_w, v_mla_out_norm_w, v_w_out, v_ffn_norm_w, v_ffn_w_up, v_ffn_conv_w, v_ffn_conv_b, v_ffn_w_down, v_final_norm_w):
    return _step(dict(locals()))
```

```python
import functools
import math

import jax
import jax.numpy as jnp
from jax import lax
from jax.experimental import pallas as pl
from jax.experimental.pallas import tpu as pltpu

F32 = jnp.float32
BF16 = jnp.bfloat16

SSM_GROUP = 16
SSM_STATE = 64
QK_NOPE_DIM = 128
QK_ROPE_DIM = 64
V_HEAD_DIM = 128
ROPE_THETA = 10000.0
RMS_EPS = 1e-6
ADAM_LR, ADAM_B1, ADAM_B2, ADAM_EPS, ADAM_WD, ADAM_STEP = 0.001, 0.9, 0.999, 1e-08, 0.01, 10

LANES = 128
SUBLANES = 8
VMEM_LIMIT_BYTES = 56 * 1024 * 1024

GROUPS_PER_BATCH = LANES // SSM_GROUP
STATE_PER_BATCH = GROUPS_PER_BATCH * SSM_STATE
HEAD_SLOT = 2 * LANES
NEG_INF = -1e30
ATTN_BLOCK = 512

N_CHIPS = 4
N_CORES = 2


def _tile(n, pref, align=LANES):
    if n <= pref:
        return n
    t = (pref // align) * align
    while t >= align:
        if n % t == 0:
            return t
        t -= align
    return n


def _params(sem):
    return pltpu.CompilerParams(dimension_semantics=sem, vmem_limit_bytes=VMEM_LIMIT_BYTES)


def _dot(a, b, dims):
    return lax.dot_general(a, b, (dims, ((), ())), preferred_element_type=F32)


def _dot_nn(a, b):
    return _dot(a, b, ((1,), (0,)))


def _dot_nt(a, b):
    return _dot(a, b, ((1,), (1,)))


def _dot_tn(a, b):
    return _dot(a, b, ((0,), (0,)))


def _matmul(a, b, *, mode, name, tm=512, tn=1024, tk=2048, bias=None, add=None, out_dtype=F32,
            out_blocks=None, a_split=False, b_split=False):
    if a_split:
        assert mode == "nt"
        a_shape = (a.shape[1], 2 * a.shape[2])
    else:
        a_shape = a.shape
    if b_split:
        assert mode == "tn"
        b_shape = (b.shape[1], 2 * b.shape[2])
    else:
        b_shape = b.shape
    if mode == "nn":
        (m, k), (k2, n) = a_shape, b_shape
    elif mode == "nt":
        (m, k), (n, k2) = a_shape, b_shape
    else:
        (k, m), (k2, n) = a_shape, b_shape
    assert k == k2, (a.shape, b.shape, mode)
    tm, tn, tk = _tile(m, tm, SUBLANES), _tile(n, tn), _tile(k, tk)
    nk = k // tk
    a_spec = {"nn": pl.BlockSpec((tm, tk), lambda i, j, kk: (i, kk)),
              "nt": pl.BlockSpec((tm, tk), lambda i, j, kk: (i, kk)),
              "tn": pl.BlockSpec((tk, tm), lambda i, j, kk: (kk, i))}[mode]
    b_spec = {"nn": pl.BlockSpec((tk, tn), lambda i, j, kk: (kk, j)),
              "nt": pl.BlockSpec((tn, tk), lambda i, j, kk: (j, kk)),
              "tn": pl.BlockSpec((tk, tn), lambda i, j, kk: (kk, j))}[mode]
    if a_split:
        kb = a.shape[2] // tk
        assert a.shape[2] % tk == 0
        a_spec = pl.BlockSpec((None, tm, tk), lambda i, j, kk: (kk // kb, i, kk % kb))
    if b_split:
        nb = b.shape[2] // tn
        assert b.shape[2] % tn == 0
        b_spec = pl.BlockSpec((None, tk, tn), lambda i, j, kk: (j // nb, kk, j % nb))
    dot = {"nn": _dot_nn, "nt": _dot_nt, "tn": _dot_tn}[mode]
    in_specs, operands = [a_spec, b_spec], [a, b]
    if bias is not None:
        in_specs.append(pl.BlockSpec((1, tn), lambda i, j, kk: (0, j)))
        operands.append(bias)
    if add is not None:
        in_specs.append(pl.BlockSpec((tm, tn), lambda i, j, kk: (i, j)))
        operands.append(add)

    def body(*refs):
        a_ref, b_ref = refs[0], refs[1]
        rest = list(refs[2:])
        bias_ref = rest.pop(0) if bias is not None else None
        add_ref = rest.pop(0) if add is not None else None
        o_ref, acc_ref = rest

        def finish(acc):
            if bias_ref is not None:
                acc = acc + bias_ref[...]
            if add_ref is not None:
                acc = acc + add_ref[...]
            o_ref[...] = acc.astype(o_ref.dtype)

        part = dot(a_ref[...].astype(BF16), b_ref[...].astype(BF16))
        if nk == 1:
            finish(part)
        else:
            kk = pl.program_id(2)

            @pl.when(kk == 0)
            def _():
                acc_ref[...] = part

            @pl.when(jnp.logical_and(kk > 0, kk < nk - 1))
            def _():
                acc_ref[...] += part

            @pl.when(kk == nk - 1)
            def _():
                finish(acc_ref[...] + part)

    if out_blocks is None:
        out_shape = jax.ShapeDtypeStruct((m, n), out_dtype)
        out_spec = pl.BlockSpec((tm, tn), lambda i, j, kk: (i, j))
    else:
        shape, block, index_map = out_blocks(tm, tn)
        out_shape = jax.ShapeDtypeStruct(shape, out_dtype)
        out_spec = pl.BlockSpec(block, index_map)
    acc_shape = (tm, tn) if nk > 1 else (SUBLANES, LANES)
    return pl.pallas_call(
        body, name=name, grid=(m // tm, n // tn, nk), in_specs=in_specs, out_specs=out_spec, out_shape=out_shape,
        scratch_shapes=[pltpu.VMEM(acc_shape, F32)],
        compiler_params=_params(("parallel", "parallel", "arbitrary")),
    )(*operands)


def _wgrad_blocks(rows, cols, row_sharded):
    if row_sharded:
        sr, sc = rows // N_CHIPS, cols // N_CORES
    else:
        sr, sc = rows // N_CORES, cols // N_CHIPS

    def make(tm, tn):
        assert sr % tm == 0 and sc % tn == 0, (rows, cols, tm, tn)
        rb, cb = sr // tm, sc // tn
        if row_sharded:
            def index_map(i, j, kk):
                return (j // cb, i // rb, i % rb, j % cb)
        else:
            def index_map(i, j, kk):
                return (i // rb, j // cb, i % rb, j % cb)
        return (N_CORES, N_CHIPS, sr, sc), (None, None, tm, tn), index_map

    return make, (sr, sc)


def _rms_rows(x):
    return lax.rsqrt(jnp.mean(x * x, axis=-1, keepdims=True) + RMS_EPS)


def _rmsnorm_fwd(x, w, *, name, width=None, col=0, out_dtype=BF16, tr=256):
    rows = x.shape[0]
    width = x.shape[1] if width is None else width
    tr = _tile(rows, tr, SUBLANES)

    def body(x_ref, w_ref, o_ref):
        xv = x_ref[...]
        o_ref[...] = (xv * _rms_rows(xv) * w_ref[...]).astype(o_ref.dtype)

    return pl.pallas_call(
        body, name=name, grid=(rows // tr,),
        in_specs=[pl.BlockSpec((tr, width), lambda i: (i, col)), pl.BlockSpec((1, width), lambda i: (0, 0))],
        out_specs=pl.BlockSpec((tr, width), lambda i: (i, 0)),
        out_shape=jax.ShapeDtypeStruct((rows, width), out_dtype),
        compiler_params=_params(("parallel",)),
    )(x, w)


def _rmsnorm_bwd_rows(xv, w, dy):
    r = _rms_rows(xv)
    n = xv * r
    dn = dy * w
    dx = r * (dn - n * jnp.mean(dn * n, axis=-1, keepdims=True))
    return dx, dy * n


def _rmsnorm_bwd(x, w, dy, *, name, width=None, col=0, dy_col=0, add=None, tr=256):
    rows = x.shape[0]
    width = x.shape[1] if width is None else width
    tr = _tile(rows, tr, SUBLANES)
    in_specs = [pl.BlockSpec((tr, width), lambda i: (i, col)), pl.BlockSpec((1, width), lambda i: (0, 0)),
                pl.BlockSpec((tr, width), lambda i: (i, dy_col))]
    operands = [x, w, dy]
    if add is not None:
        in_specs.append(pl.BlockSpec((tr, width), lambda i: (i, 0)))
        operands.append(add)

    def body(*refs):
        x_ref, w_ref, dy_ref = refs[:3]
        add_ref = refs[3] if add is not None else None
        dx_ref, dw_ref = refs[-2:]
        dx, dwp = _rmsnorm_bwd_rows(x_ref[...], w_ref[...], dy_ref[...])
        if add_ref is not None:
            dx = dx + add_ref[...]
        dx_ref[...] = dx
        part = jnp.sum(dwp, axis=0, keepdims=True)

        @pl.when(pl.program_id(0) == 0)
        def _():
            dw_ref[...] = part

        @pl.when(pl.program_id(0) > 0)
        def _():
            dw_ref[...] += part

    return pl.pallas_call(
        body, name=name, grid=(rows // tr,), in_specs=in_specs,
        out_specs=[pl.BlockSpec((tr, width), lambda i: (i, 0)), pl.BlockSpec((1, width), lambda i: (0, 0))],
        out_shape=[jax.ShapeDtypeStruct((rows, width), F32), jax.ShapeDtypeStruct((1, width), F32)],
        compiler_params=_params(("arbitrary",)),
    )(*operands)


def _final_norm_loss(h, w, target, *, tr=256):
    rows, d = h.shape
    tr = _tile(rows, tr, SUBLANES)

    def body(h_ref, w_ref, t_ref, loss_ref, dh_ref, dw_ref):
        hv, wv = h_ref[...], w_ref[...]
        r = _rms_rows(hv)
        n = hv * r
        err = n * wv - t_ref[...]
        d_out = err * (1.0 / d)
        dn = d_out * wv
        dh_ref[...] = r * (dn - n * jnp.mean(dn * n, axis=-1, keepdims=True))
        dw_part = jnp.sum(d_out * n, axis=0, keepdims=True)
        loss_part = jnp.full((SUBLANES, LANES), 0.5 / d, F32) * jnp.sum(err * err)

        @pl.when(pl.program_id(0) == 0)
        def _():
            dw_ref[...] = dw_part
            loss_ref[...] = loss_part

        @pl.when(pl.program_id(0) > 0)
        def _():
            dw_ref[...] += dw_part
            loss_ref[...] += loss_part

    return pl.pallas_call(
        body, name="final_norm_loss", grid=(rows // tr,),
        in_specs=[pl.BlockSpec((tr, d), lambda i: (i, 0)), pl.BlockSpec((1, d), lambda i: (0, 0)),
                  pl.BlockSpec((tr, d), lambda i: (i, 0))],
        out_specs=[pl.BlockSpec((SUBLANES, LANES), lambda i: (0, 0)), pl.BlockSpec((tr, d), lambda i: (i, 0)),
                   pl.BlockSpec((1, d), lambda i: (0, 0))],
        out_shape=[jax.ShapeDtypeStruct((SUBLANES, LANES), F32), jax.ShapeDtypeStruct((rows, d), F32),
                   jax.ShapeDtypeStruct((1, d), F32)],
        compiler_params=_params(("arbitrary",)),
    )(h, w, target)


def _cmul(ar, ai, br, bi):
    return ar * br - ai * bi, ar * bi + ai * br


def _expand_matrix(groups, reps):
    row = lax.broadcasted_iota(jnp.int32, (groups, groups * reps), 0)
    colg = lax.broadcasted_iota(jnp.int32, (groups, groups * reps), 1) // reps
    return (row == colg).astype(F32)


def _dot_exact(a, b, dims):
    return lax.dot_general(a, b, (dims, ((), ())), preferred_element_type=F32, precision=lax.Precision.HIGHEST)


def _s5_discretize(lr, li, dt):
    mag = jnp.exp(lr * dt)
    th = li * dt
    ar, ai = mag * jnp.cos(th), mag * jnp.sin(th)
    nr, ni = ar - 1.0, ai
    den = lr * lr + li * li
    zr = (nr * lr + ni * li) / den
    zi = (ni * lr - nr * li) / den
    return mag, ar, ai, nr, ni, den, zr, zi


def _s5_params(lam_re, lam_im, log_dt, b_re, b_im):
    g, p = lam_re.shape
    ph = b_re.shape[1]

    def body(lr_ref, li_ref, ldt_ref, br_ref, bi_ref, ar_ref, ai_ref, bbr_ref, bbi_ref):
        dt = jnp.exp(ldt_ref[...])
        _, ar, ai, _, _, _, zr, zi = _s5_discretize(lr_ref[...], li_ref[...], dt)
        ar_ref[...] = ar
        ai_ref[...] = ai
        e = _expand_matrix(p, ph // p)
        zr_x = _dot_exact(zr, e, ((1,), (0,)))
        zi_x = _dot_exact(zi, e, ((1,), (0,)))
        bre, bim = br_ref[...], bi_ref[...]
        bbr_ref[...] = zr_x * bre - zi_x * bim
        bbi_ref[...] = zr_x * bim + zi_x * bre

    return pl.pallas_call(
        body, name="s5_params",
        out_shape=[jax.ShapeDtypeStruct((g, p), F32)] * 2 + [jax.ShapeDtypeStruct((g, ph), F32)] * 2,
    )(lam_re, lam_im, log_dt, b_re, b_im)


def _s5_params_bwd(lam_re, lam_im, log_dt, b_re, b_im, d_ar, d_ai, d_bbr, d_bbi):
    g, p = lam_re.shape
    ph = b_re.shape[1]

    def body(lr_ref, li_ref, ldt_ref, br_ref, bi_ref, dar_ref, dai_ref, dbr_ref, dbi_ref,
             dlr_ref, dli_ref, dldt_ref, dbre_ref, dbim_ref):
        lr, li = lr_ref[...], li_ref[...]
        dt = jnp.exp(ldt_ref[...])
        mag, ar, ai, nr, ni, den, zr, zi = _s5_discretize(lr, li, dt)
        e = _expand_matrix(p, ph // p)
        zr_x = _dot_exact(zr, e, ((1,), (0,)))
        zi_x = _dot_exact(zi, e, ((1,), (0,)))
        bre, bim, dbr, dbi = br_ref[...], bi_ref[...], dbr_ref[...], dbi_ref[...]
        dbre_ref[...] = zr_x * dbr + zi_x * dbi
        dbim_ref[...] = zr_x * dbi - zi_x * dbr
        dzr = _dot_exact(bre * dbr + bim * dbi, e, ((1,), (1,)))
        dzi = _dot_exact(bre * dbi - bim * dbr, e, ((1,), (1,)))
        inv = 1.0 / den
        d_nr = (dzr * lr - dzi * li) * inv
        d_ni = (dzr * li + dzi * lr) * inv
        d_den = -(dzr * zr + dzi * zi) * inv
        d_lr = (dzr * nr + dzi * ni) * inv + 2.0 * lr * d_den
        d_li = (dzr * ni - dzi * nr) * inv + 2.0 * li * d_den
        t_ar = dar_ref[...] + d_nr
        t_ai = dai_ref[...] + d_ni
        d_lrdt = t_ar * ar + t_ai * ai
        d_th = t_ai * ar - t_ar * ai
        dlr_ref[...] = d_lr + d_lrdt * dt
        dli_ref[...] = d_li + d_th * dt
        dldt_ref[...] = jnp.sum(d_lrdt * lr + d_th * li, axis=1, keepdims=True) * dt

    return pl.pallas_call(
        body, name="s5_params_bwd",
        out_shape=[jax.ShapeDtypeStruct((g, p), F32)] * 2 + [jax.ShapeDtypeStruct((g, 1), F32)]
        + [jax.ShapeDtypeStruct((g, ph), F32)] * 2,
    )(lam_re, lam_im, log_dt, b_re, b_im, d_ar, d_ai, d_bbr, d_bbi)


def _powers(ar, ai, count):
    out = [(ar, ai)]
    for _ in range(count - 1):
        out.append(_cmul(out[-1][0], out[-1][1], ar, ai))
    return out


def _scan_coefs(ar, ai, reverse):
    w = ar.shape[-1]
    pw = _powers(ar, ai, SUBLANES)
    row = lax.broadcasted_iota(jnp.int32, (SUBLANES, w), 0)
    steps = []
    d = 1
    while d < SUBLANES:
        keep = (row < SUBLANES - d) if reverse else (row >= d)
        pr, pi = pw[d - 1]
        steps.append((d, jnp.where(keep, pr, 0.0), jnp.where(keep, pi, 0.0)))
        d *= 2
    cr = jnp.zeros((SUBLANES, w), F32)
    ci = jnp.zeros((SUBLANES, w), F32)
    for t in range(SUBLANES):
        pr, pi = pw[SUBLANES - 1 - t] if reverse else pw[t]
        cr = jnp.where(row == t, pr, cr)
        ci = jnp.where(row == t, pi, ci)
    return steps, cr, ci


def _scan_tile(xr, xi, carry_r, carry_i, coefs, reverse):
    steps, cr, ci = coefs
    for d, mr, mi in steps:
        shift = SUBLANES - d if reverse else d
        sr, si = pltpu.roll(xr, shift, 0), pltpu.roll(xi, shift, 0)
        pr, pi = _cmul(mr, mi, sr, si)
        xr, xi = xr + pr, xi + pi
    pr, pi = _cmul(cr, ci, carry_r, carry_i)
    return xr + pr, xi + pi


def _gelu(x):
    c = math.sqrt(2.0 / math.pi)
    return 0.5 * x * (1.0 + jnp.tanh(c * (x + 0.044715 * x * x * x)))


def _gelu_grad(x):
    c = math.sqrt(2.0 / math.pi)
    t = jnp.tanh(c * (x + 0.044715 * x * x * x))
    return 0.5 * (1.0 + t) + 0.5 * x * (1.0 - t * t) * c * (1.0 + 3.0 * 0.044715 * x * x)


def _s5_fwd(proj, wb, wc, d_skip, abar):
    rows = proj.shape[0]
    nb = wb.shape[0]
    s2 = 2 * STATE_PER_BATCH
    st = STATE_PER_BATCH
    chunk = _tile(rows, 512, SUBLANES)

    def body(u_ref, wb_ref, wc_ref, d_ref, a_ref, s_ref, y_ref, yg_ref):
        for c0 in range(0, rows, chunk):
            s_ref[pl.ds(c0, chunk), :] = _dot_nn(u_ref[pl.ds(c0, chunk), :].astype(BF16), wb_ref[...])
        av = a_ref[...]
        coefs = _scan_coefs(av[:, :st], av[:, st:], reverse=False)

        def tile(b, carry):
            r0 = pl.multiple_of(b * SUBLANES, SUBLANES)
            xr, xi = _scan_tile(s_ref[pl.ds(r0, SUBLANES), :st], s_ref[pl.ds(r0, SUBLANES), st:], carry[0], carry[1],
                                coefs, False)
            s_ref[pl.ds(r0, SUBLANES), :st] = xr
            s_ref[pl.ds(r0, SUBLANES), st:] = xi
            return xr[SUBLANES - 1:, :], xi[SUBLANES - 1:, :]

        zero = jnp.zeros((1, st), F32)
        lax.fori_loop(0, rows // SUBLANES, tile, (zero, zero))
        for c0 in range(0, rows, chunk):
            y = _dot_nn(s_ref[pl.ds(c0, chunk), :].astype(BF16), wc_ref[...]) + d_ref[...] * u_ref[pl.ds(c0, chunk), :]
            y_ref[pl.ds(c0, chunk), :] = y
            yg_ref[pl.ds(c0, chunk), :] = _gelu(y).astype(BF16)

    return pl.pallas_call(
        body, name="s5_fwd", grid=(nb,),
        in_specs=[pl.BlockSpec((rows, LANES), lambda j: (0, j)), pl.BlockSpec((None, LANES, s2), lambda j: (j, 0, 0)),
                  pl.BlockSpec((None, s2, LANES), lambda j: (j, 0, 0)), pl.BlockSpec((1, LANES), lambda j: (0, j)),
                  pl.BlockSpec((None, 1, s2), lambda j: (j, 0, 0))],
        out_specs=[pl.BlockSpec((rows, s2), lambda j: (0, j)), pl.BlockSpec((rows, LANES), lambda j: (0, j)),
                   pl.BlockSpec((rows, LANES), lambda j: (0, j))],
        out_shape=[jax.ShapeDtypeStruct((rows, nb * s2), F32), jax.ShapeDtypeStruct((rows, nb * LANES), F32),
                   jax.ShapeDtypeStruct((rows, nb * LANES), BF16)],
        compiler_params=_params(("parallel",)),
    )(proj, wb, wc, d_skip, abar)


def _s5_bwd(proj, states, y_pre, dyg_a, dyg_b, wb, wc, d_skip, abar):
    rows = proj.shape[0]
    nb = wb.shape[0]
    s2 = 2 * STATE_PER_BATCH
    st = STATE_PER_BATCH
    chunk = _tile(rows, 512, SUBLANES)
    n_tiles = rows // SUBLANES

    def body(u_ref, s_ref, y_ref, ga_ref, gb_ref, wb_ref, wc_ref, d_ref, a_ref,
             du_ref, dwb_ref, dwc_ref, da_ref, dd_ref, ds_ref, dy_ref):
        dy_ref[...] = (ga_ref[...] + gb_ref[...]) * _gelu_grad(y_ref[...])
        dd_ref[...] = jnp.sum(dy_ref[...] * u_ref[...], axis=0, keepdims=True)
        for c0 in range(0, rows, chunk):
            ds_ref[pl.ds(c0, chunk), :] = _dot_nt(dy_ref[pl.ds(c0, chunk), :].astype(BF16), wc_ref[...])
        dwc_ref[...] = _dot_tn(s_ref[...].astype(BF16), dy_ref[...].astype(BF16))
        av = a_ref[...]
        coefs = _scan_coefs(av[:, :st], -av[:, st:], reverse=True)
        row = lax.broadcasted_iota(jnp.int32, (SUBLANES, st), 0)

        def tile(k, carry):
            cr, ci, acc_r, acc_i = carry
            b = n_tiles - 1 - k
            r0 = pl.multiple_of(b * SUBLANES, SUBLANES)
            rp = pl.multiple_of(jnp.maximum(b - 1, 0) * SUBLANES, SUBLANES)
            xr, xi = _scan_tile(ds_ref[pl.ds(r0, SUBLANES), :st], ds_ref[pl.ds(r0, SUBLANES), st:], cr, ci, coefs, True)
            ds_ref[pl.ds(r0, SUBLANES), :st] = xr
            ds_ref[pl.ds(r0, SUBLANES), st:] = xi
            first = jnp.where(b > 0, 1.0, 0.0)
            pr = jnp.where(row == 0, pltpu.roll(s_ref[pl.ds(rp, SUBLANES), :st], 1, 0) * first,
                           pltpu.roll(s_ref[pl.ds(r0, SUBLANES), :st], 1, 0))
            pi = jnp.where(row == 0, pltpu.roll(s_ref[pl.ds(rp, SUBLANES), st:], 1, 0) * first,
                           pltpu.roll(s_ref[pl.ds(r0, SUBLANES), st:], 1, 0))
            acc_r = acc_r + pr * xr + pi * xi
            acc_i = acc_i + pr * xi - pi * xr
            return xr[:1, :], xi[:1, :], acc_r, acc_i

        zero = jnp.zeros((1, st), F32)
        zacc = jnp.zeros((SUBLANES, st), F32)
        _, _, acc_r, acc_i = lax.fori_loop(0, n_tiles, tile, (zero, zero, zacc, zacc))
        da_ref[:, :st] = jnp.sum(acc_r, axis=0, keepdims=True)
        da_ref[:, st:] = jnp.sum(acc_i, axis=0, keepdims=True)
        for c0 in range(0, rows, chunk):
            du_ref[pl.ds(c0, chunk), :] = (_dot_nt(ds_ref[pl.ds(c0, chunk), :].astype(BF16), wb_ref[...])
                                           + d_ref[...] * dy_ref[pl.ds(c0, chunk), :])
        dwb_ref[...] = _dot_tn(u_ref[...].astype(BF16), ds_ref[...].astype(BF16))

    col = pl.BlockSpec((rows, LANES), lambda j: (0, j))
    return pl.pallas_call(
        body, name="s5_bwd", grid=(nb,),
        in_specs=[col, pl.BlockSpec((rows, s2), lambda j: (0, j)), col, col, col,
                  pl.BlockSpec((None, LANES, s2), lambda j: (j, 0, 0)), pl.BlockSpec((None, s2, LANES), lambda j: (j, 0, 0)),
                  pl.BlockSpec((1, LANES), lambda j: (0, j)), pl.BlockSpec((None, 1, s2), lambda j: (j, 0, 0))],
        out_specs=[col, pl.BlockSpec((None, LANES, s2), lambda j: (j, 0, 0)),
                   pl.BlockSpec((None, s2, LANES), lambda j: (j, 0, 0)), pl.BlockSpec((None, 1, s2), lambda j: (j, 0, 0)),
                   pl.BlockSpec((1, LANES), lambda j: (0, j))],
        out_shape=[jax.ShapeDtypeStruct((rows, nb * LANES), F32), jax.ShapeDtypeStruct((nb, LANES, s2), F32),
                   jax.ShapeDtypeStruct((nb, s2, LANES), F32), jax.ShapeDtypeStruct((nb, 1, s2), F32),
                   jax.ShapeDtypeStruct((1, nb * LANES), F32)],
        scratch_shapes=[pltpu.VMEM((rows, s2), F32), pltpu.VMEM((rows, LANES), F32)],
        compiler_params=_params(("parallel",)),
    )(proj, states, y_pre, dyg_a, dyg_b, wb, wc, d_skip, abar)


def _glu_norm_fwd(y_pre, z, w, *, tr=256):
    rows, width = y_pre.shape
    tr = _tile(rows, tr, SUBLANES)

    def body(y_ref, z_ref, w_ref, o_ref):
        v = _gelu(y_ref[...]) * jax.nn.sigmoid(z_ref[...])
        o_ref[...] = (v * _rms_rows(v) * w_ref[...]).astype(o_ref.dtype)

    blk = pl.BlockSpec((tr, width), lambda i: (i, 0))
    return pl.pallas_call(
        body, name="glu_norm_fwd", grid=(rows // tr,),
        in_specs=[blk, blk, pl.BlockSpec((1, width), lambda i: (0, 0))], out_specs=blk,
        out_shape=jax.ShapeDtypeStruct((rows, width), BF16), compiler_params=_params(("parallel",)),
    )(y_pre, z, w)


def _glu_norm_bwd(y_pre, z, w, dycat, *, tr=256):
    rows, width = y_pre.shape
    tr = _tile(rows, tr, SUBLANES)

    def body(y_ref, z_ref, w_ref, dy_ref, dz_ref, dg_ref, dw_ref, db_ref):
        yg = _gelu(y_ref[...])
        sg = jax.nn.sigmoid(z_ref[...])
        dv, dwp = _rmsnorm_bwd_rows(yg * sg, w_ref[...], dy_ref[...])
        dz = dv * yg * sg * (1.0 - sg)
        dz_ref[...] = dz
        dg_ref[...] = dv * sg
        dw_part = jnp.sum(dwp, axis=0, keepdims=True)
        db_part = jnp.sum(dz, axis=0, keepdims=True)

        @pl.when(pl.program_id(0) == 0)
        def _():
            dw_ref[...] = dw_part
            db_ref[...] = db_part

        @pl.when(pl.program_id(0) > 0)
        def _():
            dw_ref[...] += dw_part
            db_ref[...] += db_part

    blk = pl.BlockSpec((tr, width), lambda i: (i, 0))
    vec = pl.BlockSpec((1, width), lambda i: (0, 0))
    return pl.pallas_call(
        body, name="glu_norm_bwd", grid=(rows // tr,), in_specs=[blk, blk, vec, blk], out_specs=[blk, blk, vec, vec],
        out_shape=[jax.ShapeDtypeStruct((rows, width), F32)] * 2 + [jax.ShapeDtypeStruct((1, width), F32)] * 2,
        compiler_params=_params(("arbitrary",)),
    )(y_pre, z, w, dycat)


def _rope_tables(pos, freq, sign):
    rows = pos.shape[0]

    def body(p_ref, f_ref, s_ref, cos_ref, sin_ref):
        ang = p_ref[...] * f_ref[...]
        cos_ref[...] = jnp.cos(ang)
        sin_ref[...] = jnp.sin(ang) * s_ref[...]

    return pl.pallas_call(body, name="rope_tables", out_shape=[jax.ShapeDtypeStruct((rows, LANES), F32)] * 2)(pos, freq, sign)


def _rope(x, cos, sin_signed):
    lane = lax.broadcasted_iota(jnp.int32, x.shape, 1)
    half = QK_ROPE_DIM // 2
    swapped = jnp.where(lane < half, pltpu.roll(x, LANES - half, 1), pltpu.roll(x, half, 1))
    return x * cos + swapped * sin_signed


def _attn_prep(q, kv, proj, kpe_col, cos, sin, *, tr=256):
    rows = q.shape[0]
    heads = q.shape[1] // HEAD_SLOT
    tr = _tile(rows, tr, SUBLANES)

    def body(q_ref, kv_ref, kpe_ref, cos_ref, sin_ref, qc_ref, kc_ref, v_ref):
        c, s = cos_ref[...], sin_ref[...]
        qc_ref[:, :LANES] = q_ref[:, :LANES].astype(BF16)
        qc_ref[:, LANES:] = _rope(q_ref[:, LANES:], c, s).astype(BF16)
        kc_ref[:, :LANES] = kv_ref[:, :LANES].astype(BF16)
        kc_ref[:, LANES:] = _rope(kpe_ref[...], c, s).astype(BF16)
        v_ref[...] = kv_ref[:, LANES:].astype(BF16)

    slot = pl.BlockSpec((tr, HEAD_SLOT), lambda i, h: (i, h))
    tab = pl.BlockSpec((tr, LANES), lambda i, h: (i, 0))
    return pl.pallas_call(
        body, name="attn_prep", grid=(rows // tr, heads),
        in_specs=[slot, slot, pl.BlockSpec((tr, LANES), lambda i, h: (i, kpe_col)), tab, tab],
        out_specs=[slot, slot, pl.BlockSpec((tr, LANES), lambda i, h: (i, h))],
        out_shape=[jax.ShapeDtypeStruct((rows, heads * HEAD_SLOT), BF16)] * 2
        + [jax.ShapeDtypeStruct((rows, heads * LANES), BF16)],
        compiler_params=_params(("parallel", "parallel")),
    )(q, kv, proj, cos, sin)


def _causal(i, j, tq, tk):
    qpos = i * tq + lax.broadcasted_iota(jnp.int32, (tq, tk), 0)
    kpos = j * tk + lax.broadcasted_iota(jnp.int32, (tq, tk), 1)
    return kpos <= qpos


def _attn_fwd(qc, kc, vb, *, scale, tq=512):
    rows = qc.shape[0]
    heads = qc.shape[1] // HEAD_SLOT
    tq = _tile(rows, tq, SUBLANES)
    tk = tq

    def body(q_ref, k_ref, v_ref, o_ref, lse_ref):
        i = pl.program_id(1)
        q = q_ref[...]

        def step(j, carry):
            m, l, acc = carry
            k0 = pl.multiple_of(j * tk, tk)
            s = _dot_nt(q, k_ref[pl.ds(k0, tk), :]) * scale
            s = jnp.where(_causal(i, j, tq, tk), s, NEG_INF)
            m_new = jnp.maximum(m, jnp.max(s, axis=-1, keepdims=True))
            p = jnp.exp(s - m_new)
            alpha = jnp.exp(m - m_new)
            l = alpha * l + jnp.sum(p, axis=-1, keepdims=True)
            acc = alpha * acc + _dot_nn(p.astype(BF16), v_ref[pl.ds(k0, tk), :])
            return m_new, l, acc

        init = (jnp.full((tq, 1), NEG_INF, F32), jnp.zeros((tq, 1), F32), jnp.zeros((tq, LANES), F32))
        m, l, acc = lax.fori_loop(0, i + 1, step, init)
        o_ref[...] = acc / l
        lse_ref[...] = jnp.broadcast_to(m + jnp.log(l), (tq, LANES))

    return pl.pallas_call(
        body, name="attn_fwd", grid=(heads, rows // tq),
        in_specs=[pl.BlockSpec((tq, HEAD_SLOT), lambda h, i: (i, h)), pl.BlockSpec((rows, HEAD_SLOT), lambda h, i: (0, h)),
                  pl.BlockSpec((rows, LANES), lambda h, i: (0, h))],
        out_specs=[pl.BlockSpec((tq, LANES), lambda h, i: (i, h))] * 2,
        out_shape=[jax.ShapeDtypeStruct((rows, heads * LANES), F32)] * 2,
        compiler_params=_params(("parallel", "parallel")),
    )(qc, kc, vb)


def _attn_bwd_q(qc, kc, vb, o, do, lse, cos, sin, *, scale, tq=512):
    rows = qc.shape[0]
    heads = qc.shape[1] // HEAD_SLOT
    tq = _tile(rows, tq, SUBLANES)
    tk = tq

    def body(q_ref, k_ref, v_ref, o_ref, do_ref, lse_ref, cos_ref, sin_ref, dq_ref, delta_ref):
        i = pl.program_id(1)
        q = q_ref[...]
        dov = do_ref[...]
        delta = jnp.sum(dov * o_ref[...], axis=-1, keepdims=True)
        delta_ref[...] = jnp.broadcast_to(delta, (tq, LANES))
        dob = dov.astype(BF16)
        lse_col = lse_ref[:, :1]

        def step(j, dq):
            k0 = pl.multiple_of(j * tk, tk)
            kb = k_ref[pl.ds(k0, tk), :]
            s = _dot_nt(q, kb) * scale
            p = jnp.where(_causal(i, j, tq, tk), jnp.exp(s - lse_col), 0.0)
            dp = _dot_nt(dob, v_ref[pl.ds(k0, tk), :])
            ds = p * (dp - delta)
            return dq + _dot_nn(ds.astype(BF16), kb)

        dq = lax.fori_loop(0, i + 1, step, jnp.zeros((tq, HEAD_SLOT), F32)) * scale
        dq_ref[:, :LANES] = dq[:, :LANES]
        dq_ref[:, LANES:] = _rope(dq[:, LANES:], cos_ref[...], -sin_ref[...])

    qblk = pl.BlockSpec((tq, HEAD_SLOT), lambda h, i: (i, h))
    vblk = pl.BlockSpec((tq, LANES), lambda h, i: (i, h))
    tab = pl.BlockSpec((tq, LANES), lambda h, i: (i, 0))
    return pl.pallas_call(
        body, name="attn_bwd_q", grid=(heads, rows // tq),
        in_specs=[qblk, pl.BlockSpec((rows, HEAD_SLOT), lambda h, i: (0, h)), pl.BlockSpec((rows, LANES), lambda h, i: (0, h)),
                  vblk, vblk, vblk, tab, tab],
        out_specs=[qblk, vblk],
        out_shape=[jax.ShapeDtypeStruct((rows, heads * HEAD_SLOT), F32), jax.ShapeDtypeStruct((rows, heads * LANES), F32)],
        compiler_params=_params(("parallel", "parallel")),
    )(qc, kc, vb, o, do, lse, cos, sin)


def _attn_bwd_kv(qc, kc, vb, do, lse, delta, cos, sin, *, scale, tk=512):
    rows = qc.shape[0]
    heads = qc.shape[1] // HEAD_SLOT
    tk = _tile(rows, tk, SUBLANES)
    tq = tk
    nq = rows // tq

    def body(q_ref, k_ref, v_ref, do_ref, lse_ref, delta_ref, cos_ref, sin_ref, dkv_ref, dkpe_ref):
        j, h = pl.program_id(0), pl.program_id(1)
        kb, vv = k_ref[...], v_ref[...]

        def step(i, carry):
            dk, dv = carry
            q0 = pl.multiple_of(i * tq, tq)
            qb = q_ref[pl.ds(q0, tq), :]
            dob = do_ref[pl.ds(q0, tq), :].astype(BF16)
            s = _dot_nt(qb, kb) * scale
            p = jnp.where(_causal(i, j, tq, tk), jnp.exp(s - lse_ref[pl.ds(q0, tq), :1]), 0.0)
            dv = dv + _dot_tn(p.astype(BF16), dob)
            ds = p * (_dot_nt(dob, vv) - delta_ref[pl.ds(q0, tq), :1])
            dk = dk + _dot_tn(ds.astype(BF16), qb)
            return dk, dv

        dk, dv = lax.fori_loop(j, nq, step, (jnp.zeros((tk, HEAD_SLOT), F32), jnp.zeros((tk, LANES), F32)))
        dkv_ref[:, :LANES] = dk[:, :LANES] * scale
        dkv_ref[:, LANES:] = dv
        part = dk[:, LANES:] * scale

        @pl.when(h == 0)
        def _():
            dkpe_ref[...] = part

        @pl.when(h > 0)
        def _():
            dkpe_ref[...] += part

        @pl.when(h == heads - 1)
        def _():
            dkpe_ref[...] = _rope(dkpe_ref[...], cos_ref[...], -sin_ref[...])

    full_q = pl.BlockSpec((rows, HEAD_SLOT), lambda j, h: (0, h))
    full_v = pl.BlockSpec((rows, LANES), lambda j, h: (0, h))
    tab = pl.BlockSpec((tk, LANES), lambda j, h: (j, 0))
    return pl.pallas_call(
        body, name="attn_bwd_kv", grid=(rows // tk, heads),
        in_specs=[full_q, pl.BlockSpec((tk, HEAD_SLOT), lambda j, h: (j, h)), pl.BlockSpec((tk, LANES), lambda j, h: (j, h)),
                  full_v, full_v, full_v, tab, tab],
        out_specs=[pl.BlockSpec((tk, HEAD_SLOT), lambda j, h: (j, h)), pl.BlockSpec((tk, LANES), lambda j, h: (j, 0))],
        out_shape=[jax.ShapeDtypeStruct((rows, heads * HEAD_SLOT), F32), jax.ShapeDtypeStruct((rows, LANES), F32)],
        compiler_params=_params(("parallel", "arbitrary")),
    )(qc, kc, vb, do, lse, delta, cos, sin)


def _shift_down(x, d):
    row = lax.broadcasted_iota(jnp.int32, x.shape, 0)
    return jnp.where(row >= d, pltpu.roll(x, d, 0), 0.0)


def _shift_up(x, d):
    rows = x.shape[0]
    row = lax.broadcasted_iota(jnp.int32, x.shape, 0)
    return jnp.where(row < rows - d, pltpu.roll(x, rows - d, 0), 0.0)


def _conv3(a, w, b):
    return w[2:3, :] * a + w[1:2, :] * _shift_down(a, 1) + w[0:1, :] * _shift_down(a, 2) + b


def _conv_gate_fwd(a, conv_w, conv_b, *, tc=256):
    rows, f2 = a.shape
    f = f2 // 2
    tc = _tile(f, tc)
    nc = f // tc

    def body(ag_ref, av_ref, wg_ref, wv_ref, bg_ref, bv_ref, o_ref):
        gate = _conv3(ag_ref[...], wg_ref[...], bg_ref[...])
        val = _conv3(av_ref[...], wv_ref[...], bv_ref[...])
        o_ref[...] = (gate * jax.nn.sigmoid(gate) * val).astype(o_ref.dtype)

    return pl.pallas_call(
        body, name="conv_gate_fwd", grid=(nc,),
        in_specs=[pl.BlockSpec((rows, tc), lambda j: (0, j)), pl.BlockSpec((rows, tc), lambda j: (0, j + nc)),
                  pl.BlockSpec((SUBLANES, tc), lambda j: (0, j)), pl.BlockSpec((SUBLANES, tc), lambda j: (0, j + nc)),
                  pl.BlockSpec((1, tc), lambda j: (0, j)), pl.BlockSpec((1, tc), lambda j: (0, j + nc))],
        out_specs=pl.BlockSpec((rows, tc), lambda j: (0, j)),
        out_shape=jax.ShapeDtypeStruct((rows, f), BF16), compiler_params=_params(("parallel",)),
    )(a, a, conv_w, conv_w, conv_b, conv_b)


def _conv_gate_bwd(a, conv_w, conv_b, dg, *, tc=256):
    rows, f2 = a.shape
    f = f2 // 2
    tc = _tile(f, tc)
    nc = f // tc

    def conv_bwd(a_val, w, d_out):
        da = w[2:3, :] * d_out + w[1:2, :] * _shift_up(d_out, 1) + w[0:1, :] * _shift_up(d_out, 2)
        db = jnp.sum(d_out, axis=0, keepdims=True)
        row = lax.broadcasted_iota(jnp.int32, (SUBLANES, a_val.shape[1]), 0)
        dw = jnp.zeros((SUBLANES, a_val.shape[1]), F32)
        for tap in range(3):
            t = jnp.sum(d_out * (_shift_down(a_val, 2 - tap) if tap < 2 else a_val), axis=0, keepdims=True)
            dw = jnp.where(row == tap, t, dw)
        return da, dw, db

    def body(ag_ref, av_ref, wg_ref, wv_ref, bg_ref, bv_ref, dg_ref, da_ref, dw_ref, db_ref):
        ag, av, wg, wv = ag_ref[...], av_ref[...], wg_ref[...], wv_ref[...]
        gate = _conv3(ag, wg, bg_ref[...])
        val = _conv3(av, wv, bv_ref[...])
        sg = jax.nn.sigmoid(gate)
        dgv = dg_ref[...]
        d_gate = dgv * val * sg * (1.0 + gate * (1.0 - sg))
        d_val = dgv * gate * sg
        for half, (a_val, w, d_out) in enumerate(((ag, wg, d_gate), (av, wv, d_val))):
            da, dw, db = conv_bwd(a_val, w, d_out)
            da_ref[half] = da.astype(da_ref.dtype)
            dw_ref[half] = dw
            db_ref[half] = db

    lo = lambda j: (0, j)
    hi = lambda j: (0, j + nc)
    both = lambda j: (0, 0, j)
    return pl.pallas_call(
        body, name="conv_gate_bwd", grid=(nc,),
        in_specs=[pl.BlockSpec((rows, tc), lo), pl.BlockSpec((rows, tc), hi), pl.BlockSpec((SUBLANES, tc), lo),
                  pl.BlockSpec((SUBLANES, tc), hi), pl.BlockSpec((1, tc), lo), pl.BlockSpec((1, tc), hi),
                  pl.BlockSpec((rows, tc), lo)],
        out_specs=[pl.BlockSpec((2, rows, tc), both), pl.BlockSpec((2, SUBLANES, tc), both), pl.BlockSpec((2, 1, tc), both)],
        out_shape=[jax.ShapeDtypeStruct((2, rows, f), BF16), jax.ShapeDtypeStruct((2, SUBLANES, f), F32),
                   jax.ShapeDtypeStruct((2, 1, f), F32)],
        compiler_params=_params(("parallel",)),
    )(a, a, conv_w, conv_w, conv_b, conv_b, dg)


def _wgrad(a, b, rows, cols, row_sharded, name, **kw):
    make, (sr, sc) = _wgrad_blocks(rows, cols, row_sharded)
    tm = kw.pop("tm", _tile(sr, 512))
    tn = kw.pop("tn", _tile(sc, 1024))
    return _matmul(a, b, mode="tn", name=name, tm=tm, tn=tn, out_blocks=make, **kw)


def _block_diag(x):
    nb, g, r, c = x.shape
    eye = jnp.eye(g, dtype=x.dtype)
    return (x[:, :, :, None, :] * eye[None, :, None, :, None]).reshape(nb, g * r, g * c)


def _block_diag_part(x, r, c):
    nb = x.shape[0]
    g = GROUPS_PER_BATCH
    eye = jnp.eye(g, dtype=x.dtype)
    return jnp.sum(x.reshape(nb, g, r, g, c) * eye[None, :, None, :, None], axis=3)


class _NoExchange:
    def __init__(self, ffn):
        self.ffn = ffn

    def ffn_weights(self, after):
        return self.ffn

    def ffn_grads(self, g_down, g_up):
        return None

    def ffn_backward_done(self, after):
        return None


def _local_step(x, posf, target, w, hooks):
    rows, d = x.shape
    width = w["ssm_d"].shape[1]
    qr, kvr = w["mla_q_norm_w"].shape[1], w["mla_kv_norm_w"].shape[1]
    heads = w["mla_w_ukv"].shape[1] // HEAD_SLOT
    f2 = w["ffn_conv_b"].shape[1]
    inp = w["w_in"].shape[1]
    groups = width // SSM_GROUP
    nb = groups // GROUPS_PER_BATCH
    scale = (QK_NOPE_DIM + QK_ROPE_DIM) ** -0.5
    g = {}

    hn = _rmsnorm_fwd(x, w["attn_norm_w"], name="attn_norm")
    proj = _matmul(hn, w["w_in"], mode="nn", name="in_proj")

    ar, ai, bbr, bbi = _s5_params(w["ssm_lambda_re"], w["ssm_lambda_im"], w["ssm_log_dt"], w["ssm_b_re"], w["ssm_b_im"])

    def b_band(bb):
        return _block_diag(bb.reshape(nb, GROUPS_PER_BATCH, SSM_STATE, SSM_GROUP).transpose(0, 1, 3, 2))

    def c_band(c):
        return _block_diag(c.reshape(nb, GROUPS_PER_BATCH, SSM_GROUP, SSM_STATE).transpose(0, 1, 3, 2))

    wb = jnp.concatenate([b_band(bbr), b_band(bbi)], axis=2).astype(BF16)
    wc = jnp.concatenate([c_band(w["ssm_c_re"]), -c_band(w["ssm_c_im"])], axis=1).astype(BF16)
    abar = jnp.concatenate([ar.reshape(nb, 1, STATE_PER_BATCH), ai.reshape(nb, 1, STATE_PER_BATCH)], axis=2)
    states, y_pre, yg = _s5_fwd(proj, wb, wc, w["ssm_d"], abar)
    z = _matmul(yg, w["ssm_w_glu"], mode="nn", name="glu_proj", bias=w["ssm_b_glu"])
    ys = _glu_norm_fwd(y_pre, z, w["ssm_out_norm_w"])

    q_col, kv_col, kpe_col = width // qr, (width + qr) // kvr, (width + qr + kvr) // LANES
    assert width % qr == 0 and (width + qr) % kvr == 0
    qn = _rmsnorm_fwd(proj, w["mla_q_norm_w"], name="q_norm", width=qr, col=q_col)
    kvn = _rmsnorm_fwd(proj, w["mla_kv_norm_w"], name="kv_norm", width=kvr, col=kv_col)
    q = _matmul(qn, w["mla_w_uq"], mode="nn", name="q_proj")
    kv = _matmul(kvn, w["mla_w_ukv"], mode="nn", name="kv_proj")
    half = QK_ROPE_DIM // 2
    inv_freq = ROPE_THETA ** (-jnp.arange(0, QK_ROPE_DIM, 2, dtype=F32) / QK_ROPE_DIM)
    zeros = jnp.zeros((LANES - QK_ROPE_DIM,), F32)
    freq = jnp.concatenate([inv_freq, inv_freq, zeros]).reshape(1, LANES)
    sign = jnp.concatenate([-jnp.ones((half,), F32), jnp.ones((half,), F32), zeros]).reshape(1, LANES)
    cos, sin = _rope_tables(posf, freq, sign)
    qc, kc, vb = _attn_prep(q, kv, proj, kpe_col, cos, sin)
    o, lse = _attn_fwd(qc, kc, vb, scale=scale, tq=ATTN_BLOCK)
    ym = _rmsnorm_fwd(o, w["mla_out_norm_w"], name="mla_out_norm")
    ycat = jnp.concatenate([ys, ym], axis=1)
    h1 = _matmul(ycat, w["w_out"], mode="nn", name="out_proj", add=x)

    hn2 = _rmsnorm_fwd(h1, w["ffn_norm_w"], name="ffn_norm")
    ffn = hooks.ffn_weights(hn2)
    a = _matmul(hn2, ffn["ffn_w_up"], mode="nn", name="ffn_up")
    gated = _conv_gate_fwd(a, ffn["ffn_conv_w"], w["ffn_conv_b"])
    h2 = _matmul(gated, ffn["ffn_w_down"], mode="nn", name="ffn_down", add=h1, tk=2816)
    loss_tile, dh2, g["final_norm_w"] = _final_norm_loss(h2, w["final_norm_w"], target)

    dgated = _matmul(dh2, ffn["ffn_w_down"], mode="nt", name="ffn_down_dx")
    g["ffn_w_down"] = _wgrad(gated, dh2, f2 // 2, d, True, "ffn_down_dw", tm=f2 // 2 // N_CHIPS, tn=512)
    da, dcw, dcb = _conv_gate_bwd(a, ffn["ffn_conv_w"], w["ffn_conv_b"], dgated)
    g["ffn_conv_w"] = jnp.concatenate([dcw[0, :3], dcw[1, :3]], axis=1)
    g["ffn_conv_b"] = jnp.concatenate([dcb[0], dcb[1]], axis=1)
    g["ffn_w_up"] = _wgrad(hn2, da, d, f2, False, "ffn_up_dw", b_split=True, tn=_tile(f2 // N_CHIPS, 1408))
    started = hooks.ffn_grads(g["ffn_w_down"], g["ffn_w_up"])
    ffn_norm_w = w["ffn_norm_w"] if started is None else w["ffn_norm_w"] + started[:1, :1]
    dhn2 = _matmul(da, ffn["ffn_w_up"], mode="nt", name="ffn_up_dx", a_split=True, tk=_tile(f2 // 2, 2816))
    dh1, g["ffn_norm_w"] = _rmsnorm_bwd(h1, ffn_norm_w, dhn2, name="ffn_norm_bwd", add=dh2)

    dycat = _matmul(dh1, w["w_out"], mode="nt", name="out_proj_dx")
    g["w_out"] = _wgrad(ycat, dh1, 2 * width, d, True, "out_proj_dw")
    started = hooks.ffn_backward_done(dycat)
    mla_out_norm_w, ssm_out_norm_w = w["mla_out_norm_w"], w["ssm_out_norm_w"]
    if started is not None:
        mla_out_norm_w, ssm_out_norm_w = mla_out_norm_w + started[:1, :1], ssm_out_norm_w + started[:1, :1]

    do, g["mla_out_norm_w"] = _rmsnorm_bwd(o, mla_out_norm_w, dycat, name="mla_out_norm_bwd", width=width, dy_col=1)
    dq, delta = _attn_bwd_q(qc, kc, vb, o, do, lse, cos, sin, scale=scale, tq=ATTN_BLOCK)
    dkv, dkpe = _attn_bwd_kv(qc, kc, vb, do, lse, delta, cos, sin, scale=scale, tk=ATTN_BLOCK)
    g["mla_w_uq"] = _wgrad(qn, dq, qr, heads * HEAD_SLOT, False, "q_proj_dw")
    dqn = _matmul(dq, w["mla_w_uq"], mode="nt", name="q_proj_dx")
    dcq, g["mla_q_norm_w"] = _rmsnorm_bwd(proj, w["mla_q_norm_w"], dqn, name="q_norm_bwd", width=qr, col=q_col)
    g["mla_w_ukv"] = _wgrad(kvn, dkv, kvr, heads * HEAD_SLOT, False, "kv_proj_dw")
    dkvn = _matmul(dkv, w["mla_w_ukv"], mode="nt", name="kv_proj_dx")
    dckv, g["mla_kv_norm_w"] = _rmsnorm_bwd(proj, w["mla_kv_norm_w"], dkvn, name="kv_norm_bwd", width=kvr, col=kv_col)

    dz, dyg_a, g["ssm_out_norm_w"], g["ssm_b_glu"] = _glu_norm_bwd(y_pre, z, ssm_out_norm_w, dycat)
    dyg_b = _matmul(dz, w["ssm_w_glu"], mode="nt", name="glu_proj_dx")
    g["ssm_w_glu"] = _wgrad(yg, dz, width, width, True, "glu_proj_dw")
    du, dwb, dwc, dabar, g["ssm_d"] = _s5_bwd(proj, states, y_pre, dyg_a, dyg_b, wb, wc, w["ssm_d"], abar)

    def b_unband(x):
        return _block_diag_part(x, SSM_GROUP, SSM_STATE).transpose(0, 1, 3, 2).reshape(groups, SSM_STATE * SSM_GROUP)

    def c_unband(x):
        return _block_diag_part(x, SSM_STATE, SSM_GROUP).transpose(0, 1, 3, 2).reshape(groups, SSM_GROUP, SSM_STATE)

    st = STATE_PER_BATCH
    g["ssm_c_re"] = c_unband(dwc[:, :st, :])
    g["ssm_c_im"] = -c_unband(dwc[:, st:, :])
    d_ar = dabar[:, 0, :st].reshape(groups, SSM_STATE)
    d_ai = dabar[:, 0, st:].reshape(groups, SSM_STATE)
    (g["ssm_lambda_re"], g["ssm_lambda_im"], g["ssm_log_dt"], g["ssm_b_re"], g["ssm_b_im"]) = _s5_params_bwd(
        w["ssm_lambda_re"], w["ssm_lambda_im"], w["ssm_log_dt"], w["ssm_b_re"], w["ssm_b_im"], d_ar, d_ai,
        b_unband(dwb[:, :, :st]), b_unband(dwb[:, :, st:]))

    pad = jnp.zeros((rows, inp - (width + qr + kvr + LANES)), F32)
    dproj = jnp.concatenate([du, dcq, dckv, dkpe, pad], axis=1)
    g["w_in"] = _wgrad(hn, dproj, d, inp, True, "in_proj_dw")
    dhn = _matmul(dproj, w["w_in"], mode="nt", name="in_proj_dx")
    dx, g["attn_norm_w"] = _rmsnorm_bwd(x, w["attn_norm_w"], dhn, name="attn_norm_bwd", add=dh1)
    return loss_tile, dx, g


ANY = pl.BlockSpec(memory_space=pl.ANY)
MESH = pl.DeviceIdType.MESH


def _mesh_pos():
    return lax.axis_index("x"), lax.axis_index("y"), lax.axis_index("c")


def _other_chips(x, y):
    return [(1 - x, y), (x, 1 - y), (1 - x, 1 - y)]


def _remote(src, dst, send_sems, recv_sems, k, to):
    return pltpu.make_async_remote_copy(src_ref=src, dst_ref=dst, send_sem=send_sems.at[k], recv_sem=recv_sems.at[k],
                                        device_id=to, device_id_type=MESH)


def _place_shard(shard, piece_idx, row_sharded, name, out_dtype=BF16, pieces=N_CHIPS):
    rs, cs = shard.shape
    tr = _tile(rs, 256, 2 * SUBLANES)
    rb = rs // tr

    def body(p_ref, x_ref, o_ref):
        o_ref[...] = x_ref[...].astype(o_ref.dtype)

    if row_sharded:
        out_shape, out_map = (pieces * rs, cs), (lambda i, p_ref: (p_ref[0] * rb + i, 0))
    else:
        out_shape, out_map = (rs, pieces * cs), (lambda i, p_ref: (i, p_ref[0]))
    return pl.pallas_call(
        body, name=name, out_shape=jax.ShapeDtypeStruct(out_shape, out_dtype),
        grid_spec=pltpu.PrefetchScalarGridSpec(
            num_scalar_prefetch=1, grid=(rb,), in_specs=[pl.BlockSpec((tr, cs), lambda i, p_ref: (i, 0))],
            out_specs=pl.BlockSpec((tr, cs), out_map)),
        compiler_params=_params(("parallel",)),
    )(piece_idx, shard)


def _gather_weights(placed, name):
    n = len(placed)
    meta = [(row_sharded, direct) for _, row_sharded, direct in placed]
    over_ici, over_d2d = _gather_plans(meta)
    forwarded = [t for t, (_, direct) in enumerate(meta) if not direct]

    def body(*refs):
        outs = refs[n:2 * n]
        send_sems, recv_sems, pass_send_sems, pass_recv_sems = refs[2 * n:]
        first, arrivals = over_ici(outs, send_sems, recv_sems)
        passed, passed_arrivals = over_d2d([outs[t] for t in forwarded], pass_send_sems, pass_recv_sems)
        for cp in first:
            cp.start()
        for t in range(n):
            for j in range(3):
                arrivals[3 * t + j].wait_recv()
                if t in forwarded:
                    passed[3 * forwarded.index(t) + j].start()
        for cp in passed_arrivals:
            cp.wait_recv()
        for cp in first + passed:
            cp.wait_send()

    return pl.pallas_call(
        body, name=name, in_specs=[ANY] * n, out_specs=[ANY] * n,
        out_shape=[jax.ShapeDtypeStruct(arr.shape, arr.dtype) for arr, _, _ in placed],
        input_output_aliases={t: t for t in range(n)},
        scratch_shapes=[pltpu.SemaphoreType.DMA((3 * n,)), pltpu.SemaphoreType.DMA((3 * n,)),
                        pltpu.SemaphoreType.DMA((3 * len(forwarded),)), pltpu.SemaphoreType.DMA((3 * len(forwarded),))],
    )(*[arr for arr, _, _ in placed])


def _gather_plans(meta):
    def window(ref, row_sharded, piece, half):
        r, cc = ref.shape
        if row_sharded:
            rs = r // N_CHIPS
            if half is None:
                return ref.at[pl.ds(piece * rs, rs), :]
            return ref.at[pl.ds(piece * rs + half * (rs // 2), rs // 2), :]
        cs = cc // N_CHIPS
        if half is None:
            return ref.at[:, pl.ds(piece * cs, cs)]
        return ref.at[pl.ds(half * (r // 2), r // 2), pl.ds(piece * cs, cs)]

    def over_ici(refs, send_sems, recv_sems):
        x, y, c = _mesh_pos()
        sends, recvs = [], []
        for t, (row_sharded, direct) in enumerate(meta):
            mine = window(refs[t], row_sharded, 2 * x + y, None if direct else c)
            for j, (px, py) in enumerate(_other_chips(x, y)):
                theirs = window(refs[t], row_sharded, 2 * px + py, None if direct else c)
                sends.append(_remote(mine, mine, send_sems, recv_sems, 3 * t + j, (px, py, c)))
                recvs.append(_remote(theirs, theirs, send_sems, recv_sems, 3 * t + j, (px, py, c)))
        return sends, recvs

    def over_d2d(refs, send_sems, recv_sems):
        x, y, c = _mesh_pos()
        sends, recvs = [], []
        rows = [row_sharded for row_sharded, direct in meta if not direct]
        for t, row_sharded in enumerate(rows):
            for j, (px, py) in enumerate(_other_chips(x, y)):
                got = window(refs[t], row_sharded, 2 * px + py, c)
                other = window(refs[t], row_sharded, 2 * px + py, 1 - c)
                sends.append(_remote(got, got, send_sems, recv_sems, 3 * t + j, (x, y, 1 - c)))
                recvs.append(_remote(other, other, send_sems, recv_sems, 3 * t + j, (x, y, 1 - c)))
        return sends, recvs

    return over_ici, over_d2d


HBM = pl.BlockSpec(memory_space=pltpu.HBM)
SEMAPHORES = pl.BlockSpec(memory_space=pltpu.SEMAPHORE)
DATAFLOW = pltpu.SideEffectType.DATAFLOW_SIDE_EFFECTING


def _start_copies(name, arrays, plan, n_copies, after):
    n = len(arrays)

    def body(*refs):
        sends, _ = plan(refs[:n], refs[n + 1], refs[n + 2])
        for cp in sends:
            cp.start()
        token = refs[2 * n + 3]
        token[...] = jnp.zeros_like(token)

    out = pl.pallas_call(
        body, name=name,
        out_shape=(pltpu.SemaphoreType.DMA((n_copies,)), pltpu.SemaphoreType.DMA((n_copies,)),
                   *[pltpu.HBM(a.shape, a.dtype) for a in arrays], jax.ShapeDtypeStruct((SUBLANES, LANES), F32)),
        in_specs=[HBM] * n + [ANY],
        out_specs=(SEMAPHORES, SEMAPHORES, *[HBM] * n, pl.BlockSpec(memory_space=pltpu.VMEM)),
        input_output_aliases={t: t + 2 for t in range(n)},
        compiler_params=pltpu.CompilerParams(has_side_effects=DATAFLOW),
    )(*[pltpu.with_memory_space_constraint(a, pltpu.HBM) for a in arrays], after)
    return out[0], out[1], list(out[2:2 + n]), out[2 + n]


def _wait_copies(name, started, plan, after):
    send_sems, recv_sems, arrays, _ = started
    n = len(arrays)

    def body(*refs):
        sends, recvs = plan(refs[:n], refs[n], refs[n + 1])
        for cp in sends:
            cp.wait_send()
        for cp in recvs:
            cp.wait_recv()

    out = pl.pallas_call(
        body, name=name, out_shape=[pltpu.HBM(a.shape, a.dtype) for a in arrays],
        in_specs=[HBM] * n + [SEMAPHORES, SEMAPHORES, ANY], out_specs=[HBM] * n,
        input_output_aliases={t: t for t in range(n)},
        compiler_params=pltpu.CompilerParams(has_side_effects=DATAFLOW),
    )(*arrays, send_sems, recv_sems, after)
    return list(out)


def _exchange(name, arrays, out_shapes, plan, n_copies, in_place=False):
    n = len(arrays)

    def body(*refs):
        ins, outs = refs[:n], refs[n:n + len(out_shapes)]
        send_sems, recv_sems = refs[n + len(out_shapes):]
        sends, recvs = plan(ins, outs, send_sems, recv_sems)
        for cp in sends:
            cp.start()
        for cp in recvs:
            cp.wait_recv()
        for cp in sends:
            cp.wait_send()

    return pl.pallas_call(
        body, name=name, in_specs=[ANY] * n, out_specs=[ANY] * len(out_shapes), out_shape=out_shapes,
        input_output_aliases={t: t for t in range(n)} if in_place else {},
        scratch_shapes=[pltpu.SemaphoreType.DMA((n_copies,)), pltpu.SemaphoreType.DMA((n_copies,))],
    )(*arrays)


def _swap_plan(n):
    def plan(refs, send_sems, recv_sems):
        x, y, c = _mesh_pos()
        sends = [_remote(refs[t].at[1 - c], refs[n + t], send_sems, recv_sems, t, (x, y, 1 - c)) for t in range(n)]
        return sends, sends

    return plan


def _scatter_plan(n):
    def plan(refs, send_sems, recv_sems):
        x, y, c = _mesh_pos()
        sends = []
        for t in range(n):
            for j, (px, py) in enumerate(_other_chips(x, y)):
                sends.append(_remote(refs[t].at[2 * px + py], refs[n + t].at[j], send_sems, recv_sems, 3 * t + j, (px, py, c)))
        return sends, sends

    return plan


def _swap_shapes(grads):
    return [jax.ShapeDtypeStruct(g.shape[1:], g.dtype) for g in grads]


def _scatter_shapes(sums):
    return [jax.ShapeDtypeStruct((3,) + s.shape[1:], s.dtype) for s in sums]


def _swap_other_half(grads, name):
    plan = _swap_plan(len(grads))
    return _exchange(name, grads, _swap_shapes(grads), lambda ins, outs, s, r: plan(list(ins) + list(outs), s, r), len(grads))


def _scatter_pieces(sums, name):
    plan = _scatter_plan(len(sums))
    return _exchange(name, sums, _scatter_shapes(sums), lambda ins, outs, s, r: plan(list(ins) + list(outs), s, r),
                     3 * len(sums))


def _join_halves(halves):
    def plan(ins, outs, send_sems, recv_sems):
        x, y, c = _mesh_pos()
        sends = [_remote(outs[t].at[c], outs[t].at[c], send_sems, recv_sems, t, (x, y, 1 - c)) for t in range(len(ins))]
        recvs = [_remote(outs[t].at[1 - c], outs[t].at[1 - c], send_sems, recv_sems, t, (x, y, 1 - c))
                 for t in range(len(ins))]
        return sends, recvs

    shapes = [jax.ShapeDtypeStruct(h.shape, h.dtype) for h in halves]
    return _exchange("grad_join_halves", halves, shapes, plan, len(halves), in_place=True)


def _all_to_all_small(slots):
    def plan(ins, outs, send_sems, recv_sems):
        x, y, c = _mesh_pos()
        mine = outs[0].at[4 * x + 2 * y + c]
        sends, recvs = [], []
        for mask in range(1, 8):
            px, py, pc = x ^ ((mask >> 2) & 1), y ^ ((mask >> 1) & 1), c ^ (mask & 1)
            theirs = outs[0].at[4 * px + 2 * py + pc]
            sends.append(_remote(mine, mine, send_sems, recv_sems, mask - 1, (px, py, pc)))
            recvs.append(_remote(theirs, theirs, send_sems, recv_sems, mask - 1, (px, py, pc)))
        return sends, recvs

    shape = jax.ShapeDtypeStruct(slots.shape, slots.dtype)
    return _exchange("small_grads_all_to_all", [slots], [shape], plan, 7, in_place=True)[0]


def _add_other_half(g4, got, where, name):
    _, pieces, sr, sc = g4.shape
    tr = _tile(sr, 256, 2 * SUBLANES)

    def body(w_ref, a_ref, b_ref, o_ref):
        o_ref[...] = (a_ref[...] + b_ref[...]).astype(o_ref.dtype)

    blk = pl.BlockSpec((None, tr, sc), lambda p, i, w_ref: (p, i, 0))
    return pl.pallas_call(
        body, name=name, out_shape=jax.ShapeDtypeStruct((pieces, sr, sc), BF16),
        grid_spec=pltpu.PrefetchScalarGridSpec(
            num_scalar_prefetch=1, grid=(pieces, sr // tr),
            in_specs=[pl.BlockSpec((None, None, tr, sc), lambda p, i, w_ref: (w_ref[0], p, i, 0)), blk], out_specs=blk),
        compiler_params=_params(("parallel", "parallel")),
    )(where, g4, got)


def _add_pieces(g4, got_half, got_pieces, where, name):
    _, _, sr, sc = g4.shape
    tr = _tile(sr, 256, 2 * SUBLANES)

    def body(w_ref, a_ref, b_ref, r_ref, o_ref):
        acc = a_ref[...] + b_ref[...]
        for j in range(3):
            acc = acc + r_ref[j].astype(F32)
        o_ref[...] = acc

    return pl.pallas_call(
        body, name=name, out_shape=jax.ShapeDtypeStruct((N_CORES, sr, sc), F32),
        grid_spec=pltpu.PrefetchScalarGridSpec(
            num_scalar_prefetch=1, grid=(sr // tr,),
            in_specs=[pl.BlockSpec((None, None, tr, sc), lambda i, w_ref: (w_ref[0], w_ref[1], i, 0)),
                      pl.BlockSpec((None, tr, sc), lambda i, w_ref: (w_ref[1], i, 0)),
                      pl.BlockSpec((3, tr, sc), lambda i, w_ref: (0, i, 0))],
            out_specs=pl.BlockSpec((None, tr, sc), lambda i, w_ref: (w_ref[0], i, 0))),
        compiler_params=_params(("parallel",)),
    )(where, g4, got_half, got_pieces)


def _sum_slots(slots):
    n, rows, lanes = slots.shape
    tr = _tile(rows, 512, SUBLANES)

    def body(s_ref, o_ref):
        acc = s_ref[0]
        for k in range(1, n):
            acc = acc + s_ref[k]
        o_ref[...] = acc

    return pl.pallas_call(
        body, name="small_grads_sum", grid=(rows // tr,),
        in_specs=[pl.BlockSpec((n, tr, lanes), lambda i: (0, i, 0))], out_specs=pl.BlockSpec((tr, lanes), lambda i: (i, 0)),
        out_shape=jax.ShapeDtypeStruct((rows, lanes), F32), compiler_params=_params(("parallel",)),
    )(slots)


def _adamw(w, g, m, v, name):
    rows, cols = w.shape
    tr = _tile(rows, max(SUBLANES, (1 << 19) // max(cols, 1) // SUBLANES * SUBLANES), SUBLANES)

    def body(w_ref, g_ref, m_ref, v_ref, d_ref, nm_ref, nv_ref):
        gv = g_ref[...]
        nm = ADAM_B1 * m_ref[...] + (1.0 - ADAM_B1) * gv
        nv = ADAM_B2 * v_ref[...] + (1.0 - ADAM_B2) * (gv * gv)
        m_hat = nm / (1.0 - ADAM_B1 ** ADAM_STEP)
        v_hat = nv / (1.0 - ADAM_B2 ** ADAM_STEP)
        d_ref[...] = -ADAM_LR * (m_hat / (jnp.sqrt(v_hat) + ADAM_EPS) + ADAM_WD * w_ref[...])
        nm_ref[...] = nm
        nv_ref[...] = nv

    blk = pl.BlockSpec((tr, cols), lambda i: (i, 0))
    return pl.pallas_call(
        body, name=name, grid=(rows // tr,), in_specs=[blk] * 4, out_specs=[blk] * 3,
        out_shape=[jax.ShapeDtypeStruct((rows, cols), F32)] * 3, compiler_params=_params(("parallel",)),
    )(w, g, m, v)


WEIGHTS = ['attn_norm_w', 'w_in', 'ssm_lambda_re', 'ssm_lambda_im', 'ssm_log_dt', 'ssm_b_re', 'ssm_b_im', 'ssm_c_re',
           'ssm_c_im', 'ssm_d', 'ssm_w_glu', 'ssm_b_glu', 'mla_q_norm_w', 'mla_w_uq', 'mla_kv_norm_w', 'mla_w_ukv',
           'ssm_out_norm_w', 'mla_out_norm_w', 'w_out', 'ffn_norm_w', 'ffn_w_up', 'ffn_conv_w', 'ffn_conv_b',
           'ffn_w_down', 'final_norm_w']
SHARDED = {'w_in': True, 'ssm_w_glu': True, 'mla_w_uq': False, 'mla_w_ukv': False, 'w_out': True, 'ffn_w_up': False,
           'ffn_w_down': True}
SMALL = [n for n in WEIGHTS if n not in SHARDED and n != 'ffn_conv_w']
ROPE_PAD = HEAD_SLOT - QK_NOPE_DIM - QK_ROPE_DIM


def _pad_heads(w_uq, heads):
    qr = w_uq.shape[0]
    w3 = w_uq.reshape(qr, heads, QK_NOPE_DIM + QK_ROPE_DIM)
    return jnp.concatenate([w3, jnp.zeros((qr, heads, ROPE_PAD), w_uq.dtype)], axis=2).reshape(qr, heads * HEAD_SLOT)


def _unpad_heads(g_uq, heads):
    qr = g_uq.shape[0]
    return g_uq.reshape(qr, heads, HEAD_SLOT)[:, :, :QK_NOPE_DIM + QK_ROPE_DIM].reshape(qr, -1)


FFN = ['ffn_w_up', 'ffn_w_down']
FFN_GATHER = FFN + ['ffn_conv_w']
FFN_GATHER_META = [(SHARDED[n], False) for n in FFN] + [(False, True)]


class _Overlapped:
    def __init__(self, placed, where, after):
        self.where = where
        self.over_ici, self.over_d2d = _gather_plans(FFN_GATHER_META)
        self.gather = _start_copies("gather_ffn_start", placed, self.over_ici, 3 * len(placed), after)
        self.gather_started = self.gather[3]

    def ffn_weights(self, after):
        arrived = _wait_copies("gather_ffn_wait", self.gather, self.over_ici, after)
        n = len(FFN)
        shapes = [jax.ShapeDtypeStruct(a.shape, a.dtype) for a in arrived[:n]]
        passed = _exchange("gather_ffn_pass", arrived[:n], shapes, lambda ins, outs, s, r: self.over_d2d(outs, s, r),
                           3 * n, in_place=True)
        return dict(zip(FFN_GATHER, list(passed) + arrived[n:]))

    def ffn_grads(self, g_down, g_up):
        grads = [g_up, g_down]
        lands = [lax.empty(s.shape, s.dtype) for s in _swap_shapes(grads)]
        self.swap = _start_copies("grad_ffn_swap_start", grads + lands, _swap_plan(len(grads)), len(grads), g_up)
        return self.swap[3]

    def ffn_backward_done(self, after):
        n = len(FFN)
        out = _wait_copies("grad_ffn_swap_wait", self.swap, _swap_plan(n), after)
        self.grads, self.got_half = out[:n], out[n:]
        sums = [_add_other_half(self.grads[t], self.got_half[t], self.where, "grad_add_half_" + name)
                for t, name in enumerate(FFN)]
        lands = [lax.empty(s.shape, s.dtype) for s in _scatter_shapes(sums)]
        self.scatter = _start_copies("grad_ffn_scatter_start", sums + lands, _scatter_plan(n), 3 * n, after)
        return self.scatter[3]

    def ffn_reduced(self, after):
        n = len(FFN)
        got_pieces = _wait_copies("grad_ffn_scatter_wait", self.scatter, _scatter_plan(n), after)[n:]
        return [_add_pieces(self.grads[t], self.got_half[t], got_pieces[t], self.where, "grad_add_pieces_" + name)
                for t, name in enumerate(FFN)]


def _step(args):
    x, positions, target = args["x"][0], args["positions"], args["loss_target"][0]
    rows = x.shape[0]
    p = {n: args[n] for n in WEIGHTS}
    xi, yi, ci = _mesh_pos()
    piece = 2 * xi + yi

    w_in = p["w_in"][0]
    in_width = w_in.shape[1]
    in_pad = (-in_width) % (2 * LANES)
    heads_here = p["mla_w_uq"].shape[2] // (QK_NOPE_DIM + QK_ROPE_DIM)
    shards = {
        "w_in": jnp.pad(w_in, ((0, 0), (0, in_pad))),
        "ssm_w_glu": p["ssm_w_glu"][0],
        "mla_w_uq": _pad_heads(p["mla_w_uq"][0], heads_here),
        "mla_w_ukv": p["mla_w_ukv"][0],
        "w_out": p["w_out"][0],
        "ffn_w_up": p["ffn_w_up"][0],
        "ffn_w_down": p["ffn_w_down"][0],
    }
    conv_w = jnp.pad(p["ffn_conv_w"][0], ((0, SUBLANES - p["ffn_conv_w"].shape[1]), (0, 0)))
    order = list(SHARDED)
    piece_idx = piece.reshape(1).astype(jnp.int32)
    placed = {n: _place_shard(shards[n], piece_idx, SHARDED[n], "place_" + n) for n in order}
    placed["ffn_conv_w"] = _place_shard(conv_w, piece_idx, False, "place_ffn_conv_w", out_dtype=F32)
    mixer = [n for n in order if n not in FFN]
    w = dict(zip(mixer, _gather_weights([(placed[n], SHARDED[n], False) for n in mixer], "gather_mixer_weights")))
    where = jnp.stack([ci, piece]).astype(jnp.int32)
    hooks = _Overlapped([placed[n] for n in FFN_GATHER], where, after=w["w_in"])
    groups = p["ssm_lambda_re"].shape[1]
    w.update({
        "attn_norm_w": p["attn_norm_w"] + hooks.gather_started[:1, :1],
        "ssm_lambda_re": p["ssm_lambda_re"][0], "ssm_lambda_im": p["ssm_lambda_im"][0],
        "ssm_log_dt": p["ssm_log_dt"].reshape(groups, 1), "ssm_b_re": p["ssm_b_re"].reshape(groups, -1),
        "ssm_b_im": p["ssm_b_im"].reshape(groups, -1), "ssm_c_re": p["ssm_c_re"][0], "ssm_c_im": p["ssm_c_im"][0],
        "ssm_d": p["ssm_d"], "ssm_b_glu": p["ssm_b_glu"], "mla_q_norm_w": p["mla_q_norm_w"],
        "mla_kv_norm_w": p["mla_kv_norm_w"], "ssm_out_norm_w": p["ssm_out_norm_w"], "mla_out_norm_w": p["mla_out_norm_w"],
        "ffn_norm_w": p["ffn_norm_w"], "ffn_conv_b": p["ffn_conv_b"], "final_norm_w": p["final_norm_w"].reshape(1, -1),
    })

    loss_tile, dx, g = _local_step(x, positions.reshape(rows, 1).astype(F32), target, w, hooks)
    loss = lax.psum(loss_tile[0, 0], ("x", "y", "c"))

    g_mixer = [g[n] for n in mixer]
    got_half = _swap_other_half(g_mixer, "grad_swap_halves")
    sums = [_add_other_half(g_mixer[t], got_half[t], where, "grad_add_half_" + n) for t, n in enumerate(mixer)]
    got_pieces = _scatter_pieces(sums, "grad_scatter_pieces")
    halves = {n: _add_pieces(g_mixer[t], got_half[t], got_pieces[t], where, "grad_add_pieces_" + n)
              for t, n in enumerate(mixer)}
    halves.update(zip(FFN, hooks.ffn_reduced(dx)))
    joined = _join_halves([halves[n] for n in order])
    grads = {}
    for t, n in enumerate(order):
        j = joined[t]
        grads[n] = jnp.concatenate([j[0], j[1]], axis=1) if SHARDED[n] else j.reshape(2 * j.shape[1], j.shape[2])
    grads["w_in"] = grads["w_in"][:, :in_width]
    grads["mla_w_uq"] = _unpad_heads(grads["mla_w_uq"], heads_here)

    flat = [g[n].reshape(-1) for n in SMALL] + [g["ffn_conv_w"].reshape(-1)]
    sizes = [f.shape[0] for f in flat]
    total = sum(sizes)
    tile_elems = SUBLANES * LANES
    padded = -(-total // tile_elems) * tile_elems

    def pack(parts):
        parts = list(parts)
        have = sum(q.shape[0] for q in parts)
        return jnp.concatenate(parts + [jnp.zeros((padded - have,), F32)]).reshape(padded // LANES, LANES)

    me_idx = (4 * xi + 2 * yi + ci).reshape(1).astype(jnp.int32)
    slots = _place_shard(pack(flat), me_idx, True, "place_small_grads", out_dtype=F32, pieces=N_CHIPS * N_CORES)
    small_sum = _sum_slots(_all_to_all_small(slots.reshape(N_CHIPS * N_CORES, padded // LANES, LANES)))
    flat_sum = small_sum.reshape(-1)
    offs = [0]
    for s in sizes:
        offs.append(offs[-1] + s)
    for k, n in enumerate(SMALL):
        grads[n] = flat_sum[offs[k]:offs[k + 1]].reshape(p[n].shape)
    taps, cols_here = p["ffn_conv_w"].shape[1], p["ffn_conv_w"].shape[2]
    conv_full = flat_sum[offs[len(SMALL)]:offs[len(SMALL) + 1]].reshape(taps, N_CHIPS * cols_here)
    grads["ffn_conv_w"] = lax.dynamic_slice_in_dim(conv_full, piece * cols_here, cols_here, axis=1)

    delta, new_m, new_v = {}, {}, {}
    for n in list(SHARDED) + ["ffn_conv_w"]:
        shape = p[n].shape
        d2, m2, v2 = _adamw(p[n].reshape(shape[1:]), grads[n], args["m_" + n].reshape(shape[1:]),
                            args["v_" + n].reshape(shape[1:]), "adamw_" + n)
        grads[n] = grads[n].reshape(shape)
        delta[n], new_m[n], new_v[n] = d2.reshape(shape), m2.reshape(shape), v2.reshape(shape)
    d2, m2, v2 = _adamw(pack(p[n].reshape(-1) for n in SMALL), small_sum, pack(args["m_" + n].reshape(-1) for n in SMALL),
                        pack(args["v_" + n].reshape(-1) for n in SMALL), "adamw_small")
    for k, n in enumerate(SMALL):
        for src, dst in ((d2, delta), (m2, new_m), (v2, new_v)):
            dst[n] = src.reshape(-1)[offs[k]:offs[k + 1]].reshape(p[n].shape)

    return (loss, dx[None], *[grads[n] for n in WEIGHTS], *[delta[n] for n in WEIGHTS],
            *[new_m[n] for n in WEIGHTS], *[new_v[n] for n in WEIGHTS])


def kernel(x, positions, attn_norm_w, w_in, ssm_lambda_re, ssm_lambda_im, ssm_log_dt, ssm_b_re, ssm_b_im, ssm_c_re, ssm_c_im, ssm_d, ssm_w_glu, ssm_b_glu, mla_q_norm_w, mla_w_uq, mla_kv_norm_w, mla_w_ukv, ssm_out_norm_w, mla_out_norm_w, w_out, ffn_norm_w, ffn_w_up, ffn_conv_w, ffn_conv_b, ffn_w_down, final_norm_w, loss_target, m_attn_norm_w, m_w_in, m_ssm_lambda_re, m_ssm_lambda_im, m_ssm_log_dt, m_ssm_b_re, m_ssm_b_im, m_ssm_c_re, m_ssm_c_im, m_ssm_d, m_ssm_w_glu, m_ssm_b_glu, m_mla_q_norm_w, m_mla_w_uq, m_mla_kv_norm_w, m_mla_w_ukv, m_ssm_out_norm_w, m_mla_out_norm_w, m_w_out, m_ffn_norm_w, m_ffn_w_up, m_ffn_conv_w, m_ffn_conv_b, m_ffn_w_down, m_final_norm_w, v_attn_norm_w, v_w_in, v_ssm_lambda_re, v_ssm_lambda_im, v_ssm_log_dt, v_ssm_b_re, v_ssm_b_im, v_ssm_c_re, v_ssm_c_im, v_ssm_d, v_ssm_w_glu, v_ssm_b_glu, v_mla_q_norm_w, v_mla_w_uq, v_mla_kv_norm_w, v_mla_w_ukv, v_ssm_out_norm_w, v_mla_out_norm_w, v_w_out, v_ffn_norm_w, v_ffn_w_up, v_ffn_conv_w, v_ffn_conv_b, v_ffn_w_down, v_final_norm_w):
    return _step(dict(locals()))
```

```python
import functools
import math

import jax
import jax.numpy as jnp
from jax import lax
from jax.experimental import pallas as pl
from jax.experimental.pallas import tpu as pltpu

F32 = jnp.float32
BF16 = jnp.bfloat16

SSM_GROUP = 16
SSM_STATE = 64
QK_NOPE_DIM = 128
QK_ROPE_DIM = 64
V_HEAD_DIM = 128
ROPE_THETA = 10000.0
RMS_EPS = 1e-6
ADAM_LR, ADAM_B1, ADAM_B2, ADAM_EPS, ADAM_WD, ADAM_STEP = 0.001, 0.9, 0.999, 1e-08, 0.01, 10

LANES = 128
SUBLANES = 8
VMEM_LIMIT_BYTES = 56 * 1024 * 1024

GROUPS_PER_BATCH = LANES // SSM_GROUP
STATE_PER_BATCH = GROUPS_PER_BATCH * SSM_STATE
HEAD_SLOT = 2 * LANES
NEG_INF = -1e30
ATTN_BLOCK = 512

N_CHIPS = 4
N_CORES = 2


def _tile(n, pref, align=LANES):
    if n <= pref:
        return n
    t = (pref // align) * align
    while t >= align:
        if n % t == 0:
            return t
        t -= align
    return n


def _params(sem):
    return pltpu.CompilerParams(dimension_semantics=sem, vmem_limit_bytes=VMEM_LIMIT_BYTES)


def _dot(a, b, dims):
    return lax.dot_general(a, b, (dims, ((), ())), preferred_element_type=F32)


def _dot_nn(a, b):
    return _dot(a, b, ((1,), (0,)))


def _dot_nt(a, b):
    return _dot(a, b, ((1,), (1,)))


def _dot_tn(a, b):
    return _dot(a, b, ((0,), (0,)))


def _matmul(a, b, *, mode, name, tm=512, tn=1024, tk=2048, bias=None, add=None, out_dtype=F32,
            out_blocks=None, a_split=False, b_split=False):
    if a_split:
        assert mode == "nt"
        a_shape = (a.shape[1], 2 * a.shape[2])
    else:
        a_shape = a.shape
    if b_split:
        assert mode == "tn"
        b_shape = (b.shape[1], 2 * b.shape[2])
    else:
        b_shape = b.shape
    if mode == "nn":
        (m, k), (k2, n) = a_shape, b_shape
    elif mode == "nt":
        (m, k), (n, k2) = a_shape, b_shape
    else:
        (k, m), (k2, n) = a_shape, b_shape
    assert k == k2, (a.shape, b.shape, mode)
    tm, tn, tk = _tile(m, tm, SUBLANES), _tile(n, tn), _tile(k, tk)
    nk = k // tk
    a_spec = {"nn": pl.BlockSpec((tm, tk), lambda i, j, kk: (i, kk)),
              "nt": pl.BlockSpec((tm, tk), lambda i, j, kk: (i, kk)),
              "tn": pl.BlockSpec((tk, tm), lambda i, j, kk: (kk, i))}[mode]
    b_spec = {"nn": pl.BlockSpec((tk, tn), lambda i, j, kk: (kk, j)),
              "nt": pl.BlockSpec((tn, tk), lambda i, j, kk: (j, kk)),
              "tn": pl.BlockSpec((tk, tn), lambda i, j, kk: (kk, j))}[mode]
    if a_split:
        kb = a.shape[2] // tk
        assert a.shape[2] % tk == 0
        a_spec = pl.BlockSpec((None, tm, tk), lambda i, j, kk: (kk // kb, i, kk % kb))
    if b_split:
        nb = b.shape[2] // tn
        assert b.shape[2] % tn == 0
        b_spec = pl.BlockSpec((None, tk, tn), lambda i, j, kk: (j // nb, kk, j % nb))
    dot = {"nn": _dot_nn, "nt": _dot_nt, "tn": _dot_tn}[mode]
    in_specs, operands = [a_spec, b_spec], [a, b]
    if bias is not None:
        in_specs.append(pl.BlockSpec((1, tn), lambda i, j, kk: (0, j)))
        operands.append(bias)
    if add is not None:
        in_specs.append(pl.BlockSpec((tm, tn), lambda i, j, kk: (i, j)))
        operands.append(add)

    def body(*refs):
        a_ref, b_ref = refs[0], refs[1]
        rest = list(refs[2:])
        bias_ref = rest.pop(0) if bias is not None else None
        add_ref = rest.pop(0) if add is not None else None
        o_ref, acc_ref = rest

        def finish(acc):
            if bias_ref is not None:
                acc = acc + bias_ref[...]
            if add_ref is not None:
                acc = acc + add_ref[...]
            o_ref[...] = acc.astype(o_ref.dtype)

        part = dot(a_ref[...].astype(BF16), b_ref[...].astype(BF16))
        if nk == 1:
            finish(part)
        else:
            kk = pl.program_id(2)

            @pl.when(kk == 0)
            def _():
                acc_ref[...] = part

            @pl.when(jnp.logical_and(kk > 0, kk < nk - 1))
            def _():
                acc_ref[...] += part

            @pl.when(kk == nk - 1)
            def _():
                finish(acc_ref[...] + part)

    if out_blocks is None:
        out_shape = jax.ShapeDtypeStruct((m, n), out_dtype)
        out_spec = pl.BlockSpec((tm, tn), lambda i, j, kk: (i, j))
    else:
        shape, block, index_map = out_blocks(tm, tn)
        out_shape = jax.ShapeDtypeStruct(shape, out_dtype)
        out_spec = pl.BlockSpec(block, index_map)
    acc_shape = (tm, tn) if nk > 1 else (SUBLANES, LANES)
    return pl.pallas_call(
        body, name=name, grid=(m // tm, n // tn, nk), in_specs=in_specs, out_specs=out_spec, out_shape=out_shape,
        scratch_shapes=[pltpu.VMEM(acc_shape, F32)],
        compiler_params=_params(("parallel", "parallel", "arbitrary")),
    )(*operands)


def _wgrad_blocks(rows, cols, row_sharded):
    if row_sharded:
        sr, sc = rows // N_CHIPS, cols // N_CORES
    else:
        sr, sc = rows // N_CORES, cols // N_CHIPS

    def make(tm, tn):
        assert sr % tm == 0 and sc % tn == 0, (rows, cols, tm, tn)
        rb, cb = sr // tm, sc // tn
        if row_sharded:
            def index_map(i, j, kk):
                return (j // cb, i // rb, i % rb, j % cb)
        else:
            def index_map(i, j, kk):
                return (i // rb, j // cb, i % rb, j % cb)
        return (N_CORES, N_CHIPS, sr, sc), (None, None, tm, tn), index_map

    return make, (sr, sc)


def _rms_rows(x):
    return lax.rsqrt(jnp.mean(x * x, axis=-1, keepdims=True) + RMS_EPS)


def _rmsnorm_fwd(x, w, *, name, width=None, col=0, out_dtype=BF16, tr=256):
    rows = x.shape[0]
    width = x.shape[1] if width is None else width
    tr = _tile(rows, tr, SUBLANES)

    def body(x_ref, w_ref, o_ref):
        xv = x_ref[...]
        o_ref[...] = (xv * _rms_rows(xv) * w_ref[...]).astype(o_ref.dtype)

    return pl.pallas_call(
        body, name=name, grid=(rows // tr,),
        in_specs=[pl.BlockSpec((tr, width), lambda i: (i, col)), pl.BlockSpec((1, width), lambda i: (0, 0))],
        out_specs=pl.BlockSpec((tr, width), lambda i: (i, 0)),
        out_shape=jax.ShapeDtypeStruct((rows, width), out_dtype),
        compiler_params=_params(("parallel",)),
    )(x, w)


def _rmsnorm_bwd_rows(xv, w, dy):
    r = _rms_rows(xv)
    n = xv * r
    dn = dy * w
    dx = r * (dn - n * jnp.mean(dn * n, axis=-1, keepdims=True))
    return dx, dy * n


def _rmsnorm_bwd(x, w, dy, *, name, width=None, col=0, dy_col=0, add=None, tr=256):
    rows = x.shape[0]
    width = x.shape[1] if width is None else width
    tr = _tile(rows, tr, SUBLANES)
    in_specs = [pl.BlockSpec((tr, width), lambda i: (i, col)), pl.BlockSpec((1, width), lambda i: (0, 0)),
                pl.BlockSpec((tr, width), lambda i: (i, dy_col))]
    operands = [x, w, dy]
    if add is not None:
        in_specs.append(pl.BlockSpec((tr, width), lambda i: (i, 0)))
        operands.append(add)

    def body(*refs):
        x_ref, w_ref, dy_ref = refs[:3]
        add_ref = refs[3] if add is not None else None
        dx_ref, dw_ref = refs[-2:]
        dx, dwp = _rmsnorm_bwd_rows(x_ref[...], w_ref[...], dy_ref[...])
        if add_ref is not None:
            dx = dx + add_ref[...]
        dx_ref[...] = dx
        part = jnp.sum(dwp, axis=0, keepdims=True)

        @pl.when(pl.program_id(0) == 0)
        def _():
            dw_ref[...] = part

        @pl.when(pl.program_id(0) > 0)
        def _():
            dw_ref[...] += part

    return pl.pallas_call(
        body, name=name, grid=(rows // tr,), in_specs=in_specs,
        out_specs=[pl.BlockSpec((tr, width), lambda i: (i, 0)), pl.BlockSpec((1, width), lambda i: (0, 0))],
        out_shape=[jax.ShapeDtypeStruct((rows, width), F32), jax.ShapeDtypeStruct((1, width), F32)],
        compiler_params=_params(("arbitrary",)),
    )(*operands)


def _final_norm_loss(h, w, target, *, tr=256):
    rows, d = h.shape
    tr = _tile(rows, tr, SUBLANES)

    def body(h_ref, w_ref, t_ref, loss_ref, dh_ref, dw_ref):
        hv, wv = h_ref[...], w_ref[...]
        r = _rms_rows(hv)
        n = hv * r
        err = n * wv - t_ref[...]
        d_out = err * (1.0 / d)
        dn = d_out * wv
        dh_ref[...] = r * (dn - n * jnp.mean(dn * n, axis=-1, keepdims=True))
        dw_part = jnp.sum(d_out * n, axis=0, keepdims=True)
        loss_part = jnp.full((SUBLANES, LANES), 0.5 / d, F32) * jnp.sum(err * err)

        @pl.when(pl.program_id(0) == 0)
        def _():
            dw_ref[...] = dw_part
            loss_ref[...] = loss_part

        @pl.when(pl.program_id(0) > 0)
        def _():
            dw_ref[...] += dw_part
            loss_ref[...] += loss_part

    return pl.pallas_call(
        body, name="final_norm_loss", grid=(rows // tr,),
        in_specs=[pl.BlockSpec((tr, d), lambda i: (i, 0)), pl.BlockSpec((1, d), lambda i: (0, 0)),
                  pl.BlockSpec((tr, d), lambda i: (i, 0))],
        out_specs=[pl.BlockSpec((SUBLANES, LANES), lambda i: (0, 0)), pl.BlockSpec((tr, d), lambda i: (i, 0)),
                   pl.BlockSpec((1, d), lambda i: (0, 0))],
        out_shape=[jax.ShapeDtypeStruct((SUBLANES, LANES), F32), jax.ShapeDtypeStruct((rows, d), F32),
                   jax.ShapeDtypeStruct((1, d), F32)],
        compiler_params=_params(("arbitrary",)),
    )(h, w, target)


def _cmul(ar, ai, br, bi):
    return ar * br - ai * bi, ar * bi + ai * br


def _expand_matrix(groups, reps):
    row = lax.broadcasted_iota(jnp.int32, (groups, groups * reps), 0)
    colg = lax.broadcasted_iota(jnp.int32, (groups, groups * reps), 1) // reps
    return (row == colg).astype(F32)


def _dot_exact(a, b, dims):
    return lax.dot_general(a, b, (dims, ((), ())), preferred_element_type=F32, precision=lax.Precision.HIGHEST)


def _s5_discretize(lr, li, dt):
    mag = jnp.exp(lr * dt)
    th = li * dt
    ar, ai = mag * jnp.cos(th), mag * jnp.sin(th)
    nr, ni = ar - 1.0, ai
    den = lr * lr + li * li
    zr = (nr * lr + ni * li) / den
    zi = (ni * lr - nr * li) / den
    return mag, ar, ai, nr, ni, den, zr, zi


def _s5_params(lam_re, lam_im, log_dt, b_re, b_im):
    g, p = lam_re.shape
    ph = b_re.shape[1]

    def body(lr_ref, li_ref, ldt_ref, br_ref, bi_ref, ar_ref, ai_ref, bbr_ref, bbi_ref):
        dt = jnp.exp(ldt_ref[...])
        _, ar, ai, _, _, _, zr, zi = _s5_discretize(lr_ref[...], li_ref[...], dt)
        ar_ref[...] = ar
        ai_ref[...] = ai
        e = _expand_matrix(p, ph // p)
        zr_x = _dot_exact(zr, e, ((1,), (0,)))
        zi_x = _dot_exact(zi, e, ((1,), (0,)))
        bre, bim = br_ref[...], bi_ref[...]
        bbr_ref[...] = zr_x * bre - zi_x * bim
        bbi_ref[...] = zr_x * bim + zi_x * bre

    return pl.pallas_call(
        body, name="s5_params",
        out_shape=[jax.ShapeDtypeStruct((g, p), F32)] * 2 + [jax.ShapeDtypeStruct((g, ph), F32)] * 2,
    )(lam_re, lam_im, log_dt, b_re, b_im)


def _s5_params_bwd(lam_re, lam_im, log_dt, b_re, b_im, d_ar, d_ai, d_bbr, d_bbi):
    g, p = lam_re.shape
    ph = b_re.shape[1]

    def body(lr_ref, li_ref, ldt_ref, br_ref, bi_ref, dar_ref, dai_ref, dbr_ref, dbi_ref,
             dlr_ref, dli_ref, dldt_ref, dbre_ref, dbim_ref):
        lr, li = lr_ref[...], li_ref[...]
        dt = jnp.exp(ldt_ref[...])
        mag, ar, ai, nr, ni, den, zr, zi = _s5_discretize(lr, li, dt)
        e = _expand_matrix(p, ph // p)
        zr_x = _dot_exact(zr, e, ((1,), (0,)))
        zi_x = _dot_exact(zi, e, ((1,), (0,)))
        bre, bim, dbr, dbi = br_ref[...], bi_ref[...], dbr_ref[...], dbi_ref[...]
        dbre_ref[...] = zr_x * dbr + zi_x * dbi
        dbim_ref[...] = zr_x * dbi - zi_x * dbr
        dzr = _dot_exact(bre * dbr + bim * dbi, e, ((1,), (1,)))
        dzi = _dot_exact(bre * dbi - bim * dbr, e, ((1,), (1,)))
        inv = 1.0 / den
        d_nr = (dzr * lr - dzi * li) * inv
        d_ni = (dzr * li + dzi * lr) * inv
        d_den = -(dzr * zr + dzi * zi) * inv
        d_lr = (dzr * nr + dzi * ni) * inv + 2.0 * lr * d_den
        d_li = (dzr * ni - dzi * nr) * inv + 2.0 * li * d_den
        t_ar = dar_ref[...] + d_nr
        t_ai = dai_ref[...] + d_ni
        d_lrdt = t_ar * ar + t_ai * ai
        d_th = t_ai * ar - t_ar * ai
        dlr_ref[...] = d_lr + d_lrdt * dt
        dli_ref[...] = d_li + d_th * dt
        dldt_ref[...] = jnp.sum(d_lrdt * lr + d_th * li, axis=1, keepdims=True) * dt

    return pl.pallas_call(
        body, name="s5_params_bwd",
        out_shape=[jax.ShapeDtypeStruct((g, p), F32)] * 2 + [jax.ShapeDtypeStruct((g, 1), F32)]
        + [jax.ShapeDtypeStruct((g, ph), F32)] * 2,
    )(lam_re, lam_im, log_dt, b_re, b_im, d_ar, d_ai, d_bbr, d_bbi)


def _powers(ar, ai, count):
    out = [(ar, ai)]
    for _ in range(count - 1):
        out.append(_cmul(out[-1][0], out[-1][1], ar, ai))
    return out


def _scan_coefs(ar, ai, reverse):
    w = ar.shape[-1]
    pw = _powers(ar, ai, SUBLANES)
    row = lax.broadcasted_iota(jnp.int32, (SUBLANES, w), 0)
    steps = []
    d = 1
    while d < SUBLANES:
        keep = (row < SUBLANES - d) if reverse else (row >= d)
        pr, pi = pw[d - 1]
        steps.append((d, jnp.where(keep, pr, 0.0), jnp.where(keep, pi, 0.0)))
        d *= 2
    cr = jnp.zeros((SUBLANES, w), F32)
    ci = jnp.zeros((SUBLANES, w), F32)
    for t in range(SUBLANES):
        pr, pi = pw[SUBLANES - 1 - t] if reverse else pw[t]
        cr = jnp.where(row == t, pr, cr)
        ci = jnp.where(row == t, pi, ci)
    return steps, cr, ci


def _scan_tile(xr, xi, carry_r, carry_i, coefs, reverse):
    steps, cr, ci = coefs
    for d, mr, mi in steps:
        shift = SUBLANES - d if reverse else d
        sr, si = pltpu.roll(xr, shift, 0), pltpu.roll(xi, shift, 0)
        pr, pi = _cmul(mr, mi, sr, si)
        xr, xi = xr + pr, xi + pi
    pr, pi = _cmul(cr, ci, carry_r, carry_i)
    return xr + pr, xi + pi


def _gelu(x):
    c = math.sqrt(2.0 / math.pi)
    return 0.5 * x * (1.0 + jnp.tanh(c * (x + 0.044715 * x * x * x)))


def _gelu_grad(x):
    c = math.sqrt(2.0 / math.pi)
    t = jnp.tanh(c * (x + 0.044715 * x * x * x))
    return 0.5 * (1.0 + t) + 0.5 * x * (1.0 - t * t) * c * (1.0 + 3.0 * 0.044715 * x * x)


def _s5_fwd(proj, wb, wc, d_skip, abar):
    rows = proj.shape[0]
    nb = wb.shape[0]
    s2 = 2 * STATE_PER_BATCH
    st = STATE_PER_BATCH
    chunk = _tile(rows, 512, SUBLANES)

    def body(u_ref, wb_ref, wc_ref, d_ref, a_ref, s_ref, y_ref, yg_ref):
        for c0 in range(0, rows, chunk):
            s_ref[pl.ds(c0, chunk), :] = _dot_nn(u_ref[pl.ds(c0, chunk), :].astype(BF16), wb_ref[...])
        av = a_ref[...]
        coefs = _scan_coefs(av[:, :st], av[:, st:], reverse=False)

        def tile(b, carry):
            r0 = pl.multiple_of(b * SUBLANES, SUBLANES)
            xr, xi = _scan_tile(s_ref[pl.ds(r0, SUBLANES), :st], s_ref[pl.ds(r0, SUBLANES), st:], carry[0], carry[1],
                                coefs, False)
            s_ref[pl.ds(r0, SUBLANES), :st] = xr
            s_ref[pl.ds(r0, SUBLANES), st:] = xi
            return xr[SUBLANES - 1:, :], xi[SUBLANES - 1:, :]

        zero = jnp.zeros((1, st), F32)
        lax.fori_loop(0, rows // SUBLANES, tile, (zero, zero))
        for c0 in range(0, rows, chunk):
            y = _dot_nn(s_ref[pl.ds(c0, chunk), :].astype(BF16), wc_ref[...]) + d_ref[...] * u_ref[pl.ds(c0, chunk), :]
            y_ref[pl.ds(c0, chunk), :] = y
            yg_ref[pl.ds(c0, chunk), :] = _gelu(y).astype(BF16)

    return pl.pallas_call(
        body, name="s5_fwd", grid=(nb,),
        in_specs=[pl.BlockSpec((rows, LANES), lambda j: (0, j)), pl.BlockSpec((None, LANES, s2), lambda j: (j, 0, 0)),
                  pl.BlockSpec((None, s2, LANES), lambda j: (j, 0, 0)), pl.BlockSpec((1, LANES), lambda j: (0, j)),
                  pl.BlockSpec((None, 1, s2), lambda j: (j, 0, 0))],
        out_specs=[pl.BlockSpec((rows, s2), lambda j: (0, j)), pl.BlockSpec((rows, LANES), lambda j: (0, j)),
                   pl.BlockSpec((rows, LANES), lambda j: (0, j))],
        out_shape=[jax.ShapeDtypeStruct((rows, nb * s2), F32), jax.ShapeDtypeStruct((rows, nb * LANES), F32),
                   jax.ShapeDtypeStruct((rows, nb * LANES), BF16)],
        compiler_params=_params(("parallel",)),
    )(proj, wb, wc, d_skip, abar)


def _s5_bwd(proj, states, y_pre, dyg_a, dyg_b, wb, wc, d_skip, abar):
    rows = proj.shape[0]
    nb = wb.shape[0]
    s2 = 2 * STATE_PER_BATCH
    st = STATE_PER_BATCH
    chunk = _tile(rows, 512, SUBLANES)
    n_tiles = rows // SUBLANES

    def body(u_ref, s_ref, y_ref, ga_ref, gb_ref, wb_ref, wc_ref, d_ref, a_ref,
             du_ref, dwb_ref, dwc_ref, da_ref, dd_ref, ds_ref, dy_ref):
        dy_ref[...] = (ga_ref[...] + gb_ref[...]) * _gelu_grad(y_ref[...])
        dd_ref[...] = jnp.sum(dy_ref[...] * u_ref[...], axis=0, keepdims=True)
        for c0 in range(0, rows, chunk):
            ds_ref[pl.ds(c0, chunk), :] = _dot_nt(dy_ref[pl.ds(c0, chunk), :].astype(BF16), wc_ref[...])
        dwc_ref[...] = _dot_tn(s_ref[...].astype(BF16), dy_ref[...].astype(BF16))
        av = a_ref[...]
        coefs = _scan_coefs(av[:, :st], -av[:, st:], reverse=True)
        row = lax.broadcasted_iota(jnp.int32, (SUBLANES, st), 0)

        def tile(k, carry):
            cr, ci, acc_r, acc_i = carry
            b = n_tiles - 1 - k
            r0 = pl.multiple_of(b * SUBLANES, SUBLANES)
            rp = pl.multiple_of(jnp.maximum(b - 1, 0) * SUBLANES, SUBLANES)
            xr, xi = _scan_tile(ds_ref[pl.ds(r0, SUBLANES), :st], ds_ref[pl.ds(r0, SUBLANES), st:], cr, ci, coefs, True)
            ds_ref[pl.ds(r0, SUBLANES), :st] = xr
            ds_ref[pl.ds(r0, SUBLANES), st:] = xi
            first = jnp.where(b > 0, 1.0, 0.0)
            pr = jnp.where(row == 0, pltpu.roll(s_ref[pl.ds(rp, SUBLANES), :st], 1, 0) * first,
                           pltpu.roll(s_ref[pl.ds(r0, SUBLANES), :st], 1, 0))
            pi = jnp.where(row == 0, pltpu.roll(s_ref[pl.ds(rp, SUBLANES), st:], 1, 0) * first,
                           pltpu.roll(s_ref[pl.ds(r0, SUBLANES), st:], 1, 0))
            acc_r = acc_r + pr * xr + pi * xi
            acc_i = acc_i + pr * xi - pi * xr
            return xr[:1, :], xi[:1, :], acc_r, acc_i

        zero = jnp.zeros((1, st), F32)
        zacc = jnp.zeros((SUBLANES, st), F32)
        _, _, acc_r, acc_i = lax.fori_loop(0, n_tiles, tile, (zero, zero, zacc, zacc))
        da_ref[:, :st] = jnp.sum(acc_r, axis=0, keepdims=True)
        da_ref[:, st:] = jnp.sum(acc_i, axis=0, keepdims=True)
        for c0 in range(0, rows, chunk):
            du_ref[pl.ds(c0, chunk), :] = (_dot_nt(ds_ref[pl.ds(c0, chunk), :].astype(BF16), wb_ref[...])
                                           + d_ref[...] * dy_ref[pl.ds(c0, chunk), :])
        dwb_ref[...] = _dot_tn(u_ref[...].astype(BF16), ds_ref[...].astype(BF16))

    col = pl.BlockSpec((rows, LANES), lambda j: (0, j))
    return pl.pallas_call(
        body, name="s5_bwd", grid=(nb,),
        in_specs=[col, pl.BlockSpec((rows, s2), lambda j: (0, j)), col, col, col,
                  pl.BlockSpec((None, LANES, s2), lambda j: (j, 0, 0)), pl.BlockSpec((None, s2, LANES), lambda j: (j, 0, 0)),
                  pl.BlockSpec((1, LANES), lambda j: (0, j)), pl.BlockSpec((None, 1, s2), lambda j: (j, 0, 0))],
        out_specs=[col, pl.BlockSpec((None, LANES, s2), lambda j: (j, 0, 0)),
                   pl.BlockSpec((None, s2, LANES), lambda j: (j, 0, 0)), pl.BlockSpec((None, 1, s2), lambda j: (j, 0, 0)),
                   pl.BlockSpec((1, LANES), lambda j: (0, j))],
        out_shape=[jax.ShapeDtypeStruct((rows, nb * LANES), F32), jax.ShapeDtypeStruct((nb, LANES, s2), F32),
                   jax.ShapeDtypeStruct((nb, s2, LANES), F32), jax.ShapeDtypeStruct((nb, 1, s2), F32),
                   jax.ShapeDtypeStruct((1, nb * LANES), F32)],
        scratch_shapes=[pltpu.VMEM((rows, s2), F32), pltpu.VMEM((rows, LANES), F32)],
        compiler_params=_params(("parallel",)),
    )(proj, states, y_pre, dyg_a, dyg_b, wb, wc, d_skip, abar)


def _glu_norm_fwd(y_pre, z, w, *, tr=256):
    rows, width = y_pre.shape
    tr = _tile(rows, tr, SUBLANES)

    def body(y_ref, z_ref, w_ref, o_ref):
        v = _gelu(y_ref[...]) * jax.nn.sigmoid(z_ref[...])
        o_ref[...] = (v * _rms_rows(v) * w_ref[...]).astype(o_ref.dtype)

    blk = pl.BlockSpec((tr, width), lambda i: (i, 0))
    return pl.pallas_call(
        body, name="glu_norm_fwd", grid=(rows // tr,),
        in_specs=[blk, blk, pl.BlockSpec((1, width), lambda i: (0, 0))], out_specs=blk,
        out_shape=jax.ShapeDtypeStruct((rows, width), BF16), compiler_params=_params(("parallel",)),
    )(y_pre, z, w)


def _glu_norm_bwd(y_pre, z, w, dycat, *, tr=256):
    rows, width = y_pre.shape
    tr = _tile(rows, tr, SUBLANES)

    def body(y_ref, z_ref, w_ref, dy_ref, dz_ref, dg_ref, dw_ref, db_ref):
        yg = _gelu(y_ref[...])
        sg = jax.nn.sigmoid(z_ref[...])
        dv, dwp = _rmsnorm_bwd_rows(yg * sg, w_ref[...], dy_ref[...])
        dz = dv * yg * sg * (1.0 - sg)
        dz_ref[...] = dz
        dg_ref[...] = dv * sg
        dw_part = jnp.sum(dwp, axis=0, keepdims=True)
        db_part = jnp.sum(dz, axis=0, keepdims=True)

        @pl.when(pl.program_id(0) == 0)
        def _():
            dw_ref[...] = dw_part
            db_ref[...] = db_part

        @pl.when(pl.program_id(0) > 0)
        def _():
            dw_ref[...] += dw_part
            db_ref[...] += db_part

    blk = pl.BlockSpec((tr, width), lambda i: (i, 0))
    vec = pl.BlockSpec((1, width), lambda i: (0, 0))
    return pl.pallas_call(
        body, name="glu_norm_bwd", grid=(rows // tr,), in_specs=[blk, blk, vec, blk], out_specs=[blk, blk, vec, vec],
        out_shape=[jax.ShapeDtypeStruct((rows, width), F32)] * 2 + [jax.ShapeDtypeStruct((1, width), F32)] * 2,
        compiler_params=_params(("arbitrary",)),
    )(y_pre, z, w, dycat)


def _rope_tables(pos, freq, sign):
    rows = pos.shape[0]

    def body(p_ref, f_ref, s_ref, cos_ref, sin_ref):
        ang = p_ref[...] * f_ref[...]
        cos_ref[...] = jnp.cos(ang)
        sin_ref[...] = jnp.sin(ang) * s_ref[...]

    return pl.pallas_call(body, name="rope_tables", out_shape=[jax.ShapeDtypeStruct((rows, LANES), F32)] * 2)(pos, freq, sign)


def _rope(x, cos, sin_signed):
    lane = lax.broadcasted_iota(jnp.int32, x.shape, 1)
    half = QK_ROPE_DIM // 2
    swapped = jnp.where(lane < half, pltpu.roll(x, LANES - half, 1), pltpu.roll(x, half, 1))
    return x * cos + swapped * sin_signed


def _attn_prep(q, kv, proj, kpe_col, cos, sin, *, tr=256):
    rows = q.shape[0]
    heads = q.shape[1] // HEAD_SLOT
    tr = _tile(rows, tr, SUBLANES)

    def body(q_ref, kv_ref, kpe_ref, cos_ref, sin_ref, qc_ref, kc_ref, v_ref):
        c, s = cos_ref[...], sin_ref[...]
        qc_ref[:, :LANES] = q_ref[:, :LANES].astype(BF16)
        qc_ref[:, LANES:] = _rope(q_ref[:, LANES:], c, s).astype(BF16)
        kc_ref[:, :LANES] = kv_ref[:, :LANES].astype(BF16)
        kc_ref[:, LANES:] = _rope(kpe_ref[...], c, s).astype(BF16)
        v_ref[...] = kv_ref[:, LANES:].astype(BF16)

    slot = pl.BlockSpec((tr, HEAD_SLOT), lambda i, h: (i, h))
    tab = pl.BlockSpec((tr, LANES), lambda i, h: (i, 0))
    return pl.pallas_call(
        body, name="attn_prep", grid=(rows // tr, heads),
        in_specs=[slot, slot, pl.BlockSpec((tr, LANES), lambda i, h: (i, kpe_col)), tab, tab],
        out_specs=[slot, slot, pl.BlockSpec((tr, LANES), lambda i, h: (i, h))],
        out_shape=[jax.ShapeDtypeStruct((rows, heads * HEAD_SLOT), BF16)] * 2
        + [jax.ShapeDtypeStruct((rows, heads * LANES), BF16)],
        compiler_params=_params(("parallel", "parallel")),
    )(q, kv, proj, cos, sin)


def _causal(i, j, tq, tk):
    qpos = i * tq + lax.broadcasted_iota(jnp.int32, (tq, tk), 0)
    kpos = j * tk + lax.broadcasted_iota(jnp.int32, (tq, tk), 1)
    return kpos <= qpos


def _attn_fwd(qc, kc, vb, *, scale, tq=512):
    rows = qc.shape[0]
    heads = qc.shape[1] // HEAD_SLOT
    tq = _tile(rows, tq, SUBLANES)
    tk = tq

    def body(q_ref, k_ref, v_ref, o_ref, lse_ref):
        i = pl.program_id(1)
        q = q_ref[...]

        def step(j, carry):
            m, l, acc = carry
            k0 = pl.multiple_of(j * tk, tk)
            s = _dot_nt(q, k_ref[pl.ds(k0, tk), :]) * scale
            s = jnp.where(_causal(i, j, tq, tk), s, NEG_INF)
            m_new = jnp.maximum(m, jnp.max(s, axis=-1, keepdims=True))
            p = jnp.exp(s - m_new)
            alpha = jnp.exp(m - m_new)
            l = alpha * l + jnp.sum(p, axis=-1, keepdims=True)
            acc = alpha * acc + _dot_nn(p.astype(BF16), v_ref[pl.ds(k0, tk), :])
            return m_new, l, acc

        init = (jnp.full((tq, 1), NEG_INF, F32), jnp.zeros((tq, 1), F32), jnp.zeros((tq, LANES), F32))
        m, l, acc = lax.fori_loop(0, i + 1, step, init)
        o_ref[...] = acc / l
        lse_ref[...] = jnp.broadcast_to(m + jnp.log(l), (tq, LANES))

    return pl.pallas_call(
        body, name="attn_fwd", grid=(heads, rows // tq),
        in_specs=[pl.BlockSpec((tq, HEAD_SLOT), lambda h, i: (i, h)), pl.BlockSpec((rows, HEAD_SLOT), lambda h, i: (0, h)),
                  pl.BlockSpec((rows, LANES), lambda h, i: (0, h))],
        out_specs=[pl.BlockSpec((tq, LANES), lambda h, i: (i, h))] * 2,
        out_shape=[jax.ShapeDtypeStruct((rows, heads * LANES), F32)] * 2,
        compiler_params=_params(("parallel", "parallel")),
    )(qc, kc, vb)


def _attn_bwd_q(qc, kc, vb, o, do, lse, cos, sin, *, scale, tq=512):
    rows = qc.shape[0]
    heads = qc.shape[1] // HEAD_SLOT
    tq = _tile(rows, tq, SUBLANES)
    tk = tq

    def body(q_ref, k_ref, v_ref, o_ref, do_ref, lse_ref, cos_ref, sin_ref, dq_ref, delta_ref):
        i = pl.program_id(1)
        q = q_ref[...]
        dov = do_ref[...]
        delta = jnp.sum(dov * o_ref[...], axis=-1, keepdims=True)
        delta_ref[...] = jnp.broadcast_to(delta, (tq, LANES))
        dob = dov.astype(BF16)
        lse_col = lse_ref[:, :1]

        def step(j, dq):
            k0 = pl.multiple_of(j * tk, tk)
            kb = k_ref[pl.ds(k0, tk), :]
            s = _dot_nt(q, kb) * scale
            p = jnp.where(_causal(i, j, tq, tk), jnp.exp(s - lse_col), 0.0)
            dp = _dot_nt(dob, v_ref[pl.ds(k0, tk), :])
            ds = p * (dp - delta)
            return dq + _dot_nn(ds.astype(BF16), kb)

        dq = lax.fori_loop(0, i + 1, step, jnp.zeros((tq, HEAD_SLOT), F32)) * scale
        dq_ref[:, :LANES] = dq[:, :LANES]
        dq_ref[:, LANES:] = _rope(dq[:, LANES:], cos_ref[...], -sin_ref[...])

    qblk = pl.BlockSpec((tq, HEAD_SLOT), lambda h, i: (i, h))
    vblk = pl.BlockSpec((tq, LANES), lambda h, i: (i, h))
    tab = pl.BlockSpec((tq, LANES), lambda h, i: (i, 0))
    return pl.pallas_call(
        body, name="attn_bwd_q", grid=(heads, rows // tq),
        in_specs=[qblk, pl.BlockSpec((rows, HEAD_SLOT), lambda h, i: (0, h)), pl.BlockSpec((rows, LANES), lambda h, i: (0, h)),
                  vblk, vblk, vblk, tab, tab],
        out_specs=[qblk, vblk],
        out_shape=[jax.ShapeDtypeStruct((rows, heads * HEAD_SLOT), F32), jax.ShapeDtypeStruct((rows, heads * LANES), F32)],
        compiler_params=_params(("parallel", "parallel")),
    )(qc, kc, vb, o, do, lse, cos, sin)


def _attn_bwd_kv(qc, kc, vb, do, lse, delta, cos, sin, *, scale, tk=512):
    rows = qc.shape[0]
    heads = qc.shape[1] // HEAD_SLOT
    tk = _tile(rows, tk, SUBLANES)
    tq = tk
    nq = rows // tq

    def body(q_ref, k_ref, v_ref, do_ref, lse_ref, delta_ref, cos_ref, sin_ref, dkv_ref, dkpe_ref):
        j, h = pl.program_id(0), pl.program_id(1)
        kb, vv = k_ref[...], v_ref[...]

        def step(i, carry):
            dk, dv = carry
            q0 = pl.multiple_of(i * tq, tq)
            qb = q_ref[pl.ds(q0, tq), :]
            dob = do_ref[pl.ds(q0, tq), :].astype(BF16)
            s = _dot_nt(qb, kb) * scale
            p = jnp.where(_causal(i, j, tq, tk), jnp.exp(s - lse_ref[pl.ds(q0, tq), :1]), 0.0)
            dv = dv + _dot_tn(p.astype(BF16), dob)
            ds = p * (_dot_nt(dob, vv) - delta_ref[pl.ds(q0, tq), :1])
            dk = dk + _dot_tn(ds.astype(BF16), qb)
            return dk, dv

        dk, dv = lax.fori_loop(j, nq, step, (jnp.zeros((tk, HEAD_SLOT), F32), jnp.zeros((tk, LANES), F32)))
        dkv_ref[:, :LANES] = dk[:, :LANES] * scale
        dkv_ref[:, LANES:] = dv
        part = dk[:, LANES:] * scale

        @pl.when(h == 0)
        def _():
            dkpe_ref[...] = part

        @pl.when(h > 0)
        def _():
            dkpe_ref[...] += part

        @pl.when(h == heads - 1)
        def _():
            dkpe_ref[...] = _rope(dkpe_ref[...], cos_ref[...], -sin_ref[...])

    full_q = pl.BlockSpec((rows, HEAD_SLOT), lambda j, h: (0, h))
    full_v = pl.BlockSpec((rows, LANES), lambda j, h: (0, h))
    tab = pl.BlockSpec((tk, LANES), lambda j, h: (j, 0))
    return pl.pallas_call(
        body, name="attn_bwd_kv", grid=(rows // tk, heads),
        in_specs=[full_q, pl.BlockSpec((tk, HEAD_SLOT), lambda j, h: (j, h)), pl.BlockSpec((tk, LANES), lambda j, h: (j, h)),
                  full_v, full_v, full_v, tab, tab],
        out_specs=[pl.BlockSpec((tk, HEAD_SLOT), lambda j, h: (j, h)), pl.BlockSpec((tk, LANES), lambda j, h: (j, 0))],
        out_shape=[jax.ShapeDtypeStruct((rows, heads * HEAD_SLOT), F32), jax.ShapeDtypeStruct((rows, LANES), F32)],
        compiler_params=_params(("parallel", "arbitrary")),
    )(qc, kc, vb, do, lse, delta, cos, sin)


def _shift_down(x, d):
    row = lax.broadcasted_iota(jnp.int32, x.shape, 0)
    return jnp.where(row >= d, pltpu.roll(x, d, 0), 0.0)


def _shift_up(x, d):
    rows = x.shape[0]
    row = lax.broadcasted_iota(jnp.int32, x.shape, 0)
    return jnp.where(row < rows - d, pltpu.roll(x, rows - d, 0), 0.0)


def _conv3(a, w, b):
    return w[2:3, :] * a + w[1:2, :] * _shift_down(a, 1) + w[0:1, :] * _shift_down(a, 2) + b


def _conv_gate_fwd(a, conv_w, conv_b, *, tc=256):
    rows, f2 = a.shape
    f = f2 // 2
    tc = _tile(f, tc)
    nc = f // tc

    def body(ag_ref, av_ref, wg_ref, wv_ref, bg_ref, bv_ref, o_ref):
        gate = _conv3(ag_ref[...], wg_ref[...], bg_ref[...])
        val = _conv3(av_ref[...], wv_ref[...], bv_ref[...])
        o_ref[...] = (gate * jax.nn.sigmoid(gate) * val).astype(o_ref.dtype)

    return pl.pallas_call(
        body, name="conv_gate_fwd", grid=(nc,),
        in_specs=[pl.BlockSpec((rows, tc), lambda j: (0, j)), pl.BlockSpec((rows, tc), lambda j: (0, j + nc)),
                  pl.BlockSpec((SUBLANES, tc), lambda j: (0, j)), pl.BlockSpec((SUBLANES, tc), lambda j: (0, j + nc)),
                  pl.BlockSpec((1, tc), lambda j: (0, j)), pl.BlockSpec((1, tc), lambda j: (0, j + nc))],
        out_specs=pl.BlockSpec((rows, tc), lambda j: (0, j)),
        out_shape=jax.ShapeDtypeStruct((rows, f), BF16), compiler_params=_params(("parallel",)),
    )(a, a, conv_w, conv_w, conv_b, conv_b)


def _conv_gate_bwd(a, conv_w, conv_b, dg, *, tc=256):
    rows, f2 = a.shape
    f = f2 // 2
    tc = _tile(f, tc)
    nc = f // tc

    def conv_bwd(a_val, w, d_out):
        da = w[2:3, :] * d_out + w[1:2, :] * _shift_up(d_out, 1) + w[0:1, :] * _shift_up(d_out, 2)
        db = jnp.sum(d_out, axis=0, keepdims=True)
        row = lax.broadcasted_iota(jnp.int32, (SUBLANES, a_val.shape[1]), 0)
        dw = jnp.zeros((SUBLANES, a_val.shape[1]), F32)
        for tap in range(3):
            t = jnp.sum(d_out * (_shift_down(a_val, 2 - tap) if tap < 2 else a_val), axis=0, keepdims=True)
            dw = jnp.where(row == tap, t, dw)
        return da, dw, db

    def body(ag_ref, av_ref, wg_ref, wv_ref, bg_ref, bv_ref, dg_ref, da_ref, dw_ref, db_ref):
        ag, av, wg, wv = ag_ref[...], av_ref[...], wg_ref[...], wv_ref[...]
        gate = _conv3(ag, wg, bg_ref[...])
        val = _conv3(av, wv, bv_ref[...])
        sg = jax.nn.sigmoid(gate)
        dgv = dg_ref[...]
        d_gate = dgv * val * sg * (1.0 + gate * (1.0 - sg))
        d_val = dgv * gate * sg
        for half, (a_val, w, d_out) in enumerate(((ag, wg, d_gate), (av, wv, d_val))):
            da, dw, db = conv_bwd(a_val, w, d_out)
            da_ref[half] = da.astype(da_ref.dtype)
            dw_ref[half] = dw
            db_ref[half] = db

    lo = lambda j: (0, j)
    hi = lambda j: (0, j + nc)
    both = lambda j: (0, 0, j)
    return pl.pallas_call(
        body, name="conv_gate_bwd", grid=(nc,),
        in_specs=[pl.BlockSpec((rows, tc), lo), pl.BlockSpec((rows, tc), hi), pl.BlockSpec((SUBLANES, tc), lo),
                  pl.BlockSpec((SUBLANES, tc), hi), pl.BlockSpec((1, tc), lo), pl.BlockSpec((1, tc), hi),
                  pl.BlockSpec((rows, tc), lo)],
        out_specs=[pl.BlockSpec((2, rows, tc), both), pl.BlockSpec((2, SUBLANES, tc), both), pl.BlockSpec((2, 1, tc), both)],
        out_shape=[jax.ShapeDtypeStruct((2, rows, f), BF16), jax.ShapeDtypeStruct((2, SUBLANES, f), F32),
                   jax.ShapeDtypeStruct((2, 1, f), F32)],
        compiler_params=_params(("parallel",)),
    )(a, a, conv_w, conv_w, conv_b, conv_b, dg)


def _wgrad(a, b, rows, cols, row_sharded, name, **kw):
    make, (sr, sc) = _wgrad_blocks(rows, cols, row_sharded)
    tm = kw.pop("tm", _tile(sr, 512))
    tn = kw.pop("tn", _tile(sc, 1024))
    return _matmul(a, b, mode="tn", name=name, tm=tm, tn=tn, out_blocks=make, **kw)


def _block_diag(x):
    nb, g, r, c = x.shape
    eye = jnp.eye(g, dtype=x.dtype)
    return (x[:, :, :, None, :] * eye[None, :, None, :, None]).reshape(nb, g * r, g * c)


def _block_diag_part(x, r, c):
    nb = x.shape[0]
    g = GROUPS_PER_BATCH
    eye = jnp.eye(g, dtype=x.dtype)
    return jnp.sum(x.reshape(nb, g, r, g, c) * eye[None, :, None, :, None], axis=3)


class _NoExchange:
    def __init__(self, ffn):
        self.ffn = ffn

    def ffn_weights(self, after):
        return self.ffn

    def ffn_grads(self, g_down, g_up, after):
        return None

    def ffn_backward_done(self, after):
        return None


def _local_step(x, posf, target, w, hooks):
    rows, d = x.shape
    width = w["ssm_d"].shape[1]
    qr, kvr = w["mla_q_norm_w"].shape[1], w["mla_kv_norm_w"].shape[1]
    heads = w["mla_w_ukv"].shape[1] // HEAD_SLOT
    f2 = w["ffn_conv_b"].shape[1]
    inp = w["w_in"].shape[1]
    groups = width // SSM_GROUP
    nb = groups // GROUPS_PER_BATCH
    scale = (QK_NOPE_DIM + QK_ROPE_DIM) ** -0.5
    g = {}

    hn = _rmsnorm_fwd(x, w["attn_norm_w"], name="attn_norm")
    proj = _matmul(hn, w["w_in"], mode="nn", name="in_proj")

    ar, ai, bbr, bbi = _s5_params(w["ssm_lambda_re"], w["ssm_lambda_im"], w["ssm_log_dt"], w["ssm_b_re"], w["ssm_b_im"])

    def b_band(bb):
        return _block_diag(bb.reshape(nb, GROUPS_PER_BATCH, SSM_STATE, SSM_GROUP).transpose(0, 1, 3, 2))

    def c_band(c):
        return _block_diag(c.reshape(nb, GROUPS_PER_BATCH, SSM_GROUP, SSM_STATE).transpose(0, 1, 3, 2))

    wb = jnp.concatenate([b_band(bbr), b_band(bbi)], axis=2).astype(BF16)
    wc = jnp.concatenate([c_band(w["ssm_c_re"]), -c_band(w["ssm_c_im"])], axis=1).astype(BF16)
    abar = jnp.concatenate([ar.reshape(nb, 1, STATE_PER_BATCH), ai.reshape(nb, 1, STATE_PER_BATCH)], axis=2)
    states, y_pre, yg = _s5_fwd(proj, wb, wc, w["ssm_d"], abar)
    z = _matmul(yg, w["ssm_w_glu"], mode="nn", name="glu_proj", bias=w["ssm_b_glu"])
    ys = _glu_norm_fwd(y_pre, z, w["ssm_out_norm_w"])

    q_col, kv_col, kpe_col = width // qr, (width + qr) // kvr, (width + qr + kvr) // LANES
    assert width % qr == 0 and (width + qr) % kvr == 0
    qn = _rmsnorm_fwd(proj, w["mla_q_norm_w"], name="q_norm", width=qr, col=q_col)
    kvn = _rmsnorm_fwd(proj, w["mla_kv_norm_w"], name="kv_norm", width=kvr, col=kv_col)
    q = _matmul(qn, w["mla_w_uq"], mode="nn", name="q_proj")
    kv = _matmul(kvn, w["mla_w_ukv"], mode="nn", name="kv_proj")
    half = QK_ROPE_DIM // 2
    inv_freq = ROPE_THETA ** (-jnp.arange(0, QK_ROPE_DIM, 2, dtype=F32) / QK_ROPE_DIM)
    zeros = jnp.zeros((LANES - QK_ROPE_DIM,), F32)
    freq = jnp.concatenate([inv_freq, inv_freq, zeros]).reshape(1, LANES)
    sign = jnp.concatenate([-jnp.ones((half,), F32), jnp.ones((half,), F32), zeros]).reshape(1, LANES)
    cos, sin = _rope_tables(posf, freq, sign)
    qc, kc, vb = _attn_prep(q, kv, proj, kpe_col, cos, sin)
    o, lse = _attn_fwd(qc, kc, vb, scale=scale, tq=ATTN_BLOCK)
    ym = _rmsnorm_fwd(o, w["mla_out_norm_w"], name="mla_out_norm")
    ycat = jnp.concatenate([ys, ym], axis=1)
    h1 = _matmul(ycat, w["w_out"], mode="nn", name="out_proj", add=x)

    hn2 = _rmsnorm_fwd(h1, w["ffn_norm_w"], name="ffn_norm")
    ffn = hooks.ffn_weights(hn2)
    a = _matmul(hn2, ffn["ffn_w_up"], mode="nn", name="ffn_up")
    gated = _conv_gate_fwd(a, ffn["ffn_conv_w"], w["ffn_conv_b"])
    h2 = _matmul(gated, ffn["ffn_w_down"], mode="nn", name="ffn_down", add=h1, tk=2816)
    loss_tile, dh2, g["final_norm_w"] = _final_norm_loss(h2, w["final_norm_w"], target)

    dgated = _matmul(dh2, ffn["ffn_w_down"], mode="nt", name="ffn_down_dx")
    g["ffn_w_down"] = _wgrad(gated, dh2, f2 // 2, d, True, "ffn_down_dw", tm=f2 // 2 // N_CHIPS, tn=512)
    da, dcw, dcb = _conv_gate_bwd(a, ffn["ffn_conv_w"], w["ffn_conv_b"], dgated)
    g["ffn_conv_w"] = jnp.concatenate([dcw[0, :3], dcw[1, :3]], axis=1)
    g["ffn_conv_b"] = jnp.concatenate([dcb[0], dcb[1]], axis=1)
    g["ffn_w_up"] = _wgrad(hn2, da, d, f2, False, "ffn_up_dw", b_split=True, tn=_tile(f2 // N_CHIPS, 1408))
    started = hooks.ffn_grads(g["ffn_w_down"], g["ffn_w_up"], dcb)
    ffn_norm_w = w["ffn_norm_w"] if started is None else w["ffn_norm_w"] + started[:1, :1]
    dhn2 = _matmul(da, ffn["ffn_w_up"], mode="nt", name="ffn_up_dx", a_split=True, tk=_tile(f2 // 2, 2816))
    dh1, g["ffn_norm_w"] = _rmsnorm_bwd(h1, ffn_norm_w, dhn2, name="ffn_norm_bwd", add=dh2)

    dycat = _matmul(dh1, w["w_out"], mode="nt", name="out_proj_dx")
    g["w_out"] = _wgrad(ycat, dh1, 2 * width, d, True, "out_proj_dw")
    started = hooks.ffn_backward_done(dycat)
    mla_out_norm_w, ssm_out_norm_w = w["mla_out_norm_w"], w["ssm_out_norm_w"]
    if started is not None:
        mla_out_norm_w, ssm_out_norm_w = mla_out_norm_w + started[:1, :1], ssm_out_norm_w + started[:1, :1]

    do, g["mla_out_norm_w"] = _rmsnorm_bwd(o, mla_out_norm_w, dycat, name="mla_out_norm_bwd", width=width, dy_col=1)
    dq, delta = _attn_bwd_q(qc, kc, vb, o, do, lse, cos, sin, scale=scale, tq=ATTN_BLOCK)
    dkv, dkpe = _attn_bwd_kv(qc, kc, vb, do, lse, delta, cos, sin, scale=scale, tk=ATTN_BLOCK)
    g["mla_w_uq"] = _wgrad(qn, dq, qr, heads * HEAD_SLOT, False, "q_proj_dw")
    dqn = _matmul(dq, w["mla_w_uq"], mode="nt", name="q_proj_dx")
    dcq, g["mla_q_norm_w"] = _rmsnorm_bwd(proj, w["mla_q_norm_w"], dqn, name="q_norm_bwd", width=qr, col=q_col)
    g["mla_w_ukv"] = _wgrad(kvn, dkv, kvr, heads * HEAD_SLOT, False, "kv_proj_dw")
    dkvn = _matmul(dkv, w["mla_w_ukv"], mode="nt", name="kv_proj_dx")
    dckv, g["mla_kv_norm_w"] = _rmsnorm_bwd(proj, w["mla_kv_norm_w"], dkvn, name="kv_norm_bwd", width=kvr, col=kv_col)

    dz, dyg_a, g["ssm_out_norm_w"], g["ssm_b_glu"] = _glu_norm_bwd(y_pre, z, ssm_out_norm_w, dycat)
    dyg_b = _matmul(dz, w["ssm_w_glu"], mode="nt", name="glu_proj_dx")
    g["ssm_w_glu"] = _wgrad(yg, dz, width, width, True, "glu_proj_dw")
    du, dwb, dwc, dabar, g["ssm_d"] = _s5_bwd(proj, states, y_pre, dyg_a, dyg_b, wb, wc, w["ssm_d"], abar)

    def b_unband(x):
        return _block_diag_part(x, SSM_GROUP, SSM_STATE).transpose(0, 1, 3, 2).reshape(groups, SSM_STATE * SSM_GROUP)

    def c_unband(x):
        return _block_diag_part(x, SSM_STATE, SSM_GROUP).transpose(0, 1, 3, 2).reshape(groups, SSM_GROUP, SSM_STATE)

    st = STATE_PER_BATCH
    g["ssm_c_re"] = c_unband(dwc[:, :st, :])
    g["ssm_c_im"] = -c_unband(dwc[:, st:, :])
    d_ar = dabar[:, 0, :st].reshape(groups, SSM_STATE)
    d_ai = dabar[:, 0, st:].reshape(groups, SSM_STATE)
    (g["ssm_lambda_re"], g["ssm_lambda_im"], g["ssm_log_dt"], g["ssm_b_re"], g["ssm_b_im"]) = _s5_params_bwd(
        w["ssm_lambda_re"], w["ssm_lambda_im"], w["ssm_log_dt"], w["ssm_b_re"], w["ssm_b_im"], d_ar, d_ai,
        b_unband(dwb[:, :, :st]), b_unband(dwb[:, :, st:]))

    pad = jnp.zeros((rows, inp - (width + qr + kvr + LANES)), F32)
    dproj = jnp.concatenate([du, dcq, dckv, dkpe, pad], axis=1)
    g["w_in"] = _wgrad(hn, dproj, d, inp, True, "in_proj_dw")
    dhn = _matmul(dproj, w["w_in"], mode="nt", name="in_proj_dx")
    dx, g["attn_norm_w"] = _rmsnorm_bwd(x, w["attn_norm_w"], dhn, name="attn_norm_bwd", add=dh1)
    return loss_tile, dx, g


ANY = pl.BlockSpec(memory_space=pl.ANY)
MESH = pl.DeviceIdType.MESH


def _mesh_pos():
    return lax.axis_index("x"), lax.axis_index("y"), lax.axis_index("c")


def _other_chips(x, y):
    return [(1 - x, y), (x, 1 - y), (1 - x, 1 - y)]


def _remote(src, dst, send_sems, recv_sems, k, to):
    return pltpu.make_async_remote_copy(src_ref=src, dst_ref=dst, send_sem=send_sems.at[k], recv_sem=recv_sems.at[k],
                                        device_id=to, device_id_type=MESH)


def _place_shard(shard, piece_idx, row_sharded, name, out_dtype=BF16, pieces=N_CHIPS):
    rs, cs = shard.shape
    tr = _tile(rs, 256, 2 * SUBLANES)
    rb = rs // tr

    def body(p_ref, x_ref, o_ref):
        o_ref[...] = x_ref[...].astype(o_ref.dtype)

    if row_sharded:
        out_shape, out_map = (pieces * rs, cs), (lambda i, p_ref: (p_ref[0] * rb + i, 0))
    else:
        out_shape, out_map = (rs, pieces * cs), (lambda i, p_ref: (i, p_ref[0]))
    return pl.pallas_call(
        body, name=name, out_shape=jax.ShapeDtypeStruct(out_shape, out_dtype),
        grid_spec=pltpu.PrefetchScalarGridSpec(
            num_scalar_prefetch=1, grid=(rb,), in_specs=[pl.BlockSpec((tr, cs), lambda i, p_ref: (i, 0))],
            out_specs=pl.BlockSpec((tr, cs), out_map)),
        compiler_params=_params(("parallel",)),
    )(piece_idx, shard)


def _gather_weights(placed, name):
    n = len(placed)
    meta = [(row_sharded, direct) for _, row_sharded, direct in placed]
    over_ici, over_d2d = _gather_plans(meta)
    forwarded = [t for t, (_, direct) in enumerate(meta) if not direct]

    def body(*refs):
        outs = refs[n:2 * n]
        send_sems, recv_sems, pass_send_sems, pass_recv_sems = refs[2 * n:]
        first, arrivals = over_ici(outs, send_sems, recv_sems)
        passed, passed_arrivals = over_d2d([outs[t] for t in forwarded], pass_send_sems, pass_recv_sems)
        for cp in first:
            cp.start()
        for t in range(n):
            for j in range(3):
                arrivals[3 * t + j].wait_recv()
                if t in forwarded:
                    passed[3 * forwarded.index(t) + j].start()
        for cp in passed_arrivals:
            cp.wait_recv()
        for cp in first + passed:
            cp.wait_send()

    return pl.pallas_call(
        body, name=name, in_specs=[ANY] * n, out_specs=[ANY] * n,
        out_shape=[jax.ShapeDtypeStruct(arr.shape, arr.dtype) for arr, _, _ in placed],
        input_output_aliases={t: t for t in range(n)},
        scratch_shapes=[pltpu.SemaphoreType.DMA((3 * n,)), pltpu.SemaphoreType.DMA((3 * n,)),
                        pltpu.SemaphoreType.DMA((3 * len(forwarded),)), pltpu.SemaphoreType.DMA((3 * len(forwarded),))],
    )(*[arr for arr, _, _ in placed])


def _gather_plans(meta):
    def window(ref, row_sharded, piece, half):
        r, cc = ref.shape
        if row_sharded:
            rs = r // N_CHIPS
            if half is None:
                return ref.at[pl.ds(piece * rs, rs), :]
            return ref.at[pl.ds(piece * rs + half * (rs // 2), rs // 2), :]
        cs = cc // N_CHIPS
        if half is None:
            return ref.at[:, pl.ds(piece * cs, cs)]
        return ref.at[pl.ds(half * (r // 2), r // 2), pl.ds(piece * cs, cs)]

    def over_ici(refs, send_sems, recv_sems):
        x, y, c = _mesh_pos()
        sends, recvs = [], []
        for t, (row_sharded, direct) in enumerate(meta):
            mine = window(refs[t], row_sharded, 2 * x + y, None if direct else c)
            for j, (px, py) in enumerate(_other_chips(x, y)):
                theirs = window(refs[t], row_sharded, 2 * px + py, None if direct else c)
                sends.append(_remote(mine, mine, send_sems, recv_sems, 3 * t + j, (px, py, c)))
                recvs.append(_remote(theirs, theirs, send_sems, recv_sems, 3 * t + j, (px, py, c)))
        return sends, recvs

    def over_d2d(refs, send_sems, recv_sems):
        x, y, c = _mesh_pos()
        sends, recvs = [], []
        rows = [row_sharded for row_sharded, direct in meta if not direct]
        for t, row_sharded in enumerate(rows):
            for j, (px, py) in enumerate(_other_chips(x, y)):
                got = window(refs[t], row_sharded, 2 * px + py, c)
                other = window(refs[t], row_sharded, 2 * px + py, 1 - c)
                sends.append(_remote(got, got, send_sems, recv_sems, 3 * t + j, (x, y, 1 - c)))
                recvs.append(_remote(other, other, send_sems, recv_sems, 3 * t + j, (x, y, 1 - c)))
        return sends, recvs

    return over_ici, over_d2d


HBM = pl.BlockSpec(memory_space=pltpu.HBM)
SEMAPHORES = pl.BlockSpec(memory_space=pltpu.SEMAPHORE)
DATAFLOW = pltpu.SideEffectType.DATAFLOW_SIDE_EFFECTING


def _start_copies(name, arrays, plan, n_copies, after):
    n = len(arrays)

    def body(*refs):
        sends, _ = plan(refs[:n], refs[n + 1], refs[n + 2])
        for cp in sends:
            cp.start()
        token = refs[2 * n + 3]
        token[...] = jnp.zeros_like(token)

    out = pl.pallas_call(
        body, name=name,
        out_shape=(pltpu.SemaphoreType.DMA((n_copies,)), pltpu.SemaphoreType.DMA((n_copies,)),
                   *[pltpu.HBM(a.shape, a.dtype) for a in arrays], jax.ShapeDtypeStruct((SUBLANES, LANES), F32)),
        in_specs=[HBM] * n + [ANY],
        out_specs=(SEMAPHORES, SEMAPHORES, *[HBM] * n, pl.BlockSpec(memory_space=pltpu.VMEM)),
        input_output_aliases={t: t + 2 for t in range(n)},
        compiler_params=pltpu.CompilerParams(has_side_effects=DATAFLOW),
    )(*[pltpu.with_memory_space_constraint(a, pltpu.HBM) for a in arrays], after)
    return out[0], out[1], list(out[2:2 + n]), out[2 + n]


def _wait_copies(name, started, plan, after):
    send_sems, recv_sems, arrays, _ = started
    n = len(arrays)

    def body(*refs):
        sends, recvs = plan(refs[:n], refs[n], refs[n + 1])
        for cp in sends:
            cp.wait_send()
        for cp in recvs:
            cp.wait_recv()

    out = pl.pallas_call(
        body, name=name, out_shape=[pltpu.HBM(a.shape, a.dtype) for a in arrays],
        in_specs=[HBM] * n + [SEMAPHORES, SEMAPHORES, ANY], out_specs=[HBM] * n,
        input_output_aliases={t: t for t in range(n)},
        compiler_params=pltpu.CompilerParams(has_side_effects=DATAFLOW),
    )(*arrays, send_sems, recv_sems, after)
    return list(out)


def _exchange(name, arrays, out_shapes, plan, n_copies, in_place=False, after=None):
    n = len(arrays)
    extra = [] if after is None else [after]

    def body(*refs):
        ins, outs = refs[:n], refs[n + len(extra):n + len(extra) + len(out_shapes)]
        send_sems, recv_sems = refs[n + len(extra) + len(out_shapes):]
        sends, recvs = plan(ins, outs, send_sems, recv_sems)
        for cp in sends:
            cp.start()
        for cp in recvs:
            cp.wait_recv()
        for cp in sends:
            cp.wait_send()

    return pl.pallas_call(
        body, name=name, in_specs=[ANY] * (n + len(extra)), out_specs=[ANY] * len(out_shapes), out_shape=out_shapes,
        input_output_aliases={t: t for t in range(n)} if in_place else {},
        scratch_shapes=[pltpu.SemaphoreType.DMA((n_copies,)), pltpu.SemaphoreType.DMA((n_copies,))],
    )(*arrays, *extra)


def _swap_plan(n):
    def plan(refs, send_sems, recv_sems):
        x, y, c = _mesh_pos()
        sends = [_remote(refs[t].at[1 - c], refs[n + t], send_sems, recv_sems, t, (x, y, 1 - c)) for t in range(n)]
        return sends, sends

    return plan


def _scatter_plan(n):
    def plan(refs, send_sems, recv_sems):
        x, y, c = _mesh_pos()
        sends = []
        for t in range(n):
            for j, (px, py) in enumerate(_other_chips(x, y)):
                sends.append(_remote(refs[t].at[2 * px + py], refs[n + t].at[j], send_sems, recv_sems, 3 * t + j, (px, py, c)))
        return sends, sends

    return plan


def _swap_shapes(grads):
    return [jax.ShapeDtypeStruct(g.shape[1:], g.dtype) for g in grads]


def _scatter_shapes(sums):
    return [jax.ShapeDtypeStruct((3,) + s.shape[1:], s.dtype) for s in sums]


def _swap_other_half(grads, name):
    plan = _swap_plan(len(grads))
    return _exchange(name, grads, _swap_shapes(grads), lambda ins, outs, s, r: plan(list(ins) + list(outs), s, r), len(grads))


def _join_halves(halves, name, after=None):
    def plan(ins, outs, send_sems, recv_sems):
        x, y, c = _mesh_pos()
        sends = [_remote(outs[t].at[c], outs[t].at[c], send_sems, recv_sems, t, (x, y, 1 - c)) for t in range(len(ins))]
        recvs = [_remote(outs[t].at[1 - c], outs[t].at[1 - c], send_sems, recv_sems, t, (x, y, 1 - c))
                 for t in range(len(ins))]
        return sends, recvs

    shapes = [jax.ShapeDtypeStruct(h.shape, h.dtype) for h in halves]
    return _exchange(name, halves, shapes, plan, len(halves), in_place=True, after=after)


def _add_other_half(g4, got, where, name, wire_dtype=BF16):
    _, pieces, sr, sc = g4.shape
    tr = _tile(sr, 256, 2 * SUBLANES)

    def body(w_ref, a_ref, b_ref, o_ref):
        o_ref[...] = (a_ref[...] + b_ref[...]).astype(o_ref.dtype)

    blk = pl.BlockSpec((None, tr, sc), lambda p, i, w_ref: (p, i, 0))
    return pl.pallas_call(
        body, name=name, out_shape=jax.ShapeDtypeStruct((pieces, sr, sc), wire_dtype),
        grid_spec=pltpu.PrefetchScalarGridSpec(
            num_scalar_prefetch=1, grid=(pieces, sr // tr),
            in_specs=[pl.BlockSpec((None, None, tr, sc), lambda p, i, w_ref: (w_ref[0], p, i, 0)), blk], out_specs=blk),
        compiler_params=_params(("parallel", "parallel")),
    )(where, g4, got)


def _add_pieces(g4, got_half, got_pieces, where, name):
    _, _, sr, sc = g4.shape
    tr = _tile(sr, 256, 2 * SUBLANES)

    def body(w_ref, a_ref, b_ref, r_ref, o_ref):
        acc = a_ref[...] + b_ref[...]
        for j in range(3):
            acc = acc + r_ref[j].astype(F32)
        o_ref[...] = acc

    return pl.pallas_call(
        body, name=name, out_shape=jax.ShapeDtypeStruct((N_CORES, sr, sc), F32),
        grid_spec=pltpu.PrefetchScalarGridSpec(
            num_scalar_prefetch=1, grid=(sr // tr,),
            in_specs=[pl.BlockSpec((None, None, tr, sc), lambda i, w_ref: (w_ref[0], w_ref[1], i, 0)),
                      pl.BlockSpec((None, tr, sc), lambda i, w_ref: (w_ref[1], i, 0)),
                      pl.BlockSpec((3, tr, sc), lambda i, w_ref: (0, i, 0))],
            out_specs=pl.BlockSpec((None, tr, sc), lambda i, w_ref: (w_ref[0], i, 0))),
        compiler_params=_params(("parallel",)),
    )(where, g4, got_half, got_pieces)


def _adamw(w, g, m, v, name):
    rows, cols = w.shape
    tr = _tile(rows, max(SUBLANES, (1 << 19) // max(cols, 1) // SUBLANES * SUBLANES), SUBLANES)

    def body(w_ref, g_ref, m_ref, v_ref, d_ref, nm_ref, nv_ref):
        gv = g_ref[...]
        nm = ADAM_B1 * m_ref[...] + (1.0 - ADAM_B1) * gv
        nv = ADAM_B2 * v_ref[...] + (1.0 - ADAM_B2) * (gv * gv)
        m_hat = nm / (1.0 - ADAM_B1 ** ADAM_STEP)
        v_hat = nv / (1.0 - ADAM_B2 ** ADAM_STEP)
        d_ref[...] = -ADAM_LR * (m_hat / (jnp.sqrt(v_hat) + ADAM_EPS) + ADAM_WD * w_ref[...])
        nm_ref[...] = nm
        nv_ref[...] = nv

    blk = pl.BlockSpec((tr, cols), lambda i: (i, 0))
    return pl.pallas_call(
        body, name=name, grid=(rows // tr,), in_specs=[blk] * 4, out_specs=[blk] * 3,
        out_shape=[jax.ShapeDtypeStruct((rows, cols), F32)] * 3, compiler_params=_params(("parallel",)),
    )(w, g, m, v)


WEIGHTS = ['attn_norm_w', 'w_in', 'ssm_lambda_re', 'ssm_lambda_im', 'ssm_log_dt', 'ssm_b_re', 'ssm_b_im', 'ssm_c_re',
           'ssm_c_im', 'ssm_d', 'ssm_w_glu', 'ssm_b_glu', 'mla_q_norm_w', 'mla_w_uq', 'mla_kv_norm_w', 'mla_w_ukv',
           'ssm_out_norm_w', 'mla_out_norm_w', 'w_out', 'ffn_norm_w', 'ffn_w_up', 'ffn_conv_w', 'ffn_conv_b',
           'ffn_w_down', 'final_norm_w']
SHARDED = {'w_in': True, 'ssm_w_glu': True, 'mla_w_uq': False, 'mla_w_ukv': False, 'w_out': True, 'ffn_w_up': False,
           'ffn_w_down': True}
SMALL = [n for n in WEIGHTS if n not in SHARDED and n != 'ffn_conv_w']
ROPE_PAD = HEAD_SLOT - QK_NOPE_DIM - QK_ROPE_DIM
SMALL_COLS = 8 * LANES


def _pad_heads(w_uq, heads):
    qr = w_uq.shape[0]
    w3 = w_uq.reshape(qr, heads, QK_NOPE_DIM + QK_ROPE_DIM)
    return jnp.concatenate([w3, jnp.zeros((qr, heads, ROPE_PAD), w_uq.dtype)], axis=2).reshape(qr, heads * HEAD_SLOT)


def _unpad_heads(g_uq, heads):
    qr = g_uq.shape[0]
    return g_uq.reshape(qr, heads, HEAD_SLOT)[:, :, :QK_NOPE_DIM + QK_ROPE_DIM].reshape(qr, -1)


FFN = ['ffn_w_up', 'ffn_w_down']
FFN_GATHER = FFN + ['ffn_conv_w']
FFN_GATHER_META = [(SHARDED[n], False) for n in FFN] + [(False, True)]


class _Overlapped:
    def __init__(self, placed, where, after):
        self.where = where
        self.over_ici, self.over_d2d = _gather_plans(FFN_GATHER_META)
        self.gather = _start_copies("gather_ffn_start", placed, self.over_ici, 3 * len(placed), after)
        self.gather_started = self.gather[3]

    def ffn_weights(self, after):
        arrived = _wait_copies("gather_ffn_wait", self.gather, self.over_ici, after)
        n = len(FFN)
        shapes = [jax.ShapeDtypeStruct(a.shape, a.dtype) for a in arrived[:n]]
        passed = _exchange("gather_ffn_pass", arrived[:n], shapes, lambda ins, outs, s, r: self.over_d2d(outs, s, r),
                           3 * n, in_place=True)
        return dict(zip(FFN_GATHER, list(passed) + arrived[n:]))

    def ffn_grads(self, g_down, g_up, after):
        grads = [g_up, g_down]
        lands = [lax.empty(s.shape, s.dtype) for s in _swap_shapes(grads)]
        self.swap = _start_copies("grad_ffn_swap_start", grads + lands, _swap_plan(len(grads)), len(grads), after)
        return self.swap[3]

    def ffn_backward_done(self, after):
        n = len(FFN)
        out = _wait_copies("grad_ffn_swap_wait", self.swap, _swap_plan(n), after)
        self.grads, self.got_half = out[:n], out[n:]
        sums = [_add_other_half(self.grads[t], self.got_half[t], self.where, "grad_add_half_" + name)
                for t, name in enumerate(FFN)]
        lands = [lax.empty(s.shape, s.dtype) for s in _scatter_shapes(sums)]
        self.scatter = _start_copies("grad_ffn_scatter_start", sums + lands, _scatter_plan(n), 3 * n, after)
        return self.scatter[3]

    def ffn_reduced(self, after):
        n = len(FFN)
        got_pieces = _wait_copies("grad_ffn_scatter_wait", self.scatter, _scatter_plan(n), after)[n:]
        return [_add_pieces(self.grads[t], self.got_half[t], got_pieces[t], self.where, "grad_add_pieces_" + name)
                for t, name in enumerate(FFN)]


def _step(args):
    x, positions, target = args["x"][0], args["positions"], args["loss_target"][0]
    rows = x.shape[0]
    p = {n: args[n] for n in WEIGHTS}
    xi, yi, ci = _mesh_pos()
    piece = 2 * xi + yi

    w_in = p["w_in"][0]
    in_width = w_in.shape[1]
    in_pad = (-in_width) % (2 * LANES)
    heads_here = p["mla_w_uq"].shape[2] // (QK_NOPE_DIM + QK_ROPE_DIM)
    shards = {
        "w_in": jnp.pad(w_in, ((0, 0), (0, in_pad))),
        "ssm_w_glu": p["ssm_w_glu"][0],
        "mla_w_uq": _pad_heads(p["mla_w_uq"][0], heads_here),
        "mla_w_ukv": p["mla_w_ukv"][0],
        "w_out": p["w_out"][0],
        "ffn_w_up": p["ffn_w_up"][0],
        "ffn_w_down": p["ffn_w_down"][0],
    }
    conv_w = jnp.pad(p["ffn_conv_w"][0], ((0, SUBLANES - p["ffn_conv_w"].shape[1]), (0, 0)))
    order = list(SHARDED)
    piece_idx = piece.reshape(1).astype(jnp.int32)
    placed = {n: _place_shard(shards[n], piece_idx, SHARDED[n], "place_" + n) for n in order}
    placed["ffn_conv_w"] = _place_shard(conv_w, piece_idx, False, "place_ffn_conv_w", out_dtype=F32)
    mixer = [n for n in order if n not in FFN]
    w = dict(zip(mixer, _gather_weights([(placed[n], SHARDED[n], False) for n in mixer], "gather_mixer_weights")))
    where = jnp.stack([ci, piece]).astype(jnp.int32)
    hooks = _Overlapped([placed[n] for n in FFN_GATHER], where, after=w["w_in"])
    groups = p["ssm_lambda_re"].shape[1]
    w.update({
        "attn_norm_w": p["attn_norm_w"] + hooks.gather_started[:1, :1],
        "ssm_lambda_re": p["ssm_lambda_re"][0], "ssm_lambda_im": p["ssm_lambda_im"][0],
        "ssm_log_dt": p["ssm_log_dt"].reshape(groups, 1), "ssm_b_re": p["ssm_b_re"].reshape(groups, -1),
        "ssm_b_im": p["ssm_b_im"].reshape(groups, -1), "ssm_c_re": p["ssm_c_re"][0], "ssm_c_im": p["ssm_c_im"][0],
        "ssm_d": p["ssm_d"], "ssm_b_glu": p["ssm_b_glu"], "mla_q_norm_w": p["mla_q_norm_w"],
        "mla_kv_norm_w": p["mla_kv_norm_w"], "ssm_out_norm_w": p["ssm_out_norm_w"], "mla_out_norm_w": p["mla_out_norm_w"],
        "ffn_norm_w": p["ffn_norm_w"], "ffn_conv_b": p["ffn_conv_b"], "final_norm_w": p["final_norm_w"].reshape(1, -1),
    })

    loss_tile, dx, g = _local_step(x, positions.reshape(rows, 1).astype(F32), target, w, hooks)
    loss = lax.psum(loss_tile[0, 0], ("x", "y", "c"))

    flat = [g[n].reshape(-1) for n in SMALL] + [g["ffn_conv_w"].reshape(-1)]
    sizes = [f.shape[0] for f in flat]
    per_block = -(-sum(sizes) // (N_CORES * N_CHIPS * SMALL_COLS))
    small_rows = -(-per_block // (2 * SUBLANES)) * (2 * SUBLANES)
    padded = N_CORES * N_CHIPS * small_rows * SMALL_COLS

    def pack(parts):
        parts = list(parts)
        have = sum(q.shape[0] for q in parts)
        return jnp.concatenate(parts + [jnp.zeros((padded - have,), F32)])

    reduced = mixer + ["small"]
    g_rs = [g[n] for n in mixer] + [pack(flat).reshape(N_CORES, N_CHIPS, small_rows, SMALL_COLS)]
    wire = [BF16] * len(mixer) + [F32]
    got_half = _swap_other_half(g_rs, "grad_swap_halves")
    sums = [_add_other_half(g_rs[t], got_half[t], where, "grad_add_half_" + n, wire[t]) for t, n in enumerate(reduced)]
    lands = [lax.empty(s.shape, s.dtype) for s in _scatter_shapes(sums)]
    scatter_plan = _scatter_plan(len(reduced))
    scatter = _start_copies("grad_mixer_scatter_start", sums + lands, scatter_plan, 3 * len(reduced), dx)

    grads, delta, new_m, new_v = {}, {}, {}, {}

    def finish(n, joined):
        grad = jnp.concatenate([joined[0], joined[1]], axis=1) if SHARDED[n] else joined.reshape(-1, joined.shape[2])
        if n == "w_in":
            grad = grad[:, :in_width]
        if n == "mla_w_uq":
            grad = _unpad_heads(grad, heads_here)
        adam(n, grad)

    def adam(n, grad):
        shape = p[n].shape
        d2, m2, v2 = _adamw(p[n].reshape(shape[1:]), grad, args["m_" + n].reshape(shape[1:]),
                            args["v_" + n].reshape(shape[1:]), "adamw_" + n)
        grads[n] = grad.reshape(shape)
        delta[n], new_m[n], new_v[n] = d2.reshape(shape), m2.reshape(shape), v2.reshape(shape)

    for n, joined in zip(FFN, _join_halves(hooks.ffn_reduced(dx), "grad_ffn_join_halves", after=scatter[3])):
        finish(n, joined)
    got_pieces = _wait_copies("grad_mixer_scatter_wait", scatter, scatter_plan, delta[FFN[0]])[len(reduced):]
    halves = [_add_pieces(g_rs[t], got_half[t], got_pieces[t], where, "grad_add_pieces_" + n) for t, n in enumerate(reduced)]
    joined = _join_halves(halves, "grad_join_halves")
    for n, j in zip(mixer, joined):
        finish(n, j)
    eighths = _place_shard(joined[-1].reshape(N_CORES * small_rows, SMALL_COLS), piece_idx, True, "place_small_grads",
                           out_dtype=F32)
    small_sum = _gather_weights([(eighths, True, False)], "gather_small_grads")[0]
    flat_sum = small_sum.reshape(N_CHIPS, N_CORES, small_rows * SMALL_COLS).transpose(1, 0, 2).reshape(-1)
    offs = [0]
    for s in sizes:
        offs.append(offs[-1] + s)
    for k, n in enumerate(SMALL):
        grads[n] = flat_sum[offs[k]:offs[k + 1]].reshape(p[n].shape)
    taps, cols_here = p["ffn_conv_w"].shape[1], p["ffn_conv_w"].shape[2]
    conv_full = flat_sum[offs[len(SMALL)]:offs[len(SMALL) + 1]].reshape(taps, N_CHIPS * cols_here)
    adam("ffn_conv_w", lax.dynamic_slice_in_dim(conv_full, piece * cols_here, cols_here, axis=1))

    def pack_rows(parts):
        return pack(parts).reshape(-1, LANES)

    d2, m2, v2 = _adamw(pack_rows(p[n].reshape(-1) for n in SMALL), flat_sum.reshape(-1, LANES),
                        pack_rows(args["m_" + n].reshape(-1) for n in SMALL),
                        pack_rows(args["v_" + n].reshape(-1) for n in SMALL), "adamw_small")
    for k, n in enumerate(SMALL):
        for src, dst in ((d2, delta), (m2, new_m), (v2, new_v)):
            dst[n] = src.reshape(-1)[offs[k]:offs[k + 1]].reshape(p[n].shape)

    return (loss, dx[None], *[grads[n] for n in WEIGHTS], *[delta[n] for n in WEIGHTS],
            *[new_m[n] for n in WEIGHTS], *[new_v[n] for n in WEIGHTS])


def kernel(x, positions, attn_norm_w, w_in, ssm_lambda_re, ssm_lambda_im, ssm_log_dt, ssm_b_re, ssm_b_im, ssm_c_re, ssm_c_im, ssm_d, ssm_w_glu, ssm_b_glu, mla_q_norm_w, mla_w_uq, mla_kv_norm_w, mla_w_ukv, ssm_out_norm_w, mla_out_norm_w, w_out, ffn_norm_w, ffn_w_up, ffn_conv_w, ffn_conv_b, ffn_w_down, final_norm_w, loss_target, m_attn_norm_w, m_w_in, m_ssm_lambda_re, m_ssm_lambda_im, m_ssm_log_dt, m_ssm_b_re, m_ssm_b_im, m_ssm_c_re, m_ssm_c_im, m_ssm_d, m_ssm_w_glu, m_ssm_b_glu, m_mla_q_norm_w, m_mla_w_uq, m_mla_kv_norm_w, m_mla_w_ukv, m_ssm_out_norm_w, m_mla_out_norm_w, m_w_out, m_ffn_norm_w, m_ffn_w_up, m_ffn_conv_w, m_ffn_conv_b, m_ffn_w_down, m_final_norm_w, v_attn_norm_w, v_w_in, v_ssm_lambda_re, v_ssm_lambda_im, v_ssm_log_dt, v_ssm_b_re, v_ssm_b_im, v_ssm_c_re, v_ssm_c_im, v_ssm_d, v_ssm_w_glu, v_ssm_b_glu, v_mla_q_norm_w, v_mla_w_uq, v_mla_kv_norm_w, v_mla_w_ukv, v_ssm_out_norm_w, v_mla_out_norm_w, v_w_out, v_ffn_norm_w, v_ffn_w_up, v_ffn_conv_w, v_ffn_conv_b, v_ffn_w_down, v_final_norm_w):
    return _step(dict(locals()))
```

```python
import functools
import math

import jax
import jax.numpy as jnp
from jax import lax
from jax.experimental import pallas as pl
from jax.experimental.pallas import tpu as pltpu

F32 = jnp.float32
BF16 = jnp.bfloat16

SSM_GROUP = 16
SSM_STATE = 64
QK_NOPE_DIM = 128
QK_ROPE_DIM = 64
V_HEAD_DIM = 128
ROPE_THETA = 10000.0
RMS_EPS = 1e-6
ADAM_LR, ADAM_B1, ADAM_B2, ADAM_EPS, ADAM_WD, ADAM_STEP = 0.001, 0.9, 0.999, 1e-08, 0.01, 10

LANES = 128
SUBLANES = 8
VMEM_LIMIT_BYTES = 56 * 1024 * 1024

GROUPS_PER_BATCH = LANES // SSM_GROUP
STATE_PER_BATCH = GROUPS_PER_BATCH * SSM_STATE
HEAD_SLOT = 2 * LANES
NEG_INF = -1e30
ATTN_BLOCK = 512
FFN_ROWS = 1024

N_CHIPS = 4
N_CORES = 2


def _tile(n, pref, align=LANES):
    if n <= pref:
        return n
    t = (pref // align) * align
    while t >= align:
        if n % t == 0:
            return t
        t -= align
    return n


def _params(sem):
    return pltpu.CompilerParams(dimension_semantics=sem, vmem_limit_bytes=VMEM_LIMIT_BYTES)


def _dot(a, b, dims):
    return lax.dot_general(a, b, (dims, ((), ())), preferred_element_type=F32)


def _dot_nn(a, b):
    return _dot(a, b, ((1,), (0,)))


def _dot_nt(a, b):
    return _dot(a, b, ((1,), (1,)))


def _dot_tn(a, b):
    return _dot(a, b, ((0,), (0,)))


def _matmul(a, b, *, mode, name, tm=512, tn=1024, tk=2048, bias=None, add=None, out_dtype=F32,
            out_blocks=None, a_split=False, b_split=False):
    if a_split:
        assert mode == "nt"
        a_shape = (a.shape[1], 2 * a.shape[2])
    else:
        a_shape = a.shape
    if b_split:
        assert mode == "tn"
        b_shape = (b.shape[1], 2 * b.shape[2])
    else:
        b_shape = b.shape
    if mode == "nn":
        (m, k), (k2, n) = a_shape, b_shape
    elif mode == "nt":
        (m, k), (n, k2) = a_shape, b_shape
    else:
        (k, m), (k2, n) = a_shape, b_shape
    assert k == k2, (a.shape, b.shape, mode)
    tm, tn, tk = _tile(m, tm, SUBLANES), _tile(n, tn), _tile(k, tk)
    nk = k // tk
    a_spec = {"nn": pl.BlockSpec((tm, tk), lambda i, j, kk: (i, kk)),
              "nt": pl.BlockSpec((tm, tk), lambda i, j, kk: (i, kk)),
              "tn": pl.BlockSpec((tk, tm), lambda i, j, kk: (kk, i))}[mode]
    b_spec = {"nn": pl.BlockSpec((tk, tn), lambda i, j, kk: (kk, j)),
              "nt": pl.BlockSpec((tn, tk), lambda i, j, kk: (j, kk)),
              "tn": pl.BlockSpec((tk, tn), lambda i, j, kk: (kk, j))}[mode]
    if a_split:
        kb = a.shape[2] // tk
        assert a.shape[2] % tk == 0
        a_spec = pl.BlockSpec((None, tm, tk), lambda i, j, kk: (kk // kb, i, kk % kb))
    if b_split:
        nb = b.shape[2] // tn
        assert b.shape[2] % tn == 0
        b_spec = pl.BlockSpec((None, tk, tn), lambda i, j, kk: (j // nb, kk, j % nb))
    dot = {"nn": _dot_nn, "nt": _dot_nt, "tn": _dot_tn}[mode]
    in_specs, operands = [a_spec, b_spec], [a, b]
    if bias is not None:
        in_specs.append(pl.BlockSpec((1, tn), lambda i, j, kk: (0, j)))
        operands.append(bias)
    if add is not None:
        in_specs.append(pl.BlockSpec((tm, tn), lambda i, j, kk: (i, j)))
        operands.append(add)

    def body(*refs):
        a_ref, b_ref = refs[0], refs[1]
        rest = list(refs[2:])
        bias_ref = rest.pop(0) if bias is not None else None
        add_ref = rest.pop(0) if add is not None else None
        o_ref, acc_ref = rest

        def finish(acc):
            if bias_ref is not None:
                acc = acc + bias_ref[...]
            if add_ref is not None:
                acc = acc + add_ref[...]
            o_ref[...] = acc.astype(o_ref.dtype)

        part = dot(a_ref[...].astype(BF16), b_ref[...].astype(BF16))
        if nk == 1:
            finish(part)
        else:
            kk = pl.program_id(2)

            @pl.when(kk == 0)
            def _():
                acc_ref[...] = part

            @pl.when(jnp.logical_and(kk > 0, kk < nk - 1))
            def _():
                acc_ref[...] += part

            @pl.when(kk == nk - 1)
            def _():
                finish(acc_ref[...] + part)

    if out_blocks is None:
        out_shape = jax.ShapeDtypeStruct((m, n), out_dtype)
        out_spec = pl.BlockSpec((tm, tn), lambda i, j, kk: (i, j))
    else:
        shape, block, index_map = out_blocks(tm, tn)
        out_shape = jax.ShapeDtypeStruct(shape, out_dtype)
        out_spec = pl.BlockSpec(block, index_map)
    acc_shape = (tm, tn) if nk > 1 else (SUBLANES, LANES)
    return pl.pallas_call(
        body, name=name, grid=(m // tm, n // tn, nk), in_specs=in_specs, out_specs=out_spec, out_shape=out_shape,
        scratch_shapes=[pltpu.VMEM(acc_shape, F32)],
        compiler_params=_params(("parallel", "parallel", "arbitrary")),
    )(*operands)


def _wgrad_blocks(rows, cols, row_sharded):
    if row_sharded:
        sr, sc = rows // N_CHIPS, cols // N_CORES
    else:
        sr, sc = rows // N_CORES, cols // N_CHIPS

    def make(tm, tn):
        assert sr % tm == 0 and sc % tn == 0, (rows, cols, tm, tn)
        rb, cb = sr // tm, sc // tn
        if row_sharded:
            def index_map(i, j, kk):
                return (j // cb, i // rb, i % rb, j % cb)
        else:
            def index_map(i, j, kk):
                return (i // rb, j // cb, i % rb, j % cb)
        return (N_CORES, N_CHIPS, sr, sc), (None, None, tm, tn), index_map

    return make, (sr, sc)


def _rms_rows(x):
    return lax.rsqrt(jnp.mean(x * x, axis=-1, keepdims=True) + RMS_EPS)


def _rmsnorm_fwd(x, w, *, name, width=None, col=0, out_dtype=BF16, tr=256):
    rows = x.shape[0]
    width = x.shape[1] if width is None else width
    tr = _tile(rows, tr, SUBLANES)

    def body(x_ref, w_ref, o_ref):
        xv = x_ref[...]
        o_ref[...] = (xv * _rms_rows(xv) * w_ref[...]).astype(o_ref.dtype)

    return pl.pallas_call(
        body, name=name, grid=(rows // tr,),
        in_specs=[pl.BlockSpec((tr, width), lambda i: (i, col)), pl.BlockSpec((1, width), lambda i: (0, 0))],
        out_specs=pl.BlockSpec((tr, width), lambda i: (i, 0)),
        out_shape=jax.ShapeDtypeStruct((rows, width), out_dtype),
        compiler_params=_params(("parallel",)),
    )(x, w)


def _rmsnorm_bwd_rows(xv, w, dy):
    r = _rms_rows(xv)
    n = xv * r
    dn = dy * w
    dx = r * (dn - n * jnp.mean(dn * n, axis=-1, keepdims=True))
    return dx, dy * n


def _rmsnorm_bwd(x, w, dy, *, name, width=None, col=0, dy_col=0, add=None, tr=256, dx_dtypes=(F32,)):
    rows = x.shape[0]
    n_dx = len(dx_dtypes)
    width = x.shape[1] if width is None else width
    tr = _tile(rows, tr, SUBLANES)
    in_specs = [pl.BlockSpec((tr, width), lambda i: (i, col)), pl.BlockSpec((1, width), lambda i: (0, 0)),
                pl.BlockSpec((tr, width), lambda i: (i, dy_col))]
    operands = [x, w, dy]
    if add is not None:
        in_specs.append(pl.BlockSpec((tr, width), lambda i: (i, 0)))
        operands.append(add)

    def body(*refs):
        x_ref, w_ref, dy_ref = refs[:3]
        add_ref = refs[3] if add is not None else None
        dx_refs, dw_ref = refs[-1 - n_dx:-1], refs[-1]
        dx, dwp = _rmsnorm_bwd_rows(x_ref[...], w_ref[...], dy_ref[...])
        if add_ref is not None:
            dx = dx + add_ref[...]
        for dx_ref in dx_refs:
            dx_ref[...] = dx.astype(dx_ref.dtype)
        part = jnp.sum(dwp, axis=0, keepdims=True)

        @pl.when(pl.program_id(0) == 0)
        def _():
            dw_ref[...] = part

        @pl.when(pl.program_id(0) > 0)
        def _():
            dw_ref[...] += part

    return pl.pallas_call(
        body, name=name, grid=(rows // tr,), in_specs=in_specs,
        out_specs=[pl.BlockSpec((tr, width), lambda i: (i, 0))] * n_dx + [pl.BlockSpec((1, width), lambda i: (0, 0))],
        out_shape=[jax.ShapeDtypeStruct((rows, width), dt) for dt in dx_dtypes] + [jax.ShapeDtypeStruct((1, width), F32)],
        compiler_params=_params(("arbitrary",)),
    )(*operands)


def _final_norm_loss(h, w, target, *, tr=256):
    rows, d = h.shape
    tr = _tile(rows, tr, SUBLANES)

    def body(h_ref, w_ref, t_ref, loss_ref, dh_ref, dhb_ref, dw_ref):
        hv, wv = h_ref[...], w_ref[...]
        r = _rms_rows(hv)
        n = hv * r
        err = n * wv - t_ref[...]
        d_out = err * (1.0 / d)
        dn = d_out * wv
        dh = r * (dn - n * jnp.mean(dn * n, axis=-1, keepdims=True))
        dh_ref[...] = dh
        dhb_ref[...] = dh.astype(BF16)
        dw_part = jnp.sum(d_out * n, axis=0, keepdims=True)
        loss_part = jnp.full((SUBLANES, LANES), 0.5 / d, F32) * jnp.sum(err * err)

        @pl.when(pl.program_id(0) == 0)
        def _():
            dw_ref[...] = dw_part
            loss_ref[...] = loss_part

        @pl.when(pl.program_id(0) > 0)
        def _():
            dw_ref[...] += dw_part
            loss_ref[...] += loss_part

    return pl.pallas_call(
        body, name="final_norm_loss", grid=(rows // tr,),
        in_specs=[pl.BlockSpec((tr, d), lambda i: (i, 0)), pl.BlockSpec((1, d), lambda i: (0, 0)),
                  pl.BlockSpec((tr, d), lambda i: (i, 0))],
        out_specs=[pl.BlockSpec((SUBLANES, LANES), lambda i: (0, 0)), pl.BlockSpec((tr, d), lambda i: (i, 0)),
                   pl.BlockSpec((tr, d), lambda i: (i, 0)), pl.BlockSpec((1, d), lambda i: (0, 0))],
        out_shape=[jax.ShapeDtypeStruct((SUBLANES, LANES), F32), jax.ShapeDtypeStruct((rows, d), F32),
                   jax.ShapeDtypeStruct((rows, d), BF16), jax.ShapeDtypeStruct((1, d), F32)],
        compiler_params=_params(("arbitrary",)),
    )(h, w, target)


def _cmul(ar, ai, br, bi):
    return ar * br - ai * bi, ar * bi + ai * br


def _expand_matrix(groups, reps):
    row = lax.broadcasted_iota(jnp.int32, (groups, groups * reps), 0)
    colg = lax.broadcasted_iota(jnp.int32, (groups, groups * reps), 1) // reps
    return (row == colg).astype(F32)


def _dot_exact(a, b, dims):
    return lax.dot_general(a, b, (dims, ((), ())), preferred_element_type=F32, precision=lax.Precision.HIGHEST)


def _s5_discretize(lr, li, dt):
    mag = jnp.exp(lr * dt)
    th = li * dt
    ar, ai = mag * jnp.cos(th), mag * jnp.sin(th)
    nr, ni = ar - 1.0, ai
    den = lr * lr + li * li
    zr = (nr * lr + ni * li) / den
    zi = (ni * lr - nr * li) / den
    return mag, ar, ai, nr, ni, den, zr, zi


def _s5_params(lam_re, lam_im, log_dt, b_re, b_im):
    g, p = lam_re.shape
    ph = b_re.shape[1]

    def body(lr_ref, li_ref, ldt_ref, br_ref, bi_ref, ar_ref, ai_ref, bbr_ref, bbi_ref):
        dt = jnp.exp(ldt_ref[...])
        _, ar, ai, _, _, _, zr, zi = _s5_discretize(lr_ref[...], li_ref[...], dt)
        ar_ref[...] = ar
        ai_ref[...] = ai
        e = _expand_matrix(p, ph // p)
        zr_x = _dot_exact(zr, e, ((1,), (0,)))
        zi_x = _dot_exact(zi, e, ((1,), (0,)))
        bre, bim = br_ref[...], bi_ref[...]
        bbr_ref[...] = zr_x * bre - zi_x * bim
        bbi_ref[...] = zr_x * bim + zi_x * bre

    return pl.pallas_call(
        body, name="s5_params",
        out_shape=[jax.ShapeDtypeStruct((g, p), F32)] * 2 + [jax.ShapeDtypeStruct((g, ph), F32)] * 2,
    )(lam_re, lam_im, log_dt, b_re, b_im)


def _s5_params_bwd(lam_re, lam_im, log_dt, b_re, b_im, d_ar, d_ai, d_bbr, d_bbi):
    g, p = lam_re.shape
    ph = b_re.shape[1]

    def body(lr_ref, li_ref, ldt_ref, br_ref, bi_ref, dar_ref, dai_ref, dbr_ref, dbi_ref,
             dlr_ref, dli_ref, dldt_ref, dbre_ref, dbim_ref):
        lr, li = lr_ref[...], li_ref[...]
        dt = jnp.exp(ldt_ref[...])
        mag, ar, ai, nr, ni, den, zr, zi = _s5_discretize(lr, li, dt)
        e = _expand_matrix(p, ph // p)
        zr_x = _dot_exact(zr, e, ((1,), (0,)))
        zi_x = _dot_exact(zi, e, ((1,), (0,)))
        bre, bim, dbr, dbi = br_ref[...], bi_ref[...], dbr_ref[...], dbi_ref[...]
        dbre_ref[...] = zr_x * dbr + zi_x * dbi
        dbim_ref[...] = zr_x * dbi - zi_x * dbr
        dzr = _dot_exact(bre * dbr + bim * dbi, e, ((1,), (1,)))
        dzi = _dot_exact(bre * dbi - bim * dbr, e, ((1,), (1,)))
        inv = 1.0 / den
        d_nr = (dzr * lr - dzi * li) * inv
        d_ni = (dzr * li + dzi * lr) * inv
        d_den = -(dzr * zr + dzi * zi) * inv
        d_lr = (dzr * nr + dzi * ni) * inv + 2.0 * lr * d_den
        d_li = (dzr * ni - dzi * nr) * inv + 2.0 * li * d_den
        t_ar = dar_ref[...] + d_nr
        t_ai = dai_ref[...] + d_ni
        d_lrdt = t_ar * ar + t_ai * ai
        d_th = t_ai * ar - t_ar * ai
        dlr_ref[...] = d_lr + d_lrdt * dt
        dli_ref[...] = d_li + d_th * dt
        dldt_ref[...] = jnp.sum(d_lrdt * lr + d_th * li, axis=1, keepdims=True) * dt

    return pl.pallas_call(
        body, name="s5_params_bwd",
        out_shape=[jax.ShapeDtypeStruct((g, p), F32)] * 2 + [jax.ShapeDtypeStruct((g, 1), F32)]
        + [jax.ShapeDtypeStruct((g, ph), F32)] * 2,
    )(lam_re, lam_im, log_dt, b_re, b_im, d_ar, d_ai, d_bbr, d_bbi)


def _powers(ar, ai, count):
    out = [(ar, ai)]
    for _ in range(count - 1):
        out.append(_cmul(out[-1][0], out[-1][1], ar, ai))
    return out


def _scan_coefs(ar, ai, reverse):
    w = ar.shape[-1]
    pw = _powers(ar, ai, SUBLANES)
    row = lax.broadcasted_iota(jnp.int32, (SUBLANES, w), 0)
    steps = []
    d = 1
    while d < SUBLANES:
        keep = (row < SUBLANES - d) if reverse else (row >= d)
        pr, pi = pw[d - 1]
        steps.append((d, jnp.where(keep, pr, 0.0), jnp.where(keep, pi, 0.0)))
        d *= 2
    cr = jnp.zeros((SUBLANES, w), F32)
    ci = jnp.zeros((SUBLANES, w), F32)
    for t in range(SUBLANES):
        pr, pi = pw[SUBLANES - 1 - t] if reverse else pw[t]
        cr = jnp.where(row == t, pr, cr)
        ci = jnp.where(row == t, pi, ci)
    return steps, cr, ci


def _scan_tile(xr, xi, carry_r, carry_i, coefs, reverse):
    steps, cr, ci = coefs
    for d, mr, mi in steps:
        shift = SUBLANES - d if reverse else d
        sr, si = pltpu.roll(xr, shift, 0), pltpu.roll(xi, shift, 0)
        pr, pi = _cmul(mr, mi, sr, si)
        xr, xi = xr + pr, xi + pi
    pr, pi = _cmul(cr, ci, carry_r, carry_i)
    return xr + pr, xi + pi


def _gelu(x):
    c = math.sqrt(2.0 / math.pi)
    return 0.5 * x * (1.0 + jnp.tanh(c * (x + 0.044715 * x * x * x)))


def _gelu_grad(x):
    c = math.sqrt(2.0 / math.pi)
    t = jnp.tanh(c * (x + 0.044715 * x * x * x))
    return 0.5 * (1.0 + t) + 0.5 * x * (1.0 - t * t) * c * (1.0 + 3.0 * 0.044715 * x * x)


def _s5_fwd(proj, wb, wc, d_skip, abar):
    rows = proj.shape[0]
    nb = wb.shape[0]
    s2 = 2 * STATE_PER_BATCH
    st = STATE_PER_BATCH
    chunk = _tile(rows, 512, SUBLANES)

    def body(u_ref, wb_ref, wc_ref, d_ref, a_ref, s_ref, y_ref, yg_ref):
        for c0 in range(0, rows, chunk):
            s_ref[pl.ds(c0, chunk), :] = _dot_nn(u_ref[pl.ds(c0, chunk), :].astype(BF16), wb_ref[...])
        av = a_ref[...]
        coefs = _scan_coefs(av[:, :st], av[:, st:], reverse=False)

        def tile(b, carry):
            r0 = pl.multiple_of(b * SUBLANES, SUBLANES)
            xr, xi = _scan_tile(s_ref[pl.ds(r0, SUBLANES), :st], s_ref[pl.ds(r0, SUBLANES), st:], carry[0], carry[1],
                                coefs, False)
            s_ref[pl.ds(r0, SUBLANES), :st] = xr
            s_ref[pl.ds(r0, SUBLANES), st:] = xi
            return xr[SUBLANES - 1:, :], xi[SUBLANES - 1:, :]

        zero = jnp.zeros((1, st), F32)
        lax.fori_loop(0, rows // SUBLANES, tile, (zero, zero))
        for c0 in range(0, rows, chunk):
            y = _dot_nn(s_ref[pl.ds(c0, chunk), :].astype(BF16), wc_ref[...]) + d_ref[...] * u_ref[pl.ds(c0, chunk), :]
            y_ref[pl.ds(c0, chunk), :] = y
            yg_ref[pl.ds(c0, chunk), :] = _gelu(y).astype(BF16)

    return pl.pallas_call(
        body, name="s5_fwd", grid=(nb,),
        in_specs=[pl.BlockSpec((rows, LANES), lambda j: (0, j)), pl.BlockSpec((None, LANES, s2), lambda j: (j, 0, 0)),
                  pl.BlockSpec((None, s2, LANES), lambda j: (j, 0, 0)), pl.BlockSpec((1, LANES), lambda j: (0, j)),
                  pl.BlockSpec((None, 1, s2), lambda j: (j, 0, 0))],
        out_specs=[pl.BlockSpec((rows, s2), lambda j: (0, j)), pl.BlockSpec((rows, LANES), lambda j: (0, j)),
                   pl.BlockSpec((rows, LANES), lambda j: (0, j))],
        out_shape=[jax.ShapeDtypeStruct((rows, nb * s2), F32), jax.ShapeDtypeStruct((rows, nb * LANES), F32),
                   jax.ShapeDtypeStruct((rows, nb * LANES), BF16)],
        compiler_params=_params(("parallel",)),
    )(proj, wb, wc, d_skip, abar)


def _s5_bwd(proj, states, y_pre, dyg_a, dyg_b, wb, wc, d_skip, abar):
    rows = proj.shape[0]
    nb = wb.shape[0]
    s2 = 2 * STATE_PER_BATCH
    st = STATE_PER_BATCH
    chunk = _tile(rows, 512, SUBLANES)
    n_tiles = rows // SUBLANES

    def body(u_ref, s_ref, y_ref, ga_ref, gb_ref, wb_ref, wc_ref, d_ref, a_ref,
             du_ref, dwb_ref, dwc_ref, da_ref, dd_ref, ds_ref, dy_ref):
        dy_ref[...] = (ga_ref[...] + gb_ref[...]) * _gelu_grad(y_ref[...])
        dd_ref[...] = jnp.sum(dy_ref[...] * u_ref[...], axis=0, keepdims=True)
        for c0 in range(0, rows, chunk):
            ds_ref[pl.ds(c0, chunk), :] = _dot_nt(dy_ref[pl.ds(c0, chunk), :].astype(BF16), wc_ref[...])
        dwc_ref[...] = _dot_tn(s_ref[...].astype(BF16), dy_ref[...].astype(BF16))
        av = a_ref[...]
        coefs = _scan_coefs(av[:, :st], -av[:, st:], reverse=True)
        row = lax.broadcasted_iota(jnp.int32, (SUBLANES, st), 0)

        def tile(k, carry):
            cr, ci, acc_r, acc_i = carry
            b = n_tiles - 1 - k
            r0 = pl.multiple_of(b * SUBLANES, SUBLANES)
            rp = pl.multiple_of(jnp.maximum(b - 1, 0) * SUBLANES, SUBLANES)
            xr, xi = _scan_tile(ds_ref[pl.ds(r0, SUBLANES), :st], ds_ref[pl.ds(r0, SUBLANES), st:], cr, ci, coefs, True)
            ds_ref[pl.ds(r0, SUBLANES), :st] = xr
            ds_ref[pl.ds(r0, SUBLANES), st:] = xi
            first = jnp.where(b > 0, 1.0, 0.0)
            pr = jnp.where(row == 0, pltpu.roll(s_ref[pl.ds(rp, SUBLANES), :st], 1, 0) * first,
                           pltpu.roll(s_ref[pl.ds(r0, SUBLANES), :st], 1, 0))
            pi = jnp.where(row == 0, pltpu.roll(s_ref[pl.ds(rp, SUBLANES), st:], 1, 0) * first,
                           pltpu.roll(s_ref[pl.ds(r0, SUBLANES), st:], 1, 0))
            acc_r = acc_r + pr * xr + pi * xi
            acc_i = acc_i + pr * xi - pi * xr
            return xr[:1, :], xi[:1, :], acc_r, acc_i

        zero = jnp.zeros((1, st), F32)
        zacc = jnp.zeros((SUBLANES, st), F32)
        _, _, acc_r, acc_i = lax.fori_loop(0, n_tiles, tile, (zero, zero, zacc, zacc))
        da_ref[:, :st] = jnp.sum(acc_r, axis=0, keepdims=True)
        da_ref[:, st:] = jnp.sum(acc_i, axis=0, keepdims=True)
        for c0 in range(0, rows, chunk):
            du_ref[pl.ds(c0, chunk), :] = (_dot_nt(ds_ref[pl.ds(c0, chunk), :].astype(BF16), wb_ref[...])
                                           + d_ref[...] * dy_ref[pl.ds(c0, chunk), :])
        dwb_ref[...] = _dot_tn(u_ref[...].astype(BF16), ds_ref[...].astype(BF16))

    col = pl.BlockSpec((rows, LANES), lambda j: (0, j))
    return pl.pallas_call(
        body, name="s5_bwd", grid=(nb,),
        in_specs=[col, pl.BlockSpec((rows, s2), lambda j: (0, j)), col, col, col,
                  pl.BlockSpec((None, LANES, s2), lambda j: (j, 0, 0)), pl.BlockSpec((None, s2, LANES), lambda j: (j, 0, 0)),
                  pl.BlockSpec((1, LANES), lambda j: (0, j)), pl.BlockSpec((None, 1, s2), lambda j: (j, 0, 0))],
        out_specs=[col, pl.BlockSpec((None, LANES, s2), lambda j: (j, 0, 0)),
                   pl.BlockSpec((None, s2, LANES), lambda j: (j, 0, 0)), pl.BlockSpec((None, 1, s2), lambda j: (j, 0, 0)),
                   pl.BlockSpec((1, LANES), lambda j: (0, j))],
        out_shape=[jax.ShapeDtypeStruct((rows, nb * LANES), F32), jax.ShapeDtypeStruct((nb, LANES, s2), F32),
                   jax.ShapeDtypeStruct((nb, s2, LANES), F32), jax.ShapeDtypeStruct((nb, 1, s2), F32),
                   jax.ShapeDtypeStruct((1, nb * LANES), F32)],
        scratch_shapes=[pltpu.VMEM((rows, s2), F32), pltpu.VMEM((rows, LANES), F32)],
        compiler_params=_params(("parallel",)),
    )(proj, states, y_pre, dyg_a, dyg_b, wb, wc, d_skip, abar)


def _glu_norm_fwd(y_pre, z, w, *, tr=256):
    rows, width = y_pre.shape
    tr = _tile(rows, tr, SUBLANES)

    def body(y_ref, z_ref, w_ref, o_ref):
        v = _gelu(y_ref[...]) * jax.nn.sigmoid(z_ref[...])
        o_ref[...] = (v * _rms_rows(v) * w_ref[...]).astype(o_ref.dtype)

    blk = pl.BlockSpec((tr, width), lambda i: (i, 0))
    return pl.pallas_call(
        body, name="glu_norm_fwd", grid=(rows // tr,),
        in_specs=[blk, blk, pl.BlockSpec((1, width), lambda i: (0, 0))], out_specs=blk,
        out_shape=jax.ShapeDtypeStruct((rows, width), BF16), compiler_params=_params(("parallel",)),
    )(y_pre, z, w)


def _glu_norm_bwd(y_pre, z, w, dycat, *, tr=256):
    rows, width = y_pre.shape
    tr = _tile(rows, tr, SUBLANES)

    def body(y_ref, z_ref, w_ref, dy_ref, dz_ref, dg_ref, dw_ref, db_ref):
        yg = _gelu(y_ref[...])
        sg = jax.nn.sigmoid(z_ref[...])
        dv, dwp = _rmsnorm_bwd_rows(yg * sg, w_ref[...], dy_ref[...])
        dz = dv * yg * sg * (1.0 - sg)
        dz_ref[...] = dz
        dg_ref[...] = dv * sg
        dw_part = jnp.sum(dwp, axis=0, keepdims=True)
        db_part = jnp.sum(dz, axis=0, keepdims=True)

        @pl.when(pl.program_id(0) == 0)
        def _():
            dw_ref[...] = dw_part
            db_ref[...] = db_part

        @pl.when(pl.program_id(0) > 0)
        def _():
            dw_ref[...] += dw_part
            db_ref[...] += db_part

    blk = pl.BlockSpec((tr, width), lambda i: (i, 0))
    vec = pl.BlockSpec((1, width), lambda i: (0, 0))
    return pl.pallas_call(
        body, name="glu_norm_bwd", grid=(rows // tr,), in_specs=[blk, blk, vec, blk], out_specs=[blk, blk, vec, vec],
        out_shape=[jax.ShapeDtypeStruct((rows, width), F32)] * 2 + [jax.ShapeDtypeStruct((1, width), F32)] * 2,
        compiler_params=_params(("arbitrary",)),
    )(y_pre, z, w, dycat)


def _rope_tables(pos, freq, sign):
    rows = pos.shape[0]

    def body(p_ref, f_ref, s_ref, cos_ref, sin_ref):
        ang = p_ref[...] * f_ref[...]
        cos_ref[...] = jnp.cos(ang)
        sin_ref[...] = jnp.sin(ang) * s_ref[...]

    return pl.pallas_call(body, name="rope_tables", out_shape=[jax.ShapeDtypeStruct((rows, LANES), F32)] * 2)(pos, freq, sign)


def _rope(x, cos, sin_signed):
    lane = lax.broadcasted_iota(jnp.int32, x.shape, 1)
    half = QK_ROPE_DIM // 2
    swapped = jnp.where(lane < half, pltpu.roll(x, LANES - half, 1), pltpu.roll(x, half, 1))
    return x * cos + swapped * sin_signed


def _attn_prep(q, kv, proj, kpe_col, cos, sin, *, tr=256):
    rows = q.shape[0]
    heads = q.shape[1] // HEAD_SLOT
    tr = _tile(rows, tr, SUBLANES)

    def body(q_ref, kv_ref, kpe_ref, cos_ref, sin_ref, qc_ref, kc_ref, v_ref):
        c, s = cos_ref[...], sin_ref[...]
        qc_ref[:, :LANES] = q_ref[:, :LANES].astype(BF16)
        qc_ref[:, LANES:] = _rope(q_ref[:, LANES:], c, s).astype(BF16)
        kc_ref[:, :LANES] = kv_ref[:, :LANES].astype(BF16)
        kc_ref[:, LANES:] = _rope(kpe_ref[...], c, s).astype(BF16)
        v_ref[...] = kv_ref[:, LANES:].astype(BF16)

    slot = pl.BlockSpec((tr, HEAD_SLOT), lambda i, h: (i, h))
    tab = pl.BlockSpec((tr, LANES), lambda i, h: (i, 0))
    return pl.pallas_call(
        body, name="attn_prep", grid=(rows // tr, heads),
        in_specs=[slot, slot, pl.BlockSpec((tr, LANES), lambda i, h: (i, kpe_col)), tab, tab],
        out_specs=[slot, slot, pl.BlockSpec((tr, LANES), lambda i, h: (i, h))],
        out_shape=[jax.ShapeDtypeStruct((rows, heads * HEAD_SLOT), BF16)] * 2
        + [jax.ShapeDtypeStruct((rows, heads * LANES), BF16)],
        compiler_params=_params(("parallel", "parallel")),
    )(q, kv, proj, cos, sin)


def _causal(i, j, tq, tk):
    qpos = i * tq + lax.broadcasted_iota(jnp.int32, (tq, tk), 0)
    kpos = j * tk + lax.broadcasted_iota(jnp.int32, (tq, tk), 1)
    return kpos <= qpos


def _attn_fwd(qc, kc, vb, *, scale, tq=512):
    rows = qc.shape[0]
    heads = qc.shape[1] // HEAD_SLOT
    tq = _tile(rows, tq, SUBLANES)
    tk = tq

    def body(q_ref, k_ref, v_ref, o_ref, lse_ref):
        i = pl.program_id(1)
        q = q_ref[...]

        def step(j, carry):
            m, l, acc = carry
            k0 = pl.multiple_of(j * tk, tk)
            s = _dot_nt(q, k_ref[pl.ds(k0, tk), :]) * scale
            s = jnp.where(_causal(i, j, tq, tk), s, NEG_INF)
            m_new = jnp.maximum(m, jnp.max(s, axis=-1, keepdims=True))
            p = jnp.exp(s - m_new)
            alpha = jnp.exp(m - m_new)
            l = alpha * l + jnp.sum(p, axis=-1, keepdims=True)
            acc = alpha * acc + _dot_nn(p.astype(BF16), v_ref[pl.ds(k0, tk), :])
            return m_new, l, acc

        init = (jnp.full((tq, 1), NEG_INF, F32), jnp.zeros((tq, 1), F32), jnp.zeros((tq, LANES), F32))
        m, l, acc = lax.fori_loop(0, i + 1, step, init)
        o_ref[...] = acc / l
        lse_ref[...] = jnp.broadcast_to(m + jnp.log(l), (tq, LANES))

    return pl.pallas_call(
        body, name="attn_fwd", grid=(heads, rows // tq),
        in_specs=[pl.BlockSpec((tq, HEAD_SLOT), lambda h, i: (i, h)), pl.BlockSpec((rows, HEAD_SLOT), lambda h, i: (0, h)),
                  pl.BlockSpec((rows, LANES), lambda h, i: (0, h))],
        out_specs=[pl.BlockSpec((tq, LANES), lambda h, i: (i, h))] * 2,
        out_shape=[jax.ShapeDtypeStruct((rows, heads * LANES), F32)] * 2,
        compiler_params=_params(("parallel", "parallel")),
    )(qc, kc, vb)


def _attn_bwd_q(qc, kc, vb, o, do, lse, cos, sin, *, scale, tq=512):
    rows = qc.shape[0]
    heads = qc.shape[1] // HEAD_SLOT
    tq = _tile(rows, tq, SUBLANES)
    tk = tq

    def body(q_ref, k_ref, v_ref, o_ref, do_ref, lse_ref, cos_ref, sin_ref, dq_ref, delta_ref):
        i = pl.program_id(1)
        q = q_ref[...]
        dov = do_ref[...]
        delta = jnp.sum(dov * o_ref[...], axis=-1, keepdims=True)
        delta_ref[...] = jnp.broadcast_to(delta, (tq, LANES))
        dob = dov.astype(BF16)
        lse_col = lse_ref[:, :1]

        def step(j, dq):
            k0 = pl.multiple_of(j * tk, tk)
            kb = k_ref[pl.ds(k0, tk), :]
            s = _dot_nt(q, kb) * scale
            p = jnp.where(_causal(i, j, tq, tk), jnp.exp(s - lse_col), 0.0)
            dp = _dot_nt(dob, v_ref[pl.ds(k0, tk), :])
            ds = p * (dp - delta)
            return dq + _dot_nn(ds.astype(BF16), kb)

        dq = lax.fori_loop(0, i + 1, step, jnp.zeros((tq, HEAD_SLOT), F32)) * scale
        dq_ref[:, :LANES] = dq[:, :LANES]
        dq_ref[:, LANES:] = _rope(dq[:, LANES:], cos_ref[...], -sin_ref[...])

    qblk = pl.BlockSpec((tq, HEAD_SLOT), lambda h, i: (i, h))
    vblk = pl.BlockSpec((tq, LANES), lambda h, i: (i, h))
    tab = pl.BlockSpec((tq, LANES), lambda h, i: (i, 0))
    return pl.pallas_call(
        body, name="attn_bwd_q", grid=(heads, rows // tq),
        in_specs=[qblk, pl.BlockSpec((rows, HEAD_SLOT), lambda h, i: (0, h)), pl.BlockSpec((rows, LANES), lambda h, i: (0, h)),
                  vblk, vblk, vblk, tab, tab],
        out_specs=[qblk, vblk],
        out_shape=[jax.ShapeDtypeStruct((rows, heads * HEAD_SLOT), F32), jax.ShapeDtypeStruct((rows, heads * LANES), F32)],
        compiler_params=_params(("parallel", "parallel")),
    )(qc, kc, vb, o, do, lse, cos, sin)


def _attn_bwd_kv(qc, kc, vb, do, lse, delta, cos, sin, *, scale, tk=512):
    rows = qc.shape[0]
    heads = qc.shape[1] // HEAD_SLOT
    tk = _tile(rows, tk, SUBLANES)
    tq = tk
    nq = rows // tq

    def body(q_ref, k_ref, v_ref, do_ref, lse_ref, delta_ref, cos_ref, sin_ref, dkv_ref, dkpe_ref):
        j, h = pl.program_id(0), pl.program_id(1)
        kb, vv = k_ref[...], v_ref[...]

        def step(i, carry):
            dk, dv = carry
            q0 = pl.multiple_of(i * tq, tq)
            qb = q_ref[pl.ds(q0, tq), :]
            dob = do_ref[pl.ds(q0, tq), :].astype(BF16)
            s = _dot_nt(qb, kb) * scale
            p = jnp.where(_causal(i, j, tq, tk), jnp.exp(s - lse_ref[pl.ds(q0, tq), :1]), 0.0)
            dv = dv + _dot_tn(p.astype(BF16), dob)
            ds = p * (_dot_nt(dob, vv) - delta_ref[pl.ds(q0, tq), :1])
            dk = dk + _dot_tn(ds.astype(BF16), qb)
            return dk, dv

        dk, dv = lax.fori_loop(j, nq, step, (jnp.zeros((tk, HEAD_SLOT), F32), jnp.zeros((tk, LANES), F32)))
        dkv_ref[:, :LANES] = dk[:, :LANES] * scale
        dkv_ref[:, LANES:] = dv
        part = dk[:, LANES:] * scale

        @pl.when(h == 0)
        def _():
            dkpe_ref[...] = part

        @pl.when(h > 0)
        def _():
            dkpe_ref[...] += part

        @pl.when(h == heads - 1)
        def _():
            dkpe_ref[...] = _rope(dkpe_ref[...], cos_ref[...], -sin_ref[...])

    full_q = pl.BlockSpec((rows, HEAD_SLOT), lambda j, h: (0, h))
    full_v = pl.BlockSpec((rows, LANES), lambda j, h: (0, h))
    tab = pl.BlockSpec((tk, LANES), lambda j, h: (j, 0))
    return pl.pallas_call(
        body, name="attn_bwd_kv", grid=(rows // tk, heads),
        in_specs=[full_q, pl.BlockSpec((tk, HEAD_SLOT), lambda j, h: (j, h)), pl.BlockSpec((tk, LANES), lambda j, h: (j, h)),
                  full_v, full_v, full_v, tab, tab],
        out_specs=[pl.BlockSpec((tk, HEAD_SLOT), lambda j, h: (j, h)), pl.BlockSpec((tk, LANES), lambda j, h: (j, 0))],
        out_shape=[jax.ShapeDtypeStruct((rows, heads * HEAD_SLOT), F32), jax.ShapeDtypeStruct((rows, LANES), F32)],
        compiler_params=_params(("parallel", "arbitrary")),
    )(qc, kc, vb, do, lse, delta, cos, sin)


def _shift_down(x, d):
    row = lax.broadcasted_iota(jnp.int32, x.shape, 0)
    return jnp.where(row >= d, pltpu.roll(x, d, 0), 0.0)


def _shift_up(x, d):
    rows = x.shape[0]
    row = lax.broadcasted_iota(jnp.int32, x.shape, 0)
    return jnp.where(row < rows - d, pltpu.roll(x, rows - d, 0), 0.0)


def _conv3(a, w, b):
    return w[2:3, :] * a + w[1:2, :] * _shift_down(a, 1) + w[0:1, :] * _shift_down(a, 2) + b


def _conv_gate_fwd(a, conv_w, conv_b, *, tc=256):
    rows, f2 = a.shape
    f = f2 // 2
    tc = _tile(f, tc)
    nc = f // tc

    def body(ag_ref, av_ref, wg_ref, wv_ref, bg_ref, bv_ref, o_ref):
        gate = _conv3(ag_ref[...], wg_ref[...], bg_ref[...])
        val = _conv3(av_ref[...], wv_ref[...], bv_ref[...])
        o_ref[...] = (gate * jax.nn.sigmoid(gate) * val).astype(o_ref.dtype)

    return pl.pallas_call(
        body, name="conv_gate_fwd", grid=(nc,),
        in_specs=[pl.BlockSpec((rows, tc), lambda j: (0, j)), pl.BlockSpec((rows, tc), lambda j: (0, j + nc)),
                  pl.BlockSpec((SUBLANES, tc), lambda j: (0, j)), pl.BlockSpec((SUBLANES, tc), lambda j: (0, j + nc)),
                  pl.BlockSpec((1, tc), lambda j: (0, j)), pl.BlockSpec((1, tc), lambda j: (0, j + nc))],
        out_specs=pl.BlockSpec((rows, tc), lambda j: (0, j)),
        out_shape=jax.ShapeDtypeStruct((rows, f), BF16), compiler_params=_params(("parallel",)),
    )(a, a, conv_w, conv_w, conv_b, conv_b)


def _conv_gate_bwd(a, conv_w, conv_b, dg, *, tc=256):
    rows, f2 = a.shape
    f = f2 // 2
    tc = _tile(f, tc)
    nc = f // tc

    def conv_bwd(a_val, w, d_out):
        da = w[2:3, :] * d_out + w[1:2, :] * _shift_up(d_out, 1) + w[0:1, :] * _shift_up(d_out, 2)
        db = jnp.sum(d_out, axis=0, keepdims=True)
        row = lax.broadcasted_iota(jnp.int32, (SUBLANES, a_val.shape[1]), 0)
        dw = jnp.zeros((SUBLANES, a_val.shape[1]), F32)
        for tap in range(3):
            t = jnp.sum(d_out * (_shift_down(a_val, 2 - tap) if tap < 2 else a_val), axis=0, keepdims=True)
            dw = jnp.where(row == tap, t, dw)
        return da, dw, db

    def body(ag_ref, av_ref, wg_ref, wv_ref, bg_ref, bv_ref, dg_ref, da_ref, dw_ref, db_ref):
        ag, av, wg, wv = ag_ref[...], av_ref[...], wg_ref[...], wv_ref[...]
        gate = _conv3(ag, wg, bg_ref[...])
        val = _conv3(av, wv, bv_ref[...])
        sg = jax.nn.sigmoid(gate)
        dgv = dg_ref[...]
        d_gate = dgv * val * sg * (1.0 + gate * (1.0 - sg))
        d_val = dgv * gate * sg
        for half, (a_val, w, d_out) in enumerate(((ag, wg, d_gate), (av, wv, d_val))):
            da, dw, db = conv_bwd(a_val, w, d_out)
            da_ref[half] = da.astype(da_ref.dtype)
            dw_ref[half] = dw
            db_ref[half] = db

    lo = lambda j: (0, j)
    hi = lambda j: (0, j + nc)
    both = lambda j: (0, 0, j)
    return pl.pallas_call(
        body, name="conv_gate_bwd", grid=(nc,),
        in_specs=[pl.BlockSpec((rows, tc), lo), pl.BlockSpec((rows, tc), hi), pl.BlockSpec((SUBLANES, tc), lo),
                  pl.BlockSpec((SUBLANES, tc), hi), pl.BlockSpec((1, tc), lo), pl.BlockSpec((1, tc), hi),
                  pl.BlockSpec((rows, tc), lo)],
        out_specs=[pl.BlockSpec((2, rows, tc), both), pl.BlockSpec((2, SUBLANES, tc), both), pl.BlockSpec((2, 1, tc), both)],
        out_shape=[jax.ShapeDtypeStruct((2, rows, f), BF16), jax.ShapeDtypeStruct((2, SUBLANES, f), F32),
                   jax.ShapeDtypeStruct((2, 1, f), F32)],
        compiler_params=_params(("parallel",)),
    )(a, a, conv_w, conv_w, conv_b, conv_b, dg)


def _wgrad(a, b, rows, cols, row_sharded, name, **kw):
    make, (sr, sc) = _wgrad_blocks(rows, cols, row_sharded)
    tm = kw.pop("tm", _tile(sr, 512))
    tn = kw.pop("tn", _tile(sc, 1024))
    return _matmul(a, b, mode="tn", name=name, tm=tm, tn=tn, out_blocks=make, **kw)


def _block_diag(x):
    nb, g, r, c = x.shape
    eye = jnp.eye(g, dtype=x.dtype)
    return (x[:, :, :, None, :] * eye[None, :, None, :, None]).reshape(nb, g * r, g * c)


def _block_diag_part(x, r, c):
    nb = x.shape[0]
    g = GROUPS_PER_BATCH
    eye = jnp.eye(g, dtype=x.dtype)
    return jnp.sum(x.reshape(nb, g, r, g, c) * eye[None, :, None, :, None], axis=3)


class _NoExchange:
    def __init__(self, ffn):
        self.ffn = ffn

    def ffn_weights(self, after):
        return self.ffn

    def ffn_grads(self, g_down, g_up, after):
        return None

    def ffn_backward_done(self, after):
        return None


def _local_step(x, posf, target, w, hooks):
    rows, d = x.shape
    width = w["ssm_d"].shape[1]
    qr, kvr = w["mla_q_norm_w"].shape[1], w["mla_kv_norm_w"].shape[1]
    heads = w["mla_w_ukv"].shape[1] // HEAD_SLOT
    f2 = w["ffn_conv_b"].shape[1]
    inp = w["w_in"].shape[1]
    groups = width // SSM_GROUP
    nb = groups // GROUPS_PER_BATCH
    scale = (QK_NOPE_DIM + QK_ROPE_DIM) ** -0.5
    g = {}

    hn = _rmsnorm_fwd(x, w["attn_norm_w"], name="attn_norm")
    proj = _matmul(hn, w["w_in"], mode="nn", name="in_proj")

    ar, ai, bbr, bbi = _s5_params(w["ssm_lambda_re"], w["ssm_lambda_im"], w["ssm_log_dt"], w["ssm_b_re"], w["ssm_b_im"])

    def b_band(bb):
        return _block_diag(bb.reshape(nb, GROUPS_PER_BATCH, SSM_STATE, SSM_GROUP).transpose(0, 1, 3, 2))

    def c_band(c):
        return _block_diag(c.reshape(nb, GROUPS_PER_BATCH, SSM_GROUP, SSM_STATE).transpose(0, 1, 3, 2))

    wb = jnp.concatenate([b_band(bbr), b_band(bbi)], axis=2).astype(BF16)
    wc = jnp.concatenate([c_band(w["ssm_c_re"]), -c_band(w["ssm_c_im"])], axis=1).astype(BF16)
    abar = jnp.concatenate([ar.reshape(nb, 1, STATE_PER_BATCH), ai.reshape(nb, 1, STATE_PER_BATCH)], axis=2)
    states, y_pre, yg = _s5_fwd(proj, wb, wc, w["ssm_d"], abar)
    z = _matmul(yg, w["ssm_w_glu"], mode="nn", name="glu_proj", bias=w["ssm_b_glu"])
    ys = _glu_norm_fwd(y_pre, z, w["ssm_out_norm_w"])

    q_col, kv_col, kpe_col = width // qr, (width + qr) // kvr, (width + qr + kvr) // LANES
    assert width % qr == 0 and (width + qr) % kvr == 0
    qn = _rmsnorm_fwd(proj, w["mla_q_norm_w"], name="q_norm", width=qr, col=q_col)
    kvn = _rmsnorm_fwd(proj, w["mla_kv_norm_w"], name="kv_norm", width=kvr, col=kv_col)
    q = _matmul(qn, w["mla_w_uq"], mode="nn", name="q_proj")
    kv = _matmul(kvn, w["mla_w_ukv"], mode="nn", name="kv_proj")
    half = QK_ROPE_DIM // 2
    inv_freq = ROPE_THETA ** (-jnp.arange(0, QK_ROPE_DIM, 2, dtype=F32) / QK_ROPE_DIM)
    zeros = jnp.zeros((LANES - QK_ROPE_DIM,), F32)
    freq = jnp.concatenate([inv_freq, inv_freq, zeros]).reshape(1, LANES)
    sign = jnp.concatenate([-jnp.ones((half,), F32), jnp.ones((half,), F32), zeros]).reshape(1, LANES)
    cos, sin = _rope_tables(posf, freq, sign)
    qc, kc, vb = _attn_prep(q, kv, proj, kpe_col, cos, sin)
    o, lse = _attn_fwd(qc, kc, vb, scale=scale, tq=ATTN_BLOCK)
    ym = _rmsnorm_fwd(o, w["mla_out_norm_w"], name="mla_out_norm")
    ycat = jnp.concatenate([ys, ym], axis=1)
    h1 = _matmul(ycat, w["w_out"], mode="nn", name="out_proj", add=x)

    hn2 = _rmsnorm_fwd(h1, w["ffn_norm_w"], name="ffn_norm")
    ffn = hooks.ffn_weights(hn2)
    a = _matmul(hn2, ffn["ffn_w_up"], mode="nn", name="ffn_up", tm=FFN_ROWS)
    gated = _conv_gate_fwd(a, ffn["ffn_conv_w"], w["ffn_conv_b"])
    h2 = _matmul(gated, ffn["ffn_w_down"], mode="nn", name="ffn_down", add=h1, tk=2816, tm=FFN_ROWS)
    loss_tile, dh2, dh2_mxu, g["final_norm_w"] = _final_norm_loss(h2, w["final_norm_w"], target)

    dgated = _matmul(dh2_mxu, ffn["ffn_w_down"], mode="nt", name="ffn_down_dx", tm=FFN_ROWS)
    g["ffn_w_down"] = _wgrad(gated, dh2_mxu, f2 // 2, d, True, "ffn_down_dw", tm=f2 // 2 // N_CHIPS, tn=1024)
    da, dcw, dcb = _conv_gate_bwd(a, ffn["ffn_conv_w"], w["ffn_conv_b"], dgated)
    g["ffn_conv_w"] = jnp.concatenate([dcw[0, :3], dcw[1, :3]], axis=1)
    g["ffn_conv_b"] = jnp.concatenate([dcb[0], dcb[1]], axis=1)
    g["ffn_w_up"] = _wgrad(hn2, da, d, f2, False, "ffn_up_dw", b_split=True, tm=FFN_ROWS, tn=_tile(f2 // N_CHIPS, 1408))
    started = hooks.ffn_grads(g["ffn_w_down"], g["ffn_w_up"], dcb)
    ffn_norm_w = w["ffn_norm_w"] if started is None else w["ffn_norm_w"] + started[:1, :1]
    dhn2 = _matmul(da, ffn["ffn_w_up"], mode="nt", name="ffn_up_dx", a_split=True, tk=_tile(f2 // 2, 2816), tm=FFN_ROWS)
    dh1, dh1_mxu, g["ffn_norm_w"] = _rmsnorm_bwd(h1, ffn_norm_w, dhn2, name="ffn_norm_bwd", add=dh2, dx_dtypes=(F32, BF16))

    dycat = _matmul(dh1_mxu, w["w_out"], mode="nt", name="out_proj_dx")
    g["w_out"] = _wgrad(ycat, dh1_mxu, 2 * width, d, True, "out_proj_dw")
    started = hooks.ffn_backward_done(dycat)
    mla_out_norm_w, ssm_out_norm_w = w["mla_out_norm_w"], w["ssm_out_norm_w"]
    if started is not None:
        mla_out_norm_w, ssm_out_norm_w = mla_out_norm_w + started[:1, :1], ssm_out_norm_w + started[:1, :1]

    do, g["mla_out_norm_w"] = _rmsnorm_bwd(o, mla_out_norm_w, dycat, name="mla_out_norm_bwd", width=width, dy_col=1)
    dq, delta = _attn_bwd_q(qc, kc, vb, o, do, lse, cos, sin, scale=scale, tq=ATTN_BLOCK)
    dkv, dkpe = _attn_bwd_kv(qc, kc, vb, do, lse, delta, cos, sin, scale=scale, tk=ATTN_BLOCK)
    g["mla_w_uq"] = _wgrad(qn, dq, qr, heads * HEAD_SLOT, False, "q_proj_dw")
    dqn = _matmul(dq, w["mla_w_uq"], mode="nt", name="q_proj_dx")
    dcq, g["mla_q_norm_w"] = _rmsnorm_bwd(proj, w["mla_q_norm_w"], dqn, name="q_norm_bwd", width=qr, col=q_col)
    g["mla_w_ukv"] = _wgrad(kvn, dkv, kvr, heads * HEAD_SLOT, False, "kv_proj_dw")
    dkvn = _matmul(dkv, w["mla_w_ukv"], mode="nt", name="kv_proj_dx")
    dckv, g["mla_kv_norm_w"] = _rmsnorm_bwd(proj, w["mla_kv_norm_w"], dkvn, name="kv_norm_bwd", width=kvr, col=kv_col)

    dz, dyg_a, g["ssm_out_norm_w"], g["ssm_b_glu"] = _glu_norm_bwd(y_pre, z, ssm_out_norm_w, dycat)
    dyg_b = _matmul(dz, w["ssm_w_glu"], mode="nt", name="glu_proj_dx")
    g["ssm_w_glu"] = _wgrad(yg, dz, width, width, True, "glu_proj_dw")
    du, dwb, dwc, dabar, g["ssm_d"] = _s5_bwd(proj, states, y_pre, dyg_a, dyg_b, wb, wc, w["ssm_d"], abar)

    def b_unband(x):
        return _block_diag_part(x, SSM_GROUP, SSM_STATE).transpose(0, 1, 3, 2).reshape(groups, SSM_STATE * SSM_GROUP)

    def c_unband(x):
        return _block_diag_part(x, SSM_STATE, SSM_GROUP).transpose(0, 1, 3, 2).reshape(groups, SSM_GROUP, SSM_STATE)

    st = STATE_PER_BATCH
    g["ssm_c_re"] = c_unband(dwc[:, :st, :])
    g["ssm_c_im"] = -c_unband(dwc[:, st:, :])
    d_ar = dabar[:, 0, :st].reshape(groups, SSM_STATE)
    d_ai = dabar[:, 0, st:].reshape(groups, SSM_STATE)
    (g["ssm_lambda_re"], g["ssm_lambda_im"], g["ssm_log_dt"], g["ssm_b_re"], g["ssm_b_im"]) = _s5_params_bwd(
        w["ssm_lambda_re"], w["ssm_lambda_im"], w["ssm_log_dt"], w["ssm_b_re"], w["ssm_b_im"], d_ar, d_ai,
        b_unband(dwb[:, :, :st]), b_unband(dwb[:, :, st:]))

    pad = jnp.zeros((rows, inp - (width + qr + kvr + LANES)), F32)
    dproj = jnp.concatenate([du, dcq, dckv, dkpe, pad], axis=1)
    g["w_in"] = _wgrad(hn, dproj, d, inp, True, "in_proj_dw")
    dhn = _matmul(dproj, w["w_in"], mode="nt", name="in_proj_dx")
    dx, g["attn_norm_w"] = _rmsnorm_bwd(x, w["attn_norm_w"], dhn, name="attn_norm_bwd", add=dh1)
    return loss_tile, dx, g


ANY = pl.BlockSpec(memory_space=pl.ANY)
MESH = pl.DeviceIdType.MESH


def _mesh_pos():
    return lax.axis_index("x"), lax.axis_index("y"), lax.axis_index("c")


def _other_chips(x, y):
    return [(1 - x, y), (x, 1 - y), (1 - x, 1 - y)]


def _remote(src, dst, send_sems, recv_sems, k, to):
    return pltpu.make_async_remote_copy(src_ref=src, dst_ref=dst, send_sem=send_sems.at[k], recv_sem=recv_sems.at[k],
                                        device_id=to, device_id_type=MESH)


def _place_shard(shard, piece_idx, row_sharded, name, out_dtype=BF16, pieces=N_CHIPS):
    rs, cs = shard.shape
    tr = _tile(rs, 256, 2 * SUBLANES)
    rb = rs // tr

    def body(p_ref, x_ref, o_ref):
        o_ref[...] = x_ref[...].astype(o_ref.dtype)

    if row_sharded:
        out_shape, out_map = (pieces * rs, cs), (lambda i, p_ref: (p_ref[0] * rb + i, 0))
    else:
        out_shape, out_map = (rs, pieces * cs), (lambda i, p_ref: (i, p_ref[0]))
    return pl.pallas_call(
        body, name=name, out_shape=jax.ShapeDtypeStruct(out_shape, out_dtype),
        grid_spec=pltpu.PrefetchScalarGridSpec(
            num_scalar_prefetch=1, grid=(rb,), in_specs=[pl.BlockSpec((tr, cs), lambda i, p_ref: (i, 0))],
            out_specs=pl.BlockSpec((tr, cs), out_map)),
        compiler_params=_params(("parallel",)),
    )(piece_idx, shard)


def _gather_weights(placed, name):
    n = len(placed)
    meta = [(row_sharded, direct) for _, row_sharded, direct in placed]
    over_ici, over_d2d = _gather_plans(meta)
    forwarded = [t for t, (_, direct) in enumerate(meta) if not direct]

    def body(*refs):
        outs = refs[n:2 * n]
        send_sems, recv_sems, pass_send_sems, pass_recv_sems = refs[2 * n:]
        first, arrivals = over_ici(outs, send_sems, recv_sems)
        passed, passed_arrivals = over_d2d([outs[t] for t in forwarded], pass_send_sems, pass_recv_sems)
        for cp in first:
            cp.start()
        for t in range(n):
            for j in range(3):
                arrivals[3 * t + j].wait_recv()
                if t in forwarded:
                    passed[3 * forwarded.index(t) + j].start()
        for cp in passed_arrivals:
            cp.wait_recv()
        for cp in first + passed:
            cp.wait_send()

    return pl.pallas_call(
        body, name=name, in_specs=[ANY] * n, out_specs=[ANY] * n,
        out_shape=[jax.ShapeDtypeStruct(arr.shape, arr.dtype) for arr, _, _ in placed],
        input_output_aliases={t: t for t in range(n)},
        scratch_shapes=[pltpu.SemaphoreType.DMA((3 * n,)), pltpu.SemaphoreType.DMA((3 * n,)),
                        pltpu.SemaphoreType.DMA((3 * len(forwarded),)), pltpu.SemaphoreType.DMA((3 * len(forwarded),))],
    )(*[arr for arr, _, _ in placed])


def _gather_plans(meta):
    def window(ref, row_sharded, piece, half):
        r, cc = ref.shape
        if row_sharded:
            rs = r // N_CHIPS
            if half is None:
                return ref.at[pl.ds(piece * rs, rs), :]
            return ref.at[pl.ds(piece * rs + half * (rs // 2), rs // 2), :]
        cs = cc // N_CHIPS
        if half is None:
            return ref.at[:, pl.ds(piece * cs, cs)]
        return ref.at[pl.ds(half * (r // 2), r // 2), pl.ds(piece * cs, cs)]

    def over_ici(refs, send_sems, recv_sems):
        x, y, c = _mesh_pos()
        sends, recvs = [], []
        for t, (row_sharded, direct) in enumerate(meta):
            mine = window(refs[t], row_sharded, 2 * x + y, None if direct else c)
            for j, (px, py) in enumerate(_other_chips(x, y)):
                theirs = window(refs[t], row_sharded, 2 * px + py, None if direct else c)
                sends.append(_remote(mine, mine, send_sems, recv_sems, 3 * t + j, (px, py, c)))
                recvs.append(_remote(theirs, theirs, send_sems, recv_sems, 3 * t + j, (px, py, c)))
        return sends, recvs

    def over_d2d(refs, send_sems, recv_sems):
        x, y, c = _mesh_pos()
        sends, recvs = [], []
        rows = [row_sharded for row_sharded, direct in meta if not direct]
        for t, row_sharded in enumerate(rows):
            for j, (px, py) in enumerate(_other_chips(x, y)):
                got = window(refs[t], row_sharded, 2 * px + py, c)
                other = window(refs[t], row_sharded, 2 * px + py, 1 - c)
                sends.append(_remote(got, got, send_sems, recv_sems, 3 * t + j, (x, y, 1 - c)))
                recvs.append(_remote(other, other, send_sems, recv_sems, 3 * t + j, (x, y, 1 - c)))
        return sends, recvs

    return over_ici, over_d2d


HBM = pl.BlockSpec(memory_space=pltpu.HBM)
SEMAPHORES = pl.BlockSpec(memory_space=pltpu.SEMAPHORE)
DATAFLOW = pltpu.SideEffectType.DATAFLOW_SIDE_EFFECTING


def _start_copies(name, arrays, plan, n_copies, after):
    n = len(arrays)

    def body(*refs):
        sends, _ = plan(refs[:n], refs[n + 1], refs[n + 2])
        for cp in sends:
            cp.start()
        token = refs[2 * n + 3]
        token[...] = jnp.zeros_like(token)

    out = pl.pallas_call(
        body, name=name,
        out_shape=(pltpu.SemaphoreType.DMA((n_copies,)), pltpu.SemaphoreType.DMA((n_copies,)),
                   *[pltpu.HBM(a.shape, a.dtype) for a in arrays], jax.ShapeDtypeStruct((SUBLANES, LANES), F32)),
        in_specs=[HBM] * n + [ANY],
        out_specs=(SEMAPHORES, SEMAPHORES, *[HBM] * n, pl.BlockSpec(memory_space=pltpu.VMEM)),
        input_output_aliases={t: t + 2 for t in range(n)},
        compiler_params=pltpu.CompilerParams(has_side_effects=DATAFLOW),
    )(*[pltpu.with_memory_space_constraint(a, pltpu.HBM) for a in arrays], after)
    return out[0], out[1], list(out[2:2 + n]), out[2 + n]


def _wait_copies(name, started, plan, after):
    send_sems, recv_sems, arrays, _ = started
    n = len(arrays)

    def body(*refs):
        sends, recvs = plan(refs[:n], refs[n], refs[n + 1])
        for cp in sends:
            cp.wait_send()
        for cp in recvs:
            cp.wait_recv()

    out = pl.pallas_call(
        body, name=name, out_shape=[pltpu.HBM(a.shape, a.dtype) for a in arrays],
        in_specs=[HBM] * n + [SEMAPHORES, SEMAPHORES, ANY], out_specs=[HBM] * n,
        input_output_aliases={t: t for t in range(n)},
        compiler_params=pltpu.CompilerParams(has_side_effects=DATAFLOW),
    )(*arrays, send_sems, recv_sems, after)
    return list(out)


def _exchange(name, arrays, out_shapes, plan, n_copies, in_place=False, after=None):
    n = len(arrays)
    extra = [] if after is None else [after]

    def body(*refs):
        ins, outs = refs[:n], refs[n + len(extra):n + len(extra) + len(out_shapes)]
        send_sems, recv_sems = refs[n + len(extra) + len(out_shapes):]
        sends, recvs = plan(ins, outs, send_sems, recv_sems)
        for cp in sends:
            cp.start()
        for cp in recvs:
            cp.wait_recv()
        for cp in sends:
            cp.wait_send()

    return pl.pallas_call(
        body, name=name, in_specs=[ANY] * (n + len(extra)), out_specs=[ANY] * len(out_shapes), out_shape=out_shapes,
        input_output_aliases={t: t for t in range(n)} if in_place else {},
        scratch_shapes=[pltpu.SemaphoreType.DMA((n_copies,)), pltpu.SemaphoreType.DMA((n_copies,))],
    )(*arrays, *extra)


def _swap_plan(n):
    def plan(refs, send_sems, recv_sems):
        x, y, c = _mesh_pos()
        sends = [_remote(refs[t].at[1 - c], refs[n + t], send_sems, recv_sems, t, (x, y, 1 - c)) for t in range(n)]
        return sends, sends

    return plan


def _scatter_plan(n):
    def plan(refs, send_sems, recv_sems):
        x, y, c = _mesh_pos()
        sends = []
        for t in range(n):
            for j, (px, py) in enumerate(_other_chips(x, y)):
                sends.append(_remote(refs[t].at[2 * px + py], refs[n + t].at[j], send_sems, recv_sems, 3 * t + j, (px, py, c)))
        return sends, sends

    return plan


def _swap_shapes(grads):
    return [jax.ShapeDtypeStruct(g.shape[1:], g.dtype) for g in grads]


def _scatter_shapes(sums):
    return [jax.ShapeDtypeStruct((3,) + s.shape[1:], s.dtype) for s in sums]


def _swap_other_half(grads, name):
    plan = _swap_plan(len(grads))
    return _exchange(name, grads, _swap_shapes(grads), lambda ins, outs, s, r: plan(list(ins) + list(outs), s, r), len(grads))


def _join_halves(halves, name, after=None):
    def plan(ins, outs, send_sems, recv_sems):
        x, y, c = _mesh_pos()
        sends = [_remote(outs[t].at[c], outs[t].at[c], send_sems, recv_sems, t, (x, y, 1 - c)) for t in range(len(ins))]
        recvs = [_remote(outs[t].at[1 - c], outs[t].at[1 - c], send_sems, recv_sems, t, (x, y, 1 - c))
                 for t in range(len(ins))]
        return sends, recvs

    shapes = [jax.ShapeDtypeStruct(h.shape, h.dtype) for h in halves]
    return _exchange(name, halves, shapes, plan, len(halves), in_place=True, after=after)


def _add_other_half(g4, got, where, name, wire_dtype=BF16):
    _, pieces, sr, sc = g4.shape
    tr = _tile(sr, 256, 2 * SUBLANES)

    def body(w_ref, a_ref, b_ref, o_ref):
        o_ref[...] = (a_ref[...] + b_ref[...]).astype(o_ref.dtype)

    blk = pl.BlockSpec((None, tr, sc), lambda p, i, w_ref: (p, i, 0))
    return pl.pallas_call(
        body, name=name, out_shape=jax.ShapeDtypeStruct((pieces, sr, sc), wire_dtype),
        grid_spec=pltpu.PrefetchScalarGridSpec(
            num_scalar_prefetch=1, grid=(pieces, sr // tr),
            in_specs=[pl.BlockSpec((None, None, tr, sc), lambda p, i, w_ref: (w_ref[0], p, i, 0)), blk], out_specs=blk),
        compiler_params=_params(("parallel", "parallel")),
    )(where, g4, got)


def _add_pieces(g4, got_half, got_pieces, where, name):
    _, _, sr, sc = g4.shape
    tr = _tile(sr, 256, 2 * SUBLANES)

    def body(w_ref, a_ref, b_ref, r_ref, o_ref):
        acc = a_ref[...] + b_ref[...]
        for j in range(3):
            acc = acc + r_ref[j].astype(F32)
        o_ref[...] = acc

    return pl.pallas_call(
        body, name=name, out_shape=jax.ShapeDtypeStruct((N_CORES, sr, sc), F32),
        grid_spec=pltpu.PrefetchScalarGridSpec(
            num_scalar_prefetch=1, grid=(sr // tr,),
            in_specs=[pl.BlockSpec((None, None, tr, sc), lambda i, w_ref: (w_ref[0], w_ref[1], i, 0)),
                      pl.BlockSpec((None, tr, sc), lambda i, w_ref: (w_ref[1], i, 0)),
                      pl.BlockSpec((3, tr, sc), lambda i, w_ref: (0, i, 0))],
            out_specs=pl.BlockSpec((None, tr, sc), lambda i, w_ref: (w_ref[0], i, 0))),
        compiler_params=_params(("parallel",)),
    )(where, g4, got_half, got_pieces)


def _adamw_update(w, g, m, v):
    nm = ADAM_B1 * m + (1.0 - ADAM_B1) * g
    nv = ADAM_B2 * v + (1.0 - ADAM_B2) * (g * g)
    m_hat = nm / (1.0 - ADAM_B1 ** ADAM_STEP)
    v_hat = nv / (1.0 - ADAM_B2 ** ADAM_STEP)
    return -ADAM_LR * (m_hat / (jnp.sqrt(v_hat) + ADAM_EPS) + ADAM_WD * w), nm, nv


def _adamw(w, g, m, v, name):
    rows, cols = w.shape
    tr = _tile(rows, max(SUBLANES, (1 << 19) // max(cols, 1) // SUBLANES * SUBLANES), SUBLANES)

    def body(w_ref, g_ref, m_ref, v_ref, d_ref, nm_ref, nv_ref):
        d_ref[...], nm_ref[...], nv_ref[...] = _adamw_update(w_ref[...], g_ref[...], m_ref[...], v_ref[...])

    blk = pl.BlockSpec((tr, cols), lambda i: (i, 0))
    return pl.pallas_call(
        body, name=name, grid=(rows // tr,), in_specs=[blk] * 4, out_specs=[blk] * 3,
        out_shape=[jax.ShapeDtypeStruct((rows, cols), F32)] * 3, compiler_params=_params(("parallel",)),
    )(w, g, m, v)


def _adamw_many(ws, gs, ms, vs, name):
    n = len(ws)

    def body(*refs):
        outs = refs[4 * n:]
        for k in range(n):
            w_ref, g_ref, m_ref, v_ref = (refs[j * n + k] for j in range(4))
            outs[k][...], outs[n + k][...], outs[2 * n + k][...] = _adamw_update(w_ref[...], g_ref[...], m_ref[...], v_ref[...])

    out = pl.pallas_call(
        body, name=name, out_shape=[jax.ShapeDtypeStruct(w.shape, F32) for w in ws] * 3,
        compiler_params=pltpu.CompilerParams(vmem_limit_bytes=VMEM_LIMIT_BYTES),
    )(*ws, *gs, *ms, *vs)
    return out[:n], out[n:2 * n], out[2 * n:]


WEIGHTS = ['attn_norm_w', 'w_in', 'ssm_lambda_re', 'ssm_lambda_im', 'ssm_log_dt', 'ssm_b_re', 'ssm_b_im', 'ssm_c_re',
           'ssm_c_im', 'ssm_d', 'ssm_w_glu', 'ssm_b_glu', 'mla_q_norm_w', 'mla_w_uq', 'mla_kv_norm_w', 'mla_w_ukv',
           'ssm_out_norm_w', 'mla_out_norm_w', 'w_out', 'ffn_norm_w', 'ffn_w_up', 'ffn_conv_w', 'ffn_conv_b',
           'ffn_w_down', 'final_norm_w']
SHARDED = {'w_in': True, 'ssm_w_glu': True, 'mla_w_uq': False, 'mla_w_ukv': False, 'w_out': True, 'ffn_w_up': False,
           'ffn_w_down': True}
SMALL = [n for n in WEIGHTS if n not in SHARDED and n != 'ffn_conv_w']
ROPE_PAD = HEAD_SLOT - QK_NOPE_DIM - QK_ROPE_DIM
SMALL_COLS = 8 * LANES


def _pad_heads(w_uq, heads):
    qr = w_uq.shape[0]
    w3 = w_uq.reshape(qr, heads, QK_NOPE_DIM + QK_ROPE_DIM)
    return jnp.concatenate([w3, jnp.zeros((qr, heads, ROPE_PAD), w_uq.dtype)], axis=2).reshape(qr, heads * HEAD_SLOT)


def _unpad_heads(g_uq, heads):
    qr = g_uq.shape[0]
    return g_uq.reshape(qr, heads, HEAD_SLOT)[:, :, :QK_NOPE_DIM + QK_ROPE_DIM].reshape(qr, -1)


FFN = ['ffn_w_up', 'ffn_w_down']
FFN_GATHER = FFN + ['ffn_conv_w']
FFN_GATHER_META = [(SHARDED[n], False) for n in FFN] + [(False, True)]


class _Overlapped:
    def __init__(self, placed, where, after):
        self.where = where
        self.over_ici, self.over_d2d = _gather_plans(FFN_GATHER_META)
        self.gather = _start_copies("gather_ffn_start", placed, self.over_ici, 3 * len(placed), after)
        self.gather_started = self.gather[3]

    def ffn_weights(self, after):
        arrived = _wait_copies("gather_ffn_wait", self.gather, self.over_ici, after)
        n = len(FFN)
        shapes = [jax.ShapeDtypeStruct(a.shape, a.dtype) for a in arrived[:n]]
        passed = _exchange("gather_ffn_pass", arrived[:n], shapes, lambda ins, outs, s, r: self.over_d2d(outs, s, r),
                           3 * n, in_place=True)
        return dict(zip(FFN_GATHER, list(passed) + arrived[n:]))

    def ffn_grads(self, g_down, g_up, after):
        grads = [g_up, g_down]
        lands = [lax.empty(s.shape, s.dtype) for s in _swap_shapes(grads)]
        self.swap = _start_copies("grad_ffn_swap_start", grads + lands, _swap_plan(len(grads)), len(grads), after)
        return self.swap[3]

    def ffn_backward_done(self, after):
        n = len(FFN)
        out = _wait_copies("grad_ffn_swap_wait", self.swap, _swap_plan(n), after)
        self.grads, self.got_half = out[:n], out[n:]
        sums = [_add_other_half(self.grads[t], self.got_half[t], self.where, "grad_add_half_" + name)
                for t, name in enumerate(FFN)]
        lands = [lax.empty(s.shape, s.dtype) for s in _scatter_shapes(sums)]
        self.scatter = _start_copies("grad_ffn_scatter_start", sums + lands, _scatter_plan(n), 3 * n, after)
        return self.scatter[3]

    def ffn_reduced(self, after):
        n = len(FFN)
        got_pieces = _wait_copies("grad_ffn_scatter_wait", self.scatter, _scatter_plan(n), after)[n:]
        return [_add_pieces(self.grads[t], self.got_half[t], got_pieces[t], self.where, "grad_add_pieces_" + name)
                for t, name in enumerate(FFN)]


def _step(args):
    x, positions, target = args["x"][0], args["positions"], args["loss_target"][0]
    rows = x.shape[0]
    p = {n: args[n] for n in WEIGHTS}
    xi, yi, ci = _mesh_pos()
    piece = 2 * xi + yi

    w_in = p["w_in"][0]
    in_width = w_in.shape[1]
    in_pad = (-in_width) % (2 * LANES)
    heads_here = p["mla_w_uq"].shape[2] // (QK_NOPE_DIM + QK_ROPE_DIM)
    shards = {
        "w_in": jnp.pad(w_in, ((0, 0), (0, in_pad))),
        "ssm_w_glu": p["ssm_w_glu"][0],
        "mla_w_uq": _pad_heads(p["mla_w_uq"][0], heads_here),
        "mla_w_ukv": p["mla_w_ukv"][0],
        "w_out": p["w_out"][0],
        "ffn_w_up": p["ffn_w_up"][0],
        "ffn_w_down": p["ffn_w_down"][0],
    }
    conv_w = jnp.pad(p["ffn_conv_w"][0], ((0, SUBLANES - p["ffn_conv_w"].shape[1]), (0, 0)))
    order = list(SHARDED)
    piece_idx = piece.reshape(1).astype(jnp.int32)
    placed = {n: _place_shard(shards[n], piece_idx, SHARDED[n], "place_" + n) for n in order}
    placed["ffn_conv_w"] = _place_shard(conv_w, piece_idx, False, "place_ffn_conv_w", out_dtype=F32)
    mixer = [n for n in order if n not in FFN]
    w = dict(zip(mixer, _gather_weights([(placed[n], SHARDED[n], False) for n in mixer], "gather_mixer_weights")))
    where = jnp.stack([ci, piece]).astype(jnp.int32)
    hooks = _Overlapped([placed[n] for n in FFN_GATHER], where, after=w["w_in"])
    groups = p["ssm_lambda_re"].shape[1]
    w.update({
        "attn_norm_w": p["attn_norm_w"] + hooks.gather_started[:1, :1],
        "ssm_lambda_re": p["ssm_lambda_re"][0], "ssm_lambda_im": p["ssm_lambda_im"][0],
        "ssm_log_dt": p["ssm_log_dt"].reshape(groups, 1), "ssm_b_re": p["ssm_b_re"].reshape(groups, -1),
        "ssm_b_im": p["ssm_b_im"].reshape(groups, -1), "ssm_c_re": p["ssm_c_re"][0], "ssm_c_im": p["ssm_c_im"][0],
        "ssm_d": p["ssm_d"], "ssm_b_glu": p["ssm_b_glu"], "mla_q_norm_w": p["mla_q_norm_w"],
        "mla_kv_norm_w": p["mla_kv_norm_w"], "ssm_out_norm_w": p["ssm_out_norm_w"], "mla_out_norm_w": p["mla_out_norm_w"],
        "ffn_norm_w": p["ffn_norm_w"], "ffn_conv_b": p["ffn_conv_b"], "final_norm_w": p["final_norm_w"].reshape(1, -1),
    })

    loss_tile, dx, g = _local_step(x, positions.reshape(rows, 1).astype(F32), target, w, hooks)
    loss = lax.psum(loss_tile[0, 0], ("x", "y", "c"))

    flat = [g[n].reshape(-1) for n in SMALL] + [g["ffn_conv_w"].reshape(-1)]
    sizes = [f.shape[0] for f in flat]
    per_block = -(-sum(sizes) // (N_CORES * N_CHIPS * SMALL_COLS))
    small_rows = -(-per_block // (2 * SUBLANES)) * (2 * SUBLANES)
    padded = N_CORES * N_CHIPS * small_rows * SMALL_COLS

    def pack(parts):
        parts = list(parts)
        have = sum(q.shape[0] for q in parts)
        return jnp.concatenate(parts + [jnp.zeros((padded - have,), F32)])

    reduced = mixer + ["small"]
    g_rs = [g[n] for n in mixer] + [pack(flat).reshape(N_CORES, N_CHIPS, small_rows, SMALL_COLS)]
    wire = [BF16] * len(mixer) + [F32]
    got_half = _swap_other_half(g_rs, "grad_swap_halves")
    sums = [_add_other_half(g_rs[t], got_half[t], where, "grad_add_half_" + n, wire[t]) for t, n in enumerate(reduced)]
    lands = [lax.empty(s.shape, s.dtype) for s in _scatter_shapes(sums)]
    scatter_plan = _scatter_plan(len(reduced))
    scatter = _start_copies("grad_mixer_scatter_start", sums + lands, scatter_plan, 3 * len(reduced), dx)

    grads, delta, new_m, new_v = {}, {}, {}, {}

    def finish(n, joined):
        grad = jnp.concatenate([joined[0], joined[1]], axis=1) if SHARDED[n] else joined.reshape(-1, joined.shape[2])
        if n == "w_in":
            grad = grad[:, :in_width]
        if n == "mla_w_uq":
            grad = _unpad_heads(grad, heads_here)
        adam(n, grad)

    def adam(n, grad):
        shape = p[n].shape
        d2, m2, v2 = _adamw(p[n].reshape(shape[1:]), grad, args["m_" + n].reshape(shape[1:]),
                            args["v_" + n].reshape(shape[1:]), "adamw_" + n)
        grads[n] = grad.reshape(shape)
        delta[n], new_m[n], new_v[n] = d2.reshape(shape), m2.reshape(shape), v2.reshape(shape)

    for n, joined in zip(FFN, _join_halves(hooks.ffn_reduced(dx), "grad_ffn_join_halves", after=scatter[3])):
        finish(n, joined)
    got_pieces = _wait_copies("grad_mixer_scatter_wait", scatter, scatter_plan, delta[FFN[0]])[len(reduced):]
    halves = [_add_pieces(g_rs[t], got_half[t], got_pieces[t], where, "grad_add_pieces_" + n) for t, n in enumerate(reduced)]
    joined = _join_halves(halves, "grad_join_halves")
    for n, j in zip(mixer, joined):
        finish(n, j)
    eighths = _place_shard(joined[-1].reshape(N_CORES * small_rows, SMALL_COLS), piece_idx, True, "place_small_grads",
                           out_dtype=F32)
    small_sum = _gather_weights([(eighths, True, False)], "gather_small_grads")[0]
    flat_sum = small_sum.reshape(N_CHIPS, N_CORES, small_rows * SMALL_COLS).transpose(1, 0, 2).reshape(-1)
    offs = [0]
    for s in sizes:
        offs.append(offs[-1] + s)
    for k, n in enumerate(SMALL):
        grads[n] = flat_sum[offs[k]:offs[k + 1]].reshape(p[n].shape)
    taps, cols_here = p["ffn_conv_w"].shape[1], p["ffn_conv_w"].shape[2]
    conv_full = flat_sum[offs[len(SMALL)]:offs[len(SMALL) + 1]].reshape(taps, N_CHIPS * cols_here)
    adam("ffn_conv_w", lax.dynamic_slice_in_dim(conv_full, piece * cols_here, cols_here, axis=1))

    def rank2(a):
        return a.reshape(1, -1) if a.ndim == 1 else a

    d_s, m_s, v_s = _adamw_many([rank2(p[n]) for n in SMALL], [rank2(grads[n]) for n in SMALL],
                                [rank2(args["m_" + n]) for n in SMALL], [rank2(args["v_" + n]) for n in SMALL], "adamw_small")
    for k, n in enumerate(SMALL):
        delta[n], new_m[n], new_v[n] = (a.reshape(p[n].shape) for a in (d_s[k], m_s[k], v_s[k]))

    return (loss, dx[None], *[grads[n] for n in WEIGHTS], *[delta[n] for n in WEIGHTS],
            *[new_m[n] for n in WEIGHTS], *[new_v[n] for n in WEIGHTS])


def kernel(x, positions, attn_norm_w, w_in, ssm_lambda_re, ssm_lambda_im, ssm_log_dt, ssm_b_re, ssm_b_im, ssm_c_re, ssm_c_im, ssm_d, ssm_w_glu, ssm_b_glu, mla_q_norm_w, mla_w_uq, mla_kv_norm_w, mla_w_ukv, ssm_out_norm_w, mla_out_norm_w, w_out, ffn_norm_w, ffn_w_up, ffn_conv_w, ffn_conv_b, ffn_w_down, final_norm_w, loss_target, m_attn_norm_w, m_w_in, m_ssm_lambda_re, m_ssm_lambda_im, m_ssm_log_dt, m_ssm_b_re, m_ssm_b_im, m_ssm_c_re, m_ssm_c_im, m_ssm_d, m_ssm_w_glu, m_ssm_b_glu, m_mla_q_norm_w, m_mla_w_uq, m_mla_kv_norm_w, m_mla_w_ukv, m_ssm_out_norm_w, m_mla_out_norm_w, m_w_out, m_ffn_norm_w, m_ffn_w_up, m_ffn_conv_w, m_ffn_conv_b, m_ffn_w_down, m_final_norm_w, v_attn_norm_w, v_w_in, v_ssm_lambda_re, v_ssm_lambda_im, v_ssm_log_dt, v_ssm_b_re, v_ssm_b_im, v_ssm_c_re, v_ssm_c_im, v_ssm_d, v_ssm_w_glu, v_ssm_b_glu, v_mla_q_norm_w, v_mla_w_uq, v_mla_kv_norm_w, v_mla_w_ukv, v_ssm_out_norm_w, v_mla_out_norm_w, v_w_out, v_ffn_norm_w, v_ffn_w_up, v_ffn_conv_w, v_ffn_conv_b, v_ffn_w_down, v_final_norm_w):
    return _step(dict(locals()))
```

```python
import functools
import math

import jax
import jax.numpy as jnp
from jax import lax
from jax.experimental import pallas as pl
from jax.experimental.pallas import tpu as pltpu

F32 = jnp.float32
BF16 = jnp.bfloat16

SSM_GROUP = 16
SSM_STATE = 64
QK_NOPE_DIM = 128
QK_ROPE_DIM = 64
V_HEAD_DIM = 128
ROPE_THETA = 10000.0
RMS_EPS = 1e-6
ADAM_LR, ADAM_B1, ADAM_B2, ADAM_EPS, ADAM_WD, ADAM_STEP = 0.001, 0.9, 0.999, 1e-08, 0.01, 10

LANES = 128
SUBLANES = 8
VMEM_LIMIT_BYTES = 56 * 1024 * 1024

GROUPS_PER_BATCH = LANES // SSM_GROUP
STATE_PER_BATCH = GROUPS_PER_BATCH * SSM_STATE
HEAD_SLOT = 2 * LANES
NEG_INF = -1e30
ATTN_BLOCK = 512
FFN_ROWS = 1024

N_CHIPS = 4
N_CORES = 2


def _tile(n, pref, align=LANES):
    if n <= pref:
        return n
    t = (pref // align) * align
    while t >= align:
        if n % t == 0:
            return t
        t -= align
    return n


def _params(sem):
    return pltpu.CompilerParams(dimension_semantics=sem, vmem_limit_bytes=VMEM_LIMIT_BYTES)


def _dot(a, b, dims):
    return lax.dot_general(a, b, (dims, ((), ())), preferred_element_type=F32)


def _dot_nn(a, b):
    return _dot(a, b, ((1,), (0,)))


def _dot_nt(a, b):
    return _dot(a, b, ((1,), (1,)))


def _dot_tn(a, b):
    return _dot(a, b, ((0,), (0,)))


def _matmul(a, b, *, mode, name, tm=512, tn=1024, tk=2048, bias=None, add=None, out_dtype=F32,
            out_blocks=None, a_split=False, b_split=False, after=None):
    if a_split:
        assert mode == "nt"
        a_shape = (a.shape[1], 2 * a.shape[2])
    else:
        a_shape = a.shape
    if b_split:
        assert mode == "tn"
        b_shape = (b.shape[1], 2 * b.shape[2])
    else:
        b_shape = b.shape
    if mode == "nn":
        (m, k), (k2, n) = a_shape, b_shape
    elif mode == "nt":
        (m, k), (n, k2) = a_shape, b_shape
    else:
        (k, m), (k2, n) = a_shape, b_shape
    assert k == k2, (a.shape, b.shape, mode)
    tm, tn, tk = _tile(m, tm, SUBLANES), _tile(n, tn), _tile(k, tk)
    nk = k // tk
    a_spec = {"nn": pl.BlockSpec((tm, tk), lambda i, j, kk: (i, kk)),
              "nt": pl.BlockSpec((tm, tk), lambda i, j, kk: (i, kk)),
              "tn": pl.BlockSpec((tk, tm), lambda i, j, kk: (kk, i))}[mode]
    b_spec = {"nn": pl.BlockSpec((tk, tn), lambda i, j, kk: (kk, j)),
              "nt": pl.BlockSpec((tn, tk), lambda i, j, kk: (j, kk)),
              "tn": pl.BlockSpec((tk, tn), lambda i, j, kk: (kk, j))}[mode]
    if a_split:
        kb = a.shape[2] // tk
        assert a.shape[2] % tk == 0
        a_spec = pl.BlockSpec((None, tm, tk), lambda i, j, kk: (kk // kb, i, kk % kb))
    if b_split:
        nb = b.shape[2] // tn
        assert b.shape[2] % tn == 0
        b_spec = pl.BlockSpec((None, tk, tn), lambda i, j, kk: (j // nb, kk, j % nb))
    dot = {"nn": _dot_nn, "nt": _dot_nt, "tn": _dot_tn}[mode]
    in_specs, operands = [a_spec, b_spec], [a, b]
    if bias is not None:
        in_specs.append(pl.BlockSpec((1, tn), lambda i, j, kk: (0, j)))
        operands.append(bias)
    if add is not None:
        in_specs.append(pl.BlockSpec((tm, tn), lambda i, j, kk: (i, j)))
        operands.append(add)
    if after is not None:
        in_specs.append(pl.BlockSpec(memory_space=pl.ANY))
        operands.append(after)

    def body(*refs):
        a_ref, b_ref = refs[0], refs[1]
        rest = list(refs[2:])
        bias_ref = rest.pop(0) if bias is not None else None
        add_ref = rest.pop(0) if add is not None else None
        if after is not None:
            rest.pop(0)
        o_ref, acc_ref = rest

        def finish(acc):
            if bias_ref is not None:
                acc = acc + bias_ref[...]
            if add_ref is not None:
                acc = acc + add_ref[...]
            o_ref[...] = acc.astype(o_ref.dtype)

        part = dot(a_ref[...].astype(BF16), b_ref[...].astype(BF16))
        if nk == 1:
            finish(part)
        else:
            kk = pl.program_id(2)

            @pl.when(kk == 0)
            def _():
                acc_ref[...] = part

            @pl.when(jnp.logical_and(kk > 0, kk < nk - 1))
            def _():
                acc_ref[...] += part

            @pl.when(kk == nk - 1)
            def _():
                finish(acc_ref[...] + part)

    if out_blocks is None:
        out_shape = jax.ShapeDtypeStruct((m, n), out_dtype)
        out_spec = pl.BlockSpec((tm, tn), lambda i, j, kk: (i, j))
    else:
        shape, block, index_map = out_blocks(tm, tn)
        out_shape = jax.ShapeDtypeStruct(shape, out_dtype)
        out_spec = pl.BlockSpec(block, index_map)
    acc_shape = (tm, tn) if nk > 1 else (SUBLANES, LANES)
    return pl.pallas_call(
        body, name=name, grid=(m // tm, n // tn, nk), in_specs=in_specs, out_specs=out_spec, out_shape=out_shape,
        scratch_shapes=[pltpu.VMEM(acc_shape, F32)],
        compiler_params=_params(("parallel", "parallel", "arbitrary")),
    )(*operands)


def _wgrad_blocks(rows, cols, row_sharded):
    if row_sharded:
        sr, sc = rows // N_CHIPS, cols // N_CORES
    else:
        sr, sc = rows // N_CORES, cols // N_CHIPS

    def make(tm, tn):
        assert sr % tm == 0 and sc % tn == 0, (rows, cols, tm, tn)
        rb, cb = sr // tm, sc // tn
        if row_sharded:
            def index_map(i, j, kk):
                return (j // cb, i // rb, i % rb, j % cb)
        else:
            def index_map(i, j, kk):
                return (i // rb, j // cb, i % rb, j % cb)
        return (N_CORES, N_CHIPS, sr, sc), (None, None, tm, tn), index_map

    return make, (sr, sc)


def _rms_rows(x):
    return lax.rsqrt(jnp.mean(x * x, axis=-1, keepdims=True) + RMS_EPS)


def _rmsnorm_fwd(x, w, *, name, width=None, col=0, out_dtype=BF16, tr=256):
    rows = x.shape[0]
    width = x.shape[1] if width is None else width
    tr = _tile(rows, tr, SUBLANES)

    def body(x_ref, w_ref, o_ref):
        xv = x_ref[...]
        o_ref[...] = (xv * _rms_rows(xv) * w_ref[...]).astype(o_ref.dtype)

    return pl.pallas_call(
        body, name=name, grid=(rows // tr,),
        in_specs=[pl.BlockSpec((tr, width), lambda i: (i, col)), pl.BlockSpec((1, width), lambda i: (0, 0))],
        out_specs=pl.BlockSpec((tr, width), lambda i: (i, 0)),
        out_shape=jax.ShapeDtypeStruct((rows, width), out_dtype),
        compiler_params=_params(("parallel",)),
    )(x, w)


def _rmsnorm_bwd_rows(xv, w, dy):
    r = _rms_rows(xv)
    n = xv * r
    dn = dy * w
    dx = r * (dn - n * jnp.mean(dn * n, axis=-1, keepdims=True))
    return dx, dy * n


def _rmsnorm_bwd(x, w, dy, *, name, width=None, col=0, dy_col=0, add=None, tr=256, dx_dtypes=(F32,)):
    rows = x.shape[0]
    n_dx = len(dx_dtypes)
    width = x.shape[1] if width is None else width
    tr = _tile(rows, tr, SUBLANES)
    in_specs = [pl.BlockSpec((tr, width), lambda i: (i, col)), pl.BlockSpec((1, width), lambda i: (0, 0)),
                pl.BlockSpec((tr, width), lambda i: (i, dy_col))]
    operands = [x, w, dy]
    if add is not None:
        in_specs.append(pl.BlockSpec((tr, width), lambda i: (i, 0)))
        operands.append(add)

    def body(*refs):
        x_ref, w_ref, dy_ref = refs[:3]
        add_ref = refs[3] if add is not None else None
        dx_refs, dw_ref = refs[-1 - n_dx:-1], refs[-1]
        dx, dwp = _rmsnorm_bwd_rows(x_ref[...], w_ref[...], dy_ref[...])
        if add_ref is not None:
            dx = dx + add_ref[...]
        for dx_ref in dx_refs:
            dx_ref[...] = dx.astype(dx_ref.dtype)
        part = jnp.sum(dwp, axis=0, keepdims=True)

        @pl.when(pl.program_id(0) == 0)
        def _():
            dw_ref[...] = part

        @pl.when(pl.program_id(0) > 0)
        def _():
            dw_ref[...] += part

    return pl.pallas_call(
        body, name=name, grid=(rows // tr,), in_specs=in_specs,
        out_specs=[pl.BlockSpec((tr, width), lambda i: (i, 0))] * n_dx + [pl.BlockSpec((1, width), lambda i: (0, 0))],
        out_shape=[jax.ShapeDtypeStruct((rows, width), dt) for dt in dx_dtypes] + [jax.ShapeDtypeStruct((1, width), F32)],
        compiler_params=_params(("arbitrary",)),
    )(*operands)


def _final_norm_loss(h, w, target, *, tr=256):
    rows, d = h.shape
    tr = _tile(rows, tr, SUBLANES)

    def body(h_ref, w_ref, t_ref, loss_ref, dh_ref, dhb_ref, dw_ref):
        hv, wv = h_ref[...], w_ref[...]
        r = _rms_rows(hv)
        n = hv * r
        err = n * wv - t_ref[...]
        d_out = err * (1.0 / d)
        dn = d_out * wv
        dh = r * (dn - n * jnp.mean(dn * n, axis=-1, keepdims=True))
        dh_ref[...] = dh
        dhb_ref[...] = dh.astype(BF16)
        dw_part = jnp.sum(d_out * n, axis=0, keepdims=True)
        loss_part = jnp.full((SUBLANES, LANES), 0.5 / d, F32) * jnp.sum(err * err)

        @pl.when(pl.program_id(0) == 0)
        def _():
            dw_ref[...] = dw_part
            loss_ref[...] = loss_part

        @pl.when(pl.program_id(0) > 0)
        def _():
            dw_ref[...] += dw_part
            loss_ref[...] += loss_part

    return pl.pallas_call(
        body, name="final_norm_loss", grid=(rows // tr,),
        in_specs=[pl.BlockSpec((tr, d), lambda i: (i, 0)), pl.BlockSpec((1, d), lambda i: (0, 0)),
                  pl.BlockSpec((tr, d), lambda i: (i, 0))],
        out_specs=[pl.BlockSpec((SUBLANES, LANES), lambda i: (0, 0)), pl.BlockSpec((tr, d), lambda i: (i, 0)),
                   pl.BlockSpec((tr, d), lambda i: (i, 0)), pl.BlockSpec((1, d), lambda i: (0, 0))],
        out_shape=[jax.ShapeDtypeStruct((SUBLANES, LANES), F32), jax.ShapeDtypeStruct((rows, d), F32),
                   jax.ShapeDtypeStruct((rows, d), BF16), jax.ShapeDtypeStruct((1, d), F32)],
        compiler_params=_params(("arbitrary",)),
    )(h, w, target)


def _cmul(ar, ai, br, bi):
    return ar * br - ai * bi, ar * bi + ai * br


def _expand_matrix(groups, reps):
    row = lax.broadcasted_iota(jnp.int32, (groups, groups * reps), 0)
    colg = lax.broadcasted_iota(jnp.int32, (groups, groups * reps), 1) // reps
    return (row == colg).astype(F32)


def _dot_exact(a, b, dims):
    return lax.dot_general(a, b, (dims, ((), ())), preferred_element_type=F32, precision=lax.Precision.HIGHEST)


def _s5_discretize(lr, li, dt):
    mag = jnp.exp(lr * dt)
    th = li * dt
    ar, ai = mag * jnp.cos(th), mag * jnp.sin(th)
    nr, ni = ar - 1.0, ai
    den = lr * lr + li * li
    zr = (nr * lr + ni * li) / den
    zi = (ni * lr - nr * li) / den
    return mag, ar, ai, nr, ni, den, zr, zi


def _s5_params(lam_re, lam_im, log_dt, b_re, b_im):
    g, p = lam_re.shape
    ph = b_re.shape[1]

    def body(lr_ref, li_ref, ldt_ref, br_ref, bi_ref, ar_ref, ai_ref, bbr_ref, bbi_ref):
        dt = jnp.exp(ldt_ref[...])
        _, ar, ai, _, _, _, zr, zi = _s5_discretize(lr_ref[...], li_ref[...], dt)
        ar_ref[...] = ar
        ai_ref[...] = ai
        e = _expand_matrix(p, ph // p)
        zr_x = _dot_exact(zr, e, ((1,), (0,)))
        zi_x = _dot_exact(zi, e, ((1,), (0,)))
        bre, bim = br_ref[...], bi_ref[...]
        bbr_ref[...] = zr_x * bre - zi_x * bim
        bbi_ref[...] = zr_x * bim + zi_x * bre

    return pl.pallas_call(
        body, name="s5_params",
        out_shape=[jax.ShapeDtypeStruct((g, p), F32)] * 2 + [jax.ShapeDtypeStruct((g, ph), F32)] * 2,
    )(lam_re, lam_im, log_dt, b_re, b_im)


def _s5_params_bwd(lam_re, lam_im, log_dt, b_re, b_im, d_ar, d_ai, d_bbr, d_bbi):
    g, p = lam_re.shape
    ph = b_re.shape[1]

    def body(lr_ref, li_ref, ldt_ref, br_ref, bi_ref, dar_ref, dai_ref, dbr_ref, dbi_ref,
             dlr_ref, dli_ref, dldt_ref, dbre_ref, dbim_ref):
        lr, li = lr_ref[...], li_ref[...]
        dt = jnp.exp(ldt_ref[...])
        mag, ar, ai, nr, ni, den, zr, zi = _s5_discretize(lr, li, dt)
        e = _expand_matrix(p, ph // p)
        zr_x = _dot_exact(zr, e, ((1,), (0,)))
        zi_x = _dot_exact(zi, e, ((1,), (0,)))
        bre, bim, dbr, dbi = br_ref[...], bi_ref[...], dbr_ref[...], dbi_ref[...]
        dbre_ref[...] = zr_x * dbr + zi_x * dbi
        dbim_ref[...] = zr_x * dbi - zi_x * dbr
        dzr = _dot_exact(bre * dbr + bim * dbi, e, ((1,), (1,)))
        dzi = _dot_exact(bre * dbi - bim * dbr, e, ((1,), (1,)))
        inv = 1.0 / den
        d_nr = (dzr * lr - dzi * li) * inv
        d_ni = (dzr * li + dzi * lr) * inv
        d_den = -(dzr * zr + dzi * zi) * inv
        d_lr = (dzr * nr + dzi * ni) * inv + 2.0 * lr * d_den
        d_li = (dzr * ni - dzi * nr) * inv + 2.0 * li * d_den
        t_ar = dar_ref[...] + d_nr
        t_ai = dai_ref[...] + d_ni
        d_lrdt = t_ar * ar + t_ai * ai
        d_th = t_ai * ar - t_ar * ai
        dlr_ref[...] = d_lr + d_lrdt * dt
        dli_ref[...] = d_li + d_th * dt
        dldt_ref[...] = jnp.sum(d_lrdt * lr + d_th * li, axis=1, keepdims=True) * dt

    return pl.pallas_call(
        body, name="s5_params_bwd",
        out_shape=[jax.ShapeDtypeStruct((g, p), F32)] * 2 + [jax.ShapeDtypeStruct((g, 1), F32)]
        + [jax.ShapeDtypeStruct((g, ph), F32)] * 2,
    )(lam_re, lam_im, log_dt, b_re, b_im, d_ar, d_ai, d_bbr, d_bbi)


def _powers(ar, ai, count):
    out = [(ar, ai)]
    for _ in range(count - 1):
        out.append(_cmul(out[-1][0], out[-1][1], ar, ai))
    return out


def _scan_coefs(ar, ai, reverse):
    w = ar.shape[-1]
    pw = _powers(ar, ai, SUBLANES)
    row = lax.broadcasted_iota(jnp.int32, (SUBLANES, w), 0)
    steps = []
    d = 1
    while d < SUBLANES:
        keep = (row < SUBLANES - d) if reverse else (row >= d)
        pr, pi = pw[d - 1]
        steps.append((d, jnp.where(keep, pr, 0.0), jnp.where(keep, pi, 0.0)))
        d *= 2
    cr = jnp.zeros((SUBLANES, w), F32)
    ci = jnp.zeros((SUBLANES, w), F32)
    for t in range(SUBLANES):
        pr, pi = pw[SUBLANES - 1 - t] if reverse else pw[t]
        cr = jnp.where(row == t, pr, cr)
        ci = jnp.where(row == t, pi, ci)
    return steps, cr, ci


def _scan_tile(xr, xi, carry_r, carry_i, coefs, reverse):
    steps, cr, ci = coefs
    for d, mr, mi in steps:
        shift = SUBLANES - d if reverse else d
        sr, si = pltpu.roll(xr, shift, 0), pltpu.roll(xi, shift, 0)
        pr, pi = _cmul(mr, mi, sr, si)
        xr, xi = xr + pr, xi + pi
    pr, pi = _cmul(cr, ci, carry_r, carry_i)
    return xr + pr, xi + pi


def _gelu(x):
    c = math.sqrt(2.0 / math.pi)
    return 0.5 * x * (1.0 + jnp.tanh(c * (x + 0.044715 * x * x * x)))


def _gelu_grad(x):
    c = math.sqrt(2.0 / math.pi)
    t = jnp.tanh(c * (x + 0.044715 * x * x * x))
    return 0.5 * (1.0 + t) + 0.5 * x * (1.0 - t * t) * c * (1.0 + 3.0 * 0.044715 * x * x)


def _s5_fwd(proj, wb, wc, d_skip, abar):
    rows = proj.shape[0]
    nb = wb.shape[0]
    s2 = 2 * STATE_PER_BATCH
    st = STATE_PER_BATCH
    chunk = _tile(rows, 512, SUBLANES)

    def body(u_ref, wb_ref, wc_ref, d_ref, a_ref, s_ref, y_ref, yg_ref):
        for c0 in range(0, rows, chunk):
            s_ref[pl.ds(c0, chunk), :] = _dot_nn(u_ref[pl.ds(c0, chunk), :].astype(BF16), wb_ref[...])
        av = a_ref[...]
        coefs = _scan_coefs(av[:, :st], av[:, st:], reverse=False)

        def tile(b, carry):
            r0 = pl.multiple_of(b * SUBLANES, SUBLANES)
            xr, xi = _scan_tile(s_ref[pl.ds(r0, SUBLANES), :st], s_ref[pl.ds(r0, SUBLANES), st:], carry[0], carry[1],
                                coefs, False)
            s_ref[pl.ds(r0, SUBLANES), :st] = xr
            s_ref[pl.ds(r0, SUBLANES), st:] = xi
            return xr[SUBLANES - 1:, :], xi[SUBLANES - 1:, :]

        zero = jnp.zeros((1, st), F32)
        lax.fori_loop(0, rows // SUBLANES, tile, (zero, zero))
        for c0 in range(0, rows, chunk):
            y = _dot_nn(s_ref[pl.ds(c0, chunk), :].astype(BF16), wc_ref[...]) + d_ref[...] * u_ref[pl.ds(c0, chunk), :]
            y_ref[pl.ds(c0, chunk), :] = y
            yg_ref[pl.ds(c0, chunk), :] = _gelu(y).astype(BF16)

    return pl.pallas_call(
        body, name="s5_fwd", grid=(nb,),
        in_specs=[pl.BlockSpec((rows, LANES), lambda j: (0, j)), pl.BlockSpec((None, LANES, s2), lambda j: (j, 0, 0)),
                  pl.BlockSpec((None, s2, LANES), lambda j: (j, 0, 0)), pl.BlockSpec((1, LANES), lambda j: (0, j)),
                  pl.BlockSpec((None, 1, s2), lambda j: (j, 0, 0))],
        out_specs=[pl.BlockSpec((rows, s2), lambda j: (0, j)), pl.BlockSpec((rows, LANES), lambda j: (0, j)),
                   pl.BlockSpec((rows, LANES), lambda j: (0, j))],
        out_shape=[jax.ShapeDtypeStruct((rows, nb * s2), F32), jax.ShapeDtypeStruct((rows, nb * LANES), F32),
                   jax.ShapeDtypeStruct((rows, nb * LANES), BF16)],
        compiler_params=_params(("parallel",)),
    )(proj, wb, wc, d_skip, abar)


def _s5_bwd(proj, states, y_pre, dyg_a, dyg_b, wb, wc, d_skip, abar):
    rows = proj.shape[0]
    nb = wb.shape[0]
    s2 = 2 * STATE_PER_BATCH
    st = STATE_PER_BATCH
    chunk = _tile(rows, 512, SUBLANES)
    n_tiles = rows // SUBLANES

    def body(u_ref, s_ref, y_ref, ga_ref, gb_ref, wb_ref, wc_ref, d_ref, a_ref,
             du_ref, dwb_ref, dwc_ref, da_ref, dd_ref, ds_ref, dy_ref):
        dy_ref[...] = (ga_ref[...] + gb_ref[...]) * _gelu_grad(y_ref[...])
        dd_ref[...] = jnp.sum(dy_ref[...] * u_ref[...], axis=0, keepdims=True)
        for c0 in range(0, rows, chunk):
            ds_ref[pl.ds(c0, chunk), :] = _dot_nt(dy_ref[pl.ds(c0, chunk), :].astype(BF16), wc_ref[...])
        dwc_ref[...] = _dot_tn(s_ref[...].astype(BF16), dy_ref[...].astype(BF16))
        av = a_ref[...]
        coefs = _scan_coefs(av[:, :st], -av[:, st:], reverse=True)
        row = lax.broadcasted_iota(jnp.int32, (SUBLANES, st), 0)

        def tile(k, carry):
            cr, ci, acc_r, acc_i = carry
            b = n_tiles - 1 - k
            r0 = pl.multiple_of(b * SUBLANES, SUBLANES)
            rp = pl.multiple_of(jnp.maximum(b - 1, 0) * SUBLANES, SUBLANES)
            xr, xi = _scan_tile(ds_ref[pl.ds(r0, SUBLANES), :st], ds_ref[pl.ds(r0, SUBLANES), st:], cr, ci, coefs, True)
            ds_ref[pl.ds(r0, SUBLANES), :st] = xr
            ds_ref[pl.ds(r0, SUBLANES), st:] = xi
            first = jnp.where(b > 0, 1.0, 0.0)
            pr = jnp.where(row == 0, pltpu.roll(s_ref[pl.ds(rp, SUBLANES), :st], 1, 0) * first,
                           pltpu.roll(s_ref[pl.ds(r0, SUBLANES), :st], 1, 0))
            pi = jnp.where(row == 0, pltpu.roll(s_ref[pl.ds(rp, SUBLANES), st:], 1, 0) * first,
                           pltpu.roll(s_ref[pl.ds(r0, SUBLANES), st:], 1, 0))
            acc_r = acc_r + pr * xr + pi * xi
            acc_i = acc_i + pr * xi - pi * xr
            return xr[:1, :], xi[:1, :], acc_r, acc_i

        zero = jnp.zeros((1, st), F32)
        zacc = jnp.zeros((SUBLANES, st), F32)
        _, _, acc_r, acc_i = lax.fori_loop(0, n_tiles, tile, (zero, zero, zacc, zacc))
        da_ref[:, :st] = jnp.sum(acc_r, axis=0, keepdims=True)
        da_ref[:, st:] = jnp.sum(acc_i, axis=0, keepdims=True)
        for c0 in range(0, rows, chunk):
            du_ref[pl.ds(c0, chunk), :] = (_dot_nt(ds_ref[pl.ds(c0, chunk), :].astype(BF16), wb_ref[...])
                                           + d_ref[...] * dy_ref[pl.ds(c0, chunk), :]).astype(du_ref.dtype)
        dwb_ref[...] = _dot_tn(u_ref[...].astype(BF16), ds_ref[...].astype(BF16))

    col = pl.BlockSpec((rows, LANES), lambda j: (0, j))
    return pl.pallas_call(
        body, name="s5_bwd", grid=(nb,),
        in_specs=[col, pl.BlockSpec((rows, s2), lambda j: (0, j)), col, col, col,
                  pl.BlockSpec((None, LANES, s2), lambda j: (j, 0, 0)), pl.BlockSpec((None, s2, LANES), lambda j: (j, 0, 0)),
                  pl.BlockSpec((1, LANES), lambda j: (0, j)), pl.BlockSpec((None, 1, s2), lambda j: (j, 0, 0))],
        out_specs=[col, pl.BlockSpec((None, LANES, s2), lambda j: (j, 0, 0)),
                   pl.BlockSpec((None, s2, LANES), lambda j: (j, 0, 0)), pl.BlockSpec((None, 1, s2), lambda j: (j, 0, 0)),
                   pl.BlockSpec((1, LANES), lambda j: (0, j))],
        out_shape=[jax.ShapeDtypeStruct((rows, nb * LANES), BF16), jax.ShapeDtypeStruct((nb, LANES, s2), F32),
                   jax.ShapeDtypeStruct((nb, s2, LANES), F32), jax.ShapeDtypeStruct((nb, 1, s2), F32),
                   jax.ShapeDtypeStruct((1, nb * LANES), F32)],
        scratch_shapes=[pltpu.VMEM((rows, s2), F32), pltpu.VMEM((rows, LANES), F32)],
        compiler_params=_params(("parallel",)),
    )(proj, states, y_pre, dyg_a, dyg_b, wb, wc, d_skip, abar)


def _glu_norm_fwd(y_pre, z, w, *, tr=256):
    rows, width = y_pre.shape
    tr = _tile(rows, tr, SUBLANES)

    def body(y_ref, z_ref, w_ref, o_ref):
        v = _gelu(y_ref[...]) * jax.nn.sigmoid(z_ref[...])
        o_ref[...] = (v * _rms_rows(v) * w_ref[...]).astype(o_ref.dtype)

    blk = pl.BlockSpec((tr, width), lambda i: (i, 0))
    return pl.pallas_call(
        body, name="glu_norm_fwd", grid=(rows // tr,),
        in_specs=[blk, blk, pl.BlockSpec((1, width), lambda i: (0, 0))], out_specs=blk,
        out_shape=jax.ShapeDtypeStruct((rows, width), BF16), compiler_params=_params(("parallel",)),
    )(y_pre, z, w)


def _glu_norm_bwd(y_pre, z, w, dycat, *, tr=256):
    rows, width = y_pre.shape
    tr = _tile(rows, tr, SUBLANES)

    def body(y_ref, z_ref, w_ref, dy_ref, dz_ref, dg_ref, dw_ref, db_ref):
        yg = _gelu(y_ref[...])
        sg = jax.nn.sigmoid(z_ref[...])
        dv, dwp = _rmsnorm_bwd_rows(yg * sg, w_ref[...], dy_ref[...])
        dz = dv * yg * sg * (1.0 - sg)
        dz_ref[...] = dz.astype(dz_ref.dtype)
        dg_ref[...] = dv * sg
        dw_part = jnp.sum(dwp, axis=0, keepdims=True)
        db_part = jnp.sum(dz, axis=0, keepdims=True)

        @pl.when(pl.program_id(0) == 0)
        def _():
            dw_ref[...] = dw_part
            db_ref[...] = db_part

        @pl.when(pl.program_id(0) > 0)
        def _():
            dw_ref[...] += dw_part
            db_ref[...] += db_part

    blk = pl.BlockSpec((tr, width), lambda i: (i, 0))
    vec = pl.BlockSpec((1, width), lambda i: (0, 0))
    return pl.pallas_call(
        body, name="glu_norm_bwd", grid=(rows // tr,), in_specs=[blk, blk, vec, blk], out_specs=[blk, blk, vec, vec],
        out_shape=[jax.ShapeDtypeStruct((rows, width), BF16), jax.ShapeDtypeStruct((rows, width), F32)]
        + [jax.ShapeDtypeStruct((1, width), F32)] * 2,
        compiler_params=_params(("arbitrary",)),
    )(y_pre, z, w, dycat)


def _rope_tables(pos, freq, sign):
    rows = pos.shape[0]

    def body(p_ref, f_ref, s_ref, cos_ref, sin_ref):
        ang = p_ref[...] * f_ref[...]
        cos_ref[...] = jnp.cos(ang)
        sin_ref[...] = jnp.sin(ang) * s_ref[...]

    return pl.pallas_call(body, name="rope_tables", out_shape=[jax.ShapeDtypeStruct((rows, LANES), F32)] * 2)(pos, freq, sign)


def _rope(x, cos, sin_signed):
    lane = lax.broadcasted_iota(jnp.int32, x.shape, 1)
    half = QK_ROPE_DIM // 2
    swapped = jnp.where(lane < half, pltpu.roll(x, LANES - half, 1), pltpu.roll(x, half, 1))
    return x * cos + swapped * sin_signed


def _attn_prep(q, kv, proj, kpe_col, cos, sin, *, tr=256):
    rows = q.shape[0]
    heads = q.shape[1] // HEAD_SLOT
    tr = _tile(rows, tr, SUBLANES)

    def body(q_ref, kv_ref, kpe_ref, cos_ref, sin_ref, qc_ref, kc_ref, v_ref):
        c, s = cos_ref[...], sin_ref[...]
        qc_ref[:, :LANES] = q_ref[:, :LANES].astype(BF16)
        qc_ref[:, LANES:] = _rope(q_ref[:, LANES:], c, s).astype(BF16)
        kc_ref[:, :LANES] = kv_ref[:, :LANES].astype(BF16)
        kc_ref[:, LANES:] = _rope(kpe_ref[...], c, s).astype(BF16)
        v_ref[...] = kv_ref[:, LANES:].astype(BF16)

    slot = pl.BlockSpec((tr, HEAD_SLOT), lambda i, h: (i, h))
    tab = pl.BlockSpec((tr, LANES), lambda i, h: (i, 0))
    return pl.pallas_call(
        body, name="attn_prep", grid=(rows // tr, heads),
        in_specs=[slot, slot, pl.BlockSpec((tr, LANES), lambda i, h: (i, kpe_col)), tab, tab],
        out_specs=[slot, slot, pl.BlockSpec((tr, LANES), lambda i, h: (i, h))],
        out_shape=[jax.ShapeDtypeStruct((rows, heads * HEAD_SLOT), BF16)] * 2
        + [jax.ShapeDtypeStruct((rows, heads * LANES), BF16)],
        compiler_params=_params(("parallel", "parallel")),
    )(q, kv, proj, cos, sin)


def _causal(i, j, tq, tk):
    qpos = i * tq + lax.broadcasted_iota(jnp.int32, (tq, tk), 0)
    kpos = j * tk + lax.broadcasted_iota(jnp.int32, (tq, tk), 1)
    return kpos <= qpos


def _attn_fwd(qc, kc, vb, *, scale, tq=512):
    rows = qc.shape[0]
    heads = qc.shape[1] // HEAD_SLOT
    tq = _tile(rows, tq, SUBLANES)
    tk = tq

    def body(q_ref, k_ref, v_ref, o_ref, lse_ref):
        i = pl.program_id(1)
        q = q_ref[...]

        def step(j, carry):
            m, l, acc = carry
            k0 = pl.multiple_of(j * tk, tk)
            s = _dot_nt(q, k_ref[pl.ds(k0, tk), :]) * scale
            s = jnp.where(_causal(i, j, tq, tk), s, NEG_INF)
            m_new = jnp.maximum(m, jnp.max(s, axis=-1, keepdims=True))
            p = jnp.exp(s - m_new)
            alpha = jnp.exp(m - m_new)
            l = alpha * l + jnp.sum(p, axis=-1, keepdims=True)
            acc = alpha * acc + _dot_nn(p.astype(BF16), v_ref[pl.ds(k0, tk), :])
            return m_new, l, acc

        init = (jnp.full((tq, 1), NEG_INF, F32), jnp.zeros((tq, 1), F32), jnp.zeros((tq, LANES), F32))
        m, l, acc = lax.fori_loop(0, i + 1, step, init)
        o_ref[...] = acc / l
        lse_ref[...] = jnp.broadcast_to(m + jnp.log(l), (tq, LANES))

    return pl.pallas_call(
        body, name="attn_fwd", grid=(heads, rows // tq),
        in_specs=[pl.BlockSpec((tq, HEAD_SLOT), lambda h, i: (i, h)), pl.BlockSpec((rows, HEAD_SLOT), lambda h, i: (0, h)),
                  pl.BlockSpec((rows, LANES), lambda h, i: (0, h))],
        out_specs=[pl.BlockSpec((tq, LANES), lambda h, i: (i, h))] * 2,
        out_shape=[jax.ShapeDtypeStruct((rows, heads * LANES), F32)] * 2,
        compiler_params=_params(("parallel", "parallel")),
    )(qc, kc, vb)


def _attn_bwd_q(qc, kc, vb, o, do, lse, cos, sin, *, scale, tq=512):
    rows = qc.shape[0]
    heads = qc.shape[1] // HEAD_SLOT
    tq = _tile(rows, tq, SUBLANES)
    tk = tq

    def body(q_ref, k_ref, v_ref, o_ref, do_ref, lse_ref, cos_ref, sin_ref, dq_ref, delta_ref):
        i = pl.program_id(1)
        q = q_ref[...]
        dov = do_ref[...]
        delta = jnp.sum(dov * o_ref[...], axis=-1, keepdims=True)
        delta_ref[...] = jnp.broadcast_to(delta, (tq, LANES))
        dob = dov.astype(BF16)
        lse_col = lse_ref[:, :1]

        def step(j, dq):
            k0 = pl.multiple_of(j * tk, tk)
            kb = k_ref[pl.ds(k0, tk), :]
            s = _dot_nt(q, kb) * scale
            p = jnp.where(_causal(i, j, tq, tk), jnp.exp(s - lse_col), 0.0)
            dp = _dot_nt(dob, v_ref[pl.ds(k0, tk), :])
            ds = p * (dp - delta)
            return dq + _dot_nn(ds.astype(BF16), kb)

        dq = lax.fori_loop(0, i + 1, step, jnp.zeros((tq, HEAD_SLOT), F32)) * scale
        dq_ref[:, :LANES] = dq[:, :LANES].astype(dq_ref.dtype)
        dq_ref[:, LANES:] = _rope(dq[:, LANES:], cos_ref[...], -sin_ref[...]).astype(dq_ref.dtype)

    qblk = pl.BlockSpec((tq, HEAD_SLOT), lambda h, i: (i, h))
    vblk = pl.BlockSpec((tq, LANES), lambda h, i: (i, h))
    tab = pl.BlockSpec((tq, LANES), lambda h, i: (i, 0))
    return pl.pallas_call(
        body, name="attn_bwd_q", grid=(heads, rows // tq),
        in_specs=[qblk, pl.BlockSpec((rows, HEAD_SLOT), lambda h, i: (0, h)), pl.BlockSpec((rows, LANES), lambda h, i: (0, h)),
                  vblk, vblk, vblk, tab, tab],
        out_specs=[qblk, vblk],
        out_shape=[jax.ShapeDtypeStruct((rows, heads * HEAD_SLOT), BF16), jax.ShapeDtypeStruct((rows, heads * LANES), F32)],
        compiler_params=_params(("parallel", "parallel")),
    )(qc, kc, vb, o, do, lse, cos, sin)


def _attn_bwd_kv(qc, kc, vb, do, lse, delta, cos, sin, *, scale, tk=512):
    rows = qc.shape[0]
    heads = qc.shape[1] // HEAD_SLOT
    tk = _tile(rows, tk, SUBLANES)
    tq = tk
    nq = rows // tq

    def body(q_ref, k_ref, v_ref, do_ref, lse_ref, delta_ref, cos_ref, sin_ref, dkv_ref, dkpe_ref):
        j, h = pl.program_id(0), pl.program_id(1)
        kb, vv = k_ref[...], v_ref[...]

        def step(i, carry):
            dk, dv = carry
            q0 = pl.multiple_of(i * tq, tq)
            qb = q_ref[pl.ds(q0, tq), :]
            dob = do_ref[pl.ds(q0, tq), :].astype(BF16)
            s = _dot_nt(qb, kb) * scale
            p = jnp.where(_causal(i, j, tq, tk), jnp.exp(s - lse_ref[pl.ds(q0, tq), :1]), 0.0)
            dv = dv + _dot_tn(p.astype(BF16), dob)
            ds = p * (_dot_nt(dob, vv) - delta_ref[pl.ds(q0, tq), :1])
            dk = dk + _dot_tn(ds.astype(BF16), qb)
            return dk, dv

        dk, dv = lax.fori_loop(j, nq, step, (jnp.zeros((tk, HEAD_SLOT), F32), jnp.zeros((tk, LANES), F32)))
        dkv_ref[:, :LANES] = (dk[:, :LANES] * scale).astype(dkv_ref.dtype)
        dkv_ref[:, LANES:] = dv.astype(dkv_ref.dtype)
        part = dk[:, LANES:] * scale

        @pl.when(h == 0)
        def _():
            dkpe_ref[...] = part

        @pl.when(h > 0)
        def _():
            dkpe_ref[...] += part

        @pl.when(h == heads - 1)
        def _():
            dkpe_ref[...] = _rope(dkpe_ref[...], cos_ref[...], -sin_ref[...])

    full_q = pl.BlockSpec((rows, HEAD_SLOT), lambda j, h: (0, h))
    full_v = pl.BlockSpec((rows, LANES), lambda j, h: (0, h))
    tab = pl.BlockSpec((tk, LANES), lambda j, h: (j, 0))
    return pl.pallas_call(
        body, name="attn_bwd_kv", grid=(rows // tk, heads),
        in_specs=[full_q, pl.BlockSpec((tk, HEAD_SLOT), lambda j, h: (j, h)), pl.BlockSpec((tk, LANES), lambda j, h: (j, h)),
                  full_v, full_v, full_v, tab, tab],
        out_specs=[pl.BlockSpec((tk, HEAD_SLOT), lambda j, h: (j, h)), pl.BlockSpec((tk, LANES), lambda j, h: (j, 0))],
        out_shape=[jax.ShapeDtypeStruct((rows, heads * HEAD_SLOT), BF16), jax.ShapeDtypeStruct((rows, LANES), F32)],
        compiler_params=_params(("parallel", "arbitrary")),
    )(qc, kc, vb, do, lse, delta, cos, sin)


def _shift_down(x, d):
    row = lax.broadcasted_iota(jnp.int32, x.shape, 0)
    return jnp.where(row >= d, pltpu.roll(x, d, 0), 0.0)


def _shift_up(x, d):
    rows = x.shape[0]
    row = lax.broadcasted_iota(jnp.int32, x.shape, 0)
    return jnp.where(row < rows - d, pltpu.roll(x, rows - d, 0), 0.0)


def _conv3(a, w, b):
    return w[2:3, :] * a + w[1:2, :] * _shift_down(a, 1) + w[0:1, :] * _shift_down(a, 2) + b


def _conv_gate_fwd(a, conv_w, conv_b, *, tc=256):
    rows, f2 = a.shape
    f = f2 // 2
    tc = _tile(f, tc)
    nc = f // tc

    def body(ag_ref, av_ref, wg_ref, wv_ref, bg_ref, bv_ref, o_ref):
        gate = _conv3(ag_ref[...], wg_ref[...], bg_ref[...])
        val = _conv3(av_ref[...], wv_ref[...], bv_ref[...])
        o_ref[...] = (gate * jax.nn.sigmoid(gate) * val).astype(o_ref.dtype)

    return pl.pallas_call(
        body, name="conv_gate_fwd", grid=(nc,),
        in_specs=[pl.BlockSpec((rows, tc), lambda j: (0, j)), pl.BlockSpec((rows, tc), lambda j: (0, j + nc)),
                  pl.BlockSpec((SUBLANES, tc), lambda j: (0, j)), pl.BlockSpec((SUBLANES, tc), lambda j: (0, j + nc)),
                  pl.BlockSpec((1, tc), lambda j: (0, j)), pl.BlockSpec((1, tc), lambda j: (0, j + nc))],
        out_specs=pl.BlockSpec((rows, tc), lambda j: (0, j)),
        out_shape=jax.ShapeDtypeStruct((rows, f), BF16), compiler_params=_params(("parallel",)),
    )(a, a, conv_w, conv_w, conv_b, conv_b)


def _conv_gate_bwd(a, conv_w, conv_b, dg, *, tc=256):
    rows, f2 = a.shape
    f = f2 // 2
    tc = _tile(f, tc)
    nc = f // tc

    def conv_bwd(a_val, w, d_out):
        da = w[2:3, :] * d_out + w[1:2, :] * _shift_up(d_out, 1) + w[0:1, :] * _shift_up(d_out, 2)
        db = jnp.sum(d_out, axis=0, keepdims=True)
        row = lax.broadcasted_iota(jnp.int32, (SUBLANES, a_val.shape[1]), 0)
        dw = jnp.zeros((SUBLANES, a_val.shape[1]), F32)
        for tap in range(3):
            t = jnp.sum(d_out * (_shift_down(a_val, 2 - tap) if tap < 2 else a_val), axis=0, keepdims=True)
            dw = jnp.where(row == tap, t, dw)
        return da, dw, db

    def body(ag_ref, av_ref, wg_ref, wv_ref, bg_ref, bv_ref, dg_ref, da_ref, dw_ref, db_ref):
        ag, av, wg, wv = ag_ref[...], av_ref[...], wg_ref[...], wv_ref[...]
        gate = _conv3(ag, wg, bg_ref[...])
        val = _conv3(av, wv, bv_ref[...])
        sg = jax.nn.sigmoid(gate)
        dgv = dg_ref[...]
        d_gate = dgv * val * sg * (1.0 + gate * (1.0 - sg))
        d_val = dgv * gate * sg
        for half, (a_val, w, d_out) in enumerate(((ag, wg, d_gate), (av, wv, d_val))):
            da, dw, db = conv_bwd(a_val, w, d_out)
            da_ref[half] = da.astype(da_ref.dtype)
            dw_ref[half] = dw
            db_ref[half] = db

    lo = lambda j: (0, j)
    hi = lambda j: (0, j + nc)
    both = lambda j: (0, 0, j)
    return pl.pallas_call(
        body, name="conv_gate_bwd", grid=(nc,),
        in_specs=[pl.BlockSpec((rows, tc), lo), pl.BlockSpec((rows, tc), hi), pl.BlockSpec((SUBLANES, tc), lo),
                  pl.BlockSpec((SUBLANES, tc), hi), pl.BlockSpec((1, tc), lo), pl.BlockSpec((1, tc), hi),
                  pl.BlockSpec((rows, tc), lo)],
        out_specs=[pl.BlockSpec((2, rows, tc), both), pl.BlockSpec((2, SUBLANES, tc), both), pl.BlockSpec((2, 1, tc), both)],
        out_shape=[jax.ShapeDtypeStruct((2, rows, f), BF16), jax.ShapeDtypeStruct((2, SUBLANES, f), F32),
                   jax.ShapeDtypeStruct((2, 1, f), F32)],
        compiler_params=_params(("parallel",)),
    )(a, a, conv_w, conv_w, conv_b, conv_b, dg)


def _wgrad(a, b, rows, cols, row_sharded, name, **kw):
    make, (sr, sc) = _wgrad_blocks(rows, cols, row_sharded)
    tm = kw.pop("tm", _tile(sr, 512))
    tn = kw.pop("tn", _tile(sc, 1024))
    return _matmul(a, b, mode="tn", name=name, tm=tm, tn=tn, out_blocks=make, **kw)


def _block_diag(x):
    nb, g, r, c = x.shape
    eye = jnp.eye(g, dtype=x.dtype)
    return (x[:, :, :, None, :] * eye[None, :, None, :, None]).reshape(nb, g * r, g * c)


def _block_diag_part(x, r, c):
    nb = x.shape[0]
    g = GROUPS_PER_BATCH
    eye = jnp.eye(g, dtype=x.dtype)
    return jnp.sum(x.reshape(nb, g, r, g, c) * eye[None, :, None, :, None], axis=3)


class _NoExchange:
    def __init__(self, ffn):
        self.ffn = ffn

    def ffn_weights_arrived(self, after):
        return None

    def ffn_weights(self, after):
        return self.ffn

    def ffn_grads(self, g_down, g_up, after):
        return None

    def ffn_backward_done(self, after):
        return None


def _local_step(x, posf, target, w, hooks):
    rows, d = x.shape
    width = w["ssm_d"].shape[1]
    qr, kvr = w["mla_q_norm_w"].shape[1], w["mla_kv_norm_w"].shape[1]
    heads = w["mla_w_ukv"].shape[1] // HEAD_SLOT
    f2 = w["ffn_conv_b"].shape[1]
    inp = w["w_in"].shape[0]
    groups = width // SSM_GROUP
    nb = groups // GROUPS_PER_BATCH
    scale = (QK_NOPE_DIM + QK_ROPE_DIM) ** -0.5
    g = {}

    hn = _rmsnorm_fwd(x, w["attn_norm_w"], name="attn_norm")
    proj = _matmul(hn, w["w_in"], mode="nt", name="in_proj")

    ar, ai, bbr, bbi = _s5_params(w["ssm_lambda_re"], w["ssm_lambda_im"], w["ssm_log_dt"], w["ssm_b_re"], w["ssm_b_im"])

    def b_band(bb):
        return _block_diag(bb.reshape(nb, GROUPS_PER_BATCH, SSM_STATE, SSM_GROUP).transpose(0, 1, 3, 2))

    def c_band(c):
        return _block_diag(c.reshape(nb, GROUPS_PER_BATCH, SSM_GROUP, SSM_STATE).transpose(0, 1, 3, 2))

    wb = jnp.concatenate([b_band(bbr), b_band(bbi)], axis=2).astype(BF16)
    wc = jnp.concatenate([c_band(w["ssm_c_re"]), -c_band(w["ssm_c_im"])], axis=1).astype(BF16)
    abar = jnp.concatenate([ar.reshape(nb, 1, STATE_PER_BATCH), ai.reshape(nb, 1, STATE_PER_BATCH)], axis=2)
    states, y_pre, yg = _s5_fwd(proj, wb, wc, w["ssm_d"], abar)
    z = _matmul(yg, w["ssm_w_glu"], mode="nn", name="glu_proj", bias=w["ssm_b_glu"])
    ys = _glu_norm_fwd(y_pre, z, w["ssm_out_norm_w"])

    q_col, kv_col, kpe_col = width // qr, (width + qr) // kvr, (width + qr + kvr) // LANES
    assert width % qr == 0 and (width + qr) % kvr == 0
    qn = _rmsnorm_fwd(proj, w["mla_q_norm_w"], name="q_norm", width=qr, col=q_col)
    kvn = _rmsnorm_fwd(proj, w["mla_kv_norm_w"], name="kv_norm", width=kvr, col=kv_col)
    q = _matmul(qn, w["mla_w_uq"], mode="nn", name="q_proj")
    kv = _matmul(kvn, w["mla_w_ukv"], mode="nn", name="kv_proj")
    half = QK_ROPE_DIM // 2
    inv_freq = ROPE_THETA ** (-jnp.arange(0, QK_ROPE_DIM, 2, dtype=F32) / QK_ROPE_DIM)
    zeros = jnp.zeros((LANES - QK_ROPE_DIM,), F32)
    freq = jnp.concatenate([inv_freq, inv_freq, zeros]).reshape(1, LANES)
    sign = jnp.concatenate([-jnp.ones((half,), F32), jnp.ones((half,), F32), zeros]).reshape(1, LANES)
    cos, sin = _rope_tables(posf, freq, sign)
    qc, kc, vb = _attn_prep(q, kv, proj, kpe_col, cos, sin)
    o, lse = _attn_fwd(qc, kc, vb, scale=scale, tq=ATTN_BLOCK)
    ym = _rmsnorm_fwd(o, w["mla_out_norm_w"], name="mla_out_norm")
    ycat = jnp.concatenate([ys, ym], axis=1)
    h1 = _matmul(ycat, w["w_out"], mode="nn", name="out_proj", add=x, after=hooks.ffn_weights_arrived(ycat))

    hn2 = _rmsnorm_fwd(h1, w["ffn_norm_w"], name="ffn_norm")
    ffn = hooks.ffn_weights(hn2)
    a = _matmul(hn2, ffn["ffn_w_up"], mode="nn", name="ffn_up", tm=FFN_ROWS)
    gated = _conv_gate_fwd(a, ffn["ffn_conv_w"], w["ffn_conv_b"])
    h2 = _matmul(gated, ffn["ffn_w_down"], mode="nn", name="ffn_down", add=h1, tk=2816, tm=FFN_ROWS)
    loss_tile, dh2, dh2_mxu, g["final_norm_w"] = _final_norm_loss(h2, w["final_norm_w"], target)

    dgated = _matmul(dh2_mxu, ffn["ffn_w_down"], mode="nt", name="ffn_down_dx", tm=FFN_ROWS)
    g["ffn_w_down"] = _wgrad(gated, dh2_mxu, f2 // 2, d, True, "ffn_down_dw", tm=f2 // 2 // N_CHIPS, tn=1024)
    da, dcw, dcb = _conv_gate_bwd(a, ffn["ffn_conv_w"], w["ffn_conv_b"], dgated)
    g["ffn_conv_w"] = jnp.concatenate([dcw[0, :3], dcw[1, :3]], axis=1)
    g["ffn_conv_b"] = jnp.concatenate([dcb[0], dcb[1]], axis=1)
    g["ffn_w_up"] = _wgrad(hn2, da, d, f2, False, "ffn_up_dw", b_split=True, tm=FFN_ROWS, tn=_tile(f2 // N_CHIPS, 1408))
    started = hooks.ffn_grads(g["ffn_w_down"], g["ffn_w_up"], dcb)
    dhn2 = _matmul(da, ffn["ffn_w_up"], mode="nt", name="ffn_up_dx", a_split=True, tk=_tile(f2 // 2, 2816), tm=FFN_ROWS,
                   after=started)
    dh1, dh1_mxu, g["ffn_norm_w"] = _rmsnorm_bwd(h1, w["ffn_norm_w"], dhn2, name="ffn_norm_bwd", add=dh2,
                                                dx_dtypes=(F32, BF16))

    dycat = _matmul(dh1_mxu, w["w_out"], mode="nt", name="out_proj_dx")
    g["w_out"] = _wgrad(ycat, dh1_mxu, 2 * width, d, True, "out_proj_dw")
    started = hooks.ffn_backward_done(dycat)
    mla_out_norm_w, ssm_out_norm_w = w["mla_out_norm_w"], w["ssm_out_norm_w"]
    if started is not None:
        mla_out_norm_w, ssm_out_norm_w = mla_out_norm_w + started[:1, :1], ssm_out_norm_w + started[:1, :1]

    do, g["mla_out_norm_w"] = _rmsnorm_bwd(o, mla_out_norm_w, dycat, name="mla_out_norm_bwd", width=width, dy_col=1)
    dq, delta = _attn_bwd_q(qc, kc, vb, o, do, lse, cos, sin, scale=scale, tq=ATTN_BLOCK)
    dkv, dkpe = _attn_bwd_kv(qc, kc, vb, do, lse, delta, cos, sin, scale=scale, tk=ATTN_BLOCK)
    g["mla_w_uq"] = _wgrad(qn, dq, qr, heads * HEAD_SLOT, False, "q_proj_dw")
    dqn = _matmul(dq, w["mla_w_uq"], mode="nt", name="q_proj_dx")
    dcq, g["mla_q_norm_w"] = _rmsnorm_bwd(proj, w["mla_q_norm_w"], dqn, name="q_norm_bwd", width=qr, col=q_col,
                                          dx_dtypes=(BF16,))
    g["mla_w_ukv"] = _wgrad(kvn, dkv, kvr, heads * HEAD_SLOT, False, "kv_proj_dw")
    dkvn = _matmul(dkv, w["mla_w_ukv"], mode="nt", name="kv_proj_dx")
    dckv, g["mla_kv_norm_w"] = _rmsnorm_bwd(proj, w["mla_kv_norm_w"], dkvn, name="kv_norm_bwd", width=kvr, col=kv_col,
                                            dx_dtypes=(BF16,))

    dz, dyg_a, g["ssm_out_norm_w"], g["ssm_b_glu"] = _glu_norm_bwd(y_pre, z, ssm_out_norm_w, dycat)
    dyg_b = _matmul(dz, w["ssm_w_glu"], mode="nt", name="glu_proj_dx")
    g["ssm_w_glu"] = _wgrad(yg, dz, width, width, True, "glu_proj_dw")
    du, dwb, dwc, dabar, g["ssm_d"] = _s5_bwd(proj, states, y_pre, dyg_a, dyg_b, wb, wc, w["ssm_d"], abar)

    def b_unband(x):
        return _block_diag_part(x, SSM_GROUP, SSM_STATE).transpose(0, 1, 3, 2).reshape(groups, SSM_STATE * SSM_GROUP)

    def c_unband(x):
        return _block_diag_part(x, SSM_STATE, SSM_GROUP).transpose(0, 1, 3, 2).reshape(groups, SSM_GROUP, SSM_STATE)

    st = STATE_PER_BATCH
    g["ssm_c_re"] = c_unband(dwc[:, :st, :])
    g["ssm_c_im"] = -c_unband(dwc[:, st:, :])
    d_ar = dabar[:, 0, :st].reshape(groups, SSM_STATE)
    d_ai = dabar[:, 0, st:].reshape(groups, SSM_STATE)
    (g["ssm_lambda_re"], g["ssm_lambda_im"], g["ssm_log_dt"], g["ssm_b_re"], g["ssm_b_im"]) = _s5_params_bwd(
        w["ssm_lambda_re"], w["ssm_lambda_im"], w["ssm_log_dt"], w["ssm_b_re"], w["ssm_b_im"], d_ar, d_ai,
        b_unband(dwb[:, :, :st]), b_unband(dwb[:, :, st:]))

    pad = jnp.zeros((rows, inp - (width + qr + kvr + LANES)), BF16)
    dproj = jnp.concatenate([du, dcq, dckv, dkpe.astype(BF16), pad], axis=1)
    g["w_in"] = _wgrad(dproj, hn, inp, d, False, "in_proj_dw")
    dhn = _matmul(dproj, w["w_in"], mode="nn", name="in_proj_dx")
    dx, g["attn_norm_w"] = _rmsnorm_bwd(x, w["attn_norm_w"], dhn, name="attn_norm_bwd", add=dh1)
    return loss_tile, dx, g


ANY = pl.BlockSpec(memory_space=pl.ANY)
MESH = pl.DeviceIdType.MESH


def _mesh_pos():
    return lax.axis_index("x"), lax.axis_index("y"), lax.axis_index("c")


def _other_chips(x, y):
    return [(1 - x, y), (x, 1 - y), (1 - x, 1 - y)]


def _remote(src, dst, send_sems, recv_sems, k, to):
    return pltpu.make_async_remote_copy(src_ref=src, dst_ref=dst, send_sem=send_sems.at[k], recv_sem=recv_sems.at[k],
                                        device_id=to, device_id_type=MESH)


def _place_shard(shard, piece_idx, row_sharded, name, out_dtype=BF16, pieces=N_CHIPS):
    rs, cs = shard.shape
    tr = _tile(rs, 256, 2 * SUBLANES)
    rb = rs // tr

    def body(p_ref, x_ref, o_ref):
        o_ref[...] = x_ref[...].astype(o_ref.dtype)

    if row_sharded:
        out_shape, out_map = (pieces * rs, cs), (lambda i, p_ref: (p_ref[0] * rb + i, 0))
    else:
        out_shape, out_map = (rs, pieces * cs), (lambda i, p_ref: (i, p_ref[0]))
    return pl.pallas_call(
        body, name=name, out_shape=jax.ShapeDtypeStruct(out_shape, out_dtype),
        grid_spec=pltpu.PrefetchScalarGridSpec(
            num_scalar_prefetch=1, grid=(rb,), in_specs=[pl.BlockSpec((tr, cs), lambda i, p_ref: (i, 0))],
            out_specs=pl.BlockSpec((tr, cs), out_map)),
        compiler_params=_params(("parallel",)),
    )(piece_idx, shard)


def _gather_weights(placed, name):
    n = len(placed)
    meta = [(row_sharded, direct) for _, row_sharded, direct in placed]
    over_ici, over_d2d = _gather_plans(meta)
    forwarded = [t for t, (_, direct) in enumerate(meta) if not direct]

    def body(*refs):
        outs = refs[n:2 * n]
        send_sems, recv_sems, pass_send_sems, pass_recv_sems = refs[2 * n:]
        first, arrivals = over_ici(outs, send_sems, recv_sems)
        passed, passed_arrivals = over_d2d([outs[t] for t in forwarded], pass_send_sems, pass_recv_sems)
        for cp in first:
            cp.start()
        for t in range(n):
            for j in range(3):
                arrivals[3 * t + j].wait_recv()
                if t in forwarded:
                    passed[3 * forwarded.index(t) + j].start()
        for cp in passed_arrivals:
            cp.wait_recv()
        for cp in first + passed:
            cp.wait_send()

    return pl.pallas_call(
        body, name=name, in_specs=[ANY] * n, out_specs=[ANY] * n,
        out_shape=[jax.ShapeDtypeStruct(arr.shape, arr.dtype) for arr, _, _ in placed],
        input_output_aliases={t: t for t in range(n)},
        scratch_shapes=[pltpu.SemaphoreType.DMA((3 * n,)), pltpu.SemaphoreType.DMA((3 * n,)),
                        pltpu.SemaphoreType.DMA((3 * len(forwarded),)), pltpu.SemaphoreType.DMA((3 * len(forwarded),))],
    )(*[arr for arr, _, _ in placed])


def _gather_plans(meta):
    def window(ref, row_sharded, piece, half):
        r, cc = ref.shape
        if row_sharded:
            rs = r // N_CHIPS
            if half is None:
                return ref.at[pl.ds(piece * rs, rs), :]
            return ref.at[pl.ds(piece * rs + half * (rs // 2), rs // 2), :]
        cs = cc // N_CHIPS
        if half is None:
            return ref.at[:, pl.ds(piece * cs, cs)]
        return ref.at[pl.ds(half * (r // 2), r // 2), pl.ds(piece * cs, cs)]

    def over_ici(refs, send_sems, recv_sems):
        x, y, c = _mesh_pos()
        sends, recvs = [], []
        for t, (row_sharded, direct) in enumerate(meta):
            mine = window(refs[t], row_sharded, 2 * x + y, None if direct else c)
            for j, (px, py) in enumerate(_other_chips(x, y)):
                theirs = window(refs[t], row_sharded, 2 * px + py, None if direct else c)
                sends.append(_remote(mine, mine, send_sems, recv_sems, 3 * t + j, (px, py, c)))
                recvs.append(_remote(theirs, theirs, send_sems, recv_sems, 3 * t + j, (px, py, c)))
        return sends, recvs

    def over_d2d(refs, send_sems, recv_sems):
        x, y, c = _mesh_pos()
        sends, recvs = [], []
        rows = [row_sharded for row_sharded, direct in meta if not direct]
        for t, row_sharded in enumerate(rows):
            for j, (px, py) in enumerate(_other_chips(x, y)):
                got = window(refs[t], row_sharded, 2 * px + py, c)
                other = window(refs[t], row_sharded, 2 * px + py, 1 - c)
                sends.append(_remote(got, got, send_sems, recv_sems, 3 * t + j, (x, y, 1 - c)))
                recvs.append(_remote(other, other, send_sems, recv_sems, 3 * t + j, (x, y, 1 - c)))
        return sends, recvs

    return over_ici, over_d2d


HBM = pl.BlockSpec(memory_space=pltpu.HBM)
SEMAPHORES = pl.BlockSpec(memory_space=pltpu.SEMAPHORE)
DATAFLOW = pltpu.SideEffectType.DATAFLOW_SIDE_EFFECTING


def _start_copies(name, arrays, plan, n_copies, after):
    n = len(arrays)

    def body(*refs):
        sends, _ = plan(refs[:n], refs[n + 1], refs[n + 2])
        for cp in sends:
            cp.start()
        token = refs[2 * n + 3]
        token[...] = jnp.zeros_like(token)

    out = pl.pallas_call(
        body, name=name,
        out_shape=(pltpu.SemaphoreType.DMA((n_copies,)), pltpu.SemaphoreType.DMA((n_copies,)),
                   *[pltpu.HBM(a.shape, a.dtype) for a in arrays], jax.ShapeDtypeStruct((SUBLANES, LANES), F32)),
        in_specs=[HBM] * n + [ANY],
        out_specs=(SEMAPHORES, SEMAPHORES, *[HBM] * n, pl.BlockSpec(memory_space=pltpu.VMEM)),
        input_output_aliases={t: t + 2 for t in range(n)},
        compiler_params=pltpu.CompilerParams(has_side_effects=DATAFLOW),
    )(*[pltpu.with_memory_space_constraint(a, pltpu.HBM) for a in arrays], after)
    return out[0], out[1], list(out[2:2 + n]), out[2 + n]


def _wait_copies(name, started, plan, after):
    send_sems, recv_sems, arrays, _ = started
    n = len(arrays)

    def body(*refs):
        sends, recvs = plan(refs[:n], refs[n], refs[n + 1])
        for cp in sends:
            cp.wait_send()
        for cp in recvs:
            cp.wait_recv()

    out = pl.pallas_call(
        body, name=name, out_shape=[pltpu.HBM(a.shape, a.dtype) for a in arrays],
        in_specs=[HBM] * n + [SEMAPHORES, SEMAPHORES, ANY], out_specs=[HBM] * n,
        input_output_aliases={t: t for t in range(n)},
        compiler_params=pltpu.CompilerParams(has_side_effects=DATAFLOW),
    )(*arrays, send_sems, recv_sems, after)
    return list(out)


def _exchange(name, arrays, out_shapes, plan, n_copies, in_place=False, after=None):
    n = len(arrays)
    extra = [] if after is None else [after]

    def body(*refs):
        ins, outs = refs[:n], refs[n + len(extra):n + len(extra) + len(out_shapes)]
        send_sems, recv_sems = refs[n + len(extra) + len(out_shapes):]
        sends, recvs = plan(ins, outs, send_sems, recv_sems)
        for cp in sends:
            cp.start()
        for cp in recvs:
            cp.wait_recv()
        for cp in sends:
            cp.wait_send()

    return pl.pallas_call(
        body, name=name, in_specs=[ANY] * (n + len(extra)), out_specs=[ANY] * len(out_shapes), out_shape=out_shapes,
        input_output_aliases={t: t for t in range(n)} if in_place else {},
        scratch_shapes=[pltpu.SemaphoreType.DMA((n_copies,)), pltpu.SemaphoreType.DMA((n_copies,))],
    )(*arrays, *extra)


def _swap_plan(n):
    def plan(refs, send_sems, recv_sems):
        x, y, c = _mesh_pos()
        sends = [_remote(refs[t].at[1 - c], refs[n + t], send_sems, recv_sems, t, (x, y, 1 - c)) for t in range(n)]
        return sends, sends

    return plan


def _scatter_plan(n):
    def plan(refs, send_sems, recv_sems):
        x, y, c = _mesh_pos()
        sends = []
        for t in range(n):
            for j, (px, py) in enumerate(_other_chips(x, y)):
                sends.append(_remote(refs[t].at[2 * px + py], refs[n + t].at[j], send_sems, recv_sems, 3 * t + j, (px, py, c)))
        return sends, sends

    return plan


def _swap_shapes(grads):
    return [jax.ShapeDtypeStruct(g.shape[1:], g.dtype) for g in grads]


def _scatter_shapes(sums):
    return [jax.ShapeDtypeStruct((3,) + s.shape[1:], s.dtype) for s in sums]


def _swap_other_half(grads, name):
    plan = _swap_plan(len(grads))
    return _exchange(name, grads, _swap_shapes(grads), lambda ins, outs, s, r: plan(list(ins) + list(outs), s, r), len(grads))


def _join_halves(halves, name, after=None):
    def plan(ins, outs, send_sems, recv_sems):
        x, y, c = _mesh_pos()
        sends = [_remote(outs[t].at[c], outs[t].at[c], send_sems, recv_sems, t, (x, y, 1 - c)) for t in range(len(ins))]
        recvs = [_remote(outs[t].at[1 - c], outs[t].at[1 - c], send_sems, recv_sems, t, (x, y, 1 - c))
                 for t in range(len(ins))]
        return sends, recvs

    shapes = [jax.ShapeDtypeStruct(h.shape, h.dtype) for h in halves]
    return _exchange(name, halves, shapes, plan, len(halves), in_place=True, after=after)


def _add_other_half(g4, got, where, name, wire_dtype=BF16):
    _, pieces, sr, sc = g4.shape
    tr = _tile(sr, 256, 2 * SUBLANES)

    def body(w_ref, a_ref, b_ref, o_ref):
        o_ref[...] = (a_ref[...] + b_ref[...]).astype(o_ref.dtype)

    blk = pl.BlockSpec((None, tr, sc), lambda p, i, w_ref: (p, i, 0))
    return pl.pallas_call(
        body, name=name, out_shape=jax.ShapeDtypeStruct((pieces, sr, sc), wire_dtype),
        grid_spec=pltpu.PrefetchScalarGridSpec(
            num_scalar_prefetch=1, grid=(pieces, sr // tr),
            in_specs=[pl.BlockSpec((None, None, tr, sc), lambda p, i, w_ref: (w_ref[0], p, i, 0)), blk], out_specs=blk),
        compiler_params=_params(("parallel", "parallel")),
    )(where, g4, got)


def _add_pieces(g4, got_half, got_pieces, where, name):
    _, _, sr, sc = g4.shape
    tr = _tile(sr, 256, 2 * SUBLANES)

    def body(w_ref, a_ref, b_ref, r_ref, o_ref):
        acc = a_ref[...] + b_ref[...]
        for j in range(3):
            acc = acc + r_ref[j].astype(F32)
        o_ref[...] = acc

    return pl.pallas_call(
        body, name=name, out_shape=jax.ShapeDtypeStruct((N_CORES, sr, sc), F32),
        grid_spec=pltpu.PrefetchScalarGridSpec(
            num_scalar_prefetch=1, grid=(sr // tr,),
            in_specs=[pl.BlockSpec((None, None, tr, sc), lambda i, w_ref: (w_ref[0], w_ref[1], i, 0)),
                      pl.BlockSpec((None, tr, sc), lambda i, w_ref: (w_ref[1], i, 0)),
                      pl.BlockSpec((3, tr, sc), lambda i, w_ref: (0, i, 0))],
            out_specs=pl.BlockSpec((None, tr, sc), lambda i, w_ref: (w_ref[0], i, 0))),
        compiler_params=_params(("parallel",)),
    )(where, g4, got_half, got_pieces)


def _adamw_update(w, g, m, v):
    nm = ADAM_B1 * m + (1.0 - ADAM_B1) * g
    nv = ADAM_B2 * v + (1.0 - ADAM_B2) * (g * g)
    m_hat = nm / (1.0 - ADAM_B1 ** ADAM_STEP)
    v_hat = nv / (1.0 - ADAM_B2 ** ADAM_STEP)
    return -ADAM_LR * (m_hat / (jnp.sqrt(v_hat) + ADAM_EPS) + ADAM_WD * w), nm, nv


def _adamw(w, g, m, v, name):
    rows, cols = w.shape
    halves = 2 if g.ndim == 3 else 1
    bc = cols // halves
    tr = _tile(rows, max(SUBLANES, (1 << 19) // max(bc, 1) // SUBLANES * SUBLANES), SUBLANES)

    def body(w_ref, g_ref, m_ref, v_ref, d_ref, nm_ref, nv_ref, go_ref):
        gv = g_ref[...]
        d_ref[...], nm_ref[...], nv_ref[...] = _adamw_update(w_ref[...], gv, m_ref[...], v_ref[...])
        go_ref[...] = gv

    blk = pl.BlockSpec((tr, bc), lambda i, h: (i, h))
    g_blk = pl.BlockSpec((None, tr, bc), lambda i, h: (h, i, 0)) if halves == 2 else blk
    return pl.pallas_call(
        body, name=name, grid=(rows // tr, halves), in_specs=[blk, g_blk, blk, blk], out_specs=[blk] * 4,
        out_shape=[jax.ShapeDtypeStruct((rows, cols), F32)] * 4, compiler_params=_params(("parallel", "parallel")),
    )(w, g, m, v)


def _adamw_many(ws, gs, ms, vs, name):
    n = len(ws)

    def body(*refs):
        outs = refs[4 * n:]
        for k in range(n):
            w_ref, g_ref, m_ref, v_ref = (refs[j * n + k] for j in range(4))
            outs[k][...], outs[n + k][...], outs[2 * n + k][...] = _adamw_update(w_ref[...], g_ref[...], m_ref[...], v_ref[...])

    out = pl.pallas_call(
        body, name=name, out_shape=[jax.ShapeDtypeStruct(w.shape, F32) for w in ws] * 3,
        compiler_params=pltpu.CompilerParams(vmem_limit_bytes=VMEM_LIMIT_BYTES),
    )(*ws, *gs, *ms, *vs)
    return out[:n], out[n:2 * n], out[2 * n:]


WEIGHTS = ['attn_norm_w', 'w_in', 'ssm_lambda_re', 'ssm_lambda_im', 'ssm_log_dt', 'ssm_b_re', 'ssm_b_im', 'ssm_c_re',
           'ssm_c_im', 'ssm_d', 'ssm_w_glu', 'ssm_b_glu', 'mla_q_norm_w', 'mla_w_uq', 'mla_kv_norm_w', 'mla_w_ukv',
           'ssm_out_norm_w', 'mla_out_norm_w', 'w_out', 'ffn_norm_w', 'ffn_w_up', 'ffn_conv_w', 'ffn_conv_b',
           'ffn_w_down', 'final_norm_w']
SHARDED = {'w_in': False, 'ssm_w_glu': True, 'mla_w_uq': False, 'mla_w_ukv': False, 'w_out': True, 'ffn_w_up': False,
           'ffn_w_down': True}
SMALL = [n for n in WEIGHTS if n not in SHARDED and n != 'ffn_conv_w']
ROPE_PAD = HEAD_SLOT - QK_NOPE_DIM - QK_ROPE_DIM
SMALL_COLS = 8 * LANES


def _pad_heads(w_uq, heads):
    qr = w_uq.shape[0]
    w3 = w_uq.reshape(qr, heads, QK_NOPE_DIM + QK_ROPE_DIM)
    return jnp.concatenate([w3, jnp.zeros((qr, heads, ROPE_PAD), w_uq.dtype)], axis=2).reshape(qr, heads * HEAD_SLOT)


def _unpad_heads(g_uq, heads):
    qr = g_uq.shape[0]
    return g_uq.reshape(qr, heads, HEAD_SLOT)[:, :, :QK_NOPE_DIM + QK_ROPE_DIM].reshape(qr, -1)


FFN = ['ffn_w_up', 'ffn_w_down']
FFN_GATHER = FFN + ['ffn_conv_w']
FFN_GATHER_META = [(SHARDED[n], False) for n in FFN] + [(False, True)]


class _Overlapped:
    def __init__(self, placed, where, after):
        self.where = where
        self.over_ici, self.over_d2d = _gather_plans(FFN_GATHER_META)
        self.gather = _start_copies("gather_ffn_start", placed, self.over_ici, 3 * len(placed), after)
        self.gather_started = self.gather[3]

    def ffn_weights_arrived(self, after):
        arrived = _wait_copies("gather_ffn_wait", self.gather, self.over_ici, after)
        n = len(FFN)
        self.direct = arrived[n:]
        self.passing = _start_copies("gather_ffn_pass_start", arrived[:n], self.over_d2d, 3 * n, after)
        return self.passing[3]

    def ffn_weights(self, after):
        passed = _wait_copies("gather_ffn_pass_wait", self.passing, self.over_d2d, after)
        return dict(zip(FFN_GATHER, passed + self.direct))

    def ffn_grads(self, g_down, g_up, after):
        grads = [g_up, g_down]
        lands = [lax.empty(s.shape, s.dtype) for s in _swap_shapes(grads)]
        self.swap = _start_copies("grad_ffn_swap_start", grads + lands, _swap_plan(len(grads)), len(grads), after)
        return self.swap[3]

    def ffn_backward_done(self, after):
        n = len(FFN)
        out = _wait_copies("grad_ffn_swap_wait", self.swap, _swap_plan(n), after)
        self.grads, self.got_half = out[:n], out[n:]
        sums = [_add_other_half(self.grads[t], self.got_half[t], self.where, "grad_add_half_" + name)
                for t, name in enumerate(FFN)]
        lands = [lax.empty(s.shape, s.dtype) for s in _scatter_shapes(sums)]
        self.scatter = _start_copies("grad_ffn_scatter_start", sums + lands, _scatter_plan(n), 3 * n, after)
        return self.scatter[3]

    def ffn_reduced(self, after):
        n = len(FFN)
        got_pieces = _wait_copies("grad_ffn_scatter_wait", self.scatter, _scatter_plan(n), after)[n:]
        return [_add_pieces(self.grads[t], self.got_half[t], got_pieces[t], self.where, "grad_add_pieces_" + name)
                for t, name in enumerate(FFN)]


def _step(args):
    x, positions, target = args["x"][0], args["positions"], args["loss_target"][0]
    rows = x.shape[0]
    p = {n: args[n] for n in WEIGHTS}
    xi, yi, ci = _mesh_pos()
    piece = 2 * xi + yi

    def transposed(a):
        return jnp.swapaxes(a[0], 0, 1)

    w_in = transposed(p["w_in"])
    in_width = w_in.shape[0]
    in_pad = (-in_width) % (2 * LANES)
    heads_here = p["mla_w_uq"].shape[2] // (QK_NOPE_DIM + QK_ROPE_DIM)
    shards = {
        "w_in": jnp.pad(w_in, ((0, in_pad), (0, 0))),
        "ssm_w_glu": p["ssm_w_glu"][0],
        "mla_w_uq": _pad_heads(p["mla_w_uq"][0], heads_here),
        "mla_w_ukv": p["mla_w_ukv"][0],
        "w_out": p["w_out"][0],
        "ffn_w_up": p["ffn_w_up"][0],
        "ffn_w_down": p["ffn_w_down"][0],
    }
    conv_w = jnp.pad(p["ffn_conv_w"][0], ((0, SUBLANES - p["ffn_conv_w"].shape[1]), (0, 0)))
    order = list(SHARDED)
    piece_idx = piece.reshape(1).astype(jnp.int32)
    placed = {n: _place_shard(shards[n], piece_idx, SHARDED[n], "place_" + n) for n in order}
    placed["ffn_conv_w"] = _place_shard(conv_w, piece_idx, False, "place_ffn_conv_w", out_dtype=F32)
    mixer = [n for n in order if n not in FFN]
    w = dict(zip(mixer, _gather_weights([(placed[n], SHARDED[n], False) for n in mixer], "gather_mixer_weights")))
    where = jnp.stack([ci, piece]).astype(jnp.int32)
    hooks = _Overlapped([placed[n] for n in FFN_GATHER], where, after=w["w_in"])
    groups = p["ssm_lambda_re"].shape[1]
    w.update({
        "attn_norm_w": p["attn_norm_w"] + hooks.gather_started[:1, :1],
        "ssm_lambda_re": p["ssm_lambda_re"][0], "ssm_lambda_im": p["ssm_lambda_im"][0],
        "ssm_log_dt": p["ssm_log_dt"].reshape(groups, 1), "ssm_b_re": p["ssm_b_re"].reshape(groups, -1),
        "ssm_b_im": p["ssm_b_im"].reshape(groups, -1), "ssm_c_re": p["ssm_c_re"][0], "ssm_c_im": p["ssm_c_im"][0],
        "ssm_d": p["ssm_d"], "ssm_b_glu": p["ssm_b_glu"], "mla_q_norm_w": p["mla_q_norm_w"],
        "mla_kv_norm_w": p["mla_kv_norm_w"], "ssm_out_norm_w": p["ssm_out_norm_w"], "mla_out_norm_w": p["mla_out_norm_w"],
        "ffn_norm_w": p["ffn_norm_w"], "ffn_conv_b": p["ffn_conv_b"], "final_norm_w": p["final_norm_w"].reshape(1, -1),
    })

    loss_tile, dx, g = _local_step(x, positions.reshape(rows, 1).astype(F32), target, w, hooks)
    loss = lax.psum(loss_tile[0, 0], ("x", "y", "c"))

    flat = [g[n].reshape(-1) for n in SMALL] + [g["ffn_conv_w"].reshape(-1)]
    sizes = [f.shape[0] for f in flat]
    per_block = -(-sum(sizes) // (N_CORES * N_CHIPS * SMALL_COLS))
    small_rows = -(-per_block // (2 * SUBLANES)) * (2 * SUBLANES)
    padded = N_CORES * N_CHIPS * small_rows * SMALL_COLS

    def pack(parts):
        parts = list(parts)
        have = sum(q.shape[0] for q in parts)
        return jnp.concatenate(parts + [jnp.zeros((padded - have,), F32)])

    reduced = mixer + ["small"]
    g_rs = [g[n] for n in mixer] + [pack(flat).reshape(N_CORES, N_CHIPS, small_rows, SMALL_COLS)]
    wire = [BF16] * len(mixer) + [F32]
    got_half = _swap_other_half(g_rs, "grad_swap_halves")
    sums = [_add_other_half(g_rs[t], got_half[t], where, "grad_add_half_" + n, wire[t]) for t, n in enumerate(reduced)]
    lands = [lax.empty(s.shape, s.dtype) for s in _scatter_shapes(sums)]
    scatter_plan = _scatter_plan(len(reduced))
    scatter = _start_copies("grad_mixer_scatter_start", sums + lands, scatter_plan, 3 * len(reduced), dx)

    grads, delta, new_m, new_v = {}, {}, {}, {}

    def finish(n, joined):
        grad = joined if SHARDED[n] else joined.reshape(-1, joined.shape[2])
        if n == "w_in":
            wt, mt, vt = w_in, transposed(args["m_w_in"]), transposed(args["v_w_in"])
            out = _adamw(wt, grad, mt, vt, "adamw_w_in")
            delta[n], new_m[n], new_v[n], grads[n] = (jnp.swapaxes(a, 0, 1)[None] for a in out)
            return
        if n == "mla_w_uq":
            grad = _unpad_heads(grad, heads_here)
        adam(n, grad)

    def adam(n, grad):
        shape = p[n].shape
        out = _adamw(p[n].reshape(shape[1:]), grad, args["m_" + n].reshape(shape[1:]),
                     args["v_" + n].reshape(shape[1:]), "adamw_" + n)
        delta[n], new_m[n], new_v[n], grads[n] = (a.reshape(shape) for a in out)

    for n, joined in zip(FFN, _join_halves(hooks.ffn_reduced(dx), "grad_ffn_join_halves", after=scatter[3])):
        finish(n, joined)
    got_pieces = _wait_copies("grad_mixer_scatter_wait", scatter, scatter_plan, delta[FFN[0]])[len(reduced):]
    halves = [_add_pieces(g_rs[t], got_half[t], got_pieces[t], where, "grad_add_pieces_" + n) for t, n in enumerate(reduced)]
    joined = _join_halves(halves, "grad_join_halves")
    for n, j in zip(mixer, joined):
        finish(n, j)
    eighths = _place_shard(joined[-1].reshape(N_CORES * small_rows, SMALL_COLS), piece_idx, True, "place_small_grads",
                           out_dtype=F32)
    small_sum = _gather_weights([(eighths, True, False)], "gather_small_grads")[0]
    flat_sum = small_sum.reshape(N_CHIPS, N_CORES, small_rows * SMALL_COLS).transpose(1, 0, 2).reshape(-1)
    offs = [0]
    for s in sizes:
        offs.append(offs[-1] + s)
    for k, n in enumerate(SMALL):
        grads[n] = flat_sum[offs[k]:offs[k + 1]].reshape(p[n].shape)
    taps, cols_here = p["ffn_conv_w"].shape[1], p["ffn_conv_w"].shape[2]
    conv_full = flat_sum[offs[len(SMALL)]:offs[len(SMALL) + 1]].reshape(taps, N_CHIPS * cols_here)
    adam("ffn_conv_w", lax.dynamic_slice_in_dim(conv_full, piece * cols_here, cols_here, axis=1))

    def rank2(a):
        return a.reshape(1, -1) if a.ndim == 1 else a

    d_s, m_s, v_s = _adamw_many([rank2(p[n]) for n in SMALL], [rank2(grads[n]) for n in SMALL],
                                [rank2(args["m_" + n]) for n in SMALL], [rank2(args["v_" + n]) for n in SMALL], "adamw_small")
    for k, n in enumerate(SMALL):
        delta[n], new_m[n], new_v[n] = (a.reshape(p[n].shape) for a in (d_s[k], m_s[k], v_s[k]))

    return (loss, dx[None], *[grads[n] for n in WEIGHTS], *[delta[n] for n in WEIGHTS],
            *[new_m[n] for n in WEIGHTS], *[new_v[n] for n in WEIGHTS])


def kernel(x, positions, attn_norm_w, w_in, ssm_lambda_re, ssm_lambda_im, ssm_log_dt, ssm_b_re, ssm_b_im, ssm_c_re, ssm_c_im, ssm_d, ssm_w_glu, ssm_b_glu, mla_q_norm_w, mla_w_uq, mla_kv_norm_w, mla_w_ukv, ssm_out_norm_w, mla_out_norm_w, w_out, ffn_norm_w, ffn_w_up, ffn_conv_w, ffn_conv_b, ffn_w_down, final_norm_w, loss_target, m_attn_norm_w, m_w_in, m_ssm_lambda_re, m_ssm_lambda_im, m_ssm_log_dt, m_ssm_b_re, m_ssm_b_im, m_ssm_c_re, m_ssm_c_im, m_ssm_d, m_ssm_w_glu, m_ssm_b_glu, m_mla_q_norm_w, m_mla_w_uq, m_mla_kv_norm_w, m_mla_w_ukv, m_ssm_out_norm_w, m_mla_out_norm_w, m_w_out, m_ffn_norm_w, m_ffn_w_up, m_ffn_conv_w, m_ffn_conv_b, m_ffn_w_down, m_final_norm_w, v_attn_norm_w, v_w_in, v_ssm_lambda_re, v_ssm_lambda_im, v_ssm_log_dt, v_ssm_b_re, v_ssm_b_im, v_ssm_c_re, v_ssm_c_im, v_ssm_d, v_ssm_w_glu, v_ssm_b_glu, v_mla_q_norm_w, v_mla_w_uq, v_mla_kv_norm_w, v_mla_w_ukv, v_ssm_out_norm_w, v_mla_out_norm_w, v_w_out, v_ffn_norm_w, v_ffn_w_up, v_ffn_conv_w, v_ffn_conv_b, v_ffn_w_down, v_final_norm_w):
    return _step(dict(locals()))
```

```python
import functools
import math

import jax
import jax.numpy as jnp
from jax import lax
from jax.experimental import pallas as pl
from jax.experimental.pallas import tpu as pltpu

F32 = jnp.float32
BF16 = jnp.bfloat16

SSM_GROUP = 16
SSM_STATE = 64
QK_NOPE_DIM = 128
QK_ROPE_DIM = 64
V_HEAD_DIM = 128
ROPE_THETA = 10000.0
RMS_EPS = 1e-6
ADAM_LR, ADAM_B1, ADAM_B2, ADAM_EPS, ADAM_WD, ADAM_STEP = 0.001, 0.9, 0.999, 1e-08, 0.01, 10

LANES = 128
SUBLANES = 8
VMEM_LIMIT_BYTES = 56 * 1024 * 1024

GROUPS_PER_BATCH = LANES // SSM_GROUP
STATE_PER_BATCH = GROUPS_PER_BATCH * SSM_STATE
HEAD_SLOT = 2 * LANES
NEG_INF = -1e30
ATTN_BLOCK = 512
FFN_ROWS = 1024

N_CHIPS = 4
N_CORES = 2


def _tile(n, pref, align=LANES):
    if n <= pref:
        return n
    t = (pref // align) * align
    while t >= align:
        if n % t == 0:
            return t
        t -= align
    return n


def _params(sem):
    return pltpu.CompilerParams(dimension_semantics=sem, vmem_limit_bytes=VMEM_LIMIT_BYTES)


def _dot(a, b, dims):
    return lax.dot_general(a, b, (dims, ((), ())), preferred_element_type=F32)


def _dot_nn(a, b):
    return _dot(a, b, ((1,), (0,)))


def _dot_nt(a, b):
    return _dot(a, b, ((1,), (1,)))


def _dot_tn(a, b):
    return _dot(a, b, ((0,), (0,)))


def _matmul(a, b, *, mode, name, tm=512, tn=1024, tk=2048, bias=None, add=None, out_dtype=F32,
            out_blocks=None, a_split=False, b_split=False, after=None):
    if a_split:
        assert mode == "nt"
        a_shape = (a.shape[1], 2 * a.shape[2])
    else:
        a_shape = a.shape
    if b_split:
        assert mode == "tn"
        b_shape = (b.shape[1], 2 * b.shape[2])
    else:
        b_shape = b.shape
    if mode == "nn":
        (m, k), (k2, n) = a_shape, b_shape
    elif mode == "nt":
        (m, k), (n, k2) = a_shape, b_shape
    else:
        (k, m), (k2, n) = a_shape, b_shape
    assert k == k2, (a.shape, b.shape, mode)
    tm, tn, tk = _tile(m, tm, SUBLANES), _tile(n, tn), _tile(k, tk)
    nk = k // tk
    a_spec = {"nn": pl.BlockSpec((tm, tk), lambda i, j, kk: (i, kk)),
              "nt": pl.BlockSpec((tm, tk), lambda i, j, kk: (i, kk)),
              "tn": pl.BlockSpec((tk, tm), lambda i, j, kk: (kk, i))}[mode]
    b_spec = {"nn": pl.BlockSpec((tk, tn), lambda i, j, kk: (kk, j)),
              "nt": pl.BlockSpec((tn, tk), lambda i, j, kk: (j, kk)),
              "tn": pl.BlockSpec((tk, tn), lambda i, j, kk: (kk, j))}[mode]
    if a_split:
        kb = a.shape[2] // tk
        assert a.shape[2] % tk == 0
        a_spec = pl.BlockSpec((None, tm, tk), lambda i, j, kk: (kk // kb, i, kk % kb))
    if b_split:
        nb = b.shape[2] // tn
        assert b.shape[2] % tn == 0
        b_spec = pl.BlockSpec((None, tk, tn), lambda i, j, kk: (j // nb, kk, j % nb))
    dot = {"nn": _dot_nn, "nt": _dot_nt, "tn": _dot_tn}[mode]
    in_specs, operands = [a_spec, b_spec], [a, b]
    if bias is not None:
        in_specs.append(pl.BlockSpec((1, tn), lambda i, j, kk: (0, j)))
        operands.append(bias)
    if add is not None:
        in_specs.append(pl.BlockSpec((tm, tn), lambda i, j, kk: (i, j)))
        operands.append(add)
    if after is not None:
        in_specs.append(pl.BlockSpec(memory_space=pl.ANY))
        operands.append(after)

    def body(*refs):
        a_ref, b_ref = refs[0], refs[1]
        rest = list(refs[2:])
        bias_ref = rest.pop(0) if bias is not None else None
        add_ref = rest.pop(0) if add is not None else None
        if after is not None:
            rest.pop(0)
        o_ref, acc_ref = rest

        def finish(acc):
            if bias_ref is not None:
                acc = acc + bias_ref[...]
            if add_ref is not None:
                acc = acc + add_ref[...]
            o_ref[...] = acc.astype(o_ref.dtype)

        part = dot(a_ref[...].astype(BF16), b_ref[...].astype(BF16))
        if nk == 1:
            finish(part)
        else:
            kk = pl.program_id(2)

            @pl.when(kk == 0)
            def _():
                acc_ref[...] = part

            @pl.when(jnp.logical_and(kk > 0, kk < nk - 1))
            def _():
                acc_ref[...] += part

            @pl.when(kk == nk - 1)
            def _():
                finish(acc_ref[...] + part)

    if out_blocks is None:
        out_shape = jax.ShapeDtypeStruct((m, n), out_dtype)
        out_spec = pl.BlockSpec((tm, tn), lambda i, j, kk: (i, j))
    else:
        shape, block, index_map = out_blocks(tm, tn)
        out_shape = jax.ShapeDtypeStruct(shape, out_dtype)
        out_spec = pl.BlockSpec(block, index_map)
    acc_shape = (tm, tn) if nk > 1 else (SUBLANES, LANES)
    return pl.pallas_call(
        body, name=name, grid=(m // tm, n // tn, nk), in_specs=in_specs, out_specs=out_spec, out_shape=out_shape,
        scratch_shapes=[pltpu.VMEM(acc_shape, F32)],
        compiler_params=_params(("parallel", "parallel", "arbitrary")),
    )(*operands)


def _wgrad_blocks(rows, cols, row_sharded):
    if row_sharded:
        sr, sc = rows // N_CHIPS, cols // N_CORES
    else:
        sr, sc = rows // N_CORES, cols // N_CHIPS

    def make(tm, tn):
        assert sr % tm == 0 and sc % tn == 0, (rows, cols, tm, tn)
        rb, cb = sr // tm, sc // tn
        if row_sharded:
            def index_map(i, j, kk):
                return (j // cb, i // rb, i % rb, j % cb)
        else:
            def index_map(i, j, kk):
                return (i // rb, j // cb, i % rb, j % cb)
        return (N_CORES, N_CHIPS, sr, sc), (None, None, tm, tn), index_map

    return make, (sr, sc)


def _rms_rows(x):
    return lax.rsqrt(jnp.mean(x * x, axis=-1, keepdims=True) + RMS_EPS)


def _rmsnorm_fwd(x, w, *, name, width=None, col=0, out_dtype=BF16, tr=256):
    rows = x.shape[0]
    width = x.shape[1] if width is None else width
    tr = _tile(rows, tr, SUBLANES)

    def body(x_ref, w_ref, o_ref):
        xv = x_ref[...]
        o_ref[...] = (xv * _rms_rows(xv) * w_ref[...]).astype(o_ref.dtype)

    return pl.pallas_call(
        body, name=name, grid=(rows // tr,),
        in_specs=[pl.BlockSpec((tr, width), lambda i: (i, col)), pl.BlockSpec((1, width), lambda i: (0, 0))],
        out_specs=pl.BlockSpec((tr, width), lambda i: (i, 0)),
        out_shape=jax.ShapeDtypeStruct((rows, width), out_dtype),
        compiler_params=_params(("parallel",)),
    )(x, w)


def _rmsnorm_bwd_rows(xv, w, dy):
    r = _rms_rows(xv)
    n = xv * r
    dn = dy * w
    dx = r * (dn - n * jnp.mean(dn * n, axis=-1, keepdims=True))
    return dx, dy * n


def _rmsnorm_bwd(x, w, dy, *, name, width=None, col=0, dy_col=0, add=None, tr=256, dx_dtypes=(F32,)):
    rows = x.shape[0]
    n_dx = len(dx_dtypes)
    width = x.shape[1] if width is None else width
    tr = _tile(rows, tr, SUBLANES)
    in_specs = [pl.BlockSpec((tr, width), lambda i: (i, col)), pl.BlockSpec((1, width), lambda i: (0, 0)),
                pl.BlockSpec((tr, width), lambda i: (i, dy_col))]
    operands = [x, w, dy]
    if add is not None:
        in_specs.append(pl.BlockSpec((tr, width), lambda i: (i, 0)))
        operands.append(add)

    def body(*refs):
        x_ref, w_ref, dy_ref = refs[:3]
        add_ref = refs[3] if add is not None else None
        dx_refs, dw_ref = refs[-1 - n_dx:-1], refs[-1]
        dx, dwp = _rmsnorm_bwd_rows(x_ref[...], w_ref[...], dy_ref[...])
        if add_ref is not None:
            dx = dx + add_ref[...]
        for dx_ref in dx_refs:
            dx_ref[...] = dx.astype(dx_ref.dtype)
        part = jnp.sum(dwp, axis=0, keepdims=True)

        @pl.when(pl.program_id(0) == 0)
        def _():
            dw_ref[...] = part

        @pl.when(pl.program_id(0) > 0)
        def _():
            dw_ref[...] += part

    return pl.pallas_call(
        body, name=name, grid=(rows // tr,), in_specs=in_specs,
        out_specs=[pl.BlockSpec((tr, width), lambda i: (i, 0))] * n_dx + [pl.BlockSpec((1, width), lambda i: (0, 0))],
        out_shape=[jax.ShapeDtypeStruct((rows, width), dt) for dt in dx_dtypes] + [jax.ShapeDtypeStruct((1, width), F32)],
        compiler_params=_params(("arbitrary",)),
    )(*operands)


def _final_norm_loss(h, w, target, *, tr=256):
    rows, d = h.shape
    tr = _tile(rows, tr, SUBLANES)

    def body(h_ref, w_ref, t_ref, loss_ref, dh_ref, dhb_ref, dw_ref):
        hv, wv = h_ref[...], w_ref[...]
        r = _rms_rows(hv)
        n = hv * r
        err = n * wv - t_ref[...]
        d_out = err * (1.0 / d)
        dn = d_out * wv
        dh = r * (dn - n * jnp.mean(dn * n, axis=-1, keepdims=True))
        dh_ref[...] = dh
        dhb_ref[...] = dh.astype(BF16)
        dw_part = jnp.sum(d_out * n, axis=0, keepdims=True)
        loss_part = jnp.full((SUBLANES, LANES), 0.5 / d, F32) * jnp.sum(err * err)

        @pl.when(pl.program_id(0) == 0)
        def _():
            dw_ref[...] = dw_part
            loss_ref[...] = loss_part

        @pl.when(pl.program_id(0) > 0)
        def _():
            dw_ref[...] += dw_part
            loss_ref[...] += loss_part

    return pl.pallas_call(
        body, name="final_norm_loss", grid=(rows // tr,),
        in_specs=[pl.BlockSpec((tr, d), lambda i: (i, 0)), pl.BlockSpec((1, d), lambda i: (0, 0)),
                  pl.BlockSpec((tr, d), lambda i: (i, 0))],
        out_specs=[pl.BlockSpec((SUBLANES, LANES), lambda i: (0, 0)), pl.BlockSpec((tr, d), lambda i: (i, 0)),
                   pl.BlockSpec((tr, d), lambda i: (i, 0)), pl.BlockSpec((1, d), lambda i: (0, 0))],
        out_shape=[jax.ShapeDtypeStruct((SUBLANES, LANES), F32), jax.ShapeDtypeStruct((rows, d), F32),
                   jax.ShapeDtypeStruct((rows, d), BF16), jax.ShapeDtypeStruct((1, d), F32)],
        compiler_params=_params(("arbitrary",)),
    )(h, w, target)


def _cmul(ar, ai, br, bi):
    return ar * br - ai * bi, ar * bi + ai * br


def _expand_matrix(groups, reps):
    row = lax.broadcasted_iota(jnp.int32, (groups, groups * reps), 0)
    colg = lax.broadcasted_iota(jnp.int32, (groups, groups * reps), 1) // reps
    return (row == colg).astype(F32)


def _dot_exact(a, b, dims):
    return lax.dot_general(a, b, (dims, ((), ())), preferred_element_type=F32, precision=lax.Precision.HIGHEST)


def _s5_discretize(lr, li, dt):
    mag = jnp.exp(lr * dt)
    th = li * dt
    ar, ai = mag * jnp.cos(th), mag * jnp.sin(th)
    nr, ni = ar - 1.0, ai
    den = lr * lr + li * li
    zr = (nr * lr + ni * li) / den
    zi = (ni * lr - nr * li) / den
    return mag, ar, ai, nr, ni, den, zr, zi


def _s5_params(lam_re, lam_im, log_dt, b_re, b_im):
    g, p = lam_re.shape
    ph = b_re.shape[1]

    def body(lr_ref, li_ref, ldt_ref, br_ref, bi_ref, ar_ref, ai_ref, bbr_ref, bbi_ref):
        dt = jnp.exp(ldt_ref[...])
        _, ar, ai, _, _, _, zr, zi = _s5_discretize(lr_ref[...], li_ref[...], dt)
        ar_ref[...] = ar
        ai_ref[...] = ai
        e = _expand_matrix(p, ph // p)
        zr_x = _dot_exact(zr, e, ((1,), (0,)))
        zi_x = _dot_exact(zi, e, ((1,), (0,)))
        bre, bim = br_ref[...], bi_ref[...]
        bbr_ref[...] = zr_x * bre - zi_x * bim
        bbi_ref[...] = zr_x * bim + zi_x * bre

    return pl.pallas_call(
        body, name="s5_params",
        out_shape=[jax.ShapeDtypeStruct((g, p), F32)] * 2 + [jax.ShapeDtypeStruct((g, ph), F32)] * 2,
    )(lam_re, lam_im, log_dt, b_re, b_im)


def _s5_params_bwd(lam_re, lam_im, log_dt, b_re, b_im, d_ar, d_ai, d_bbr, d_bbi):
    g, p = lam_re.shape
    ph = b_re.shape[1]

    def body(lr_ref, li_ref, ldt_ref, br_ref, bi_ref, dar_ref, dai_ref, dbr_ref, dbi_ref,
             dlr_ref, dli_ref, dldt_ref, dbre_ref, dbim_ref):
        lr, li = lr_ref[...], li_ref[...]
        dt = jnp.exp(ldt_ref[...])
        mag, ar, ai, nr, ni, den, zr, zi = _s5_discretize(lr, li, dt)
        e = _expand_matrix(p, ph // p)
        zr_x = _dot_exact(zr, e, ((1,), (0,)))
        zi_x = _dot_exact(zi, e, ((1,), (0,)))
        bre, bim, dbr, dbi = br_ref[...], bi_ref[...], dbr_ref[...], dbi_ref[...]
        dbre_ref[...] = zr_x * dbr + zi_x * dbi
        dbim_ref[...] = zr_x * dbi - zi_x * dbr
        dzr = _dot_exact(bre * dbr + bim * dbi, e, ((1,), (1,)))
        dzi = _dot_exact(bre * dbi - bim * dbr, e, ((1,), (1,)))
        inv = 1.0 / den
        d_nr = (dzr * lr - dzi * li) * inv
        d_ni = (dzr * li + dzi * lr) * inv
        d_den = -(dzr * zr + dzi * zi) * inv
        d_lr = (dzr * nr + dzi * ni) * inv + 2.0 * lr * d_den
        d_li = (dzr * ni - dzi * nr) * inv + 2.0 * li * d_den
        t_ar = dar_ref[...] + d_nr
        t_ai = dai_ref[...] + d_ni
        d_lrdt = t_ar * ar + t_ai * ai
        d_th = t_ai * ar - t_ar * ai
        dlr_ref[...] = d_lr + d_lrdt * dt
        dli_ref[...] = d_li + d_th * dt
        dldt_ref[...] = jnp.sum(d_lrdt * lr + d_th * li, axis=1, keepdims=True) * dt

    return pl.pallas_call(
        body, name="s5_params_bwd",
        out_shape=[jax.ShapeDtypeStruct((g, p), F32)] * 2 + [jax.ShapeDtypeStruct((g, 1), F32)]
        + [jax.ShapeDtypeStruct((g, ph), F32)] * 2,
    )(lam_re, lam_im, log_dt, b_re, b_im, d_ar, d_ai, d_bbr, d_bbi)


def _powers(ar, ai, count):
    out = [(ar, ai)]
    for _ in range(count - 1):
        out.append(_cmul(out[-1][0], out[-1][1], ar, ai))
    return out


def _scan_coefs(ar, ai, reverse):
    w = ar.shape[-1]
    pw = _powers(ar, ai, SUBLANES)
    row = lax.broadcasted_iota(jnp.int32, (SUBLANES, w), 0)
    steps = []
    d = 1
    while d < SUBLANES:
        keep = (row < SUBLANES - d) if reverse else (row >= d)
        pr, pi = pw[d - 1]
        steps.append((d, jnp.where(keep, pr, 0.0), jnp.where(keep, pi, 0.0)))
        d *= 2
    cr = jnp.zeros((SUBLANES, w), F32)
    ci = jnp.zeros((SUBLANES, w), F32)
    for t in range(SUBLANES):
        pr, pi = pw[SUBLANES - 1 - t] if reverse else pw[t]
        cr = jnp.where(row == t, pr, cr)
        ci = jnp.where(row == t, pi, ci)
    return steps, cr, ci


def _scan_tile(xr, xi, carry_r, carry_i, coefs, reverse):
    steps, cr, ci = coefs
    for d, mr, mi in steps:
        shift = SUBLANES - d if reverse else d
        sr, si = pltpu.roll(xr, shift, 0), pltpu.roll(xi, shift, 0)
        pr, pi = _cmul(mr, mi, sr, si)
        xr, xi = xr + pr, xi + pi
    pr, pi = _cmul(cr, ci, carry_r, carry_i)
    return xr + pr, xi + pi


def _gelu(x):
    c = math.sqrt(2.0 / math.pi)
    return 0.5 * x * (1.0 + jnp.tanh(c * (x + 0.044715 * x * x * x)))


def _gelu_grad(x):
    c = math.sqrt(2.0 / math.pi)
    t = jnp.tanh(c * (x + 0.044715 * x * x * x))
    return 0.5 * (1.0 + t) + 0.5 * x * (1.0 - t * t) * c * (1.0 + 3.0 * 0.044715 * x * x)


def _s5_fwd(proj, wb, wc, d_skip, abar):
    rows = proj.shape[0]
    nb = wb.shape[0]
    s2 = 2 * STATE_PER_BATCH
    st = STATE_PER_BATCH
    chunk = _tile(rows, 512, SUBLANES)

    def body(u_ref, wb_ref, wc_ref, d_ref, a_ref, s_ref, y_ref, yg_ref):
        for c0 in range(0, rows, chunk):
            s_ref[pl.ds(c0, chunk), :] = _dot_nn(u_ref[pl.ds(c0, chunk), :].astype(BF16), wb_ref[...])
        av = a_ref[...]
        coefs = _scan_coefs(av[:, :st], av[:, st:], reverse=False)

        def tile(b, carry):
            r0 = pl.multiple_of(b * SUBLANES, SUBLANES)
            xr, xi = _scan_tile(s_ref[pl.ds(r0, SUBLANES), :st], s_ref[pl.ds(r0, SUBLANES), st:], carry[0], carry[1],
                                coefs, False)
            s_ref[pl.ds(r0, SUBLANES), :st] = xr
            s_ref[pl.ds(r0, SUBLANES), st:] = xi
            return xr[SUBLANES - 1:, :], xi[SUBLANES - 1:, :]

        zero = jnp.zeros((1, st), F32)
        lax.fori_loop(0, rows // SUBLANES, tile, (zero, zero))
        for c0 in range(0, rows, chunk):
            y = _dot_nn(s_ref[pl.ds(c0, chunk), :].astype(BF16), wc_ref[...]) + d_ref[...] * u_ref[pl.ds(c0, chunk), :]
            y_ref[pl.ds(c0, chunk), :] = y
            yg_ref[pl.ds(c0, chunk), :] = _gelu(y).astype(BF16)

    return pl.pallas_call(
        body, name="s5_fwd", grid=(nb,),
        in_specs=[pl.BlockSpec((rows, LANES), lambda j: (0, j)), pl.BlockSpec((None, LANES, s2), lambda j: (j, 0, 0)),
                  pl.BlockSpec((None, s2, LANES), lambda j: (j, 0, 0)), pl.BlockSpec((1, LANES), lambda j: (0, j)),
                  pl.BlockSpec((None, 1, s2), lambda j: (j, 0, 0))],
        out_specs=[pl.BlockSpec((rows, s2), lambda j: (0, j)), pl.BlockSpec((rows, LANES), lambda j: (0, j)),
                   pl.BlockSpec((rows, LANES), lambda j: (0, j))],
        out_shape=[jax.ShapeDtypeStruct((rows, nb * s2), F32), jax.ShapeDtypeStruct((rows, nb * LANES), F32),
                   jax.ShapeDtypeStruct((rows, nb * LANES), BF16)],
        compiler_params=_params(("parallel",)),
    )(proj, wb, wc, d_skip, abar)


def _s5_bwd(proj, states, y_pre, dyg_a, dyg_b, wb, wc, d_skip, abar):
    rows = proj.shape[0]
    nb = wb.shape[0]
    s2 = 2 * STATE_PER_BATCH
    st = STATE_PER_BATCH
    chunk = _tile(rows, 512, SUBLANES)
    n_tiles = rows // SUBLANES

    def body(u_ref, s_ref, y_ref, ga_ref, gb_ref, wb_ref, wc_ref, d_ref, a_ref,
             du_ref, dwb_ref, dwc_ref, da_ref, dd_ref, ds_ref, dy_ref):
        dy_ref[...] = (ga_ref[...] + gb_ref[...]) * _gelu_grad(y_ref[...])
        dd_ref[...] = jnp.sum(dy_ref[...] * u_ref[...], axis=0, keepdims=True)
        for c0 in range(0, rows, chunk):
            ds_ref[pl.ds(c0, chunk), :] = _dot_nt(dy_ref[pl.ds(c0, chunk), :].astype(BF16), wc_ref[...])
        dwc_ref[...] = _dot_tn(s_ref[...].astype(BF16), dy_ref[...].astype(BF16))
        av = a_ref[...]
        coefs = _scan_coefs(av[:, :st], -av[:, st:], reverse=True)
        row = lax.broadcasted_iota(jnp.int32, (SUBLANES, st), 0)

        def tile(k, carry):
            cr, ci, acc_r, acc_i = carry
            b = n_tiles - 1 - k
            r0 = pl.multiple_of(b * SUBLANES, SUBLANES)
            rp = pl.multiple_of(jnp.maximum(b - 1, 0) * SUBLANES, SUBLANES)
            xr, xi = _scan_tile(ds_ref[pl.ds(r0, SUBLANES), :st], ds_ref[pl.ds(r0, SUBLANES), st:], cr, ci, coefs, True)
            ds_ref[pl.ds(r0, SUBLANES), :st] = xr
            ds_ref[pl.ds(r0, SUBLANES), st:] = xi
            first = jnp.where(b > 0, 1.0, 0.0)
            pr = jnp.where(row == 0, pltpu.roll(s_ref[pl.ds(rp, SUBLANES), :st], 1, 0) * first,
                           pltpu.roll(s_ref[pl.ds(r0, SUBLANES), :st], 1, 0))
            pi = jnp.where(row == 0, pltpu.roll(s_ref[pl.ds(rp, SUBLANES), st:], 1, 0) * first,
                           pltpu.roll(s_ref[pl.ds(r0, SUBLANES), st:], 1, 0))
            acc_r = acc_r + pr * xr + pi * xi
            acc_i = acc_i + pr * xi - pi * xr
            return xr[:1, :], xi[:1, :], acc_r, acc_i

        zero = jnp.zeros((1, st), F32)
        zacc = jnp.zeros((SUBLANES, st), F32)
        _, _, acc_r, acc_i = lax.fori_loop(0, n_tiles, tile, (zero, zero, zacc, zacc))
        da_ref[:, :st] = jnp.sum(acc_r, axis=0, keepdims=True)
        da_ref[:, st:] = jnp.sum(acc_i, axis=0, keepdims=True)
        for c0 in range(0, rows, chunk):
            du_ref[pl.ds(c0, chunk), :] = (_dot_nt(ds_ref[pl.ds(c0, chunk), :].astype(BF16), wb_ref[...])
                                           + d_ref[...] * dy_ref[pl.ds(c0, chunk), :]).astype(du_ref.dtype)
        dwb_ref[...] = _dot_tn(u_ref[...].astype(BF16), ds_ref[...].astype(BF16))

    col = pl.BlockSpec((rows, LANES), lambda j: (0, j))
    return pl.pallas_call(
        body, name="s5_bwd", grid=(nb,),
        in_specs=[col, pl.BlockSpec((rows, s2), lambda j: (0, j)), col, col, col,
                  pl.BlockSpec((None, LANES, s2), lambda j: (j, 0, 0)), pl.BlockSpec((None, s2, LANES), lambda j: (j, 0, 0)),
                  pl.BlockSpec((1, LANES), lambda j: (0, j)), pl.BlockSpec((None, 1, s2), lambda j: (j, 0, 0))],
        out_specs=[col, pl.BlockSpec((None, LANES, s2), lambda j: (j, 0, 0)),
                   pl.BlockSpec((None, s2, LANES), lambda j: (j, 0, 0)), pl.BlockSpec((None, 1, s2), lambda j: (j, 0, 0)),
                   pl.BlockSpec((1, LANES), lambda j: (0, j))],
        out_shape=[jax.ShapeDtypeStruct((rows, nb * LANES), BF16), jax.ShapeDtypeStruct((nb, LANES, s2), F32),
                   jax.ShapeDtypeStruct((nb, s2, LANES), F32), jax.ShapeDtypeStruct((nb, 1, s2), F32),
                   jax.ShapeDtypeStruct((1, nb * LANES), F32)],
        scratch_shapes=[pltpu.VMEM((rows, s2), F32), pltpu.VMEM((rows, LANES), F32)],
        compiler_params=_params(("parallel",)),
    )(proj, states, y_pre, dyg_a, dyg_b, wb, wc, d_skip, abar)


def _glu_norm_fwd(y_pre, z, w, *, tr=256):
    rows, width = y_pre.shape
    tr = _tile(rows, tr, SUBLANES)

    def body(y_ref, z_ref, w_ref, o_ref):
        v = _gelu(y_ref[...]) * jax.nn.sigmoid(z_ref[...])
        o_ref[...] = (v * _rms_rows(v) * w_ref[...]).astype(o_ref.dtype)

    blk = pl.BlockSpec((tr, width), lambda i: (i, 0))
    return pl.pallas_call(
        body, name="glu_norm_fwd", grid=(rows // tr,),
        in_specs=[blk, blk, pl.BlockSpec((1, width), lambda i: (0, 0))], out_specs=blk,
        out_shape=jax.ShapeDtypeStruct((rows, width), BF16), compiler_params=_params(("parallel",)),
    )(y_pre, z, w)


def _glu_norm_bwd(y_pre, z, w, dycat, *, tr=256):
    rows, width = y_pre.shape
    tr = _tile(rows, tr, SUBLANES)

    def body(y_ref, z_ref, w_ref, dy_ref, dz_ref, dg_ref, dw_ref, db_ref):
        yg = _gelu(y_ref[...])
        sg = jax.nn.sigmoid(z_ref[...])
        dv, dwp = _rmsnorm_bwd_rows(yg * sg, w_ref[...], dy_ref[...])
        dz = dv * yg * sg * (1.0 - sg)
        dz_ref[...] = dz.astype(dz_ref.dtype)
        dg_ref[...] = dv * sg
        dw_part = jnp.sum(dwp, axis=0, keepdims=True)
        db_part = jnp.sum(dz, axis=0, keepdims=True)

        @pl.when(pl.program_id(0) == 0)
        def _():
            dw_ref[...] = dw_part
            db_ref[...] = db_part

        @pl.when(pl.program_id(0) > 0)
        def _():
            dw_ref[...] += dw_part
            db_ref[...] += db_part

    blk = pl.BlockSpec((tr, width), lambda i: (i, 0))
    vec = pl.BlockSpec((1, width), lambda i: (0, 0))
    return pl.pallas_call(
        body, name="glu_norm_bwd", grid=(rows // tr,), in_specs=[blk, blk, vec, blk], out_specs=[blk, blk, vec, vec],
        out_shape=[jax.ShapeDtypeStruct((rows, width), BF16), jax.ShapeDtypeStruct((rows, width), F32)]
        + [jax.ShapeDtypeStruct((1, width), F32)] * 2,
        compiler_params=_params(("arbitrary",)),
    )(y_pre, z, w, dycat)


def _rope_tables(pos, freq, sign):
    rows = pos.shape[0]

    def body(p_ref, f_ref, s_ref, cos_ref, sin_ref):
        ang = p_ref[...] * f_ref[...]
        cos_ref[...] = jnp.cos(ang)
        sin_ref[...] = jnp.sin(ang) * s_ref[...]

    return pl.pallas_call(body, name="rope_tables", out_shape=[jax.ShapeDtypeStruct((rows, LANES), F32)] * 2)(pos, freq, sign)


def _rope(x, cos, sin_signed):
    lane = lax.broadcasted_iota(jnp.int32, x.shape, 1)
    half = QK_ROPE_DIM // 2
    swapped = jnp.where(lane < half, pltpu.roll(x, LANES - half, 1), pltpu.roll(x, half, 1))
    return x * cos + swapped * sin_signed


def _attn_prep(q, kv, proj, kpe_col, cos, sin, *, tr=256):
    rows = q.shape[0]
    heads = q.shape[1] // HEAD_SLOT
    tr = _tile(rows, tr, SUBLANES)

    def body(q_ref, kv_ref, kpe_ref, cos_ref, sin_ref, qc_ref, kc_ref, v_ref):
        c, s = cos_ref[...], sin_ref[...]
        qc_ref[:, :LANES] = q_ref[:, :LANES].astype(BF16)
        qc_ref[:, LANES:] = _rope(q_ref[:, LANES:], c, s).astype(BF16)
        kc_ref[:, :LANES] = kv_ref[:, :LANES].astype(BF16)
        kc_ref[:, LANES:] = _rope(kpe_ref[...], c, s).astype(BF16)
        v_ref[...] = kv_ref[:, LANES:].astype(BF16)

    slot = pl.BlockSpec((tr, HEAD_SLOT), lambda i, h: (i, h))
    tab = pl.BlockSpec((tr, LANES), lambda i, h: (i, 0))
    return pl.pallas_call(
        body, name="attn_prep", grid=(rows // tr, heads),
        in_specs=[slot, slot, pl.BlockSpec((tr, LANES), lambda i, h: (i, kpe_col)), tab, tab],
        out_specs=[slot, slot, pl.BlockSpec((tr, LANES), lambda i, h: (i, h))],
        out_shape=[jax.ShapeDtypeStruct((rows, heads * HEAD_SLOT), BF16)] * 2
        + [jax.ShapeDtypeStruct((rows, heads * LANES), BF16)],
        compiler_params=_params(("parallel", "parallel")),
    )(q, kv, proj, cos, sin)


def _causal(tq, tk):
    return lax.broadcasted_iota(jnp.int32, (tq, tk), 1) <= lax.broadcasted_iota(jnp.int32, (tq, tk), 0)


def _attn_fwd(qc, kc, vb, *, scale, tq=512):
    rows = qc.shape[0]
    heads = qc.shape[1] // HEAD_SLOT
    tq = _tile(rows, tq, SUBLANES)
    tk = tq

    def body(q_ref, k_ref, v_ref, o_ref, lse_ref):
        i = pl.program_id(1)
        q = q_ref[...]

        def step(j, carry, diagonal):
            m, l, acc = carry
            k0 = pl.multiple_of(j * tk, tk)
            s = _dot_nt(q, k_ref[pl.ds(k0, tk), :]) * scale
            if diagonal:
                s = jnp.where(_causal(tq, tk), s, NEG_INF)
            m_new = jnp.maximum(m, jnp.max(s, axis=-1, keepdims=True))
            p = jnp.exp(s - m_new)
            alpha = jnp.exp(m - m_new)
            l = alpha * l + jnp.sum(p, axis=-1, keepdims=True)
            acc = alpha * acc + _dot_nn(p.astype(BF16), v_ref[pl.ds(k0, tk), :])
            return m_new, l, acc

        init = (jnp.full((tq, 1), NEG_INF, F32), jnp.zeros((tq, 1), F32), jnp.zeros((tq, LANES), F32))
        below = lax.fori_loop(0, i, lambda j, carry: step(j, carry, False), init)
        m, l, acc = step(i, below, True)
        o_ref[...] = acc / l
        lse_ref[...] = jnp.broadcast_to(m + jnp.log(l), (tq, LANES))

    return pl.pallas_call(
        body, name="attn_fwd", grid=(heads, rows // tq),
        in_specs=[pl.BlockSpec((tq, HEAD_SLOT), lambda h, i: (i, h)), pl.BlockSpec((rows, HEAD_SLOT), lambda h, i: (0, h)),
                  pl.BlockSpec((rows, LANES), lambda h, i: (0, h))],
        out_specs=[pl.BlockSpec((tq, LANES), lambda h, i: (i, h))] * 2,
        out_shape=[jax.ShapeDtypeStruct((rows, heads * LANES), F32)] * 2,
        compiler_params=_params(("parallel", "parallel")),
    )(qc, kc, vb)


def _attn_bwd(qc, kc, vb, o, do, lse, cos, sin, *, scale, tk=512):
    rows = qc.shape[0]
    heads = qc.shape[1] // HEAD_SLOT
    tk = _tile(rows, tk, SUBLANES)
    tq = tk
    nq = rows // tq

    def body(q_ref, k_ref, v_ref, o_ref, do_ref, lse_ref, cos_ref, sin_ref, dq_ref, dkv_ref, dkpe_ref, dq_acc, delta_ref):
        j = pl.program_id(1)

        @pl.when(j == 0)
        def _():
            dq_acc[...] = jnp.zeros_like(dq_acc)
            for r0 in range(0, rows, tq):
                d = jnp.sum(do_ref[pl.ds(r0, tq), :] * o_ref[pl.ds(r0, tq), :], axis=-1, keepdims=True)
                delta_ref[pl.ds(r0, tq), :] = jnp.broadcast_to(d, (tq, LANES))

        kb, vv = k_ref[...], v_ref[...]

        def step(i, carry, diagonal):
            dk, dv = carry
            q0 = pl.multiple_of(i * tq, tq)
            qb = q_ref[pl.ds(q0, tq), :]
            dob = do_ref[pl.ds(q0, tq), :].astype(BF16)
            s = _dot_nt(qb, kb) * scale
            p = jnp.exp(s - lse_ref[pl.ds(q0, tq), :1])
            if diagonal:
                p = jnp.where(_causal(tq, tk), p, 0.0)
            dv = dv + _dot_tn(p.astype(BF16), dob)
            ds = (p * (_dot_nt(dob, vv) - delta_ref[pl.ds(q0, tq), :1])).astype(BF16)
            dk = dk + _dot_tn(ds, qb)
            dq_acc[pl.ds(q0, tq), :] += _dot_nn(ds, kb)
            return dk, dv

        zero = (jnp.zeros((tk, HEAD_SLOT), F32), jnp.zeros((tk, LANES), F32))
        dk, dv = lax.fori_loop(j + 1, nq, lambda i, carry: step(i, carry, False), step(j, zero, True))
        dkv_ref[:, :LANES] = (dk[:, :LANES] * scale).astype(dkv_ref.dtype)
        dkv_ref[:, LANES:] = dv.astype(dkv_ref.dtype)
        dkpe_ref[...] = dk[:, LANES:] * scale

        @pl.when(j == nq - 1)
        def _():
            for r0 in range(0, rows, tq):
                dq = dq_acc[pl.ds(r0, tq), :] * scale
                dq_ref[pl.ds(r0, tq), :LANES] = dq[:, :LANES].astype(dq_ref.dtype)
                dq_ref[pl.ds(r0, tq), LANES:] = _rope(dq[:, LANES:], cos_ref[pl.ds(r0, tq), :],
                                                      -sin_ref[pl.ds(r0, tq), :]).astype(dq_ref.dtype)

    full_q = pl.BlockSpec((rows, HEAD_SLOT), lambda h, j: (0, h))
    full_v = pl.BlockSpec((rows, LANES), lambda h, j: (0, h))
    tab = pl.BlockSpec((rows, LANES), lambda h, j: (0, 0))
    return pl.pallas_call(
        body, name="attn_bwd", grid=(heads, rows // tk),
        in_specs=[full_q, pl.BlockSpec((tk, HEAD_SLOT), lambda h, j: (j, h)), pl.BlockSpec((tk, LANES), lambda h, j: (j, h)),
                  full_v, full_v, full_v, tab, tab],
        out_specs=[full_q, pl.BlockSpec((tk, HEAD_SLOT), lambda h, j: (j, h)), pl.BlockSpec((tk, LANES), lambda h, j: (j, h))],
        out_shape=[jax.ShapeDtypeStruct((rows, heads * HEAD_SLOT), BF16), jax.ShapeDtypeStruct((rows, heads * HEAD_SLOT), BF16),
                   jax.ShapeDtypeStruct((rows, heads * LANES), F32)],
        scratch_shapes=[pltpu.VMEM((rows, HEAD_SLOT), F32), pltpu.VMEM((rows, LANES), F32)],
        compiler_params=_params(("parallel", "arbitrary")),
    )(qc, kc, vb, o, do, lse, cos, sin)


def _kpe_bwd(dkpe_heads, cos, sin, *, tr=512):
    rows = dkpe_heads.shape[0]
    heads = dkpe_heads.shape[1] // LANES
    tr = _tile(rows, tr, 2 * SUBLANES)

    def body(d_ref, cos_ref, sin_ref, o_ref):
        acc = d_ref[:, :LANES]
        for h in range(1, heads):
            acc = acc + d_ref[:, h * LANES:(h + 1) * LANES]
        o_ref[...] = _rope(acc, cos_ref[...], -sin_ref[...]).astype(o_ref.dtype)

    tab = pl.BlockSpec((tr, LANES), lambda i: (i, 0))
    return pl.pallas_call(
        body, name="kpe_bwd", grid=(rows // tr,),
        in_specs=[pl.BlockSpec((tr, heads * LANES), lambda i: (i, 0)), tab, tab], out_specs=tab,
        out_shape=jax.ShapeDtypeStruct((rows, LANES), BF16), compiler_params=_params(("parallel",)),
    )(dkpe_heads, cos, sin)


def _shift_down(x, d):
    row = lax.broadcasted_iota(jnp.int32, x.shape, 0)
    return jnp.where(row >= d, pltpu.roll(x, d, 0), 0.0)


def _shift_up(x, d):
    rows = x.shape[0]
    row = lax.broadcasted_iota(jnp.int32, x.shape, 0)
    return jnp.where(row < rows - d, pltpu.roll(x, rows - d, 0), 0.0)


def _conv3(a, w, b):
    return w[2:3, :] * a + w[1:2, :] * _shift_down(a, 1) + w[0:1, :] * _shift_down(a, 2) + b


def _conv_gate_fwd(a, conv_w, conv_b, *, tc=256):
    rows, f2 = a.shape
    f = f2 // 2
    tc = _tile(f, tc)
    nc = f // tc

    def body(ag_ref, av_ref, wg_ref, wv_ref, bg_ref, bv_ref, o_ref):
        gate = _conv3(ag_ref[...], wg_ref[...], bg_ref[...])
        val = _conv3(av_ref[...], wv_ref[...], bv_ref[...])
        o_ref[...] = (gate * jax.nn.sigmoid(gate) * val).astype(o_ref.dtype)

    return pl.pallas_call(
        body, name="conv_gate_fwd", grid=(nc,),
        in_specs=[pl.BlockSpec((rows, tc), lambda j: (0, j)), pl.BlockSpec((rows, tc), lambda j: (0, j + nc)),
                  pl.BlockSpec((SUBLANES, tc), lambda j: (0, j)), pl.BlockSpec((SUBLANES, tc), lambda j: (0, j + nc)),
                  pl.BlockSpec((1, tc), lambda j: (0, j)), pl.BlockSpec((1, tc), lambda j: (0, j + nc))],
        out_specs=pl.BlockSpec((rows, tc), lambda j: (0, j)),
        out_shape=jax.ShapeDtypeStruct((rows, f), BF16), compiler_params=_params(("parallel",)),
    )(a, a, conv_w, conv_w, conv_b, conv_b)


def _conv_gate_bwd(a, conv_w, conv_b, dg, *, tc=256):
    rows, f2 = a.shape
    f = f2 // 2
    tc = _tile(f, tc)
    nc = f // tc

    def conv_bwd(a_val, w, d_out):
        da = w[2:3, :] * d_out + w[1:2, :] * _shift_up(d_out, 1) + w[0:1, :] * _shift_up(d_out, 2)
        db = jnp.sum(d_out, axis=0, keepdims=True)
        row = lax.broadcasted_iota(jnp.int32, (SUBLANES, a_val.shape[1]), 0)
        dw = jnp.zeros((SUBLANES, a_val.shape[1]), F32)
        for tap in range(3):
            t = jnp.sum(d_out * (_shift_down(a_val, 2 - tap) if tap < 2 else a_val), axis=0, keepdims=True)
            dw = jnp.where(row == tap, t, dw)
        return da, dw, db

    def body(ag_ref, av_ref, wg_ref, wv_ref, bg_ref, bv_ref, dg_ref, da_ref, dw_ref, db_ref):
        ag, av, wg, wv = ag_ref[...], av_ref[...], wg_ref[...], wv_ref[...]
        gate = _conv3(ag, wg, bg_ref[...])
        val = _conv3(av, wv, bv_ref[...])
        sg = jax.nn.sigmoid(gate)
        dgv = dg_ref[...]
        d_gate = dgv * val * sg * (1.0 + gate * (1.0 - sg))
        d_val = dgv * gate * sg
        for half, (a_val, w, d_out) in enumerate(((ag, wg, d_gate), (av, wv, d_val))):
            da, dw, db = conv_bwd(a_val, w, d_out)
            da_ref[half] = da.astype(da_ref.dtype)
            dw_ref[half] = dw
            db_ref[half] = db

    lo = lambda j: (0, j)
    hi = lambda j: (0, j + nc)
    both = lambda j: (0, 0, j)
    return pl.pallas_call(
        body, name="conv_gate_bwd", grid=(nc,),
        in_specs=[pl.BlockSpec((rows, tc), lo), pl.BlockSpec((rows, tc), hi), pl.BlockSpec((SUBLANES, tc), lo),
                  pl.BlockSpec((SUBLANES, tc), hi), pl.BlockSpec((1, tc), lo), pl.BlockSpec((1, tc), hi),
                  pl.BlockSpec((rows, tc), lo)],
        out_specs=[pl.BlockSpec((2, rows, tc), both), pl.BlockSpec((2, SUBLANES, tc), both), pl.BlockSpec((2, 1, tc), both)],
        out_shape=[jax.ShapeDtypeStruct((2, rows, f), BF16), jax.ShapeDtypeStruct((2, SUBLANES, f), F32),
                   jax.ShapeDtypeStruct((2, 1, f), F32)],
        compiler_params=_params(("parallel",)),
    )(a, a, conv_w, conv_w, conv_b, conv_b, dg)


def _wgrad(a, b, rows, cols, row_sharded, name, **kw):
    make, (sr, sc) = _wgrad_blocks(rows, cols, row_sharded)
    tm = kw.pop("tm", _tile(sr, 512))
    tn = kw.pop("tn", _tile(sc, 1024))
    return _matmul(a, b, mode="tn", name=name, tm=tm, tn=tn, out_blocks=make, **kw)


def _block_diag(x):
    nb, g, r, c = x.shape
    eye = jnp.eye(g, dtype=x.dtype)
    return (x[:, :, :, None, :] * eye[None, :, None, :, None]).reshape(nb, g * r, g * c)


def _block_diag_part(x, r, c):
    nb = x.shape[0]
    g = GROUPS_PER_BATCH
    eye = jnp.eye(g, dtype=x.dtype)
    return jnp.sum(x.reshape(nb, g, r, g, c) * eye[None, :, None, :, None], axis=3)


class _NoExchange:
    def __init__(self, ffn):
        self.ffn = ffn

    def ffn_weights_arrived(self, after):
        return None

    def ffn_weights(self, after):
        return self.ffn

    def ffn_grads(self, g_down, g_up, after):
        return None

    def ffn_backward_done(self, after):
        return None


def _local_step(x, posf, target, w, hooks):
    rows, d = x.shape
    width = w["ssm_d"].shape[1]
    qr, kvr = w["mla_q_norm_w"].shape[1], w["mla_kv_norm_w"].shape[1]
    heads = w["mla_w_ukv"].shape[1] // HEAD_SLOT
    f2 = w["ffn_conv_b"].shape[1]
    inp = w["w_in"].shape[0]
    groups = width // SSM_GROUP
    nb = groups // GROUPS_PER_BATCH
    scale = (QK_NOPE_DIM + QK_ROPE_DIM) ** -0.5
    g = {}

    hn = _rmsnorm_fwd(x, w["attn_norm_w"], name="attn_norm")
    proj = _matmul(hn, w["w_in"], mode="nt", name="in_proj")

    ar, ai, bbr, bbi = _s5_params(w["ssm_lambda_re"], w["ssm_lambda_im"], w["ssm_log_dt"], w["ssm_b_re"], w["ssm_b_im"])

    def b_band(bb):
        return _block_diag(bb.reshape(nb, GROUPS_PER_BATCH, SSM_STATE, SSM_GROUP).transpose(0, 1, 3, 2))

    def c_band(c):
        return _block_diag(c.reshape(nb, GROUPS_PER_BATCH, SSM_GROUP, SSM_STATE).transpose(0, 1, 3, 2))

    wb = jnp.concatenate([b_band(bbr), b_band(bbi)], axis=2).astype(BF16)
    wc = jnp.concatenate([c_band(w["ssm_c_re"]), -c_band(w["ssm_c_im"])], axis=1).astype(BF16)
    abar = jnp.concatenate([ar.reshape(nb, 1, STATE_PER_BATCH), ai.reshape(nb, 1, STATE_PER_BATCH)], axis=2)
    states, y_pre, yg = _s5_fwd(proj, wb, wc, w["ssm_d"], abar)
    z = _matmul(yg, w["ssm_w_glu"], mode="nn", name="glu_proj", bias=w["ssm_b_glu"])
    ys = _glu_norm_fwd(y_pre, z, w["ssm_out_norm_w"])

    q_col, kv_col, kpe_col = width // qr, (width + qr) // kvr, (width + qr + kvr) // LANES
    assert width % qr == 0 and (width + qr) % kvr == 0
    qn = _rmsnorm_fwd(proj, w["mla_q_norm_w"], name="q_norm", width=qr, col=q_col)
    kvn = _rmsnorm_fwd(proj, w["mla_kv_norm_w"], name="kv_norm", width=kvr, col=kv_col)
    q = _matmul(qn, w["mla_w_uq"], mode="nn", name="q_proj")
    kv = _matmul(kvn, w["mla_w_ukv"], mode="nn", name="kv_proj")
    half = QK_ROPE_DIM // 2
    inv_freq = ROPE_THETA ** (-jnp.arange(0, QK_ROPE_DIM, 2, dtype=F32) / QK_ROPE_DIM)
    zeros = jnp.zeros((LANES - QK_ROPE_DIM,), F32)
    freq = jnp.concatenate([inv_freq, inv_freq, zeros]).reshape(1, LANES)
    sign = jnp.concatenate([-jnp.ones((half,), F32), jnp.ones((half,), F32), zeros]).reshape(1, LANES)
    cos, sin = _rope_tables(posf, freq, sign)
    qc, kc, vb = _attn_prep(q, kv, proj, kpe_col, cos, sin)
    o, lse = _attn_fwd(qc, kc, vb, scale=scale, tq=ATTN_BLOCK)
    ym = _rmsnorm_fwd(o, w["mla_out_norm_w"], name="mla_out_norm")
    ycat = jnp.concatenate([ys, ym], axis=1)
    h1 = _matmul(ycat, w["w_out"], mode="nn", name="out_proj", add=x, after=hooks.ffn_weights_arrived(ycat))

    hn2 = _rmsnorm_fwd(h1, w["ffn_norm_w"], name="ffn_norm")
    ffn = hooks.ffn_weights(hn2)
    a = _matmul(hn2, ffn["ffn_w_up"], mode="nn", name="ffn_up", tm=FFN_ROWS)
    gated = _conv_gate_fwd(a, ffn["ffn_conv_w"], w["ffn_conv_b"])
    h2 = _matmul(gated, ffn["ffn_w_down"], mode="nn", name="ffn_down", add=h1, tk=2816, tm=FFN_ROWS)
    loss_tile, dh2, dh2_mxu, g["final_norm_w"] = _final_norm_loss(h2, w["final_norm_w"], target)

    dgated = _matmul(dh2_mxu, ffn["ffn_w_down"], mode="nt", name="ffn_down_dx", tm=FFN_ROWS)
    g["ffn_w_down"] = _wgrad(gated, dh2_mxu, f2 // 2, d, True, "ffn_down_dw", tm=f2 // 2 // N_CHIPS, tn=1024)
    da, dcw, dcb = _conv_gate_bwd(a, ffn["ffn_conv_w"], w["ffn_conv_b"], dgated)
    g["ffn_conv_w"] = jnp.concatenate([dcw[0, :3], dcw[1, :3]], axis=1)
    g["ffn_conv_b"] = jnp.concatenate([dcb[0], dcb[1]], axis=1)
    g["ffn_w_up"] = _wgrad(hn2, da, d, f2, False, "ffn_up_dw", b_split=True, tm=FFN_ROWS, tn=_tile(f2 // N_CHIPS, 1408))
    started = hooks.ffn_grads(g["ffn_w_down"], g["ffn_w_up"], dcb)
    dhn2 = _matmul(da, ffn["ffn_w_up"], mode="nt", name="ffn_up_dx", a_split=True, tk=_tile(f2 // 2, 2816), tm=FFN_ROWS,
                   after=started)
    dh1, dh1_mxu, g["ffn_norm_w"] = _rmsnorm_bwd(h1, w["ffn_norm_w"], dhn2, name="ffn_norm_bwd", add=dh2,
                                                dx_dtypes=(F32, BF16))

    dycat = _matmul(dh1_mxu, w["w_out"], mode="nt", name="out_proj_dx")
    g["w_out"] = _wgrad(ycat, dh1_mxu, 2 * width, d, True, "out_proj_dw")
    started = hooks.ffn_backward_done(dycat)
    mla_out_norm_w, ssm_out_norm_w = w["mla_out_norm_w"], w["ssm_out_norm_w"]
    if started is not None:
        mla_out_norm_w, ssm_out_norm_w = mla_out_norm_w + started[:1, :1], ssm_out_norm_w + started[:1, :1]

    do, g["mla_out_norm_w"] = _rmsnorm_bwd(o, mla_out_norm_w, dycat, name="mla_out_norm_bwd", width=width, dy_col=1)
    dq, dkv, dkpe_heads = _attn_bwd(qc, kc, vb, o, do, lse, cos, sin, scale=scale, tk=ATTN_BLOCK)
    dkpe = _kpe_bwd(dkpe_heads, cos, sin)
    g["mla_w_uq"] = _wgrad(qn, dq, qr, heads * HEAD_SLOT, False, "q_proj_dw")
    dqn = _matmul(dq, w["mla_w_uq"], mode="nt", name="q_proj_dx")
    dcq, g["mla_q_norm_w"] = _rmsnorm_bwd(proj, w["mla_q_norm_w"], dqn, name="q_norm_bwd", width=qr, col=q_col,
                                          dx_dtypes=(BF16,))
    g["mla_w_ukv"] = _wgrad(kvn, dkv, kvr, heads * HEAD_SLOT, False, "kv_proj_dw")
    dkvn = _matmul(dkv, w["mla_w_ukv"], mode="nt", name="kv_proj_dx")
    dckv, g["mla_kv_norm_w"] = _rmsnorm_bwd(proj, w["mla_kv_norm_w"], dkvn, name="kv_norm_bwd", width=kvr, col=kv_col,
                                            dx_dtypes=(BF16,))

    dz, dyg_a, g["ssm_out_norm_w"], g["ssm_b_glu"] = _glu_norm_bwd(y_pre, z, ssm_out_norm_w, dycat)
    dyg_b = _matmul(dz, w["ssm_w_glu"], mode="nt", name="glu_proj_dx")
    g["ssm_w_glu"] = _wgrad(yg, dz, width, width, True, "glu_proj_dw")
    du, dwb, dwc, dabar, g["ssm_d"] = _s5_bwd(proj, states, y_pre, dyg_a, dyg_b, wb, wc, w["ssm_d"], abar)

    def b_unband(x):
        return _block_diag_part(x, SSM_GROUP, SSM_STATE).transpose(0, 1, 3, 2).reshape(groups, SSM_STATE * SSM_GROUP)

    def c_unband(x):
        return _block_diag_part(x, SSM_STATE, SSM_GROUP).transpose(0, 1, 3, 2).reshape(groups, SSM_GROUP, SSM_STATE)

    st = STATE_PER_BATCH
    g["ssm_c_re"] = c_unband(dwc[:, :st, :])
    g["ssm_c_im"] = -c_unband(dwc[:, st:, :])
    d_ar = dabar[:, 0, :st].reshape(groups, SSM_STATE)
    d_ai = dabar[:, 0, st:].reshape(groups, SSM_STATE)
    (g["ssm_lambda_re"], g["ssm_lambda_im"], g["ssm_log_dt"], g["ssm_b_re"], g["ssm_b_im"]) = _s5_params_bwd(
        w["ssm_lambda_re"], w["ssm_lambda_im"], w["ssm_log_dt"], w["ssm_b_re"], w["ssm_b_im"], d_ar, d_ai,
        b_unband(dwb[:, :, :st]), b_unband(dwb[:, :, st:]))

    pad = jnp.zeros((rows, inp - (width + qr + kvr + LANES)), BF16)
    dproj = jnp.concatenate([du, dcq, dckv, dkpe, pad], axis=1)
    g["w_in"] = _wgrad(dproj, hn, inp, d, False, "in_proj_dw")
    dhn = _matmul(dproj, w["w_in"], mode="nn", name="in_proj_dx")
    dx, g["attn_norm_w"] = _rmsnorm_bwd(x, w["attn_norm_w"], dhn, name="attn_norm_bwd", add=dh1)
    return loss_tile, dx, g


ANY = pl.BlockSpec(memory_space=pl.ANY)
MESH = pl.DeviceIdType.MESH


def _mesh_pos():
    return lax.axis_index("x"), lax.axis_index("y"), lax.axis_index("c")


def _other_chips(x, y):
    return [(1 - x, y), (x, 1 - y), (1 - x, 1 - y)]


def _remote(src, dst, send_sems, recv_sems, k, to):
    return pltpu.make_async_remote_copy(src_ref=src, dst_ref=dst, send_sem=send_sems.at[k], recv_sem=recv_sems.at[k],
                                        device_id=to, device_id_type=MESH)


def _place_shard(shard, piece_idx, row_sharded, name, out_dtype=BF16, pieces=N_CHIPS):
    rs, cs = shard.shape
    tr = _tile(rs, 256, 2 * SUBLANES)
    rb = rs // tr

    def body(p_ref, x_ref, o_ref):
        o_ref[...] = x_ref[...].astype(o_ref.dtype)

    if row_sharded:
        out_shape, out_map = (pieces * rs, cs), (lambda i, p_ref: (p_ref[0] * rb + i, 0))
    else:
        out_shape, out_map = (rs, pieces * cs), (lambda i, p_ref: (i, p_ref[0]))
    return pl.pallas_call(
        body, name=name, out_shape=jax.ShapeDtypeStruct(out_shape, out_dtype),
        grid_spec=pltpu.PrefetchScalarGridSpec(
            num_scalar_prefetch=1, grid=(rb,), in_specs=[pl.BlockSpec((tr, cs), lambda i, p_ref: (i, 0))],
            out_specs=pl.BlockSpec((tr, cs), out_map)),
        compiler_params=_params(("parallel",)),
    )(piece_idx, shard)


def _gather_weights(placed, name):
    n = len(placed)
    meta = [(row_sharded, direct) for _, row_sharded, direct in placed]
    over_ici, over_d2d = _gather_plans(meta)
    forwarded = [t for t, (_, direct) in enumerate(meta) if not direct]

    def body(*refs):
        outs = refs[n:2 * n]
        send_sems, recv_sems, pass_send_sems, pass_recv_sems = refs[2 * n:]
        first, arrivals = over_ici(outs, send_sems, recv_sems)
        passed, passed_arrivals = over_d2d([outs[t] for t in forwarded], pass_send_sems, pass_recv_sems)
        for cp in first:
            cp.start()
        for t in range(n):
            for j in range(3):
                arrivals[3 * t + j].wait_recv()
                if t in forwarded:
                    passed[3 * forwarded.index(t) + j].start()
        for cp in passed_arrivals:
            cp.wait_recv()
        for cp in first + passed:
            cp.wait_send()

    return pl.pallas_call(
        body, name=name, in_specs=[ANY] * n, out_specs=[ANY] * n,
        out_shape=[jax.ShapeDtypeStruct(arr.shape, arr.dtype) for arr, _, _ in placed],
        input_output_aliases={t: t for t in range(n)},
        scratch_shapes=[pltpu.SemaphoreType.DMA((3 * n,)), pltpu.SemaphoreType.DMA((3 * n,)),
                        pltpu.SemaphoreType.DMA((3 * len(forwarded),)), pltpu.SemaphoreType.DMA((3 * len(forwarded),))],
    )(*[arr for arr, _, _ in placed])


def _gather_plans(meta):
    def window(ref, row_sharded, piece, half):
        r, cc = ref.shape
        if row_sharded:
            rs = r // N_CHIPS
            if half is None:
                return ref.at[pl.ds(piece * rs, rs), :]
            return ref.at[pl.ds(piece * rs + half * (rs // 2), rs // 2), :]
        cs = cc // N_CHIPS
        if half is None:
            return ref.at[:, pl.ds(piece * cs, cs)]
        return ref.at[pl.ds(half * (r // 2), r // 2), pl.ds(piece * cs, cs)]

    def over_ici(refs, send_sems, recv_sems):
        x, y, c = _mesh_pos()
        sends, recvs = [], []
        for t, (row_sharded, direct) in enumerate(meta):
            mine = window(refs[t], row_sharded, 2 * x + y, None if direct else c)
            for j, (px, py) in enumerate(_other_chips(x, y)):
                theirs = window(refs[t], row_sharded, 2 * px + py, None if direct else c)
                sends.append(_remote(mine, mine, send_sems, recv_sems, 3 * t + j, (px, py, c)))
                recvs.append(_remote(theirs, theirs, send_sems, recv_sems, 3 * t + j, (px, py, c)))
        return sends, recvs

    def over_d2d(refs, send_sems, recv_sems):
        x, y, c = _mesh_pos()
        sends, recvs = [], []
        rows = [row_sharded for row_sharded, direct in meta if not direct]
        for t, row_sharded in enumerate(rows):
            for j, (px, py) in enumerate(_other_chips(x, y)):
                got = window(refs[t], row_sharded, 2 * px + py, c)
                other = window(refs[t], row_sharded, 2 * px + py, 1 - c)
                sends.append(_remote(got, got, send_sems, recv_sems, 3 * t + j, (x, y, 1 - c)))
                recvs.append(_remote(other, other, send_sems, recv_sems, 3 * t + j, (x, y, 1 - c)))
        return sends, recvs

    return over_ici, over_d2d


HBM = pl.BlockSpec(memory_space=pltpu.HBM)
SEMAPHORES = pl.BlockSpec(memory_space=pltpu.SEMAPHORE)
DATAFLOW = pltpu.SideEffectType.DATAFLOW_SIDE_EFFECTING


def _start_copies(name, arrays, plan, n_copies, after):
    n = len(arrays)

    def body(*refs):
        sends, _ = plan(refs[:n], refs[n + 1], refs[n + 2])
        for cp in sends:
            cp.start()
        token = refs[2 * n + 3]
        token[...] = jnp.zeros_like(token)

    out = pl.pallas_call(
        body, name=name,
        out_shape=(pltpu.SemaphoreType.DMA((n_copies,)), pltpu.SemaphoreType.DMA((n_copies,)),
                   *[pltpu.HBM(a.shape, a.dtype) for a in arrays], jax.ShapeDtypeStruct((SUBLANES, LANES), F32)),
        in_specs=[HBM] * n + [ANY],
        out_specs=(SEMAPHORES, SEMAPHORES, *[HBM] * n, pl.BlockSpec(memory_space=pltpu.VMEM)),
        input_output_aliases={t: t + 2 for t in range(n)},
        compiler_params=pltpu.CompilerParams(has_side_effects=DATAFLOW),
    )(*[pltpu.with_memory_space_constraint(a, pltpu.HBM) for a in arrays], after)
    return out[0], out[1], list(out[2:2 + n]), out[2 + n]


def _wait_copies(name, started, plan, after):
    send_sems, recv_sems, arrays, _ = started
    n = len(arrays)

    def body(*refs):
        sends, recvs = plan(refs[:n], refs[n], refs[n + 1])
        for cp in sends:
            cp.wait_send()
        for cp in recvs:
            cp.wait_recv()

    out = pl.pallas_call(
        body, name=name, out_shape=[pltpu.HBM(a.shape, a.dtype) for a in arrays],
        in_specs=[HBM] * n + [SEMAPHORES, SEMAPHORES, ANY], out_specs=[HBM] * n,
        input_output_aliases={t: t for t in range(n)},
        compiler_params=pltpu.CompilerParams(has_side_effects=DATAFLOW),
    )(*arrays, send_sems, recv_sems, after)
    return list(out)


def _exchange(name, arrays, out_shapes, plan, n_copies, in_place=False, after=None):
    n = len(arrays)
    extra = [] if after is None else [after]

    def body(*refs):
        ins, outs = refs[:n], refs[n + len(extra):n + len(extra) + len(out_shapes)]
        send_sems, recv_sems = refs[n + len(extra) + len(out_shapes):]
        sends, recvs = plan(ins, outs, send_sems, recv_sems)
        for cp in sends:
            cp.start()
        for cp in recvs:
            cp.wait_recv()
        for cp in sends:
            cp.wait_send()

    return pl.pallas_call(
        body, name=name, in_specs=[ANY] * (n + len(extra)), out_specs=[ANY] * len(out_shapes), out_shape=out_shapes,
        input_output_aliases={t: t for t in range(n)} if in_place else {},
        scratch_shapes=[pltpu.SemaphoreType.DMA((n_copies,)), pltpu.SemaphoreType.DMA((n_copies,))],
    )(*arrays, *extra)


def _swap_plan(n):
    def plan(refs, send_sems, recv_sems):
        x, y, c = _mesh_pos()
        sends = [_remote(refs[t].at[1 - c], refs[n + t], send_sems, recv_sems, t, (x, y, 1 - c)) for t in range(n)]
        return sends, sends

    return plan


def _scatter_plan(n):
    def plan(refs, send_sems, recv_sems):
        x, y, c = _mesh_pos()
        sends = []
        for t in range(n):
            for j, (px, py) in enumerate(_other_chips(x, y)):
                sends.append(_remote(refs[t].at[2 * px + py], refs[n + t].at[j], send_sems, recv_sems, 3 * t + j, (px, py, c)))
        return sends, sends

    return plan


def _swap_shapes(grads):
    return [jax.ShapeDtypeStruct(g.shape[1:], g.dtype) for g in grads]


def _scatter_shapes(sums):
    return [jax.ShapeDtypeStruct((3,) + s.shape[1:], s.dtype) for s in sums]


def _swap_other_half(grads, name):
    plan = _swap_plan(len(grads))
    return _exchange(name, grads, _swap_shapes(grads), lambda ins, outs, s, r: plan(list(ins) + list(outs), s, r), len(grads))


def _join_halves(halves, name, after=None):
    def plan(ins, outs, send_sems, recv_sems):
        x, y, c = _mesh_pos()
        sends = [_remote(outs[t].at[c], outs[t].at[c], send_sems, recv_sems, t, (x, y, 1 - c)) for t in range(len(ins))]
        recvs = [_remote(outs[t].at[1 - c], outs[t].at[1 - c], send_sems, recv_sems, t, (x, y, 1 - c))
                 for t in range(len(ins))]
        return sends, recvs

    shapes = [jax.ShapeDtypeStruct(h.shape, h.dtype) for h in halves]
    return _exchange(name, halves, shapes, plan, len(halves), in_place=True, after=after)


def _add_other_half(g4, got, where, name, wire_dtype=BF16):
    _, pieces, sr, sc = g4.shape
    tr = _tile(sr, 256, 2 * SUBLANES)

    def body(w_ref, a_ref, b_ref, o_ref):
        o_ref[...] = (a_ref[...] + b_ref[...]).astype(o_ref.dtype)

    blk = pl.BlockSpec((None, tr, sc), lambda p, i, w_ref: (p, i, 0))
    return pl.pallas_call(
        body, name=name, out_shape=jax.ShapeDtypeStruct((pieces, sr, sc), wire_dtype),
        grid_spec=pltpu.PrefetchScalarGridSpec(
            num_scalar_prefetch=1, grid=(pieces, sr // tr),
            in_specs=[pl.BlockSpec((None, None, tr, sc), lambda p, i, w_ref: (w_ref[0], p, i, 0)), blk], out_specs=blk),
        compiler_params=_params(("parallel", "parallel")),
    )(where, g4, got)


def _add_pieces(g4, got_half, got_pieces, where, name):
    _, _, sr, sc = g4.shape
    tr = _tile(sr, 256, 2 * SUBLANES)

    def body(w_ref, a_ref, b_ref, r_ref, o_ref):
        acc = a_ref[...] + b_ref[...]
        for j in range(3):
            acc = acc + r_ref[j].astype(F32)
        o_ref[...] = acc

    return pl.pallas_call(
        body, name=name, out_shape=jax.ShapeDtypeStruct((N_CORES, sr, sc), F32),
        grid_spec=pltpu.PrefetchScalarGridSpec(
            num_scalar_prefetch=1, grid=(sr // tr,),
            in_specs=[pl.BlockSpec((None, None, tr, sc), lambda i, w_ref: (w_ref[0], w_ref[1], i, 0)),
                      pl.BlockSpec((None, tr, sc), lambda i, w_ref: (w_ref[1], i, 0)),
                      pl.BlockSpec((3, tr, sc), lambda i, w_ref: (0, i, 0))],
            out_specs=pl.BlockSpec((None, tr, sc), lambda i, w_ref: (w_ref[0], i, 0))),
        compiler_params=_params(("parallel",)),
    )(where, g4, got_half, got_pieces)


def _adamw_update(w, g, m, v):
    nm = ADAM_B1 * m + (1.0 - ADAM_B1) * g
    nv = ADAM_B2 * v + (1.0 - ADAM_B2) * (g * g)
    m_hat = nm / (1.0 - ADAM_B1 ** ADAM_STEP)
    v_hat = nv / (1.0 - ADAM_B2 ** ADAM_STEP)
    return -ADAM_LR * (m_hat / (jnp.sqrt(v_hat) + ADAM_EPS) + ADAM_WD * w), nm, nv


def _adamw(w, g, m, v, name):
    rows, cols = w.shape
    halves = 2 if g.ndim == 3 else 1
    bc = cols // halves
    tr = _tile(rows, max(SUBLANES, (1 << 19) // max(bc, 1) // SUBLANES * SUBLANES), SUBLANES)

    def body(w_ref, g_ref, m_ref, v_ref, d_ref, nm_ref, nv_ref, go_ref):
        gv = g_ref[...]
        d_ref[...], nm_ref[...], nv_ref[...] = _adamw_update(w_ref[...], gv, m_ref[...], v_ref[...])
        go_ref[...] = gv

    blk = pl.BlockSpec((tr, bc), lambda i, h: (i, h))
    g_blk = pl.BlockSpec((None, tr, bc), lambda i, h: (h, i, 0)) if halves == 2 else blk
    return pl.pallas_call(
        body, name=name, grid=(rows // tr, halves), in_specs=[blk, g_blk, blk, blk], out_specs=[blk] * 4,
        out_shape=[jax.ShapeDtypeStruct((rows, cols), F32)] * 4, compiler_params=_params(("parallel", "parallel")),
    )(w, g, m, v)


def _adamw_many(ws, gs, ms, vs, name):
    n = len(ws)

    def body(*refs):
        outs = refs[4 * n:]
        for k in range(n):
            w_ref, g_ref, m_ref, v_ref = (refs[j * n + k] for j in range(4))
            outs[k][...], outs[n + k][...], outs[2 * n + k][...] = _adamw_update(w_ref[...], g_ref[...], m_ref[...], v_ref[...])

    out = pl.pallas_call(
        body, name=name, out_shape=[jax.ShapeDtypeStruct(w.shape, F32) for w in ws] * 3,
        compiler_params=pltpu.CompilerParams(vmem_limit_bytes=VMEM_LIMIT_BYTES),
    )(*ws, *gs, *ms, *vs)
    return out[:n], out[n:2 * n], out[2 * n:]


WEIGHTS = ['attn_norm_w', 'w_in', 'ssm_lambda_re', 'ssm_lambda_im', 'ssm_log_dt', 'ssm_b_re', 'ssm_b_im', 'ssm_c_re',
           'ssm_c_im', 'ssm_d', 'ssm_w_glu', 'ssm_b_glu', 'mla_q_norm_w', 'mla_w_uq', 'mla_kv_norm_w', 'mla_w_ukv',
           'ssm_out_norm_w', 'mla_out_norm_w', 'w_out', 'ffn_norm_w', 'ffn_w_up', 'ffn_conv_w', 'ffn_conv_b',
           'ffn_w_down', 'final_norm_w']
SHARDED = {'w_in': False, 'ssm_w_glu': True, 'mla_w_uq': False, 'mla_w_ukv': False, 'w_out': True, 'ffn_w_up': False,
           'ffn_w_down': True}
SMALL = [n for n in WEIGHTS if n not in SHARDED and n != 'ffn_conv_w']
ROPE_PAD = HEAD_SLOT - QK_NOPE_DIM - QK_ROPE_DIM
SMALL_COLS = 8 * LANES


def _pad_heads(w_uq, heads):
    qr = w_uq.shape[0]
    w3 = w_uq.reshape(qr, heads, QK_NOPE_DIM + QK_ROPE_DIM)
    return jnp.concatenate([w3, jnp.zeros((qr, heads, ROPE_PAD), w_uq.dtype)], axis=2).reshape(qr, heads * HEAD_SLOT)


def _unpad_heads(g_uq, heads):
    qr = g_uq.shape[0]
    return g_uq.reshape(qr, heads, HEAD_SLOT)[:, :, :QK_NOPE_DIM + QK_ROPE_DIM].reshape(qr, -1)


FFN = ['ffn_w_up', 'ffn_w_down']
FFN_GATHER = FFN + ['ffn_conv_w']
FFN_GATHER_META = [(SHARDED[n], False) for n in FFN] + [(False, True)]


class _Overlapped:
    def __init__(self, placed, where, after):
        self.where = where
        self.over_ici, self.over_d2d = _gather_plans(FFN_GATHER_META)
        self.gather = _start_copies("gather_ffn_start", placed, self.over_ici, 3 * len(placed), after)
        self.gather_started = self.gather[3]

    def ffn_weights_arrived(self, after):
        arrived = _wait_copies("gather_ffn_wait", self.gather, self.over_ici, after)
        n = len(FFN)
        self.direct = arrived[n:]
        self.passing = _start_copies("gather_ffn_pass_start", arrived[:n], self.over_d2d, 3 * n, after)
        return self.passing[3]

    def ffn_weights(self, after):
        passed = _wait_copies("gather_ffn_pass_wait", self.passing, self.over_d2d, after)
        return dict(zip(FFN_GATHER, passed + self.direct))

    def ffn_grads(self, g_down, g_up, after):
        grads = [g_up, g_down]
        lands = [lax.empty(s.shape, s.dtype) for s in _swap_shapes(grads)]
        self.swap = _start_copies("grad_ffn_swap_start", grads + lands, _swap_plan(len(grads)), len(grads), after)
        return self.swap[3]

    def ffn_backward_done(self, after):
        n = len(FFN)
        out = _wait_copies("grad_ffn_swap_wait", self.swap, _swap_plan(n), after)
        self.grads, self.got_half = out[:n], out[n:]
        sums = [_add_other_half(self.grads[t], self.got_half[t], self.where, "grad_add_half_" + name)
                for t, name in enumerate(FFN)]
        lands = [lax.empty(s.shape, s.dtype) for s in _scatter_shapes(sums)]
        self.scatter = _start_copies("grad_ffn_scatter_start", sums + lands, _scatter_plan(n), 3 * n, after)
        return self.scatter[3]

    def ffn_reduced(self, after):
        n = len(FFN)
        got_pieces = _wait_copies("grad_ffn_scatter_wait", self.scatter, _scatter_plan(n), after)[n:]
        return [_add_pieces(self.grads[t], self.got_half[t], got_pieces[t], self.where, "grad_add_pieces_" + name)
                for t, name in enumerate(FFN)]


def _step(args):
    x, positions, target = args["x"][0], args["positions"], args["loss_target"][0]
    rows = x.shape[0]
    p = {n: args[n] for n in WEIGHTS}
    xi, yi, ci = _mesh_pos()
    piece = 2 * xi + yi

    def transposed(a):
        return jnp.swapaxes(a[0], 0, 1)

    w_in = transposed(p["w_in"])
    in_width = w_in.shape[0]
    in_pad = (-in_width) % (2 * LANES)
    heads_here = p["mla_w_uq"].shape[2] // (QK_NOPE_DIM + QK_ROPE_DIM)
    shards = {
        "w_in": jnp.pad(w_in, ((0, in_pad), (0, 0))),
        "ssm_w_glu": p["ssm_w_glu"][0],
        "mla_w_uq": _pad_heads(p["mla_w_uq"][0], heads_here),
        "mla_w_ukv": p["mla_w_ukv"][0],
        "w_out": p["w_out"][0],
        "ffn_w_up": p["ffn_w_up"][0],
        "ffn_w_down": p["ffn_w_down"][0],
    }
    conv_w = jnp.pad(p["ffn_conv_w"][0], ((0, SUBLANES - p["ffn_conv_w"].shape[1]), (0, 0)))
    order = list(SHARDED)
    piece_idx = piece.reshape(1).astype(jnp.int32)
    placed = {n: _place_shard(shards[n], piece_idx, SHARDED[n], "place_" + n) for n in order}
    placed["ffn_conv_w"] = _place_shard(conv_w, piece_idx, False, "place_ffn_conv_w", out_dtype=F32)
    mixer = [n for n in order if n not in FFN]
    w = dict(zip(mixer, _gather_weights([(placed[n], SHARDED[n], False) for n in mixer], "gather_mixer_weights")))
    where = jnp.stack([ci, piece]).astype(jnp.int32)
    hooks = _Overlapped([placed[n] for n in FFN_GATHER], where, after=w["w_in"])
    groups = p["ssm_lambda_re"].shape[1]
    w.update({
        "attn_norm_w": p["attn_norm_w"] + hooks.gather_started[:1, :1],
        "ssm_lambda_re": p["ssm_lambda_re"][0], "ssm_lambda_im": p["ssm_lambda_im"][0],
        "ssm_log_dt": p["ssm_log_dt"].reshape(groups, 1), "ssm_b_re": p["ssm_b_re"].reshape(groups, -1),
        "ssm_b_im": p["ssm_b_im"].reshape(groups, -1), "ssm_c_re": p["ssm_c_re"][0], "ssm_c_im": p["ssm_c_im"][0],
        "ssm_d": p["ssm_d"], "ssm_b_glu": p["ssm_b_glu"], "mla_q_norm_w": p["mla_q_norm_w"],
        "mla_kv_norm_w": p["mla_kv_norm_w"], "ssm_out_norm_w": p["ssm_out_norm_w"], "mla_out_norm_w": p["mla_out_norm_w"],
        "ffn_norm_w": p["ffn_norm_w"], "ffn_conv_b": p["ffn_conv_b"], "final_norm_w": p["final_norm_w"].reshape(1, -1),
    })

    loss_tile, dx, g = _local_step(x, positions.reshape(rows, 1).astype(F32), target, w, hooks)
    loss = lax.psum(loss_tile[0, 0], ("x", "y", "c"))

    flat = [g[n].reshape(-1) for n in SMALL] + [g["ffn_conv_w"].reshape(-1)]
    sizes = [f.shape[0] for f in flat]
    per_block = -(-sum(sizes) // (N_CORES * N_CHIPS * SMALL_COLS))
    small_rows = -(-per_block // (2 * SUBLANES)) * (2 * SUBLANES)
    padded = N_CORES * N_CHIPS * small_rows * SMALL_COLS

    def pack(parts):
        parts = list(parts)
        have = sum(q.shape[0] for q in parts)
        return jnp.concatenate(parts + [jnp.zeros((padded - have,), F32)])

    reduced = mixer + ["small"]
    g_rs = [g[n] for n in mixer] + [pack(flat).reshape(N_CORES, N_CHIPS, small_rows, SMALL_COLS)]
    wire = [BF16] * len(mixer) + [F32]
    got_half = _swap_other_half(g_rs, "grad_swap_halves")
    sums = [_add_other_half(g_rs[t], got_half[t], where, "grad_add_half_" + n, wire[t]) for t, n in enumerate(reduced)]
    lands = [lax.empty(s.shape, s.dtype) for s in _scatter_shapes(sums)]
    scatter_plan = _scatter_plan(len(reduced))
    scatter = _start_copies("grad_mixer_scatter_start", sums + lands, scatter_plan, 3 * len(reduced), dx)

    grads, delta, new_m, new_v = {}, {}, {}, {}

    def finish(n, joined):
        grad = joined if SHARDED[n] else joined.reshape(-1, joined.shape[2])
        if n == "w_in":
            wt, mt, vt = w_in, transposed(args["m_w_in"]), transposed(args["v_w_in"])
            out = _adamw(wt, grad, mt, vt, "adamw_w_in")
            delta[n], new_m[n], new_v[n], grads[n] = (jnp.swapaxes(a, 0, 1)[None] for a in out)
            return
        if n == "mla_w_uq":
            grad = _unpad_heads(grad, heads_here)
        adam(n, grad)

    def adam(n, grad):
        shape = p[n].shape
        out = _adamw(p[n].reshape(shape[1:]), grad, args["m_" + n].reshape(shape[1:]),
                     args["v_" + n].reshape(shape[1:]), "adamw_" + n)
        delta[n], new_m[n], new_v[n], grads[n] = (a.reshape(shape) for a in out)

    for n, joined in zip(FFN, _join_halves(hooks.ffn_reduced(dx), "grad_ffn_join_halves", after=scatter[3])):
        finish(n, joined)
    got_pieces = _wait_copies("grad_mixer_scatter_wait", scatter, scatter_plan, delta[FFN[0]])[len(reduced):]
    halves = [_add_pieces(g_rs[t], got_half[t], got_pieces[t], where, "grad_add_pieces_" + n) for t, n in enumerate(reduced)]
    joined = _join_halves(halves, "grad_join_halves")
    for n, j in zip(mixer, joined):
        finish(n, j)
    eighths = _place_shard(joined[-1].reshape(N_CORES * small_rows, SMALL_COLS), piece_idx, True, "place_small_grads",
                           out_dtype=F32)
    small_sum = _gather_weights([(eighths, True, False)], "gather_small_grads")[0]
    flat_sum = small_sum.reshape(N_CHIPS, N_CORES, small_rows * SMALL_COLS).transpose(1, 0, 2).reshape(-1)
    offs = [0]
    for s in sizes:
        offs.append(offs[-1] + s)
    for k, n in enumerate(SMALL):
        grads[n] = flat_sum[offs[k]:offs[k + 1]].reshape(p[n].shape)
    taps, cols_here = p["ffn_conv_w"].shape[1], p["ffn_conv_w"].shape[2]
    conv_full = flat_sum[offs[len(SMALL)]:offs[len(SMALL) + 1]].reshape(taps, N_CHIPS * cols_here)
    adam("ffn_conv_w", lax.dynamic_slice_in_dim(conv_full, piece * cols_here, cols_here, axis=1))

    def rank2(a):
        return a.reshape(1, -1) if a.ndim == 1 else a

    d_s, m_s, v_s = _adamw_many([rank2(p[n]) for n in SMALL], [rank2(grads[n]) for n in SMALL],
                                [rank2(args["m_" + n]) for n in SMALL], [rank2(args["v_" + n]) for n in SMALL], "adamw_small")
    for k, n in enumerate(SMALL):
        delta[n], new_m[n], new_v[n] = (a.reshape(p[n].shape) for a in (d_s[k], m_s[k], v_s[k]))

    return (loss, dx[None], *[grads[n] for n in WEIGHTS], *[delta[n] for n in WEIGHTS],
            *[new_m[n] for n in WEIGHTS], *[new_v[n] for n in WEIGHTS])


def kernel(x, positions, attn_norm_w, w_in, ssm_lambda_re, ssm_lambda_im, ssm_log_dt, ssm_b_re, ssm_b_im, ssm_c_re, ssm_c_im, ssm_d, ssm_w_glu, ssm_b_glu, mla_q_norm_w, mla_w_uq, mla_kv_norm_w, mla_w_ukv, ssm_out_norm_w, mla_out_norm_w, w_out, ffn_norm_w, ffn_w_up, ffn_conv_w, ffn_conv_b, ffn_w_down, final_norm_w, loss_target, m_attn_norm_w, m_w_in, m_ssm_lambda_re, m_ssm_lambda_im, m_ssm_log_dt, m_ssm_b_re, m_ssm_b_im, m_ssm_c_re, m_ssm_c_im, m_ssm_d, m_ssm_w_glu, m_ssm_b_glu, m_mla_q_norm_w, m_mla_w_uq, m_mla_kv_norm_w, m_mla_w_ukv, m_ssm_out_norm_w, m_mla_out_norm_w, m_w_out, m_ffn_norm_w, m_ffn_w_up, m_ffn_conv_w, m_ffn_conv_b, m_ffn_w_down, m_final_norm_w, v_attn_norm_w, v_w_in, v_ssm_lambda_re, v_ssm_lambda_im, v_ssm_log_dt, v_ssm_b_re, v_ssm_b_im, v_ssm_c_re, v_ssm_c_im, v_ssm_d, v_ssm_w_glu, v_ssm_b_glu, v_mla_q_norm_w, v_mla_w_uq, v_mla_kv_norm_w, v_mla_w_ukv, v_ssm_out_norm_w, v_mla_out_norm_w, v_w_out, v_ffn_norm_w, v_ffn_w_up, v_ffn_conv_w, v_ffn_conv_b, v_ffn_w_down, v_final_norm_w):
    return _step(dict(locals()))
```

```python
import functools
import math

import jax
import jax.numpy as jnp
from jax import lax
from jax.experimental import pallas as pl
from jax.experimental.pallas import tpu as pltpu

F32 = jnp.float32
BF16 = jnp.bfloat16

SSM_GROUP = 16
SSM_STATE = 64
QK_NOPE_DIM = 128
QK_ROPE_DIM = 64
V_HEAD_DIM = 128
ROPE_THETA = 10000.0
RMS_EPS = 1e-6
ADAM_LR, ADAM_B1, ADAM_B2, ADAM_EPS, ADAM_WD, ADAM_STEP = 0.001, 0.9, 0.999, 1e-08, 0.01, 10

LANES = 128
SUBLANES = 8
VMEM_LIMIT_BYTES = 56 * 1024 * 1024

GROUPS_PER_BATCH = LANES // SSM_GROUP
STATE_PER_BATCH = GROUPS_PER_BATCH * SSM_STATE
HEAD_SLOT = 2 * LANES
NEG_INF = -1e30
ATTN_BLOCK = 512
FFN_ROWS = 1024

N_CHIPS = 4
N_CORES = 2


def _tile(n, pref, align=LANES):
    if n <= pref:
        return n
    t = (pref // align) * align
    while t >= align:
        if n % t == 0:
            return t
        t -= align
    return n


def _params(sem):
    return pltpu.CompilerParams(dimension_semantics=sem, vmem_limit_bytes=VMEM_LIMIT_BYTES)


def _dot(a, b, dims):
    return lax.dot_general(a, b, (dims, ((), ())), preferred_element_type=F32)


def _dot_nn(a, b):
    return _dot(a, b, ((1,), (0,)))


def _dot_nt(a, b):
    return _dot(a, b, ((1,), (1,)))


def _dot_tn(a, b):
    return _dot(a, b, ((0,), (0,)))


def _matmul(a, b, *, mode, name, tm=512, tn=1024, tk=2048, bias=None, add=None, out_dtype=F32,
            out_blocks=None, a_split=False, b_split=False, after=None):
    if a_split:
        assert mode == "nt"
        a_shape = (a.shape[1], 2 * a.shape[2])
    else:
        a_shape = a.shape
    if b_split:
        assert mode == "tn"
        b_shape = (b.shape[1], 2 * b.shape[2])
    else:
        b_shape = b.shape
    if mode == "nn":
        (m, k), (k2, n) = a_shape, b_shape
    elif mode == "nt":
        (m, k), (n, k2) = a_shape, b_shape
    else:
        (k, m), (k2, n) = a_shape, b_shape
    assert k == k2, (a.shape, b.shape, mode)
    tm, tn, tk = _tile(m, tm, SUBLANES), _tile(n, tn), _tile(k, tk)
    nk = k // tk
    a_spec = {"nn": pl.BlockSpec((tm, tk), lambda i, j, kk: (i, kk)),
              "nt": pl.BlockSpec((tm, tk), lambda i, j, kk: (i, kk)),
              "tn": pl.BlockSpec((tk, tm), lambda i, j, kk: (kk, i))}[mode]
    b_spec = {"nn": pl.BlockSpec((tk, tn), lambda i, j, kk: (kk, j)),
              "nt": pl.BlockSpec((tn, tk), lambda i, j, kk: (j, kk)),
              "tn": pl.BlockSpec((tk, tn), lambda i, j, kk: (kk, j))}[mode]
    if a_split:
        kb = a.shape[2] // tk
        assert a.shape[2] % tk == 0
        a_spec = pl.BlockSpec((None, tm, tk), lambda i, j, kk: (kk // kb, i, kk % kb))
    if b_split:
        nb = b.shape[2] // tn
        assert b.shape[2] % tn == 0
        b_spec = pl.BlockSpec((None, tk, tn), lambda i, j, kk: (j // nb, kk, j % nb))
    dot = {"nn": _dot_nn, "nt": _dot_nt, "tn": _dot_tn}[mode]
    in_specs, operands = [a_spec, b_spec], [a, b]
    if bias is not None:
        in_specs.append(pl.BlockSpec((1, tn), lambda i, j, kk: (0, j)))
        operands.append(bias)
    if add is not None:
        in_specs.append(pl.BlockSpec((tm, tn), lambda i, j, kk: (i, j)))
        operands.append(add)
    if after is not None:
        in_specs.append(pl.BlockSpec(memory_space=pl.ANY))
        operands.append(after)

    def body(*refs):
        a_ref, b_ref = refs[0], refs[1]
        rest = list(refs[2:])
        bias_ref = rest.pop(0) if bias is not None else None
        add_ref = rest.pop(0) if add is not None else None
        if after is not None:
            rest.pop(0)
        o_ref, acc_ref = rest

        def finish(acc):
            if bias_ref is not None:
                acc = acc + bias_ref[...]
            if add_ref is not None:
                acc = acc + add_ref[...]
            o_ref[...] = acc.astype(o_ref.dtype)

        part = dot(a_ref[...].astype(BF16), b_ref[...].astype(BF16))
        if nk == 1:
            finish(part)
        else:
            kk = pl.program_id(2)

            @pl.when(kk == 0)
            def _():
                acc_ref[...] = part

            @pl.when(jnp.logical_and(kk > 0, kk < nk - 1))
            def _():
                acc_ref[...] += part

            @pl.when(kk == nk - 1)
            def _():
                finish(acc_ref[...] + part)

    if out_blocks is None:
        out_shape = jax.ShapeDtypeStruct((m, n), out_dtype)
        out_spec = pl.BlockSpec((tm, tn), lambda i, j, kk: (i, j))
    else:
        shape, block, index_map = out_blocks(tm, tn)
        out_shape = jax.ShapeDtypeStruct(shape, out_dtype)
        out_spec = pl.BlockSpec(block, index_map)
    acc_shape = (tm, tn) if nk > 1 else (SUBLANES, LANES)
    return pl.pallas_call(
        body, name=name, grid=(m // tm, n // tn, nk), in_specs=in_specs, out_specs=out_spec, out_shape=out_shape,
        scratch_shapes=[pltpu.VMEM(acc_shape, F32)],
        compiler_params=_params(("parallel", "parallel", "arbitrary")),
    )(*operands)


def _wgrad_blocks(rows, cols, row_sharded):
    if row_sharded:
        sr, sc = rows // N_CHIPS, cols // N_CORES
    else:
        sr, sc = rows // N_CORES, cols // N_CHIPS

    def make(tm, tn):
        assert sr % tm == 0 and sc % tn == 0, (rows, cols, tm, tn)
        rb, cb = sr // tm, sc // tn
        if row_sharded:
            def index_map(i, j, kk):
                return (j // cb, i // rb, i % rb, j % cb)
        else:
            def index_map(i, j, kk):
                return (i // rb, j // cb, i % rb, j % cb)
        return (N_CORES, N_CHIPS, sr, sc), (None, None, tm, tn), index_map

    return make, (sr, sc)


def _rms_rows(x):
    return lax.rsqrt(jnp.mean(x * x, axis=-1, keepdims=True) + RMS_EPS)


def _rmsnorm_fwd(x, w, *, name, width=None, col=0, out_dtype=BF16, tr=256):
    rows = x.shape[0]
    width = x.shape[1] if width is None else width
    tr = _tile(rows, tr, SUBLANES)

    def body(x_ref, w_ref, o_ref):
        xv = x_ref[...]
        o_ref[...] = (xv * _rms_rows(xv) * w_ref[...]).astype(o_ref.dtype)

    return pl.pallas_call(
        body, name=name, grid=(rows // tr,),
        in_specs=[pl.BlockSpec((tr, width), lambda i: (i, col)), pl.BlockSpec((1, width), lambda i: (0, 0))],
        out_specs=pl.BlockSpec((tr, width), lambda i: (i, 0)),
        out_shape=jax.ShapeDtypeStruct((rows, width), out_dtype),
        compiler_params=_params(("parallel",)),
    )(x, w)


def _rmsnorm_bwd_rows(xv, w, dy):
    r = _rms_rows(xv)
    n = xv * r
    dn = dy * w
    dx = r * (dn - n * jnp.mean(dn * n, axis=-1, keepdims=True))
    return dx, dy * n


def _rmsnorm_bwd(x, w, dy, *, name, width=None, col=0, dy_col=0, add=None, tr=256, dx_dtypes=(F32,)):
    rows = x.shape[0]
    n_dx = len(dx_dtypes)
    width = x.shape[1] if width is None else width
    tr = _tile(rows, tr, SUBLANES)
    in_specs = [pl.BlockSpec((tr, width), lambda i: (i, col)), pl.BlockSpec((1, width), lambda i: (0, 0)),
                pl.BlockSpec((tr, width), lambda i: (i, dy_col))]
    operands = [x, w, dy]
    if add is not None:
        in_specs.append(pl.BlockSpec((tr, width), lambda i: (i, 0)))
        operands.append(add)

    def body(*refs):
        x_ref, w_ref, dy_ref = refs[:3]
        add_ref = refs[3] if add is not None else None
        dx_refs, dw_ref = refs[-1 - n_dx:-1], refs[-1]
        dx, dwp = _rmsnorm_bwd_rows(x_ref[...], w_ref[...], dy_ref[...])
        if add_ref is not None:
            dx = dx + add_ref[...]
        for dx_ref in dx_refs:
            dx_ref[...] = dx.astype(dx_ref.dtype)
        part = jnp.sum(dwp, axis=0, keepdims=True)

        @pl.when(pl.program_id(0) == 0)
        def _():
            dw_ref[...] = part

        @pl.when(pl.program_id(0) > 0)
        def _():
            dw_ref[...] += part

    return pl.pallas_call(
        body, name=name, grid=(rows // tr,), in_specs=in_specs,
        out_specs=[pl.BlockSpec((tr, width), lambda i: (i, 0))] * n_dx + [pl.BlockSpec((1, width), lambda i: (0, 0))],
        out_shape=[jax.ShapeDtypeStruct((rows, width), dt) for dt in dx_dtypes] + [jax.ShapeDtypeStruct((1, width), F32)],
        compiler_params=_params(("arbitrary",)),
    )(*operands)


def _final_norm_loss(h, w, target, *, tr=256):
    rows, d = h.shape
    tr = _tile(rows, tr, SUBLANES)

    def body(h_ref, w_ref, t_ref, loss_ref, dh_ref, dhb_ref, dw_ref):
        hv, wv = h_ref[...], w_ref[...]
        r = _rms_rows(hv)
        n = hv * r
        err = n * wv - t_ref[...]
        d_out = err * (1.0 / d)
        dn = d_out * wv
        dh = r * (dn - n * jnp.mean(dn * n, axis=-1, keepdims=True))
        dh_ref[...] = dh
        dhb_ref[...] = dh.astype(BF16)
        dw_part = jnp.sum(d_out * n, axis=0, keepdims=True)
        loss_part = jnp.full((SUBLANES, LANES), 0.5 / d, F32) * jnp.sum(err * err)

        @pl.when(pl.program_id(0) == 0)
        def _():
            dw_ref[...] = dw_part
            loss_ref[...] = loss_part

        @pl.when(pl.program_id(0) > 0)
        def _():
            dw_ref[...] += dw_part
            loss_ref[...] += loss_part

    return pl.pallas_call(
        body, name="final_norm_loss", grid=(rows // tr,),
        in_specs=[pl.BlockSpec((tr, d), lambda i: (i, 0)), pl.BlockSpec((1, d), lambda i: (0, 0)),
                  pl.BlockSpec((tr, d), lambda i: (i, 0))],
        out_specs=[pl.BlockSpec((SUBLANES, LANES), lambda i: (0, 0)), pl.BlockSpec((tr, d), lambda i: (i, 0)),
                   pl.BlockSpec((tr, d), lambda i: (i, 0)), pl.BlockSpec((1, d), lambda i: (0, 0))],
        out_shape=[jax.ShapeDtypeStruct((SUBLANES, LANES), F32), jax.ShapeDtypeStruct((rows, d), F32),
                   jax.ShapeDtypeStruct((rows, d), BF16), jax.ShapeDtypeStruct((1, d), F32)],
        compiler_params=_params(("arbitrary",)),
    )(h, w, target)


def _cmul(ar, ai, br, bi):
    return ar * br - ai * bi, ar * bi + ai * br


def _expand_matrix(groups, reps):
    row = lax.broadcasted_iota(jnp.int32, (groups, groups * reps), 0)
    colg = lax.broadcasted_iota(jnp.int32, (groups, groups * reps), 1) // reps
    return (row == colg).astype(F32)


def _dot_exact(a, b, dims):
    return lax.dot_general(a, b, (dims, ((), ())), preferred_element_type=F32, precision=lax.Precision.HIGHEST)


def _s5_discretize(lr, li, dt):
    mag = jnp.exp(lr * dt)
    th = li * dt
    ar, ai = mag * jnp.cos(th), mag * jnp.sin(th)
    nr, ni = ar - 1.0, ai
    den = lr * lr + li * li
    zr = (nr * lr + ni * li) / den
    zi = (ni * lr - nr * li) / den
    return mag, ar, ai, nr, ni, den, zr, zi


def _s5_params(lam_re, lam_im, log_dt, b_re, b_im):
    g, p = lam_re.shape
    ph = b_re.shape[1]

    def body(lr_ref, li_ref, ldt_ref, br_ref, bi_ref, ar_ref, ai_ref, bbr_ref, bbi_ref):
        dt = jnp.exp(ldt_ref[...])
        _, ar, ai, _, _, _, zr, zi = _s5_discretize(lr_ref[...], li_ref[...], dt)
        ar_ref[...] = ar
        ai_ref[...] = ai
        e = _expand_matrix(p, ph // p)
        zr_x = _dot_exact(zr, e, ((1,), (0,)))
        zi_x = _dot_exact(zi, e, ((1,), (0,)))
        bre, bim = br_ref[...], bi_ref[...]
        bbr_ref[...] = zr_x * bre - zi_x * bim
        bbi_ref[...] = zr_x * bim + zi_x * bre

    return pl.pallas_call(
        body, name="s5_params",
        out_shape=[jax.ShapeDtypeStruct((g, p), F32)] * 2 + [jax.ShapeDtypeStruct((g, ph), F32)] * 2,
    )(lam_re, lam_im, log_dt, b_re, b_im)


def _s5_params_bwd(lam_re, lam_im, log_dt, b_re, b_im, d_ar, d_ai, d_bbr, d_bbi):
    g, p = lam_re.shape
    ph = b_re.shape[1]

    def body(lr_ref, li_ref, ldt_ref, br_ref, bi_ref, dar_ref, dai_ref, dbr_ref, dbi_ref,
             dlr_ref, dli_ref, dldt_ref, dbre_ref, dbim_ref):
        lr, li = lr_ref[...], li_ref[...]
        dt = jnp.exp(ldt_ref[...])
        mag, ar, ai, nr, ni, den, zr, zi = _s5_discretize(lr, li, dt)
        e = _expand_matrix(p, ph // p)
        zr_x = _dot_exact(zr, e, ((1,), (0,)))
        zi_x = _dot_exact(zi, e, ((1,), (0,)))
        bre, bim, dbr, dbi = br_ref[...], bi_ref[...], dbr_ref[...], dbi_ref[...]
        dbre_ref[...] = zr_x * dbr + zi_x * dbi
        dbim_ref[...] = zr_x * dbi - zi_x * dbr
        dzr = _dot_exact(bre * dbr + bim * dbi, e, ((1,), (1,)))
        dzi = _dot_exact(bre * dbi - bim * dbr, e, ((1,), (1,)))
        inv = 1.0 / den
        d_nr = (dzr * lr - dzi * li) * inv
        d_ni = (dzr * li + dzi * lr) * inv
        d_den = -(dzr * zr + dzi * zi) * inv
        d_lr = (dzr * nr + dzi * ni) * inv + 2.0 * lr * d_den
        d_li = (dzr * ni - dzi * nr) * inv + 2.0 * li * d_den
        t_ar = dar_ref[...] + d_nr
        t_ai = dai_ref[...] + d_ni
        d_lrdt = t_ar * ar + t_ai * ai
        d_th = t_ai * ar - t_ar * ai
        dlr_ref[...] = d_lr + d_lrdt * dt
        dli_ref[...] = d_li + d_th * dt
        dldt_ref[...] = jnp.sum(d_lrdt * lr + d_th * li, axis=1, keepdims=True) * dt

    return pl.pallas_call(
        body, name="s5_params_bwd",
        out_shape=[jax.ShapeDtypeStruct((g, p), F32)] * 2 + [jax.ShapeDtypeStruct((g, 1), F32)]
        + [jax.ShapeDtypeStruct((g, ph), F32)] * 2,
    )(lam_re, lam_im, log_dt, b_re, b_im, d_ar, d_ai, d_bbr, d_bbi)


def _powers(ar, ai, count):
    out = [(ar, ai)]
    for _ in range(count - 1):
        out.append(_cmul(out[-1][0], out[-1][1], ar, ai))
    return out


def _scan_coefs(ar, ai, reverse):
    w = ar.shape[-1]
    pw = _powers(ar, ai, SUBLANES)
    row = lax.broadcasted_iota(jnp.int32, (SUBLANES, w), 0)
    steps = []
    d = 1
    while d < SUBLANES:
        keep = (row < SUBLANES - d) if reverse else (row >= d)
        pr, pi = pw[d - 1]
        steps.append((d, jnp.where(keep, pr, 0.0), jnp.where(keep, pi, 0.0)))
        d *= 2
    cr = jnp.zeros((SUBLANES, w), F32)
    ci = jnp.zeros((SUBLANES, w), F32)
    for t in range(SUBLANES):
        pr, pi = pw[SUBLANES - 1 - t] if reverse else pw[t]
        cr = jnp.where(row == t, pr, cr)
        ci = jnp.where(row == t, pi, ci)
    return steps, cr, ci


def _scan_tile(xr, xi, carry_r, carry_i, coefs, reverse):
    steps, cr, ci = coefs
    for d, mr, mi in steps:
        shift = SUBLANES - d if reverse else d
        sr, si = pltpu.roll(xr, shift, 0), pltpu.roll(xi, shift, 0)
        pr, pi = _cmul(mr, mi, sr, si)
        xr, xi = xr + pr, xi + pi
    pr, pi = _cmul(cr, ci, carry_r, carry_i)
    return xr + pr, xi + pi


def _gelu(x):
    c = math.sqrt(2.0 / math.pi)
    return 0.5 * x * (1.0 + jnp.tanh(c * (x + 0.044715 * x * x * x)))


def _gelu_grad(x):
    c = math.sqrt(2.0 / math.pi)
    t = jnp.tanh(c * (x + 0.044715 * x * x * x))
    return 0.5 * (1.0 + t) + 0.5 * x * (1.0 - t * t) * c * (1.0 + 3.0 * 0.044715 * x * x)


def _s5_fwd(proj, wb, wc, d_skip, abar):
    rows = proj.shape[0]
    nb = wb.shape[0]
    s2 = 2 * STATE_PER_BATCH
    st = STATE_PER_BATCH
    chunk = _tile(rows, 512, SUBLANES)

    def body(u_ref, wb_ref, wc_ref, d_ref, a_ref, s_ref, y_ref, yg_ref):
        for c0 in range(0, rows, chunk):
            s_ref[pl.ds(c0, chunk), :] = _dot_nn(u_ref[pl.ds(c0, chunk), :].astype(BF16), wb_ref[...])
        av = a_ref[...]
        coefs = _scan_coefs(av[:, :st], av[:, st:], reverse=False)

        def tile(b, carry):
            r0 = pl.multiple_of(b * SUBLANES, SUBLANES)
            xr, xi = _scan_tile(s_ref[pl.ds(r0, SUBLANES), :st], s_ref[pl.ds(r0, SUBLANES), st:], carry[0], carry[1],
                                coefs, False)
            s_ref[pl.ds(r0, SUBLANES), :st] = xr
            s_ref[pl.ds(r0, SUBLANES), st:] = xi
            return xr[SUBLANES - 1:, :], xi[SUBLANES - 1:, :]

        zero = jnp.zeros((1, st), F32)
        lax.fori_loop(0, rows // SUBLANES, tile, (zero, zero))
        for c0 in range(0, rows, chunk):
            y = _dot_nn(s_ref[pl.ds(c0, chunk), :].astype(BF16), wc_ref[...]) + d_ref[...] * u_ref[pl.ds(c0, chunk), :]
            y_ref[pl.ds(c0, chunk), :] = y
            yg_ref[pl.ds(c0, chunk), :] = _gelu(y).astype(BF16)

    return pl.pallas_call(
        body, name="s5_fwd", grid=(nb,),
        in_specs=[pl.BlockSpec((rows, LANES), lambda j: (0, j)), pl.BlockSpec((None, LANES, s2), lambda j: (j, 0, 0)),
                  pl.BlockSpec((None, s2, LANES), lambda j: (j, 0, 0)), pl.BlockSpec((1, LANES), lambda j: (0, j)),
                  pl.BlockSpec((None, 1, s2), lambda j: (j, 0, 0))],
        out_specs=[pl.BlockSpec((rows, s2), lambda j: (0, j)), pl.BlockSpec((rows, LANES), lambda j: (0, j)),
                   pl.BlockSpec((rows, LANES), lambda j: (0, j))],
        out_shape=[jax.ShapeDtypeStruct((rows, nb * s2), F32), jax.ShapeDtypeStruct((rows, nb * LANES), F32),
                   jax.ShapeDtypeStruct((rows, nb * LANES), BF16)],
        compiler_params=_params(("parallel",)),
    )(proj, wb, wc, d_skip, abar)


def _s5_bwd(proj, states, y_pre, dyg_a, dyg_b, wb, wc, d_skip, abar):
    rows = proj.shape[0]
    nb = wb.shape[0]
    s2 = 2 * STATE_PER_BATCH
    st = STATE_PER_BATCH
    chunk = _tile(rows, 512, SUBLANES)
    n_tiles = rows // SUBLANES

    def body(u_ref, s_ref, y_ref, ga_ref, gb_ref, wb_ref, wc_ref, d_ref, a_ref,
             du_ref, dwb_ref, dwc_ref, da_ref, dd_ref, ds_ref, dy_ref):
        dy_ref[...] = (ga_ref[...] + gb_ref[...]) * _gelu_grad(y_ref[...])
        dd_ref[...] = jnp.sum(dy_ref[...] * u_ref[...], axis=0, keepdims=True)
        for c0 in range(0, rows, chunk):
            ds_ref[pl.ds(c0, chunk), :] = _dot_nt(dy_ref[pl.ds(c0, chunk), :].astype(BF16), wc_ref[...])
        dwc_ref[...] = _dot_tn(s_ref[...].astype(BF16), dy_ref[...].astype(BF16))
        av = a_ref[...]
        coefs = _scan_coefs(av[:, :st], -av[:, st:], reverse=True)
        row = lax.broadcasted_iota(jnp.int32, (SUBLANES, st), 0)

        def tile(k, carry):
            cr, ci, acc_r, acc_i = carry
            b = n_tiles - 1 - k
            r0 = pl.multiple_of(b * SUBLANES, SUBLANES)
            rp = pl.multiple_of(jnp.maximum(b - 1, 0) * SUBLANES, SUBLANES)
            xr, xi = _scan_tile(ds_ref[pl.ds(r0, SUBLANES), :st], ds_ref[pl.ds(r0, SUBLANES), st:], cr, ci, coefs, True)
            ds_ref[pl.ds(r0, SUBLANES), :st] = xr
            ds_ref[pl.ds(r0, SUBLANES), st:] = xi
            first = jnp.where(b > 0, 1.0, 0.0)
            pr = jnp.where(row == 0, pltpu.roll(s_ref[pl.ds(rp, SUBLANES), :st], 1, 0) * first,
                           pltpu.roll(s_ref[pl.ds(r0, SUBLANES), :st], 1, 0))
            pi = jnp.where(row == 0, pltpu.roll(s_ref[pl.ds(rp, SUBLANES), st:], 1, 0) * first,
                           pltpu.roll(s_ref[pl.ds(r0, SUBLANES), st:], 1, 0))
            acc_r = acc_r + pr * xr + pi * xi
            acc_i = acc_i + pr * xi - pi * xr
            return xr[:1, :], xi[:1, :], acc_r, acc_i

        zero = jnp.zeros((1, st), F32)
        zacc = jnp.zeros((SUBLANES, st), F32)
        _, _, acc_r, acc_i = lax.fori_loop(0, n_tiles, tile, (zero, zero, zacc, zacc))
        da_ref[:, :st] = jnp.sum(acc_r, axis=0, keepdims=True)
        da_ref[:, st:] = jnp.sum(acc_i, axis=0, keepdims=True)
        for c0 in range(0, rows, chunk):
            du_ref[pl.ds(c0, chunk), :] = (_dot_nt(ds_ref[pl.ds(c0, chunk), :].astype(BF16), wb_ref[...])
                                           + d_ref[...] * dy_ref[pl.ds(c0, chunk), :]).astype(du_ref.dtype)
        dwb_ref[...] = _dot_tn(u_ref[...].astype(BF16), ds_ref[...].astype(BF16))

    col = pl.BlockSpec((rows, LANES), lambda j: (0, j))
    return pl.pallas_call(
        body, name="s5_bwd", grid=(nb,),
        in_specs=[col, pl.BlockSpec((rows, s2), lambda j: (0, j)), col, col, col,
                  pl.BlockSpec((None, LANES, s2), lambda j: (j, 0, 0)), pl.BlockSpec((None, s2, LANES), lambda j: (j, 0, 0)),
                  pl.BlockSpec((1, LANES), lambda j: (0, j)), pl.BlockSpec((None, 1, s2), lambda j: (j, 0, 0))],
        out_specs=[col, pl.BlockSpec((None, LANES, s2), lambda j: (j, 0, 0)),
                   pl.BlockSpec((None, s2, LANES), lambda j: (j, 0, 0)), pl.BlockSpec((None, 1, s2), lambda j: (j, 0, 0)),
                   pl.BlockSpec((1, LANES), lambda j: (0, j))],
        out_shape=[jax.ShapeDtypeStruct((rows, nb * LANES), BF16), jax.ShapeDtypeStruct((nb, LANES, s2), F32),
                   jax.ShapeDtypeStruct((nb, s2, LANES), F32), jax.ShapeDtypeStruct((nb, 1, s2), F32),
                   jax.ShapeDtypeStruct((1, nb * LANES), F32)],
        scratch_shapes=[pltpu.VMEM((rows, s2), F32), pltpu.VMEM((rows, LANES), F32)],
        compiler_params=_params(("parallel",)),
    )(proj, states, y_pre, dyg_a, dyg_b, wb, wc, d_skip, abar)


def _glu_norm_fwd(y_pre, z, w, *, tr=256):
    rows, width = y_pre.shape
    tr = _tile(rows, tr, SUBLANES)

    def body(y_ref, z_ref, w_ref, o_ref):
        v = _gelu(y_ref[...]) * jax.nn.sigmoid(z_ref[...])
        o_ref[...] = (v * _rms_rows(v) * w_ref[...]).astype(o_ref.dtype)

    blk = pl.BlockSpec((tr, width), lambda i: (i, 0))
    return pl.pallas_call(
        body, name="glu_norm_fwd", grid=(rows // tr,),
        in_specs=[blk, blk, pl.BlockSpec((1, width), lambda i: (0, 0))], out_specs=blk,
        out_shape=jax.ShapeDtypeStruct((rows, width), BF16), compiler_params=_params(("parallel",)),
    )(y_pre, z, w)


def _glu_norm_bwd(y_pre, z, w, dycat, *, tr=256):
    rows, width = y_pre.shape
    tr = _tile(rows, tr, SUBLANES)

    def body(y_ref, z_ref, w_ref, dy_ref, dz_ref, dg_ref, dw_ref, db_ref):
        yg = _gelu(y_ref[...])
        sg = jax.nn.sigmoid(z_ref[...])
        dv, dwp = _rmsnorm_bwd_rows(yg * sg, w_ref[...], dy_ref[...])
        dz = dv * yg * sg * (1.0 - sg)
        dz_ref[...] = dz.astype(dz_ref.dtype)
        dg_ref[...] = dv * sg
        dw_part = jnp.sum(dwp, axis=0, keepdims=True)
        db_part = jnp.sum(dz, axis=0, keepdims=True)

        @pl.when(pl.program_id(0) == 0)
        def _():
            dw_ref[...] = dw_part
            db_ref[...] = db_part

        @pl.when(pl.program_id(0) > 0)
        def _():
            dw_ref[...] += dw_part
            db_ref[...] += db_part

    blk = pl.BlockSpec((tr, width), lambda i: (i, 0))
    vec = pl.BlockSpec((1, width), lambda i: (0, 0))
    return pl.pallas_call(
        body, name="glu_norm_bwd", grid=(rows // tr,), in_specs=[blk, blk, vec, blk], out_specs=[blk, blk, vec, vec],
        out_shape=[jax.ShapeDtypeStruct((rows, width), BF16), jax.ShapeDtypeStruct((rows, width), F32)]
        + [jax.ShapeDtypeStruct((1, width), F32)] * 2,
        compiler_params=_params(("arbitrary",)),
    )(y_pre, z, w, dycat)


def _rope_tables(pos, freq, sign):
    rows = pos.shape[0]

    def body(p_ref, f_ref, s_ref, cos_ref, sin_ref):
        ang = p_ref[...] * f_ref[...]
        cos_ref[...] = jnp.cos(ang)
        sin_ref[...] = jnp.sin(ang) * s_ref[...]

    return pl.pallas_call(body, name="rope_tables", out_shape=[jax.ShapeDtypeStruct((rows, LANES), F32)] * 2)(pos, freq, sign)


def _rope(x, cos, sin_signed):
    lane = lax.broadcasted_iota(jnp.int32, x.shape, 1)
    half = QK_ROPE_DIM // 2
    swapped = jnp.where(lane < half, pltpu.roll(x, LANES - half, 1), pltpu.roll(x, half, 1))
    return x * cos + swapped * sin_signed


def _attn_prep(q, kv, proj, kpe_col, cos, sin, *, tr=256):
    rows = q.shape[0]
    heads = q.shape[1] // HEAD_SLOT
    tr = _tile(rows, tr, SUBLANES)

    def body(q_ref, kv_ref, kpe_ref, cos_ref, sin_ref, qc_ref, kc_ref, v_ref):
        c, s = cos_ref[...], sin_ref[...]
        qc_ref[:, :LANES] = q_ref[:, :LANES].astype(BF16)
        qc_ref[:, LANES:] = _rope(q_ref[:, LANES:], c, s).astype(BF16)
        kc_ref[:, :LANES] = kv_ref[:, :LANES].astype(BF16)
        kc_ref[:, LANES:] = _rope(kpe_ref[...], c, s).astype(BF16)
        v_ref[...] = kv_ref[:, LANES:].astype(BF16)

    slot = pl.BlockSpec((tr, HEAD_SLOT), lambda i, h: (i, h))
    tab = pl.BlockSpec((tr, LANES), lambda i, h: (i, 0))
    return pl.pallas_call(
        body, name="attn_prep", grid=(rows // tr, heads),
        in_specs=[slot, slot, pl.BlockSpec((tr, LANES), lambda i, h: (i, kpe_col)), tab, tab],
        out_specs=[slot, slot, pl.BlockSpec((tr, LANES), lambda i, h: (i, h))],
        out_shape=[jax.ShapeDtypeStruct((rows, heads * HEAD_SLOT), BF16)] * 2
        + [jax.ShapeDtypeStruct((rows, heads * LANES), BF16)],
        compiler_params=_params(("parallel", "parallel")),
    )(q, kv, proj, cos, sin)


def _causal(tq, tk):
    return lax.broadcasted_iota(jnp.int32, (tq, tk), 1) <= lax.broadcasted_iota(jnp.int32, (tq, tk), 0)


def _attn_fwd(qc, kc, vb, *, scale, tq=512):
    rows = qc.shape[0]
    heads = qc.shape[1] // HEAD_SLOT
    tq = _tile(rows, tq, SUBLANES)
    tk = tq

    def body(q_ref, k_ref, v_ref, o_ref, lse_ref):
        i = pl.program_id(1)
        q = q_ref[...]

        def step(j, carry, diagonal):
            m, l, acc = carry
            k0 = pl.multiple_of(j * tk, tk)
            s = _dot_nt(q, k_ref[pl.ds(k0, tk), :]) * scale
            if diagonal:
                s = jnp.where(_causal(tq, tk), s, NEG_INF)
            m_new = jnp.maximum(m, jnp.max(s, axis=-1, keepdims=True))
            p = jnp.exp(s - m_new)
            alpha = jnp.exp(m - m_new)
            l = alpha * l + jnp.sum(p, axis=-1, keepdims=True)
            acc = alpha * acc + _dot_nn(p.astype(BF16), v_ref[pl.ds(k0, tk), :])
            return m_new, l, acc

        init = (jnp.full((tq, 1), NEG_INF, F32), jnp.zeros((tq, 1), F32), jnp.zeros((tq, LANES), F32))
        below = lax.fori_loop(0, i, lambda j, carry: step(j, carry, False), init)
        m, l, acc = step(i, below, True)
        o_ref[...] = acc / l
        lse_ref[...] = jnp.broadcast_to(m + jnp.log(l), (tq, LANES))

    return pl.pallas_call(
        body, name="attn_fwd", grid=(heads, rows // tq),
        in_specs=[pl.BlockSpec((tq, HEAD_SLOT), lambda h, i: (i, h)), pl.BlockSpec((rows, HEAD_SLOT), lambda h, i: (0, h)),
                  pl.BlockSpec((rows, LANES), lambda h, i: (0, h))],
        out_specs=[pl.BlockSpec((tq, LANES), lambda h, i: (i, h))] * 2,
        out_shape=[jax.ShapeDtypeStruct((rows, heads * LANES), F32)] * 2,
        compiler_params=_params(("parallel", "parallel")),
    )(qc, kc, vb)


def _attn_bwd(qc, kc, vb, o, do, lse, cos, sin, *, scale, tk=512):
    rows = qc.shape[0]
    heads = qc.shape[1] // HEAD_SLOT
    tk = _tile(rows, tk, SUBLANES)
    tq = tk
    nq = rows // tq

    def body(q_ref, k_ref, v_ref, o_ref, do_ref, lse_ref, cos_ref, sin_ref, dq_ref, dkv_ref, dkpe_ref, dq_acc, delta_ref):
        j = pl.program_id(1)

        @pl.when(j == 0)
        def _():
            dq_acc[...] = jnp.zeros_like(dq_acc)
            for r0 in range(0, rows, tq):
                d = jnp.sum(do_ref[pl.ds(r0, tq), :] * o_ref[pl.ds(r0, tq), :], axis=-1, keepdims=True)
                delta_ref[pl.ds(r0, tq), :] = jnp.broadcast_to(d, (tq, LANES))

        kb, vv = k_ref[...], v_ref[...]

        def step(i, carry, diagonal):
            dk, dv = carry
            q0 = pl.multiple_of(i * tq, tq)
            qb = q_ref[pl.ds(q0, tq), :]
            dob = do_ref[pl.ds(q0, tq), :].astype(BF16)
            s = _dot_nt(qb, kb) * scale
            p = jnp.exp(s - lse_ref[pl.ds(q0, tq), :1])
            if diagonal:
                p = jnp.where(_causal(tq, tk), p, 0.0)
            dv = dv + _dot_tn(p.astype(BF16), dob)
            ds = (p * (_dot_nt(dob, vv) - delta_ref[pl.ds(q0, tq), :1])).astype(BF16)
            dk = dk + _dot_tn(ds, qb)
            dq_acc[pl.ds(q0, tq), :] += _dot_nn(ds, kb)
            return dk, dv

        zero = (jnp.zeros((tk, HEAD_SLOT), F32), jnp.zeros((tk, LANES), F32))
        dk, dv = lax.fori_loop(j + 1, nq, lambda i, carry: step(i, carry, False), step(j, zero, True))
        dkv_ref[:, :LANES] = (dk[:, :LANES] * scale).astype(dkv_ref.dtype)
        dkv_ref[:, LANES:] = dv.astype(dkv_ref.dtype)
        dkpe_ref[...] = dk[:, LANES:] * scale

        @pl.when(j == nq - 1)
        def _():
            for r0 in range(0, rows, tq):
                dq = dq_acc[pl.ds(r0, tq), :] * scale
                dq_ref[pl.ds(r0, tq), :LANES] = dq[:, :LANES].astype(dq_ref.dtype)
                dq_ref[pl.ds(r0, tq), LANES:] = _rope(dq[:, LANES:], cos_ref[pl.ds(r0, tq), :],
                                                      -sin_ref[pl.ds(r0, tq), :]).astype(dq_ref.dtype)

    full_q = pl.BlockSpec((rows, HEAD_SLOT), lambda h, j: (0, h))
    full_v = pl.BlockSpec((rows, LANES), lambda h, j: (0, h))
    tab = pl.BlockSpec((rows, LANES), lambda h, j: (0, 0))
    return pl.pallas_call(
        body, name="attn_bwd", grid=(heads, rows // tk),
        in_specs=[full_q, pl.BlockSpec((tk, HEAD_SLOT), lambda h, j: (j, h)), pl.BlockSpec((tk, LANES), lambda h, j: (j, h)),
                  full_v, full_v, full_v, tab, tab],
        out_specs=[full_q, pl.BlockSpec((tk, HEAD_SLOT), lambda h, j: (j, h)), pl.BlockSpec((tk, LANES), lambda h, j: (j, h))],
        out_shape=[jax.ShapeDtypeStruct((rows, heads * HEAD_SLOT), BF16), jax.ShapeDtypeStruct((rows, heads * HEAD_SLOT), BF16),
                   jax.ShapeDtypeStruct((rows, heads * LANES), F32)],
        scratch_shapes=[pltpu.VMEM((rows, HEAD_SLOT), F32), pltpu.VMEM((rows, LANES), F32)],
        compiler_params=_params(("parallel", "arbitrary")),
    )(qc, kc, vb, o, do, lse, cos, sin)


def _kpe_bwd(dkpe_heads, cos, sin, *, tr=512):
    rows = dkpe_heads.shape[0]
    heads = dkpe_heads.shape[1] // LANES
    tr = _tile(rows, tr, 2 * SUBLANES)

    def body(d_ref, cos_ref, sin_ref, o_ref):
        acc = d_ref[:, :LANES]
        for h in range(1, heads):
            acc = acc + d_ref[:, h * LANES:(h + 1) * LANES]
        o_ref[...] = _rope(acc, cos_ref[...], -sin_ref[...]).astype(o_ref.dtype)

    tab = pl.BlockSpec((tr, LANES), lambda i: (i, 0))
    return pl.pallas_call(
        body, name="kpe_bwd", grid=(rows // tr,),
        in_specs=[pl.BlockSpec((tr, heads * LANES), lambda i: (i, 0)), tab, tab], out_specs=tab,
        out_shape=jax.ShapeDtypeStruct((rows, LANES), BF16), compiler_params=_params(("parallel",)),
    )(dkpe_heads, cos, sin)


CONV_ROWS = 128


def _with_halo(ref, r0, ci, n_chunks, ch, lanes, before, after):
    parts = []
    if before:
        lo = pl.multiple_of(jnp.maximum(r0 - SUBLANES, 0), SUBLANES)
        parts.append(ref[pl.ds(lo, SUBLANES), lanes] * jnp.where(ci > 0, 1.0, 0.0))
    parts.append(ref[pl.ds(r0, ch), lanes])
    if after:
        hi = pl.multiple_of(jnp.minimum(r0 + ch, n_chunks * ch - SUBLANES), SUBLANES)
        parts.append(ref[pl.ds(hi, SUBLANES), lanes] * jnp.where(ci < n_chunks - 1, 1.0, 0.0))
    return jnp.concatenate(parts, axis=0)


def _taps(ext):
    return pltpu.roll(ext, 2, 0)[SUBLANES:], pltpu.roll(ext, 1, 0)[SUBLANES:], ext[SUBLANES:]


def _conv3(taps, w, b):
    return w[0:1, :] * taps[0] + w[1:2, :] * taps[1] + w[2:3, :] * taps[2] + b


def _conv_gate_fwd(a, conv_w, conv_b, *, tc=256):
    rows, f2 = a.shape
    f = f2 // 2
    tc = _tile(f, tc)
    nc = f // tc
    ch = _tile(rows, CONV_ROWS, SUBLANES)
    n_chunks = rows // ch

    def body(ag_ref, av_ref, wg_ref, wv_ref, bg_ref, bv_ref, o_ref):
        for lt in range(tc // LANES):
            lanes = slice(lt * LANES, (lt + 1) * LANES)
            wg, wv, bg, bv = wg_ref[:, lanes], wv_ref[:, lanes], bg_ref[:, lanes], bv_ref[:, lanes]

            def chunk(ci, carry):
                r0 = pl.multiple_of(ci * ch, ch)
                gate = _conv3(_taps(_with_halo(ag_ref, r0, ci, n_chunks, ch, lanes, True, False)), wg, bg)
                val = _conv3(_taps(_with_halo(av_ref, r0, ci, n_chunks, ch, lanes, True, False)), wv, bv)
                o_ref[pl.ds(r0, ch), lanes] = (gate * jax.nn.sigmoid(gate) * val).astype(o_ref.dtype)
                return carry

            lax.fori_loop(0, n_chunks, chunk, 0)

    return pl.pallas_call(
        body, name="conv_gate_fwd", grid=(nc,),
        in_specs=[pl.BlockSpec((rows, tc), lambda j: (0, j)), pl.BlockSpec((rows, tc), lambda j: (0, j + nc)),
                  pl.BlockSpec((SUBLANES, tc), lambda j: (0, j)), pl.BlockSpec((SUBLANES, tc), lambda j: (0, j + nc)),
                  pl.BlockSpec((1, tc), lambda j: (0, j)), pl.BlockSpec((1, tc), lambda j: (0, j + nc))],
        out_specs=pl.BlockSpec((rows, tc), lambda j: (0, j)),
        out_shape=jax.ShapeDtypeStruct((rows, f), BF16), compiler_params=_params(("parallel",)),
    )(a, a, conv_w, conv_w, conv_b, conv_b)


def _conv_gate_bwd(a, conv_w, conv_b, dg, *, tc=256):
    rows, f2 = a.shape
    f = f2 // 2
    tc = _tile(f, tc)
    nc = f // tc
    ch = _tile(rows, CONV_ROWS, SUBLANES)
    n_chunks = rows // ch
    ext_rows = ch + SUBLANES

    def fold(x):
        return jnp.sum(x.reshape(ch // SUBLANES, SUBLANES, LANES), axis=0)

    def body(ag_ref, av_ref, wg_ref, wv_ref, bg_ref, bv_ref, dg_ref, da_ref, dw_ref, db_ref):
        for lt in range(tc // LANES):
            lanes = slice(lt * LANES, (lt + 1) * LANES)
            wg, wv, bg, bv = wg_ref[:, lanes], wv_ref[:, lanes], bg_ref[:, lanes], bv_ref[:, lanes]

            def chunk(ci, acc):
                r0 = pl.multiple_of(ci * ch, ch)
                taps_g = _taps(_with_halo(ag_ref, r0, ci, n_chunks, ch, lanes, True, True))
                taps_v = _taps(_with_halo(av_ref, r0, ci, n_chunks, ch, lanes, True, True))
                dge = _with_halo(dg_ref, r0, ci, n_chunks, ch, lanes, False, True)
                gate, val = _conv3(taps_g, wg, bg), _conv3(taps_v, wv, bv)
                sg = jax.nn.sigmoid(gate)
                d_gate = dge * val * sg * (1.0 + gate * (1.0 - sg))
                d_val = dge * gate * sg
                new = []
                for half, (taps, w, d) in enumerate(((taps_g, wg, d_gate), (taps_v, wv, d_val))):
                    da = (w[2:3, :] * d[:ch] + w[1:2, :] * pltpu.roll(d, ext_rows - 1, 0)[:ch]
                          + w[0:1, :] * pltpu.roll(d, ext_rows - 2, 0)[:ch])
                    da_ref[half, pl.ds(r0, ch), lanes] = da.astype(da_ref.dtype)
                    dc = d[:ch]
                    sums = [fold(dc)] + [fold(dc * t[:ch]) for t in taps]
                    new.append(tuple(x + s for x, s in zip(acc[half], sums)))
                return tuple(new)

            zero = tuple(jnp.zeros((SUBLANES, LANES), F32) for _ in range(4))
            acc = lax.fori_loop(0, n_chunks, chunk, (zero, zero))
            row = lax.broadcasted_iota(jnp.int32, (SUBLANES, LANES), 0)
            for half in range(2):
                db, *taps = (jnp.sum(x, axis=0, keepdims=True) for x in acc[half])
                db_ref[half, :, lanes] = db
                dw = jnp.zeros((SUBLANES, LANES), F32)
                for tap in range(3):
                    dw = jnp.where(row == tap, taps[tap], dw)
                dw_ref[half, :, lanes] = dw

    lo = lambda j: (0, j)
    hi = lambda j: (0, j + nc)
    both = lambda j: (0, 0, j)
    return pl.pallas_call(
        body, name="conv_gate_bwd", grid=(nc,),
        in_specs=[pl.BlockSpec((rows, tc), lo), pl.BlockSpec((rows, tc), hi), pl.BlockSpec((SUBLANES, tc), lo),
                  pl.BlockSpec((SUBLANES, tc), hi), pl.BlockSpec((1, tc), lo), pl.BlockSpec((1, tc), hi),
                  pl.BlockSpec((rows, tc), lo)],
        out_specs=[pl.BlockSpec((2, rows, tc), both), pl.BlockSpec((2, SUBLANES, tc), both), pl.BlockSpec((2, 1, tc), both)],
        out_shape=[jax.ShapeDtypeStruct((2, rows, f), BF16), jax.ShapeDtypeStruct((2, SUBLANES, f), F32),
                   jax.ShapeDtypeStruct((2, 1, f), F32)],
        compiler_params=_params(("parallel",)),
    )(a, a, conv_w, conv_w, conv_b, conv_b, dg)


def _wgrad(a, b, rows, cols, row_sharded, name, **kw):
    make, (sr, sc) = _wgrad_blocks(rows, cols, row_sharded)
    tm = kw.pop("tm", _tile(sr, 512))
    tn = kw.pop("tn", _tile(sc, 1024))
    return _matmul(a, b, mode="tn", name=name, tm=tm, tn=tn, out_blocks=make, **kw)


def _block_diag(x):
    nb, g, r, c = x.shape
    eye = jnp.eye(g, dtype=x.dtype)
    return (x[:, :, :, None, :] * eye[None, :, None, :, None]).reshape(nb, g * r, g * c)


def _block_diag_part(x, r, c):
    nb = x.shape[0]
    g = GROUPS_PER_BATCH
    eye = jnp.eye(g, dtype=x.dtype)
    return jnp.sum(x.reshape(nb, g, r, g, c) * eye[None, :, None, :, None], axis=3)


class _NoExchange:
    def __init__(self, ffn):
        self.ffn = ffn

    def ffn_weights_arrived(self, after):
        return None

    def ffn_weights(self, after):
        return self.ffn

    def ffn_grads(self, g_down, g_up, after):
        return None

    def ffn_backward_done(self, after):
        return None


def _local_step(x, posf, target, w, hooks):
    rows, d = x.shape
    width = w["ssm_d"].shape[1]
    qr, kvr = w["mla_q_norm_w"].shape[1], w["mla_kv_norm_w"].shape[1]
    heads = w["mla_w_ukv"].shape[1] // HEAD_SLOT
    f2 = w["ffn_conv_b"].shape[1]
    inp = w["w_in"].shape[0]
    groups = width // SSM_GROUP
    nb = groups // GROUPS_PER_BATCH
    scale = (QK_NOPE_DIM + QK_ROPE_DIM) ** -0.5
    g = {}

    hn = _rmsnorm_fwd(x, w["attn_norm_w"], name="attn_norm")
    proj = _matmul(hn, w["w_in"], mode="nt", name="in_proj")

    ar, ai, bbr, bbi = _s5_params(w["ssm_lambda_re"], w["ssm_lambda_im"], w["ssm_log_dt"], w["ssm_b_re"], w["ssm_b_im"])

    def b_band(bb):
        return _block_diag(bb.reshape(nb, GROUPS_PER_BATCH, SSM_STATE, SSM_GROUP).transpose(0, 1, 3, 2))

    def c_band(c):
        return _block_diag(c.reshape(nb, GROUPS_PER_BATCH, SSM_GROUP, SSM_STATE).transpose(0, 1, 3, 2))

    wb = jnp.concatenate([b_band(bbr), b_band(bbi)], axis=2).astype(BF16)
    wc = jnp.concatenate([c_band(w["ssm_c_re"]), -c_band(w["ssm_c_im"])], axis=1).astype(BF16)
    abar = jnp.concatenate([ar.reshape(nb, 1, STATE_PER_BATCH), ai.reshape(nb, 1, STATE_PER_BATCH)], axis=2)
    states, y_pre, yg = _s5_fwd(proj, wb, wc, w["ssm_d"], abar)
    z = _matmul(yg, w["ssm_w_glu"], mode="nn", name="glu_proj", bias=w["ssm_b_glu"])
    ys = _glu_norm_fwd(y_pre, z, w["ssm_out_norm_w"])

    q_col, kv_col, kpe_col = width // qr, (width + qr) // kvr, (width + qr + kvr) // LANES
    assert width % qr == 0 and (width + qr) % kvr == 0
    qn = _rmsnorm_fwd(proj, w["mla_q_norm_w"], name="q_norm", width=qr, col=q_col)
    kvn = _rmsnorm_fwd(proj, w["mla_kv_norm_w"], name="kv_norm", width=kvr, col=kv_col)
    q = _matmul(qn, w["mla_w_uq"], mode="nn", name="q_proj")
    kv = _matmul(kvn, w["mla_w_ukv"], mode="nn", name="kv_proj")
    half = QK_ROPE_DIM // 2
    inv_freq = ROPE_THETA ** (-jnp.arange(0, QK_ROPE_DIM, 2, dtype=F32) / QK_ROPE_DIM)
    zeros = jnp.zeros((LANES - QK_ROPE_DIM,), F32)
    freq = jnp.concatenate([inv_freq, inv_freq, zeros]).reshape(1, LANES)
    sign = jnp.concatenate([-jnp.ones((half,), F32), jnp.ones((half,), F32), zeros]).reshape(1, LANES)
    cos, sin = _rope_tables(posf, freq, sign)
    qc, kc, vb = _attn_prep(q, kv, proj, kpe_col, cos, sin)
    o, lse = _attn_fwd(qc, kc, vb, scale=scale, tq=ATTN_BLOCK)
    ym = _rmsnorm_fwd(o, w["mla_out_norm_w"], name="mla_out_norm")
    ycat = jnp.concatenate([ys, ym], axis=1)
    h1 = _matmul(ycat, w["w_out"], mode="nn", name="out_proj", add=x, after=hooks.ffn_weights_arrived(ycat))

    hn2 = _rmsnorm_fwd(h1, w["ffn_norm_w"], name="ffn_norm")
    ffn = hooks.ffn_weights(hn2)
    a = _matmul(hn2, ffn["ffn_w_up"], mode="nn", name="ffn_up", tm=FFN_ROWS)
    gated = _conv_gate_fwd(a, ffn["ffn_conv_w"], w["ffn_conv_b"])
    h2 = _matmul(gated, ffn["ffn_w_down"], mode="nn", name="ffn_down", add=h1, tk=2816, tm=FFN_ROWS)
    loss_tile, dh2, dh2_mxu, g["final_norm_w"] = _final_norm_loss(h2, w["final_norm_w"], target)

    dgated = _matmul(dh2_mxu, ffn["ffn_w_down"], mode="nt", name="ffn_down_dx", tm=FFN_ROWS)
    g["ffn_w_down"] = _wgrad(gated, dh2_mxu, f2 // 2, d, True, "ffn_down_dw", tm=f2 // 2 // N_CHIPS, tn=1024)
    da, dcw, dcb = _conv_gate_bwd(a, ffn["ffn_conv_w"], w["ffn_conv_b"], dgated)
    g["ffn_conv_w"] = jnp.concatenate([dcw[0, :3], dcw[1, :3]], axis=1)
    g["ffn_conv_b"] = jnp.concatenate([dcb[0], dcb[1]], axis=1)
    g["ffn_w_up"] = _wgrad(hn2, da, d, f2, False, "ffn_up_dw", b_split=True, tm=FFN_ROWS, tn=_tile(f2 // N_CHIPS, 1408))
    started = hooks.ffn_grads(g["ffn_w_down"], g["ffn_w_up"], dcb)
    dhn2 = _matmul(da, ffn["ffn_w_up"], mode="nt", name="ffn_up_dx", a_split=True, tk=_tile(f2 // 2, 2816), tm=FFN_ROWS,
                   after=started)
    dh1, dh1_mxu, g["ffn_norm_w"] = _rmsnorm_bwd(h1, w["ffn_norm_w"], dhn2, name="ffn_norm_bwd", add=dh2,
                                                dx_dtypes=(F32, BF16))

    dycat = _matmul(dh1_mxu, w["w_out"], mode="nt", name="out_proj_dx")
    g["w_out"] = _wgrad(ycat, dh1_mxu, 2 * width, d, True, "out_proj_dw")
    started = hooks.ffn_backward_done(dycat)
    mla_out_norm_w, ssm_out_norm_w = w["mla_out_norm_w"], w["ssm_out_norm_w"]
    if started is not None:
        mla_out_norm_w, ssm_out_norm_w = mla_out_norm_w + started[:1, :1], ssm_out_norm_w + started[:1, :1]

    do, g["mla_out_norm_w"] = _rmsnorm_bwd(o, mla_out_norm_w, dycat, name="mla_out_norm_bwd", width=width, dy_col=1)
    dq, dkv, dkpe_heads = _attn_bwd(qc, kc, vb, o, do, lse, cos, sin, scale=scale, tk=ATTN_BLOCK)
    dkpe = _kpe_bwd(dkpe_heads, cos, sin)
    g["mla_w_uq"] = _wgrad(qn, dq, qr, heads * HEAD_SLOT, False, "q_proj_dw")
    dqn = _matmul(dq, w["mla_w_uq"], mode="nt", name="q_proj_dx")
    dcq, g["mla_q_norm_w"] = _rmsnorm_bwd(proj, w["mla_q_norm_w"], dqn, name="q_norm_bwd", width=qr, col=q_col,
                                          dx_dtypes=(BF16,))
    g["mla_w_ukv"] = _wgrad(kvn, dkv, kvr, heads * HEAD_SLOT, False, "kv_proj_dw")
    dkvn = _matmul(dkv, w["mla_w_ukv"], mode="nt", name="kv_proj_dx")
    dckv, g["mla_kv_norm_w"] = _rmsnorm_bwd(proj, w["mla_kv_norm_w"], dkvn, name="kv_norm_bwd", width=kvr, col=kv_col,
                                            dx_dtypes=(BF16,))

    dz, dyg_a, g["ssm_out_norm_w"], g["ssm_b_glu"] = _glu_norm_bwd(y_pre, z, ssm_out_norm_w, dycat)
    dyg_b = _matmul(dz, w["ssm_w_glu"], mode="nt", name="glu_proj_dx")
    g["ssm_w_glu"] = _wgrad(yg, dz, width, width, True, "glu_proj_dw")
    du, dwb, dwc, dabar, g["ssm_d"] = _s5_bwd(proj, states, y_pre, dyg_a, dyg_b, wb, wc, w["ssm_d"], abar)

    def b_unband(x):
        return _block_diag_part(x, SSM_GROUP, SSM_STATE).transpose(0, 1, 3, 2).reshape(groups, SSM_STATE * SSM_GROUP)

    def c_unband(x):
        return _block_diag_part(x, SSM_STATE, SSM_GROUP).transpose(0, 1, 3, 2).reshape(groups, SSM_GROUP, SSM_STATE)

    st = STATE_PER_BATCH
    g["ssm_c_re"] = c_unband(dwc[:, :st, :])
    g["ssm_c_im"] = -c_unband(dwc[:, st:, :])
    d_ar = dabar[:, 0, :st].reshape(groups, SSM_STATE)
    d_ai = dabar[:, 0, st:].reshape(groups, SSM_STATE)
    (g["ssm_lambda_re"], g["ssm_lambda_im"], g["ssm_log_dt"], g["ssm_b_re"], g["ssm_b_im"]) = _s5_params_bwd(
        w["ssm_lambda_re"], w["ssm_lambda_im"], w["ssm_log_dt"], w["ssm_b_re"], w["ssm_b_im"], d_ar, d_ai,
        b_unband(dwb[:, :, :st]), b_unband(dwb[:, :, st:]))

    pad = jnp.zeros((rows, inp - (width + qr + kvr + LANES)), BF16)
    dproj = jnp.concatenate([du, dcq, dckv, dkpe, pad], axis=1)
    g["w_in"] = _wgrad(dproj, hn, inp, d, False, "in_proj_dw")
    dhn = _matmul(dproj, w["w_in"], mode="nn", name="in_proj_dx")
    dx, g["attn_norm_w"] = _rmsnorm_bwd(x, w["attn_norm_w"], dhn, name="attn_norm_bwd", add=dh1)
    return loss_tile, dx, g


ANY = pl.BlockSpec(memory_space=pl.ANY)
MESH = pl.DeviceIdType.MESH


def _mesh_pos():
    return lax.axis_index("x"), lax.axis_index("y"), lax.axis_index("c")


def _other_chips(x, y):
    return [(1 - x, y), (x, 1 - y), (1 - x, 1 - y)]


def _remote(src, dst, send_sems, recv_sems, k, to):
    return pltpu.make_async_remote_copy(src_ref=src, dst_ref=dst, send_sem=send_sems.at[k], recv_sem=recv_sems.at[k],
                                        device_id=to, device_id_type=MESH)


def _place_shard(shard, piece_idx, row_sharded, name, out_dtype=BF16, pieces=N_CHIPS):
    rs, cs = shard.shape
    tr = _tile(rs, 256, 2 * SUBLANES)
    rb = rs // tr

    def body(p_ref, x_ref, o_ref):
        o_ref[...] = x_ref[...].astype(o_ref.dtype)

    if row_sharded:
        out_shape, out_map = (pieces * rs, cs), (lambda i, p_ref: (p_ref[0] * rb + i, 0))
    else:
        out_shape, out_map = (rs, pieces * cs), (lambda i, p_ref: (i, p_ref[0]))
    return pl.pallas_call(
        body, name=name, out_shape=jax.ShapeDtypeStruct(out_shape, out_dtype),
        grid_spec=pltpu.PrefetchScalarGridSpec(
            num_scalar_prefetch=1, grid=(rb,), in_specs=[pl.BlockSpec((tr, cs), lambda i, p_ref: (i, 0))],
            out_specs=pl.BlockSpec((tr, cs), out_map)),
        compiler_params=_params(("parallel",)),
    )(piece_idx, shard)


def _gather_weights(placed, name):
    n = len(placed)
    meta = [(row_sharded, direct) for _, row_sharded, direct in placed]
    over_ici, over_d2d = _gather_plans(meta)
    forwarded = [t for t, (_, direct) in enumerate(meta) if not direct]

    def body(*refs):
        outs = refs[n:2 * n]
        send_sems, recv_sems, pass_send_sems, pass_recv_sems = refs[2 * n:]
        first, arrivals = over_ici(outs, send_sems, recv_sems)
        passed, passed_arrivals = over_d2d([outs[t] for t in forwarded], pass_send_sems, pass_recv_sems)
        for cp in first:
            cp.start()
        for t in range(n):
            for j in range(3):
                arrivals[3 * t + j].wait_recv()
                if t in forwarded:
                    passed[3 * forwarded.index(t) + j].start()
        for cp in passed_arrivals:
            cp.wait_recv()
        for cp in first + passed:
            cp.wait_send()

    return pl.pallas_call(
        body, name=name, in_specs=[ANY] * n, out_specs=[ANY] * n,
        out_shape=[jax.ShapeDtypeStruct(arr.shape, arr.dtype) for arr, _, _ in placed],
        input_output_aliases={t: t for t in range(n)},
        scratch_shapes=[pltpu.SemaphoreType.DMA((3 * n,)), pltpu.SemaphoreType.DMA((3 * n,)),
                        pltpu.SemaphoreType.DMA((3 * len(forwarded),)), pltpu.SemaphoreType.DMA((3 * len(forwarded),))],
    )(*[arr for arr, _, _ in placed])


def _gather_plans(meta):
    def window(ref, row_sharded, piece, half):
        r, cc = ref.shape
        if row_sharded:
            rs = r // N_CHIPS
            if half is None:
                return ref.at[pl.ds(piece * rs, rs), :]
            return ref.at[pl.ds(piece * rs + half * (rs // 2), rs // 2), :]
        cs = cc // N_CHIPS
        if half is None:
            return ref.at[:, pl.ds(piece * cs, cs)]
        return ref.at[pl.ds(half * (r // 2), r // 2), pl.ds(piece * cs, cs)]

    def over_ici(refs, send_sems, recv_sems):
        x, y, c = _mesh_pos()
        sends, recvs = [], []
        for t, (row_sharded, direct) in enumerate(meta):
            mine = window(refs[t], row_sharded, 2 * x + y, None if direct else c)
            for j, (px, py) in enumerate(_other_chips(x, y)):
                theirs = window(refs[t], row_sharded, 2 * px + py, None if direct else c)
                sends.append(_remote(mine, mine, send_sems, recv_sems, 3 * t + j, (px, py, c)))
                recvs.append(_remote(theirs, theirs, send_sems, recv_sems, 3 * t + j, (px, py, c)))
        return sends, recvs

    def over_d2d(refs, send_sems, recv_sems):
        x, y, c = _mesh_pos()
        sends, recvs = [], []
        rows = [row_sharded for row_sharded, direct in meta if not direct]
        for t, row_sharded in enumerate(rows):
            for j, (px, py) in enumerate(_other_chips(x, y)):
                got = window(refs[t], row_sharded, 2 * px + py, c)
                other = window(refs[t], row_sharded, 2 * px + py, 1 - c)
                sends.append(_remote(got, got, send_sems, recv_sems, 3 * t + j, (x, y, 1 - c)))
                recvs.append(_remote(other, other, send_sems, recv_sems, 3 * t + j, (x, y, 1 - c)))
        return sends, recvs

    return over_ici, over_d2d


HBM = pl.BlockSpec(memory_space=pltpu.HBM)
SEMAPHORES = pl.BlockSpec(memory_space=pltpu.SEMAPHORE)
DATAFLOW = pltpu.SideEffectType.DATAFLOW_SIDE_EFFECTING


def _start_copies(name, arrays, plan, n_copies, after):
    n = len(arrays)

    def body(*refs):
        sends, _ = plan(refs[:n], refs[n + 1], refs[n + 2])
        for cp in sends:
            cp.start()
        token = refs[2 * n + 3]
        token[...] = jnp.zeros_like(token)

    out = pl.pallas_call(
        body, name=name,
        out_shape=(pltpu.SemaphoreType.DMA((n_copies,)), pltpu.SemaphoreType.DMA((n_copies,)),
                   *[pltpu.HBM(a.shape, a.dtype) for a in arrays], jax.ShapeDtypeStruct((SUBLANES, LANES), F32)),
        in_specs=[HBM] * n + [ANY],
        out_specs=(SEMAPHORES, SEMAPHORES, *[HBM] * n, pl.BlockSpec(memory_space=pltpu.VMEM)),
        input_output_aliases={t: t + 2 for t in range(n)},
        compiler_params=pltpu.CompilerParams(has_side_effects=DATAFLOW),
    )(*[pltpu.with_memory_space_constraint(a, pltpu.HBM) for a in arrays], after)
    return out[0], out[1], list(out[2:2 + n]), out[2 + n]


def _wait_copies(name, started, plan, after):
    send_sems, recv_sems, arrays, _ = started
    n = len(arrays)

    def body(*refs):
        sends, recvs = plan(refs[:n], refs[n], refs[n + 1])
        for cp in sends:
            cp.wait_send()
        for cp in recvs:
            cp.wait_recv()

    out = pl.pallas_call(
        body, name=name, out_shape=[pltpu.HBM(a.shape, a.dtype) for a in arrays],
        in_specs=[HBM] * n + [SEMAPHORES, SEMAPHORES, ANY], out_specs=[HBM] * n,
        input_output_aliases={t: t for t in range(n)},
        compiler_params=pltpu.CompilerParams(has_side_effects=DATAFLOW),
    )(*arrays, send_sems, recv_sems, after)
    return list(out)


def _exchange(name, arrays, out_shapes, plan, n_copies, in_place=False, after=None):
    n = len(arrays)
    extra = [] if after is None else [after]

    def body(*refs):
        ins, outs = refs[:n], refs[n + len(extra):n + len(extra) + len(out_shapes)]
        send_sems, recv_sems = refs[n + len(extra) + len(out_shapes):]
        sends, recvs = plan(ins, outs, send_sems, recv_sems)
        for cp in sends:
            cp.start()
        for cp in recvs:
            cp.wait_recv()
        for cp in sends:
            cp.wait_send()

    return pl.pallas_call(
        body, name=name, in_specs=[ANY] * (n + len(extra)), out_specs=[ANY] * len(out_shapes), out_shape=out_shapes,
        input_output_aliases={t: t for t in range(n)} if in_place else {},
        scratch_shapes=[pltpu.SemaphoreType.DMA((n_copies,)), pltpu.SemaphoreType.DMA((n_copies,))],
    )(*arrays, *extra)


def _swap_plan(n):
    def plan(refs, send_sems, recv_sems):
        x, y, c = _mesh_pos()
        sends = [_remote(refs[t].at[1 - c], refs[n + t], send_sems, recv_sems, t, (x, y, 1 - c)) for t in range(n)]
        return sends, sends

    return plan


def _scatter_plan(n):
    def plan(refs, send_sems, recv_sems):
        x, y, c = _mesh_pos()
        sends = []
        for t in range(n):
            for j, (px, py) in enumerate(_other_chips(x, y)):
                sends.append(_remote(refs[t].at[2 * px + py], refs[n + t].at[j], send_sems, recv_sems, 3 * t + j, (px, py, c)))
        return sends, sends

    return plan


def _swap_shapes(grads):
    return [jax.ShapeDtypeStruct(g.shape[1:], g.dtype) for g in grads]


def _scatter_shapes(sums):
    return [jax.ShapeDtypeStruct((3,) + s.shape[1:], s.dtype) for s in sums]


def _swap_other_half(grads, name):
    plan = _swap_plan(len(grads))
    return _exchange(name, grads, _swap_shapes(grads), lambda ins, outs, s, r: plan(list(ins) + list(outs), s, r), len(grads))


def _join_halves(halves, name, after=None):
    def plan(ins, outs, send_sems, recv_sems):
        x, y, c = _mesh_pos()
        sends = [_remote(outs[t].at[c], outs[t].at[c], send_sems, recv_sems, t, (x, y, 1 - c)) for t in range(len(ins))]
        recvs = [_remote(outs[t].at[1 - c], outs[t].at[1 - c], send_sems, recv_sems, t, (x, y, 1 - c))
                 for t in range(len(ins))]
        return sends, recvs

    shapes = [jax.ShapeDtypeStruct(h.shape, h.dtype) for h in halves]
    return _exchange(name, halves, shapes, plan, len(halves), in_place=True, after=after)


def _add_other_half(g4, got, where, name, wire_dtype=BF16):
    _, pieces, sr, sc = g4.shape
    tr = _tile(sr, 256, 2 * SUBLANES)

    def body(w_ref, a_ref, b_ref, o_ref):
        o_ref[...] = (a_ref[...] + b_ref[...]).astype(o_ref.dtype)

    blk = pl.BlockSpec((None, tr, sc), lambda p, i, w_ref: (p, i, 0))
    return pl.pallas_call(
        body, name=name, out_shape=jax.ShapeDtypeStruct((pieces, sr, sc), wire_dtype),
        grid_spec=pltpu.PrefetchScalarGridSpec(
            num_scalar_prefetch=1, grid=(pieces, sr // tr),
            in_specs=[pl.BlockSpec((None, None, tr, sc), lambda p, i, w_ref: (w_ref[0], p, i, 0)), blk], out_specs=blk),
        compiler_params=_params(("parallel", "parallel")),
    )(where, g4, got)


def _add_pieces(g4, got_half, got_pieces, where, name):
    _, _, sr, sc = g4.shape
    tr = _tile(sr, 256, 2 * SUBLANES)

    def body(w_ref, a_ref, b_ref, r_ref, o_ref):
        acc = a_ref[...] + b_ref[...]
        for j in range(3):
            acc = acc + r_ref[j].astype(F32)
        o_ref[...] = acc

    return pl.pallas_call(
        body, name=name, out_shape=jax.ShapeDtypeStruct((N_CORES, sr, sc), F32),
        grid_spec=pltpu.PrefetchScalarGridSpec(
            num_scalar_prefetch=1, grid=(sr // tr,),
            in_specs=[pl.BlockSpec((None, None, tr, sc), lambda i, w_ref: (w_ref[0], w_ref[1], i, 0)),
                      pl.BlockSpec((None, tr, sc), lambda i, w_ref: (w_ref[1], i, 0)),
                      pl.BlockSpec((3, tr, sc), lambda i, w_ref: (0, i, 0))],
            out_specs=pl.BlockSpec((None, tr, sc), lambda i, w_ref: (w_ref[0], i, 0))),
        compiler_params=_params(("parallel",)),
    )(where, g4, got_half, got_pieces)


def _adamw_update(w, g, m, v):
    nm = ADAM_B1 * m + (1.0 - ADAM_B1) * g
    nv = ADAM_B2 * v + (1.0 - ADAM_B2) * (g * g)
    m_hat = nm / (1.0 - ADAM_B1 ** ADAM_STEP)
    v_hat = nv / (1.0 - ADAM_B2 ** ADAM_STEP)
    return -ADAM_LR * (m_hat / (jnp.sqrt(v_hat) + ADAM_EPS) + ADAM_WD * w), nm, nv


def _adamw(w, g, m, v, name):
    rows, cols = w.shape
    halves = 2 if g.ndim == 3 else 1
    bc = cols // halves
    tr = _tile(rows, max(SUBLANES, (1 << 19) // max(bc, 1) // SUBLANES * SUBLANES), SUBLANES)

    def body(w_ref, g_ref, m_ref, v_ref, d_ref, nm_ref, nv_ref, go_ref):
        gv = g_ref[...]
        d_ref[...], nm_ref[...], nv_ref[...] = _adamw_update(w_ref[...], gv, m_ref[...], v_ref[...])
        go_ref[...] = gv

    blk = pl.BlockSpec((tr, bc), lambda i, h: (i, h))
    g_blk = pl.BlockSpec((None, tr, bc), lambda i, h: (h, i, 0)) if halves == 2 else blk
    return pl.pallas_call(
        body, name=name, grid=(rows // tr, halves), in_specs=[blk, g_blk, blk, blk], out_specs=[blk] * 4,
        out_shape=[jax.ShapeDtypeStruct((rows, cols), F32)] * 4, compiler_params=_params(("parallel", "parallel")),
    )(w, g, m, v)


def _adamw_many(ws, gs, ms, vs, name):
    n = len(ws)

    def body(*refs):
        outs = refs[4 * n:]
        for k in range(n):
            w_ref, g_ref, m_ref, v_ref = (refs[j * n + k] for j in range(4))
            outs[k][...], outs[n + k][...], outs[2 * n + k][...] = _adamw_update(w_ref[...], g_ref[...], m_ref[...], v_ref[...])

    out = pl.pallas_call(
        body, name=name, out_shape=[jax.ShapeDtypeStruct(w.shape, F32) for w in ws] * 3,
        compiler_params=pltpu.CompilerParams(vmem_limit_bytes=VMEM_LIMIT_BYTES),
    )(*ws, *gs, *ms, *vs)
    return out[:n], out[n:2 * n], out[2 * n:]


WEIGHTS = ['attn_norm_w', 'w_in', 'ssm_lambda_re', 'ssm_lambda_im', 'ssm_log_dt', 'ssm_b_re', 'ssm_b_im', 'ssm_c_re',
           'ssm_c_im', 'ssm_d', 'ssm_w_glu', 'ssm_b_glu', 'mla_q_norm_w', 'mla_w_uq', 'mla_kv_norm_w', 'mla_w_ukv',
           'ssm_out_norm_w', 'mla_out_norm_w', 'w_out', 'ffn_norm_w', 'ffn_w_up', 'ffn_conv_w', 'ffn_conv_b',
           'ffn_w_down', 'final_norm_w']
SHARDED = {'w_in': False, 'ssm_w_glu': True, 'mla_w_uq': False, 'mla_w_ukv': False, 'w_out': True, 'ffn_w_up': False,
           'ffn_w_down': True}
SMALL = [n for n in WEIGHTS if n not in SHARDED and n != 'ffn_conv_w']
ROPE_PAD = HEAD_SLOT - QK_NOPE_DIM - QK_ROPE_DIM
SMALL_COLS = 8 * LANES


def _pad_heads(w_uq, heads):
    qr = w_uq.shape[0]
    w3 = w_uq.reshape(qr, heads, QK_NOPE_DIM + QK_ROPE_DIM)
    return jnp.concatenate([w3, jnp.zeros((qr, heads, ROPE_PAD), w_uq.dtype)], axis=2).reshape(qr, heads * HEAD_SLOT)


def _unpad_heads(g_uq, heads):
    qr = g_uq.shape[0]
    return g_uq.reshape(qr, heads, HEAD_SLOT)[:, :, :QK_NOPE_DIM + QK_ROPE_DIM].reshape(qr, -1)


FFN = ['ffn_w_up', 'ffn_w_down']
FFN_GATHER = FFN + ['ffn_conv_w']
FFN_GATHER_META = [(SHARDED[n], False) for n in FFN] + [(False, True)]


class _Overlapped:
    def __init__(self, placed, where, after):
        self.where = where
        self.over_ici, self.over_d2d = _gather_plans(FFN_GATHER_META)
        self.gather = _start_copies("gather_ffn_start", placed, self.over_ici, 3 * len(placed), after)
        self.gather_started = self.gather[3]

    def ffn_weights_arrived(self, after):
        arrived = _wait_copies("gather_ffn_wait", self.gather, self.over_ici, after)
        n = len(FFN)
        self.direct = arrived[n:]
        self.passing = _start_copies("gather_ffn_pass_start", arrived[:n], self.over_d2d, 3 * n, after)
        return self.passing[3]

    def ffn_weights(self, after):
        passed = _wait_copies("gather_ffn_pass_wait", self.passing, self.over_d2d, after)
        return dict(zip(FFN_GATHER, passed + self.direct))

    def ffn_grads(self, g_down, g_up, after):
        grads = [g_up, g_down]
        lands = [lax.empty(s.shape, s.dtype) for s in _swap_shapes(grads)]
        self.swap = _start_copies("grad_ffn_swap_start", grads + lands, _swap_plan(len(grads)), len(grads), after)
        return self.swap[3]

    def ffn_backward_done(self, after):
        n = len(FFN)
        out = _wait_copies("grad_ffn_swap_wait", self.swap, _swap_plan(n), after)
        self.grads, self.got_half = out[:n], out[n:]
        sums = [_add_other_half(self.grads[t], self.got_half[t], self.where, "grad_add_half_" + name)
                for t, name in enumerate(FFN)]
        lands = [lax.empty(s.shape, s.dtype) for s in _scatter_shapes(sums)]
        self.scatter = _start_copies("grad_ffn_scatter_start", sums + lands, _scatter_plan(n), 3 * n, after)
        return self.scatter[3]

    def ffn_reduced(self, after):
        n = len(FFN)
        got_pieces = _wait_copies("grad_ffn_scatter_wait", self.scatter, _scatter_plan(n), after)[n:]
        return [_add_pieces(self.grads[t], self.got_half[t], got_pieces[t], self.where, "grad_add_pieces_" + name)
                for t, name in enumerate(FFN)]


def _step(args):
    x, positions, target = args["x"][0], args["positions"], args["loss_target"][0]
    rows = x.shape[0]
    p = {n: args[n] for n in WEIGHTS}
    xi, yi, ci = _mesh_pos()
    piece = 2 * xi + yi

    def transposed(a):
        return jnp.swapaxes(a[0], 0, 1)

    w_in = transposed(p["w_in"])
    in_width = w_in.shape[0]
    in_pad = (-in_width) % (2 * LANES)
    heads_here = p["mla_w_uq"].shape[2] // (QK_NOPE_DIM + QK_ROPE_DIM)
    shards = {
        "w_in": jnp.pad(w_in, ((0, in_pad), (0, 0))),
        "ssm_w_glu": p["ssm_w_glu"][0],
        "mla_w_uq": _pad_heads(p["mla_w_uq"][0], heads_here),
        "mla_w_ukv": p["mla_w_ukv"][0],
        "w_out": p["w_out"][0],
        "ffn_w_up": p["ffn_w_up"][0],
        "ffn_w_down": p["ffn_w_down"][0],
    }
    conv_w = jnp.pad(p["ffn_conv_w"][0], ((0, SUBLANES - p["ffn_conv_w"].shape[1]), (0, 0)))
    order = list(SHARDED)
    piece_idx = piece.reshape(1).astype(jnp.int32)
    placed = {n: _place_shard(shards[n], piece_idx, SHARDED[n], "place_" + n) for n in order}
    placed["ffn_conv_w"] = _place_shard(conv_w, piece_idx, False, "place_ffn_conv_w", out_dtype=F32)
    mixer = [n for n in order if n not in FFN]
    w = dict(zip(mixer, _gather_weights([(placed[n], SHARDED[n], False) for n in mixer], "gather_mixer_weights")))
    where = jnp.stack([ci, piece]).astype(jnp.int32)
    hooks = _Overlapped([placed[n] for n in FFN_GATHER], where, after=w["w_in"])
    groups = p["ssm_lambda_re"].shape[1]
    w.update({
        "attn_norm_w": p["attn_norm_w"] + hooks.gather_started[:1, :1],
        "ssm_lambda_re": p["ssm_lambda_re"][0], "ssm_lambda_im": p["ssm_lambda_im"][0],
        "ssm_log_dt": p["ssm_log_dt"].reshape(groups, 1), "ssm_b_re": p["ssm_b_re"].reshape(groups, -1),
        "ssm_b_im": p["ssm_b_im"].reshape(groups, -1), "ssm_c_re": p["ssm_c_re"][0], "ssm_c_im": p["ssm_c_im"][0],
        "ssm_d": p["ssm_d"], "ssm_b_glu": p["ssm_b_glu"], "mla_q_norm_w": p["mla_q_norm_w"],
        "mla_kv_norm_w": p["mla_kv_norm_w"], "ssm_out_norm_w": p["ssm_out_norm_w"], "mla_out_norm_w": p["mla_out_norm_w"],
        "ffn_norm_w": p["ffn_norm_w"], "ffn_conv_b": p["ffn_conv_b"], "final_norm_w": p["final_norm_w"].reshape(1, -1),
    })

    loss_tile, dx, g = _local_step(x, positions.reshape(rows, 1).astype(F32), target, w, hooks)
    loss = lax.psum(loss_tile[0, 0], ("x", "y", "c"))

    flat = [g[n].reshape(-1) for n in SMALL] + [g["ffn_conv_w"].reshape(-1)]
    sizes = [f.shape[0] for f in flat]
    per_block = -(-sum(sizes) // (N_CORES * N_CHIPS * SMALL_COLS))
    small_rows = -(-per_block // (2 * SUBLANES)) * (2 * SUBLANES)
    padded = N_CORES * N_CHIPS * small_rows * SMALL_COLS

    def pack(parts):
        parts = list(parts)
        have = sum(q.shape[0] for q in parts)
        return jnp.concatenate(parts + [jnp.zeros((padded - have,), F32)])

    reduced = mixer + ["small"]
    g_rs = [g[n] for n in mixer] + [pack(flat).reshape(N_CORES, N_CHIPS, small_rows, SMALL_COLS)]
    wire = [BF16] * len(mixer) + [F32]
    got_half = _swap_other_half(g_rs, "grad_swap_halves")
    sums = [_add_other_half(g_rs[t], got_half[t], where, "grad_add_half_" + n, wire[t]) for t, n in enumerate(reduced)]
    lands = [lax.empty(s.shape, s.dtype) for s in _scatter_shapes(sums)]
    scatter_plan = _scatter_plan(len(reduced))
    scatter = _start_copies("grad_mixer_scatter_start", sums + lands, scatter_plan, 3 * len(reduced), dx)

    grads, delta, new_m, new_v = {}, {}, {}, {}

    def finish(n, joined):
        grad = joined if SHARDED[n] else joined.reshape(-1, joined.shape[2])
        if n == "w_in":
            wt, mt, vt = w_in, transposed(args["m_w_in"]), transposed(args["v_w_in"])
            out = _adamw(wt, grad, mt, vt, "adamw_w_in")
            delta[n], new_m[n], new_v[n], grads[n] = (jnp.swapaxes(a, 0, 1)[None] for a in out)
            return
        if n == "mla_w_uq":
            grad = _unpad_heads(grad, heads_here)
        adam(n, grad)

    def adam(n, grad):
        shape = p[n].shape
        out = _adamw(p[n].reshape(shape[1:]), grad, args["m_" + n].reshape(shape[1:]),
                     args["v_" + n].reshape(shape[1:]), "adamw_" + n)
        delta[n], new_m[n], new_v[n], grads[n] = (a.reshape(shape) for a in out)

    for n, joined in zip(FFN, _join_halves(hooks.ffn_reduced(dx), "grad_ffn_join_halves", after=scatter[3])):
        finish(n, joined)
    got_pieces = _wait_copies("grad_mixer_scatter_wait", scatter, scatter_plan, delta[FFN[0]])[len(reduced):]
    halves = [_add_pieces(g_rs[t], got_half[t], got_pieces[t], where, "grad_add_pieces_" + n) for t, n in enumerate(reduced)]
    joined = _join_halves(halves, "grad_join_halves")
    for n, j in zip(mixer, joined):
        finish(n, j)
    eighths = _place_shard(joined[-1].reshape(N_CORES * small_rows, SMALL_COLS), piece_idx, True, "place_small_grads",
                           out_dtype=F32)
    small_sum = _gather_weights([(eighths, True, False)], "gather_small_grads")[0]
    flat_sum = small_sum.reshape(N_CHIPS, N_CORES, small_rows * SMALL_COLS).transpose(1, 0, 2).reshape(-1)
    offs = [0]
    for s in sizes:
        offs.append(offs[-1] + s)
    for k, n in enumerate(SMALL):
        grads[n] = flat_sum[offs[k]:offs[k + 1]].reshape(p[n].shape)
    taps, cols_here = p["ffn_conv_w"].shape[1], p["ffn_conv_w"].shape[2]
    conv_full = flat_sum[offs[len(SMALL)]:offs[len(SMALL) + 1]].reshape(taps, N_CHIPS * cols_here)
    adam("ffn_conv_w", lax.dynamic_slice_in_dim(conv_full, piece * cols_here, cols_here, axis=1))

    def rank2(a):
        return a.reshape(1, -1) if a.ndim == 1 else a

    d_s, m_s, v_s = _adamw_many([rank2(p[n]) for n in SMALL], [rank2(grads[n]) for n in SMALL],
                                [rank2(args["m_" + n]) for n in SMALL], [rank2(args["v_" + n]) for n in SMALL], "adamw_small")
    for k, n in enumerate(SMALL):
        delta[n], new_m[n], new_v[n] = (a.reshape(p[n].shape) for a in (d_s[k], m_s[k], v_s[k]))

    return (loss, dx[None], *[grads[n] for n in WEIGHTS], *[delta[n] for n in WEIGHTS],
            *[new_m[n] for n in WEIGHTS], *[new_v[n] for n in WEIGHTS])


def kernel(x, positions, attn_norm_w, w_in, ssm_lambda_re, ssm_lambda_im, ssm_log_dt, ssm_b_re, ssm_b_im, ssm_c_re, ssm_c_im, ssm_d, ssm_w_glu, ssm_b_glu, mla_q_norm_w, mla_w_uq, mla_kv_norm_w, mla_w_ukv, ssm_out_norm_w, mla_out_norm_w, w_out, ffn_norm_w, ffn_w_up, ffn_conv_w, ffn_conv_b, ffn_w_down, final_norm_w, loss_target, m_attn_norm_w, m_w_in, m_ssm_lambda_re, m_ssm_lambda_im, m_ssm_log_dt, m_ssm_b_re, m_ssm_b_im, m_ssm_c_re, m_ssm_c_im, m_ssm_d, m_ssm_w_glu, m_ssm_b_glu, m_mla_q_norm_w, m_mla_w_uq, m_mla_kv_norm_w, m_mla_w_ukv, m_ssm_out_norm_w, m_mla_out_norm_w, m_w_out, m_ffn_norm_w, m_ffn_w_up, m_ffn_conv_w, m_ffn_conv_b, m_ffn_w_down, m_final_norm_w, v_attn_norm_w, v_w_in, v_ssm_lambda_re, v_ssm_lambda_im, v_ssm_log_dt, v_ssm_b_re, v_ssm_b_im, v_ssm_c_re, v_ssm_c_im, v_ssm_d, v_ssm_w_glu, v_ssm_b_glu, v_mla_q_norm_w, v_mla_w_uq, v_mla_kv_norm_w, v_mla_w_ukv, v_ssm_out_norm_w, v_mla_out_norm_w, v_w_out, v_ffn_norm_w, v_ffn_w_up, v_ffn_conv_w, v_ffn_conv_b, v_ffn_w_down, v_final_norm_w):
    return _step(dict(locals()))
```

```python
import functools
import math

import jax
import jax.numpy as jnp
from jax import lax
from jax.experimental import pallas as pl
from jax.experimental.pallas import tpu as pltpu

F32 = jnp.float32
BF16 = jnp.bfloat16

SSM_GROUP = 16
SSM_STATE = 64
QK_NOPE_DIM = 128
QK_ROPE_DIM = 64
V_HEAD_DIM = 128
ROPE_THETA = 10000.0
RMS_EPS = 1e-6
ADAM_LR, ADAM_B1, ADAM_B2, ADAM_EPS, ADAM_WD, ADAM_STEP = 0.001, 0.9, 0.999, 1e-08, 0.01, 10

LANES = 128
SUBLANES = 8
VMEM_LIMIT_BYTES = 56 * 1024 * 1024

GROUPS_PER_BATCH = LANES // SSM_GROUP
STATE_PER_BATCH = GROUPS_PER_BATCH * SSM_STATE
HEAD_SLOT = 2 * LANES
NEG_INF = -1e30
ATTN_BLOCK = 512
FFN_ROWS = 1024

N_CHIPS = 4
N_CORES = 2


def _tile(n, pref, align=LANES):
    if n <= pref:
        return n
    t = (pref // align) * align
    while t >= align:
        if n % t == 0:
            return t
        t -= align
    return n


def _params(sem):
    return pltpu.CompilerParams(dimension_semantics=sem, vmem_limit_bytes=VMEM_LIMIT_BYTES)


def _dot(a, b, dims):
    return lax.dot_general(a, b, (dims, ((), ())), preferred_element_type=F32)


def _dot_nn(a, b):
    return _dot(a, b, ((1,), (0,)))


def _dot_nt(a, b):
    return _dot(a, b, ((1,), (1,)))


def _dot_tn(a, b):
    return _dot(a, b, ((0,), (0,)))


def _matmul(a, b, *, mode, name, tm=512, tn=1024, tk=2048, bias=None, add=None, out_dtype=F32,
            out_blocks=None, a_split=False, b_split=False, after=None):
    if a_split:
        assert mode == "nt"
        a_shape = (a.shape[1], 2 * a.shape[2])
    else:
        a_shape = a.shape
    if b_split:
        assert mode == "tn"
        b_shape = (b.shape[1], 2 * b.shape[2])
    else:
        b_shape = b.shape
    if mode == "nn":
        (m, k), (k2, n) = a_shape, b_shape
    elif mode == "nt":
        (m, k), (n, k2) = a_shape, b_shape
    else:
        (k, m), (k2, n) = a_shape, b_shape
    assert k == k2, (a.shape, b.shape, mode)
    tm, tn, tk = _tile(m, tm, SUBLANES), _tile(n, tn), _tile(k, tk)
    nk = k // tk
    a_spec = {"nn": pl.BlockSpec((tm, tk), lambda i, j, kk: (i, kk)),
              "nt": pl.BlockSpec((tm, tk), lambda i, j, kk: (i, kk)),
              "tn": pl.BlockSpec((tk, tm), lambda i, j, kk: (kk, i))}[mode]
    b_spec = {"nn": pl.BlockSpec((tk, tn), lambda i, j, kk: (kk, j)),
              "nt": pl.BlockSpec((tn, tk), lambda i, j, kk: (j, kk)),
              "tn": pl.BlockSpec((tk, tn), lambda i, j, kk: (kk, j))}[mode]
    if a_split:
        kb = a.shape[2] // tk
        assert a.shape[2] % tk == 0
        a_spec = pl.BlockSpec((None, tm, tk), lambda i, j, kk: (kk // kb, i, kk % kb))
    if b_split:
        nb = b.shape[2] // tn
        assert b.shape[2] % tn == 0
        b_spec = pl.BlockSpec((None, tk, tn), lambda i, j, kk: (j // nb, kk, j % nb))
    dot = {"nn": _dot_nn, "nt": _dot_nt, "tn": _dot_tn}[mode]
    in_specs, operands = [a_spec, b_spec], [a, b]
    if bias is not None:
        in_specs.append(pl.BlockSpec((1, tn), lambda i, j, kk: (0, j)))
        operands.append(bias)
    if add is not None:
        in_specs.append(pl.BlockSpec((tm, tn), lambda i, j, kk: (i, j)))
        operands.append(add)
    if after is not None:
        in_specs.append(pl.BlockSpec(memory_space=pl.ANY))
        operands.append(after)

    def body(*refs):
        a_ref, b_ref = refs[0], refs[1]
        rest = list(refs[2:])
        bias_ref = rest.pop(0) if bias is not None else None
        add_ref = rest.pop(0) if add is not None else None
        if after is not None:
            rest.pop(0)
        o_ref, acc_ref = rest

        def finish(acc):
            if bias_ref is not None:
                acc = acc + bias_ref[...]
            if add_ref is not None:
                acc = acc + add_ref[...]
            o_ref[...] = acc.astype(o_ref.dtype)

        part = dot(a_ref[...].astype(BF16), b_ref[...].astype(BF16))
        if nk == 1:
            finish(part)
        else:
            kk = pl.program_id(2)

            @pl.when(kk == 0)
            def _():
                acc_ref[...] = part

            @pl.when(jnp.logical_and(kk > 0, kk < nk - 1))
            def _():
                acc_ref[...] += part

            @pl.when(kk == nk - 1)
            def _():
                finish(acc_ref[...] + part)

    if out_blocks is None:
        out_shape = jax.ShapeDtypeStruct((m, n), out_dtype)
        out_spec = pl.BlockSpec((tm, tn), lambda i, j, kk: (i, j))
    else:
        shape, block, index_map = out_blocks(tm, tn)
        out_shape = jax.ShapeDtypeStruct(shape, out_dtype)
        out_spec = pl.BlockSpec(block, index_map)
    acc_shape = (tm, tn) if nk > 1 else (SUBLANES, LANES)
    return pl.pallas_call(
        body, name=name, grid=(m // tm, n // tn, nk), in_specs=in_specs, out_specs=out_spec, out_shape=out_shape,
        scratch_shapes=[pltpu.VMEM(acc_shape, F32)],
        compiler_params=_params(("parallel", "parallel", "arbitrary")),
    )(*operands)


def _wgrad_half(a, b, rows, cols, row_sharded, name, which, *, suffix="", add=None, wire=False, tm=None, tn=None,
                b_split=False):
    tokens = a.shape[0]
    if row_sharded:
        sr, sc = rows // N_CHIPS, cols // N_CORES
    else:
        sr, sc = rows // N_CORES, cols // N_CHIPS
    tm = _tile(sr, 512) if tm is None else tm
    tn = _tile(sc, 1024) if tn is None else tn
    assert sr % tm == 0 and sc % tn == 0, (rows, cols, tm, tn)
    rb, cb = sr // tm, sc // tn
    if row_sharded:
        a_map = lambda p, i, j, h: (0, p * rb + i)
        b_tile = lambda p, i, j, h: h[0] * cb + j
    else:
        a_map = lambda p, i, j, h: (0, h[0] * rb + i)
        b_tile = lambda p, i, j, h: p * cb + j
    if b_split:
        nbh = b.shape[2] // tn
        assert b.shape[2] % tn == 0
        b_spec = pl.BlockSpec((None, tokens, tn), lambda p, i, j, h: (b_tile(p, i, j, h) // nbh, 0, b_tile(p, i, j, h) % nbh))
    else:
        b_spec = pl.BlockSpec((tokens, tn), lambda p, i, j, h: (0, b_tile(p, i, j, h)))
    out_spec = pl.BlockSpec((None, tm, tn), lambda p, i, j, h: (p, i, j))
    in_specs, operands = [pl.BlockSpec((tokens, tm), a_map), b_spec], [a, b]
    if add is not None:
        in_specs.append(out_spec)
        operands.append(add)

    def body(h_ref, a_ref, b_ref, *rest):
        acc = _dot_tn(a_ref[...].astype(BF16), b_ref[...].astype(BF16))
        if add is not None:
            acc = acc + rest[0][...]
        for o_ref in rest[1 if add is not None else 0:]:
            o_ref[...] = acc.astype(o_ref.dtype)

    out_dtypes = [F32, BF16] if wire else [F32]
    out = pl.pallas_call(
        body, name=name + suffix, out_shape=[jax.ShapeDtypeStruct((N_CHIPS, sr, sc), dt) for dt in out_dtypes],
        grid_spec=pltpu.PrefetchScalarGridSpec(num_scalar_prefetch=1, grid=(N_CHIPS, rb, cb), in_specs=in_specs,
                                               out_specs=[out_spec] * len(out_dtypes)),
        compiler_params=_params(("parallel", "parallel", "parallel")),
    )(which, *operands)
    return tuple(out) if wire else out[0]


def _rms_rows(x):
    return lax.rsqrt(jnp.mean(x * x, axis=-1, keepdims=True) + RMS_EPS)


def _rmsnorm_fwd(x, w, *, name, width=None, col=0, out_dtype=BF16, tr=256):
    rows = x.shape[0]
    width = x.shape[1] if width is None else width
    tr = _tile(rows, tr, SUBLANES)

    def body(x_ref, w_ref, o_ref):
        xv = x_ref[...]
        o_ref[...] = (xv * _rms_rows(xv) * w_ref[...]).astype(o_ref.dtype)

    return pl.pallas_call(
        body, name=name, grid=(rows // tr,),
        in_specs=[pl.BlockSpec((tr, width), lambda i: (i, col)), pl.BlockSpec((1, width), lambda i: (0, 0))],
        out_specs=pl.BlockSpec((tr, width), lambda i: (i, 0)),
        out_shape=jax.ShapeDtypeStruct((rows, width), out_dtype),
        compiler_params=_params(("parallel",)),
    )(x, w)


def _rmsnorm_bwd_rows(xv, w, dy):
    r = _rms_rows(xv)
    n = xv * r
    dn = dy * w
    dx = r * (dn - n * jnp.mean(dn * n, axis=-1, keepdims=True))
    return dx, dy * n


def _rmsnorm_bwd(x, w, dy, *, name, width=None, col=0, dy_col=0, add=None, tr=256, dx_dtypes=(F32,)):
    rows = x.shape[0]
    n_dx = len(dx_dtypes)
    width = x.shape[1] if width is None else width
    tr = _tile(rows, tr, SUBLANES)
    in_specs = [pl.BlockSpec((tr, width), lambda i: (i, col)), pl.BlockSpec((1, width), lambda i: (0, 0)),
                pl.BlockSpec((tr, width), lambda i: (i, dy_col))]
    operands = [x, w, dy]
    if add is not None:
        in_specs.append(pl.BlockSpec((tr, width), lambda i: (i, 0)))
        operands.append(add)

    def body(*refs):
        x_ref, w_ref, dy_ref = refs[:3]
        add_ref = refs[3] if add is not None else None
        dx_refs, dw_ref = refs[-1 - n_dx:-1], refs[-1]
        dx, dwp = _rmsnorm_bwd_rows(x_ref[...], w_ref[...], dy_ref[...])
        if add_ref is not None:
            dx = dx + add_ref[...]
        for dx_ref in dx_refs:
            dx_ref[...] = dx.astype(dx_ref.dtype)
        part = jnp.sum(dwp, axis=0, keepdims=True)

        @pl.when(pl.program_id(0) == 0)
        def _():
            dw_ref[...] = part

        @pl.when(pl.program_id(0) > 0)
        def _():
            dw_ref[...] += part

    return pl.pallas_call(
        body, name=name, grid=(rows // tr,), in_specs=in_specs,
        out_specs=[pl.BlockSpec((tr, width), lambda i: (i, 0))] * n_dx + [pl.BlockSpec((1, width), lambda i: (0, 0))],
        out_shape=[jax.ShapeDtypeStruct((rows, width), dt) for dt in dx_dtypes] + [jax.ShapeDtypeStruct((1, width), F32)],
        compiler_params=_params(("arbitrary",)),
    )(*operands)


def _final_norm_loss(h, w, target, *, tr=256):
    rows, d = h.shape
    tr = _tile(rows, tr, SUBLANES)

    def body(h_ref, w_ref, t_ref, loss_ref, dh_ref, dhb_ref, dw_ref):
        hv, wv = h_ref[...], w_ref[...]
        r = _rms_rows(hv)
        n = hv * r
        err = n * wv - t_ref[...]
        d_out = err * (1.0 / d)
        dn = d_out * wv
        dh = r * (dn - n * jnp.mean(dn * n, axis=-1, keepdims=True))
        dh_ref[...] = dh
        dhb_ref[...] = dh.astype(BF16)
        dw_part = jnp.sum(d_out * n, axis=0, keepdims=True)
        loss_part = jnp.full((SUBLANES, LANES), 0.5 / d, F32) * jnp.sum(err * err)

        @pl.when(pl.program_id(0) == 0)
        def _():
            dw_ref[...] = dw_part
            loss_ref[...] = loss_part

        @pl.when(pl.program_id(0) > 0)
        def _():
            dw_ref[...] += dw_part
            loss_ref[...] += loss_part

    return pl.pallas_call(
        body, name="final_norm_loss", grid=(rows // tr,),
        in_specs=[pl.BlockSpec((tr, d), lambda i: (i, 0)), pl.BlockSpec((1, d), lambda i: (0, 0)),
                  pl.BlockSpec((tr, d), lambda i: (i, 0))],
        out_specs=[pl.BlockSpec((SUBLANES, LANES), lambda i: (0, 0)), pl.BlockSpec((tr, d), lambda i: (i, 0)),
                   pl.BlockSpec((tr, d), lambda i: (i, 0)), pl.BlockSpec((1, d), lambda i: (0, 0))],
        out_shape=[jax.ShapeDtypeStruct((SUBLANES, LANES), F32), jax.ShapeDtypeStruct((rows, d), F32),
                   jax.ShapeDtypeStruct((rows, d), BF16), jax.ShapeDtypeStruct((1, d), F32)],
        compiler_params=_params(("arbitrary",)),
    )(h, w, target)


def _cmul(ar, ai, br, bi):
    return ar * br - ai * bi, ar * bi + ai * br


def _expand_matrix(groups, reps):
    row = lax.broadcasted_iota(jnp.int32, (groups, groups * reps), 0)
    colg = lax.broadcasted_iota(jnp.int32, (groups, groups * reps), 1) // reps
    return (row == colg).astype(F32)


def _dot_exact(a, b, dims):
    return lax.dot_general(a, b, (dims, ((), ())), preferred_element_type=F32, precision=lax.Precision.HIGHEST)


def _s5_discretize(lr, li, dt):
    mag = jnp.exp(lr * dt)
    th = li * dt
    ar, ai = mag * jnp.cos(th), mag * jnp.sin(th)
    nr, ni = ar - 1.0, ai
    den = lr * lr + li * li
    zr = (nr * lr + ni * li) / den
    zi = (ni * lr - nr * li) / den
    return mag, ar, ai, nr, ni, den, zr, zi


def _s5_params(lam_re, lam_im, log_dt, b_re, b_im):
    g, p = lam_re.shape
    ph = b_re.shape[1]

    def body(lr_ref, li_ref, ldt_ref, br_ref, bi_ref, ar_ref, ai_ref, bbr_ref, bbi_ref):
        dt = jnp.exp(ldt_ref[...])
        _, ar, ai, _, _, _, zr, zi = _s5_discretize(lr_ref[...], li_ref[...], dt)
        ar_ref[...] = ar
        ai_ref[...] = ai
        e = _expand_matrix(p, ph // p)
        zr_x = _dot_exact(zr, e, ((1,), (0,)))
        zi_x = _dot_exact(zi, e, ((1,), (0,)))
        bre, bim = br_ref[...], bi_ref[...]
        bbr_ref[...] = zr_x * bre - zi_x * bim
        bbi_ref[...] = zr_x * bim + zi_x * bre

    return pl.pallas_call(
        body, name="s5_params",
        out_shape=[jax.ShapeDtypeStruct((g, p), F32)] * 2 + [jax.ShapeDtypeStruct((g, ph), F32)] * 2,
    )(lam_re, lam_im, log_dt, b_re, b_im)


def _s5_params_bwd(lam_re, lam_im, log_dt, b_re, b_im, d_ar, d_ai, d_bbr, d_bbi):
    g, p = lam_re.shape
    ph = b_re.shape[1]

    def body(lr_ref, li_ref, ldt_ref, br_ref, bi_ref, dar_ref, dai_ref, dbr_ref, dbi_ref,
             dlr_ref, dli_ref, dldt_ref, dbre_ref, dbim_ref):
        lr, li = lr_ref[...], li_ref[...]
        dt = jnp.exp(ldt_ref[...])
        mag, ar, ai, nr, ni, den, zr, zi = _s5_discretize(lr, li, dt)
        e = _expand_matrix(p, ph // p)
        zr_x = _dot_exact(zr, e, ((1,), (0,)))
        zi_x = _dot_exact(zi, e, ((1,), (0,)))
        bre, bim, dbr, dbi = br_ref[...], bi_ref[...], dbr_ref[...], dbi_ref[...]
        dbre_ref[...] = zr_x * dbr + zi_x * dbi
        dbim_ref[...] = zr_x * dbi - zi_x * dbr
        dzr = _dot_exact(bre * dbr + bim * dbi, e, ((1,), (1,)))
        dzi = _dot_exact(bre * dbi - bim * dbr, e, ((1,), (1,)))
        inv = 1.0 / den
        d_nr = (dzr * lr - dzi * li) * inv
        d_ni = (dzr * li + dzi * lr) * inv
        d_den = -(dzr * zr + dzi * zi) * inv
        d_lr = (dzr * nr + dzi * ni) * inv + 2.0 * lr * d_den
        d_li = (dzr * ni - dzi * nr) * inv + 2.0 * li * d_den
        t_ar = dar_ref[...] + d_nr
        t_ai = dai_ref[...] + d_ni
        d_lrdt = t_ar * ar + t_ai * ai
        d_th = t_ai * ar - t_ar * ai
        dlr_ref[...] = d_lr + d_lrdt * dt
        dli_ref[...] = d_li + d_th * dt
        dldt_ref[...] = jnp.sum(d_lrdt * lr + d_th * li, axis=1, keepdims=True) * dt

    return pl.pallas_call(
        body, name="s5_params_bwd",
        out_shape=[jax.ShapeDtypeStruct((g, p), F32)] * 2 + [jax.ShapeDtypeStruct((g, 1), F32)]
        + [jax.ShapeDtypeStruct((g, ph), F32)] * 2,
    )(lam_re, lam_im, log_dt, b_re, b_im, d_ar, d_ai, d_bbr, d_bbi)


def _powers(ar, ai, count):
    out = [(ar, ai)]
    for _ in range(count - 1):
        out.append(_cmul(out[-1][0], out[-1][1], ar, ai))
    return out


def _scan_coefs(ar, ai, reverse):
    w = ar.shape[-1]
    pw = _powers(ar, ai, SUBLANES)
    row = lax.broadcasted_iota(jnp.int32, (SUBLANES, w), 0)
    steps = []
    d = 1
    while d < SUBLANES:
        keep = (row < SUBLANES - d) if reverse else (row >= d)
        pr, pi = pw[d - 1]
        steps.append((d, jnp.where(keep, pr, 0.0), jnp.where(keep, pi, 0.0)))
        d *= 2
    cr = jnp.zeros((SUBLANES, w), F32)
    ci = jnp.zeros((SUBLANES, w), F32)
    for t in range(SUBLANES):
        pr, pi = pw[SUBLANES - 1 - t] if reverse else pw[t]
        cr = jnp.where(row == t, pr, cr)
        ci = jnp.where(row == t, pi, ci)
    return steps, cr, ci


def _scan_tile(xr, xi, carry_r, carry_i, coefs, reverse):
    steps, cr, ci = coefs
    for d, mr, mi in steps:
        shift = SUBLANES - d if reverse else d
        sr, si = pltpu.roll(xr, shift, 0), pltpu.roll(xi, shift, 0)
        pr, pi = _cmul(mr, mi, sr, si)
        xr, xi = xr + pr, xi + pi
    pr, pi = _cmul(cr, ci, carry_r, carry_i)
    return xr + pr, xi + pi


def _gelu(x):
    c = math.sqrt(2.0 / math.pi)
    return 0.5 * x * (1.0 + jnp.tanh(c * (x + 0.044715 * x * x * x)))


def _gelu_grad(x):
    c = math.sqrt(2.0 / math.pi)
    t = jnp.tanh(c * (x + 0.044715 * x * x * x))
    return 0.5 * (1.0 + t) + 0.5 * x * (1.0 - t * t) * c * (1.0 + 3.0 * 0.044715 * x * x)


def _s5_fwd(proj, wb, wc, d_skip, abar):
    rows = proj.shape[0]
    nb = wb.shape[0]
    s2 = 2 * STATE_PER_BATCH
    st = STATE_PER_BATCH
    chunk = _tile(rows, 512, SUBLANES)

    def body(u_ref, wb_ref, wc_ref, d_ref, a_ref, s_ref, y_ref, yg_ref):
        for c0 in range(0, rows, chunk):
            s_ref[pl.ds(c0, chunk), :] = _dot_nn(u_ref[pl.ds(c0, chunk), :].astype(BF16), wb_ref[...])
        av = a_ref[...]
        coefs = _scan_coefs(av[:, :st], av[:, st:], reverse=False)

        def tile(b, carry):
            r0 = pl.multiple_of(b * SUBLANES, SUBLANES)
            xr, xi = _scan_tile(s_ref[pl.ds(r0, SUBLANES), :st], s_ref[pl.ds(r0, SUBLANES), st:], carry[0], carry[1],
                                coefs, False)
            s_ref[pl.ds(r0, SUBLANES), :st] = xr
            s_ref[pl.ds(r0, SUBLANES), st:] = xi
            return xr[SUBLANES - 1:, :], xi[SUBLANES - 1:, :]

        zero = jnp.zeros((1, st), F32)
        lax.fori_loop(0, rows // SUBLANES, tile, (zero, zero))
        for c0 in range(0, rows, chunk):
            y = _dot_nn(s_ref[pl.ds(c0, chunk), :].astype(BF16), wc_ref[...]) + d_ref[...] * u_ref[pl.ds(c0, chunk), :]
            y_ref[pl.ds(c0, chunk), :] = y
            yg_ref[pl.ds(c0, chunk), :] = _gelu(y).astype(BF16)

    return pl.pallas_call(
        body, name="s5_fwd", grid=(nb,),
        in_specs=[pl.BlockSpec((rows, LANES), lambda j: (0, j)), pl.BlockSpec((None, LANES, s2), lambda j: (j, 0, 0)),
                  pl.BlockSpec((None, s2, LANES), lambda j: (j, 0, 0)), pl.BlockSpec((1, LANES), lambda j: (0, j)),
                  pl.BlockSpec((None, 1, s2), lambda j: (j, 0, 0))],
        out_specs=[pl.BlockSpec((rows, s2), lambda j: (0, j)), pl.BlockSpec((rows, LANES), lambda j: (0, j)),
                   pl.BlockSpec((rows, LANES), lambda j: (0, j))],
        out_shape=[jax.ShapeDtypeStruct((rows, nb * s2), F32), jax.ShapeDtypeStruct((rows, nb * LANES), F32),
                   jax.ShapeDtypeStruct((rows, nb * LANES), BF16)],
        compiler_params=_params(("parallel",)),
    )(proj, wb, wc, d_skip, abar)


def _s5_bwd(proj, states, y_pre, dyg_a, dyg_b, wb, wc, d_skip, abar):
    rows = proj.shape[0]
    nb = wb.shape[0]
    s2 = 2 * STATE_PER_BATCH
    st = STATE_PER_BATCH
    chunk = _tile(rows, 512, SUBLANES)
    n_tiles = rows // SUBLANES

    def body(u_ref, s_ref, y_ref, ga_ref, gb_ref, wb_ref, wc_ref, d_ref, a_ref,
             du_ref, dwb_ref, dwc_ref, da_ref, dd_ref, ds_ref, dy_ref):
        dy_ref[...] = (ga_ref[...] + gb_ref[...]) * _gelu_grad(y_ref[...])
        dd_ref[...] = jnp.sum(dy_ref[...] * u_ref[...], axis=0, keepdims=True)
        for c0 in range(0, rows, chunk):
            ds_ref[pl.ds(c0, chunk), :] = _dot_nt(dy_ref[pl.ds(c0, chunk), :].astype(BF16), wc_ref[...])
        dwc_ref[...] = _dot_tn(s_ref[...].astype(BF16), dy_ref[...].astype(BF16))
        av = a_ref[...]
        coefs = _scan_coefs(av[:, :st], -av[:, st:], reverse=True)
        row = lax.broadcasted_iota(jnp.int32, (SUBLANES, st), 0)

        def tile(k, carry):
            cr, ci, acc_r, acc_i = carry
            b = n_tiles - 1 - k
            r0 = pl.multiple_of(b * SUBLANES, SUBLANES)
            rp = pl.multiple_of(jnp.maximum(b - 1, 0) * SUBLANES, SUBLANES)
            xr, xi = _scan_tile(ds_ref[pl.ds(r0, SUBLANES), :st], ds_ref[pl.ds(r0, SUBLANES), st:], cr, ci, coefs, True)
            ds_ref[pl.ds(r0, SUBLANES), :st] = xr
            ds_ref[pl.ds(r0, SUBLANES), st:] = xi
            first = jnp.where(b > 0, 1.0, 0.0)
            pr = jnp.where(row == 0, pltpu.roll(s_ref[pl.ds(rp, SUBLANES), :st], 1, 0) * first,
                           pltpu.roll(s_ref[pl.ds(r0, SUBLANES), :st], 1, 0))
            pi = jnp.where(row == 0, pltpu.roll(s_ref[pl.ds(rp, SUBLANES), st:], 1, 0) * first,
                           pltpu.roll(s_ref[pl.ds(r0, SUBLANES), st:], 1, 0))
            acc_r = acc_r + pr * xr + pi * xi
            acc_i = acc_i + pr * xi - pi * xr
            return xr[:1, :], xi[:1, :], acc_r, acc_i

        zero = jnp.zeros((1, st), F32)
        zacc = jnp.zeros((SUBLANES, st), F32)
        _, _, acc_r, acc_i = lax.fori_loop(0, n_tiles, tile, (zero, zero, zacc, zacc))
        da_ref[:, :st] = jnp.sum(acc_r, axis=0, keepdims=True)
        da_ref[:, st:] = jnp.sum(acc_i, axis=0, keepdims=True)
        for c0 in range(0, rows, chunk):
            du_ref[pl.ds(c0, chunk), :] = (_dot_nt(ds_ref[pl.ds(c0, chunk), :].astype(BF16), wb_ref[...])
                                           + d_ref[...] * dy_ref[pl.ds(c0, chunk), :]).astype(du_ref.dtype)
        dwb_ref[...] = _dot_tn(u_ref[...].astype(BF16), ds_ref[...].astype(BF16))

    col = pl.BlockSpec((rows, LANES), lambda j: (0, j))
    return pl.pallas_call(
        body, name="s5_bwd", grid=(nb,),
        in_specs=[col, pl.BlockSpec((rows, s2), lambda j: (0, j)), col, col, col,
                  pl.BlockSpec((None, LANES, s2), lambda j: (j, 0, 0)), pl.BlockSpec((None, s2, LANES), lambda j: (j, 0, 0)),
                  pl.BlockSpec((1, LANES), lambda j: (0, j)), pl.BlockSpec((None, 1, s2), lambda j: (j, 0, 0))],
        out_specs=[col, pl.BlockSpec((None, LANES, s2), lambda j: (j, 0, 0)),
                   pl.BlockSpec((None, s2, LANES), lambda j: (j, 0, 0)), pl.BlockSpec((None, 1, s2), lambda j: (j, 0, 0)),
                   pl.BlockSpec((1, LANES), lambda j: (0, j))],
        out_shape=[jax.ShapeDtypeStruct((rows, nb * LANES), BF16), jax.ShapeDtypeStruct((nb, LANES, s2), F32),
                   jax.ShapeDtypeStruct((nb, s2, LANES), F32), jax.ShapeDtypeStruct((nb, 1, s2), F32),
                   jax.ShapeDtypeStruct((1, nb * LANES), F32)],
        scratch_shapes=[pltpu.VMEM((rows, s2), F32), pltpu.VMEM((rows, LANES), F32)],
        compiler_params=_params(("parallel",)),
    )(proj, states, y_pre, dyg_a, dyg_b, wb, wc, d_skip, abar)


def _glu_norm_fwd(y_pre, z, w, *, tr=256):
    rows, width = y_pre.shape
    tr = _tile(rows, tr, SUBLANES)

    def body(y_ref, z_ref, w_ref, o_ref):
        v = _gelu(y_ref[...]) * jax.nn.sigmoid(z_ref[...])
        o_ref[...] = (v * _rms_rows(v) * w_ref[...]).astype(o_ref.dtype)

    blk = pl.BlockSpec((tr, width), lambda i: (i, 0))
    return pl.pallas_call(
        body, name="glu_norm_fwd", grid=(rows // tr,),
        in_specs=[blk, blk, pl.BlockSpec((1, width), lambda i: (0, 0))], out_specs=blk,
        out_shape=jax.ShapeDtypeStruct((rows, width), BF16), compiler_params=_params(("parallel",)),
    )(y_pre, z, w)


def _glu_norm_bwd(y_pre, z, w, dycat, *, tr=256):
    rows, width = y_pre.shape
    tr = _tile(rows, tr, SUBLANES)

    def body(y_ref, z_ref, w_ref, dy_ref, dz_ref, dg_ref, dw_ref, db_ref):
        yg = _gelu(y_ref[...])
        sg = jax.nn.sigmoid(z_ref[...])
        dv, dwp = _rmsnorm_bwd_rows(yg * sg, w_ref[...], dy_ref[...])
        dz = dv * yg * sg * (1.0 - sg)
        dz_ref[...] = dz.astype(dz_ref.dtype)
        dg_ref[...] = dv * sg
        dw_part = jnp.sum(dwp, axis=0, keepdims=True)
        db_part = jnp.sum(dz, axis=0, keepdims=True)

        @pl.when(pl.program_id(0) == 0)
        def _():
            dw_ref[...] = dw_part
            db_ref[...] = db_part

        @pl.when(pl.program_id(0) > 0)
        def _():
            dw_ref[...] += dw_part
            db_ref[...] += db_part

    blk = pl.BlockSpec((tr, width), lambda i: (i, 0))
    vec = pl.BlockSpec((1, width), lambda i: (0, 0))
    return pl.pallas_call(
        body, name="glu_norm_bwd", grid=(rows // tr,), in_specs=[blk, blk, vec, blk], out_specs=[blk, blk, vec, vec],
        out_shape=[jax.ShapeDtypeStruct((rows, width), BF16), jax.ShapeDtypeStruct((rows, width), F32)]
        + [jax.ShapeDtypeStruct((1, width), F32)] * 2,
        compiler_params=_params(("arbitrary",)),
    )(y_pre, z, w, dycat)


def _rope_tables(pos, freq, sign):
    rows = pos.shape[0]

    def body(p_ref, f_ref, s_ref, cos_ref, sin_ref):
        ang = p_ref[...] * f_ref[...]
        cos_ref[...] = jnp.cos(ang)
        sin_ref[...] = jnp.sin(ang) * s_ref[...]

    return pl.pallas_call(body, name="rope_tables", out_shape=[jax.ShapeDtypeStruct((rows, LANES), F32)] * 2)(pos, freq, sign)


def _rope(x, cos, sin_signed):
    lane = lax.broadcasted_iota(jnp.int32, x.shape, 1)
    half = QK_ROPE_DIM // 2
    swapped = jnp.where(lane < half, pltpu.roll(x, LANES - half, 1), pltpu.roll(x, half, 1))
    return x * cos + swapped * sin_signed


def _attn_prep(q, kv, proj, kpe_col, cos, sin, *, tr=256):
    rows = q.shape[0]
    heads = q.shape[1] // HEAD_SLOT
    tr = _tile(rows, tr, SUBLANES)

    def body(q_ref, kv_ref, kpe_ref, cos_ref, sin_ref, qc_ref, kc_ref, v_ref):
        c, s = cos_ref[...], sin_ref[...]
        qc_ref[:, :LANES] = q_ref[:, :LANES].astype(BF16)
        qc_ref[:, LANES:] = _rope(q_ref[:, LANES:], c, s).astype(BF16)
        kc_ref[:, :LANES] = kv_ref[:, :LANES].astype(BF16)
        kc_ref[:, LANES:] = _rope(kpe_ref[...], c, s).astype(BF16)
        v_ref[...] = kv_ref[:, LANES:].astype(BF16)

    slot = pl.BlockSpec((tr, HEAD_SLOT), lambda i, h: (i, h))
    tab = pl.BlockSpec((tr, LANES), lambda i, h: (i, 0))
    return pl.pallas_call(
        body, name="attn_prep", grid=(rows // tr, heads),
        in_specs=[slot, slot, pl.BlockSpec((tr, LANES), lambda i, h: (i, kpe_col)), tab, tab],
        out_specs=[slot, slot, pl.BlockSpec((tr, LANES), lambda i, h: (i, h))],
        out_shape=[jax.ShapeDtypeStruct((rows, heads * HEAD_SLOT), BF16)] * 2
        + [jax.ShapeDtypeStruct((rows, heads * LANES), BF16)],
        compiler_params=_params(("parallel", "parallel")),
    )(q, kv, proj, cos, sin)


def _causal(tq, tk):
    return lax.broadcasted_iota(jnp.int32, (tq, tk), 1) <= lax.broadcasted_iota(jnp.int32, (tq, tk), 0)


def _attn_fwd(qc, kc, vb, *, scale, tq=512):
    rows = qc.shape[0]
    heads = qc.shape[1] // HEAD_SLOT
    tq = _tile(rows, tq, SUBLANES)
    tk = tq

    def body(q_ref, k_ref, v_ref, o_ref, lse_ref):
        i = pl.program_id(1)
        q = q_ref[...]

        def step(j, carry, diagonal):
            m, l, acc = carry
            k0 = pl.multiple_of(j * tk, tk)
            s = _dot_nt(q, k_ref[pl.ds(k0, tk), :]) * scale
            if diagonal:
                s = jnp.where(_causal(tq, tk), s, NEG_INF)
            m_new = jnp.maximum(m, jnp.max(s, axis=-1, keepdims=True))
            p = jnp.exp(s - m_new)
            alpha = jnp.exp(m - m_new)
            l = alpha * l + jnp.sum(p, axis=-1, keepdims=True)
            acc = alpha * acc + _dot_nn(p.astype(BF16), v_ref[pl.ds(k0, tk), :])
            return m_new, l, acc

        init = (jnp.full((tq, 1), NEG_INF, F32), jnp.zeros((tq, 1), F32), jnp.zeros((tq, LANES), F32))
        below = lax.fori_loop(0, i, lambda j, carry: step(j, carry, False), init)
        m, l, acc = step(i, below, True)
        o_ref[...] = acc / l
        lse_ref[...] = jnp.broadcast_to(m + jnp.log(l), (tq, LANES))

    return pl.pallas_call(
        body, name="attn_fwd", grid=(heads, rows // tq),
        in_specs=[pl.BlockSpec((tq, HEAD_SLOT), lambda h, i: (i, h)), pl.BlockSpec((rows, HEAD_SLOT), lambda h, i: (0, h)),
                  pl.BlockSpec((rows, LANES), lambda h, i: (0, h))],
        out_specs=[pl.BlockSpec((tq, LANES), lambda h, i: (i, h))] * 2,
        out_shape=[jax.ShapeDtypeStruct((rows, heads * LANES), F32)] * 2,
        compiler_params=_params(("parallel", "parallel")),
    )(qc, kc, vb)


def _attn_bwd(qc, kc, vb, o, do, lse, cos, sin, *, scale, tk=512):
    rows = qc.shape[0]
    heads = qc.shape[1] // HEAD_SLOT
    tk = _tile(rows, tk, SUBLANES)
    tq = tk
    nq = rows // tq

    def body(q_ref, k_ref, v_ref, o_ref, do_ref, lse_ref, cos_ref, sin_ref, dq_ref, dkv_ref, dkpe_ref, dq_acc, delta_ref):
        j = pl.program_id(1)

        @pl.when(j == 0)
        def _():
            dq_acc[...] = jnp.zeros_like(dq_acc)
            for r0 in range(0, rows, tq):
                d = jnp.sum(do_ref[pl.ds(r0, tq), :] * o_ref[pl.ds(r0, tq), :], axis=-1, keepdims=True)
                delta_ref[pl.ds(r0, tq), :] = jnp.broadcast_to(d, (tq, LANES))

        kb, vv = k_ref[...], v_ref[...]

        def step(i, carry, diagonal):
            dk, dv = carry
            q0 = pl.multiple_of(i * tq, tq)
            qb = q_ref[pl.ds(q0, tq), :]
            dob = do_ref[pl.ds(q0, tq), :].astype(BF16)
            s = _dot_nt(qb, kb) * scale
            p = jnp.exp(s - lse_ref[pl.ds(q0, tq), :1])
            if diagonal:
                p = jnp.where(_causal(tq, tk), p, 0.0)
            dv = dv + _dot_tn(p.astype(BF16), dob)
            ds = (p * (_dot_nt(dob, vv) - delta_ref[pl.ds(q0, tq), :1])).astype(BF16)
            dk = dk + _dot_tn(ds, qb)
            dq_acc[pl.ds(q0, tq), :] += _dot_nn(ds, kb)
            return dk, dv

        zero = (jnp.zeros((tk, HEAD_SLOT), F32), jnp.zeros((tk, LANES), F32))
        dk, dv = lax.fori_loop(j + 1, nq, lambda i, carry: step(i, carry, False), step(j, zero, True))
        dkv_ref[:, :LANES] = (dk[:, :LANES] * scale).astype(dkv_ref.dtype)
        dkv_ref[:, LANES:] = dv.astype(dkv_ref.dtype)
        dkpe_ref[...] = dk[:, LANES:] * scale

        @pl.when(j == nq - 1)
        def _():
            for r0 in range(0, rows, tq):
                dq = dq_acc[pl.ds(r0, tq), :] * scale
                dq_ref[pl.ds(r0, tq), :LANES] = dq[:, :LANES].astype(dq_ref.dtype)
                dq_ref[pl.ds(r0, tq), LANES:] = _rope(dq[:, LANES:], cos_ref[pl.ds(r0, tq), :],
                                                      -sin_ref[pl.ds(r0, tq), :]).astype(dq_ref.dtype)

    full_q = pl.BlockSpec((rows, HEAD_SLOT), lambda h, j: (0, h))
    full_v = pl.BlockSpec((rows, LANES), lambda h, j: (0, h))
    tab = pl.BlockSpec((rows, LANES), lambda h, j: (0, 0))
    return pl.pallas_call(
        body, name="attn_bwd", grid=(heads, rows // tk),
        in_specs=[full_q, pl.BlockSpec((tk, HEAD_SLOT), lambda h, j: (j, h)), pl.BlockSpec((tk, LANES), lambda h, j: (j, h)),
                  full_v, full_v, full_v, tab, tab],
        out_specs=[full_q, pl.BlockSpec((tk, HEAD_SLOT), lambda h, j: (j, h)), pl.BlockSpec((tk, LANES), lambda h, j: (j, h))],
        out_shape=[jax.ShapeDtypeStruct((rows, heads * HEAD_SLOT), BF16), jax.ShapeDtypeStruct((rows, heads * HEAD_SLOT), BF16),
                   jax.ShapeDtypeStruct((rows, heads * LANES), F32)],
        scratch_shapes=[pltpu.VMEM((rows, HEAD_SLOT), F32), pltpu.VMEM((rows, LANES), F32)],
        compiler_params=_params(("parallel", "arbitrary")),
    )(qc, kc, vb, o, do, lse, cos, sin)


def _kpe_bwd(dkpe_heads, cos, sin, *, tr=512):
    rows = dkpe_heads.shape[0]
    heads = dkpe_heads.shape[1] // LANES
    tr = _tile(rows, tr, 2 * SUBLANES)

    def body(d_ref, cos_ref, sin_ref, o_ref):
        acc = d_ref[:, :LANES]
        for h in range(1, heads):
            acc = acc + d_ref[:, h * LANES:(h + 1) * LANES]
        o_ref[...] = _rope(acc, cos_ref[...], -sin_ref[...]).astype(o_ref.dtype)

    tab = pl.BlockSpec((tr, LANES), lambda i: (i, 0))
    return pl.pallas_call(
        body, name="kpe_bwd", grid=(rows // tr,),
        in_specs=[pl.BlockSpec((tr, heads * LANES), lambda i: (i, 0)), tab, tab], out_specs=tab,
        out_shape=jax.ShapeDtypeStruct((rows, LANES), BF16), compiler_params=_params(("parallel",)),
    )(dkpe_heads, cos, sin)


CONV_ROWS = 128


def _with_halo(ref, r0, ci, n_chunks, ch, lanes, before, after):
    parts = []
    if before:
        lo = pl.multiple_of(jnp.maximum(r0 - SUBLANES, 0), SUBLANES)
        parts.append(ref[pl.ds(lo, SUBLANES), lanes] * jnp.where(ci > 0, 1.0, 0.0))
    parts.append(ref[pl.ds(r0, ch), lanes])
    if after:
        hi = pl.multiple_of(jnp.minimum(r0 + ch, n_chunks * ch - SUBLANES), SUBLANES)
        parts.append(ref[pl.ds(hi, SUBLANES), lanes] * jnp.where(ci < n_chunks - 1, 1.0, 0.0))
    return jnp.concatenate(parts, axis=0)


def _taps(ext):
    return pltpu.roll(ext, 2, 0)[SUBLANES:], pltpu.roll(ext, 1, 0)[SUBLANES:], ext[SUBLANES:]


def _conv3(taps, w, b):
    return w[0:1, :] * taps[0] + w[1:2, :] * taps[1] + w[2:3, :] * taps[2] + b


def _conv_gate_fwd(a, conv_w, conv_b, *, tc=256):
    rows, f2 = a.shape
    f = f2 // 2
    tc = _tile(f, tc)
    nc = f // tc
    ch = _tile(rows, CONV_ROWS, SUBLANES)
    n_chunks = rows // ch

    def body(ag_ref, av_ref, wg_ref, wv_ref, bg_ref, bv_ref, o_ref):
        for lt in range(tc // LANES):
            lanes = slice(lt * LANES, (lt + 1) * LANES)
            wg, wv, bg, bv = wg_ref[:, lanes], wv_ref[:, lanes], bg_ref[:, lanes], bv_ref[:, lanes]

            def chunk(ci, carry):
                r0 = pl.multiple_of(ci * ch, ch)
                gate = _conv3(_taps(_with_halo(ag_ref, r0, ci, n_chunks, ch, lanes, True, False)), wg, bg)
                val = _conv3(_taps(_with_halo(av_ref, r0, ci, n_chunks, ch, lanes, True, False)), wv, bv)
                o_ref[pl.ds(r0, ch), lanes] = (gate * jax.nn.sigmoid(gate) * val).astype(o_ref.dtype)
                return carry

            lax.fori_loop(0, n_chunks, chunk, 0)

    return pl.pallas_call(
        body, name="conv_gate_fwd", grid=(nc,),
        in_specs=[pl.BlockSpec((rows, tc), lambda j: (0, j)), pl.BlockSpec((rows, tc), lambda j: (0, j + nc)),
                  pl.BlockSpec((SUBLANES, tc), lambda j: (0, j)), pl.BlockSpec((SUBLANES, tc), lambda j: (0, j + nc)),
                  pl.BlockSpec((1, tc), lambda j: (0, j)), pl.BlockSpec((1, tc), lambda j: (0, j + nc))],
        out_specs=pl.BlockSpec((rows, tc), lambda j: (0, j)),
        out_shape=jax.ShapeDtypeStruct((rows, f), BF16), compiler_params=_params(("parallel",)),
    )(a, a, conv_w, conv_w, conv_b, conv_b)


def _conv_gate_bwd(a, conv_w, conv_b, dg, *, tc=256):
    rows, f2 = a.shape
    f = f2 // 2
    tc = _tile(f, tc)
    nc = f // tc
    ch = _tile(rows, CONV_ROWS, SUBLANES)
    n_chunks = rows // ch
    ext_rows = ch + SUBLANES

    def fold(x):
        return jnp.sum(x.reshape(ch // SUBLANES, SUBLANES, LANES), axis=0)

    def body(ag_ref, av_ref, wg_ref, wv_ref, bg_ref, bv_ref, dg_ref, da_ref, dw_ref, db_ref):
        for lt in range(tc // LANES):
            lanes = slice(lt * LANES, (lt + 1) * LANES)
            wg, wv, bg, bv = wg_ref[:, lanes], wv_ref[:, lanes], bg_ref[:, lanes], bv_ref[:, lanes]

            def chunk(ci, acc):
                r0 = pl.multiple_of(ci * ch, ch)
                taps_g = _taps(_with_halo(ag_ref, r0, ci, n_chunks, ch, lanes, True, True))
                taps_v = _taps(_with_halo(av_ref, r0, ci, n_chunks, ch, lanes, True, True))
                dge = _with_halo(dg_ref, r0, ci, n_chunks, ch, lanes, False, True)
                gate, val = _conv3(taps_g, wg, bg), _conv3(taps_v, wv, bv)
                sg = jax.nn.sigmoid(gate)
                d_gate = dge * val * sg * (1.0 + gate * (1.0 - sg))
                d_val = dge * gate * sg
                new = []
                for half, (taps, w, d) in enumerate(((taps_g, wg, d_gate), (taps_v, wv, d_val))):
                    da = (w[2:3, :] * d[:ch] + w[1:2, :] * pltpu.roll(d, ext_rows - 1, 0)[:ch]
                          + w[0:1, :] * pltpu.roll(d, ext_rows - 2, 0)[:ch])
                    da_ref[half, pl.ds(r0, ch), lanes] = da.astype(da_ref.dtype)
                    dc = d[:ch]
                    sums = [fold(dc)] + [fold(dc * t[:ch]) for t in taps]
                    new.append(tuple(x + s for x, s in zip(acc[half], sums)))
                return tuple(new)

            zero = tuple(jnp.zeros((SUBLANES, LANES), F32) for _ in range(4))
            acc = lax.fori_loop(0, n_chunks, chunk, (zero, zero))
            row = lax.broadcasted_iota(jnp.int32, (SUBLANES, LANES), 0)
            for half in range(2):
                db, *taps = (jnp.sum(x, axis=0, keepdims=True) for x in acc[half])
                db_ref[half, :, lanes] = db
                dw = jnp.zeros((SUBLANES, LANES), F32)
                for tap in range(3):
                    dw = jnp.where(row == tap, taps[tap], dw)
                dw_ref[half, :, lanes] = dw

    lo = lambda j: (0, j)
    hi = lambda j: (0, j + nc)
    both = lambda j: (0, 0, j)
    return pl.pallas_call(
        body, name="conv_gate_bwd", grid=(nc,),
        in_specs=[pl.BlockSpec((rows, tc), lo), pl.BlockSpec((rows, tc), hi), pl.BlockSpec((SUBLANES, tc), lo),
                  pl.BlockSpec((SUBLANES, tc), hi), pl.BlockSpec((1, tc), lo), pl.BlockSpec((1, tc), hi),
                  pl.BlockSpec((rows, tc), lo)],
        out_specs=[pl.BlockSpec((2, rows, tc), both), pl.BlockSpec((2, SUBLANES, tc), both), pl.BlockSpec((2, 1, tc), both)],
        out_shape=[jax.ShapeDtypeStruct((2, rows, f), BF16), jax.ShapeDtypeStruct((2, SUBLANES, f), F32),
                   jax.ShapeDtypeStruct((2, 1, f), F32)],
        compiler_params=_params(("parallel",)),
    )(a, a, conv_w, conv_w, conv_b, conv_b, dg)


def _wgrad(a, b, rows, cols, row_sharded, name, **kw):
    return functools.partial(_wgrad_half, a, b, rows, cols, row_sharded, name, **kw)


def _block_diag(x):
    nb, g, r, c = x.shape
    eye = jnp.eye(g, dtype=x.dtype)
    return (x[:, :, :, None, :] * eye[None, :, None, :, None]).reshape(nb, g * r, g * c)


def _block_diag_part(x, r, c):
    nb = x.shape[0]
    g = GROUPS_PER_BATCH
    eye = jnp.eye(g, dtype=x.dtype)
    return jnp.sum(x.reshape(nb, g, r, g, c) * eye[None, :, None, :, None], axis=3)


class _NoExchange:
    def __init__(self, ffn):
        self.ffn = ffn

    def ffn_weights_arrived(self, after):
        return None

    def ffn_weights(self, after):
        return self.ffn

    def ffn_grads(self, makers, after):
        self.ffn_makers = makers
        return None

    def ffn_backward_done(self, after):
        return None


def _local_step(x, posf, target, w, hooks):
    rows, d = x.shape
    width = w["ssm_d"].shape[1]
    qr, kvr = w["mla_q_norm_w"].shape[1], w["mla_kv_norm_w"].shape[1]
    heads = w["mla_w_ukv"].shape[1] // HEAD_SLOT
    f2 = w["ffn_conv_b"].shape[1]
    inp = w["w_in"].shape[0]
    groups = width // SSM_GROUP
    nb = groups // GROUPS_PER_BATCH
    scale = (QK_NOPE_DIM + QK_ROPE_DIM) ** -0.5
    g = {}

    hn = _rmsnorm_fwd(x, w["attn_norm_w"], name="attn_norm")
    proj = _matmul(hn, w["w_in"], mode="nt", name="in_proj")

    ar, ai, bbr, bbi = _s5_params(w["ssm_lambda_re"], w["ssm_lambda_im"], w["ssm_log_dt"], w["ssm_b_re"], w["ssm_b_im"])

    def b_band(bb):
        return _block_diag(bb.reshape(nb, GROUPS_PER_BATCH, SSM_STATE, SSM_GROUP).transpose(0, 1, 3, 2))

    def c_band(c):
        return _block_diag(c.reshape(nb, GROUPS_PER_BATCH, SSM_GROUP, SSM_STATE).transpose(0, 1, 3, 2))

    wb = jnp.concatenate([b_band(bbr), b_band(bbi)], axis=2).astype(BF16)
    wc = jnp.concatenate([c_band(w["ssm_c_re"]), -c_band(w["ssm_c_im"])], axis=1).astype(BF16)
    abar = jnp.concatenate([ar.reshape(nb, 1, STATE_PER_BATCH), ai.reshape(nb, 1, STATE_PER_BATCH)], axis=2)
    states, y_pre, yg = _s5_fwd(proj, wb, wc, w["ssm_d"], abar)
    z = _matmul(yg, w["ssm_w_glu"], mode="nn", name="glu_proj", bias=w["ssm_b_glu"])
    ys = _glu_norm_fwd(y_pre, z, w["ssm_out_norm_w"])

    q_col, kv_col, kpe_col = width // qr, (width + qr) // kvr, (width + qr + kvr) // LANES
    assert width % qr == 0 and (width + qr) % kvr == 0
    qn = _rmsnorm_fwd(proj, w["mla_q_norm_w"], name="q_norm", width=qr, col=q_col)
    kvn = _rmsnorm_fwd(proj, w["mla_kv_norm_w"], name="kv_norm", width=kvr, col=kv_col)
    q = _matmul(qn, w["mla_w_uq"], mode="nn", name="q_proj")
    kv = _matmul(kvn, w["mla_w_ukv"], mode="nn", name="kv_proj")
    half = QK_ROPE_DIM // 2
    inv_freq = ROPE_THETA ** (-jnp.arange(0, QK_ROPE_DIM, 2, dtype=F32) / QK_ROPE_DIM)
    zeros = jnp.zeros((LANES - QK_ROPE_DIM,), F32)
    freq = jnp.concatenate([inv_freq, inv_freq, zeros]).reshape(1, LANES)
    sign = jnp.concatenate([-jnp.ones((half,), F32), jnp.ones((half,), F32), zeros]).reshape(1, LANES)
    cos, sin = _rope_tables(posf, freq, sign)
    qc, kc, vb = _attn_prep(q, kv, proj, kpe_col, cos, sin)
    o, lse = _attn_fwd(qc, kc, vb, scale=scale, tq=ATTN_BLOCK)
    ym = _rmsnorm_fwd(o, w["mla_out_norm_w"], name="mla_out_norm")
    ycat = jnp.concatenate([ys, ym], axis=1)
    h1 = _matmul(ycat, w["w_out"], mode="nn", name="out_proj", add=x, after=hooks.ffn_weights_arrived(ycat))

    hn2 = _rmsnorm_fwd(h1, w["ffn_norm_w"], name="ffn_norm")
    ffn = hooks.ffn_weights(hn2)
    a = _matmul(hn2, ffn["ffn_w_up"], mode="nn", name="ffn_up", tm=FFN_ROWS)
    gated = _conv_gate_fwd(a, ffn["ffn_conv_w"], w["ffn_conv_b"])
    h2 = _matmul(gated, ffn["ffn_w_down"], mode="nn", name="ffn_down", add=h1, tk=2816, tm=FFN_ROWS)
    loss_tile, dh2, dh2_mxu, g["final_norm_w"] = _final_norm_loss(h2, w["final_norm_w"], target)

    dgated = _matmul(dh2_mxu, ffn["ffn_w_down"], mode="nt", name="ffn_down_dx", tm=FFN_ROWS)
    da, dcw, dcb = _conv_gate_bwd(a, ffn["ffn_conv_w"], w["ffn_conv_b"], dgated)
    g["ffn_conv_w"] = jnp.concatenate([dcw[0, :3], dcw[1, :3]], axis=1)
    g["ffn_conv_b"] = jnp.concatenate([dcb[0], dcb[1]], axis=1)
    started = hooks.ffn_grads({
        "ffn_w_up": _wgrad(hn2, da, d, f2, False, "ffn_up_dw", b_split=True, tn=_tile(f2 // N_CHIPS, 1408)),
        "ffn_w_down": _wgrad(gated, dh2_mxu, f2 // 2, d, True, "ffn_down_dw", tm=f2 // 2 // N_CHIPS, tn=512)}, dcb)
    dhn2 = _matmul(da, ffn["ffn_w_up"], mode="nt", name="ffn_up_dx", a_split=True, tk=_tile(f2 // 2, 2816), tm=FFN_ROWS,
                   after=started)
    dh1, dh1_mxu, g["ffn_norm_w"] = _rmsnorm_bwd(h1, w["ffn_norm_w"], dhn2, name="ffn_norm_bwd", add=dh2,
                                                dx_dtypes=(F32, BF16))

    dycat = _matmul(dh1_mxu, w["w_out"], mode="nt", name="out_proj_dx")
    g["w_out"] = _wgrad(ycat, dh1_mxu, 2 * width, d, True, "out_proj_dw")
    started = hooks.ffn_backward_done(dycat)
    mla_out_norm_w, ssm_out_norm_w = w["mla_out_norm_w"], w["ssm_out_norm_w"]
    if started is not None:
        mla_out_norm_w, ssm_out_norm_w = mla_out_norm_w + started[:1, :1], ssm_out_norm_w + started[:1, :1]

    do, g["mla_out_norm_w"] = _rmsnorm_bwd(o, mla_out_norm_w, dycat, name="mla_out_norm_bwd", width=width, dy_col=1)
    dq, dkv, dkpe_heads = _attn_bwd(qc, kc, vb, o, do, lse, cos, sin, scale=scale, tk=ATTN_BLOCK)
    dkpe = _kpe_bwd(dkpe_heads, cos, sin)
    g["mla_w_uq"] = _wgrad(qn, dq, qr, heads * HEAD_SLOT, False, "q_proj_dw")
    dqn = _matmul(dq, w["mla_w_uq"], mode="nt", name="q_proj_dx")
    dcq, g["mla_q_norm_w"] = _rmsnorm_bwd(proj, w["mla_q_norm_w"], dqn, name="q_norm_bwd", width=qr, col=q_col,
                                          dx_dtypes=(BF16,))
    g["mla_w_ukv"] = _wgrad(kvn, dkv, kvr, heads * HEAD_SLOT, False, "kv_proj_dw")
    dkvn = _matmul(dkv, w["mla_w_ukv"], mode="nt", name="kv_proj_dx")
    dckv, g["mla_kv_norm_w"] = _rmsnorm_bwd(proj, w["mla_kv_norm_w"], dkvn, name="kv_norm_bwd", width=kvr, col=kv_col,
                                            dx_dtypes=(BF16,))

    dz, dyg_a, g["ssm_out_norm_w"], g["ssm_b_glu"] = _glu_norm_bwd(y_pre, z, ssm_out_norm_w, dycat)
    dyg_b = _matmul(dz, w["ssm_w_glu"], mode="nt", name="glu_proj_dx")
    g["ssm_w_glu"] = _wgrad(yg, dz, width, width, True, "glu_proj_dw")
    du, dwb, dwc, dabar, g["ssm_d"] = _s5_bwd(proj, states, y_pre, dyg_a, dyg_b, wb, wc, w["ssm_d"], abar)

    def b_unband(x):
        return _block_diag_part(x, SSM_GROUP, SSM_STATE).transpose(0, 1, 3, 2).reshape(groups, SSM_STATE * SSM_GROUP)

    def c_unband(x):
        return _block_diag_part(x, SSM_STATE, SSM_GROUP).transpose(0, 1, 3, 2).reshape(groups, SSM_GROUP, SSM_STATE)

    st = STATE_PER_BATCH
    g["ssm_c_re"] = c_unband(dwc[:, :st, :])
    g["ssm_c_im"] = -c_unband(dwc[:, st:, :])
    d_ar = dabar[:, 0, :st].reshape(groups, SSM_STATE)
    d_ai = dabar[:, 0, st:].reshape(groups, SSM_STATE)
    (g["ssm_lambda_re"], g["ssm_lambda_im"], g["ssm_log_dt"], g["ssm_b_re"], g["ssm_b_im"]) = _s5_params_bwd(
        w["ssm_lambda_re"], w["ssm_lambda_im"], w["ssm_log_dt"], w["ssm_b_re"], w["ssm_b_im"], d_ar, d_ai,
        b_unband(dwb[:, :, :st]), b_unband(dwb[:, :, st:]))

    pad = jnp.zeros((rows, inp - (width + qr + kvr + LANES)), BF16)
    dproj = jnp.concatenate([du, dcq, dckv, dkpe, pad], axis=1)
    g["w_in"] = _wgrad(dproj, hn, inp, d, False, "in_proj_dw")
    dhn = _matmul(dproj, w["w_in"], mode="nn", name="in_proj_dx")
    dx, g["attn_norm_w"] = _rmsnorm_bwd(x, w["attn_norm_w"], dhn, name="attn_norm_bwd", add=dh1)
    return loss_tile, dx, g


ANY = pl.BlockSpec(memory_space=pl.ANY)
MESH = pl.DeviceIdType.MESH


def _mesh_pos():
    return lax.axis_index("x"), lax.axis_index("y"), lax.axis_index("c")


def _other_chips(x, y):
    return [(1 - x, y), (x, 1 - y), (1 - x, 1 - y)]


def _remote(src, dst, send_sems, recv_sems, k, to):
    return pltpu.make_async_remote_copy(src_ref=src, dst_ref=dst, send_sem=send_sems.at[k], recv_sem=recv_sems.at[k],
                                        device_id=to, device_id_type=MESH)


def _place_shard(shard, piece_idx, row_sharded, name, out_dtype=BF16, pieces=N_CHIPS):
    rs, cs = shard.shape
    tr = _tile(rs, 256, 2 * SUBLANES)
    rb = rs // tr

    def body(p_ref, x_ref, o_ref):
        o_ref[...] = x_ref[...].astype(o_ref.dtype)

    if row_sharded:
        out_shape, out_map = (pieces * rs, cs), (lambda i, p_ref: (p_ref[0] * rb + i, 0))
    else:
        out_shape, out_map = (rs, pieces * cs), (lambda i, p_ref: (i, p_ref[0]))
    return pl.pallas_call(
        body, name=name, out_shape=jax.ShapeDtypeStruct(out_shape, out_dtype),
        grid_spec=pltpu.PrefetchScalarGridSpec(
            num_scalar_prefetch=1, grid=(rb,), in_specs=[pl.BlockSpec((tr, cs), lambda i, p_ref: (i, 0))],
            out_specs=pl.BlockSpec((tr, cs), out_map)),
        compiler_params=_params(("parallel",)),
    )(piece_idx, shard)


def _gather_weights(placed, name):
    n = len(placed)
    meta = [(row_sharded, direct) for _, row_sharded, direct in placed]
    over_ici, over_d2d = _gather_plans(meta)
    forwarded = [t for t, (_, direct) in enumerate(meta) if not direct]

    def body(*refs):
        outs = refs[n:2 * n]
        send_sems, recv_sems, pass_send_sems, pass_recv_sems = refs[2 * n:]
        first, arrivals = over_ici(outs, send_sems, recv_sems)
        passed, passed_arrivals = over_d2d([outs[t] for t in forwarded], pass_send_sems, pass_recv_sems)
        for cp in first:
            cp.start()
        for t in range(n):
            for j in range(3):
                arrivals[3 * t + j].wait_recv()
                if t in forwarded:
                    passed[3 * forwarded.index(t) + j].start()
        for cp in passed_arrivals:
            cp.wait_recv()
        for cp in first + passed:
            cp.wait_send()

    return pl.pallas_call(
        body, name=name, in_specs=[ANY] * n, out_specs=[ANY] * n,
        out_shape=[jax.ShapeDtypeStruct(arr.shape, arr.dtype) for arr, _, _ in placed],
        input_output_aliases={t: t for t in range(n)},
        scratch_shapes=[pltpu.SemaphoreType.DMA((3 * n,)), pltpu.SemaphoreType.DMA((3 * n,)),
                        pltpu.SemaphoreType.DMA((3 * len(forwarded),)), pltpu.SemaphoreType.DMA((3 * len(forwarded),))],
    )(*[arr for arr, _, _ in placed])


def _gather_plans(meta):
    def window(ref, row_sharded, piece, half):
        r, cc = ref.shape
        if row_sharded:
            rs = r // N_CHIPS
            if half is None:
                return ref.at[pl.ds(piece * rs, rs), :]
            return ref.at[pl.ds(piece * rs + half * (rs // 2), rs // 2), :]
        cs = cc // N_CHIPS
        if half is None:
            return ref.at[:, pl.ds(piece * cs, cs)]
        return ref.at[pl.ds(half * (r // 2), r // 2), pl.ds(piece * cs, cs)]

    def over_ici(refs, send_sems, recv_sems):
        x, y, c = _mesh_pos()
        sends, recvs = [], []
        for t, (row_sharded, direct) in enumerate(meta):
            mine = window(refs[t], row_sharded, 2 * x + y, None if direct else c)
            for j, (px, py) in enumerate(_other_chips(x, y)):
                theirs = window(refs[t], row_sharded, 2 * px + py, None if direct else c)
                sends.append(_remote(mine, mine, send_sems, recv_sems, 3 * t + j, (px, py, c)))
                recvs.append(_remote(theirs, theirs, send_sems, recv_sems, 3 * t + j, (px, py, c)))
        return sends, recvs

    def over_d2d(refs, send_sems, recv_sems):
        x, y, c = _mesh_pos()
        sends, recvs = [], []
        rows = [row_sharded for row_sharded, direct in meta if not direct]
        for t, row_sharded in enumerate(rows):
            for j, (px, py) in enumerate(_other_chips(x, y)):
                got = window(refs[t], row_sharded, 2 * px + py, c)
                other = window(refs[t], row_sharded, 2 * px + py, 1 - c)
                sends.append(_remote(got, got, send_sems, recv_sems, 3 * t + j, (x, y, 1 - c)))
                recvs.append(_remote(other, other, send_sems, recv_sems, 3 * t + j, (x, y, 1 - c)))
        return sends, recvs

    return over_ici, over_d2d


HBM = pl.BlockSpec(memory_space=pltpu.HBM)
SEMAPHORES = pl.BlockSpec(memory_space=pltpu.SEMAPHORE)
DATAFLOW = pltpu.SideEffectType.DATAFLOW_SIDE_EFFECTING


def _start_copies(name, arrays, plan, n_copies, after):
    n = len(arrays)

    def body(*refs):
        sends, _ = plan(refs[:n], refs[n + 1], refs[n + 2])
        for cp in sends:
            cp.start()
        token = refs[2 * n + 3]
        token[...] = jnp.zeros_like(token)

    out = pl.pallas_call(
        body, name=name,
        out_shape=(pltpu.SemaphoreType.DMA((n_copies,)), pltpu.SemaphoreType.DMA((n_copies,)),
                   *[pltpu.HBM(a.shape, a.dtype) for a in arrays], jax.ShapeDtypeStruct((SUBLANES, LANES), F32)),
        in_specs=[HBM] * n + [ANY],
        out_specs=(SEMAPHORES, SEMAPHORES, *[HBM] * n, pl.BlockSpec(memory_space=pltpu.VMEM)),
        input_output_aliases={t: t + 2 for t in range(n)},
        compiler_params=pltpu.CompilerParams(has_side_effects=DATAFLOW),
    )(*[pltpu.with_memory_space_constraint(a, pltpu.HBM) for a in arrays], after)
    return out[0], out[1], list(out[2:2 + n]), out[2 + n]


def _wait_copies(name, started, plan, after):
    send_sems, recv_sems, arrays, _ = started
    n = len(arrays)

    def body(*refs):
        sends, recvs = plan(refs[:n], refs[n], refs[n + 1])
        for cp in sends:
            cp.wait_send()
        for cp in recvs:
            cp.wait_recv()

    out = pl.pallas_call(
        body, name=name, out_shape=[pltpu.HBM(a.shape, a.dtype) for a in arrays],
        in_specs=[HBM] * n + [SEMAPHORES, SEMAPHORES, ANY], out_specs=[HBM] * n,
        input_output_aliases={t: t for t in range(n)},
        compiler_params=pltpu.CompilerParams(has_side_effects=DATAFLOW),
    )(*arrays, send_sems, recv_sems, after)
    return list(out)


def _exchange(name, arrays, out_shapes, plan, n_copies, in_place=False, after=None):
    n = len(arrays)
    extra = [] if after is None else [after]

    def body(*refs):
        ins, outs = refs[:n], refs[n + len(extra):n + len(extra) + len(out_shapes)]
        send_sems, recv_sems = refs[n + len(extra) + len(out_shapes):]
        sends, recvs = plan(ins, outs, send_sems, recv_sems)
        for cp in sends:
            cp.start()
        for cp in recvs:
            cp.wait_recv()
        for cp in sends:
            cp.wait_send()

    return pl.pallas_call(
        body, name=name, in_specs=[ANY] * (n + len(extra)), out_specs=[ANY] * len(out_shapes), out_shape=out_shapes,
        input_output_aliases={t: t for t in range(n)} if in_place else {},
        scratch_shapes=[pltpu.SemaphoreType.DMA((n_copies,)), pltpu.SemaphoreType.DMA((n_copies,))],
    )(*arrays, *extra)


def _give_plan(n):
    def plan(refs, send_sems, recv_sems):
        x, y, c = _mesh_pos()
        sends = [_remote(refs[t], refs[n + t], send_sems, recv_sems, t, (x, y, 1 - c)) for t in range(n)]
        return sends, sends

    return plan


def _scatter_plan(n):
    def plan(refs, send_sems, recv_sems):
        x, y, c = _mesh_pos()
        sends = []
        for t in range(n):
            for j, (px, py) in enumerate(_other_chips(x, y)):
                sends.append(_remote(refs[t].at[2 * px + py], refs[n + t].at[j], send_sems, recv_sems, 3 * t + j, (px, py, c)))
        return sends, sends

    return plan


def _scatter_shapes(sums):
    return [jax.ShapeDtypeStruct((3,) + s.shape[1:], s.dtype) for s in sums]


def _join_halves(halves, name, after=None):
    def plan(ins, outs, send_sems, recv_sems):
        x, y, c = _mesh_pos()
        sends = [_remote(outs[t].at[c], outs[t].at[c], send_sems, recv_sems, t, (x, y, 1 - c)) for t in range(len(ins))]
        recvs = [_remote(outs[t].at[1 - c], outs[t].at[1 - c], send_sems, recv_sems, t, (x, y, 1 - c))
                 for t in range(len(ins))]
        return sends, recvs

    shapes = [jax.ShapeDtypeStruct(h.shape, h.dtype) for h in halves]
    return _exchange(name, halves, shapes, plan, len(halves), in_place=True, after=after)


def _add_other_half(g4, got, where, name, wire_dtype=BF16):
    _, pieces, sr, sc = g4.shape
    tr = _tile(sr, 256, 2 * SUBLANES)

    def body(w_ref, a_ref, b_ref, o_ref):
        o_ref[...] = (a_ref[...] + b_ref[...]).astype(o_ref.dtype)

    blk = pl.BlockSpec((None, tr, sc), lambda p, i, w_ref: (p, i, 0))
    return pl.pallas_call(
        body, name=name, out_shape=jax.ShapeDtypeStruct((pieces, sr, sc), wire_dtype),
        grid_spec=pltpu.PrefetchScalarGridSpec(
            num_scalar_prefetch=1, grid=(pieces, sr // tr),
            in_specs=[pl.BlockSpec((None, None, tr, sc), lambda p, i, w_ref: (w_ref[0], p, i, 0)), blk], out_specs=blk),
        compiler_params=_params(("parallel", "parallel")),
    )(where, g4, got)


def _add_pieces(sums, got_pieces, where, name):
    _, sr, sc = sums.shape
    tr = _tile(sr, 256, 2 * SUBLANES)

    def body(w_ref, a_ref, r_ref, o_ref):
        acc = a_ref[...]
        for j in range(3):
            acc = acc + r_ref[j].astype(F32)
        o_ref[...] = acc

    return pl.pallas_call(
        body, name=name, out_shape=jax.ShapeDtypeStruct((N_CORES, sr, sc), F32),
        grid_spec=pltpu.PrefetchScalarGridSpec(
            num_scalar_prefetch=1, grid=(sr // tr,),
            in_specs=[pl.BlockSpec((None, tr, sc), lambda i, w_ref: (w_ref[1], i, 0)),
                      pl.BlockSpec((3, tr, sc), lambda i, w_ref: (0, i, 0))],
            out_specs=pl.BlockSpec((None, tr, sc), lambda i, w_ref: (w_ref[0], i, 0))),
        compiler_params=_params(("parallel",)),
    )(where, sums, got_pieces)


def _adamw_update(w, g, m, v):
    nm = ADAM_B1 * m + (1.0 - ADAM_B1) * g
    nv = ADAM_B2 * v + (1.0 - ADAM_B2) * (g * g)
    m_hat = nm / (1.0 - ADAM_B1 ** ADAM_STEP)
    v_hat = nv / (1.0 - ADAM_B2 ** ADAM_STEP)
    return -ADAM_LR * (m_hat / (jnp.sqrt(v_hat) + ADAM_EPS) + ADAM_WD * w), nm, nv


def _adamw(w, g, m, v, name, after=None):
    rows, cols = w.shape
    halves = 2 if g.ndim == 3 else 1
    bc = cols // halves
    tr = _tile(rows, max(SUBLANES, (1 << 19) // max(bc, 1) // SUBLANES * SUBLANES), SUBLANES)

    def body(w_ref, g_ref, m_ref, v_ref, *rest):
        d_ref, nm_ref, nv_ref, go_ref = rest[-4:]
        gv = g_ref[...]
        d_ref[...], nm_ref[...], nv_ref[...] = _adamw_update(w_ref[...], gv, m_ref[...], v_ref[...])
        go_ref[...] = gv

    blk = pl.BlockSpec((tr, bc), lambda i, h: (i, h))
    g_blk = pl.BlockSpec((None, tr, bc), lambda i, h: (h, i, 0)) if halves == 2 else blk
    extra = [] if after is None else [after]
    return pl.pallas_call(
        body, name=name, grid=(rows // tr, halves),
        in_specs=[blk, g_blk, blk, blk] + [pl.BlockSpec(memory_space=pl.ANY)] * len(extra), out_specs=[blk] * 4,
        out_shape=[jax.ShapeDtypeStruct((rows, cols), F32)] * 4, compiler_params=_params(("parallel", "parallel")),
    )(w, g, m, v, *extra)


def _adamw_many(ws, gs, ms, vs, name):
    n = len(ws)

    def body(*refs):
        outs = refs[4 * n:]
        for k in range(n):
            w_ref, g_ref, m_ref, v_ref = (refs[j * n + k] for j in range(4))
            outs[k][...], outs[n + k][...], outs[2 * n + k][...] = _adamw_update(w_ref[...], g_ref[...], m_ref[...], v_ref[...])

    out = pl.pallas_call(
        body, name=name, out_shape=[jax.ShapeDtypeStruct(w.shape, F32) for w in ws] * 3,
        compiler_params=pltpu.CompilerParams(vmem_limit_bytes=VMEM_LIMIT_BYTES),
    )(*ws, *gs, *ms, *vs)
    return out[:n], out[n:2 * n], out[2 * n:]


WEIGHTS = ['attn_norm_w', 'w_in', 'ssm_lambda_re', 'ssm_lambda_im', 'ssm_log_dt', 'ssm_b_re', 'ssm_b_im', 'ssm_c_re',
           'ssm_c_im', 'ssm_d', 'ssm_w_glu', 'ssm_b_glu', 'mla_q_norm_w', 'mla_w_uq', 'mla_kv_norm_w', 'mla_w_ukv',
           'ssm_out_norm_w', 'mla_out_norm_w', 'w_out', 'ffn_norm_w', 'ffn_w_up', 'ffn_conv_w', 'ffn_conv_b',
           'ffn_w_down', 'final_norm_w']
SHARDED = {'w_in': False, 'ssm_w_glu': True, 'mla_w_uq': False, 'mla_w_ukv': False, 'w_out': True, 'ffn_w_up': False,
           'ffn_w_down': True}
SMALL = [n for n in WEIGHTS if n not in SHARDED and n != 'ffn_conv_w']
ROPE_PAD = HEAD_SLOT - QK_NOPE_DIM - QK_ROPE_DIM
SMALL_COLS = 8 * LANES


def _pad_heads(w_uq, heads):
    qr = w_uq.shape[0]
    w3 = w_uq.reshape(qr, heads, QK_NOPE_DIM + QK_ROPE_DIM)
    return jnp.concatenate([w3, jnp.zeros((qr, heads, ROPE_PAD), w_uq.dtype)], axis=2).reshape(qr, heads * HEAD_SLOT)


def _unpad_heads(g_uq, heads):
    qr = g_uq.shape[0]
    return g_uq.reshape(qr, heads, HEAD_SLOT)[:, :, :QK_NOPE_DIM + QK_ROPE_DIM].reshape(qr, -1)


FFN = ['ffn_w_up', 'ffn_w_down']
FFN_GATHER = FFN + ['ffn_conv_w']
FFN_GATHER_META = [(SHARDED[n], False) for n in FFN] + [(False, True)]


class _Overlapped:
    def __init__(self, placed, where, after):
        self.where, self.mine, self.other = where, where[:1], 1 - where[:1]
        self.over_ici, self.over_d2d = _gather_plans(FFN_GATHER_META)
        self.gather = _start_copies("gather_ffn_start", placed, self.over_ici, 3 * len(placed), after)
        self.gather_started = self.gather[3]

    def ffn_weights_arrived(self, after):
        arrived = _wait_copies("gather_ffn_wait", self.gather, self.over_ici, after)
        n = len(FFN)
        self.direct = arrived[n:]
        self.passing = _start_copies("gather_ffn_pass_start", arrived[:n], self.over_d2d, 3 * n, after)
        return self.passing[3]

    def ffn_weights(self, after):
        passed = _wait_copies("gather_ffn_pass_wait", self.passing, self.over_d2d, after)
        return dict(zip(FFN_GATHER, passed + self.direct))

    def ffn_grads(self, makers, after):
        self.makers = [makers[name] for name in FFN]
        n = len(FFN)
        give = [make(self.other, suffix="_give") for make in self.makers]
        lands = [lax.empty(g.shape, g.dtype) for g in give]
        self.swap = _start_copies("grad_ffn_swap_start", give + lands, _give_plan(n), n, after)
        return self.swap[3]

    def ffn_backward_done(self, after):
        n = len(FFN)
        got = _wait_copies("grad_ffn_swap_wait", self.swap, _give_plan(n), after)[n:]
        kept = [make(self.mine, suffix="_keep", add=got[t], wire=True) for t, make in enumerate(self.makers)]
        self.sums = [k[0] for k in kept]
        wires = [k[1] for k in kept]
        lands = [lax.empty(s.shape, s.dtype) for s in _scatter_shapes(wires)]
        self.scatter = _start_copies("grad_ffn_scatter_start", wires + lands, _scatter_plan(n), 3 * n, after)
        return self.scatter[3]

    def ffn_reduced(self, after):
        n = len(FFN)
        got_pieces = _wait_copies("grad_ffn_scatter_wait", self.scatter, _scatter_plan(n), after)[n:]
        return [_add_pieces(self.sums[t], got_pieces[t], self.where, "grad_add_pieces_" + name) for t, name in enumerate(FFN)]


def _step(args):
    x, positions, target = args["x"][0], args["positions"], args["loss_target"][0]
    rows = x.shape[0]
    p = {n: args[n] for n in WEIGHTS}
    xi, yi, ci = _mesh_pos()
    piece = 2 * xi + yi

    def transposed(a):
        return jnp.swapaxes(a[0], 0, 1)

    w_in = transposed(p["w_in"])
    in_width = w_in.shape[0]
    in_pad = (-in_width) % (2 * LANES)
    heads_here = p["mla_w_uq"].shape[2] // (QK_NOPE_DIM + QK_ROPE_DIM)
    shards = {
        "w_in": jnp.pad(w_in, ((0, in_pad), (0, 0))),
        "ssm_w_glu": p["ssm_w_glu"][0],
        "mla_w_uq": _pad_heads(p["mla_w_uq"][0], heads_here),
        "mla_w_ukv": p["mla_w_ukv"][0],
        "w_out": p["w_out"][0],
        "ffn_w_up": p["ffn_w_up"][0],
        "ffn_w_down": p["ffn_w_down"][0],
    }
    conv_w = jnp.pad(p["ffn_conv_w"][0], ((0, SUBLANES - p["ffn_conv_w"].shape[1]), (0, 0)))
    order = list(SHARDED)
    piece_idx = piece.reshape(1).astype(jnp.int32)
    placed = {n: _place_shard(shards[n], piece_idx, SHARDED[n], "place_" + n) for n in order}
    placed["ffn_conv_w"] = _place_shard(conv_w, piece_idx, False, "place_ffn_conv_w", out_dtype=F32)
    mixer = [n for n in order if n not in FFN]
    w = dict(zip(mixer, _gather_weights([(placed[n], SHARDED[n], False) for n in mixer], "gather_mixer_weights")))
    where = jnp.stack([ci, piece]).astype(jnp.int32)
    hooks = _Overlapped([placed[n] for n in FFN_GATHER], where, after=w["w_in"])
    groups = p["ssm_lambda_re"].shape[1]
    w.update({
        "attn_norm_w": p["attn_norm_w"] + hooks.gather_started[:1, :1],
        "ssm_lambda_re": p["ssm_lambda_re"][0], "ssm_lambda_im": p["ssm_lambda_im"][0],
        "ssm_log_dt": p["ssm_log_dt"].reshape(groups, 1), "ssm_b_re": p["ssm_b_re"].reshape(groups, -1),
        "ssm_b_im": p["ssm_b_im"].reshape(groups, -1), "ssm_c_re": p["ssm_c_re"][0], "ssm_c_im": p["ssm_c_im"][0],
        "ssm_d": p["ssm_d"], "ssm_b_glu": p["ssm_b_glu"], "mla_q_norm_w": p["mla_q_norm_w"],
        "mla_kv_norm_w": p["mla_kv_norm_w"], "ssm_out_norm_w": p["ssm_out_norm_w"], "mla_out_norm_w": p["mla_out_norm_w"],
        "ffn_norm_w": p["ffn_norm_w"], "ffn_conv_b": p["ffn_conv_b"], "final_norm_w": p["final_norm_w"].reshape(1, -1),
    })

    loss_tile, dx, g = _local_step(x, positions.reshape(rows, 1).astype(F32), target, w, hooks)
    loss = lax.psum(loss_tile[0, 0], ("x", "y", "c"))

    flat = [g[n].reshape(-1) for n in SMALL] + [g["ffn_conv_w"].reshape(-1)]
    sizes = [f.shape[0] for f in flat]
    per_block = -(-sum(sizes) // (N_CORES * N_CHIPS * SMALL_COLS))
    small_rows = -(-per_block // (2 * SUBLANES)) * (2 * SUBLANES)
    padded = N_CORES * N_CHIPS * small_rows * SMALL_COLS

    def pack(parts):
        parts = list(parts)
        have = sum(q.shape[0] for q in parts)
        return jnp.concatenate(parts + [jnp.zeros((padded - have,), F32)])

    reduced = mixer + ["small"]
    small = pack(flat).reshape(N_CORES, N_CHIPS, small_rows, SMALL_COLS)
    give = [g[n](hooks.other, suffix="_give") for n in mixer] + [lax.dynamic_index_in_dim(small, 1 - ci, 0, keepdims=False)]
    lands = [lax.empty(a.shape, a.dtype) for a in give]
    give_plan = _give_plan(len(reduced))
    swap = _start_copies("grad_mixer_swap_start", give + lands, give_plan, len(reduced), dx)

    grads, delta, new_m, new_v = {}, {}, {}, {}

    def finish(n, joined, after=None):
        grad = joined if SHARDED[n] else joined.reshape(-1, joined.shape[2])
        if n == "w_in":
            wt, mt, vt = w_in, transposed(args["m_w_in"]), transposed(args["v_w_in"])
            out = _adamw(wt, grad, mt, vt, "adamw_w_in")
            delta[n], new_m[n], new_v[n], grads[n] = (jnp.swapaxes(a, 0, 1)[None] for a in out)
            return
        if n == "mla_w_uq":
            grad = _unpad_heads(grad, heads_here)
        adam(n, grad, after)

    def adam(n, grad, after=None):
        shape = p[n].shape
        out = _adamw(p[n].reshape(shape[1:]), grad, args["m_" + n].reshape(shape[1:]),
                     args["v_" + n].reshape(shape[1:]), "adamw_" + n, after)
        delta[n], new_m[n], new_v[n], grads[n] = (a.reshape(shape) for a in out)

    ffn_joined = _join_halves(hooks.ffn_reduced(dx), "grad_ffn_join_halves", after=swap[3])
    got = _wait_copies("grad_mixer_swap_wait", swap, give_plan, ffn_joined[0])[len(reduced):]
    kept = [g[n](hooks.mine, suffix="_keep", add=got[t], wire=True) for t, n in enumerate(mixer)]
    small_sum = _add_other_half(small, got[-1], where, "grad_add_half_small", F32)
    sums = [k[0] for k in kept] + [small_sum]
    wires = [k[1] for k in kept] + [small_sum]
    lands = [lax.empty(s.shape, s.dtype) for s in _scatter_shapes(wires)]
    scatter_plan = _scatter_plan(len(reduced))
    scatter = _start_copies("grad_mixer_scatter_start", wires + lands, scatter_plan, 3 * len(reduced), kept[0][0])
    for n, joined in zip(FFN, ffn_joined):
        finish(n, joined, after=scatter[3])
    got_pieces = _wait_copies("grad_mixer_scatter_wait", scatter, scatter_plan, delta[FFN[-1]])[len(reduced):]
    halves = [_add_pieces(sums[t], got_pieces[t], where, "grad_add_pieces_" + n) for t, n in enumerate(reduced)]
    joined = _join_halves(halves, "grad_join_halves")
    for n, j in zip(mixer, joined):
        finish(n, j)
    eighths = _place_shard(joined[-1].reshape(N_CORES * small_rows, SMALL_COLS), piece_idx, True, "place_small_grads",
                           out_dtype=F32)
    small_sum = _gather_weights([(eighths, True, False)], "gather_small_grads")[0]
    flat_sum = small_sum.reshape(N_CHIPS, N_CORES, small_rows * SMALL_COLS).transpose(1, 0, 2).reshape(-1)
    offs = [0]
    for s in sizes:
        offs.append(offs[-1] + s)
    for k, n in enumerate(SMALL):
        grads[n] = flat_sum[offs[k]:offs[k + 1]].reshape(p[n].shape)
    taps, cols_here = p["ffn_conv_w"].shape[1], p["ffn_conv_w"].shape[2]
    conv_full = flat_sum[offs[len(SMALL)]:offs[len(SMALL) + 1]].reshape(taps, N_CHIPS * cols_here)
    adam("ffn_conv_w", lax.dynamic_slice_in_dim(conv_full, piece * cols_here, cols_here, axis=1))

    def rank2(a):
        return a.reshape(1, -1) if a.ndim == 1 else a

    d_s, m_s, v_s = _adamw_many([rank2(p[n]) for n in SMALL], [rank2(grads[n]) for n in SMALL],
                                [rank2(args["m_" + n]) for n in SMALL], [rank2(args["v_" + n]) for n in SMALL], "adamw_small")
    for k, n in enumerate(SMALL):
        delta[n], new_m[n], new_v[n] = (a.reshape(p[n].shape) for a in (d_s[k], m_s[k], v_s[k]))

    return (loss, dx[None], *[grads[n] for n in WEIGHTS], *[delta[n] for n in WEIGHTS],
            *[new_m[n] for n in WEIGHTS], *[new_v[n] for n in WEIGHTS])


def kernel(x, positions, attn_norm_w, w_in, ssm_lambda_re, ssm_lambda_im, ssm_log_dt, ssm_b_re, ssm_b_im, ssm_c_re, ssm_c_im, ssm_d, ssm_w_glu, ssm_b_glu, mla_q_norm_w, mla_w_uq, mla_kv_norm_w, mla_w_ukv, ssm_out_norm_w, mla_out_norm_w, w_out, ffn_norm_w, ffn_w_up, ffn_conv_w, ffn_conv_b, ffn_w_down, final_norm_w, loss_target, m_attn_norm_w, m_w_in, m_ssm_lambda_re, m_ssm_lambda_im, m_ssm_log_dt, m_ssm_b_re, m_ssm_b_im, m_ssm_c_re, m_ssm_c_im, m_ssm_d, m_ssm_w_glu, m_ssm_b_glu, m_mla_q_norm_w, m_mla_w_uq, m_mla_kv_norm_w, m_mla_w_ukv, m_ssm_out_norm_w, m_mla_out_norm_w, m_w_out, m_ffn_norm_w, m_ffn_w_up, m_ffn_conv_w, m_ffn_conv_b, m_ffn_w_down, m_final_norm_w, v_attn_norm_w, v_w_in, v_ssm_lambda_re, v_ssm_lambda_im, v_ssm_log_dt, v_ssm_b_re, v_ssm_b_im, v_ssm_c_re, v_ssm_c_im, v_ssm_d, v_ssm_w_glu, v_ssm_b_glu, v_mla_q_norm_w, v_mla_w_uq, v_mla_kv_norm_w, v_mla_w_ukv, v_ssm_out_norm_w, v_mla_out_norm_w, v_w_out, v_ffn_norm_w, v_ffn_w_up, v_ffn_conv_w, v_ffn_conv_b, v_ffn_w_down, v_final_norm_w):
    return _step(dict(locals()))
```

```python
import functools
import math

import jax
import jax.numpy as jnp
from jax import lax
from jax.experimental import pallas as pl
from jax.experimental.pallas import tpu as pltpu

F32 = jnp.float32
BF16 = jnp.bfloat16

SSM_GROUP = 16
SSM_STATE = 64
QK_NOPE_DIM = 128
QK_ROPE_DIM = 64
V_HEAD_DIM = 128
ROPE_THETA = 10000.0
RMS_EPS = 1e-6
ADAM_LR, ADAM_B1, ADAM_B2, ADAM_EPS, ADAM_WD, ADAM_STEP = 0.001, 0.9, 0.999, 1e-08, 0.01, 10

LANES = 128
SUBLANES = 8
VMEM_LIMIT_BYTES = 56 * 1024 * 1024

GROUPS_PER_BATCH = LANES // SSM_GROUP
STATE_PER_BATCH = GROUPS_PER_BATCH * SSM_STATE
HEAD_SLOT = 2 * LANES
NEG_INF = -1e30
ATTN_BLOCK = 512
FFN_ROWS = 1024

N_CHIPS = 4
N_CORES = 2


def _tile(n, pref, align=LANES):
    if n <= pref:
        return n
    t = (pref // align) * align
    while t >= align:
        if n % t == 0:
            return t
        t -= align
    return n


def _params(sem):
    return pltpu.CompilerParams(dimension_semantics=sem, vmem_limit_bytes=VMEM_LIMIT_BYTES)


def _dot(a, b, dims):
    return lax.dot_general(a, b, (dims, ((), ())), preferred_element_type=F32)


def _dot_nn(a, b):
    return _dot(a, b, ((1,), (0,)))


def _dot_nt(a, b):
    return _dot(a, b, ((1,), (1,)))


def _dot_tn(a, b):
    return _dot(a, b, ((0,), (0,)))


def _matmul(a, b, *, mode, name, tm=512, tn=1024, tk=2048, bias=None, add=None, out_dtype=F32,
            out_blocks=None, a_split=False, b_split=False, after=None):
    if a_split:
        assert mode == "nt"
        a_shape = (a.shape[1], 2 * a.shape[2])
    else:
        a_shape = a.shape
    if b_split:
        assert mode == "tn"
        b_shape = (b.shape[1], 2 * b.shape[2])
    else:
        b_shape = b.shape
    if mode == "nn":
        (m, k), (k2, n) = a_shape, b_shape
    elif mode == "nt":
        (m, k), (n, k2) = a_shape, b_shape
    else:
        (k, m), (k2, n) = a_shape, b_shape
    assert k == k2, (a.shape, b.shape, mode)
    tm, tn, tk = _tile(m, tm, SUBLANES), _tile(n, tn), _tile(k, tk)
    nk = k // tk
    a_spec = {"nn": pl.BlockSpec((tm, tk), lambda i, j, kk: (i, kk)),
              "nt": pl.BlockSpec((tm, tk), lambda i, j, kk: (i, kk)),
              "tn": pl.BlockSpec((tk, tm), lambda i, j, kk: (kk, i))}[mode]
    b_spec = {"nn": pl.BlockSpec((tk, tn), lambda i, j, kk: (kk, j)),
              "nt": pl.BlockSpec((tn, tk), lambda i, j, kk: (j, kk)),
              "tn": pl.BlockSpec((tk, tn), lambda i, j, kk: (kk, j))}[mode]
    if a_split:
        kb = a.shape[2] // tk
        assert a.shape[2] % tk == 0
        a_spec = pl.BlockSpec((None, tm, tk), lambda i, j, kk: (kk // kb, i, kk % kb))
    if b_split:
        nb = b.shape[2] // tn
        assert b.shape[2] % tn == 0
        b_spec = pl.BlockSpec((None, tk, tn), lambda i, j, kk: (j // nb, kk, j % nb))
    dot = {"nn": _dot_nn, "nt": _dot_nt, "tn": _dot_tn}[mode]
    in_specs, operands = [a_spec, b_spec], [a, b]
    if bias is not None:
        in_specs.append(pl.BlockSpec((1, tn), lambda i, j, kk: (0, j)))
        operands.append(bias)
    if add is not None:
        in_specs.append(pl.BlockSpec((tm, tn), lambda i, j, kk: (i, j)))
        operands.append(add)
    if after is not None:
        in_specs.append(pl.BlockSpec(memory_space=pl.ANY))
        operands.append(after)

    def body(*refs):
        a_ref, b_ref = refs[0], refs[1]
        rest = list(refs[2:])
        bias_ref = rest.pop(0) if bias is not None else None
        add_ref = rest.pop(0) if add is not None else None
        if after is not None:
            rest.pop(0)
        o_ref, acc_ref = rest

        def finish(acc):
            if bias_ref is not None:
                acc = acc + bias_ref[...]
            if add_ref is not None:
                acc = acc + add_ref[...]
            o_ref[...] = acc.astype(o_ref.dtype)

        part = dot(a_ref[...].astype(BF16), b_ref[...].astype(BF16))
        if nk == 1:
            finish(part)
        else:
            kk = pl.program_id(2)

            @pl.when(kk == 0)
            def _():
                acc_ref[...] = part

            @pl.when(jnp.logical_and(kk > 0, kk < nk - 1))
            def _():
                acc_ref[...] += part

            @pl.when(kk == nk - 1)
            def _():
                finish(acc_ref[...] + part)

    if out_blocks is None:
        out_shape = jax.ShapeDtypeStruct((m, n), out_dtype)
        out_spec = pl.BlockSpec((tm, tn), lambda i, j, kk: (i, j))
    else:
        shape, block, index_map = out_blocks(tm, tn)
        out_shape = jax.ShapeDtypeStruct(shape, out_dtype)
        out_spec = pl.BlockSpec(block, index_map)
    acc_shape = (tm, tn) if nk > 1 else (SUBLANES, LANES)
    return pl.pallas_call(
        body, name=name, grid=(m // tm, n // tn, nk), in_specs=in_specs, out_specs=out_spec, out_shape=out_shape,
        scratch_shapes=[pltpu.VMEM(acc_shape, F32)],
        compiler_params=_params(("parallel", "parallel", "arbitrary")),
    )(*operands)


def _wgrad_half(a, b, rows, cols, row_sharded, name, which, *, suffix="", add=None, wire=False, tm=None, tn=None,
                b_split=False):
    tokens = a.shape[0]
    if row_sharded:
        sr, sc = rows // N_CHIPS, cols // N_CORES
    else:
        sr, sc = rows // N_CORES, cols // N_CHIPS
    tm = _tile(sr, 512) if tm is None else tm
    tn = _tile(sc, 1024) if tn is None else tn
    assert sr % tm == 0 and sc % tn == 0, (rows, cols, tm, tn)
    rb, cb = sr // tm, sc // tn
    if tn >= tm:
        ij, grid = (lambda s, t: (t, s)), (N_CHIPS, cb, rb)
    else:
        ij, grid = (lambda s, t: (s, t)), (N_CHIPS, rb, cb)
    if row_sharded:
        a_tile = lambda p, i, j, h: p * rb + i
        b_tile = lambda p, i, j, h: h[0] * cb + j
    else:
        a_tile = lambda p, i, j, h: h[0] * rb + i
        b_tile = lambda p, i, j, h: p * cb + j
    a_spec = pl.BlockSpec((tokens, tm), lambda p, s, t, h: (0, a_tile(p, *ij(s, t), h)))
    if b_split:
        nbh = b.shape[2] // tn
        assert b.shape[2] % tn == 0
        b_spec = pl.BlockSpec((None, tokens, tn), lambda p, s, t, h: (b_tile(p, *ij(s, t), h) // nbh, 0,
                                                                       b_tile(p, *ij(s, t), h) % nbh))
    else:
        b_spec = pl.BlockSpec((tokens, tn), lambda p, s, t, h: (0, b_tile(p, *ij(s, t), h)))
    out_spec = pl.BlockSpec((None, tm, tn), lambda p, s, t, h: (p, *ij(s, t)))
    in_specs, operands = [a_spec, b_spec], [a, b]
    if add is not None:
        in_specs.append(out_spec)
        operands.append(add)

    def body(h_ref, a_ref, b_ref, *rest):
        acc = _dot_tn(a_ref[...].astype(BF16), b_ref[...].astype(BF16))
        if add is not None:
            acc = acc + rest[0][...]
        for o_ref in rest[1 if add is not None else 0:]:
            o_ref[...] = acc.astype(o_ref.dtype)

    out_dtypes = [F32, BF16] if wire else [F32]
    out = pl.pallas_call(
        body, name=name + suffix, out_shape=[jax.ShapeDtypeStruct((N_CHIPS, sr, sc), dt) for dt in out_dtypes],
        grid_spec=pltpu.PrefetchScalarGridSpec(num_scalar_prefetch=1, grid=grid, in_specs=in_specs,
                                               out_specs=[out_spec] * len(out_dtypes)),
        compiler_params=_params(("parallel", "parallel", "parallel")),
    )(which, *operands)
    return tuple(out) if wire else out[0]


def _rms_rows(x):
    return lax.rsqrt(jnp.mean(x * x, axis=-1, keepdims=True) + RMS_EPS)


def _rmsnorm_fwd(x, w, *, name, width=None, col=0, out_dtype=BF16, tr=256):
    rows = x.shape[0]
    width = x.shape[1] if width is None else width
    tr = _tile(rows, tr, SUBLANES)

    def body(x_ref, w_ref, o_ref):
        xv = x_ref[...]
        o_ref[...] = (xv * _rms_rows(xv) * w_ref[...]).astype(o_ref.dtype)

    return pl.pallas_call(
        body, name=name, grid=(rows // tr,),
        in_specs=[pl.BlockSpec((tr, width), lambda i: (i, col)), pl.BlockSpec((1, width), lambda i: (0, 0))],
        out_specs=pl.BlockSpec((tr, width), lambda i: (i, 0)),
        out_shape=jax.ShapeDtypeStruct((rows, width), out_dtype),
        compiler_params=_params(("parallel",)),
    )(x, w)


def _rmsnorm_bwd_rows(xv, w, dy):
    r = _rms_rows(xv)
    n = xv * r
    dn = dy * w
    dx = r * (dn - n * jnp.mean(dn * n, axis=-1, keepdims=True))
    return dx, dy * n


def _rmsnorm_bwd(x, w, dy, *, name, width=None, col=0, dy_col=0, add=None, tr=256, dx_dtypes=(F32,)):
    rows = x.shape[0]
    n_dx = len(dx_dtypes)
    width = x.shape[1] if width is None else width
    tr = _tile(rows, tr, SUBLANES)
    in_specs = [pl.BlockSpec((tr, width), lambda i: (i, col)), pl.BlockSpec((1, width), lambda i: (0, 0)),
                pl.BlockSpec((tr, width), lambda i: (i, dy_col))]
    operands = [x, w, dy]
    if add is not None:
        in_specs.append(pl.BlockSpec((tr, width), lambda i: (i, 0)))
        operands.append(add)

    def body(*refs):
        x_ref, w_ref, dy_ref = refs[:3]
        add_ref = refs[3] if add is not None else None
        dx_refs, dw_ref = refs[-1 - n_dx:-1], refs[-1]
        dx, dwp = _rmsnorm_bwd_rows(x_ref[...], w_ref[...], dy_ref[...])
        if add_ref is not None:
            dx = dx + add_ref[...]
        for dx_ref in dx_refs:
            dx_ref[...] = dx.astype(dx_ref.dtype)
        part = jnp.sum(dwp, axis=0, keepdims=True)

        @pl.when(pl.program_id(0) == 0)
        def _():
            dw_ref[...] = part

        @pl.when(pl.program_id(0) > 0)
        def _():
            dw_ref[...] += part

    return pl.pallas_call(
        body, name=name, grid=(rows // tr,), in_specs=in_specs,
        out_specs=[pl.BlockSpec((tr, width), lambda i: (i, 0))] * n_dx + [pl.BlockSpec((1, width), lambda i: (0, 0))],
        out_shape=[jax.ShapeDtypeStruct((rows, width), dt) for dt in dx_dtypes] + [jax.ShapeDtypeStruct((1, width), F32)],
        compiler_params=_params(("arbitrary",)),
    )(*operands)


def _final_norm_loss(h, w, target, *, tr=256):
    rows, d = h.shape
    tr = _tile(rows, tr, SUBLANES)

    def body(h_ref, w_ref, t_ref, loss_ref, dh_ref, dhb_ref, dw_ref):
        hv, wv = h_ref[...], w_ref[...]
        r = _rms_rows(hv)
        n = hv * r
        err = n * wv - t_ref[...]
        d_out = err * (1.0 / d)
        dn = d_out * wv
        dh = r * (dn - n * jnp.mean(dn * n, axis=-1, keepdims=True))
        dh_ref[...] = dh
        dhb_ref[...] = dh.astype(BF16)
        dw_part = jnp.sum(d_out * n, axis=0, keepdims=True)
        loss_part = jnp.full((SUBLANES, LANES), 0.5 / d, F32) * jnp.sum(err * err)

        @pl.when(pl.program_id(0) == 0)
        def _():
            dw_ref[...] = dw_part
            loss_ref[...] = loss_part

        @pl.when(pl.program_id(0) > 0)
        def _():
            dw_ref[...] += dw_part
            loss_ref[...] += loss_part

    return pl.pallas_call(
        body, name="final_norm_loss", grid=(rows // tr,),
        in_specs=[pl.BlockSpec((tr, d), lambda i: (i, 0)), pl.BlockSpec((1, d), lambda i: (0, 0)),
                  pl.BlockSpec((tr, d), lambda i: (i, 0))],
        out_specs=[pl.BlockSpec((SUBLANES, LANES), lambda i: (0, 0)), pl.BlockSpec((tr, d), lambda i: (i, 0)),
                   pl.BlockSpec((tr, d), lambda i: (i, 0)), pl.BlockSpec((1, d), lambda i: (0, 0))],
        out_shape=[jax.ShapeDtypeStruct((SUBLANES, LANES), F32), jax.ShapeDtypeStruct((rows, d), F32),
                   jax.ShapeDtypeStruct((rows, d), BF16), jax.ShapeDtypeStruct((1, d), F32)],
        compiler_params=_params(("arbitrary",)),
    )(h, w, target)


def _cmul(ar, ai, br, bi):
    return ar * br - ai * bi, ar * bi + ai * br


def _expand_matrix(groups, reps):
    row = lax.broadcasted_iota(jnp.int32, (groups, groups * reps), 0)
    colg = lax.broadcasted_iota(jnp.int32, (groups, groups * reps), 1) // reps
    return (row == colg).astype(F32)


def _dot_exact(a, b, dims):
    return lax.dot_general(a, b, (dims, ((), ())), preferred_element_type=F32, precision=lax.Precision.HIGHEST)


def _s5_discretize(lr, li, dt):
    mag = jnp.exp(lr * dt)
    th = li * dt
    ar, ai = mag * jnp.cos(th), mag * jnp.sin(th)
    nr, ni = ar - 1.0, ai
    den = lr * lr + li * li
    zr = (nr * lr + ni * li) / den
    zi = (ni * lr - nr * li) / den
    return mag, ar, ai, nr, ni, den, zr, zi


def _s5_params(lam_re, lam_im, log_dt, b_re, b_im):
    g, p = lam_re.shape
    ph = b_re.shape[1]

    def body(lr_ref, li_ref, ldt_ref, br_ref, bi_ref, ar_ref, ai_ref, bbr_ref, bbi_ref):
        dt = jnp.exp(ldt_ref[...])
        _, ar, ai, _, _, _, zr, zi = _s5_discretize(lr_ref[...], li_ref[...], dt)
        ar_ref[...] = ar
        ai_ref[...] = ai
        e = _expand_matrix(p, ph // p)
        zr_x = _dot_exact(zr, e, ((1,), (0,)))
        zi_x = _dot_exact(zi, e, ((1,), (0,)))
        bre, bim = br_ref[...], bi_ref[...]
        bbr_ref[...] = zr_x * bre - zi_x * bim
        bbi_ref[...] = zr_x * bim + zi_x * bre

    return pl.pallas_call(
        body, name="s5_params",
        out_shape=[jax.ShapeDtypeStruct((g, p), F32)] * 2 + [jax.ShapeDtypeStruct((g, ph), F32)] * 2,
    )(lam_re, lam_im, log_dt, b_re, b_im)


def _s5_params_bwd(lam_re, lam_im, log_dt, b_re, b_im, d_ar, d_ai, d_bbr, d_bbi):
    g, p = lam_re.shape
    ph = b_re.shape[1]

    def body(lr_ref, li_ref, ldt_ref, br_ref, bi_ref, dar_ref, dai_ref, dbr_ref, dbi_ref,
             dlr_ref, dli_ref, dldt_ref, dbre_ref, dbim_ref):
        lr, li = lr_ref[...], li_ref[...]
        dt = jnp.exp(ldt_ref[...])
        mag, ar, ai, nr, ni, den, zr, zi = _s5_discretize(lr, li, dt)
        e = _expand_matrix(p, ph // p)
        zr_x = _dot_exact(zr, e, ((1,), (0,)))
        zi_x = _dot_exact(zi, e, ((1,), (0,)))
        bre, bim, dbr, dbi = br_ref[...], bi_ref[...], dbr_ref[...], dbi_ref[...]
        dbre_ref[...] = zr_x * dbr + zi_x * dbi
        dbim_ref[...] = zr_x * dbi - zi_x * dbr
        dzr = _dot_exact(bre * dbr + bim * dbi, e, ((1,), (1,)))
        dzi = _dot_exact(bre * dbi - bim * dbr, e, ((1,), (1,)))
        inv = 1.0 / den
        d_nr = (dzr * lr - dzi * li) * inv
        d_ni = (dzr * li + dzi * lr) * inv
        d_den = -(dzr * zr + dzi * zi) * inv
        d_lr = (dzr * nr + dzi * ni) * inv + 2.0 * lr * d_den
        d_li = (dzr * ni - dzi * nr) * inv + 2.0 * li * d_den
        t_ar = dar_ref[...] + d_nr
        t_ai = dai_ref[...] + d_ni
        d_lrdt = t_ar * ar + t_ai * ai
        d_th = t_ai * ar - t_ar * ai
        dlr_ref[...] = d_lr + d_lrdt * dt
        dli_ref[...] = d_li + d_th * dt
        dldt_ref[...] = jnp.sum(d_lrdt * lr + d_th * li, axis=1, keepdims=True) * dt

    return pl.pallas_call(
        body, name="s5_params_bwd",
        out_shape=[jax.ShapeDtypeStruct((g, p), F32)] * 2 + [jax.ShapeDtypeStruct((g, 1), F32)]
        + [jax.ShapeDtypeStruct((g, ph), F32)] * 2,
    )(lam_re, lam_im, log_dt, b_re, b_im, d_ar, d_ai, d_bbr, d_bbi)


def _powers(ar, ai, count):
    out = [(ar, ai)]
    for _ in range(count - 1):
        out.append(_cmul(out[-1][0], out[-1][1], ar, ai))
    return out


def _scan_coefs(ar, ai, reverse):
    w = ar.shape[-1]
    pw = _powers(ar, ai, SUBLANES)
    row = lax.broadcasted_iota(jnp.int32, (SUBLANES, w), 0)
    steps = []
    d = 1
    while d < SUBLANES:
        keep = (row < SUBLANES - d) if reverse else (row >= d)
        pr, pi = pw[d - 1]
        steps.append((d, jnp.where(keep, pr, 0.0), jnp.where(keep, pi, 0.0)))
        d *= 2
    cr = jnp.zeros((SUBLANES, w), F32)
    ci = jnp.zeros((SUBLANES, w), F32)
    for t in range(SUBLANES):
        pr, pi = pw[SUBLANES - 1 - t] if reverse else pw[t]
        cr = jnp.where(row == t, pr, cr)
        ci = jnp.where(row == t, pi, ci)
    return steps, cr, ci


def _scan_tile(xr, xi, carry_r, carry_i, coefs, reverse):
    steps, cr, ci = coefs
    for d, mr, mi in steps:
        shift = SUBLANES - d if reverse else d
        sr, si = pltpu.roll(xr, shift, 0), pltpu.roll(xi, shift, 0)
        pr, pi = _cmul(mr, mi, sr, si)
        xr, xi = xr + pr, xi + pi
    pr, pi = _cmul(cr, ci, carry_r, carry_i)
    return xr + pr, xi + pi


def _gelu(x):
    c = math.sqrt(2.0 / math.pi)
    return 0.5 * x * (1.0 + jnp.tanh(c * (x + 0.044715 * x * x * x)))


def _gelu_grad(x):
    c = math.sqrt(2.0 / math.pi)
    t = jnp.tanh(c * (x + 0.044715 * x * x * x))
    return 0.5 * (1.0 + t) + 0.5 * x * (1.0 - t * t) * c * (1.0 + 3.0 * 0.044715 * x * x)


def _s5_fwd(proj, wb, wc, d_skip, abar):
    rows = proj.shape[0]
    nb = wb.shape[0]
    s2 = 2 * STATE_PER_BATCH
    st = STATE_PER_BATCH
    chunk = _tile(rows, 512, SUBLANES)

    def body(u_ref, wb_ref, wc_ref, d_ref, a_ref, s_ref, y_ref, yg_ref):
        for c0 in range(0, rows, chunk):
            s_ref[pl.ds(c0, chunk), :] = _dot_nn(u_ref[pl.ds(c0, chunk), :].astype(BF16), wb_ref[...])
        av = a_ref[...]
        coefs = _scan_coefs(av[:, :st], av[:, st:], reverse=False)

        def tile(b, carry):
            r0 = pl.multiple_of(b * SUBLANES, SUBLANES)
            xr, xi = _scan_tile(s_ref[pl.ds(r0, SUBLANES), :st], s_ref[pl.ds(r0, SUBLANES), st:], carry[0], carry[1],
                                coefs, False)
            s_ref[pl.ds(r0, SUBLANES), :st] = xr
            s_ref[pl.ds(r0, SUBLANES), st:] = xi
            return xr[SUBLANES - 1:, :], xi[SUBLANES - 1:, :]

        zero = jnp.zeros((1, st), F32)
        lax.fori_loop(0, rows // SUBLANES, tile, (zero, zero))
        for c0 in range(0, rows, chunk):
            y = _dot_nn(s_ref[pl.ds(c0, chunk), :].astype(BF16), wc_ref[...]) + d_ref[...] * u_ref[pl.ds(c0, chunk), :]
            y_ref[pl.ds(c0, chunk), :] = y
            yg_ref[pl.ds(c0, chunk), :] = _gelu(y).astype(BF16)

    return pl.pallas_call(
        body, name="s5_fwd", grid=(nb,),
        in_specs=[pl.BlockSpec((rows, LANES), lambda j: (0, j)), pl.BlockSpec((None, LANES, s2), lambda j: (j, 0, 0)),
                  pl.BlockSpec((None, s2, LANES), lambda j: (j, 0, 0)), pl.BlockSpec((1, LANES), lambda j: (0, j)),
                  pl.BlockSpec((None, 1, s2), lambda j: (j, 0, 0))],
        out_specs=[pl.BlockSpec((rows, s2), lambda j: (0, j)), pl.BlockSpec((rows, LANES), lambda j: (0, j)),
                   pl.BlockSpec((rows, LANES), lambda j: (0, j))],
        out_shape=[jax.ShapeDtypeStruct((rows, nb * s2), F32), jax.ShapeDtypeStruct((rows, nb * LANES), F32),
                   jax.ShapeDtypeStruct((rows, nb * LANES), BF16)],
        compiler_params=_params(("parallel",)),
    )(proj, wb, wc, d_skip, abar)


def _s5_bwd(proj, states, y_pre, dyg_a, dyg_b, wb, wc, d_skip, abar):
    rows = proj.shape[0]
    nb = wb.shape[0]
    s2 = 2 * STATE_PER_BATCH
    st = STATE_PER_BATCH
    chunk = _tile(rows, 512, SUBLANES)
    n_tiles = rows // SUBLANES

    def body(u_ref, s_ref, y_ref, ga_ref, gb_ref, wb_ref, wc_ref, d_ref, a_ref,
             du_ref, dwb_ref, dwc_ref, da_ref, dd_ref, ds_ref, dy_ref):
        dy_ref[...] = (ga_ref[...] + gb_ref[...]) * _gelu_grad(y_ref[...])
        dd_ref[...] = jnp.sum(dy_ref[...] * u_ref[...], axis=0, keepdims=True)
        for c0 in range(0, rows, chunk):
            ds_ref[pl.ds(c0, chunk), :] = _dot_nt(dy_ref[pl.ds(c0, chunk), :].astype(BF16), wc_ref[...])
        dwc_ref[...] = _dot_tn(s_ref[...].astype(BF16), dy_ref[...].astype(BF16))
        av = a_ref[...]
        coefs = _scan_coefs(av[:, :st], -av[:, st:], reverse=True)
        row = lax.broadcasted_iota(jnp.int32, (SUBLANES, st), 0)

        def tile(k, carry):
            cr, ci, acc_r, acc_i = carry
            b = n_tiles - 1 - k
            r0 = pl.multiple_of(b * SUBLANES, SUBLANES)
            rp = pl.multiple_of(jnp.maximum(b - 1, 0) * SUBLANES, SUBLANES)
            xr, xi = _scan_tile(ds_ref[pl.ds(r0, SUBLANES), :st], ds_ref[pl.ds(r0, SUBLANES), st:], cr, ci, coefs, True)
            ds_ref[pl.ds(r0, SUBLANES), :st] = xr
            ds_ref[pl.ds(r0, SUBLANES), st:] = xi
            first = jnp.where(b > 0, 1.0, 0.0)
            pr = jnp.where(row == 0, pltpu.roll(s_ref[pl.ds(rp, SUBLANES), :st], 1, 0) * first,
                           pltpu.roll(s_ref[pl.ds(r0, SUBLANES), :st], 1, 0))
            pi = jnp.where(row == 0, pltpu.roll(s_ref[pl.ds(rp, SUBLANES), st:], 1, 0) * first,
                           pltpu.roll(s_ref[pl.ds(r0, SUBLANES), st:], 1, 0))
            acc_r = acc_r + pr * xr + pi * xi
            acc_i = acc_i + pr * xi - pi * xr
            return xr[:1, :], xi[:1, :], acc_r, acc_i

        zero = jnp.zeros((1, st), F32)
        zacc = jnp.zeros((SUBLANES, st), F32)
        _, _, acc_r, acc_i = lax.fori_loop(0, n_tiles, tile, (zero, zero, zacc, zacc))
        da_ref[:, :st] = jnp.sum(acc_r, axis=0, keepdims=True)
        da_ref[:, st:] = jnp.sum(acc_i, axis=0, keepdims=True)
        for c0 in range(0, rows, chunk):
            du_ref[pl.ds(c0, chunk), :] = (_dot_nt(ds_ref[pl.ds(c0, chunk), :].astype(BF16), wb_ref[...])
                                           + d_ref[...] * dy_ref[pl.ds(c0, chunk), :]).astype(du_ref.dtype)
        dwb_ref[...] = _dot_tn(u_ref[...].astype(BF16), ds_ref[...].astype(BF16))

    col = pl.BlockSpec((rows, LANES), lambda j: (0, j))
    return pl.pallas_call(
        body, name="s5_bwd", grid=(nb,),
        in_specs=[col, pl.BlockSpec((rows, s2), lambda j: (0, j)), col, col, col,
                  pl.BlockSpec((None, LANES, s2), lambda j: (j, 0, 0)), pl.BlockSpec((None, s2, LANES), lambda j: (j, 0, 0)),
                  pl.BlockSpec((1, LANES), lambda j: (0, j)), pl.BlockSpec((None, 1, s2), lambda j: (j, 0, 0))],
        out_specs=[col, pl.BlockSpec((None, LANES, s2), lambda j: (j, 0, 0)),
                   pl.BlockSpec((None, s2, LANES), lambda j: (j, 0, 0)), pl.BlockSpec((None, 1, s2), lambda j: (j, 0, 0)),
                   pl.BlockSpec((1, LANES), lambda j: (0, j))],
        out_shape=[jax.ShapeDtypeStruct((rows, nb * LANES), BF16), jax.ShapeDtypeStruct((nb, LANES, s2), F32),
                   jax.ShapeDtypeStruct((nb, s2, LANES), F32), jax.ShapeDtypeStruct((nb, 1, s2), F32),
                   jax.ShapeDtypeStruct((1, nb * LANES), F32)],
        scratch_shapes=[pltpu.VMEM((rows, s2), F32), pltpu.VMEM((rows, LANES), F32)],
        compiler_params=_params(("parallel",)),
    )(proj, states, y_pre, dyg_a, dyg_b, wb, wc, d_skip, abar)


def _glu_norm_fwd(y_pre, z, w, *, tr=256):
    rows, width = y_pre.shape
    tr = _tile(rows, tr, SUBLANES)

    def body(y_ref, z_ref, w_ref, o_ref):
        v = _gelu(y_ref[...]) * jax.nn.sigmoid(z_ref[...])
        o_ref[...] = (v * _rms_rows(v) * w_ref[...]).astype(o_ref.dtype)

    blk = pl.BlockSpec((tr, width), lambda i: (i, 0))
    return pl.pallas_call(
        body, name="glu_norm_fwd", grid=(rows // tr,),
        in_specs=[blk, blk, pl.BlockSpec((1, width), lambda i: (0, 0))], out_specs=blk,
        out_shape=jax.ShapeDtypeStruct((rows, width), BF16), compiler_params=_params(("parallel",)),
    )(y_pre, z, w)


def _glu_norm_bwd(y_pre, z, w, dycat, *, tr=256):
    rows, width = y_pre.shape
    tr = _tile(rows, tr, SUBLANES)

    def body(y_ref, z_ref, w_ref, dy_ref, dz_ref, dg_ref, dw_ref, db_ref):
        yg = _gelu(y_ref[...])
        sg = jax.nn.sigmoid(z_ref[...])
        dv, dwp = _rmsnorm_bwd_rows(yg * sg, w_ref[...], dy_ref[...])
        dz = dv * yg * sg * (1.0 - sg)
        dz_ref[...] = dz.astype(dz_ref.dtype)
        dg_ref[...] = dv * sg
        dw_part = jnp.sum(dwp, axis=0, keepdims=True)
        db_part = jnp.sum(dz, axis=0, keepdims=True)

        @pl.when(pl.program_id(0) == 0)
        def _():
            dw_ref[...] = dw_part
            db_ref[...] = db_part

        @pl.when(pl.program_id(0) > 0)
        def _():
            dw_ref[...] += dw_part
            db_ref[...] += db_part

    blk = pl.BlockSpec((tr, width), lambda i: (i, 0))
    vec = pl.BlockSpec((1, width), lambda i: (0, 0))
    return pl.pallas_call(
        body, name="glu_norm_bwd", grid=(rows // tr,), in_specs=[blk, blk, vec, blk], out_specs=[blk, blk, vec, vec],
        out_shape=[jax.ShapeDtypeStruct((rows, width), BF16), jax.ShapeDtypeStruct((rows, width), F32)]
        + [jax.ShapeDtypeStruct((1, width), F32)] * 2,
        compiler_params=_params(("arbitrary",)),
    )(y_pre, z, w, dycat)


def _rope_tables(pos, freq, sign):
    rows = pos.shape[0]

    def body(p_ref, f_ref, s_ref, cos_ref, sin_ref):
        ang = p_ref[...] * f_ref[...]
        cos_ref[...] = jnp.cos(ang)
        sin_ref[...] = jnp.sin(ang) * s_ref[...]

    return pl.pallas_call(body, name="rope_tables", out_shape=[jax.ShapeDtypeStruct((rows, LANES), F32)] * 2)(pos, freq, sign)


def _rope(x, cos, sin_signed):
    lane = lax.broadcasted_iota(jnp.int32, x.shape, 1)
    half = QK_ROPE_DIM // 2
    swapped = jnp.where(lane < half, pltpu.roll(x, LANES - half, 1), pltpu.roll(x, half, 1))
    return x * cos + swapped * sin_signed


def _attn_prep(q, kv, proj, kpe_col, cos, sin, *, tr=256):
    rows = q.shape[0]
    heads = q.shape[1] // HEAD_SLOT
    tr = _tile(rows, tr, SUBLANES)

    def body(q_ref, kv_ref, kpe_ref, cos_ref, sin_ref, qc_ref, kc_ref, v_ref):
        c, s = cos_ref[...], sin_ref[...]
        qc_ref[:, :LANES] = q_ref[:, :LANES].astype(BF16)
        qc_ref[:, LANES:] = _rope(q_ref[:, LANES:], c, s).astype(BF16)
        kc_ref[:, :LANES] = kv_ref[:, :LANES].astype(BF16)
        kc_ref[:, LANES:] = _rope(kpe_ref[...], c, s).astype(BF16)
        v_ref[...] = kv_ref[:, LANES:].astype(BF16)

    slot = pl.BlockSpec((tr, HEAD_SLOT), lambda i, h: (i, h))
    tab = pl.BlockSpec((tr, LANES), lambda i, h: (i, 0))
    return pl.pallas_call(
        body, name="attn_prep", grid=(rows // tr, heads),
        in_specs=[slot, slot, pl.BlockSpec((tr, LANES), lambda i, h: (i, kpe_col)), tab, tab],
        out_specs=[slot, slot, pl.BlockSpec((tr, LANES), lambda i, h: (i, h))],
        out_shape=[jax.ShapeDtypeStruct((rows, heads * HEAD_SLOT), BF16)] * 2
        + [jax.ShapeDtypeStruct((rows, heads * LANES), BF16)],
        compiler_params=_params(("parallel", "parallel")),
    )(q, kv, proj, cos, sin)


def _causal(tq, tk):
    return lax.broadcasted_iota(jnp.int32, (tq, tk), 1) <= lax.broadcasted_iota(jnp.int32, (tq, tk), 0)


def _attn_fwd(qc, kc, vb, *, scale, tq=512):
    rows = qc.shape[0]
    heads = qc.shape[1] // HEAD_SLOT
    tq = _tile(rows, tq, SUBLANES)
    tk = tq

    def body(q_ref, k_ref, v_ref, o_ref, lse_ref):
        i = pl.program_id(1)
        q = q_ref[...]

        def step(j, carry, diagonal):
            m, l, acc = carry
            k0 = pl.multiple_of(j * tk, tk)
            s = _dot_nt(q, k_ref[pl.ds(k0, tk), :]) * scale
            if diagonal:
                s = jnp.where(_causal(tq, tk), s, NEG_INF)
            m_new = jnp.maximum(m, jnp.max(s, axis=-1, keepdims=True))
            p = jnp.exp(s - m_new)
            alpha = jnp.exp(m - m_new)
            l = alpha * l + jnp.sum(p, axis=-1, keepdims=True)
            acc = alpha * acc + _dot_nn(p.astype(BF16), v_ref[pl.ds(k0, tk), :])
            return m_new, l, acc

        init = (jnp.full((tq, 1), NEG_INF, F32), jnp.zeros((tq, 1), F32), jnp.zeros((tq, LANES), F32))
        below = lax.fori_loop(0, i, lambda j, carry: step(j, carry, False), init)
        m, l, acc = step(i, below, True)
        o_ref[...] = acc / l
        lse_ref[...] = jnp.broadcast_to(m + jnp.log(l), (tq, LANES))

    return pl.pallas_call(
        body, name="attn_fwd", grid=(heads, rows // tq),
        in_specs=[pl.BlockSpec((tq, HEAD_SLOT), lambda h, i: (i, h)), pl.BlockSpec((rows, HEAD_SLOT), lambda h, i: (0, h)),
                  pl.BlockSpec((rows, LANES), lambda h, i: (0, h))],
        out_specs=[pl.BlockSpec((tq, LANES), lambda h, i: (i, h))] * 2,
        out_shape=[jax.ShapeDtypeStruct((rows, heads * LANES), F32)] * 2,
        compiler_params=_params(("parallel", "parallel")),
    )(qc, kc, vb)


def _attn_bwd(qc, kc, vb, o, do, lse, cos, sin, *, scale, tk=512):
    rows = qc.shape[0]
    heads = qc.shape[1] // HEAD_SLOT
    tk = _tile(rows, tk, SUBLANES)
    tq = tk
    nq = rows // tq

    def body(q_ref, k_ref, v_ref, o_ref, do_ref, lse_ref, cos_ref, sin_ref, dq_ref, dkv_ref, dkpe_ref, dq_acc, delta_ref):
        j = pl.program_id(1)

        @pl.when(j == 0)
        def _():
            dq_acc[...] = jnp.zeros_like(dq_acc)
            for r0 in range(0, rows, tq):
                d = jnp.sum(do_ref[pl.ds(r0, tq), :] * o_ref[pl.ds(r0, tq), :], axis=-1, keepdims=True)
                delta_ref[pl.ds(r0, tq), :] = jnp.broadcast_to(d, (tq, LANES))

        kb, vv = k_ref[...], v_ref[...]

        def step(i, carry, diagonal):
            dk, dv = carry
            q0 = pl.multiple_of(i * tq, tq)
            qb = q_ref[pl.ds(q0, tq), :]
            dob = do_ref[pl.ds(q0, tq), :].astype(BF16)
            s = _dot_nt(qb, kb) * scale
            p = jnp.exp(s - lse_ref[pl.ds(q0, tq), :1])
            if diagonal:
                p = jnp.where(_causal(tq, tk), p, 0.0)
            dv = dv + _dot_tn(p.astype(BF16), dob)
            ds = (p * (_dot_nt(dob, vv) - delta_ref[pl.ds(q0, tq), :1])).astype(BF16)
            dk = dk + _dot_tn(ds, qb)
            dq_acc[pl.ds(q0, tq), :] += _dot_nn(ds, kb)
            return dk, dv

        zero = (jnp.zeros((tk, HEAD_SLOT), F32), jnp.zeros((tk, LANES), F32))
        dk, dv = lax.fori_loop(j + 1, nq, lambda i, carry: step(i, carry, False), step(j, zero, True))
        dkv_ref[:, :LANES] = (dk[:, :LANES] * scale).astype(dkv_ref.dtype)
        dkv_ref[:, LANES:] = dv.astype(dkv_ref.dtype)
        dkpe_ref[...] = dk[:, LANES:] * scale

        @pl.when(j == nq - 1)
        def _():
            for r0 in range(0, rows, tq):
                dq = dq_acc[pl.ds(r0, tq), :] * scale
                dq_ref[pl.ds(r0, tq), :LANES] = dq[:, :LANES].astype(dq_ref.dtype)
                dq_ref[pl.ds(r0, tq), LANES:] = _rope(dq[:, LANES:], cos_ref[pl.ds(r0, tq), :],
                                                      -sin_ref[pl.ds(r0, tq), :]).astype(dq_ref.dtype)

    full_q = pl.BlockSpec((rows, HEAD_SLOT), lambda h, j: (0, h))
    full_v = pl.BlockSpec((rows, LANES), lambda h, j: (0, h))
    tab = pl.BlockSpec((rows, LANES), lambda h, j: (0, 0))
    return pl.pallas_call(
        body, name="attn_bwd", grid=(heads, rows // tk),
        in_specs=[full_q, pl.BlockSpec((tk, HEAD_SLOT), lambda h, j: (j, h)), pl.BlockSpec((tk, LANES), lambda h, j: (j, h)),
                  full_v, full_v, full_v, tab, tab],
        out_specs=[full_q, pl.BlockSpec((tk, HEAD_SLOT), lambda h, j: (j, h)), pl.BlockSpec((tk, LANES), lambda h, j: (j, h))],
        out_shape=[jax.ShapeDtypeStruct((rows, heads * HEAD_SLOT), BF16), jax.ShapeDtypeStruct((rows, heads * HEAD_SLOT), BF16),
                   jax.ShapeDtypeStruct((rows, heads * LANES), F32)],
        scratch_shapes=[pltpu.VMEM((rows, HEAD_SLOT), F32), pltpu.VMEM((rows, LANES), F32)],
        compiler_params=_params(("parallel", "arbitrary")),
    )(qc, kc, vb, o, do, lse, cos, sin)


def _kpe_bwd(dkpe_heads, cos, sin, *, tr=512):
    rows = dkpe_heads.shape[0]
    heads = dkpe_heads.shape[1] // LANES
    tr = _tile(rows, tr, 2 * SUBLANES)

    def body(d_ref, cos_ref, sin_ref, o_ref):
        acc = d_ref[:, :LANES]
        for h in range(1, heads):
            acc = acc + d_ref[:, h * LANES:(h + 1) * LANES]
        o_ref[...] = _rope(acc, cos_ref[...], -sin_ref[...]).astype(o_ref.dtype)

    tab = pl.BlockSpec((tr, LANES), lambda i: (i, 0))
    return pl.pallas_call(
        body, name="kpe_bwd", grid=(rows // tr,),
        in_specs=[pl.BlockSpec((tr, heads * LANES), lambda i: (i, 0)), tab, tab], out_specs=tab,
        out_shape=jax.ShapeDtypeStruct((rows, LANES), BF16), compiler_params=_params(("parallel",)),
    )(dkpe_heads, cos, sin)


CONV_ROWS = 128


def _with_halo(ref, r0, ci, n_chunks, ch, lanes, before, after):
    parts = []
    if before:
        lo = pl.multiple_of(jnp.maximum(r0 - SUBLANES, 0), SUBLANES)
        parts.append(ref[pl.ds(lo, SUBLANES), lanes] * jnp.where(ci > 0, 1.0, 0.0))
    parts.append(ref[pl.ds(r0, ch), lanes])
    if after:
        hi = pl.multiple_of(jnp.minimum(r0 + ch, n_chunks * ch - SUBLANES), SUBLANES)
        parts.append(ref[pl.ds(hi, SUBLANES), lanes] * jnp.where(ci < n_chunks - 1, 1.0, 0.0))
    return jnp.concatenate(parts, axis=0)


def _taps(ext):
    return pltpu.roll(ext, 2, 0)[SUBLANES:], pltpu.roll(ext, 1, 0)[SUBLANES:], ext[SUBLANES:]


def _conv3(taps, w, b):
    return w[0:1, :] * taps[0] + w[1:2, :] * taps[1] + w[2:3, :] * taps[2] + b


def _conv_gate_fwd(a, conv_w, conv_b, *, tc=256):
    rows, f2 = a.shape
    f = f2 // 2
    tc = _tile(f, tc)
    nc = f // tc
    ch = _tile(rows, CONV_ROWS, SUBLANES)
    n_chunks = rows // ch

    def body(ag_ref, av_ref, wg_ref, wv_ref, bg_ref, bv_ref, o_ref):
        for lt in range(tc // LANES):
            lanes = slice(lt * LANES, (lt + 1) * LANES)
            wg, wv, bg, bv = wg_ref[:, lanes], wv_ref[:, lanes], bg_ref[:, lanes], bv_ref[:, lanes]

            def chunk(ci, carry):
                r0 = pl.multiple_of(ci * ch, ch)
                gate = _conv3(_taps(_with_halo(ag_ref, r0, ci, n_chunks, ch, lanes, True, False)), wg, bg)
                val = _conv3(_taps(_with_halo(av_ref, r0, ci, n_chunks, ch, lanes, True, False)), wv, bv)
                o_ref[pl.ds(r0, ch), lanes] = (gate * jax.nn.sigmoid(gate) * val).astype(o_ref.dtype)
                return carry

            lax.fori_loop(0, n_chunks, chunk, 0)

    return pl.pallas_call(
        body, name="conv_gate_fwd", grid=(nc,),
        in_specs=[pl.BlockSpec((rows, tc), lambda j: (0, j)), pl.BlockSpec((rows, tc), lambda j: (0, j + nc)),
                  pl.BlockSpec((SUBLANES, tc), lambda j: (0, j)), pl.BlockSpec((SUBLANES, tc), lambda j: (0, j + nc)),
                  pl.BlockSpec((1, tc), lambda j: (0, j)), pl.BlockSpec((1, tc), lambda j: (0, j + nc))],
        out_specs=pl.BlockSpec((rows, tc), lambda j: (0, j)),
        out_shape=jax.ShapeDtypeStruct((rows, f), BF16), compiler_params=_params(("parallel",)),
    )(a, a, conv_w, conv_w, conv_b, conv_b)


def _conv_gate_bwd(a, conv_w, conv_b, dg, *, tc=256):
    rows, f2 = a.shape
    f = f2 // 2
    tc = _tile(f, tc)
    nc = f // tc
    ch = _tile(rows, CONV_ROWS, SUBLANES)
    n_chunks = rows // ch
    ext_rows = ch + SUBLANES

    def fold(x):
        return jnp.sum(x.reshape(ch // SUBLANES, SUBLANES, LANES), axis=0)

    def body(ag_ref, av_ref, wg_ref, wv_ref, bg_ref, bv_ref, dg_ref, da_ref, dw_ref, db_ref):
        for lt in range(tc // LANES):
            lanes = slice(lt * LANES, (lt + 1) * LANES)
            wg, wv, bg, bv = wg_ref[:, lanes], wv_ref[:, lanes], bg_ref[:, lanes], bv_ref[:, lanes]

            def chunk(ci, acc):
                r0 = pl.multiple_of(ci * ch, ch)
                taps_g = _taps(_with_halo(ag_ref, r0, ci, n_chunks, ch, lanes, True, True))
                taps_v = _taps(_with_halo(av_ref, r0, ci, n_chunks, ch, lanes, True, True))
                dge = _with_halo(dg_ref, r0, ci, n_chunks, ch, lanes, False, True)
                gate, val = _conv3(taps_g, wg, bg), _conv3(taps_v, wv, bv)
                sg = jax.nn.sigmoid(gate)
                d_gate = dge * val * sg * (1.0 + gate * (1.0 - sg))
                d_val = dge * gate * sg
                new = []
                for half, (taps, w, d) in enumerate(((taps_g, wg, d_gate), (taps_v, wv, d_val))):
                    da = (w[2:3, :] * d[:ch] + w[1:2, :] * pltpu.roll(d, ext_rows - 1, 0)[:ch]
                          + w[0:1, :] * pltpu.roll(d, ext_rows - 2, 0)[:ch])
                    da_ref[half, pl.ds(r0, ch), lanes] = da.astype(da_ref.dtype)
                    dc = d[:ch]
                    sums = [fold(dc)] + [fold(dc * t[:ch]) for t in taps]
                    new.append(tuple(x + s for x, s in zip(acc[half], sums)))
                return tuple(new)

            zero = tuple(jnp.zeros((SUBLANES, LANES), F32) for _ in range(4))
            acc = lax.fori_loop(0, n_chunks, chunk, (zero, zero))
            row = lax.broadcasted_iota(jnp.int32, (SUBLANES, LANES), 0)
            for half in range(2):
                db, *taps = (jnp.sum(x, axis=0, keepdims=True) for x in acc[half])
                db_ref[half, :, lanes] = db
                dw = jnp.zeros((SUBLANES, LANES), F32)
                for tap in range(3):
                    dw = jnp.where(row == tap, taps[tap], dw)
                dw_ref[half, :, lanes] = dw

    lo = lambda j: (0, j)
    hi = lambda j: (0, j + nc)
    both = lambda j: (0, 0, j)
    return pl.pallas_call(
        body, name="conv_gate_bwd", grid=(nc,),
        in_specs=[pl.BlockSpec((rows, tc), lo), pl.BlockSpec((rows, tc), hi), pl.BlockSpec((SUBLANES, tc), lo),
                  pl.BlockSpec((SUBLANES, tc), hi), pl.BlockSpec((1, tc), lo), pl.BlockSpec((1, tc), hi),
                  pl.BlockSpec((rows, tc), lo)],
        out_specs=[pl.BlockSpec((2, rows, tc), both), pl.BlockSpec((2, SUBLANES, tc), both), pl.BlockSpec((2, 1, tc), both)],
        out_shape=[jax.ShapeDtypeStruct((2, rows, f), BF16), jax.ShapeDtypeStruct((2, SUBLANES, f), F32),
                   jax.ShapeDtypeStruct((2, 1, f), F32)],
        compiler_params=_params(("parallel",)),
    )(a, a, conv_w, conv_w, conv_b, conv_b, dg)


def _wgrad(a, b, rows, cols, row_sharded, name, **kw):
    return functools.partial(_wgrad_half, a, b, rows, cols, row_sharded, name, **kw)


def _block_diag(x):
    nb, g, r, c = x.shape
    eye = jnp.eye(g, dtype=x.dtype)
    return (x[:, :, :, None, :] * eye[None, :, None, :, None]).reshape(nb, g * r, g * c)


def _block_diag_part(x, r, c):
    nb = x.shape[0]
    g = GROUPS_PER_BATCH
    eye = jnp.eye(g, dtype=x.dtype)
    return jnp.sum(x.reshape(nb, g, r, g, c) * eye[None, :, None, :, None], axis=3)


class _NoExchange:
    def __init__(self, ffn):
        self.ffn = ffn

    def ffn_weights_arrived(self, after):
        return None

    def ffn_weights(self, after):
        return self.ffn

    def ffn_grads(self, makers, after):
        self.ffn_makers = makers
        return None

    def ffn_backward_done(self, after):
        return None


def _local_step(x, posf, target, w, hooks):
    rows, d = x.shape
    width = w["ssm_d"].shape[1]
    qr, kvr = w["mla_q_norm_w"].shape[1], w["mla_kv_norm_w"].shape[1]
    heads = w["mla_w_ukv"].shape[1] // HEAD_SLOT
    f2 = w["ffn_conv_b"].shape[1]
    inp = w["w_in"].shape[0]
    groups = width // SSM_GROUP
    nb = groups // GROUPS_PER_BATCH
    scale = (QK_NOPE_DIM + QK_ROPE_DIM) ** -0.5
    g = {}

    hn = _rmsnorm_fwd(x, w["attn_norm_w"], name="attn_norm")
    proj = _matmul(hn, w["w_in"], mode="nt", name="in_proj")

    ar, ai, bbr, bbi = _s5_params(w["ssm_lambda_re"], w["ssm_lambda_im"], w["ssm_log_dt"], w["ssm_b_re"], w["ssm_b_im"])

    def b_band(bb):
        return _block_diag(bb.reshape(nb, GROUPS_PER_BATCH, SSM_STATE, SSM_GROUP).transpose(0, 1, 3, 2))

    def c_band(c):
        return _block_diag(c.reshape(nb, GROUPS_PER_BATCH, SSM_GROUP, SSM_STATE).transpose(0, 1, 3, 2))

    wb = jnp.concatenate([b_band(bbr), b_band(bbi)], axis=2).astype(BF16)
    wc = jnp.concatenate([c_band(w["ssm_c_re"]), -c_band(w["ssm_c_im"])], axis=1).astype(BF16)
    abar = jnp.concatenate([ar.reshape(nb, 1, STATE_PER_BATCH), ai.reshape(nb, 1, STATE_PER_BATCH)], axis=2)
    states, y_pre, yg = _s5_fwd(proj, wb, wc, w["ssm_d"], abar)
    z = _matmul(yg, w["ssm_w_glu"], mode="nn", name="glu_proj", bias=w["ssm_b_glu"])
    ys = _glu_norm_fwd(y_pre, z, w["ssm_out_norm_w"])

    q_col, kv_col, kpe_col = width // qr, (width + qr) // kvr, (width + qr + kvr) // LANES
    assert width % qr == 0 and (width + qr) % kvr == 0
    qn = _rmsnorm_fwd(proj, w["mla_q_norm_w"], name="q_norm", width=qr, col=q_col)
    kvn = _rmsnorm_fwd(proj, w["mla_kv_norm_w"], name="kv_norm", width=kvr, col=kv_col)
    q = _matmul(qn, w["mla_w_uq"], mode="nn", name="q_proj")
    kv = _matmul(kvn, w["mla_w_ukv"], mode="nn", name="kv_proj")
    half = QK_ROPE_DIM // 2
    inv_freq = ROPE_THETA ** (-jnp.arange(0, QK_ROPE_DIM, 2, dtype=F32) / QK_ROPE_DIM)
    zeros = jnp.zeros((LANES - QK_ROPE_DIM,), F32)
    freq = jnp.concatenate([inv_freq, inv_freq, zeros]).reshape(1, LANES)
    sign = jnp.concatenate([-jnp.ones((half,), F32), jnp.ones((half,), F32), zeros]).reshape(1, LANES)
    cos, sin = _rope_tables(posf, freq, sign)
    qc, kc, vb = _attn_prep(q, kv, proj, kpe_col, cos, sin)
    o, lse = _attn_fwd(qc, kc, vb, scale=scale, tq=ATTN_BLOCK)
    ym = _rmsnorm_fwd(o, w["mla_out_norm_w"], name="mla_out_norm")
    ycat = jnp.concatenate([ys, ym], axis=1)
    h1 = _matmul(ycat, w["w_out"], mode="nn", name="out_proj", add=x, after=hooks.ffn_weights_arrived(ycat))

    hn2 = _rmsnorm_fwd(h1, w["ffn_norm_w"], name="ffn_norm")
    ffn = hooks.ffn_weights(hn2)
    a = _matmul(hn2, ffn["ffn_w_up"], mode="nn", name="ffn_up", tm=FFN_ROWS)
    gated = _conv_gate_fwd(a, ffn["ffn_conv_w"], w["ffn_conv_b"])
    h2 = _matmul(gated, ffn["ffn_w_down"], mode="nn", name="ffn_down", add=h1, tk=2816, tm=FFN_ROWS)
    loss_tile, dh2, dh2_mxu, g["final_norm_w"] = _final_norm_loss(h2, w["final_norm_w"], target)

    dgated = _matmul(dh2_mxu, ffn["ffn_w_down"], mode="nt", name="ffn_down_dx", tm=FFN_ROWS)
    da, dcw, dcb = _conv_gate_bwd(a, ffn["ffn_conv_w"], w["ffn_conv_b"], dgated)
    g["ffn_conv_w"] = jnp.concatenate([dcw[0, :3], dcw[1, :3]], axis=1)
    g["ffn_conv_b"] = jnp.concatenate([dcb[0], dcb[1]], axis=1)
    started = hooks.ffn_grads({
        "ffn_w_up": _wgrad(hn2, da, d, f2, False, "ffn_up_dw", b_split=True, tn=_tile(f2 // N_CHIPS, 1408)),
        "ffn_w_down": _wgrad(gated, dh2_mxu, f2 // 2, d, True, "ffn_down_dw", tm=f2 // 2 // N_CHIPS, tn=512)}, dcb)
    dhn2 = _matmul(da, ffn["ffn_w_up"], mode="nt", name="ffn_up_dx", a_split=True, tk=_tile(f2 // 2, 2816), tm=FFN_ROWS,
                   after=started)
    dh1, dh1_mxu, g["ffn_norm_w"] = _rmsnorm_bwd(h1, w["ffn_norm_w"], dhn2, name="ffn_norm_bwd", add=dh2,
                                                dx_dtypes=(F32, BF16))

    dycat = _matmul(dh1_mxu, w["w_out"], mode="nt", name="out_proj_dx")
    g["w_out"] = _wgrad(ycat, dh1_mxu, 2 * width, d, True, "out_proj_dw")
    started = hooks.ffn_backward_done(dycat)
    mla_out_norm_w, ssm_out_norm_w = w["mla_out_norm_w"], w["ssm_out_norm_w"]
    if started is not None:
        mla_out_norm_w, ssm_out_norm_w = mla_out_norm_w + started[:1, :1], ssm_out_norm_w + started[:1, :1]

    do, g["mla_out_norm_w"] = _rmsnorm_bwd(o, mla_out_norm_w, dycat, name="mla_out_norm_bwd", width=width, dy_col=1)
    dq, dkv, dkpe_heads = _attn_bwd(qc, kc, vb, o, do, lse, cos, sin, scale=scale, tk=ATTN_BLOCK)
    dkpe = _kpe_bwd(dkpe_heads, cos, sin)
    g["mla_w_uq"] = _wgrad(qn, dq, qr, heads * HEAD_SLOT, False, "q_proj_dw")
    dqn = _matmul(dq, w["mla_w_uq"], mode="nt", name="q_proj_dx")
    dcq, g["mla_q_norm_w"] = _rmsnorm_bwd(proj, w["mla_q_norm_w"], dqn, name="q_norm_bwd", width=qr, col=q_col,
                                          dx_dtypes=(BF16,))
    g["mla_w_ukv"] = _wgrad(kvn, dkv, kvr, heads * HEAD_SLOT, False, "kv_proj_dw")
    dkvn = _matmul(dkv, w["mla_w_ukv"], mode="nt", name="kv_proj_dx")
    dckv, g["mla_kv_norm_w"] = _rmsnorm_bwd(proj, w["mla_kv_norm_w"], dkvn, name="kv_norm_bwd", width=kvr, col=kv_col,
                                            dx_dtypes=(BF16,))

    dz, dyg_a, g["ssm_out_norm_w"], g["ssm_b_glu"] = _glu_norm_bwd(y_pre, z, ssm_out_norm_w, dycat)
    dyg_b = _matmul(dz, w["ssm_w_glu"], mode="nt", name="glu_proj_dx")
    g["ssm_w_glu"] = _wgrad(yg, dz, width, width, True, "glu_proj_dw")
    du, dwb, dwc, dabar, g["ssm_d"] = _s5_bwd(proj, states, y_pre, dyg_a, dyg_b, wb, wc, w["ssm_d"], abar)

    def b_unband(x):
        return _block_diag_part(x, SSM_GROUP, SSM_STATE).transpose(0, 1, 3, 2).reshape(groups, SSM_STATE * SSM_GROUP)

    def c_unband(x):
        return _block_diag_part(x, SSM_STATE, SSM_GROUP).transpose(0, 1, 3, 2).reshape(groups, SSM_GROUP, SSM_STATE)

    st = STATE_PER_BATCH
    g["ssm_c_re"] = c_unband(dwc[:, :st, :])
    g["ssm_c_im"] = -c_unband(dwc[:, st:, :])
    d_ar = dabar[:, 0, :st].reshape(groups, SSM_STATE)
    d_ai = dabar[:, 0, st:].reshape(groups, SSM_STATE)
    (g["ssm_lambda_re"], g["ssm_lambda_im"], g["ssm_log_dt"], g["ssm_b_re"], g["ssm_b_im"]) = _s5_params_bwd(
        w["ssm_lambda_re"], w["ssm_lambda_im"], w["ssm_log_dt"], w["ssm_b_re"], w["ssm_b_im"], d_ar, d_ai,
        b_unband(dwb[:, :, :st]), b_unband(dwb[:, :, st:]))

    pad = jnp.zeros((rows, inp - (width + qr + kvr + LANES)), BF16)
    dproj = jnp.concatenate([du, dcq, dckv, dkpe, pad], axis=1)
    g["w_in"] = _wgrad(dproj, hn, inp, d, False, "in_proj_dw")
    dhn = _matmul(dproj, w["w_in"], mode="nn", name="in_proj_dx")
    dx, g["attn_norm_w"] = _rmsnorm_bwd(x, w["attn_norm_w"], dhn, name="attn_norm_bwd", add=dh1)
    return loss_tile, dx, g


ANY = pl.BlockSpec(memory_space=pl.ANY)
MESH = pl.DeviceIdType.MESH


def _mesh_pos():
    return lax.axis_index("x"), lax.axis_index("y"), lax.axis_index("c")


def _other_chips(x, y):
    return [(1 - x, y), (x, 1 - y), (1 - x, 1 - y)]


def _remote(src, dst, send_sems, recv_sems, k, to):
    return pltpu.make_async_remote_copy(src_ref=src, dst_ref=dst, send_sem=send_sems.at[k], recv_sem=recv_sems.at[k],
                                        device_id=to, device_id_type=MESH)


def _place_shard(shard, piece_idx, row_sharded, name, out_dtype=BF16, pieces=N_CHIPS):
    rs, cs = shard.shape
    tr = _tile(rs, 256, 2 * SUBLANES)
    rb = rs // tr

    def body(p_ref, x_ref, o_ref):
        o_ref[...] = x_ref[...].astype(o_ref.dtype)

    if row_sharded:
        out_shape, out_map = (pieces * rs, cs), (lambda i, p_ref: (p_ref[0] * rb + i, 0))
    else:
        out_shape, out_map = (rs, pieces * cs), (lambda i, p_ref: (i, p_ref[0]))
    return pl.pallas_call(
        body, name=name, out_shape=jax.ShapeDtypeStruct(out_shape, out_dtype),
        grid_spec=pltpu.PrefetchScalarGridSpec(
            num_scalar_prefetch=1, grid=(rb,), in_specs=[pl.BlockSpec((tr, cs), lambda i, p_ref: (i, 0))],
            out_specs=pl.BlockSpec((tr, cs), out_map)),
        compiler_params=_params(("parallel",)),
    )(piece_idx, shard)


def _gather_weights(placed, name):
    n = len(placed)
    meta = [(row_sharded, direct) for _, row_sharded, direct in placed]
    over_ici, over_d2d = _gather_plans(meta)
    forwarded = [t for t, (_, direct) in enumerate(meta) if not direct]

    def body(*refs):
        outs = refs[n:2 * n]
        send_sems, recv_sems, pass_send_sems, pass_recv_sems = refs[2 * n:]
        first, arrivals = over_ici(outs, send_sems, recv_sems)
        passed, passed_arrivals = over_d2d([outs[t] for t in forwarded], pass_send_sems, pass_recv_sems)
        for cp in first:
            cp.start()
        for t in range(n):
            for j in range(3):
                arrivals[3 * t + j].wait_recv()
                if t in forwarded:
                    passed[3 * forwarded.index(t) + j].start()
        for cp in passed_arrivals:
            cp.wait_recv()
        for cp in first + passed:
            cp.wait_send()

    return pl.pallas_call(
        body, name=name, in_specs=[ANY] * n, out_specs=[ANY] * n,
        out_shape=[jax.ShapeDtypeStruct(arr.shape, arr.dtype) for arr, _, _ in placed],
        input_output_aliases={t: t for t in range(n)},
        scratch_shapes=[pltpu.SemaphoreType.DMA((3 * n,)), pltpu.SemaphoreType.DMA((3 * n,)),
                        pltpu.SemaphoreType.DMA((3 * len(forwarded),)), pltpu.SemaphoreType.DMA((3 * len(forwarded),))],
    )(*[arr for arr, _, _ in placed])


def _gather_plans(meta):
    def window(ref, row_sharded, piece, half):
        r, cc = ref.shape
        if row_sharded:
            rs = r // N_CHIPS
            if half is None:
                return ref.at[pl.ds(piece * rs, rs), :]
            return ref.at[pl.ds(piece * rs + half * (rs // 2), rs // 2), :]
        cs = cc // N_CHIPS
        if half is None:
            return ref.at[:, pl.ds(piece * cs, cs)]
        return ref.at[pl.ds(half * (r // 2), r // 2), pl.ds(piece * cs, cs)]

    def over_ici(refs, send_sems, recv_sems):
        x, y, c = _mesh_pos()
        sends, recvs = [], []
        for t, (row_sharded, direct) in enumerate(meta):
            mine = window(refs[t], row_sharded, 2 * x + y, None if direct else c)
            for j, (px, py) in enumerate(_other_chips(x, y)):
                theirs = window(refs[t], row_sharded, 2 * px + py, None if direct else c)
                sends.append(_remote(mine, mine, send_sems, recv_sems, 3 * t + j, (px, py, c)))
                recvs.append(_remote(theirs, theirs, send_sems, recv_sems, 3 * t + j, (px, py, c)))
        return sends, recvs

    def over_d2d(refs, send_sems, recv_sems):
        x, y, c = _mesh_pos()
        sends, recvs = [], []
        rows = [row_sharded for row_sharded, direct in meta if not direct]
        for t, row_sharded in enumerate(rows):
            for j, (px, py) in enumerate(_other_chips(x, y)):
                got = window(refs[t], row_sharded, 2 * px + py, c)
                other = window(refs[t], row_sharded, 2 * px + py, 1 - c)
                sends.append(_remote(got, got, send_sems, recv_sems, 3 * t + j, (x, y, 1 - c)))
                recvs.append(_remote(other, other, send_sems, recv_sems, 3 * t + j, (x, y, 1 - c)))
        return sends, recvs

    return over_ici, over_d2d


HBM = pl.BlockSpec(memory_space=pltpu.HBM)
SEMAPHORES = pl.BlockSpec(memory_space=pltpu.SEMAPHORE)
DATAFLOW = pltpu.SideEffectType.DATAFLOW_SIDE_EFFECTING


def _start_copies(name, arrays, plan, n_copies, after):
    n = len(arrays)

    def body(*refs):
        sends, _ = plan(refs[:n], refs[n + 1], refs[n + 2])
        for cp in sends:
            cp.start()
        token = refs[2 * n + 3]
        token[...] = jnp.zeros_like(token)

    out = pl.pallas_call(
        body, name=name,
        out_shape=(pltpu.SemaphoreType.DMA((n_copies,)), pltpu.SemaphoreType.DMA((n_copies,)),
                   *[pltpu.HBM(a.shape, a.dtype) for a in arrays], jax.ShapeDtypeStruct((SUBLANES, LANES), F32)),
        in_specs=[HBM] * n + [ANY],
        out_specs=(SEMAPHORES, SEMAPHORES, *[HBM] * n, pl.BlockSpec(memory_space=pltpu.VMEM)),
        input_output_aliases={t: t + 2 for t in range(n)},
        compiler_params=pltpu.CompilerParams(has_side_effects=DATAFLOW),
    )(*[pltpu.with_memory_space_constraint(a, pltpu.HBM) for a in arrays], after)
    return out[0], out[1], list(out[2:2 + n]), out[2 + n]


def _wait_copies(name, started, plan, after):
    send_sems, recv_sems, arrays, _ = started
    n = len(arrays)

    def body(*refs):
        sends, recvs = plan(refs[:n], refs[n], refs[n + 1])
        for cp in sends:
            cp.wait_send()
        for cp in recvs:
            cp.wait_recv()

    out = pl.pallas_call(
        body, name=name, out_shape=[pltpu.HBM(a.shape, a.dtype) for a in arrays],
        in_specs=[HBM] * n + [SEMAPHORES, SEMAPHORES, ANY], out_specs=[HBM] * n,
        input_output_aliases={t: t for t in range(n)},
        compiler_params=pltpu.CompilerParams(has_side_effects=DATAFLOW),
    )(*arrays, send_sems, recv_sems, after)
    return list(out)


def _exchange(name, arrays, out_shapes, plan, n_copies, in_place=False, after=None):
    n = len(arrays)
    extra = [] if after is None else [after]

    def body(*refs):
        ins, outs = refs[:n], refs[n + len(extra):n + len(extra) + len(out_shapes)]
        send_sems, recv_sems = refs[n + len(extra) + len(out_shapes):]
        sends, recvs = plan(ins, outs, send_sems, recv_sems)
        for cp in sends:
            cp.start()
        for cp in recvs:
            cp.wait_recv()
        for cp in sends:
            cp.wait_send()

    return pl.pallas_call(
        body, name=name, in_specs=[ANY] * (n + len(extra)), out_specs=[ANY] * len(out_shapes), out_shape=out_shapes,
        input_output_aliases={t: t for t in range(n)} if in_place else {},
        scratch_shapes=[pltpu.SemaphoreType.DMA((n_copies,)), pltpu.SemaphoreType.DMA((n_copies,))],
    )(*arrays, *extra)


def _give_plan(n):
    def plan(refs, send_sems, recv_sems):
        x, y, c = _mesh_pos()
        sends = [_remote(refs[t], refs[n + t], send_sems, recv_sems, t, (x, y, 1 - c)) for t in range(n)]
        return sends, sends

    return plan


def _scatter_plan(n):
    def plan(refs, send_sems, recv_sems):
        x, y, c = _mesh_pos()
        sends = []
        for t in range(n):
            for j, (px, py) in enumerate(_other_chips(x, y)):
                sends.append(_remote(refs[t].at[2 * px + py], refs[n + t].at[j], send_sems, recv_sems, 3 * t + j, (px, py, c)))
        return sends, sends

    return plan


def _scatter_shapes(sums):
    return [jax.ShapeDtypeStruct((3,) + s.shape[1:], s.dtype) for s in sums]


def _join_halves(halves, name, after=None):
    def plan(ins, outs, send_sems, recv_sems):
        x, y, c = _mesh_pos()
        sends = [_remote(outs[t].at[c], outs[t].at[c], send_sems, recv_sems, t, (x, y, 1 - c)) for t in range(len(ins))]
        recvs = [_remote(outs[t].at[1 - c], outs[t].at[1 - c], send_sems, recv_sems, t, (x, y, 1 - c))
                 for t in range(len(ins))]
        return sends, recvs

    shapes = [jax.ShapeDtypeStruct(h.shape, h.dtype) for h in halves]
    return _exchange(name, halves, shapes, plan, len(halves), in_place=True, after=after)


def _add_other_half(g4, got, where, name, wire_dtype=BF16):
    _, pieces, sr, sc = g4.shape
    tr = _tile(sr, 256, 2 * SUBLANES)

    def body(w_ref, a_ref, b_ref, o_ref):
        o_ref[...] = (a_ref[...] + b_ref[...]).astype(o_ref.dtype)

    blk = pl.BlockSpec((None, tr, sc), lambda p, i, w_ref: (p, i, 0))
    return pl.pallas_call(
        body, name=name, out_shape=jax.ShapeDtypeStruct((pieces, sr, sc), wire_dtype),
        grid_spec=pltpu.PrefetchScalarGridSpec(
            num_scalar_prefetch=1, grid=(pieces, sr // tr),
            in_specs=[pl.BlockSpec((None, None, tr, sc), lambda p, i, w_ref: (w_ref[0], p, i, 0)), blk], out_specs=blk),
        compiler_params=_params(("parallel", "parallel")),
    )(where, g4, got)


def _add_pieces(sums, got_pieces, where, name):
    _, sr, sc = sums.shape
    tr = _tile(sr, 256, 2 * SUBLANES)

    def body(w_ref, a_ref, r_ref, o_ref):
        acc = a_ref[...]
        for j in range(3):
            acc = acc + r_ref[j].astype(F32)
        o_ref[...] = acc

    return pl.pallas_call(
        body, name=name, out_shape=jax.ShapeDtypeStruct((N_CORES, sr, sc), F32),
        grid_spec=pltpu.PrefetchScalarGridSpec(
            num_scalar_prefetch=1, grid=(sr // tr,),
            in_specs=[pl.BlockSpec((None, tr, sc), lambda i, w_ref: (w_ref[1], i, 0)),
                      pl.BlockSpec((3, tr, sc), lambda i, w_ref: (0, i, 0))],
            out_specs=pl.BlockSpec((None, tr, sc), lambda i, w_ref: (w_ref[0], i, 0))),
        compiler_params=_params(("parallel",)),
    )(where, sums, got_pieces)


def _adamw_update(w, g, m, v):
    nm = ADAM_B1 * m + (1.0 - ADAM_B1) * g
    nv = ADAM_B2 * v + (1.0 - ADAM_B2) * (g * g)
    m_hat = nm / (1.0 - ADAM_B1 ** ADAM_STEP)
    v_hat = nv / (1.0 - ADAM_B2 ** ADAM_STEP)
    return -ADAM_LR * (m_hat / (jnp.sqrt(v_hat) + ADAM_EPS) + ADAM_WD * w), nm, nv


def _adamw(w, g, m, v, name, after=None):
    rows, cols = w.shape
    halves = 2 if g.ndim == 3 else 1
    bc = cols // halves
    tr = _tile(rows, max(SUBLANES, (1 << 19) // max(bc, 1) // SUBLANES * SUBLANES), SUBLANES)

    def body(w_ref, g_ref, m_ref, v_ref, *rest):
        d_ref, nm_ref, nv_ref, go_ref = rest[-4:]
        gv = g_ref[...]
        d_ref[...], nm_ref[...], nv_ref[...] = _adamw_update(w_ref[...], gv, m_ref[...], v_ref[...])
        go_ref[...] = gv

    blk = pl.BlockSpec((tr, bc), lambda i, h: (i, h))
    g_blk = pl.BlockSpec((None, tr, bc), lambda i, h: (h, i, 0)) if halves == 2 else blk
    extra = [] if after is None else [after]
    return pl.pallas_call(
        body, name=name, grid=(rows // tr, halves),
        in_specs=[blk, g_blk, blk, blk] + [pl.BlockSpec(memory_space=pl.ANY)] * len(extra), out_specs=[blk] * 4,
        out_shape=[jax.ShapeDtypeStruct((rows, cols), F32)] * 4, compiler_params=_params(("parallel", "parallel")),
    )(w, g, m, v, *extra)


def _adamw_many(ws, gs, ms, vs, name):
    n = len(ws)

    def body(*refs):
        outs = refs[4 * n:]
        for k in range(n):
            w_ref, g_ref, m_ref, v_ref = (refs[j * n + k] for j in range(4))
            outs[k][...], outs[n + k][...], outs[2 * n + k][...] = _adamw_update(w_ref[...], g_ref[...], m_ref[...], v_ref[...])

    out = pl.pallas_call(
        body, name=name, out_shape=[jax.ShapeDtypeStruct(w.shape, F32) for w in ws] * 3,
        compiler_params=pltpu.CompilerParams(vmem_limit_bytes=VMEM_LIMIT_BYTES),
    )(*ws, *gs, *ms, *vs)
    return out[:n], out[n:2 * n], out[2 * n:]


WEIGHTS = ['attn_norm_w', 'w_in', 'ssm_lambda_re', 'ssm_lambda_im', 'ssm_log_dt', 'ssm_b_re', 'ssm_b_im', 'ssm_c_re',
           'ssm_c_im', 'ssm_d', 'ssm_w_glu', 'ssm_b_glu', 'mla_q_norm_w', 'mla_w_uq', 'mla_kv_norm_w', 'mla_w_ukv',
           'ssm_out_norm_w', 'mla_out_norm_w', 'w_out', 'ffn_norm_w', 'ffn_w_up', 'ffn_conv_w', 'ffn_conv_b',
           'ffn_w_down', 'final_norm_w']
SHARDED = {'w_in': False, 'ssm_w_glu': True, 'mla_w_uq': False, 'mla_w_ukv': False, 'w_out': True, 'ffn_w_up': False,
           'ffn_w_down': True}
SMALL = [n for n in WEIGHTS if n not in SHARDED and n != 'ffn_conv_w']
ROPE_PAD = HEAD_SLOT - QK_NOPE_DIM - QK_ROPE_DIM
SMALL_COLS = 8 * LANES


def _pad_heads(w_uq, heads):
    qr = w_uq.shape[0]
    w3 = w_uq.reshape(qr, heads, QK_NOPE_DIM + QK_ROPE_DIM)
    return jnp.concatenate([w3, jnp.zeros((qr, heads, ROPE_PAD), w_uq.dtype)], axis=2).reshape(qr, heads * HEAD_SLOT)


def _unpad_heads(g_uq, heads):
    qr = g_uq.shape[0]
    return g_uq.reshape(qr, heads, HEAD_SLOT)[:, :, :QK_NOPE_DIM + QK_ROPE_DIM].reshape(qr, -1)


FFN = ['ffn_w_up', 'ffn_w_down']
FFN_GATHER = FFN + ['ffn_conv_w']
FFN_GATHER_META = [(SHARDED[n], False) for n in FFN] + [(False, True)]


class _Overlapped:
    def __init__(self, placed, where, after):
        self.where, self.mine, self.other = where, where[:1], 1 - where[:1]
        self.over_ici, self.over_d2d = _gather_plans(FFN_GATHER_META)
        self.gather = _start_copies("gather_ffn_start", placed, self.over_ici, 3 * len(placed), after)
        self.gather_started = self.gather[3]

    def ffn_weights_arrived(self, after):
        arrived = _wait_copies("gather_ffn_wait", self.gather, self.over_ici, after)
        n = len(FFN)
        self.direct = arrived[n:]
        self.passing = _start_copies("gather_ffn_pass_start", arrived[:n], self.over_d2d, 3 * n, after)
        return self.passing[3]

    def ffn_weights(self, after):
        passed = _wait_copies("gather_ffn_pass_wait", self.passing, self.over_d2d, after)
        return dict(zip(FFN_GATHER, passed + self.direct))

    def ffn_grads(self, makers, after):
        self.makers = [makers[name] for name in FFN]
        n = len(FFN)
        give = [make(self.other, suffix="_give") for make in self.makers]
        lands = [lax.empty(g.shape, g.dtype) for g in give]
        self.swap = _start_copies("grad_ffn_swap_start", give + lands, _give_plan(n), n, after)
        return self.swap[3]

    def ffn_backward_done(self, after):
        n = len(FFN)
        got = _wait_copies("grad_ffn_swap_wait", self.swap, _give_plan(n), after)[n:]
        kept = [make(self.mine, suffix="_keep", add=got[t], wire=True) for t, make in enumerate(self.makers)]
        self.sums = [k[0] for k in kept]
        wires = [k[1] for k in kept]
        lands = [lax.empty(s.shape, s.dtype) for s in _scatter_shapes(wires)]
        self.scatter = _start_copies("grad_ffn_scatter_start", wires + lands, _scatter_plan(n), 3 * n, after)
        return self.scatter[3]

    def ffn_reduced(self, after):
        n = len(FFN)
        got_pieces = _wait_copies("grad_ffn_scatter_wait", self.scatter, _scatter_plan(n), after)[n:]
        return [_add_pieces(self.sums[t], got_pieces[t], self.where, "grad_add_pieces_" + name) for t, name in enumerate(FFN)]


def _step(args):
    x, positions, target = args["x"][0], args["positions"], args["loss_target"][0]
    rows = x.shape[0]
    p = {n: args[n] for n in WEIGHTS}
    xi, yi, ci = _mesh_pos()
    piece = 2 * xi + yi

    def transposed(a):
        return jnp.swapaxes(a[0], 0, 1)

    w_in = transposed(p["w_in"])
    in_width = w_in.shape[0]
    in_pad = (-in_width) % (2 * LANES)
    heads_here = p["mla_w_uq"].shape[2] // (QK_NOPE_DIM + QK_ROPE_DIM)
    shards = {
        "w_in": jnp.pad(w_in, ((0, in_pad), (0, 0))),
        "ssm_w_glu": p["ssm_w_glu"][0],
        "mla_w_uq": _pad_heads(p["mla_w_uq"][0], heads_here),
        "mla_w_ukv": p["mla_w_ukv"][0],
        "w_out": p["w_out"][0],
        "ffn_w_up": p["ffn_w_up"][0],
        "ffn_w_down": p["ffn_w_down"][0],
    }
    conv_w = jnp.pad(p["ffn_conv_w"][0], ((0, SUBLANES - p["ffn_conv_w"].shape[1]), (0, 0)))
    order = list(SHARDED)
    piece_idx = piece.reshape(1).astype(jnp.int32)
    placed = {n: _place_shard(shards[n], piece_idx, SHARDED[n], "place_" + n) for n in order}
    placed["ffn_conv_w"] = _place_shard(conv_w, piece_idx, False, "place_ffn_conv_w", out_dtype=F32)
    mixer = [n for n in order if n not in FFN]
    w = dict(zip(mixer, _gather_weights([(placed[n], SHARDED[n], False) for n in mixer], "gather_mixer_weights")))
    where = jnp.stack([ci, piece]).astype(jnp.int32)
    hooks = _Overlapped([placed[n] for n in FFN_GATHER], where, after=w["w_in"])
    groups = p["ssm_lambda_re"].shape[1]
    w.update({
        "attn_norm_w": p["attn_norm_w"] + hooks.gather_started[:1, :1],
        "ssm_lambda_re": p["ssm_lambda_re"][0], "ssm_lambda_im": p["ssm_lambda_im"][0],
        "ssm_log_dt": p["ssm_log_dt"].reshape(groups, 1), "ssm_b_re": p["ssm_b_re"].reshape(groups, -1),
        "ssm_b_im": p["ssm_b_im"].reshape(groups, -1), "ssm_c_re": p["ssm_c_re"][0], "ssm_c_im": p["ssm_c_im"][0],
        "ssm_d": p["ssm_d"], "ssm_b_glu": p["ssm_b_glu"], "mla_q_norm_w": p["mla_q_norm_w"],
        "mla_kv_norm_w": p["mla_kv_norm_w"], "ssm_out_norm_w": p["ssm_out_norm_w"], "mla_out_norm_w": p["mla_out_norm_w"],
        "ffn_norm_w": p["ffn_norm_w"], "ffn_conv_b": p["ffn_conv_b"], "final_norm_w": p["final_norm_w"].reshape(1, -1),
    })

    loss_tile, dx, g = _local_step(x, positions.reshape(rows, 1).astype(F32), target, w, hooks)
    loss = lax.psum(loss_tile[0, 0], ("x", "y", "c"))

    flat = [g[n].reshape(-1) for n in SMALL] + [g["ffn_conv_w"].reshape(-1)]
    sizes = [f.shape[0] for f in flat]
    per_block = -(-sum(sizes) // (N_CORES * N_CHIPS * SMALL_COLS))
    small_rows = -(-per_block // (2 * SUBLANES)) * (2 * SUBLANES)
    padded = N_CORES * N_CHIPS * small_rows * SMALL_COLS

    def pack(parts):
        parts = list(parts)
        have = sum(q.shape[0] for q in parts)
        return jnp.concatenate(parts + [jnp.zeros((padded - have,), F32)])

    reduced = mixer + ["small"]
    small = pack(flat).reshape(N_CORES, N_CHIPS, small_rows, SMALL_COLS)
    give = [g[n](hooks.other, suffix="_give") for n in mixer] + [lax.dynamic_index_in_dim(small, 1 - ci, 0, keepdims=False)]
    lands = [lax.empty(a.shape, a.dtype) for a in give]
    give_plan = _give_plan(len(reduced))
    swap = _start_copies("grad_mixer_swap_start", give + lands, give_plan, len(reduced), dx)

    grads, delta, new_m, new_v = {}, {}, {}, {}

    def finish(n, joined, after=None):
        grad = joined if SHARDED[n] else joined.reshape(-1, joined.shape[2])
        if n == "w_in":
            wt, mt, vt = w_in, transposed(args["m_w_in"]), transposed(args["v_w_in"])
            out = _adamw(wt, grad, mt, vt, "adamw_w_in")
            delta[n], new_m[n], new_v[n], grads[n] = (jnp.swapaxes(a, 0, 1)[None] for a in out)
            return
        if n == "mla_w_uq":
            grad = _unpad_heads(grad, heads_here)
        adam(n, grad, after)

    def adam(n, grad, after=None):
        shape = p[n].shape
        out = _adamw(p[n].reshape(shape[1:]), grad, args["m_" + n].reshape(shape[1:]),
                     args["v_" + n].reshape(shape[1:]), "adamw_" + n, after)
        delta[n], new_m[n], new_v[n], grads[n] = (a.reshape(shape) for a in out)

    ffn_joined = _join_halves(hooks.ffn_reduced(dx), "grad_ffn_join_halves", after=swap[3])
    got = _wait_copies("grad_mixer_swap_wait", swap, give_plan, ffn_joined[0])[len(reduced):]
    kept = [g[n](hooks.mine, suffix="_keep", add=got[t], wire=True) for t, n in enumerate(mixer)]
    small_sum = _add_other_half(small, got[-1], where, "grad_add_half_small", F32)
    sums = [k[0] for k in kept] + [small_sum]
    wires = [k[1] for k in kept] + [small_sum]
    lands = [lax.empty(s.shape, s.dtype) for s in _scatter_shapes(wires)]
    scatter_plan = _scatter_plan(len(reduced))
    scatter = _start_copies("grad_mixer_scatter_start", wires + lands, scatter_plan, 3 * len(reduced), kept[0][0])
    behind = scatter[3]
    for n, joined in zip(FFN, ffn_joined):
        finish(n, joined, after=behind)
        behind = delta[n]
    got_pieces = _wait_copies("grad_mixer_scatter_wait", scatter, scatter_plan, delta[FFN[-1]])[len(reduced):]
    halves = [_add_pieces(sums[t], got_pieces[t], where, "grad_add_pieces_" + n) for t, n in enumerate(reduced)]
    joined = _join_halves(halves, "grad_join_halves")
    for n, j in zip(mixer, joined):
        finish(n, j)
    eighths = _place_shard(joined[-1].reshape(N_CORES * small_rows, SMALL_COLS), piece_idx, True, "place_small_grads",
                           out_dtype=F32)
    small_sum = _gather_weights([(eighths, True, False)], "gather_small_grads")[0]
    flat_sum = small_sum.reshape(N_CHIPS, N_CORES, small_rows * SMALL_COLS).transpose(1, 0, 2).reshape(-1)
    offs = [0]
    for s in sizes:
        offs.append(offs[-1] + s)
    for k, n in enumerate(SMALL):
        grads[n] = flat_sum[offs[k]:offs[k + 1]].reshape(p[n].shape)
    taps, cols_here = p["ffn_conv_w"].shape[1], p["ffn_conv_w"].shape[2]
    conv_full = flat_sum[offs[len(SMALL)]:offs[len(SMALL) + 1]].reshape(taps, N_CHIPS * cols_here)
    adam("ffn_conv_w", lax.dynamic_slice_in_dim(conv_full, piece * cols_here, cols_here, axis=1))

    def rank2(a):
        return a.reshape(1, -1) if a.ndim == 1 else a

    d_s, m_s, v_s = _adamw_many([rank2(p[n]) for n in SMALL], [rank2(grads[n]) for n in SMALL],
                                [rank2(args["m_" + n]) for n in SMALL], [rank2(args["v_" + n]) for n in SMALL], "adamw_small")
    for k, n in enumerate(SMALL):
        delta[n], new_m[n], new_v[n] = (a.reshape(p[n].shape) for a in (d_s[k], m_s[k], v_s[k]))

    return (loss, dx[None], *[grads[n] for n in WEIGHTS], *[delta[n] for n in WEIGHTS],
            *[new_m[n] for n in WEIGHTS], *[new_v[n] for n in WEIGHTS])


def kernel(x, positions, attn_norm_w, w_in, ssm_lambda_re, ssm_lambda_im, ssm_log_dt, ssm_b_re, ssm_b_im, ssm_c_re, ssm_c_im, ssm_d, ssm_w_glu, ssm_b_glu, mla_q_norm_w, mla_w_uq, mla_kv_norm_w, mla_w_ukv, ssm_out_norm_w, mla_out_norm_w, w_out, ffn_norm_w, ffn_w_up, ffn_conv_w, ffn_conv_b, ffn_w_down, final_norm_w, loss_target, m_attn_norm_w, m_w_in, m_ssm_lambda_re, m_ssm_lambda_im, m_ssm_log_dt, m_ssm_b_re, m_ssm_b_im, m_ssm_c_re, m_ssm_c_im, m_ssm_d, m_ssm_w_glu, m_ssm_b_glu, m_mla_q_norm_w, m_mla_w_uq, m_mla_kv_norm_w, m_mla_w_ukv, m_ssm_out_norm_w, m_mla_out_norm_w, m_w_out, m_ffn_norm_w, m_ffn_w_up, m_ffn_conv_w, m_ffn_conv_b, m_ffn_w_down, m_final_norm_w, v_attn_norm_w, v_w_in, v_ssm_lambda_re, v_ssm_lambda_im, v_ssm_log_dt, v_ssm_b_re, v_ssm_b_im, v_ssm_c_re, v_ssm_c_im, v_ssm_d, v_ssm_w_glu, v_ssm_b_glu, v_mla_q_norm_w, v_mla_w_uq, v_mla_kv_norm_w, v_mla_w_ukv, v_ssm_out_norm_w, v_mla_out_norm_w, v_w_out, v_ffn_norm_w, v_ffn_w_up, v_ffn_conv_w, v_ffn_conv_b, v_ffn_w_down, v_final_norm_w):
    return _step(dict(locals()))
```

```python
import functools
import math

import jax
import jax.numpy as jnp
from jax import lax
from jax.experimental import pallas as pl
from jax.experimental.pallas import tpu as pltpu

F32 = jnp.float32
BF16 = jnp.bfloat16

SSM_GROUP = 16
SSM_STATE = 64
QK_NOPE_DIM = 128
QK_ROPE_DIM = 64
V_HEAD_DIM = 128
ROPE_THETA = 10000.0
RMS_EPS = 1e-6
ADAM_LR, ADAM_B1, ADAM_B2, ADAM_EPS, ADAM_WD, ADAM_STEP = 0.001, 0.9, 0.999, 1e-08, 0.01, 10

LANES = 128
SUBLANES = 8
VMEM_LIMIT_BYTES = 56 * 1024 * 1024

GROUPS_PER_BATCH = LANES // SSM_GROUP
STATE_PER_BATCH = GROUPS_PER_BATCH * SSM_STATE
HEAD_SLOT = 2 * LANES
NEG_INF = -1e30
ATTN_BLOCK = 512
FFN_ROWS = 1024

N_CHIPS = 4
N_CORES = 2


def _tile(n, pref, align=LANES):
    if n <= pref:
        return n
    t = (pref // align) * align
    while t >= align:
        if n % t == 0:
            return t
        t -= align
    return n


def _params(sem):
    return pltpu.CompilerParams(dimension_semantics=sem, vmem_limit_bytes=VMEM_LIMIT_BYTES)


def _dot(a, b, dims):
    return lax.dot_general(a, b, (dims, ((), ())), preferred_element_type=F32)


def _dot_nn(a, b):
    return _dot(a, b, ((1,), (0,)))


def _dot_nt(a, b):
    return _dot(a, b, ((1,), (1,)))


def _dot_tn(a, b):
    return _dot(a, b, ((0,), (0,)))


def _matmul(a, b, *, mode, name, tm=512, tn=1024, tk=2048, bias=None, add=None, out_dtype=F32,
            out_blocks=None, a_split=False, b_split=False, after=None):
    if a_split:
        assert mode == "nt"
        a_shape = (a.shape[1], 2 * a.shape[2])
    else:
        a_shape = a.shape
    if b_split:
        assert mode == "tn"
        b_shape = (b.shape[1], 2 * b.shape[2])
    else:
        b_shape = b.shape
    if mode == "nn":
        (m, k), (k2, n) = a_shape, b_shape
    elif mode == "nt":
        (m, k), (n, k2) = a_shape, b_shape
    else:
        (k, m), (k2, n) = a_shape, b_shape
    assert k == k2, (a.shape, b.shape, mode)
    tm, tn, tk = _tile(m, tm, SUBLANES), _tile(n, tn), _tile(k, tk)
    nk = k // tk
    a_spec = {"nn": pl.BlockSpec((tm, tk), lambda i, j, kk: (i, kk)),
              "nt": pl.BlockSpec((tm, tk), lambda i, j, kk: (i, kk)),
              "tn": pl.BlockSpec((tk, tm), lambda i, j, kk: (kk, i))}[mode]
    b_spec = {"nn": pl.BlockSpec((tk, tn), lambda i, j, kk: (kk, j)),
              "nt": pl.BlockSpec((tn, tk), lambda i, j, kk: (j, kk)),
              "tn": pl.BlockSpec((tk, tn), lambda i, j, kk: (kk, j))}[mode]
    if a_split:
        kb = a.shape[2] // tk
        assert a.shape[2] % tk == 0
        a_spec = pl.BlockSpec((None, tm, tk), lambda i, j, kk: (kk // kb, i, kk % kb))
    if b_split:
        nb = b.shape[2] // tn
        assert b.shape[2] % tn == 0
        b_spec = pl.BlockSpec((None, tk, tn), lambda i, j, kk: (j // nb, kk, j % nb))
    dot = {"nn": _dot_nn, "nt": _dot_nt, "tn": _dot_tn}[mode]
    in_specs, operands = [a_spec, b_spec], [a, b]
    if bias is not None:
        in_specs.append(pl.BlockSpec((1, tn), lambda i, j, kk: (0, j)))
        operands.append(bias)
    if add is not None:
        in_specs.append(pl.BlockSpec((tm, tn), lambda i, j, kk: (i, j)))
        operands.append(add)
    if after is not None:
        in_specs.append(pl.BlockSpec(memory_space=pl.ANY))
        operands.append(after)

    def body(*refs):
        a_ref, b_ref = refs[0], refs[1]
        rest = list(refs[2:])
        bias_ref = rest.pop(0) if bias is not None else None
        add_ref = rest.pop(0) if add is not None else None
        if after is not None:
            rest.pop(0)
        o_ref, acc_ref = rest

        def finish(acc):
            if bias_ref is not None:
                acc = acc + bias_ref[...]
            if add_ref is not None:
                acc = acc + add_ref[...]
            o_ref[...] = acc.astype(o_ref.dtype)

        part = dot(a_ref[...].astype(BF16), b_ref[...].astype(BF16))
        if nk == 1:
            finish(part)
        else:
            kk = pl.program_id(2)

            @pl.when(kk == 0)
            def _():
                acc_ref[...] = part

            @pl.when(jnp.logical_and(kk > 0, kk < nk - 1))
            def _():
                acc_ref[...] += part

            @pl.when(kk == nk - 1)
            def _():
                finish(acc_ref[...] + part)

    if out_blocks is None:
        out_shape = jax.ShapeDtypeStruct((m, n), out_dtype)
        out_spec = pl.BlockSpec((tm, tn), lambda i, j, kk: (i, j))
    else:
        shape, block, index_map = out_blocks(tm, tn)
        out_shape = jax.ShapeDtypeStruct(shape, out_dtype)
        out_spec = pl.BlockSpec(block, index_map)
    acc_shape = (tm, tn) if nk > 1 else (SUBLANES, LANES)
    return pl.pallas_call(
        body, name=name, grid=(m // tm, n // tn, nk), in_specs=in_specs, out_specs=out_spec, out_shape=out_shape,
        scratch_shapes=[pltpu.VMEM(acc_shape, F32)],
        compiler_params=_params(("parallel", "parallel", "arbitrary")),
    )(*operands)


def _wgrad_half(a, b, rows, cols, row_sharded, name, which, *, suffix="", add=None, wire=False, tm=None, tn=None,
                b_split=False):
    tokens = a.shape[0]
    if row_sharded:
        sr, sc = rows // N_CHIPS, cols // N_CORES
    else:
        sr, sc = rows // N_CORES, cols // N_CHIPS
    tm = _tile(sr, 512) if tm is None else tm
    tn = _tile(sc, 1024) if tn is None else tn
    assert sr % tm == 0 and sc % tn == 0, (rows, cols, tm, tn)
    rb, cb = sr // tm, sc // tn
    if tn >= tm:
        ij, grid = (lambda s, t: (t, s)), (N_CHIPS, cb, rb)
    else:
        ij, grid = (lambda s, t: (s, t)), (N_CHIPS, rb, cb)
    if row_sharded:
        a_tile = lambda p, i, j, h: p * rb + i
        b_tile = lambda p, i, j, h: h[0] * cb + j
    else:
        a_tile = lambda p, i, j, h: h[0] * rb + i
        b_tile = lambda p, i, j, h: p * cb + j
    a_spec = pl.BlockSpec((tokens, tm), lambda p, s, t, h: (0, a_tile(p, *ij(s, t), h)))
    if b_split:
        nbh = b.shape[2] // tn
        assert b.shape[2] % tn == 0
        b_spec = pl.BlockSpec((None, tokens, tn), lambda p, s, t, h: (b_tile(p, *ij(s, t), h) // nbh, 0,
                                                                       b_tile(p, *ij(s, t), h) % nbh))
    else:
        b_spec = pl.BlockSpec((tokens, tn), lambda p, s, t, h: (0, b_tile(p, *ij(s, t), h)))
    out_spec = pl.BlockSpec((None, tm, tn), lambda p, s, t, h: (p, *ij(s, t)))
    in_specs, operands = [a_spec, b_spec], [a, b]
    if add is not None:
        in_specs.append(out_spec)
        operands.append(add)

    def body(h_ref, a_ref, b_ref, *rest):
        acc = _dot_tn(a_ref[...].astype(BF16), b_ref[...].astype(BF16))
        if add is not None:
            acc = acc + rest[0][...]
        for o_ref in rest[1 if add is not None else 0:]:
            o_ref[...] = acc.astype(o_ref.dtype)

    out_dtypes = [F32, BF16] if wire else [F32]
    out = pl.pallas_call(
        body, name=name + suffix, out_shape=[jax.ShapeDtypeStruct((N_CHIPS, sr, sc), dt) for dt in out_dtypes],
        grid_spec=pltpu.PrefetchScalarGridSpec(num_scalar_prefetch=1, grid=grid, in_specs=in_specs,
                                               out_specs=[out_spec] * len(out_dtypes)),
        compiler_params=_params(("parallel", "parallel", "parallel")),
    )(which, *operands)
    return tuple(out) if wire else out[0]


def _rms_rows(x):
    return lax.rsqrt(jnp.mean(x * x, axis=-1, keepdims=True) + RMS_EPS)


def _rmsnorm_fwd(x, w, *, name, width=None, col=0, out_dtype=BF16, tr=256):
    rows = x.shape[0]
    width = x.shape[1] if width is None else width
    tr = _tile(rows, tr, SUBLANES)

    def body(x_ref, w_ref, o_ref):
        xv = x_ref[...]
        o_ref[...] = (xv * _rms_rows(xv) * w_ref[...]).astype(o_ref.dtype)

    return pl.pallas_call(
        body, name=name, grid=(rows // tr,),
        in_specs=[pl.BlockSpec((tr, width), lambda i: (i, col)), pl.BlockSpec((1, width), lambda i: (0, 0))],
        out_specs=pl.BlockSpec((tr, width), lambda i: (i, 0)),
        out_shape=jax.ShapeDtypeStruct((rows, width), out_dtype),
        compiler_params=_params(("parallel",)),
    )(x, w)


def _rmsnorm_bwd_rows(xv, w, dy):
    r = _rms_rows(xv)
    n = xv * r
    dn = dy * w
    dx = r * (dn - n * jnp.mean(dn * n, axis=-1, keepdims=True))
    return dx, dy * n


def _rmsnorm_bwd(x, w, dy, *, name, width=None, col=0, dy_col=0, add=None, tr=256, dx_dtypes=(F32,)):
    rows = x.shape[0]
    n_dx = len(dx_dtypes)
    width = x.shape[1] if width is None else width
    tr = _tile(rows, tr, SUBLANES)
    in_specs = [pl.BlockSpec((tr, width), lambda i: (i, col)), pl.BlockSpec((1, width), lambda i: (0, 0)),
                pl.BlockSpec((tr, width), lambda i: (i, dy_col))]
    operands = [x, w, dy]
    if add is not None:
        in_specs.append(pl.BlockSpec((tr, width), lambda i: (i, 0)))
        operands.append(add)

    def body(*refs):
        x_ref, w_ref, dy_ref = refs[:3]
        add_ref = refs[3] if add is not None else None
        dx_refs, dw_ref = refs[-1 - n_dx:-1], refs[-1]
        dx, dwp = _rmsnorm_bwd_rows(x_ref[...], w_ref[...], dy_ref[...])
        if add_ref is not None:
            dx = dx + add_ref[...]
        for dx_ref in dx_refs:
            dx_ref[...] = dx.astype(dx_ref.dtype)
        part = jnp.sum(dwp, axis=0, keepdims=True)

        @pl.when(pl.program_id(0) == 0)
        def _():
            dw_ref[...] = part

        @pl.when(pl.program_id(0) > 0)
        def _():
            dw_ref[...] += part

    return pl.pallas_call(
        body, name=name, grid=(rows // tr,), in_specs=in_specs,
        out_specs=[pl.BlockSpec((tr, width), lambda i: (i, 0))] * n_dx + [pl.BlockSpec((1, width), lambda i: (0, 0))],
        out_shape=[jax.ShapeDtypeStruct((rows, width), dt) for dt in dx_dtypes] + [jax.ShapeDtypeStruct((1, width), F32)],
        compiler_params=_params(("arbitrary",)),
    )(*operands)


def _final_norm_loss(h, w, target, *, tr=256):
    rows, d = h.shape
    tr = _tile(rows, tr, SUBLANES)

    def body(h_ref, w_ref, t_ref, loss_ref, dh_ref, dhb_ref, dw_ref):
        hv, wv = h_ref[...], w_ref[...]
        r = _rms_rows(hv)
        n = hv * r
        err = n * wv - t_ref[...]
        d_out = err * (1.0 / d)
        dn = d_out * wv
        dh = r * (dn - n * jnp.mean(dn * n, axis=-1, keepdims=True))
        dh_ref[...] = dh
        dhb_ref[...] = dh.astype(BF16)
        dw_part = jnp.sum(d_out * n, axis=0, keepdims=True)
        loss_part = jnp.full((SUBLANES, LANES), 0.5 / d, F32) * jnp.sum(err * err)

        @pl.when(pl.program_id(0) == 0)
        def _():
            dw_ref[...] = dw_part
            loss_ref[...] = loss_part

        @pl.when(pl.program_id(0) > 0)
        def _():
            dw_ref[...] += dw_part
            loss_ref[...] += loss_part

    return pl.pallas_call(
        body, name="final_norm_loss", grid=(rows // tr,),
        in_specs=[pl.BlockSpec((tr, d), lambda i: (i, 0)), pl.BlockSpec((1, d), lambda i: (0, 0)),
                  pl.BlockSpec((tr, d), lambda i: (i, 0))],
        out_specs=[pl.BlockSpec((SUBLANES, LANES), lambda i: (0, 0)), pl.BlockSpec((tr, d), lambda i: (i, 0)),
                   pl.BlockSpec((tr, d), lambda i: (i, 0)), pl.BlockSpec((1, d), lambda i: (0, 0))],
        out_shape=[jax.ShapeDtypeStruct((SUBLANES, LANES), F32), jax.ShapeDtypeStruct((rows, d), F32),
                   jax.ShapeDtypeStruct((rows, d), BF16), jax.ShapeDtypeStruct((1, d), F32)],
        compiler_params=_params(("arbitrary",)),
    )(h, w, target)


def _cmul(ar, ai, br, bi):
    return ar * br - ai * bi, ar * bi + ai * br


def _expand_matrix(groups, reps):
    row = lax.broadcasted_iota(jnp.int32, (groups, groups * reps), 0)
    colg = lax.broadcasted_iota(jnp.int32, (groups, groups * reps), 1) // reps
    return (row == colg).astype(F32)


def _dot_exact(a, b, dims):
    return lax.dot_general(a, b, (dims, ((), ())), preferred_element_type=F32, precision=lax.Precision.HIGHEST)


def _s5_discretize(lr, li, dt):
    mag = jnp.exp(lr * dt)
    th = li * dt
    ar, ai = mag * jnp.cos(th), mag * jnp.sin(th)
    nr, ni = ar - 1.0, ai
    den = lr * lr + li * li
    zr = (nr * lr + ni * li) / den
    zi = (ni * lr - nr * li) / den
    return mag, ar, ai, nr, ni, den, zr, zi


def _s5_params(lam_re, lam_im, log_dt, b_re, b_im):
    g, p = lam_re.shape
    ph = b_re.shape[1]

    def body(lr_ref, li_ref, ldt_ref, br_ref, bi_ref, ar_ref, ai_ref, bbr_ref, bbi_ref):
        dt = jnp.exp(ldt_ref[...])
        _, ar, ai, _, _, _, zr, zi = _s5_discretize(lr_ref[...], li_ref[...], dt)
        ar_ref[...] = ar
        ai_ref[...] = ai
        e = _expand_matrix(p, ph // p)
        zr_x = _dot_exact(zr, e, ((1,), (0,)))
        zi_x = _dot_exact(zi, e, ((1,), (0,)))
        bre, bim = br_ref[...], bi_ref[...]
        bbr_ref[...] = zr_x * bre - zi_x * bim
        bbi_ref[...] = zr_x * bim + zi_x * bre

    return pl.pallas_call(
        body, name="s5_params",
        out_shape=[jax.ShapeDtypeStruct((g, p), F32)] * 2 + [jax.ShapeDtypeStruct((g, ph), F32)] * 2,
    )(lam_re, lam_im, log_dt, b_re, b_im)


def _s5_params_bwd(lam_re, lam_im, log_dt, b_re, b_im, d_ar, d_ai, d_bbr, d_bbi):
    g, p = lam_re.shape
    ph = b_re.shape[1]

    def body(lr_ref, li_ref, ldt_ref, br_ref, bi_ref, dar_ref, dai_ref, dbr_ref, dbi_ref,
             dlr_ref, dli_ref, dldt_ref, dbre_ref, dbim_ref):
        lr, li = lr_ref[...], li_ref[...]
        dt = jnp.exp(ldt_ref[...])
        mag, ar, ai, nr, ni, den, zr, zi = _s5_discretize(lr, li, dt)
        e = _expand_matrix(p, ph // p)
        zr_x = _dot_exact(zr, e, ((1,), (0,)))
        zi_x = _dot_exact(zi, e, ((1,), (0,)))
        bre, bim, dbr, dbi = br_ref[...], bi_ref[...], dbr_ref[...], dbi_ref[...]
        dbre_ref[...] = zr_x * dbr + zi_x * dbi
        dbim_ref[...] = zr_x * dbi - zi_x * dbr
        dzr = _dot_exact(bre * dbr + bim * dbi, e, ((1,), (1,)))
        dzi = _dot_exact(bre * dbi - bim * dbr, e, ((1,), (1,)))
        inv = 1.0 / den
        d_nr = (dzr * lr - dzi * li) * inv
        d_ni = (dzr * li + dzi * lr) * inv
        d_den = -(dzr * zr + dzi * zi) * inv
        d_lr = (dzr * nr + dzi * ni) * inv + 2.0 * lr * d_den
        d_li = (dzr * ni - dzi * nr) * inv + 2.0 * li * d_den
        t_ar = dar_ref[...] + d_nr
        t_ai = dai_ref[...] + d_ni
        d_lrdt = t_ar * ar + t_ai * ai
        d_th = t_ai * ar - t_ar * ai
        dlr_ref[...] = d_lr + d_lrdt * dt
        dli_ref[...] = d_li + d_th * dt
        dldt_ref[...] = jnp.sum(d_lrdt * lr + d_th * li, axis=1, keepdims=True) * dt

    return pl.pallas_call(
        body, name="s5_params_bwd",
        out_shape=[jax.ShapeDtypeStruct((g, p), F32)] * 2 + [jax.ShapeDtypeStruct((g, 1), F32)]
        + [jax.ShapeDtypeStruct((g, ph), F32)] * 2,
    )(lam_re, lam_im, log_dt, b_re, b_im, d_ar, d_ai, d_bbr, d_bbi)


def _powers(ar, ai, count):
    out = [(ar, ai)]
    for _ in range(count - 1):
        out.append(_cmul(out[-1][0], out[-1][1], ar, ai))
    return out


def _scan_coefs(ar, ai, reverse):
    w = ar.shape[-1]
    pw = _powers(ar, ai, SUBLANES)
    row = lax.broadcasted_iota(jnp.int32, (SUBLANES, w), 0)
    steps = []
    d = 1
    while d < SUBLANES:
        keep = (row < SUBLANES - d) if reverse else (row >= d)
        pr, pi = pw[d - 1]
        steps.append((d, jnp.where(keep, pr, 0.0), jnp.where(keep, pi, 0.0)))
        d *= 2
    cr = jnp.zeros((SUBLANES, w), F32)
    ci = jnp.zeros((SUBLANES, w), F32)
    for t in range(SUBLANES):
        pr, pi = pw[SUBLANES - 1 - t] if reverse else pw[t]
        cr = jnp.where(row == t, pr, cr)
        ci = jnp.where(row == t, pi, ci)
    return steps, cr, ci


def _scan_tile(xr, xi, carry_r, carry_i, coefs, reverse):
    steps, cr, ci = coefs
    for d, mr, mi in steps:
        shift = SUBLANES - d if reverse else d
        sr, si = pltpu.roll(xr, shift, 0), pltpu.roll(xi, shift, 0)
        pr, pi = _cmul(mr, mi, sr, si)
        xr, xi = xr + pr, xi + pi
    pr, pi = _cmul(cr, ci, carry_r, carry_i)
    return xr + pr, xi + pi


def _gelu(x):
    c = math.sqrt(2.0 / math.pi)
    return 0.5 * x * (1.0 + jnp.tanh(c * (x + 0.044715 * x * x * x)))


def _gelu_grad(x):
    c = math.sqrt(2.0 / math.pi)
    t = jnp.tanh(c * (x + 0.044715 * x * x * x))
    return 0.5 * (1.0 + t) + 0.5 * x * (1.0 - t * t) * c * (1.0 + 3.0 * 0.044715 * x * x)


def _s5_fwd(proj, wb, wc, d_skip, abar):
    rows = proj.shape[0]
    nb = wb.shape[0]
    s2 = 2 * STATE_PER_BATCH
    st = STATE_PER_BATCH
    chunk = _tile(rows, 512, SUBLANES)

    def body(u_ref, wb_ref, wc_ref, d_ref, a_ref, s_ref, y_ref, yg_ref):
        for c0 in range(0, rows, chunk):
            s_ref[pl.ds(c0, chunk), :] = _dot_nn(u_ref[pl.ds(c0, chunk), :].astype(BF16), wb_ref[...])
        av = a_ref[...]
        coefs = _scan_coefs(av[:, :st], av[:, st:], reverse=False)

        def tile(b, carry):
            r0 = pl.multiple_of(b * SUBLANES, SUBLANES)
            xr, xi = _scan_tile(s_ref[pl.ds(r0, SUBLANES), :st], s_ref[pl.ds(r0, SUBLANES), st:], carry[0], carry[1],
                                coefs, False)
            s_ref[pl.ds(r0, SUBLANES), :st] = xr
            s_ref[pl.ds(r0, SUBLANES), st:] = xi
            return xr[SUBLANES - 1:, :], xi[SUBLANES - 1:, :]

        zero = jnp.zeros((1, st), F32)
        lax.fori_loop(0, rows // SUBLANES, tile, (zero, zero))
        for c0 in range(0, rows, chunk):
            y = _dot_nn(s_ref[pl.ds(c0, chunk), :].astype(BF16), wc_ref[...]) + d_ref[...] * u_ref[pl.ds(c0, chunk), :]
            y_ref[pl.ds(c0, chunk), :] = y
            yg_ref[pl.ds(c0, chunk), :] = _gelu(y).astype(BF16)

    return pl.pallas_call(
        body, name="s5_fwd", grid=(nb,),
        in_specs=[pl.BlockSpec((rows, LANES), lambda j: (0, j)), pl.BlockSpec((None, LANES, s2), lambda j: (j, 0, 0)),
                  pl.BlockSpec((None, s2, LANES), lambda j: (j, 0, 0)), pl.BlockSpec((1, LANES), lambda j: (0, j)),
                  pl.BlockSpec((None, 1, s2), lambda j: (j, 0, 0))],
        out_specs=[pl.BlockSpec((rows, s2), lambda j: (0, j)), pl.BlockSpec((rows, LANES), lambda j: (0, j)),
                   pl.BlockSpec((rows, LANES), lambda j: (0, j))],
        out_shape=[jax.ShapeDtypeStruct((rows, nb * s2), F32), jax.ShapeDtypeStruct((rows, nb * LANES), F32),
                   jax.ShapeDtypeStruct((rows, nb * LANES), BF16)],
        compiler_params=_params(("parallel",)),
    )(proj, wb, wc, d_skip, abar)


def _s5_bwd(proj, states, y_pre, dyg_a, dyg_b, wb, wc, d_skip, abar):
    rows = proj.shape[0]
    nb = wb.shape[0]
    s2 = 2 * STATE_PER_BATCH
    st = STATE_PER_BATCH
    chunk = _tile(rows, 512, SUBLANES)
    n_tiles = rows // SUBLANES

    def body(u_ref, s_ref, y_ref, ga_ref, gb_ref, wb_ref, wc_ref, d_ref, a_ref,
             du_ref, dwb_ref, dwc_ref, da_ref, dd_ref, ds_ref, dy_ref):
        dy_ref[...] = (ga_ref[...] + gb_ref[...]) * _gelu_grad(y_ref[...])
        dd_ref[...] = jnp.sum(dy_ref[...] * u_ref[...], axis=0, keepdims=True)
        for c0 in range(0, rows, chunk):
            ds_ref[pl.ds(c0, chunk), :] = _dot_nt(dy_ref[pl.ds(c0, chunk), :].astype(BF16), wc_ref[...])
        dwc_ref[...] = _dot_tn(s_ref[...].astype(BF16), dy_ref[...].astype(BF16))
        av = a_ref[...]
        coefs = _scan_coefs(av[:, :st], -av[:, st:], reverse=True)
        row = lax.broadcasted_iota(jnp.int32, (SUBLANES, st), 0)

        def tile(k, carry):
            cr, ci, acc_r, acc_i = carry
            b = n_tiles - 1 - k
            r0 = pl.multiple_of(b * SUBLANES, SUBLANES)
            rp = pl.multiple_of(jnp.maximum(b - 1, 0) * SUBLANES, SUBLANES)
            xr, xi = _scan_tile(ds_ref[pl.ds(r0, SUBLANES), :st], ds_ref[pl.ds(r0, SUBLANES), st:], cr, ci, coefs, True)
            ds_ref[pl.ds(r0, SUBLANES), :st] = xr
            ds_ref[pl.ds(r0, SUBLANES), st:] = xi
            first = jnp.where(b > 0, 1.0, 0.0)
            pr = jnp.where(row == 0, pltpu.roll(s_ref[pl.ds(rp, SUBLANES), :st], 1, 0) * first,
                           pltpu.roll(s_ref[pl.ds(r0, SUBLANES), :st], 1, 0))
            pi = jnp.where(row == 0, pltpu.roll(s_ref[pl.ds(rp, SUBLANES), st:], 1, 0) * first,
                           pltpu.roll(s_ref[pl.ds(r0, SUBLANES), st:], 1, 0))
            acc_r = acc_r + pr * xr + pi * xi
            acc_i = acc_i + pr * xi - pi * xr
            return xr[:1, :], xi[:1, :], acc_r, acc_i

        zero = jnp.zeros((1, st), F32)
        zacc = jnp.zeros((SUBLANES, st), F32)
        _, _, acc_r, acc_i = lax.fori_loop(0, n_tiles, tile, (zero, zero, zacc, zacc))
        da_ref[:, :st] = jnp.sum(acc_r, axis=0, keepdims=True)
        da_ref[:, st:] = jnp.sum(acc_i, axis=0, keepdims=True)
        for c0 in range(0, rows, chunk):
            du_ref[pl.ds(c0, chunk), :] = (_dot_nt(ds_ref[pl.ds(c0, chunk), :].astype(BF16), wb_ref[...])
                                           + d_ref[...] * dy_ref[pl.ds(c0, chunk), :]).astype(du_ref.dtype)
        dwb_ref[...] = _dot_tn(u_ref[...].astype(BF16), ds_ref[...].astype(BF16))

    col = pl.BlockSpec((rows, LANES), lambda j: (0, j))
    return pl.pallas_call(
        body, name="s5_bwd", grid=(nb,),
        in_specs=[col, pl.BlockSpec((rows, s2), lambda j: (0, j)), col, col, col,
                  pl.BlockSpec((None, LANES, s2), lambda j: (j, 0, 0)), pl.BlockSpec((None, s2, LANES), lambda j: (j, 0, 0)),
                  pl.BlockSpec((1, LANES), lambda j: (0, j)), pl.BlockSpec((None, 1, s2), lambda j: (j, 0, 0))],
        out_specs=[col, pl.BlockSpec((None, LANES, s2), lambda j: (j, 0, 0)),
                   pl.BlockSpec((None, s2, LANES), lambda j: (j, 0, 0)), pl.BlockSpec((None, 1, s2), lambda j: (j, 0, 0)),
                   pl.BlockSpec((1, LANES), lambda j: (0, j))],
        out_shape=[jax.ShapeDtypeStruct((rows, nb * LANES), BF16), jax.ShapeDtypeStruct((nb, LANES, s2), F32),
                   jax.ShapeDtypeStruct((nb, s2, LANES), F32), jax.ShapeDtypeStruct((nb, 1, s2), F32),
                   jax.ShapeDtypeStruct((1, nb * LANES), F32)],
        scratch_shapes=[pltpu.VMEM((rows, s2), F32), pltpu.VMEM((rows, LANES), F32)],
        compiler_params=_params(("parallel",)),
    )(proj, states, y_pre, dyg_a, dyg_b, wb, wc, d_skip, abar)


def _glu_norm_fwd(y_pre, z, w, *, tr=256):
    rows, width = y_pre.shape
    tr = _tile(rows, tr, SUBLANES)

    def body(y_ref, z_ref, w_ref, o_ref):
        v = _gelu(y_ref[...]) * jax.nn.sigmoid(z_ref[...])
        o_ref[...] = (v * _rms_rows(v) * w_ref[...]).astype(o_ref.dtype)

    blk = pl.BlockSpec((tr, width), lambda i: (i, 0))
    return pl.pallas_call(
        body, name="glu_norm_fwd", grid=(rows // tr,),
        in_specs=[blk, blk, pl.BlockSpec((1, width), lambda i: (0, 0))], out_specs=blk,
        out_shape=jax.ShapeDtypeStruct((rows, width), BF16), compiler_params=_params(("parallel",)),
    )(y_pre, z, w)


def _glu_norm_bwd(y_pre, z, w, dycat, *, tr=256):
    rows, width = y_pre.shape
    tr = _tile(rows, tr, SUBLANES)

    def body(y_ref, z_ref, w_ref, dy_ref, dz_ref, dg_ref, dw_ref, db_ref):
        yg = _gelu(y_ref[...])
        sg = jax.nn.sigmoid(z_ref[...])
        dv, dwp = _rmsnorm_bwd_rows(yg * sg, w_ref[...], dy_ref[...])
        dz = dv * yg * sg * (1.0 - sg)
        dz_ref[...] = dz.astype(dz_ref.dtype)
        dg_ref[...] = dv * sg
        dw_part = jnp.sum(dwp, axis=0, keepdims=True)
        db_part = jnp.sum(dz, axis=0, keepdims=True)

        @pl.when(pl.program_id(0) == 0)
        def _():
            dw_ref[...] = dw_part
            db_ref[...] = db_part

        @pl.when(pl.program_id(0) > 0)
        def _():
            dw_ref[...] += dw_part
            db_ref[...] += db_part

    blk = pl.BlockSpec((tr, width), lambda i: (i, 0))
    vec = pl.BlockSpec((1, width), lambda i: (0, 0))
    return pl.pallas_call(
        body, name="glu_norm_bwd", grid=(rows // tr,), in_specs=[blk, blk, vec, blk], out_specs=[blk, blk, vec, vec],
        out_shape=[jax.ShapeDtypeStruct((rows, width), BF16), jax.ShapeDtypeStruct((rows, width), F32)]
        + [jax.ShapeDtypeStruct((1, width), F32)] * 2,
        compiler_params=_params(("arbitrary",)),
    )(y_pre, z, w, dycat)


def _rope_tables(pos, freq, sign):
    rows = pos.shape[0]

    def body(p_ref, f_ref, s_ref, cos_ref, sin_ref):
        ang = p_ref[...] * f_ref[...]
        cos_ref[...] = jnp.cos(ang)
        sin_ref[...] = jnp.sin(ang) * s_ref[...]

    return pl.pallas_call(body, name="rope_tables", out_shape=[jax.ShapeDtypeStruct((rows, LANES), F32)] * 2)(pos, freq, sign)


def _rope(x, cos, sin_signed):
    lane = lax.broadcasted_iota(jnp.int32, x.shape, 1)
    half = QK_ROPE_DIM // 2
    swapped = jnp.where(lane < half, pltpu.roll(x, LANES - half, 1), pltpu.roll(x, half, 1))
    return x * cos + swapped * sin_signed


def _attn_prep(q, kv, proj, kpe_col, cos, sin, *, tr=256):
    rows = q.shape[0]
    heads = q.shape[1] // HEAD_SLOT
    tr = _tile(rows, tr, SUBLANES)

    def body(q_ref, kv_ref, kpe_ref, cos_ref, sin_ref, qc_ref, kc_ref, v_ref):
        c, s = cos_ref[...], sin_ref[...]
        qc_ref[:, :LANES] = q_ref[:, :LANES].astype(BF16)
        qc_ref[:, LANES:] = _rope(q_ref[:, LANES:], c, s).astype(BF16)
        kc_ref[:, :LANES] = kv_ref[:, :LANES].astype(BF16)
        kc_ref[:, LANES:] = _rope(kpe_ref[...], c, s).astype(BF16)
        v_ref[...] = kv_ref[:, LANES:].astype(BF16)

    slot = pl.BlockSpec((tr, HEAD_SLOT), lambda i, h: (i, h))
    tab = pl.BlockSpec((tr, LANES), lambda i, h: (i, 0))
    return pl.pallas_call(
        body, name="attn_prep", grid=(rows // tr, heads),
        in_specs=[slot, slot, pl.BlockSpec((tr, LANES), lambda i, h: (i, kpe_col)), tab, tab],
        out_specs=[slot, slot, pl.BlockSpec((tr, LANES), lambda i, h: (i, h))],
        out_shape=[jax.ShapeDtypeStruct((rows, heads * HEAD_SLOT), BF16)] * 2
        + [jax.ShapeDtypeStruct((rows, heads * LANES), BF16)],
        compiler_params=_params(("parallel", "parallel")),
    )(q, kv, proj, cos, sin)


def _causal(tq, tk):
    return lax.broadcasted_iota(jnp.int32, (tq, tk), 1) <= lax.broadcasted_iota(jnp.int32, (tq, tk), 0)


def _attn_fwd(qc, kc, vb, *, scale, tq=512):
    rows = qc.shape[0]
    heads = qc.shape[1] // HEAD_SLOT
    tq = _tile(rows, tq, SUBLANES)
    tk = tq

    def body(q_ref, k_ref, v_ref, o_ref, lse_ref):
        i = pl.program_id(1)
        q = q_ref[...]

        def step(j, carry, diagonal):
            m, l, acc = carry
            k0 = pl.multiple_of(j * tk, tk)
            s = _dot_nt(q, k_ref[pl.ds(k0, tk), :]) * scale
            if diagonal:
                s = jnp.where(_causal(tq, tk), s, NEG_INF)
            m_new = jnp.maximum(m, jnp.max(s, axis=-1, keepdims=True))
            p = jnp.exp(s - m_new)
            alpha = jnp.exp(m - m_new)
            l = alpha * l + jnp.sum(p, axis=-1, keepdims=True)
            acc = alpha * acc + _dot_nn(p.astype(BF16), v_ref[pl.ds(k0, tk), :])
            return m_new, l, acc

        init = (jnp.full((tq, 1), NEG_INF, F32), jnp.zeros((tq, 1), F32), jnp.zeros((tq, LANES), F32))
        below = lax.fori_loop(0, i, lambda j, carry: step(j, carry, False), init)
        m, l, acc = step(i, below, True)
        o_ref[...] = acc / l
        lse_ref[...] = jnp.broadcast_to(m + jnp.log(l), (tq, LANES))

    return pl.pallas_call(
        body, name="attn_fwd", grid=(heads, rows // tq),
        in_specs=[pl.BlockSpec((tq, HEAD_SLOT), lambda h, i: (i, h)), pl.BlockSpec((rows, HEAD_SLOT), lambda h, i: (0, h)),
                  pl.BlockSpec((rows, LANES), lambda h, i: (0, h))],
        out_specs=[pl.BlockSpec((tq, LANES), lambda h, i: (i, h))] * 2,
        out_shape=[jax.ShapeDtypeStruct((rows, heads * LANES), F32)] * 2,
        compiler_params=_params(("parallel", "parallel")),
    )(qc, kc, vb)


def _attn_bwd(qc, kc, vb, o, do, lse, cos, sin, *, scale, tk=512):
    rows = qc.shape[0]
    heads = qc.shape[1] // HEAD_SLOT
    tk = _tile(rows, tk, SUBLANES)
    tq = tk
    nq = rows // tq

    def body(q_ref, k_ref, v_ref, o_ref, do_ref, lse_ref, cos_ref, sin_ref, dq_ref, dkv_ref, dkpe_ref, dq_acc, delta_ref):
        j = pl.program_id(1)

        @pl.when(j == 0)
        def _():
            dq_acc[...] = jnp.zeros_like(dq_acc)
            for r0 in range(0, rows, tq):
                d = jnp.sum(do_ref[pl.ds(r0, tq), :] * o_ref[pl.ds(r0, tq), :], axis=-1, keepdims=True)
                delta_ref[pl.ds(r0, tq), :] = jnp.broadcast_to(d, (tq, LANES))

        kb, vv = k_ref[...], v_ref[...]

        def step(i, carry, diagonal):
            dk, dv = carry
            q0 = pl.multiple_of(i * tq, tq)
            qb = q_ref[pl.ds(q0, tq), :]
            dob = do_ref[pl.ds(q0, tq), :].astype(BF16)
            s = _dot_nt(qb, kb) * scale
            p = jnp.exp(s - lse_ref[pl.ds(q0, tq), :1])
            if diagonal:
                p = jnp.where(_causal(tq, tk), p, 0.0)
            dv = dv + _dot_tn(p.astype(BF16), dob)
            ds = (p * (_dot_nt(dob, vv) - delta_ref[pl.ds(q0, tq), :1])).astype(BF16)
            dk = dk + _dot_tn(ds, qb)
            dq_acc[pl.ds(q0, tq), :] += _dot_nn(ds, kb)
            return dk, dv

        zero = (jnp.zeros((tk, HEAD_SLOT), F32), jnp.zeros((tk, LANES), F32))
        dk, dv = lax.fori_loop(j + 1, nq, lambda i, carry: step(i, carry, False), step(j, zero, True))
        dkv_ref[:, :LANES] = (dk[:, :LANES] * scale).astype(dkv_ref.dtype)
        dkv_ref[:, LANES:] = dv.astype(dkv_ref.dtype)
        dkpe_ref[...] = dk[:, LANES:] * scale

        @pl.when(j == nq - 1)
        def _():
            for r0 in range(0, rows, tq):
                dq = dq_acc[pl.ds(r0, tq), :] * scale
                dq_ref[pl.ds(r0, tq), :LANES] = dq[:, :LANES].astype(dq_ref.dtype)
                dq_ref[pl.ds(r0, tq), LANES:] = _rope(dq[:, LANES:], cos_ref[pl.ds(r0, tq), :],
                                                      -sin_ref[pl.ds(r0, tq), :]).astype(dq_ref.dtype)

    full_q = pl.BlockSpec((rows, HEAD_SLOT), lambda h, j: (0, h))
    full_v = pl.BlockSpec((rows, LANES), lambda h, j: (0, h))
    tab = pl.BlockSpec((rows, LANES), lambda h, j: (0, 0))
    return pl.pallas_call(
        body, name="attn_bwd", grid=(heads, rows // tk),
        in_specs=[full_q, pl.BlockSpec((tk, HEAD_SLOT), lambda h, j: (j, h)), pl.BlockSpec((tk, LANES), lambda h, j: (j, h)),
                  full_v, full_v, full_v, tab, tab],
        out_specs=[full_q, pl.BlockSpec((tk, HEAD_SLOT), lambda h, j: (j, h)), pl.BlockSpec((tk, LANES), lambda h, j: (j, h))],
        out_shape=[jax.ShapeDtypeStruct((rows, heads * HEAD_SLOT), BF16), jax.ShapeDtypeStruct((rows, heads * HEAD_SLOT), BF16),
                   jax.ShapeDtypeStruct((rows, heads * LANES), F32)],
        scratch_shapes=[pltpu.VMEM((rows, HEAD_SLOT), F32), pltpu.VMEM((rows, LANES), F32)],
        compiler_params=_params(("parallel", "arbitrary")),
    )(qc, kc, vb, o, do, lse, cos, sin)


def _kpe_bwd(dkpe_heads, cos, sin, *, tr=512):
    rows = dkpe_heads.shape[0]
    heads = dkpe_heads.shape[1] // LANES
    tr = _tile(rows, tr, 2 * SUBLANES)

    def body(d_ref, cos_ref, sin_ref, o_ref):
        acc = d_ref[:, :LANES]
        for h in range(1, heads):
            acc = acc + d_ref[:, h * LANES:(h + 1) * LANES]
        o_ref[...] = _rope(acc, cos_ref[...], -sin_ref[...]).astype(o_ref.dtype)

    tab = pl.BlockSpec((tr, LANES), lambda i: (i, 0))
    return pl.pallas_call(
        body, name="kpe_bwd", grid=(rows // tr,),
        in_specs=[pl.BlockSpec((tr, heads * LANES), lambda i: (i, 0)), tab, tab], out_specs=tab,
        out_shape=jax.ShapeDtypeStruct((rows, LANES), BF16), compiler_params=_params(("parallel",)),
    )(dkpe_heads, cos, sin)


CONV_ROWS = 128


def _with_halo(ref, r0, ci, n_chunks, ch, lanes, before, after):
    parts = []
    if before:
        lo = pl.multiple_of(jnp.maximum(r0 - SUBLANES, 0), SUBLANES)
        parts.append(ref[pl.ds(lo, SUBLANES), lanes] * jnp.where(ci > 0, 1.0, 0.0))
    parts.append(ref[pl.ds(r0, ch), lanes])
    if after:
        hi = pl.multiple_of(jnp.minimum(r0 + ch, n_chunks * ch - SUBLANES), SUBLANES)
        parts.append(ref[pl.ds(hi, SUBLANES), lanes] * jnp.where(ci < n_chunks - 1, 1.0, 0.0))
    return jnp.concatenate(parts, axis=0)


def _taps(ext):
    return pltpu.roll(ext, 2, 0)[SUBLANES:], pltpu.roll(ext, 1, 0)[SUBLANES:], ext[SUBLANES:]


def _conv3(taps, w, b):
    return w[0:1, :] * taps[0] + w[1:2, :] * taps[1] + w[2:3, :] * taps[2] + b


def _conv_gate_fwd(a, conv_w, conv_b, *, tc=256):
    rows, f2 = a.shape
    f = f2 // 2
    tc = _tile(f, tc)
    nc = f // tc
    ch = _tile(rows, CONV_ROWS, SUBLANES)
    n_chunks = rows // ch

    def body(ag_ref, av_ref, wg_ref, wv_ref, bg_ref, bv_ref, o_ref):
        for lt in range(tc // LANES):
            lanes = slice(lt * LANES, (lt + 1) * LANES)
            wg, wv, bg, bv = wg_ref[:, lanes], wv_ref[:, lanes], bg_ref[:, lanes], bv_ref[:, lanes]

            def chunk(ci, carry):
                r0 = pl.multiple_of(ci * ch, ch)
                gate = _conv3(_taps(_with_halo(ag_ref, r0, ci, n_chunks, ch, lanes, True, False)), wg, bg)
                val = _conv3(_taps(_with_halo(av_ref, r0, ci, n_chunks, ch, lanes, True, False)), wv, bv)
                o_ref[pl.ds(r0, ch), lanes] = (gate * jax.nn.sigmoid(gate) * val).astype(o_ref.dtype)
                return carry

            lax.fori_loop(0, n_chunks, chunk, 0)

    return pl.pallas_call(
        body, name="conv_gate_fwd", grid=(nc,),
        in_specs=[pl.BlockSpec((rows, tc), lambda j: (0, j)), pl.BlockSpec((rows, tc), lambda j: (0, j + nc)),
                  pl.BlockSpec((SUBLANES, tc), lambda j: (0, j)), pl.BlockSpec((SUBLANES, tc), lambda j: (0, j + nc)),
                  pl.BlockSpec((1, tc), lambda j: (0, j)), pl.BlockSpec((1, tc), lambda j: (0, j + nc))],
        out_specs=pl.BlockSpec((rows, tc), lambda j: (0, j)),
        out_shape=jax.ShapeDtypeStruct((rows, f), BF16), compiler_params=_params(("parallel",)),
    )(a, a, conv_w, conv_w, conv_b, conv_b)


def _conv_gate_bwd(a, conv_w, conv_b, dg, *, tc=256):
    rows, f2 = a.shape
    f = f2 // 2
    tc = _tile(f, tc)
    nc = f // tc
    ch = _tile(rows, CONV_ROWS, SUBLANES)
    n_chunks = rows // ch
    ext_rows = ch + SUBLANES

    def fold(x):
        return jnp.sum(x.reshape(ch // SUBLANES, SUBLANES, LANES), axis=0)

    def body(ag_ref, av_ref, wg_ref, wv_ref, bg_ref, bv_ref, dg_ref, da_ref, dw_ref, db_ref):
        for lt in range(tc // LANES):
            lanes = slice(lt * LANES, (lt + 1) * LANES)
            wg, wv, bg, bv = wg_ref[:, lanes], wv_ref[:, lanes], bg_ref[:, lanes], bv_ref[:, lanes]

            def chunk(ci, acc):
                r0 = pl.multiple_of(ci * ch, ch)
                taps_g = _taps(_with_halo(ag_ref, r0, ci, n_chunks, ch, lanes, True, True))
                taps_v = _taps(_with_halo(av_ref, r0, ci, n_chunks, ch, lanes, True, True))
                dge = _with_halo(dg_ref, r0, ci, n_chunks, ch, lanes, False, True)
                gate, val = _conv3(taps_g, wg, bg), _conv3(taps_v, wv, bv)
                sg = jax.nn.sigmoid(gate)
                d_gate = dge * val * sg * (1.0 + gate * (1.0 - sg))
                d_val = dge * gate * sg
                new = []
                for half, (taps, w, d) in enumerate(((taps_g, wg, d_gate), (taps_v, wv, d_val))):
                    da = (w[2:3, :] * d[:ch] + w[1:2, :] * pltpu.roll(d, ext_rows - 1, 0)[:ch]
                          + w[0:1, :] * pltpu.roll(d, ext_rows - 2, 0)[:ch])
                    da_ref[half, pl.ds(r0, ch), lanes] = da.astype(da_ref.dtype)
                    dc = d[:ch]
                    sums = [fold(dc)] + [fold(dc * t[:ch]) for t in taps]
                    new.append(tuple(x + s for x, s in zip(acc[half], sums)))
                return tuple(new)

            zero = tuple(jnp.zeros((SUBLANES, LANES), F32) for _ in range(4))
            acc = lax.fori_loop(0, n_chunks, chunk, (zero, zero))
            row = lax.broadcasted_iota(jnp.int32, (SUBLANES, LANES), 0)
            for half in range(2):
                db, *taps = (jnp.sum(x, axis=0, keepdims=True) for x in acc[half])
                db_ref[half, :, lanes] = db
                dw = jnp.zeros((SUBLANES, LANES), F32)
                for tap in range(3):
                    dw = jnp.where(row == tap, taps[tap], dw)
                dw_ref[half, :, lanes] = dw

    lo = lambda j: (0, j)
    hi = lambda j: (0, j + nc)
    both = lambda j: (0, 0, j)
    return pl.pallas_call(
        body, name="conv_gate_bwd", grid=(nc,),
        in_specs=[pl.BlockSpec((rows, tc), lo), pl.BlockSpec((rows, tc), hi), pl.BlockSpec((SUBLANES, tc), lo),
                  pl.BlockSpec((SUBLANES, tc), hi), pl.BlockSpec((1, tc), lo), pl.BlockSpec((1, tc), hi),
                  pl.BlockSpec((rows, tc), lo)],
        out_specs=[pl.BlockSpec((2, rows, tc), both), pl.BlockSpec((2, SUBLANES, tc), both), pl.BlockSpec((2, 1, tc), both)],
        out_shape=[jax.ShapeDtypeStruct((2, rows, f), BF16), jax.ShapeDtypeStruct((2, SUBLANES, f), F32),
                   jax.ShapeDtypeStruct((2, 1, f), F32)],
        compiler_params=_params(("parallel",)),
    )(a, a, conv_w, conv_w, conv_b, conv_b, dg)


def _wgrad(a, b, rows, cols, row_sharded, name, **kw):
    return functools.partial(_wgrad_half, a, b, rows, cols, row_sharded, name, **kw)


def _block_diag(x):
    nb, g, r, c = x.shape
    eye = jnp.eye(g, dtype=x.dtype)
    return (x[:, :, :, None, :] * eye[None, :, None, :, None]).reshape(nb, g * r, g * c)


def _block_diag_part(x, r, c):
    nb = x.shape[0]
    g = GROUPS_PER_BATCH
    eye = jnp.eye(g, dtype=x.dtype)
    return jnp.sum(x.reshape(nb, g, r, g, c) * eye[None, :, None, :, None], axis=3)


class _NoExchange:
    def __init__(self, later, ffn):
        self.later, self.ffn = later, ffn

    def mixer_weights(self, after):
        return self.later

    def ffn_weights_arrived(self, after):
        return None

    def ffn_weights(self, after):
        return self.ffn

    def ffn_grads(self, makers, after):
        self.ffn_makers = makers
        return None

    def ffn_backward_done(self, after):
        return None


def _local_step(x, posf, target, w, hooks):
    rows, d = x.shape
    width = w["ssm_d"].shape[1]
    qr, kvr = w["mla_q_norm_w"].shape[1], w["mla_kv_norm_w"].shape[1]
    heads = w["mla_w_ukv"].shape[1] // HEAD_SLOT
    f2 = w["ffn_conv_b"].shape[1]
    inp = w["w_in"].shape[0]
    groups = width // SSM_GROUP
    nb = groups // GROUPS_PER_BATCH
    scale = (QK_NOPE_DIM + QK_ROPE_DIM) ** -0.5
    g = {}

    hn = _rmsnorm_fwd(x, w["attn_norm_w"], name="attn_norm")
    proj = _matmul(hn, w["w_in"], mode="nt", name="in_proj")

    ar, ai, bbr, bbi = _s5_params(w["ssm_lambda_re"], w["ssm_lambda_im"], w["ssm_log_dt"], w["ssm_b_re"], w["ssm_b_im"])

    def b_band(bb):
        return _block_diag(bb.reshape(nb, GROUPS_PER_BATCH, SSM_STATE, SSM_GROUP).transpose(0, 1, 3, 2))

    def c_band(c):
        return _block_diag(c.reshape(nb, GROUPS_PER_BATCH, SSM_GROUP, SSM_STATE).transpose(0, 1, 3, 2))

    wb = jnp.concatenate([b_band(bbr), b_band(bbi)], axis=2).astype(BF16)
    wc = jnp.concatenate([c_band(w["ssm_c_re"]), -c_band(w["ssm_c_im"])], axis=1).astype(BF16)
    abar = jnp.concatenate([ar.reshape(nb, 1, STATE_PER_BATCH), ai.reshape(nb, 1, STATE_PER_BATCH)], axis=2)
    states, y_pre, yg = _s5_fwd(proj, wb, wc, w["ssm_d"], abar)
    later = hooks.mixer_weights(yg)
    z = _matmul(yg, later["ssm_w_glu"], mode="nn", name="glu_proj", bias=w["ssm_b_glu"])
    ys = _glu_norm_fwd(y_pre, z, w["ssm_out_norm_w"])

    q_col, kv_col, kpe_col = width // qr, (width + qr) // kvr, (width + qr + kvr) // LANES
    assert width % qr == 0 and (width + qr) % kvr == 0
    qn = _rmsnorm_fwd(proj, w["mla_q_norm_w"], name="q_norm", width=qr, col=q_col)
    kvn = _rmsnorm_fwd(proj, w["mla_kv_norm_w"], name="kv_norm", width=kvr, col=kv_col)
    q = _matmul(qn, w["mla_w_uq"], mode="nn", name="q_proj")
    kv = _matmul(kvn, w["mla_w_ukv"], mode="nn", name="kv_proj")
    half = QK_ROPE_DIM // 2
    inv_freq = ROPE_THETA ** (-jnp.arange(0, QK_ROPE_DIM, 2, dtype=F32) / QK_ROPE_DIM)
    zeros = jnp.zeros((LANES - QK_ROPE_DIM,), F32)
    freq = jnp.concatenate([inv_freq, inv_freq, zeros]).reshape(1, LANES)
    sign = jnp.concatenate([-jnp.ones((half,), F32), jnp.ones((half,), F32), zeros]).reshape(1, LANES)
    cos, sin = _rope_tables(posf, freq, sign)
    qc, kc, vb = _attn_prep(q, kv, proj, kpe_col, cos, sin)
    o, lse = _attn_fwd(qc, kc, vb, scale=scale, tq=ATTN_BLOCK)
    ym = _rmsnorm_fwd(o, w["mla_out_norm_w"], name="mla_out_norm")
    ycat = jnp.concatenate([ys, ym], axis=1)
    h1 = _matmul(ycat, later["w_out"], mode="nn", name="out_proj", add=x, after=hooks.ffn_weights_arrived(ycat))

    hn2 = _rmsnorm_fwd(h1, w["ffn_norm_w"], name="ffn_norm")
    ffn = hooks.ffn_weights(hn2)
    a = _matmul(hn2, ffn["ffn_w_up"], mode="nn", name="ffn_up", tm=FFN_ROWS)
    gated = _conv_gate_fwd(a, ffn["ffn_conv_w"], w["ffn_conv_b"])
    h2 = _matmul(gated, ffn["ffn_w_down"], mode="nn", name="ffn_down", add=h1, tk=2816, tm=FFN_ROWS)
    loss_tile, dh2, dh2_mxu, g["final_norm_w"] = _final_norm_loss(h2, w["final_norm_w"], target)

    dgated = _matmul(dh2_mxu, ffn["ffn_w_down"], mode="nt", name="ffn_down_dx", tm=FFN_ROWS)
    da, dcw, dcb = _conv_gate_bwd(a, ffn["ffn_conv_w"], w["ffn_conv_b"], dgated)
    g["ffn_conv_w"] = jnp.concatenate([dcw[0, :3], dcw[1, :3]], axis=1)
    g["ffn_conv_b"] = jnp.concatenate([dcb[0], dcb[1]], axis=1)
    started = hooks.ffn_grads({
        "ffn_w_up": _wgrad(hn2, da, d, f2, False, "ffn_up_dw", b_split=True, tn=_tile(f2 // N_CHIPS, 1408)),
        "ffn_w_down": _wgrad(gated, dh2_mxu, f2 // 2, d, True, "ffn_down_dw", tm=f2 // 2 // N_CHIPS, tn=512)}, dcb)
    dhn2 = _matmul(da, ffn["ffn_w_up"], mode="nt", name="ffn_up_dx", a_split=True, tk=_tile(f2 // 2, 2816), tm=FFN_ROWS,
                   after=started)
    dh1, dh1_mxu, g["ffn_norm_w"] = _rmsnorm_bwd(h1, w["ffn_norm_w"], dhn2, name="ffn_norm_bwd", add=dh2,
                                                dx_dtypes=(F32, BF16))

    dycat = _matmul(dh1_mxu, later["w_out"], mode="nt", name="out_proj_dx")
    g["w_out"] = _wgrad(ycat, dh1_mxu, 2 * width, d, True, "out_proj_dw")
    started = hooks.ffn_backward_done(dycat)
    mla_out_norm_w, ssm_out_norm_w = w["mla_out_norm_w"], w["ssm_out_norm_w"]
    if started is not None:
        mla_out_norm_w, ssm_out_norm_w = mla_out_norm_w + started[:1, :1], ssm_out_norm_w + started[:1, :1]

    do, g["mla_out_norm_w"] = _rmsnorm_bwd(o, mla_out_norm_w, dycat, name="mla_out_norm_bwd", width=width, dy_col=1)
    dq, dkv, dkpe_heads = _attn_bwd(qc, kc, vb, o, do, lse, cos, sin, scale=scale, tk=ATTN_BLOCK)
    dkpe = _kpe_bwd(dkpe_heads, cos, sin)
    g["mla_w_uq"] = _wgrad(qn, dq, qr, heads * HEAD_SLOT, False, "q_proj_dw")
    dqn = _matmul(dq, w["mla_w_uq"], mode="nt", name="q_proj_dx")
    dcq, g["mla_q_norm_w"] = _rmsnorm_bwd(proj, w["mla_q_norm_w"], dqn, name="q_norm_bwd", width=qr, col=q_col,
                                          dx_dtypes=(BF16,))
    g["mla_w_ukv"] = _wgrad(kvn, dkv, kvr, heads * HEAD_SLOT, False, "kv_proj_dw")
    dkvn = _matmul(dkv, w["mla_w_ukv"], mode="nt", name="kv_proj_dx")
    dckv, g["mla_kv_norm_w"] = _rmsnorm_bwd(proj, w["mla_kv_norm_w"], dkvn, name="kv_norm_bwd", width=kvr, col=kv_col,
                                            dx_dtypes=(BF16,))

    dz, dyg_a, g["ssm_out_norm_w"], g["ssm_b_glu"] = _glu_norm_bwd(y_pre, z, ssm_out_norm_w, dycat)
    dyg_b = _matmul(dz, later["ssm_w_glu"], mode="nt", name="glu_proj_dx")
    g["ssm_w_glu"] = _wgrad(yg, dz, width, width, True, "glu_proj_dw")
    du, dwb, dwc, dabar, g["ssm_d"] = _s5_bwd(proj, states, y_pre, dyg_a, dyg_b, wb, wc, w["ssm_d"], abar)

    def b_unband(x):
        return _block_diag_part(x, SSM_GROUP, SSM_STATE).transpose(0, 1, 3, 2).reshape(groups, SSM_STATE * SSM_GROUP)

    def c_unband(x):
        return _block_diag_part(x, SSM_STATE, SSM_GROUP).transpose(0, 1, 3, 2).reshape(groups, SSM_GROUP, SSM_STATE)

    st = STATE_PER_BATCH
    g["ssm_c_re"] = c_unband(dwc[:, :st, :])
    g["ssm_c_im"] = -c_unband(dwc[:, st:, :])
    d_ar = dabar[:, 0, :st].reshape(groups, SSM_STATE)
    d_ai = dabar[:, 0, st:].reshape(groups, SSM_STATE)
    (g["ssm_lambda_re"], g["ssm_lambda_im"], g["ssm_log_dt"], g["ssm_b_re"], g["ssm_b_im"]) = _s5_params_bwd(
        w["ssm_lambda_re"], w["ssm_lambda_im"], w["ssm_log_dt"], w["ssm_b_re"], w["ssm_b_im"], d_ar, d_ai,
        b_unband(dwb[:, :, :st]), b_unband(dwb[:, :, st:]))

    pad = jnp.zeros((rows, inp - (width + qr + kvr + LANES)), BF16)
    dproj = jnp.concatenate([du, dcq, dckv, dkpe, pad], axis=1)
    g["w_in"] = _wgrad(dproj, hn, inp, d, False, "in_proj_dw")
    dhn = _matmul(dproj, w["w_in"], mode="nn", name="in_proj_dx")
    dx, g["attn_norm_w"] = _rmsnorm_bwd(x, w["attn_norm_w"], dhn, name="attn_norm_bwd", add=dh1)
    return loss_tile, dx, g


ANY = pl.BlockSpec(memory_space=pl.ANY)
MESH = pl.DeviceIdType.MESH


def _mesh_pos():
    return lax.axis_index("x"), lax.axis_index("y"), lax.axis_index("c")


def _other_chips(x, y):
    return [(1 - x, y), (x, 1 - y), (1 - x, 1 - y)]


def _remote(src, dst, send_sems, recv_sems, k, to):
    return pltpu.make_async_remote_copy(src_ref=src, dst_ref=dst, send_sem=send_sems.at[k], recv_sem=recv_sems.at[k],
                                        device_id=to, device_id_type=MESH)


def _place_shard(shard, piece_idx, row_sharded, name, out_dtype=BF16, pieces=N_CHIPS):
    rs, cs = shard.shape
    tr = _tile(rs, 256, 2 * SUBLANES)
    rb = rs // tr

    def body(p_ref, x_ref, o_ref):
        o_ref[...] = x_ref[...].astype(o_ref.dtype)

    if row_sharded:
        out_shape, out_map = (pieces * rs, cs), (lambda i, p_ref: (p_ref[0] * rb + i, 0))
    else:
        out_shape, out_map = (rs, pieces * cs), (lambda i, p_ref: (i, p_ref[0]))
    return pl.pallas_call(
        body, name=name, out_shape=jax.ShapeDtypeStruct(out_shape, out_dtype),
        grid_spec=pltpu.PrefetchScalarGridSpec(
            num_scalar_prefetch=1, grid=(rb,), in_specs=[pl.BlockSpec((tr, cs), lambda i, p_ref: (i, 0))],
            out_specs=pl.BlockSpec((tr, cs), out_map)),
        compiler_params=_params(("parallel",)),
    )(piece_idx, shard)


def _gather_weights(placed, name):
    n = len(placed)
    meta = [(row_sharded, direct) for _, row_sharded, direct in placed]
    over_ici, over_d2d = _gather_plans(meta)
    forwarded = [t for t, (_, direct) in enumerate(meta) if not direct]

    def body(*refs):
        outs = refs[n:2 * n]
        send_sems, recv_sems, pass_send_sems, pass_recv_sems = refs[2 * n:]
        first, arrivals = over_ici(outs, send_sems, recv_sems)
        passed, passed_arrivals = over_d2d([outs[t] for t in forwarded], pass_send_sems, pass_recv_sems)
        for cp in first:
            cp.start()
        for t in range(n):
            for j in range(3):
                arrivals[3 * t + j].wait_recv()
                if t in forwarded:
                    passed[3 * forwarded.index(t) + j].start()
        for cp in passed_arrivals:
            cp.wait_recv()
        for cp in first + passed:
            cp.wait_send()

    return pl.pallas_call(
        body, name=name, in_specs=[ANY] * n, out_specs=[ANY] * n,
        out_shape=[jax.ShapeDtypeStruct(arr.shape, arr.dtype) for arr, _, _ in placed],
        input_output_aliases={t: t for t in range(n)},
        scratch_shapes=[pltpu.SemaphoreType.DMA((3 * n,)), pltpu.SemaphoreType.DMA((3 * n,)),
                        pltpu.SemaphoreType.DMA((3 * len(forwarded),)), pltpu.SemaphoreType.DMA((3 * len(forwarded),))],
    )(*[arr for arr, _, _ in placed])


def _gather_plans(meta):
    def window(ref, row_sharded, piece, half):
        r, cc = ref.shape
        if row_sharded:
            rs = r // N_CHIPS
            if half is None:
                return ref.at[pl.ds(piece * rs, rs), :]
            return ref.at[pl.ds(piece * rs + half * (rs // 2), rs // 2), :]
        cs = cc // N_CHIPS
        if half is None:
            return ref.at[:, pl.ds(piece * cs, cs)]
        return ref.at[pl.ds(half * (r // 2), r // 2), pl.ds(piece * cs, cs)]

    def over_ici(refs, send_sems, recv_sems):
        x, y, c = _mesh_pos()
        sends, recvs = [], []
        for t, (row_sharded, direct) in enumerate(meta):
            mine = window(refs[t], row_sharded, 2 * x + y, None if direct else c)
            for j, (px, py) in enumerate(_other_chips(x, y)):
                theirs = window(refs[t], row_sharded, 2 * px + py, None if direct else c)
                sends.append(_remote(mine, mine, send_sems, recv_sems, 3 * t + j, (px, py, c)))
                recvs.append(_remote(theirs, theirs, send_sems, recv_sems, 3 * t + j, (px, py, c)))
        return sends, recvs

    def over_d2d(refs, send_sems, recv_sems):
        x, y, c = _mesh_pos()
        sends, recvs = [], []
        rows = [row_sharded for row_sharded, direct in meta if not direct]
        for t, row_sharded in enumerate(rows):
            for j, (px, py) in enumerate(_other_chips(x, y)):
                got = window(refs[t], row_sharded, 2 * px + py, c)
                other = window(refs[t], row_sharded, 2 * px + py, 1 - c)
                sends.append(_remote(got, got, send_sems, recv_sems, 3 * t + j, (x, y, 1 - c)))
                recvs.append(_remote(other, other, send_sems, recv_sems, 3 * t + j, (x, y, 1 - c)))
        return sends, recvs

    return over_ici, over_d2d


HBM = pl.BlockSpec(memory_space=pltpu.HBM)
SEMAPHORES = pl.BlockSpec(memory_space=pltpu.SEMAPHORE)
DATAFLOW = pltpu.SideEffectType.DATAFLOW_SIDE_EFFECTING


def _start_copies(name, arrays, plan, n_copies, after):
    n = len(arrays)

    def body(*refs):
        sends, _ = plan(refs[:n], refs[n + 1], refs[n + 2])
        for cp in sends:
            cp.start()
        token = refs[2 * n + 3]
        token[...] = jnp.zeros_like(token)

    out = pl.pallas_call(
        body, name=name,
        out_shape=(pltpu.SemaphoreType.DMA((n_copies,)), pltpu.SemaphoreType.DMA((n_copies,)),
                   *[pltpu.HBM(a.shape, a.dtype) for a in arrays], jax.ShapeDtypeStruct((SUBLANES, LANES), F32)),
        in_specs=[HBM] * n + [ANY],
        out_specs=(SEMAPHORES, SEMAPHORES, *[HBM] * n, pl.BlockSpec(memory_space=pltpu.VMEM)),
        input_output_aliases={t: t + 2 for t in range(n)},
        compiler_params=pltpu.CompilerParams(has_side_effects=DATAFLOW),
    )(*[pltpu.with_memory_space_constraint(a, pltpu.HBM) for a in arrays], after)
    return out[0], out[1], list(out[2:2 + n]), out[2 + n]


def _wait_copies(name, started, plan, after):
    send_sems, recv_sems, arrays, _ = started
    n = len(arrays)

    def body(*refs):
        sends, recvs = plan(refs[:n], refs[n], refs[n + 1])
        for cp in sends:
            cp.wait_send()
        for cp in recvs:
            cp.wait_recv()

    out = pl.pallas_call(
        body, name=name, out_shape=[pltpu.HBM(a.shape, a.dtype) for a in arrays],
        in_specs=[HBM] * n + [SEMAPHORES, SEMAPHORES, ANY], out_specs=[HBM] * n,
        input_output_aliases={t: t for t in range(n)},
        compiler_params=pltpu.CompilerParams(has_side_effects=DATAFLOW),
    )(*arrays, send_sems, recv_sems, after)
    return list(out)


def _exchange(name, arrays, out_shapes, plan, n_copies, in_place=False, after=None):
    n = len(arrays)
    extra = [] if after is None else [after]

    def body(*refs):
        ins, outs = refs[:n], refs[n + len(extra):n + len(extra) + len(out_shapes)]
        send_sems, recv_sems = refs[n + len(extra) + len(out_shapes):]
        sends, recvs = plan(ins, outs, send_sems, recv_sems)
        for cp in sends:
            cp.start()
        for cp in recvs:
            cp.wait_recv()
        for cp in sends:
            cp.wait_send()

    return pl.pallas_call(
        body, name=name, in_specs=[ANY] * (n + len(extra)), out_specs=[ANY] * len(out_shapes), out_shape=out_shapes,
        input_output_aliases={t: t for t in range(n)} if in_place else {},
        scratch_shapes=[pltpu.SemaphoreType.DMA((n_copies,)), pltpu.SemaphoreType.DMA((n_copies,))],
    )(*arrays, *extra)


def _give_plan(n):
    def plan(refs, send_sems, recv_sems):
        x, y, c = _mesh_pos()
        sends = [_remote(refs[t], refs[n + t], send_sems, recv_sems, t, (x, y, 1 - c)) for t in range(n)]
        return sends, sends

    return plan


def _scatter_plan(n):
    def plan(refs, send_sems, recv_sems):
        x, y, c = _mesh_pos()
        sends = []
        for t in range(n):
            for j, (px, py) in enumerate(_other_chips(x, y)):
                sends.append(_remote(refs[t].at[2 * px + py], refs[n + t].at[j], send_sems, recv_sems, 3 * t + j, (px, py, c)))
        return sends, sends

    return plan


def _scatter_shapes(sums):
    return [jax.ShapeDtypeStruct((3,) + s.shape[1:], s.dtype) for s in sums]


def _join_halves(halves, name, after=None):
    def plan(ins, outs, send_sems, recv_sems):
        x, y, c = _mesh_pos()
        sends = [_remote(outs[t].at[c], outs[t].at[c], send_sems, recv_sems, t, (x, y, 1 - c)) for t in range(len(ins))]
        recvs = [_remote(outs[t].at[1 - c], outs[t].at[1 - c], send_sems, recv_sems, t, (x, y, 1 - c))
                 for t in range(len(ins))]
        return sends, recvs

    shapes = [jax.ShapeDtypeStruct(h.shape, h.dtype) for h in halves]
    return _exchange(name, halves, shapes, plan, len(halves), in_place=True, after=after)


def _add_other_half(g4, got, where, name, wire_dtype=BF16):
    _, pieces, sr, sc = g4.shape
    tr = _tile(sr, 256, 2 * SUBLANES)

    def body(w_ref, a_ref, b_ref, o_ref):
        o_ref[...] = (a_ref[...] + b_ref[...]).astype(o_ref.dtype)

    blk = pl.BlockSpec((None, tr, sc), lambda p, i, w_ref: (p, i, 0))
    return pl.pallas_call(
        body, name=name, out_shape=jax.ShapeDtypeStruct((pieces, sr, sc), wire_dtype),
        grid_spec=pltpu.PrefetchScalarGridSpec(
            num_scalar_prefetch=1, grid=(pieces, sr // tr),
            in_specs=[pl.BlockSpec((None, None, tr, sc), lambda p, i, w_ref: (w_ref[0], p, i, 0)), blk], out_specs=blk),
        compiler_params=_params(("parallel", "parallel")),
    )(where, g4, got)


def _add_pieces(sums, got_pieces, where, name):
    _, sr, sc = sums.shape
    tr = _tile(sr, 256, 2 * SUBLANES)

    def body(w_ref, a_ref, r_ref, o_ref):
        acc = a_ref[...]
        for j in range(3):
            acc = acc + r_ref[j].astype(F32)
        o_ref[...] = acc

    return pl.pallas_call(
        body, name=name, out_shape=jax.ShapeDtypeStruct((N_CORES, sr, sc), F32),
        grid_spec=pltpu.PrefetchScalarGridSpec(
            num_scalar_prefetch=1, grid=(sr // tr,),
            in_specs=[pl.BlockSpec((None, tr, sc), lambda i, w_ref: (w_ref[1], i, 0)),
                      pl.BlockSpec((3, tr, sc), lambda i, w_ref: (0, i, 0))],
            out_specs=pl.BlockSpec((None, tr, sc), lambda i, w_ref: (w_ref[0], i, 0))),
        compiler_params=_params(("parallel",)),
    )(where, sums, got_pieces)


def _adamw_update(w, g, m, v):
    nm = ADAM_B1 * m + (1.0 - ADAM_B1) * g
    nv = ADAM_B2 * v + (1.0 - ADAM_B2) * (g * g)
    m_hat = nm / (1.0 - ADAM_B1 ** ADAM_STEP)
    v_hat = nv / (1.0 - ADAM_B2 ** ADAM_STEP)
    return -ADAM_LR * (m_hat / (jnp.sqrt(v_hat) + ADAM_EPS) + ADAM_WD * w), nm, nv


def _adamw(w, g, m, v, name, after=None):
    rows, cols = w.shape
    halves = 2 if g.ndim == 3 else 1
    bc = cols // halves
    tr = _tile(rows, max(SUBLANES, (1 << 19) // max(bc, 1) // SUBLANES * SUBLANES), SUBLANES)

    def body(w_ref, g_ref, m_ref, v_ref, *rest):
        d_ref, nm_ref, nv_ref, go_ref = rest[-4:]
        gv = g_ref[...]
        d_ref[...], nm_ref[...], nv_ref[...] = _adamw_update(w_ref[...], gv, m_ref[...], v_ref[...])
        go_ref[...] = gv

    blk = pl.BlockSpec((tr, bc), lambda i, h: (i, h))
    g_blk = pl.BlockSpec((None, tr, bc), lambda i, h: (h, i, 0)) if halves == 2 else blk
    extra = [] if after is None else [after]
    return pl.pallas_call(
        body, name=name, grid=(rows // tr, halves),
        in_specs=[blk, g_blk, blk, blk] + [pl.BlockSpec(memory_space=pl.ANY)] * len(extra), out_specs=[blk] * 4,
        out_shape=[jax.ShapeDtypeStruct((rows, cols), F32)] * 4, compiler_params=_params(("parallel", "parallel")),
    )(w, g, m, v, *extra)


def _adamw_many(ws, gs, ms, vs, name):
    n = len(ws)

    def body(*refs):
        outs = refs[4 * n:]
        for k in range(n):
            w_ref, g_ref, m_ref, v_ref = (refs[j * n + k] for j in range(4))
            outs[k][...], outs[n + k][...], outs[2 * n + k][...] = _adamw_update(w_ref[...], g_ref[...], m_ref[...], v_ref[...])

    out = pl.pallas_call(
        body, name=name, out_shape=[jax.ShapeDtypeStruct(w.shape, F32) for w in ws] * 3,
        compiler_params=pltpu.CompilerParams(vmem_limit_bytes=VMEM_LIMIT_BYTES),
    )(*ws, *gs, *ms, *vs)
    return out[:n], out[n:2 * n], out[2 * n:]


WEIGHTS = ['attn_norm_w', 'w_in', 'ssm_lambda_re', 'ssm_lambda_im', 'ssm_log_dt', 'ssm_b_re', 'ssm_b_im', 'ssm_c_re',
           'ssm_c_im', 'ssm_d', 'ssm_w_glu', 'ssm_b_glu', 'mla_q_norm_w', 'mla_w_uq', 'mla_kv_norm_w', 'mla_w_ukv',
           'ssm_out_norm_w', 'mla_out_norm_w', 'w_out', 'ffn_norm_w', 'ffn_w_up', 'ffn_conv_w', 'ffn_conv_b',
           'ffn_w_down', 'final_norm_w']
SHARDED = {'w_in': False, 'ssm_w_glu': True, 'mla_w_uq': False, 'mla_w_ukv': False, 'w_out': True, 'ffn_w_up': False,
           'ffn_w_down': True}
SMALL = [n for n in WEIGHTS if n not in SHARDED and n != 'ffn_conv_w']
ROPE_PAD = HEAD_SLOT - QK_NOPE_DIM - QK_ROPE_DIM
SMALL_COLS = 8 * LANES


def _pad_heads(w_uq, heads):
    qr = w_uq.shape[0]
    w3 = w_uq.reshape(qr, heads, QK_NOPE_DIM + QK_ROPE_DIM)
    return jnp.concatenate([w3, jnp.zeros((qr, heads, ROPE_PAD), w_uq.dtype)], axis=2).reshape(qr, heads * HEAD_SLOT)


def _unpad_heads(g_uq, heads):
    qr = g_uq.shape[0]
    return g_uq.reshape(qr, heads, HEAD_SLOT)[:, :, :QK_NOPE_DIM + QK_ROPE_DIM].reshape(qr, -1)


FFN = ['ffn_w_up', 'ffn_w_down']
MIXER_LATER = ['ssm_w_glu', 'w_out']
FFN_GATHER = FFN + ['ffn_conv_w']
FFN_GATHER_META = [(SHARDED[n], False) for n in FFN] + [(False, True)]


class _Overlapped:
    def __init__(self, placed_later, placed, where, after):
        self.where, self.mine, self.other = where, where[:1], 1 - where[:1]
        self.later_ici, self.later_d2d = _gather_plans([(SHARDED[n], False) for n in MIXER_LATER])
        self.later = _start_copies("gather_later_start", placed_later, self.later_ici, 3 * len(placed_later), after)
        self.over_ici, self.over_d2d = _gather_plans(FFN_GATHER_META)
        self.gather = _start_copies("gather_ffn_start", placed, self.over_ici, 3 * len(placed), self.later[3])
        self.gather_started = self.gather[3]

    def mixer_weights(self, after):
        arrived = _wait_copies("gather_later_wait", self.later, self.later_ici, after)
        shapes = [jax.ShapeDtypeStruct(a.shape, a.dtype) for a in arrived]
        passed = _exchange("gather_later_pass", arrived, shapes, lambda ins, outs, s, r: self.later_d2d(outs, s, r),
                           3 * len(arrived), in_place=True)
        return dict(zip(MIXER_LATER, passed))

    def ffn_weights_arrived(self, after):
        arrived = _wait_copies("gather_ffn_wait", self.gather, self.over_ici, after)
        n = len(FFN)
        self.direct = arrived[n:]
        self.passing = _start_copies("gather_ffn_pass_start", arrived[:n], self.over_d2d, 3 * n, after)
        return self.passing[3]

    def ffn_weights(self, after):
        passed = _wait_copies("gather_ffn_pass_wait", self.passing, self.over_d2d, after)
        return dict(zip(FFN_GATHER, passed + self.direct))

    def ffn_grads(self, makers, after):
        self.makers = [makers[name] for name in FFN]
        n = len(FFN)
        give = [make(self.other, suffix="_give") for make in self.makers]
        lands = [lax.empty(g.shape, g.dtype) for g in give]
        self.swap = _start_copies("grad_ffn_swap_start", give + lands, _give_plan(n), n, after)
        return self.swap[3]

    def ffn_backward_done(self, after):
        n = len(FFN)
        got = _wait_copies("grad_ffn_swap_wait", self.swap, _give_plan(n), after)[n:]
        kept = [make(self.mine, suffix="_keep", add=got[t], wire=True) for t, make in enumerate(self.makers)]
        self.sums = [k[0] for k in kept]
        wires = [k[1] for k in kept]
        lands = [lax.empty(s.shape, s.dtype) for s in _scatter_shapes(wires)]
        self.scatter = _start_copies("grad_ffn_scatter_start", wires + lands, _scatter_plan(n), 3 * n, after)
        return self.scatter[3]

    def ffn_reduced(self, after):
        n = len(FFN)
        got_pieces = _wait_copies("grad_ffn_scatter_wait", self.scatter, _scatter_plan(n), after)[n:]
        return [_add_pieces(self.sums[t], got_pieces[t], self.where, "grad_add_pieces_" + name) for t, name in enumerate(FFN)]


def _step(args):
    x, positions, target = args["x"][0], args["positions"], args["loss_target"][0]
    rows = x.shape[0]
    p = {n: args[n] for n in WEIGHTS}
    xi, yi, ci = _mesh_pos()
    piece = 2 * xi + yi

    def transposed(a):
        return jnp.swapaxes(a[0], 0, 1)

    w_in = transposed(p["w_in"])
    in_width = w_in.shape[0]
    in_pad = (-in_width) % (2 * LANES)
    heads_here = p["mla_w_uq"].shape[2] // (QK_NOPE_DIM + QK_ROPE_DIM)
    shards = {
        "w_in": jnp.pad(w_in, ((0, in_pad), (0, 0))),
        "ssm_w_glu": p["ssm_w_glu"][0],
        "mla_w_uq": _pad_heads(p["mla_w_uq"][0], heads_here),
        "mla_w_ukv": p["mla_w_ukv"][0],
        "w_out": p["w_out"][0],
        "ffn_w_up": p["ffn_w_up"][0],
        "ffn_w_down": p["ffn_w_down"][0],
    }
    conv_w = jnp.pad(p["ffn_conv_w"][0], ((0, SUBLANES - p["ffn_conv_w"].shape[1]), (0, 0)))
    order = list(SHARDED)
    piece_idx = piece.reshape(1).astype(jnp.int32)
    placed = {n: _place_shard(shards[n], piece_idx, SHARDED[n], "place_" + n) for n in order}
    placed["ffn_conv_w"] = _place_shard(conv_w, piece_idx, False, "place_ffn_conv_w", out_dtype=F32)
    mixer = [n for n in order if n not in FFN]
    first = [n for n in mixer if n not in MIXER_LATER]
    w = dict(zip(first, _gather_weights([(placed[n], SHARDED[n], False) for n in first], "gather_first_weights")))
    where = jnp.stack([ci, piece]).astype(jnp.int32)
    hooks = _Overlapped([placed[n] for n in MIXER_LATER], [placed[n] for n in FFN_GATHER], where, after=w["w_in"])
    groups = p["ssm_lambda_re"].shape[1]
    w.update({
        "attn_norm_w": p["attn_norm_w"] + hooks.gather_started[:1, :1],
        "ssm_lambda_re": p["ssm_lambda_re"][0], "ssm_lambda_im": p["ssm_lambda_im"][0],
        "ssm_log_dt": p["ssm_log_dt"].reshape(groups, 1), "ssm_b_re": p["ssm_b_re"].reshape(groups, -1),
        "ssm_b_im": p["ssm_b_im"].reshape(groups, -1), "ssm_c_re": p["ssm_c_re"][0], "ssm_c_im": p["ssm_c_im"][0],
        "ssm_d": p["ssm_d"], "ssm_b_glu": p["ssm_b_glu"], "mla_q_norm_w": p["mla_q_norm_w"],
        "mla_kv_norm_w": p["mla_kv_norm_w"], "ssm_out_norm_w": p["ssm_out_norm_w"], "mla_out_norm_w": p["mla_out_norm_w"],
        "ffn_norm_w": p["ffn_norm_w"], "ffn_conv_b": p["ffn_conv_b"], "final_norm_w": p["final_norm_w"].reshape(1, -1),
    })

    loss_tile, dx, g = _local_step(x, positions.reshape(rows, 1).astype(F32), target, w, hooks)
    loss = lax.psum(loss_tile[0, 0], ("x", "y", "c"))

    flat = [g[n].reshape(-1) for n in SMALL] + [g["ffn_conv_w"].reshape(-1)]
    sizes = [f.shape[0] for f in flat]
    per_block = -(-sum(sizes) // (N_CORES * N_CHIPS * SMALL_COLS))
    small_rows = -(-per_block // (2 * SUBLANES)) * (2 * SUBLANES)
    padded = N_CORES * N_CHIPS * small_rows * SMALL_COLS

    def pack(parts):
        parts = list(parts)
        have = sum(q.shape[0] for q in parts)
        return jnp.concatenate(parts + [jnp.zeros((padded - have,), F32)])

    reduced = mixer + ["small"]
    small = pack(flat).reshape(N_CORES, N_CHIPS, small_rows, SMALL_COLS)
    give = [g[n](hooks.other, suffix="_give") for n in mixer] + [lax.dynamic_index_in_dim(small, 1 - ci, 0, keepdims=False)]
    lands = [lax.empty(a.shape, a.dtype) for a in give]
    give_plan = _give_plan(len(reduced))
    swap = _start_copies("grad_mixer_swap_start", give + lands, give_plan, len(reduced), dx)

    grads, delta, new_m, new_v = {}, {}, {}, {}

    def finish(n, joined, after=None):
        grad = joined if SHARDED[n] else joined.reshape(-1, joined.shape[2])
        if n == "w_in":
            wt, mt, vt = w_in, transposed(args["m_w_in"]), transposed(args["v_w_in"])
            out = _adamw(wt, grad, mt, vt, "adamw_w_in")
            delta[n], new_m[n], new_v[n], grads[n] = (jnp.swapaxes(a, 0, 1)[None] for a in out)
            return
        if n == "mla_w_uq":
            grad = _unpad_heads(grad, heads_here)
        adam(n, grad, after)

    def adam(n, grad, after=None):
        shape = p[n].shape
        out = _adamw(p[n].reshape(shape[1:]), grad, args["m_" + n].reshape(shape[1:]),
                     args["v_" + n].reshape(shape[1:]), "adamw_" + n, after)
        delta[n], new_m[n], new_v[n], grads[n] = (a.reshape(shape) for a in out)

    ffn_joined = _join_halves(hooks.ffn_reduced(dx), "grad_ffn_join_halves", after=swap[3])
    got = _wait_copies("grad_mixer_swap_wait", swap, give_plan, ffn_joined[0])[len(reduced):]
    kept = [g[n](hooks.mine, suffix="_keep", add=got[t], wire=True) for t, n in enumerate(mixer)]
    small_sum = _add_other_half(small, got[-1], where, "grad_add_half_small", F32)
    sums = [k[0] for k in kept] + [small_sum]
    wires = [k[1] for k in kept] + [small_sum]
    lands = [lax.empty(s.shape, s.dtype) for s in _scatter_shapes(wires)]
    scatter_plan = _scatter_plan(len(reduced))
    scatter = _start_copies("grad_mixer_scatter_start", wires + lands, scatter_plan, 3 * len(reduced), kept[0][0])
    behind = scatter[3]
    for n, joined in zip(FFN, ffn_joined):
        finish(n, joined, after=behind)
        behind = delta[n]
    got_pieces = _wait_copies("grad_mixer_scatter_wait", scatter, scatter_plan, delta[FFN[-1]])[len(reduced):]
    halves = [_add_pieces(sums[t], got_pieces[t], where, "grad_add_pieces_" + n) for t, n in enumerate(reduced)]
    joined = _join_halves(halves, "grad_join_halves")
    for n, j in zip(mixer, joined):
        finish(n, j)
    eighths = _place_shard(joined[-1].reshape(N_CORES * small_rows, SMALL_COLS), piece_idx, True, "place_small_grads",
                           out_dtype=F32)
    small_sum = _gather_weights([(eighths, True, False)], "gather_small_grads")[0]
    flat_sum = small_sum.reshape(N_CHIPS, N_CORES, small_rows * SMALL_COLS).transpose(1, 0, 2).reshape(-1)
    offs = [0]
    for s in sizes:
        offs.append(offs[-1] + s)
    for k, n in enumerate(SMALL):
        grads[n] = flat_sum[offs[k]:offs[k + 1]].reshape(p[n].shape)
    taps, cols_here = p["ffn_conv_w"].shape[1], p["ffn_conv_w"].shape[2]
    conv_full = flat_sum[offs[len(SMALL)]:offs[len(SMALL) + 1]].reshape(taps, N_CHIPS * cols_here)
    adam("ffn_conv_w", lax.dynamic_slice_in_dim(conv_full, piece * cols_here, cols_here, axis=1))

    def rank2(a):
        return a.reshape(1, -1) if a.ndim == 1 else a

    d_s, m_s, v_s = _adamw_many([rank2(p[n]) for n in SMALL], [rank2(grads[n]) for n in SMALL],
                                [rank2(args["m_" + n]) for n in SMALL], [rank2(args["v_" + n]) for n in SMALL], "adamw_small")
    for k, n in enumerate(SMALL):
        delta[n], new_m[n], new_v[n] = (a.reshape(p[n].shape) for a in (d_s[k], m_s[k], v_s[k]))

    return (loss, dx[None], *[grads[n] for n in WEIGHTS], *[delta[n] for n in WEIGHTS],
            *[new_m[n] for n in WEIGHTS], *[new_v[n] for n in WEIGHTS])


def kernel(x, positions, attn_norm_w, w_in, ssm_lambda_re, ssm_lambda_im, ssm_log_dt, ssm_b_re, ssm_b_im, ssm_c_re, ssm_c_im, ssm_d, ssm_w_glu, ssm_b_glu, mla_q_norm_w, mla_w_uq, mla_kv_norm_w, mla_w_ukv, ssm_out_norm_w, mla_out_norm_w, w_out, ffn_norm_w, ffn_w_up, ffn_conv_w, ffn_conv_b, ffn_w_down, final_norm_w, loss_target, m_attn_norm_w, m_w_in, m_ssm_lambda_re, m_ssm_lambda_im, m_ssm_log_dt, m_ssm_b_re, m_ssm_b_im, m_ssm_c_re, m_ssm_c_im, m_ssm_d, m_ssm_w_glu, m_ssm_b_glu, m_mla_q_norm_w, m_mla_w_uq, m_mla_kv_norm_w, m_mla_w_ukv, m_ssm_out_norm_w, m_mla_out_norm_w, m_w_out, m_ffn_norm_w, m_ffn_w_up, m_ffn_conv_w, m_ffn_conv_b, m_ffn_w_down, m_final_norm_w, v_attn_norm_w, v_w_in, v_ssm_lambda_re, v_ssm_lambda_im, v_ssm_log_dt, v_ssm_b_re, v_ssm_b_im, v_ssm_c_re, v_ssm_c_im, v_ssm_d, v_ssm_w_glu, v_ssm_b_glu, v_mla_q_norm_w, v_mla_w_uq, v_mla_kv_norm_w, v_mla_w_ukv, v_ssm_out_norm_w, v_mla_out_norm_w, v_w_out, v_ffn_norm_w, v_ffn_w_up, v_ffn_conv_w, v_ffn_conv_b, v_ffn_w_down, v_final_norm_w):
    return _step(dict(locals()))
```

```python
import functools
import math

import jax
import jax.numpy as jnp
from jax import lax
from jax.experimental import pallas as pl
from jax.experimental.pallas import tpu as pltpu

F32 = jnp.float32
BF16 = jnp.bfloat16

SSM_GROUP = 16
SSM_STATE = 64
QK_NOPE_DIM = 128
QK_ROPE_DIM = 64
V_HEAD_DIM = 128
ROPE_THETA = 10000.0
RMS_EPS = 1e-6
ADAM_LR, ADAM_B1, ADAM_B2, ADAM_EPS, ADAM_WD, ADAM_STEP = 0.001, 0.9, 0.999, 1e-08, 0.01, 10

LANES = 128
SUBLANES = 8
VMEM_LIMIT_BYTES = 56 * 1024 * 1024

GROUPS_PER_BATCH = LANES // SSM_GROUP
STATE_PER_BATCH = GROUPS_PER_BATCH * SSM_STATE
HEAD_SLOT = 2 * LANES
NEG_INF = -1e30
ATTN_BLOCK = 512
FFN_ROWS = 1024

N_CHIPS = 4
N_CORES = 2


def _tile(n, pref, align=LANES):
    if n <= pref:
        return n
    t = (pref // align) * align
    while t >= align:
        if n % t == 0:
            return t
        t -= align
    return n


def _params(sem):
    return pltpu.CompilerParams(dimension_semantics=sem, vmem_limit_bytes=VMEM_LIMIT_BYTES)


def _dot(a, b, dims):
    return lax.dot_general(a, b, (dims, ((), ())), preferred_element_type=F32)


def _dot_nn(a, b):
    return _dot(a, b, ((1,), (0,)))


def _dot_nt(a, b):
    return _dot(a, b, ((1,), (1,)))


def _dot_tn(a, b):
    return _dot(a, b, ((0,), (0,)))


def _matmul(a, b, *, mode, name, tm=512, tn=1024, tk=2048, bias=None, add=None, out_dtype=F32,
            out_blocks=None, a_split=False, b_split=False, after=None):
    if a_split:
        assert mode == "nt"
        a_shape = (a.shape[1], 2 * a.shape[2])
    else:
        a_shape = a.shape
    if b_split:
        assert mode == "tn"
        b_shape = (b.shape[1], 2 * b.shape[2])
    else:
        b_shape = b.shape
    if mode == "nn":
        (m, k), (k2, n) = a_shape, b_shape
    elif mode == "nt":
        (m, k), (n, k2) = a_shape, b_shape
    else:
        (k, m), (k2, n) = a_shape, b_shape
    assert k == k2, (a.shape, b.shape, mode)
    tm, tn, tk = _tile(m, tm, SUBLANES), _tile(n, tn), _tile(k, tk)
    nk = k // tk
    a_spec = {"nn": pl.BlockSpec((tm, tk), lambda i, j, kk: (i, kk)),
              "nt": pl.BlockSpec((tm, tk), lambda i, j, kk: (i, kk)),
              "tn": pl.BlockSpec((tk, tm), lambda i, j, kk: (kk, i))}[mode]
    b_spec = {"nn": pl.BlockSpec((tk, tn), lambda i, j, kk: (kk, j)),
              "nt": pl.BlockSpec((tn, tk), lambda i, j, kk: (j, kk)),
              "tn": pl.BlockSpec((tk, tn), lambda i, j, kk: (kk, j))}[mode]
    if a_split:
        kb = a.shape[2] // tk
        assert a.shape[2] % tk == 0
        a_spec = pl.BlockSpec((None, tm, tk), lambda i, j, kk: (kk // kb, i, kk % kb))
    if b_split:
        nb = b.shape[2] // tn
        assert b.shape[2] % tn == 0
        b_spec = pl.BlockSpec((None, tk, tn), lambda i, j, kk: (j // nb, kk, j % nb))
    dot = {"nn": _dot_nn, "nt": _dot_nt, "tn": _dot_tn}[mode]
    in_specs, operands = [a_spec, b_spec], [a, b]
    if bias is not None:
        in_specs.append(pl.BlockSpec((1, tn), lambda i, j, kk: (0, j)))
        operands.append(bias)
    if add is not None:
        in_specs.append(pl.BlockSpec((tm, tn), lambda i, j, kk: (i, j)))
        operands.append(add)
    if after is not None:
        in_specs.append(pl.BlockSpec(memory_space=pl.ANY))
        operands.append(after)

    def body(*refs):
        a_ref, b_ref = refs[0], refs[1]
        rest = list(refs[2:])
        bias_ref = rest.pop(0) if bias is not None else None
        add_ref = rest.pop(0) if add is not None else None
        if after is not None:
            rest.pop(0)
        o_ref, acc_ref = rest

        def finish(acc):
            if bias_ref is not None:
                acc = acc + bias_ref[...]
            if add_ref is not None:
                acc = acc + add_ref[...]
            o_ref[...] = acc.astype(o_ref.dtype)

        part = dot(a_ref[...].astype(BF16), b_ref[...].astype(BF16))
        if nk == 1:
            finish(part)
        else:
            kk = pl.program_id(2)

            @pl.when(kk == 0)
            def _():
                acc_ref[...] = part

            @pl.when(jnp.logical_and(kk > 0, kk < nk - 1))
            def _():
                acc_ref[...] += part

            @pl.when(kk == nk - 1)
            def _():
                finish(acc_ref[...] + part)

    if out_blocks is None:
        out_shape = jax.ShapeDtypeStruct((m, n), out_dtype)
        out_spec = pl.BlockSpec((tm, tn), lambda i, j, kk: (i, j))
    else:
        shape, block, index_map = out_blocks(tm, tn)
        out_shape = jax.ShapeDtypeStruct(shape, out_dtype)
        out_spec = pl.BlockSpec(block, index_map)
    acc_shape = (tm, tn) if nk > 1 else (SUBLANES, LANES)
    return pl.pallas_call(
        body, name=name, grid=(m // tm, n // tn, nk), in_specs=in_specs, out_specs=out_spec, out_shape=out_shape,
        scratch_shapes=[pltpu.VMEM(acc_shape, F32)],
        compiler_params=_params(("parallel", "parallel", "arbitrary")),
    )(*operands)


def _wgrad_half(a, b, rows, cols, row_sharded, name, which, *, suffix="", add=None, wire=False, tm=None, tn=None,
                b_split=False):
    tokens = a.shape[0]
    if row_sharded:
        sr, sc = rows // N_CHIPS, cols // N_CORES
    else:
        sr, sc = rows // N_CORES, cols // N_CHIPS
    tm = _tile(sr, 512) if tm is None else tm
    tn = _tile(sc, 1024) if tn is None else tn
    assert sr % tm == 0 and sc % tn == 0, (rows, cols, tm, tn)
    rb, cb = sr // tm, sc // tn
    if tn >= tm:
        ij, grid = (lambda s, t: (t, s)), (N_CHIPS, cb, rb)
    else:
        ij, grid = (lambda s, t: (s, t)), (N_CHIPS, rb, cb)
    if row_sharded:
        a_tile = lambda p, i, j, h: p * rb + i
        b_tile = lambda p, i, j, h: h[0] * cb + j
    else:
        a_tile = lambda p, i, j, h: h[0] * rb + i
        b_tile = lambda p, i, j, h: p * cb + j
    a_spec = pl.BlockSpec((tokens, tm), lambda p, s, t, h: (0, a_tile(p, *ij(s, t), h)))
    if b_split:
        nbh = b.shape[2] // tn
        assert b.shape[2] % tn == 0
        b_spec = pl.BlockSpec((None, tokens, tn), lambda p, s, t, h: (b_tile(p, *ij(s, t), h) // nbh, 0,
                                                                       b_tile(p, *ij(s, t), h) % nbh))
    else:
        b_spec = pl.BlockSpec((tokens, tn), lambda p, s, t, h: (0, b_tile(p, *ij(s, t), h)))
    out_spec = pl.BlockSpec((None, tm, tn), lambda p, s, t, h: (p, *ij(s, t)))
    in_specs, operands = [a_spec, b_spec], [a, b]
    if add is not None:
        in_specs.append(out_spec)
        operands.append(add)

    def body(h_ref, a_ref, b_ref, *rest):
        acc = _dot_tn(a_ref[...].astype(BF16), b_ref[...].astype(BF16))
        if add is not None:
            acc = acc + rest[0][...]
        for o_ref in rest[1 if add is not None else 0:]:
            o_ref[...] = acc.astype(o_ref.dtype)

    out_dtypes = [F32, BF16] if wire else [F32]
    out = pl.pallas_call(
        body, name=name + suffix, out_shape=[jax.ShapeDtypeStruct((N_CHIPS, sr, sc), dt) for dt in out_dtypes],
        grid_spec=pltpu.PrefetchScalarGridSpec(num_scalar_prefetch=1, grid=grid, in_specs=in_specs,
                                               out_specs=[out_spec] * len(out_dtypes)),
        compiler_params=_params(("parallel", "parallel", "parallel")),
    )(which, *operands)
    return tuple(out) if wire else out[0]


def _rms_rows(x):
    return lax.rsqrt(jnp.mean(x * x, axis=-1, keepdims=True) + RMS_EPS)


def _rmsnorm_fwd(x, w, *, name, width=None, col=0, out_dtype=BF16, tr=256):
    rows = x.shape[0]
    width = x.shape[1] if width is None else width
    tr = _tile(rows, tr, SUBLANES)

    def body(x_ref, w_ref, o_ref):
        xv = x_ref[...]
        o_ref[...] = (xv * _rms_rows(xv) * w_ref[...]).astype(o_ref.dtype)

    return pl.pallas_call(
        body, name=name, grid=(rows // tr,),
        in_specs=[pl.BlockSpec((tr, width), lambda i: (i, col)), pl.BlockSpec((1, width), lambda i: (0, 0))],
        out_specs=pl.BlockSpec((tr, width), lambda i: (i, 0)),
        out_shape=jax.ShapeDtypeStruct((rows, width), out_dtype),
        compiler_params=_params(("parallel",)),
    )(x, w)


def _rmsnorm_bwd_rows(xv, w, dy):
    r = _rms_rows(xv)
    n = xv * r
    dn = dy * w
    dx = r * (dn - n * jnp.mean(dn * n, axis=-1, keepdims=True))
    return dx, dy * n


def _rmsnorm_bwd(x, w, dy, *, name, width=None, col=0, dy_col=0, add=None, tr=256, dx_dtypes=(F32,)):
    rows = x.shape[0]
    n_dx = len(dx_dtypes)
    width = x.shape[1] if width is None else width
    tr = _tile(rows, tr, SUBLANES)
    in_specs = [pl.BlockSpec((tr, width), lambda i: (i, col)), pl.BlockSpec((1, width), lambda i: (0, 0)),
                pl.BlockSpec((tr, width), lambda i: (i, dy_col))]
    operands = [x, w, dy]
    if add is not None:
        in_specs.append(pl.BlockSpec((tr, width), lambda i: (i, 0)))
        operands.append(add)

    def body(*refs):
        x_ref, w_ref, dy_ref = refs[:3]
        add_ref = refs[3] if add is not None else None
        dx_refs, dw_ref = refs[-1 - n_dx:-1], refs[-1]
        dx, dwp = _rmsnorm_bwd_rows(x_ref[...], w_ref[...], dy_ref[...])
        if add_ref is not None:
            dx = dx + add_ref[...]
        for dx_ref in dx_refs:
            dx_ref[...] = dx.astype(dx_ref.dtype)
        part = jnp.sum(dwp, axis=0, keepdims=True)

        @pl.when(pl.program_id(0) == 0)
        def _():
            dw_ref[...] = part

        @pl.when(pl.program_id(0) > 0)
        def _():
            dw_ref[...] += part

    return pl.pallas_call(
        body, name=name, grid=(rows // tr,), in_specs=in_specs,
        out_specs=[pl.BlockSpec((tr, width), lambda i: (i, 0))] * n_dx + [pl.BlockSpec((1, width), lambda i: (0, 0))],
        out_shape=[jax.ShapeDtypeStruct((rows, width), dt) for dt in dx_dtypes] + [jax.ShapeDtypeStruct((1, width), F32)],
        compiler_params=_params(("arbitrary",)),
    )(*operands)


def _final_norm_loss(h, w, target, *, tr=256):
    rows, d = h.shape
    tr = _tile(rows, tr, SUBLANES)

    def body(h_ref, w_ref, t_ref, loss_ref, dh_ref, dhb_ref, dw_ref):
        hv, wv = h_ref[...], w_ref[...]
        r = _rms_rows(hv)
        n = hv * r
        err = n * wv - t_ref[...]
        d_out = err * (1.0 / d)
        dn = d_out * wv
        dh = r * (dn - n * jnp.mean(dn * n, axis=-1, keepdims=True))
        dh_ref[...] = dh
        dhb_ref[...] = dh.astype(BF16)
        dw_part = jnp.sum(d_out * n, axis=0, keepdims=True)
        loss_part = jnp.full((SUBLANES, LANES), 0.5 / d, F32) * jnp.sum(err * err)

        @pl.when(pl.program_id(0) == 0)
        def _():
            dw_ref[...] = dw_part
            loss_ref[...] = loss_part

        @pl.when(pl.program_id(0) > 0)
        def _():
            dw_ref[...] += dw_part
            loss_ref[...] += loss_part

    return pl.pallas_call(
        body, name="final_norm_loss", grid=(rows // tr,),
        in_specs=[pl.BlockSpec((tr, d), lambda i: (i, 0)), pl.BlockSpec((1, d), lambda i: (0, 0)),
                  pl.BlockSpec((tr, d), lambda i: (i, 0))],
        out_specs=[pl.BlockSpec((SUBLANES, LANES), lambda i: (0, 0)), pl.BlockSpec((tr, d), lambda i: (i, 0)),
                   pl.BlockSpec((tr, d), lambda i: (i, 0)), pl.BlockSpec((1, d), lambda i: (0, 0))],
        out_shape=[jax.ShapeDtypeStruct((SUBLANES, LANES), F32), jax.ShapeDtypeStruct((rows, d), F32),
                   jax.ShapeDtypeStruct((rows, d), BF16), jax.ShapeDtypeStruct((1, d), F32)],
        compiler_params=_params(("arbitrary",)),
    )(h, w, target)


def _cmul(ar, ai, br, bi):
    return ar * br - ai * bi, ar * bi + ai * br


def _expand_matrix(groups, reps):
    row = lax.broadcasted_iota(jnp.int32, (groups, groups * reps), 0)
    colg = lax.broadcasted_iota(jnp.int32, (groups, groups * reps), 1) // reps
    return (row == colg).astype(F32)


def _dot_exact(a, b, dims):
    return lax.dot_general(a, b, (dims, ((), ())), preferred_element_type=F32, precision=lax.Precision.HIGHEST)


def _s5_discretize(lr, li, dt):
    mag = jnp.exp(lr * dt)
    th = li * dt
    ar, ai = mag * jnp.cos(th), mag * jnp.sin(th)
    nr, ni = ar - 1.0, ai
    den = lr * lr + li * li
    zr = (nr * lr + ni * li) / den
    zi = (ni * lr - nr * li) / den
    return mag, ar, ai, nr, ni, den, zr, zi


def _s5_params(lam_re, lam_im, log_dt, b_re, b_im):
    g, p = lam_re.shape
    ph = b_re.shape[1]

    def body(lr_ref, li_ref, ldt_ref, br_ref, bi_ref, ar_ref, ai_ref, bbr_ref, bbi_ref):
        dt = jnp.exp(ldt_ref[...])
        _, ar, ai, _, _, _, zr, zi = _s5_discretize(lr_ref[...], li_ref[...], dt)
        ar_ref[...] = ar
        ai_ref[...] = ai
        e = _expand_matrix(p, ph // p)
        zr_x = _dot_exact(zr, e, ((1,), (0,)))
        zi_x = _dot_exact(zi, e, ((1,), (0,)))
        bre, bim = br_ref[...], bi_ref[...]
        bbr_ref[...] = zr_x * bre - zi_x * bim
        bbi_ref[...] = zr_x * bim + zi_x * bre

    return pl.pallas_call(
        body, name="s5_params",
        out_shape=[jax.ShapeDtypeStruct((g, p), F32)] * 2 + [jax.ShapeDtypeStruct((g, ph), F32)] * 2,
    )(lam_re, lam_im, log_dt, b_re, b_im)


def _s5_params_bwd(lam_re, lam_im, log_dt, b_re, b_im, d_ar, d_ai, d_bbr, d_bbi):
    g, p = lam_re.shape
    ph = b_re.shape[1]

    def body(lr_ref, li_ref, ldt_ref, br_ref, bi_ref, dar_ref, dai_ref, dbr_ref, dbi_ref,
             dlr_ref, dli_ref, dldt_ref, dbre_ref, dbim_ref):
        lr, li = lr_ref[...], li_ref[...]
        dt = jnp.exp(ldt_ref[...])
        mag, ar, ai, nr, ni, den, zr, zi = _s5_discretize(lr, li, dt)
        e = _expand_matrix(p, ph // p)
        zr_x = _dot_exact(zr, e, ((1,), (0,)))
        zi_x = _dot_exact(zi, e, ((1,), (0,)))
        bre, bim, dbr, dbi = br_ref[...], bi_ref[...], dbr_ref[...], dbi_ref[...]
        dbre_ref[...] = zr_x * dbr + zi_x * dbi
        dbim_ref[...] = zr_x * dbi - zi_x * dbr
        dzr = _dot_exact(bre * dbr + bim * dbi, e, ((1,), (1,)))
        dzi = _dot_exact(bre * dbi - bim * dbr, e, ((1,), (1,)))
        inv = 1.0 / den
        d_nr = (dzr * lr - dzi * li) * inv
        d_ni = (dzr * li + dzi * lr) * inv
        d_den = -(dzr * zr + dzi * zi) * inv
        d_lr = (dzr * nr + dzi * ni) * inv + 2.0 * lr * d_den
        d_li = (dzr * ni - dzi * nr) * inv + 2.0 * li * d_den
        t_ar = dar_ref[...] + d_nr
        t_ai = dai_ref[...] + d_ni
        d_lrdt = t_ar * ar + t_ai * ai
        d_th = t_ai * ar - t_ar * ai
        dlr_ref[...] = d_lr + d_lrdt * dt
        dli_ref[...] = d_li + d_th * dt
        dldt_ref[...] = jnp.sum(d_lrdt * lr + d_th * li, axis=1, keepdims=True) * dt

    return pl.pallas_call(
        body, name="s5_params_bwd",
        out_shape=[jax.ShapeDtypeStruct((g, p), F32)] * 2 + [jax.ShapeDtypeStruct((g, 1), F32)]
        + [jax.ShapeDtypeStruct((g, ph), F32)] * 2,
    )(lam_re, lam_im, log_dt, b_re, b_im, d_ar, d_ai, d_bbr, d_bbi)


def _powers(ar, ai, count):
    out = [(ar, ai)]
    for _ in range(count - 1):
        out.append(_cmul(out[-1][0], out[-1][1], ar, ai))
    return out


def _scan_coefs(ar, ai, reverse):
    w = ar.shape[-1]
    pw = _powers(ar, ai, SUBLANES)
    row = lax.broadcasted_iota(jnp.int32, (SUBLANES, w), 0)
    steps = []
    d = 1
    while d < SUBLANES:
        keep = (row < SUBLANES - d) if reverse else (row >= d)
        pr, pi = pw[d - 1]
        steps.append((d, jnp.where(keep, pr, 0.0), jnp.where(keep, pi, 0.0)))
        d *= 2
    cr = jnp.zeros((SUBLANES, w), F32)
    ci = jnp.zeros((SUBLANES, w), F32)
    for t in range(SUBLANES):
        pr, pi = pw[SUBLANES - 1 - t] if reverse else pw[t]
        cr = jnp.where(row == t, pr, cr)
        ci = jnp.where(row == t, pi, ci)
    return steps, cr, ci


def _scan_tile(xr, xi, carry_r, carry_i, coefs, reverse):
    steps, cr, ci = coefs
    for d, mr, mi in steps:
        shift = SUBLANES - d if reverse else d
        sr, si = pltpu.roll(xr, shift, 0), pltpu.roll(xi, shift, 0)
        pr, pi = _cmul(mr, mi, sr, si)
        xr, xi = xr + pr, xi + pi
    pr, pi = _cmul(cr, ci, carry_r, carry_i)
    return xr + pr, xi + pi


def _gelu(x):
    c = math.sqrt(2.0 / math.pi)
    return 0.5 * x * (1.0 + jnp.tanh(c * (x + 0.044715 * x * x * x)))


def _gelu_grad(x):
    c = math.sqrt(2.0 / math.pi)
    t = jnp.tanh(c * (x + 0.044715 * x * x * x))
    return 0.5 * (1.0 + t) + 0.5 * x * (1.0 - t * t) * c * (1.0 + 3.0 * 0.044715 * x * x)


def _s5_fwd(proj, wb, wc, d_skip, abar):
    rows = proj.shape[0]
    nb = wb.shape[0]
    s2 = 2 * STATE_PER_BATCH
    st = STATE_PER_BATCH
    chunk = _tile(rows, 512, SUBLANES)

    def body(u_ref, wb_ref, wc_ref, d_ref, a_ref, s_ref, y_ref, yg_ref):
        for c0 in range(0, rows, chunk):
            s_ref[pl.ds(c0, chunk), :] = _dot_nn(u_ref[pl.ds(c0, chunk), :].astype(BF16), wb_ref[...])
        av = a_ref[...]
        coefs = _scan_coefs(av[:, :st], av[:, st:], reverse=False)

        def tile(b, carry):
            r0 = pl.multiple_of(b * SUBLANES, SUBLANES)
            xr, xi = _scan_tile(s_ref[pl.ds(r0, SUBLANES), :st], s_ref[pl.ds(r0, SUBLANES), st:], carry[0], carry[1],
                                coefs, False)
            s_ref[pl.ds(r0, SUBLANES), :st] = xr
            s_ref[pl.ds(r0, SUBLANES), st:] = xi
            return xr[SUBLANES - 1:, :], xi[SUBLANES - 1:, :]

        zero = jnp.zeros((1, st), F32)
        lax.fori_loop(0, rows // SUBLANES, tile, (zero, zero))
        for c0 in range(0, rows, chunk):
            y = _dot_nn(s_ref[pl.ds(c0, chunk), :].astype(BF16), wc_ref[...]) + d_ref[...] * u_ref[pl.ds(c0, chunk), :]
            y_ref[pl.ds(c0, chunk), :] = y
            yg_ref[pl.ds(c0, chunk), :] = _gelu(y).astype(BF16)

    return pl.pallas_call(
        body, name="s5_fwd", grid=(nb,),
        in_specs=[pl.BlockSpec((rows, LANES), lambda j: (0, j)), pl.BlockSpec((None, LANES, s2), lambda j: (j, 0, 0)),
                  pl.BlockSpec((None, s2, LANES), lambda j: (j, 0, 0)), pl.BlockSpec((1, LANES), lambda j: (0, j)),
                  pl.BlockSpec((None, 1, s2), lambda j: (j, 0, 0))],
        out_specs=[pl.BlockSpec((rows, s2), lambda j: (0, j)), pl.BlockSpec((rows, LANES), lambda j: (0, j)),
                   pl.BlockSpec((rows, LANES), lambda j: (0, j))],
        out_shape=[jax.ShapeDtypeStruct((rows, nb * s2), F32), jax.ShapeDtypeStruct((rows, nb * LANES), F32),
                   jax.ShapeDtypeStruct((rows, nb * LANES), BF16)],
        compiler_params=_params(("parallel",)),
    )(proj, wb, wc, d_skip, abar)


def _s5_bwd(proj, states, y_pre, dyg_a, dyg_b, wb, wc, d_skip, abar):
    rows = proj.shape[0]
    nb = wb.shape[0]
    s2 = 2 * STATE_PER_BATCH
    st = STATE_PER_BATCH
    chunk = _tile(rows, 512, SUBLANES)
    n_tiles = rows // SUBLANES

    def body(u_ref, s_ref, y_ref, ga_ref, gb_ref, wb_ref, wc_ref, d_ref, a_ref,
             du_ref, dwb_ref, dwc_ref, da_ref, dd_ref, ds_ref, dy_ref):
        dy_ref[...] = (ga_ref[...] + gb_ref[...]) * _gelu_grad(y_ref[...])
        dd_ref[...] = jnp.sum(dy_ref[...] * u_ref[...], axis=0, keepdims=True)
        for c0 in range(0, rows, chunk):
            ds_ref[pl.ds(c0, chunk), :] = _dot_nt(dy_ref[pl.ds(c0, chunk), :].astype(BF16), wc_ref[...])
        dwc_ref[...] = _dot_tn(s_ref[...].astype(BF16), dy_ref[...].astype(BF16))
        av = a_ref[...]
        coefs = _scan_coefs(av[:, :st], -av[:, st:], reverse=True)
        row = lax.broadcasted_iota(jnp.int32, (SUBLANES, st), 0)

        def tile(k, carry):
            cr, ci, acc_r, acc_i = carry
            b = n_tiles - 1 - k
            r0 = pl.multiple_of(b * SUBLANES, SUBLANES)
            rp = pl.multiple_of(jnp.maximum(b - 1, 0) * SUBLANES, SUBLANES)
            xr, xi = _scan_tile(ds_ref[pl.ds(r0, SUBLANES), :st], ds_ref[pl.ds(r0, SUBLANES), st:], cr, ci, coefs, True)
            ds_ref[pl.ds(r0, SUBLANES), :st] = xr
            ds_ref[pl.ds(r0, SUBLANES), st:] = xi
            first = jnp.where(b > 0, 1.0, 0.0)
            pr = jnp.where(row == 0, pltpu.roll(s_ref[pl.ds(rp, SUBLANES), :st], 1, 0) * first,
                           pltpu.roll(s_ref[pl.ds(r0, SUBLANES), :st], 1, 0))
            pi = jnp.where(row == 0, pltpu.roll(s_ref[pl.ds(rp, SUBLANES), st:], 1, 0) * first,
                           pltpu.roll(s_ref[pl.ds(r0, SUBLANES), st:], 1, 0))
            acc_r = acc_r + pr * xr + pi * xi
            acc_i = acc_i + pr * xi - pi * xr
            return xr[:1, :], xi[:1, :], acc_r, acc_i

        zero = jnp.zeros((1, st), F32)
        zacc = jnp.zeros((SUBLANES, st), F32)
        _, _, acc_r, acc_i = lax.fori_loop(0, n_tiles, tile, (zero, zero, zacc, zacc))
        da_ref[:, :st] = jnp.sum(acc_r, axis=0, keepdims=True)
        da_ref[:, st:] = jnp.sum(acc_i, axis=0, keepdims=True)
        for c0 in range(0, rows, chunk):
            du_ref[pl.ds(c0, chunk), :] = (_dot_nt(ds_ref[pl.ds(c0, chunk), :].astype(BF16), wb_ref[...])
                                           + d_ref[...] * dy_ref[pl.ds(c0, chunk), :]).astype(du_ref.dtype)
        dwb_ref[...] = _dot_tn(u_ref[...].astype(BF16), ds_ref[...].astype(BF16))

    col = pl.BlockSpec((rows, LANES), lambda j: (0, j))
    return pl.pallas_call(
        body, name="s5_bwd", grid=(nb,),
        in_specs=[col, pl.BlockSpec((rows, s2), lambda j: (0, j)), col, col, col,
                  pl.BlockSpec((None, LANES, s2), lambda j: (j, 0, 0)), pl.BlockSpec((None, s2, LANES), lambda j: (j, 0, 0)),
                  pl.BlockSpec((1, LANES), lambda j: (0, j)), pl.BlockSpec((None, 1, s2), lambda j: (j, 0, 0))],
        out_specs=[col, pl.BlockSpec((None, LANES, s2), lambda j: (j, 0, 0)),
                   pl.BlockSpec((None, s2, LANES), lambda j: (j, 0, 0)), pl.BlockSpec((None, 1, s2), lambda j: (j, 0, 0)),
                   pl.BlockSpec((1, LANES), lambda j: (0, j))],
        out_shape=[jax.ShapeDtypeStruct((rows, nb * LANES), BF16), jax.ShapeDtypeStruct((nb, LANES, s2), F32),
                   jax.ShapeDtypeStruct((nb, s2, LANES), F32), jax.ShapeDtypeStruct((nb, 1, s2), F32),
                   jax.ShapeDtypeStruct((1, nb * LANES), F32)],
        scratch_shapes=[pltpu.VMEM((rows, s2), F32), pltpu.VMEM((rows, LANES), F32)],
        compiler_params=_params(("parallel",)),
    )(proj, states, y_pre, dyg_a, dyg_b, wb, wc, d_skip, abar)


def _glu_norm_fwd(y_pre, z, w, *, tr=256):
    rows, width = y_pre.shape
    tr = _tile(rows, tr, SUBLANES)

    def body(y_ref, z_ref, w_ref, o_ref):
        v = _gelu(y_ref[...]) * jax.nn.sigmoid(z_ref[...])
        o_ref[...] = (v * _rms_rows(v) * w_ref[...]).astype(o_ref.dtype)

    blk = pl.BlockSpec((tr, width), lambda i: (i, 0))
    return pl.pallas_call(
        body, name="glu_norm_fwd", grid=(rows // tr,),
        in_specs=[blk, blk, pl.BlockSpec((1, width), lambda i: (0, 0))], out_specs=blk,
        out_shape=jax.ShapeDtypeStruct((rows, width), BF16), compiler_params=_params(("parallel",)),
    )(y_pre, z, w)


def _glu_norm_bwd(y_pre, z, w, dycat, *, tr=256):
    rows, width = y_pre.shape
    tr = _tile(rows, tr, SUBLANES)

    def body(y_ref, z_ref, w_ref, dy_ref, dz_ref, dg_ref, dw_ref, db_ref):
        yg = _gelu(y_ref[...])
        sg = jax.nn.sigmoid(z_ref[...])
        dv, dwp = _rmsnorm_bwd_rows(yg * sg, w_ref[...], dy_ref[...])
        dz = dv * yg * sg * (1.0 - sg)
        dz_ref[...] = dz.astype(dz_ref.dtype)
        dg_ref[...] = dv * sg
        dw_part = jnp.sum(dwp, axis=0, keepdims=True)
        db_part = jnp.sum(dz, axis=0, keepdims=True)

        @pl.when(pl.program_id(0) == 0)
        def _():
            dw_ref[...] = dw_part
            db_ref[...] = db_part

        @pl.when(pl.program_id(0) > 0)
        def _():
            dw_ref[...] += dw_part
            db_ref[...] += db_part

    blk = pl.BlockSpec((tr, width), lambda i: (i, 0))
    vec = pl.BlockSpec((1, width), lambda i: (0, 0))
    return pl.pallas_call(
        body, name="glu_norm_bwd", grid=(rows // tr,), in_specs=[blk, blk, vec, blk], out_specs=[blk, blk, vec, vec],
        out_shape=[jax.ShapeDtypeStruct((rows, width), BF16), jax.ShapeDtypeStruct((rows, width), F32)]
        + [jax.ShapeDtypeStruct((1, width), F32)] * 2,
        compiler_params=_params(("arbitrary",)),
    )(y_pre, z, w, dycat)


def _rope_tables(pos, freq, sign):
    rows = pos.shape[0]

    def body(p_ref, f_ref, s_ref, cos_ref, sin_ref):
        ang = p_ref[...] * f_ref[...]
        cos_ref[...] = jnp.cos(ang)
        sin_ref[...] = jnp.sin(ang) * s_ref[...]

    return pl.pallas_call(body, name="rope_tables", out_shape=[jax.ShapeDtypeStruct((rows, LANES), F32)] * 2)(pos, freq, sign)


def _rope(x, cos, sin_signed):
    lane = lax.broadcasted_iota(jnp.int32, x.shape, 1)
    half = QK_ROPE_DIM // 2
    swapped = jnp.where(lane < half, pltpu.roll(x, LANES - half, 1), pltpu.roll(x, half, 1))
    return x * cos + swapped * sin_signed


def _attn_prep(q, kv, proj, kpe_col, cos, sin, *, tr=256):
    rows = q.shape[0]
    heads = q.shape[1] // HEAD_SLOT
    tr = _tile(rows, tr, SUBLANES)

    def body(q_ref, kv_ref, kpe_ref, cos_ref, sin_ref, qc_ref, kc_ref, v_ref):
        c, s = cos_ref[...], sin_ref[...]
        qc_ref[:, :LANES] = q_ref[:, :LANES].astype(BF16)
        qc_ref[:, LANES:] = _rope(q_ref[:, LANES:], c, s).astype(BF16)
        kc_ref[:, :LANES] = kv_ref[:, :LANES].astype(BF16)
        kc_ref[:, LANES:] = _rope(kpe_ref[...], c, s).astype(BF16)
        v_ref[...] = kv_ref[:, LANES:].astype(BF16)

    slot = pl.BlockSpec((tr, HEAD_SLOT), lambda i, h: (i, h))
    tab = pl.BlockSpec((tr, LANES), lambda i, h: (i, 0))
    return pl.pallas_call(
        body, name="attn_prep", grid=(rows // tr, heads),
        in_specs=[slot, slot, pl.BlockSpec((tr, LANES), lambda i, h: (i, kpe_col)), tab, tab],
        out_specs=[slot, slot, pl.BlockSpec((tr, LANES), lambda i, h: (i, h))],
        out_shape=[jax.ShapeDtypeStruct((rows, heads * HEAD_SLOT), BF16)] * 2
        + [jax.ShapeDtypeStruct((rows, heads * LANES), BF16)],
        compiler_params=_params(("parallel", "parallel")),
    )(q, kv, proj, cos, sin)


def _causal(tq, tk):
    return lax.broadcasted_iota(jnp.int32, (tq, tk), 1) <= lax.broadcasted_iota(jnp.int32, (tq, tk), 0)


def _attn_fwd(qc, kc, vb, *, scale, tq=512):
    rows = qc.shape[0]
    heads = qc.shape[1] // HEAD_SLOT
    tq = _tile(rows, tq, SUBLANES)
    tk = tq

    def body(q_ref, k_ref, v_ref, o_ref, lse_ref):
        i = pl.program_id(1)
        q = q_ref[...]

        def step(j, carry, diagonal):
            m, l, acc = carry
            k0 = pl.multiple_of(j * tk, tk)
            s = _dot_nt(q, k_ref[pl.ds(k0, tk), :]) * scale
            if diagonal:
                s = jnp.where(_causal(tq, tk), s, NEG_INF)
            m_new = jnp.maximum(m, jnp.max(s, axis=-1, keepdims=True))
            p = jnp.exp(s - m_new)
            alpha = jnp.exp(m - m_new)
            l = alpha * l + jnp.sum(p, axis=-1, keepdims=True)
            acc = alpha * acc + _dot_nn(p.astype(BF16), v_ref[pl.ds(k0, tk), :])
            return m_new, l, acc

        init = (jnp.full((tq, 1), NEG_INF, F32), jnp.zeros((tq, 1), F32), jnp.zeros((tq, LANES), F32))
        below = lax.fori_loop(0, i, lambda j, carry: step(j, carry, False), init)
        m, l, acc = step(i, below, True)
        o_ref[...] = acc / l
        lse_ref[...] = jnp.broadcast_to(m + jnp.log(l), (tq, LANES))

    return pl.pallas_call(
        body, name="attn_fwd", grid=(heads, rows // tq),
        in_specs=[pl.BlockSpec((tq, HEAD_SLOT), lambda h, i: (i, h)), pl.BlockSpec((rows, HEAD_SLOT), lambda h, i: (0, h)),
                  pl.BlockSpec((rows, LANES), lambda h, i: (0, h))],
        out_specs=[pl.BlockSpec((tq, LANES), lambda h, i: (i, h))] * 2,
        out_shape=[jax.ShapeDtypeStruct((rows, heads * LANES), F32)] * 2,
        compiler_params=_params(("parallel", "parallel")),
    )(qc, kc, vb)


def _attn_bwd(qc, kc, vb, o, do, lse, cos, sin, *, scale, tk=512):
    rows = qc.shape[0]
    heads = qc.shape[1] // HEAD_SLOT
    tk = _tile(rows, tk, SUBLANES)
    tq = tk
    nq = rows // tq

    def body(q_ref, k_ref, v_ref, o_ref, do_ref, lse_ref, cos_ref, sin_ref, dq_ref, dkv_ref, dkpe_ref, dq_acc, delta_ref):
        j = pl.program_id(1)

        @pl.when(j == 0)
        def _():
            dq_acc[...] = jnp.zeros_like(dq_acc)
            for r0 in range(0, rows, tq):
                d = jnp.sum(do_ref[pl.ds(r0, tq), :] * o_ref[pl.ds(r0, tq), :], axis=-1, keepdims=True)
                delta_ref[pl.ds(r0, tq), :] = jnp.broadcast_to(d, (tq, LANES))

        kb, vv = k_ref[...], v_ref[...]

        def step(i, carry, diagonal):
            dk, dv = carry
            q0 = pl.multiple_of(i * tq, tq)
            qb = q_ref[pl.ds(q0, tq), :]
            dob = do_ref[pl.ds(q0, tq), :].astype(BF16)
            s = _dot_nt(qb, kb) * scale
            p = jnp.exp(s - lse_ref[pl.ds(q0, tq), :1])
            if diagonal:
                p = jnp.where(_causal(tq, tk), p, 0.0)
            dv = dv + _dot_tn(p.astype(BF16), dob)
            ds = (p * (_dot_nt(dob, vv) - delta_ref[pl.ds(q0, tq), :1])).astype(BF16)
            dk = dk + _dot_tn(ds, qb)
            dq_acc[pl.ds(q0, tq), :] += _dot_nn(ds, kb)
            return dk, dv

        zero = (jnp.zeros((tk, HEAD_SLOT), F32), jnp.zeros((tk, LANES), F32))
        dk, dv = lax.fori_loop(j + 1, nq, lambda i, carry: step(i, carry, False), step(j, zero, True))
        dkv_ref[:, :LANES] = (dk[:, :LANES] * scale).astype(dkv_ref.dtype)
        dkv_ref[:, LANES:] = dv.astype(dkv_ref.dtype)
        dkpe_ref[...] = dk[:, LANES:] * scale

        @pl.when(j == nq - 1)
        def _():
            for r0 in range(0, rows, tq):
                dq = dq_acc[pl.ds(r0, tq), :] * scale
                dq_ref[pl.ds(r0, tq), :LANES] = dq[:, :LANES].astype(dq_ref.dtype)
                dq_ref[pl.ds(r0, tq), LANES:] = _rope(dq[:, LANES:], cos_ref[pl.ds(r0, tq), :],
                                                      -sin_ref[pl.ds(r0, tq), :]).astype(dq_ref.dtype)

    full_q = pl.BlockSpec((rows, HEAD_SLOT), lambda h, j: (0, h))
    full_v = pl.BlockSpec((rows, LANES), lambda h, j: (0, h))
    tab = pl.BlockSpec((rows, LANES), lambda h, j: (0, 0))
    return pl.pallas_call(
        body, name="attn_bwd", grid=(heads, rows // tk),
        in_specs=[full_q, pl.BlockSpec((tk, HEAD_SLOT), lambda h, j: (j, h)), pl.BlockSpec((tk, LANES), lambda h, j: (j, h)),
                  full_v, full_v, full_v, tab, tab],
        out_specs=[full_q, pl.BlockSpec((tk, HEAD_SLOT), lambda h, j: (j, h)), pl.BlockSpec((tk, LANES), lambda h, j: (j, h))],
        out_shape=[jax.ShapeDtypeStruct((rows, heads * HEAD_SLOT), BF16), jax.ShapeDtypeStruct((rows, heads * HEAD_SLOT), BF16),
                   jax.ShapeDtypeStruct((rows, heads * LANES), F32)],
        scratch_shapes=[pltpu.VMEM((rows, HEAD_SLOT), F32), pltpu.VMEM((rows, LANES), F32)],
        compiler_params=_params(("parallel", "arbitrary")),
    )(qc, kc, vb, o, do, lse, cos, sin)


def _kpe_bwd(dkpe_heads, cos, sin, *, tr=512):
    rows = dkpe_heads.shape[0]
    heads = dkpe_heads.shape[1] // LANES
    tr = _tile(rows, tr, 2 * SUBLANES)

    def body(d_ref, cos_ref, sin_ref, o_ref):
        acc = d_ref[:, :LANES]
        for h in range(1, heads):
            acc = acc + d_ref[:, h * LANES:(h + 1) * LANES]
        o_ref[...] = _rope(acc, cos_ref[...], -sin_ref[...]).astype(o_ref.dtype)

    tab = pl.BlockSpec((tr, LANES), lambda i: (i, 0))
    return pl.pallas_call(
        body, name="kpe_bwd", grid=(rows // tr,),
        in_specs=[pl.BlockSpec((tr, heads * LANES), lambda i: (i, 0)), tab, tab], out_specs=tab,
        out_shape=jax.ShapeDtypeStruct((rows, LANES), BF16), compiler_params=_params(("parallel",)),
    )(dkpe_heads, cos, sin)


CONV_ROWS = 128


def _with_halo(ref, r0, ci, n_chunks, ch, lanes, before, after):
    parts = []
    if before:
        lo = pl.multiple_of(jnp.maximum(r0 - SUBLANES, 0), SUBLANES)
        parts.append(ref[pl.ds(lo, SUBLANES), lanes] * jnp.where(ci > 0, 1.0, 0.0))
    parts.append(ref[pl.ds(r0, ch), lanes])
    if after:
        hi = pl.multiple_of(jnp.minimum(r0 + ch, n_chunks * ch - SUBLANES), SUBLANES)
        parts.append(ref[pl.ds(hi, SUBLANES), lanes] * jnp.where(ci < n_chunks - 1, 1.0, 0.0))
    return jnp.concatenate(parts, axis=0)


def _taps(ext):
    return pltpu.roll(ext, 2, 0)[SUBLANES:], pltpu.roll(ext, 1, 0)[SUBLANES:], ext[SUBLANES:]


def _conv3(taps, w, b):
    return w[0:1, :] * taps[0] + w[1:2, :] * taps[1] + w[2:3, :] * taps[2] + b


def _conv_gate_fwd(a, conv_w, conv_b, *, tc=256):
    rows, f2 = a.shape
    f = f2 // 2
    tc = _tile(f, tc)
    nc = f // tc
    ch = _tile(rows, CONV_ROWS, SUBLANES)
    n_chunks = rows // ch

    def body(ag_ref, av_ref, wg_ref, wv_ref, bg_ref, bv_ref, o_ref):
        for lt in range(tc // LANES):
            lanes = slice(lt * LANES, (lt + 1) * LANES)
            wg, wv, bg, bv = wg_ref[:, lanes], wv_ref[:, lanes], bg_ref[:, lanes], bv_ref[:, lanes]

            def chunk(ci, carry):
                r0 = pl.multiple_of(ci * ch, ch)
                gate = _conv3(_taps(_with_halo(ag_ref, r0, ci, n_chunks, ch, lanes, True, False)), wg, bg)
                val = _conv3(_taps(_with_halo(av_ref, r0, ci, n_chunks, ch, lanes, True, False)), wv, bv)
                o_ref[pl.ds(r0, ch), lanes] = (gate * jax.nn.sigmoid(gate) * val).astype(o_ref.dtype)
                return carry

            lax.fori_loop(0, n_chunks, chunk, 0)

    return pl.pallas_call(
        body, name="conv_gate_fwd", grid=(nc,),
        in_specs=[pl.BlockSpec((rows, tc), lambda j: (0, j)), pl.BlockSpec((rows, tc), lambda j: (0, j + nc)),
                  pl.BlockSpec((SUBLANES, tc), lambda j: (0, j)), pl.BlockSpec((SUBLANES, tc), lambda j: (0, j + nc)),
                  pl.BlockSpec((1, tc), lambda j: (0, j)), pl.BlockSpec((1, tc), lambda j: (0, j + nc))],
        out_specs=pl.BlockSpec((rows, tc), lambda j: (0, j)),
        out_shape=jax.ShapeDtypeStruct((rows, f), BF16), compiler_params=_params(("parallel",)),
    )(a, a, conv_w, conv_w, conv_b, conv_b)


def _conv_gate_bwd(a, conv_w, conv_b, dg, *, tc=256):
    rows, f2 = a.shape
    f = f2 // 2
    tc = _tile(f, tc)
    nc = f // tc
    ch = _tile(rows, CONV_ROWS, SUBLANES)
    n_chunks = rows // ch
    ext_rows = ch + SUBLANES

    def fold(x):
        return jnp.sum(x.reshape(ch // SUBLANES, SUBLANES, LANES), axis=0)

    def body(ag_ref, av_ref, wg_ref, wv_ref, bg_ref, bv_ref, dg_ref, da_ref, dw_ref, db_ref):
        for lt in range(tc // LANES):
            lanes = slice(lt * LANES, (lt + 1) * LANES)
            wg, wv, bg, bv = wg_ref[:, lanes], wv_ref[:, lanes], bg_ref[:, lanes], bv_ref[:, lanes]

            def chunk(ci, acc):
                r0 = pl.multiple_of(ci * ch, ch)
                taps_g = _taps(_with_halo(ag_ref, r0, ci, n_chunks, ch, lanes, True, True))
                taps_v = _taps(_with_halo(av_ref, r0, ci, n_chunks, ch, lanes, True, True))
                dge = _with_halo(dg_ref, r0, ci, n_chunks, ch, lanes, False, True)
                gate, val = _conv3(taps_g, wg, bg), _conv3(taps_v, wv, bv)
                sg = jax.nn.sigmoid(gate)
                d_gate = dge * val * sg * (1.0 + gate * (1.0 - sg))
                d_val = dge * gate * sg
                new = []
                for half, (taps, w, d) in enumerate(((taps_g, wg, d_gate), (taps_v, wv, d_val))):
                    da = (w[2:3, :] * d[:ch] + w[1:2, :] * pltpu.roll(d, ext_rows - 1, 0)[:ch]
                          + w[0:1, :] * pltpu.roll(d, ext_rows - 2, 0)[:ch])
                    da_ref[half, pl.ds(r0, ch), lanes] = da.astype(da_ref.dtype)
                    dc = d[:ch]
                    sums = [fold(dc)] + [fold(dc * t[:ch]) for t in taps]
                    new.append(tuple(x + s for x, s in zip(acc[half], sums)))
                return tuple(new)

            zero = tuple(jnp.zeros((SUBLANES, LANES), F32) for _ in range(4))
            acc = lax.fori_loop(0, n_chunks, chunk, (zero, zero))
            row = lax.broadcasted_iota(jnp.int32, (SUBLANES, LANES), 0)
            for half in range(2):
                db, *taps = (jnp.sum(x, axis=0, keepdims=True) for x in acc[half])
                db_ref[half, :, lanes] = db
                dw = jnp.zeros((SUBLANES, LANES), F32)
                for tap in range(3):
                    dw = jnp.where(row == tap, taps[tap], dw)
                dw_ref[half, :, lanes] = dw

    lo = lambda j: (0, j)
    hi = lambda j: (0, j + nc)
    both = lambda j: (0, 0, j)
    return pl.pallas_call(
        body, name="conv_gate_bwd", grid=(nc,),
        in_specs=[pl.BlockSpec((rows, tc), lo), pl.BlockSpec((rows, tc), hi), pl.BlockSpec((SUBLANES, tc), lo),
                  pl.BlockSpec((SUBLANES, tc), hi), pl.BlockSpec((1, tc), lo), pl.BlockSpec((1, tc), hi),
                  pl.BlockSpec((rows, tc), lo)],
        out_specs=[pl.BlockSpec((2, rows, tc), both), pl.BlockSpec((2, SUBLANES, tc), both), pl.BlockSpec((2, 1, tc), both)],
        out_shape=[jax.ShapeDtypeStruct((2, rows, f), BF16), jax.ShapeDtypeStruct((2, SUBLANES, f), F32),
                   jax.ShapeDtypeStruct((2, 1, f), F32)],
        compiler_params=_params(("parallel",)),
    )(a, a, conv_w, conv_w, conv_b, conv_b, dg)


def _wgrad(a, b, rows, cols, row_sharded, name, **kw):
    return functools.partial(_wgrad_half, a, b, rows, cols, row_sharded, name, **kw)


def _block_diag(x):
    nb, g, r, c = x.shape
    eye = jnp.eye(g, dtype=x.dtype)
    return (x[:, :, :, None, :] * eye[None, :, None, :, None]).reshape(nb, g * r, g * c)


def _block_diag_part(x, r, c):
    nb = x.shape[0]
    g = GROUPS_PER_BATCH
    eye = jnp.eye(g, dtype=x.dtype)
    return jnp.sum(x.reshape(nb, g, r, g, c) * eye[None, :, None, :, None], axis=3)


class _NoExchange:
    def __init__(self, later, ffn):
        self.later, self.ffn = later, ffn

    def mixer_weights(self, after):
        return self.later

    def ffn_weights_arrived(self, after):
        return None

    def ffn_weights(self, after):
        return self.ffn

    def ffn_down_weight(self, after):
        return self.ffn["ffn_w_down"]

    def ffn_grads(self, makers, after):
        self.ffn_makers = makers
        return None

    def ffn_backward_done(self, after):
        return None


def _local_step(x, posf, target, w, hooks):
    rows, d = x.shape
    width = w["ssm_d"].shape[1]
    qr, kvr = w["mla_q_norm_w"].shape[1], w["mla_kv_norm_w"].shape[1]
    heads = w["mla_w_ukv"].shape[1] // HEAD_SLOT
    f2 = w["ffn_conv_b"].shape[1]
    inp = w["w_in"].shape[0]
    groups = width // SSM_GROUP
    nb = groups // GROUPS_PER_BATCH
    scale = (QK_NOPE_DIM + QK_ROPE_DIM) ** -0.5
    g = {}

    hn = _rmsnorm_fwd(x, w["attn_norm_w"], name="attn_norm")
    proj = _matmul(hn, w["w_in"], mode="nt", name="in_proj")

    ar, ai, bbr, bbi = _s5_params(w["ssm_lambda_re"], w["ssm_lambda_im"], w["ssm_log_dt"], w["ssm_b_re"], w["ssm_b_im"])

    def b_band(bb):
        return _block_diag(bb.reshape(nb, GROUPS_PER_BATCH, SSM_STATE, SSM_GROUP).transpose(0, 1, 3, 2))

    def c_band(c):
        return _block_diag(c.reshape(nb, GROUPS_PER_BATCH, SSM_GROUP, SSM_STATE).transpose(0, 1, 3, 2))

    wb = jnp.concatenate([b_band(bbr), b_band(bbi)], axis=2).astype(BF16)
    wc = jnp.concatenate([c_band(w["ssm_c_re"]), -c_band(w["ssm_c_im"])], axis=1).astype(BF16)
    abar = jnp.concatenate([ar.reshape(nb, 1, STATE_PER_BATCH), ai.reshape(nb, 1, STATE_PER_BATCH)], axis=2)
    states, y_pre, yg = _s5_fwd(proj, wb, wc, w["ssm_d"], abar)
    later = hooks.mixer_weights(yg)
    z = _matmul(yg, later["ssm_w_glu"], mode="nn", name="glu_proj", bias=w["ssm_b_glu"])
    ys = _glu_norm_fwd(y_pre, z, w["ssm_out_norm_w"])

    q_col, kv_col, kpe_col = width // qr, (width + qr) // kvr, (width + qr + kvr) // LANES
    assert width % qr == 0 and (width + qr) % kvr == 0
    qn = _rmsnorm_fwd(proj, w["mla_q_norm_w"], name="q_norm", width=qr, col=q_col)
    kvn = _rmsnorm_fwd(proj, w["mla_kv_norm_w"], name="kv_norm", width=kvr, col=kv_col)
    q = _matmul(qn, w["mla_w_uq"], mode="nn", name="q_proj")
    kv = _matmul(kvn, w["mla_w_ukv"], mode="nn", name="kv_proj")
    half = QK_ROPE_DIM // 2
    inv_freq = ROPE_THETA ** (-jnp.arange(0, QK_ROPE_DIM, 2, dtype=F32) / QK_ROPE_DIM)
    zeros = jnp.zeros((LANES - QK_ROPE_DIM,), F32)
    freq = jnp.concatenate([inv_freq, inv_freq, zeros]).reshape(1, LANES)
    sign = jnp.concatenate([-jnp.ones((half,), F32), jnp.ones((half,), F32), zeros]).reshape(1, LANES)
    cos, sin = _rope_tables(posf, freq, sign)
    qc, kc, vb = _attn_prep(q, kv, proj, kpe_col, cos, sin)
    o, lse = _attn_fwd(qc, kc, vb, scale=scale, tq=ATTN_BLOCK)
    ym = _rmsnorm_fwd(o, w["mla_out_norm_w"], name="mla_out_norm")
    ycat = jnp.concatenate([ys, ym], axis=1)
    h1 = _matmul(ycat, later["w_out"], mode="nn", name="out_proj", add=x, after=hooks.ffn_weights_arrived(ycat))

    hn2 = _rmsnorm_fwd(h1, w["ffn_norm_w"], name="ffn_norm")
    ffn = hooks.ffn_weights(hn2)
    a = _matmul(hn2, ffn["ffn_w_up"], mode="nn", name="ffn_up", tm=FFN_ROWS, after=ffn.get("started"))
    gated = _conv_gate_fwd(a, ffn["ffn_conv_w"], w["ffn_conv_b"])
    w_down = hooks.ffn_down_weight(gated)
    h2 = _matmul(gated, w_down, mode="nn", name="ffn_down", add=h1, tk=2816, tm=FFN_ROWS)
    loss_tile, dh2, dh2_mxu, g["final_norm_w"] = _final_norm_loss(h2, w["final_norm_w"], target)

    dgated = _matmul(dh2_mxu, w_down, mode="nt", name="ffn_down_dx", tm=FFN_ROWS)
    da, dcw, dcb = _conv_gate_bwd(a, ffn["ffn_conv_w"], w["ffn_conv_b"], dgated)
    g["ffn_conv_w"] = jnp.concatenate([dcw[0, :3], dcw[1, :3]], axis=1)
    g["ffn_conv_b"] = jnp.concatenate([dcb[0], dcb[1]], axis=1)
    started = hooks.ffn_grads({
        "ffn_w_up": _wgrad(hn2, da, d, f2, False, "ffn_up_dw", b_split=True, tn=_tile(f2 // N_CHIPS, 1408)),
        "ffn_w_down": _wgrad(gated, dh2_mxu, f2 // 2, d, True, "ffn_down_dw", tm=f2 // 2 // N_CHIPS, tn=512)}, dcb)
    dhn2 = _matmul(da, ffn["ffn_w_up"], mode="nt", name="ffn_up_dx", a_split=True, tk=_tile(f2 // 2, 2816), tm=FFN_ROWS,
                   after=started)
    dh1, dh1_mxu, g["ffn_norm_w"] = _rmsnorm_bwd(h1, w["ffn_norm_w"], dhn2, name="ffn_norm_bwd", add=dh2,
                                                dx_dtypes=(F32, BF16))

    dycat = _matmul(dh1_mxu, later["w_out"], mode="nt", name="out_proj_dx")
    g["w_out"] = _wgrad(ycat, dh1_mxu, 2 * width, d, True, "out_proj_dw")
    started = hooks.ffn_backward_done(dycat)
    mla_out_norm_w, ssm_out_norm_w = w["mla_out_norm_w"], w["ssm_out_norm_w"]
    if started is not None:
        mla_out_norm_w, ssm_out_norm_w = mla_out_norm_w + started[:1, :1], ssm_out_norm_w + started[:1, :1]

    do, g["mla_out_norm_w"] = _rmsnorm_bwd(o, mla_out_norm_w, dycat, name="mla_out_norm_bwd", width=width, dy_col=1)
    dq, dkv, dkpe_heads = _attn_bwd(qc, kc, vb, o, do, lse, cos, sin, scale=scale, tk=ATTN_BLOCK)
    dkpe = _kpe_bwd(dkpe_heads, cos, sin)
    g["mla_w_uq"] = _wgrad(qn, dq, qr, heads * HEAD_SLOT, False, "q_proj_dw")
    dqn = _matmul(dq, w["mla_w_uq"], mode="nt", name="q_proj_dx")
    dcq, g["mla_q_norm_w"] = _rmsnorm_bwd(proj, w["mla_q_norm_w"], dqn, name="q_norm_bwd", width=qr, col=q_col,
                                          dx_dtypes=(BF16,))
    g["mla_w_ukv"] = _wgrad(kvn, dkv, kvr, heads * HEAD_SLOT, False, "kv_proj_dw")
    dkvn = _matmul(dkv, w["mla_w_ukv"], mode="nt", name="kv_proj_dx")
    dckv, g["mla_kv_norm_w"] = _rmsnorm_bwd(proj, w["mla_kv_norm_w"], dkvn, name="kv_norm_bwd", width=kvr, col=kv_col,
                                            dx_dtypes=(BF16,))

    dz, dyg_a, g["ssm_out_norm_w"], g["ssm_b_glu"] = _glu_norm_bwd(y_pre, z, ssm_out_norm_w, dycat)
    dyg_b = _matmul(dz, later["ssm_w_glu"], mode="nt", name="glu_proj_dx")
    g["ssm_w_glu"] = _wgrad(yg, dz, width, width, True, "glu_proj_dw")
    du, dwb, dwc, dabar, g["ssm_d"] = _s5_bwd(proj, states, y_pre, dyg_a, dyg_b, wb, wc, w["ssm_d"], abar)

    def b_unband(x):
        return _block_diag_part(x, SSM_GROUP, SSM_STATE).transpose(0, 1, 3, 2).reshape(groups, SSM_STATE * SSM_GROUP)

    def c_unband(x):
        return _block_diag_part(x, SSM_STATE, SSM_GROUP).transpose(0, 1, 3, 2).reshape(groups, SSM_GROUP, SSM_STATE)

    st = STATE_PER_BATCH
    g["ssm_c_re"] = c_unband(dwc[:, :st, :])
    g["ssm_c_im"] = -c_unband(dwc[:, st:, :])
    d_ar = dabar[:, 0, :st].reshape(groups, SSM_STATE)
    d_ai = dabar[:, 0, st:].reshape(groups, SSM_STATE)
    (g["ssm_lambda_re"], g["ssm_lambda_im"], g["ssm_log_dt"], g["ssm_b_re"], g["ssm_b_im"]) = _s5_params_bwd(
        w["ssm_lambda_re"], w["ssm_lambda_im"], w["ssm_log_dt"], w["ssm_b_re"], w["ssm_b_im"], d_ar, d_ai,
        b_unband(dwb[:, :, :st]), b_unband(dwb[:, :, st:]))

    pad = jnp.zeros((rows, inp - (width + qr + kvr + LANES)), BF16)
    dproj = jnp.concatenate([du, dcq, dckv, dkpe, pad], axis=1)
    g["w_in"] = _wgrad(dproj, hn, inp, d, False, "in_proj_dw")
    dhn = _matmul(dproj, w["w_in"], mode="nn", name="in_proj_dx")
    dx, g["attn_norm_w"] = _rmsnorm_bwd(x, w["attn_norm_w"], dhn, name="attn_norm_bwd", add=dh1)
    return loss_tile, dx, g


ANY = pl.BlockSpec(memory_space=pl.ANY)
MESH = pl.DeviceIdType.MESH


def _mesh_pos():
    return lax.axis_index("x"), lax.axis_index("y"), lax.axis_index("c")


def _other_chips(x, y):
    return [(1 - x, y), (x, 1 - y), (1 - x, 1 - y)]


def _remote(src, dst, send_sems, recv_sems, k, to):
    return pltpu.make_async_remote_copy(src_ref=src, dst_ref=dst, send_sem=send_sems.at[k], recv_sem=recv_sems.at[k],
                                        device_id=to, device_id_type=MESH)


def _place_shard(shard, piece_idx, row_sharded, name, out_dtype=BF16, pieces=N_CHIPS):
    rs, cs = shard.shape
    tr = _tile(rs, 256, 2 * SUBLANES)
    rb = rs // tr

    def body(p_ref, x_ref, o_ref):
        o_ref[...] = x_ref[...].astype(o_ref.dtype)

    if row_sharded:
        out_shape, out_map = (pieces * rs, cs), (lambda i, p_ref: (p_ref[0] * rb + i, 0))
    else:
        out_shape, out_map = (rs, pieces * cs), (lambda i, p_ref: (i, p_ref[0]))
    return pl.pallas_call(
        body, name=name, out_shape=jax.ShapeDtypeStruct(out_shape, out_dtype),
        grid_spec=pltpu.PrefetchScalarGridSpec(
            num_scalar_prefetch=1, grid=(rb,), in_specs=[pl.BlockSpec((tr, cs), lambda i, p_ref: (i, 0))],
            out_specs=pl.BlockSpec((tr, cs), out_map)),
        compiler_params=_params(("parallel",)),
    )(piece_idx, shard)


def _gather_weights(placed, name):
    n = len(placed)
    meta = [(row_sharded, direct) for _, row_sharded, direct in placed]
    over_ici, over_d2d = _gather_plans(meta)
    forwarded = [t for t, (_, direct) in enumerate(meta) if not direct]

    def body(*refs):
        outs = refs[n:2 * n]
        send_sems, recv_sems, pass_send_sems, pass_recv_sems = refs[2 * n:]
        first, arrivals = over_ici(outs, send_sems, recv_sems)
        passed, passed_arrivals = over_d2d([outs[t] for t in forwarded], pass_send_sems, pass_recv_sems)
        for cp in first:
            cp.start()
        for t in range(n):
            for j in range(3):
                arrivals[3 * t + j].wait_recv()
                if t in forwarded:
                    passed[3 * forwarded.index(t) + j].start()
        for cp in passed_arrivals:
            cp.wait_recv()
        for cp in first + passed:
            cp.wait_send()

    return pl.pallas_call(
        body, name=name, in_specs=[ANY] * n, out_specs=[ANY] * n,
        out_shape=[jax.ShapeDtypeStruct(arr.shape, arr.dtype) for arr, _, _ in placed],
        input_output_aliases={t: t for t in range(n)},
        scratch_shapes=[pltpu.SemaphoreType.DMA((3 * n,)), pltpu.SemaphoreType.DMA((3 * n,)),
                        pltpu.SemaphoreType.DMA((3 * len(forwarded),)), pltpu.SemaphoreType.DMA((3 * len(forwarded),))],
    )(*[arr for arr, _, _ in placed])


def _gather_plans(meta):
    def window(ref, row_sharded, piece, half):
        r, cc = ref.shape
        if row_sharded:
            rs = r // N_CHIPS
            if half is None:
                return ref.at[pl.ds(piece * rs, rs), :]
            return ref.at[pl.ds(piece * rs + half * (rs // 2), rs // 2), :]
        cs = cc // N_CHIPS
        if half is None:
            return ref.at[:, pl.ds(piece * cs, cs)]
        return ref.at[pl.ds(half * (r // 2), r // 2), pl.ds(piece * cs, cs)]

    def over_ici(refs, send_sems, recv_sems):
        x, y, c = _mesh_pos()
        sends, recvs = [], []
        for t, (row_sharded, direct) in enumerate(meta):
            mine = window(refs[t], row_sharded, 2 * x + y, None if direct else c)
            for j, (px, py) in enumerate(_other_chips(x, y)):
                theirs = window(refs[t], row_sharded, 2 * px + py, None if direct else c)
                sends.append(_remote(mine, mine, send_sems, recv_sems, 3 * t + j, (px, py, c)))
                recvs.append(_remote(theirs, theirs, send_sems, recv_sems, 3 * t + j, (px, py, c)))
        return sends, recvs

    def over_d2d(refs, send_sems, recv_sems):
        x, y, c = _mesh_pos()
        sends, recvs = [], []
        rows = [row_sharded for row_sharded, direct in meta if not direct]
        for t, row_sharded in enumerate(rows):
            for j, (px, py) in enumerate(_other_chips(x, y)):
                got = window(refs[t], row_sharded, 2 * px + py, c)
                other = window(refs[t], row_sharded, 2 * px + py, 1 - c)
                sends.append(_remote(got, got, send_sems, recv_sems, 3 * t + j, (x, y, 1 - c)))
                recvs.append(_remote(other, other, send_sems, recv_sems, 3 * t + j, (x, y, 1 - c)))
        return sends, recvs

    return over_ici, over_d2d


HBM = pl.BlockSpec(memory_space=pltpu.HBM)
SEMAPHORES = pl.BlockSpec(memory_space=pltpu.SEMAPHORE)
DATAFLOW = pltpu.SideEffectType.DATAFLOW_SIDE_EFFECTING


def _start_copies(name, arrays, plan, n_copies, after):
    n = len(arrays)

    def body(*refs):
        sends, _ = plan(refs[:n], refs[n + 1], refs[n + 2])
        for cp in sends:
            cp.start()
        token = refs[2 * n + 3]
        token[...] = jnp.zeros_like(token)

    out = pl.pallas_call(
        body, name=name,
        out_shape=(pltpu.SemaphoreType.DMA((n_copies,)), pltpu.SemaphoreType.DMA((n_copies,)),
                   *[pltpu.HBM(a.shape, a.dtype) for a in arrays], jax.ShapeDtypeStruct((SUBLANES, LANES), F32)),
        in_specs=[HBM] * n + [ANY],
        out_specs=(SEMAPHORES, SEMAPHORES, *[HBM] * n, pl.BlockSpec(memory_space=pltpu.VMEM)),
        input_output_aliases={t: t + 2 for t in range(n)},
        compiler_params=pltpu.CompilerParams(has_side_effects=DATAFLOW),
    )(*[pltpu.with_memory_space_constraint(a, pltpu.HBM) for a in arrays], after)
    return out[0], out[1], list(out[2:2 + n]), out[2 + n]


def _wait_copies(name, started, plan, after):
    send_sems, recv_sems, arrays, _ = started
    n = len(arrays)

    def body(*refs):
        sends, recvs = plan(refs[:n], refs[n], refs[n + 1])
        for cp in sends:
            cp.wait_send()
        for cp in recvs:
            cp.wait_recv()

    out = pl.pallas_call(
        body, name=name, out_shape=[pltpu.HBM(a.shape, a.dtype) for a in arrays],
        in_specs=[HBM] * n + [SEMAPHORES, SEMAPHORES, ANY], out_specs=[HBM] * n,
        input_output_aliases={t: t for t in range(n)},
        compiler_params=pltpu.CompilerParams(has_side_effects=DATAFLOW),
    )(*arrays, send_sems, recv_sems, after)
    return list(out)


def _exchange(name, arrays, out_shapes, plan, n_copies, in_place=False, after=None):
    n = len(arrays)
    extra = [] if after is None else [after]

    def body(*refs):
        ins, outs = refs[:n], refs[n + len(extra):n + len(extra) + len(out_shapes)]
        send_sems, recv_sems = refs[n + len(extra) + len(out_shapes):]
        sends, recvs = plan(ins, outs, send_sems, recv_sems)
        for cp in sends:
            cp.start()
        for cp in recvs:
            cp.wait_recv()
        for cp in sends:
            cp.wait_send()

    return pl.pallas_call(
        body, name=name, in_specs=[ANY] * (n + len(extra)), out_specs=[ANY] * len(out_shapes), out_shape=out_shapes,
        input_output_aliases={t: t for t in range(n)} if in_place else {},
        scratch_shapes=[pltpu.SemaphoreType.DMA((n_copies,)), pltpu.SemaphoreType.DMA((n_copies,))],
    )(*arrays, *extra)


def _give_plan(n):
    def plan(refs, send_sems, recv_sems):
        x, y, c = _mesh_pos()
        sends = [_remote(refs[t], refs[n + t], send_sems, recv_sems, t, (x, y, 1 - c)) for t in range(n)]
        return sends, sends

    return plan


def _scatter_plan(n):
    def plan(refs, send_sems, recv_sems):
        x, y, c = _mesh_pos()
        sends = []
        for t in range(n):
            for j, (px, py) in enumerate(_other_chips(x, y)):
                sends.append(_remote(refs[t].at[2 * px + py], refs[n + t].at[j], send_sems, recv_sems, 3 * t + j, (px, py, c)))
        return sends, sends

    return plan


def _scatter_shapes(sums):
    return [jax.ShapeDtypeStruct((3,) + s.shape[1:], s.dtype) for s in sums]


def _join_halves(halves, name, after=None):
    def plan(ins, outs, send_sems, recv_sems):
        x, y, c = _mesh_pos()
        sends = [_remote(outs[t].at[c], outs[t].at[c], send_sems, recv_sems, t, (x, y, 1 - c)) for t in range(len(ins))]
        recvs = [_remote(outs[t].at[1 - c], outs[t].at[1 - c], send_sems, recv_sems, t, (x, y, 1 - c))
                 for t in range(len(ins))]
        return sends, recvs

    shapes = [jax.ShapeDtypeStruct(h.shape, h.dtype) for h in halves]
    return _exchange(name, halves, shapes, plan, len(halves), in_place=True, after=after)


def _add_other_half(g4, got, where, name, wire_dtype=BF16):
    _, pieces, sr, sc = g4.shape
    tr = _tile(sr, 256, 2 * SUBLANES)

    def body(w_ref, a_ref, b_ref, o_ref):
        o_ref[...] = (a_ref[...] + b_ref[...]).astype(o_ref.dtype)

    blk = pl.BlockSpec((None, tr, sc), lambda p, i, w_ref: (p, i, 0))
    return pl.pallas_call(
        body, name=name, out_shape=jax.ShapeDtypeStruct((pieces, sr, sc), wire_dtype),
        grid_spec=pltpu.PrefetchScalarGridSpec(
            num_scalar_prefetch=1, grid=(pieces, sr // tr),
            in_specs=[pl.BlockSpec((None, None, tr, sc), lambda p, i, w_ref: (w_ref[0], p, i, 0)), blk], out_specs=blk),
        compiler_params=_params(("parallel", "parallel")),
    )(where, g4, got)


def _add_pieces(sums, got_pieces, where, name):
    _, sr, sc = sums.shape
    tr = _tile(sr, 256, 2 * SUBLANES)

    def body(w_ref, a_ref, r_ref, o_ref):
        acc = a_ref[...]
        for j in range(3):
            acc = acc + r_ref[j].astype(F32)
        o_ref[...] = acc

    return pl.pallas_call(
        body, name=name, out_shape=jax.ShapeDtypeStruct((N_CORES, sr, sc), F32),
        grid_spec=pltpu.PrefetchScalarGridSpec(
            num_scalar_prefetch=1, grid=(sr // tr,),
            in_specs=[pl.BlockSpec((None, tr, sc), lambda i, w_ref: (w_ref[1], i, 0)),
                      pl.BlockSpec((3, tr, sc), lambda i, w_ref: (0, i, 0))],
            out_specs=pl.BlockSpec((None, tr, sc), lambda i, w_ref: (w_ref[0], i, 0))),
        compiler_params=_params(("parallel",)),
    )(where, sums, got_pieces)


def _adamw_update(w, g, m, v):
    nm = ADAM_B1 * m + (1.0 - ADAM_B1) * g
    nv = ADAM_B2 * v + (1.0 - ADAM_B2) * (g * g)
    m_hat = nm / (1.0 - ADAM_B1 ** ADAM_STEP)
    v_hat = nv / (1.0 - ADAM_B2 ** ADAM_STEP)
    return -ADAM_LR * (m_hat / (jnp.sqrt(v_hat) + ADAM_EPS) + ADAM_WD * w), nm, nv


def _adamw(w, g, m, v, name, after=None):
    rows, cols = w.shape
    halves = 2 if g.ndim == 3 else 1
    bc = cols // halves
    tr = _tile(rows, max(SUBLANES, (1 << 19) // max(bc, 1) // SUBLANES * SUBLANES), SUBLANES)

    def body(w_ref, g_ref, m_ref, v_ref, *rest):
        d_ref, nm_ref, nv_ref, go_ref = rest[-4:]
        gv = g_ref[...]
        d_ref[...], nm_ref[...], nv_ref[...] = _adamw_update(w_ref[...], gv, m_ref[...], v_ref[...])
        go_ref[...] = gv

    blk = pl.BlockSpec((tr, bc), lambda i, h: (i, h))
    g_blk = pl.BlockSpec((None, tr, bc), lambda i, h: (h, i, 0)) if halves == 2 else blk
    extra = [] if after is None else [after]
    return pl.pallas_call(
        body, name=name, grid=(rows // tr, halves),
        in_specs=[blk, g_blk, blk, blk] + [pl.BlockSpec(memory_space=pl.ANY)] * len(extra), out_specs=[blk] * 4,
        out_shape=[jax.ShapeDtypeStruct((rows, cols), F32)] * 4, compiler_params=_params(("parallel", "parallel")),
    )(w, g, m, v, *extra)


def _adamw_many(ws, gs, ms, vs, name):
    n = len(ws)

    def body(*refs):
        outs = refs[4 * n:]
        for k in range(n):
            w_ref, g_ref, m_ref, v_ref = (refs[j * n + k] for j in range(4))
            outs[k][...], outs[n + k][...], outs[2 * n + k][...] = _adamw_update(w_ref[...], g_ref[...], m_ref[...], v_ref[...])

    out = pl.pallas_call(
        body, name=name, out_shape=[jax.ShapeDtypeStruct(w.shape, F32) for w in ws] * 3,
        compiler_params=pltpu.CompilerParams(vmem_limit_bytes=VMEM_LIMIT_BYTES),
    )(*ws, *gs, *ms, *vs)
    return out[:n], out[n:2 * n], out[2 * n:]


WEIGHTS = ['attn_norm_w', 'w_in', 'ssm_lambda_re', 'ssm_lambda_im', 'ssm_log_dt', 'ssm_b_re', 'ssm_b_im', 'ssm_c_re',
           'ssm_c_im', 'ssm_d', 'ssm_w_glu', 'ssm_b_glu', 'mla_q_norm_w', 'mla_w_uq', 'mla_kv_norm_w', 'mla_w_ukv',
           'ssm_out_norm_w', 'mla_out_norm_w', 'w_out', 'ffn_norm_w', 'ffn_w_up', 'ffn_conv_w', 'ffn_conv_b',
           'ffn_w_down', 'final_norm_w']
SHARDED = {'w_in': False, 'ssm_w_glu': True, 'mla_w_uq': False, 'mla_w_ukv': False, 'w_out': True, 'ffn_w_up': False,
           'ffn_w_down': True}
SMALL = [n for n in WEIGHTS if n not in SHARDED and n != 'ffn_conv_w']
ROPE_PAD = HEAD_SLOT - QK_NOPE_DIM - QK_ROPE_DIM
SMALL_COLS = 8 * LANES


def _pad_heads(w_uq, heads):
    qr = w_uq.shape[0]
    w3 = w_uq.reshape(qr, heads, QK_NOPE_DIM + QK_ROPE_DIM)
    return jnp.concatenate([w3, jnp.zeros((qr, heads, ROPE_PAD), w_uq.dtype)], axis=2).reshape(qr, heads * HEAD_SLOT)


def _unpad_heads(g_uq, heads):
    qr = g_uq.shape[0]
    return g_uq.reshape(qr, heads, HEAD_SLOT)[:, :, :QK_NOPE_DIM + QK_ROPE_DIM].reshape(qr, -1)


FFN = ['ffn_w_up', 'ffn_w_down']
MIXER_LATER = ['ssm_w_glu', 'w_out']
FFN_GATHER = FFN + ['ffn_conv_w']
FFN_GATHER_META = [(SHARDED[n], False) for n in FFN] + [(False, True)]


class _Overlapped:
    def __init__(self, placed_later, placed, where, after):
        self.where, self.mine, self.other = where, where[:1], 1 - where[:1]
        self.later_ici, self.later_d2d = _gather_plans([(SHARDED[n], False) for n in MIXER_LATER])
        self.later = _start_copies("gather_later_start", placed_later, self.later_ici, 3 * len(placed_later), after)
        up, down, taps = placed
        self.up_ici, self.up_d2d = _gather_plans([(SHARDED["ffn_w_up"], False)])
        self.up = _start_copies("gather_ffn_up_start", [up], self.up_ici, 3, self.later[3])
        self.down_ici, self.down_d2d = _gather_plans([(SHARDED["ffn_w_down"], False), (False, True)])
        self.down = _start_copies("gather_ffn_down_start", [down, taps], self.down_ici, 6, self.up[3])
        self.gather_started = self.down[3]

    def mixer_weights(self, after):
        arrived = _wait_copies("gather_later_wait", self.later, self.later_ici, after)
        shapes = [jax.ShapeDtypeStruct(a.shape, a.dtype) for a in arrived]
        passed = _exchange("gather_later_pass", arrived, shapes, lambda ins, outs, s, r: self.later_d2d(outs, s, r),
                           3 * len(arrived), in_place=True)
        return dict(zip(MIXER_LATER, passed))

    def ffn_weights_arrived(self, after):
        arrived = _wait_copies("gather_ffn_up_wait", self.up, self.up_ici, after)
        self.up_passing = _start_copies("gather_ffn_up_pass_start", arrived, self.up_d2d, 3, after)
        return self.up_passing[3]

    def ffn_weights(self, after):
        w_up, = _wait_copies("gather_ffn_up_pass_wait", self.up_passing, self.up_d2d, after)
        down, taps = _wait_copies("gather_ffn_down_wait", self.down, self.down_ici, after)
        self.down_passing = _start_copies("gather_ffn_down_pass_start", [down], self.down_d2d, 3, w_up)
        return {"ffn_w_up": w_up, "ffn_conv_w": taps, "started": self.down_passing[3]}

    def ffn_down_weight(self, after):
        return _wait_copies("gather_ffn_down_pass_wait", self.down_passing, self.down_d2d, after)[0]

    def ffn_grads(self, makers, after):
        self.makers = [makers[name] for name in FFN]
        n = len(FFN)
        give = [make(self.other, suffix="_give") for make in self.makers]
        lands = [lax.empty(g.shape, g.dtype) for g in give]
        self.swap = _start_copies("grad_ffn_swap_start", give + lands, _give_plan(n), n, after)
        return self.swap[3]

    def ffn_backward_done(self, after):
        n = len(FFN)
        got = _wait_copies("grad_ffn_swap_wait", self.swap, _give_plan(n), after)[n:]
        kept = [make(self.mine, suffix="_keep", add=got[t], wire=True) for t, make in enumerate(self.makers)]
        self.sums = [k[0] for k in kept]
        wires = [k[1] for k in kept]
        lands = [lax.empty(s.shape, s.dtype) for s in _scatter_shapes(wires)]
        self.scatter = _start_copies("grad_ffn_scatter_start", wires + lands, _scatter_plan(n), 3 * n, after)
        return self.scatter[3]

    def ffn_reduced(self, after):
        n = len(FFN)
        got_pieces = _wait_copies("grad_ffn_scatter_wait", self.scatter, _scatter_plan(n), after)[n:]
        return [_add_pieces(self.sums[t], got_pieces[t], self.where, "grad_add_pieces_" + name) for t, name in enumerate(FFN)]


def _step(args):
    x, positions, target = args["x"][0], args["positions"], args["loss_target"][0]
    rows = x.shape[0]
    p = {n: args[n] for n in WEIGHTS}
    xi, yi, ci = _mesh_pos()
    piece = 2 * xi + yi

    def transposed(a):
        return jnp.swapaxes(a[0], 0, 1)

    w_in = transposed(p["w_in"])
    in_width = w_in.shape[0]
    in_pad = (-in_width) % (2 * LANES)
    heads_here = p["mla_w_uq"].shape[2] // (QK_NOPE_DIM + QK_ROPE_DIM)
    shards = {
        "w_in": jnp.pad(w_in, ((0, in_pad), (0, 0))),
        "ssm_w_glu": p["ssm_w_glu"][0],
        "mla_w_uq": _pad_heads(p["mla_w_uq"][0], heads_here),
        "mla_w_ukv": p["mla_w_ukv"][0],
        "w_out": p["w_out"][0],
        "ffn_w_up": p["ffn_w_up"][0],
        "ffn_w_down": p["ffn_w_down"][0],
    }
    conv_w = jnp.pad(p["ffn_conv_w"][0], ((0, SUBLANES - p["ffn_conv_w"].shape[1]), (0, 0)))
    order = list(SHARDED)
    piece_idx = piece.reshape(1).astype(jnp.int32)
    placed = {n: _place_shard(shards[n], piece_idx, SHARDED[n], "place_" + n) for n in order}
    placed["ffn_conv_w"] = _place_shard(conv_w, piece_idx, False, "place_ffn_conv_w", out_dtype=F32)
    mixer = [n for n in order if n not in FFN]
    first = [n for n in mixer if n not in MIXER_LATER]
    w = dict(zip(first, _gather_weights([(placed[n], SHARDED[n], False) for n in first], "gather_first_weights")))
    where = jnp.stack([ci, piece]).astype(jnp.int32)
    hooks = _Overlapped([placed[n] for n in MIXER_LATER], [placed[n] for n in FFN_GATHER], where, after=w["w_in"])
    groups = p["ssm_lambda_re"].shape[1]
    w.update({
        "attn_norm_w": p["attn_norm_w"] + hooks.gather_started[:1, :1],
        "ssm_lambda_re": p["ssm_lambda_re"][0], "ssm_lambda_im": p["ssm_lambda_im"][0],
        "ssm_log_dt": p["ssm_log_dt"].reshape(groups, 1), "ssm_b_re": p["ssm_b_re"].reshape(groups, -1),
        "ssm_b_im": p["ssm_b_im"].reshape(groups, -1), "ssm_c_re": p["ssm_c_re"][0], "ssm_c_im": p["ssm_c_im"][0],
        "ssm_d": p["ssm_d"], "ssm_b_glu": p["ssm_b_glu"], "mla_q_norm_w": p["mla_q_norm_w"],
        "mla_kv_norm_w": p["mla_kv_norm_w"], "ssm_out_norm_w": p["ssm_out_norm_w"], "mla_out_norm_w": p["mla_out_norm_w"],
        "ffn_norm_w": p["ffn_norm_w"], "ffn_conv_b": p["ffn_conv_b"], "final_norm_w": p["final_norm_w"].reshape(1, -1),
    })

    loss_tile, dx, g = _local_step(x, positions.reshape(rows, 1).astype(F32), target, w, hooks)
    loss = lax.psum(loss_tile[0, 0], ("x", "y", "c"))

    flat = [g[n].reshape(-1) for n in SMALL] + [g["ffn_conv_w"].reshape(-1)]
    sizes = [f.shape[0] for f in flat]
    per_block = -(-sum(sizes) // (N_CORES * N_CHIPS * SMALL_COLS))
    small_rows = -(-per_block // (2 * SUBLANES)) * (2 * SUBLANES)
    padded = N_CORES * N_CHIPS * small_rows * SMALL_COLS

    def pack(parts):
        parts = list(parts)
        have = sum(q.shape[0] for q in parts)
        return jnp.concatenate(parts + [jnp.zeros((padded - have,), F32)])

    reduced = mixer + ["small"]
    small = pack(flat).reshape(N_CORES, N_CHIPS, small_rows, SMALL_COLS)
    give = [g[n](hooks.other, suffix="_give") for n in mixer] + [lax.dynamic_index_in_dim(small, 1 - ci, 0, keepdims=False)]
    lands = [lax.empty(a.shape, a.dtype) for a in give]
    give_plan = _give_plan(len(reduced))
    swap = _start_copies("grad_mixer_swap_start", give + lands, give_plan, len(reduced), dx)

    grads, delta, new_m, new_v = {}, {}, {}, {}

    def finish(n, joined, after=None):
        grad = joined if SHARDED[n] else joined.reshape(-1, joined.shape[2])
        if n == "w_in":
            wt, mt, vt = w_in, transposed(args["m_w_in"]), transposed(args["v_w_in"])
            out = _adamw(wt, grad, mt, vt, "adamw_w_in")
            delta[n], new_m[n], new_v[n], grads[n] = (jnp.swapaxes(a, 0, 1)[None] for a in out)
            return
        if n == "mla_w_uq":
            grad = _unpad_heads(grad, heads_here)
        adam(n, grad, after)

    def adam(n, grad, after=None):
        shape = p[n].shape
        out = _adamw(p[n].reshape(shape[1:]), grad, args["m_" + n].reshape(shape[1:]),
                     args["v_" + n].reshape(shape[1:]), "adamw_" + n, after)
        delta[n], new_m[n], new_v[n], grads[n] = (a.reshape(shape) for a in out)

    ffn_joined = _join_halves(hooks.ffn_reduced(dx), "grad_ffn_join_halves", after=swap[3])
    got = _wait_copies("grad_mixer_swap_wait", swap, give_plan, ffn_joined[0])[len(reduced):]
    kept = [g[n](hooks.mine, suffix="_keep", add=got[t], wire=True) for t, n in enumerate(mixer)]
    small_sum = _add_other_half(small, got[-1], where, "grad_add_half_small", F32)
    sums = [k[0] for k in kept] + [small_sum]
    wires = [k[1] for k in kept] + [small_sum]
    lands = [lax.empty(s.shape, s.dtype) for s in _scatter_shapes(wires)]
    scatter_plan = _scatter_plan(len(reduced))
    scatter = _start_copies("grad_mixer_scatter_start", wires + lands, scatter_plan, 3 * len(reduced), kept[0][0])
    behind = scatter[3]
    for n, joined in zip(FFN, ffn_joined):
        finish(n, joined, after=behind)
        behind = delta[n]
    got_pieces = _wait_copies("grad_mixer_scatter_wait", scatter, scatter_plan, delta[FFN[-1]])[len(reduced):]
    halves = [_add_pieces(sums[t], got_pieces[t], where, "grad_add_pieces_" + n) for t, n in enumerate(reduced)]
    joined = _join_halves(halves, "grad_join_halves")
    for n, j in zip(mixer, joined):
        finish(n, j)
    eighths = _place_shard(joined[-1].reshape(N_CORES * small_rows, SMALL_COLS), piece_idx, True, "place_small_grads",
                           out_dtype=F32)
    small_sum = _gather_weights([(eighths, True, False)], "gather_small_grads")[0]
    flat_sum = small_sum.reshape(N_CHIPS, N_CORES, small_rows * SMALL_COLS).transpose(1, 0, 2).reshape(-1)
    offs = [0]
    for s in sizes:
        offs.append(offs[-1] + s)
    for k, n in enumerate(SMALL):
        grads[n] = flat_sum[offs[k]:offs[k + 1]].reshape(p[n].shape)
    taps, cols_here = p["ffn_conv_w"].shape[1], p["ffn_conv_w"].shape[2]
    conv_full = flat_sum[offs[len(SMALL)]:offs[len(SMALL) + 1]].reshape(taps, N_CHIPS * cols_here)
    adam("ffn_conv_w", lax.dynamic_slice_in_dim(conv_full, piece * cols_here, cols_here, axis=1))

    def rank2(a):
        return a.reshape(1, -1) if a.ndim == 1 else a

    d_s, m_s, v_s = _adamw_many([rank2(p[n]) for n in SMALL], [rank2(grads[n]) for n in SMALL],
                                [rank2(args["m_" + n]) for n in SMALL], [rank2(args["v_" + n]) for n in SMALL], "adamw_small")
    for k, n in enumerate(SMALL):
        delta[n], new_m[n], new_v[n] = (a.reshape(p[n].shape) for a in (d_s[k], m_s[k], v_s[k]))

    return (loss, dx[None], *[grads[n] for n in WEIGHTS], *[delta[n] for n in WEIGHTS],
            *[new_m[n] for n in WEIGHTS], *[new_v[n] for n in WEIGHTS])


def kernel(x, positions, attn_norm_w, w_in, ssm_lambda_re, ssm_lambda_im, ssm_log_dt, ssm_b_re, ssm_b_im, ssm_c_re, ssm_c_im, ssm_d, ssm_w_glu, ssm_b_glu, mla_q_norm_w, mla_w_uq, mla_kv_norm_w, mla_w_ukv, ssm_out_norm_w, mla_out_norm_w, w_out, ffn_norm_w, ffn_w_up, ffn_conv_w, ffn_conv_b, ffn_w_down, final_norm_w, loss_target, m_attn_norm_w, m_w_in, m_ssm_lambda_re, m_ssm_lambda_im, m_ssm_log_dt, m_ssm_b_re, m_ssm_b_im, m_ssm_c_re, m_ssm_c_im, m_ssm_d, m_ssm_w_glu, m_ssm_b_glu, m_mla_q_norm_w, m_mla_w_uq, m_mla_kv_norm_w, m_mla_w_ukv, m_ssm_out_norm_w, m_mla_out_norm_w, m_w_out, m_ffn_norm_w, m_ffn_w_up, m_ffn_conv_w, m_ffn_conv_b, m_ffn_w_down, m_final_norm_w, v_attn_norm_w, v_w_in, v_ssm_lambda_re, v_ssm_lambda_im, v_ssm_log_dt, v_ssm_b_re, v_ssm_b_im, v_ssm_c_re, v_ssm_c_im, v_ssm_d, v_ssm_w_glu, v_ssm_b_glu, v_mla_q_norm_w, v_mla_w_uq, v_mla_kv_norm_w, v_mla_w_ukv, v_ssm_out_norm_w, v_mla_out_norm_w, v_w_out, v_ffn_norm_w, v_ffn_w_up, v_ffn_conv_w, v_ffn_conv_b, v_ffn_w_down, v_final_norm_w):
    return _step(dict(locals()))
```

```python
import functools
import math

import jax
import jax.numpy as jnp
from jax import lax
from jax.experimental import pallas as pl
from jax.experimental.pallas import tpu as pltpu

F32 = jnp.float32
BF16 = jnp.bfloat16

SSM_GROUP = 16
SSM_STATE = 64
QK_NOPE_DIM = 128
QK_ROPE_DIM = 64
V_HEAD_DIM = 128
ROPE_THETA = 10000.0
RMS_EPS = 1e-6
ADAM_LR, ADAM_B1, ADAM_B2, ADAM_EPS, ADAM_WD, ADAM_STEP = 0.001, 0.9, 0.999, 1e-08, 0.01, 10

LANES = 128
SUBLANES = 8
VMEM_LIMIT_BYTES = 56 * 1024 * 1024

GROUPS_PER_BATCH = LANES // SSM_GROUP
STATE_PER_BATCH = GROUPS_PER_BATCH * SSM_STATE
HEAD_SLOT = 2 * LANES
NEG_INF = -1e30
ATTN_BLOCK = 512
FFN_ROWS = 1024

N_CHIPS = 4
N_CORES = 2


def _tile(n, pref, align=LANES):
    if n <= pref:
        return n
    t = (pref // align) * align
    while t >= align:
        if n % t == 0:
            return t
        t -= align
    return n


def _params(sem):
    return pltpu.CompilerParams(dimension_semantics=sem, vmem_limit_bytes=VMEM_LIMIT_BYTES)


def _dot(a, b, dims):
    return lax.dot_general(a, b, (dims, ((), ())), preferred_element_type=F32)


def _dot_nn(a, b):
    return _dot(a, b, ((1,), (0,)))


def _dot_nt(a, b):
    return _dot(a, b, ((1,), (1,)))


def _dot_tn(a, b):
    return _dot(a, b, ((0,), (0,)))


def _matmul(a, b, *, mode, name, tm=512, tn=1024, tk=2048, bias=None, add=None, out_dtype=F32,
            out_blocks=None, a_split=False, b_split=False, after=None):
    if a_split:
        assert mode == "nt"
        a_shape = (a.shape[1], 2 * a.shape[2])
    else:
        a_shape = a.shape
    if b_split:
        assert mode == "tn"
        b_shape = (b.shape[1], 2 * b.shape[2])
    else:
        b_shape = b.shape
    if mode == "nn":
        (m, k), (k2, n) = a_shape, b_shape
    elif mode == "nt":
        (m, k), (n, k2) = a_shape, b_shape
    else:
        (k, m), (k2, n) = a_shape, b_shape
    assert k == k2, (a.shape, b.shape, mode)
    tm, tn, tk = _tile(m, tm, SUBLANES), _tile(n, tn), _tile(k, tk)
    nk = k // tk
    a_spec = {"nn": pl.BlockSpec((tm, tk), lambda i, j, kk: (i, kk)),
              "nt": pl.BlockSpec((tm, tk), lambda i, j, kk: (i, kk)),
              "tn": pl.BlockSpec((tk, tm), lambda i, j, kk: (kk, i))}[mode]
    b_spec = {"nn": pl.BlockSpec((tk, tn), lambda i, j, kk: (kk, j)),
              "nt": pl.BlockSpec((tn, tk), lambda i, j, kk: (j, kk)),
              "tn": pl.BlockSpec((tk, tn), lambda i, j, kk: (kk, j))}[mode]
    if a_split:
        kb = a.shape[2] // tk
        assert a.shape[2] % tk == 0
        a_spec = pl.BlockSpec((None, tm, tk), lambda i, j, kk: (kk // kb, i, kk % kb))
    if b_split:
        nb = b.shape[2] // tn
        assert b.shape[2] % tn == 0
        b_spec = pl.BlockSpec((None, tk, tn), lambda i, j, kk: (j // nb, kk, j % nb))
    dot = {"nn": _dot_nn, "nt": _dot_nt, "tn": _dot_tn}[mode]
    in_specs, operands = [a_spec, b_spec], [a, b]
    if bias is not None:
        in_specs.append(pl.BlockSpec((1, tn), lambda i, j, kk: (0, j)))
        operands.append(bias)
    if add is not None:
        in_specs.append(pl.BlockSpec((tm, tn), lambda i, j, kk: (i, j)))
        operands.append(add)
    if after is not None:
        in_specs.append(pl.BlockSpec(memory_space=pl.ANY))
        operands.append(after)

    def body(*refs):
        a_ref, b_ref = refs[0], refs[1]
        rest = list(refs[2:])
        bias_ref = rest.pop(0) if bias is not None else None
        add_ref = rest.pop(0) if add is not None else None
        if after is not None:
            rest.pop(0)
        o_ref, acc_ref = rest

        def finish(acc):
            if bias_ref is not None:
                acc = acc + bias_ref[...]
            if add_ref is not None:
                acc = acc + add_ref[...]
            o_ref[...] = acc.astype(o_ref.dtype)

        part = dot(a_ref[...].astype(BF16), b_ref[...].astype(BF16))
        if nk == 1:
            finish(part)
        else:
            kk = pl.program_id(2)

            @pl.when(kk == 0)
            def _():
                acc_ref[...] = part

            @pl.when(jnp.logical_and(kk > 0, kk < nk - 1))
            def _():
                acc_ref[...] += part

            @pl.when(kk == nk - 1)
            def _():
                finish(acc_ref[...] + part)

    if out_blocks is None:
        out_shape = jax.ShapeDtypeStruct((m, n), out_dtype)
        out_spec = pl.BlockSpec((tm, tn), lambda i, j, kk: (i, j))
    else:
        shape, block, index_map = out_blocks(tm, tn)
        out_shape = jax.ShapeDtypeStruct(shape, out_dtype)
        out_spec = pl.BlockSpec(block, index_map)
    acc_shape = (tm, tn) if nk > 1 else (SUBLANES, LANES)
    return pl.pallas_call(
        body, name=name, grid=(m // tm, n // tn, nk), in_specs=in_specs, out_specs=out_spec, out_shape=out_shape,
        scratch_shapes=[pltpu.VMEM(acc_shape, F32)],
        compiler_params=_params(("parallel", "parallel", "arbitrary")),
    )(*operands)


def _wgrad_half(a, b, rows, cols, row_sharded, name, which, *, suffix="", add=None, wire=False, tm=None, tn=None,
                b_split=False):
    tokens = a.shape[0]
    if row_sharded:
        sr, sc = rows // N_CHIPS, cols // N_CORES
    else:
        sr, sc = rows // N_CORES, cols // N_CHIPS
    tm = _tile(sr, 512) if tm is None else tm
    tn = _tile(sc, 1024) if tn is None else tn
    assert sr % tm == 0 and sc % tn == 0, (rows, cols, tm, tn)
    rb, cb = sr // tm, sc // tn
    if tn >= tm:
        ij, grid = (lambda s, t: (t, s)), (N_CHIPS, cb, rb)
    else:
        ij, grid = (lambda s, t: (s, t)), (N_CHIPS, rb, cb)
    if row_sharded:
        a_tile = lambda p, i, j, h: p * rb + i
        b_tile = lambda p, i, j, h: h[0] * cb + j
    else:
        a_tile = lambda p, i, j, h: h[0] * rb + i
        b_tile = lambda p, i, j, h: p * cb + j
    a_spec = pl.BlockSpec((tokens, tm), lambda p, s, t, h: (0, a_tile(p, *ij(s, t), h)))
    if b_split:
        nbh = b.shape[2] // tn
        assert b.shape[2] % tn == 0
        b_spec = pl.BlockSpec((None, tokens, tn), lambda p, s, t, h: (b_tile(p, *ij(s, t), h) // nbh, 0,
                                                                       b_tile(p, *ij(s, t), h) % nbh))
    else:
        b_spec = pl.BlockSpec((tokens, tn), lambda p, s, t, h: (0, b_tile(p, *ij(s, t), h)))
    out_spec = pl.BlockSpec((None, tm, tn), lambda p, s, t, h: (p, *ij(s, t)))
    in_specs, operands = [a_spec, b_spec], [a, b]
    if add is not None:
        in_specs.append(out_spec)
        operands.append(add)

    def body(h_ref, a_ref, b_ref, *rest):
        acc = _dot_tn(a_ref[...].astype(BF16), b_ref[...].astype(BF16))
        if add is not None:
            acc = acc + rest[0][...]
        for o_ref in rest[1 if add is not None else 0:]:
            o_ref[...] = acc.astype(o_ref.dtype)

    out_dtypes = [F32, BF16] if wire else [F32]
    out = pl.pallas_call(
        body, name=name + suffix, out_shape=[jax.ShapeDtypeStruct((N_CHIPS, sr, sc), dt) for dt in out_dtypes],
        grid_spec=pltpu.PrefetchScalarGridSpec(num_scalar_prefetch=1, grid=grid, in_specs=in_specs,
                                               out_specs=[out_spec] * len(out_dtypes)),
        compiler_params=_params(("parallel", "parallel", "parallel")),
    )(which, *operands)
    return tuple(out) if wire else out[0]


def _rms_rows(x):
    return lax.rsqrt(jnp.mean(x * x, axis=-1, keepdims=True) + RMS_EPS)


def _rmsnorm_fwd(x, w, *, name, width=None, col=0, out_dtype=BF16, tr=256):
    rows = x.shape[0]
    width = x.shape[1] if width is None else width
    tr = _tile(rows, tr, SUBLANES)

    def body(x_ref, w_ref, o_ref):
        xv = x_ref[...]
        o_ref[...] = (xv * _rms_rows(xv) * w_ref[...]).astype(o_ref.dtype)

    return pl.pallas_call(
        body, name=name, grid=(rows // tr,),
        in_specs=[pl.BlockSpec((tr, width), lambda i: (i, col)), pl.BlockSpec((1, width), lambda i: (0, 0))],
        out_specs=pl.BlockSpec((tr, width), lambda i: (i, 0)),
        out_shape=jax.ShapeDtypeStruct((rows, width), out_dtype),
        compiler_params=_params(("parallel",)),
    )(x, w)


def _rmsnorm_bwd_rows(xv, w, dy):
    r = _rms_rows(xv)
    n = xv * r
    dn = dy * w
    dx = r * (dn - n * jnp.mean(dn * n, axis=-1, keepdims=True))
    return dx, dy * n


def _rmsnorm_bwd(x, w, dy, *, name, width=None, col=0, dy_col=0, add=None, tr=256, dx_dtypes=(F32,)):
    rows = x.shape[0]
    n_dx = len(dx_dtypes)
    width = x.shape[1] if width is None else width
    tr = _tile(rows, tr, SUBLANES)
    in_specs = [pl.BlockSpec((tr, width), lambda i: (i, col)), pl.BlockSpec((1, width), lambda i: (0, 0)),
                pl.BlockSpec((tr, width), lambda i: (i, dy_col))]
    operands = [x, w, dy]
    if add is not None:
        in_specs.append(pl.BlockSpec((tr, width), lambda i: (i, 0)))
        operands.append(add)

    def body(*refs):
        x_ref, w_ref, dy_ref = refs[:3]
        add_ref = refs[3] if add is not None else None
        dx_refs, dw_ref = refs[-1 - n_dx:-1], refs[-1]
        dx, dwp = _rmsnorm_bwd_rows(x_ref[...], w_ref[...], dy_ref[...])
        if add_ref is not None:
            dx = dx + add_ref[...]
        for dx_ref in dx_refs:
            dx_ref[...] = dx.astype(dx_ref.dtype)
        part = jnp.sum(dwp, axis=0, keepdims=True)

        @pl.when(pl.program_id(0) == 0)
        def _():
            dw_ref[...] = part

        @pl.when(pl.program_id(0) > 0)
        def _():
            dw_ref[...] += part

    return pl.pallas_call(
        body, name=name, grid=(rows // tr,), in_specs=in_specs,
        out_specs=[pl.BlockSpec((tr, width), lambda i: (i, 0))] * n_dx + [pl.BlockSpec((1, width), lambda i: (0, 0))],
        out_shape=[jax.ShapeDtypeStruct((rows, width), dt) for dt in dx_dtypes] + [jax.ShapeDtypeStruct((1, width), F32)],
        compiler_params=_params(("arbitrary",)),
    )(*operands)


def _final_norm_loss(h, w, target, *, tr=256):
    rows, d = h.shape
    tr = _tile(rows, tr, SUBLANES)

    def body(h_ref, w_ref, t_ref, loss_ref, dh_ref, dhb_ref, dw_ref):
        hv, wv = h_ref[...], w_ref[...]
        r = _rms_rows(hv)
        n = hv * r
        err = n * wv - t_ref[...]
        d_out = err * (1.0 / d)
        dn = d_out * wv
        dh = r * (dn - n * jnp.mean(dn * n, axis=-1, keepdims=True))
        dh_ref[...] = dh
        dhb_ref[...] = dh.astype(BF16)
        dw_part = jnp.sum(d_out * n, axis=0, keepdims=True)
        loss_part = jnp.full((SUBLANES, LANES), 0.5 / d, F32) * jnp.sum(err * err)

        @pl.when(pl.program_id(0) == 0)
        def _():
            dw_ref[...] = dw_part
            loss_ref[...] = loss_part

        @pl.when(pl.program_id(0) > 0)
        def _():
            dw_ref[...] += dw_part
            loss_ref[...] += loss_part

    return pl.pallas_call(
        body, name="final_norm_loss", grid=(rows // tr,),
        in_specs=[pl.BlockSpec((tr, d), lambda i: (i, 0)), pl.BlockSpec((1, d), lambda i: (0, 0)),
                  pl.BlockSpec((tr, d), lambda i: (i, 0))],
        out_specs=[pl.BlockSpec((SUBLANES, LANES), lambda i: (0, 0)), pl.BlockSpec((tr, d), lambda i: (i, 0)),
                   pl.BlockSpec((tr, d), lambda i: (i, 0)), pl.BlockSpec((1, d), lambda i: (0, 0))],
        out_shape=[jax.ShapeDtypeStruct((SUBLANES, LANES), F32), jax.ShapeDtypeStruct((rows, d), F32),
                   jax.ShapeDtypeStruct((rows, d), BF16), jax.ShapeDtypeStruct((1, d), F32)],
        compiler_params=_params(("arbitrary",)),
    )(h, w, target)


def _cmul(ar, ai, br, bi):
    return ar * br - ai * bi, ar * bi + ai * br


def _expand_matrix(groups, reps):
    row = lax.broadcasted_iota(jnp.int32, (groups, groups * reps), 0)
    colg = lax.broadcasted_iota(jnp.int32, (groups, groups * reps), 1) // reps
    return (row == colg).astype(F32)


def _dot_exact(a, b, dims):
    return lax.dot_general(a, b, (dims, ((), ())), preferred_element_type=F32, precision=lax.Precision.HIGHEST)


def _s5_discretize(lr, li, dt):
    mag = jnp.exp(lr * dt)
    th = li * dt
    ar, ai = mag * jnp.cos(th), mag * jnp.sin(th)
    nr, ni = ar - 1.0, ai
    den = lr * lr + li * li
    zr = (nr * lr + ni * li) / den
    zi = (ni * lr - nr * li) / den
    return mag, ar, ai, nr, ni, den, zr, zi


def _s5_params(lam_re, lam_im, log_dt, b_re, b_im):
    g, p = lam_re.shape
    ph = b_re.shape[1]

    def body(lr_ref, li_ref, ldt_ref, br_ref, bi_ref, ar_ref, ai_ref, bbr_ref, bbi_ref):
        dt = jnp.exp(ldt_ref[...])
        _, ar, ai, _, _, _, zr, zi = _s5_discretize(lr_ref[...], li_ref[...], dt)
        ar_ref[...] = ar
        ai_ref[...] = ai
        e = _expand_matrix(p, ph // p)
        zr_x = _dot_exact(zr, e, ((1,), (0,)))
        zi_x = _dot_exact(zi, e, ((1,), (0,)))
        bre, bim = br_ref[...], bi_ref[...]
        bbr_ref[...] = zr_x * bre - zi_x * bim
        bbi_ref[...] = zr_x * bim + zi_x * bre

    return pl.pallas_call(
        body, name="s5_params",
        out_shape=[jax.ShapeDtypeStruct((g, p), F32)] * 2 + [jax.ShapeDtypeStruct((g, ph), F32)] * 2,
    )(lam_re, lam_im, log_dt, b_re, b_im)


def _s5_params_bwd(lam_re, lam_im, log_dt, b_re, b_im, d_ar, d_ai, d_bbr, d_bbi):
    g, p = lam_re.shape
    ph = b_re.shape[1]

    def body(lr_ref, li_ref, ldt_ref, br_ref, bi_ref, dar_ref, dai_ref, dbr_ref, dbi_ref,
             dlr_ref, dli_ref, dldt_ref, dbre_ref, dbim_ref):
        lr, li = lr_ref[...], li_ref[...]
        dt = jnp.exp(ldt_ref[...])
        mag, ar, ai, nr, ni, den, zr, zi = _s5_discretize(lr, li, dt)
        e = _expand_matrix(p, ph // p)
        zr_x = _dot_exact(zr, e, ((1,), (0,)))
        zi_x = _dot_exact(zi, e, ((1,), (0,)))
        bre, bim, dbr, dbi = br_ref[...], bi_ref[...], dbr_ref[...], dbi_ref[...]
        dbre_ref[...] = zr_x * dbr + zi_x * dbi
        dbim_ref[...] = zr_x * dbi - zi_x * dbr
        dzr = _dot_exact(bre * dbr + bim * dbi, e, ((1,), (1,)))
        dzi = _dot_exact(bre * dbi - bim * dbr, e, ((1,), (1,)))
        inv = 1.0 / den
        d_nr = (dzr * lr - dzi * li) * inv
        d_ni = (dzr * li + dzi * lr) * inv
        d_den = -(dzr * zr + dzi * zi) * inv
        d_lr = (dzr * nr + dzi * ni) * inv + 2.0 * lr * d_den
        d_li = (dzr * ni - dzi * nr) * inv + 2.0 * li * d_den
        t_ar = dar_ref[...] + d_nr
        t_ai = dai_ref[...] + d_ni
        d_lrdt = t_ar * ar + t_ai * ai
        d_th = t_ai * ar - t_ar * ai
        dlr_ref[...] = d_lr + d_lrdt * dt
        dli_ref[...] = d_li + d_th * dt
        dldt_ref[...] = jnp.sum(d_lrdt * lr + d_th * li, axis=1, keepdims=True) * dt

    return pl.pallas_call(
        body, name="s5_params_bwd",
        out_shape=[jax.ShapeDtypeStruct((g, p), F32)] * 2 + [jax.ShapeDtypeStruct((g, 1), F32)]
        + [jax.ShapeDtypeStruct((g, ph), F32)] * 2,
    )(lam_re, lam_im, log_dt, b_re, b_im, d_ar, d_ai, d_bbr, d_bbi)


def _powers(ar, ai, count):
    out = [(ar, ai)]
    for _ in range(count - 1):
        out.append(_cmul(out[-1][0], out[-1][1], ar, ai))
    return out


def _scan_coefs(ar, ai, reverse):
    w = ar.shape[-1]
    pw = _powers(ar, ai, SUBLANES)
    row = lax.broadcasted_iota(jnp.int32, (SUBLANES, w), 0)
    steps = []
    d = 1
    while d < SUBLANES:
        keep = (row < SUBLANES - d) if reverse else (row >= d)
        pr, pi = pw[d - 1]
        steps.append((d, jnp.where(keep, pr, 0.0), jnp.where(keep, pi, 0.0)))
        d *= 2
    cr = jnp.zeros((SUBLANES, w), F32)
    ci = jnp.zeros((SUBLANES, w), F32)
    for t in range(SUBLANES):
        pr, pi = pw[SUBLANES - 1 - t] if reverse else pw[t]
        cr = jnp.where(row == t, pr, cr)
        ci = jnp.where(row == t, pi, ci)
    return steps, cr, ci


def _scan_tile(xr, xi, carry_r, carry_i, coefs, reverse):
    steps, cr, ci = coefs
    for d, mr, mi in steps:
        shift = SUBLANES - d if reverse else d
        sr, si = pltpu.roll(xr, shift, 0), pltpu.roll(xi, shift, 0)
        pr, pi = _cmul(mr, mi, sr, si)
        xr, xi = xr + pr, xi + pi
    pr, pi = _cmul(cr, ci, carry_r, carry_i)
    return xr + pr, xi + pi


def _gelu(x):
    c = math.sqrt(2.0 / math.pi)
    return 0.5 * x * (1.0 + jnp.tanh(c * (x + 0.044715 * x * x * x)))


def _gelu_grad(x):
    c = math.sqrt(2.0 / math.pi)
    t = jnp.tanh(c * (x + 0.044715 * x * x * x))
    return 0.5 * (1.0 + t) + 0.5 * x * (1.0 - t * t) * c * (1.0 + 3.0 * 0.044715 * x * x)


def _s5_fwd(proj, wb, wc, d_skip, abar):
    rows = proj.shape[0]
    nb = wb.shape[0]
    s2 = 2 * STATE_PER_BATCH
    st = STATE_PER_BATCH
    chunk = _tile(rows, 512, SUBLANES)

    def body(u_ref, wb_ref, wc_ref, d_ref, a_ref, s_ref, y_ref, yg_ref):
        for c0 in range(0, rows, chunk):
            s_ref[pl.ds(c0, chunk), :] = _dot_nn(u_ref[pl.ds(c0, chunk), :].astype(BF16), wb_ref[...])
        av = a_ref[...]
        coefs = _scan_coefs(av[:, :st], av[:, st:], reverse=False)

        def tile(b, carry):
            r0 = pl.multiple_of(b * SUBLANES, SUBLANES)
            xr, xi = _scan_tile(s_ref[pl.ds(r0, SUBLANES), :st], s_ref[pl.ds(r0, SUBLANES), st:], carry[0], carry[1],
                                coefs, False)
            s_ref[pl.ds(r0, SUBLANES), :st] = xr
            s_ref[pl.ds(r0, SUBLANES), st:] = xi
            return xr[SUBLANES - 1:, :], xi[SUBLANES - 1:, :]

        zero = jnp.zeros((1, st), F32)
        lax.fori_loop(0, rows // SUBLANES, tile, (zero, zero))
        for c0 in range(0, rows, chunk):
            y = _dot_nn(s_ref[pl.ds(c0, chunk), :].astype(BF16), wc_ref[...]) + d_ref[...] * u_ref[pl.ds(c0, chunk), :]
            y_ref[pl.ds(c0, chunk), :] = y
            yg_ref[pl.ds(c0, chunk), :] = _gelu(y).astype(BF16)

    return pl.pallas_call(
        body, name="s5_fwd", grid=(nb,),
        in_specs=[pl.BlockSpec((rows, LANES), lambda j: (0, j)), pl.BlockSpec((None, LANES, s2), lambda j: (j, 0, 0)),
                  pl.BlockSpec((None, s2, LANES), lambda j: (j, 0, 0)), pl.BlockSpec((1, LANES), lambda j: (0, j)),
                  pl.BlockSpec((None, 1, s2), lambda j: (j, 0, 0))],
        out_specs=[pl.BlockSpec((rows, s2), lambda j: (0, j)), pl.BlockSpec((rows, LANES), lambda j: (0, j)),
                   pl.BlockSpec((rows, LANES), lambda j: (0, j))],
        out_shape=[jax.ShapeDtypeStruct((rows, nb * s2), F32), jax.ShapeDtypeStruct((rows, nb * LANES), F32),
                   jax.ShapeDtypeStruct((rows, nb * LANES), BF16)],
        compiler_params=_params(("parallel",)),
    )(proj, wb, wc, d_skip, abar)


def _s5_bwd(proj, states, y_pre, dyg_a, dyg_b, wb, wc, d_skip, abar):
    rows = proj.shape[0]
    nb = wb.shape[0]
    s2 = 2 * STATE_PER_BATCH
    st = STATE_PER_BATCH
    chunk = _tile(rows, 512, SUBLANES)
    n_tiles = rows // SUBLANES

    def body(u_ref, s_ref, y_ref, ga_ref, gb_ref, wb_ref, wc_ref, d_ref, a_ref,
             du_ref, dwb_ref, dwc_ref, da_ref, dd_ref, ds_ref, dy_ref):
        dy_ref[...] = (ga_ref[...] + gb_ref[...]) * _gelu_grad(y_ref[...])
        dd_ref[...] = jnp.sum(dy_ref[...] * u_ref[...], axis=0, keepdims=True)
        for c0 in range(0, rows, chunk):
            ds_ref[pl.ds(c0, chunk), :] = _dot_nt(dy_ref[pl.ds(c0, chunk), :].astype(BF16), wc_ref[...])
        dwc_ref[...] = _dot_tn(s_ref[...].astype(BF16), dy_ref[...].astype(BF16))
        av = a_ref[...]
        coefs = _scan_coefs(av[:, :st], -av[:, st:], reverse=True)
        row = lax.broadcasted_iota(jnp.int32, (SUBLANES, st), 0)

        def tile(k, carry):
            cr, ci, acc_r, acc_i = carry
            b = n_tiles - 1 - k
            r0 = pl.multiple_of(b * SUBLANES, SUBLANES)
            rp = pl.multiple_of(jnp.maximum(b - 1, 0) * SUBLANES, SUBLANES)
            xr, xi = _scan_tile(ds_ref[pl.ds(r0, SUBLANES), :st], ds_ref[pl.ds(r0, SUBLANES), st:], cr, ci, coefs, True)
            ds_ref[pl.ds(r0, SUBLANES), :st] = xr
            ds_ref[pl.ds(r0, SUBLANES), st:] = xi
            first = jnp.where(b > 0, 1.0, 0.0)
            pr = jnp.where(row == 0, pltpu.roll(s_ref[pl.ds(rp, SUBLANES), :st], 1, 0) * first,
                           pltpu.roll(s_ref[pl.ds(r0, SUBLANES), :st], 1, 0))
            pi = jnp.where(row == 0, pltpu.roll(s_ref[pl.ds(rp, SUBLANES), st:], 1, 0) * first,
                           pltpu.roll(s_ref[pl.ds(r0, SUBLANES), st:], 1, 0))
            acc_r = acc_r + pr * xr + pi * xi
            acc_i = acc_i + pr * xi - pi * xr
            return xr[:1, :], xi[:1, :], acc_r, acc_i

        zero = jnp.zeros((1, st), F32)
        zacc = jnp.zeros((SUBLANES, st), F32)
        _, _, acc_r, acc_i = lax.fori_loop(0, n_tiles, tile, (zero, zero, zacc, zacc))
        da_ref[:, :st] = jnp.sum(acc_r, axis=0, keepdims=True)
        da_ref[:, st:] = jnp.sum(acc_i, axis=0, keepdims=True)
        for c0 in range(0, rows, chunk):
            du_ref[pl.ds(c0, chunk), :] = (_dot_nt(ds_ref[pl.ds(c0, chunk), :].astype(BF16), wb_ref[...])
                                           + d_ref[...] * dy_ref[pl.ds(c0, chunk), :]).astype(du_ref.dtype)
        dwb_ref[...] = _dot_tn(u_ref[...].astype(BF16), ds_ref[...].astype(BF16))

    col = pl.BlockSpec((rows, LANES), lambda j: (0, j))
    return pl.pallas_call(
        body, name="s5_bwd", grid=(nb,),
        in_specs=[col, pl.BlockSpec((rows, s2), lambda j: (0, j)), col, col, col,
                  pl.BlockSpec((None, LANES, s2), lambda j: (j, 0, 0)), pl.BlockSpec((None, s2, LANES), lambda j: (j, 0, 0)),
                  pl.BlockSpec((1, LANES), lambda j: (0, j)), pl.BlockSpec((None, 1, s2), lambda j: (j, 0, 0))],
        out_specs=[col, pl.BlockSpec((None, LANES, s2), lambda j: (j, 0, 0)),
                   pl.BlockSpec((None, s2, LANES), lambda j: (j, 0, 0)), pl.BlockSpec((None, 1, s2), lambda j: (j, 0, 0)),
                   pl.BlockSpec((1, LANES), lambda j: (0, j))],
        out_shape=[jax.ShapeDtypeStruct((rows, nb * LANES), BF16), jax.ShapeDtypeStruct((nb, LANES, s2), F32),
                   jax.ShapeDtypeStruct((nb, s2, LANES), F32), jax.ShapeDtypeStruct((nb, 1, s2), F32),
                   jax.ShapeDtypeStruct((1, nb * LANES), F32)],
        scratch_shapes=[pltpu.VMEM((rows, s2), F32), pltpu.VMEM((rows, LANES), F32)],
        compiler_params=_params(("parallel",)),
    )(proj, states, y_pre, dyg_a, dyg_b, wb, wc, d_skip, abar)


def _glu_norm_fwd(y_pre, z, w, *, tr=256):
    rows, width = y_pre.shape
    tr = _tile(rows, tr, SUBLANES)

    def body(y_ref, z_ref, w_ref, o_ref):
        v = _gelu(y_ref[...]) * jax.nn.sigmoid(z_ref[...])
        o_ref[...] = (v * _rms_rows(v) * w_ref[...]).astype(o_ref.dtype)

    blk = pl.BlockSpec((tr, width), lambda i: (i, 0))
    return pl.pallas_call(
        body, name="glu_norm_fwd", grid=(rows // tr,),
        in_specs=[blk, blk, pl.BlockSpec((1, width), lambda i: (0, 0))], out_specs=blk,
        out_shape=jax.ShapeDtypeStruct((rows, width), BF16), compiler_params=_params(("parallel",)),
    )(y_pre, z, w)


def _glu_norm_bwd(y_pre, z, w, dycat, *, tr=256):
    rows, width = y_pre.shape
    tr = _tile(rows, tr, SUBLANES)

    def body(y_ref, z_ref, w_ref, dy_ref, dz_ref, dg_ref, dw_ref, db_ref):
        yg = _gelu(y_ref[...])
        sg = jax.nn.sigmoid(z_ref[...])
        dv, dwp = _rmsnorm_bwd_rows(yg * sg, w_ref[...], dy_ref[...])
        dz = dv * yg * sg * (1.0 - sg)
        dz_ref[...] = dz.astype(dz_ref.dtype)
        dg_ref[...] = dv * sg
        dw_part = jnp.sum(dwp, axis=0, keepdims=True)
        db_part = jnp.sum(dz, axis=0, keepdims=True)

        @pl.when(pl.program_id(0) == 0)
        def _():
            dw_ref[...] = dw_part
            db_ref[...] = db_part

        @pl.when(pl.program_id(0) > 0)
        def _():
            dw_ref[...] += dw_part
            db_ref[...] += db_part

    blk = pl.BlockSpec((tr, width), lambda i: (i, 0))
    vec = pl.BlockSpec((1, width), lambda i: (0, 0))
    return pl.pallas_call(
        body, name="glu_norm_bwd", grid=(rows // tr,), in_specs=[blk, blk, vec, blk], out_specs=[blk, blk, vec, vec],
        out_shape=[jax.ShapeDtypeStruct((rows, width), BF16), jax.ShapeDtypeStruct((rows, width), F32)]
        + [jax.ShapeDtypeStruct((1, width), F32)] * 2,
        compiler_params=_params(("arbitrary",)),
    )(y_pre, z, w, dycat)


def _rope_tables(pos, freq, sign):
    rows = pos.shape[0]

    def body(p_ref, f_ref, s_ref, cos_ref, sin_ref):
        ang = p_ref[...] * f_ref[...]
        cos_ref[...] = jnp.cos(ang)
        sin_ref[...] = jnp.sin(ang) * s_ref[...]

    return pl.pallas_call(body, name="rope_tables", out_shape=[jax.ShapeDtypeStruct((rows, LANES), F32)] * 2)(pos, freq, sign)


def _rope(x, cos, sin_signed):
    half = QK_ROPE_DIM // 2
    src = lax.broadcasted_iota(jnp.int32, (LANES, LANES), 0)
    dst = lax.broadcasted_iota(jnp.int32, (LANES, LANES), 1)
    swap = jnp.where(jnp.logical_or(jnp.logical_and(dst < half, src == dst + half),
                                    jnp.logical_and(jnp.logical_and(dst >= half, dst < 2 * half), src == dst - half)),
                     1.0, 0.0).astype(F32)
    swapped = _dot_exact(x, swap, ((1,), (0,)))
    return x * cos + swapped * sin_signed


def _attn_prep(q, kv, proj, kpe_col, cos, sin, *, tr=256):
    rows = q.shape[0]
    heads = q.shape[1] // HEAD_SLOT
    tr = _tile(rows, tr, SUBLANES)

    def body(q_ref, kv_ref, kpe_ref, cos_ref, sin_ref, qc_ref, kc_ref, v_ref):
        c, s = cos_ref[...], sin_ref[...]
        kpe = _rope(kpe_ref[...], c, s).astype(BF16)
        for h in range(heads):
            nope = slice(h * HEAD_SLOT, h * HEAD_SLOT + LANES)
            pe = slice(h * HEAD_SLOT + LANES, (h + 1) * HEAD_SLOT)
            qc_ref[:, nope] = q_ref[:, nope].astype(BF16)
            qc_ref[:, pe] = _rope(q_ref[:, pe], c, s).astype(BF16)
            kc_ref[:, nope] = kv_ref[:, nope].astype(BF16)
            kc_ref[:, pe] = kpe
            v_ref[:, h * LANES:(h + 1) * LANES] = kv_ref[:, pe].astype(BF16)

    slots = pl.BlockSpec((tr, heads * HEAD_SLOT), lambda i: (i, 0))
    tab = pl.BlockSpec((tr, LANES), lambda i: (i, 0))
    return pl.pallas_call(
        body, name="attn_prep", grid=(rows // tr,),
        in_specs=[slots, slots, pl.BlockSpec((tr, LANES), lambda i: (i, kpe_col)), tab, tab],
        out_specs=[slots, slots, pl.BlockSpec((tr, heads * LANES), lambda i: (i, 0))],
        out_shape=[jax.ShapeDtypeStruct((rows, heads * HEAD_SLOT), BF16)] * 2
        + [jax.ShapeDtypeStruct((rows, heads * LANES), BF16)],
        compiler_params=_params(("parallel",)),
    )(q, kv, proj, cos, sin)


def _causal(tq, tk):
    return lax.broadcasted_iota(jnp.int32, (tq, tk), 1) <= lax.broadcasted_iota(jnp.int32, (tq, tk), 0)


def _attn_fwd(qc, kc, vb, *, scale, tq=512):
    rows = qc.shape[0]
    heads = qc.shape[1] // HEAD_SLOT
    tq = _tile(rows, tq, SUBLANES)
    tk = tq

    def body(q_ref, k_ref, v_ref, o_ref, lse_ref):
        i = pl.program_id(1)
        q = q_ref[...]

        def step(j, carry, diagonal):
            m, l, acc = carry
            k0 = pl.multiple_of(j * tk, tk)
            s = _dot_nt(q, k_ref[pl.ds(k0, tk), :]) * scale
            if diagonal:
                s = jnp.where(_causal(tq, tk), s, NEG_INF)
            m_new = jnp.maximum(m, jnp.max(s, axis=-1, keepdims=True))
            p = jnp.exp(s - m_new)
            alpha = jnp.exp(m - m_new)
            l = alpha * l + jnp.sum(p, axis=-1, keepdims=True)
            acc = alpha * acc + _dot_nn(p.astype(BF16), v_ref[pl.ds(k0, tk), :])
            return m_new, l, acc

        init = (jnp.full((tq, 1), NEG_INF, F32), jnp.zeros((tq, 1), F32), jnp.zeros((tq, LANES), F32))
        below = lax.fori_loop(0, i, lambda j, carry: step(j, carry, False), init)
        m, l, acc = step(i, below, True)
        o_ref[...] = acc / l
        lse_ref[...] = jnp.broadcast_to(m + jnp.log(l), (tq, LANES))

    return pl.pallas_call(
        body, name="attn_fwd", grid=(heads, rows // tq),
        in_specs=[pl.BlockSpec((tq, HEAD_SLOT), lambda h, i: (i, h)), pl.BlockSpec((rows, HEAD_SLOT), lambda h, i: (0, h)),
                  pl.BlockSpec((rows, LANES), lambda h, i: (0, h))],
        out_specs=[pl.BlockSpec((tq, LANES), lambda h, i: (i, h))] * 2,
        out_shape=[jax.ShapeDtypeStruct((rows, heads * LANES), F32)] * 2,
        compiler_params=_params(("parallel", "parallel")),
    )(qc, kc, vb)


def _attn_bwd(qc, kc, vb, o, do, lse, cos, sin, *, scale, tk=512):
    rows = qc.shape[0]
    heads = qc.shape[1] // HEAD_SLOT
    tk = _tile(rows, tk, SUBLANES)
    tq = tk
    nq = rows // tq

    def body(q_ref, k_ref, v_ref, o_ref, do_ref, lse_ref, cos_ref, sin_ref, dq_ref, dkv_ref, dkpe_ref, dq_acc, delta_ref):
        j = pl.program_id(1)

        @pl.when(j == 0)
        def _():
            dq_acc[...] = jnp.zeros_like(dq_acc)
            for r0 in range(0, rows, tq):
                d = jnp.sum(do_ref[pl.ds(r0, tq), :] * o_ref[pl.ds(r0, tq), :], axis=-1, keepdims=True)
                delta_ref[pl.ds(r0, tq), :] = jnp.broadcast_to(d, (tq, LANES))

        kb, vv = k_ref[...], v_ref[...]

        def step(i, carry, diagonal):
            dk, dv = carry
            q0 = pl.multiple_of(i * tq, tq)
            qb = q_ref[pl.ds(q0, tq), :]
            dob = do_ref[pl.ds(q0, tq), :].astype(BF16)
            s = _dot_nt(qb, kb) * scale
            p = jnp.exp(s - lse_ref[pl.ds(q0, tq), :1])
            if diagonal:
                p = jnp.where(_causal(tq, tk), p, 0.0)
            dv = dv + _dot_tn(p.astype(BF16), dob)
            ds = (p * (_dot_nt(dob, vv) - delta_ref[pl.ds(q0, tq), :1])).astype(BF16)
            dk = dk + _dot_tn(ds, qb)
            dq_acc[pl.ds(q0, tq), :] += _dot_nn(ds, kb)
            return dk, dv

        zero = (jnp.zeros((tk, HEAD_SLOT), F32), jnp.zeros((tk, LANES), F32))
        dk, dv = lax.fori_loop(j + 1, nq, lambda i, carry: step(i, carry, False), step(j, zero, True))
        dkv_ref[:, :LANES] = (dk[:, :LANES] * scale).astype(dkv_ref.dtype)
        dkv_ref[:, LANES:] = dv.astype(dkv_ref.dtype)
        dkpe_ref[...] = dk[:, LANES:] * scale

        @pl.when(j == nq - 1)
        def _():
            for r0 in range(0, rows, tq):
                dq = dq_acc[pl.ds(r0, tq), :] * scale
                dq_ref[pl.ds(r0, tq), :LANES] = dq[:, :LANES].astype(dq_ref.dtype)
                dq_ref[pl.ds(r0, tq), LANES:] = _rope(dq[:, LANES:], cos_ref[pl.ds(r0, tq), :],
                                                      -sin_ref[pl.ds(r0, tq), :]).astype(dq_ref.dtype)

    full_q = pl.BlockSpec((rows, HEAD_SLOT), lambda h, j: (0, h))
    full_v = pl.BlockSpec((rows, LANES), lambda h, j: (0, h))
    tab = pl.BlockSpec((rows, LANES), lambda h, j: (0, 0))
    return pl.pallas_call(
        body, name="attn_bwd", grid=(heads, rows // tk),
        in_specs=[full_q, pl.BlockSpec((tk, HEAD_SLOT), lambda h, j: (j, h)), pl.BlockSpec((tk, LANES), lambda h, j: (j, h)),
                  full_v, full_v, full_v, tab, tab],
        out_specs=[full_q, pl.BlockSpec((tk, HEAD_SLOT), lambda h, j: (j, h)), pl.BlockSpec((tk, LANES), lambda h, j: (j, h))],
        out_shape=[jax.ShapeDtypeStruct((rows, heads * HEAD_SLOT), BF16), jax.ShapeDtypeStruct((rows, heads * HEAD_SLOT), BF16),
                   jax.ShapeDtypeStruct((rows, heads * LANES), F32)],
        scratch_shapes=[pltpu.VMEM((rows, HEAD_SLOT), F32), pltpu.VMEM((rows, LANES), F32)],
        compiler_params=_params(("parallel", "arbitrary")),
    )(qc, kc, vb, o, do, lse, cos, sin)


def _kpe_bwd(dkpe_heads, cos, sin, *, tr=512):
    rows = dkpe_heads.shape[0]
    heads = dkpe_heads.shape[1] // LANES
    tr = _tile(rows, tr, 2 * SUBLANES)

    def body(d_ref, cos_ref, sin_ref, o_ref):
        acc = d_ref[:, :LANES]
        for h in range(1, heads):
            acc = acc + d_ref[:, h * LANES:(h + 1) * LANES]
        o_ref[...] = _rope(acc, cos_ref[...], -sin_ref[...]).astype(o_ref.dtype)

    tab = pl.BlockSpec((tr, LANES), lambda i: (i, 0))
    return pl.pallas_call(
        body, name="kpe_bwd", grid=(rows // tr,),
        in_specs=[pl.BlockSpec((tr, heads * LANES), lambda i: (i, 0)), tab, tab], out_specs=tab,
        out_shape=jax.ShapeDtypeStruct((rows, LANES), BF16), compiler_params=_params(("parallel",)),
    )(dkpe_heads, cos, sin)


CONV_ROWS = 128


def _with_halo(ref, r0, ci, n_chunks, ch, lanes, before, after):
    parts = []
    if before:
        lo = pl.multiple_of(jnp.maximum(r0 - SUBLANES, 0), SUBLANES)
        parts.append(ref[pl.ds(lo, SUBLANES), lanes] * jnp.where(ci > 0, 1.0, 0.0))
    parts.append(ref[pl.ds(r0, ch), lanes])
    if after:
        hi = pl.multiple_of(jnp.minimum(r0 + ch, n_chunks * ch - SUBLANES), SUBLANES)
        parts.append(ref[pl.ds(hi, SUBLANES), lanes] * jnp.where(ci < n_chunks - 1, 1.0, 0.0))
    return jnp.concatenate(parts, axis=0)


def _taps(ext):
    return pltpu.roll(ext, 2, 0)[SUBLANES:], pltpu.roll(ext, 1, 0)[SUBLANES:], ext[SUBLANES:]


def _conv3(taps, w, b):
    return w[0:1, :] * taps[0] + w[1:2, :] * taps[1] + w[2:3, :] * taps[2] + b


def _conv_gate_fwd(a, conv_w, conv_b, *, tc=256):
    rows, f2 = a.shape
    f = f2 // 2
    tc = _tile(f, tc)
    nc = f // tc
    ch = _tile(rows, CONV_ROWS, SUBLANES)
    n_chunks = rows // ch

    def body(ag_ref, av_ref, wg_ref, wv_ref, bg_ref, bv_ref, o_ref):
        for lt in range(tc // LANES):
            lanes = slice(lt * LANES, (lt + 1) * LANES)
            wg, wv, bg, bv = wg_ref[:, lanes], wv_ref[:, lanes], bg_ref[:, lanes], bv_ref[:, lanes]

            def chunk(ci, carry):
                r0 = pl.multiple_of(ci * ch, ch)
                gate = _conv3(_taps(_with_halo(ag_ref, r0, ci, n_chunks, ch, lanes, True, False)), wg, bg)
                val = _conv3(_taps(_with_halo(av_ref, r0, ci, n_chunks, ch, lanes, True, False)), wv, bv)
                o_ref[pl.ds(r0, ch), lanes] = (gate * jax.nn.sigmoid(gate) * val).astype(o_ref.dtype)
                return carry

            lax.fori_loop(0, n_chunks, chunk, 0)

    return pl.pallas_call(
        body, name="conv_gate_fwd", grid=(nc,),
        in_specs=[pl.BlockSpec((rows, tc), lambda j: (0, j)), pl.BlockSpec((rows, tc), lambda j: (0, j + nc)),
                  pl.BlockSpec((SUBLANES, tc), lambda j: (0, j)), pl.BlockSpec((SUBLANES, tc), lambda j: (0, j + nc)),
                  pl.BlockSpec((1, tc), lambda j: (0, j)), pl.BlockSpec((1, tc), lambda j: (0, j + nc))],
        out_specs=pl.BlockSpec((rows, tc), lambda j: (0, j)),
        out_shape=jax.ShapeDtypeStruct((rows, f), BF16), compiler_params=_params(("parallel",)),
    )(a, a, conv_w, conv_w, conv_b, conv_b)


def _conv_gate_bwd(a, conv_w, conv_b, dg, *, tc=256):
    rows, f2 = a.shape
    f = f2 // 2
    tc = _tile(f, tc)
    nc = f // tc
    ch = _tile(rows, CONV_ROWS, SUBLANES)
    n_chunks = rows // ch
    ext_rows = ch + SUBLANES

    def fold(x):
        return jnp.sum(x.reshape(ch // SUBLANES, SUBLANES, LANES), axis=0)

    def body(ag_ref, av_ref, wg_ref, wv_ref, bg_ref, bv_ref, dg_ref, da_ref, dw_ref, db_ref):
        for lt in range(tc // LANES):
            lanes = slice(lt * LANES, (lt + 1) * LANES)
            wg, wv, bg, bv = wg_ref[:, lanes], wv_ref[:, lanes], bg_ref[:, lanes], bv_ref[:, lanes]

            def chunk(ci, acc):
                r0 = pl.multiple_of(ci * ch, ch)
                taps_g = _taps(_with_halo(ag_ref, r0, ci, n_chunks, ch, lanes, True, True))
                taps_v = _taps(_with_halo(av_ref, r0, ci, n_chunks, ch, lanes, True, True))
                dge = _with_halo(dg_ref, r0, ci, n_chunks, ch, lanes, False, True)
                gate, val = _conv3(taps_g, wg, bg), _conv3(taps_v, wv, bv)
                sg = jax.nn.sigmoid(gate)
                d_gate = dge * val * sg * (1.0 + gate * (1.0 - sg))
                d_val = dge * gate * sg
                new = []
                for half, (taps, w, d) in enumerate(((taps_g, wg, d_gate), (taps_v, wv, d_val))):
                    da = (w[2:3, :] * d[:ch] + w[1:2, :] * pltpu.roll(d, ext_rows - 1, 0)[:ch]
                          + w[0:1, :] * pltpu.roll(d, ext_rows - 2, 0)[:ch])
                    da_ref[half, pl.ds(r0, ch), lanes] = da.astype(da_ref.dtype)
                    dc = d[:ch]
                    sums = [fold(dc)] + [fold(dc * t[:ch]) for t in taps]
                    new.append(tuple(x + s for x, s in zip(acc[half], sums)))
                return tuple(new)

            zero = tuple(jnp.zeros((SUBLANES, LANES), F32) for _ in range(4))
            acc = lax.fori_loop(0, n_chunks, chunk, (zero, zero))
            row = lax.broadcasted_iota(jnp.int32, (SUBLANES, LANES), 0)
            for half in range(2):
                db, *taps = (jnp.sum(x, axis=0, keepdims=True) for x in acc[half])
                db_ref[half, :, lanes] = db
                dw = jnp.zeros((SUBLANES, LANES), F32)
                for tap in range(3):
                    dw = jnp.where(row == tap, taps[tap], dw)
                dw_ref[half, :, lanes] = dw

    lo = lambda j: (0, j)
    hi = lambda j: (0, j + nc)
    both = lambda j: (0, 0, j)
    return pl.pallas_call(
        body, name="conv_gate_bwd", grid=(nc,),
        in_specs=[pl.BlockSpec((rows, tc), lo), pl.BlockSpec((rows, tc), hi), pl.BlockSpec((SUBLANES, tc), lo),
                  pl.BlockSpec((SUBLANES, tc), hi), pl.BlockSpec((1, tc), lo), pl.BlockSpec((1, tc), hi),
                  pl.BlockSpec((rows, tc), lo)],
        out_specs=[pl.BlockSpec((2, rows, tc), both), pl.BlockSpec((2, SUBLANES, tc), both), pl.BlockSpec((2, 1, tc), both)],
        out_shape=[jax.ShapeDtypeStruct((2, rows, f), BF16), jax.ShapeDtypeStruct((2, SUBLANES, f), F32),
                   jax.ShapeDtypeStruct((2, 1, f), F32)],
        compiler_params=_params(("parallel",)),
    )(a, a, conv_w, conv_w, conv_b, conv_b, dg)


def _wgrad(a, b, rows, cols, row_sharded, name, **kw):
    return functools.partial(_wgrad_half, a, b, rows, cols, row_sharded, name, **kw)


def _block_diag(x):
    nb, g, r, c = x.shape
    eye = jnp.eye(g, dtype=x.dtype)
    return (x[:, :, :, None, :] * eye[None, :, None, :, None]).reshape(nb, g * r, g * c)


def _block_diag_part(x, r, c):
    nb = x.shape[0]
    g = GROUPS_PER_BATCH
    eye = jnp.eye(g, dtype=x.dtype)
    return jnp.sum(x.reshape(nb, g, r, g, c) * eye[None, :, None, :, None], axis=3)


class _NoExchange:
    def __init__(self, later, ffn):
        self.later, self.ffn = later, ffn

    def mixer_weights(self, after):
        return self.later

    def ffn_weights_arrived(self, after):
        return None

    def ffn_weights(self, after):
        return self.ffn

    def ffn_down_weight(self, after):
        return self.ffn["ffn_w_down"]

    def ffn_grads(self, makers, after):
        self.ffn_makers = makers
        return None

    def ffn_backward_done(self, after):
        return None


def _local_step(x, posf, target, w, hooks):
    rows, d = x.shape
    width = w["ssm_d"].shape[1]
    qr, kvr = w["mla_q_norm_w"].shape[1], w["mla_kv_norm_w"].shape[1]
    heads = w["mla_w_ukv"].shape[1] // HEAD_SLOT
    f2 = w["ffn_conv_b"].shape[1]
    inp = w["w_in"].shape[0]
    groups = width // SSM_GROUP
    nb = groups // GROUPS_PER_BATCH
    scale = (QK_NOPE_DIM + QK_ROPE_DIM) ** -0.5
    g = {}

    hn = _rmsnorm_fwd(x, w["attn_norm_w"], name="attn_norm")
    proj = _matmul(hn, w["w_in"], mode="nt", name="in_proj")

    ar, ai, bbr, bbi = _s5_params(w["ssm_lambda_re"], w["ssm_lambda_im"], w["ssm_log_dt"], w["ssm_b_re"], w["ssm_b_im"])

    def b_band(bb):
        return _block_diag(bb.reshape(nb, GROUPS_PER_BATCH, SSM_STATE, SSM_GROUP).transpose(0, 1, 3, 2))

    def c_band(c):
        return _block_diag(c.reshape(nb, GROUPS_PER_BATCH, SSM_GROUP, SSM_STATE).transpose(0, 1, 3, 2))

    wb = jnp.concatenate([b_band(bbr), b_band(bbi)], axis=2).astype(BF16)
    wc = jnp.concatenate([c_band(w["ssm_c_re"]), -c_band(w["ssm_c_im"])], axis=1).astype(BF16)
    abar = jnp.concatenate([ar.reshape(nb, 1, STATE_PER_BATCH), ai.reshape(nb, 1, STATE_PER_BATCH)], axis=2)
    states, y_pre, yg = _s5_fwd(proj, wb, wc, w["ssm_d"], abar)
    later = hooks.mixer_weights(yg)
    z = _matmul(yg, later["ssm_w_glu"], mode="nn", name="glu_proj", bias=w["ssm_b_glu"])
    ys = _glu_norm_fwd(y_pre, z, w["ssm_out_norm_w"])

    q_col, kv_col, kpe_col = width // qr, (width + qr) // kvr, (width + qr + kvr) // LANES
    assert width % qr == 0 and (width + qr) % kvr == 0
    qn = _rmsnorm_fwd(proj, w["mla_q_norm_w"], name="q_norm", width=qr, col=q_col)
    kvn = _rmsnorm_fwd(proj, w["mla_kv_norm_w"], name="kv_norm", width=kvr, col=kv_col)
    q = _matmul(qn, w["mla_w_uq"], mode="nn", name="q_proj")
    kv = _matmul(kvn, w["mla_w_ukv"], mode="nn", name="kv_proj")
    half = QK_ROPE_DIM // 2
    inv_freq = ROPE_THETA ** (-jnp.arange(0, QK_ROPE_DIM, 2, dtype=F32) / QK_ROPE_DIM)
    zeros = jnp.zeros((LANES - QK_ROPE_DIM,), F32)
    freq = jnp.concatenate([inv_freq, inv_freq, zeros]).reshape(1, LANES)
    sign = jnp.concatenate([-jnp.ones((half,), F32), jnp.ones((half,), F32), zeros]).reshape(1, LANES)
    cos, sin = _rope_tables(posf, freq, sign)
    qc, kc, vb = _attn_prep(q, kv, proj, kpe_col, cos, sin)
    o, lse = _attn_fwd(qc, kc, vb, scale=scale, tq=ATTN_BLOCK)
    ym = _rmsnorm_fwd(o, w["mla_out_norm_w"], name="mla_out_norm")
    ycat = jnp.concatenate([ys, ym], axis=1)
    h1 = _matmul(ycat, later["w_out"], mode="nn", name="out_proj", add=x, after=hooks.ffn_weights_arrived(ycat))

    hn2 = _rmsnorm_fwd(h1, w["ffn_norm_w"], name="ffn_norm")
    ffn = hooks.ffn_weights(hn2)
    a = _matmul(hn2, ffn["ffn_w_up"], mode="nn", name="ffn_up", tm=FFN_ROWS, after=ffn.get("started"))
    gated = _conv_gate_fwd(a, ffn["ffn_conv_w"], w["ffn_conv_b"])
    w_down = hooks.ffn_down_weight(gated)
    h2 = _matmul(gated, w_down, mode="nn", name="ffn_down", add=h1, tk=2816, tm=FFN_ROWS)
    loss_tile, dh2, dh2_mxu, g["final_norm_w"] = _final_norm_loss(h2, w["final_norm_w"], target)

    dgated = _matmul(dh2_mxu, w_down, mode="nt", name="ffn_down_dx", tm=FFN_ROWS)
    da, dcw, dcb = _conv_gate_bwd(a, ffn["ffn_conv_w"], w["ffn_conv_b"], dgated)
    g["ffn_conv_w"] = jnp.concatenate([dcw[0, :3], dcw[1, :3]], axis=1)
    g["ffn_conv_b"] = jnp.concatenate([dcb[0], dcb[1]], axis=1)
    started = hooks.ffn_grads({
        "ffn_w_up": _wgrad(hn2, da, d, f2, False, "ffn_up_dw", b_split=True, tn=_tile(f2 // N_CHIPS, 1408)),
        "ffn_w_down": _wgrad(gated, dh2_mxu, f2 // 2, d, True, "ffn_down_dw", tm=f2 // 2 // N_CHIPS, tn=512)}, dcb)
    dhn2 = _matmul(da, ffn["ffn_w_up"], mode="nt", name="ffn_up_dx", a_split=True, tk=_tile(f2 // 2, 2816), tm=FFN_ROWS,
                   after=started)
    dh1, dh1_mxu, g["ffn_norm_w"] = _rmsnorm_bwd(h1, w["ffn_norm_w"], dhn2, name="ffn_norm_bwd", add=dh2,
                                                dx_dtypes=(F32, BF16))

    dycat = _matmul(dh1_mxu, later["w_out"], mode="nt", name="out_proj_dx")
    g["w_out"] = _wgrad(ycat, dh1_mxu, 2 * width, d, True, "out_proj_dw")
    started = hooks.ffn_backward_done(dycat)
    mla_out_norm_w, ssm_out_norm_w = w["mla_out_norm_w"], w["ssm_out_norm_w"]
    if started is not None:
        mla_out_norm_w, ssm_out_norm_w = mla_out_norm_w + started[:1, :1], ssm_out_norm_w + started[:1, :1]

    do, g["mla_out_norm_w"] = _rmsnorm_bwd(o, mla_out_norm_w, dycat, name="mla_out_norm_bwd", width=width, dy_col=1)
    dq, dkv, dkpe_heads = _attn_bwd(qc, kc, vb, o, do, lse, cos, sin, scale=scale, tk=ATTN_BLOCK)
    dkpe = _kpe_bwd(dkpe_heads, cos, sin)
    g["mla_w_uq"] = _wgrad(qn, dq, qr, heads * HEAD_SLOT, False, "q_proj_dw")
    dqn = _matmul(dq, w["mla_w_uq"], mode="nt", name="q_proj_dx")
    dcq, g["mla_q_norm_w"] = _rmsnorm_bwd(proj, w["mla_q_norm_w"], dqn, name="q_norm_bwd", width=qr, col=q_col,
                                          dx_dtypes=(BF16,))
    g["mla_w_ukv"] = _wgrad(kvn, dkv, kvr, heads * HEAD_SLOT, False, "kv_proj_dw")
    dkvn = _matmul(dkv, w["mla_w_ukv"], mode="nt", name="kv_proj_dx")
    dckv, g["mla_kv_norm_w"] = _rmsnorm_bwd(proj, w["mla_kv_norm_w"], dkvn, name="kv_norm_bwd", width=kvr, col=kv_col,
                                            dx_dtypes=(BF16,))

    dz, dyg_a, g["ssm_out_norm_w"], g["ssm_b_glu"] = _glu_norm_bwd(y_pre, z, ssm_out_norm_w, dycat)
    dyg_b = _matmul(dz, later["ssm_w_glu"], mode="nt", name="glu_proj_dx")
    g["ssm_w_glu"] = _wgrad(yg, dz, width, width, True, "glu_proj_dw")
    du, dwb, dwc, dabar, g["ssm_d"] = _s5_bwd(proj, states, y_pre, dyg_a, dyg_b, wb, wc, w["ssm_d"], abar)

    def b_unband(x):
        return _block_diag_part(x, SSM_GROUP, SSM_STATE).transpose(0, 1, 3, 2).reshape(groups, SSM_STATE * SSM_GROUP)

    def c_unband(x):
        return _block_diag_part(x, SSM_STATE, SSM_GROUP).transpose(0, 1, 3, 2).reshape(groups, SSM_GROUP, SSM_STATE)

    st = STATE_PER_BATCH
    g["ssm_c_re"] = c_unband(dwc[:, :st, :])
    g["ssm_c_im"] = -c_unband(dwc[:, st:, :])
    d_ar = dabar[:, 0, :st].reshape(groups, SSM_STATE)
    d_ai = dabar[:, 0, st:].reshape(groups, SSM_STATE)
    (g["ssm_lambda_re"], g["ssm_lambda_im"], g["ssm_log_dt"], g["ssm_b_re"], g["ssm_b_im"]) = _s5_params_bwd(
        w["ssm_lambda_re"], w["ssm_lambda_im"], w["ssm_log_dt"], w["ssm_b_re"], w["ssm_b_im"], d_ar, d_ai,
        b_unband(dwb[:, :, :st]), b_unband(dwb[:, :, st:]))

    pad = jnp.zeros((rows, inp - (width + qr + kvr + LANES)), BF16)
    dproj = jnp.concatenate([du, dcq, dckv, dkpe, pad], axis=1)
    g["w_in"] = _wgrad(dproj, hn, inp, d, False, "in_proj_dw")
    dhn = _matmul(dproj, w["w_in"], mode="nn", name="in_proj_dx")
    dx, g["attn_norm_w"] = _rmsnorm_bwd(x, w["attn_norm_w"], dhn, name="attn_norm_bwd", add=dh1)
    return loss_tile, dx, g


ANY = pl.BlockSpec(memory_space=pl.ANY)
MESH = pl.DeviceIdType.MESH


def _mesh_pos():
    return lax.axis_index("x"), lax.axis_index("y"), lax.axis_index("c")


def _other_chips(x, y):
    return [(1 - x, y), (x, 1 - y), (1 - x, 1 - y)]


def _remote(src, dst, send_sems, recv_sems, k, to):
    return pltpu.make_async_remote_copy(src_ref=src, dst_ref=dst, send_sem=send_sems.at[k], recv_sem=recv_sems.at[k],
                                        device_id=to, device_id_type=MESH)


def _place_shard(shard, piece_idx, row_sharded, name, out_dtype=BF16, pieces=N_CHIPS):
    rs, cs = shard.shape
    tr = _tile(rs, 256, 2 * SUBLANES)
    rb = rs // tr

    def body(p_ref, x_ref, o_ref):
        o_ref[...] = x_ref[...].astype(o_ref.dtype)

    if row_sharded:
        out_shape, out_map = (pieces * rs, cs), (lambda i, p_ref: (p_ref[0] * rb + i, 0))
    else:
        out_shape, out_map = (rs, pieces * cs), (lambda i, p_ref: (i, p_ref[0]))
    return pl.pallas_call(
        body, name=name, out_shape=jax.ShapeDtypeStruct(out_shape, out_dtype),
        grid_spec=pltpu.PrefetchScalarGridSpec(
            num_scalar_prefetch=1, grid=(rb,), in_specs=[pl.BlockSpec((tr, cs), lambda i, p_ref: (i, 0))],
            out_specs=pl.BlockSpec((tr, cs), out_map)),
        compiler_params=_params(("parallel",)),
    )(piece_idx, shard)


def _gather_weights(placed, name):
    n = len(placed)
    meta = [(row_sharded, direct) for _, row_sharded, direct in placed]
    over_ici, over_d2d = _gather_plans(meta)
    forwarded = [t for t, (_, direct) in enumerate(meta) if not direct]

    def body(*refs):
        outs = refs[n:2 * n]
        send_sems, recv_sems, pass_send_sems, pass_recv_sems = refs[2 * n:]
        first, arrivals = over_ici(outs, send_sems, recv_sems)
        passed, passed_arrivals = over_d2d([outs[t] for t in forwarded], pass_send_sems, pass_recv_sems)
        for cp in first:
            cp.start()
        for t in range(n):
            for j in range(3):
                arrivals[3 * t + j].wait_recv()
                if t in forwarded:
                    passed[3 * forwarded.index(t) + j].start()
        for cp in passed_arrivals:
            cp.wait_recv()
        for cp in first + passed:
            cp.wait_send()

    return pl.pallas_call(
        body, name=name, in_specs=[ANY] * n, out_specs=[ANY] * n,
        out_shape=[jax.ShapeDtypeStruct(arr.shape, arr.dtype) for arr, _, _ in placed],
        input_output_aliases={t: t for t in range(n)},
        scratch_shapes=[pltpu.SemaphoreType.DMA((3 * n,)), pltpu.SemaphoreType.DMA((3 * n,)),
                        pltpu.SemaphoreType.DMA((3 * len(forwarded),)), pltpu.SemaphoreType.DMA((3 * len(forwarded),))],
    )(*[arr for arr, _, _ in placed])


def _gather_plans(meta):
    def window(ref, row_sharded, piece, half):
        r, cc = ref.shape
        if row_sharded:
            rs = r // N_CHIPS
            if half is None:
                return ref.at[pl.ds(piece * rs, rs), :]
            return ref.at[pl.ds(piece * rs + half * (rs // 2), rs // 2), :]
        cs = cc // N_CHIPS
        if half is None:
            return ref.at[:, pl.ds(piece * cs, cs)]
        return ref.at[pl.ds(half * (r // 2), r // 2), pl.ds(piece * cs, cs)]

    def over_ici(refs, send_sems, recv_sems):
        x, y, c = _mesh_pos()
        sends, recvs = [], []
        for t, (row_sharded, direct) in enumerate(meta):
            mine = window(refs[t], row_sharded, 2 * x + y, None if direct else c)
            for j, (px, py) in enumerate(_other_chips(x, y)):
                theirs = window(refs[t], row_sharded, 2 * px + py, None if direct else c)
                sends.append(_remote(mine, mine, send_sems, recv_sems, 3 * t + j, (px, py, c)))
                recvs.append(_remote(theirs, theirs, send_sems, recv_sems, 3 * t + j, (px, py, c)))
        return sends, recvs

    def over_d2d(refs, send_sems, recv_sems):
        x, y, c = _mesh_pos()
        sends, recvs = [], []
        rows = [row_sharded for row_sharded, direct in meta if not direct]
        for t, row_sharded in enumerate(rows):
            for j, (px, py) in enumerate(_other_chips(x, y)):
                got = window(refs[t], row_sharded, 2 * px + py, c)
                other = window(refs[t], row_sharded, 2 * px + py, 1 - c)
                sends.append(_remote(got, got, send_sems, recv_sems, 3 * t + j, (x, y, 1 - c)))
                recvs.append(_remote(other, other, send_sems, recv_sems, 3 * t + j, (x, y, 1 - c)))
        return sends, recvs

    return over_ici, over_d2d


HBM = pl.BlockSpec(memory_space=pltpu.HBM)
SEMAPHORES = pl.BlockSpec(memory_space=pltpu.SEMAPHORE)
DATAFLOW = pltpu.SideEffectType.DATAFLOW_SIDE_EFFECTING


def _start_copies(name, arrays, plan, n_copies, after):
    n = len(arrays)

    def body(*refs):
        sends, _ = plan(refs[:n], refs[n + 1], refs[n + 2])
        for cp in sends:
            cp.start()
        token = refs[2 * n + 3]
        token[...] = jnp.zeros_like(token)

    out = pl.pallas_call(
        body, name=name,
        out_shape=(pltpu.SemaphoreType.DMA((n_copies,)), pltpu.SemaphoreType.DMA((n_copies,)),
                   *[pltpu.HBM(a.shape, a.dtype) for a in arrays], jax.ShapeDtypeStruct((SUBLANES, LANES), F32)),
        in_specs=[HBM] * n + [ANY],
        out_specs=(SEMAPHORES, SEMAPHORES, *[HBM] * n, pl.BlockSpec(memory_space=pltpu.VMEM)),
        input_output_aliases={t: t + 2 for t in range(n)},
        compiler_params=pltpu.CompilerParams(has_side_effects=DATAFLOW),
    )(*[pltpu.with_memory_space_constraint(a, pltpu.HBM) for a in arrays], after)
    return out[0], out[1], list(out[2:2 + n]), out[2 + n]


def _wait_copies(name, started, plan, after):
    send_sems, recv_sems, arrays, _ = started
    n = len(arrays)

    def body(*refs):
        sends, recvs = plan(refs[:n], refs[n], refs[n + 1])
        for cp in sends:
            cp.wait_send()
        for cp in recvs:
            cp.wait_recv()

    out = pl.pallas_call(
        body, name=name, out_shape=[pltpu.HBM(a.shape, a.dtype) for a in arrays],
        in_specs=[HBM] * n + [SEMAPHORES, SEMAPHORES, ANY], out_specs=[HBM] * n,
        input_output_aliases={t: t for t in range(n)},
        compiler_params=pltpu.CompilerParams(has_side_effects=DATAFLOW),
    )(*arrays, send_sems, recv_sems, after)
    return list(out)


def _exchange(name, arrays, out_shapes, plan, n_copies, in_place=False, after=None):
    n = len(arrays)
    extra = [] if after is None else [after]

    def body(*refs):
        ins, outs = refs[:n], refs[n + len(extra):n + len(extra) + len(out_shapes)]
        send_sems, recv_sems = refs[n + len(extra) + len(out_shapes):]
        sends, recvs = plan(ins, outs, send_sems, recv_sems)
        for cp in sends:
            cp.start()
        for cp in recvs:
            cp.wait_recv()
        for cp in sends:
            cp.wait_send()

    return pl.pallas_call(
        body, name=name, in_specs=[ANY] * (n + len(extra)), out_specs=[ANY] * len(out_shapes), out_shape=out_shapes,
        input_output_aliases={t: t for t in range(n)} if in_place else {},
        scratch_shapes=[pltpu.SemaphoreType.DMA((n_copies,)), pltpu.SemaphoreType.DMA((n_copies,))],
    )(*arrays, *extra)


def _give_plan(n):
    def plan(refs, send_sems, recv_sems):
        x, y, c = _mesh_pos()
        sends = [_remote(refs[t], refs[n + t], send_sems, recv_sems, t, (x, y, 1 - c)) for t in range(n)]
        return sends, sends

    return plan


def _scatter_plan(n):
    def plan(refs, send_sems, recv_sems):
        x, y, c = _mesh_pos()
        sends = []
        for t in range(n):
            for j, (px, py) in enumerate(_other_chips(x, y)):
                sends.append(_remote(refs[t].at[2 * px + py], refs[n + t].at[j], send_sems, recv_sems, 3 * t + j, (px, py, c)))
        return sends, sends

    return plan


def _scatter_shapes(sums):
    return [jax.ShapeDtypeStruct((3,) + s.shape[1:], s.dtype) for s in sums]


def _join_halves(halves, name, after=None):
    def plan(ins, outs, send_sems, recv_sems):
        x, y, c = _mesh_pos()
        sends = [_remote(outs[t].at[c], outs[t].at[c], send_sems, recv_sems, t, (x, y, 1 - c)) for t in range(len(ins))]
        recvs = [_remote(outs[t].at[1 - c], outs[t].at[1 - c], send_sems, recv_sems, t, (x, y, 1 - c))
                 for t in range(len(ins))]
        return sends, recvs

    shapes = [jax.ShapeDtypeStruct(h.shape, h.dtype) for h in halves]
    return _exchange(name, halves, shapes, plan, len(halves), in_place=True, after=after)


def _add_other_half(g4, got, where, name, wire_dtype=BF16):
    _, pieces, sr, sc = g4.shape
    tr = _tile(sr, 256, 2 * SUBLANES)

    def body(w_ref, a_ref, b_ref, o_ref):
        o_ref[...] = (a_ref[...] + b_ref[...]).astype(o_ref.dtype)

    blk = pl.BlockSpec((None, tr, sc), lambda p, i, w_ref: (p, i, 0))
    return pl.pallas_call(
        body, name=name, out_shape=jax.ShapeDtypeStruct((pieces, sr, sc), wire_dtype),
        grid_spec=pltpu.PrefetchScalarGridSpec(
            num_scalar_prefetch=1, grid=(pieces, sr // tr),
            in_specs=[pl.BlockSpec((None, None, tr, sc), lambda p, i, w_ref: (w_ref[0], p, i, 0)), blk], out_specs=blk),
        compiler_params=_params(("parallel", "parallel")),
    )(where, g4, got)


def _add_pieces(sums, got_pieces, where, name):
    _, sr, sc = sums.shape
    tr = _tile(sr, 256, 2 * SUBLANES)

    def body(w_ref, a_ref, r_ref, o_ref):
        acc = a_ref[...]
        for j in range(3):
            acc = acc + r_ref[j].astype(F32)
        o_ref[...] = acc

    return pl.pallas_call(
        body, name=name, out_shape=jax.ShapeDtypeStruct((N_CORES, sr, sc), F32),
        grid_spec=pltpu.PrefetchScalarGridSpec(
            num_scalar_prefetch=1, grid=(sr // tr,),
            in_specs=[pl.BlockSpec((None, tr, sc), lambda i, w_ref: (w_ref[1], i, 0)),
                      pl.BlockSpec((3, tr, sc), lambda i, w_ref: (0, i, 0))],
            out_specs=pl.BlockSpec((None, tr, sc), lambda i, w_ref: (w_ref[0], i, 0))),
        compiler_params=_params(("parallel",)),
    )(where, sums, got_pieces)


def _adamw_update(w, g, m, v):
    nm = ADAM_B1 * m + (1.0 - ADAM_B1) * g
    nv = ADAM_B2 * v + (1.0 - ADAM_B2) * (g * g)
    m_hat = nm / (1.0 - ADAM_B1 ** ADAM_STEP)
    v_hat = nv / (1.0 - ADAM_B2 ** ADAM_STEP)
    return -ADAM_LR * (m_hat / (jnp.sqrt(v_hat) + ADAM_EPS) + ADAM_WD * w), nm, nv


def _adamw(w, g, m, v, name, after=None):
    rows, cols = w.shape
    halves = 2 if g.ndim == 3 else 1
    bc = cols // halves
    tr = _tile(rows, max(SUBLANES, (1 << 19) // max(bc, 1) // SUBLANES * SUBLANES), SUBLANES)

    def body(w_ref, g_ref, m_ref, v_ref, *rest):
        d_ref, nm_ref, nv_ref, go_ref = rest[-4:]
        gv = g_ref[...]
        d_ref[...], nm_ref[...], nv_ref[...] = _adamw_update(w_ref[...], gv, m_ref[...], v_ref[...])
        go_ref[...] = gv

    blk = pl.BlockSpec((tr, bc), lambda i, h: (i, h))
    g_blk = pl.BlockSpec((None, tr, bc), lambda i, h: (h, i, 0)) if halves == 2 else blk
    extra = [] if after is None else [after]
    return pl.pallas_call(
        body, name=name, grid=(rows // tr, halves),
        in_specs=[blk, g_blk, blk, blk] + [pl.BlockSpec(memory_space=pl.ANY)] * len(extra), out_specs=[blk] * 4,
        out_shape=[jax.ShapeDtypeStruct((rows, cols), F32)] * 4, compiler_params=_params(("parallel", "parallel")),
    )(w, g, m, v, *extra)


def _adamw_many(ws, gs, ms, vs, name):
    n = len(ws)

    def body(*refs):
        outs = refs[4 * n:]
        for k in range(n):
            w_ref, g_ref, m_ref, v_ref = (refs[j * n + k] for j in range(4))
            outs[k][...], outs[n + k][...], outs[2 * n + k][...] = _adamw_update(w_ref[...], g_ref[...], m_ref[...], v_ref[...])

    out = pl.pallas_call(
        body, name=name, out_shape=[jax.ShapeDtypeStruct(w.shape, F32) for w in ws] * 3,
        compiler_params=pltpu.CompilerParams(vmem_limit_bytes=VMEM_LIMIT_BYTES),
    )(*ws, *gs, *ms, *vs)
    return out[:n], out[n:2 * n], out[2 * n:]


WEIGHTS = ['attn_norm_w', 'w_in', 'ssm_lambda_re', 'ssm_lambda_im', 'ssm_log_dt', 'ssm_b_re', 'ssm_b_im', 'ssm_c_re',
           'ssm_c_im', 'ssm_d', 'ssm_w_glu', 'ssm_b_glu', 'mla_q_norm_w', 'mla_w_uq', 'mla_kv_norm_w', 'mla_w_ukv',
           'ssm_out_norm_w', 'mla_out_norm_w', 'w_out', 'ffn_norm_w', 'ffn_w_up', 'ffn_conv_w', 'ffn_conv_b',
           'ffn_w_down', 'final_norm_w']
SHARDED = {'w_in': False, 'ssm_w_glu': True, 'mla_w_uq': False, 'mla_w_ukv': False, 'w_out': True, 'ffn_w_up': False,
           'ffn_w_down': True}
SMALL = [n for n in WEIGHTS if n not in SHARDED and n != 'ffn_conv_w']
ROPE_PAD = HEAD_SLOT - QK_NOPE_DIM - QK_ROPE_DIM
SMALL_COLS = 8 * LANES


def _pad_heads(w_uq, heads):
    qr = w_uq.shape[0]
    w3 = w_uq.reshape(qr, heads, QK_NOPE_DIM + QK_ROPE_DIM)
    return jnp.concatenate([w3, jnp.zeros((qr, heads, ROPE_PAD), w_uq.dtype)], axis=2).reshape(qr, heads * HEAD_SLOT)


def _unpad_heads(g_uq, heads):
    qr = g_uq.shape[0]
    return g_uq.reshape(qr, heads, HEAD_SLOT)[:, :, :QK_NOPE_DIM + QK_ROPE_DIM].reshape(qr, -1)


FFN = ['ffn_w_up', 'ffn_w_down']
MIXER_LATER = ['ssm_w_glu', 'w_out']
FFN_GATHER = FFN + ['ffn_conv_w']
FFN_GATHER_META = [(SHARDED[n], False) for n in FFN] + [(False, True)]


class _Overlapped:
    def __init__(self, placed_later, placed, where, after):
        self.where, self.mine, self.other = where, where[:1], 1 - where[:1]
        self.later_ici, self.later_d2d = _gather_plans([(SHARDED[n], False) for n in MIXER_LATER])
        self.later = _start_copies("gather_later_start", placed_later, self.later_ici, 3 * len(placed_later), after)
        up, down, taps = placed
        self.up_ici, self.up_d2d = _gather_plans([(SHARDED["ffn_w_up"], False)])
        self.up = _start_copies("gather_ffn_up_start", [up], self.up_ici, 3, self.later[3])
        self.down_ici, self.down_d2d = _gather_plans([(SHARDED["ffn_w_down"], False), (False, True)])
        self.down = _start_copies("gather_ffn_down_start", [down, taps], self.down_ici, 6, self.up[3])
        self.gather_started = self.down[3]

    def mixer_weights(self, after):
        arrived = _wait_copies("gather_later_wait", self.later, self.later_ici, after)
        shapes = [jax.ShapeDtypeStruct(a.shape, a.dtype) for a in arrived]
        passed = _exchange("gather_later_pass", arrived, shapes, lambda ins, outs, s, r: self.later_d2d(outs, s, r),
                           3 * len(arrived), in_place=True)
        return dict(zip(MIXER_LATER, passed))

    def ffn_weights_arrived(self, after):
        arrived = _wait_copies("gather_ffn_up_wait", self.up, self.up_ici, after)
        self.up_passing = _start_copies("gather_ffn_up_pass_start", arrived, self.up_d2d, 3, after)
        return self.up_passing[3]

    def ffn_weights(self, after):
        w_up, = _wait_copies("gather_ffn_up_pass_wait", self.up_passing, self.up_d2d, after)
        down, taps = _wait_copies("gather_ffn_down_wait", self.down, self.down_ici, after)
        self.down_passing = _start_copies("gather_ffn_down_pass_start", [down], self.down_d2d, 3, w_up)
        return {"ffn_w_up": w_up, "ffn_conv_w": taps, "started": self.down_passing[3]}

    def ffn_down_weight(self, after):
        return _wait_copies("gather_ffn_down_pass_wait", self.down_passing, self.down_d2d, after)[0]

    def ffn_grads(self, makers, after):
        self.makers = [makers[name] for name in FFN]
        n = len(FFN)
        give = [make(self.other, suffix="_give") for make in self.makers]
        lands = [lax.empty(g.shape, g.dtype) for g in give]
        self.swap = _start_copies("grad_ffn_swap_start", give + lands, _give_plan(n), n, after)
        return self.swap[3]

    def ffn_backward_done(self, after):
        n = len(FFN)
        got = _wait_copies("grad_ffn_swap_wait", self.swap, _give_plan(n), after)[n:]
        kept = [make(self.mine, suffix="_keep", add=got[t], wire=True) for t, make in enumerate(self.makers)]
        self.sums = [k[0] for k in kept]
        wires = [k[1] for k in kept]
        lands = [lax.empty(s.shape, s.dtype) for s in _scatter_shapes(wires)]
        self.scatter = _start_copies("grad_ffn_scatter_start", wires + lands, _scatter_plan(n), 3 * n, after)
        return self.scatter[3]

    def ffn_reduced(self, after):
        n = len(FFN)
        got_pieces = _wait_copies("grad_ffn_scatter_wait", self.scatter, _scatter_plan(n), after)[n:]
        return [_add_pieces(self.sums[t], got_pieces[t], self.where, "grad_add_pieces_" + name) for t, name in enumerate(FFN)]


def _step(args):
    x, positions, target = args["x"][0], args["positions"], args["loss_target"][0]
    rows = x.shape[0]
    p = {n: args[n] for n in WEIGHTS}
    xi, yi, ci = _mesh_pos()
    piece = 2 * xi + yi

    def transposed(a):
        return jnp.swapaxes(a[0], 0, 1)

    w_in = transposed(p["w_in"])
    in_width = w_in.shape[0]
    in_pad = (-in_width) % (2 * LANES)
    heads_here = p["mla_w_uq"].shape[2] // (QK_NOPE_DIM + QK_ROPE_DIM)
    shards = {
        "w_in": jnp.pad(w_in, ((0, in_pad), (0, 0))),
        "ssm_w_glu": p["ssm_w_glu"][0],
        "mla_w_uq": _pad_heads(p["mla_w_uq"][0], heads_here),
        "mla_w_ukv": p["mla_w_ukv"][0],
        "w_out": p["w_out"][0],
        "ffn_w_up": p["ffn_w_up"][0],
        "ffn_w_down": p["ffn_w_down"][0],
    }
    conv_w = jnp.pad(p["ffn_conv_w"][0], ((0, SUBLANES - p["ffn_conv_w"].shape[1]), (0, 0)))
    order = list(SHARDED)
    piece_idx = piece.reshape(1).astype(jnp.int32)
    placed = {n: _place_shard(shards[n], piece_idx, SHARDED[n], "place_" + n) for n in order}
    placed["ffn_conv_w"] = _place_shard(conv_w, piece_idx, False, "place_ffn_conv_w", out_dtype=F32)
    mixer = [n for n in order if n not in FFN]
    first = [n for n in mixer if n not in MIXER_LATER]
    w = dict(zip(first, _gather_weights([(placed[n], SHARDED[n], False) for n in first], "gather_first_weights")))
    where = jnp.stack([ci, piece]).astype(jnp.int32)
    hooks = _Overlapped([placed[n] for n in MIXER_LATER], [placed[n] for n in FFN_GATHER], where, after=w["w_in"])
    groups = p["ssm_lambda_re"].shape[1]
    w.update({
        "attn_norm_w": p["attn_norm_w"] + hooks.gather_started[:1, :1],
        "ssm_lambda_re": p["ssm_lambda_re"][0], "ssm_lambda_im": p["ssm_lambda_im"][0],
        "ssm_log_dt": p["ssm_log_dt"].reshape(groups, 1), "ssm_b_re": p["ssm_b_re"].reshape(groups, -1),
        "ssm_b_im": p["ssm_b_im"].reshape(groups, -1), "ssm_c_re": p["ssm_c_re"][0], "ssm_c_im": p["ssm_c_im"][0],
        "ssm_d": p["ssm_d"], "ssm_b_glu": p["ssm_b_glu"], "mla_q_norm_w": p["mla_q_norm_w"],
        "mla_kv_norm_w": p["mla_kv_norm_w"], "ssm_out_norm_w": p["ssm_out_norm_w"], "mla_out_norm_w": p["mla_out_norm_w"],
        "ffn_norm_w": p["ffn_norm_w"], "ffn_conv_b": p["ffn_conv_b"], "final_norm_w": p["final_norm_w"].reshape(1, -1),
    })

    loss_tile, dx, g = _local_step(x, positions.reshape(rows, 1).astype(F32), target, w, hooks)
    loss = lax.psum(loss_tile[0, 0], ("x", "y", "c"))

    flat = [g[n].reshape(-1) for n in SMALL] + [g["ffn_conv_w"].reshape(-1)]
    sizes = [f.shape[0] for f in flat]
    per_block = -(-sum(sizes) // (N_CORES * N_CHIPS * SMALL_COLS))
    small_rows = -(-per_block // (2 * SUBLANES)) * (2 * SUBLANES)
    padded = N_CORES * N_CHIPS * small_rows * SMALL_COLS

    def pack(parts):
        parts = list(parts)
        have = sum(q.shape[0] for q in parts)
        return jnp.concatenate(parts + [jnp.zeros((padded - have,), F32)])

    reduced = mixer + ["small"]
    small = pack(flat).reshape(N_CORES, N_CHIPS, small_rows, SMALL_COLS)
    give = [g[n](hooks.other, suffix="_give") for n in mixer] + [lax.dynamic_index_in_dim(small, 1 - ci, 0, keepdims=False)]
    lands = [lax.empty(a.shape, a.dtype) for a in give]
    give_plan = _give_plan(len(reduced))
    swap = _start_copies("grad_mixer_swap_start", give + lands, give_plan, len(reduced), dx)

    grads, delta, new_m, new_v = {}, {}, {}, {}

    def finish(n, joined, after=None):
        grad = joined if SHARDED[n] else joined.reshape(-1, joined.shape[2])
        if n == "w_in":
            wt, mt, vt = w_in, transposed(args["m_w_in"]), transposed(args["v_w_in"])
            out = _adamw(wt, grad, mt, vt, "adamw_w_in")
            delta[n], new_m[n], new_v[n], grads[n] = (jnp.swapaxes(a, 0, 1)[None] for a in out)
            return
        if n == "mla_w_uq":
            grad = _unpad_heads(grad, heads_here)
        adam(n, grad, after)

    def adam(n, grad, after=None):
        shape = p[n].shape
        out = _adamw(p[n].reshape(shape[1:]), grad, args["m_" + n].reshape(shape[1:]),
                     args["v_" + n].reshape(shape[1:]), "adamw_" + n, after)
        delta[n], new_m[n], new_v[n], grads[n] = (a.reshape(shape) for a in out)

    ffn_joined = _join_halves(hooks.ffn_reduced(dx), "grad_ffn_join_halves", after=swap[3])
    got = _wait_copies("grad_mixer_swap_wait", swap, give_plan, ffn_joined[0])[len(reduced):]
    kept = [g[n](hooks.mine, suffix="_keep", add=got[t], wire=True) for t, n in enumerate(mixer)]
    small_sum = _add_other_half(small, got[-1], where, "grad_add_half_small", F32)
    sums = [k[0] for k in kept] + [small_sum]
    wires = [k[1] for k in kept] + [small_sum]
    lands = [lax.empty(s.shape, s.dtype) for s in _scatter_shapes(wires)]
    scatter_plan = _scatter_plan(len(reduced))
    scatter = _start_copies("grad_mixer_scatter_start", wires + lands, scatter_plan, 3 * len(reduced), kept[0][0])
    behind = scatter[3]
    for n, joined in zip(FFN, ffn_joined):
        finish(n, joined, after=behind)
        behind = delta[n]
    got_pieces = _wait_copies("grad_mixer_scatter_wait", scatter, scatter_plan, delta[FFN[-1]])[len(reduced):]
    halves = [_add_pieces(sums[t], got_pieces[t], where, "grad_add_pieces_" + n) for t, n in enumerate(reduced)]
    joined = _join_halves(halves, "grad_join_halves")
    for n, j in zip(mixer, joined):
        finish(n, j)
    eighths = _place_shard(joined[-1].reshape(N_CORES * small_rows, SMALL_COLS), piece_idx, True, "place_small_grads",
                           out_dtype=F32)
    small_sum = _gather_weights([(eighths, True, False)], "gather_small_grads")[0]
    flat_sum = small_sum.reshape(N_CHIPS, N_CORES, small_rows * SMALL_COLS).transpose(1, 0, 2).reshape(-1)
    offs = [0]
    for s in sizes:
        offs.append(offs[-1] + s)
    for k, n in enumerate(SMALL):
        grads[n] = flat_sum[offs[k]:offs[k + 1]].reshape(p[n].shape)
    taps, cols_here = p["ffn_conv_w"].shape[1], p["ffn_conv_w"].shape[2]
    conv_full = flat_sum[offs[len(SMALL)]:offs[len(SMALL) + 1]].reshape(taps, N_CHIPS * cols_here)
    adam("ffn_conv_w", lax.dynamic_slice_in_dim(conv_full, piece * cols_here, cols_here, axis=1))

    def rank2(a):
        return a.reshape(1, -1) if a.ndim == 1 else a

    d_s, m_s, v_s = _adamw_many([rank2(p[n]) for n in SMALL], [rank2(grads[n]) for n in SMALL],
                                [rank2(args["m_" + n]) for n in SMALL], [rank2(args["v_" + n]) for n in SMALL], "adamw_small")
    for k, n in enumerate(SMALL):
        delta[n], new_m[n], new_v[n] = (a.reshape(p[n].shape) for a in (d_s[k], m_s[k], v_s[k]))

    return (loss, dx[None], *[grads[n] for n in WEIGHTS], *[delta[n] for n in WEIGHTS],
            *[new_m[n] for n in WEIGHTS], *[new_v[n] for n in WEIGHTS])


def kernel(x, positions, attn_norm_w, w_in, ssm_lambda_re, ssm_lambda_im, ssm_log_dt, ssm_b_re, ssm_b_im, ssm_c_re, ssm_c_im, ssm_d, ssm_w_glu, ssm_b_glu, mla_q_norm_w, mla_w_uq, mla_kv_norm_w, mla_w_ukv, ssm_out_norm_w, mla_out_norm_w, w_out, ffn_norm_w, ffn_w_up, ffn_conv_w, ffn_conv_b, ffn_w_down, final_norm_w, loss_target, m_attn_norm_w, m_w_in, m_ssm_lambda_re, m_ssm_lambda_im, m_ssm_log_dt, m_ssm_b_re, m_ssm_b_im, m_ssm_c_re, m_ssm_c_im, m_ssm_d, m_ssm_w_glu, m_ssm_b_glu, m_mla_q_norm_w, m_mla_w_uq, m_mla_kv_norm_w, m_mla_w_ukv, m_ssm_out_norm_w, m_mla_out_norm_w, m_w_out, m_ffn_norm_w, m_ffn_w_up, m_ffn_conv_w, m_ffn_conv_b, m_ffn_w_down, m_final_norm_w, v_attn_norm_w, v_w_in, v_ssm_lambda_re, v_ssm_lambda_im, v_ssm_log_dt, v_ssm_b_re, v_ssm_b_im, v_ssm_c_re, v_ssm_c_im, v_ssm_d, v_ssm_w_glu, v_ssm_b_glu, v_mla_q_norm_w, v_mla_w_uq, v_mla_kv_norm_w, v_mla_w_ukv, v_ssm_out_norm_w, v_mla_out_norm_w, v_w_out, v_ffn_norm_w, v_ffn_w_up, v_ffn_conv_w, v_ffn_conv_b, v_ffn_w_down, v_final_norm_w):
    return _step(dict(locals()))
```

```python
import functools
import math

import jax
import jax.numpy as jnp
from jax import lax
from jax.experimental import pallas as pl
from jax.experimental.pallas import tpu as pltpu

F32 = jnp.float32
BF16 = jnp.bfloat16

SSM_GROUP = 16
SSM_STATE = 64
QK_NOPE_DIM = 128
QK_ROPE_DIM = 64
V_HEAD_DIM = 128
ROPE_THETA = 10000.0
RMS_EPS = 1e-6
ADAM_LR, ADAM_B1, ADAM_B2, ADAM_EPS, ADAM_WD, ADAM_STEP = 0.001, 0.9, 0.999, 1e-08, 0.01, 10

LANES = 128
SUBLANES = 8
VMEM_LIMIT_BYTES = 56 * 1024 * 1024

GROUPS_PER_BATCH = LANES // SSM_GROUP
STATE_PER_BATCH = GROUPS_PER_BATCH * SSM_STATE
HEAD_SLOT = 2 * LANES
NEG_INF = -1e30
ATTN_BLOCK = 512
FFN_ROWS = 1024

N_CHIPS = 4
N_CORES = 2


def _tile(n, pref, align=LANES):
    if n <= pref:
        return n
    t = (pref // align) * align
    while t >= align:
        if n % t == 0:
            return t
        t -= align
    return n


def _params(sem):
    return pltpu.CompilerParams(dimension_semantics=sem, vmem_limit_bytes=VMEM_LIMIT_BYTES)


def _dot(a, b, dims):
    return lax.dot_general(a, b, (dims, ((), ())), preferred_element_type=F32)


def _dot_nn(a, b):
    return _dot(a, b, ((1,), (0,)))


def _dot_nt(a, b):
    return _dot(a, b, ((1,), (1,)))


def _dot_tn(a, b):
    return _dot(a, b, ((0,), (0,)))


def _matmul(a, b, *, mode, name, tm=512, tn=1024, tk=2048, bias=None, add=None, out_dtype=F32,
            out_blocks=None, a_split=False, b_split=False, after=None):
    if a_split:
        assert mode == "nt"
        a_shape = (a.shape[1], 2 * a.shape[2])
    else:
        a_shape = a.shape
    if b_split:
        assert mode == "tn"
        b_shape = (b.shape[1], 2 * b.shape[2])
    else:
        b_shape = b.shape
    if mode == "nn":
        (m, k), (k2, n) = a_shape, b_shape
    elif mode == "nt":
        (m, k), (n, k2) = a_shape, b_shape
    else:
        (k, m), (k2, n) = a_shape, b_shape
    assert k == k2, (a.shape, b.shape, mode)
    tm, tn, tk = _tile(m, tm, SUBLANES), _tile(n, tn), _tile(k, tk)
    nk = k // tk
    a_spec = {"nn": pl.BlockSpec((tm, tk), lambda i, j, kk: (i, kk)),
              "nt": pl.BlockSpec((tm, tk), lambda i, j, kk: (i, kk)),
              "tn": pl.BlockSpec((tk, tm), lambda i, j, kk: (kk, i))}[mode]
    b_spec = {"nn": pl.BlockSpec((tk, tn), lambda i, j, kk: (kk, j)),
              "nt": pl.BlockSpec((tn, tk), lambda i, j, kk: (j, kk)),
              "tn": pl.BlockSpec((tk, tn), lambda i, j, kk: (kk, j))}[mode]
    if a_split:
        kb = a.shape[2] // tk
        assert a.shape[2] % tk == 0
        a_spec = pl.BlockSpec((None, tm, tk), lambda i, j, kk: (kk // kb, i, kk % kb))
    if b_split:
        nb = b.shape[2] // tn
        assert b.shape[2] % tn == 0
        b_spec = pl.BlockSpec((None, tk, tn), lambda i, j, kk: (j // nb, kk, j % nb))
    dot = {"nn": _dot_nn, "nt": _dot_nt, "tn": _dot_tn}[mode]
    in_specs, operands = [a_spec, b_spec], [a, b]
    if bias is not None:
        in_specs.append(pl.BlockSpec((1, tn), lambda i, j, kk: (0, j)))
        operands.append(bias)
    if add is not None:
        in_specs.append(pl.BlockSpec((tm, tn), lambda i, j, kk: (i, j)))
        operands.append(add)
    if after is not None:
        in_specs.append(pl.BlockSpec(memory_space=pl.ANY))
        operands.append(after)

    def body(*refs):
        a_ref, b_ref = refs[0], refs[1]
        rest = list(refs[2:])
        bias_ref = rest.pop(0) if bias is not None else None
        add_ref = rest.pop(0) if add is not None else None
        if after is not None:
            rest.pop(0)
        o_ref, acc_ref = rest

        def finish(acc):
            if bias_ref is not None:
                acc = acc + bias_ref[...]
            if add_ref is not None:
                acc = acc + add_ref[...]
            o_ref[...] = acc.astype(o_ref.dtype)

        part = dot(a_ref[...].astype(BF16), b_ref[...].astype(BF16))
        if nk == 1:
            finish(part)
        else:
            kk = pl.program_id(2)

            @pl.when(kk == 0)
            def _():
                acc_ref[...] = part

            @pl.when(jnp.logical_and(kk > 0, kk < nk - 1))
            def _():
                acc_ref[...] += part

            @pl.when(kk == nk - 1)
            def _():
                finish(acc_ref[...] + part)

    if out_blocks is None:
        out_shape = jax.ShapeDtypeStruct((m, n), out_dtype)
        out_spec = pl.BlockSpec((tm, tn), lambda i, j, kk: (i, j))
    else:
        shape, block, index_map = out_blocks(tm, tn)
        out_shape = jax.ShapeDtypeStruct(shape, out_dtype)
        out_spec = pl.BlockSpec(block, index_map)
    acc_shape = (tm, tn) if nk > 1 else (SUBLANES, LANES)
    return pl.pallas_call(
        body, name=name, grid=(m // tm, n // tn, nk), in_specs=in_specs, out_specs=out_spec, out_shape=out_shape,
        scratch_shapes=[pltpu.VMEM(acc_shape, F32)],
        compiler_params=_params(("parallel", "parallel", "arbitrary")),
    )(*operands)


def _wgrad_half(a, b, rows, cols, row_sharded, name, which, *, suffix="", add=None, wire=False, tm=None, tn=None,
                b_split=False):
    tokens = a.shape[0]
    if row_sharded:
        sr, sc = rows // N_CHIPS, cols // N_CORES
    else:
        sr, sc = rows // N_CORES, cols // N_CHIPS
    tm = _tile(sr, 512) if tm is None else tm
    tn = _tile(sc, 1024) if tn is None else tn
    assert sr % tm == 0 and sc % tn == 0, (rows, cols, tm, tn)
    rb, cb = sr // tm, sc // tn
    if tn >= tm:
        ij, grid = (lambda s, t: (t, s)), (N_CHIPS, cb, rb)
    else:
        ij, grid = (lambda s, t: (s, t)), (N_CHIPS, rb, cb)
    if row_sharded:
        a_tile = lambda p, i, j, h: p * rb + i
        b_tile = lambda p, i, j, h: h[0] * cb + j
    else:
        a_tile = lambda p, i, j, h: h[0] * rb + i
        b_tile = lambda p, i, j, h: p * cb + j
    a_spec = pl.BlockSpec((tokens, tm), lambda p, s, t, h: (0, a_tile(p, *ij(s, t), h)))
    if b_split:
        nbh = b.shape[2] // tn
        assert b.shape[2] % tn == 0
        b_spec = pl.BlockSpec((None, tokens, tn), lambda p, s, t, h: (b_tile(p, *ij(s, t), h) // nbh, 0,
                                                                       b_tile(p, *ij(s, t), h) % nbh))
    else:
        b_spec = pl.BlockSpec((tokens, tn), lambda p, s, t, h: (0, b_tile(p, *ij(s, t), h)))
    out_spec = pl.BlockSpec((None, tm, tn), lambda p, s, t, h: (p, *ij(s, t)))
    in_specs, operands = [a_spec, b_spec], [a, b]
    if add is not None:
        in_specs.append(out_spec)
        operands.append(add)

    def body(h_ref, a_ref, b_ref, *rest):
        acc = _dot_tn(a_ref[...].astype(BF16), b_ref[...].astype(BF16))
        if add is not None:
            acc = acc + rest[0][...]
        for o_ref in rest[1 if add is not None else 0:]:
            o_ref[...] = acc.astype(o_ref.dtype)

    out_dtypes = [F32, BF16] if wire else [F32]
    out = pl.pallas_call(
        body, name=name + suffix, out_shape=[jax.ShapeDtypeStruct((N_CHIPS, sr, sc), dt) for dt in out_dtypes],
        grid_spec=pltpu.PrefetchScalarGridSpec(num_scalar_prefetch=1, grid=grid, in_specs=in_specs,
                                               out_specs=[out_spec] * len(out_dtypes)),
        compiler_params=_params(("parallel", "parallel", "parallel")),
    )(which, *operands)
    return tuple(out) if wire else out[0]


def _rms_rows(x):
    return lax.rsqrt(jnp.mean(x * x, axis=-1, keepdims=True) + RMS_EPS)


def _rmsnorm_fwd(x, w, *, name, width=None, col=0, out_dtype=BF16, tr=256):
    rows = x.shape[0]
    width = x.shape[1] if width is None else width
    tr = _tile(rows, tr, SUBLANES)

    def body(x_ref, w_ref, o_ref):
        xv = x_ref[...]
        o_ref[...] = (xv * _rms_rows(xv) * w_ref[...]).astype(o_ref.dtype)

    return pl.pallas_call(
        body, name=name, grid=(rows // tr,),
        in_specs=[pl.BlockSpec((tr, width), lambda i: (i, col)), pl.BlockSpec((1, width), lambda i: (0, 0))],
        out_specs=pl.BlockSpec((tr, width), lambda i: (i, 0)),
        out_shape=jax.ShapeDtypeStruct((rows, width), out_dtype),
        compiler_params=_params(("parallel",)),
    )(x, w)


def _rmsnorm_bwd_rows(xv, w, dy):
    r = _rms_rows(xv)
    n = xv * r
    dn = dy * w
    dx = r * (dn - n * jnp.mean(dn * n, axis=-1, keepdims=True))
    return dx, dy * n


def _rmsnorm_bwd(x, w, dy, *, name, width=None, col=0, dy_col=0, add=None, tr=256, dx_dtypes=(F32,)):
    rows = x.shape[0]
    n_dx = len(dx_dtypes)
    width = x.shape[1] if width is None else width
    tr = _tile(rows, tr, SUBLANES)
    in_specs = [pl.BlockSpec((tr, width), lambda i: (i, col)), pl.BlockSpec((1, width), lambda i: (0, 0)),
                pl.BlockSpec((tr, width), lambda i: (i, dy_col))]
    operands = [x, w, dy]
    if add is not None:
        in_specs.append(pl.BlockSpec((tr, width), lambda i: (i, 0)))
        operands.append(add)

    def body(*refs):
        x_ref, w_ref, dy_ref = refs[:3]
        add_ref = refs[3] if add is not None else None
        dx_refs, dw_ref = refs[-1 - n_dx:-1], refs[-1]
        dx, dwp = _rmsnorm_bwd_rows(x_ref[...], w_ref[...], dy_ref[...])
        if add_ref is not None:
            dx = dx + add_ref[...]
        for dx_ref in dx_refs:
            dx_ref[...] = dx.astype(dx_ref.dtype)
        part = jnp.sum(dwp, axis=0, keepdims=True)

        @pl.when(pl.program_id(0) == 0)
        def _():
            dw_ref[...] = part

        @pl.when(pl.program_id(0) > 0)
        def _():
            dw_ref[...] += part

    return pl.pallas_call(
        body, name=name, grid=(rows // tr,), in_specs=in_specs,
        out_specs=[pl.BlockSpec((tr, width), lambda i: (i, 0))] * n_dx + [pl.BlockSpec((1, width), lambda i: (0, 0))],
        out_shape=[jax.ShapeDtypeStruct((rows, width), dt) for dt in dx_dtypes] + [jax.ShapeDtypeStruct((1, width), F32)],
        compiler_params=_params(("arbitrary",)),
    )(*operands)


def _final_norm_loss(h, w, target, *, tr=256):
    rows, d = h.shape
    tr = _tile(rows, tr, SUBLANES)

    def body(h_ref, w_ref, t_ref, loss_ref, dh_ref, dhb_ref, dw_ref):
        hv, wv = h_ref[...], w_ref[...]
        r = _rms_rows(hv)
        n = hv * r
        err = n * wv - t_ref[...]
        d_out = err * (1.0 / d)
        dn = d_out * wv
        dh = r * (dn - n * jnp.mean(dn * n, axis=-1, keepdims=True))
        dh_ref[...] = dh
        dhb_ref[...] = dh.astype(BF16)
        dw_part = jnp.sum(d_out * n, axis=0, keepdims=True)
        loss_part = jnp.full((SUBLANES, LANES), 0.5 / d, F32) * jnp.sum(err * err)

        @pl.when(pl.program_id(0) == 0)
        def _():
            dw_ref[...] = dw_part
            loss_ref[...] = loss_part

        @pl.when(pl.program_id(0) > 0)
        def _():
            dw_ref[...] += dw_part
            loss_ref[...] += loss_part

    return pl.pallas_call(
        body, name="final_norm_loss", grid=(rows // tr,),
        in_specs=[pl.BlockSpec((tr, d), lambda i: (i, 0)), pl.BlockSpec((1, d), lambda i: (0, 0)),
                  pl.BlockSpec((tr, d), lambda i: (i, 0))],
        out_specs=[pl.BlockSpec((SUBLANES, LANES), lambda i: (0, 0)), pl.BlockSpec((tr, d), lambda i: (i, 0)),
                   pl.BlockSpec((tr, d), lambda i: (i, 0)), pl.BlockSpec((1, d), lambda i: (0, 0))],
        out_shape=[jax.ShapeDtypeStruct((SUBLANES, LANES), F32), jax.ShapeDtypeStruct((rows, d), F32),
                   jax.ShapeDtypeStruct((rows, d), BF16), jax.ShapeDtypeStruct((1, d), F32)],
        compiler_params=_params(("arbitrary",)),
    )(h, w, target)


def _cmul(ar, ai, br, bi):
    return ar * br - ai * bi, ar * bi + ai * br


def _expand_matrix(groups, reps):
    row = lax.broadcasted_iota(jnp.int32, (groups, groups * reps), 0)
    colg = lax.broadcasted_iota(jnp.int32, (groups, groups * reps), 1) // reps
    return (row == colg).astype(F32)


def _dot_exact(a, b, dims):
    return lax.dot_general(a, b, (dims, ((), ())), preferred_element_type=F32, precision=lax.Precision.HIGHEST)


def _s5_discretize(lr, li, dt):
    mag = jnp.exp(lr * dt)
    th = li * dt
    ar, ai = mag * jnp.cos(th), mag * jnp.sin(th)
    nr, ni = ar - 1.0, ai
    den = lr * lr + li * li
    zr = (nr * lr + ni * li) / den
    zi = (ni * lr - nr * li) / den
    return mag, ar, ai, nr, ni, den, zr, zi


def _s5_params(lam_re, lam_im, log_dt, b_re, b_im):
    g, p = lam_re.shape
    ph = b_re.shape[1]

    def body(lr_ref, li_ref, ldt_ref, br_ref, bi_ref, ar_ref, ai_ref, bbr_ref, bbi_ref):
        dt = jnp.exp(ldt_ref[...])
        _, ar, ai, _, _, _, zr, zi = _s5_discretize(lr_ref[...], li_ref[...], dt)
        ar_ref[...] = ar
        ai_ref[...] = ai
        e = _expand_matrix(p, ph // p)
        zr_x = _dot_exact(zr, e, ((1,), (0,)))
        zi_x = _dot_exact(zi, e, ((1,), (0,)))
        bre, bim = br_ref[...], bi_ref[...]
        bbr_ref[...] = zr_x * bre - zi_x * bim
        bbi_ref[...] = zr_x * bim + zi_x * bre

    return pl.pallas_call(
        body, name="s5_params",
        out_shape=[jax.ShapeDtypeStruct((g, p), F32)] * 2 + [jax.ShapeDtypeStruct((g, ph), F32)] * 2,
    )(lam_re, lam_im, log_dt, b_re, b_im)


def _s5_params_bwd(lam_re, lam_im, log_dt, b_re, b_im, d_ar, d_ai, d_bbr, d_bbi):
    g, p = lam_re.shape
    ph = b_re.shape[1]

    def body(lr_ref, li_ref, ldt_ref, br_ref, bi_ref, dar_ref, dai_ref, dbr_ref, dbi_ref,
             dlr_ref, dli_ref, dldt_ref, dbre_ref, dbim_ref):
        lr, li = lr_ref[...], li_ref[...]
        dt = jnp.exp(ldt_ref[...])
        mag, ar, ai, nr, ni, den, zr, zi = _s5_discretize(lr, li, dt)
        e = _expand_matrix(p, ph // p)
        zr_x = _dot_exact(zr, e, ((1,), (0,)))
        zi_x = _dot_exact(zi, e, ((1,), (0,)))
        bre, bim, dbr, dbi = br_ref[...], bi_ref[...], dbr_ref[...], dbi_ref[...]
        dbre_ref[...] = zr_x * dbr + zi_x * dbi
        dbim_ref[...] = zr_x * dbi - zi_x * dbr
        dzr = _dot_exact(bre * dbr + bim * dbi, e, ((1,), (1,)))
        dzi = _dot_exact(bre * dbi - bim * dbr, e, ((1,), (1,)))
        inv = 1.0 / den
        d_nr = (dzr * lr - dzi * li) * inv
        d_ni = (dzr * li + dzi * lr) * inv
        d_den = -(dzr * zr + dzi * zi) * inv
        d_lr = (dzr * nr + dzi * ni) * inv + 2.0 * lr * d_den
        d_li = (dzr * ni - dzi * nr) * inv + 2.0 * li * d_den
        t_ar = dar_ref[...] + d_nr
        t_ai = dai_ref[...] + d_ni
        d_lrdt = t_ar * ar + t_ai * ai
        d_th = t_ai * ar - t_ar * ai
        dlr_ref[...] = d_lr + d_lrdt * dt
        dli_ref[...] = d_li + d_th * dt
        dldt_ref[...] = jnp.sum(d_lrdt * lr + d_th * li, axis=1, keepdims=True) * dt

    return pl.pallas_call(
        body, name="s5_params_bwd",
        out_shape=[jax.ShapeDtypeStruct((g, p), F32)] * 2 + [jax.ShapeDtypeStruct((g, 1), F32)]
        + [jax.ShapeDtypeStruct((g, ph), F32)] * 2,
    )(lam_re, lam_im, log_dt, b_re, b_im, d_ar, d_ai, d_bbr, d_bbi)


def _powers(ar, ai, count):
    out = [(ar, ai)]
    for _ in range(count - 1):
        out.append(_cmul(out[-1][0], out[-1][1], ar, ai))
    return out


def _scan_coefs(ar, ai, reverse):
    w = ar.shape[-1]
    pw = _powers(ar, ai, SUBLANES)
    row = lax.broadcasted_iota(jnp.int32, (SUBLANES, w), 0)
    steps = []
    d = 1
    while d < SUBLANES:
        keep = (row < SUBLANES - d) if reverse else (row >= d)
        pr, pi = pw[d - 1]
        steps.append((d, jnp.where(keep, pr, 0.0), jnp.where(keep, pi, 0.0)))
        d *= 2
    cr = jnp.zeros((SUBLANES, w), F32)
    ci = jnp.zeros((SUBLANES, w), F32)
    for t in range(SUBLANES):
        pr, pi = pw[SUBLANES - 1 - t] if reverse else pw[t]
        cr = jnp.where(row == t, pr, cr)
        ci = jnp.where(row == t, pi, ci)
    return steps, cr, ci


def _scan_tile(xr, xi, carry_r, carry_i, coefs, reverse):
    steps, cr, ci = coefs
    for d, mr, mi in steps:
        shift = SUBLANES - d if reverse else d
        sr, si = pltpu.roll(xr, shift, 0), pltpu.roll(xi, shift, 0)
        pr, pi = _cmul(mr, mi, sr, si)
        xr, xi = xr + pr, xi + pi
    pr, pi = _cmul(cr, ci, carry_r, carry_i)
    return xr + pr, xi + pi


def _gelu(x):
    c = math.sqrt(2.0 / math.pi)
    return 0.5 * x * (1.0 + jnp.tanh(c * (x + 0.044715 * x * x * x)))


def _gelu_grad(x):
    c = math.sqrt(2.0 / math.pi)
    t = jnp.tanh(c * (x + 0.044715 * x * x * x))
    return 0.5 * (1.0 + t) + 0.5 * x * (1.0 - t * t) * c * (1.0 + 3.0 * 0.044715 * x * x)


def _s5_fwd(proj, wb, wc, d_skip, abar):
    rows = proj.shape[0]
    nb = wb.shape[0]
    s2 = 2 * STATE_PER_BATCH
    st = STATE_PER_BATCH
    chunk = _tile(rows, 512, SUBLANES)

    def body(u_ref, wb_ref, wc_ref, d_ref, a_ref, s_ref, y_ref, yg_ref):
        for c0 in range(0, rows, chunk):
            s_ref[pl.ds(c0, chunk), :] = _dot_nn(u_ref[pl.ds(c0, chunk), :].astype(BF16), wb_ref[...])
        av = a_ref[...]
        coefs = _scan_coefs(av[:, :st], av[:, st:], reverse=False)

        def tile(b, carry):
            r0 = pl.multiple_of(b * SUBLANES, SUBLANES)
            xr, xi = _scan_tile(s_ref[pl.ds(r0, SUBLANES), :st], s_ref[pl.ds(r0, SUBLANES), st:], carry[0], carry[1],
                                coefs, False)
            s_ref[pl.ds(r0, SUBLANES), :st] = xr
            s_ref[pl.ds(r0, SUBLANES), st:] = xi
            return xr[SUBLANES - 1:, :], xi[SUBLANES - 1:, :]

        zero = jnp.zeros((1, st), F32)
        lax.fori_loop(0, rows // SUBLANES, tile, (zero, zero))
        for c0 in range(0, rows, chunk):
            y = _dot_nn(s_ref[pl.ds(c0, chunk), :].astype(BF16), wc_ref[...]) + d_ref[...] * u_ref[pl.ds(c0, chunk), :]
            y_ref[pl.ds(c0, chunk), :] = y
            yg_ref[pl.ds(c0, chunk), :] = _gelu(y).astype(BF16)

    return pl.pallas_call(
        body, name="s5_fwd", grid=(nb,),
        in_specs=[pl.BlockSpec((rows, LANES), lambda j: (0, j)), pl.BlockSpec((None, LANES, s2), lambda j: (j, 0, 0)),
                  pl.BlockSpec((None, s2, LANES), lambda j: (j, 0, 0)), pl.BlockSpec((1, LANES), lambda j: (0, j)),
                  pl.BlockSpec((None, 1, s2), lambda j: (j, 0, 0))],
        out_specs=[pl.BlockSpec((rows, s2), lambda j: (0, j)), pl.BlockSpec((rows, LANES), lambda j: (0, j)),
                   pl.BlockSpec((rows, LANES), lambda j: (0, j))],
        out_shape=[jax.ShapeDtypeStruct((rows, nb * s2), F32), jax.ShapeDtypeStruct((rows, nb * LANES), F32),
                   jax.ShapeDtypeStruct((rows, nb * LANES), BF16)],
        compiler_params=_params(("parallel",)),
    )(proj, wb, wc, d_skip, abar)


def _s5_bwd(proj, states, y_pre, dyg_a, dyg_b, wb, wc, d_skip, abar):
    rows = proj.shape[0]
    nb = wb.shape[0]
    s2 = 2 * STATE_PER_BATCH
    st = STATE_PER_BATCH
    chunk = _tile(rows, 512, SUBLANES)
    n_tiles = rows // SUBLANES

    def body(u_ref, s_ref, y_ref, ga_ref, gb_ref, wb_ref, wc_ref, d_ref, a_ref,
             du_ref, dwb_ref, dwc_ref, da_ref, dd_ref, ds_ref, dy_ref):
        dy_ref[...] = (ga_ref[...] + gb_ref[...]) * _gelu_grad(y_ref[...])
        dd_ref[...] = jnp.sum(dy_ref[...] * u_ref[...], axis=0, keepdims=True)
        for c0 in range(0, rows, chunk):
            ds_ref[pl.ds(c0, chunk), :] = _dot_nt(dy_ref[pl.ds(c0, chunk), :].astype(BF16), wc_ref[...])
        dwc_ref[...] = _dot_tn(s_ref[...].astype(BF16), dy_ref[...].astype(BF16))
        av = a_ref[...]
        coefs = _scan_coefs(av[:, :st], -av[:, st:], reverse=True)
        row = lax.broadcasted_iota(jnp.int32, (SUBLANES, st), 0)

        def tile(k, carry):
            cr, ci, acc_r, acc_i = carry
            b = n_tiles - 1 - k
            r0 = pl.multiple_of(b * SUBLANES, SUBLANES)
            rp = pl.multiple_of(jnp.maximum(b - 1, 0) * SUBLANES, SUBLANES)
            xr, xi = _scan_tile(ds_ref[pl.ds(r0, SUBLANES), :st], ds_ref[pl.ds(r0, SUBLANES), st:], cr, ci, coefs, True)
            ds_ref[pl.ds(r0, SUBLANES), :st] = xr
            ds_ref[pl.ds(r0, SUBLANES), st:] = xi
            first = jnp.where(b > 0, 1.0, 0.0)
            pr = jnp.where(row == 0, pltpu.roll(s_ref[pl.ds(rp, SUBLANES), :st], 1, 0) * first,
                           pltpu.roll(s_ref[pl.ds(r0, SUBLANES), :st], 1, 0))
            pi = jnp.where(row == 0, pltpu.roll(s_ref[pl.ds(rp, SUBLANES), st:], 1, 0) * first,
                           pltpu.roll(s_ref[pl.ds(r0, SUBLANES), st:], 1, 0))
            acc_r = acc_r + pr * xr + pi * xi
            acc_i = acc_i + pr * xi - pi * xr
            return xr[:1, :], xi[:1, :], acc_r, acc_i

        zero = jnp.zeros((1, st), F32)
        zacc = jnp.zeros((SUBLANES, st), F32)
        _, _, acc_r, acc_i = lax.fori_loop(0, n_tiles, tile, (zero, zero, zacc, zacc))
        da_ref[:, :st] = jnp.sum(acc_r, axis=0, keepdims=True)
        da_ref[:, st:] = jnp.sum(acc_i, axis=0, keepdims=True)
        for c0 in range(0, rows, chunk):
            du_ref[pl.ds(c0, chunk), :] = (_dot_nt(ds_ref[pl.ds(c0, chunk), :].astype(BF16), wb_ref[...])
                                           + d_ref[...] * dy_ref[pl.ds(c0, chunk), :]).astype(du_ref.dtype)
        dwb_ref[...] = _dot_tn(u_ref[...].astype(BF16), ds_ref[...].astype(BF16))

    col = pl.BlockSpec((rows, LANES), lambda j: (0, j))
    return pl.pallas_call(
        body, name="s5_bwd", grid=(nb,),
        in_specs=[col, pl.BlockSpec((rows, s2), lambda j: (0, j)), col, col, col,
                  pl.BlockSpec((None, LANES, s2), lambda j: (j, 0, 0)), pl.BlockSpec((None, s2, LANES), lambda j: (j, 0, 0)),
                  pl.BlockSpec((1, LANES), lambda j: (0, j)), pl.BlockSpec((None, 1, s2), lambda j: (j, 0, 0))],
        out_specs=[col, pl.BlockSpec((None, LANES, s2), lambda j: (j, 0, 0)),
                   pl.BlockSpec((None, s2, LANES), lambda j: (j, 0, 0)), pl.BlockSpec((None, 1, s2), lambda j: (j, 0, 0)),
                   pl.BlockSpec((1, LANES), lambda j: (0, j))],
        out_shape=[jax.ShapeDtypeStruct((rows, nb * LANES), BF16), jax.ShapeDtypeStruct((nb, LANES, s2), F32),
                   jax.ShapeDtypeStruct((nb, s2, LANES), F32), jax.ShapeDtypeStruct((nb, 1, s2), F32),
                   jax.ShapeDtypeStruct((1, nb * LANES), F32)],
        scratch_shapes=[pltpu.VMEM((rows, s2), F32), pltpu.VMEM((rows, LANES), F32)],
        compiler_params=_params(("parallel",)),
    )(proj, states, y_pre, dyg_a, dyg_b, wb, wc, d_skip, abar)


def _glu_norm_fwd(y_pre, z, w, *, tr=256):
    rows, width = y_pre.shape
    tr = _tile(rows, tr, SUBLANES)

    def body(y_ref, z_ref, w_ref, o_ref):
        v = _gelu(y_ref[...]) * jax.nn.sigmoid(z_ref[...])
        o_ref[...] = (v * _rms_rows(v) * w_ref[...]).astype(o_ref.dtype)

    blk = pl.BlockSpec((tr, width), lambda i: (i, 0))
    return pl.pallas_call(
        body, name="glu_norm_fwd", grid=(rows // tr,),
        in_specs=[blk, blk, pl.BlockSpec((1, width), lambda i: (0, 0))], out_specs=blk,
        out_shape=jax.ShapeDtypeStruct((rows, width), BF16), compiler_params=_params(("parallel",)),
    )(y_pre, z, w)


def _glu_norm_bwd(y_pre, z, w, dycat, *, tr=256):
    rows, width = y_pre.shape
    tr = _tile(rows, tr, SUBLANES)

    def body(y_ref, z_ref, w_ref, dy_ref, dz_ref, dg_ref, dw_ref, db_ref):
        yg = _gelu(y_ref[...])
        sg = jax.nn.sigmoid(z_ref[...])
        dv, dwp = _rmsnorm_bwd_rows(yg * sg, w_ref[...], dy_ref[...])
        dz = dv * yg * sg * (1.0 - sg)
        dz_ref[...] = dz.astype(dz_ref.dtype)
        dg_ref[...] = dv * sg
        dw_part = jnp.sum(dwp, axis=0, keepdims=True)
        db_part = jnp.sum(dz, axis=0, keepdims=True)

        @pl.when(pl.program_id(0) == 0)
        def _():
            dw_ref[...] = dw_part
            db_ref[...] = db_part

        @pl.when(pl.program_id(0) > 0)
        def _():
            dw_ref[...] += dw_part
            db_ref[...] += db_part

    blk = pl.BlockSpec((tr, width), lambda i: (i, 0))
    vec = pl.BlockSpec((1, width), lambda i: (0, 0))
    return pl.pallas_call(
        body, name="glu_norm_bwd", grid=(rows // tr,), in_specs=[blk, blk, vec, blk], out_specs=[blk, blk, vec, vec],
        out_shape=[jax.ShapeDtypeStruct((rows, width), BF16), jax.ShapeDtypeStruct((rows, width), F32)]
        + [jax.ShapeDtypeStruct((1, width), F32)] * 2,
        compiler_params=_params(("arbitrary",)),
    )(y_pre, z, w, dycat)


def _rope_tables(pos, freq, sign):
    rows = pos.shape[0]

    def body(p_ref, f_ref, s_ref, cos_ref, sin_ref):
        ang = p_ref[...] * f_ref[...]
        cos_ref[...] = jnp.cos(ang)
        sin_ref[...] = jnp.sin(ang) * s_ref[...]

    return pl.pallas_call(body, name="rope_tables", out_shape=[jax.ShapeDtypeStruct((rows, LANES), F32)] * 2)(pos, freq, sign)


def _rope(x, cos, sin_signed):
    half = QK_ROPE_DIM // 2
    src = lax.broadcasted_iota(jnp.int32, (LANES, LANES), 0)
    dst = lax.broadcasted_iota(jnp.int32, (LANES, LANES), 1)
    swap = jnp.where(jnp.logical_or(jnp.logical_and(dst < half, src == dst + half),
                                    jnp.logical_and(jnp.logical_and(dst >= half, dst < 2 * half), src == dst - half)),
                     1.0, 0.0).astype(F32)
    swapped = _dot_exact(x, swap, ((1,), (0,)))
    return x * cos + swapped * sin_signed


def _attn_prep(q, kv, proj, kpe_col, cos, sin, *, tr=256):
    rows = q.shape[0]
    heads = q.shape[1] // HEAD_SLOT
    tr = _tile(rows, tr, SUBLANES)

    def body(q_ref, kv_ref, kpe_ref, cos_ref, sin_ref, qc_ref, kc_ref, v_ref):
        c, s = cos_ref[...], sin_ref[...]
        kpe = _rope(kpe_ref[...], c, s).astype(BF16)
        for h in range(heads):
            nope = slice(h * HEAD_SLOT, h * HEAD_SLOT + LANES)
            pe = slice(h * HEAD_SLOT + LANES, (h + 1) * HEAD_SLOT)
            qc_ref[:, nope] = q_ref[:, nope].astype(BF16)
            qc_ref[:, pe] = _rope(q_ref[:, pe], c, s).astype(BF16)
            kc_ref[:, nope] = kv_ref[:, nope].astype(BF16)
            kc_ref[:, pe] = kpe
            v_ref[:, h * LANES:(h + 1) * LANES] = kv_ref[:, pe].astype(BF16)

    slots = pl.BlockSpec((tr, heads * HEAD_SLOT), lambda i: (i, 0))
    tab = pl.BlockSpec((tr, LANES), lambda i: (i, 0))
    return pl.pallas_call(
        body, name="attn_prep", grid=(rows // tr,),
        in_specs=[slots, slots, pl.BlockSpec((tr, LANES), lambda i: (i, kpe_col)), tab, tab],
        out_specs=[slots, slots, pl.BlockSpec((tr, heads * LANES), lambda i: (i, 0))],
        out_shape=[jax.ShapeDtypeStruct((rows, heads * HEAD_SLOT), BF16)] * 2
        + [jax.ShapeDtypeStruct((rows, heads * LANES), BF16)],
        compiler_params=_params(("parallel",)),
    )(q, kv, proj, cos, sin)


def _causal(tq, tk):
    return lax.broadcasted_iota(jnp.int32, (tq, tk), 1) <= lax.broadcasted_iota(jnp.int32, (tq, tk), 0)


def _attn_fwd(qc, kc, vb, *, scale, tq=512):
    rows = qc.shape[0]
    heads = qc.shape[1] // HEAD_SLOT
    tq = _tile(rows, tq, SUBLANES)
    tk = tq

    def body(q_ref, k_ref, v_ref, o_ref, lse_ref):
        i = pl.program_id(1)
        q = q_ref[...]

        def step(j, carry, diagonal):
            m, l, acc = carry
            k0 = pl.multiple_of(j * tk, tk)
            s = _dot_nt(q, k_ref[pl.ds(k0, tk), :]) * scale
            if diagonal:
                s = jnp.where(_causal(tq, tk), s, NEG_INF)
            m_new = jnp.maximum(m, jnp.max(s, axis=-1, keepdims=True))
            p = jnp.exp(s - m_new)
            alpha = jnp.exp(m - m_new)
            l = alpha * l + jnp.sum(p, axis=-1, keepdims=True)
            acc = alpha * acc + _dot_nn(p.astype(BF16), v_ref[pl.ds(k0, tk), :])
            return m_new, l, acc

        init = (jnp.full((tq, 1), NEG_INF, F32), jnp.zeros((tq, 1), F32), jnp.zeros((tq, LANES), F32))
        below = lax.fori_loop(0, i, lambda j, carry: step(j, carry, False), init)
        m, l, acc = step(i, below, True)
        o_ref[...] = acc / l
        lse_ref[...] = jnp.broadcast_to(m + jnp.log(l), (tq, LANES))

    return pl.pallas_call(
        body, name="attn_fwd", grid=(heads, rows // tq),
        in_specs=[pl.BlockSpec((tq, HEAD_SLOT), lambda h, i: (i, h)), pl.BlockSpec((rows, HEAD_SLOT), lambda h, i: (0, h)),
                  pl.BlockSpec((rows, LANES), lambda h, i: (0, h))],
        out_specs=[pl.BlockSpec((tq, LANES), lambda h, i: (i, h))] * 2,
        out_shape=[jax.ShapeDtypeStruct((rows, heads * LANES), F32)] * 2,
        compiler_params=_params(("parallel", "parallel")),
    )(qc, kc, vb)


def _attn_bwd(qc, kc, vb, o, do, lse, cos, sin, *, scale, tk=512):
    rows = qc.shape[0]
    heads = qc.shape[1] // HEAD_SLOT
    tk = _tile(rows, tk, SUBLANES)
    tq = tk
    nq = rows // tq

    def body(q_ref, k_ref, v_ref, o_ref, do_ref, lse_ref, cos_ref, sin_ref, dq_ref, dkv_ref, dkpe_ref, dq_acc, delta_ref):
        j = pl.program_id(1)

        @pl.when(j == 0)
        def _():
            dq_acc[...] = jnp.zeros_like(dq_acc)
            for r0 in range(0, rows, tq):
                d = jnp.sum(do_ref[pl.ds(r0, tq), :] * o_ref[pl.ds(r0, tq), :], axis=-1, keepdims=True)
                delta_ref[pl.ds(r0, tq), :] = jnp.broadcast_to(d, (tq, LANES))

        kb, vv = k_ref[...], v_ref[...]

        def step(i, carry, diagonal):
            dk, dv = carry
            q0 = pl.multiple_of(i * tq, tq)
            qb = q_ref[pl.ds(q0, tq), :]
            dob = do_ref[pl.ds(q0, tq), :].astype(BF16)
            s = _dot_nt(qb, kb) * scale
            p = jnp.exp(s - lse_ref[pl.ds(q0, tq), :1])
            if diagonal:
                p = jnp.where(_causal(tq, tk), p, 0.0)
            dv = dv + _dot_tn(p.astype(BF16), dob)
            ds = (p * (_dot_nt(dob, vv) - delta_ref[pl.ds(q0, tq), :1])).astype(BF16)
            dk = dk + _dot_tn(ds, qb)
            dq_acc[pl.ds(q0, tq), :] += _dot_nn(ds, kb)
            return dk, dv

        zero = (jnp.zeros((tk, HEAD_SLOT), F32), jnp.zeros((tk, LANES), F32))
        dk, dv = lax.fori_loop(j + 1, nq, lambda i, carry: step(i, carry, False), step(j, zero, True))
        dkv_ref[:, :LANES] = (dk[:, :LANES] * scale).astype(dkv_ref.dtype)
        dkv_ref[:, LANES:] = dv.astype(dkv_ref.dtype)
        dkpe_ref[...] = dk[:, LANES:] * scale

        @pl.when(j == nq - 1)
        def _():
            for r0 in range(0, rows, tq):
                dq = dq_acc[pl.ds(r0, tq), :] * scale
                dq_ref[pl.ds(r0, tq), :LANES] = dq[:, :LANES].astype(dq_ref.dtype)
                dq_ref[pl.ds(r0, tq), LANES:] = _rope(dq[:, LANES:], cos_ref[pl.ds(r0, tq), :],
                                                      -sin_ref[pl.ds(r0, tq), :]).astype(dq_ref.dtype)

    full_q = pl.BlockSpec((rows, HEAD_SLOT), lambda h, j: (0, h))
    full_v = pl.BlockSpec((rows, LANES), lambda h, j: (0, h))
    tab = pl.BlockSpec((rows, LANES), lambda h, j: (0, 0))
    return pl.pallas_call(
        body, name="attn_bwd", grid=(heads, rows // tk),
        in_specs=[full_q, pl.BlockSpec((tk, HEAD_SLOT), lambda h, j: (j, h)), pl.BlockSpec((tk, LANES), lambda h, j: (j, h)),
                  full_v, full_v, full_v, tab, tab],
        out_specs=[full_q, pl.BlockSpec((tk, HEAD_SLOT), lambda h, j: (j, h)), pl.BlockSpec((tk, LANES), lambda h, j: (j, h))],
        out_shape=[jax.ShapeDtypeStruct((rows, heads * HEAD_SLOT), BF16), jax.ShapeDtypeStruct((rows, heads * HEAD_SLOT), BF16),
                   jax.ShapeDtypeStruct((rows, heads * LANES), F32)],
        scratch_shapes=[pltpu.VMEM((rows, HEAD_SLOT), F32), pltpu.VMEM((rows, LANES), F32)],
        compiler_params=_params(("parallel", "arbitrary")),
    )(qc, kc, vb, o, do, lse, cos, sin)


def _kpe_bwd(dkpe_heads, cos, sin, *, tr=512):
    rows = dkpe_heads.shape[0]
    heads = dkpe_heads.shape[1] // LANES
    tr = _tile(rows, tr, 2 * SUBLANES)

    def body(d_ref, cos_ref, sin_ref, o_ref):
        acc = d_ref[:, :LANES]
        for h in range(1, heads):
            acc = acc + d_ref[:, h * LANES:(h + 1) * LANES]
        o_ref[...] = _rope(acc, cos_ref[...], -sin_ref[...]).astype(o_ref.dtype)

    tab = pl.BlockSpec((tr, LANES), lambda i: (i, 0))
    return pl.pallas_call(
        body, name="kpe_bwd", grid=(rows // tr,),
        in_specs=[pl.BlockSpec((tr, heads * LANES), lambda i: (i, 0)), tab, tab], out_specs=tab,
        out_shape=jax.ShapeDtypeStruct((rows, LANES), BF16), compiler_params=_params(("parallel",)),
    )(dkpe_heads, cos, sin)


CONV_ROWS = 128


def _with_halo(ref, r0, ci, n_chunks, ch, lanes, before, after):
    parts = []
    if before:
        lo = pl.multiple_of(jnp.maximum(r0 - SUBLANES, 0), SUBLANES)
        parts.append(ref[pl.ds(lo, SUBLANES), lanes] * jnp.where(ci > 0, 1.0, 0.0))
    parts.append(ref[pl.ds(r0, ch), lanes])
    if after:
        hi = pl.multiple_of(jnp.minimum(r0 + ch, n_chunks * ch - SUBLANES), SUBLANES)
        parts.append(ref[pl.ds(hi, SUBLANES), lanes] * jnp.where(ci < n_chunks - 1, 1.0, 0.0))
    return jnp.concatenate(parts, axis=0)


def _taps(ext):
    return pltpu.roll(ext, 2, 0)[SUBLANES:], pltpu.roll(ext, 1, 0)[SUBLANES:], ext[SUBLANES:]


def _conv3(taps, w, b):
    return w[0:1, :] * taps[0] + w[1:2, :] * taps[1] + w[2:3, :] * taps[2] + b


def _conv_gate_fwd(a, conv_w, conv_b, *, tc=256):
    rows, f2 = a.shape
    f = f2 // 2
    tc = _tile(f, tc)
    nc = f // tc
    ch = _tile(rows, CONV_ROWS, SUBLANES)
    n_chunks = rows // ch

    def body(ag_ref, av_ref, wg_ref, wv_ref, bg_ref, bv_ref, o_ref):
        for lt in range(tc // LANES):
            lanes = slice(lt * LANES, (lt + 1) * LANES)
            wg, wv, bg, bv = wg_ref[:, lanes], wv_ref[:, lanes], bg_ref[:, lanes], bv_ref[:, lanes]

            def chunk(ci, carry):
                r0 = pl.multiple_of(ci * ch, ch)
                gate = _conv3(_taps(_with_halo(ag_ref, r0, ci, n_chunks, ch, lanes, True, False)), wg, bg)
                val = _conv3(_taps(_with_halo(av_ref, r0, ci, n_chunks, ch, lanes, True, False)), wv, bv)
                o_ref[pl.ds(r0, ch), lanes] = (gate * jax.nn.sigmoid(gate) * val).astype(o_ref.dtype)
                return carry

            lax.fori_loop(0, n_chunks, chunk, 0)

    return pl.pallas_call(
        body, name="conv_gate_fwd", grid=(nc,),
        in_specs=[pl.BlockSpec((rows, tc), lambda j: (0, j)), pl.BlockSpec((rows, tc), lambda j: (0, j + nc)),
                  pl.BlockSpec((SUBLANES, tc), lambda j: (0, j)), pl.BlockSpec((SUBLANES, tc), lambda j: (0, j + nc)),
                  pl.BlockSpec((1, tc), lambda j: (0, j)), pl.BlockSpec((1, tc), lambda j: (0, j + nc))],
        out_specs=pl.BlockSpec((rows, tc), lambda j: (0, j)),
        out_shape=jax.ShapeDtypeStruct((rows, f), BF16), compiler_params=_params(("parallel",)),
    )(a, a, conv_w, conv_w, conv_b, conv_b)


def _conv_gate_bwd(a, conv_w, conv_b, dg, *, tc=256):
    rows, f2 = a.shape
    f = f2 // 2
    tc = _tile(f, tc)
    nc = f // tc
    ch = _tile(rows, CONV_ROWS, SUBLANES)
    n_chunks = rows // ch
    ext_rows = ch + SUBLANES

    def fold(x):
        return jnp.sum(x.reshape(ch // SUBLANES, SUBLANES, LANES), axis=0)

    def body(ag_ref, av_ref, wg_ref, wv_ref, bg_ref, bv_ref, dg_ref, da_ref, dw_ref, db_ref):
        for lt in range(tc // LANES):
            lanes = slice(lt * LANES, (lt + 1) * LANES)
            wg, wv, bg, bv = wg_ref[:, lanes], wv_ref[:, lanes], bg_ref[:, lanes], bv_ref[:, lanes]

            def chunk(ci, acc):
                r0 = pl.multiple_of(ci * ch, ch)
                taps_g = _taps(_with_halo(ag_ref, r0, ci, n_chunks, ch, lanes, True, True))
                taps_v = _taps(_with_halo(av_ref, r0, ci, n_chunks, ch, lanes, True, True))
                dge = _with_halo(dg_ref, r0, ci, n_chunks, ch, lanes, False, True)
                gate, val = _conv3(taps_g, wg, bg), _conv3(taps_v, wv, bv)
                sg = jax.nn.sigmoid(gate)
                d_gate = dge * val * sg * (1.0 + gate * (1.0 - sg))
                d_val = dge * gate * sg
                new = []
                for half, (taps, w, d) in enumerate(((taps_g, wg, d_gate), (taps_v, wv, d_val))):
                    da = (w[2:3, :] * d[:ch] + w[1:2, :] * pltpu.roll(d, ext_rows - 1, 0)[:ch]
                          + w[0:1, :] * pltpu.roll(d, ext_rows - 2, 0)[:ch])
                    da_ref[half, pl.ds(r0, ch), lanes] = da.astype(da_ref.dtype)
                    dc = d[:ch]
                    sums = [fold(dc)] + [fold(dc * t[:ch]) for t in taps]
                    new.append(tuple(x + s for x, s in zip(acc[half], sums)))
                return tuple(new)

            zero = tuple(jnp.zeros((SUBLANES, LANES), F32) for _ in range(4))
            acc = lax.fori_loop(0, n_chunks, chunk, (zero, zero))
            row = lax.broadcasted_iota(jnp.int32, (SUBLANES, LANES), 0)
            for half in range(2):
                db, *taps = (jnp.sum(x, axis=0, keepdims=True) for x in acc[half])
                db_ref[half, :, lanes] = db
                dw = jnp.zeros((SUBLANES, LANES), F32)
                for tap in range(3):
                    dw = jnp.where(row == tap, taps[tap], dw)
                dw_ref[half, :, lanes] = dw

    lo = lambda j: (0, j)
    hi = lambda j: (0, j + nc)
    both = lambda j: (0, 0, j)
    return pl.pallas_call(
        body, name="conv_gate_bwd", grid=(nc,),
        in_specs=[pl.BlockSpec((rows, tc), lo), pl.BlockSpec((rows, tc), hi), pl.BlockSpec((SUBLANES, tc), lo),
                  pl.BlockSpec((SUBLANES, tc), hi), pl.BlockSpec((1, tc), lo), pl.BlockSpec((1, tc), hi),
                  pl.BlockSpec((rows, tc), lo)],
        out_specs=[pl.BlockSpec((2, rows, tc), both), pl.BlockSpec((2, SUBLANES, tc), both), pl.BlockSpec((2, 1, tc), both)],
        out_shape=[jax.ShapeDtypeStruct((2, rows, f), BF16), jax.ShapeDtypeStruct((2, SUBLANES, f), F32),
                   jax.ShapeDtypeStruct((2, 1, f), F32)],
        compiler_params=_params(("parallel",)),
    )(a, a, conv_w, conv_w, conv_b, conv_b, dg)


def _wgrad(a, b, rows, cols, row_sharded, name, **kw):
    return functools.partial(_wgrad_half, a, b, rows, cols, row_sharded, name, **kw)


def _block_diag(x):
    nb, g, r, c = x.shape
    eye = jnp.eye(g, dtype=x.dtype)
    return (x[:, :, :, None, :] * eye[None, :, None, :, None]).reshape(nb, g * r, g * c)


def _block_diag_part(x, r, c):
    nb = x.shape[0]
    g = GROUPS_PER_BATCH
    eye = jnp.eye(g, dtype=x.dtype)
    return jnp.sum(x.reshape(nb, g, r, g, c) * eye[None, :, None, :, None], axis=3)


class _NoExchange:
    def __init__(self, later, ffn):
        self.later, self.ffn = later, ffn

    def mixer_weights(self, after):
        return self.later

    def ffn_weights_arrived(self, after):
        return None

    def ffn_weights(self, after):
        return self.ffn

    def ffn_down_weight(self, after):
        return self.ffn["ffn_w_down"]

    def ffn_grads(self, makers, after):
        self.ffn_makers = makers
        return None

    def ffn_backward_done(self, after):
        return None


def _local_step(x, posf, target, w, hooks):
    rows, d = x.shape
    width = w["ssm_d"].shape[1]
    qr, kvr = w["mla_q_norm_w"].shape[1], w["mla_kv_norm_w"].shape[1]
    heads = w["mla_w_ukv"].shape[1] // HEAD_SLOT
    f2 = w["ffn_conv_b"].shape[1]
    inp = w["w_in"].shape[0]
    groups = width // SSM_GROUP
    nb = groups // GROUPS_PER_BATCH
    scale = (QK_NOPE_DIM + QK_ROPE_DIM) ** -0.5
    g = {}

    hn = _rmsnorm_fwd(x, w["attn_norm_w"], name="attn_norm")
    proj = _matmul(hn, w["w_in"], mode="nt", name="in_proj")

    ar, ai, bbr, bbi = _s5_params(w["ssm_lambda_re"], w["ssm_lambda_im"], w["ssm_log_dt"], w["ssm_b_re"], w["ssm_b_im"])

    def b_band(bb):
        return _block_diag(bb.reshape(nb, GROUPS_PER_BATCH, SSM_STATE, SSM_GROUP).transpose(0, 1, 3, 2))

    def c_band(c):
        return _block_diag(c.reshape(nb, GROUPS_PER_BATCH, SSM_GROUP, SSM_STATE).transpose(0, 1, 3, 2))

    wb = jnp.concatenate([b_band(bbr), b_band(bbi)], axis=2).astype(BF16)
    wc = jnp.concatenate([c_band(w["ssm_c_re"]), -c_band(w["ssm_c_im"])], axis=1).astype(BF16)
    abar = jnp.concatenate([ar.reshape(nb, 1, STATE_PER_BATCH), ai.reshape(nb, 1, STATE_PER_BATCH)], axis=2)
    states, y_pre, yg = _s5_fwd(proj, wb, wc, w["ssm_d"], abar)
    later = hooks.mixer_weights(yg)
    z = _matmul(yg, later["ssm_w_glu"], mode="nn", name="glu_proj", bias=w["ssm_b_glu"])
    ys = _glu_norm_fwd(y_pre, z, w["ssm_out_norm_w"])

    q_col, kv_col, kpe_col = width // qr, (width + qr) // kvr, (width + qr + kvr) // LANES
    assert width % qr == 0 and (width + qr) % kvr == 0
    qn = _rmsnorm_fwd(proj, w["mla_q_norm_w"], name="q_norm", width=qr, col=q_col)
    kvn = _rmsnorm_fwd(proj, w["mla_kv_norm_w"], name="kv_norm", width=kvr, col=kv_col)
    q = _matmul(qn, w["mla_w_uq"], mode="nn", name="q_proj")
    kv = _matmul(kvn, w["mla_w_ukv"], mode="nn", name="kv_proj")
    half = QK_ROPE_DIM // 2
    inv_freq = ROPE_THETA ** (-jnp.arange(0, QK_ROPE_DIM, 2, dtype=F32) / QK_ROPE_DIM)
    zeros = jnp.zeros((LANES - QK_ROPE_DIM,), F32)
    freq = jnp.concatenate([inv_freq, inv_freq, zeros]).reshape(1, LANES)
    sign = jnp.concatenate([-jnp.ones((half,), F32), jnp.ones((half,), F32), zeros]).reshape(1, LANES)
    cos, sin = _rope_tables(posf, freq, sign)
    qc, kc, vb = _attn_prep(q, kv, proj, kpe_col, cos, sin)
    o, lse = _attn_fwd(qc, kc, vb, scale=scale, tq=ATTN_BLOCK)
    ym = _rmsnorm_fwd(o, w["mla_out_norm_w"], name="mla_out_norm")
    ycat = jnp.concatenate([ys, ym], axis=1)
    h1 = _matmul(ycat, later["w_out"], mode="nn", name="out_proj", add=x, after=hooks.ffn_weights_arrived(ycat))

    hn2 = _rmsnorm_fwd(h1, w["ffn_norm_w"], name="ffn_norm")
    ffn = hooks.ffn_weights(hn2)
    a = _matmul(hn2, ffn["ffn_w_up"], mode="nn", name="ffn_up", tm=FFN_ROWS, after=ffn.get("started"))
    gated = _conv_gate_fwd(a, ffn["ffn_conv_w"], w["ffn_conv_b"])
    w_down = hooks.ffn_down_weight(gated)
    h2 = _matmul(gated, w_down, mode="nn", name="ffn_down", add=h1, tk=2816, tm=FFN_ROWS)
    loss_tile, dh2, dh2_mxu, g["final_norm_w"] = _final_norm_loss(h2, w["final_norm_w"], target)

    dgated = _matmul(dh2_mxu, w_down, mode="nt", name="ffn_down_dx", tm=FFN_ROWS)
    da, dcw, dcb = _conv_gate_bwd(a, ffn["ffn_conv_w"], w["ffn_conv_b"], dgated)
    g["ffn_conv_w"] = jnp.concatenate([dcw[0, :3], dcw[1, :3]], axis=1)
    g["ffn_conv_b"] = jnp.concatenate([dcb[0], dcb[1]], axis=1)
    started = hooks.ffn_grads({
        "ffn_w_up": _wgrad(hn2, da, d, f2, False, "ffn_up_dw", b_split=True, tn=_tile(f2 // N_CHIPS, 1408)),
        "ffn_w_down": _wgrad(gated, dh2_mxu, f2 // 2, d, True, "ffn_down_dw", tm=f2 // 2 // N_CHIPS, tn=512)}, dcb)
    dhn2 = _matmul(da, ffn["ffn_w_up"], mode="nt", name="ffn_up_dx", a_split=True, tk=_tile(f2 // 2, 2816), tm=FFN_ROWS,
                   after=started)
    dh1, dh1_mxu, g["ffn_norm_w"] = _rmsnorm_bwd(h1, w["ffn_norm_w"], dhn2, name="ffn_norm_bwd", add=dh2,
                                                dx_dtypes=(F32, BF16))

    dycat = _matmul(dh1_mxu, later["w_out"], mode="nt", name="out_proj_dx")
    g["w_out"] = _wgrad(ycat, dh1_mxu, 2 * width, d, True, "out_proj_dw")
    started = hooks.ffn_backward_done(dycat)
    mla_out_norm_w, ssm_out_norm_w = w["mla_out_norm_w"], w["ssm_out_norm_w"]
    if started is not None:
        mla_out_norm_w, ssm_out_norm_w = mla_out_norm_w + started[:1, :1], ssm_out_norm_w + started[:1, :1]

    do, g["mla_out_norm_w"] = _rmsnorm_bwd(o, mla_out_norm_w, dycat, name="mla_out_norm_bwd", width=width, dy_col=1)
    dq, dkv, dkpe_heads = _attn_bwd(qc, kc, vb, o, do, lse, cos, sin, scale=scale, tk=ATTN_BLOCK)
    dkpe = _kpe_bwd(dkpe_heads, cos, sin)
    g["mla_w_uq"] = _wgrad(qn, dq, qr, heads * HEAD_SLOT, False, "q_proj_dw")
    dqn = _matmul(dq, w["mla_w_uq"], mode="nt", name="q_proj_dx")
    dcq, g["mla_q_norm_w"] = _rmsnorm_bwd(proj, w["mla_q_norm_w"], dqn, name="q_norm_bwd", width=qr, col=q_col,
                                          dx_dtypes=(BF16,))
    g["mla_w_ukv"] = _wgrad(kvn, dkv, kvr, heads * HEAD_SLOT, False, "kv_proj_dw")
    dkvn = _matmul(dkv, w["mla_w_ukv"], mode="nt", name="kv_proj_dx")
    dckv, g["mla_kv_norm_w"] = _rmsnorm_bwd(proj, w["mla_kv_norm_w"], dkvn, name="kv_norm_bwd", width=kvr, col=kv_col,
                                            dx_dtypes=(BF16,))

    dz, dyg_a, g["ssm_out_norm_w"], g["ssm_b_glu"] = _glu_norm_bwd(y_pre, z, ssm_out_norm_w, dycat)
    dyg_b = _matmul(dz, later["ssm_w_glu"], mode="nt", name="glu_proj_dx")
    g["ssm_w_glu"] = _wgrad(yg, dz, width, width, True, "glu_proj_dw")
    du, dwb, dwc, dabar, g["ssm_d"] = _s5_bwd(proj, states, y_pre, dyg_a, dyg_b, wb, wc, w["ssm_d"], abar)

    def b_unband(x):
        return _block_diag_part(x, SSM_GROUP, SSM_STATE).transpose(0, 1, 3, 2).reshape(groups, SSM_STATE * SSM_GROUP)

    def c_unband(x):
        return _block_diag_part(x, SSM_STATE, SSM_GROUP).transpose(0, 1, 3, 2).reshape(groups, SSM_GROUP, SSM_STATE)

    st = STATE_PER_BATCH
    g["ssm_c_re"] = c_unband(dwc[:, :st, :])
    g["ssm_c_im"] = -c_unband(dwc[:, st:, :])
    d_ar = dabar[:, 0, :st].reshape(groups, SSM_STATE)
    d_ai = dabar[:, 0, st:].reshape(groups, SSM_STATE)
    (g["ssm_lambda_re"], g["ssm_lambda_im"], g["ssm_log_dt"], g["ssm_b_re"], g["ssm_b_im"]) = _s5_params_bwd(
        w["ssm_lambda_re"], w["ssm_lambda_im"], w["ssm_log_dt"], w["ssm_b_re"], w["ssm_b_im"], d_ar, d_ai,
        b_unband(dwb[:, :, :st]), b_unband(dwb[:, :, st:]))

    pad = jnp.zeros((rows, inp - (width + qr + kvr + LANES)), BF16)
    dproj = jnp.concatenate([du, dcq, dckv, dkpe, pad], axis=1)
    g["w_in"] = _wgrad(dproj, hn, inp, d, False, "in_proj_dw")
    dhn = _matmul(dproj, w["w_in"], mode="nn", name="in_proj_dx")
    dx, g["attn_norm_w"] = _rmsnorm_bwd(x, w["attn_norm_w"], dhn, name="attn_norm_bwd", add=dh1)
    return loss_tile, dx, g


ANY = pl.BlockSpec(memory_space=pl.ANY)
MESH = pl.DeviceIdType.MESH


def _mesh_pos():
    return lax.axis_index("x"), lax.axis_index("y"), lax.axis_index("c")


def _other_chips(x, y):
    return [(1 - x, y), (x, 1 - y), (1 - x, 1 - y)]


def _remote(src, dst, send_sems, recv_sems, k, to):
    return pltpu.make_async_remote_copy(src_ref=src, dst_ref=dst, send_sem=send_sems.at[k], recv_sem=recv_sems.at[k],
                                        device_id=to, device_id_type=MESH)


def _place_shard(shard, piece_idx, row_sharded, name, out_dtype=BF16, pieces=N_CHIPS):
    rs, cs = shard.shape
    tr = _tile(rs, 256, 2 * SUBLANES)
    rb = rs // tr

    def body(p_ref, x_ref, o_ref):
        o_ref[...] = x_ref[...].astype(o_ref.dtype)

    if row_sharded:
        out_shape, out_map = (pieces * rs, cs), (lambda i, p_ref: (p_ref[0] * rb + i, 0))
    else:
        out_shape, out_map = (rs, pieces * cs), (lambda i, p_ref: (i, p_ref[0]))
    return pl.pallas_call(
        body, name=name, out_shape=jax.ShapeDtypeStruct(out_shape, out_dtype),
        grid_spec=pltpu.PrefetchScalarGridSpec(
            num_scalar_prefetch=1, grid=(rb,), in_specs=[pl.BlockSpec((tr, cs), lambda i, p_ref: (i, 0))],
            out_specs=pl.BlockSpec((tr, cs), out_map)),
        compiler_params=_params(("parallel",)),
    )(piece_idx, shard)


def _gather_weights(placed, name):
    n = len(placed)
    meta = [(row_sharded, direct) for _, row_sharded, direct in placed]
    over_ici, over_d2d = _gather_plans(meta)
    forwarded = [t for t, (_, direct) in enumerate(meta) if not direct]

    def body(*refs):
        outs = refs[n:2 * n]
        send_sems, recv_sems, pass_send_sems, pass_recv_sems = refs[2 * n:]
        first, arrivals = over_ici(outs, send_sems, recv_sems)
        passed, passed_arrivals = over_d2d([outs[t] for t in forwarded], pass_send_sems, pass_recv_sems)
        for cp in first:
            cp.start()
        for t in range(n):
            for j in range(3):
                arrivals[3 * t + j].wait_recv()
                if t in forwarded:
                    passed[3 * forwarded.index(t) + j].start()
        for cp in passed_arrivals:
            cp.wait_recv()
        for cp in first + passed:
            cp.wait_send()

    return pl.pallas_call(
        body, name=name, in_specs=[ANY] * n, out_specs=[ANY] * n,
        out_shape=[jax.ShapeDtypeStruct(arr.shape, arr.dtype) for arr, _, _ in placed],
        input_output_aliases={t: t for t in range(n)},
        scratch_shapes=[pltpu.SemaphoreType.DMA((3 * n,)), pltpu.SemaphoreType.DMA((3 * n,)),
                        pltpu.SemaphoreType.DMA((3 * len(forwarded),)), pltpu.SemaphoreType.DMA((3 * len(forwarded),))],
    )(*[arr for arr, _, _ in placed])


def _gather_plans(meta):
    def window(ref, row_sharded, piece, half):
        r, cc = ref.shape
        if row_sharded:
            rs = r // N_CHIPS
            if half is None:
                return ref.at[pl.ds(piece * rs, rs), :]
            return ref.at[pl.ds(piece * rs + half * (rs // 2), rs // 2), :]
        cs = cc // N_CHIPS
        if half is None:
            return ref.at[:, pl.ds(piece * cs, cs)]
        return ref.at[pl.ds(half * (r // 2), r // 2), pl.ds(piece * cs, cs)]

    def over_ici(refs, send_sems, recv_sems):
        x, y, c = _mesh_pos()
        sends, recvs = [], []
        for t, (row_sharded, direct) in enumerate(meta):
            mine = window(refs[t], row_sharded, 2 * x + y, None if direct else c)
            for j, (px, py) in enumerate(_other_chips(x, y)):
                theirs = window(refs[t], row_sharded, 2 * px + py, None if direct else c)
                sends.append(_remote(mine, mine, send_sems, recv_sems, 3 * t + j, (px, py, c)))
                recvs.append(_remote(theirs, theirs, send_sems, recv_sems, 3 * t + j, (px, py, c)))
        return sends, recvs

    def over_d2d(refs, send_sems, recv_sems):
        x, y, c = _mesh_pos()
        sends, recvs = [], []
        rows = [row_sharded for row_sharded, direct in meta if not direct]
        for t, row_sharded in enumerate(rows):
            for j, (px, py) in enumerate(_other_chips(x, y)):
                got = window(refs[t], row_sharded, 2 * px + py, c)
                other = window(refs[t], row_sharded, 2 * px + py, 1 - c)
                sends.append(_remote(got, got, send_sems, recv_sems, 3 * t + j, (x, y, 1 - c)))
                recvs.append(_remote(other, other, send_sems, recv_sems, 3 * t + j, (x, y, 1 - c)))
        return sends, recvs

    return over_ici, over_d2d


HBM = pl.BlockSpec(memory_space=pltpu.HBM)
SEMAPHORES = pl.BlockSpec(memory_space=pltpu.SEMAPHORE)
DATAFLOW = pltpu.SideEffectType.DATAFLOW_SIDE_EFFECTING


def _start_copies(name, arrays, plan, n_copies, after):
    n = len(arrays)

    def body(*refs):
        sends, _ = plan(refs[:n], refs[n + 1], refs[n + 2])
        for cp in sends:
            cp.start()
        token = refs[2 * n + 3]
        token[...] = jnp.zeros_like(token)

    out = pl.pallas_call(
        body, name=name,
        out_shape=(pltpu.SemaphoreType.DMA((n_copies,)), pltpu.SemaphoreType.DMA((n_copies,)),
                   *[pltpu.HBM(a.shape, a.dtype) for a in arrays], jax.ShapeDtypeStruct((SUBLANES, LANES), F32)),
        in_specs=[HBM] * n + [ANY],
        out_specs=(SEMAPHORES, SEMAPHORES, *[HBM] * n, pl.BlockSpec(memory_space=pltpu.VMEM)),
        input_output_aliases={t: t + 2 for t in range(n)},
        compiler_params=pltpu.CompilerParams(has_side_effects=DATAFLOW),
    )(*[pltpu.with_memory_space_constraint(a, pltpu.HBM) for a in arrays], after)
    return out[0], out[1], list(out[2:2 + n]), out[2 + n]


def _wait_copies(name, started, plan, after):
    send_sems, recv_sems, arrays, _ = started
    n = len(arrays)

    def body(*refs):
        sends, recvs = plan(refs[:n], refs[n], refs[n + 1])
        for cp in sends:
            cp.wait_send()
        for cp in recvs:
            cp.wait_recv()

    out = pl.pallas_call(
        body, name=name, out_shape=[pltpu.HBM(a.shape, a.dtype) for a in arrays],
        in_specs=[HBM] * n + [SEMAPHORES, SEMAPHORES, ANY], out_specs=[HBM] * n,
        input_output_aliases={t: t for t in range(n)},
        compiler_params=pltpu.CompilerParams(has_side_effects=DATAFLOW),
    )(*arrays, send_sems, recv_sems, after)
    return list(out)


def _exchange(name, arrays, out_shapes, plan, n_copies, in_place=False, after=None):
    n = len(arrays)
    extra = [] if after is None else [after]

    def body(*refs):
        ins, outs = refs[:n], refs[n + len(extra):n + len(extra) + len(out_shapes)]
        send_sems, recv_sems = refs[n + len(extra) + len(out_shapes):]
        sends, recvs = plan(ins, outs, send_sems, recv_sems)
        for cp in sends:
            cp.start()
        for cp in recvs:
            cp.wait_recv()
        for cp in sends:
            cp.wait_send()

    return pl.pallas_call(
        body, name=name, in_specs=[ANY] * (n + len(extra)), out_specs=[ANY] * len(out_shapes), out_shape=out_shapes,
        input_output_aliases={t: t for t in range(n)} if in_place else {},
        scratch_shapes=[pltpu.SemaphoreType.DMA((n_copies,)), pltpu.SemaphoreType.DMA((n_copies,))],
    )(*arrays, *extra)


def _give_plan(n):
    def plan(refs, send_sems, recv_sems):
        x, y, c = _mesh_pos()
        sends = [_remote(refs[t], refs[n + t], send_sems, recv_sems, t, (x, y, 1 - c)) for t in range(n)]
        return sends, sends

    return plan


def _scatter_plan(n):
    def plan(refs, send_sems, recv_sems):
        x, y, c = _mesh_pos()
        sends = []
        for t in range(n):
            for j, (px, py) in enumerate(_other_chips(x, y)):
                sends.append(_remote(refs[t].at[2 * px + py], refs[n + t].at[j], send_sems, recv_sems, 3 * t + j, (px, py, c)))
        return sends, sends

    return plan


def _scatter_shapes(sums):
    return [jax.ShapeDtypeStruct((3,) + s.shape[1:], s.dtype) for s in sums]


def _join_plan(n):
    def plan(refs, send_sems, recv_sems):
        x, y, c = _mesh_pos()
        sends = [_remote(refs[t].at[c], refs[t].at[c], send_sems, recv_sems, t, (x, y, 1 - c)) for t in range(n)]
        recvs = [_remote(refs[t].at[1 - c], refs[t].at[1 - c], send_sems, recv_sems, t, (x, y, 1 - c)) for t in range(n)]
        return sends, recvs

    return plan


def _join_halves(halves, name, after=None):
    plan = _join_plan(len(halves))
    shapes = [jax.ShapeDtypeStruct(h.shape, h.dtype) for h in halves]
    return _exchange(name, halves, shapes, lambda ins, outs, s, r: plan(outs, s, r), len(halves), in_place=True, after=after)


def _add_other_half(g4, got, where, name, wire_dtype=BF16):
    _, pieces, sr, sc = g4.shape
    tr = _tile(sr, 256, 2 * SUBLANES)

    def body(w_ref, a_ref, b_ref, o_ref):
        o_ref[...] = (a_ref[...] + b_ref[...]).astype(o_ref.dtype)

    blk = pl.BlockSpec((None, tr, sc), lambda p, i, w_ref: (p, i, 0))
    return pl.pallas_call(
        body, name=name, out_shape=jax.ShapeDtypeStruct((pieces, sr, sc), wire_dtype),
        grid_spec=pltpu.PrefetchScalarGridSpec(
            num_scalar_prefetch=1, grid=(pieces, sr // tr),
            in_specs=[pl.BlockSpec((None, None, tr, sc), lambda p, i, w_ref: (w_ref[0], p, i, 0)), blk], out_specs=blk),
        compiler_params=_params(("parallel", "parallel")),
    )(where, g4, got)


def _add_pieces(sums, got_pieces, where, name):
    _, sr, sc = sums.shape
    tr = _tile(sr, 256, 2 * SUBLANES)

    def body(w_ref, a_ref, r_ref, o_ref):
        acc = a_ref[...]
        for j in range(3):
            acc = acc + r_ref[j].astype(F32)
        o_ref[...] = acc

    return pl.pallas_call(
        body, name=name, out_shape=jax.ShapeDtypeStruct((N_CORES, sr, sc), F32),
        grid_spec=pltpu.PrefetchScalarGridSpec(
            num_scalar_prefetch=1, grid=(sr // tr,),
            in_specs=[pl.BlockSpec((None, tr, sc), lambda i, w_ref: (w_ref[1], i, 0)),
                      pl.BlockSpec((3, tr, sc), lambda i, w_ref: (0, i, 0))],
            out_specs=pl.BlockSpec((None, tr, sc), lambda i, w_ref: (w_ref[0], i, 0))),
        compiler_params=_params(("parallel",)),
    )(where, sums, got_pieces)


def _adamw_update(w, g, m, v):
    nm = ADAM_B1 * m + (1.0 - ADAM_B1) * g
    nv = ADAM_B2 * v + (1.0 - ADAM_B2) * (g * g)
    m_hat = nm / (1.0 - ADAM_B1 ** ADAM_STEP)
    v_hat = nv / (1.0 - ADAM_B2 ** ADAM_STEP)
    return -ADAM_LR * (m_hat / (jnp.sqrt(v_hat) + ADAM_EPS) + ADAM_WD * w), nm, nv


def _adamw(w, g, m, v, name, after=None):
    rows, cols = w.shape
    halves = 2 if g.ndim == 3 else 1
    bc = cols // halves
    tr = _tile(rows, max(SUBLANES, (1 << 19) // max(bc, 1) // SUBLANES * SUBLANES), SUBLANES)

    def body(w_ref, g_ref, m_ref, v_ref, *rest):
        d_ref, nm_ref, nv_ref, go_ref = rest[-4:]
        gv = g_ref[...]
        d_ref[...], nm_ref[...], nv_ref[...] = _adamw_update(w_ref[...], gv, m_ref[...], v_ref[...])
        go_ref[...] = gv

    blk = pl.BlockSpec((tr, bc), lambda i, h: (i, h))
    g_blk = pl.BlockSpec((None, tr, bc), lambda i, h: (h, i, 0)) if halves == 2 else blk
    extra = [] if after is None else [after]
    return pl.pallas_call(
        body, name=name, grid=(rows // tr, halves),
        in_specs=[blk, g_blk, blk, blk] + [pl.BlockSpec(memory_space=pl.ANY)] * len(extra), out_specs=[blk] * 4,
        out_shape=[jax.ShapeDtypeStruct((rows, cols), F32)] * 4, compiler_params=_params(("parallel", "parallel")),
    )(w, g, m, v, *extra)


def _adamw_many(ws, gs, ms, vs, name):
    n = len(ws)

    def body(*refs):
        outs = refs[4 * n:]
        for k in range(n):
            w_ref, g_ref, m_ref, v_ref = (refs[j * n + k] for j in range(4))
            outs[k][...], outs[n + k][...], outs[2 * n + k][...] = _adamw_update(w_ref[...], g_ref[...], m_ref[...], v_ref[...])

    out = pl.pallas_call(
        body, name=name, out_shape=[jax.ShapeDtypeStruct(w.shape, F32) for w in ws] * 3,
        compiler_params=pltpu.CompilerParams(vmem_limit_bytes=VMEM_LIMIT_BYTES),
    )(*ws, *gs, *ms, *vs)
    return out[:n], out[n:2 * n], out[2 * n:]


WEIGHTS = ['attn_norm_w', 'w_in', 'ssm_lambda_re', 'ssm_lambda_im', 'ssm_log_dt', 'ssm_b_re', 'ssm_b_im', 'ssm_c_re',
           'ssm_c_im', 'ssm_d', 'ssm_w_glu', 'ssm_b_glu', 'mla_q_norm_w', 'mla_w_uq', 'mla_kv_norm_w', 'mla_w_ukv',
           'ssm_out_norm_w', 'mla_out_norm_w', 'w_out', 'ffn_norm_w', 'ffn_w_up', 'ffn_conv_w', 'ffn_conv_b',
           'ffn_w_down', 'final_norm_w']
SHARDED = {'w_in': False, 'ssm_w_glu': True, 'mla_w_uq': False, 'mla_w_ukv': False, 'w_out': True, 'ffn_w_up': False,
           'ffn_w_down': True}
SMALL = [n for n in WEIGHTS if n not in SHARDED and n != 'ffn_conv_w']
ROPE_PAD = HEAD_SLOT - QK_NOPE_DIM - QK_ROPE_DIM
SMALL_COLS = 8 * LANES


def _pad_heads(w_uq, heads):
    qr = w_uq.shape[0]
    w3 = w_uq.reshape(qr, heads, QK_NOPE_DIM + QK_ROPE_DIM)
    return jnp.concatenate([w3, jnp.zeros((qr, heads, ROPE_PAD), w_uq.dtype)], axis=2).reshape(qr, heads * HEAD_SLOT)


def _unpad_heads(g_uq, heads):
    qr = g_uq.shape[0]
    return g_uq.reshape(qr, heads, HEAD_SLOT)[:, :, :QK_NOPE_DIM + QK_ROPE_DIM].reshape(qr, -1)


FFN = ['ffn_w_up', 'ffn_w_down']
MIXER_LATER = ['ssm_w_glu', 'w_out']
FFN_GATHER = FFN + ['ffn_conv_w']
FFN_GATHER_META = [(SHARDED[n], False) for n in FFN] + [(False, True)]


class _Overlapped:
    def __init__(self, placed_later, placed, where, after):
        self.where, self.mine, self.other = where, where[:1], 1 - where[:1]
        self.later_ici, self.later_d2d = _gather_plans([(SHARDED[n], False) for n in MIXER_LATER])
        self.later = _start_copies("gather_later_start", placed_later, self.later_ici, 3 * len(placed_later), after)
        up, down, taps = placed
        self.up_ici, self.up_d2d = _gather_plans([(SHARDED["ffn_w_up"], False)])
        self.up = _start_copies("gather_ffn_up_start", [up], self.up_ici, 3, self.later[3])
        self.down_ici, self.down_d2d = _gather_plans([(SHARDED["ffn_w_down"], False), (False, True)])
        self.down = _start_copies("gather_ffn_down_start", [down, taps], self.down_ici, 6, self.up[3])
        self.gather_started = self.down[3]

    def mixer_weights(self, after):
        arrived = _wait_copies("gather_later_wait", self.later, self.later_ici, after)
        shapes = [jax.ShapeDtypeStruct(a.shape, a.dtype) for a in arrived]
        passed = _exchange("gather_later_pass", arrived, shapes, lambda ins, outs, s, r: self.later_d2d(outs, s, r),
                           3 * len(arrived), in_place=True)
        return dict(zip(MIXER_LATER, passed))

    def ffn_weights_arrived(self, after):
        arrived = _wait_copies("gather_ffn_up_wait", self.up, self.up_ici, after)
        self.up_passing = _start_copies("gather_ffn_up_pass_start", arrived, self.up_d2d, 3, after)
        return self.up_passing[3]

    def ffn_weights(self, after):
        w_up, = _wait_copies("gather_ffn_up_pass_wait", self.up_passing, self.up_d2d, after)
        down, taps = _wait_copies("gather_ffn_down_wait", self.down, self.down_ici, after)
        self.down_passing = _start_copies("gather_ffn_down_pass_start", [down], self.down_d2d, 3, w_up)
        return {"ffn_w_up": w_up, "ffn_conv_w": taps, "started": self.down_passing[3]}

    def ffn_down_weight(self, after):
        return _wait_copies("gather_ffn_down_pass_wait", self.down_passing, self.down_d2d, after)[0]

    def ffn_grads(self, makers, after):
        self.makers = [makers[name] for name in FFN]
        n = len(FFN)
        give = [make(self.other, suffix="_give") for make in self.makers]
        lands = [lax.empty(g.shape, g.dtype) for g in give]
        self.swap = _start_copies("grad_ffn_swap_start", give + lands, _give_plan(n), n, after)
        return self.swap[3]

    def ffn_backward_done(self, after):
        n = len(FFN)
        got = _wait_copies("grad_ffn_swap_wait", self.swap, _give_plan(n), after)[n:]
        kept = [make(self.mine, suffix="_keep", add=got[t], wire=True) for t, make in enumerate(self.makers)]
        self.sums = [k[0] for k in kept]
        wires = [k[1] for k in kept]
        lands = [lax.empty(s.shape, s.dtype) for s in _scatter_shapes(wires)]
        self.scatter = _start_copies("grad_ffn_scatter_start", wires + lands, _scatter_plan(n), 3 * n, after)
        return self.scatter[3]

    def ffn_reduced(self, after):
        n = len(FFN)
        got_pieces = _wait_copies("grad_ffn_scatter_wait", self.scatter, _scatter_plan(n), after)[n:]
        return [_add_pieces(self.sums[t], got_pieces[t], self.where, "grad_add_pieces_" + name) for t, name in enumerate(FFN)]


def _step(args):
    x, positions, target = args["x"][0], args["positions"], args["loss_target"][0]
    rows = x.shape[0]
    p = {n: args[n] for n in WEIGHTS}
    xi, yi, ci = _mesh_pos()
    piece = 2 * xi + yi

    def transposed(a):
        return jnp.swapaxes(a[0], 0, 1)

    w_in = transposed(p["w_in"])
    in_width = w_in.shape[0]
    in_pad = (-in_width) % (2 * LANES)
    heads_here = p["mla_w_uq"].shape[2] // (QK_NOPE_DIM + QK_ROPE_DIM)
    shards = {
        "w_in": jnp.pad(w_in, ((0, in_pad), (0, 0))),
        "ssm_w_glu": p["ssm_w_glu"][0],
        "mla_w_uq": _pad_heads(p["mla_w_uq"][0], heads_here),
        "mla_w_ukv": p["mla_w_ukv"][0],
        "w_out": p["w_out"][0],
        "ffn_w_up": p["ffn_w_up"][0],
        "ffn_w_down": p["ffn_w_down"][0],
    }
    conv_w = jnp.pad(p["ffn_conv_w"][0], ((0, SUBLANES - p["ffn_conv_w"].shape[1]), (0, 0)))
    order = list(SHARDED)
    piece_idx = piece.reshape(1).astype(jnp.int32)
    placed = {n: _place_shard(shards[n], piece_idx, SHARDED[n], "place_" + n) for n in order}
    placed["ffn_conv_w"] = _place_shard(conv_w, piece_idx, False, "place_ffn_conv_w", out_dtype=F32)
    mixer = [n for n in order if n not in FFN]
    first = [n for n in mixer if n not in MIXER_LATER]
    w = dict(zip(first, _gather_weights([(placed[n], SHARDED[n], False) for n in first], "gather_first_weights")))
    where = jnp.stack([ci, piece]).astype(jnp.int32)
    hooks = _Overlapped([placed[n] for n in MIXER_LATER], [placed[n] for n in FFN_GATHER], where, after=w["w_in"])
    groups = p["ssm_lambda_re"].shape[1]
    w.update({
        "attn_norm_w": p["attn_norm_w"] + hooks.gather_started[:1, :1],
        "ssm_lambda_re": p["ssm_lambda_re"][0], "ssm_lambda_im": p["ssm_lambda_im"][0],
        "ssm_log_dt": p["ssm_log_dt"].reshape(groups, 1), "ssm_b_re": p["ssm_b_re"].reshape(groups, -1),
        "ssm_b_im": p["ssm_b_im"].reshape(groups, -1), "ssm_c_re": p["ssm_c_re"][0], "ssm_c_im": p["ssm_c_im"][0],
        "ssm_d": p["ssm_d"], "ssm_b_glu": p["ssm_b_glu"], "mla_q_norm_w": p["mla_q_norm_w"],
        "mla_kv_norm_w": p["mla_kv_norm_w"], "ssm_out_norm_w": p["ssm_out_norm_w"], "mla_out_norm_w": p["mla_out_norm_w"],
        "ffn_norm_w": p["ffn_norm_w"], "ffn_conv_b": p["ffn_conv_b"], "final_norm_w": p["final_norm_w"].reshape(1, -1),
    })

    loss_tile, dx, g = _local_step(x, positions.reshape(rows, 1).astype(F32), target, w, hooks)
    loss = lax.psum(loss_tile[0, 0], ("x", "y", "c"))

    flat = [g[n].reshape(-1) for n in SMALL] + [g["ffn_conv_w"].reshape(-1)]
    sizes = [f.shape[0] for f in flat]
    per_block = -(-sum(sizes) // (N_CORES * N_CHIPS * SMALL_COLS))
    small_rows = -(-per_block // (2 * SUBLANES)) * (2 * SUBLANES)
    padded = N_CORES * N_CHIPS * small_rows * SMALL_COLS

    def pack(parts):
        parts = list(parts)
        have = sum(q.shape[0] for q in parts)
        return jnp.concatenate(parts + [jnp.zeros((padded - have,), F32)])

    reduced = mixer + ["small"]
    small = pack(flat).reshape(N_CORES, N_CHIPS, small_rows, SMALL_COLS)
    give = [g[n](hooks.other, suffix="_give") for n in mixer] + [lax.dynamic_index_in_dim(small, 1 - ci, 0, keepdims=False)]
    lands = [lax.empty(a.shape, a.dtype) for a in give]
    give_plan = _give_plan(len(reduced))
    swap = _start_copies("grad_mixer_swap_start", give + lands, give_plan, len(reduced), dx)

    grads, delta, new_m, new_v = {}, {}, {}, {}

    def finish(n, joined, after=None):
        grad = joined if SHARDED[n] else joined.reshape(-1, joined.shape[2])
        if n == "w_in":
            wt, mt, vt = w_in, transposed(args["m_w_in"]), transposed(args["v_w_in"])
            out = _adamw(wt, grad, mt, vt, "adamw_w_in")
            delta[n], new_m[n], new_v[n], grads[n] = (jnp.swapaxes(a, 0, 1)[None] for a in out)
            return
        if n == "mla_w_uq":
            grad = _unpad_heads(grad, heads_here)
        adam(n, grad, after)

    def adam(n, grad, after=None):
        shape = p[n].shape
        out = _adamw(p[n].reshape(shape[1:]), grad, args["m_" + n].reshape(shape[1:]),
                     args["v_" + n].reshape(shape[1:]), "adamw_" + n, after)
        delta[n], new_m[n], new_v[n], grads[n] = (a.reshape(shape) for a in out)

    ffn_halves = hooks.ffn_reduced(swap[3])
    got = _wait_copies("grad_mixer_swap_wait", swap, give_plan, ffn_halves[-1])[len(reduced):]
    join_plan = _join_plan(len(FFN))
    ffn_join = _start_copies("grad_ffn_join_start", ffn_halves, join_plan, len(FFN), got[0])
    kept = [g[n](hooks.mine, suffix="_keep", add=got[t], wire=True) for t, n in enumerate(mixer)]
    small_sum = _add_other_half(small, got[-1], where, "grad_add_half_small", F32)
    sums = [k[0] for k in kept] + [small_sum]
    wires = [k[1] for k in kept] + [small_sum]
    ffn_joined = _wait_copies("grad_ffn_join_wait", ffn_join, join_plan, kept[-1][0])
    lands = [lax.empty(s.shape, s.dtype) for s in _scatter_shapes(wires)]
    scatter_plan = _scatter_plan(len(reduced))
    scatter = _start_copies("grad_mixer_scatter_start", wires + lands, scatter_plan, 3 * len(reduced), ffn_joined[0])
    behind = scatter[3]
    for n, joined in zip(FFN, ffn_joined):
        finish(n, joined, after=behind)
        behind = delta[n]
    got_pieces = _wait_copies("grad_mixer_scatter_wait", scatter, scatter_plan, delta[FFN[-1]])[len(reduced):]
    halves = [_add_pieces(sums[t], got_pieces[t], where, "grad_add_pieces_" + n) for t, n in enumerate(reduced)]
    joined = _join_halves(halves, "grad_join_halves")
    for n, j in zip(mixer, joined):
        finish(n, j)
    eighths = _place_shard(joined[-1].reshape(N_CORES * small_rows, SMALL_COLS), piece_idx, True, "place_small_grads",
                           out_dtype=F32)
    small_sum = _gather_weights([(eighths, True, False)], "gather_small_grads")[0]
    flat_sum = small_sum.reshape(N_CHIPS, N_CORES, small_rows * SMALL_COLS).transpose(1, 0, 2).reshape(-1)
    offs = [0]
    for s in sizes:
        offs.append(offs[-1] + s)
    for k, n in enumerate(SMALL):
        grads[n] = flat_sum[offs[k]:offs[k + 1]].reshape(p[n].shape)
    taps, cols_here = p["ffn_conv_w"].shape[1], p["ffn_conv_w"].shape[2]
    conv_full = flat_sum[offs[len(SMALL)]:offs[len(SMALL) + 1]].reshape(taps, N_CHIPS * cols_here)
    adam("ffn_conv_w", lax.dynamic_slice_in_dim(conv_full, piece * cols_here, cols_here, axis=1))

    def rank2(a):
        return a.reshape(1, -1) if a.ndim == 1 else a

    d_s, m_s, v_s = _adamw_many([rank2(p[n]) for n in SMALL], [rank2(grads[n]) for n in SMALL],
                                [rank2(args["m_" + n]) for n in SMALL], [rank2(args["v_" + n]) for n in SMALL], "adamw_small")
    for k, n in enumerate(SMALL):
        delta[n], new_m[n], new_v[n] = (a.reshape(p[n].shape) for a in (d_s[k], m_s[k], v_s[k]))

    return (loss, dx[None], *[grads[n] for n in WEIGHTS], *[delta[n] for n in WEIGHTS],
            *[new_m[n] for n in WEIGHTS], *[new_v[n] for n in WEIGHTS])


def kernel(x, positions, attn_norm_w, w_in, ssm_lambda_re, ssm_lambda_im, ssm_log_dt, ssm_b_re, ssm_b_im, ssm_c_re, ssm_c_im, ssm_d, ssm_w_glu, ssm_b_glu, mla_q_norm_w, mla_w_uq, mla_kv_norm_w, mla_w_ukv, ssm_out_norm_w, mla_out_norm_w, w_out, ffn_norm_w, ffn_w_up, ffn_conv_w, ffn_conv_b, ffn_w_down, final_norm_w, loss_target, m_attn_norm_w, m_w_in, m_ssm_lambda_re, m_ssm_lambda_im, m_ssm_log_dt, m_ssm_b_re, m_ssm_b_im, m_ssm_c_re, m_ssm_c_im, m_ssm_d, m_ssm_w_glu, m_ssm_b_glu, m_mla_q_norm_w, m_mla_w_uq, m_mla_kv_norm_w, m_mla_w_ukv, m_ssm_out_norm_w, m_mla_out_norm_w, m_w_out, m_ffn_norm_w, m_ffn_w_up, m_ffn_conv_w, m_ffn_conv_b, m_ffn_w_down, m_final_norm_w, v_attn_norm_w, v_w_in, v_ssm_lambda_re, v_ssm_lambda_im, v_ssm_log_dt, v_ssm_b_re, v_ssm_b_im, v_ssm_c_re, v_ssm_c_im, v_ssm_d, v_ssm_w_glu, v_ssm_b_glu, v_mla_q_norm_w, v_mla_w_uq, v_mla_kv_norm_w, v_mla_w_ukv, v_ssm_out_norm_w, v_mla_out_norm_w, v_w_out, v_ffn_norm_w, v_ffn_w_up, v_ffn_conv_w, v_ffn_conv_b, v_ffn_w_down, v_final_norm_w):
    return _step(dict(locals()))
```

```python
import functools
import math

import jax
import jax.numpy as jnp
from jax import lax
from jax.experimental import pallas as pl
from jax.experimental.pallas import tpu as pltpu

F32 = jnp.float32
BF16 = jnp.bfloat16

SSM_GROUP = 16
SSM_STATE = 64
QK_NOPE_DIM = 128
QK_ROPE_DIM = 64
V_HEAD_DIM = 128
ROPE_THETA = 10000.0
RMS_EPS = 1e-6
ADAM_LR, ADAM_B1, ADAM_B2, ADAM_EPS, ADAM_WD, ADAM_STEP = 0.001, 0.9, 0.999, 1e-08, 0.01, 10

LANES = 128
SUBLANES = 8
VMEM_LIMIT_BYTES = 56 * 1024 * 1024

GROUPS_PER_BATCH = LANES // SSM_GROUP
STATE_PER_BATCH = GROUPS_PER_BATCH * SSM_STATE
HEAD_SLOT = 2 * LANES
NEG_INF = -1e30
ATTN_BLOCK = 512
FFN_ROWS = 1024

N_CHIPS = 4
N_CORES = 2


def _tile(n, pref, align=LANES):
    if n <= pref:
        return n
    t = (pref // align) * align
    while t >= align:
        if n % t == 0:
            return t
        t -= align
    return n


def _params(sem):
    return pltpu.CompilerParams(dimension_semantics=sem, vmem_limit_bytes=VMEM_LIMIT_BYTES)


def _dot(a, b, dims):
    return lax.dot_general(a, b, (dims, ((), ())), preferred_element_type=F32)


def _dot_nn(a, b):
    return _dot(a, b, ((1,), (0,)))


def _dot_nt(a, b):
    return _dot(a, b, ((1,), (1,)))


def _dot_tn(a, b):
    return _dot(a, b, ((0,), (0,)))


def _matmul(a, b, *, mode, name, tm=512, tn=1024, tk=2048, bias=None, add=None, out_dtype=F32,
            out_blocks=None, a_split=False, b_split=False, after=None):
    if a_split:
        assert mode == "nt"
        a_shape = (a.shape[1], 2 * a.shape[2])
    else:
        a_shape = a.shape
    if b_split:
        assert mode == "tn"
        b_shape = (b.shape[1], 2 * b.shape[2])
    else:
        b_shape = b.shape
    if mode == "nn":
        (m, k), (k2, n) = a_shape, b_shape
    elif mode == "nt":
        (m, k), (n, k2) = a_shape, b_shape
    else:
        (k, m), (k2, n) = a_shape, b_shape
    assert k == k2, (a.shape, b.shape, mode)
    tm, tn, tk = _tile(m, tm, SUBLANES), _tile(n, tn), _tile(k, tk)
    nk = k // tk
    a_spec = {"nn": pl.BlockSpec((tm, tk), lambda i, j, kk: (i, kk)),
              "nt": pl.BlockSpec((tm, tk), lambda i, j, kk: (i, kk)),
              "tn": pl.BlockSpec((tk, tm), lambda i, j, kk: (kk, i))}[mode]
    b_spec = {"nn": pl.BlockSpec((tk, tn), lambda i, j, kk: (kk, j)),
              "nt": pl.BlockSpec((tn, tk), lambda i, j, kk: (j, kk)),
              "tn": pl.BlockSpec((tk, tn), lambda i, j, kk: (kk, j))}[mode]
    if a_split:
        kb = a.shape[2] // tk
        assert a.shape[2] % tk == 0
        a_spec = pl.BlockSpec((None, tm, tk), lambda i, j, kk: (kk // kb, i, kk % kb))
    if b_split:
        nb = b.shape[2] // tn
        assert b.shape[2] % tn == 0
        b_spec = pl.BlockSpec((None, tk, tn), lambda i, j, kk: (j // nb, kk, j % nb))
    dot = {"nn": _dot_nn, "nt": _dot_nt, "tn": _dot_tn}[mode]
    in_specs, operands = [a_spec, b_spec], [a, b]
    if bias is not None:
        in_specs.append(pl.BlockSpec((1, tn), lambda i, j, kk: (0, j)))
        operands.append(bias)
    if add is not None:
        in_specs.append(pl.BlockSpec((tm, tn), lambda i, j, kk: (i, j)))
        operands.append(add)
    if after is not None:
        in_specs.append(pl.BlockSpec(memory_space=pl.ANY))
        operands.append(after)

    def body(*refs):
        a_ref, b_ref = refs[0], refs[1]
        rest = list(refs[2:])
        bias_ref = rest.pop(0) if bias is not None else None
        add_ref = rest.pop(0) if add is not None else None
        if after is not None:
            rest.pop(0)
        o_ref, acc_ref = rest

        def finish(acc):
            if bias_ref is not None:
                acc = acc + bias_ref[...]
            if add_ref is not None:
                acc = acc + add_ref[...]
            o_ref[...] = acc.astype(o_ref.dtype)

        part = dot(a_ref[...].astype(BF16), b_ref[...].astype(BF16))
        if nk == 1:
            finish(part)
        else:
            kk = pl.program_id(2)

            @pl.when(kk == 0)
            def _():
                acc_ref[...] = part

            @pl.when(jnp.logical_and(kk > 0, kk < nk - 1))
            def _():
                acc_ref[...] += part

            @pl.when(kk == nk - 1)
            def _():
                finish(acc_ref[...] + part)

    if out_blocks is None:
        out_shape = jax.ShapeDtypeStruct((m, n), out_dtype)
        out_spec = pl.BlockSpec((tm, tn), lambda i, j, kk: (i, j))
    else:
        shape, block, index_map = out_blocks(tm, tn)
        out_shape = jax.ShapeDtypeStruct(shape, out_dtype)
        out_spec = pl.BlockSpec(block, index_map)
    acc_shape = (tm, tn) if nk > 1 else (SUBLANES, LANES)
    return pl.pallas_call(
        body, name=name, grid=(m // tm, n // tn, nk), in_specs=in_specs, out_specs=out_spec, out_shape=out_shape,
        scratch_shapes=[pltpu.VMEM(acc_shape, F32)],
        compiler_params=_params(("parallel", "parallel", "arbitrary")),
    )(*operands)


def _wgrad_half(a, b, rows, cols, row_sharded, name, which, *, suffix="", add=None, wire=False, tm=None, tn=None,
                b_split=False):
    tokens = a.shape[0]
    if row_sharded:
        sr, sc = rows // N_CHIPS, cols // N_CORES
    else:
        sr, sc = rows // N_CORES, cols // N_CHIPS
    tm = _tile(sr, 512) if tm is None else tm
    tn = _tile(sc, 1024) if tn is None else tn
    assert sr % tm == 0 and sc % tn == 0, (rows, cols, tm, tn)
    rb, cb = sr // tm, sc // tn
    if tn >= tm:
        ij, grid = (lambda s, t: (t, s)), (N_CHIPS, cb, rb)
    else:
        ij, grid = (lambda s, t: (s, t)), (N_CHIPS, rb, cb)
    if row_sharded:
        a_tile = lambda p, i, j, h: p * rb + i
        b_tile = lambda p, i, j, h: h[0] * cb + j
    else:
        a_tile = lambda p, i, j, h: h[0] * rb + i
        b_tile = lambda p, i, j, h: p * cb + j
    a_spec = pl.BlockSpec((tokens, tm), lambda p, s, t, h: (0, a_tile(p, *ij(s, t), h)))
    if b_split:
        nbh = b.shape[2] // tn
        assert b.shape[2] % tn == 0
        b_spec = pl.BlockSpec((None, tokens, tn), lambda p, s, t, h: (b_tile(p, *ij(s, t), h) // nbh, 0,
                                                                       b_tile(p, *ij(s, t), h) % nbh))
    else:
        b_spec = pl.BlockSpec((tokens, tn), lambda p, s, t, h: (0, b_tile(p, *ij(s, t), h)))
    out_spec = pl.BlockSpec((None, tm, tn), lambda p, s, t, h: (p, *ij(s, t)))
    in_specs, operands = [a_spec, b_spec], [a, b]
    if add is not None:
        in_specs.append(out_spec)
        operands.append(add)

    def body(h_ref, a_ref, b_ref, *rest):
        acc = _dot_tn(a_ref[...].astype(BF16), b_ref[...].astype(BF16))
        if add is not None:
            acc = acc + rest[0][...]
        for o_ref in rest[1 if add is not None else 0:]:
            o_ref[...] = acc.astype(o_ref.dtype)

    out_dtypes = [F32, BF16] if wire else [F32]
    out = pl.pallas_call(
        body, name=name + suffix, out_shape=[jax.ShapeDtypeStruct((N_CHIPS, sr, sc), dt) for dt in out_dtypes],
        grid_spec=pltpu.PrefetchScalarGridSpec(num_scalar_prefetch=1, grid=grid, in_specs=in_specs,
                                               out_specs=[out_spec] * len(out_dtypes)),
        compiler_params=_params(("parallel", "parallel", "parallel")),
    )(which, *operands)
    return tuple(out) if wire else out[0]


def _rms_rows(x):
    return lax.rsqrt(jnp.mean(x * x, axis=-1, keepdims=True) + RMS_EPS)


def _rmsnorm_fwd(x, w, *, name, width=None, col=0, out_dtype=BF16, tr=256):
    rows = x.shape[0]
    width = x.shape[1] if width is None else width
    tr = _tile(rows, tr, SUBLANES)

    def body(x_ref, w_ref, o_ref):
        xv = x_ref[...]
        o_ref[...] = (xv * _rms_rows(xv) * w_ref[...]).astype(o_ref.dtype)

    return pl.pallas_call(
        body, name=name, grid=(rows // tr,),
        in_specs=[pl.BlockSpec((tr, width), lambda i: (i, col)), pl.BlockSpec((1, width), lambda i: (0, 0))],
        out_specs=pl.BlockSpec((tr, width), lambda i: (i, 0)),
        out_shape=jax.ShapeDtypeStruct((rows, width), out_dtype),
        compiler_params=_params(("parallel",)),
    )(x, w)


def _rmsnorm_bwd_rows(xv, w, dy):
    r = _rms_rows(xv)
    n = xv * r
    dn = dy * w
    dx = r * (dn - n * jnp.mean(dn * n, axis=-1, keepdims=True))
    return dx, dy * n


def _rmsnorm_bwd(x, w, dy, *, name, width=None, col=0, dy_col=0, add=None, tr=256, dx_dtypes=(F32,)):
    rows = x.shape[0]
    n_dx = len(dx_dtypes)
    width = x.shape[1] if width is None else width
    tr = _tile(rows, tr, SUBLANES)
    in_specs = [pl.BlockSpec((tr, width), lambda i: (i, col)), pl.BlockSpec((1, width), lambda i: (0, 0)),
                pl.BlockSpec((tr, width), lambda i: (i, dy_col))]
    operands = [x, w, dy]
    if add is not None:
        in_specs.append(pl.BlockSpec((tr, width), lambda i: (i, 0)))
        operands.append(add)

    def body(*refs):
        x_ref, w_ref, dy_ref = refs[:3]
        add_ref = refs[3] if add is not None else None
        dx_refs, dw_ref = refs[-1 - n_dx:-1], refs[-1]
        dx, dwp = _rmsnorm_bwd_rows(x_ref[...], w_ref[...], dy_ref[...])
        if add_ref is not None:
            dx = dx + add_ref[...]
        for dx_ref in dx_refs:
            dx_ref[...] = dx.astype(dx_ref.dtype)
        part = jnp.sum(dwp, axis=0, keepdims=True)

        @pl.when(pl.program_id(0) == 0)
        def _():
            dw_ref[...] = part

        @pl.when(pl.program_id(0) > 0)
        def _():
            dw_ref[...] += part

    return pl.pallas_call(
        body, name=name, grid=(rows // tr,), in_specs=in_specs,
        out_specs=[pl.BlockSpec((tr, width), lambda i: (i, 0))] * n_dx + [pl.BlockSpec((1, width), lambda i: (0, 0))],
        out_shape=[jax.ShapeDtypeStruct((rows, width), dt) for dt in dx_dtypes] + [jax.ShapeDtypeStruct((1, width), F32)],
        compiler_params=_params(("arbitrary",)),
    )(*operands)


def _final_norm_loss(h, w, target, *, tr=256):
    rows, d = h.shape
    tr = _tile(rows, tr, SUBLANES)

    def body(h_ref, w_ref, t_ref, loss_ref, dh_ref, dhb_ref, dw_ref):
        hv, wv = h_ref[...], w_ref[...]
        r = _rms_rows(hv)
        n = hv * r
        err = n * wv - t_ref[...]
        d_out = err * (1.0 / d)
        dn = d_out * wv
        dh = r * (dn - n * jnp.mean(dn * n, axis=-1, keepdims=True))
        dh_ref[...] = dh
        dhb_ref[...] = dh.astype(BF16)
        dw_part = jnp.sum(d_out * n, axis=0, keepdims=True)
        loss_part = jnp.full((SUBLANES, LANES), 0.5 / d, F32) * jnp.sum(err * err)

        @pl.when(pl.program_id(0) == 0)
        def _():
            dw_ref[...] = dw_part
            loss_ref[...] = loss_part

        @pl.when(pl.program_id(0) > 0)
        def _():
            dw_ref[...] += dw_part
            loss_ref[...] += loss_part

    return pl.pallas_call(
        body, name="final_norm_loss", grid=(rows // tr,),
        in_specs=[pl.BlockSpec((tr, d), lambda i: (i, 0)), pl.BlockSpec((1, d), lambda i: (0, 0)),
                  pl.BlockSpec((tr, d), lambda i: (i, 0))],
        out_specs=[pl.BlockSpec((SUBLANES, LANES), lambda i: (0, 0)), pl.BlockSpec((tr, d), lambda i: (i, 0)),
                   pl.BlockSpec((tr, d), lambda i: (i, 0)), pl.BlockSpec((1, d), lambda i: (0, 0))],
        out_shape=[jax.ShapeDtypeStruct((SUBLANES, LANES), F32), jax.ShapeDtypeStruct((rows, d), F32),
                   jax.ShapeDtypeStruct((rows, d), BF16), jax.ShapeDtypeStruct((1, d), F32)],
        compiler_params=_params(("arbitrary",)),
    )(h, w, target)


def _cmul(ar, ai, br, bi):
    return ar * br - ai * bi, ar * bi + ai * br


def _expand_matrix(groups, reps):
    row = lax.broadcasted_iota(jnp.int32, (groups, groups * reps), 0)
    colg = lax.broadcasted_iota(jnp.int32, (groups, groups * reps), 1) // reps
    return (row == colg).astype(F32)


def _dot_exact(a, b, dims):
    return lax.dot_general(a, b, (dims, ((), ())), preferred_element_type=F32, precision=lax.Precision.HIGHEST)


def _s5_discretize(lr, li, dt):
    mag = jnp.exp(lr * dt)
    th = li * dt
    ar, ai = mag * jnp.cos(th), mag * jnp.sin(th)
    nr, ni = ar - 1.0, ai
    den = lr * lr + li * li
    zr = (nr * lr + ni * li) / den
    zi = (ni * lr - nr * li) / den
    return mag, ar, ai, nr, ni, den, zr, zi


def _s5_params(lam_re, lam_im, log_dt, b_re, b_im):
    g, p = lam_re.shape
    ph = b_re.shape[1]

    def body(lr_ref, li_ref, ldt_ref, br_ref, bi_ref, ar_ref, ai_ref, bbr_ref, bbi_ref):
        dt = jnp.exp(ldt_ref[...])
        _, ar, ai, _, _, _, zr, zi = _s5_discretize(lr_ref[...], li_ref[...], dt)
        ar_ref[...] = ar
        ai_ref[...] = ai
        e = _expand_matrix(p, ph // p)
        zr_x = _dot_exact(zr, e, ((1,), (0,)))
        zi_x = _dot_exact(zi, e, ((1,), (0,)))
        bre, bim = br_ref[...], bi_ref[...]
        bbr_ref[...] = zr_x * bre - zi_x * bim
        bbi_ref[...] = zr_x * bim + zi_x * bre

    return pl.pallas_call(
        body, name="s5_params",
        out_shape=[jax.ShapeDtypeStruct((g, p), F32)] * 2 + [jax.ShapeDtypeStruct((g, ph), F32)] * 2,
    )(lam_re, lam_im, log_dt, b_re, b_im)


def _s5_params_bwd(lam_re, lam_im, log_dt, b_re, b_im, d_ar, d_ai, d_bbr, d_bbi):
    g, p = lam_re.shape
    ph = b_re.shape[1]

    def body(lr_ref, li_ref, ldt_ref, br_ref, bi_ref, dar_ref, dai_ref, dbr_ref, dbi_ref,
             dlr_ref, dli_ref, dldt_ref, dbre_ref, dbim_ref):
        lr, li = lr_ref[...], li_ref[...]
        dt = jnp.exp(ldt_ref[...])
        mag, ar, ai, nr, ni, den, zr, zi = _s5_discretize(lr, li, dt)
        e = _expand_matrix(p, ph // p)
        zr_x = _dot_exact(zr, e, ((1,), (0,)))
        zi_x = _dot_exact(zi, e, ((1,), (0,)))
        bre, bim, dbr, dbi = br_ref[...], bi_ref[...], dbr_ref[...], dbi_ref[...]
        dbre_ref[...] = zr_x * dbr + zi_x * dbi
        dbim_ref[...] = zr_x * dbi - zi_x * dbr
        dzr = _dot_exact(bre * dbr + bim * dbi, e, ((1,), (1,)))
        dzi = _dot_exact(bre * dbi - bim * dbr, e, ((1,), (1,)))
        inv = 1.0 / den
        d_nr = (dzr * lr - dzi * li) * inv
        d_ni = (dzr * li + dzi * lr) * inv
        d_den = -(dzr * zr + dzi * zi) * inv
        d_lr = (dzr * nr + dzi * ni) * inv + 2.0 * lr * d_den
        d_li = (dzr * ni - dzi * nr) * inv + 2.0 * li * d_den
        t_ar = dar_ref[...] + d_nr
        t_ai = dai_ref[...] + d_ni
        d_lrdt = t_ar * ar + t_ai * ai
        d_th = t_ai * ar - t_ar * ai
        dlr_ref[...] = d_lr + d_lrdt * dt
        dli_ref[...] = d_li + d_th * dt
        dldt_ref[...] = jnp.sum(d_lrdt * lr + d_th * li, axis=1, keepdims=True) * dt

    return pl.pallas_call(
        body, name="s5_params_bwd",
        out_shape=[jax.ShapeDtypeStruct((g, p), F32)] * 2 + [jax.ShapeDtypeStruct((g, 1), F32)]
        + [jax.ShapeDtypeStruct((g, ph), F32)] * 2,
    )(lam_re, lam_im, log_dt, b_re, b_im, d_ar, d_ai, d_bbr, d_bbi)


def _powers(ar, ai, count):
    out = [(ar, ai)]
    for _ in range(count - 1):
        out.append(_cmul(out[-1][0], out[-1][1], ar, ai))
    return out


def _scan_coefs(ar, ai, reverse):
    w = ar.shape[-1]
    pw = _powers(ar, ai, SUBLANES)
    row = lax.broadcasted_iota(jnp.int32, (SUBLANES, w), 0)
    steps = []
    d = 1
    while d < SUBLANES:
        keep = (row < SUBLANES - d) if reverse else (row >= d)
        pr, pi = pw[d - 1]
        steps.append((d, jnp.where(keep, pr, 0.0), jnp.where(keep, pi, 0.0)))
        d *= 2
    cr = jnp.zeros((SUBLANES, w), F32)
    ci = jnp.zeros((SUBLANES, w), F32)
    for t in range(SUBLANES):
        pr, pi = pw[SUBLANES - 1 - t] if reverse else pw[t]
        cr = jnp.where(row == t, pr, cr)
        ci = jnp.where(row == t, pi, ci)
    return steps, cr, ci


def _scan_tile(xr, xi, carry_r, carry_i, coefs, reverse):
    steps, cr, ci = coefs
    for d, mr, mi in steps:
        shift = SUBLANES - d if reverse else d
        sr, si = pltpu.roll(xr, shift, 0), pltpu.roll(xi, shift, 0)
        pr, pi = _cmul(mr, mi, sr, si)
        xr, xi = xr + pr, xi + pi
    pr, pi = _cmul(cr, ci, carry_r, carry_i)
    return xr + pr, xi + pi


def _gelu(x):
    c = math.sqrt(2.0 / math.pi)
    return 0.5 * x * (1.0 + jnp.tanh(c * (x + 0.044715 * x * x * x)))


def _gelu_grad(x):
    c = math.sqrt(2.0 / math.pi)
    t = jnp.tanh(c * (x + 0.044715 * x * x * x))
    return 0.5 * (1.0 + t) + 0.5 * x * (1.0 - t * t) * c * (1.0 + 3.0 * 0.044715 * x * x)


def _s5_fwd(proj, wb, wc, d_skip, abar):
    rows = proj.shape[0]
    nb = wb.shape[0]
    s2 = 2 * STATE_PER_BATCH
    st = STATE_PER_BATCH
    chunk = _tile(rows, 512, SUBLANES)

    def body(u_ref, wb_ref, wc_ref, d_ref, a_ref, s_ref, y_ref, yg_ref):
        for c0 in range(0, rows, chunk):
            s_ref[pl.ds(c0, chunk), :] = _dot_nn(u_ref[pl.ds(c0, chunk), :].astype(BF16), wb_ref[...])
        av = a_ref[...]
        coefs = _scan_coefs(av[:, :st], av[:, st:], reverse=False)

        def tile(b, carry):
            r0 = pl.multiple_of(b * SUBLANES, SUBLANES)
            xr, xi = _scan_tile(s_ref[pl.ds(r0, SUBLANES), :st], s_ref[pl.ds(r0, SUBLANES), st:], carry[0], carry[1],
                                coefs, False)
            s_ref[pl.ds(r0, SUBLANES), :st] = xr
            s_ref[pl.ds(r0, SUBLANES), st:] = xi
            return xr[SUBLANES - 1:, :], xi[SUBLANES - 1:, :]

        zero = jnp.zeros((1, st), F32)
        lax.fori_loop(0, rows // SUBLANES, tile, (zero, zero))
        for c0 in range(0, rows, chunk):
            y = _dot_nn(s_ref[pl.ds(c0, chunk), :].astype(BF16), wc_ref[...]) + d_ref[...] * u_ref[pl.ds(c0, chunk), :]
            y_ref[pl.ds(c0, chunk), :] = y
            yg_ref[pl.ds(c0, chunk), :] = _gelu(y).astype(BF16)

    return pl.pallas_call(
        body, name="s5_fwd", grid=(nb,),
        in_specs=[pl.BlockSpec((rows, LANES), lambda j: (0, j)), pl.BlockSpec((None, LANES, s2), lambda j: (j, 0, 0)),
                  pl.BlockSpec((None, s2, LANES), lambda j: (j, 0, 0)), pl.BlockSpec((1, LANES), lambda j: (0, j)),
                  pl.BlockSpec((None, 1, s2), lambda j: (j, 0, 0))],
        out_specs=[pl.BlockSpec((rows, s2), lambda j: (0, j)), pl.BlockSpec((rows, LANES), lambda j: (0, j)),
                   pl.BlockSpec((rows, LANES), lambda j: (0, j))],
        out_shape=[jax.ShapeDtypeStruct((rows, nb * s2), F32), jax.ShapeDtypeStruct((rows, nb * LANES), F32),
                   jax.ShapeDtypeStruct((rows, nb * LANES), BF16)],
        compiler_params=_params(("parallel",)),
    )(proj, wb, wc, d_skip, abar)


def _s5_bwd(proj, states, y_pre, dyg_a, dyg_b, wb, wc, d_skip, abar):
    rows = proj.shape[0]
    nb = wb.shape[0]
    s2 = 2 * STATE_PER_BATCH
    st = STATE_PER_BATCH
    chunk = _tile(rows, 512, SUBLANES)
    n_tiles = rows // SUBLANES

    def body(u_ref, s_ref, y_ref, ga_ref, gb_ref, wb_ref, wc_ref, d_ref, a_ref,
             du_ref, dwb_ref, dwc_ref, da_ref, dd_ref, ds_ref, dy_ref):
        dy_ref[...] = (ga_ref[...] + gb_ref[...]) * _gelu_grad(y_ref[...])
        dd_ref[...] = jnp.sum(dy_ref[...] * u_ref[...], axis=0, keepdims=True)
        for c0 in range(0, rows, chunk):
            ds_ref[pl.ds(c0, chunk), :] = _dot_nt(dy_ref[pl.ds(c0, chunk), :].astype(BF16), wc_ref[...])
        dwc_ref[...] = _dot_tn(s_ref[...].astype(BF16), dy_ref[...].astype(BF16))
        av = a_ref[...]
        coefs = _scan_coefs(av[:, :st], -av[:, st:], reverse=True)
        row = lax.broadcasted_iota(jnp.int32, (SUBLANES, st), 0)

        def tile(k, carry):
            cr, ci, acc_r, acc_i = carry
            b = n_tiles - 1 - k
            r0 = pl.multiple_of(b * SUBLANES, SUBLANES)
            rp = pl.multiple_of(jnp.maximum(b - 1, 0) * SUBLANES, SUBLANES)
            xr, xi = _scan_tile(ds_ref[pl.ds(r0, SUBLANES), :st], ds_ref[pl.ds(r0, SUBLANES), st:], cr, ci, coefs, True)
            ds_ref[pl.ds(r0, SUBLANES), :st] = xr
            ds_ref[pl.ds(r0, SUBLANES), st:] = xi
            first = jnp.where(b > 0, 1.0, 0.0)
            pr = jnp.where(row == 0, pltpu.roll(s_ref[pl.ds(rp, SUBLANES), :st], 1, 0) * first,
                           pltpu.roll(s_ref[pl.ds(r0, SUBLANES), :st], 1, 0))
            pi = jnp.where(row == 0, pltpu.roll(s_ref[pl.ds(rp, SUBLANES), st:], 1, 0) * first,
                           pltpu.roll(s_ref[pl.ds(r0, SUBLANES), st:], 1, 0))
            acc_r = acc_r + pr * xr + pi * xi
            acc_i = acc_i + pr * xi - pi * xr
            return xr[:1, :], xi[:1, :], acc_r, acc_i

        zero = jnp.zeros((1, st), F32)
        zacc = jnp.zeros((SUBLANES, st), F32)
        _, _, acc_r, acc_i = lax.fori_loop(0, n_tiles, tile, (zero, zero, zacc, zacc))
        da_ref[:, :st] = jnp.sum(acc_r, axis=0, keepdims=True)
        da_ref[:, st:] = jnp.sum(acc_i, axis=0, keepdims=True)
        for c0 in range(0, rows, chunk):
            du_ref[pl.ds(c0, chunk), :] = (_dot_nt(ds_ref[pl.ds(c0, chunk), :].astype(BF16), wb_ref[...])
                                           + d_ref[...] * dy_ref[pl.ds(c0, chunk), :]).astype(du_ref.dtype)
        dwb_ref[...] = _dot_tn(u_ref[...].astype(BF16), ds_ref[...].astype(BF16))

    col = pl.BlockSpec((rows, LANES), lambda j: (0, j))
    return pl.pallas_call(
        body, name="s5_bwd", grid=(nb,),
        in_specs=[col, pl.BlockSpec((rows, s2), lambda j: (0, j)), col, col, col,
                  pl.BlockSpec((None, LANES, s2), lambda j: (j, 0, 0)), pl.BlockSpec((None, s2, LANES), lambda j: (j, 0, 0)),
                  pl.BlockSpec((1, LANES), lambda j: (0, j)), pl.BlockSpec((None, 1, s2), lambda j: (j, 0, 0))],
        out_specs=[col, pl.BlockSpec((None, LANES, s2), lambda j: (j, 0, 0)),
                   pl.BlockSpec((None, s2, LANES), lambda j: (j, 0, 0)), pl.BlockSpec((None, 1, s2), lambda j: (j, 0, 0)),
                   pl.BlockSpec((1, LANES), lambda j: (0, j))],
        out_shape=[jax.ShapeDtypeStruct((rows, nb * LANES), BF16), jax.ShapeDtypeStruct((nb, LANES, s2), F32),
                   jax.ShapeDtypeStruct((nb, s2, LANES), F32), jax.ShapeDtypeStruct((nb, 1, s2), F32),
                   jax.ShapeDtypeStruct((1, nb * LANES), F32)],
        scratch_shapes=[pltpu.VMEM((rows, s2), F32), pltpu.VMEM((rows, LANES), F32)],
        compiler_params=_params(("parallel",)),
    )(proj, states, y_pre, dyg_a, dyg_b, wb, wc, d_skip, abar)


def _glu_norm_fwd(y_pre, z, w, *, tr=256):
    rows, width = y_pre.shape
    tr = _tile(rows, tr, SUBLANES)

    def body(y_ref, z_ref, w_ref, o_ref):
        v = _gelu(y_ref[...]) * jax.nn.sigmoid(z_ref[...])
        o_ref[...] = (v * _rms_rows(v) * w_ref[...]).astype(o_ref.dtype)

    blk = pl.BlockSpec((tr, width), lambda i: (i, 0))
    return pl.pallas_call(
        body, name="glu_norm_fwd", grid=(rows // tr,),
        in_specs=[blk, blk, pl.BlockSpec((1, width), lambda i: (0, 0))], out_specs=blk,
        out_shape=jax.ShapeDtypeStruct((rows, width), BF16), compiler_params=_params(("parallel",)),
    )(y_pre, z, w)


def _glu_norm_bwd(y_pre, z, w, dycat, *, tr=256):
    rows, width = y_pre.shape
    tr = _tile(rows, tr, SUBLANES)

    def body(y_ref, z_ref, w_ref, dy_ref, dz_ref, dg_ref, dw_ref, db_ref):
        yg = _gelu(y_ref[...])
        sg = jax.nn.sigmoid(z_ref[...])
        dv, dwp = _rmsnorm_bwd_rows(yg * sg, w_ref[...], dy_ref[...])
        dz = dv * yg * sg * (1.0 - sg)
        dz_ref[...] = dz.astype(dz_ref.dtype)
        dg_ref[...] = dv * sg
        dw_part = jnp.sum(dwp, axis=0, keepdims=True)
        db_part = jnp.sum(dz, axis=0, keepdims=True)

        @pl.when(pl.program_id(0) == 0)
        def _():
            dw_ref[...] = dw_part
            db_ref[...] = db_part

        @pl.when(pl.program_id(0) > 0)
        def _():
            dw_ref[...] += dw_part
            db_ref[...] += db_part

    blk = pl.BlockSpec((tr, width), lambda i: (i, 0))
    vec = pl.BlockSpec((1, width), lambda i: (0, 0))
    return pl.pallas_call(
        body, name="glu_norm_bwd", grid=(rows // tr,), in_specs=[blk, blk, vec, blk], out_specs=[blk, blk, vec, vec],
        out_shape=[jax.ShapeDtypeStruct((rows, width), BF16), jax.ShapeDtypeStruct((rows, width), F32)]
        + [jax.ShapeDtypeStruct((1, width), F32)] * 2,
        compiler_params=_params(("arbitrary",)),
    )(y_pre, z, w, dycat)


def _rope_tables(pos, freq, sign):
    rows = pos.shape[0]

    def body(p_ref, f_ref, s_ref, cos_ref, sin_ref):
        ang = p_ref[...] * f_ref[...]
        cos_ref[...] = jnp.cos(ang)
        sin_ref[...] = jnp.sin(ang) * s_ref[...]

    return pl.pallas_call(body, name="rope_tables", out_shape=[jax.ShapeDtypeStruct((rows, LANES), F32)] * 2)(pos, freq, sign)


def _rope(x, cos, sin_signed):
    half = QK_ROPE_DIM // 2
    src = lax.broadcasted_iota(jnp.int32, (LANES, LANES), 0)
    dst = lax.broadcasted_iota(jnp.int32, (LANES, LANES), 1)
    swap = jnp.where(jnp.logical_or(jnp.logical_and(dst < half, src == dst + half),
                                    jnp.logical_and(jnp.logical_and(dst >= half, dst < 2 * half), src == dst - half)),
                     1.0, 0.0).astype(F32)
    swapped = _dot_exact(x, swap, ((1,), (0,)))
    return x * cos + swapped * sin_signed


def _attn_prep(q, kv, proj, kpe_col, cos, sin, *, tr=256):
    rows = q.shape[0]
    heads = q.shape[1] // HEAD_SLOT
    tr = _tile(rows, tr, SUBLANES)

    def body(q_ref, kv_ref, kpe_ref, cos_ref, sin_ref, qc_ref, kc_ref, v_ref):
        c, s = cos_ref[...], sin_ref[...]
        kpe = _rope(kpe_ref[...], c, s).astype(BF16)
        for h in range(heads):
            nope = slice(h * HEAD_SLOT, h * HEAD_SLOT + LANES)
            pe = slice(h * HEAD_SLOT + LANES, (h + 1) * HEAD_SLOT)
            qc_ref[:, nope] = q_ref[:, nope].astype(BF16)
            qc_ref[:, pe] = _rope(q_ref[:, pe], c, s).astype(BF16)
            kc_ref[:, nope] = kv_ref[:, nope].astype(BF16)
            kc_ref[:, pe] = kpe
            v_ref[:, h * LANES:(h + 1) * LANES] = kv_ref[:, pe].astype(BF16)

    slots = pl.BlockSpec((tr, heads * HEAD_SLOT), lambda i: (i, 0))
    tab = pl.BlockSpec((tr, LANES), lambda i: (i, 0))
    return pl.pallas_call(
        body, name="attn_prep", grid=(rows // tr,),
        in_specs=[slots, slots, pl.BlockSpec((tr, LANES), lambda i: (i, kpe_col)), tab, tab],
        out_specs=[slots, slots, pl.BlockSpec((tr, heads * LANES), lambda i: (i, 0))],
        out_shape=[jax.ShapeDtypeStruct((rows, heads * HEAD_SLOT), BF16)] * 2
        + [jax.ShapeDtypeStruct((rows, heads * LANES), BF16)],
        compiler_params=_params(("parallel",)),
    )(q, kv, proj, cos, sin)


def _causal(tq, tk):
    return lax.broadcasted_iota(jnp.int32, (tq, tk), 1) <= lax.broadcasted_iota(jnp.int32, (tq, tk), 0)


def _attn_fwd(qc, kc, vb, *, scale, tq=512):
    rows = qc.shape[0]
    heads = qc.shape[1] // HEAD_SLOT
    tq = _tile(rows, tq, SUBLANES)
    tk = tq

    def body(q_ref, k_ref, v_ref, o_ref, lse_ref):
        i = pl.program_id(1)
        q = q_ref[...]

        def step(j, carry, diagonal):
            m, l, acc = carry
            k0 = pl.multiple_of(j * tk, tk)
            s = _dot_nt(q, k_ref[pl.ds(k0, tk), :]) * scale
            if diagonal:
                s = jnp.where(_causal(tq, tk), s, NEG_INF)
            m_new = jnp.maximum(m, jnp.max(s, axis=-1, keepdims=True))
            p = jnp.exp(s - m_new)
            alpha = jnp.exp(m - m_new)
            l = alpha * l + jnp.sum(p, axis=-1, keepdims=True)
            acc = alpha * acc + _dot_nn(p.astype(BF16), v_ref[pl.ds(k0, tk), :])
            return m_new, l, acc

        init = (jnp.full((tq, 1), NEG_INF, F32), jnp.zeros((tq, 1), F32), jnp.zeros((tq, LANES), F32))
        below = lax.fori_loop(0, i, lambda j, carry: step(j, carry, False), init)
        m, l, acc = step(i, below, True)
        o_ref[...] = acc / l
        lse_ref[...] = jnp.broadcast_to(m + jnp.log(l), (tq, LANES))

    return pl.pallas_call(
        body, name="attn_fwd", grid=(heads, rows // tq),
        in_specs=[pl.BlockSpec((tq, HEAD_SLOT), lambda h, i: (i, h)), pl.BlockSpec((rows, HEAD_SLOT), lambda h, i: (0, h)),
                  pl.BlockSpec((rows, LANES), lambda h, i: (0, h))],
        out_specs=[pl.BlockSpec((tq, LANES), lambda h, i: (i, h))] * 2,
        out_shape=[jax.ShapeDtypeStruct((rows, heads * LANES), F32)] * 2,
        compiler_params=_params(("parallel", "parallel")),
    )(qc, kc, vb)


def _attn_bwd(qc, kc, vb, o, do, lse, cos, sin, *, scale, tk=512):
    rows = qc.shape[0]
    heads = qc.shape[1] // HEAD_SLOT
    tk = _tile(rows, tk, SUBLANES)
    tq = tk
    nq = rows // tq

    def body(q_ref, k_ref, v_ref, o_ref, do_ref, lse_ref, cos_ref, sin_ref, dq_ref, dkv_ref, dkpe_ref, dq_acc, delta_ref):
        j = pl.program_id(1)

        @pl.when(j == 0)
        def _():
            dq_acc[...] = jnp.zeros_like(dq_acc)
            for r0 in range(0, rows, tq):
                d = jnp.sum(do_ref[pl.ds(r0, tq), :] * o_ref[pl.ds(r0, tq), :], axis=-1, keepdims=True)
                delta_ref[pl.ds(r0, tq), :] = jnp.broadcast_to(d, (tq, LANES))

        kb, vv = k_ref[...], v_ref[...]

        def step(i, carry, diagonal):
            dk, dv = carry
            q0 = pl.multiple_of(i * tq, tq)
            qb = q_ref[pl.ds(q0, tq), :]
            dob = do_ref[pl.ds(q0, tq), :].astype(BF16)
            s = _dot_nt(qb, kb) * scale
            p = jnp.exp(s - lse_ref[pl.ds(q0, tq), :1])
            if diagonal:
                p = jnp.where(_causal(tq, tk), p, 0.0)
            dv = dv + _dot_tn(p.astype(BF16), dob)
            ds = (p * (_dot_nt(dob, vv) - delta_ref[pl.ds(q0, tq), :1])).astype(BF16)
            dk = dk + _dot_tn(ds, qb)
            dq_acc[pl.ds(q0, tq), :] += _dot_nn(ds, kb)
            return dk, dv

        zero = (jnp.zeros((tk, HEAD_SLOT), F32), jnp.zeros((tk, LANES), F32))
        dk, dv = lax.fori_loop(j + 1, nq, lambda i, carry: step(i, carry, False), step(j, zero, True))
        dkv_ref[:, :LANES] = (dk[:, :LANES] * scale).astype(dkv_ref.dtype)
        dkv_ref[:, LANES:] = dv.astype(dkv_ref.dtype)
        dkpe_ref[...] = dk[:, LANES:] * scale

        @pl.when(j == nq - 1)
        def _():
            for r0 in range(0, rows, tq):
                dq = dq_acc[pl.ds(r0, tq), :] * scale
                dq_ref[pl.ds(r0, tq), :LANES] = dq[:, :LANES].astype(dq_ref.dtype)
                dq_ref[pl.ds(r0, tq), LANES:] = _rope(dq[:, LANES:], cos_ref[pl.ds(r0, tq), :],
                                                      -sin_ref[pl.ds(r0, tq), :]).astype(dq_ref.dtype)

    full_q = pl.BlockSpec((rows, HEAD_SLOT), lambda h, j: (0, h))
    full_v = pl.BlockSpec((rows, LANES), lambda h, j: (0, h))
    tab = pl.BlockSpec((rows, LANES), lambda h, j: (0, 0))
    return pl.pallas_call(
        body, name="attn_bwd", grid=(heads, rows // tk),
        in_specs=[full_q, pl.BlockSpec((tk, HEAD_SLOT), lambda h, j: (j, h)), pl.BlockSpec((tk, LANES), lambda h, j: (j, h)),
                  full_v, full_v, full_v, tab, tab],
        out_specs=[full_q, pl.BlockSpec((tk, HEAD_SLOT), lambda h, j: (j, h)), pl.BlockSpec((tk, LANES), lambda h, j: (j, h))],
        out_shape=[jax.ShapeDtypeStruct((rows, heads * HEAD_SLOT), BF16), jax.ShapeDtypeStruct((rows, heads * HEAD_SLOT), BF16),
                   jax.ShapeDtypeStruct((rows, heads * LANES), F32)],
        scratch_shapes=[pltpu.VMEM((rows, HEAD_SLOT), F32), pltpu.VMEM((rows, LANES), F32)],
        compiler_params=_params(("parallel", "arbitrary")),
    )(qc, kc, vb, o, do, lse, cos, sin)


def _kpe_bwd(dkpe_heads, cos, sin, *, tr=512):
    rows = dkpe_heads.shape[0]
    heads = dkpe_heads.shape[1] // LANES
    tr = _tile(rows, tr, 2 * SUBLANES)

    def body(d_ref, cos_ref, sin_ref, o_ref):
        acc = d_ref[:, :LANES]
        for h in range(1, heads):
            acc = acc + d_ref[:, h * LANES:(h + 1) * LANES]
        o_ref[...] = _rope(acc, cos_ref[...], -sin_ref[...]).astype(o_ref.dtype)

    tab = pl.BlockSpec((tr, LANES), lambda i: (i, 0))
    return pl.pallas_call(
        body, name="kpe_bwd", grid=(rows // tr,),
        in_specs=[pl.BlockSpec((tr, heads * LANES), lambda i: (i, 0)), tab, tab], out_specs=tab,
        out_shape=jax.ShapeDtypeStruct((rows, LANES), BF16), compiler_params=_params(("parallel",)),
    )(dkpe_heads, cos, sin)


CONV_ROWS = 128


def _with_halo(ref, r0, ci, n_chunks, ch, lanes, before, after):
    parts = []
    if before:
        lo = pl.multiple_of(jnp.maximum(r0 - SUBLANES, 0), SUBLANES)
        parts.append(ref[pl.ds(lo, SUBLANES), lanes] * jnp.where(ci > 0, 1.0, 0.0))
    parts.append(ref[pl.ds(r0, ch), lanes])
    if after:
        hi = pl.multiple_of(jnp.minimum(r0 + ch, n_chunks * ch - SUBLANES), SUBLANES)
        parts.append(ref[pl.ds(hi, SUBLANES), lanes] * jnp.where(ci < n_chunks - 1, 1.0, 0.0))
    return jnp.concatenate(parts, axis=0)


def _taps(ext):
    return pltpu.roll(ext, 2, 0)[SUBLANES:], pltpu.roll(ext, 1, 0)[SUBLANES:], ext[SUBLANES:]


def _conv3(taps, w, b):
    return w[0:1, :] * taps[0] + w[1:2, :] * taps[1] + w[2:3, :] * taps[2] + b


def _conv_gate_fwd(a, conv_w, conv_b, *, tc=256):
    rows, f2 = a.shape
    f = f2 // 2
    tc = _tile(f, tc)
    nc = f // tc
    ch = _tile(rows, CONV_ROWS, SUBLANES)
    n_chunks = rows // ch

    def body(ag_ref, av_ref, wg_ref, wv_ref, bg_ref, bv_ref, o_ref):
        for lt in range(tc // LANES):
            lanes = slice(lt * LANES, (lt + 1) * LANES)
            wg, wv, bg, bv = wg_ref[:, lanes], wv_ref[:, lanes], bg_ref[:, lanes], bv_ref[:, lanes]

            def chunk(ci, carry):
                r0 = pl.multiple_of(ci * ch, ch)
                gate = _conv3(_taps(_with_halo(ag_ref, r0, ci, n_chunks, ch, lanes, True, False)), wg, bg)
                val = _conv3(_taps(_with_halo(av_ref, r0, ci, n_chunks, ch, lanes, True, False)), wv, bv)
                o_ref[pl.ds(r0, ch), lanes] = (gate * jax.nn.sigmoid(gate) * val).astype(o_ref.dtype)
                return carry

            lax.fori_loop(0, n_chunks, chunk, 0)

    return pl.pallas_call(
        body, name="conv_gate_fwd", grid=(nc,),
        in_specs=[pl.BlockSpec((rows, tc), lambda j: (0, j)), pl.BlockSpec((rows, tc), lambda j: (0, j + nc)),
                  pl.BlockSpec((SUBLANES, tc), lambda j: (0, j)), pl.BlockSpec((SUBLANES, tc), lambda j: (0, j + nc)),
                  pl.BlockSpec((1, tc), lambda j: (0, j)), pl.BlockSpec((1, tc), lambda j: (0, j + nc))],
        out_specs=pl.BlockSpec((rows, tc), lambda j: (0, j)),
        out_shape=jax.ShapeDtypeStruct((rows, f), BF16), compiler_params=_params(("parallel",)),
    )(a, a, conv_w, conv_w, conv_b, conv_b)


def _conv_gate_bwd(a, conv_w, conv_b, dg, *, tc=256):
    rows, f2 = a.shape
    f = f2 // 2
    tc = _tile(f, tc)
    nc = f // tc
    ch = _tile(rows, CONV_ROWS, SUBLANES)
    n_chunks = rows // ch
    ext_rows = ch + SUBLANES

    def fold(x):
        return jnp.sum(x.reshape(ch // SUBLANES, SUBLANES, LANES), axis=0)

    def body(ag_ref, av_ref, wg_ref, wv_ref, bg_ref, bv_ref, dg_ref, da_ref, dw_ref, db_ref):
        for lt in range(tc // LANES):
            lanes = slice(lt * LANES, (lt + 1) * LANES)
            wg, wv, bg, bv = wg_ref[:, lanes], wv_ref[:, lanes], bg_ref[:, lanes], bv_ref[:, lanes]

            def chunk(ci, acc):
                r0 = pl.multiple_of(ci * ch, ch)
                taps_g = _taps(_with_halo(ag_ref, r0, ci, n_chunks, ch, lanes, True, True))
                taps_v = _taps(_with_halo(av_ref, r0, ci, n_chunks, ch, lanes, True, True))
                dge = _with_halo(dg_ref, r0, ci, n_chunks, ch, lanes, False, True)
                gate, val = _conv3(taps_g, wg, bg), _conv3(taps_v, wv, bv)
                sg = jax.nn.sigmoid(gate)
                d_gate = dge * val * sg * (1.0 + gate * (1.0 - sg))
                d_val = dge * gate * sg
                new = []
                for half, (taps, w, d) in enumerate(((taps_g, wg, d_gate), (taps_v, wv, d_val))):
                    da = (w[2:3, :] * d[:ch] + w[1:2, :] * pltpu.roll(d, ext_rows - 1, 0)[:ch]
                          + w[0:1, :] * pltpu.roll(d, ext_rows - 2, 0)[:ch])
                    da_ref[half, pl.ds(r0, ch), lanes] = da.astype(da_ref.dtype)
                    dc = d[:ch]
                    sums = [fold(dc)] + [fold(dc * t[:ch]) for t in taps]
                    new.append(tuple(x + s for x, s in zip(acc[half], sums)))
                return tuple(new)

            zero = tuple(jnp.zeros((SUBLANES, LANES), F32) for _ in range(4))
            acc = lax.fori_loop(0, n_chunks, chunk, (zero, zero))
            row = lax.broadcasted_iota(jnp.int32, (SUBLANES, LANES), 0)
            for half in range(2):
                db, *taps = (jnp.sum(x, axis=0, keepdims=True) for x in acc[half])
                db_ref[half, :, lanes] = db
                dw = jnp.zeros((SUBLANES, LANES), F32)
                for tap in range(3):
                    dw = jnp.where(row == tap, taps[tap], dw)
                dw_ref[half, :, lanes] = dw

    lo = lambda j: (0, j)
    hi = lambda j: (0, j + nc)
    both = lambda j: (0, 0, j)
    return pl.pallas_call(
        body, name="conv_gate_bwd", grid=(nc,),
        in_specs=[pl.BlockSpec((rows, tc), lo), pl.BlockSpec((rows, tc), hi), pl.BlockSpec((SUBLANES, tc), lo),
                  pl.BlockSpec((SUBLANES, tc), hi), pl.BlockSpec((1, tc), lo), pl.BlockSpec((1, tc), hi),
                  pl.BlockSpec((rows, tc), lo)],
        out_specs=[pl.BlockSpec((2, rows, tc), both), pl.BlockSpec((2, SUBLANES, tc), both), pl.BlockSpec((2, 1, tc), both)],
        out_shape=[jax.ShapeDtypeStruct((2, rows, f), BF16), jax.ShapeDtypeStruct((2, SUBLANES, f), F32),
                   jax.ShapeDtypeStruct((2, 1, f), F32)],
        compiler_params=_params(("parallel",)),
    )(a, a, conv_w, conv_w, conv_b, conv_b, dg)


def _wgrad(a, b, rows, cols, row_sharded, name, **kw):
    return functools.partial(_wgrad_half, a, b, rows, cols, row_sharded, name, **kw)


def _block_diag(x):
    nb, g, r, c = x.shape
    eye = jnp.eye(g, dtype=x.dtype)
    return (x[:, :, :, None, :] * eye[None, :, None, :, None]).reshape(nb, g * r, g * c)


def _block_diag_part(x, r, c):
    nb = x.shape[0]
    g = GROUPS_PER_BATCH
    eye = jnp.eye(g, dtype=x.dtype)
    return jnp.sum(x.reshape(nb, g, r, g, c) * eye[None, :, None, :, None], axis=3)


class _NoExchange:
    def __init__(self, later, ffn):
        self.later, self.ffn = later, ffn

    def mixer_weights(self, after):
        return self.later

    def ffn_weights_arrived(self, after):
        return None

    def ffn_weights(self, after):
        return self.ffn

    def ffn_down_weight(self, after):
        return self.ffn["ffn_w_down"]

    def ffn_grads(self, makers, after):
        self.ffn_makers = makers
        return None

    def ffn_backward_done(self, after):
        return None


def _local_step(x, posf, target, w, hooks):
    rows, d = x.shape
    width = w["ssm_d"].shape[1]
    qr, kvr = w["mla_q_norm_w"].shape[1], w["mla_kv_norm_w"].shape[1]
    heads = w["mla_w_ukv"].shape[1] // HEAD_SLOT
    f2 = w["ffn_conv_b"].shape[1]
    inp = w["w_in"].shape[0]
    groups = width // SSM_GROUP
    nb = groups // GROUPS_PER_BATCH
    scale = (QK_NOPE_DIM + QK_ROPE_DIM) ** -0.5
    g = {}

    hn = _rmsnorm_fwd(x, w["attn_norm_w"], name="attn_norm")
    proj = _matmul(hn, w["w_in"], mode="nt", name="in_proj")

    ar, ai, bbr, bbi = _s5_params(w["ssm_lambda_re"], w["ssm_lambda_im"], w["ssm_log_dt"], w["ssm_b_re"], w["ssm_b_im"])

    def b_band(bb):
        return _block_diag(bb.reshape(nb, GROUPS_PER_BATCH, SSM_STATE, SSM_GROUP).transpose(0, 1, 3, 2))

    def c_band(c):
        return _block_diag(c.reshape(nb, GROUPS_PER_BATCH, SSM_GROUP, SSM_STATE).transpose(0, 1, 3, 2))

    wb = jnp.concatenate([b_band(bbr), b_band(bbi)], axis=2).astype(BF16)
    wc = jnp.concatenate([c_band(w["ssm_c_re"]), -c_band(w["ssm_c_im"])], axis=1).astype(BF16)
    abar = jnp.concatenate([ar.reshape(nb, 1, STATE_PER_BATCH), ai.reshape(nb, 1, STATE_PER_BATCH)], axis=2)
    states, y_pre, yg = _s5_fwd(proj, wb, wc, w["ssm_d"], abar)
    later = hooks.mixer_weights(yg)
    z = _matmul(yg, later["ssm_w_glu"], mode="nn", name="glu_proj", bias=w["ssm_b_glu"])
    ys = _glu_norm_fwd(y_pre, z, w["ssm_out_norm_w"])

    q_col, kv_col, kpe_col = width // qr, (width + qr) // kvr, (width + qr + kvr) // LANES
    assert width % qr == 0 and (width + qr) % kvr == 0
    qn = _rmsnorm_fwd(proj, w["mla_q_norm_w"], name="q_norm", width=qr, col=q_col)
    kvn = _rmsnorm_fwd(proj, w["mla_kv_norm_w"], name="kv_norm", width=kvr, col=kv_col)
    q = _matmul(qn, w["mla_w_uq"], mode="nn", name="q_proj")
    kv = _matmul(kvn, w["mla_w_ukv"], mode="nn", name="kv_proj")
    half = QK_ROPE_DIM // 2
    inv_freq = ROPE_THETA ** (-jnp.arange(0, QK_ROPE_DIM, 2, dtype=F32) / QK_ROPE_DIM)
    zeros = jnp.zeros((LANES - QK_ROPE_DIM,), F32)
    freq = jnp.concatenate([inv_freq, inv_freq, zeros]).reshape(1, LANES)
    sign = jnp.concatenate([-jnp.ones((half,), F32), jnp.ones((half,), F32), zeros]).reshape(1, LANES)
    cos, sin = _rope_tables(posf, freq, sign)
    qc, kc, vb = _attn_prep(q, kv, proj, kpe_col, cos, sin)
    o, lse = _attn_fwd(qc, kc, vb, scale=scale, tq=ATTN_BLOCK)
    ym = _rmsnorm_fwd(o, w["mla_out_norm_w"], name="mla_out_norm")
    ycat = jnp.concatenate([ys, ym], axis=1)
    h1 = _matmul(ycat, later["w_out"], mode="nn", name="out_proj", add=x, after=hooks.ffn_weights_arrived(ycat))

    hn2 = _rmsnorm_fwd(h1, w["ffn_norm_w"], name="ffn_norm")
    ffn = hooks.ffn_weights(hn2)
    a = _matmul(hn2, ffn["ffn_w_up"], mode="nn", name="ffn_up", tm=FFN_ROWS, after=ffn.get("started"))
    gated = _conv_gate_fwd(a, ffn["ffn_conv_w"], w["ffn_conv_b"])
    w_down = hooks.ffn_down_weight(gated)
    h2 = _matmul(gated, w_down, mode="nn", name="ffn_down", add=h1, tk=2816, tm=FFN_ROWS)
    loss_tile, dh2, dh2_mxu, g["final_norm_w"] = _final_norm_loss(h2, w["final_norm_w"], target)

    dgated = _matmul(dh2_mxu, w_down, mode="nt", name="ffn_down_dx", tm=FFN_ROWS)
    da, dcw, dcb = _conv_gate_bwd(a, ffn["ffn_conv_w"], w["ffn_conv_b"], dgated)
    g["ffn_conv_w"] = jnp.concatenate([dcw[0, :3], dcw[1, :3]], axis=1)
    g["ffn_conv_b"] = jnp.concatenate([dcb[0], dcb[1]], axis=1)
    started = hooks.ffn_grads({
        "ffn_w_up": _wgrad(hn2, da, d, f2, False, "ffn_up_dw", b_split=True, tn=_tile(f2 // N_CHIPS, 1408)),
        "ffn_w_down": _wgrad(gated, dh2_mxu, f2 // 2, d, True, "ffn_down_dw", tm=f2 // 2 // N_CHIPS, tn=512)}, dcb)
    dhn2 = _matmul(da, ffn["ffn_w_up"], mode="nt", name="ffn_up_dx", a_split=True, tk=_tile(f2 // 2, 2816), tm=FFN_ROWS,
                   after=started)
    dh1, dh1_mxu, g["ffn_norm_w"] = _rmsnorm_bwd(h1, w["ffn_norm_w"], dhn2, name="ffn_norm_bwd", add=dh2,
                                                dx_dtypes=(F32, BF16))

    dycat = _matmul(dh1_mxu, later["w_out"], mode="nt", name="out_proj_dx")
    g["w_out"] = _wgrad(ycat, dh1_mxu, 2 * width, d, True, "out_proj_dw")
    started = hooks.ffn_backward_done(dycat)
    mla_out_norm_w, ssm_out_norm_w = w["mla_out_norm_w"], w["ssm_out_norm_w"]
    if started is not None:
        mla_out_norm_w, ssm_out_norm_w = mla_out_norm_w + started[:1, :1], ssm_out_norm_w + started[:1, :1]

    do, g["mla_out_norm_w"] = _rmsnorm_bwd(o, mla_out_norm_w, dycat, name="mla_out_norm_bwd", width=width, dy_col=1)
    dq, dkv, dkpe_heads = _attn_bwd(qc, kc, vb, o, do, lse, cos, sin, scale=scale, tk=ATTN_BLOCK)
    dkpe = _kpe_bwd(dkpe_heads, cos, sin)
    g["mla_w_uq"] = _wgrad(qn, dq, qr, heads * HEAD_SLOT, False, "q_proj_dw")
    dqn = _matmul(dq, w["mla_w_uq"], mode="nt", name="q_proj_dx")
    dcq, g["mla_q_norm_w"] = _rmsnorm_bwd(proj, w["mla_q_norm_w"], dqn, name="q_norm_bwd", width=qr, col=q_col,
                                          dx_dtypes=(BF16,))
    g["mla_w_ukv"] = _wgrad(kvn, dkv, kvr, heads * HEAD_SLOT, False, "kv_proj_dw")
    dkvn = _matmul(dkv, w["mla_w_ukv"], mode="nt", name="kv_proj_dx")
    dckv, g["mla_kv_norm_w"] = _rmsnorm_bwd(proj, w["mla_kv_norm_w"], dkvn, name="kv_norm_bwd", width=kvr, col=kv_col,
                                            dx_dtypes=(BF16,))

    dz, dyg_a, g["ssm_out_norm_w"], g["ssm_b_glu"] = _glu_norm_bwd(y_pre, z, ssm_out_norm_w, dycat)
    dyg_b = _matmul(dz, later["ssm_w_glu"], mode="nt", name="glu_proj_dx")
    g["ssm_w_glu"] = _wgrad(yg, dz, width, width, True, "glu_proj_dw")
    du, dwb, dwc, dabar, g["ssm_d"] = _s5_bwd(proj, states, y_pre, dyg_a, dyg_b, wb, wc, w["ssm_d"], abar)

    def b_unband(x):
        return _block_diag_part(x, SSM_GROUP, SSM_STATE).transpose(0, 1, 3, 2).reshape(groups, SSM_STATE * SSM_GROUP)

    def c_unband(x):
        return _block_diag_part(x, SSM_STATE, SSM_GROUP).transpose(0, 1, 3, 2).reshape(groups, SSM_GROUP, SSM_STATE)

    st = STATE_PER_BATCH
    g["ssm_c_re"] = c_unband(dwc[:, :st, :])
    g["ssm_c_im"] = -c_unband(dwc[:, st:, :])
    d_ar = dabar[:, 0, :st].reshape(groups, SSM_STATE)
    d_ai = dabar[:, 0, st:].reshape(groups, SSM_STATE)
    (g["ssm_lambda_re"], g["ssm_lambda_im"], g["ssm_log_dt"], g["ssm_b_re"], g["ssm_b_im"]) = _s5_params_bwd(
        w["ssm_lambda_re"], w["ssm_lambda_im"], w["ssm_log_dt"], w["ssm_b_re"], w["ssm_b_im"], d_ar, d_ai,
        b_unband(dwb[:, :, :st]), b_unband(dwb[:, :, st:]))

    pad = jnp.zeros((rows, inp - (width + qr + kvr + LANES)), BF16)
    dproj = jnp.concatenate([du, dcq, dckv, dkpe, pad], axis=1)
    g["w_in"] = _wgrad(dproj, hn, inp, d, False, "in_proj_dw")
    dhn = _matmul(dproj, w["w_in"], mode="nn", name="in_proj_dx")
    dx, g["attn_norm_w"] = _rmsnorm_bwd(x, w["attn_norm_w"], dhn, name="attn_norm_bwd", add=dh1)
    return loss_tile, dx, g


ANY = pl.BlockSpec(memory_space=pl.ANY)
MESH = pl.DeviceIdType.MESH


def _mesh_pos():
    return lax.axis_index("x"), lax.axis_index("y"), lax.axis_index("c")


def _other_chips(x, y):
    return [(1 - x, y), (x, 1 - y), (1 - x, 1 - y)]


def _remote(src, dst, send_sems, recv_sems, k, to):
    return pltpu.make_async_remote_copy(src_ref=src, dst_ref=dst, send_sem=send_sems.at[k], recv_sem=recv_sems.at[k],
                                        device_id=to, device_id_type=MESH)


def _place_shard(shard, piece_idx, row_sharded, name, out_dtype=BF16, pieces=N_CHIPS, after=None):
    rs, cs = shard.shape
    tr = _tile(rs, 256, 2 * SUBLANES)
    rb = rs // tr
    extra = [] if after is None else [after]

    def body(p_ref, x_ref, *rest):
        o_ref = rest[-1]
        o_ref[...] = x_ref[...].astype(o_ref.dtype)

    if row_sharded:
        out_shape, out_map = (pieces * rs, cs), (lambda i, p_ref: (p_ref[0] * rb + i, 0))
    else:
        out_shape, out_map = (rs, pieces * cs), (lambda i, p_ref: (i, p_ref[0]))
    return pl.pallas_call(
        body, name=name, out_shape=jax.ShapeDtypeStruct(out_shape, out_dtype),
        grid_spec=pltpu.PrefetchScalarGridSpec(
            num_scalar_prefetch=1, grid=(rb,),
            in_specs=[pl.BlockSpec((tr, cs), lambda i, p_ref: (i, 0))] + [pl.BlockSpec(memory_space=pl.ANY)] * len(extra),
            out_specs=pl.BlockSpec((tr, cs), out_map)),
        compiler_params=_params(("parallel",)),
    )(piece_idx, shard, *extra)


def _gather_weights(placed, name):
    n = len(placed)
    meta = [(row_sharded, direct) for _, row_sharded, direct in placed]
    over_ici, over_d2d = _gather_plans(meta)
    forwarded = [t for t, (_, direct) in enumerate(meta) if not direct]

    def body(*refs):
        outs = refs[n:2 * n]
        send_sems, recv_sems, pass_send_sems, pass_recv_sems = refs[2 * n:]
        first, arrivals = over_ici(outs, send_sems, recv_sems)
        passed, passed_arrivals = over_d2d([outs[t] for t in forwarded], pass_send_sems, pass_recv_sems)
        for cp in first:
            cp.start()
        for t in range(n):
            for j in range(3):
                arrivals[3 * t + j].wait_recv()
                if t in forwarded:
                    passed[3 * forwarded.index(t) + j].start()
        for cp in passed_arrivals:
            cp.wait_recv()
        for cp in first + passed:
            cp.wait_send()

    return pl.pallas_call(
        body, name=name, in_specs=[ANY] * n, out_specs=[ANY] * n,
        out_shape=[jax.ShapeDtypeStruct(arr.shape, arr.dtype) for arr, _, _ in placed],
        input_output_aliases={t: t for t in range(n)},
        scratch_shapes=[pltpu.SemaphoreType.DMA((3 * n,)), pltpu.SemaphoreType.DMA((3 * n,)),
                        pltpu.SemaphoreType.DMA((3 * len(forwarded),)), pltpu.SemaphoreType.DMA((3 * len(forwarded),))],
    )(*[arr for arr, _, _ in placed])


def _gather_plans(meta):
    def window(ref, row_sharded, piece, half):
        r, cc = ref.shape
        if row_sharded:
            rs = r // N_CHIPS
            if half is None:
                return ref.at[pl.ds(piece * rs, rs), :]
            return ref.at[pl.ds(piece * rs + half * (rs // 2), rs // 2), :]
        cs = cc // N_CHIPS
        if half is None:
            return ref.at[:, pl.ds(piece * cs, cs)]
        return ref.at[pl.ds(half * (r // 2), r // 2), pl.ds(piece * cs, cs)]

    def over_ici(refs, send_sems, recv_sems):
        x, y, c = _mesh_pos()
        sends, recvs = [], []
        for t, (row_sharded, direct) in enumerate(meta):
            mine = window(refs[t], row_sharded, 2 * x + y, None if direct else c)
            for j, (px, py) in enumerate(_other_chips(x, y)):
                theirs = window(refs[t], row_sharded, 2 * px + py, None if direct else c)
                sends.append(_remote(mine, mine, send_sems, recv_sems, 3 * t + j, (px, py, c)))
                recvs.append(_remote(theirs, theirs, send_sems, recv_sems, 3 * t + j, (px, py, c)))
        return sends, recvs

    def over_d2d(refs, send_sems, recv_sems):
        x, y, c = _mesh_pos()
        sends, recvs = [], []
        rows = [row_sharded for row_sharded, direct in meta if not direct]
        for t, row_sharded in enumerate(rows):
            for j, (px, py) in enumerate(_other_chips(x, y)):
                got = window(refs[t], row_sharded, 2 * px + py, c)
                other = window(refs[t], row_sharded, 2 * px + py, 1 - c)
                sends.append(_remote(got, got, send_sems, recv_sems, 3 * t + j, (x, y, 1 - c)))
                recvs.append(_remote(other, other, send_sems, recv_sems, 3 * t + j, (x, y, 1 - c)))
        return sends, recvs

    return over_ici, over_d2d


HBM = pl.BlockSpec(memory_space=pltpu.HBM)
SEMAPHORES = pl.BlockSpec(memory_space=pltpu.SEMAPHORE)
DATAFLOW = pltpu.SideEffectType.DATAFLOW_SIDE_EFFECTING


def _start_copies(name, arrays, plan, n_copies, after):
    n = len(arrays)

    def body(*refs):
        sends, _ = plan(refs[:n], refs[n + 1], refs[n + 2])
        for cp in sends:
            cp.start()
        token = refs[2 * n + 3]
        token[...] = jnp.zeros_like(token)

    out = pl.pallas_call(
        body, name=name,
        out_shape=(pltpu.SemaphoreType.DMA((n_copies,)), pltpu.SemaphoreType.DMA((n_copies,)),
                   *[pltpu.HBM(a.shape, a.dtype) for a in arrays], jax.ShapeDtypeStruct((SUBLANES, LANES), F32)),
        in_specs=[HBM] * n + [ANY],
        out_specs=(SEMAPHORES, SEMAPHORES, *[HBM] * n, pl.BlockSpec(memory_space=pltpu.VMEM)),
        input_output_aliases={t: t + 2 for t in range(n)},
        compiler_params=pltpu.CompilerParams(has_side_effects=DATAFLOW),
    )(*[pltpu.with_memory_space_constraint(a, pltpu.HBM) for a in arrays], after)
    return out[0], out[1], list(out[2:2 + n]), out[2 + n]


def _wait_copies(name, started, plan, after):
    send_sems, recv_sems, arrays, _ = started
    n = len(arrays)

    def body(*refs):
        sends, recvs = plan(refs[:n], refs[n], refs[n + 1])
        for cp in sends:
            cp.wait_send()
        for cp in recvs:
            cp.wait_recv()

    out = pl.pallas_call(
        body, name=name, out_shape=[pltpu.HBM(a.shape, a.dtype) for a in arrays],
        in_specs=[HBM] * n + [SEMAPHORES, SEMAPHORES, ANY], out_specs=[HBM] * n,
        input_output_aliases={t: t for t in range(n)},
        compiler_params=pltpu.CompilerParams(has_side_effects=DATAFLOW),
    )(*arrays, send_sems, recv_sems, after)
    return list(out)


def _exchange(name, arrays, out_shapes, plan, n_copies, in_place=False, after=None):
    n = len(arrays)
    extra = [] if after is None else [after]

    def body(*refs):
        ins, outs = refs[:n], refs[n + len(extra):n + len(extra) + len(out_shapes)]
        send_sems, recv_sems = refs[n + len(extra) + len(out_shapes):]
        sends, recvs = plan(ins, outs, send_sems, recv_sems)
        for cp in sends:
            cp.start()
        for cp in recvs:
            cp.wait_recv()
        for cp in sends:
            cp.wait_send()

    return pl.pallas_call(
        body, name=name, in_specs=[ANY] * (n + len(extra)), out_specs=[ANY] * len(out_shapes), out_shape=out_shapes,
        input_output_aliases={t: t for t in range(n)} if in_place else {},
        scratch_shapes=[pltpu.SemaphoreType.DMA((n_copies,)), pltpu.SemaphoreType.DMA((n_copies,))],
    )(*arrays, *extra)


def _give_plan(n):
    def plan(refs, send_sems, recv_sems):
        x, y, c = _mesh_pos()
        sends = [_remote(refs[t], refs[n + t], send_sems, recv_sems, t, (x, y, 1 - c)) for t in range(n)]
        return sends, sends

    return plan


def _scatter_plan(n):
    def plan(refs, send_sems, recv_sems):
        x, y, c = _mesh_pos()
        sends = []
        for t in range(n):
            for j, (px, py) in enumerate(_other_chips(x, y)):
                sends.append(_remote(refs[t].at[2 * px + py], refs[n + t].at[j], send_sems, recv_sems, 3 * t + j, (px, py, c)))
        return sends, sends

    return plan


def _scatter_shapes(sums):
    return [jax.ShapeDtypeStruct((3,) + s.shape[1:], s.dtype) for s in sums]


def _join_plan(n):
    def plan(refs, send_sems, recv_sems):
        x, y, c = _mesh_pos()
        sends = [_remote(refs[t].at[c], refs[t].at[c], send_sems, recv_sems, t, (x, y, 1 - c)) for t in range(n)]
        recvs = [_remote(refs[t].at[1 - c], refs[t].at[1 - c], send_sems, recv_sems, t, (x, y, 1 - c)) for t in range(n)]
        return sends, recvs

    return plan


def _join_halves(halves, name, after=None):
    plan = _join_plan(len(halves))
    shapes = [jax.ShapeDtypeStruct(h.shape, h.dtype) for h in halves]
    return _exchange(name, halves, shapes, lambda ins, outs, s, r: plan(outs, s, r), len(halves), in_place=True, after=after)


def _add_other_half(g4, got, where, name, wire_dtype=BF16):
    _, pieces, sr, sc = g4.shape
    tr = _tile(sr, 256, 2 * SUBLANES)

    def body(w_ref, a_ref, b_ref, o_ref):
        o_ref[...] = (a_ref[...] + b_ref[...]).astype(o_ref.dtype)

    blk = pl.BlockSpec((None, tr, sc), lambda p, i, w_ref: (p, i, 0))
    return pl.pallas_call(
        body, name=name, out_shape=jax.ShapeDtypeStruct((pieces, sr, sc), wire_dtype),
        grid_spec=pltpu.PrefetchScalarGridSpec(
            num_scalar_prefetch=1, grid=(pieces, sr // tr),
            in_specs=[pl.BlockSpec((None, None, tr, sc), lambda p, i, w_ref: (w_ref[0], p, i, 0)), blk], out_specs=blk),
        compiler_params=_params(("parallel", "parallel")),
    )(where, g4, got)


def _add_pieces(sums, got_pieces, where, name):
    _, sr, sc = sums.shape
    tr = _tile(sr, 256, 2 * SUBLANES)

    def body(w_ref, a_ref, r_ref, o_ref):
        acc = a_ref[...]
        for j in range(3):
            acc = acc + r_ref[j].astype(F32)
        o_ref[...] = acc

    return pl.pallas_call(
        body, name=name, out_shape=jax.ShapeDtypeStruct((N_CORES, sr, sc), F32),
        grid_spec=pltpu.PrefetchScalarGridSpec(
            num_scalar_prefetch=1, grid=(sr // tr,),
            in_specs=[pl.BlockSpec((None, tr, sc), lambda i, w_ref: (w_ref[1], i, 0)),
                      pl.BlockSpec((3, tr, sc), lambda i, w_ref: (0, i, 0))],
            out_specs=pl.BlockSpec((None, tr, sc), lambda i, w_ref: (w_ref[0], i, 0))),
        compiler_params=_params(("parallel",)),
    )(where, sums, got_pieces)


def _adamw_update(w, g, m, v):
    nm = ADAM_B1 * m + (1.0 - ADAM_B1) * g
    nv = ADAM_B2 * v + (1.0 - ADAM_B2) * (g * g)
    m_hat = nm / (1.0 - ADAM_B1 ** ADAM_STEP)
    v_hat = nv / (1.0 - ADAM_B2 ** ADAM_STEP)
    return -ADAM_LR * (m_hat / (jnp.sqrt(v_hat) + ADAM_EPS) + ADAM_WD * w), nm, nv


def _adamw(w, g, m, v, name, after=None):
    rows, cols = w.shape
    halves = 2 if g.ndim == 3 else 1
    bc = cols // halves
    tr = _tile(rows, max(SUBLANES, (1 << 19) // max(bc, 1) // SUBLANES * SUBLANES), SUBLANES)

    def body(w_ref, g_ref, m_ref, v_ref, *rest):
        d_ref, nm_ref, nv_ref, go_ref = rest[-4:]
        gv = g_ref[...]
        d_ref[...], nm_ref[...], nv_ref[...] = _adamw_update(w_ref[...], gv, m_ref[...], v_ref[...])
        go_ref[...] = gv

    blk = pl.BlockSpec((tr, bc), lambda i, h: (i, h))
    g_blk = pl.BlockSpec((None, tr, bc), lambda i, h: (h, i, 0)) if halves == 2 else blk
    extra = [] if after is None else [after]
    return pl.pallas_call(
        body, name=name, grid=(rows // tr, halves),
        in_specs=[blk, g_blk, blk, blk] + [pl.BlockSpec(memory_space=pl.ANY)] * len(extra), out_specs=[blk] * 4,
        out_shape=[jax.ShapeDtypeStruct((rows, cols), F32)] * 4, compiler_params=_params(("parallel", "parallel")),
    )(w, g, m, v, *extra)


def _adamw_many(ws, gs, ms, vs, name):
    n = len(ws)

    def body(*refs):
        outs = refs[4 * n:]
        for k in range(n):
            w_ref, g_ref, m_ref, v_ref = (refs[j * n + k] for j in range(4))
            outs[k][...], outs[n + k][...], outs[2 * n + k][...] = _adamw_update(w_ref[...], g_ref[...], m_ref[...], v_ref[...])

    out = pl.pallas_call(
        body, name=name, out_shape=[jax.ShapeDtypeStruct(w.shape, F32) for w in ws] * 3,
        compiler_params=pltpu.CompilerParams(vmem_limit_bytes=VMEM_LIMIT_BYTES),
    )(*ws, *gs, *ms, *vs)
    return out[:n], out[n:2 * n], out[2 * n:]


WEIGHTS = ['attn_norm_w', 'w_in', 'ssm_lambda_re', 'ssm_lambda_im', 'ssm_log_dt', 'ssm_b_re', 'ssm_b_im', 'ssm_c_re',
           'ssm_c_im', 'ssm_d', 'ssm_w_glu', 'ssm_b_glu', 'mla_q_norm_w', 'mla_w_uq', 'mla_kv_norm_w', 'mla_w_ukv',
           'ssm_out_norm_w', 'mla_out_norm_w', 'w_out', 'ffn_norm_w', 'ffn_w_up', 'ffn_conv_w', 'ffn_conv_b',
           'ffn_w_down', 'final_norm_w']
SHARDED = {'w_in': False, 'ssm_w_glu': True, 'mla_w_uq': False, 'mla_w_ukv': False, 'w_out': True, 'ffn_w_up': False,
           'ffn_w_down': True}
SMALL = [n for n in WEIGHTS if n not in SHARDED and n != 'ffn_conv_w']
ROPE_PAD = HEAD_SLOT - QK_NOPE_DIM - QK_ROPE_DIM
SMALL_COLS = 8 * LANES


def _pad_heads(w_uq, heads):
    qr = w_uq.shape[0]
    w3 = w_uq.reshape(qr, heads, QK_NOPE_DIM + QK_ROPE_DIM)
    return jnp.concatenate([w3, jnp.zeros((qr, heads, ROPE_PAD), w_uq.dtype)], axis=2).reshape(qr, heads * HEAD_SLOT)


def _unpad_heads(g_uq, heads):
    qr = g_uq.shape[0]
    return g_uq.reshape(qr, heads, HEAD_SLOT)[:, :, :QK_NOPE_DIM + QK_ROPE_DIM].reshape(qr, -1)


FFN = ['ffn_w_up', 'ffn_w_down']
MIXER_LATER = ['ssm_w_glu', 'w_out']
FFN_GATHER = FFN + ['ffn_conv_w']
FFN_GATHER_META = [(SHARDED[n], False) for n in FFN] + [(False, True)]


class _Overlapped:
    def __init__(self, placed_first, first_sharding, where):
        self.where, self.mine, self.other = where, where[:1], 1 - where[:1]
        self.first_ici, self.first_d2d = _gather_plans([(r, False) for r in first_sharding])
        self.first = _start_copies("gather_first_start", placed_first, self.first_ici, 3 * len(placed_first), where)
        self.first_started = self.first[3]

    def start_rest(self, placed_later, placed):
        self.later_ici, self.later_d2d = _gather_plans([(SHARDED[n], False) for n in MIXER_LATER])
        self.later = _start_copies("gather_later_start", placed_later, self.later_ici, 3 * len(placed_later),
                                   self.first_started)
        up, down, taps = placed
        self.up_ici, self.up_d2d = _gather_plans([(SHARDED["ffn_w_up"], False)])
        self.up = _start_copies("gather_ffn_up_start", [up], self.up_ici, 3, self.later[3])
        self.down_ici, self.down_d2d = _gather_plans([(SHARDED["ffn_w_down"], False), (False, True)])
        self.down = _start_copies("gather_ffn_down_start", [down, taps], self.down_ici, 6, self.up[3])
        self.gather_started = self.down[3]
        arrived = _wait_copies("gather_first_wait", self.first, self.first_ici, self.gather_started)
        shapes = [jax.ShapeDtypeStruct(a.shape, a.dtype) for a in arrived]
        return _exchange("gather_first_pass", arrived, shapes, lambda ins, outs, s, r: self.first_d2d(outs, s, r),
                         3 * len(arrived), in_place=True)

    def mixer_weights(self, after):
        arrived = _wait_copies("gather_later_wait", self.later, self.later_ici, after)
        shapes = [jax.ShapeDtypeStruct(a.shape, a.dtype) for a in arrived]
        passed = _exchange("gather_later_pass", arrived, shapes, lambda ins, outs, s, r: self.later_d2d(outs, s, r),
                           3 * len(arrived), in_place=True)
        return dict(zip(MIXER_LATER, passed))

    def ffn_weights_arrived(self, after):
        arrived = _wait_copies("gather_ffn_up_wait", self.up, self.up_ici, after)
        self.up_passing = _start_copies("gather_ffn_up_pass_start", arrived, self.up_d2d, 3, after)
        return self.up_passing[3]

    def ffn_weights(self, after):
        w_up, = _wait_copies("gather_ffn_up_pass_wait", self.up_passing, self.up_d2d, after)
        down, taps = _wait_copies("gather_ffn_down_wait", self.down, self.down_ici, after)
        self.down_passing = _start_copies("gather_ffn_down_pass_start", [down], self.down_d2d, 3, w_up)
        return {"ffn_w_up": w_up, "ffn_conv_w": taps, "started": self.down_passing[3]}

    def ffn_down_weight(self, after):
        return _wait_copies("gather_ffn_down_pass_wait", self.down_passing, self.down_d2d, after)[0]

    def ffn_grads(self, makers, after):
        self.makers = [makers[name] for name in FFN]
        n = len(FFN)
        give = [make(self.other, suffix="_give") for make in self.makers]
        lands = [lax.empty(g.shape, g.dtype) for g in give]
        self.swap = _start_copies("grad_ffn_swap_start", give + lands, _give_plan(n), n, after)
        return self.swap[3]

    def ffn_backward_done(self, after):
        n = len(FFN)
        got = _wait_copies("grad_ffn_swap_wait", self.swap, _give_plan(n), after)[n:]
        kept = [make(self.mine, suffix="_keep", add=got[t], wire=True) for t, make in enumerate(self.makers)]
        self.sums = [k[0] for k in kept]
        wires = [k[1] for k in kept]
        lands = [lax.empty(s.shape, s.dtype) for s in _scatter_shapes(wires)]
        self.scatter = _start_copies("grad_ffn_scatter_start", wires + lands, _scatter_plan(n), 3 * n, after)
        return self.scatter[3]

    def ffn_reduced(self, after):
        n = len(FFN)
        got_pieces = _wait_copies("grad_ffn_scatter_wait", self.scatter, _scatter_plan(n), after)[n:]
        return [_add_pieces(self.sums[t], got_pieces[t], self.where, "grad_add_pieces_" + name) for t, name in enumerate(FFN)]


def _step(args):
    x, positions, target = args["x"][0], args["positions"], args["loss_target"][0]
    rows = x.shape[0]
    p = {n: args[n] for n in WEIGHTS}
    xi, yi, ci = _mesh_pos()
    piece = 2 * xi + yi

    def transposed(a):
        return jnp.swapaxes(a[0], 0, 1)

    w_in = transposed(p["w_in"])
    in_width = w_in.shape[0]
    in_pad = (-in_width) % (2 * LANES)
    heads_here = p["mla_w_uq"].shape[2] // (QK_NOPE_DIM + QK_ROPE_DIM)
    shards = {
        "w_in": jnp.pad(w_in, ((0, in_pad), (0, 0))),
        "ssm_w_glu": p["ssm_w_glu"][0],
        "mla_w_uq": _pad_heads(p["mla_w_uq"][0], heads_here),
        "mla_w_ukv": p["mla_w_ukv"][0],
        "w_out": p["w_out"][0],
        "ffn_w_up": p["ffn_w_up"][0],
        "ffn_w_down": p["ffn_w_down"][0],
    }
    conv_w = jnp.pad(p["ffn_conv_w"][0], ((0, SUBLANES - p["ffn_conv_w"].shape[1]), (0, 0)))
    order = list(SHARDED)
    piece_idx = piece.reshape(1).astype(jnp.int32)
    mixer = [n for n in order if n not in FFN]
    first = [n for n in mixer if n not in MIXER_LATER]
    where = jnp.stack([ci, piece]).astype(jnp.int32)
    placed = {n: _place_shard(shards[n], piece_idx, SHARDED[n], "place_" + n) for n in first}
    hooks = _Overlapped([placed[n] for n in first], [SHARDED[n] for n in first], where)
    for n in order:
        if n not in first:
            placed[n] = _place_shard(shards[n], piece_idx, SHARDED[n], "place_" + n, after=hooks.first_started)
    placed["ffn_conv_w"] = _place_shard(conv_w, piece_idx, False, "place_ffn_conv_w", out_dtype=F32,
                                        after=hooks.first_started)
    w = dict(zip(first, hooks.start_rest([placed[n] for n in MIXER_LATER], [placed[n] for n in FFN_GATHER])))
    groups = p["ssm_lambda_re"].shape[1]
    w.update({
        "attn_norm_w": p["attn_norm_w"] + hooks.gather_started[:1, :1],
        "ssm_lambda_re": p["ssm_lambda_re"][0], "ssm_lambda_im": p["ssm_lambda_im"][0],
        "ssm_log_dt": p["ssm_log_dt"].reshape(groups, 1), "ssm_b_re": p["ssm_b_re"].reshape(groups, -1),
        "ssm_b_im": p["ssm_b_im"].reshape(groups, -1), "ssm_c_re": p["ssm_c_re"][0], "ssm_c_im": p["ssm_c_im"][0],
        "ssm_d": p["ssm_d"], "ssm_b_glu": p["ssm_b_glu"], "mla_q_norm_w": p["mla_q_norm_w"],
        "mla_kv_norm_w": p["mla_kv_norm_w"], "ssm_out_norm_w": p["ssm_out_norm_w"], "mla_out_norm_w": p["mla_out_norm_w"],
        "ffn_norm_w": p["ffn_norm_w"], "ffn_conv_b": p["ffn_conv_b"], "final_norm_w": p["final_norm_w"].reshape(1, -1),
    })

    loss_tile, dx, g = _local_step(x, positions.reshape(rows, 1).astype(F32), target, w, hooks)

    flat = [g[n].reshape(-1) for n in SMALL] + [g["ffn_conv_w"].reshape(-1), loss_tile[0, :1]]
    sizes = [f.shape[0] for f in flat]
    per_block = -(-sum(sizes) // (N_CORES * N_CHIPS * SMALL_COLS))
    small_rows = -(-per_block // (2 * SUBLANES)) * (2 * SUBLANES)
    padded = N_CORES * N_CHIPS * small_rows * SMALL_COLS

    def pack(parts):
        parts = list(parts)
        have = sum(q.shape[0] for q in parts)
        return jnp.concatenate(parts + [jnp.zeros((padded - have,), F32)])

    reduced = mixer + ["small"]
    small = pack(flat).reshape(N_CORES, N_CHIPS, small_rows, SMALL_COLS)
    give = [g[n](hooks.other, suffix="_give") for n in mixer] + [lax.dynamic_index_in_dim(small, 1 - ci, 0, keepdims=False)]
    lands = [lax.empty(a.shape, a.dtype) for a in give]
    give_plan = _give_plan(len(reduced))
    swap = _start_copies("grad_mixer_swap_start", give + lands, give_plan, len(reduced), dx)

    grads, delta, new_m, new_v = {}, {}, {}, {}

    def finish(n, joined, after=None):
        grad = joined if SHARDED[n] else joined.reshape(-1, joined.shape[2])
        if n == "w_in":
            wt, mt, vt = w_in, transposed(args["m_w_in"]), transposed(args["v_w_in"])
            out = _adamw(wt, grad, mt, vt, "adamw_w_in")
            delta[n], new_m[n], new_v[n], grads[n] = (jnp.swapaxes(a, 0, 1)[None] for a in out)
            return
        if n == "mla_w_uq":
            grad = _unpad_heads(grad, heads_here)
        adam(n, grad, after)

    def adam(n, grad, after=None):
        shape = p[n].shape
        out = _adamw(p[n].reshape(shape[1:]), grad, args["m_" + n].reshape(shape[1:]),
                     args["v_" + n].reshape(shape[1:]), "adamw_" + n, after)
        delta[n], new_m[n], new_v[n], grads[n] = (a.reshape(shape) for a in out)

    ffn_halves = hooks.ffn_reduced(swap[3])
    got = _wait_copies("grad_mixer_swap_wait", swap, give_plan, ffn_halves[-1])[len(reduced):]
    join_plan = _join_plan(len(FFN))
    ffn_join = _start_copies("grad_ffn_join_start", ffn_halves, join_plan, len(FFN), got[0])
    kept = [g[n](hooks.mine, suffix="_keep", add=got[t], wire=True) for t, n in enumerate(mixer)]
    small_sum = _add_other_half(small, got[-1], where, "grad_add_half_small", F32)
    sums = [k[0] for k in kept] + [small_sum]
    wires = [k[1] for k in kept] + [small_sum]
    ffn_joined = _wait_copies("grad_ffn_join_wait", ffn_join, join_plan, kept[-1][0])
    lands = [lax.empty(s.shape, s.dtype) for s in _scatter_shapes(wires)]
    scatter_plan = _scatter_plan(len(reduced))
    scatter = _start_copies("grad_mixer_scatter_start", wires + lands, scatter_plan, 3 * len(reduced), ffn_joined[0])
    behind = scatter[3]
    for n, joined in zip(FFN, ffn_joined):
        finish(n, joined, after=behind)
        behind = delta[n]
    got_pieces = _wait_copies("grad_mixer_scatter_wait", scatter, scatter_plan, delta[FFN[-1]])[len(reduced):]
    halves = [_add_pieces(sums[t], got_pieces[t], where, "grad_add_pieces_" + n) for t, n in enumerate(reduced)]
    joined = _join_halves(halves, "grad_join_halves")
    for n, j in zip(mixer, joined):
        finish(n, j)
    eighths = _place_shard(joined[-1].reshape(N_CORES * small_rows, SMALL_COLS), piece_idx, True, "place_small_grads",
                           out_dtype=F32)
    small_sum = _gather_weights([(eighths, True, False)], "gather_small_grads")[0]
    flat_sum = small_sum.reshape(N_CHIPS, N_CORES, small_rows * SMALL_COLS).transpose(1, 0, 2).reshape(-1)
    offs = [0]
    for s in sizes:
        offs.append(offs[-1] + s)
    for k, n in enumerate(SMALL):
        grads[n] = flat_sum[offs[k]:offs[k + 1]].reshape(p[n].shape)
    taps, cols_here = p["ffn_conv_w"].shape[1], p["ffn_conv_w"].shape[2]
    conv_full = flat_sum[offs[len(SMALL)]:offs[len(SMALL) + 1]].reshape(taps, N_CHIPS * cols_here)
    adam("ffn_conv_w", lax.dynamic_slice_in_dim(conv_full, piece * cols_here, cols_here, axis=1))
    loss = flat_sum[offs[len(SMALL) + 1]]

    def rank2(a):
        return a.reshape(1, -1) if a.ndim == 1 else a

    d_s, m_s, v_s = _adamw_many([rank2(p[n]) for n in SMALL], [rank2(grads[n]) for n in SMALL],
                                [rank2(args["m_" + n]) for n in SMALL], [rank2(args["v_" + n]) for n in SMALL], "adamw_small")
    for k, n in enumerate(SMALL):
        delta[n], new_m[n], new_v[n] = (a.reshape(p[n].shape) for a in (d_s[k], m_s[k], v_s[k]))

    return (loss, dx[None], *[grads[n] for n in WEIGHTS], *[delta[n] for n in WEIGHTS],
            *[new_m[n] for n in WEIGHTS], *[new_v[n] for n in WEIGHTS])


def kernel(x, positions, attn_norm_w, w_in, ssm_lambda_re, ssm_lambda_im, ssm_log_dt, ssm_b_re, ssm_b_im, ssm_c_re, ssm_c_im, ssm_d, ssm_w_glu, ssm_b_glu, mla_q_norm_w, mla_w_uq, mla_kv_norm_w, mla_w_ukv, ssm_out_norm_w, mla_out_norm_w, w_out, ffn_norm_w, ffn_w_up, ffn_conv_w, ffn_conv_b, ffn_w_down, final_norm_w, loss_target, m_attn_norm_w, m_w_in, m_ssm_lambda_re, m_ssm_lambda_im, m_ssm_log_dt, m_ssm_b_re, m_ssm_b_im, m_ssm_c_re, m_ssm_c_im, m_ssm_d, m_ssm_w_glu, m_ssm_b_glu, m_mla_q_norm_w, m_mla_w_uq, m_mla_kv_norm_w, m_mla_w_ukv, m_ssm_out_norm_w, m_mla_out_norm_w, m_w_out, m_ffn_norm_w, m_ffn_w_up, m_ffn_conv_w, m_ffn_conv_b, m_ffn_w_down, m_final_norm_w, v_attn_norm_w, v_w_in, v_ssm_lambda_re, v_ssm_lambda_im, v_ssm_log_dt, v_ssm_b_re, v_ssm_b_im, v_ssm_c_re, v_ssm_c_im, v_ssm_d, v_ssm_w_glu, v_ssm_b_glu, v_mla_q_norm_w, v_mla_w_uq, v_mla_kv_norm_w, v_mla_w_ukv, v_ssm_out_norm_w, v_mla_out_norm_w, v_w_out, v_ffn_norm_w, v_ffn_w_up, v_ffn_conv_w, v_ffn_conv_b, v_ffn_w_down, v_final_norm_w):
    return _step(dict(locals()))
```

```python
import functools
import math

import jax
import jax.numpy as jnp
from jax import lax
from jax.experimental import pallas as pl
from jax.experimental.pallas import tpu as pltpu

F32 = jnp.float32
BF16 = jnp.bfloat16

SSM_GROUP = 16
SSM_STATE = 64
QK_NOPE_DIM = 128
QK_ROPE_DIM = 64
V_HEAD_DIM = 128
ROPE_THETA = 10000.0
RMS_EPS = 1e-6
ADAM_LR, ADAM_B1, ADAM_B2, ADAM_EPS, ADAM_WD, ADAM_STEP = 0.001, 0.9, 0.999, 1e-08, 0.01, 10

LANES = 128
SUBLANES = 8
VMEM_LIMIT_BYTES = 56 * 1024 * 1024

GROUPS_PER_BATCH = LANES // SSM_GROUP
STATE_PER_BATCH = GROUPS_PER_BATCH * SSM_STATE
HEAD_SLOT = 2 * LANES
NEG_INF = -1e30
ATTN_BLOCK = 512
FFN_ROWS = 1024

N_CHIPS = 4
N_CORES = 2


def _tile(n, pref, align=LANES):
    if n <= pref:
        return n
    t = (pref // align) * align
    while t >= align:
        if n % t == 0:
            return t
        t -= align
    return n


def _params(sem):
    return pltpu.CompilerParams(dimension_semantics=sem, vmem_limit_bytes=VMEM_LIMIT_BYTES)


def _dot(a, b, dims):
    return lax.dot_general(a, b, (dims, ((), ())), preferred_element_type=F32)


def _dot_nn(a, b):
    return _dot(a, b, ((1,), (0,)))


def _dot_nt(a, b):
    return _dot(a, b, ((1,), (1,)))


def _dot_tn(a, b):
    return _dot(a, b, ((0,), (0,)))


def _matmul(a, b, *, mode, name, tm=512, tn=1024, tk=2048, bias=None, add=None, out_dtype=F32,
            out_blocks=None, a_split=False, b_split=False, after=None):
    if a_split:
        assert mode == "nt"
        a_shape = (a.shape[1], 2 * a.shape[2])
    else:
        a_shape = a.shape
    if b_split:
        assert mode == "tn"
        b_shape = (b.shape[1], 2 * b.shape[2])
    else:
        b_shape = b.shape
    if mode == "nn":
        (m, k), (k2, n) = a_shape, b_shape
    elif mode == "nt":
        (m, k), (n, k2) = a_shape, b_shape
    else:
        (k, m), (k2, n) = a_shape, b_shape
    assert k == k2, (a.shape, b.shape, mode)
    tm, tn, tk = _tile(m, tm, SUBLANES), _tile(n, tn), _tile(k, tk)
    nk = k // tk
    a_spec = {"nn": pl.BlockSpec((tm, tk), lambda i, j, kk: (i, kk)),
              "nt": pl.BlockSpec((tm, tk), lambda i, j, kk: (i, kk)),
              "tn": pl.BlockSpec((tk, tm), lambda i, j, kk: (kk, i))}[mode]
    b_spec = {"nn": pl.BlockSpec((tk, tn), lambda i, j, kk: (kk, j)),
              "nt": pl.BlockSpec((tn, tk), lambda i, j, kk: (j, kk)),
              "tn": pl.BlockSpec((tk, tn), lambda i, j, kk: (kk, j))}[mode]
    if a_split:
        kb = a.shape[2] // tk
        assert a.shape[2] % tk == 0
        a_spec = pl.BlockSpec((None, tm, tk), lambda i, j, kk: (kk // kb, i, kk % kb))
    if b_split:
        nb = b.shape[2] // tn
        assert b.shape[2] % tn == 0
        b_spec = pl.BlockSpec((None, tk, tn), lambda i, j, kk: (j // nb, kk, j % nb))
    dot = {"nn": _dot_nn, "nt": _dot_nt, "tn": _dot_tn}[mode]
    in_specs, operands = [a_spec, b_spec], [a, b]
    if bias is not None:
        in_specs.append(pl.BlockSpec((1, tn), lambda i, j, kk: (0, j)))
        operands.append(bias)
    if add is not None:
        in_specs.append(pl.BlockSpec((tm, tn), lambda i, j, kk: (i, j)))
        operands.append(add)
    if after is not None:
        in_specs.append(pl.BlockSpec(memory_space=pl.ANY))
        operands.append(after)

    def body(*refs):
        a_ref, b_ref = refs[0], refs[1]
        rest = list(refs[2:])
        bias_ref = rest.pop(0) if bias is not None else None
        add_ref = rest.pop(0) if add is not None else None
        if after is not None:
            rest.pop(0)
        o_ref, acc_ref = rest

        def finish(acc):
            if bias_ref is not None:
                acc = acc + bias_ref[...]
            if add_ref is not None:
                acc = acc + add_ref[...]
            o_ref[...] = acc.astype(o_ref.dtype)

        part = dot(a_ref[...].astype(BF16), b_ref[...].astype(BF16))
        if nk == 1:
            finish(part)
        else:
            kk = pl.program_id(2)

            @pl.when(kk == 0)
            def _():
                acc_ref[...] = part

            @pl.when(jnp.logical_and(kk > 0, kk < nk - 1))
            def _():
                acc_ref[...] += part

            @pl.when(kk == nk - 1)
            def _():
                finish(acc_ref[...] + part)

    if out_blocks is None:
        out_shape = jax.ShapeDtypeStruct((m, n), out_dtype)
        out_spec = pl.BlockSpec((tm, tn), lambda i, j, kk: (i, j))
    else:
        shape, block, index_map = out_blocks(tm, tn)
        out_shape = jax.ShapeDtypeStruct(shape, out_dtype)
        out_spec = pl.BlockSpec(block, index_map)
    acc_shape = (tm, tn) if nk > 1 else (SUBLANES, LANES)
    return pl.pallas_call(
        body, name=name, grid=(m // tm, n // tn, nk), in_specs=in_specs, out_specs=out_spec, out_shape=out_shape,
        scratch_shapes=[pltpu.VMEM(acc_shape, F32)],
        compiler_params=_params(("parallel", "parallel", "arbitrary")),
    )(*operands)


def _wgrad_half(a, b, rows, cols, row_sharded, name, which, *, suffix="", add=None, wire=False, tm=None, tn=None,
                b_split=False):
    tokens = a.shape[0]
    if row_sharded:
        sr, sc = rows // N_CHIPS, cols // N_CORES
    else:
        sr, sc = rows // N_CORES, cols // N_CHIPS
    tm = _tile(sr, 512) if tm is None else tm
    tn = _tile(sc, 1024) if tn is None else tn
    assert sr % tm == 0 and sc % tn == 0, (rows, cols, tm, tn)
    rb, cb = sr // tm, sc // tn
    if tn >= tm:
        ij, grid = (lambda s, t: (t, s)), (N_CHIPS, cb, rb)
    else:
        ij, grid = (lambda s, t: (s, t)), (N_CHIPS, rb, cb)
    if row_sharded:
        a_tile = lambda p, i, j, h: p * rb + i
        b_tile = lambda p, i, j, h: h[0] * cb + j
    else:
        a_tile = lambda p, i, j, h: h[0] * rb + i
        b_tile = lambda p, i, j, h: p * cb + j
    a_spec = pl.BlockSpec((tokens, tm), lambda p, s, t, h: (0, a_tile(p, *ij(s, t), h)))
    if b_split:
        nbh = b.shape[2] // tn
        assert b.shape[2] % tn == 0
        b_spec = pl.BlockSpec((None, tokens, tn), lambda p, s, t, h: (b_tile(p, *ij(s, t), h) // nbh, 0,
                                                                       b_tile(p, *ij(s, t), h) % nbh))
    else:
        b_spec = pl.BlockSpec((tokens, tn), lambda p, s, t, h: (0, b_tile(p, *ij(s, t), h)))
    out_spec = pl.BlockSpec((None, tm, tn), lambda p, s, t, h: (p, *ij(s, t)))
    in_specs, operands = [a_spec, b_spec], [a, b]
    if add is not None:
        in_specs.append(out_spec)
        operands.append(add)

    def body(h_ref, a_ref, b_ref, *rest):
        acc = _dot_tn(a_ref[...].astype(BF16), b_ref[...].astype(BF16))
        if add is not None:
            acc = acc + rest[0][...]
        for o_ref in rest[1 if add is not None else 0:]:
            o_ref[...] = acc.astype(o_ref.dtype)

    out_dtypes = [F32, BF16] if wire else [F32]
    out = pl.pallas_call(
        body, name=name + suffix, out_shape=[jax.ShapeDtypeStruct((N_CHIPS, sr, sc), dt) for dt in out_dtypes],
        grid_spec=pltpu.PrefetchScalarGridSpec(num_scalar_prefetch=1, grid=grid, in_specs=in_specs,
                                               out_specs=[out_spec] * len(out_dtypes)),
        compiler_params=_params(("parallel", "parallel", "parallel")),
    )(which, *operands)
    return tuple(out) if wire else out[0]


def _rms_rows(x):
    return lax.rsqrt(jnp.mean(x * x, axis=-1, keepdims=True) + RMS_EPS)


def _rmsnorm_fwd(x, w, *, name, width=None, col=0, out_dtype=BF16, tr=256):
    rows = x.shape[0]
    width = x.shape[1] if width is None else width
    tr = _tile(rows, tr, SUBLANES)

    def body(x_ref, w_ref, o_ref):
        xv = x_ref[...]
        o_ref[...] = (xv * _rms_rows(xv) * w_ref[...]).astype(o_ref.dtype)

    return pl.pallas_call(
        body, name=name, grid=(rows // tr,),
        in_specs=[pl.BlockSpec((tr, width), lambda i: (i, col)), pl.BlockSpec((1, width), lambda i: (0, 0))],
        out_specs=pl.BlockSpec((tr, width), lambda i: (i, 0)),
        out_shape=jax.ShapeDtypeStruct((rows, width), out_dtype),
        compiler_params=_params(("parallel",)),
    )(x, w)


def _rmsnorm_bwd_rows(xv, w, dy):
    r = _rms_rows(xv)
    n = xv * r
    dn = dy * w
    dx = r * (dn - n * jnp.mean(dn * n, axis=-1, keepdims=True))
    return dx, dy * n


def _rmsnorm_bwd(x, w, dy, *, name, width=None, col=0, dy_col=0, add=None, tr=256, dx_dtypes=(F32,)):
    rows = x.shape[0]
    n_dx = len(dx_dtypes)
    width = x.shape[1] if width is None else width
    tr = _tile(rows, tr, SUBLANES)
    in_specs = [pl.BlockSpec((tr, width), lambda i: (i, col)), pl.BlockSpec((1, width), lambda i: (0, 0)),
                pl.BlockSpec((tr, width), lambda i: (i, dy_col))]
    operands = [x, w, dy]
    if add is not None:
        in_specs.append(pl.BlockSpec((tr, width), lambda i: (i, 0)))
        operands.append(add)

    def body(*refs):
        x_ref, w_ref, dy_ref = refs[:3]
        add_ref = refs[3] if add is not None else None
        dx_refs, dw_ref = refs[-1 - n_dx:-1], refs[-1]
        dx, dwp = _rmsnorm_bwd_rows(x_ref[...], w_ref[...], dy_ref[...])
        if add_ref is not None:
            dx = dx + add_ref[...]
        for dx_ref in dx_refs:
            dx_ref[...] = dx.astype(dx_ref.dtype)
        part = jnp.sum(dwp, axis=0, keepdims=True)

        @pl.when(pl.program_id(0) == 0)
        def _():
            dw_ref[...] = part

        @pl.when(pl.program_id(0) > 0)
        def _():
            dw_ref[...] += part

    return pl.pallas_call(
        body, name=name, grid=(rows // tr,), in_specs=in_specs,
        out_specs=[pl.BlockSpec((tr, width), lambda i: (i, 0))] * n_dx + [pl.BlockSpec((1, width), lambda i: (0, 0))],
        out_shape=[jax.ShapeDtypeStruct((rows, width), dt) for dt in dx_dtypes] + [jax.ShapeDtypeStruct((1, width), F32)],
        compiler_params=_params(("arbitrary",)),
    )(*operands)


def _final_norm_loss(h, w, target, *, tr=256):
    rows, d = h.shape
    tr = _tile(rows, tr, SUBLANES)

    def body(h_ref, w_ref, t_ref, loss_ref, dh_ref, dhb_ref, dw_ref):
        hv, wv = h_ref[...], w_ref[...]
        r = _rms_rows(hv)
        n = hv * r
        err = n * wv - t_ref[...]
        d_out = err * (1.0 / d)
        dn = d_out * wv
        dh = r * (dn - n * jnp.mean(dn * n, axis=-1, keepdims=True))
        dh_ref[...] = dh
        dhb_ref[...] = dh.astype(BF16)
        dw_part = jnp.sum(d_out * n, axis=0, keepdims=True)
        loss_part = jnp.full((SUBLANES, LANES), 0.5 / d, F32) * jnp.sum(err * err)

        @pl.when(pl.program_id(0) == 0)
        def _():
            dw_ref[...] = dw_part
            loss_ref[...] = loss_part

        @pl.when(pl.program_id(0) > 0)
        def _():
            dw_ref[...] += dw_part
            loss_ref[...] += loss_part

    return pl.pallas_call(
        body, name="final_norm_loss", grid=(rows // tr,),
        in_specs=[pl.BlockSpec((tr, d), lambda i: (i, 0)), pl.BlockSpec((1, d), lambda i: (0, 0)),
                  pl.BlockSpec((tr, d), lambda i: (i, 0))],
        out_specs=[pl.BlockSpec((SUBLANES, LANES), lambda i: (0, 0)), pl.BlockSpec((tr, d), lambda i: (i, 0)),
                   pl.BlockSpec((tr, d), lambda i: (i, 0)), pl.BlockSpec((1, d), lambda i: (0, 0))],
        out_shape=[jax.ShapeDtypeStruct((SUBLANES, LANES), F32), jax.ShapeDtypeStruct((rows, d), F32),
                   jax.ShapeDtypeStruct((rows, d), BF16), jax.ShapeDtypeStruct((1, d), F32)],
        compiler_params=_params(("arbitrary",)),
    )(h, w, target)


def _cmul(ar, ai, br, bi):
    return ar * br - ai * bi, ar * bi + ai * br


def _dot_exact(a, b, dims):
    return lax.dot_general(a, b, (dims, ((), ())), preferred_element_type=F32, precision=lax.Precision.HIGHEST)


def _s5_discretize(lr, li, dt):
    mag = jnp.exp(lr * dt)
    th = li * dt
    ar, ai = mag * jnp.cos(th), mag * jnp.sin(th)
    nr, ni = ar - 1.0, ai
    den = lr * lr + li * li
    zr = (nr * lr + ni * li) / den
    zi = (ni * lr - nr * li) / den
    return mag, ar, ai, nr, ni, den, zr, zi


def _band_slices(group):
    j, gi = divmod(group, GROUPS_PER_BATCH)
    rows = slice(gi * SSM_GROUP, (gi + 1) * SSM_GROUP)
    re = slice(gi * SSM_STATE, (gi + 1) * SSM_STATE)
    im = slice(STATE_PER_BATCH + gi * SSM_STATE, STATE_PER_BATCH + (gi + 1) * SSM_STATE)
    return j, rows, re, im


def _s5_bands(lam_re, lam_im, log_dt, b_re, b_im, c_re, c_im):
    g, _ = lam_re.shape
    nb = g // GROUPS_PER_BATCH
    s2 = 2 * STATE_PER_BATCH

    def body(lr_ref, li_ref, ldt_ref, br_ref, bi_ref, cr_ref, ci_ref, wb_ref, wct_ref, a_ref):
        dt = jnp.exp(ldt_ref[...])
        _, ar, ai, _, _, _, zr, zi = _s5_discretize(lr_ref[...], li_ref[...], dt)
        wb_ref[...] = jnp.zeros_like(wb_ref)
        wct_ref[...] = jnp.zeros_like(wct_ref)
        for group in range(g):
            j, rows, re, im = _band_slices(group)
            zr_g, zi_g = zr[group:group + 1, :], zi[group:group + 1, :]
            bre, bim = br_ref[group], bi_ref[group]
            wb_ref[j, rows, re] = (zr_g * bre - zi_g * bim).astype(BF16)
            wb_ref[j, rows, im] = (zr_g * bim + zi_g * bre).astype(BF16)
            wct_ref[j, rows, re] = cr_ref[group].astype(BF16)
            wct_ref[j, rows, im] = (-ci_ref[group]).astype(BF16)
            a_ref[j, :, re] = ar[group:group + 1, :]
            a_ref[j, :, im] = ai[group:group + 1, :]

    return pl.pallas_call(
        body, name="s5_bands",
        out_shape=[jax.ShapeDtypeStruct((nb, LANES, s2), BF16)] * 2 + [jax.ShapeDtypeStruct((nb, 1, s2), F32)],
    )(lam_re, lam_im, log_dt, b_re, b_im, c_re, c_im)


def _s5_bands_bwd(lam_re, lam_im, log_dt, b_re, b_im, dwb, dwct, dabar):
    g, p = lam_re.shape
    gh = b_re.shape[1:]

    def body(lr_ref, li_ref, ldt_ref, br_ref, bi_ref, dwb_ref, dwct_ref, da_ref,
             dlr_ref, dli_ref, dldt_ref, dbre_ref, dbim_ref, dcre_ref, dcim_ref, dzr_ref, dzi_ref, dar_ref, dai_ref):
        lr, li = lr_ref[...], li_ref[...]
        dt = jnp.exp(ldt_ref[...])
        mag, ar, ai, nr, ni, den, zr, zi = _s5_discretize(lr, li, dt)
        for group in range(g):
            j, rows, re, im = _band_slices(group)
            zr_g, zi_g = zr[group:group + 1, :], zi[group:group + 1, :]
            bre, bim = br_ref[group], bi_ref[group]
            dbr, dbi = dwb_ref[j, rows, re], dwb_ref[j, rows, im]
            dbre_ref[group] = zr_g * dbr + zi_g * dbi
            dbim_ref[group] = zr_g * dbi - zi_g * dbr
            dzr_ref[group:group + 1, :] = jnp.sum(bre * dbr + bim * dbi, axis=0, keepdims=True)
            dzi_ref[group:group + 1, :] = jnp.sum(bre * dbi - bim * dbr, axis=0, keepdims=True)
            dcre_ref[group] = dwct_ref[j, rows, re]
            dcim_ref[group] = -dwct_ref[j, rows, im]
            dar_ref[group:group + 1, :] = da_ref[j, :, re]
            dai_ref[group:group + 1, :] = da_ref[j, :, im]
        dzr, dzi = dzr_ref[...], dzi_ref[...]
        inv = 1.0 / den
        d_nr = (dzr * lr - dzi * li) * inv
        d_ni = (dzr * li + dzi * lr) * inv
        d_den = -(dzr * zr + dzi * zi) * inv
        d_lr = (dzr * nr + dzi * ni) * inv + 2.0 * lr * d_den
        d_li = (dzr * ni - dzi * nr) * inv + 2.0 * li * d_den
        t_ar = dar_ref[...] + d_nr
        t_ai = dai_ref[...] + d_ni
        d_lrdt = t_ar * ar + t_ai * ai
        d_th = t_ai * ar - t_ar * ai
        dlr_ref[...] = d_lr + d_lrdt * dt
        dli_ref[...] = d_li + d_th * dt
        dldt_ref[...] = jnp.sum(d_lrdt * lr + d_th * li, axis=1, keepdims=True) * dt

    return pl.pallas_call(
        body, name="s5_bands_bwd",
        out_shape=[jax.ShapeDtypeStruct((g, p), F32)] * 2 + [jax.ShapeDtypeStruct((g, 1), F32)]
        + [jax.ShapeDtypeStruct((g,) + gh, F32)] * 4,
        scratch_shapes=[pltpu.VMEM((g, p), F32)] * 4,
    )(lam_re, lam_im, log_dt, b_re, b_im, dwb, dwct, dabar)


def _powers(ar, ai, count):
    out = [(ar, ai)]
    for _ in range(count - 1):
        out.append(_cmul(out[-1][0], out[-1][1], ar, ai))
    return out


def _scan_coefs(ar, ai, reverse):
    w = ar.shape[-1]
    pw = _powers(ar, ai, SUBLANES)
    row = lax.broadcasted_iota(jnp.int32, (SUBLANES, w), 0)
    steps = []
    d = 1
    while d < SUBLANES:
        keep = (row < SUBLANES - d) if reverse else (row >= d)
        pr, pi = pw[d - 1]
        steps.append((d, jnp.where(keep, pr, 0.0), jnp.where(keep, pi, 0.0)))
        d *= 2
    cr = jnp.zeros((SUBLANES, w), F32)
    ci = jnp.zeros((SUBLANES, w), F32)
    for t in range(SUBLANES):
        pr, pi = pw[SUBLANES - 1 - t] if reverse else pw[t]
        cr = jnp.where(row == t, pr, cr)
        ci = jnp.where(row == t, pi, ci)
    return steps, cr, ci


def _scan_tile(xr, xi, carry_r, carry_i, coefs, reverse):
    steps, cr, ci = coefs
    for d, mr, mi in steps:
        shift = SUBLANES - d if reverse else d
        sr, si = pltpu.roll(xr, shift, 0), pltpu.roll(xi, shift, 0)
        pr, pi = _cmul(mr, mi, sr, si)
        xr, xi = xr + pr, xi + pi
    pr, pi = _cmul(cr, ci, carry_r, carry_i)
    return xr + pr, xi + pi


def _gelu(x):
    c = math.sqrt(2.0 / math.pi)
    return 0.5 * x * (1.0 + jnp.tanh(c * (x + 0.044715 * x * x * x)))


def _gelu_grad(x):
    c = math.sqrt(2.0 / math.pi)
    t = jnp.tanh(c * (x + 0.044715 * x * x * x))
    return 0.5 * (1.0 + t) + 0.5 * x * (1.0 - t * t) * c * (1.0 + 3.0 * 0.044715 * x * x)


def _s5_fwd(proj, wb, wct, d_skip, abar):
    rows = proj.shape[0]
    nb = wb.shape[0]
    s2 = 2 * STATE_PER_BATCH
    st = STATE_PER_BATCH
    chunk = _tile(rows, 512, SUBLANES)

    def body(u_ref, wb_ref, wc_ref, d_ref, a_ref, s_ref, y_ref, yg_ref):
        for c0 in range(0, rows, chunk):
            s_ref[pl.ds(c0, chunk), :] = _dot_nn(u_ref[pl.ds(c0, chunk), :].astype(BF16), wb_ref[...])
        av = a_ref[...]
        coefs = _scan_coefs(av[:, :st], av[:, st:], reverse=False)

        def tile(b, carry):
            r0 = pl.multiple_of(b * SUBLANES, SUBLANES)
            xr, xi = _scan_tile(s_ref[pl.ds(r0, SUBLANES), :st], s_ref[pl.ds(r0, SUBLANES), st:], carry[0], carry[1],
                                coefs, False)
            s_ref[pl.ds(r0, SUBLANES), :st] = xr
            s_ref[pl.ds(r0, SUBLANES), st:] = xi
            return xr[SUBLANES - 1:, :], xi[SUBLANES - 1:, :]

        zero = jnp.zeros((1, st), F32)
        lax.fori_loop(0, rows // SUBLANES, tile, (zero, zero))
        for c0 in range(0, rows, chunk):
            y = _dot_nt(s_ref[pl.ds(c0, chunk), :].astype(BF16), wc_ref[...]) + d_ref[...] * u_ref[pl.ds(c0, chunk), :]
            y_ref[pl.ds(c0, chunk), :] = y
            yg_ref[pl.ds(c0, chunk), :] = _gelu(y).astype(BF16)

    return pl.pallas_call(
        body, name="s5_fwd", grid=(nb,),
        in_specs=[pl.BlockSpec((rows, LANES), lambda j: (0, j)), pl.BlockSpec((None, LANES, s2), lambda j: (j, 0, 0)),
                  pl.BlockSpec((None, LANES, s2), lambda j: (j, 0, 0)), pl.BlockSpec((1, LANES), lambda j: (0, j)),
                  pl.BlockSpec((None, 1, s2), lambda j: (j, 0, 0))],
        out_specs=[pl.BlockSpec((rows, s2), lambda j: (0, j)), pl.BlockSpec((rows, LANES), lambda j: (0, j)),
                   pl.BlockSpec((rows, LANES), lambda j: (0, j))],
        out_shape=[jax.ShapeDtypeStruct((rows, nb * s2), F32), jax.ShapeDtypeStruct((rows, nb * LANES), F32),
                   jax.ShapeDtypeStruct((rows, nb * LANES), BF16)],
        compiler_params=_params(("parallel",)),
    )(proj, wb, wct, d_skip, abar)


def _s5_bwd(proj, states, y_pre, dyg_a, dyg_b, wb, wct, d_skip, abar):
    rows = proj.shape[0]
    nb = wb.shape[0]
    s2 = 2 * STATE_PER_BATCH
    st = STATE_PER_BATCH
    chunk = _tile(rows, 512, SUBLANES)
    n_tiles = rows // SUBLANES

    def body(u_ref, s_ref, y_ref, ga_ref, gb_ref, wb_ref, wc_ref, d_ref, a_ref,
             du_ref, dwb_ref, dwc_ref, da_ref, dd_ref, ds_ref, dy_ref):
        dy_ref[...] = (ga_ref[...] + gb_ref[...]) * _gelu_grad(y_ref[...])
        dd_ref[...] = jnp.sum(dy_ref[...] * u_ref[...], axis=0, keepdims=True)
        for c0 in range(0, rows, chunk):
            ds_ref[pl.ds(c0, chunk), :] = _dot_nn(dy_ref[pl.ds(c0, chunk), :].astype(BF16), wc_ref[...])
        dwc_ref[...] = _dot_tn(dy_ref[...].astype(BF16), s_ref[...].astype(BF16))
        av = a_ref[...]
        coefs = _scan_coefs(av[:, :st], -av[:, st:], reverse=True)
        row = lax.broadcasted_iota(jnp.int32, (SUBLANES, st), 0)

        def tile(k, carry):
            cr, ci, acc_r, acc_i = carry
            b = n_tiles - 1 - k
            r0 = pl.multiple_of(b * SUBLANES, SUBLANES)
            rp = pl.multiple_of(jnp.maximum(b - 1, 0) * SUBLANES, SUBLANES)
            xr, xi = _scan_tile(ds_ref[pl.ds(r0, SUBLANES), :st], ds_ref[pl.ds(r0, SUBLANES), st:], cr, ci, coefs, True)
            ds_ref[pl.ds(r0, SUBLANES), :st] = xr
            ds_ref[pl.ds(r0, SUBLANES), st:] = xi
            first = jnp.where(b > 0, 1.0, 0.0)
            pr = jnp.where(row == 0, pltpu.roll(s_ref[pl.ds(rp, SUBLANES), :st], 1, 0) * first,
                           pltpu.roll(s_ref[pl.ds(r0, SUBLANES), :st], 1, 0))
            pi = jnp.where(row == 0, pltpu.roll(s_ref[pl.ds(rp, SUBLANES), st:], 1, 0) * first,
                           pltpu.roll(s_ref[pl.ds(r0, SUBLANES), st:], 1, 0))
            acc_r = acc_r + pr * xr + pi * xi
            acc_i = acc_i + pr * xi - pi * xr
            return xr[:1, :], xi[:1, :], acc_r, acc_i

        zero = jnp.zeros((1, st), F32)
        zacc = jnp.zeros((SUBLANES, st), F32)
        _, _, acc_r, acc_i = lax.fori_loop(0, n_tiles, tile, (zero, zero, zacc, zacc))
        da_ref[:, :st] = jnp.sum(acc_r, axis=0, keepdims=True)
        da_ref[:, st:] = jnp.sum(acc_i, axis=0, keepdims=True)
        for c0 in range(0, rows, chunk):
            du_ref[pl.ds(c0, chunk), :] = (_dot_nt(ds_ref[pl.ds(c0, chunk), :].astype(BF16), wb_ref[...])
                                           + d_ref[...] * dy_ref[pl.ds(c0, chunk), :]).astype(du_ref.dtype)
        dwb_ref[...] = _dot_tn(u_ref[...].astype(BF16), ds_ref[...].astype(BF16))

    col = pl.BlockSpec((rows, LANES), lambda j: (0, j))
    return pl.pallas_call(
        body, name="s5_bwd", grid=(nb,),
        in_specs=[col, pl.BlockSpec((rows, s2), lambda j: (0, j)), col, col, col,
                  pl.BlockSpec((None, LANES, s2), lambda j: (j, 0, 0)), pl.BlockSpec((None, LANES, s2), lambda j: (j, 0, 0)),
                  pl.BlockSpec((1, LANES), lambda j: (0, j)), pl.BlockSpec((None, 1, s2), lambda j: (j, 0, 0))],
        out_specs=[col, pl.BlockSpec((None, LANES, s2), lambda j: (j, 0, 0)),
                   pl.BlockSpec((None, LANES, s2), lambda j: (j, 0, 0)), pl.BlockSpec((None, 1, s2), lambda j: (j, 0, 0)),
                   pl.BlockSpec((1, LANES), lambda j: (0, j))],
        out_shape=[jax.ShapeDtypeStruct((rows, nb * LANES), BF16), jax.ShapeDtypeStruct((nb, LANES, s2), F32),
                   jax.ShapeDtypeStruct((nb, LANES, s2), F32), jax.ShapeDtypeStruct((nb, 1, s2), F32),
                   jax.ShapeDtypeStruct((1, nb * LANES), F32)],
        scratch_shapes=[pltpu.VMEM((rows, s2), F32), pltpu.VMEM((rows, LANES), F32)],
        compiler_params=_params(("parallel",)),
    )(proj, states, y_pre, dyg_a, dyg_b, wb, wct, d_skip, abar)


def _glu_norm_fwd(y_pre, z, w, *, tr=256):
    rows, width = y_pre.shape
    tr = _tile(rows, tr, SUBLANES)

    def body(y_ref, z_ref, w_ref, o_ref):
        v = _gelu(y_ref[...]) * jax.nn.sigmoid(z_ref[...])
        o_ref[...] = (v * _rms_rows(v) * w_ref[...]).astype(o_ref.dtype)

    blk = pl.BlockSpec((tr, width), lambda i: (i, 0))
    return pl.pallas_call(
        body, name="glu_norm_fwd", grid=(rows // tr,),
        in_specs=[blk, blk, pl.BlockSpec((1, width), lambda i: (0, 0))], out_specs=blk,
        out_shape=jax.ShapeDtypeStruct((rows, width), BF16), compiler_params=_params(("parallel",)),
    )(y_pre, z, w)


def _glu_norm_bwd(y_pre, z, w, dycat, *, tr=256):
    rows, width = y_pre.shape
    tr = _tile(rows, tr, SUBLANES)

    def body(y_ref, z_ref, w_ref, dy_ref, dz_ref, dg_ref, dw_ref, db_ref):
        yg = _gelu(y_ref[...])
        sg = jax.nn.sigmoid(z_ref[...])
        dv, dwp = _rmsnorm_bwd_rows(yg * sg, w_ref[...], dy_ref[...])
        dz = dv * yg * sg * (1.0 - sg)
        dz_ref[...] = dz.astype(dz_ref.dtype)
        dg_ref[...] = dv * sg
        dw_part = jnp.sum(dwp, axis=0, keepdims=True)
        db_part = jnp.sum(dz, axis=0, keepdims=True)

        @pl.when(pl.program_id(0) == 0)
        def _():
            dw_ref[...] = dw_part
            db_ref[...] = db_part

        @pl.when(pl.program_id(0) > 0)
        def _():
            dw_ref[...] += dw_part
            db_ref[...] += db_part

    blk = pl.BlockSpec((tr, width), lambda i: (i, 0))
    vec = pl.BlockSpec((1, width), lambda i: (0, 0))
    return pl.pallas_call(
        body, name="glu_norm_bwd", grid=(rows // tr,), in_specs=[blk, blk, vec, blk], out_specs=[blk, blk, vec, vec],
        out_shape=[jax.ShapeDtypeStruct((rows, width), BF16), jax.ShapeDtypeStruct((rows, width), F32)]
        + [jax.ShapeDtypeStruct((1, width), F32)] * 2,
        compiler_params=_params(("arbitrary",)),
    )(y_pre, z, w, dycat)


def _rope_tables(pos, freq, sign):
    rows = pos.shape[0]

    def body(p_ref, f_ref, s_ref, cos_ref, sin_ref):
        ang = p_ref[...] * f_ref[...]
        cos_ref[...] = jnp.cos(ang)
        sin_ref[...] = jnp.sin(ang) * s_ref[...]

    return pl.pallas_call(body, name="rope_tables", out_shape=[jax.ShapeDtypeStruct((rows, LANES), F32)] * 2)(pos, freq, sign)


def _rope(x, cos, sin_signed):
    half = QK_ROPE_DIM // 2
    src = lax.broadcasted_iota(jnp.int32, (LANES, LANES), 0)
    dst = lax.broadcasted_iota(jnp.int32, (LANES, LANES), 1)
    swap = jnp.where(jnp.logical_or(jnp.logical_and(dst < half, src == dst + half),
                                    jnp.logical_and(jnp.logical_and(dst >= half, dst < 2 * half), src == dst - half)),
                     1.0, 0.0).astype(F32)
    swapped = _dot_exact(x, swap, ((1,), (0,)))
    return x * cos + swapped * sin_signed


def _attn_prep(q, kv, proj, kpe_col, cos, sin, *, tr=256):
    rows = q.shape[0]
    heads = q.shape[1] // HEAD_SLOT
    tr = _tile(rows, tr, SUBLANES)

    def body(q_ref, kv_ref, kpe_ref, cos_ref, sin_ref, qc_ref, kc_ref, v_ref):
        c, s = cos_ref[...], sin_ref[...]
        kpe = _rope(kpe_ref[...], c, s).astype(BF16)
        for h in range(heads):
            nope = slice(h * HEAD_SLOT, h * HEAD_SLOT + LANES)
            pe = slice(h * HEAD_SLOT + LANES, (h + 1) * HEAD_SLOT)
            qc_ref[:, nope] = q_ref[:, nope].astype(BF16)
            qc_ref[:, pe] = _rope(q_ref[:, pe], c, s).astype(BF16)
            kc_ref[:, nope] = kv_ref[:, nope].astype(BF16)
            kc_ref[:, pe] = kpe
            v_ref[:, h * LANES:(h + 1) * LANES] = kv_ref[:, pe].astype(BF16)

    slots = pl.BlockSpec((tr, heads * HEAD_SLOT), lambda i: (i, 0))
    tab = pl.BlockSpec((tr, LANES), lambda i: (i, 0))
    return pl.pallas_call(
        body, name="attn_prep", grid=(rows // tr,),
        in_specs=[slots, slots, pl.BlockSpec((tr, LANES), lambda i: (i, kpe_col)), tab, tab],
        out_specs=[slots, slots, pl.BlockSpec((tr, heads * LANES), lambda i: (i, 0))],
        out_shape=[jax.ShapeDtypeStruct((rows, heads * HEAD_SLOT), BF16)] * 2
        + [jax.ShapeDtypeStruct((rows, heads * LANES), BF16)],
        compiler_params=_params(("parallel",)),
    )(q, kv, proj, cos, sin)


def _causal(tq, tk):
    return lax.broadcasted_iota(jnp.int32, (tq, tk), 1) <= lax.broadcasted_iota(jnp.int32, (tq, tk), 0)


def _attn_fwd(qc, kc, vb, *, scale, tq=512):
    rows = qc.shape[0]
    heads = qc.shape[1] // HEAD_SLOT
    tq = _tile(rows, tq, SUBLANES)
    tk = tq

    def body(q_ref, k_ref, v_ref, o_ref, lse_ref):
        i = pl.program_id(1)
        q = q_ref[...]

        def step(j, carry, diagonal):
            m, l, acc = carry
            k0 = pl.multiple_of(j * tk, tk)
            s = _dot_nt(q, k_ref[pl.ds(k0, tk), :]) * scale
            if diagonal:
                s = jnp.where(_causal(tq, tk), s, NEG_INF)
            m_new = jnp.maximum(m, jnp.max(s, axis=-1, keepdims=True))
            p = jnp.exp(s - m_new)
            alpha = jnp.exp(m - m_new)
            l = alpha * l + jnp.sum(p, axis=-1, keepdims=True)
            acc = alpha * acc + _dot_nn(p.astype(BF16), v_ref[pl.ds(k0, tk), :])
            return m_new, l, acc

        init = (jnp.full((tq, 1), NEG_INF, F32), jnp.zeros((tq, 1), F32), jnp.zeros((tq, LANES), F32))
        below = lax.fori_loop(0, i, lambda j, carry: step(j, carry, False), init)
        m, l, acc = step(i, below, True)
        o_ref[...] = acc / l
        lse_ref[...] = jnp.broadcast_to(m + jnp.log(l), (tq, LANES))

    return pl.pallas_call(
        body, name="attn_fwd", grid=(heads, rows // tq),
        in_specs=[pl.BlockSpec((tq, HEAD_SLOT), lambda h, i: (i, h)), pl.BlockSpec((rows, HEAD_SLOT), lambda h, i: (0, h)),
                  pl.BlockSpec((rows, LANES), lambda h, i: (0, h))],
        out_specs=[pl.BlockSpec((tq, LANES), lambda h, i: (i, h))] * 2,
        out_shape=[jax.ShapeDtypeStruct((rows, heads * LANES), F32)] * 2,
        compiler_params=_params(("parallel", "parallel")),
    )(qc, kc, vb)


def _attn_bwd(qc, kc, vb, o, do, lse, cos, sin, *, scale, tk=512):
    rows = qc.shape[0]
    heads = qc.shape[1] // HEAD_SLOT
    tk = _tile(rows, tk, SUBLANES)
    tq = tk
    nq = rows // tq

    def body(q_ref, k_ref, v_ref, o_ref, do_ref, lse_ref, cos_ref, sin_ref, dq_ref, dkv_ref, dkpe_ref, dq_acc, delta_ref):
        j = pl.program_id(1)

        @pl.when(j == 0)
        def _():
            dq_acc[...] = jnp.zeros_like(dq_acc)
            for r0 in range(0, rows, tq):
                d = jnp.sum(do_ref[pl.ds(r0, tq), :] * o_ref[pl.ds(r0, tq), :], axis=-1, keepdims=True)
                delta_ref[pl.ds(r0, tq), :] = jnp.broadcast_to(d, (tq, LANES))

        kb, vv = k_ref[...], v_ref[...]

        def step(i, carry, diagonal):
            dk, dv = carry
            q0 = pl.multiple_of(i * tq, tq)
            qb = q_ref[pl.ds(q0, tq), :]
            dob = do_ref[pl.ds(q0, tq), :].astype(BF16)
            s = _dot_nt(qb, kb) * scale
            p = jnp.exp(s - lse_ref[pl.ds(q0, tq), :1])
            if diagonal:
                p = jnp.where(_causal(tq, tk), p, 0.0)
            dv = dv + _dot_tn(p.astype(BF16), dob)
            ds = (p * (_dot_nt(dob, vv) - delta_ref[pl.ds(q0, tq), :1])).astype(BF16)
            dk = dk + _dot_tn(ds, qb)
            dq_acc[pl.ds(q0, tq), :] += _dot_nn(ds, kb)
            return dk, dv

        zero = (jnp.zeros((tk, HEAD_SLOT), F32), jnp.zeros((tk, LANES), F32))
        dk, dv = lax.fori_loop(j + 1, nq, lambda i, carry: step(i, carry, False), step(j, zero, True))
        dkv_ref[:, :LANES] = (dk[:, :LANES] * scale).astype(dkv_ref.dtype)
        dkv_ref[:, LANES:] = dv.astype(dkv_ref.dtype)
        dkpe_ref[...] = dk[:, LANES:] * scale

        @pl.when(j == nq - 1)
        def _():
            for r0 in range(0, rows, tq):
                dq = dq_acc[pl.ds(r0, tq), :] * scale
                dq_ref[pl.ds(r0, tq), :LANES] = dq[:, :LANES].astype(dq_ref.dtype)
                dq_ref[pl.ds(r0, tq), LANES:] = _rope(dq[:, LANES:], cos_ref[pl.ds(r0, tq), :],
                                                      -sin_ref[pl.ds(r0, tq), :]).astype(dq_ref.dtype)

    full_q = pl.BlockSpec((rows, HEAD_SLOT), lambda h, j: (0, h))
    full_v = pl.BlockSpec((rows, LANES), lambda h, j: (0, h))
    tab = pl.BlockSpec((rows, LANES), lambda h, j: (0, 0))
    return pl.pallas_call(
        body, name="attn_bwd", grid=(heads, rows // tk),
        in_specs=[full_q, pl.BlockSpec((tk, HEAD_SLOT), lambda h, j: (j, h)), pl.BlockSpec((tk, LANES), lambda h, j: (j, h)),
                  full_v, full_v, full_v, tab, tab],
        out_specs=[full_q, pl.BlockSpec((tk, HEAD_SLOT), lambda h, j: (j, h)), pl.BlockSpec((tk, LANES), lambda h, j: (j, h))],
        out_shape=[jax.ShapeDtypeStruct((rows, heads * HEAD_SLOT), BF16), jax.ShapeDtypeStruct((rows, heads * HEAD_SLOT), BF16),
                   jax.ShapeDtypeStruct((rows, heads * LANES), F32)],
        scratch_shapes=[pltpu.VMEM((rows, HEAD_SLOT), F32), pltpu.VMEM((rows, LANES), F32)],
        compiler_params=_params(("parallel", "arbitrary")),
    )(qc, kc, vb, o, do, lse, cos, sin)


def _kpe_bwd(dkpe_heads, cos, sin, *, tr=512):
    rows = dkpe_heads.shape[0]
    heads = dkpe_heads.shape[1] // LANES
    tr = _tile(rows, tr, 2 * SUBLANES)

    def body(d_ref, cos_ref, sin_ref, o_ref):
        acc = d_ref[:, :LANES]
        for h in range(1, heads):
            acc = acc + d_ref[:, h * LANES:(h + 1) * LANES]
        o_ref[...] = _rope(acc, cos_ref[...], -sin_ref[...]).astype(o_ref.dtype)

    tab = pl.BlockSpec((tr, LANES), lambda i: (i, 0))
    return pl.pallas_call(
        body, name="kpe_bwd", grid=(rows // tr,),
        in_specs=[pl.BlockSpec((tr, heads * LANES), lambda i: (i, 0)), tab, tab], out_specs=tab,
        out_shape=jax.ShapeDtypeStruct((rows, LANES), BF16), compiler_params=_params(("parallel",)),
    )(dkpe_heads, cos, sin)


CONV_ROWS = 128


def _with_halo(ref, r0, ci, n_chunks, ch, lanes, before, after):
    parts = []
    if before:
        lo = pl.multiple_of(jnp.maximum(r0 - SUBLANES, 0), SUBLANES)
        parts.append(ref[pl.ds(lo, SUBLANES), lanes] * jnp.where(ci > 0, 1.0, 0.0))
    parts.append(ref[pl.ds(r0, ch), lanes])
    if after:
        hi = pl.multiple_of(jnp.minimum(r0 + ch, n_chunks * ch - SUBLANES), SUBLANES)
        parts.append(ref[pl.ds(hi, SUBLANES), lanes] * jnp.where(ci < n_chunks - 1, 1.0, 0.0))
    return jnp.concatenate(parts, axis=0)


def _taps(ext):
    return pltpu.roll(ext, 2, 0)[SUBLANES:], pltpu.roll(ext, 1, 0)[SUBLANES:], ext[SUBLANES:]


def _conv3(taps, w, b):
    return w[0:1, :] * taps[0] + w[1:2, :] * taps[1] + w[2:3, :] * taps[2] + b


def _conv_gate_fwd(a, conv_w, conv_b, *, tc=256):
    rows, f2 = a.shape
    f = f2 // 2
    tc = _tile(f, tc)
    nc = f // tc
    ch = _tile(rows, CONV_ROWS, SUBLANES)
    n_chunks = rows // ch

    def body(ag_ref, av_ref, wg_ref, wv_ref, bg_ref, bv_ref, o_ref):
        for lt in range(tc // LANES):
            lanes = slice(lt * LANES, (lt + 1) * LANES)
            wg, wv, bg, bv = wg_ref[:, lanes], wv_ref[:, lanes], bg_ref[:, lanes], bv_ref[:, lanes]

            def chunk(ci, carry):
                r0 = pl.multiple_of(ci * ch, ch)
                gate = _conv3(_taps(_with_halo(ag_ref, r0, ci, n_chunks, ch, lanes, True, False)), wg, bg)
                val = _conv3(_taps(_with_halo(av_ref, r0, ci, n_chunks, ch, lanes, True, False)), wv, bv)
                o_ref[pl.ds(r0, ch), lanes] = (gate * jax.nn.sigmoid(gate) * val).astype(o_ref.dtype)
                return carry

            lax.fori_loop(0, n_chunks, chunk, 0)

    return pl.pallas_call(
        body, name="conv_gate_fwd", grid=(nc,),
        in_specs=[pl.BlockSpec((rows, tc), lambda j: (0, j)), pl.BlockSpec((rows, tc), lambda j: (0, j + nc)),
                  pl.BlockSpec((SUBLANES, tc), lambda j: (0, j)), pl.BlockSpec((SUBLANES, tc), lambda j: (0, j + nc)),
                  pl.BlockSpec((1, tc), lambda j: (0, j)), pl.BlockSpec((1, tc), lambda j: (0, j + nc))],
        out_specs=pl.BlockSpec((rows, tc), lambda j: (0, j)),
        out_shape=jax.ShapeDtypeStruct((rows, f), BF16), compiler_params=_params(("parallel",)),
    )(a, a, conv_w, conv_w, conv_b, conv_b)


def _conv_gate_bwd(a, conv_w, conv_b, dg, *, tc=256):
    rows, f2 = a.shape
    f = f2 // 2
    tc = _tile(f, tc)
    nc = f // tc
    ch = _tile(rows, CONV_ROWS, SUBLANES)
    n_chunks = rows // ch
    ext_rows = ch + SUBLANES

    def fold(x):
        return jnp.sum(x.reshape(ch // SUBLANES, SUBLANES, LANES), axis=0)

    def body(ag_ref, av_ref, wg_ref, wv_ref, bg_ref, bv_ref, dg_ref, da_ref, dw_ref, db_ref):
        for lt in range(tc // LANES):
            lanes = slice(lt * LANES, (lt + 1) * LANES)
            wg, wv, bg, bv = wg_ref[:, lanes], wv_ref[:, lanes], bg_ref[:, lanes], bv_ref[:, lanes]

            def chunk(ci, acc):
                r0 = pl.multiple_of(ci * ch, ch)
                taps_g = _taps(_with_halo(ag_ref, r0, ci, n_chunks, ch, lanes, True, True))
                taps_v = _taps(_with_halo(av_ref, r0, ci, n_chunks, ch, lanes, True, True))
                dge = _with_halo(dg_ref, r0, ci, n_chunks, ch, lanes, False, True)
                gate, val = _conv3(taps_g, wg, bg), _conv3(taps_v, wv, bv)
                sg = jax.nn.sigmoid(gate)
                d_gate = dge * val * sg * (1.0 + gate * (1.0 - sg))
                d_val = dge * gate * sg
                new = []
                for half, (taps, w, d) in enumerate(((taps_g, wg, d_gate), (taps_v, wv, d_val))):
                    da = (w[2:3, :] * d[:ch] + w[1:2, :] * pltpu.roll(d, ext_rows - 1, 0)[:ch]
                          + w[0:1, :] * pltpu.roll(d, ext_rows - 2, 0)[:ch])
                    da_ref[half, pl.ds(r0, ch), lanes] = da.astype(da_ref.dtype)
                    dc = d[:ch]
                    sums = [fold(dc)] + [fold(dc * t[:ch]) for t in taps]
                    new.append(tuple(x + s for x, s in zip(acc[half], sums)))
                return tuple(new)

            zero = tuple(jnp.zeros((SUBLANES, LANES), F32) for _ in range(4))
            acc = lax.fori_loop(0, n_chunks, chunk, (zero, zero))
            row = lax.broadcasted_iota(jnp.int32, (SUBLANES, LANES), 0)
            for half in range(2):
                db, *taps = (jnp.sum(x, axis=0, keepdims=True) for x in acc[half])
                db_ref[half, :, lanes] = db
                dw = jnp.zeros((SUBLANES, LANES), F32)
                for tap in range(3):
                    dw = jnp.where(row == tap, taps[tap], dw)
                dw_ref[half, :, lanes] = dw

    lo = lambda j: (0, j)
    hi = lambda j: (0, j + nc)
    both = lambda j: (0, 0, j)
    return pl.pallas_call(
        body, name="conv_gate_bwd", grid=(nc,),
        in_specs=[pl.BlockSpec((rows, tc), lo), pl.BlockSpec((rows, tc), hi), pl.BlockSpec((SUBLANES, tc), lo),
                  pl.BlockSpec((SUBLANES, tc), hi), pl.BlockSpec((1, tc), lo), pl.BlockSpec((1, tc), hi),
                  pl.BlockSpec((rows, tc), lo)],
        out_specs=[pl.BlockSpec((2, rows, tc), both), pl.BlockSpec((2, SUBLANES, tc), both), pl.BlockSpec((2, 1, tc), both)],
        out_shape=[jax.ShapeDtypeStruct((2, rows, f), BF16), jax.ShapeDtypeStruct((2, SUBLANES, f), F32),
                   jax.ShapeDtypeStruct((2, 1, f), F32)],
        compiler_params=_params(("parallel",)),
    )(a, a, conv_w, conv_w, conv_b, conv_b, dg)


def _wgrad(a, b, rows, cols, row_sharded, name, **kw):
    return functools.partial(_wgrad_half, a, b, rows, cols, row_sharded, name, **kw)


class _NoExchange:
    def __init__(self, later, ffn):
        self.later, self.ffn = later, ffn

    def mixer_weights(self, after):
        return self.later

    def ffn_weights_arrived(self, after):
        return None

    def ffn_weights(self, after):
        return self.ffn

    def ffn_down_weight(self, after):
        return self.ffn["ffn_w_down"]

    def ffn_grads(self, makers, after):
        self.ffn_makers = makers
        return None

    def ffn_backward_done(self, after):
        return None


def _local_step(x, posf, target, w, hooks):
    rows, d = x.shape
    width = w["ssm_d"].shape[1]
    qr, kvr = w["mla_q_norm_w"].shape[1], w["mla_kv_norm_w"].shape[1]
    heads = w["mla_w_ukv"].shape[1] // HEAD_SLOT
    f2 = w["ffn_conv_b"].shape[1]
    inp = w["w_in"].shape[0]
    groups = width // SSM_GROUP
    nb = groups // GROUPS_PER_BATCH
    scale = (QK_NOPE_DIM + QK_ROPE_DIM) ** -0.5
    g = {}

    hn = _rmsnorm_fwd(x, w["attn_norm_w"], name="attn_norm")
    proj = _matmul(hn, w["w_in"], mode="nt", name="in_proj")

    s5_weights = (w["ssm_lambda_re"], w["ssm_lambda_im"], w["ssm_log_dt"], w["ssm_b_re"], w["ssm_b_im"])
    wb, wct, abar = _s5_bands(*s5_weights, w["ssm_c_re"], w["ssm_c_im"])
    states, y_pre, yg = _s5_fwd(proj, wb, wct, w["ssm_d"], abar)
    later = hooks.mixer_weights(yg)
    z = _matmul(yg, later["ssm_w_glu"], mode="nn", name="glu_proj", bias=w["ssm_b_glu"])
    ys = _glu_norm_fwd(y_pre, z, w["ssm_out_norm_w"])

    q_col, kv_col, kpe_col = width // qr, (width + qr) // kvr, (width + qr + kvr) // LANES
    assert width % qr == 0 and (width + qr) % kvr == 0
    qn = _rmsnorm_fwd(proj, w["mla_q_norm_w"], name="q_norm", width=qr, col=q_col)
    kvn = _rmsnorm_fwd(proj, w["mla_kv_norm_w"], name="kv_norm", width=kvr, col=kv_col)
    q = _matmul(qn, w["mla_w_uq"], mode="nn", name="q_proj")
    kv = _matmul(kvn, w["mla_w_ukv"], mode="nn", name="kv_proj")
    half = QK_ROPE_DIM // 2
    inv_freq = ROPE_THETA ** (-jnp.arange(0, QK_ROPE_DIM, 2, dtype=F32) / QK_ROPE_DIM)
    zeros = jnp.zeros((LANES - QK_ROPE_DIM,), F32)
    freq = jnp.concatenate([inv_freq, inv_freq, zeros]).reshape(1, LANES)
    sign = jnp.concatenate([-jnp.ones((half,), F32), jnp.ones((half,), F32), zeros]).reshape(1, LANES)
    cos, sin = _rope_tables(posf, freq, sign)
    qc, kc, vb = _attn_prep(q, kv, proj, kpe_col, cos, sin)
    o, lse = _attn_fwd(qc, kc, vb, scale=scale, tq=ATTN_BLOCK)
    ym = _rmsnorm_fwd(o, w["mla_out_norm_w"], name="mla_out_norm")
    ycat = jnp.concatenate([ys, ym], axis=1)
    h1 = _matmul(ycat, later["w_out"], mode="nn", name="out_proj", add=x, after=hooks.ffn_weights_arrived(ycat))

    hn2 = _rmsnorm_fwd(h1, w["ffn_norm_w"], name="ffn_norm")
    ffn = hooks.ffn_weights(hn2)
    a = _matmul(hn2, ffn["ffn_w_up"], mode="nn", name="ffn_up", tm=FFN_ROWS, after=ffn.get("started"))
    gated = _conv_gate_fwd(a, ffn["ffn_conv_w"], w["ffn_conv_b"])
    w_down = hooks.ffn_down_weight(gated)
    h2 = _matmul(gated, w_down, mode="nn", name="ffn_down", add=h1, tk=2816, tm=FFN_ROWS)
    loss_tile, dh2, dh2_mxu, g["final_norm_w"] = _final_norm_loss(h2, w["final_norm_w"], target)

    dgated = _matmul(dh2_mxu, w_down, mode="nt", name="ffn_down_dx", tm=FFN_ROWS)
    da, dcw, dcb = _conv_gate_bwd(a, ffn["ffn_conv_w"], w["ffn_conv_b"], dgated)
    g["ffn_conv_w"] = jnp.concatenate([dcw[0, :3], dcw[1, :3]], axis=1)
    g["ffn_conv_b"] = jnp.concatenate([dcb[0], dcb[1]], axis=1)
    started = hooks.ffn_grads({
        "ffn_w_up": _wgrad(hn2, da, d, f2, False, "ffn_up_dw", b_split=True, tn=_tile(f2 // N_CHIPS, 1408)),
        "ffn_w_down": _wgrad(gated, dh2_mxu, f2 // 2, d, True, "ffn_down_dw", tm=f2 // 2 // N_CHIPS, tn=512)}, dcb)
    dhn2 = _matmul(da, ffn["ffn_w_up"], mode="nt", name="ffn_up_dx", a_split=True, tk=_tile(f2 // 2, 2816), tm=FFN_ROWS,
                   after=started)
    dh1, dh1_mxu, g["ffn_norm_w"] = _rmsnorm_bwd(h1, w["ffn_norm_w"], dhn2, name="ffn_norm_bwd", add=dh2,
                                                dx_dtypes=(F32, BF16))

    dycat = _matmul(dh1_mxu, later["w_out"], mode="nt", name="out_proj_dx")
    g["w_out"] = _wgrad(ycat, dh1_mxu, 2 * width, d, True, "out_proj_dw")
    started = hooks.ffn_backward_done(dycat)
    mla_out_norm_w, ssm_out_norm_w = w["mla_out_norm_w"], w["ssm_out_norm_w"]
    if started is not None:
        mla_out_norm_w, ssm_out_norm_w = mla_out_norm_w + started[:1, :1], ssm_out_norm_w + started[:1, :1]

    do, g["mla_out_norm_w"] = _rmsnorm_bwd(o, mla_out_norm_w, dycat, name="mla_out_norm_bwd", width=width, dy_col=1)
    dq, dkv, dkpe_heads = _attn_bwd(qc, kc, vb, o, do, lse, cos, sin, scale=scale, tk=ATTN_BLOCK)
    dkpe = _kpe_bwd(dkpe_heads, cos, sin)
    g["mla_w_uq"] = _wgrad(qn, dq, qr, heads * HEAD_SLOT, False, "q_proj_dw")
    dqn = _matmul(dq, w["mla_w_uq"], mode="nt", name="q_proj_dx")
    dcq, g["mla_q_norm_w"] = _rmsnorm_bwd(proj, w["mla_q_norm_w"], dqn, name="q_norm_bwd", width=qr, col=q_col,
                                          dx_dtypes=(BF16,))
    g["mla_w_ukv"] = _wgrad(kvn, dkv, kvr, heads * HEAD_SLOT, False, "kv_proj_dw")
    dkvn = _matmul(dkv, w["mla_w_ukv"], mode="nt", name="kv_proj_dx")
    dckv, g["mla_kv_norm_w"] = _rmsnorm_bwd(proj, w["mla_kv_norm_w"], dkvn, name="kv_norm_bwd", width=kvr, col=kv_col,
                                            dx_dtypes=(BF16,))

    dz, dyg_a, g["ssm_out_norm_w"], g["ssm_b_glu"] = _glu_norm_bwd(y_pre, z, ssm_out_norm_w, dycat)
    dyg_b = _matmul(dz, later["ssm_w_glu"], mode="nt", name="glu_proj_dx")
    g["ssm_w_glu"] = _wgrad(yg, dz, width, width, True, "glu_proj_dw")
    du, dwb, dwct, dabar, g["ssm_d"] = _s5_bwd(proj, states, y_pre, dyg_a, dyg_b, wb, wct, w["ssm_d"], abar)
    (g["ssm_lambda_re"], g["ssm_lambda_im"], g["ssm_log_dt"], g["ssm_b_re"], g["ssm_b_im"], g["ssm_c_re"],
     g["ssm_c_im"]) = _s5_bands_bwd(*s5_weights, dwb, dwct, dabar)

    pad = jnp.zeros((rows, inp - (width + qr + kvr + LANES)), BF16)
    dproj = jnp.concatenate([du, dcq, dckv, dkpe, pad], axis=1)
    g["w_in"] = _wgrad(dproj, hn, inp, d, False, "in_proj_dw")
    dhn = _matmul(dproj, w["w_in"], mode="nn", name="in_proj_dx")
    dx, g["attn_norm_w"] = _rmsnorm_bwd(x, w["attn_norm_w"], dhn, name="attn_norm_bwd", add=dh1)
    return loss_tile, dx, g


ANY = pl.BlockSpec(memory_space=pl.ANY)
MESH = pl.DeviceIdType.MESH


def _mesh_pos():
    return lax.axis_index("x"), lax.axis_index("y"), lax.axis_index("c")


def _other_chips(x, y):
    return [(1 - x, y), (x, 1 - y), (1 - x, 1 - y)]


def _remote(src, dst, send_sems, recv_sems, k, to):
    return pltpu.make_async_remote_copy(src_ref=src, dst_ref=dst, send_sem=send_sems.at[k], recv_sem=recv_sems.at[k],
                                        device_id=to, device_id_type=MESH)


def _place_shard(shard, piece_idx, row_sharded, name, out_dtype=BF16, pieces=N_CHIPS, after=None):
    rs, cs = shard.shape
    tr = _tile(rs, 256, 2 * SUBLANES)
    rb = rs // tr
    extra = [] if after is None else [after]

    def body(p_ref, x_ref, *rest):
        o_ref = rest[-1]
        o_ref[...] = x_ref[...].astype(o_ref.dtype)

    if row_sharded:
        out_shape, out_map = (pieces * rs, cs), (lambda i, p_ref: (p_ref[0] * rb + i, 0))
    else:
        out_shape, out_map = (rs, pieces * cs), (lambda i, p_ref: (i, p_ref[0]))
    return pl.pallas_call(
        body, name=name, out_shape=jax.ShapeDtypeStruct(out_shape, out_dtype),
        grid_spec=pltpu.PrefetchScalarGridSpec(
            num_scalar_prefetch=1, grid=(rb,),
            in_specs=[pl.BlockSpec((tr, cs), lambda i, p_ref: (i, 0))] + [pl.BlockSpec(memory_space=pl.ANY)] * len(extra),
            out_specs=pl.BlockSpec((tr, cs), out_map)),
        compiler_params=_params(("parallel",)),
    )(piece_idx, shard, *extra)


def _gather_weights(placed, name):
    n = len(placed)
    meta = [(row_sharded, direct) for _, row_sharded, direct in placed]
    over_ici, over_d2d = _gather_plans(meta)
    forwarded = [t for t, (_, direct) in enumerate(meta) if not direct]

    def body(*refs):
        outs = refs[n:2 * n]
        send_sems, recv_sems, pass_send_sems, pass_recv_sems = refs[2 * n:]
        first, arrivals = over_ici(outs, send_sems, recv_sems)
        passed, passed_arrivals = over_d2d([outs[t] for t in forwarded], pass_send_sems, pass_recv_sems)
        for cp in first:
            cp.start()
        for t in range(n):
            for j in range(3):
                arrivals[3 * t + j].wait_recv()
                if t in forwarded:
                    passed[3 * forwarded.index(t) + j].start()
        for cp in passed_arrivals:
            cp.wait_recv()
        for cp in first + passed:
            cp.wait_send()

    return pl.pallas_call(
        body, name=name, in_specs=[ANY] * n, out_specs=[ANY] * n,
        out_shape=[jax.ShapeDtypeStruct(arr.shape, arr.dtype) for arr, _, _ in placed],
        input_output_aliases={t: t for t in range(n)},
        scratch_shapes=[pltpu.SemaphoreType.DMA((3 * n,)), pltpu.SemaphoreType.DMA((3 * n,)),
                        pltpu.SemaphoreType.DMA((3 * len(forwarded),)), pltpu.SemaphoreType.DMA((3 * len(forwarded),))],
    )(*[arr for arr, _, _ in placed])


def _gather_plans(meta):
    def window(ref, row_sharded, piece, half):
        r, cc = ref.shape
        if row_sharded:
            rs = r // N_CHIPS
            if half is None:
                return ref.at[pl.ds(piece * rs, rs), :]
            return ref.at[pl.ds(piece * rs + half * (rs // 2), rs // 2), :]
        cs = cc // N_CHIPS
        if half is None:
            return ref.at[:, pl.ds(piece * cs, cs)]
        return ref.at[pl.ds(half * (r // 2), r // 2), pl.ds(piece * cs, cs)]

    def over_ici(refs, send_sems, recv_sems):
        x, y, c = _mesh_pos()
        sends, recvs = [], []
        for t, (row_sharded, direct) in enumerate(meta):
            mine = window(refs[t], row_sharded, 2 * x + y, None if direct else c)
            for j, (px, py) in enumerate(_other_chips(x, y)):
                theirs = window(refs[t], row_sharded, 2 * px + py, None if direct else c)
                sends.append(_remote(mine, mine, send_sems, recv_sems, 3 * t + j, (px, py, c)))
                recvs.append(_remote(theirs, theirs, send_sems, recv_sems, 3 * t + j, (px, py, c)))
        return sends, recvs

    def over_d2d(refs, send_sems, recv_sems):
        x, y, c = _mesh_pos()
        sends, recvs = [], []
        rows = [row_sharded for row_sharded, direct in meta if not direct]
        for t, row_sharded in enumerate(rows):
            for j, (px, py) in enumerate(_other_chips(x, y)):
                got = window(refs[t], row_sharded, 2 * px + py, c)
                other = window(refs[t], row_sharded, 2 * px + py, 1 - c)
                sends.append(_remote(got, got, send_sems, recv_sems, 3 * t + j, (x, y, 1 - c)))
                recvs.append(_remote(other, other, send_sems, recv_sems, 3 * t + j, (x, y, 1 - c)))
        return sends, recvs

    return over_ici, over_d2d


HBM = pl.BlockSpec(memory_space=pltpu.HBM)
SEMAPHORES = pl.BlockSpec(memory_space=pltpu.SEMAPHORE)
DATAFLOW = pltpu.SideEffectType.DATAFLOW_SIDE_EFFECTING


def _start_copies(name, arrays, plan, n_copies, after):
    n = len(arrays)

    def body(*refs):
        sends, _ = plan(refs[:n], refs[n + 1], refs[n + 2])
        for cp in sends:
            cp.start()
        token = refs[2 * n + 3]
        token[...] = jnp.zeros_like(token)

    out = pl.pallas_call(
        body, name=name,
        out_shape=(pltpu.SemaphoreType.DMA((n_copies,)), pltpu.SemaphoreType.DMA((n_copies,)),
                   *[pltpu.HBM(a.shape, a.dtype) for a in arrays], jax.ShapeDtypeStruct((SUBLANES, LANES), F32)),
        in_specs=[HBM] * n + [ANY],
        out_specs=(SEMAPHORES, SEMAPHORES, *[HBM] * n, pl.BlockSpec(memory_space=pltpu.VMEM)),
        input_output_aliases={t: t + 2 for t in range(n)},
        compiler_params=pltpu.CompilerParams(has_side_effects=DATAFLOW),
    )(*[pltpu.with_memory_space_constraint(a, pltpu.HBM) for a in arrays], after)
    return out[0], out[1], list(out[2:2 + n]), out[2 + n]


def _wait_copies(name, started, plan, after):
    send_sems, recv_sems, arrays, _ = started
    n = len(arrays)

    def body(*refs):
        sends, recvs = plan(refs[:n], refs[n], refs[n + 1])
        for cp in sends:
            cp.wait_send()
        for cp in recvs:
            cp.wait_recv()

    out = pl.pallas_call(
        body, name=name, out_shape=[pltpu.HBM(a.shape, a.dtype) for a in arrays],
        in_specs=[HBM] * n + [SEMAPHORES, SEMAPHORES, ANY], out_specs=[HBM] * n,
        input_output_aliases={t: t for t in range(n)},
        compiler_params=pltpu.CompilerParams(has_side_effects=DATAFLOW),
    )(*arrays, send_sems, recv_sems, after)
    return list(out)


def _exchange(name, arrays, out_shapes, plan, n_copies, in_place=False, after=None):
    n = len(arrays)
    extra = [] if after is None else [after]

    def body(*refs):
        ins, outs = refs[:n], refs[n + len(extra):n + len(extra) + len(out_shapes)]
        send_sems, recv_sems = refs[n + len(extra) + len(out_shapes):]
        sends, recvs = plan(ins, outs, send_sems, recv_sems)
        for cp in sends:
            cp.start()
        for cp in recvs:
            cp.wait_recv()
        for cp in sends:
            cp.wait_send()

    return pl.pallas_call(
        body, name=name, in_specs=[ANY] * (n + len(extra)), out_specs=[ANY] * len(out_shapes), out_shape=out_shapes,
        input_output_aliases={t: t for t in range(n)} if in_place else {},
        scratch_shapes=[pltpu.SemaphoreType.DMA((n_copies,)), pltpu.SemaphoreType.DMA((n_copies,))],
    )(*arrays, *extra)


def _give_plan(n):
    def plan(refs, send_sems, recv_sems):
        x, y, c = _mesh_pos()
        sends = [_remote(refs[t], refs[n + t], send_sems, recv_sems, t, (x, y, 1 - c)) for t in range(n)]
        return sends, sends

    return plan


def _scatter_plan(n):
    def plan(refs, send_sems, recv_sems):
        x, y, c = _mesh_pos()
        sends = []
        for t in range(n):
            for j, (px, py) in enumerate(_other_chips(x, y)):
                sends.append(_remote(refs[t].at[2 * px + py], refs[n + t].at[j], send_sems, recv_sems, 3 * t + j, (px, py, c)))
        return sends, sends

    return plan


def _scatter_shapes(sums):
    return [jax.ShapeDtypeStruct((3,) + s.shape[1:], s.dtype) for s in sums]


def _join_plan(n):
    def plan(refs, send_sems, recv_sems):
        x, y, c = _mesh_pos()
        sends = [_remote(refs[t].at[c], refs[t].at[c], send_sems, recv_sems, t, (x, y, 1 - c)) for t in range(n)]
        recvs = [_remote(refs[t].at[1 - c], refs[t].at[1 - c], send_sems, recv_sems, t, (x, y, 1 - c)) for t in range(n)]
        return sends, recvs

    return plan


def _join_halves(halves, name, after=None):
    plan = _join_plan(len(halves))
    shapes = [jax.ShapeDtypeStruct(h.shape, h.dtype) for h in halves]
    return _exchange(name, halves, shapes, lambda ins, outs, s, r: plan(outs, s, r), len(halves), in_place=True, after=after)


def _add_other_half(g4, got, where, name, wire_dtype=BF16):
    _, pieces, sr, sc = g4.shape
    tr = _tile(sr, 256, 2 * SUBLANES)

    def body(w_ref, a_ref, b_ref, o_ref):
        o_ref[...] = (a_ref[...] + b_ref[...]).astype(o_ref.dtype)

    blk = pl.BlockSpec((None, tr, sc), lambda p, i, w_ref: (p, i, 0))
    return pl.pallas_call(
        body, name=name, out_shape=jax.ShapeDtypeStruct((pieces, sr, sc), wire_dtype),
        grid_spec=pltpu.PrefetchScalarGridSpec(
            num_scalar_prefetch=1, grid=(pieces, sr // tr),
            in_specs=[pl.BlockSpec((None, None, tr, sc), lambda p, i, w_ref: (w_ref[0], p, i, 0)), blk], out_specs=blk),
        compiler_params=_params(("parallel", "parallel")),
    )(where, g4, got)


def _add_pieces(sums, got_pieces, where, name):
    _, sr, sc = sums.shape
    tr = _tile(sr, 256, 2 * SUBLANES)

    def body(w_ref, a_ref, r_ref, o_ref):
        acc = a_ref[...]
        for j in range(3):
            acc = acc + r_ref[j].astype(F32)
        o_ref[...] = acc

    return pl.pallas_call(
        body, name=name, out_shape=jax.ShapeDtypeStruct((N_CORES, sr, sc), F32),
        grid_spec=pltpu.PrefetchScalarGridSpec(
            num_scalar_prefetch=1, grid=(sr // tr,),
            in_specs=[pl.BlockSpec((None, tr, sc), lambda i, w_ref: (w_ref[1], i, 0)),
                      pl.BlockSpec((3, tr, sc), lambda i, w_ref: (0, i, 0))],
            out_specs=pl.BlockSpec((None, tr, sc), lambda i, w_ref: (w_ref[0], i, 0))),
        compiler_params=_params(("parallel",)),
    )(where, sums, got_pieces)


def _adamw_update(w, g, m, v):
    nm = ADAM_B1 * m + (1.0 - ADAM_B1) * g
    nv = ADAM_B2 * v + (1.0 - ADAM_B2) * (g * g)
    m_hat = nm / (1.0 - ADAM_B1 ** ADAM_STEP)
    v_hat = nv / (1.0 - ADAM_B2 ** ADAM_STEP)
    return -ADAM_LR * (m_hat / (jnp.sqrt(v_hat) + ADAM_EPS) + ADAM_WD * w), nm, nv


def _adamw(w, g, m, v, name, after=None):
    rows, cols = w.shape
    halves = 2 if g.ndim == 3 else 1
    bc = cols // halves
    tr = _tile(rows, max(SUBLANES, (1 << 19) // max(bc, 1) // SUBLANES * SUBLANES), SUBLANES)

    def body(w_ref, g_ref, m_ref, v_ref, *rest):
        d_ref, nm_ref, nv_ref, go_ref = rest[-4:]
        gv = g_ref[...]
        d_ref[...], nm_ref[...], nv_ref[...] = _adamw_update(w_ref[...], gv, m_ref[...], v_ref[...])
        go_ref[...] = gv

    blk = pl.BlockSpec((tr, bc), lambda i, h: (i, h))
    g_blk = pl.BlockSpec((None, tr, bc), lambda i, h: (h, i, 0)) if halves == 2 else blk
    extra = [] if after is None else [after]
    return pl.pallas_call(
        body, name=name, grid=(rows // tr, halves),
        in_specs=[blk, g_blk, blk, blk] + [pl.BlockSpec(memory_space=pl.ANY)] * len(extra), out_specs=[blk] * 4,
        out_shape=[jax.ShapeDtypeStruct((rows, cols), F32)] * 4, compiler_params=_params(("parallel", "parallel")),
    )(w, g, m, v, *extra)


def _adamw_many(ws, gs, ms, vs, name):
    n = len(ws)

    def body(*refs):
        outs = refs[4 * n:]
        for k in range(n):
            w_ref, g_ref, m_ref, v_ref = (refs[j * n + k] for j in range(4))
            outs[k][...], outs[n + k][...], outs[2 * n + k][...] = _adamw_update(w_ref[...], g_ref[...], m_ref[...], v_ref[...])

    out = pl.pallas_call(
        body, name=name, out_shape=[jax.ShapeDtypeStruct(w.shape, F32) for w in ws] * 3,
        compiler_params=pltpu.CompilerParams(vmem_limit_bytes=VMEM_LIMIT_BYTES),
    )(*ws, *gs, *ms, *vs)
    return out[:n], out[n:2 * n], out[2 * n:]


WEIGHTS = ['attn_norm_w', 'w_in', 'ssm_lambda_re', 'ssm_lambda_im', 'ssm_log_dt', 'ssm_b_re', 'ssm_b_im', 'ssm_c_re',
           'ssm_c_im', 'ssm_d', 'ssm_w_glu', 'ssm_b_glu', 'mla_q_norm_w', 'mla_w_uq', 'mla_kv_norm_w', 'mla_w_ukv',
           'ssm_out_norm_w', 'mla_out_norm_w', 'w_out', 'ffn_norm_w', 'ffn_w_up', 'ffn_conv_w', 'ffn_conv_b',
           'ffn_w_down', 'final_norm_w']
SHARDED = {'w_in': False, 'ssm_w_glu': True, 'mla_w_uq': False, 'mla_w_ukv': False, 'w_out': True, 'ffn_w_up': False,
           'ffn_w_down': True}
SMALL = [n for n in WEIGHTS if n not in SHARDED and n != 'ffn_conv_w']
ROPE_PAD = HEAD_SLOT - QK_NOPE_DIM - QK_ROPE_DIM
SMALL_COLS = 8 * LANES


def _pad_heads(w_uq, heads):
    qr = w_uq.shape[0]
    w3 = w_uq.reshape(qr, heads, QK_NOPE_DIM + QK_ROPE_DIM)
    return jnp.concatenate([w3, jnp.zeros((qr, heads, ROPE_PAD), w_uq.dtype)], axis=2).reshape(qr, heads * HEAD_SLOT)


def _unpad_heads(g_uq, heads):
    qr = g_uq.shape[0]
    return g_uq.reshape(qr, heads, HEAD_SLOT)[:, :, :QK_NOPE_DIM + QK_ROPE_DIM].reshape(qr, -1)


FFN = ['ffn_w_up', 'ffn_w_down']
MIXER_LATER = ['ssm_w_glu', 'w_out']
FFN_GATHER = FFN + ['ffn_conv_w']
FFN_GATHER_META = [(SHARDED[n], False) for n in FFN] + [(False, True)]


class _Overlapped:
    def __init__(self, placed_first, first_sharding, where):
        self.where, self.mine, self.other = where, where[:1], 1 - where[:1]
        self.first_ici, self.first_d2d = _gather_plans([(r, False) for r in first_sharding])
        self.first = _start_copies("gather_first_start", placed_first, self.first_ici, 3 * len(placed_first), where)
        self.first_started = self.first[3]

    def start_rest(self, placed_later, placed):
        self.later_ici, self.later_d2d = _gather_plans([(SHARDED[n], False) for n in MIXER_LATER])
        self.later = _start_copies("gather_later_start", placed_later, self.later_ici, 3 * len(placed_later),
                                   self.first_started)
        up, down, taps = placed
        self.up_ici, self.up_d2d = _gather_plans([(SHARDED["ffn_w_up"], False)])
        self.up = _start_copies("gather_ffn_up_start", [up], self.up_ici, 3, self.later[3])
        self.down_ici, self.down_d2d = _gather_plans([(SHARDED["ffn_w_down"], False), (False, True)])
        self.down = _start_copies("gather_ffn_down_start", [down, taps], self.down_ici, 6, self.up[3])
        self.gather_started = self.down[3]
        arrived = _wait_copies("gather_first_wait", self.first, self.first_ici, self.gather_started)
        shapes = [jax.ShapeDtypeStruct(a.shape, a.dtype) for a in arrived]
        return _exchange("gather_first_pass", arrived, shapes, lambda ins, outs, s, r: self.first_d2d(outs, s, r),
                         3 * len(arrived), in_place=True)

    def mixer_weights(self, after):
        arrived = _wait_copies("gather_later_wait", self.later, self.later_ici, after)
        shapes = [jax.ShapeDtypeStruct(a.shape, a.dtype) for a in arrived]
        passed = _exchange("gather_later_pass", arrived, shapes, lambda ins, outs, s, r: self.later_d2d(outs, s, r),
                           3 * len(arrived), in_place=True)
        return dict(zip(MIXER_LATER, passed))

    def ffn_weights_arrived(self, after):
        arrived = _wait_copies("gather_ffn_up_wait", self.up, self.up_ici, after)
        self.up_passing = _start_copies("gather_ffn_up_pass_start", arrived, self.up_d2d, 3, after)
        return self.up_passing[3]

    def ffn_weights(self, after):
        w_up, = _wait_copies("gather_ffn_up_pass_wait", self.up_passing, self.up_d2d, after)
        down, taps = _wait_copies("gather_ffn_down_wait", self.down, self.down_ici, after)
        self.down_passing = _start_copies("gather_ffn_down_pass_start", [down], self.down_d2d, 3, w_up)
        return {"ffn_w_up": w_up, "ffn_conv_w": taps, "started": self.down_passing[3]}

    def ffn_down_weight(self, after):
        return _wait_copies("gather_ffn_down_pass_wait", self.down_passing, self.down_d2d, after)[0]

    def ffn_grads(self, makers, after):
        self.makers = [makers[name] for name in FFN]
        n = len(FFN)
        give = [make(self.other, suffix="_give") for make in self.makers]
        lands = [lax.empty(g.shape, g.dtype) for g in give]
        self.swap = _start_copies("grad_ffn_swap_start", give + lands, _give_plan(n), n, after)
        return self.swap[3]

    def ffn_backward_done(self, after):
        n = len(FFN)
        got = _wait_copies("grad_ffn_swap_wait", self.swap, _give_plan(n), after)[n:]
        kept = [make(self.mine, suffix="_keep", add=got[t], wire=True) for t, make in enumerate(self.makers)]
        self.sums = [k[0] for k in kept]
        wires = [k[1] for k in kept]
        lands = [lax.empty(s.shape, s.dtype) for s in _scatter_shapes(wires)]
        self.scatter = _start_copies("grad_ffn_scatter_start", wires + lands, _scatter_plan(n), 3 * n, after)
        return self.scatter[3]

    def ffn_reduced(self, after):
        n = len(FFN)
        got_pieces = _wait_copies("grad_ffn_scatter_wait", self.scatter, _scatter_plan(n), after)[n:]
        return [_add_pieces(self.sums[t], got_pieces[t], self.where, "grad_add_pieces_" + name) for t, name in enumerate(FFN)]


def _step(args):
    x, positions, target = args["x"][0], args["positions"], args["loss_target"][0]
    rows = x.shape[0]
    p = {n: args[n] for n in WEIGHTS}
    xi, yi, ci = _mesh_pos()
    piece = 2 * xi + yi

    def transposed(a):
        return jnp.swapaxes(a[0], 0, 1)

    def as_stored(n, a):
        return jnp.swapaxes(a, 2, 3) if n in ("ssm_b_re", "ssm_b_im") else a

    w_in = transposed(p["w_in"])
    in_width = w_in.shape[0]
    in_pad = (-in_width) % (2 * LANES)
    heads_here = p["mla_w_uq"].shape[2] // (QK_NOPE_DIM + QK_ROPE_DIM)
    shards = {
        "w_in": jnp.pad(w_in, ((0, in_pad), (0, 0))),
        "ssm_w_glu": p["ssm_w_glu"][0],
        "mla_w_uq": _pad_heads(p["mla_w_uq"][0], heads_here),
        "mla_w_ukv": p["mla_w_ukv"][0],
        "w_out": p["w_out"][0],
        "ffn_w_up": p["ffn_w_up"][0],
        "ffn_w_down": p["ffn_w_down"][0],
    }
    conv_w = jnp.pad(p["ffn_conv_w"][0], ((0, SUBLANES - p["ffn_conv_w"].shape[1]), (0, 0)))
    order = list(SHARDED)
    piece_idx = piece.reshape(1).astype(jnp.int32)
    mixer = [n for n in order if n not in FFN]
    first = [n for n in mixer if n not in MIXER_LATER]
    where = jnp.stack([ci, piece]).astype(jnp.int32)
    placed = {n: _place_shard(shards[n], piece_idx, SHARDED[n], "place_" + n) for n in first}
    hooks = _Overlapped([placed[n] for n in first], [SHARDED[n] for n in first], where)
    for n in order:
        if n not in first:
            placed[n] = _place_shard(shards[n], piece_idx, SHARDED[n], "place_" + n, after=hooks.first_started)
    placed["ffn_conv_w"] = _place_shard(conv_w, piece_idx, False, "place_ffn_conv_w", out_dtype=F32,
                                        after=hooks.first_started)
    w = dict(zip(first, hooks.start_rest([placed[n] for n in MIXER_LATER], [placed[n] for n in FFN_GATHER])))
    groups = p["ssm_lambda_re"].shape[1]
    w.update({
        "attn_norm_w": p["attn_norm_w"] + hooks.gather_started[:1, :1],
        "ssm_lambda_re": p["ssm_lambda_re"][0], "ssm_lambda_im": p["ssm_lambda_im"][0],
        "ssm_log_dt": p["ssm_log_dt"].reshape(groups, 1), "ssm_b_re": as_stored("ssm_b_re", p["ssm_b_re"])[0],
        "ssm_b_im": as_stored("ssm_b_im", p["ssm_b_im"])[0], "ssm_c_re": p["ssm_c_re"][0], "ssm_c_im": p["ssm_c_im"][0],
        "ssm_d": p["ssm_d"], "ssm_b_glu": p["ssm_b_glu"], "mla_q_norm_w": p["mla_q_norm_w"],
        "mla_kv_norm_w": p["mla_kv_norm_w"], "ssm_out_norm_w": p["ssm_out_norm_w"], "mla_out_norm_w": p["mla_out_norm_w"],
        "ffn_norm_w": p["ffn_norm_w"], "ffn_conv_b": p["ffn_conv_b"], "final_norm_w": p["final_norm_w"].reshape(1, -1),
    })

    loss_tile, dx, g = _local_step(x, positions.reshape(rows, 1).astype(F32), target, w, hooks)

    flat = [g[n].reshape(-1) for n in SMALL] + [g["ffn_conv_w"].reshape(-1), loss_tile[0, :1]]
    sizes = [f.shape[0] for f in flat]
    per_block = -(-sum(sizes) // (N_CORES * N_CHIPS * SMALL_COLS))
    small_rows = -(-per_block // (2 * SUBLANES)) * (2 * SUBLANES)
    padded = N_CORES * N_CHIPS * small_rows * SMALL_COLS

    def pack(parts):
        parts = list(parts)
        have = sum(q.shape[0] for q in parts)
        return jnp.concatenate(parts + [jnp.zeros((padded - have,), F32)])

    reduced = mixer + ["small"]
    small = pack(flat).reshape(N_CORES, N_CHIPS, small_rows, SMALL_COLS)
    give = [g[n](hooks.other, suffix="_give") for n in mixer] + [lax.dynamic_index_in_dim(small, 1 - ci, 0, keepdims=False)]
    lands = [lax.empty(a.shape, a.dtype) for a in give]
    give_plan = _give_plan(len(reduced))
    swap = _start_copies("grad_mixer_swap_start", give + lands, give_plan, len(reduced), dx)

    grads, delta, new_m, new_v = {}, {}, {}, {}

    def finish(n, joined, after=None):
        grad = joined if SHARDED[n] else joined.reshape(-1, joined.shape[2])
        if n == "w_in":
            wt, mt, vt = w_in, transposed(args["m_w_in"]), transposed(args["v_w_in"])
            out = _adamw(wt, grad, mt, vt, "adamw_w_in")
            delta[n], new_m[n], new_v[n], grads[n] = (jnp.swapaxes(a, 0, 1)[None] for a in out)
            return
        if n == "mla_w_uq":
            grad = _unpad_heads(grad, heads_here)
        adam(n, grad, after)

    def adam(n, grad, after=None):
        shape = p[n].shape
        out = _adamw(p[n].reshape(shape[1:]), grad, args["m_" + n].reshape(shape[1:]),
                     args["v_" + n].reshape(shape[1:]), "adamw_" + n, after)
        delta[n], new_m[n], new_v[n], grads[n] = (a.reshape(shape) for a in out)

    ffn_halves = hooks.ffn_reduced(swap[3])
    got = _wait_copies("grad_mixer_swap_wait", swap, give_plan, ffn_halves[-1])[len(reduced):]
    join_plan = _join_plan(len(FFN))
    ffn_join = _start_copies("grad_ffn_join_start", ffn_halves, join_plan, len(FFN), got[0])
    kept = [g[n](hooks.mine, suffix="_keep", add=got[t], wire=True) for t, n in enumerate(mixer)]
    small_sum = _add_other_half(small, got[-1], where, "grad_add_half_small", F32)
    sums = [k[0] for k in kept] + [small_sum]
    wires = [k[1] for k in kept] + [small_sum]
    ffn_joined = _wait_copies("grad_ffn_join_wait", ffn_join, join_plan, kept[-1][0])
    lands = [lax.empty(s.shape, s.dtype) for s in _scatter_shapes(wires)]
    scatter_plan = _scatter_plan(len(reduced))
    scatter = _start_copies("grad_mixer_scatter_start", wires + lands, scatter_plan, 3 * len(reduced), ffn_joined[0])
    behind = scatter[3]
    for n, joined in zip(FFN, ffn_joined):
        finish(n, joined, after=behind)
        behind = delta[n]
    got_pieces = _wait_copies("grad_mixer_scatter_wait", scatter, scatter_plan, delta[FFN[-1]])[len(reduced):]
    halves = [_add_pieces(sums[t], got_pieces[t], where, "grad_add_pieces_" + n) for t, n in enumerate(reduced)]
    joined = _join_halves(halves, "grad_join_halves")
    for n, j in zip(mixer, joined):
        finish(n, j)
    eighths = _place_shard(joined[-1].reshape(N_CORES * small_rows, SMALL_COLS), piece_idx, True, "place_small_grads",
                           out_dtype=F32)
    small_sum = _gather_weights([(eighths, True, False)], "gather_small_grads")[0]
    flat_sum = small_sum.reshape(N_CHIPS, N_CORES, small_rows * SMALL_COLS).transpose(1, 0, 2).reshape(-1)
    offs = [0]
    for s in sizes:
        offs.append(offs[-1] + s)
    stored = {n: as_stored(n, p[n]) for n in SMALL}
    for k, n in enumerate(SMALL):
        grads[n] = flat_sum[offs[k]:offs[k + 1]].reshape(stored[n].shape)
    taps, cols_here = p["ffn_conv_w"].shape[1], p["ffn_conv_w"].shape[2]
    conv_full = flat_sum[offs[len(SMALL)]:offs[len(SMALL) + 1]].reshape(taps, N_CHIPS * cols_here)
    adam("ffn_conv_w", lax.dynamic_slice_in_dim(conv_full, piece * cols_here, cols_here, axis=1))
    loss = flat_sum[offs[len(SMALL) + 1]]

    def rank2(a):
        return a.reshape(1, -1) if a.ndim == 1 else a

    d_s, m_s, v_s = _adamw_many([rank2(stored[n]) for n in SMALL], [rank2(grads[n]) for n in SMALL],
                                [rank2(as_stored(n, args["m_" + n])) for n in SMALL],
                                [rank2(as_stored(n, args["v_" + n])) for n in SMALL], "adamw_small")
    for k, n in enumerate(SMALL):
        delta[n], new_m[n], new_v[n], grads[n] = (as_stored(n, a.reshape(stored[n].shape))
                                                  for a in (d_s[k], m_s[k], v_s[k], grads[n]))

    return (loss, dx[None], *[grads[n] for n in WEIGHTS], *[delta[n] for n in WEIGHTS],
            *[new_m[n] for n in WEIGHTS], *[new_v[n] for n in WEIGHTS])


def kernel(x, positions, attn_norm_w, w_in, ssm_lambda_re, ssm_lambda_im, ssm_log_dt, ssm_b_re, ssm_b_im, ssm_c_re, ssm_c_im, ssm_d, ssm_w_glu, ssm_b_glu, mla_q_norm_w, mla_w_uq, mla_kv_norm_w, mla_w_ukv, ssm_out_norm_w, mla_out_norm_w, w_out, ffn_norm_w, ffn_w_up, ffn_conv_w, ffn_conv_b, ffn_w_down, final_norm_w, loss_target, m_attn_norm_w, m_w_in, m_ssm_lambda_re, m_ssm_lambda_im, m_ssm_log_dt, m_ssm_b_re, m_ssm_b_im, m_ssm_c_re, m_ssm_c_im, m_ssm_d, m_ssm_w_glu, m_ssm_b_glu, m_mla_q_norm_w, m_mla_w_uq, m_mla_kv_norm_w, m_mla_w_ukv, m_ssm_out_norm_w, m_mla_out_norm_w, m_w_out, m_ffn_norm_w, m_ffn_w_up, m_ffn_conv_w, m_ffn_conv_b, m_ffn_w_down, m_final_norm_w, v_attn_norm_w, v_w_in, v_ssm_lambda_re, v_ssm_lambda_im, v_ssm_log_dt, v_ssm_b_re, v_ssm_b_im, v_ssm_c_re, v_ssm_c_im, v_ssm_d, v_ssm_w_glu, v_ssm_b_glu, v_mla_q_norm_w, v_mla_w_uq, v_mla_kv_norm_w, v_mla_w_ukv, v_ssm_out_norm_w, v_mla_out_norm_w, v_w_out, v_ffn_norm_w, v_ffn_w_up, v_ffn_conv_w, v_ffn_conv_b, v_ffn_w_down, v_final_norm_w):
    return _step(dict(locals()))
```

```python
import functools
import math

import jax
import jax.numpy as jnp
from jax import lax
from jax.experimental import pallas as pl
from jax.experimental.pallas import tpu as pltpu

F32 = jnp.float32
BF16 = jnp.bfloat16

SSM_GROUP = 16
SSM_STATE = 64
QK_NOPE_DIM = 128
QK_ROPE_DIM = 64
V_HEAD_DIM = 128
ROPE_THETA = 10000.0
RMS_EPS = 1e-6
ADAM_LR, ADAM_B1, ADAM_B2, ADAM_EPS, ADAM_WD, ADAM_STEP = 0.001, 0.9, 0.999, 1e-08, 0.01, 10

LANES = 128
SUBLANES = 8
VMEM_LIMIT_BYTES = 56 * 1024 * 1024

GROUPS_PER_BATCH = LANES // SSM_GROUP
STATE_PER_BATCH = GROUPS_PER_BATCH * SSM_STATE
HEAD_SLOT = 2 * LANES
NEG_INF = -1e30
ATTN_BLOCK = 512
FFN_ROWS = 1024

N_CHIPS = 4
N_CORES = 2


def _tile(n, pref, align=LANES):
    if n <= pref:
        return n
    t = (pref // align) * align
    while t >= align:
        if n % t == 0:
            return t
        t -= align
    return n


def _params(sem):
    return pltpu.CompilerParams(dimension_semantics=sem, vmem_limit_bytes=VMEM_LIMIT_BYTES)


def _dot(a, b, dims):
    return lax.dot_general(a, b, (dims, ((), ())), preferred_element_type=F32)


def _dot_nn(a, b):
    return _dot(a, b, ((1,), (0,)))


def _dot_nt(a, b):
    return _dot(a, b, ((1,), (1,)))


def _dot_tn(a, b):
    return _dot(a, b, ((0,), (0,)))


def _matmul(a, b, *, mode, name, tm=512, tn=1024, tk=2048, bias=None, add=None, out_dtype=F32,
            out_blocks=None, a_split=False, b_split=False, after=None):
    if a_split:
        assert mode == "nt"
        a_shape = (a.shape[1], 2 * a.shape[2])
    else:
        a_shape = a.shape
    if b_split:
        assert mode == "tn"
        b_shape = (b.shape[1], 2 * b.shape[2])
    else:
        b_shape = b.shape
    if mode == "nn":
        (m, k), (k2, n) = a_shape, b_shape
    elif mode == "nt":
        (m, k), (n, k2) = a_shape, b_shape
    else:
        (k, m), (k2, n) = a_shape, b_shape
    assert k == k2, (a.shape, b.shape, mode)
    tm, tn, tk = _tile(m, tm, SUBLANES), _tile(n, tn), _tile(k, tk)
    nk = k // tk
    a_spec = {"nn": pl.BlockSpec((tm, tk), lambda i, j, kk: (i, kk)),
              "nt": pl.BlockSpec((tm, tk), lambda i, j, kk: (i, kk)),
              "tn": pl.BlockSpec((tk, tm), lambda i, j, kk: (kk, i))}[mode]
    b_spec = {"nn": pl.BlockSpec((tk, tn), lambda i, j, kk: (kk, j)),
              "nt": pl.BlockSpec((tn, tk), lambda i, j, kk: (j, kk)),
              "tn": pl.BlockSpec((tk, tn), lambda i, j, kk: (kk, j))}[mode]
    if a_split:
        kb = a.shape[2] // tk
        assert a.shape[2] % tk == 0
        a_spec = pl.BlockSpec((None, tm, tk), lambda i, j, kk: (kk // kb, i, kk % kb))
    if b_split:
        nb = b.shape[2] // tn
        assert b.shape[2] % tn == 0
        b_spec = pl.BlockSpec((None, tk, tn), lambda i, j, kk: (j // nb, kk, j % nb))
    dot = {"nn": _dot_nn, "nt": _dot_nt, "tn": _dot_tn}[mode]
    in_specs, operands = [a_spec, b_spec], [a, b]
    if bias is not None:
        in_specs.append(pl.BlockSpec((1, tn), lambda i, j, kk: (0, j)))
        operands.append(bias)
    if add is not None:
        in_specs.append(pl.BlockSpec((tm, tn), lambda i, j, kk: (i, j)))
        operands.append(add)
    if after is not None:
        in_specs.append(pl.BlockSpec(memory_space=pl.ANY))
        operands.append(after)

    def body(*refs):
        a_ref, b_ref = refs[0], refs[1]
        rest = list(refs[2:])
        bias_ref = rest.pop(0) if bias is not None else None
        add_ref = rest.pop(0) if add is not None else None
        if after is not None:
            rest.pop(0)
        o_ref, acc_ref = rest

        def finish(acc):
            if bias_ref is not None:
                acc = acc + bias_ref[...]
            if add_ref is not None:
                acc = acc + add_ref[...]
            o_ref[...] = acc.astype(o_ref.dtype)

        part = dot(a_ref[...].astype(BF16), b_ref[...].astype(BF16))
        if nk == 1:
            finish(part)
        else:
            kk = pl.program_id(2)

            @pl.when(kk == 0)
            def _():
                acc_ref[...] = part

            @pl.when(jnp.logical_and(kk > 0, kk < nk - 1))
            def _():
                acc_ref[...] += part

            @pl.when(kk == nk - 1)
            def _():
                finish(acc_ref[...] + part)

    if out_blocks is None:
        out_shape = jax.ShapeDtypeStruct((m, n), out_dtype)
        out_spec = pl.BlockSpec((tm, tn), lambda i, j, kk: (i, j))
    else:
        shape, block, index_map = out_blocks(tm, tn)
        out_shape = jax.ShapeDtypeStruct(shape, out_dtype)
        out_spec = pl.BlockSpec(block, index_map)
    acc_shape = (tm, tn) if nk > 1 else (SUBLANES, LANES)
    return pl.pallas_call(
        body, name=name, grid=(m // tm, n // tn, nk), in_specs=in_specs, out_specs=out_spec, out_shape=out_shape,
        scratch_shapes=[pltpu.VMEM(acc_shape, F32)],
        compiler_params=_params(("parallel", "parallel", "arbitrary")),
    )(*operands)


def _wgrad_half(a, b, rows, cols, row_sharded, name, which, *, suffix="", add=None, wire=False, tm=None, tn=None,
                b_split=False):
    tokens = a.shape[0]
    if row_sharded:
        sr, sc = rows // N_CHIPS, cols // N_CORES
    else:
        sr, sc = rows // N_CORES, cols // N_CHIPS
    tm = _tile(sr, 512) if tm is None else tm
    tn = _tile(sc, 1024) if tn is None else tn
    assert sr % tm == 0 and sc % tn == 0, (rows, cols, tm, tn)
    rb, cb = sr // tm, sc // tn
    if tn >= tm:
        ij, grid = (lambda s, t: (t, s)), (N_CHIPS, cb, rb)
    else:
        ij, grid = (lambda s, t: (s, t)), (N_CHIPS, rb, cb)
    if row_sharded:
        a_tile = lambda p, i, j, h: p * rb + i
        b_tile = lambda p, i, j, h: h[0] * cb + j
    else:
        a_tile = lambda p, i, j, h: h[0] * rb + i
        b_tile = lambda p, i, j, h: p * cb + j
    a_spec = pl.BlockSpec((tokens, tm), lambda p, s, t, h: (0, a_tile(p, *ij(s, t), h)))
    if b_split:
        nbh = b.shape[2] // tn
        assert b.shape[2] % tn == 0
        b_spec = pl.BlockSpec((None, tokens, tn), lambda p, s, t, h: (b_tile(p, *ij(s, t), h) // nbh, 0,
                                                                       b_tile(p, *ij(s, t), h) % nbh))
    else:
        b_spec = pl.BlockSpec((tokens, tn), lambda p, s, t, h: (0, b_tile(p, *ij(s, t), h)))
    out_spec = pl.BlockSpec((None, tm, tn), lambda p, s, t, h: (p, *ij(s, t)))
    in_specs, operands = [a_spec, b_spec], [a, b]
    if add is not None:
        in_specs.append(out_spec)
        operands.append(add)

    def body(h_ref, a_ref, b_ref, *rest):
        acc = _dot_tn(a_ref[...].astype(BF16), b_ref[...].astype(BF16))
        if add is not None:
            acc = acc + rest[0][...]
        for o_ref in rest[1 if add is not None else 0:]:
            o_ref[...] = acc.astype(o_ref.dtype)

    out_dtypes = [F32, BF16] if wire else [F32]
    out = pl.pallas_call(
        body, name=name + suffix, out_shape=[jax.ShapeDtypeStruct((N_CHIPS, sr, sc), dt) for dt in out_dtypes],
        grid_spec=pltpu.PrefetchScalarGridSpec(num_scalar_prefetch=1, grid=grid, in_specs=in_specs,
                                               out_specs=[out_spec] * len(out_dtypes)),
        compiler_params=_params(("parallel", "parallel", "parallel")),
    )(which, *operands)
    return tuple(out) if wire else out[0]


def _rms_rows(x):
    return lax.rsqrt(jnp.mean(x * x, axis=-1, keepdims=True) + RMS_EPS)


def _rmsnorm_fwd(x, w, *, name, width=None, col=0, out_dtype=BF16, tr=256):
    rows = x.shape[0]
    width = x.shape[1] if width is None else width
    tr = _tile(rows, tr, SUBLANES)

    def body(x_ref, w_ref, o_ref):
        xv = x_ref[...]
        o_ref[...] = (xv * _rms_rows(xv) * w_ref[...]).astype(o_ref.dtype)

    return pl.pallas_call(
        body, name=name, grid=(rows // tr,),
        in_specs=[pl.BlockSpec((tr, width), lambda i: (i, col)), pl.BlockSpec((1, width), lambda i: (0, 0))],
        out_specs=pl.BlockSpec((tr, width), lambda i: (i, 0)),
        out_shape=jax.ShapeDtypeStruct((rows, width), out_dtype),
        compiler_params=_params(("parallel",)),
    )(x, w)


def _rmsnorm_bwd_rows(xv, w, dy):
    r = _rms_rows(xv)
    n = xv * r
    dn = dy * w
    dx = r * (dn - n * jnp.mean(dn * n, axis=-1, keepdims=True))
    return dx, dy * n


def _rmsnorm_bwd(x, w, dy, *, name, width=None, col=0, dy_col=0, add=None, tr=256, dx_dtypes=(F32,)):
    rows = x.shape[0]
    n_dx = len(dx_dtypes)
    width = x.shape[1] if width is None else width
    tr = _tile(rows, tr, SUBLANES)
    in_specs = [pl.BlockSpec((tr, width), lambda i: (i, col)), pl.BlockSpec((1, width), lambda i: (0, 0)),
                pl.BlockSpec((tr, width), lambda i: (i, dy_col))]
    operands = [x, w, dy]
    if add is not None:
        in_specs.append(pl.BlockSpec((tr, width), lambda i: (i, 0)))
        operands.append(add)

    def body(*refs):
        x_ref, w_ref, dy_ref = refs[:3]
        add_ref = refs[3] if add is not None else None
        dx_refs, dw_ref = refs[-1 - n_dx:-1], refs[-1]
        dx, dwp = _rmsnorm_bwd_rows(x_ref[...], w_ref[...], dy_ref[...])
        if add_ref is not None:
            dx = dx + add_ref[...]
        for dx_ref in dx_refs:
            dx_ref[...] = dx.astype(dx_ref.dtype)
        part = jnp.sum(dwp, axis=0, keepdims=True)

        @pl.when(pl.program_id(0) == 0)
        def _():
            dw_ref[...] = part

        @pl.when(pl.program_id(0) > 0)
        def _():
            dw_ref[...] += part

    return pl.pallas_call(
        body, name=name, grid=(rows // tr,), in_specs=in_specs,
        out_specs=[pl.BlockSpec((tr, width), lambda i: (i, 0))] * n_dx + [pl.BlockSpec((1, width), lambda i: (0, 0))],
        out_shape=[jax.ShapeDtypeStruct((rows, width), dt) for dt in dx_dtypes] + [jax.ShapeDtypeStruct((1, width), F32)],
        compiler_params=_params(("arbitrary",)),
    )(*operands)


def _final_norm_loss(h, w, target, *, tr=256):
    rows, d = h.shape
    tr = _tile(rows, tr, SUBLANES)

    def body(h_ref, w_ref, t_ref, loss_ref, dh_ref, dhb_ref, dw_ref):
        hv, wv = h_ref[...], w_ref[...]
        r = _rms_rows(hv)
        n = hv * r
        err = n * wv - t_ref[...]
        d_out = err * (1.0 / d)
        dn = d_out * wv
        dh = r * (dn - n * jnp.mean(dn * n, axis=-1, keepdims=True))
        dh_ref[...] = dh
        dhb_ref[...] = dh.astype(BF16)
        dw_part = jnp.sum(d_out * n, axis=0, keepdims=True)
        loss_part = jnp.full((SUBLANES, LANES), 0.5 / d, F32) * jnp.sum(err * err)

        @pl.when(pl.program_id(0) == 0)
        def _():
            dw_ref[...] = dw_part
            loss_ref[...] = loss_part

        @pl.when(pl.program_id(0) > 0)
        def _():
            dw_ref[...] += dw_part
            loss_ref[...] += loss_part

    return pl.pallas_call(
        body, name="final_norm_loss", grid=(rows // tr,),
        in_specs=[pl.BlockSpec((tr, d), lambda i: (i, 0)), pl.BlockSpec((1, d), lambda i: (0, 0)),
                  pl.BlockSpec((tr, d), lambda i: (i, 0))],
        out_specs=[pl.BlockSpec((SUBLANES, LANES), lambda i: (0, 0)), pl.BlockSpec((tr, d), lambda i: (i, 0)),
                   pl.BlockSpec((tr, d), lambda i: (i, 0)), pl.BlockSpec((1, d), lambda i: (0, 0))],
        out_shape=[jax.ShapeDtypeStruct((SUBLANES, LANES), F32), jax.ShapeDtypeStruct((rows, d), F32),
                   jax.ShapeDtypeStruct((rows, d), BF16), jax.ShapeDtypeStruct((1, d), F32)],
        compiler_params=_params(("arbitrary",)),
    )(h, w, target)


def _cmul(ar, ai, br, bi):
    return ar * br - ai * bi, ar * bi + ai * br


def _dot_exact(a, b, dims):
    return lax.dot_general(a, b, (dims, ((), ())), preferred_element_type=F32, precision=lax.Precision.HIGHEST)


def _s5_discretize(lr, li, dt):
    mag = jnp.exp(lr * dt)
    th = li * dt
    ar, ai = mag * jnp.cos(th), mag * jnp.sin(th)
    nr, ni = ar - 1.0, ai
    den = lr * lr + li * li
    zr = (nr * lr + ni * li) / den
    zi = (ni * lr - nr * li) / den
    return mag, ar, ai, nr, ni, den, zr, zi


def _band_slices(group):
    j, gi = divmod(group, GROUPS_PER_BATCH)
    rows = slice(gi * SSM_GROUP, (gi + 1) * SSM_GROUP)
    re = slice(gi * SSM_STATE, (gi + 1) * SSM_STATE)
    im = slice(STATE_PER_BATCH + gi * SSM_STATE, STATE_PER_BATCH + (gi + 1) * SSM_STATE)
    return j, rows, re, im


def _s5_bands(lam_re, lam_im, log_dt, b_re, b_im, c_re, c_im):
    g, _ = lam_re.shape
    nb = g // GROUPS_PER_BATCH
    s2 = 2 * STATE_PER_BATCH

    def body(lr_ref, li_ref, ldt_ref, br_ref, bi_ref, cr_ref, ci_ref, wb_ref, wct_ref, a_ref):
        dt = jnp.exp(ldt_ref[...])
        _, ar, ai, _, _, _, zr, zi = _s5_discretize(lr_ref[...], li_ref[...], dt)
        wb_ref[...] = jnp.zeros_like(wb_ref)
        wct_ref[...] = jnp.zeros_like(wct_ref)
        for group in range(g):
            j, rows, re, im = _band_slices(group)
            zr_g, zi_g = zr[group:group + 1, :], zi[group:group + 1, :]
            bre, bim = br_ref[group], bi_ref[group]
            wb_ref[j, rows, re] = (zr_g * bre - zi_g * bim).astype(BF16)
            wb_ref[j, rows, im] = (zr_g * bim + zi_g * bre).astype(BF16)
            wct_ref[j, rows, re] = cr_ref[group].astype(BF16)
            wct_ref[j, rows, im] = (-ci_ref[group]).astype(BF16)
            a_ref[j, :, re] = ar[group:group + 1, :]
            a_ref[j, :, im] = ai[group:group + 1, :]

    return pl.pallas_call(
        body, name="s5_bands",
        out_shape=[jax.ShapeDtypeStruct((nb, LANES, s2), BF16)] * 2 + [jax.ShapeDtypeStruct((nb, 1, s2), F32)],
    )(lam_re, lam_im, log_dt, b_re, b_im, c_re, c_im)


def _s5_bands_bwd(lam_re, lam_im, log_dt, b_re, b_im, dwb, dwct, dabar):
    g, p = lam_re.shape
    gh = b_re.shape[1:]

    def body(lr_ref, li_ref, ldt_ref, br_ref, bi_ref, dwb_ref, dwct_ref, da_ref,
             dlr_ref, dli_ref, dldt_ref, dbre_ref, dbim_ref, dcre_ref, dcim_ref, dzr_ref, dzi_ref, dar_ref, dai_ref):
        lr, li = lr_ref[...], li_ref[...]
        dt = jnp.exp(ldt_ref[...])
        mag, ar, ai, nr, ni, den, zr, zi = _s5_discretize(lr, li, dt)
        for group in range(g):
            j, rows, re, im = _band_slices(group)
            zr_g, zi_g = zr[group:group + 1, :], zi[group:group + 1, :]
            bre, bim = br_ref[group], bi_ref[group]
            dbr, dbi = dwb_ref[j, rows, re], dwb_ref[j, rows, im]
            dbre_ref[group] = zr_g * dbr + zi_g * dbi
            dbim_ref[group] = zr_g * dbi - zi_g * dbr
            dzr_ref[group:group + 1, :] = jnp.sum(bre * dbr + bim * dbi, axis=0, keepdims=True)
            dzi_ref[group:group + 1, :] = jnp.sum(bre * dbi - bim * dbr, axis=0, keepdims=True)
            dcre_ref[group] = dwct_ref[j, rows, re]
            dcim_ref[group] = -dwct_ref[j, rows, im]
            dar_ref[group:group + 1, :] = da_ref[j, :, re]
            dai_ref[group:group + 1, :] = da_ref[j, :, im]
        dzr, dzi = dzr_ref[...], dzi_ref[...]
        inv = 1.0 / den
        d_nr = (dzr * lr - dzi * li) * inv
        d_ni = (dzr * li + dzi * lr) * inv
        d_den = -(dzr * zr + dzi * zi) * inv
        d_lr = (dzr * nr + dzi * ni) * inv + 2.0 * lr * d_den
        d_li = (dzr * ni - dzi * nr) * inv + 2.0 * li * d_den
        t_ar = dar_ref[...] + d_nr
        t_ai = dai_ref[...] + d_ni
        d_lrdt = t_ar * ar + t_ai * ai
        d_th = t_ai * ar - t_ar * ai
        dlr_ref[...] = d_lr + d_lrdt * dt
        dli_ref[...] = d_li + d_th * dt
        dldt_ref[...] = jnp.sum(d_lrdt * lr + d_th * li, axis=1, keepdims=True) * dt

    return pl.pallas_call(
        body, name="s5_bands_bwd",
        out_shape=[jax.ShapeDtypeStruct((g, p), F32)] * 2 + [jax.ShapeDtypeStruct((g, 1), F32)]
        + [jax.ShapeDtypeStruct((g,) + gh, F32)] * 4,
        scratch_shapes=[pltpu.VMEM((g, p), F32)] * 4,
    )(lam_re, lam_im, log_dt, b_re, b_im, dwb, dwct, dabar)


def _powers(ar, ai, count):
    out = [(ar, ai)]
    for _ in range(count - 1):
        out.append(_cmul(out[-1][0], out[-1][1], ar, ai))
    return out


def _scan_coefs(ar, ai, reverse):
    w = ar.shape[-1]
    pw = _powers(ar, ai, SUBLANES)
    row = lax.broadcasted_iota(jnp.int32, (SUBLANES, w), 0)
    steps = []
    d = 1
    while d < SUBLANES:
        keep = (row < SUBLANES - d) if reverse else (row >= d)
        pr, pi = pw[d - 1]
        steps.append((d, jnp.where(keep, pr, 0.0), jnp.where(keep, pi, 0.0)))
        d *= 2
    cr = jnp.zeros((SUBLANES, w), F32)
    ci = jnp.zeros((SUBLANES, w), F32)
    for t in range(SUBLANES):
        pr, pi = pw[SUBLANES - 1 - t] if reverse else pw[t]
        cr = jnp.where(row == t, pr, cr)
        ci = jnp.where(row == t, pi, ci)
    return steps, cr, ci


def _scan_tile(xr, xi, carry_r, carry_i, coefs, reverse):
    steps, cr, ci = coefs
    for d, mr, mi in steps:
        shift = SUBLANES - d if reverse else d
        sr, si = pltpu.roll(xr, shift, 0), pltpu.roll(xi, shift, 0)
        pr, pi = _cmul(mr, mi, sr, si)
        xr, xi = xr + pr, xi + pi
    pr, pi = _cmul(cr, ci, carry_r, carry_i)
    return xr + pr, xi + pi


def _gelu(x):
    c = math.sqrt(2.0 / math.pi)
    return 0.5 * x * (1.0 + jnp.tanh(c * (x + 0.044715 * x * x * x)))


def _gelu_grad(x):
    c = math.sqrt(2.0 / math.pi)
    t = jnp.tanh(c * (x + 0.044715 * x * x * x))
    return 0.5 * (1.0 + t) + 0.5 * x * (1.0 - t * t) * c * (1.0 + 3.0 * 0.044715 * x * x)


def _s5_fwd(proj, wb, wct, d_skip, abar):
    rows = proj.shape[0]
    nb = wb.shape[0]
    s2 = 2 * STATE_PER_BATCH
    st = STATE_PER_BATCH
    chunk = _tile(rows, 512, SUBLANES)

    def body(u_ref, wb_ref, wc_ref, d_ref, a_ref, s_ref, y_ref, yg_ref):
        for c0 in range(0, rows, chunk):
            s_ref[pl.ds(c0, chunk), :] = _dot_nn(u_ref[pl.ds(c0, chunk), :].astype(BF16), wb_ref[...])
        av = a_ref[...]
        coefs = _scan_coefs(av[:, :st], av[:, st:], reverse=False)

        def tile(b, carry):
            r0 = pl.multiple_of(b * SUBLANES, SUBLANES)
            xr, xi = _scan_tile(s_ref[pl.ds(r0, SUBLANES), :st], s_ref[pl.ds(r0, SUBLANES), st:], carry[0], carry[1],
                                coefs, False)
            s_ref[pl.ds(r0, SUBLANES), :st] = xr
            s_ref[pl.ds(r0, SUBLANES), st:] = xi
            return xr[SUBLANES - 1:, :], xi[SUBLANES - 1:, :]

        zero = jnp.zeros((1, st), F32)
        lax.fori_loop(0, rows // SUBLANES, tile, (zero, zero))
        for c0 in range(0, rows, chunk):
            y = _dot_nt(s_ref[pl.ds(c0, chunk), :].astype(BF16), wc_ref[...]) + d_ref[...] * u_ref[pl.ds(c0, chunk), :]
            y_ref[pl.ds(c0, chunk), :] = y
            yg_ref[pl.ds(c0, chunk), :] = _gelu(y).astype(BF16)

    return pl.pallas_call(
        body, name="s5_fwd", grid=(nb,),
        in_specs=[pl.BlockSpec((rows, LANES), lambda j: (0, j)), pl.BlockSpec((None, LANES, s2), lambda j: (j, 0, 0)),
                  pl.BlockSpec((None, LANES, s2), lambda j: (j, 0, 0)), pl.BlockSpec((1, LANES), lambda j: (0, j)),
                  pl.BlockSpec((None, 1, s2), lambda j: (j, 0, 0))],
        out_specs=[pl.BlockSpec((rows, s2), lambda j: (0, j)), pl.BlockSpec((rows, LANES), lambda j: (0, j)),
                   pl.BlockSpec((rows, LANES), lambda j: (0, j))],
        out_shape=[jax.ShapeDtypeStruct((rows, nb * s2), F32), jax.ShapeDtypeStruct((rows, nb * LANES), F32),
                   jax.ShapeDtypeStruct((rows, nb * LANES), BF16)],
        compiler_params=_params(("parallel",)),
    )(proj, wb, wct, d_skip, abar)


def _s5_bwd(proj, states, y_pre, dyg_a, dyg_b, wb, wct, d_skip, abar):
    rows = proj.shape[0]
    nb = wb.shape[0]
    s2 = 2 * STATE_PER_BATCH
    st = STATE_PER_BATCH
    chunk = _tile(rows, 512, SUBLANES)
    n_tiles = rows // SUBLANES

    def body(u_ref, s_ref, y_ref, ga_ref, gb_ref, wb_ref, wc_ref, d_ref, a_ref,
             du_ref, dwb_ref, dwc_ref, da_ref, dd_ref, ds_ref, dy_ref):
        dy_ref[...] = (ga_ref[...] + gb_ref[...]) * _gelu_grad(y_ref[...])
        dd_ref[...] = jnp.sum(dy_ref[...] * u_ref[...], axis=0, keepdims=True)
        for c0 in range(0, rows, chunk):
            ds_ref[pl.ds(c0, chunk), :] = _dot_nn(dy_ref[pl.ds(c0, chunk), :].astype(BF16), wc_ref[...])
        dwc_ref[...] = _dot_tn(dy_ref[...].astype(BF16), s_ref[...].astype(BF16))
        av = a_ref[...]
        coefs = _scan_coefs(av[:, :st], -av[:, st:], reverse=True)
        row = lax.broadcasted_iota(jnp.int32, (SUBLANES, st), 0)

        def tile(k, carry):
            cr, ci, acc_r, acc_i = carry
            b = n_tiles - 1 - k
            r0 = pl.multiple_of(b * SUBLANES, SUBLANES)
            rp = pl.multiple_of(jnp.maximum(b - 1, 0) * SUBLANES, SUBLANES)
            xr, xi = _scan_tile(ds_ref[pl.ds(r0, SUBLANES), :st], ds_ref[pl.ds(r0, SUBLANES), st:], cr, ci, coefs, True)
            ds_ref[pl.ds(r0, SUBLANES), :st] = xr
            ds_ref[pl.ds(r0, SUBLANES), st:] = xi
            first = jnp.where(b > 0, 1.0, 0.0)
            pr = jnp.where(row == 0, pltpu.roll(s_ref[pl.ds(rp, SUBLANES), :st], 1, 0) * first,
                           pltpu.roll(s_ref[pl.ds(r0, SUBLANES), :st], 1, 0))
            pi = jnp.where(row == 0, pltpu.roll(s_ref[pl.ds(rp, SUBLANES), st:], 1, 0) * first,
                           pltpu.roll(s_ref[pl.ds(r0, SUBLANES), st:], 1, 0))
            acc_r = acc_r + pr * xr + pi * xi
            acc_i = acc_i + pr * xi - pi * xr
            return xr[:1, :], xi[:1, :], acc_r, acc_i

        zero = jnp.zeros((1, st), F32)
        zacc = jnp.zeros((SUBLANES, st), F32)
        _, _, acc_r, acc_i = lax.fori_loop(0, n_tiles, tile, (zero, zero, zacc, zacc))
        da_ref[:, :st] = jnp.sum(acc_r, axis=0, keepdims=True)
        da_ref[:, st:] = jnp.sum(acc_i, axis=0, keepdims=True)
        for c0 in range(0, rows, chunk):
            du_ref[pl.ds(c0, chunk), :] = (_dot_nt(ds_ref[pl.ds(c0, chunk), :].astype(BF16), wb_ref[...])
                                           + d_ref[...] * dy_ref[pl.ds(c0, chunk), :]).astype(du_ref.dtype)
        dwb_ref[...] = _dot_tn(u_ref[...].astype(BF16), ds_ref[...].astype(BF16))

    col = pl.BlockSpec((rows, LANES), lambda j: (0, j))
    return pl.pallas_call(
        body, name="s5_bwd", grid=(nb,),
        in_specs=[col, pl.BlockSpec((rows, s2), lambda j: (0, j)), col, col, col,
                  pl.BlockSpec((None, LANES, s2), lambda j: (j, 0, 0)), pl.BlockSpec((None, LANES, s2), lambda j: (j, 0, 0)),
                  pl.BlockSpec((1, LANES), lambda j: (0, j)), pl.BlockSpec((None, 1, s2), lambda j: (j, 0, 0))],
        out_specs=[col, pl.BlockSpec((None, LANES, s2), lambda j: (j, 0, 0)),
                   pl.BlockSpec((None, LANES, s2), lambda j: (j, 0, 0)), pl.BlockSpec((None, 1, s2), lambda j: (j, 0, 0)),
                   pl.BlockSpec((1, LANES), lambda j: (0, j))],
        out_shape=[jax.ShapeDtypeStruct((rows, nb * LANES), BF16), jax.ShapeDtypeStruct((nb, LANES, s2), F32),
                   jax.ShapeDtypeStruct((nb, LANES, s2), F32), jax.ShapeDtypeStruct((nb, 1, s2), F32),
                   jax.ShapeDtypeStruct((1, nb * LANES), F32)],
        scratch_shapes=[pltpu.VMEM((rows, s2), F32), pltpu.VMEM((rows, LANES), F32)],
        compiler_params=_params(("parallel",)),
    )(proj, states, y_pre, dyg_a, dyg_b, wb, wct, d_skip, abar)


def _glu_norm_fwd(y_pre, z, w, *, tr=256):
    rows, width = y_pre.shape
    tr = _tile(rows, tr, SUBLANES)

    def body(y_ref, z_ref, w_ref, o_ref):
        v = _gelu(y_ref[...]) * jax.nn.sigmoid(z_ref[...])
        o_ref[...] = (v * _rms_rows(v) * w_ref[...]).astype(o_ref.dtype)

    blk = pl.BlockSpec((tr, width), lambda i: (i, 0))
    return pl.pallas_call(
        body, name="glu_norm_fwd", grid=(rows // tr,),
        in_specs=[blk, blk, pl.BlockSpec((1, width), lambda i: (0, 0))], out_specs=blk,
        out_shape=jax.ShapeDtypeStruct((rows, width), BF16), compiler_params=_params(("parallel",)),
    )(y_pre, z, w)


def _glu_norm_bwd(y_pre, z, w, dycat, *, tr=256):
    rows, width = y_pre.shape
    tr = _tile(rows, tr, SUBLANES)

    def body(y_ref, z_ref, w_ref, dy_ref, dz_ref, dg_ref, dw_ref, db_ref):
        yg = _gelu(y_ref[...])
        sg = jax.nn.sigmoid(z_ref[...])
        dv, dwp = _rmsnorm_bwd_rows(yg * sg, w_ref[...], dy_ref[...])
        dz = dv * yg * sg * (1.0 - sg)
        dz_ref[...] = dz.astype(dz_ref.dtype)
        dg_ref[...] = dv * sg
        dw_part = jnp.sum(dwp, axis=0, keepdims=True)
        db_part = jnp.sum(dz, axis=0, keepdims=True)

        @pl.when(pl.program_id(0) == 0)
        def _():
            dw_ref[...] = dw_part
            db_ref[...] = db_part

        @pl.when(pl.program_id(0) > 0)
        def _():
            dw_ref[...] += dw_part
            db_ref[...] += db_part

    blk = pl.BlockSpec((tr, width), lambda i: (i, 0))
    vec = pl.BlockSpec((1, width), lambda i: (0, 0))
    return pl.pallas_call(
        body, name="glu_norm_bwd", grid=(rows // tr,), in_specs=[blk, blk, vec, blk], out_specs=[blk, blk, vec, vec],
        out_shape=[jax.ShapeDtypeStruct((rows, width), BF16), jax.ShapeDtypeStruct((rows, width), F32)]
        + [jax.ShapeDtypeStruct((1, width), F32)] * 2,
        compiler_params=_params(("arbitrary",)),
    )(y_pre, z, w, dycat)


def _rope_tables(pos, freq, sign):
    rows = pos.shape[0]

    def body(p_ref, f_ref, s_ref, cos_ref, sin_ref):
        ang = p_ref[...] * f_ref[...]
        cos_ref[...] = jnp.cos(ang)
        sin_ref[...] = jnp.sin(ang) * s_ref[...]

    return pl.pallas_call(body, name="rope_tables", out_shape=[jax.ShapeDtypeStruct((rows, LANES), F32)] * 2)(pos, freq, sign)


def _rope(x, cos, sin_signed):
    half = QK_ROPE_DIM // 2
    src = lax.broadcasted_iota(jnp.int32, (LANES, LANES), 0)
    dst = lax.broadcasted_iota(jnp.int32, (LANES, LANES), 1)
    swap = jnp.where(jnp.logical_or(jnp.logical_and(dst < half, src == dst + half),
                                    jnp.logical_and(jnp.logical_and(dst >= half, dst < 2 * half), src == dst - half)),
                     1.0, 0.0).astype(F32)
    swapped = _dot_exact(x, swap, ((1,), (0,)))
    return x * cos + swapped * sin_signed


def _attn_prep(q, kv, proj, kpe_col, cos, sin, *, tr=256):
    rows = q.shape[0]
    heads = q.shape[1] // HEAD_SLOT
    tr = _tile(rows, tr, SUBLANES)

    def body(q_ref, kv_ref, kpe_ref, cos_ref, sin_ref, qc_ref, kc_ref, v_ref):
        c, s = cos_ref[...], sin_ref[...]
        kpe = _rope(kpe_ref[...], c, s).astype(BF16)
        for h in range(heads):
            nope = slice(h * HEAD_SLOT, h * HEAD_SLOT + LANES)
            pe = slice(h * HEAD_SLOT + LANES, (h + 1) * HEAD_SLOT)
            qc_ref[:, nope] = q_ref[:, nope].astype(BF16)
            qc_ref[:, pe] = _rope(q_ref[:, pe], c, s).astype(BF16)
            kc_ref[:, nope] = kv_ref[:, nope].astype(BF16)
            kc_ref[:, pe] = kpe
            v_ref[:, h * LANES:(h + 1) * LANES] = kv_ref[:, pe].astype(BF16)

    slots = pl.BlockSpec((tr, heads * HEAD_SLOT), lambda i: (i, 0))
    tab = pl.BlockSpec((tr, LANES), lambda i: (i, 0))
    return pl.pallas_call(
        body, name="attn_prep", grid=(rows // tr,),
        in_specs=[slots, slots, pl.BlockSpec((tr, LANES), lambda i: (i, kpe_col)), tab, tab],
        out_specs=[slots, slots, pl.BlockSpec((tr, heads * LANES), lambda i: (i, 0))],
        out_shape=[jax.ShapeDtypeStruct((rows, heads * HEAD_SLOT), BF16)] * 2
        + [jax.ShapeDtypeStruct((rows, heads * LANES), BF16)],
        compiler_params=_params(("parallel",)),
    )(q, kv, proj, cos, sin)


def _causal(tq, tk):
    return lax.broadcasted_iota(jnp.int32, (tq, tk), 1) <= lax.broadcasted_iota(jnp.int32, (tq, tk), 0)


def _attn_fwd(qc, kc, vb, *, scale, tq=512):
    rows = qc.shape[0]
    heads = qc.shape[1] // HEAD_SLOT
    tq = _tile(rows, tq, SUBLANES)
    tk = tq

    def body(q_ref, k_ref, v_ref, o_ref, lse_ref):
        i = pl.program_id(1)
        q = q_ref[...]

        def step(j, carry, diagonal):
            m, l, acc = carry
            k0 = pl.multiple_of(j * tk, tk)
            s = _dot_nt(q, k_ref[pl.ds(k0, tk), :]) * scale
            if diagonal:
                s = jnp.where(_causal(tq, tk), s, NEG_INF)
            m_new = jnp.maximum(m, jnp.max(s, axis=-1, keepdims=True))
            p = jnp.exp(s - m_new)
            alpha = jnp.exp(m - m_new)
            l = alpha * l + jnp.sum(p, axis=-1, keepdims=True)
            acc = alpha * acc + _dot_nn(p.astype(BF16), v_ref[pl.ds(k0, tk), :])
            return m_new, l, acc

        init = (jnp.full((tq, 1), NEG_INF, F32), jnp.zeros((tq, 1), F32), jnp.zeros((tq, LANES), F32))
        below = lax.fori_loop(0, i, lambda j, carry: step(j, carry, False), init)
        m, l, acc = step(i, below, True)
        o_ref[...] = acc / l
        lse_ref[...] = jnp.broadcast_to(m + jnp.log(l), (tq, LANES))

    return pl.pallas_call(
        body, name="attn_fwd", grid=(heads, rows // tq),
        in_specs=[pl.BlockSpec((tq, HEAD_SLOT), lambda h, i: (i, h)), pl.BlockSpec((rows, HEAD_SLOT), lambda h, i: (0, h)),
                  pl.BlockSpec((rows, LANES), lambda h, i: (0, h))],
        out_specs=[pl.BlockSpec((tq, LANES), lambda h, i: (i, h))] * 2,
        out_shape=[jax.ShapeDtypeStruct((rows, heads * LANES), F32)] * 2,
        compiler_params=_params(("parallel", "parallel")),
    )(qc, kc, vb)


def _attn_bwd(qc, kc, vb, o, do, lse, cos, sin, *, scale, tk=512):
    rows = qc.shape[0]
    heads = qc.shape[1] // HEAD_SLOT
    tk = _tile(rows, tk, SUBLANES)
    tq = tk
    nq = rows // tq

    def body(q_ref, k_ref, v_ref, o_ref, do_ref, lse_ref, cos_ref, sin_ref, dq_ref, dkv_ref, dkpe_ref, dq_acc, delta_ref):
        j = pl.program_id(1)

        @pl.when(j == 0)
        def _():
            dq_acc[...] = jnp.zeros_like(dq_acc)
            for r0 in range(0, rows, tq):
                d = jnp.sum(do_ref[pl.ds(r0, tq), :] * o_ref[pl.ds(r0, tq), :], axis=-1, keepdims=True)
                delta_ref[pl.ds(r0, tq), :] = jnp.broadcast_to(d, (tq, LANES))

        kb, vv = k_ref[...], v_ref[...]

        def step(i, carry, diagonal):
            dk, dv = carry
            q0 = pl.multiple_of(i * tq, tq)
            qb = q_ref[pl.ds(q0, tq), :]
            dob = do_ref[pl.ds(q0, tq), :].astype(BF16)
            s = _dot_nt(qb, kb) * scale
            p = jnp.exp(s - lse_ref[pl.ds(q0, tq), :1])
            if diagonal:
                p = jnp.where(_causal(tq, tk), p, 0.0)
            dv = dv + _dot_tn(p.astype(BF16), dob)
            ds = (p * (_dot_nt(dob, vv) - delta_ref[pl.ds(q0, tq), :1])).astype(BF16)
            dk = dk + _dot_tn(ds, qb)
            dq_acc[pl.ds(q0, tq), :] += _dot_nn(ds, kb)
            return dk, dv

        zero = (jnp.zeros((tk, HEAD_SLOT), F32), jnp.zeros((tk, LANES), F32))
        dk, dv = lax.fori_loop(j + 1, nq, lambda i, carry: step(i, carry, False), step(j, zero, True))
        dkv_ref[:, :LANES] = (dk[:, :LANES] * scale).astype(dkv_ref.dtype)
        dkv_ref[:, LANES:] = dv.astype(dkv_ref.dtype)
        dkpe_ref[...] = dk[:, LANES:] * scale

        @pl.when(j == nq - 1)
        def _():
            for r0 in range(0, rows, tq):
                dq = dq_acc[pl.ds(r0, tq), :] * scale
                dq_ref[pl.ds(r0, tq), :LANES] = dq[:, :LANES].astype(dq_ref.dtype)
                dq_ref[pl.ds(r0, tq), LANES:] = _rope(dq[:, LANES:], cos_ref[pl.ds(r0, tq), :],
                                                      -sin_ref[pl.ds(r0, tq), :]).astype(dq_ref.dtype)

    full_q = pl.BlockSpec((rows, HEAD_SLOT), lambda h, j: (0, h))
    full_v = pl.BlockSpec((rows, LANES), lambda h, j: (0, h))
    tab = pl.BlockSpec((rows, LANES), lambda h, j: (0, 0))
    return pl.pallas_call(
        body, name="attn_bwd", grid=(heads, rows // tk),
        in_specs=[full_q, pl.BlockSpec((tk, HEAD_SLOT), lambda h, j: (j, h)), pl.BlockSpec((tk, LANES), lambda h, j: (j, h)),
                  full_v, full_v, full_v, tab, tab],
        out_specs=[full_q, pl.BlockSpec((tk, HEAD_SLOT), lambda h, j: (j, h)), pl.BlockSpec((tk, LANES), lambda h, j: (j, h))],
        out_shape=[jax.ShapeDtypeStruct((rows, heads * HEAD_SLOT), BF16), jax.ShapeDtypeStruct((rows, heads * HEAD_SLOT), BF16),
                   jax.ShapeDtypeStruct((rows, heads * LANES), F32)],
        scratch_shapes=[pltpu.VMEM((rows, HEAD_SLOT), F32), pltpu.VMEM((rows, LANES), F32)],
        compiler_params=_params(("parallel", "arbitrary")),
    )(qc, kc, vb, o, do, lse, cos, sin)


def _kpe_bwd(dkpe_heads, cos, sin, *, tr=512):
    rows = dkpe_heads.shape[0]
    heads = dkpe_heads.shape[1] // LANES
    tr = _tile(rows, tr, 2 * SUBLANES)

    def body(d_ref, cos_ref, sin_ref, o_ref):
        acc = d_ref[:, :LANES]
        for h in range(1, heads):
            acc = acc + d_ref[:, h * LANES:(h + 1) * LANES]
        o_ref[...] = _rope(acc, cos_ref[...], -sin_ref[...]).astype(o_ref.dtype)

    tab = pl.BlockSpec((tr, LANES), lambda i: (i, 0))
    return pl.pallas_call(
        body, name="kpe_bwd", grid=(rows // tr,),
        in_specs=[pl.BlockSpec((tr, heads * LANES), lambda i: (i, 0)), tab, tab], out_specs=tab,
        out_shape=jax.ShapeDtypeStruct((rows, LANES), BF16), compiler_params=_params(("parallel",)),
    )(dkpe_heads, cos, sin)


CONV_ROWS = 128


def _with_halo(ref, r0, ci, n_chunks, ch, lanes, before, after):
    parts = []
    if before:
        lo = pl.multiple_of(jnp.maximum(r0 - SUBLANES, 0), SUBLANES)
        parts.append(ref[pl.ds(lo, SUBLANES), lanes] * jnp.where(ci > 0, 1.0, 0.0))
    parts.append(ref[pl.ds(r0, ch), lanes])
    if after:
        hi = pl.multiple_of(jnp.minimum(r0 + ch, n_chunks * ch - SUBLANES), SUBLANES)
        parts.append(ref[pl.ds(hi, SUBLANES), lanes] * jnp.where(ci < n_chunks - 1, 1.0, 0.0))
    return jnp.concatenate(parts, axis=0)


def _taps(ext):
    return pltpu.roll(ext, 2, 0)[SUBLANES:], pltpu.roll(ext, 1, 0)[SUBLANES:], ext[SUBLANES:]


def _conv3(taps, w, b):
    return w[0:1, :] * taps[0] + w[1:2, :] * taps[1] + w[2:3, :] * taps[2] + b


def _conv_gate_fwd(a, conv_w, conv_b, *, tc=256):
    rows, f2 = a.shape
    f = f2 // 2
    tc = _tile(f, tc)
    nc = f // tc
    ch = _tile(rows, CONV_ROWS, SUBLANES)
    n_chunks = rows // ch

    def body(ag_ref, av_ref, wg_ref, wv_ref, bg_ref, bv_ref, o_ref):
        for lt in range(tc // LANES):
            lanes = slice(lt * LANES, (lt + 1) * LANES)
            wg, wv, bg, bv = wg_ref[:, lanes], wv_ref[:, lanes], bg_ref[:, lanes], bv_ref[:, lanes]

            def chunk(ci, carry):
                r0 = pl.multiple_of(ci * ch, ch)
                gate = _conv3(_taps(_with_halo(ag_ref, r0, ci, n_chunks, ch, lanes, True, False)), wg, bg)
                val = _conv3(_taps(_with_halo(av_ref, r0, ci, n_chunks, ch, lanes, True, False)), wv, bv)
                o_ref[pl.ds(r0, ch), lanes] = (gate * jax.nn.sigmoid(gate) * val).astype(o_ref.dtype)
                return carry

            lax.fori_loop(0, n_chunks, chunk, 0)

    return pl.pallas_call(
        body, name="conv_gate_fwd", grid=(nc,),
        in_specs=[pl.BlockSpec((rows, tc), lambda j: (0, j)), pl.BlockSpec((rows, tc), lambda j: (0, j + nc)),
                  pl.BlockSpec((SUBLANES, tc), lambda j: (0, j)), pl.BlockSpec((SUBLANES, tc), lambda j: (0, j + nc)),
                  pl.BlockSpec((1, tc), lambda j: (0, j)), pl.BlockSpec((1, tc), lambda j: (0, j + nc))],
        out_specs=pl.BlockSpec((rows, tc), lambda j: (0, j)),
        out_shape=jax.ShapeDtypeStruct((rows, f), BF16), compiler_params=_params(("parallel",)),
    )(a, a, conv_w, conv_w, conv_b, conv_b)


def _conv_gate_bwd(a, conv_w, conv_b, dg, *, tc=256):
    rows, f2 = a.shape
    f = f2 // 2
    tc = _tile(f, tc)
    nc = f // tc
    ch = _tile(rows, CONV_ROWS, SUBLANES)
    n_chunks = rows // ch
    ext_rows = ch + SUBLANES

    def fold(x):
        return jnp.sum(x.reshape(ch // SUBLANES, SUBLANES, LANES), axis=0)

    def body(ag_ref, av_ref, wg_ref, wv_ref, bg_ref, bv_ref, dg_ref, da_ref, dw_ref, db_ref):
        for lt in range(tc // LANES):
            lanes = slice(lt * LANES, (lt + 1) * LANES)
            wg, wv, bg, bv = wg_ref[:, lanes], wv_ref[:, lanes], bg_ref[:, lanes], bv_ref[:, lanes]

            def chunk(ci, acc):
                r0 = pl.multiple_of(ci * ch, ch)
                taps_g = _taps(_with_halo(ag_ref, r0, ci, n_chunks, ch, lanes, True, True))
                taps_v = _taps(_with_halo(av_ref, r0, ci, n_chunks, ch, lanes, True, True))
                dge = _with_halo(dg_ref, r0, ci, n_chunks, ch, lanes, False, True)
                gate, val = _conv3(taps_g, wg, bg), _conv3(taps_v, wv, bv)
                sg = jax.nn.sigmoid(gate)
                d_gate = dge * val * sg * (1.0 + gate * (1.0 - sg))
                d_val = dge * gate * sg
                new = []
                for half, (taps, w, d) in enumerate(((taps_g, wg, d_gate), (taps_v, wv, d_val))):
                    da = (w[2:3, :] * d[:ch] + w[1:2, :] * pltpu.roll(d, ext_rows - 1, 0)[:ch]
                          + w[0:1, :] * pltpu.roll(d, ext_rows - 2, 0)[:ch])
                    da_ref[half, pl.ds(r0, ch), lanes] = da.astype(da_ref.dtype)
                    dc = d[:ch]
                    sums = [fold(dc)] + [fold(dc * t[:ch]) for t in taps]
                    new.append(tuple(x + s for x, s in zip(acc[half], sums)))
                return tuple(new)

            zero = tuple(jnp.zeros((SUBLANES, LANES), F32) for _ in range(4))
            acc = lax.fori_loop(0, n_chunks, chunk, (zero, zero))
            row = lax.broadcasted_iota(jnp.int32, (SUBLANES, LANES), 0)
            for half in range(2):
                db, *taps = (jnp.sum(x, axis=0, keepdims=True) for x in acc[half])
                db_ref[half, :, lanes] = db
                dw = jnp.zeros((SUBLANES, LANES), F32)
                for tap in range(3):
                    dw = jnp.where(row == tap, taps[tap], dw)
                dw_ref[half, :, lanes] = dw

    lo = lambda j: (0, j)
    hi = lambda j: (0, j + nc)
    both = lambda j: (0, 0, j)
    return pl.pallas_call(
        body, name="conv_gate_bwd", grid=(nc,),
        in_specs=[pl.BlockSpec((rows, tc), lo), pl.BlockSpec((rows, tc), hi), pl.BlockSpec((SUBLANES, tc), lo),
                  pl.BlockSpec((SUBLANES, tc), hi), pl.BlockSpec((1, tc), lo), pl.BlockSpec((1, tc), hi),
                  pl.BlockSpec((rows, tc), lo)],
        out_specs=[pl.BlockSpec((2, rows, tc), both), pl.BlockSpec((2, SUBLANES, tc), both), pl.BlockSpec((2, 1, tc), both)],
        out_shape=[jax.ShapeDtypeStruct((2, rows, f), BF16), jax.ShapeDtypeStruct((2, SUBLANES, f), F32),
                   jax.ShapeDtypeStruct((2, 1, f), F32)],
        compiler_params=_params(("parallel",)),
    )(a, a, conv_w, conv_w, conv_b, conv_b, dg)


def _wgrad(a, b, rows, cols, row_sharded, name, **kw):
    return functools.partial(_wgrad_half, a, b, rows, cols, row_sharded, name, **kw)


class _NoExchange:
    def __init__(self, later, ffn):
        self.later, self.ffn = later, ffn

    def mixer_weights(self, after):
        return self.later

    def ffn_weights_arrived(self, after):
        return None

    def ffn_weights(self, after):
        return self.ffn

    def ffn_down_arrived(self, after):
        return None

    def ffn_down_weight(self, after):
        return self.ffn["ffn_w_down"]

    def ffn_grads(self, makers, after):
        self.ffn_makers = makers
        return None

    def ffn_backward_done(self, after):
        return None


def _local_step(x, posf, target, w, hooks):
    rows, d = x.shape
    width = w["ssm_d"].shape[1]
    qr, kvr = w["mla_q_norm_w"].shape[1], w["mla_kv_norm_w"].shape[1]
    heads = w["mla_w_ukv"].shape[1] // HEAD_SLOT
    f2 = w["ffn_conv_b"].shape[1]
    inp = w["w_in"].shape[0]
    groups = width // SSM_GROUP
    nb = groups // GROUPS_PER_BATCH
    scale = (QK_NOPE_DIM + QK_ROPE_DIM) ** -0.5
    g = {}

    hn = _rmsnorm_fwd(x, w["attn_norm_w"], name="attn_norm")
    proj = _matmul(hn, w["w_in"], mode="nt", name="in_proj")

    s5_weights = (w["ssm_lambda_re"], w["ssm_lambda_im"], w["ssm_log_dt"], w["ssm_b_re"], w["ssm_b_im"])
    wb, wct, abar = _s5_bands(*s5_weights, w["ssm_c_re"], w["ssm_c_im"])
    states, y_pre, yg = _s5_fwd(proj, wb, wct, w["ssm_d"], abar)
    later = hooks.mixer_weights(yg)
    z = _matmul(yg, later["ssm_w_glu"], mode="nn", name="glu_proj", bias=w["ssm_b_glu"])
    ys = _glu_norm_fwd(y_pre, z, w["ssm_out_norm_w"])

    q_col, kv_col, kpe_col = width // qr, (width + qr) // kvr, (width + qr + kvr) // LANES
    assert width % qr == 0 and (width + qr) % kvr == 0
    qn = _rmsnorm_fwd(proj, w["mla_q_norm_w"], name="q_norm", width=qr, col=q_col)
    kvn = _rmsnorm_fwd(proj, w["mla_kv_norm_w"], name="kv_norm", width=kvr, col=kv_col)
    q = _matmul(qn, w["mla_w_uq"], mode="nn", name="q_proj")
    kv = _matmul(kvn, w["mla_w_ukv"], mode="nn", name="kv_proj")
    half = QK_ROPE_DIM // 2
    inv_freq = ROPE_THETA ** (-jnp.arange(0, QK_ROPE_DIM, 2, dtype=F32) / QK_ROPE_DIM)
    zeros = jnp.zeros((LANES - QK_ROPE_DIM,), F32)
    freq = jnp.concatenate([inv_freq, inv_freq, zeros]).reshape(1, LANES)
    sign = jnp.concatenate([-jnp.ones((half,), F32), jnp.ones((half,), F32), zeros]).reshape(1, LANES)
    cos, sin = _rope_tables(posf, freq, sign)
    qc, kc, vb = _attn_prep(q, kv, proj, kpe_col, cos, sin)
    o, lse = _attn_fwd(qc, kc, vb, scale=scale, tq=ATTN_BLOCK)
    ym = _rmsnorm_fwd(o, w["mla_out_norm_w"], name="mla_out_norm")
    ycat = jnp.concatenate([ys, ym], axis=1)
    h1 = _matmul(ycat, later["w_out"], mode="nn", name="out_proj", add=x, after=hooks.ffn_weights_arrived(ycat))

    hn2 = _rmsnorm_fwd(h1, w["ffn_norm_w"], name="ffn_norm")
    ffn = hooks.ffn_weights(hn2)
    a = _matmul(hn2, ffn["ffn_w_up"], mode="nn", name="ffn_up", tm=FFN_ROWS)
    started = hooks.ffn_down_arrived(a)
    conv_b = w["ffn_conv_b"] if started is None else w["ffn_conv_b"] + started[:1, :1]
    gated = _conv_gate_fwd(a, ffn["ffn_conv_w"], conv_b)
    w_down = hooks.ffn_down_weight(gated)
    h2 = _matmul(gated, w_down, mode="nn", name="ffn_down", add=h1, tk=2816, tm=FFN_ROWS)
    loss_tile, dh2, dh2_mxu, g["final_norm_w"] = _final_norm_loss(h2, w["final_norm_w"], target)

    dgated = _matmul(dh2_mxu, w_down, mode="nt", name="ffn_down_dx", tm=FFN_ROWS)
    da, dcw, dcb = _conv_gate_bwd(a, ffn["ffn_conv_w"], w["ffn_conv_b"], dgated)
    g["ffn_conv_w"] = jnp.concatenate([dcw[0, :3], dcw[1, :3]], axis=1)
    g["ffn_conv_b"] = jnp.concatenate([dcb[0], dcb[1]], axis=1)
    started = hooks.ffn_grads({
        "ffn_w_up": _wgrad(hn2, da, d, f2, False, "ffn_up_dw", b_split=True, tn=_tile(f2 // N_CHIPS, 1408)),
        "ffn_w_down": _wgrad(gated, dh2_mxu, f2 // 2, d, True, "ffn_down_dw", tm=f2 // 2 // N_CHIPS, tn=512)}, dcb)
    dhn2 = _matmul(da, ffn["ffn_w_up"], mode="nt", name="ffn_up_dx", a_split=True, tk=_tile(f2 // 2, 2816), tm=FFN_ROWS,
                   after=started)
    dh1, dh1_mxu, g["ffn_norm_w"] = _rmsnorm_bwd(h1, w["ffn_norm_w"], dhn2, name="ffn_norm_bwd", add=dh2,
                                                dx_dtypes=(F32, BF16))

    dycat = _matmul(dh1_mxu, later["w_out"], mode="nt", name="out_proj_dx")
    g["w_out"] = _wgrad(ycat, dh1_mxu, 2 * width, d, True, "out_proj_dw")
    started = hooks.ffn_backward_done(dycat)
    mla_out_norm_w, ssm_out_norm_w = w["mla_out_norm_w"], w["ssm_out_norm_w"]
    if started is not None:
        mla_out_norm_w, ssm_out_norm_w = mla_out_norm_w + started[:1, :1], ssm_out_norm_w + started[:1, :1]

    do, g["mla_out_norm_w"] = _rmsnorm_bwd(o, mla_out_norm_w, dycat, name="mla_out_norm_bwd", width=width, dy_col=1)
    dq, dkv, dkpe_heads = _attn_bwd(qc, kc, vb, o, do, lse, cos, sin, scale=scale, tk=ATTN_BLOCK)
    dkpe = _kpe_bwd(dkpe_heads, cos, sin)
    g["mla_w_uq"] = _wgrad(qn, dq, qr, heads * HEAD_SLOT, False, "q_proj_dw")
    dqn = _matmul(dq, w["mla_w_uq"], mode="nt", name="q_proj_dx")
    dcq, g["mla_q_norm_w"] = _rmsnorm_bwd(proj, w["mla_q_norm_w"], dqn, name="q_norm_bwd", width=qr, col=q_col,
                                          dx_dtypes=(BF16,))
    g["mla_w_ukv"] = _wgrad(kvn, dkv, kvr, heads * HEAD_SLOT, False, "kv_proj_dw")
    dkvn = _matmul(dkv, w["mla_w_ukv"], mode="nt", name="kv_proj_dx")
    dckv, g["mla_kv_norm_w"] = _rmsnorm_bwd(proj, w["mla_kv_norm_w"], dkvn, name="kv_norm_bwd", width=kvr, col=kv_col,
                                            dx_dtypes=(BF16,))

    dz, dyg_a, g["ssm_out_norm_w"], g["ssm_b_glu"] = _glu_norm_bwd(y_pre, z, ssm_out_norm_w, dycat)
    dyg_b = _matmul(dz, later["ssm_w_glu"], mode="nt", name="glu_proj_dx")
    g["ssm_w_glu"] = _wgrad(yg, dz, width, width, True, "glu_proj_dw")
    du, dwb, dwct, dabar, g["ssm_d"] = _s5_bwd(proj, states, y_pre, dyg_a, dyg_b, wb, wct, w["ssm_d"], abar)
    (g["ssm_lambda_re"], g["ssm_lambda_im"], g["ssm_log_dt"], g["ssm_b_re"], g["ssm_b_im"], g["ssm_c_re"],
     g["ssm_c_im"]) = _s5_bands_bwd(*s5_weights, dwb, dwct, dabar)

    pad = jnp.zeros((rows, inp - (width + qr + kvr + LANES)), BF16)
    dproj = jnp.concatenate([du, dcq, dckv, dkpe, pad], axis=1)
    g["w_in"] = _wgrad(dproj, hn, inp, d, False, "in_proj_dw")
    dhn = _matmul(dproj, w["w_in"], mode="nn", name="in_proj_dx")
    dx, g["attn_norm_w"] = _rmsnorm_bwd(x, w["attn_norm_w"], dhn, name="attn_norm_bwd", add=dh1)
    return loss_tile, dx, g


ANY = pl.BlockSpec(memory_space=pl.ANY)
MESH = pl.DeviceIdType.MESH


def _mesh_pos():
    return lax.axis_index("x"), lax.axis_index("y"), lax.axis_index("c")


def _other_chips(x, y):
    return [(1 - x, y), (x, 1 - y), (1 - x, 1 - y)]


def _remote(src, dst, send_sems, recv_sems, k, to):
    return pltpu.make_async_remote_copy(src_ref=src, dst_ref=dst, send_sem=send_sems.at[k], recv_sem=recv_sems.at[k],
                                        device_id=to, device_id_type=MESH)


def _place_shard(shard, piece_idx, row_sharded, name, out_dtype=BF16, pieces=N_CHIPS, after=None):
    rs, cs = shard.shape
    tr = _tile(rs, 256, 2 * SUBLANES)
    rb = rs // tr
    extra = [] if after is None else [after]

    def body(p_ref, x_ref, *rest):
        o_ref = rest[-1]
        o_ref[...] = x_ref[...].astype(o_ref.dtype)

    if row_sharded:
        out_shape, out_map = (pieces * rs, cs), (lambda i, p_ref: (p_ref[0] * rb + i, 0))
    else:
        out_shape, out_map = (rs, pieces * cs), (lambda i, p_ref: (i, p_ref[0]))
    return pl.pallas_call(
        body, name=name, out_shape=jax.ShapeDtypeStruct(out_shape, out_dtype),
        grid_spec=pltpu.PrefetchScalarGridSpec(
            num_scalar_prefetch=1, grid=(rb,),
            in_specs=[pl.BlockSpec((tr, cs), lambda i, p_ref: (i, 0))] + [pl.BlockSpec(memory_space=pl.ANY)] * len(extra),
            out_specs=pl.BlockSpec((tr, cs), out_map)),
        compiler_params=_params(("parallel",)),
    )(piece_idx, shard, *extra)


def _gather_weights(placed, name):
    n = len(placed)
    meta = [(row_sharded, direct) for _, row_sharded, direct in placed]
    over_ici, over_d2d = _gather_plans(meta)
    forwarded = [t for t, (_, direct) in enumerate(meta) if not direct]

    def body(*refs):
        outs = refs[n:2 * n]
        send_sems, recv_sems, pass_send_sems, pass_recv_sems = refs[2 * n:]
        first, arrivals = over_ici(outs, send_sems, recv_sems)
        passed, passed_arrivals = over_d2d([outs[t] for t in forwarded], pass_send_sems, pass_recv_sems)
        for cp in first:
            cp.start()
        for t in range(n):
            for j in range(3):
                arrivals[3 * t + j].wait_recv()
                if t in forwarded:
                    passed[3 * forwarded.index(t) + j].start()
        for cp in passed_arrivals:
            cp.wait_recv()
        for cp in first + passed:
            cp.wait_send()

    return pl.pallas_call(
        body, name=name, in_specs=[ANY] * n, out_specs=[ANY] * n,
        out_shape=[jax.ShapeDtypeStruct(arr.shape, arr.dtype) for arr, _, _ in placed],
        input_output_aliases={t: t for t in range(n)},
        scratch_shapes=[pltpu.SemaphoreType.DMA((3 * n,)), pltpu.SemaphoreType.DMA((3 * n,)),
                        pltpu.SemaphoreType.DMA((3 * len(forwarded),)), pltpu.SemaphoreType.DMA((3 * len(forwarded),))],
    )(*[arr for arr, _, _ in placed])


def _gather_plans(meta):
    def window(ref, row_sharded, piece, half):
        r, cc = ref.shape
        if row_sharded:
            rs = r // N_CHIPS
            if half is None:
                return ref.at[pl.ds(piece * rs, rs), :]
            return ref.at[pl.ds(piece * rs + half * (rs // 2), rs // 2), :]
        cs = cc // N_CHIPS
        if half is None:
            return ref.at[:, pl.ds(piece * cs, cs)]
        return ref.at[pl.ds(half * (r // 2), r // 2), pl.ds(piece * cs, cs)]

    def over_ici(refs, send_sems, recv_sems):
        x, y, c = _mesh_pos()
        sends, recvs = [], []
        for t, (row_sharded, direct) in enumerate(meta):
            mine = window(refs[t], row_sharded, 2 * x + y, None if direct else c)
            for j, (px, py) in enumerate(_other_chips(x, y)):
                theirs = window(refs[t], row_sharded, 2 * px + py, None if direct else c)
                sends.append(_remote(mine, mine, send_sems, recv_sems, 3 * t + j, (px, py, c)))
                recvs.append(_remote(theirs, theirs, send_sems, recv_sems, 3 * t + j, (px, py, c)))
        return sends, recvs

    def over_d2d(refs, send_sems, recv_sems):
        x, y, c = _mesh_pos()
        sends, recvs = [], []
        rows = [row_sharded for row_sharded, direct in meta if not direct]
        for t, row_sharded in enumerate(rows):
            for j, (px, py) in enumerate(_other_chips(x, y)):
                got = window(refs[t], row_sharded, 2 * px + py, c)
                other = window(refs[t], row_sharded, 2 * px + py, 1 - c)
                sends.append(_remote(got, got, send_sems, recv_sems, 3 * t + j, (x, y, 1 - c)))
                recvs.append(_remote(other, other, send_sems, recv_sems, 3 * t + j, (x, y, 1 - c)))
        return sends, recvs

    return over_ici, over_d2d


HBM = pl.BlockSpec(memory_space=pltpu.HBM)
SEMAPHORES = pl.BlockSpec(memory_space=pltpu.SEMAPHORE)
DATAFLOW = pltpu.SideEffectType.DATAFLOW_SIDE_EFFECTING


def _start_copies(name, arrays, plan, n_copies, after):
    n = len(arrays)

    def body(*refs):
        sends, _ = plan(refs[:n], refs[n + 1], refs[n + 2])
        for cp in sends:
            cp.start()
        token = refs[2 * n + 3]
        token[...] = jnp.zeros_like(token)

    out = pl.pallas_call(
        body, name=name,
        out_shape=(pltpu.SemaphoreType.DMA((n_copies,)), pltpu.SemaphoreType.DMA((n_copies,)),
                   *[pltpu.HBM(a.shape, a.dtype) for a in arrays], jax.ShapeDtypeStruct((SUBLANES, LANES), F32)),
        in_specs=[HBM] * n + [ANY],
        out_specs=(SEMAPHORES, SEMAPHORES, *[HBM] * n, pl.BlockSpec(memory_space=pltpu.VMEM)),
        input_output_aliases={t: t + 2 for t in range(n)},
        compiler_params=pltpu.CompilerParams(has_side_effects=DATAFLOW),
    )(*[pltpu.with_memory_space_constraint(a, pltpu.HBM) for a in arrays], after)
    return out[0], out[1], list(out[2:2 + n]), out[2 + n]


def _wait_copies(name, started, plan, after):
    send_sems, recv_sems, arrays, _ = started
    n = len(arrays)

    def body(*refs):
        sends, recvs = plan(refs[:n], refs[n], refs[n + 1])
        for cp in sends:
            cp.wait_send()
        for cp in recvs:
            cp.wait_recv()

    out = pl.pallas_call(
        body, name=name, out_shape=[pltpu.HBM(a.shape, a.dtype) for a in arrays],
        in_specs=[HBM] * n + [SEMAPHORES, SEMAPHORES, ANY], out_specs=[HBM] * n,
        input_output_aliases={t: t for t in range(n)},
        compiler_params=pltpu.CompilerParams(has_side_effects=DATAFLOW),
    )(*arrays, send_sems, recv_sems, after)
    return list(out)


def _exchange(name, arrays, out_shapes, plan, n_copies, in_place=False, after=None):
    n = len(arrays)
    extra = [] if after is None else [after]

    def body(*refs):
        ins, outs = refs[:n], refs[n + len(extra):n + len(extra) + len(out_shapes)]
        send_sems, recv_sems = refs[n + len(extra) + len(out_shapes):]
        sends, recvs = plan(ins, outs, send_sems, recv_sems)
        for cp in sends:
            cp.start()
        for cp in recvs:
            cp.wait_recv()
        for cp in sends:
            cp.wait_send()

    return pl.pallas_call(
        body, name=name, in_specs=[ANY] * (n + len(extra)), out_specs=[ANY] * len(out_shapes), out_shape=out_shapes,
        input_output_aliases={t: t for t in range(n)} if in_place else {},
        scratch_shapes=[pltpu.SemaphoreType.DMA((n_copies,)), pltpu.SemaphoreType.DMA((n_copies,))],
    )(*arrays, *extra)


def _give_plan(n):
    def plan(refs, send_sems, recv_sems):
        x, y, c = _mesh_pos()
        sends = [_remote(refs[t], refs[n + t], send_sems, recv_sems, t, (x, y, 1 - c)) for t in range(n)]
        return sends, sends

    return plan


def _scatter_plan(n):
    def plan(refs, send_sems, recv_sems):
        x, y, c = _mesh_pos()
        sends = []
        for t in range(n):
            for j, (px, py) in enumerate(_other_chips(x, y)):
                sends.append(_remote(refs[t].at[2 * px + py], refs[n + t].at[j], send_sems, recv_sems, 3 * t + j, (px, py, c)))
        return sends, sends

    return plan


def _scatter_shapes(sums):
    return [jax.ShapeDtypeStruct((3,) + s.shape[1:], s.dtype) for s in sums]


def _join_plan(n):
    def plan(refs, send_sems, recv_sems):
        x, y, c = _mesh_pos()
        sends = [_remote(refs[t].at[c], refs[t].at[c], send_sems, recv_sems, t, (x, y, 1 - c)) for t in range(n)]
        recvs = [_remote(refs[t].at[1 - c], refs[t].at[1 - c], send_sems, recv_sems, t, (x, y, 1 - c)) for t in range(n)]
        return sends, recvs

    return plan


def _join_halves(halves, name, after=None):
    plan = _join_plan(len(halves))
    shapes = [jax.ShapeDtypeStruct(h.shape, h.dtype) for h in halves]
    return _exchange(name, halves, shapes, lambda ins, outs, s, r: plan(outs, s, r), len(halves), in_place=True, after=after)


def _add_other_half(g4, got, where, name, wire_dtype=BF16):
    _, pieces, sr, sc = g4.shape
    tr = _tile(sr, 256, 2 * SUBLANES)

    def body(w_ref, a_ref, b_ref, o_ref):
        o_ref[...] = (a_ref[...] + b_ref[...]).astype(o_ref.dtype)

    blk = pl.BlockSpec((None, tr, sc), lambda p, i, w_ref: (p, i, 0))
    return pl.pallas_call(
        body, name=name, out_shape=jax.ShapeDtypeStruct((pieces, sr, sc), wire_dtype),
        grid_spec=pltpu.PrefetchScalarGridSpec(
            num_scalar_prefetch=1, grid=(pieces, sr // tr),
            in_specs=[pl.BlockSpec((None, None, tr, sc), lambda p, i, w_ref: (w_ref[0], p, i, 0)), blk], out_specs=blk),
        compiler_params=_params(("parallel", "parallel")),
    )(where, g4, got)


def _add_pieces(sums, got_pieces, where, name):
    _, sr, sc = sums.shape
    tr = _tile(sr, 256, 2 * SUBLANES)

    def body(w_ref, a_ref, r_ref, o_ref):
        acc = a_ref[...]
        for j in range(3):
            acc = acc + r_ref[j].astype(F32)
        o_ref[...] = acc

    return pl.pallas_call(
        body, name=name, out_shape=jax.ShapeDtypeStruct((N_CORES, sr, sc), F32),
        grid_spec=pltpu.PrefetchScalarGridSpec(
            num_scalar_prefetch=1, grid=(sr // tr,),
            in_specs=[pl.BlockSpec((None, tr, sc), lambda i, w_ref: (w_ref[1], i, 0)),
                      pl.BlockSpec((3, tr, sc), lambda i, w_ref: (0, i, 0))],
            out_specs=pl.BlockSpec((None, tr, sc), lambda i, w_ref: (w_ref[0], i, 0))),
        compiler_params=_params(("parallel",)),
    )(where, sums, got_pieces)


def _adamw_update(w, g, m, v):
    nm = ADAM_B1 * m + (1.0 - ADAM_B1) * g
    nv = ADAM_B2 * v + (1.0 - ADAM_B2) * (g * g)
    m_hat = nm / (1.0 - ADAM_B1 ** ADAM_STEP)
    v_hat = nv / (1.0 - ADAM_B2 ** ADAM_STEP)
    return -ADAM_LR * (m_hat / (jnp.sqrt(v_hat) + ADAM_EPS) + ADAM_WD * w), nm, nv


def _adamw(w, g, m, v, name, after=None):
    rows, cols = w.shape
    halves = 2 if g.ndim == 3 else 1
    bc = cols // halves
    tr = _tile(rows, max(SUBLANES, (1 << 19) // max(bc, 1) // SUBLANES * SUBLANES), SUBLANES)

    def body(w_ref, g_ref, m_ref, v_ref, *rest):
        d_ref, nm_ref, nv_ref, go_ref = rest[-4:]
        gv = g_ref[...]
        d_ref[...], nm_ref[...], nv_ref[...] = _adamw_update(w_ref[...], gv, m_ref[...], v_ref[...])
        go_ref[...] = gv

    blk = pl.BlockSpec((tr, bc), lambda i, h: (i, h))
    g_blk = pl.BlockSpec((None, tr, bc), lambda i, h: (h, i, 0)) if halves == 2 else blk
    extra = [] if after is None else [after]
    return pl.pallas_call(
        body, name=name, grid=(rows // tr, halves),
        in_specs=[blk, g_blk, blk, blk] + [pl.BlockSpec(memory_space=pl.ANY)] * len(extra), out_specs=[blk] * 4,
        out_shape=[jax.ShapeDtypeStruct((rows, cols), F32)] * 4, compiler_params=_params(("parallel", "parallel")),
    )(w, g, m, v, *extra)


def _adamw_many(ws, gs, ms, vs, name):
    n = len(ws)

    def body(*refs):
        outs = refs[4 * n:]
        for k in range(n):
            w_ref, g_ref, m_ref, v_ref = (refs[j * n + k] for j in range(4))
            outs[k][...], outs[n + k][...], outs[2 * n + k][...] = _adamw_update(w_ref[...], g_ref[...], m_ref[...], v_ref[...])

    out = pl.pallas_call(
        body, name=name, out_shape=[jax.ShapeDtypeStruct(w.shape, F32) for w in ws] * 3,
        compiler_params=pltpu.CompilerParams(vmem_limit_bytes=VMEM_LIMIT_BYTES),
    )(*ws, *gs, *ms, *vs)
    return out[:n], out[n:2 * n], out[2 * n:]


WEIGHTS = ['attn_norm_w', 'w_in', 'ssm_lambda_re', 'ssm_lambda_im', 'ssm_log_dt', 'ssm_b_re', 'ssm_b_im', 'ssm_c_re',
           'ssm_c_im', 'ssm_d', 'ssm_w_glu', 'ssm_b_glu', 'mla_q_norm_w', 'mla_w_uq', 'mla_kv_norm_w', 'mla_w_ukv',
           'ssm_out_norm_w', 'mla_out_norm_w', 'w_out', 'ffn_norm_w', 'ffn_w_up', 'ffn_conv_w', 'ffn_conv_b',
           'ffn_w_down', 'final_norm_w']
SHARDED = {'w_in': False, 'ssm_w_glu': True, 'mla_w_uq': False, 'mla_w_ukv': False, 'w_out': True, 'ffn_w_up': False,
           'ffn_w_down': True}
SMALL = [n for n in WEIGHTS if n not in SHARDED and n != 'ffn_conv_w']
ROPE_PAD = HEAD_SLOT - QK_NOPE_DIM - QK_ROPE_DIM
SMALL_COLS = 8 * LANES


def _pad_heads(w_uq, heads):
    qr = w_uq.shape[0]
    w3 = w_uq.reshape(qr, heads, QK_NOPE_DIM + QK_ROPE_DIM)
    return jnp.concatenate([w3, jnp.zeros((qr, heads, ROPE_PAD), w_uq.dtype)], axis=2).reshape(qr, heads * HEAD_SLOT)


def _unpad_heads(g_uq, heads):
    qr = g_uq.shape[0]
    return g_uq.reshape(qr, heads, HEAD_SLOT)[:, :, :QK_NOPE_DIM + QK_ROPE_DIM].reshape(qr, -1)


FFN = ['ffn_w_up', 'ffn_w_down']
MIXER_LATER = ['ssm_w_glu', 'w_out']
FFN_GATHER = FFN + ['ffn_conv_w']
FFN_GATHER_META = [(SHARDED[n], False) for n in FFN] + [(False, True)]


class _Overlapped:
    def __init__(self, placed_first, first_sharding, where):
        self.where, self.mine, self.other = where, where[:1], 1 - where[:1]
        self.first_ici, self.first_d2d = _gather_plans([(r, False) for r in first_sharding])
        self.first = _start_copies("gather_first_start", placed_first, self.first_ici, 3 * len(placed_first), where)
        self.first_started = self.first[3]

    def start_rest(self, placed_later, placed):
        self.later_ici, self.later_d2d = _gather_plans([(SHARDED[n], False) for n in MIXER_LATER])
        self.later = _start_copies("gather_later_start", placed_later, self.later_ici, 3 * len(placed_later),
                                   self.first_started)
        up, down, taps = placed
        self.up_ici, self.up_d2d = _gather_plans([(SHARDED["ffn_w_up"], False), (False, True)])
        self.up = _start_copies("gather_ffn_up_start", [up, taps], self.up_ici, 6, self.later[3])
        self.down_ici, self.down_d2d = _gather_plans([(SHARDED["ffn_w_down"], False)])
        self.down = _start_copies("gather_ffn_down_start", [down], self.down_ici, 3, self.up[3])
        self.gather_started = self.down[3]
        arrived = _wait_copies("gather_first_wait", self.first, self.first_ici, self.gather_started)
        shapes = [jax.ShapeDtypeStruct(a.shape, a.dtype) for a in arrived]
        return _exchange("gather_first_pass", arrived, shapes, lambda ins, outs, s, r: self.first_d2d(outs, s, r),
                         3 * len(arrived), in_place=True)

    def mixer_weights(self, after):
        arrived = _wait_copies("gather_later_wait", self.later, self.later_ici, after)
        shapes = [jax.ShapeDtypeStruct(a.shape, a.dtype) for a in arrived]
        passed = _exchange("gather_later_pass", arrived, shapes, lambda ins, outs, s, r: self.later_d2d(outs, s, r),
                           3 * len(arrived), in_place=True)
        return dict(zip(MIXER_LATER, passed))

    def ffn_weights_arrived(self, after):
        up, self.taps = _wait_copies("gather_ffn_up_wait", self.up, self.up_ici, after)
        self.up_passing = _start_copies("gather_ffn_up_pass_start", [up], self.up_d2d, 3, after)
        return self.up_passing[3]

    def ffn_weights(self, after):
        w_up, = _wait_copies("gather_ffn_up_pass_wait", self.up_passing, self.up_d2d, after)
        return {"ffn_w_up": w_up, "ffn_conv_w": self.taps}

    def ffn_down_arrived(self, after):
        down, = _wait_copies("gather_ffn_down_wait", self.down, self.down_ici, after)
        self.down_passing = _start_copies("gather_ffn_down_pass_start", [down], self.down_d2d, 3, after)
        return self.down_passing[3]

    def ffn_down_weight(self, after):
        return _wait_copies("gather_ffn_down_pass_wait", self.down_passing, self.down_d2d, after)[0]

    def ffn_grads(self, makers, after):
        self.makers = [makers[name] for name in FFN]
        n = len(FFN)
        give = [make(self.other, suffix="_give") for make in self.makers]
        lands = [lax.empty(g.shape, g.dtype) for g in give]
        self.swap = _start_copies("grad_ffn_swap_start", give + lands, _give_plan(n), n, after)
        return self.swap[3]

    def ffn_backward_done(self, after):
        n = len(FFN)
        got = _wait_copies("grad_ffn_swap_wait", self.swap, _give_plan(n), after)[n:]
        kept = [make(self.mine, suffix="_keep", add=got[t], wire=True) for t, make in enumerate(self.makers)]
        self.sums = [k[0] for k in kept]
        wires = [k[1] for k in kept]
        lands = [lax.empty(s.shape, s.dtype) for s in _scatter_shapes(wires)]
        self.scatter = _start_copies("grad_ffn_scatter_start", wires + lands, _scatter_plan(n), 3 * n, after)
        return self.scatter[3]

    def ffn_reduced(self, after):
        n = len(FFN)
        got_pieces = _wait_copies("grad_ffn_scatter_wait", self.scatter, _scatter_plan(n), after)[n:]
        return [_add_pieces(self.sums[t], got_pieces[t], self.where, "grad_add_pieces_" + name) for t, name in enumerate(FFN)]


def _step(args):
    x, positions, target = args["x"][0], args["positions"], args["loss_target"][0]
    rows = x.shape[0]
    p = {n: args[n] for n in WEIGHTS}
    xi, yi, ci = _mesh_pos()
    piece = 2 * xi + yi

    def transposed(a):
        return jnp.swapaxes(a[0], 0, 1)

    def as_stored(n, a):
        return jnp.swapaxes(a, 2, 3) if n in ("ssm_b_re", "ssm_b_im") else a

    w_in = transposed(p["w_in"])
    in_width = w_in.shape[0]
    in_pad = (-in_width) % (2 * LANES)
    heads_here = p["mla_w_uq"].shape[2] // (QK_NOPE_DIM + QK_ROPE_DIM)
    shards = {
        "w_in": jnp.pad(w_in, ((0, in_pad), (0, 0))),
        "ssm_w_glu": p["ssm_w_glu"][0],
        "mla_w_uq": _pad_heads(p["mla_w_uq"][0], heads_here),
        "mla_w_ukv": p["mla_w_ukv"][0],
        "w_out": p["w_out"][0],
        "ffn_w_up": p["ffn_w_up"][0],
        "ffn_w_down": p["ffn_w_down"][0],
    }
    conv_w = jnp.pad(p["ffn_conv_w"][0], ((0, SUBLANES - p["ffn_conv_w"].shape[1]), (0, 0)))
    order = list(SHARDED)
    piece_idx = piece.reshape(1).astype(jnp.int32)
    mixer = [n for n in order if n not in FFN]
    first = [n for n in mixer if n not in MIXER_LATER]
    where = jnp.stack([ci, piece]).astype(jnp.int32)
    placed = {n: _place_shard(shards[n], piece_idx, SHARDED[n], "place_" + n) for n in first}
    hooks = _Overlapped([placed[n] for n in first], [SHARDED[n] for n in first], where)
    for n in order:
        if n not in first:
            placed[n] = _place_shard(shards[n], piece_idx, SHARDED[n], "place_" + n, after=hooks.first_started)
    placed["ffn_conv_w"] = _place_shard(conv_w, piece_idx, False, "place_ffn_conv_w", out_dtype=F32,
                                        after=hooks.first_started)
    w = dict(zip(first, hooks.start_rest([placed[n] for n in MIXER_LATER], [placed[n] for n in FFN_GATHER])))
    groups = p["ssm_lambda_re"].shape[1]
    w.update({
        "attn_norm_w": p["attn_norm_w"] + hooks.gather_started[:1, :1],
        "ssm_lambda_re": p["ssm_lambda_re"][0], "ssm_lambda_im": p["ssm_lambda_im"][0],
        "ssm_log_dt": p["ssm_log_dt"].reshape(groups, 1), "ssm_b_re": as_stored("ssm_b_re", p["ssm_b_re"])[0],
        "ssm_b_im": as_stored("ssm_b_im", p["ssm_b_im"])[0], "ssm_c_re": p["ssm_c_re"][0], "ssm_c_im": p["ssm_c_im"][0],
        "ssm_d": p["ssm_d"], "ssm_b_glu": p["ssm_b_glu"], "mla_q_norm_w": p["mla_q_norm_w"],
        "mla_kv_norm_w": p["mla_kv_norm_w"], "ssm_out_norm_w": p["ssm_out_norm_w"], "mla_out_norm_w": p["mla_out_norm_w"],
        "ffn_norm_w": p["ffn_norm_w"], "ffn_conv_b": p["ffn_conv_b"], "final_norm_w": p["final_norm_w"].reshape(1, -1),
    })

    loss_tile, dx, g = _local_step(x, positions.reshape(rows, 1).astype(F32), target, w, hooks)

    flat = [g[n].reshape(-1) for n in SMALL] + [g["ffn_conv_w"].reshape(-1), loss_tile[0, :1]]
    sizes = [f.shape[0] for f in flat]
    per_block = -(-sum(sizes) // (N_CORES * N_CHIPS * SMALL_COLS))
    small_rows = -(-per_block // (2 * SUBLANES)) * (2 * SUBLANES)
    padded = N_CORES * N_CHIPS * small_rows * SMALL_COLS

    def pack(parts):
        parts = list(parts)
        have = sum(q.shape[0] for q in parts)
        return jnp.concatenate(parts + [jnp.zeros((padded - have,), F32)])

    reduced = mixer + ["small"]
    small = pack(flat).reshape(N_CORES, N_CHIPS, small_rows, SMALL_COLS)
    give = [g[n](hooks.other, suffix="_give") for n in mixer] + [lax.dynamic_index_in_dim(small, 1 - ci, 0, keepdims=False)]
    lands = [lax.empty(a.shape, a.dtype) for a in give]
    give_plan = _give_plan(len(reduced))
    swap = _start_copies("grad_mixer_swap_start", give + lands, give_plan, len(reduced), dx)

    grads, delta, new_m, new_v = {}, {}, {}, {}

    def finish(n, joined, after=None):
        grad = joined if SHARDED[n] else joined.reshape(-1, joined.shape[2])
        if n == "w_in":
            wt, mt, vt = w_in, transposed(args["m_w_in"]), transposed(args["v_w_in"])
            out = _adamw(wt, grad, mt, vt, "adamw_w_in")
            delta[n], new_m[n], new_v[n], grads[n] = (jnp.swapaxes(a, 0, 1)[None] for a in out)
            return
        if n == "mla_w_uq":
            grad = _unpad_heads(grad, heads_here)
        adam(n, grad, after)

    def adam(n, grad, after=None):
        shape = p[n].shape
        out = _adamw(p[n].reshape(shape[1:]), grad, args["m_" + n].reshape(shape[1:]),
                     args["v_" + n].reshape(shape[1:]), "adamw_" + n, after)
        delta[n], new_m[n], new_v[n], grads[n] = (a.reshape(shape) for a in out)

    ffn_halves = hooks.ffn_reduced(swap[3])
    got = _wait_copies("grad_mixer_swap_wait", swap, give_plan, ffn_halves[-1])[len(reduced):]
    join_plan = _join_plan(len(FFN))
    ffn_join = _start_copies("grad_ffn_join_start", ffn_halves, join_plan, len(FFN), got[0])
    kept = [g[n](hooks.mine, suffix="_keep", add=got[t], wire=True) for t, n in enumerate(mixer)]
    small_sum = _add_other_half(small, got[-1], where, "grad_add_half_small", F32)
    sums = [k[0] for k in kept] + [small_sum]
    wires = [k[1] for k in kept] + [small_sum]
    ffn_joined = _wait_copies("grad_ffn_join_wait", ffn_join, join_plan, kept[-1][0])
    lands = [lax.empty(s.shape, s.dtype) for s in _scatter_shapes(wires)]
    scatter_plan = _scatter_plan(len(reduced))
    scatter = _start_copies("grad_mixer_scatter_start", wires + lands, scatter_plan, 3 * len(reduced), ffn_joined[0])
    behind = scatter[3]
    for n, joined in zip(FFN, ffn_joined):
        finish(n, joined, after=behind)
        behind = delta[n]
    got_pieces = _wait_copies("grad_mixer_scatter_wait", scatter, scatter_plan, delta[FFN[-1]])[len(reduced):]
    halves = [_add_pieces(sums[t], got_pieces[t], where, "grad_add_pieces_" + n) for t, n in enumerate(reduced)]
    joined = _join_halves(halves, "grad_join_halves")
    for n, j in zip(mixer, joined):
        finish(n, j)
    eighths = _place_shard(joined[-1].reshape(N_CORES * small_rows, SMALL_COLS), piece_idx, True, "place_small_grads",
                           out_dtype=F32)
    small_sum = _gather_weights([(eighths, True, False)], "gather_small_grads")[0]
    flat_sum = small_sum.reshape(N_CHIPS, N_CORES, small_rows * SMALL_COLS).transpose(1, 0, 2).reshape(-1)
    offs = [0]
    for s in sizes:
        offs.append(offs[-1] + s)
    stored = {n: as_stored(n, p[n]) for n in SMALL}
    for k, n in enumerate(SMALL):
        grads[n] = flat_sum[offs[k]:offs[k + 1]].reshape(stored[n].shape)
    taps, cols_here = p["ffn_conv_w"].shape[1], p["ffn_conv_w"].shape[2]
    conv_full = flat_sum[offs[len(SMALL)]:offs[len(SMALL) + 1]].reshape(taps, N_CHIPS * cols_here)
    adam("ffn_conv_w", lax.dynamic_slice_in_dim(conv_full, piece * cols_here, cols_here, axis=1))
    loss = flat_sum[offs[len(SMALL) + 1]]

    def rank2(a):
        return a.reshape(1, -1) if a.ndim == 1 else a

    d_s, m_s, v_s = _adamw_many([rank2(stored[n]) for n in SMALL], [rank2(grads[n]) for n in SMALL],
                                [rank2(as_stored(n, args["m_" + n])) for n in SMALL],
                                [rank2(as_stored(n, args["v_" + n])) for n in SMALL], "adamw_small")
    for k, n in enumerate(SMALL):
        delta[n], new_m[n], new_v[n], grads[n] = (as_stored(n, a.reshape(stored[n].shape))
                                                  for a in (d_s[k], m_s[k], v_s[k], grads[n]))

    return (loss, dx[None], *[grads[n] for n in WEIGHTS], *[delta[n] for n in WEIGHTS],
            *[new_m[n] for n in WEIGHTS], *[new_v[n] for n in WEIGHTS])


def kernel(x, positions, attn_norm_w, w_in, ssm_lambda_re, ssm_lambda_im, ssm_log_dt, ssm_b_re, ssm_b_im, ssm_c_re, ssm_c_im, ssm_d, ssm_w_glu, ssm_b_glu, mla_q_norm_w, mla_w_uq, mla_kv_norm_w, mla_w_ukv, ssm_out_norm_w, mla_out_norm_w, w_out, ffn_norm_w, ffn_w_up, ffn_conv_w, ffn_conv_b, ffn_w_down, final_norm_w, loss_target, m_attn_norm_w, m_w_in, m_ssm_lambda_re, m_ssm_lambda_im, m_ssm_log_dt, m_ssm_b_re, m_ssm_b_im, m_ssm_c_re, m_ssm_c_im, m_ssm_d, m_ssm_w_glu, m_ssm_b_glu, m_mla_q_norm_w, m_mla_w_uq, m_mla_kv_norm_w, m_mla_w_ukv, m_ssm_out_norm_w, m_mla_out_norm_w, m_w_out, m_ffn_norm_w, m_ffn_w_up, m_ffn_conv_w, m_ffn_conv_b, m_ffn_w_down, m_final_norm_w, v_attn_norm_w, v_w_in, v_ssm_lambda_re, v_ssm_lambda_im, v_ssm_log_dt, v_ssm_b_re, v_ssm_b_im, v_ssm_c_re, v_ssm_c_im, v_ssm_d, v_ssm_w_glu, v_ssm_b_glu, v_mla_q_norm_w, v_mla_w_uq, v_mla_kv_norm_w, v_mla_w_ukv, v_ssm_out_norm_w, v_mla_out_norm_w, v_w_out, v_ffn_norm_w, v_ffn_w_up, v_ffn_conv_w, v_ffn_conv_b, v_ffn_w_down, v_final_norm_w):
    return _step(dict(locals()))
```

```python
import functools
import math

import jax
import jax.numpy as jnp
from jax import lax
from jax.experimental import pallas as pl
from jax.experimental.pallas import tpu as pltpu

F32 = jnp.float32
BF16 = jnp.bfloat16

SSM_GROUP = 16
SSM_STATE = 64
QK_NOPE_DIM = 128
QK_ROPE_DIM = 64
V_HEAD_DIM = 128
ROPE_THETA = 10000.0
RMS_EPS = 1e-6
ADAM_LR, ADAM_B1, ADAM_B2, ADAM_EPS, ADAM_WD, ADAM_STEP = 0.001, 0.9, 0.999, 1e-08, 0.01, 10

LANES = 128
SUBLANES = 8
VMEM_LIMIT_BYTES = 56 * 1024 * 1024

GROUPS_PER_BATCH = LANES // SSM_GROUP
STATE_PER_BATCH = GROUPS_PER_BATCH * SSM_STATE
HEAD_SLOT = 2 * LANES
NEG_INF = -1e30
ATTN_BLOCK = 512
FFN_ROWS = 1024

N_CHIPS = 4
N_CORES = 2


def _tile(n, pref, align=LANES):
    if n <= pref:
        return n
    t = (pref // align) * align
    while t >= align:
        if n % t == 0:
            return t
        t -= align
    return n


def _params(sem):
    return pltpu.CompilerParams(dimension_semantics=sem, vmem_limit_bytes=VMEM_LIMIT_BYTES)


def _dot(a, b, dims):
    return lax.dot_general(a, b, (dims, ((), ())), preferred_element_type=F32)


def _dot_nn(a, b):
    return _dot(a, b, ((1,), (0,)))


def _dot_nt(a, b):
    return _dot(a, b, ((1,), (1,)))


def _dot_tn(a, b):
    return _dot(a, b, ((0,), (0,)))


def _matmul(a, b, *, mode, name, tm=512, tn=1024, tk=2048, bias=None, add=None, out_dtype=F32,
            out_blocks=None, a_split=False, b_split=False, after=None):
    if a_split:
        assert mode == "nt"
        a_shape = (a.shape[1], 2 * a.shape[2])
    else:
        a_shape = a.shape
    if b_split:
        assert mode == "tn"
        b_shape = (b.shape[1], 2 * b.shape[2])
    else:
        b_shape = b.shape
    if mode == "nn":
        (m, k), (k2, n) = a_shape, b_shape
    elif mode == "nt":
        (m, k), (n, k2) = a_shape, b_shape
    else:
        (k, m), (k2, n) = a_shape, b_shape
    assert k == k2, (a.shape, b.shape, mode)
    tm, tn, tk = _tile(m, tm, SUBLANES), _tile(n, tn), _tile(k, tk)
    nk = k // tk
    a_spec = {"nn": pl.BlockSpec((tm, tk), lambda i, j, kk: (i, kk)),
              "nt": pl.BlockSpec((tm, tk), lambda i, j, kk: (i, kk)),
              "tn": pl.BlockSpec((tk, tm), lambda i, j, kk: (kk, i))}[mode]
    b_spec = {"nn": pl.BlockSpec((tk, tn), lambda i, j, kk: (kk, j)),
              "nt": pl.BlockSpec((tn, tk), lambda i, j, kk: (j, kk)),
              "tn": pl.BlockSpec((tk, tn), lambda i, j, kk: (kk, j))}[mode]
    if a_split:
        kb = a.shape[2] // tk
        assert a.shape[2] % tk == 0
        a_spec = pl.BlockSpec((None, tm, tk), lambda i, j, kk: (kk // kb, i, kk % kb))
    if b_split:
        nb = b.shape[2] // tn
        assert b.shape[2] % tn == 0
        b_spec = pl.BlockSpec((None, tk, tn), lambda i, j, kk: (j // nb, kk, j % nb))
    dot = {"nn": _dot_nn, "nt": _dot_nt, "tn": _dot_tn}[mode]
    in_specs, operands = [a_spec, b_spec], [a, b]
    if bias is not None:
        in_specs.append(pl.BlockSpec((1, tn), lambda i, j, kk: (0, j)))
        operands.append(bias)
    if add is not None:
        in_specs.append(pl.BlockSpec((tm, tn), lambda i, j, kk: (i, j)))
        operands.append(add)
    if after is not None:
        in_specs.append(pl.BlockSpec(memory_space=pl.ANY))
        operands.append(after)

    def body(*refs):
        a_ref, b_ref = refs[0], refs[1]
        rest = list(refs[2:])
        bias_ref = rest.pop(0) if bias is not None else None
        add_ref = rest.pop(0) if add is not None else None
        if after is not None:
            rest.pop(0)
        o_ref, acc_ref = rest

        def finish(acc):
            if bias_ref is not None:
                acc = acc + bias_ref[...]
            if add_ref is not None:
                acc = acc + add_ref[...]
            o_ref[...] = acc.astype(o_ref.dtype)

        part = dot(a_ref[...].astype(BF16), b_ref[...].astype(BF16))
        if nk == 1:
            finish(part)
        else:
            kk = pl.program_id(2)

            @pl.when(kk == 0)
            def _():
                acc_ref[...] = part

            @pl.when(jnp.logical_and(kk > 0, kk < nk - 1))
            def _():
                acc_ref[...] += part

            @pl.when(kk == nk - 1)
            def _():
                finish(acc_ref[...] + part)

    if out_blocks is None:
        out_shape = jax.ShapeDtypeStruct((m, n), out_dtype)
        out_spec = pl.BlockSpec((tm, tn), lambda i, j, kk: (i, j))
    else:
        shape, block, index_map = out_blocks(tm, tn)
        out_shape = jax.ShapeDtypeStruct(shape, out_dtype)
        out_spec = pl.BlockSpec(block, index_map)
    acc_shape = (tm, tn) if nk > 1 else (SUBLANES, LANES)
    return pl.pallas_call(
        body, name=name, grid=(m // tm, n // tn, nk), in_specs=in_specs, out_specs=out_spec, out_shape=out_shape,
        scratch_shapes=[pltpu.VMEM(acc_shape, F32)],
        compiler_params=_params(("parallel", "parallel", "arbitrary")),
    )(*operands)


def _wgrad_half(a, b, rows, cols, row_sharded, name, which, *, suffix="", add=None, wire=False, tm=None, tn=None,
                b_split=False, a_transposed=False):
    tokens = a.shape[1] if a_transposed else a.shape[0]
    if row_sharded:
        sr, sc = rows // N_CHIPS, cols // N_CORES
    else:
        sr, sc = rows // N_CORES, cols // N_CHIPS
    tm = _tile(sr, 512) if tm is None else tm
    tn = _tile(sc, 1024) if tn is None else tn
    assert sr % tm == 0 and sc % tn == 0, (rows, cols, tm, tn)
    rb, cb = sr // tm, sc // tn
    if tn >= tm:
        ij, grid = (lambda s, t: (t, s)), (N_CHIPS, cb, rb)
    else:
        ij, grid = (lambda s, t: (s, t)), (N_CHIPS, rb, cb)
    if row_sharded:
        a_tile = lambda p, i, j, h: p * rb + i
        b_tile = lambda p, i, j, h: h[0] * cb + j
    else:
        a_tile = lambda p, i, j, h: h[0] * rb + i
        b_tile = lambda p, i, j, h: p * cb + j
    if a_transposed:
        a_spec = pl.BlockSpec((tm, tokens), lambda p, s, t, h: (a_tile(p, *ij(s, t), h), 0))
    else:
        a_spec = pl.BlockSpec((tokens, tm), lambda p, s, t, h: (0, a_tile(p, *ij(s, t), h)))
    if b_split:
        nbh = b.shape[2] // tn
        assert b.shape[2] % tn == 0
        b_spec = pl.BlockSpec((None, tokens, tn), lambda p, s, t, h: (b_tile(p, *ij(s, t), h) // nbh, 0,
                                                                       b_tile(p, *ij(s, t), h) % nbh))
    else:
        b_spec = pl.BlockSpec((tokens, tn), lambda p, s, t, h: (0, b_tile(p, *ij(s, t), h)))
    out_spec = pl.BlockSpec((None, tm, tn), lambda p, s, t, h: (p, *ij(s, t)))
    in_specs, operands = [a_spec, b_spec], [a, b]
    if add is not None:
        in_specs.append(out_spec)
        operands.append(add)

    def body(h_ref, a_ref, b_ref, *rest):
        acc = (_dot_nn if a_transposed else _dot_tn)(a_ref[...].astype(BF16), b_ref[...].astype(BF16))
        if add is not None:
            acc = acc + rest[0][...]
        for o_ref in rest[1 if add is not None else 0:]:
            o_ref[...] = acc.astype(o_ref.dtype)

    out_dtypes = [F32, BF16] if wire else [F32]
    out = pl.pallas_call(
        body, name=name + suffix, out_shape=[jax.ShapeDtypeStruct((N_CHIPS, sr, sc), dt) for dt in out_dtypes],
        grid_spec=pltpu.PrefetchScalarGridSpec(num_scalar_prefetch=1, grid=grid, in_specs=in_specs,
                                               out_specs=[out_spec] * len(out_dtypes)),
        compiler_params=_params(("parallel", "parallel", "parallel")),
    )(which, *operands)
    return tuple(out) if wire else out[0]


def _rms_rows(x):
    return lax.rsqrt(jnp.mean(x * x, axis=-1, keepdims=True) + RMS_EPS)


def _rmsnorm_fwd(x, w, *, name, width=None, col=0, out_dtype=BF16, tr=256, transposed_too=False):
    rows = x.shape[0]
    width = x.shape[1] if width is None else width
    tr = _tile(rows, tr, SUBLANES)

    def body(x_ref, w_ref, o_ref, *t_ref):
        xv = x_ref[...]
        out = (xv * _rms_rows(xv) * w_ref[...]).astype(o_ref.dtype)
        o_ref[...] = out
        if transposed_too:
            t_ref[0][...] = out.T

    out_specs = [pl.BlockSpec((tr, width), lambda i: (i, 0))]
    out_shape = [jax.ShapeDtypeStruct((rows, width), out_dtype)]
    if transposed_too:
        out_specs.append(pl.BlockSpec((width, tr), lambda i: (0, i)))
        out_shape.append(jax.ShapeDtypeStruct((width, rows), out_dtype))
    out = pl.pallas_call(
        body, name=name, grid=(rows // tr,),
        in_specs=[pl.BlockSpec((tr, width), lambda i: (i, col)), pl.BlockSpec((1, width), lambda i: (0, 0))],
        out_specs=out_specs, out_shape=out_shape, compiler_params=_params(("parallel",)),
    )(x, w)
    return tuple(out) if transposed_too else out[0]


def _rmsnorm_bwd_rows(xv, w, dy):
    r = _rms_rows(xv)
    n = xv * r
    dn = dy * w
    dx = r * (dn - n * jnp.mean(dn * n, axis=-1, keepdims=True))
    return dx, dy * n


def _rmsnorm_bwd(x, w, dy, *, name, width=None, col=0, dy_col=0, add=None, tr=256, dx_dtypes=(F32,)):
    rows = x.shape[0]
    n_dx = len(dx_dtypes)
    width = x.shape[1] if width is None else width
    tr = _tile(rows, tr, SUBLANES)
    in_specs = [pl.BlockSpec((tr, width), lambda i: (i, col)), pl.BlockSpec((1, width), lambda i: (0, 0)),
                pl.BlockSpec((tr, width), lambda i: (i, dy_col))]
    operands = [x, w, dy]
    if add is not None:
        in_specs.append(pl.BlockSpec((tr, width), lambda i: (i, 0)))
        operands.append(add)

    def body(*refs):
        x_ref, w_ref, dy_ref = refs[:3]
        add_ref = refs[3] if add is not None else None
        dx_refs, dw_ref = refs[-1 - n_dx:-1], refs[-1]
        dx, dwp = _rmsnorm_bwd_rows(x_ref[...], w_ref[...], dy_ref[...])
        if add_ref is not None:
            dx = dx + add_ref[...]
        for dx_ref in dx_refs:
            dx_ref[...] = dx.astype(dx_ref.dtype)
        part = jnp.sum(dwp, axis=0, keepdims=True)

        @pl.when(pl.program_id(0) == 0)
        def _():
            dw_ref[...] = part

        @pl.when(pl.program_id(0) > 0)
        def _():
            dw_ref[...] += part

    return pl.pallas_call(
        body, name=name, grid=(rows // tr,), in_specs=in_specs,
        out_specs=[pl.BlockSpec((tr, width), lambda i: (i, 0))] * n_dx + [pl.BlockSpec((1, width), lambda i: (0, 0))],
        out_shape=[jax.ShapeDtypeStruct((rows, width), dt) for dt in dx_dtypes] + [jax.ShapeDtypeStruct((1, width), F32)],
        compiler_params=_params(("arbitrary",)),
    )(*operands)


def _final_norm_loss(h, w, target, *, tr=256):
    rows, d = h.shape
    tr = _tile(rows, tr, SUBLANES)

    def body(h_ref, w_ref, t_ref, loss_ref, dh_ref, dhb_ref, dw_ref):
        hv, wv = h_ref[...], w_ref[...]
        r = _rms_rows(hv)
        n = hv * r
        err = n * wv - t_ref[...]
        d_out = err * (1.0 / d)
        dn = d_out * wv
        dh = r * (dn - n * jnp.mean(dn * n, axis=-1, keepdims=True))
        dh_ref[...] = dh
        dhb_ref[...] = dh.astype(BF16)
        dw_part = jnp.sum(d_out * n, axis=0, keepdims=True)
        loss_part = jnp.full((SUBLANES, LANES), 0.5 / d, F32) * jnp.sum(err * err)

        @pl.when(pl.program_id(0) == 0)
        def _():
            dw_ref[...] = dw_part
            loss_ref[...] = loss_part

        @pl.when(pl.program_id(0) > 0)
        def _():
            dw_ref[...] += dw_part
            loss_ref[...] += loss_part

    return pl.pallas_call(
        body, name="final_norm_loss", grid=(rows // tr,),
        in_specs=[pl.BlockSpec((tr, d), lambda i: (i, 0)), pl.BlockSpec((1, d), lambda i: (0, 0)),
                  pl.BlockSpec((tr, d), lambda i: (i, 0))],
        out_specs=[pl.BlockSpec((SUBLANES, LANES), lambda i: (0, 0)), pl.BlockSpec((tr, d), lambda i: (i, 0)),
                   pl.BlockSpec((tr, d), lambda i: (i, 0)), pl.BlockSpec((1, d), lambda i: (0, 0))],
        out_shape=[jax.ShapeDtypeStruct((SUBLANES, LANES), F32), jax.ShapeDtypeStruct((rows, d), F32),
                   jax.ShapeDtypeStruct((rows, d), BF16), jax.ShapeDtypeStruct((1, d), F32)],
        compiler_params=_params(("arbitrary",)),
    )(h, w, target)


def _cmul(ar, ai, br, bi):
    return ar * br - ai * bi, ar * bi + ai * br


def _dot_exact(a, b, dims):
    return lax.dot_general(a, b, (dims, ((), ())), preferred_element_type=F32, precision=lax.Precision.HIGHEST)


def _s5_discretize(lr, li, dt):
    mag = jnp.exp(lr * dt)
    th = li * dt
    ar, ai = mag * jnp.cos(th), mag * jnp.sin(th)
    nr, ni = ar - 1.0, ai
    den = lr * lr + li * li
    zr = (nr * lr + ni * li) / den
    zi = (ni * lr - nr * li) / den
    return mag, ar, ai, nr, ni, den, zr, zi


def _band_slices(group):
    j, gi = divmod(group, GROUPS_PER_BATCH)
    rows = slice(gi * SSM_GROUP, (gi + 1) * SSM_GROUP)
    re = slice(gi * SSM_STATE, (gi + 1) * SSM_STATE)
    im = slice(STATE_PER_BATCH + gi * SSM_STATE, STATE_PER_BATCH + (gi + 1) * SSM_STATE)
    return j, rows, re, im


def _s5_bands(lam_re, lam_im, log_dt, b_re, b_im, c_re, c_im):
    g, _ = lam_re.shape
    nb = g // GROUPS_PER_BATCH
    s2 = 2 * STATE_PER_BATCH

    def body(lr_ref, li_ref, ldt_ref, br_ref, bi_ref, cr_ref, ci_ref, wb_ref, wct_ref, a_ref):
        dt = jnp.exp(ldt_ref[...])
        _, ar, ai, _, _, _, zr, zi = _s5_discretize(lr_ref[...], li_ref[...], dt)
        wb_ref[...] = jnp.zeros_like(wb_ref)
        wct_ref[...] = jnp.zeros_like(wct_ref)
        for group in range(g):
            j, rows, re, im = _band_slices(group)
            zr_g, zi_g = zr[group:group + 1, :], zi[group:group + 1, :]
            bre, bim = br_ref[group], bi_ref[group]
            wb_ref[j, rows, re] = (zr_g * bre - zi_g * bim).astype(BF16)
            wb_ref[j, rows, im] = (zr_g * bim + zi_g * bre).astype(BF16)
            wct_ref[j, rows, re] = cr_ref[group].astype(BF16)
            wct_ref[j, rows, im] = (-ci_ref[group]).astype(BF16)
            a_ref[j, :, re] = ar[group:group + 1, :]
            a_ref[j, :, im] = ai[group:group + 1, :]

    return pl.pallas_call(
        body, name="s5_bands",
        out_shape=[jax.ShapeDtypeStruct((nb, LANES, s2), BF16)] * 2 + [jax.ShapeDtypeStruct((nb, 1, s2), F32)],
    )(lam_re, lam_im, log_dt, b_re, b_im, c_re, c_im)


def _s5_bands_bwd(lam_re, lam_im, log_dt, b_re, b_im, dwb, dwct, dabar):
    g, p = lam_re.shape
    gh = b_re.shape[1:]

    def body(lr_ref, li_ref, ldt_ref, br_ref, bi_ref, dwb_ref, dwct_ref, da_ref,
             dlr_ref, dli_ref, dldt_ref, dbre_ref, dbim_ref, dcre_ref, dcim_ref, dzr_ref, dzi_ref, dar_ref, dai_ref):
        lr, li = lr_ref[...], li_ref[...]
        dt = jnp.exp(ldt_ref[...])
        mag, ar, ai, nr, ni, den, zr, zi = _s5_discretize(lr, li, dt)
        for group in range(g):
            j, rows, re, im = _band_slices(group)
            zr_g, zi_g = zr[group:group + 1, :], zi[group:group + 1, :]
            bre, bim = br_ref[group], bi_ref[group]
            dbr, dbi = dwb_ref[j, rows, re], dwb_ref[j, rows, im]
            dbre_ref[group] = zr_g * dbr + zi_g * dbi
            dbim_ref[group] = zr_g * dbi - zi_g * dbr
            dzr_ref[group:group + 1, :] = jnp.sum(bre * dbr + bim * dbi, axis=0, keepdims=True)
            dzi_ref[group:group + 1, :] = jnp.sum(bre * dbi - bim * dbr, axis=0, keepdims=True)
            dcre_ref[group] = dwct_ref[j, rows, re]
            dcim_ref[group] = -dwct_ref[j, rows, im]
            dar_ref[group:group + 1, :] = da_ref[j, :, re]
            dai_ref[group:group + 1, :] = da_ref[j, :, im]
        dzr, dzi = dzr_ref[...], dzi_ref[...]
        inv = 1.0 / den
        d_nr = (dzr * lr - dzi * li) * inv
        d_ni = (dzr * li + dzi * lr) * inv
        d_den = -(dzr * zr + dzi * zi) * inv
        d_lr = (dzr * nr + dzi * ni) * inv + 2.0 * lr * d_den
        d_li = (dzr * ni - dzi * nr) * inv + 2.0 * li * d_den
        t_ar = dar_ref[...] + d_nr
        t_ai = dai_ref[...] + d_ni
        d_lrdt = t_ar * ar + t_ai * ai
        d_th = t_ai * ar - t_ar * ai
        dlr_ref[...] = d_lr + d_lrdt * dt
        dli_ref[...] = d_li + d_th * dt
        dldt_ref[...] = jnp.sum(d_lrdt * lr + d_th * li, axis=1, keepdims=True) * dt

    return pl.pallas_call(
        body, name="s5_bands_bwd",
        out_shape=[jax.ShapeDtypeStruct((g, p), F32)] * 2 + [jax.ShapeDtypeStruct((g, 1), F32)]
        + [jax.ShapeDtypeStruct((g,) + gh, F32)] * 4,
        scratch_shapes=[pltpu.VMEM((g, p), F32)] * 4,
    )(lam_re, lam_im, log_dt, b_re, b_im, dwb, dwct, dabar)


def _powers(ar, ai, count):
    out = [(ar, ai)]
    for _ in range(count - 1):
        out.append(_cmul(out[-1][0], out[-1][1], ar, ai))
    return out


def _scan_coefs(ar, ai, reverse):
    w = ar.shape[-1]
    pw = _powers(ar, ai, SUBLANES)
    row = lax.broadcasted_iota(jnp.int32, (SUBLANES, w), 0)
    steps = []
    d = 1
    while d < SUBLANES:
        keep = (row < SUBLANES - d) if reverse else (row >= d)
        pr, pi = pw[d - 1]
        steps.append((d, jnp.where(keep, pr, 0.0), jnp.where(keep, pi, 0.0)))
        d *= 2
    cr = jnp.zeros((SUBLANES, w), F32)
    ci = jnp.zeros((SUBLANES, w), F32)
    for t in range(SUBLANES):
        pr, pi = pw[SUBLANES - 1 - t] if reverse else pw[t]
        cr = jnp.where(row == t, pr, cr)
        ci = jnp.where(row == t, pi, ci)
    return steps, cr, ci


def _scan_tile(xr, xi, carry_r, carry_i, coefs, reverse):
    steps, cr, ci = coefs
    for d, mr, mi in steps:
        shift = SUBLANES - d if reverse else d
        sr, si = pltpu.roll(xr, shift, 0), pltpu.roll(xi, shift, 0)
        pr, pi = _cmul(mr, mi, sr, si)
        xr, xi = xr + pr, xi + pi
    pr, pi = _cmul(cr, ci, carry_r, carry_i)
    return xr + pr, xi + pi


def _gelu(x):
    c = math.sqrt(2.0 / math.pi)
    return 0.5 * x * (1.0 + jnp.tanh(c * (x + 0.044715 * x * x * x)))


def _gelu_grad(x):
    c = math.sqrt(2.0 / math.pi)
    t = jnp.tanh(c * (x + 0.044715 * x * x * x))
    return 0.5 * (1.0 + t) + 0.5 * x * (1.0 - t * t) * c * (1.0 + 3.0 * 0.044715 * x * x)


def _s5_fwd(proj, wb, wct, d_skip, abar):
    rows = proj.shape[0]
    nb = wb.shape[0]
    s2 = 2 * STATE_PER_BATCH
    st = STATE_PER_BATCH
    chunk = _tile(rows, 512, SUBLANES)

    def body(u_ref, wb_ref, wc_ref, d_ref, a_ref, s_ref, y_ref, yg_ref):
        for c0 in range(0, rows, chunk):
            s_ref[pl.ds(c0, chunk), :] = _dot_nn(u_ref[pl.ds(c0, chunk), :].astype(BF16), wb_ref[...])
        av = a_ref[...]
        coefs = _scan_coefs(av[:, :st], av[:, st:], reverse=False)

        def tile(b, carry):
            r0 = pl.multiple_of(b * SUBLANES, SUBLANES)
            xr, xi = _scan_tile(s_ref[pl.ds(r0, SUBLANES), :st], s_ref[pl.ds(r0, SUBLANES), st:], carry[0], carry[1],
                                coefs, False)
            s_ref[pl.ds(r0, SUBLANES), :st] = xr
            s_ref[pl.ds(r0, SUBLANES), st:] = xi
            return xr[SUBLANES - 1:, :], xi[SUBLANES - 1:, :]

        zero = jnp.zeros((1, st), F32)
        lax.fori_loop(0, rows // SUBLANES, tile, (zero, zero))
        for c0 in range(0, rows, chunk):
            y = _dot_nt(s_ref[pl.ds(c0, chunk), :].astype(BF16), wc_ref[...]) + d_ref[...] * u_ref[pl.ds(c0, chunk), :]
            y_ref[pl.ds(c0, chunk), :] = y
            yg_ref[pl.ds(c0, chunk), :] = _gelu(y).astype(BF16)

    return pl.pallas_call(
        body, name="s5_fwd", grid=(nb,),
        in_specs=[pl.BlockSpec((rows, LANES), lambda j: (0, j)), pl.BlockSpec((None, LANES, s2), lambda j: (j, 0, 0)),
                  pl.BlockSpec((None, LANES, s2), lambda j: (j, 0, 0)), pl.BlockSpec((1, LANES), lambda j: (0, j)),
                  pl.BlockSpec((None, 1, s2), lambda j: (j, 0, 0))],
        out_specs=[pl.BlockSpec((rows, s2), lambda j: (0, j)), pl.BlockSpec((rows, LANES), lambda j: (0, j)),
                   pl.BlockSpec((rows, LANES), lambda j: (0, j))],
        out_shape=[jax.ShapeDtypeStruct((rows, nb * s2), F32), jax.ShapeDtypeStruct((rows, nb * LANES), F32),
                   jax.ShapeDtypeStruct((rows, nb * LANES), BF16)],
        compiler_params=_params(("parallel",)),
    )(proj, wb, wct, d_skip, abar)


def _s5_bwd(proj, states, y_pre, dyg_a, dyg_b, wb, wct, d_skip, abar):
    rows = proj.shape[0]
    nb = wb.shape[0]
    s2 = 2 * STATE_PER_BATCH
    st = STATE_PER_BATCH
    chunk = _tile(rows, 512, SUBLANES)
    n_tiles = rows // SUBLANES

    def body(u_ref, s_ref, y_ref, ga_ref, gb_ref, wb_ref, wc_ref, d_ref, a_ref,
             du_ref, dwb_ref, dwc_ref, da_ref, dd_ref, ds_ref, dy_ref):
        dy_ref[...] = (ga_ref[...] + gb_ref[...]) * _gelu_grad(y_ref[...])
        dd_ref[...] = jnp.sum(dy_ref[...] * u_ref[...], axis=0, keepdims=True)
        for c0 in range(0, rows, chunk):
            ds_ref[pl.ds(c0, chunk), :] = _dot_nn(dy_ref[pl.ds(c0, chunk), :].astype(BF16), wc_ref[...])
        dwc_ref[...] = _dot_tn(dy_ref[...].astype(BF16), s_ref[...].astype(BF16))
        av = a_ref[...]
        coefs = _scan_coefs(av[:, :st], -av[:, st:], reverse=True)
        row = lax.broadcasted_iota(jnp.int32, (SUBLANES, st), 0)

        def tile(k, carry):
            cr, ci, acc_r, acc_i = carry
            b = n_tiles - 1 - k
            r0 = pl.multiple_of(b * SUBLANES, SUBLANES)
            rp = pl.multiple_of(jnp.maximum(b - 1, 0) * SUBLANES, SUBLANES)
            xr, xi = _scan_tile(ds_ref[pl.ds(r0, SUBLANES), :st], ds_ref[pl.ds(r0, SUBLANES), st:], cr, ci, coefs, True)
            ds_ref[pl.ds(r0, SUBLANES), :st] = xr
            ds_ref[pl.ds(r0, SUBLANES), st:] = xi
            first = jnp.where(b > 0, 1.0, 0.0)
            pr = jnp.where(row == 0, pltpu.roll(s_ref[pl.ds(rp, SUBLANES), :st], 1, 0) * first,
                           pltpu.roll(s_ref[pl.ds(r0, SUBLANES), :st], 1, 0))
            pi = jnp.where(row == 0, pltpu.roll(s_ref[pl.ds(rp, SUBLANES), st:], 1, 0) * first,
                           pltpu.roll(s_ref[pl.ds(r0, SUBLANES), st:], 1, 0))
            acc_r = acc_r + pr * xr + pi * xi
            acc_i = acc_i + pr * xi - pi * xr
            return xr[:1, :], xi[:1, :], acc_r, acc_i

        zero = jnp.zeros((1, st), F32)
        zacc = jnp.zeros((SUBLANES, st), F32)
        _, _, acc_r, acc_i = lax.fori_loop(0, n_tiles, tile, (zero, zero, zacc, zacc))
        da_ref[:, :st] = jnp.sum(acc_r, axis=0, keepdims=True)
        da_ref[:, st:] = jnp.sum(acc_i, axis=0, keepdims=True)
        for c0 in range(0, rows, chunk):
            du_ref[pl.ds(c0, chunk), :] = (_dot_nt(ds_ref[pl.ds(c0, chunk), :].astype(BF16), wb_ref[...])
                                           + d_ref[...] * dy_ref[pl.ds(c0, chunk), :]).astype(du_ref.dtype)
        dwb_ref[...] = _dot_tn(u_ref[...].astype(BF16), ds_ref[...].astype(BF16))

    col = pl.BlockSpec((rows, LANES), lambda j: (0, j))
    return pl.pallas_call(
        body, name="s5_bwd", grid=(nb,),
        in_specs=[col, pl.BlockSpec((rows, s2), lambda j: (0, j)), col, col, col,
                  pl.BlockSpec((None, LANES, s2), lambda j: (j, 0, 0)), pl.BlockSpec((None, LANES, s2), lambda j: (j, 0, 0)),
                  pl.BlockSpec((1, LANES), lambda j: (0, j)), pl.BlockSpec((None, 1, s2), lambda j: (j, 0, 0))],
        out_specs=[col, pl.BlockSpec((None, LANES, s2), lambda j: (j, 0, 0)),
                   pl.BlockSpec((None, LANES, s2), lambda j: (j, 0, 0)), pl.BlockSpec((None, 1, s2), lambda j: (j, 0, 0)),
                   pl.BlockSpec((1, LANES), lambda j: (0, j))],
        out_shape=[jax.ShapeDtypeStruct((rows, nb * LANES), BF16), jax.ShapeDtypeStruct((nb, LANES, s2), F32),
                   jax.ShapeDtypeStruct((nb, LANES, s2), F32), jax.ShapeDtypeStruct((nb, 1, s2), F32),
                   jax.ShapeDtypeStruct((1, nb * LANES), F32)],
        scratch_shapes=[pltpu.VMEM((rows, s2), F32), pltpu.VMEM((rows, LANES), F32)],
        compiler_params=_params(("parallel",)),
    )(proj, states, y_pre, dyg_a, dyg_b, wb, wct, d_skip, abar)


def _glu_norm_fwd(y_pre, z, w, *, tr=256):
    rows, width = y_pre.shape
    tr = _tile(rows, tr, SUBLANES)

    def body(y_ref, z_ref, w_ref, o_ref):
        v = _gelu(y_ref[...]) * jax.nn.sigmoid(z_ref[...])
        o_ref[...] = (v * _rms_rows(v) * w_ref[...]).astype(o_ref.dtype)

    blk = pl.BlockSpec((tr, width), lambda i: (i, 0))
    return pl.pallas_call(
        body, name="glu_norm_fwd", grid=(rows // tr,),
        in_specs=[blk, blk, pl.BlockSpec((1, width), lambda i: (0, 0))], out_specs=blk,
        out_shape=jax.ShapeDtypeStruct((rows, width), BF16), compiler_params=_params(("parallel",)),
    )(y_pre, z, w)


def _glu_norm_bwd(y_pre, z, w, dycat, *, tr=256):
    rows, width = y_pre.shape
    tr = _tile(rows, tr, SUBLANES)

    def body(y_ref, z_ref, w_ref, dy_ref, dz_ref, dg_ref, dw_ref, db_ref):
        yg = _gelu(y_ref[...])
        sg = jax.nn.sigmoid(z_ref[...])
        dv, dwp = _rmsnorm_bwd_rows(yg * sg, w_ref[...], dy_ref[...])
        dz = dv * yg * sg * (1.0 - sg)
        dz_ref[...] = dz.astype(dz_ref.dtype)
        dg_ref[...] = dv * sg
        dw_part = jnp.sum(dwp, axis=0, keepdims=True)
        db_part = jnp.sum(dz, axis=0, keepdims=True)

        @pl.when(pl.program_id(0) == 0)
        def _():
            dw_ref[...] = dw_part
            db_ref[...] = db_part

        @pl.when(pl.program_id(0) > 0)
        def _():
            dw_ref[...] += dw_part
            db_ref[...] += db_part

    blk = pl.BlockSpec((tr, width), lambda i: (i, 0))
    vec = pl.BlockSpec((1, width), lambda i: (0, 0))
    return pl.pallas_call(
        body, name="glu_norm_bwd", grid=(rows // tr,), in_specs=[blk, blk, vec, blk], out_specs=[blk, blk, vec, vec],
        out_shape=[jax.ShapeDtypeStruct((rows, width), BF16), jax.ShapeDtypeStruct((rows, width), F32)]
        + [jax.ShapeDtypeStruct((1, width), F32)] * 2,
        compiler_params=_params(("arbitrary",)),
    )(y_pre, z, w, dycat)


def _rope_tables(pos, freq, sign):
    rows = pos.shape[0]

    def body(p_ref, f_ref, s_ref, cos_ref, sin_ref):
        ang = p_ref[...] * f_ref[...]
        cos_ref[...] = jnp.cos(ang)
        sin_ref[...] = jnp.sin(ang) * s_ref[...]

    return pl.pallas_call(body, name="rope_tables", out_shape=[jax.ShapeDtypeStruct((rows, LANES), F32)] * 2)(pos, freq, sign)


def _rope(x, cos, sin_signed):
    half = QK_ROPE_DIM // 2
    src = lax.broadcasted_iota(jnp.int32, (LANES, LANES), 0)
    dst = lax.broadcasted_iota(jnp.int32, (LANES, LANES), 1)
    swap = jnp.where(jnp.logical_or(jnp.logical_and(dst < half, src == dst + half),
                                    jnp.logical_and(jnp.logical_and(dst >= half, dst < 2 * half), src == dst - half)),
                     1.0, 0.0).astype(F32)
    swapped = _dot_exact(x, swap, ((1,), (0,)))
    return x * cos + swapped * sin_signed


def _attn_prep(q, kv, proj, kpe_col, cos, sin, *, tr=256):
    rows = q.shape[0]
    heads = q.shape[1] // HEAD_SLOT
    tr = _tile(rows, tr, SUBLANES)

    def body(q_ref, kv_ref, kpe_ref, cos_ref, sin_ref, qc_ref, kc_ref, v_ref):
        c, s = cos_ref[...], sin_ref[...]
        kpe = _rope(kpe_ref[...], c, s).astype(BF16)
        for h in range(heads):
            nope = slice(h * HEAD_SLOT, h * HEAD_SLOT + LANES)
            pe = slice(h * HEAD_SLOT + LANES, (h + 1) * HEAD_SLOT)
            qc_ref[:, nope] = q_ref[:, nope].astype(BF16)
            qc_ref[:, pe] = _rope(q_ref[:, pe], c, s).astype(BF16)
            kc_ref[:, nope] = kv_ref[:, nope].astype(BF16)
            kc_ref[:, pe] = kpe
            v_ref[:, h * LANES:(h + 1) * LANES] = kv_ref[:, pe].astype(BF16)

    slots = pl.BlockSpec((tr, heads * HEAD_SLOT), lambda i: (i, 0))
    tab = pl.BlockSpec((tr, LANES), lambda i: (i, 0))
    return pl.pallas_call(
        body, name="attn_prep", grid=(rows // tr,),
        in_specs=[slots, slots, pl.BlockSpec((tr, LANES), lambda i: (i, kpe_col)), tab, tab],
        out_specs=[slots, slots, pl.BlockSpec((tr, heads * LANES), lambda i: (i, 0))],
        out_shape=[jax.ShapeDtypeStruct((rows, heads * HEAD_SLOT), BF16)] * 2
        + [jax.ShapeDtypeStruct((rows, heads * LANES), BF16)],
        compiler_params=_params(("parallel",)),
    )(q, kv, proj, cos, sin)


def _causal(tq, tk):
    return lax.broadcasted_iota(jnp.int32, (tq, tk), 1) <= lax.broadcasted_iota(jnp.int32, (tq, tk), 0)


def _attn_fwd(qc, kc, vb, *, scale, tq=512):
    rows = qc.shape[0]
    heads = qc.shape[1] // HEAD_SLOT
    tq = _tile(rows, tq, SUBLANES)
    tk = tq

    def body(q_ref, k_ref, v_ref, o_ref, lse_ref):
        i = pl.program_id(1)
        q = q_ref[...]

        def step(j, carry, diagonal):
            m, l, acc = carry
            k0 = pl.multiple_of(j * tk, tk)
            s = _dot_nt(q, k_ref[pl.ds(k0, tk), :]) * scale
            if diagonal:
                s = jnp.where(_causal(tq, tk), s, NEG_INF)
            m_new = jnp.maximum(m, jnp.max(s, axis=-1, keepdims=True))
            p = jnp.exp(s - m_new)
            alpha = jnp.exp(m - m_new)
            l = alpha * l + jnp.sum(p, axis=-1, keepdims=True)
            acc = alpha * acc + _dot_nn(p.astype(BF16), v_ref[pl.ds(k0, tk), :])
            return m_new, l, acc

        init = (jnp.full((tq, 1), NEG_INF, F32), jnp.zeros((tq, 1), F32), jnp.zeros((tq, LANES), F32))
        below = lax.fori_loop(0, i, lambda j, carry: step(j, carry, False), init)
        m, l, acc = step(i, below, True)
        o_ref[...] = acc / l
        lse_ref[...] = jnp.broadcast_to(m + jnp.log(l), (tq, LANES))

    return pl.pallas_call(
        body, name="attn_fwd", grid=(heads, rows // tq),
        in_specs=[pl.BlockSpec((tq, HEAD_SLOT), lambda h, i: (i, h)), pl.BlockSpec((rows, HEAD_SLOT), lambda h, i: (0, h)),
                  pl.BlockSpec((rows, LANES), lambda h, i: (0, h))],
        out_specs=[pl.BlockSpec((tq, LANES), lambda h, i: (i, h))] * 2,
        out_shape=[jax.ShapeDtypeStruct((rows, heads * LANES), F32)] * 2,
        compiler_params=_params(("parallel", "parallel")),
    )(qc, kc, vb)


def _attn_bwd(qc, kc, vb, o, do, lse, cos, sin, *, scale, tk=512):
    rows = qc.shape[0]
    heads = qc.shape[1] // HEAD_SLOT
    tk = _tile(rows, tk, SUBLANES)
    tq = tk
    nq = rows // tq

    def body(q_ref, k_ref, v_ref, o_ref, do_ref, lse_ref, cos_ref, sin_ref, dq_ref, dkv_ref, dkpe_ref, dq_acc, delta_ref):
        j = pl.program_id(1)

        @pl.when(j == 0)
        def _():
            dq_acc[...] = jnp.zeros_like(dq_acc)
            for r0 in range(0, rows, tq):
                d = jnp.sum(do_ref[pl.ds(r0, tq), :] * o_ref[pl.ds(r0, tq), :], axis=-1, keepdims=True)
                delta_ref[pl.ds(r0, tq), :] = jnp.broadcast_to(d, (tq, LANES))

        kb, vv = k_ref[...], v_ref[...]

        def step(i, carry, diagonal):
            dk, dv = carry
            q0 = pl.multiple_of(i * tq, tq)
            qb = q_ref[pl.ds(q0, tq), :]
            dob = do_ref[pl.ds(q0, tq), :].astype(BF16)
            s = _dot_nt(qb, kb) * scale
            p = jnp.exp(s - lse_ref[pl.ds(q0, tq), :1])
            if diagonal:
                p = jnp.where(_causal(tq, tk), p, 0.0)
            dv = dv + _dot_tn(p.astype(BF16), dob)
            ds = (p * (_dot_nt(dob, vv) - delta_ref[pl.ds(q0, tq), :1])).astype(BF16)
            dk = dk + _dot_tn(ds, qb)
            dq_acc[pl.ds(q0, tq), :] += _dot_nn(ds, kb)
            return dk, dv

        zero = (jnp.zeros((tk, HEAD_SLOT), F32), jnp.zeros((tk, LANES), F32))
        dk, dv = lax.fori_loop(j + 1, nq, lambda i, carry: step(i, carry, False), step(j, zero, True))
        dkv_ref[:, :LANES] = (dk[:, :LANES] * scale).astype(dkv_ref.dtype)
        dkv_ref[:, LANES:] = dv.astype(dkv_ref.dtype)
        dkpe_ref[...] = dk[:, LANES:] * scale

        @pl.when(j == nq - 1)
        def _():
            for r0 in range(0, rows, tq):
                dq = dq_acc[pl.ds(r0, tq), :] * scale
                dq_ref[pl.ds(r0, tq), :LANES] = dq[:, :LANES].astype(dq_ref.dtype)
                dq_ref[pl.ds(r0, tq), LANES:] = _rope(dq[:, LANES:], cos_ref[pl.ds(r0, tq), :],
                                                      -sin_ref[pl.ds(r0, tq), :]).astype(dq_ref.dtype)

    full_q = pl.BlockSpec((rows, HEAD_SLOT), lambda h, j: (0, h))
    full_v = pl.BlockSpec((rows, LANES), lambda h, j: (0, h))
    tab = pl.BlockSpec((rows, LANES), lambda h, j: (0, 0))
    return pl.pallas_call(
        body, name="attn_bwd", grid=(heads, rows // tk),
        in_specs=[full_q, pl.BlockSpec((tk, HEAD_SLOT), lambda h, j: (j, h)), pl.BlockSpec((tk, LANES), lambda h, j: (j, h)),
                  full_v, full_v, full_v, tab, tab],
        out_specs=[full_q, pl.BlockSpec((tk, HEAD_SLOT), lambda h, j: (j, h)), pl.BlockSpec((tk, LANES), lambda h, j: (j, h))],
        out_shape=[jax.ShapeDtypeStruct((rows, heads * HEAD_SLOT), BF16), jax.ShapeDtypeStruct((rows, heads * HEAD_SLOT), BF16),
                   jax.ShapeDtypeStruct((rows, heads * LANES), F32)],
        scratch_shapes=[pltpu.VMEM((rows, HEAD_SLOT), F32), pltpu.VMEM((rows, LANES), F32)],
        compiler_params=_params(("parallel", "arbitrary")),
    )(qc, kc, vb, o, do, lse, cos, sin)


def _kpe_bwd(dkpe_heads, cos, sin, *, tr=512):
    rows = dkpe_heads.shape[0]
    heads = dkpe_heads.shape[1] // LANES
    tr = _tile(rows, tr, 2 * SUBLANES)

    def body(d_ref, cos_ref, sin_ref, o_ref):
        acc = d_ref[:, :LANES]
        for h in range(1, heads):
            acc = acc + d_ref[:, h * LANES:(h + 1) * LANES]
        o_ref[...] = _rope(acc, cos_ref[...], -sin_ref[...]).astype(o_ref.dtype)

    tab = pl.BlockSpec((tr, LANES), lambda i: (i, 0))
    return pl.pallas_call(
        body, name="kpe_bwd", grid=(rows // tr,),
        in_specs=[pl.BlockSpec((tr, heads * LANES), lambda i: (i, 0)), tab, tab], out_specs=tab,
        out_shape=jax.ShapeDtypeStruct((rows, LANES), BF16), compiler_params=_params(("parallel",)),
    )(dkpe_heads, cos, sin)


CONV_ROWS = 128


def _with_halo(ref, r0, ci, n_chunks, ch, lanes, before, after):
    parts = []
    if before:
        lo = pl.multiple_of(jnp.maximum(r0 - SUBLANES, 0), SUBLANES)
        parts.append(ref[pl.ds(lo, SUBLANES), lanes] * jnp.where(ci > 0, 1.0, 0.0))
    parts.append(ref[pl.ds(r0, ch), lanes])
    if after:
        hi = pl.multiple_of(jnp.minimum(r0 + ch, n_chunks * ch - SUBLANES), SUBLANES)
        parts.append(ref[pl.ds(hi, SUBLANES), lanes] * jnp.where(ci < n_chunks - 1, 1.0, 0.0))
    return jnp.concatenate(parts, axis=0)


def _taps(ext):
    return pltpu.roll(ext, 2, 0)[SUBLANES:], pltpu.roll(ext, 1, 0)[SUBLANES:], ext[SUBLANES:]


def _conv3(taps, w, b):
    return w[0:1, :] * taps[0] + w[1:2, :] * taps[1] + w[2:3, :] * taps[2] + b


def _conv_gate_fwd(a, conv_w, conv_b, *, tc=256):
    rows, f2 = a.shape
    f = f2 // 2
    tc = _tile(f, tc)
    nc = f // tc
    ch = _tile(rows, CONV_ROWS, SUBLANES)
    n_chunks = rows // ch

    def body(ag_ref, av_ref, wg_ref, wv_ref, bg_ref, bv_ref, o_ref, ot_ref):
        for lt in range(tc // LANES):
            lanes = slice(lt * LANES, (lt + 1) * LANES)
            wg, wv, bg, bv = wg_ref[:, lanes], wv_ref[:, lanes], bg_ref[:, lanes], bv_ref[:, lanes]

            def chunk(ci, carry):
                r0 = pl.multiple_of(ci * ch, ch)
                gate = _conv3(_taps(_with_halo(ag_ref, r0, ci, n_chunks, ch, lanes, True, False)), wg, bg)
                val = _conv3(_taps(_with_halo(av_ref, r0, ci, n_chunks, ch, lanes, True, False)), wv, bv)
                out = (gate * jax.nn.sigmoid(gate) * val).astype(o_ref.dtype)
                o_ref[pl.ds(r0, ch), lanes] = out
                ot_ref[lanes, pl.ds(r0, ch)] = out.T
                return carry

            lax.fori_loop(0, n_chunks, chunk, 0)

    return pl.pallas_call(
        body, name="conv_gate_fwd", grid=(nc,),
        in_specs=[pl.BlockSpec((rows, tc), lambda j: (0, j)), pl.BlockSpec((rows, tc), lambda j: (0, j + nc)),
                  pl.BlockSpec((SUBLANES, tc), lambda j: (0, j)), pl.BlockSpec((SUBLANES, tc), lambda j: (0, j + nc)),
                  pl.BlockSpec((1, tc), lambda j: (0, j)), pl.BlockSpec((1, tc), lambda j: (0, j + nc))],
        out_specs=[pl.BlockSpec((rows, tc), lambda j: (0, j)), pl.BlockSpec((tc, rows), lambda j: (j, 0))],
        out_shape=[jax.ShapeDtypeStruct((rows, f), BF16), jax.ShapeDtypeStruct((f, rows), BF16)],
        compiler_params=_params(("parallel",)),
    )(a, a, conv_w, conv_w, conv_b, conv_b)


def _conv_gate_bwd(a, conv_w, conv_b, dg, *, tc=256):
    rows, f2 = a.shape
    f = f2 // 2
    tc = _tile(f, tc)
    nc = f // tc
    ch = _tile(rows, CONV_ROWS, SUBLANES)
    n_chunks = rows // ch
    ext_rows = ch + SUBLANES

    def fold(x):
        return jnp.sum(x.reshape(ch // SUBLANES, SUBLANES, LANES), axis=0)

    def body(ag_ref, av_ref, wg_ref, wv_ref, bg_ref, bv_ref, dg_ref, da_ref, dw_ref, db_ref):
        for lt in range(tc // LANES):
            lanes = slice(lt * LANES, (lt + 1) * LANES)
            wg, wv, bg, bv = wg_ref[:, lanes], wv_ref[:, lanes], bg_ref[:, lanes], bv_ref[:, lanes]

            def chunk(ci, acc):
                r0 = pl.multiple_of(ci * ch, ch)
                taps_g = _taps(_with_halo(ag_ref, r0, ci, n_chunks, ch, lanes, True, True))
                taps_v = _taps(_with_halo(av_ref, r0, ci, n_chunks, ch, lanes, True, True))
                dge = _with_halo(dg_ref, r0, ci, n_chunks, ch, lanes, False, True)
                gate, val = _conv3(taps_g, wg, bg), _conv3(taps_v, wv, bv)
                sg = jax.nn.sigmoid(gate)
                d_gate = dge * val * sg * (1.0 + gate * (1.0 - sg))
                d_val = dge * gate * sg
                new = []
                for half, (taps, w, d) in enumerate(((taps_g, wg, d_gate), (taps_v, wv, d_val))):
                    da = (w[2:3, :] * d[:ch] + w[1:2, :] * pltpu.roll(d, ext_rows - 1, 0)[:ch]
                          + w[0:1, :] * pltpu.roll(d, ext_rows - 2, 0)[:ch])
                    da_ref[half, pl.ds(r0, ch), lanes] = da.astype(da_ref.dtype)
                    dc = d[:ch]
                    sums = [fold(dc)] + [fold(dc * t[:ch]) for t in taps]
                    new.append(tuple(x + s for x, s in zip(acc[half], sums)))
                return tuple(new)

            zero = tuple(jnp.zeros((SUBLANES, LANES), F32) for _ in range(4))
            acc = lax.fori_loop(0, n_chunks, chunk, (zero, zero))
            row = lax.broadcasted_iota(jnp.int32, (SUBLANES, LANES), 0)
            for half in range(2):
                db, *taps = (jnp.sum(x, axis=0, keepdims=True) for x in acc[half])
                db_ref[half, :, lanes] = db
                dw = jnp.zeros((SUBLANES, LANES), F32)
                for tap in range(3):
                    dw = jnp.where(row == tap, taps[tap], dw)
                dw_ref[half, :, lanes] = dw

    lo = lambda j: (0, j)
    hi = lambda j: (0, j + nc)
    both = lambda j: (0, 0, j)
    return pl.pallas_call(
        body, name="conv_gate_bwd", grid=(nc,),
        in_specs=[pl.BlockSpec((rows, tc), lo), pl.BlockSpec((rows, tc), hi), pl.BlockSpec((SUBLANES, tc), lo),
                  pl.BlockSpec((SUBLANES, tc), hi), pl.BlockSpec((1, tc), lo), pl.BlockSpec((1, tc), hi),
                  pl.BlockSpec((rows, tc), lo)],
        out_specs=[pl.BlockSpec((2, rows, tc), both), pl.BlockSpec((2, SUBLANES, tc), both), pl.BlockSpec((2, 1, tc), both)],
        out_shape=[jax.ShapeDtypeStruct((2, rows, f), BF16), jax.ShapeDtypeStruct((2, SUBLANES, f), F32),
                   jax.ShapeDtypeStruct((2, 1, f), F32)],
        compiler_params=_params(("parallel",)),
    )(a, a, conv_w, conv_w, conv_b, conv_b, dg)


def _wgrad(a, b, rows, cols, row_sharded, name, **kw):
    return functools.partial(_wgrad_half, a, b, rows, cols, row_sharded, name, **kw)


class _NoExchange:
    def __init__(self, later, ffn):
        self.later, self.ffn = later, ffn

    def mixer_weights(self, after):
        return self.later

    def ffn_weights_arrived(self, after):
        return None

    def ffn_weights(self, after):
        return self.ffn

    def ffn_down_arrived(self, after):
        return None

    def ffn_down_weight(self, after):
        return self.ffn["ffn_w_down"]

    def ffn_grads(self, makers, after):
        self.ffn_makers = makers
        return None

    def ffn_backward_done(self, after):
        return None


def _local_step(x, posf, target, w, hooks):
    rows, d = x.shape
    width = w["ssm_d"].shape[1]
    qr, kvr = w["mla_q_norm_w"].shape[1], w["mla_kv_norm_w"].shape[1]
    heads = w["mla_w_ukv"].shape[1] // HEAD_SLOT
    f2 = w["ffn_conv_b"].shape[1]
    inp = w["w_in"].shape[0]
    groups = width // SSM_GROUP
    nb = groups // GROUPS_PER_BATCH
    scale = (QK_NOPE_DIM + QK_ROPE_DIM) ** -0.5
    g = {}

    hn = _rmsnorm_fwd(x, w["attn_norm_w"], name="attn_norm")
    proj = _matmul(hn, w["w_in"], mode="nt", name="in_proj")

    s5_weights = (w["ssm_lambda_re"], w["ssm_lambda_im"], w["ssm_log_dt"], w["ssm_b_re"], w["ssm_b_im"])
    wb, wct, abar = _s5_bands(*s5_weights, w["ssm_c_re"], w["ssm_c_im"])
    states, y_pre, yg = _s5_fwd(proj, wb, wct, w["ssm_d"], abar)
    later = hooks.mixer_weights(yg)
    z = _matmul(yg, later["ssm_w_glu"], mode="nn", name="glu_proj", bias=w["ssm_b_glu"])
    ys = _glu_norm_fwd(y_pre, z, w["ssm_out_norm_w"])

    q_col, kv_col, kpe_col = width // qr, (width + qr) // kvr, (width + qr + kvr) // LANES
    assert width % qr == 0 and (width + qr) % kvr == 0
    qn = _rmsnorm_fwd(proj, w["mla_q_norm_w"], name="q_norm", width=qr, col=q_col)
    kvn = _rmsnorm_fwd(proj, w["mla_kv_norm_w"], name="kv_norm", width=kvr, col=kv_col)
    q = _matmul(qn, w["mla_w_uq"], mode="nn", name="q_proj")
    kv = _matmul(kvn, w["mla_w_ukv"], mode="nn", name="kv_proj")
    half = QK_ROPE_DIM // 2
    inv_freq = ROPE_THETA ** (-jnp.arange(0, QK_ROPE_DIM, 2, dtype=F32) / QK_ROPE_DIM)
    zeros = jnp.zeros((LANES - QK_ROPE_DIM,), F32)
    freq = jnp.concatenate([inv_freq, inv_freq, zeros]).reshape(1, LANES)
    sign = jnp.concatenate([-jnp.ones((half,), F32), jnp.ones((half,), F32), zeros]).reshape(1, LANES)
    cos, sin = _rope_tables(posf, freq, sign)
    qc, kc, vb = _attn_prep(q, kv, proj, kpe_col, cos, sin)
    o, lse = _attn_fwd(qc, kc, vb, scale=scale, tq=ATTN_BLOCK)
    ym = _rmsnorm_fwd(o, w["mla_out_norm_w"], name="mla_out_norm")
    ycat = jnp.concatenate([ys, ym], axis=1)
    h1 = _matmul(ycat, later["w_out"], mode="nn", name="out_proj", add=x, after=hooks.ffn_weights_arrived(ycat))

    hn2, hn2_t = _rmsnorm_fwd(h1, w["ffn_norm_w"], name="ffn_norm", transposed_too=True)
    ffn = hooks.ffn_weights(hn2)
    a = _matmul(hn2, ffn["ffn_w_up"], mode="nn", name="ffn_up", tm=FFN_ROWS)
    started = hooks.ffn_down_arrived(a)
    conv_b = w["ffn_conv_b"] if started is None else w["ffn_conv_b"] + started[:1, :1]
    gated, gated_t = _conv_gate_fwd(a, ffn["ffn_conv_w"], conv_b)
    w_down = hooks.ffn_down_weight(gated)
    h2 = _matmul(gated, w_down, mode="nn", name="ffn_down", add=h1, tk=2816, tm=FFN_ROWS)
    loss_tile, dh2, dh2_mxu, g["final_norm_w"] = _final_norm_loss(h2, w["final_norm_w"], target)

    dgated = _matmul(dh2_mxu, w_down, mode="nt", name="ffn_down_dx", tm=FFN_ROWS)
    da, dcw, dcb = _conv_gate_bwd(a, ffn["ffn_conv_w"], w["ffn_conv_b"], dgated)
    g["ffn_conv_w"] = jnp.concatenate([dcw[0, :3], dcw[1, :3]], axis=1)
    g["ffn_conv_b"] = jnp.concatenate([dcb[0], dcb[1]], axis=1)
    started = hooks.ffn_grads({
        "ffn_w_up": _wgrad(hn2_t, da, d, f2, False, "ffn_up_dw", b_split=True, tn=_tile(f2 // N_CHIPS, 1408),
                           a_transposed=True),
        "ffn_w_down": _wgrad(gated_t, dh2_mxu, f2 // 2, d, True, "ffn_down_dw", tm=f2 // 2 // N_CHIPS, tn=512,
                             a_transposed=True)}, dcb)
    dhn2 = _matmul(da, ffn["ffn_w_up"], mode="nt", name="ffn_up_dx", a_split=True, tk=_tile(f2 // 2, 2816), tm=FFN_ROWS,
                   after=started)
    dh1, dh1_mxu, g["ffn_norm_w"] = _rmsnorm_bwd(h1, w["ffn_norm_w"], dhn2, name="ffn_norm_bwd", add=dh2,
                                                dx_dtypes=(F32, BF16))

    dycat = _matmul(dh1_mxu, later["w_out"], mode="nt", name="out_proj_dx")
    g["w_out"] = _wgrad(ycat, dh1_mxu, 2 * width, d, True, "out_proj_dw")
    started = hooks.ffn_backward_done(dycat)
    mla_out_norm_w, ssm_out_norm_w = w["mla_out_norm_w"], w["ssm_out_norm_w"]
    if started is not None:
        mla_out_norm_w, ssm_out_norm_w = mla_out_norm_w + started[:1, :1], ssm_out_norm_w + started[:1, :1]

    do, g["mla_out_norm_w"] = _rmsnorm_bwd(o, mla_out_norm_w, dycat, name="mla_out_norm_bwd", width=width, dy_col=1)
    dq, dkv, dkpe_heads = _attn_bwd(qc, kc, vb, o, do, lse, cos, sin, scale=scale, tk=ATTN_BLOCK)
    dkpe = _kpe_bwd(dkpe_heads, cos, sin)
    g["mla_w_uq"] = _wgrad(qn, dq, qr, heads * HEAD_SLOT, False, "q_proj_dw")
    dqn = _matmul(dq, w["mla_w_uq"], mode="nt", name="q_proj_dx")
    dcq, g["mla_q_norm_w"] = _rmsnorm_bwd(proj, w["mla_q_norm_w"], dqn, name="q_norm_bwd", width=qr, col=q_col,
                                          dx_dtypes=(BF16,))
    g["mla_w_ukv"] = _wgrad(kvn, dkv, kvr, heads * HEAD_SLOT, False, "kv_proj_dw")
    dkvn = _matmul(dkv, w["mla_w_ukv"], mode="nt", name="kv_proj_dx")
    dckv, g["mla_kv_norm_w"] = _rmsnorm_bwd(proj, w["mla_kv_norm_w"], dkvn, name="kv_norm_bwd", width=kvr, col=kv_col,
                                            dx_dtypes=(BF16,))

    dz, dyg_a, g["ssm_out_norm_w"], g["ssm_b_glu"] = _glu_norm_bwd(y_pre, z, ssm_out_norm_w, dycat)
    dyg_b = _matmul(dz, later["ssm_w_glu"], mode="nt", name="glu_proj_dx")
    g["ssm_w_glu"] = _wgrad(yg, dz, width, width, True, "glu_proj_dw")
    du, dwb, dwct, dabar, g["ssm_d"] = _s5_bwd(proj, states, y_pre, dyg_a, dyg_b, wb, wct, w["ssm_d"], abar)
    (g["ssm_lambda_re"], g["ssm_lambda_im"], g["ssm_log_dt"], g["ssm_b_re"], g["ssm_b_im"], g["ssm_c_re"],
     g["ssm_c_im"]) = _s5_bands_bwd(*s5_weights, dwb, dwct, dabar)

    pad = jnp.zeros((rows, inp - (width + qr + kvr + LANES)), BF16)
    dproj = jnp.concatenate([du, dcq, dckv, dkpe, pad], axis=1)
    g["w_in"] = _wgrad(dproj, hn, inp, d, False, "in_proj_dw")
    dhn = _matmul(dproj, w["w_in"], mode="nn", name="in_proj_dx")
    dx, g["attn_norm_w"] = _rmsnorm_bwd(x, w["attn_norm_w"], dhn, name="attn_norm_bwd", add=dh1)
    return loss_tile, dx, g


ANY = pl.BlockSpec(memory_space=pl.ANY)
MESH = pl.DeviceIdType.MESH


def _mesh_pos():
    return lax.axis_index("x"), lax.axis_index("y"), lax.axis_index("c")


def _other_chips(x, y):
    return [(1 - x, y), (x, 1 - y), (1 - x, 1 - y)]


def _remote(src, dst, send_sems, recv_sems, k, to):
    return pltpu.make_async_remote_copy(src_ref=src, dst_ref=dst, send_sem=send_sems.at[k], recv_sem=recv_sems.at[k],
                                        device_id=to, device_id_type=MESH)


def _place_shard(shard, piece_idx, row_sharded, name, out_dtype=BF16, pieces=N_CHIPS, after=None):
    rs, cs = shard.shape
    tr = _tile(rs, 256, 2 * SUBLANES)
    rb = rs // tr
    extra = [] if after is None else [after]

    def body(p_ref, x_ref, *rest):
        o_ref = rest[-1]
        o_ref[...] = x_ref[...].astype(o_ref.dtype)

    if row_sharded:
        out_shape, out_map = (pieces * rs, cs), (lambda i, p_ref: (p_ref[0] * rb + i, 0))
    else:
        out_shape, out_map = (rs, pieces * cs), (lambda i, p_ref: (i, p_ref[0]))
    return pl.pallas_call(
        body, name=name, out_shape=jax.ShapeDtypeStruct(out_shape, out_dtype),
        grid_spec=pltpu.PrefetchScalarGridSpec(
            num_scalar_prefetch=1, grid=(rb,),
            in_specs=[pl.BlockSpec((tr, cs), lambda i, p_ref: (i, 0))] + [pl.BlockSpec(memory_space=pl.ANY)] * len(extra),
            out_specs=pl.BlockSpec((tr, cs), out_map)),
        compiler_params=_params(("parallel",)),
    )(piece_idx, shard, *extra)


def _gather_weights(placed, name):
    n = len(placed)
    meta = [(row_sharded, direct) for _, row_sharded, direct in placed]
    over_ici, over_d2d = _gather_plans(meta)
    forwarded = [t for t, (_, direct) in enumerate(meta) if not direct]

    def body(*refs):
        outs = refs[n:2 * n]
        send_sems, recv_sems, pass_send_sems, pass_recv_sems = refs[2 * n:]
        first, arrivals = over_ici(outs, send_sems, recv_sems)
        passed, passed_arrivals = over_d2d([outs[t] for t in forwarded], pass_send_sems, pass_recv_sems)
        for cp in first:
            cp.start()
        for t in range(n):
            for j in range(3):
                arrivals[3 * t + j].wait_recv()
                if t in forwarded:
                    passed[3 * forwarded.index(t) + j].start()
        for cp in passed_arrivals:
            cp.wait_recv()
        for cp in first + passed:
            cp.wait_send()

    return pl.pallas_call(
        body, name=name, in_specs=[ANY] * n, out_specs=[ANY] * n,
        out_shape=[jax.ShapeDtypeStruct(arr.shape, arr.dtype) for arr, _, _ in placed],
        input_output_aliases={t: t for t in range(n)},
        scratch_shapes=[pltpu.SemaphoreType.DMA((3 * n,)), pltpu.SemaphoreType.DMA((3 * n,)),
                        pltpu.SemaphoreType.DMA((3 * len(forwarded),)), pltpu.SemaphoreType.DMA((3 * len(forwarded),))],
    )(*[arr for arr, _, _ in placed])


def _gather_plans(meta):
    def window(ref, row_sharded, piece, half):
        r, cc = ref.shape
        if row_sharded:
            rs = r // N_CHIPS
            if half is None:
                return ref.at[pl.ds(piece * rs, rs), :]
            return ref.at[pl.ds(piece * rs + half * (rs // 2), rs // 2), :]
        cs = cc // N_CHIPS
        if half is None:
            return ref.at[:, pl.ds(piece * cs, cs)]
        return ref.at[pl.ds(half * (r // 2), r // 2), pl.ds(piece * cs, cs)]

    def over_ici(refs, send_sems, recv_sems):
        x, y, c = _mesh_pos()
        sends, recvs = [], []
        for t, (row_sharded, direct) in enumerate(meta):
            mine = window(refs[t], row_sharded, 2 * x + y, None if direct else c)
            for j, (px, py) in enumerate(_other_chips(x, y)):
                theirs = window(refs[t], row_sharded, 2 * px + py, None if direct else c)
                sends.append(_remote(mine, mine, send_sems, recv_sems, 3 * t + j, (px, py, c)))
                recvs.append(_remote(theirs, theirs, send_sems, recv_sems, 3 * t + j, (px, py, c)))
        return sends, recvs

    def over_d2d(refs, send_sems, recv_sems):
        x, y, c = _mesh_pos()
        sends, recvs = [], []
        rows = [row_sharded for row_sharded, direct in meta if not direct]
        for t, row_sharded in enumerate(rows):
            for j, (px, py) in enumerate(_other_chips(x, y)):
                got = window(refs[t], row_sharded, 2 * px + py, c)
                other = window(refs[t], row_sharded, 2 * px + py, 1 - c)
                sends.append(_remote(got, got, send_sems, recv_sems, 3 * t + j, (x, y, 1 - c)))
                recvs.append(_remote(other, other, send_sems, recv_sems, 3 * t + j, (x, y, 1 - c)))
        return sends, recvs

    return over_ici, over_d2d


HBM = pl.BlockSpec(memory_space=pltpu.HBM)
SEMAPHORES = pl.BlockSpec(memory_space=pltpu.SEMAPHORE)
DATAFLOW = pltpu.SideEffectType.DATAFLOW_SIDE_EFFECTING


def _start_copies(name, arrays, plan, n_copies, after):
    n = len(arrays)

    def body(*refs):
        sends, _ = plan(refs[:n], refs[n + 1], refs[n + 2])
        for cp in sends:
            cp.start()
        token = refs[2 * n + 3]
        token[...] = jnp.zeros_like(token)

    out = pl.pallas_call(
        body, name=name,
        out_shape=(pltpu.SemaphoreType.DMA((n_copies,)), pltpu.SemaphoreType.DMA((n_copies,)),
                   *[pltpu.HBM(a.shape, a.dtype) for a in arrays], jax.ShapeDtypeStruct((SUBLANES, LANES), F32)),
        in_specs=[HBM] * n + [ANY],
        out_specs=(SEMAPHORES, SEMAPHORES, *[HBM] * n, pl.BlockSpec(memory_space=pltpu.VMEM)),
        input_output_aliases={t: t + 2 for t in range(n)},
        compiler_params=pltpu.CompilerParams(has_side_effects=DATAFLOW),
    )(*[pltpu.with_memory_space_constraint(a, pltpu.HBM) for a in arrays], after)
    return out[0], out[1], list(out[2:2 + n]), out[2 + n]


def _wait_copies(name, started, plan, after):
    send_sems, recv_sems, arrays, _ = started
    n = len(arrays)

    def body(*refs):
        sends, recvs = plan(refs[:n], refs[n], refs[n + 1])
        for cp in sends:
            cp.wait_send()
        for cp in recvs:
            cp.wait_recv()

    out = pl.pallas_call(
        body, name=name, out_shape=[pltpu.HBM(a.shape, a.dtype) for a in arrays],
        in_specs=[HBM] * n + [SEMAPHORES, SEMAPHORES, ANY], out_specs=[HBM] * n,
        input_output_aliases={t: t for t in range(n)},
        compiler_params=pltpu.CompilerParams(has_side_effects=DATAFLOW),
    )(*arrays, send_sems, recv_sems, after)
    return list(out)


def _exchange(name, arrays, out_shapes, plan, n_copies, in_place=False, after=None):
    n = len(arrays)
    extra = [] if after is None else [after]

    def body(*refs):
        ins, outs = refs[:n], refs[n + len(extra):n + len(extra) + len(out_shapes)]
        send_sems, recv_sems = refs[n + len(extra) + len(out_shapes):]
        sends, recvs = plan(ins, outs, send_sems, recv_sems)
        for cp in sends:
            cp.start()
        for cp in recvs:
            cp.wait_recv()
        for cp in sends:
            cp.wait_send()

    return pl.pallas_call(
        body, name=name, in_specs=[ANY] * (n + len(extra)), out_specs=[ANY] * len(out_shapes), out_shape=out_shapes,
        input_output_aliases={t: t for t in range(n)} if in_place else {},
        scratch_shapes=[pltpu.SemaphoreType.DMA((n_copies,)), pltpu.SemaphoreType.DMA((n_copies,))],
    )(*arrays, *extra)


def _give_plan(n):
    def plan(refs, send_sems, recv_sems):
        x, y, c = _mesh_pos()
        sends = [_remote(refs[t], refs[n + t], send_sems, recv_sems, t, (x, y, 1 - c)) for t in range(n)]
        return sends, sends

    return plan


def _scatter_plan(n):
    def plan(refs, send_sems, recv_sems):
        x, y, c = _mesh_pos()
        sends = []
        for t in range(n):
            for j, (px, py) in enumerate(_other_chips(x, y)):
                sends.append(_remote(refs[t].at[2 * px + py], refs[n + t].at[j], send_sems, recv_sems, 3 * t + j, (px, py, c)))
        return sends, sends

    return plan


def _scatter_shapes(sums):
    return [jax.ShapeDtypeStruct((3,) + s.shape[1:], s.dtype) for s in sums]


def _join_plan(n):
    def plan(refs, send_sems, recv_sems):
        x, y, c = _mesh_pos()
        sends = [_remote(refs[t].at[c], refs[t].at[c], send_sems, recv_sems, t, (x, y, 1 - c)) for t in range(n)]
        recvs = [_remote(refs[t].at[1 - c], refs[t].at[1 - c], send_sems, recv_sems, t, (x, y, 1 - c)) for t in range(n)]
        return sends, recvs

    return plan


def _join_halves(halves, name, after=None):
    plan = _join_plan(len(halves))
    shapes = [jax.ShapeDtypeStruct(h.shape, h.dtype) for h in halves]
    return _exchange(name, halves, shapes, lambda ins, outs, s, r: plan(outs, s, r), len(halves), in_place=True, after=after)


def _add_other_half(g4, got, where, name, wire_dtype=BF16):
    _, pieces, sr, sc = g4.shape
    tr = _tile(sr, 256, 2 * SUBLANES)

    def body(w_ref, a_ref, b_ref, o_ref):
        o_ref[...] = (a_ref[...] + b_ref[...]).astype(o_ref.dtype)

    blk = pl.BlockSpec((None, tr, sc), lambda p, i, w_ref: (p, i, 0))
    return pl.pallas_call(
        body, name=name, out_shape=jax.ShapeDtypeStruct((pieces, sr, sc), wire_dtype),
        grid_spec=pltpu.PrefetchScalarGridSpec(
            num_scalar_prefetch=1, grid=(pieces, sr // tr),
            in_specs=[pl.BlockSpec((None, None, tr, sc), lambda p, i, w_ref: (w_ref[0], p, i, 0)), blk], out_specs=blk),
        compiler_params=_params(("parallel", "parallel")),
    )(where, g4, got)


def _add_pieces(sums, got_pieces, where, name):
    _, sr, sc = sums.shape
    tr = _tile(sr, 256, 2 * SUBLANES)

    def body(w_ref, a_ref, r_ref, o_ref):
        acc = a_ref[...]
        for j in range(3):
            acc = acc + r_ref[j].astype(F32)
        o_ref[...] = acc

    return pl.pallas_call(
        body, name=name, out_shape=jax.ShapeDtypeStruct((N_CORES, sr, sc), F32),
        grid_spec=pltpu.PrefetchScalarGridSpec(
            num_scalar_prefetch=1, grid=(sr // tr,),
            in_specs=[pl.BlockSpec((None, tr, sc), lambda i, w_ref: (w_ref[1], i, 0)),
                      pl.BlockSpec((3, tr, sc), lambda i, w_ref: (0, i, 0))],
            out_specs=pl.BlockSpec((None, tr, sc), lambda i, w_ref: (w_ref[0], i, 0))),
        compiler_params=_params(("parallel",)),
    )(where, sums, got_pieces)


def _adamw_update(w, g, m, v):
    nm = ADAM_B1 * m + (1.0 - ADAM_B1) * g
    nv = ADAM_B2 * v + (1.0 - ADAM_B2) * (g * g)
    m_hat = nm / (1.0 - ADAM_B1 ** ADAM_STEP)
    v_hat = nv / (1.0 - ADAM_B2 ** ADAM_STEP)
    return -ADAM_LR * (m_hat / (jnp.sqrt(v_hat) + ADAM_EPS) + ADAM_WD * w), nm, nv


def _adamw(w, g, m, v, name, after=None):
    rows, cols = w.shape
    halves = 2 if g.ndim == 3 else 1
    bc = cols // halves
    tr = _tile(rows, max(SUBLANES, (1 << 19) // max(bc, 1) // SUBLANES * SUBLANES), SUBLANES)

    def body(w_ref, g_ref, m_ref, v_ref, *rest):
        d_ref, nm_ref, nv_ref, go_ref = rest[-4:]
        gv = g_ref[...]
        d_ref[...], nm_ref[...], nv_ref[...] = _adamw_update(w_ref[...], gv, m_ref[...], v_ref[...])
        go_ref[...] = gv

    blk = pl.BlockSpec((tr, bc), lambda i, h: (i, h))
    g_blk = pl.BlockSpec((None, tr, bc), lambda i, h: (h, i, 0)) if halves == 2 else blk
    extra = [] if after is None else [after]
    return pl.pallas_call(
        body, name=name, grid=(rows // tr, halves),
        in_specs=[blk, g_blk, blk, blk] + [pl.BlockSpec(memory_space=pl.ANY)] * len(extra), out_specs=[blk] * 4,
        out_shape=[jax.ShapeDtypeStruct((rows, cols), F32)] * 4, compiler_params=_params(("parallel", "parallel")),
    )(w, g, m, v, *extra)


def _adamw_many(ws, gs, ms, vs, name):
    n = len(ws)

    def body(*refs):
        outs = refs[4 * n:]
        for k in range(n):
            w_ref, g_ref, m_ref, v_ref = (refs[j * n + k] for j in range(4))
            outs[k][...], outs[n + k][...], outs[2 * n + k][...] = _adamw_update(w_ref[...], g_ref[...], m_ref[...], v_ref[...])

    out = pl.pallas_call(
        body, name=name, out_shape=[jax.ShapeDtypeStruct(w.shape, F32) for w in ws] * 3,
        compiler_params=pltpu.CompilerParams(vmem_limit_bytes=VMEM_LIMIT_BYTES),
    )(*ws, *gs, *ms, *vs)
    return out[:n], out[n:2 * n], out[2 * n:]


WEIGHTS = ['attn_norm_w', 'w_in', 'ssm_lambda_re', 'ssm_lambda_im', 'ssm_log_dt', 'ssm_b_re', 'ssm_b_im', 'ssm_c_re',
           'ssm_c_im', 'ssm_d', 'ssm_w_glu', 'ssm_b_glu', 'mla_q_norm_w', 'mla_w_uq', 'mla_kv_norm_w', 'mla_w_ukv',
           'ssm_out_norm_w', 'mla_out_norm_w', 'w_out', 'ffn_norm_w', 'ffn_w_up', 'ffn_conv_w', 'ffn_conv_b',
           'ffn_w_down', 'final_norm_w']
SHARDED = {'w_in': False, 'ssm_w_glu': True, 'mla_w_uq': False, 'mla_w_ukv': False, 'w_out': True, 'ffn_w_up': False,
           'ffn_w_down': True}
SMALL = [n for n in WEIGHTS if n not in SHARDED and n != 'ffn_conv_w']
ROPE_PAD = HEAD_SLOT - QK_NOPE_DIM - QK_ROPE_DIM
SMALL_COLS = 8 * LANES


def _pad_heads(w_uq, heads):
    qr = w_uq.shape[0]
    w3 = w_uq.reshape(qr, heads, QK_NOPE_DIM + QK_ROPE_DIM)
    return jnp.concatenate([w3, jnp.zeros((qr, heads, ROPE_PAD), w_uq.dtype)], axis=2).reshape(qr, heads * HEAD_SLOT)


def _unpad_heads(g_uq, heads):
    qr = g_uq.shape[0]
    return g_uq.reshape(qr, heads, HEAD_SLOT)[:, :, :QK_NOPE_DIM + QK_ROPE_DIM].reshape(qr, -1)


FFN = ['ffn_w_up', 'ffn_w_down']
MIXER_LATER = ['ssm_w_glu', 'w_out']
FFN_GATHER = FFN + ['ffn_conv_w']
FFN_GATHER_META = [(SHARDED[n], False) for n in FFN] + [(False, True)]


class _Overlapped:
    def __init__(self, placed_first, first_sharding, where):
        self.where, self.mine, self.other = where, where[:1], 1 - where[:1]
        self.first_ici, self.first_d2d = _gather_plans([(r, False) for r in first_sharding])
        self.first = _start_copies("gather_first_start", placed_first, self.first_ici, 3 * len(placed_first), where)
        self.first_started = self.first[3]

    def start_rest(self, placed_later, placed):
        self.later_ici, self.later_d2d = _gather_plans([(SHARDED[n], False) for n in MIXER_LATER])
        self.later = _start_copies("gather_later_start", placed_later, self.later_ici, 3 * len(placed_later),
                                   self.first_started)
        up, down, taps = placed
        self.up_ici, self.up_d2d = _gather_plans([(SHARDED["ffn_w_up"], False), (False, True)])
        self.up = _start_copies("gather_ffn_up_start", [up, taps], self.up_ici, 6, self.later[3])
        self.down_ici, self.down_d2d = _gather_plans([(SHARDED["ffn_w_down"], False)])
        self.down = _start_copies("gather_ffn_down_start", [down], self.down_ici, 3, self.up[3])
        self.gather_started = self.down[3]
        arrived = _wait_copies("gather_first_wait", self.first, self.first_ici, self.gather_started)
        shapes = [jax.ShapeDtypeStruct(a.shape, a.dtype) for a in arrived]
        return _exchange("gather_first_pass", arrived, shapes, lambda ins, outs, s, r: self.first_d2d(outs, s, r),
                         3 * len(arrived), in_place=True)

    def mixer_weights(self, after):
        arrived = _wait_copies("gather_later_wait", self.later, self.later_ici, after)
        shapes = [jax.ShapeDtypeStruct(a.shape, a.dtype) for a in arrived]
        passed = _exchange("gather_later_pass", arrived, shapes, lambda ins, outs, s, r: self.later_d2d(outs, s, r),
                           3 * len(arrived), in_place=True)
        return dict(zip(MIXER_LATER, passed))

    def ffn_weights_arrived(self, after):
        up, self.taps = _wait_copies("gather_ffn_up_wait", self.up, self.up_ici, after)
        self.up_passing = _start_copies("gather_ffn_up_pass_start", [up], self.up_d2d, 3, after)
        return self.up_passing[3]

    def ffn_weights(self, after):
        w_up, = _wait_copies("gather_ffn_up_pass_wait", self.up_passing, self.up_d2d, after)
        return {"ffn_w_up": w_up, "ffn_conv_w": self.taps}

    def ffn_down_arrived(self, after):
        down, = _wait_copies("gather_ffn_down_wait", self.down, self.down_ici, after)
        self.down_passing = _start_copies("gather_ffn_down_pass_start", [down], self.down_d2d, 3, after)
        return self.down_passing[3]

    def ffn_down_weight(self, after):
        return _wait_copies("gather_ffn_down_pass_wait", self.down_passing, self.down_d2d, after)[0]

    def ffn_grads(self, makers, after):
        self.makers = [makers[name] for name in FFN]
        n = len(FFN)
        give = [make(self.other, suffix="_give") for make in self.makers]
        lands = [lax.empty(g.shape, g.dtype) for g in give]
        self.swap = _start_copies("grad_ffn_swap_start", give + lands, _give_plan(n), n, after)
        return self.swap[3]

    def ffn_backward_done(self, after):
        n = len(FFN)
        got = _wait_copies("grad_ffn_swap_wait", self.swap, _give_plan(n), after)[n:]
        kept = [make(self.mine, suffix="_keep", add=got[t], wire=True) for t, make in enumerate(self.makers)]
        self.sums = [k[0] for k in kept]
        wires = [k[1] for k in kept]
        lands = [lax.empty(s.shape, s.dtype) for s in _scatter_shapes(wires)]
        self.scatter = _start_copies("grad_ffn_scatter_start", wires + lands, _scatter_plan(n), 3 * n, after)
        return self.scatter[3]

    def ffn_reduced(self, after):
        n = len(FFN)
        got_pieces = _wait_copies("grad_ffn_scatter_wait", self.scatter, _scatter_plan(n), after)[n:]
        return [_add_pieces(self.sums[t], got_pieces[t], self.where, "grad_add_pieces_" + name) for t, name in enumerate(FFN)]


def _step(args):
    x, positions, target = args["x"][0], args["positions"], args["loss_target"][0]
    rows = x.shape[0]
    p = {n: args[n] for n in WEIGHTS}
    xi, yi, ci = _mesh_pos()
    piece = 2 * xi + yi

    def transposed(a):
        return jnp.swapaxes(a[0], 0, 1)

    def as_stored(n, a):
        return jnp.swapaxes(a, 2, 3) if n in ("ssm_b_re", "ssm_b_im") else a

    w_in = transposed(p["w_in"])
    in_width = w_in.shape[0]
    in_pad = (-in_width) % (2 * LANES)
    heads_here = p["mla_w_uq"].shape[2] // (QK_NOPE_DIM + QK_ROPE_DIM)
    shards = {
        "w_in": jnp.pad(w_in, ((0, in_pad), (0, 0))),
        "ssm_w_glu": p["ssm_w_glu"][0],
        "mla_w_uq": _pad_heads(p["mla_w_uq"][0], heads_here),
        "mla_w_ukv": p["mla_w_ukv"][0],
        "w_out": p["w_out"][0],
        "ffn_w_up": p["ffn_w_up"][0],
        "ffn_w_down": p["ffn_w_down"][0],
    }
    conv_w = jnp.pad(p["ffn_conv_w"][0], ((0, SUBLANES - p["ffn_conv_w"].shape[1]), (0, 0)))
    order = list(SHARDED)
    piece_idx = piece.reshape(1).astype(jnp.int32)
    mixer = [n for n in order if n not in FFN]
    first = [n for n in mixer if n not in MIXER_LATER]
    where = jnp.stack([ci, piece]).astype(jnp.int32)
    placed = {n: _place_shard(shards[n], piece_idx, SHARDED[n], "place_" + n) for n in first}
    hooks = _Overlapped([placed[n] for n in first], [SHARDED[n] for n in first], where)
    for n in order:
        if n not in first:
            placed[n] = _place_shard(shards[n], piece_idx, SHARDED[n], "place_" + n, after=hooks.first_started)
    placed["ffn_conv_w"] = _place_shard(conv_w, piece_idx, False, "place_ffn_conv_w", out_dtype=F32,
                                        after=hooks.first_started)
    w = dict(zip(first, hooks.start_rest([placed[n] for n in MIXER_LATER], [placed[n] for n in FFN_GATHER])))
    groups = p["ssm_lambda_re"].shape[1]
    w.update({
        "attn_norm_w": p["attn_norm_w"] + hooks.gather_started[:1, :1],
        "ssm_lambda_re": p["ssm_lambda_re"][0], "ssm_lambda_im": p["ssm_lambda_im"][0],
        "ssm_log_dt": p["ssm_log_dt"].reshape(groups, 1), "ssm_b_re": as_stored("ssm_b_re", p["ssm_b_re"])[0],
        "ssm_b_im": as_stored("ssm_b_im", p["ssm_b_im"])[0], "ssm_c_re": p["ssm_c_re"][0], "ssm_c_im": p["ssm_c_im"][0],
        "ssm_d": p["ssm_d"], "ssm_b_glu": p["ssm_b_glu"], "mla_q_norm_w": p["mla_q_norm_w"],
        "mla_kv_norm_w": p["mla_kv_norm_w"], "ssm_out_norm_w": p["ssm_out_norm_w"], "mla_out_norm_w": p["mla_out_norm_w"],
        "ffn_norm_w": p["ffn_norm_w"], "ffn_conv_b": p["ffn_conv_b"], "final_norm_w": p["final_norm_w"].reshape(1, -1),
    })

    loss_tile, dx, g = _local_step(x, positions.reshape(rows, 1).astype(F32), target, w, hooks)

    flat = [g[n].reshape(-1) for n in SMALL] + [g["ffn_conv_w"].reshape(-1), loss_tile[0, :1]]
    sizes = [f.shape[0] for f in flat]
    per_block = -(-sum(sizes) // (N_CORES * N_CHIPS * SMALL_COLS))
    small_rows = -(-per_block // (2 * SUBLANES)) * (2 * SUBLANES)
    padded = N_CORES * N_CHIPS * small_rows * SMALL_COLS

    def pack(parts):
        parts = list(parts)
        have = sum(q.shape[0] for q in parts)
        return jnp.concatenate(parts + [jnp.zeros((padded - have,), F32)])

    reduced = mixer + ["small"]
    small = pack(flat).reshape(N_CORES, N_CHIPS, small_rows, SMALL_COLS)
    give = [g[n](hooks.other, suffix="_give") for n in mixer] + [lax.dynamic_index_in_dim(small, 1 - ci, 0, keepdims=False)]
    lands = [lax.empty(a.shape, a.dtype) for a in give]
    give_plan = _give_plan(len(reduced))
    swap = _start_copies("grad_mixer_swap_start", give + lands, give_plan, len(reduced), dx)

    grads, delta, new_m, new_v = {}, {}, {}, {}

    def finish(n, joined, after=None):
        grad = joined if SHARDED[n] else joined.reshape(-1, joined.shape[2])
        if n == "w_in":
            wt, mt, vt = w_in, transposed(args["m_w_in"]), transposed(args["v_w_in"])
            out = _adamw(wt, grad, mt, vt, "adamw_w_in")
            delta[n], new_m[n], new_v[n], grads[n] = (jnp.swapaxes(a, 0, 1)[None] for a in out)
            return
        if n == "mla_w_uq":
            grad = _unpad_heads(grad, heads_here)
        adam(n, grad, after)

    def adam(n, grad, after=None):
        shape = p[n].shape
        out = _adamw(p[n].reshape(shape[1:]), grad, args["m_" + n].reshape(shape[1:]),
                     args["v_" + n].reshape(shape[1:]), "adamw_" + n, after)
        delta[n], new_m[n], new_v[n], grads[n] = (a.reshape(shape) for a in out)

    ffn_halves = hooks.ffn_reduced(swap[3])
    got = _wait_copies("grad_mixer_swap_wait", swap, give_plan, ffn_halves[-1])[len(reduced):]
    join_plan = _join_plan(len(FFN))
    ffn_join = _start_copies("grad_ffn_join_start", ffn_halves, join_plan, len(FFN), got[0])
    kept = [g[n](hooks.mine, suffix="_keep", add=got[t], wire=True) for t, n in enumerate(mixer)]
    small_sum = _add_other_half(small, got[-1], where, "grad_add_half_small", F32)
    sums = [k[0] for k in kept] + [small_sum]
    wires = [k[1] for k in kept] + [small_sum]
    ffn_joined = _wait_copies("grad_ffn_join_wait", ffn_join, join_plan, kept[-1][0])
    lands = [lax.empty(s.shape, s.dtype) for s in _scatter_shapes(wires)]
    scatter_plan = _scatter_plan(len(reduced))
    scatter = _start_copies("grad_mixer_scatter_start", wires + lands, scatter_plan, 3 * len(reduced), ffn_joined[0])
    behind = scatter[3]
    for n, joined in zip(FFN, ffn_joined):
        finish(n, joined, after=behind)
        behind = delta[n]
    got_pieces = _wait_copies("grad_mixer_scatter_wait", scatter, scatter_plan, delta[FFN[-1]])[len(reduced):]
    halves = [_add_pieces(sums[t], got_pieces[t], where, "grad_add_pieces_" + n) for t, n in enumerate(reduced)]
    joined = _join_halves(halves, "grad_join_halves")
    for n, j in zip(mixer, joined):
        finish(n, j)
    eighths = _place_shard(joined[-1].reshape(N_CORES * small_rows, SMALL_COLS), piece_idx, True, "place_small_grads",
                           out_dtype=F32)
    small_sum = _gather_weights([(eighths, True, False)], "gather_small_grads")[0]
    flat_sum = small_sum.reshape(N_CHIPS, N_CORES, small_rows * SMALL_COLS).transpose(1, 0, 2).reshape(-1)
    offs = [0]
    for s in sizes:
        offs.append(offs[-1] + s)
    stored = {n: as_stored(n, p[n]) for n in SMALL}
    for k, n in enumerate(SMALL):
        grads[n] = flat_sum[offs[k]:offs[k + 1]].reshape(stored[n].shape)
    taps, cols_here = p["ffn_conv_w"].shape[1], p["ffn_conv_w"].shape[2]
    conv_full = flat_sum[offs[len(SMALL)]:offs[len(SMALL) + 1]].reshape(taps, N_CHIPS * cols_here)
    adam("ffn_conv_w", lax.dynamic_slice_in_dim(conv_full, piece * cols_here, cols_here, axis=1))
    loss = flat_sum[offs[len(SMALL) + 1]]

    def rank2(a):
        return a.reshape(1, -1) if a.ndim == 1 else a

    d_s, m_s, v_s = _adamw_many([rank2(stored[n]) for n in SMALL], [rank2(grads[n]) for n in SMALL],
                                [rank2(as_stored(n, args["m_" + n])) for n in SMALL],
                                [rank2(as_stored(n, args["v_" + n])) for n in SMALL], "adamw_small")
    for k, n in enumerate(SMALL):
        delta[n], new_m[n], new_v[n], grads[n] = (as_stored(n, a.reshape(stored[n].shape))
                                                  for a in (d_s[k], m_s[k], v_s[k], grads[n]))

    return (loss, dx[None], *[grads[n] for n in WEIGHTS], *[delta[n] for n in WEIGHTS],
            *[new_m[n] for n in WEIGHTS], *[new_v[n] for n in WEIGHTS])


def kernel(x, positions, attn_norm_w, w_in, ssm_lambda_re, ssm_lambda_im, ssm_log_dt, ssm_b_re, ssm_b_im, ssm_c_re, ssm_c_im, ssm_d, ssm_w_glu, ssm_b_glu, mla_q_norm_w, mla_w_uq, mla_kv_norm_w, mla_w_ukv, ssm_out_norm_w, mla_out_norm_w, w_out, ffn_norm_w, ffn_w_up, ffn_conv_w, ffn_conv_b, ffn_w_down, final_norm_w, loss_target, m_attn_norm_w, m_w_in, m_ssm_lambda_re, m_ssm_lambda_im, m_ssm_log_dt, m_ssm_b_re, m_ssm_b_im, m_ssm_c_re, m_ssm_c_im, m_ssm_d, m_ssm_w_glu, m_ssm_b_glu, m_mla_q_norm_w, m_mla_w_uq, m_mla_kv_norm_w, m_mla_w_ukv, m_ssm_out_norm_w, m_mla_out_norm_w, m_w_out, m_ffn_norm_w, m_ffn_w_up, m_ffn_conv_w, m_ffn_conv_b, m_ffn_w_down, m_final_norm_w, v_attn_norm_w, v_w_in, v_ssm_lambda_re, v_ssm_lambda_im, v_ssm_log_dt, v_ssm_b_re, v_ssm_b_im, v_ssm_c_re, v_ssm_c_im, v_ssm_d, v_ssm_w_glu, v_ssm_b_glu, v_mla_q_norm_w, v_mla_w_uq, v_mla_kv_norm_w, v_mla_w_ukv, v_ssm_out_norm_w, v_mla_out_norm_w, v_w_out, v_ffn_norm_w, v_ffn_w_up, v_ffn_conv_w, v_ffn_conv_b, v_ffn_w_down, v_final_norm_w):
    return _step(dict(locals()))
```

```python
import functools
import math

import jax
import jax.numpy as jnp
from jax import lax
from jax.experimental import pallas as pl
from jax.experimental.pallas import tpu as pltpu

F32 = jnp.float32
BF16 = jnp.bfloat16

SSM_GROUP = 16
SSM_STATE = 64
QK_NOPE_DIM = 128
QK_ROPE_DIM = 64
ROPE_THETA = 10000.0
RMS_EPS = 1e-6
ADAM_LR, ADAM_B1, ADAM_B2, ADAM_EPS, ADAM_WD, ADAM_STEP = 0.001, 0.9, 0.999, 1e-08, 0.01, 10

LANES = 128
SUBLANES = 8
VMEM_LIMIT_BYTES = 56 * 1024 * 1024

GROUPS_PER_BATCH = LANES // SSM_GROUP
STATE_PER_BATCH = GROUPS_PER_BATCH * SSM_STATE
HEAD_SLOT = 2 * LANES
NEG_INF = -1e30
ATTN_BLOCK = 512
FFN_ROWS = 1024

N_CHIPS = 4
N_CORES = 2


def _tile(n, pref, align=LANES):
    if n <= pref:
        return n
    t = (pref // align) * align
    while t >= align:
        if n % t == 0:
            return t
        t -= align
    return n


def _params(sem):
    return pltpu.CompilerParams(dimension_semantics=sem, vmem_limit_bytes=VMEM_LIMIT_BYTES)


def _dot(a, b, dims):
    return lax.dot_general(a, b, (dims, ((), ())), preferred_element_type=F32)


def _dot_nn(a, b):
    return _dot(a, b, ((1,), (0,)))


def _dot_nt(a, b):
    return _dot(a, b, ((1,), (1,)))


def _dot_tn(a, b):
    return _dot(a, b, ((0,), (0,)))


def _matmul(a, b, *, mode, name, tm=512, tn=1024, tk=2048, bias=None, add=None, out_dtype=F32,
            a_split=False, b_split=False, after=None):
    if a_split:
        assert mode == "nt"
        a_shape = (a.shape[1], 2 * a.shape[2])
    else:
        a_shape = a.shape
    if b_split:
        assert mode == "tn"
        b_shape = (b.shape[1], 2 * b.shape[2])
    else:
        b_shape = b.shape
    if mode == "nn":
        (m, k), (k2, n) = a_shape, b_shape
    elif mode == "nt":
        (m, k), (n, k2) = a_shape, b_shape
    else:
        (k, m), (k2, n) = a_shape, b_shape
    assert k == k2, (a.shape, b.shape, mode)
    tm, tn, tk = _tile(m, tm, SUBLANES), _tile(n, tn), _tile(k, tk)
    nk = k // tk
    a_spec = {"nn": pl.BlockSpec((tm, tk), lambda i, j, kk: (i, kk)),
              "nt": pl.BlockSpec((tm, tk), lambda i, j, kk: (i, kk)),
              "tn": pl.BlockSpec((tk, tm), lambda i, j, kk: (kk, i))}[mode]
    b_spec = {"nn": pl.BlockSpec((tk, tn), lambda i, j, kk: (kk, j)),
              "nt": pl.BlockSpec((tn, tk), lambda i, j, kk: (j, kk)),
              "tn": pl.BlockSpec((tk, tn), lambda i, j, kk: (kk, j))}[mode]
    if a_split:
        kb = a.shape[2] // tk
        assert a.shape[2] % tk == 0
        a_spec = pl.BlockSpec((None, tm, tk), lambda i, j, kk: (kk // kb, i, kk % kb))
    if b_split:
        nb = b.shape[2] // tn
        assert b.shape[2] % tn == 0
        b_spec = pl.BlockSpec((None, tk, tn), lambda i, j, kk: (j // nb, kk, j % nb))
    dot = {"nn": _dot_nn, "nt": _dot_nt, "tn": _dot_tn}[mode]
    in_specs, operands = [a_spec, b_spec], [a, b]
    if bias is not None:
        in_specs.append(pl.BlockSpec((1, tn), lambda i, j, kk: (0, j)))
        operands.append(bias)
    if add is not None:
        in_specs.append(pl.BlockSpec((tm, tn), lambda i, j, kk: (i, j)))
        operands.append(add)
    if after is not None:
        in_specs.append(pl.BlockSpec(memory_space=pl.ANY))
        operands.append(after)

    def body(*refs):
        a_ref, b_ref = refs[0], refs[1]
        rest = list(refs[2:])
        bias_ref = rest.pop(0) if bias is not None else None
        add_ref = rest.pop(0) if add is not None else None
        if after is not None:
            rest.pop(0)
        o_ref, acc_ref = rest

        def finish(acc):
            if bias_ref is not None:
                acc = acc + bias_ref[...]
            if add_ref is not None:
                acc = acc + add_ref[...]
            o_ref[...] = acc.astype(o_ref.dtype)

        part = dot(a_ref[...].astype(BF16), b_ref[...].astype(BF16))
        if nk == 1:
            finish(part)
        else:
            kk = pl.program_id(2)

            @pl.when(kk == 0)
            def _():
                acc_ref[...] = part

            @pl.when(jnp.logical_and(kk > 0, kk < nk - 1))
            def _():
                acc_ref[...] += part

            @pl.when(kk == nk - 1)
            def _():
                finish(acc_ref[...] + part)

    out_shape = jax.ShapeDtypeStruct((m, n), out_dtype)
    out_spec = pl.BlockSpec((tm, tn), lambda i, j, kk: (i, j))
    acc_shape = (tm, tn) if nk > 1 else (SUBLANES, LANES)
    return pl.pallas_call(
        body, name=name, grid=(m // tm, n // tn, nk), in_specs=in_specs, out_specs=out_spec, out_shape=out_shape,
        scratch_shapes=[pltpu.VMEM(acc_shape, F32)],
        compiler_params=_params(("parallel", "parallel", "arbitrary")),
    )(*operands)


def _wgrad_half(a, b, rows, cols, row_sharded, name, which, *, suffix="", add=None, wire=False, tm=None, tn=None,
                b_split=False):
    tokens = a.shape[0]
    if row_sharded:
        sr, sc = rows // N_CHIPS, cols // N_CORES
    else:
        sr, sc = rows // N_CORES, cols // N_CHIPS
    tm = _tile(sr, 512) if tm is None else tm
    tn = _tile(sc, 1024) if tn is None else tn
    assert sr % tm == 0 and sc % tn == 0, (rows, cols, tm, tn)
    rb, cb = sr // tm, sc // tn
    if tn >= tm:
        ij, grid = (lambda s, t: (t, s)), (N_CHIPS, cb, rb)
    else:
        ij, grid = (lambda s, t: (s, t)), (N_CHIPS, rb, cb)
    if row_sharded:
        a_tile = lambda p, i, j, h: p * rb + i
        b_tile = lambda p, i, j, h: h[0] * cb + j
    else:
        a_tile = lambda p, i, j, h: h[0] * rb + i
        b_tile = lambda p, i, j, h: p * cb + j
    a_spec = pl.BlockSpec((tokens, tm), lambda p, s, t, h: (0, a_tile(p, *ij(s, t), h)))
    if b_split:
        nbh = b.shape[2] // tn
        assert b.shape[2] % tn == 0
        b_spec = pl.BlockSpec((None, tokens, tn), lambda p, s, t, h: (b_tile(p, *ij(s, t), h) // nbh, 0,
                                                                       b_tile(p, *ij(s, t), h) % nbh))
    else:
        b_spec = pl.BlockSpec((tokens, tn), lambda p, s, t, h: (0, b_tile(p, *ij(s, t), h)))
    out_spec = pl.BlockSpec((None, tm, tn), lambda p, s, t, h: (p, *ij(s, t)))
    in_specs, operands = [a_spec, b_spec], [a, b]
    if add is not None:
        in_specs.append(out_spec)
        operands.append(add)

    def body(h_ref, a_ref, b_ref, *rest):
        acc = _dot_tn(a_ref[...].astype(BF16), b_ref[...].astype(BF16))
        if add is not None:
            acc = acc + rest[0][...]
        for o_ref in rest[1 if add is not None else 0:]:
            o_ref[...] = acc.astype(o_ref.dtype)

    out_dtypes = [F32, BF16] if wire else [F32]
    out = pl.pallas_call(
        body, name=name + suffix, out_shape=[jax.ShapeDtypeStruct((N_CHIPS, sr, sc), dt) for dt in out_dtypes],
        grid_spec=pltpu.PrefetchScalarGridSpec(num_scalar_prefetch=1, grid=grid, in_specs=in_specs,
                                               out_specs=[out_spec] * len(out_dtypes)),
        compiler_params=_params(("parallel", "parallel", "parallel")),
    )(which, *operands)
    return tuple(out) if wire else out[0]


def _rms_rows(x):
    return lax.rsqrt(jnp.mean(x * x, axis=-1, keepdims=True) + RMS_EPS)


def _rmsnorm_fwd(x, w, *, name, width=None, col=0, out_dtype=BF16, tr=256):
    rows = x.shape[0]
    width = x.shape[1] if width is None else width
    tr = _tile(rows, tr, SUBLANES)

    def body(x_ref, w_ref, o_ref):
        xv = x_ref[...]
        o_ref[...] = (xv * _rms_rows(xv) * w_ref[...]).astype(o_ref.dtype)

    return pl.pallas_call(
        body, name=name, grid=(rows // tr,),
        in_specs=[pl.BlockSpec((tr, width), lambda i: (i, col)), pl.BlockSpec((1, width), lambda i: (0, 0))],
        out_specs=pl.BlockSpec((tr, width), lambda i: (i, 0)),
        out_shape=jax.ShapeDtypeStruct((rows, width), out_dtype),
        compiler_params=_params(("parallel",)),
    )(x, w)


def _rmsnorm_bwd_rows(xv, w, dy):
    r = _rms_rows(xv)
    n = xv * r
    dn = dy * w
    dx = r * (dn - n * jnp.mean(dn * n, axis=-1, keepdims=True))
    return dx, dy * n


def _rmsnorm_bwd(x, w, dy, *, name, width=None, col=0, dy_col=0, add=None, tr=256, dx_dtypes=(F32,)):
    rows = x.shape[0]
    n_dx = len(dx_dtypes)
    width = x.shape[1] if width is None else width
    tr = _tile(rows, tr, SUBLANES)
    in_specs = [pl.BlockSpec((tr, width), lambda i: (i, col)), pl.BlockSpec((1, width), lambda i: (0, 0)),
                pl.BlockSpec((tr, width), lambda i: (i, dy_col))]
    operands = [x, w, dy]
    if add is not None:
        in_specs.append(pl.BlockSpec((tr, width), lambda i: (i, 0)))
        operands.append(add)

    def body(*refs):
        x_ref, w_ref, dy_ref = refs[:3]
        add_ref = refs[3] if add is not None else None
        dx_refs, dw_ref = refs[-1 - n_dx:-1], refs[-1]
        dx, dwp = _rmsnorm_bwd_rows(x_ref[...], w_ref[...], dy_ref[...])
        if add_ref is not None:
            dx = dx + add_ref[...]
        for dx_ref in dx_refs:
            dx_ref[...] = dx.astype(dx_ref.dtype)
        part = jnp.sum(dwp, axis=0, keepdims=True)

        @pl.when(pl.program_id(0) == 0)
        def _():
            dw_ref[...] = part

        @pl.when(pl.program_id(0) > 0)
        def _():
            dw_ref[...] += part

    return pl.pallas_call(
        body, name=name, grid=(rows // tr,), in_specs=in_specs,
        out_specs=[pl.BlockSpec((tr, width), lambda i: (i, 0))] * n_dx + [pl.BlockSpec((1, width), lambda i: (0, 0))],
        out_shape=[jax.ShapeDtypeStruct((rows, width), dt) for dt in dx_dtypes] + [jax.ShapeDtypeStruct((1, width), F32)],
        compiler_params=_params(("arbitrary",)),
    )(*operands)


def _final_norm_loss(h, w, target, *, tr=256):
    rows, d = h.shape
    tr = _tile(rows, tr, SUBLANES)

    def body(h_ref, w_ref, t_ref, loss_ref, dh_ref, dhb_ref, dw_ref):
        hv, wv = h_ref[...], w_ref[...]
        r = _rms_rows(hv)
        n = hv * r
        err = n * wv - t_ref[...]
        d_out = err * (1.0 / d)
        dn = d_out * wv
        dh = r * (dn - n * jnp.mean(dn * n, axis=-1, keepdims=True))
        dh_ref[...] = dh
        dhb_ref[...] = dh.astype(BF16)
        dw_part = jnp.sum(d_out * n, axis=0, keepdims=True)
        loss_part = jnp.full((SUBLANES, LANES), 0.5 / d, F32) * jnp.sum(err * err)

        @pl.when(pl.program_id(0) == 0)
        def _():
            dw_ref[...] = dw_part
            loss_ref[...] = loss_part

        @pl.when(pl.program_id(0) > 0)
        def _():
            dw_ref[...] += dw_part
            loss_ref[...] += loss_part

    return pl.pallas_call(
        body, name="final_norm_loss", grid=(rows // tr,),
        in_specs=[pl.BlockSpec((tr, d), lambda i: (i, 0)), pl.BlockSpec((1, d), lambda i: (0, 0)),
                  pl.BlockSpec((tr, d), lambda i: (i, 0))],
        out_specs=[pl.BlockSpec((SUBLANES, LANES), lambda i: (0, 0)), pl.BlockSpec((tr, d), lambda i: (i, 0)),
                   pl.BlockSpec((tr, d), lambda i: (i, 0)), pl.BlockSpec((1, d), lambda i: (0, 0))],
        out_shape=[jax.ShapeDtypeStruct((SUBLANES, LANES), F32), jax.ShapeDtypeStruct((rows, d), F32),
                   jax.ShapeDtypeStruct((rows, d), BF16), jax.ShapeDtypeStruct((1, d), F32)],
        compiler_params=_params(("arbitrary",)),
    )(h, w, target)


def _cmul(ar, ai, br, bi):
    return ar * br - ai * bi, ar * bi + ai * br


def _dot_exact(a, b, dims):
    return lax.dot_general(a, b, (dims, ((), ())), preferred_element_type=F32, precision=lax.Precision.HIGHEST)


def _s5_discretize(lr, li, dt):
    mag = jnp.exp(lr * dt)
    th = li * dt
    ar, ai = mag * jnp.cos(th), mag * jnp.sin(th)
    nr, ni = ar - 1.0, ai
    den = lr * lr + li * li
    zr = (nr * lr + ni * li) / den
    zi = (ni * lr - nr * li) / den
    return mag, ar, ai, nr, ni, den, zr, zi


def _band_slices(group):
    j, gi = divmod(group, GROUPS_PER_BATCH)
    rows = slice(gi * SSM_GROUP, (gi + 1) * SSM_GROUP)
    re = slice(gi * SSM_STATE, (gi + 1) * SSM_STATE)
    im = slice(STATE_PER_BATCH + gi * SSM_STATE, STATE_PER_BATCH + (gi + 1) * SSM_STATE)
    return j, rows, re, im


def _s5_bands(lam_re, lam_im, log_dt, b_re, b_im, c_re, c_im):
    g, _ = lam_re.shape
    nb = g // GROUPS_PER_BATCH
    s2 = 2 * STATE_PER_BATCH

    def body(lr_ref, li_ref, ldt_ref, br_ref, bi_ref, cr_ref, ci_ref, wb_ref, wct_ref, a_ref):
        dt = jnp.exp(ldt_ref[...])
        _, ar, ai, _, _, _, zr, zi = _s5_discretize(lr_ref[...], li_ref[...], dt)
        wb_ref[...] = jnp.zeros_like(wb_ref)
        wct_ref[...] = jnp.zeros_like(wct_ref)
        for group in range(g):
            j, rows, re, im = _band_slices(group)
            zr_g, zi_g = zr[group:group + 1, :], zi[group:group + 1, :]
            bre, bim = br_ref[group], bi_ref[group]
            wb_ref[j, rows, re] = (zr_g * bre - zi_g * bim).astype(BF16)
            wb_ref[j, rows, im] = (zr_g * bim + zi_g * bre).astype(BF16)
            wct_ref[j, rows, re] = cr_ref[group].astype(BF16)
            wct_ref[j, rows, im] = (-ci_ref[group]).astype(BF16)
            a_ref[j, :, re] = ar[group:group + 1, :]
            a_ref[j, :, im] = ai[group:group + 1, :]

    return pl.pallas_call(
        body, name="s5_bands",
        out_shape=[jax.ShapeDtypeStruct((nb, LANES, s2), BF16)] * 2 + [jax.ShapeDtypeStruct((nb, 1, s2), F32)],
    )(lam_re, lam_im, log_dt, b_re, b_im, c_re, c_im)


def _s5_bands_bwd(lam_re, lam_im, log_dt, b_re, b_im, dwb, dwct, dabar):
    g, p = lam_re.shape
    gh = b_re.shape[1:]

    def body(lr_ref, li_ref, ldt_ref, br_ref, bi_ref, dwb_ref, dwct_ref, da_ref,
             dlr_ref, dli_ref, dldt_ref, dbre_ref, dbim_ref, dcre_ref, dcim_ref, dzr_ref, dzi_ref, dar_ref, dai_ref):
        lr, li = lr_ref[...], li_ref[...]
        dt = jnp.exp(ldt_ref[...])
        mag, ar, ai, nr, ni, den, zr, zi = _s5_discretize(lr, li, dt)
        for group in range(g):
            j, rows, re, im = _band_slices(group)
            zr_g, zi_g = zr[group:group + 1, :], zi[group:group + 1, :]
            bre, bim = br_ref[group], bi_ref[group]
            dbr, dbi = dwb_ref[j, rows, re], dwb_ref[j, rows, im]
            dbre_ref[group] = zr_g * dbr + zi_g * dbi
            dbim_ref[group] = zr_g * dbi - zi_g * dbr
            dzr_ref[group:group + 1, :] = jnp.sum(bre * dbr + bim * dbi, axis=0, keepdims=True)
            dzi_ref[group:group + 1, :] = jnp.sum(bre * dbi - bim * dbr, axis=0, keepdims=True)
            dcre_ref[group] = dwct_ref[j, rows, re]
            dcim_ref[group] = -dwct_ref[j, rows, im]
            dar_ref[group:group + 1, :] = da_ref[j, :, re]
            dai_ref[group:group + 1, :] = da_ref[j, :, im]
        dzr, dzi = dzr_ref[...], dzi_ref[...]
        inv = 1.0 / den
        d_nr = (dzr * lr - dzi * li) * inv
        d_ni = (dzr * li + dzi * lr) * inv
        d_den = -(dzr * zr + dzi * zi) * inv
        d_lr = (dzr * nr + dzi * ni) * inv + 2.0 * lr * d_den
        d_li = (dzr * ni - dzi * nr) * inv + 2.0 * li * d_den
        t_ar = dar_ref[...] + d_nr
        t_ai = dai_ref[...] + d_ni
        d_lrdt = t_ar * ar + t_ai * ai
        d_th = t_ai * ar - t_ar * ai
        dlr_ref[...] = d_lr + d_lrdt * dt
        dli_ref[...] = d_li + d_th * dt
        dldt_ref[...] = jnp.sum(d_lrdt * lr + d_th * li, axis=1, keepdims=True) * dt

    return pl.pallas_call(
        body, name="s5_bands_bwd",
        out_shape=[jax.ShapeDtypeStruct((g, p), F32)] * 2 + [jax.ShapeDtypeStruct((g, 1), F32)]
        + [jax.ShapeDtypeStruct((g,) + gh, F32)] * 4,
        scratch_shapes=[pltpu.VMEM((g, p), F32)] * 4,
    )(lam_re, lam_im, log_dt, b_re, b_im, dwb, dwct, dabar)


def _powers(ar, ai, count):
    out = [(ar, ai)]
    for _ in range(count - 1):
        out.append(_cmul(out[-1][0], out[-1][1], ar, ai))
    return out


def _scan_coefs(ar, ai, reverse):
    w = ar.shape[-1]
    pw = _powers(ar, ai, SUBLANES)
    row = lax.broadcasted_iota(jnp.int32, (SUBLANES, w), 0)
    steps = []
    d = 1
    while d < SUBLANES:
        keep = (row < SUBLANES - d) if reverse else (row >= d)
        pr, pi = pw[d - 1]
        steps.append((d, jnp.where(keep, pr, 0.0), jnp.where(keep, pi, 0.0)))
        d *= 2
    cr = jnp.zeros((SUBLANES, w), F32)
    ci = jnp.zeros((SUBLANES, w), F32)
    for t in range(SUBLANES):
        pr, pi = pw[SUBLANES - 1 - t] if reverse else pw[t]
        cr = jnp.where(row == t, pr, cr)
        ci = jnp.where(row == t, pi, ci)
    return steps, cr, ci


def _scan_tile(xr, xi, carry_r, carry_i, coefs, reverse):
    steps, cr, ci = coefs
    for d, mr, mi in steps:
        shift = SUBLANES - d if reverse else d
        sr, si = pltpu.roll(xr, shift, 0), pltpu.roll(xi, shift, 0)
        pr, pi = _cmul(mr, mi, sr, si)
        xr, xi = xr + pr, xi + pi
    pr, pi = _cmul(cr, ci, carry_r, carry_i)
    return xr + pr, xi + pi


def _gelu(x):
    c = math.sqrt(2.0 / math.pi)
    return 0.5 * x * (1.0 + jnp.tanh(c * (x + 0.044715 * x * x * x)))


def _gelu_grad(x):
    c = math.sqrt(2.0 / math.pi)
    t = jnp.tanh(c * (x + 0.044715 * x * x * x))
    return 0.5 * (1.0 + t) + 0.5 * x * (1.0 - t * t) * c * (1.0 + 3.0 * 0.044715 * x * x)


def _s5_fwd(proj, wb, wct, d_skip, abar):
    rows = proj.shape[0]
    nb = wb.shape[0]
    s2 = 2 * STATE_PER_BATCH
    st = STATE_PER_BATCH
    chunk = _tile(rows, 512, SUBLANES)

    def body(u_ref, wb_ref, wc_ref, d_ref, a_ref, s_ref, y_ref, yg_ref):
        for c0 in range(0, rows, chunk):
            s_ref[pl.ds(c0, chunk), :] = _dot_nn(u_ref[pl.ds(c0, chunk), :].astype(BF16), wb_ref[...])
        av = a_ref[...]
        coefs = _scan_coefs(av[:, :st], av[:, st:], reverse=False)

        def tile(b, carry):
            r0 = pl.multiple_of(b * SUBLANES, SUBLANES)
            xr, xi = _scan_tile(s_ref[pl.ds(r0, SUBLANES), :st], s_ref[pl.ds(r0, SUBLANES), st:], carry[0], carry[1],
                                coefs, False)
            s_ref[pl.ds(r0, SUBLANES), :st] = xr
            s_ref[pl.ds(r0, SUBLANES), st:] = xi
            return xr[SUBLANES - 1:, :], xi[SUBLANES - 1:, :]

        zero = jnp.zeros((1, st), F32)
        lax.fori_loop(0, rows // SUBLANES, tile, (zero, zero))
        for c0 in range(0, rows, chunk):
            y = _dot_nt(s_ref[pl.ds(c0, chunk), :].astype(BF16), wc_ref[...]) + d_ref[...] * u_ref[pl.ds(c0, chunk), :]
            y_ref[pl.ds(c0, chunk), :] = y
            yg_ref[pl.ds(c0, chunk), :] = _gelu(y).astype(BF16)

    return pl.pallas_call(
        body, name="s5_fwd", grid=(nb,),
        in_specs=[pl.BlockSpec((rows, LANES), lambda j: (0, j)), pl.BlockSpec((None, LANES, s2), lambda j: (j, 0, 0)),
                  pl.BlockSpec((None, LANES, s2), lambda j: (j, 0, 0)), pl.BlockSpec((1, LANES), lambda j: (0, j)),
                  pl.BlockSpec((None, 1, s2), lambda j: (j, 0, 0))],
        out_specs=[pl.BlockSpec((rows, s2), lambda j: (0, j)), pl.BlockSpec((rows, LANES), lambda j: (0, j)),
                   pl.BlockSpec((rows, LANES), lambda j: (0, j))],
        out_shape=[jax.ShapeDtypeStruct((rows, nb * s2), F32), jax.ShapeDtypeStruct((rows, nb * LANES), F32),
                   jax.ShapeDtypeStruct((rows, nb * LANES), BF16)],
        compiler_params=_params(("parallel",)),
    )(proj, wb, wct, d_skip, abar)


def _s5_bwd(proj, states, y_pre, dyg_a, dyg_b, wb, wct, d_skip, abar):
    rows = proj.shape[0]
    nb = wb.shape[0]
    s2 = 2 * STATE_PER_BATCH
    st = STATE_PER_BATCH
    chunk = _tile(rows, 512, SUBLANES)
    n_tiles = rows // SUBLANES

    def body(u_ref, s_ref, y_ref, ga_ref, gb_ref, wb_ref, wc_ref, d_ref, a_ref,
             du_ref, dwb_ref, dwc_ref, da_ref, dd_ref, ds_ref, dy_ref):
        dy_ref[...] = (ga_ref[...] + gb_ref[...]) * _gelu_grad(y_ref[...])
        dd_ref[...] = jnp.sum(dy_ref[...] * u_ref[...], axis=0, keepdims=True)
        for c0 in range(0, rows, chunk):
            ds_ref[pl.ds(c0, chunk), :] = _dot_nn(dy_ref[pl.ds(c0, chunk), :].astype(BF16), wc_ref[...])
        dwc_ref[...] = _dot_tn(dy_ref[...].astype(BF16), s_ref[...].astype(BF16))
        av = a_ref[...]
        coefs = _scan_coefs(av[:, :st], -av[:, st:], reverse=True)
        row = lax.broadcasted_iota(jnp.int32, (SUBLANES, st), 0)

        def tile(k, carry):
            cr, ci, acc_r, acc_i = carry
            b = n_tiles - 1 - k
            r0 = pl.multiple_of(b * SUBLANES, SUBLANES)
            rp = pl.multiple_of(jnp.maximum(b - 1, 0) * SUBLANES, SUBLANES)
            xr, xi = _scan_tile(ds_ref[pl.ds(r0, SUBLANES), :st], ds_ref[pl.ds(r0, SUBLANES), st:], cr, ci, coefs, True)
            ds_ref[pl.ds(r0, SUBLANES), :st] = xr
            ds_ref[pl.ds(r0, SUBLANES), st:] = xi
            first = jnp.where(b > 0, 1.0, 0.0)
            pr = jnp.where(row == 0, pltpu.roll(s_ref[pl.ds(rp, SUBLANES), :st], 1, 0) * first,
                           pltpu.roll(s_ref[pl.ds(r0, SUBLANES), :st], 1, 0))
            pi = jnp.where(row == 0, pltpu.roll(s_ref[pl.ds(rp, SUBLANES), st:], 1, 0) * first,
                           pltpu.roll(s_ref[pl.ds(r0, SUBLANES), st:], 1, 0))
            acc_r = acc_r + pr * xr + pi * xi
            acc_i = acc_i + pr * xi - pi * xr
            return xr[:1, :], xi[:1, :], acc_r, acc_i

        zero = jnp.zeros((1, st), F32)
        zacc = jnp.zeros((SUBLANES, st), F32)
        _, _, acc_r, acc_i = lax.fori_loop(0, n_tiles, tile, (zero, zero, zacc, zacc))
        da_ref[:, :st] = jnp.sum(acc_r, axis=0, keepdims=True)
        da_ref[:, st:] = jnp.sum(acc_i, axis=0, keepdims=True)
        for c0 in range(0, rows, chunk):
            du_ref[pl.ds(c0, chunk), :] = (_dot_nt(ds_ref[pl.ds(c0, chunk), :].astype(BF16), wb_ref[...])
                                           + d_ref[...] * dy_ref[pl.ds(c0, chunk), :]).astype(du_ref.dtype)
        dwb_ref[...] = _dot_tn(u_ref[...].astype(BF16), ds_ref[...].astype(BF16))

    col = pl.BlockSpec((rows, LANES), lambda j: (0, j))
    return pl.pallas_call(
        body, name="s5_bwd", grid=(nb,),
        in_specs=[col, pl.BlockSpec((rows, s2), lambda j: (0, j)), col, col, col,
                  pl.BlockSpec((None, LANES, s2), lambda j: (j, 0, 0)), pl.BlockSpec((None, LANES, s2), lambda j: (j, 0, 0)),
                  pl.BlockSpec((1, LANES), lambda j: (0, j)), pl.BlockSpec((None, 1, s2), lambda j: (j, 0, 0))],
        out_specs=[col, pl.BlockSpec((None, LANES, s2), lambda j: (j, 0, 0)),
                   pl.BlockSpec((None, LANES, s2), lambda j: (j, 0, 0)), pl.BlockSpec((None, 1, s2), lambda j: (j, 0, 0)),
                   pl.BlockSpec((1, LANES), lambda j: (0, j))],
        out_shape=[jax.ShapeDtypeStruct((rows, nb * LANES), BF16), jax.ShapeDtypeStruct((nb, LANES, s2), F32),
                   jax.ShapeDtypeStruct((nb, LANES, s2), F32), jax.ShapeDtypeStruct((nb, 1, s2), F32),
                   jax.ShapeDtypeStruct((1, nb * LANES), F32)],
        scratch_shapes=[pltpu.VMEM((rows, s2), F32), pltpu.VMEM((rows, LANES), F32)],
        compiler_params=_params(("parallel",)),
    )(proj, states, y_pre, dyg_a, dyg_b, wb, wct, d_skip, abar)


def _glu_norm_fwd(y_pre, z, w, *, tr=256):
    rows, width = y_pre.shape
    tr = _tile(rows, tr, SUBLANES)

    def body(y_ref, z_ref, w_ref, o_ref):
        v = _gelu(y_ref[...]) * jax.nn.sigmoid(z_ref[...])
        o_ref[...] = (v * _rms_rows(v) * w_ref[...]).astype(o_ref.dtype)

    blk = pl.BlockSpec((tr, width), lambda i: (i, 0))
    return pl.pallas_call(
        body, name="glu_norm_fwd", grid=(rows // tr,),
        in_specs=[blk, blk, pl.BlockSpec((1, width), lambda i: (0, 0))], out_specs=blk,
        out_shape=jax.ShapeDtypeStruct((rows, width), BF16), compiler_params=_params(("parallel",)),
    )(y_pre, z, w)


def _glu_norm_bwd(y_pre, z, w, dycat, *, tr=256):
    rows, width = y_pre.shape
    tr = _tile(rows, tr, SUBLANES)

    def body(y_ref, z_ref, w_ref, dy_ref, dz_ref, dg_ref, dw_ref, db_ref):
        yg = _gelu(y_ref[...])
        sg = jax.nn.sigmoid(z_ref[...])
        dv, dwp = _rmsnorm_bwd_rows(yg * sg, w_ref[...], dy_ref[...])
        dz = dv * yg * sg * (1.0 - sg)
        dz_ref[...] = dz.astype(dz_ref.dtype)
        dg_ref[...] = dv * sg
        dw_part = jnp.sum(dwp, axis=0, keepdims=True)
        db_part = jnp.sum(dz, axis=0, keepdims=True)

        @pl.when(pl.program_id(0) == 0)
        def _():
            dw_ref[...] = dw_part
            db_ref[...] = db_part

        @pl.when(pl.program_id(0) > 0)
        def _():
            dw_ref[...] += dw_part
            db_ref[...] += db_part

    blk = pl.BlockSpec((tr, width), lambda i: (i, 0))
    vec = pl.BlockSpec((1, width), lambda i: (0, 0))
    return pl.pallas_call(
        body, name="glu_norm_bwd", grid=(rows // tr,), in_specs=[blk, blk, vec, blk], out_specs=[blk, blk, vec, vec],
        out_shape=[jax.ShapeDtypeStruct((rows, width), BF16), jax.ShapeDtypeStruct((rows, width), F32)]
        + [jax.ShapeDtypeStruct((1, width), F32)] * 2,
        compiler_params=_params(("arbitrary",)),
    )(y_pre, z, w, dycat)


def _rope_tables(pos, freq, sign):
    rows = pos.shape[0]

    def body(p_ref, f_ref, s_ref, cos_ref, sin_ref):
        ang = p_ref[...] * f_ref[...]
        cos_ref[...] = jnp.cos(ang)
        sin_ref[...] = jnp.sin(ang) * s_ref[...]

    return pl.pallas_call(body, name="rope_tables", out_shape=[jax.ShapeDtypeStruct((rows, LANES), F32)] * 2)(pos, freq, sign)


def _rope(x, cos, sin_signed):
    half = QK_ROPE_DIM // 2
    src = lax.broadcasted_iota(jnp.int32, (LANES, LANES), 0)
    dst = lax.broadcasted_iota(jnp.int32, (LANES, LANES), 1)
    swap = jnp.where(jnp.logical_or(jnp.logical_and(dst < half, src == dst + half),
                                    jnp.logical_and(jnp.logical_and(dst >= half, dst < 2 * half), src == dst - half)),
                     1.0, 0.0).astype(F32)
    swapped = _dot_exact(x, swap, ((1,), (0,)))
    return x * cos + swapped * sin_signed


def _attn_prep(q, kv, proj, kpe_col, cos, sin, *, tr=256):
    rows = q.shape[0]
    heads = q.shape[1] // HEAD_SLOT
    tr = _tile(rows, tr, SUBLANES)

    def body(q_ref, kv_ref, kpe_ref, cos_ref, sin_ref, qc_ref, kc_ref, v_ref):
        c, s = cos_ref[...], sin_ref[...]
        kpe = _rope(kpe_ref[...], c, s).astype(BF16)
        for h in range(heads):
            nope = slice(h * HEAD_SLOT, h * HEAD_SLOT + LANES)
            pe = slice(h * HEAD_SLOT + LANES, (h + 1) * HEAD_SLOT)
            qc_ref[:, nope] = q_ref[:, nope].astype(BF16)
            qc_ref[:, pe] = _rope(q_ref[:, pe], c, s).astype(BF16)
            kc_ref[:, nope] = kv_ref[:, nope].astype(BF16)
            kc_ref[:, pe] = kpe
            v_ref[:, h * LANES:(h + 1) * LANES] = kv_ref[:, pe].astype(BF16)

    slots = pl.BlockSpec((tr, heads * HEAD_SLOT), lambda i: (i, 0))
    tab = pl.BlockSpec((tr, LANES), lambda i: (i, 0))
    return pl.pallas_call(
        body, name="attn_prep", grid=(rows // tr,),
        in_specs=[slots, slots, pl.BlockSpec((tr, LANES), lambda i: (i, kpe_col)), tab, tab],
        out_specs=[slots, slots, pl.BlockSpec((tr, heads * LANES), lambda i: (i, 0))],
        out_shape=[jax.ShapeDtypeStruct((rows, heads * HEAD_SLOT), BF16)] * 2
        + [jax.ShapeDtypeStruct((rows, heads * LANES), BF16)],
        compiler_params=_params(("parallel",)),
    )(q, kv, proj, cos, sin)


def _causal(tq, tk):
    return lax.broadcasted_iota(jnp.int32, (tq, tk), 1) <= lax.broadcasted_iota(jnp.int32, (tq, tk), 0)


def _attn_fwd(qc, kc, vb, *, scale, tq=512):
    rows = qc.shape[0]
    heads = qc.shape[1] // HEAD_SLOT
    tq = _tile(rows, tq, SUBLANES)
    tk = tq

    def body(q_ref, k_ref, v_ref, o_ref, lse_ref):
        i = pl.program_id(1)
        q = q_ref[...]

        def step(j, carry, diagonal):
            m, l, acc = carry
            k0 = pl.multiple_of(j * tk, tk)
            s = _dot_nt(q, k_ref[pl.ds(k0, tk), :]) * scale
            if diagonal:
                s = jnp.where(_causal(tq, tk), s, NEG_INF)
            m_new = jnp.maximum(m, jnp.max(s, axis=-1, keepdims=True))
            p = jnp.exp(s - m_new)
            alpha = jnp.exp(m - m_new)
            l = alpha * l + jnp.sum(p, axis=-1, keepdims=True)
            acc = alpha * acc + _dot_nn(p.astype(BF16), v_ref[pl.ds(k0, tk), :])
            return m_new, l, acc

        init = (jnp.full((tq, 1), NEG_INF, F32), jnp.zeros((tq, 1), F32), jnp.zeros((tq, LANES), F32))
        below = lax.fori_loop(0, i, lambda j, carry: step(j, carry, False), init)
        m, l, acc = step(i, below, True)
        o_ref[...] = acc / l
        lse_ref[...] = jnp.broadcast_to(m + jnp.log(l), (tq, LANES))

    return pl.pallas_call(
        body, name="attn_fwd", grid=(heads, rows // tq),
        in_specs=[pl.BlockSpec((tq, HEAD_SLOT), lambda h, i: (i, h)), pl.BlockSpec((rows, HEAD_SLOT), lambda h, i: (0, h)),
                  pl.BlockSpec((rows, LANES), lambda h, i: (0, h))],
        out_specs=[pl.BlockSpec((tq, LANES), lambda h, i: (i, h))] * 2,
        out_shape=[jax.ShapeDtypeStruct((rows, heads * LANES), F32)] * 2,
        compiler_params=_params(("parallel", "parallel")),
    )(qc, kc, vb)


def _attn_bwd(qc, kc, vb, o, do, lse, cos, sin, *, scale, tk=512):
    rows = qc.shape[0]
    heads = qc.shape[1] // HEAD_SLOT
    tk = _tile(rows, tk, SUBLANES)
    tq = tk
    nq = rows // tq

    def body(q_ref, k_ref, v_ref, o_ref, do_ref, lse_ref, cos_ref, sin_ref, dq_ref, dkv_ref, dkpe_ref, dq_acc, delta_ref):
        j = pl.program_id(1)

        @pl.when(j == 0)
        def _():
            dq_acc[...] = jnp.zeros_like(dq_acc)
            for r0 in range(0, rows, tq):
                d = jnp.sum(do_ref[pl.ds(r0, tq), :] * o_ref[pl.ds(r0, tq), :], axis=-1, keepdims=True)
                delta_ref[pl.ds(r0, tq), :] = jnp.broadcast_to(d, (tq, LANES))

        kb, vv = k_ref[...], v_ref[...]

        def step(i, carry, diagonal):
            dk, dv = carry
            q0 = pl.multiple_of(i * tq, tq)
            qb = q_ref[pl.ds(q0, tq), :]
            dob = do_ref[pl.ds(q0, tq), :].astype(BF16)
            s = _dot_nt(qb, kb) * scale
            p = jnp.exp(s - lse_ref[pl.ds(q0, tq), :1])
            if diagonal:
                p = jnp.where(_causal(tq, tk), p, 0.0)
            dv = dv + _dot_tn(p.astype(BF16), dob)
            ds = (p * (_dot_nt(dob, vv) - delta_ref[pl.ds(q0, tq), :1])).astype(BF16)
            dk = dk + _dot_tn(ds, qb)
            dq_acc[pl.ds(q0, tq), :] += _dot_nn(ds, kb)
            return dk, dv

        zero = (jnp.zeros((tk, HEAD_SLOT), F32), jnp.zeros((tk, LANES), F32))
        dk, dv = lax.fori_loop(j + 1, nq, lambda i, carry: step(i, carry, False), step(j, zero, True))
        dkv_ref[:, :LANES] = (dk[:, :LANES] * scale).astype(dkv_ref.dtype)
        dkv_ref[:, LANES:] = dv.astype(dkv_ref.dtype)
        dkpe_ref[...] = dk[:, LANES:] * scale

        @pl.when(j == nq - 1)
        def _():
            for r0 in range(0, rows, tq):
                dq = dq_acc[pl.ds(r0, tq), :] * scale
                dq_ref[pl.ds(r0, tq), :LANES] = dq[:, :LANES].astype(dq_ref.dtype)
                dq_ref[pl.ds(r0, tq), LANES:] = _rope(dq[:, LANES:], cos_ref[pl.ds(r0, tq), :],
                                                      -sin_ref[pl.ds(r0, tq), :]).astype(dq_ref.dtype)

    full_q = pl.BlockSpec((rows, HEAD_SLOT), lambda h, j: (0, h))
    full_v = pl.BlockSpec((rows, LANES), lambda h, j: (0, h))
    tab = pl.BlockSpec((rows, LANES), lambda h, j: (0, 0))
    return pl.pallas_call(
        body, name="attn_bwd", grid=(heads, rows // tk),
        in_specs=[full_q, pl.BlockSpec((tk, HEAD_SLOT), lambda h, j: (j, h)), pl.BlockSpec((tk, LANES), lambda h, j: (j, h)),
                  full_v, full_v, full_v, tab, tab],
        out_specs=[full_q, pl.BlockSpec((tk, HEAD_SLOT), lambda h, j: (j, h)), pl.BlockSpec((tk, LANES), lambda h, j: (j, h))],
        out_shape=[jax.ShapeDtypeStruct((rows, heads * HEAD_SLOT), BF16), jax.ShapeDtypeStruct((rows, heads * HEAD_SLOT), BF16),
                   jax.ShapeDtypeStruct((rows, heads * LANES), F32)],
        scratch_shapes=[pltpu.VMEM((rows, HEAD_SLOT), F32), pltpu.VMEM((rows, LANES), F32)],
        compiler_params=_params(("parallel", "arbitrary")),
    )(qc, kc, vb, o, do, lse, cos, sin)


def _kpe_bwd(dkpe_heads, cos, sin, *, tr=512):
    rows = dkpe_heads.shape[0]
    heads = dkpe_heads.shape[1] // LANES
    tr = _tile(rows, tr, 2 * SUBLANES)

    def body(d_ref, cos_ref, sin_ref, o_ref):
        acc = d_ref[:, :LANES]
        for h in range(1, heads):
            acc = acc + d_ref[:, h * LANES:(h + 1) * LANES]
        o_ref[...] = _rope(acc, cos_ref[...], -sin_ref[...]).astype(o_ref.dtype)

    tab = pl.BlockSpec((tr, LANES), lambda i: (i, 0))
    return pl.pallas_call(
        body, name="kpe_bwd", grid=(rows // tr,),
        in_specs=[pl.BlockSpec((tr, heads * LANES), lambda i: (i, 0)), tab, tab], out_specs=tab,
        out_shape=jax.ShapeDtypeStruct((rows, LANES), BF16), compiler_params=_params(("parallel",)),
    )(dkpe_heads, cos, sin)


CONV_ROWS = 128


def _with_halo(ref, r0, ci, n_chunks, ch, lanes, before, after):
    parts = []
    if before:
        lo = pl.multiple_of(jnp.maximum(r0 - SUBLANES, 0), SUBLANES)
        parts.append(ref[pl.ds(lo, SUBLANES), lanes] * jnp.where(ci > 0, 1.0, 0.0))
    parts.append(ref[pl.ds(r0, ch), lanes])
    if after:
        hi = pl.multiple_of(jnp.minimum(r0 + ch, n_chunks * ch - SUBLANES), SUBLANES)
        parts.append(ref[pl.ds(hi, SUBLANES), lanes] * jnp.where(ci < n_chunks - 1, 1.0, 0.0))
    return jnp.concatenate(parts, axis=0)


def _taps(ext):
    return pltpu.roll(ext, 2, 0)[SUBLANES:], pltpu.roll(ext, 1, 0)[SUBLANES:], ext[SUBLANES:]


def _conv3(taps, w, b):
    return w[0:1, :] * taps[0] + w[1:2, :] * taps[1] + w[2:3, :] * taps[2] + b


def _conv_gate_fwd(a, conv_w, conv_b, *, tc=256):
    rows, f2 = a.shape
    f = f2 // 2
    tc = _tile(f, tc)
    nc = f // tc
    ch = _tile(rows, CONV_ROWS, SUBLANES)
    n_chunks = rows // ch

    def body(ag_ref, av_ref, wg_ref, wv_ref, bg_ref, bv_ref, o_ref):
        for lt in range(tc // LANES):
            lanes = slice(lt * LANES, (lt + 1) * LANES)
            wg, wv, bg, bv = wg_ref[:, lanes], wv_ref[:, lanes], bg_ref[:, lanes], bv_ref[:, lanes]

            def chunk(ci, carry):
                r0 = pl.multiple_of(ci * ch, ch)
                gate = _conv3(_taps(_with_halo(ag_ref, r0, ci, n_chunks, ch, lanes, True, False)), wg, bg)
                val = _conv3(_taps(_with_halo(av_ref, r0, ci, n_chunks, ch, lanes, True, False)), wv, bv)
                o_ref[pl.ds(r0, ch), lanes] = (gate * jax.nn.sigmoid(gate) * val).astype(o_ref.dtype)
                return carry

            lax.fori_loop(0, n_chunks, chunk, 0)

    return pl.pallas_call(
        body, name="conv_gate_fwd", grid=(nc,),
        in_specs=[pl.BlockSpec((rows, tc), lambda j: (0, j)), pl.BlockSpec((rows, tc), lambda j: (0, j + nc)),
                  pl.BlockSpec((SUBLANES, tc), lambda j: (0, j)), pl.BlockSpec((SUBLANES, tc), lambda j: (0, j + nc)),
                  pl.BlockSpec((1, tc), lambda j: (0, j)), pl.BlockSpec((1, tc), lambda j: (0, j + nc))],
        out_specs=pl.BlockSpec((rows, tc), lambda j: (0, j)),
        out_shape=jax.ShapeDtypeStruct((rows, f), BF16), compiler_params=_params(("parallel",)),
    )(a, a, conv_w, conv_w, conv_b, conv_b)


def _conv_gate_bwd(a, conv_w, conv_b, dg, *, tc=256):
    rows, f2 = a.shape
    f = f2 // 2
    tc = _tile(f, tc)
    nc = f // tc
    ch = _tile(rows, CONV_ROWS, SUBLANES)
    n_chunks = rows // ch
    ext_rows = ch + SUBLANES

    def fold(x):
        return jnp.sum(x.reshape(ch // SUBLANES, SUBLANES, LANES), axis=0)

    def body(ag_ref, av_ref, wg_ref, wv_ref, bg_ref, bv_ref, dg_ref, da_ref, dw_ref, db_ref):
        for lt in range(tc // LANES):
            lanes = slice(lt * LANES, (lt + 1) * LANES)
            wg, wv, bg, bv = wg_ref[:, lanes], wv_ref[:, lanes], bg_ref[:, lanes], bv_ref[:, lanes]

            def chunk(ci, acc):
                r0 = pl.multiple_of(ci * ch, ch)
                taps_g = _taps(_with_halo(ag_ref, r0, ci, n_chunks, ch, lanes, True, True))
                taps_v = _taps(_with_halo(av_ref, r0, ci, n_chunks, ch, lanes, True, True))
                dge = _with_halo(dg_ref, r0, ci, n_chunks, ch, lanes, False, True)
                gate, val = _conv3(taps_g, wg, bg), _conv3(taps_v, wv, bv)
                sg = jax.nn.sigmoid(gate)
                d_gate = dge * val * sg * (1.0 + gate * (1.0 - sg))
                d_val = dge * gate * sg
                new = []
                for half, (taps, w, d) in enumerate(((taps_g, wg, d_gate), (taps_v, wv, d_val))):
                    da = (w[2:3, :] * d[:ch] + w[1:2, :] * pltpu.roll(d, ext_rows - 1, 0)[:ch]
                          + w[0:1, :] * pltpu.roll(d, ext_rows - 2, 0)[:ch])
                    da_ref[half, pl.ds(r0, ch), lanes] = da.astype(da_ref.dtype)
                    dc = d[:ch]
                    sums = [fold(dc)] + [fold(dc * t[:ch]) for t in taps]
                    new.append(tuple(x + s for x, s in zip(acc[half], sums)))
                return tuple(new)

            zero = tuple(jnp.zeros((SUBLANES, LANES), F32) for _ in range(4))
            acc = lax.fori_loop(0, n_chunks, chunk, (zero, zero))
            row = lax.broadcasted_iota(jnp.int32, (SUBLANES, LANES), 0)
            for half in range(2):
                db, *taps = (jnp.sum(x, axis=0, keepdims=True) for x in acc[half])
                db_ref[half, :, lanes] = db
                dw = jnp.zeros((SUBLANES, LANES), F32)
                for tap in range(3):
                    dw = jnp.where(row == tap, taps[tap], dw)
                dw_ref[half, :, lanes] = dw

    lo = lambda j: (0, j)
    hi = lambda j: (0, j + nc)
    both = lambda j: (0, 0, j)
    return pl.pallas_call(
        body, name="conv_gate_bwd", grid=(nc,),
        in_specs=[pl.BlockSpec((rows, tc), lo), pl.BlockSpec((rows, tc), hi), pl.BlockSpec((SUBLANES, tc), lo),
                  pl.BlockSpec((SUBLANES, tc), hi), pl.BlockSpec((1, tc), lo), pl.BlockSpec((1, tc), hi),
                  pl.BlockSpec((rows, tc), lo)],
        out_specs=[pl.BlockSpec((2, rows, tc), both), pl.BlockSpec((2, SUBLANES, tc), both), pl.BlockSpec((2, 1, tc), both)],
        out_shape=[jax.ShapeDtypeStruct((2, rows, f), BF16), jax.ShapeDtypeStruct((2, SUBLANES, f), F32),
                   jax.ShapeDtypeStruct((2, 1, f), F32)],
        compiler_params=_params(("parallel",)),
    )(a, a, conv_w, conv_w, conv_b, conv_b, dg)


def _wgrad(a, b, rows, cols, row_sharded, name, **kw):
    return functools.partial(_wgrad_half, a, b, rows, cols, row_sharded, name, **kw)


class _NoExchange:
    def __init__(self, later, ffn):
        self.later, self.ffn = later, ffn

    def mixer_weights(self, after):
        return self.later

    def ffn_weights_arrived(self, after):
        return None

    def ffn_weights(self, after):
        return self.ffn

    def ffn_down_arrived(self, after):
        return None

    def ffn_down_weight(self, after):
        return self.ffn["ffn_w_down"]

    def ffn_grads(self, makers, after):
        self.ffn_makers = makers
        return None

    def ffn_backward_done(self, after):
        return None


def _local_step(x, posf, target, w, hooks):
    rows, d = x.shape
    width = w["ssm_d"].shape[1]
    qr, kvr = w["mla_q_norm_w"].shape[1], w["mla_kv_norm_w"].shape[1]
    heads = w["mla_w_ukv"].shape[1] // HEAD_SLOT
    f2 = w["ffn_conv_b"].shape[1]
    inp = w["w_in"].shape[0]
    scale = (QK_NOPE_DIM + QK_ROPE_DIM) ** -0.5
    g = {}

    hn = _rmsnorm_fwd(x, w["attn_norm_w"], name="attn_norm")
    proj = _matmul(hn, w["w_in"], mode="nt", name="in_proj")

    s5_weights = (w["ssm_lambda_re"], w["ssm_lambda_im"], w["ssm_log_dt"], w["ssm_b_re"], w["ssm_b_im"])
    wb, wct, abar = _s5_bands(*s5_weights, w["ssm_c_re"], w["ssm_c_im"])
    states, y_pre, yg = _s5_fwd(proj, wb, wct, w["ssm_d"], abar)
    later = hooks.mixer_weights(yg)
    z = _matmul(yg, later["ssm_w_glu"], mode="nn", name="glu_proj", bias=w["ssm_b_glu"])
    ys = _glu_norm_fwd(y_pre, z, w["ssm_out_norm_w"])

    q_col, kv_col, kpe_col = width // qr, (width + qr) // kvr, (width + qr + kvr) // LANES
    assert width % qr == 0 and (width + qr) % kvr == 0
    qn = _rmsnorm_fwd(proj, w["mla_q_norm_w"], name="q_norm", width=qr, col=q_col)
    kvn = _rmsnorm_fwd(proj, w["mla_kv_norm_w"], name="kv_norm", width=kvr, col=kv_col)
    q = _matmul(qn, w["mla_w_uq"], mode="nn", name="q_proj")
    kv = _matmul(kvn, w["mla_w_ukv"], mode="nn", name="kv_proj")
    half = QK_ROPE_DIM // 2
    inv_freq = ROPE_THETA ** (-jnp.arange(0, QK_ROPE_DIM, 2, dtype=F32) / QK_ROPE_DIM)
    zeros = jnp.zeros((LANES - QK_ROPE_DIM,), F32)
    freq = jnp.concatenate([inv_freq, inv_freq, zeros]).reshape(1, LANES)
    sign = jnp.concatenate([-jnp.ones((half,), F32), jnp.ones((half,), F32), zeros]).reshape(1, LANES)
    cos, sin = _rope_tables(posf, freq, sign)
    qc, kc, vb = _attn_prep(q, kv, proj, kpe_col, cos, sin)
    o, lse = _attn_fwd(qc, kc, vb, scale=scale, tq=ATTN_BLOCK)
    ym = _rmsnorm_fwd(o, w["mla_out_norm_w"], name="mla_out_norm")
    ycat = jnp.concatenate([ys, ym], axis=1)
    h1 = _matmul(ycat, later["w_out"], mode="nn", name="out_proj", add=x, after=hooks.ffn_weights_arrived(ycat))

    hn2 = _rmsnorm_fwd(h1, w["ffn_norm_w"], name="ffn_norm")
    ffn = hooks.ffn_weights(hn2)
    a = _matmul(hn2, ffn["ffn_w_up"], mode="nn", name="ffn_up", tm=FFN_ROWS)
    started = hooks.ffn_down_arrived(a)
    conv_b = w["ffn_conv_b"] if started is None else w["ffn_conv_b"] + started[:1, :1]
    gated = _conv_gate_fwd(a, ffn["ffn_conv_w"], conv_b)
    w_down = hooks.ffn_down_weight(gated)
    h2 = _matmul(gated, w_down, mode="nn", name="ffn_down", add=h1, tk=2816, tm=FFN_ROWS)
    loss_tile, dh2, dh2_mxu, g["final_norm_w"] = _final_norm_loss(h2, w["final_norm_w"], target)

    dgated = _matmul(dh2_mxu, w_down, mode="nt", name="ffn_down_dx", tm=FFN_ROWS)
    da, dcw, dcb = _conv_gate_bwd(a, ffn["ffn_conv_w"], w["ffn_conv_b"], dgated)
    g["ffn_conv_w"] = jnp.concatenate([dcw[0, :3], dcw[1, :3]], axis=1)
    g["ffn_conv_b"] = jnp.concatenate([dcb[0], dcb[1]], axis=1)
    started = hooks.ffn_grads({
        "ffn_w_up": _wgrad(hn2, da, d, f2, False, "ffn_up_dw", b_split=True, tn=_tile(f2 // N_CHIPS, 1408)),
        "ffn_w_down": _wgrad(gated, dh2_mxu, f2 // 2, d, True, "ffn_down_dw", tm=f2 // 2 // N_CHIPS, tn=512)}, dcb)
    dhn2 = _matmul(da, ffn["ffn_w_up"], mode="nt", name="ffn_up_dx", a_split=True, tk=_tile(f2 // 2, 1408), tm=FFN_ROWS,
                   tn=2048, after=started)
    dh1, dh1_mxu, g["ffn_norm_w"] = _rmsnorm_bwd(h1, w["ffn_norm_w"], dhn2, name="ffn_norm_bwd", add=dh2,
                                                dx_dtypes=(F32, BF16))

    dycat = _matmul(dh1_mxu, later["w_out"], mode="nt", name="out_proj_dx")
    g["w_out"] = _wgrad(ycat, dh1_mxu, 2 * width, d, True, "out_proj_dw")
    started = hooks.ffn_backward_done(dycat)
    mla_out_norm_w, ssm_out_norm_w = w["mla_out_norm_w"], w["ssm_out_norm_w"]
    if started is not None:
        mla_out_norm_w, ssm_out_norm_w = mla_out_norm_w + started[:1, :1], ssm_out_norm_w + started[:1, :1]

    do, g["mla_out_norm_w"] = _rmsnorm_bwd(o, mla_out_norm_w, dycat, name="mla_out_norm_bwd", width=width, dy_col=1)
    dq, dkv, dkpe_heads = _attn_bwd(qc, kc, vb, o, do, lse, cos, sin, scale=scale, tk=ATTN_BLOCK)
    dkpe = _kpe_bwd(dkpe_heads, cos, sin)
    g["mla_w_uq"] = _wgrad(qn, dq, qr, heads * HEAD_SLOT, False, "q_proj_dw")
    dqn = _matmul(dq, w["mla_w_uq"], mode="nt", name="q_proj_dx")
    dcq, g["mla_q_norm_w"] = _rmsnorm_bwd(proj, w["mla_q_norm_w"], dqn, name="q_norm_bwd", width=qr, col=q_col,
                                          dx_dtypes=(BF16,))
    g["mla_w_ukv"] = _wgrad(kvn, dkv, kvr, heads * HEAD_SLOT, False, "kv_proj_dw")
    dkvn = _matmul(dkv, w["mla_w_ukv"], mode="nt", name="kv_proj_dx")
    dckv, g["mla_kv_norm_w"] = _rmsnorm_bwd(proj, w["mla_kv_norm_w"], dkvn, name="kv_norm_bwd", width=kvr, col=kv_col,
                                            dx_dtypes=(BF16,))

    dz, dyg_a, g["ssm_out_norm_w"], g["ssm_b_glu"] = _glu_norm_bwd(y_pre, z, ssm_out_norm_w, dycat)
    dyg_b = _matmul(dz, later["ssm_w_glu"], mode="nt", name="glu_proj_dx")
    g["ssm_w_glu"] = _wgrad(yg, dz, width, width, True, "glu_proj_dw")
    du, dwb, dwct, dabar, g["ssm_d"] = _s5_bwd(proj, states, y_pre, dyg_a, dyg_b, wb, wct, w["ssm_d"], abar)
    (g["ssm_lambda_re"], g["ssm_lambda_im"], g["ssm_log_dt"], g["ssm_b_re"], g["ssm_b_im"], g["ssm_c_re"],
     g["ssm_c_im"]) = _s5_bands_bwd(*s5_weights, dwb, dwct, dabar)

    pad = jnp.zeros((rows, inp - (width + qr + kvr + LANES)), BF16)
    dproj = jnp.concatenate([du, dcq, dckv, dkpe, pad], axis=1)
    g["w_in"] = _wgrad(dproj, hn, inp, d, False, "in_proj_dw")
    dhn = _matmul(dproj, w["w_in"], mode="nn", name="in_proj_dx")
    dx, g["attn_norm_w"] = _rmsnorm_bwd(x, w["attn_norm_w"], dhn, name="attn_norm_bwd", add=dh1)
    return loss_tile, dx, g


ANY = pl.BlockSpec(memory_space=pl.ANY)
MESH = pl.DeviceIdType.MESH


def _mesh_pos():
    return lax.axis_index("x"), lax.axis_index("y"), lax.axis_index("c")


def _other_chips(x, y):
    return [(1 - x, y), (x, 1 - y), (1 - x, 1 - y)]


def _remote(src, dst, send_sems, recv_sems, k, to):
    return pltpu.make_async_remote_copy(src_ref=src, dst_ref=dst, send_sem=send_sems.at[k], recv_sem=recv_sems.at[k],
                                        device_id=to, device_id_type=MESH)


def _place_shard(shard, piece_idx, row_sharded, name, out_dtype=BF16, pieces=N_CHIPS, after=None):
    rs, cs = shard.shape
    tr = _tile(rs, 256, 2 * SUBLANES)
    rb = rs // tr
    extra = [] if after is None else [after]

    def body(p_ref, x_ref, *rest):
        o_ref = rest[-1]
        o_ref[...] = x_ref[...].astype(o_ref.dtype)

    if row_sharded:
        out_shape, out_map = (pieces * rs, cs), (lambda i, p_ref: (p_ref[0] * rb + i, 0))
    else:
        out_shape, out_map = (rs, pieces * cs), (lambda i, p_ref: (i, p_ref[0]))
    return pl.pallas_call(
        body, name=name, out_shape=jax.ShapeDtypeStruct(out_shape, out_dtype),
        grid_spec=pltpu.PrefetchScalarGridSpec(
            num_scalar_prefetch=1, grid=(rb,),
            in_specs=[pl.BlockSpec((tr, cs), lambda i, p_ref: (i, 0))] + [pl.BlockSpec(memory_space=pl.ANY)] * len(extra),
            out_specs=pl.BlockSpec((tr, cs), out_map)),
        compiler_params=_params(("parallel",)),
    )(piece_idx, shard, *extra)


def _gather_weights(placed, name):
    n = len(placed)
    meta = [(row_sharded, direct) for _, row_sharded, direct in placed]
    over_ici, over_d2d = _gather_plans(meta)
    forwarded = [t for t, (_, direct) in enumerate(meta) if not direct]

    def body(*refs):
        outs = refs[n:2 * n]
        send_sems, recv_sems, pass_send_sems, pass_recv_sems = refs[2 * n:]
        first, arrivals = over_ici(outs, send_sems, recv_sems)
        passed, passed_arrivals = over_d2d([outs[t] for t in forwarded], pass_send_sems, pass_recv_sems)
        for cp in first:
            cp.start()
        for t in range(n):
            for j in range(3):
                arrivals[3 * t + j].wait_recv()
                if t in forwarded:
                    passed[3 * forwarded.index(t) + j].start()
        for cp in passed_arrivals:
            cp.wait_recv()
        for cp in first + passed:
            cp.wait_send()

    return pl.pallas_call(
        body, name=name, in_specs=[ANY] * n, out_specs=[ANY] * n,
        out_shape=[jax.ShapeDtypeStruct(arr.shape, arr.dtype) for arr, _, _ in placed],
        input_output_aliases={t: t for t in range(n)},
        scratch_shapes=[pltpu.SemaphoreType.DMA((3 * n,)), pltpu.SemaphoreType.DMA((3 * n,)),
                        pltpu.SemaphoreType.DMA((3 * len(forwarded),)), pltpu.SemaphoreType.DMA((3 * len(forwarded),))],
    )(*[arr for arr, _, _ in placed])


def _gather_plans(meta):
    def window(ref, row_sharded, piece, half):
        r, cc = ref.shape
        if row_sharded:
            rs = r // N_CHIPS
            if half is None:
                return ref.at[pl.ds(piece * rs, rs), :]
            return ref.at[pl.ds(piece * rs + half * (rs // 2), rs // 2), :]
        cs = cc // N_CHIPS
        if half is None:
            return ref.at[:, pl.ds(piece * cs, cs)]
        return ref.at[pl.ds(half * (r // 2), r // 2), pl.ds(piece * cs, cs)]

    def over_ici(refs, send_sems, recv_sems):
        x, y, c = _mesh_pos()
        sends, recvs = [], []
        for t, (row_sharded, direct) in enumerate(meta):
            mine = window(refs[t], row_sharded, 2 * x + y, None if direct else c)
            for j, (px, py) in enumerate(_other_chips(x, y)):
                theirs = window(refs[t], row_sharded, 2 * px + py, None if direct else c)
                sends.append(_remote(mine, mine, send_sems, recv_sems, 3 * t + j, (px, py, c)))
                recvs.append(_remote(theirs, theirs, send_sems, recv_sems, 3 * t + j, (px, py, c)))
        return sends, recvs

    def over_d2d(refs, send_sems, recv_sems):
        x, y, c = _mesh_pos()
        sends, recvs = [], []
        rows = [row_sharded for row_sharded, direct in meta if not direct]
        for t, row_sharded in enumerate(rows):
            for j, (px, py) in enumerate(_other_chips(x, y)):
                got = window(refs[t], row_sharded, 2 * px + py, c)
                other = window(refs[t], row_sharded, 2 * px + py, 1 - c)
                sends.append(_remote(got, got, send_sems, recv_sems, 3 * t + j, (x, y, 1 - c)))
                recvs.append(_remote(other, other, send_sems, recv_sems, 3 * t + j, (x, y, 1 - c)))
        return sends, recvs

    return over_ici, over_d2d


HBM = pl.BlockSpec(memory_space=pltpu.HBM)
SEMAPHORES = pl.BlockSpec(memory_space=pltpu.SEMAPHORE)
DATAFLOW = pltpu.SideEffectType.DATAFLOW_SIDE_EFFECTING


def _start_copies(name, arrays, plan, n_copies, after):
    n = len(arrays)

    def body(*refs):
        sends, _ = plan(refs[:n], refs[n + 1], refs[n + 2])
        for cp in sends:
            cp.start()
        token = refs[2 * n + 3]
        token[...] = jnp.zeros_like(token)

    out = pl.pallas_call(
        body, name=name,
        out_shape=(pltpu.SemaphoreType.DMA((n_copies,)), pltpu.SemaphoreType.DMA((n_copies,)),
                   *[pltpu.HBM(a.shape, a.dtype) for a in arrays], jax.ShapeDtypeStruct((SUBLANES, LANES), F32)),
        in_specs=[HBM] * n + [ANY],
        out_specs=(SEMAPHORES, SEMAPHORES, *[HBM] * n, pl.BlockSpec(memory_space=pltpu.VMEM)),
        input_output_aliases={t: t + 2 for t in range(n)},
        compiler_params=pltpu.CompilerParams(has_side_effects=DATAFLOW),
    )(*[pltpu.with_memory_space_constraint(a, pltpu.HBM) for a in arrays], after)
    return out[0], out[1], list(out[2:2 + n]), out[2 + n]


def _wait_copies(name, started, plan, after):
    send_sems, recv_sems, arrays, _ = started
    n = len(arrays)

    def body(*refs):
        sends, recvs = plan(refs[:n], refs[n], refs[n + 1])
        for cp in sends:
            cp.wait_send()
        for cp in recvs:
            cp.wait_recv()

    out = pl.pallas_call(
        body, name=name, out_shape=[pltpu.HBM(a.shape, a.dtype) for a in arrays],
        in_specs=[HBM] * n + [SEMAPHORES, SEMAPHORES, ANY], out_specs=[HBM] * n,
        input_output_aliases={t: t for t in range(n)},
        compiler_params=pltpu.CompilerParams(has_side_effects=DATAFLOW),
    )(*arrays, send_sems, recv_sems, after)
    return list(out)


def _exchange(name, arrays, plan, n_copies, after=None):
    n = len(arrays)
    extra = [] if after is None else [after]

    def body(*refs):
        outs = refs[n + len(extra):2 * n + len(extra)]
        send_sems, recv_sems = refs[2 * n + len(extra):]
        sends, recvs = plan(outs, send_sems, recv_sems)
        for cp in sends:
            cp.start()
        for cp in recvs:
            cp.wait_recv()
        for cp in sends:
            cp.wait_send()

    return pl.pallas_call(
        body, name=name, in_specs=[ANY] * (n + len(extra)), out_specs=[ANY] * n,
        out_shape=[jax.ShapeDtypeStruct(a.shape, a.dtype) for a in arrays],
        input_output_aliases={t: t for t in range(n)},
        scratch_shapes=[pltpu.SemaphoreType.DMA((n_copies,)), pltpu.SemaphoreType.DMA((n_copies,))],
    )(*arrays, *extra)


def _give_plan(n):
    def plan(refs, send_sems, recv_sems):
        x, y, c = _mesh_pos()
        sends = [_remote(refs[t], refs[n + t], send_sems, recv_sems, t, (x, y, 1 - c)) for t in range(n)]
        return sends, sends

    return plan


def _scatter_plan(n):
    def plan(refs, send_sems, recv_sems):
        x, y, c = _mesh_pos()
        sends = []
        for t in range(n):
            for j, (px, py) in enumerate(_other_chips(x, y)):
                sends.append(_remote(refs[t].at[2 * px + py], refs[n + t].at[j], send_sems, recv_sems, 3 * t + j, (px, py, c)))
        return sends, sends

    return plan


def _scatter_shapes(sums):
    return [jax.ShapeDtypeStruct((3,) + s.shape[1:], s.dtype) for s in sums]


def _join_plan(n):
    def plan(refs, send_sems, recv_sems):
        x, y, c = _mesh_pos()
        sends = [_remote(refs[t].at[c], refs[t].at[c], send_sems, recv_sems, t, (x, y, 1 - c)) for t in range(n)]
        recvs = [_remote(refs[t].at[1 - c], refs[t].at[1 - c], send_sems, recv_sems, t, (x, y, 1 - c)) for t in range(n)]
        return sends, recvs

    return plan


def _join_halves(halves, name, after=None):
    return _exchange(name, halves, _join_plan(len(halves)), len(halves), after=after)


def _add_other_half(g4, got, where, name):
    _, pieces, sr, sc = g4.shape
    tr = _tile(sr, 256, 2 * SUBLANES)

    def body(w_ref, a_ref, b_ref, o_ref):
        o_ref[...] = a_ref[...] + b_ref[...]

    blk = pl.BlockSpec((None, tr, sc), lambda p, i, w_ref: (p, i, 0))
    return pl.pallas_call(
        body, name=name, out_shape=jax.ShapeDtypeStruct((pieces, sr, sc), F32),
        grid_spec=pltpu.PrefetchScalarGridSpec(
            num_scalar_prefetch=1, grid=(pieces, sr // tr),
            in_specs=[pl.BlockSpec((None, None, tr, sc), lambda p, i, w_ref: (w_ref[0], p, i, 0)), blk], out_specs=blk),
        compiler_params=_params(("parallel", "parallel")),
    )(where, g4, got)


def _add_pieces(sums, got_pieces, where, name):
    _, sr, sc = sums.shape
    tr = _tile(sr, 256, 2 * SUBLANES)

    def body(w_ref, a_ref, r_ref, o_ref):
        acc = a_ref[...]
        for j in range(3):
            acc = acc + r_ref[j].astype(F32)
        o_ref[...] = acc

    return pl.pallas_call(
        body, name=name, out_shape=jax.ShapeDtypeStruct((N_CORES, sr, sc), F32),
        grid_spec=pltpu.PrefetchScalarGridSpec(
            num_scalar_prefetch=1, grid=(sr // tr,),
            in_specs=[pl.BlockSpec((None, tr, sc), lambda i, w_ref: (w_ref[1], i, 0)),
                      pl.BlockSpec((3, tr, sc), lambda i, w_ref: (0, i, 0))],
            out_specs=pl.BlockSpec((None, tr, sc), lambda i, w_ref: (w_ref[0], i, 0))),
        compiler_params=_params(("parallel",)),
    )(where, sums, got_pieces)


def _adamw_update(w, g, m, v):
    nm = ADAM_B1 * m + (1.0 - ADAM_B1) * g
    nv = ADAM_B2 * v + (1.0 - ADAM_B2) * (g * g)
    m_hat = nm / (1.0 - ADAM_B1 ** ADAM_STEP)
    v_hat = nv / (1.0 - ADAM_B2 ** ADAM_STEP)
    return -ADAM_LR * (m_hat / (jnp.sqrt(v_hat) + ADAM_EPS) + ADAM_WD * w), nm, nv


def _adamw(w, g, m, v, name, after=None):
    rows, cols = w.shape
    halves = 2 if g.ndim == 3 else 1
    bc = cols // halves
    tr = _tile(rows, max(SUBLANES, (1 << 19) // max(bc, 1) // SUBLANES * SUBLANES), SUBLANES)

    def body(w_ref, g_ref, m_ref, v_ref, *rest):
        d_ref, nm_ref, nv_ref, go_ref = rest[-4:]
        gv = g_ref[...]
        d_ref[...], nm_ref[...], nv_ref[...] = _adamw_update(w_ref[...], gv, m_ref[...], v_ref[...])
        go_ref[...] = gv

    blk = pl.BlockSpec((tr, bc), lambda i, h: (i, h))
    g_blk = pl.BlockSpec((None, tr, bc), lambda i, h: (h, i, 0)) if halves == 2 else blk
    extra = [] if after is None else [after]
    return pl.pallas_call(
        body, name=name, grid=(rows // tr, halves),
        in_specs=[blk, g_blk, blk, blk] + [pl.BlockSpec(memory_space=pl.ANY)] * len(extra), out_specs=[blk] * 4,
        out_shape=[jax.ShapeDtypeStruct((rows, cols), F32)] * 4, compiler_params=_params(("parallel", "parallel")),
    )(w, g, m, v, *extra)


def _adamw_many(ws, gs, ms, vs, name):
    n = len(ws)

    def body(*refs):
        outs = refs[4 * n:]
        for k in range(n):
            w_ref, g_ref, m_ref, v_ref = (refs[j * n + k] for j in range(4))
            outs[k][...], outs[n + k][...], outs[2 * n + k][...] = _adamw_update(w_ref[...], g_ref[...], m_ref[...], v_ref[...])

    out = pl.pallas_call(
        body, name=name, out_shape=[jax.ShapeDtypeStruct(w.shape, F32) for w in ws] * 3,
        compiler_params=pltpu.CompilerParams(vmem_limit_bytes=VMEM_LIMIT_BYTES),
    )(*ws, *gs, *ms, *vs)
    return out[:n], out[n:2 * n], out[2 * n:]


WEIGHTS = ['attn_norm_w', 'w_in', 'ssm_lambda_re', 'ssm_lambda_im', 'ssm_log_dt', 'ssm_b_re', 'ssm_b_im', 'ssm_c_re',
           'ssm_c_im', 'ssm_d', 'ssm_w_glu', 'ssm_b_glu', 'mla_q_norm_w', 'mla_w_uq', 'mla_kv_norm_w', 'mla_w_ukv',
           'ssm_out_norm_w', 'mla_out_norm_w', 'w_out', 'ffn_norm_w', 'ffn_w_up', 'ffn_conv_w', 'ffn_conv_b',
           'ffn_w_down', 'final_norm_w']
SHARDED = {'w_in': False, 'ssm_w_glu': True, 'mla_w_uq': False, 'mla_w_ukv': False, 'w_out': True, 'ffn_w_up': False,
           'ffn_w_down': True}
SMALL = [n for n in WEIGHTS if n not in SHARDED and n != 'ffn_conv_w']
ROPE_PAD = HEAD_SLOT - QK_NOPE_DIM - QK_ROPE_DIM
SMALL_COLS = 8 * LANES


def _pad_heads(w_uq, heads):
    qr = w_uq.shape[0]
    w3 = w_uq.reshape(qr, heads, QK_NOPE_DIM + QK_ROPE_DIM)
    return jnp.concatenate([w3, jnp.zeros((qr, heads, ROPE_PAD), w_uq.dtype)], axis=2).reshape(qr, heads * HEAD_SLOT)


def _unpad_heads(g_uq, heads):
    qr = g_uq.shape[0]
    return g_uq.reshape(qr, heads, HEAD_SLOT)[:, :, :QK_NOPE_DIM + QK_ROPE_DIM].reshape(qr, -1)


FFN = ['ffn_w_up', 'ffn_w_down']
MIXER_LATER = ['ssm_w_glu', 'w_out']
FFN_GATHER = FFN + ['ffn_conv_w']


class _Overlapped:
    def __init__(self, placed_first, first_sharding, where):
        self.where, self.mine, self.other = where, where[:1], 1 - where[:1]
        self.first_ici, self.first_d2d = _gather_plans([(r, False) for r in first_sharding])
        self.first = _start_copies("gather_first_start", placed_first, self.first_ici, 3 * len(placed_first), where)
        self.first_started = self.first[3]

    def start_rest(self, placed_later, placed):
        self.later_ici, self.later_d2d = _gather_plans([(SHARDED[n], False) for n in MIXER_LATER])
        self.later = _start_copies("gather_later_start", placed_later, self.later_ici, 3 * len(placed_later),
                                   self.first_started)
        up, down, taps = placed
        self.up_ici, self.up_d2d = _gather_plans([(SHARDED["ffn_w_up"], False), (False, True)])
        self.up = _start_copies("gather_ffn_up_start", [up, taps], self.up_ici, 6, self.later[3])
        self.down_ici, self.down_d2d = _gather_plans([(SHARDED["ffn_w_down"], False)])
        self.down = _start_copies("gather_ffn_down_start", [down], self.down_ici, 3, self.up[3])
        self.gather_started = self.down[3]
        arrived = _wait_copies("gather_first_wait", self.first, self.first_ici, self.gather_started)
        return _exchange("gather_first_pass", arrived, self.first_d2d, 3 * len(arrived))

    def mixer_weights(self, after):
        arrived = _wait_copies("gather_later_wait", self.later, self.later_ici, after)
        return dict(zip(MIXER_LATER, _exchange("gather_later_pass", arrived, self.later_d2d, 3 * len(arrived))))

    def ffn_weights_arrived(self, after):
        up, self.taps = _wait_copies("gather_ffn_up_wait", self.up, self.up_ici, after)
        self.up_passing = _start_copies("gather_ffn_up_pass_start", [up], self.up_d2d, 3, after)
        return self.up_passing[3]

    def ffn_weights(self, after):
        w_up, = _wait_copies("gather_ffn_up_pass_wait", self.up_passing, self.up_d2d, after)
        return {"ffn_w_up": w_up, "ffn_conv_w": self.taps}

    def ffn_down_arrived(self, after):
        down, = _wait_copies("gather_ffn_down_wait", self.down, self.down_ici, after)
        self.down_passing = _start_copies("gather_ffn_down_pass_start", [down], self.down_d2d, 3, after)
        return self.down_passing[3]

    def ffn_down_weight(self, after):
        return _wait_copies("gather_ffn_down_pass_wait", self.down_passing, self.down_d2d, after)[0]

    def ffn_grads(self, makers, after):
        self.makers = [makers[name] for name in FFN]
        n = len(FFN)
        give = [make(self.other, suffix="_give") for make in self.makers]
        lands = [lax.empty(g.shape, g.dtype) for g in give]
        self.swap = _start_copies("grad_ffn_swap_start", give + lands, _give_plan(n), n, after)
        return self.swap[3]

    def ffn_backward_done(self, after):
        n = len(FFN)
        got = _wait_copies("grad_ffn_swap_wait", self.swap, _give_plan(n), after)[n:]
        kept = [make(self.mine, suffix="_keep", add=got[t], wire=True) for t, make in enumerate(self.makers)]
        self.sums = [k[0] for k in kept]
        wires = [k[1] for k in kept]
        lands = [lax.empty(s.shape, s.dtype) for s in _scatter_shapes(wires)]
        self.scatter = _start_copies("grad_ffn_scatter_start", wires + lands, _scatter_plan(n), 3 * n, after)
        return self.scatter[3]

    def ffn_reduced(self, after):
        n = len(FFN)
        got_pieces = _wait_copies("grad_ffn_scatter_wait", self.scatter, _scatter_plan(n), after)[n:]
        return [_add_pieces(self.sums[t], got_pieces[t], self.where, "grad_add_pieces_" + name) for t, name in enumerate(FFN)]


def _step(args):
    x, positions, target = args["x"][0], args["positions"], args["loss_target"][0]
    rows = x.shape[0]
    p = {n: args[n] for n in WEIGHTS}
    xi, yi, ci = _mesh_pos()
    piece = 2 * xi + yi

    def transposed(a):
        return jnp.swapaxes(a[0], 0, 1)

    def as_stored(n, a):
        return jnp.swapaxes(a, 2, 3) if n in ("ssm_b_re", "ssm_b_im") else a

    w_in = transposed(p["w_in"])
    in_width = w_in.shape[0]
    in_pad = (-in_width) % (2 * LANES)
    heads_here = p["mla_w_uq"].shape[2] // (QK_NOPE_DIM + QK_ROPE_DIM)
    shards = {
        "w_in": jnp.pad(w_in, ((0, in_pad), (0, 0))),
        "ssm_w_glu": p["ssm_w_glu"][0],
        "mla_w_uq": _pad_heads(p["mla_w_uq"][0], heads_here),
        "mla_w_ukv": p["mla_w_ukv"][0],
        "w_out": p["w_out"][0],
        "ffn_w_up": p["ffn_w_up"][0],
        "ffn_w_down": p["ffn_w_down"][0],
    }
    conv_w = jnp.pad(p["ffn_conv_w"][0], ((0, SUBLANES - p["ffn_conv_w"].shape[1]), (0, 0)))
    order = list(SHARDED)
    piece_idx = piece.reshape(1).astype(jnp.int32)
    mixer = [n for n in order if n not in FFN]
    first = [n for n in mixer if n not in MIXER_LATER]
    where = jnp.stack([ci, piece]).astype(jnp.int32)
    placed = {n: _place_shard(shards[n], piece_idx, SHARDED[n], "place_" + n) for n in first}
    hooks = _Overlapped([placed[n] for n in first], [SHARDED[n] for n in first], where)
    for n in order:
        if n not in first:
            placed[n] = _place_shard(shards[n], piece_idx, SHARDED[n], "place_" + n, after=hooks.first_started)
    placed["ffn_conv_w"] = _place_shard(conv_w, piece_idx, False, "place_ffn_conv_w", out_dtype=F32,
                                        after=hooks.first_started)
    w = dict(zip(first, hooks.start_rest([placed[n] for n in MIXER_LATER], [placed[n] for n in FFN_GATHER])))
    groups = p["ssm_lambda_re"].shape[1]
    w.update({
        "attn_norm_w": p["attn_norm_w"] + hooks.gather_started[:1, :1],
        "ssm_lambda_re": p["ssm_lambda_re"][0], "ssm_lambda_im": p["ssm_lambda_im"][0],
        "ssm_log_dt": p["ssm_log_dt"].reshape(groups, 1), "ssm_b_re": as_stored("ssm_b_re", p["ssm_b_re"])[0],
        "ssm_b_im": as_stored("ssm_b_im", p["ssm_b_im"])[0], "ssm_c_re": p["ssm_c_re"][0], "ssm_c_im": p["ssm_c_im"][0],
        "ssm_d": p["ssm_d"], "ssm_b_glu": p["ssm_b_glu"], "mla_q_norm_w": p["mla_q_norm_w"],
        "mla_kv_norm_w": p["mla_kv_norm_w"], "ssm_out_norm_w": p["ssm_out_norm_w"], "mla_out_norm_w": p["mla_out_norm_w"],
        "ffn_norm_w": p["ffn_norm_w"], "ffn_conv_b": p["ffn_conv_b"], "final_norm_w": p["final_norm_w"].reshape(1, -1),
    })

    loss_tile, dx, g = _local_step(x, positions.reshape(rows, 1).astype(F32), target, w, hooks)

    flat = [g[n].reshape(-1) for n in SMALL] + [g["ffn_conv_w"].reshape(-1), loss_tile[0, :1]]
    sizes = [f.shape[0] for f in flat]
    per_block = -(-sum(sizes) // (N_CORES * N_CHIPS * SMALL_COLS))
    small_rows = -(-per_block // (2 * SUBLANES)) * (2 * SUBLANES)
    padded = N_CORES * N_CHIPS * small_rows * SMALL_COLS

    def pack(parts):
        parts = list(parts)
        have = sum(q.shape[0] for q in parts)
        return jnp.concatenate(parts + [jnp.zeros((padded - have,), F32)])

    reduced = mixer + ["small"]
    small = pack(flat).reshape(N_CORES, N_CHIPS, small_rows, SMALL_COLS)
    give = [g[n](hooks.other, suffix="_give") for n in mixer] + [lax.dynamic_index_in_dim(small, 1 - ci, 0, keepdims=False)]
    lands = [lax.empty(a.shape, a.dtype) for a in give]
    give_plan = _give_plan(len(reduced))
    swap = _start_copies("grad_mixer_swap_start", give + lands, give_plan, len(reduced), dx)

    grads, delta, new_m, new_v = {}, {}, {}, {}

    def finish(n, joined, after=None):
        grad = joined if SHARDED[n] else joined.reshape(-1, joined.shape[2])
        if n == "w_in":
            wt, mt, vt = w_in, transposed(args["m_w_in"]), transposed(args["v_w_in"])
            out = _adamw(wt, grad, mt, vt, "adamw_w_in")
            delta[n], new_m[n], new_v[n], grads[n] = (jnp.swapaxes(a, 0, 1)[None] for a in out)
            return
        if n == "mla_w_uq":
            grad = _unpad_heads(grad, heads_here)
        adam(n, grad, after)

    def adam(n, grad, after=None):
        shape = p[n].shape
        out = _adamw(p[n].reshape(shape[1:]), grad, args["m_" + n].reshape(shape[1:]),
                     args["v_" + n].reshape(shape[1:]), "adamw_" + n, after)
        delta[n], new_m[n], new_v[n], grads[n] = (a.reshape(shape) for a in out)

    ffn_halves = hooks.ffn_reduced(swap[3])
    got = _wait_copies("grad_mixer_swap_wait", swap, give_plan, ffn_halves[-1])[len(reduced):]
    join_plan = _join_plan(len(FFN))
    ffn_join = _start_copies("grad_ffn_join_start", ffn_halves, join_plan, len(FFN), got[0])
    kept = [g[n](hooks.mine, suffix="_keep", add=got[t], wire=True) for t, n in enumerate(mixer)]
    small_sum = _add_other_half(small, got[-1], where, "grad_add_half_small")
    sums = [k[0] for k in kept] + [small_sum]
    wires = [k[1] for k in kept] + [small_sum]
    ffn_joined = _wait_copies("grad_ffn_join_wait", ffn_join, join_plan, kept[-1][0])
    lands = [lax.empty(s.shape, s.dtype) for s in _scatter_shapes(wires)]
    scatter_plan = _scatter_plan(len(reduced))
    scatter = _start_copies("grad_mixer_scatter_start", wires + lands, scatter_plan, 3 * len(reduced), ffn_joined[0])
    behind = scatter[3]
    for n, joined in zip(FFN, ffn_joined):
        finish(n, joined, after=behind)
        behind = delta[n]
    got_pieces = _wait_copies("grad_mixer_scatter_wait", scatter, scatter_plan, delta[FFN[-1]])[len(reduced):]
    halves = [_add_pieces(sums[t], got_pieces[t], where, "grad_add_pieces_" + n) for t, n in enumerate(reduced)]
    joined = _join_halves(halves, "grad_join_halves")
    for n, j in zip(mixer, joined):
        finish(n, j)
    eighths = _place_shard(joined[-1].reshape(N_CORES * small_rows, SMALL_COLS), piece_idx, True, "place_small_grads",
                           out_dtype=F32)
    small_sum = _gather_weights([(eighths, True, False)], "gather_small_grads")[0]
    flat_sum = small_sum.reshape(N_CHIPS, N_CORES, small_rows * SMALL_COLS).transpose(1, 0, 2).reshape(-1)
    offs = [0]
    for s in sizes:
        offs.append(offs[-1] + s)
    stored = {n: as_stored(n, p[n]) for n in SMALL}
    for k, n in enumerate(SMALL):
        grads[n] = flat_sum[offs[k]:offs[k + 1]].reshape(stored[n].shape)
    taps, cols_here = p["ffn_conv_w"].shape[1], p["ffn_conv_w"].shape[2]
    conv_full = flat_sum[offs[len(SMALL)]:offs[len(SMALL) + 1]].reshape(taps, N_CHIPS * cols_here)
    adam("ffn_conv_w", lax.dynamic_slice_in_dim(conv_full, piece * cols_here, cols_here, axis=1))
    loss = flat_sum[offs[len(SMALL) + 1]]

    def rank2(a):
        return a.reshape(1, -1) if a.ndim == 1 else a

    d_s, m_s, v_s = _adamw_many([rank2(stored[n]) for n in SMALL], [rank2(grads[n]) for n in SMALL],
                                [rank2(as_stored(n, args["m_" + n])) for n in SMALL],
                                [rank2(as_stored(n, args["v_" + n])) for n in SMALL], "adamw_small")
    for k, n in enumerate(SMALL):
        delta[n], new_m[n], new_v[n], grads[n] = (as_stored(n, a.reshape(stored[n].shape))
                                                  for a in (d_s[k], m_s[k], v_s[k], grads[n]))

    return (loss, dx[None], *[grads[n] for n in WEIGHTS], *[delta[n] for n in WEIGHTS],
            *[new_m[n] for n in WEIGHTS], *[new_v[n] for n in WEIGHTS])


def kernel(x, positions, attn_norm_w, w_in, ssm_lambda_re, ssm_lambda_im, ssm_log_dt, ssm_b_re, ssm_b_im, ssm_c_re, ssm_c_im, ssm_d, ssm_w_glu, ssm_b_glu, mla_q_norm_w, mla_w_uq, mla_kv_norm_w, mla_w_ukv, ssm_out_norm_w, mla_out_norm_w, w_out, ffn_norm_w, ffn_w_up, ffn_conv_w, ffn_conv_b, ffn_w_down, final_norm_w, loss_target, m_attn_norm_w, m_w_in, m_ssm_lambda_re, m_ssm_lambda_im, m_ssm_log_dt, m_ssm_b_re, m_ssm_b_im, m_ssm_c_re, m_ssm_c_im, m_ssm_d, m_ssm_w_glu, m_ssm_b_glu, m_mla_q_norm_w, m_mla_w_uq, m_mla_kv_norm_w, m_mla_w_ukv, m_ssm_out_norm_w, m_mla_out_norm_w, m_w_out, m_ffn_norm_w, m_ffn_w_up, m_ffn_conv_w, m_ffn_conv_b, m_ffn_w_down, m_final_norm_w, v_attn_norm_w, v_w_in, v_ssm_lambda_re, v_ssm_lambda_im, v_ssm_log_dt, v_ssm_b_re, v_ssm_b_im, v_ssm_c_re, v_ssm_c_im, v_ssm_d, v_ssm_w_glu, v_ssm_b_glu, v_mla_q_norm_w, v_mla_w_uq, v_mla_kv_norm_w, v_mla_w_ukv, v_ssm_out_norm_w, v_mla_out_norm_w, v_w_out, v_ffn_norm_w, v_ffn_w_up, v_ffn_conv_w, v_ffn_conv_b, v_ffn_w_down, v_final_norm_w):
    return _step(dict(locals()))
```

```python
import functools
import math

import jax
import jax.numpy as jnp
from jax import lax
from jax.experimental import pallas as pl
from jax.experimental.pallas import tpu as pltpu

F32 = jnp.float32
BF16 = jnp.bfloat16

SSM_GROUP = 16
SSM_STATE = 64
QK_NOPE_DIM = 128
QK_ROPE_DIM = 64
ROPE_THETA = 10000.0
RMS_EPS = 1e-6
ADAM_LR, ADAM_B1, ADAM_B2, ADAM_EPS, ADAM_WD, ADAM_STEP = 0.001, 0.9, 0.999, 1e-08, 0.01, 10

LANES = 128
SUBLANES = 8
VMEM_LIMIT_BYTES = 56 * 1024 * 1024

GROUPS_PER_BATCH = LANES // SSM_GROUP
STATE_PER_BATCH = GROUPS_PER_BATCH * SSM_STATE
HEAD_SLOT = 2 * LANES
NEG_INF = -1e30
ATTN_BLOCK = 512
FFN_ROWS = 1024

N_CHIPS = 4
N_CORES = 2


def _tile(n, pref, align=LANES):
    if n <= pref:
        return n
    t = (pref // align) * align
    while t >= align:
        if n % t == 0:
            return t
        t -= align
    return n


def _params(sem):
    return pltpu.CompilerParams(dimension_semantics=sem, vmem_limit_bytes=VMEM_LIMIT_BYTES)


def _dot(a, b, dims):
    return lax.dot_general(a, b, (dims, ((), ())), preferred_element_type=F32)


def _dot_nn(a, b):
    return _dot(a, b, ((1,), (0,)))


def _dot_nt(a, b):
    return _dot(a, b, ((1,), (1,)))


def _dot_tn(a, b):
    return _dot(a, b, ((0,), (0,)))


def _matmul(a, b, *, mode, name, tm=512, tn=1024, tk=2048, bias=None, add=None, out_dtype=F32,
            a_split=False, b_split=False, after=None):
    if a_split:
        assert mode == "nt"
        a_shape = (a.shape[1], 2 * a.shape[2])
    else:
        a_shape = a.shape
    if b_split:
        assert mode == "tn"
        b_shape = (b.shape[1], 2 * b.shape[2])
    else:
        b_shape = b.shape
    if mode == "nn":
        (m, k), (k2, n) = a_shape, b_shape
    elif mode == "nt":
        (m, k), (n, k2) = a_shape, b_shape
    else:
        (k, m), (k2, n) = a_shape, b_shape
    assert k == k2, (a.shape, b.shape, mode)
    tm, tn, tk = _tile(m, tm, SUBLANES), _tile(n, tn), _tile(k, tk)
    nk = k // tk
    a_spec = {"nn": pl.BlockSpec((tm, tk), lambda i, j, kk: (i, kk)),
              "nt": pl.BlockSpec((tm, tk), lambda i, j, kk: (i, kk)),
              "tn": pl.BlockSpec((tk, tm), lambda i, j, kk: (kk, i))}[mode]
    b_spec = {"nn": pl.BlockSpec((tk, tn), lambda i, j, kk: (kk, j)),
              "nt": pl.BlockSpec((tn, tk), lambda i, j, kk: (j, kk)),
              "tn": pl.BlockSpec((tk, tn), lambda i, j, kk: (kk, j))}[mode]
    if a_split:
        kb = a.shape[2] // tk
        assert a.shape[2] % tk == 0
        a_spec = pl.BlockSpec((None, tm, tk), lambda i, j, kk: (kk // kb, i, kk % kb))
    if b_split:
        nb = b.shape[2] // tn
        assert b.shape[2] % tn == 0
        b_spec = pl.BlockSpec((None, tk, tn), lambda i, j, kk: (j // nb, kk, j % nb))
    dot = {"nn": _dot_nn, "nt": _dot_nt, "tn": _dot_tn}[mode]
    in_specs, operands = [a_spec, b_spec], [a, b]
    if bias is not None:
        in_specs.append(pl.BlockSpec((1, tn), lambda i, j, kk: (0, j)))
        operands.append(bias)
    if add is not None:
        in_specs.append(pl.BlockSpec((tm, tn), lambda i, j, kk: (i, j)))
        operands.append(add)
    if after is not None:
        in_specs.append(pl.BlockSpec(memory_space=pl.ANY))
        operands.append(after)

    def body(*refs):
        a_ref, b_ref = refs[0], refs[1]
        rest = list(refs[2:])
        bias_ref = rest.pop(0) if bias is not None else None
        add_ref = rest.pop(0) if add is not None else None
        if after is not None:
            rest.pop(0)
        o_ref, acc_ref = rest

        def finish(acc):
            if bias_ref is not None:
                acc = acc + bias_ref[...]
            if add_ref is not None:
                acc = acc + add_ref[...]
            o_ref[...] = acc.astype(o_ref.dtype)

        part = dot(a_ref[...].astype(BF16), b_ref[...].astype(BF16))
        if nk == 1:
            finish(part)
        else:
            kk = pl.program_id(2)

            @pl.when(kk == 0)
            def _():
                acc_ref[...] = part

            @pl.when(jnp.logical_and(kk > 0, kk < nk - 1))
            def _():
                acc_ref[...] += part

            @pl.when(kk == nk - 1)
            def _():
                finish(acc_ref[...] + part)

    out_shape = jax.ShapeDtypeStruct((m, n), out_dtype)
    out_spec = pl.BlockSpec((tm, tn), lambda i, j, kk: (i, j))
    acc_shape = (tm, tn) if nk > 1 else (SUBLANES, LANES)
    return pl.pallas_call(
        body, name=name, grid=(m // tm, n // tn, nk), in_specs=in_specs, out_specs=out_spec, out_shape=out_shape,
        scratch_shapes=[pltpu.VMEM(acc_shape, F32)],
        compiler_params=_params(("parallel", "parallel", "arbitrary")),
    )(*operands)


def _wgrad_half(a, b, rows, cols, row_sharded, name, which, *, suffix="", add=None, wire=False, tm=None, tn=None,
                b_split=False):
    tokens = a.shape[0]
    if row_sharded:
        sr, sc = rows // N_CHIPS, cols // N_CORES
    else:
        sr, sc = rows // N_CORES, cols // N_CHIPS
    tm = _tile(sr, 512) if tm is None else tm
    tn = _tile(sc, 1024) if tn is None else tn
    assert sr % tm == 0 and sc % tn == 0, (rows, cols, tm, tn)
    rb, cb = sr // tm, sc // tn
    if tn >= tm:
        ij, grid = (lambda s, t: (t, s)), (N_CHIPS, cb, rb)
    else:
        ij, grid = (lambda s, t: (s, t)), (N_CHIPS, rb, cb)
    if row_sharded:
        a_tile = lambda p, i, j, h: p * rb + i
        b_tile = lambda p, i, j, h: h[0] * cb + j
    else:
        a_tile = lambda p, i, j, h: h[0] * rb + i
        b_tile = lambda p, i, j, h: p * cb + j
    a_spec = pl.BlockSpec((tokens, tm), lambda p, s, t, h: (0, a_tile(p, *ij(s, t), h)))
    if b_split:
        nbh = b.shape[2] // tn
        assert b.shape[2] % tn == 0
        b_spec = pl.BlockSpec((None, tokens, tn), lambda p, s, t, h: (b_tile(p, *ij(s, t), h) // nbh, 0,
                                                                       b_tile(p, *ij(s, t), h) % nbh))
    else:
        b_spec = pl.BlockSpec((tokens, tn), lambda p, s, t, h: (0, b_tile(p, *ij(s, t), h)))
    out_spec = pl.BlockSpec((None, tm, tn), lambda p, s, t, h: (p, *ij(s, t)))
    in_specs, operands = [a_spec, b_spec], [a, b]
    if add is not None:
        in_specs.append(out_spec)
        operands.append(add)

    def body(h_ref, a_ref, b_ref, *rest):
        acc = _dot_tn(a_ref[...].astype(BF16), b_ref[...].astype(BF16))
        if add is not None:
            acc = acc + rest[0][...]
        for o_ref in rest[1 if add is not None else 0:]:
            o_ref[...] = acc.astype(o_ref.dtype)

    out_dtypes = [F32, BF16] if wire else [F32]
    out = pl.pallas_call(
        body, name=name + suffix, out_shape=[jax.ShapeDtypeStruct((N_CHIPS, sr, sc), dt) for dt in out_dtypes],
        grid_spec=pltpu.PrefetchScalarGridSpec(num_scalar_prefetch=1, grid=grid, in_specs=in_specs,
                                               out_specs=[out_spec] * len(out_dtypes)),
        compiler_params=_params(("parallel", "parallel", "parallel")),
    )(which, *operands)
    return tuple(out) if wire else out[0]


def _rms_rows(x):
    return lax.rsqrt(jnp.mean(x * x, axis=-1, keepdims=True) + RMS_EPS)


def _rmsnorm_fwd(x, w, *, name, width=None, col=0, out_dtype=BF16, tr=256):
    rows = x.shape[0]
    width = x.shape[1] if width is None else width
    tr = _tile(rows, tr, SUBLANES)

    def body(x_ref, w_ref, o_ref):
        xv = x_ref[...]
        o_ref[...] = (xv * _rms_rows(xv) * w_ref[...]).astype(o_ref.dtype)

    return pl.pallas_call(
        body, name=name, grid=(rows // tr,),
        in_specs=[pl.BlockSpec((tr, width), lambda i: (i, col)), pl.BlockSpec((1, width), lambda i: (0, 0))],
        out_specs=pl.BlockSpec((tr, width), lambda i: (i, 0)),
        out_shape=jax.ShapeDtypeStruct((rows, width), out_dtype),
        compiler_params=_params(("parallel",)),
    )(x, w)


def _rmsnorm_bwd_rows(xv, w, dy):
    r = _rms_rows(xv)
    n = xv * r
    dn = dy * w
    dx = r * (dn - n * jnp.mean(dn * n, axis=-1, keepdims=True))
    return dx, dy * n


def _rmsnorm_bwd(x, w, dy, *, name, width=None, col=0, dy_col=0, add=None, tr=256, dx_dtypes=(F32,)):
    rows = x.shape[0]
    n_dx = len(dx_dtypes)
    width = x.shape[1] if width is None else width
    tr = _tile(rows, tr, SUBLANES)
    in_specs = [pl.BlockSpec((tr, width), lambda i: (i, col)), pl.BlockSpec((1, width), lambda i: (0, 0)),
                pl.BlockSpec((tr, width), lambda i: (i, dy_col))]
    operands = [x, w, dy]
    if add is not None:
        in_specs.append(pl.BlockSpec((tr, width), lambda i: (i, 0)))
        operands.append(add)

    def body(*refs):
        x_ref, w_ref, dy_ref = refs[:3]
        add_ref = refs[3] if add is not None else None
        dx_refs, dw_ref = refs[-1 - n_dx:-1], refs[-1]
        dx, dwp = _rmsnorm_bwd_rows(x_ref[...], w_ref[...], dy_ref[...])
        if add_ref is not None:
            dx = dx + add_ref[...]
        for dx_ref in dx_refs:
            dx_ref[...] = dx.astype(dx_ref.dtype)
        part = jnp.sum(dwp, axis=0, keepdims=True)

        @pl.when(pl.program_id(0) == 0)
        def _():
            dw_ref[...] = part

        @pl.when(pl.program_id(0) > 0)
        def _():
            dw_ref[...] += part

    return pl.pallas_call(
        body, name=name, grid=(rows // tr,), in_specs=in_specs,
        out_specs=[pl.BlockSpec((tr, width), lambda i: (i, 0))] * n_dx + [pl.BlockSpec((1, width), lambda i: (0, 0))],
        out_shape=[jax.ShapeDtypeStruct((rows, width), dt) for dt in dx_dtypes] + [jax.ShapeDtypeStruct((1, width), F32)],
        compiler_params=_params(("arbitrary",)),
    )(*operands)


def _final_norm_loss(h, w, target, *, tr=256):
    rows, d = h.shape
    tr = _tile(rows, tr, SUBLANES)

    def body(h_ref, w_ref, t_ref, loss_ref, dh_ref, dhb_ref, dw_ref):
        hv, wv = h_ref[...], w_ref[...]
        r = _rms_rows(hv)
        n = hv * r
        err = n * wv - t_ref[...]
        d_out = err * (1.0 / d)
        dn = d_out * wv
        dh = r * (dn - n * jnp.mean(dn * n, axis=-1, keepdims=True))
        dh_ref[...] = dh
        dhb_ref[...] = dh.astype(BF16)
        dw_part = jnp.sum(d_out * n, axis=0, keepdims=True)
        loss_part = jnp.full((SUBLANES, LANES), 0.5 / d, F32) * jnp.sum(err * err)

        @pl.when(pl.program_id(0) == 0)
        def _():
            dw_ref[...] = dw_part
            loss_ref[...] = loss_part

        @pl.when(pl.program_id(0) > 0)
        def _():
            dw_ref[...] += dw_part
            loss_ref[...] += loss_part

    return pl.pallas_call(
        body, name="final_norm_loss", grid=(rows // tr,),
        in_specs=[pl.BlockSpec((tr, d), lambda i: (i, 0)), pl.BlockSpec((1, d), lambda i: (0, 0)),
                  pl.BlockSpec((tr, d), lambda i: (i, 0))],
        out_specs=[pl.BlockSpec((SUBLANES, LANES), lambda i: (0, 0)), pl.BlockSpec((tr, d), lambda i: (i, 0)),
                   pl.BlockSpec((tr, d), lambda i: (i, 0)), pl.BlockSpec((1, d), lambda i: (0, 0))],
        out_shape=[jax.ShapeDtypeStruct((SUBLANES, LANES), F32), jax.ShapeDtypeStruct((rows, d), F32),
                   jax.ShapeDtypeStruct((rows, d), BF16), jax.ShapeDtypeStruct((1, d), F32)],
        compiler_params=_params(("arbitrary",)),
    )(h, w, target)


def _cmul(ar, ai, br, bi):
    return ar * br - ai * bi, ar * bi + ai * br


def _dot_exact(a, b, dims):
    return lax.dot_general(a, b, (dims, ((), ())), preferred_element_type=F32, precision=lax.Precision.HIGHEST)


def _s5_discretize(lr, li, dt):
    mag = jnp.exp(lr * dt)
    th = li * dt
    ar, ai = mag * jnp.cos(th), mag * jnp.sin(th)
    nr, ni = ar - 1.0, ai
    den = lr * lr + li * li
    zr = (nr * lr + ni * li) / den
    zi = (ni * lr - nr * li) / den
    return mag, ar, ai, nr, ni, den, zr, zi


def _band_slices(group):
    j, gi = divmod(group, GROUPS_PER_BATCH)
    rows = slice(gi * SSM_GROUP, (gi + 1) * SSM_GROUP)
    re = slice(gi * SSM_STATE, (gi + 1) * SSM_STATE)
    im = slice(STATE_PER_BATCH + gi * SSM_STATE, STATE_PER_BATCH + (gi + 1) * SSM_STATE)
    return j, rows, re, im


def _s5_bands(lam_re, lam_im, log_dt, b_re, b_im, c_re, c_im):
    g, _ = lam_re.shape
    nb = g // GROUPS_PER_BATCH
    s2 = 2 * STATE_PER_BATCH

    def body(lr_ref, li_ref, ldt_ref, br_ref, bi_ref, cr_ref, ci_ref, wb_ref, wct_ref, a_ref):
        dt = jnp.exp(ldt_ref[...])
        _, ar, ai, _, _, _, zr, zi = _s5_discretize(lr_ref[...], li_ref[...], dt)
        wb_ref[...] = jnp.zeros_like(wb_ref)
        wct_ref[...] = jnp.zeros_like(wct_ref)
        for group in range(g):
            j, rows, re, im = _band_slices(group)
            zr_g, zi_g = zr[group:group + 1, :], zi[group:group + 1, :]
            bre, bim = br_ref[group], bi_ref[group]
            wb_ref[j, rows, re] = (zr_g * bre - zi_g * bim).astype(BF16)
            wb_ref[j, rows, im] = (zr_g * bim + zi_g * bre).astype(BF16)
            wct_ref[j, rows, re] = cr_ref[group].astype(BF16)
            wct_ref[j, rows, im] = (-ci_ref[group]).astype(BF16)
            a_ref[j, :, re] = ar[group:group + 1, :]
            a_ref[j, :, im] = ai[group:group + 1, :]

    return pl.pallas_call(
        body, name="s5_bands",
        out_shape=[jax.ShapeDtypeStruct((nb, LANES, s2), BF16)] * 2 + [jax.ShapeDtypeStruct((nb, 1, s2), F32)],
    )(lam_re, lam_im, log_dt, b_re, b_im, c_re, c_im)


def _s5_bands_bwd(lam_re, lam_im, log_dt, b_re, b_im, dwb, dwct, dabar):
    g, p = lam_re.shape
    gh = b_re.shape[1:]

    def body(lr_ref, li_ref, ldt_ref, br_ref, bi_ref, dwb_ref, dwct_ref, da_ref,
             dlr_ref, dli_ref, dldt_ref, dbre_ref, dbim_ref, dcre_ref, dcim_ref, dzr_ref, dzi_ref, dar_ref, dai_ref):
        lr, li = lr_ref[...], li_ref[...]
        dt = jnp.exp(ldt_ref[...])
        mag, ar, ai, nr, ni, den, zr, zi = _s5_discretize(lr, li, dt)
        for group in range(g):
            j, rows, re, im = _band_slices(group)
            zr_g, zi_g = zr[group:group + 1, :], zi[group:group + 1, :]
            bre, bim = br_ref[group], bi_ref[group]
            dbr, dbi = dwb_ref[j, rows, re], dwb_ref[j, rows, im]
            dbre_ref[group] = zr_g * dbr + zi_g * dbi
            dbim_ref[group] = zr_g * dbi - zi_g * dbr
            dzr_ref[group:group + 1, :] = jnp.sum(bre * dbr + bim * dbi, axis=0, keepdims=True)
            dzi_ref[group:group + 1, :] = jnp.sum(bre * dbi - bim * dbr, axis=0, keepdims=True)
            dcre_ref[group] = dwct_ref[j, rows, re]
            dcim_ref[group] = -dwct_ref[j, rows, im]
            dar_ref[group:group + 1, :] = da_ref[j, :, re]
            dai_ref[group:group + 1, :] = da_ref[j, :, im]
        dzr, dzi = dzr_ref[...], dzi_ref[...]
        inv = 1.0 / den
        d_nr = (dzr * lr - dzi * li) * inv
        d_ni = (dzr * li + dzi * lr) * inv
        d_den = -(dzr * zr + dzi * zi) * inv
        d_lr = (dzr * nr + dzi * ni) * inv + 2.0 * lr * d_den
        d_li = (dzr * ni - dzi * nr) * inv + 2.0 * li * d_den
        t_ar = dar_ref[...] + d_nr
        t_ai = dai_ref[...] + d_ni
        d_lrdt = t_ar * ar + t_ai * ai
        d_th = t_ai * ar - t_ar * ai
        dlr_ref[...] = d_lr + d_lrdt * dt
        dli_ref[...] = d_li + d_th * dt
        dldt_ref[...] = jnp.sum(d_lrdt * lr + d_th * li, axis=1, keepdims=True) * dt

    return pl.pallas_call(
        body, name="s5_bands_bwd",
        out_shape=[jax.ShapeDtypeStruct((g, p), F32)] * 2 + [jax.ShapeDtypeStruct((g, 1), F32)]
        + [jax.ShapeDtypeStruct((g,) + gh, F32)] * 4,
        scratch_shapes=[pltpu.VMEM((g, p), F32)] * 4,
    )(lam_re, lam_im, log_dt, b_re, b_im, dwb, dwct, dabar)


def _powers(ar, ai, count):
    out = [(ar, ai)]
    for _ in range(count - 1):
        out.append(_cmul(out[-1][0], out[-1][1], ar, ai))
    return out


def _scan_coefs(ar, ai, reverse):
    w = ar.shape[-1]
    pw = _powers(ar, ai, SUBLANES)
    row = lax.broadcasted_iota(jnp.int32, (SUBLANES, w), 0)
    steps = []
    d = 1
    while d < SUBLANES:
        keep = (row < SUBLANES - d) if reverse else (row >= d)
        pr, pi = pw[d - 1]
        steps.append((d, jnp.where(keep, pr, 0.0), jnp.where(keep, pi, 0.0)))
        d *= 2
    cr = jnp.zeros((SUBLANES, w), F32)
    ci = jnp.zeros((SUBLANES, w), F32)
    for t in range(SUBLANES):
        pr, pi = pw[SUBLANES - 1 - t] if reverse else pw[t]
        cr = jnp.where(row == t, pr, cr)
        ci = jnp.where(row == t, pi, ci)
    return steps, cr, ci


def _scan_tile(xr, xi, carry_r, carry_i, coefs, reverse):
    steps, cr, ci = coefs
    for d, mr, mi in steps:
        shift = SUBLANES - d if reverse else d
        sr, si = pltpu.roll(xr, shift, 0), pltpu.roll(xi, shift, 0)
        pr, pi = _cmul(mr, mi, sr, si)
        xr, xi = xr + pr, xi + pi
    pr, pi = _cmul(cr, ci, carry_r, carry_i)
    return xr + pr, xi + pi


def _gelu(x):
    c = math.sqrt(2.0 / math.pi)
    return 0.5 * x * (1.0 + jnp.tanh(c * (x + 0.044715 * x * x * x)))


def _gelu_grad(x):
    c = math.sqrt(2.0 / math.pi)
    t = jnp.tanh(c * (x + 0.044715 * x * x * x))
    return 0.5 * (1.0 + t) + 0.5 * x * (1.0 - t * t) * c * (1.0 + 3.0 * 0.044715 * x * x)


def _s5_fwd(proj, wb, wct, d_skip, abar):
    rows = proj.shape[0]
    nb = wb.shape[0]
    s2 = 2 * STATE_PER_BATCH
    st = STATE_PER_BATCH
    chunk = _tile(rows, 512, SUBLANES)

    def body(u_ref, wb_ref, wc_ref, d_ref, a_ref, s_ref, y_ref, yg_ref):
        for c0 in range(0, rows, chunk):
            s_ref[pl.ds(c0, chunk), :] = _dot_nn(u_ref[pl.ds(c0, chunk), :].astype(BF16), wb_ref[...])
        av = a_ref[...]
        coefs = _scan_coefs(av[:, :st], av[:, st:], reverse=False)

        def tile(b, carry):
            r0 = pl.multiple_of(b * SUBLANES, SUBLANES)
            xr, xi = _scan_tile(s_ref[pl.ds(r0, SUBLANES), :st], s_ref[pl.ds(r0, SUBLANES), st:], carry[0], carry[1],
                                coefs, False)
            s_ref[pl.ds(r0, SUBLANES), :st] = xr
            s_ref[pl.ds(r0, SUBLANES), st:] = xi
            return xr[SUBLANES - 1:, :], xi[SUBLANES - 1:, :]

        zero = jnp.zeros((1, st), F32)
        lax.fori_loop(0, rows // SUBLANES, tile, (zero, zero))
        for c0 in range(0, rows, chunk):
            y = _dot_nt(s_ref[pl.ds(c0, chunk), :].astype(BF16), wc_ref[...]) + d_ref[...] * u_ref[pl.ds(c0, chunk), :]
            y_ref[pl.ds(c0, chunk), :] = y
            yg_ref[pl.ds(c0, chunk), :] = _gelu(y).astype(BF16)

    return pl.pallas_call(
        body, name="s5_fwd", grid=(nb,),
        in_specs=[pl.BlockSpec((rows, LANES), lambda j: (0, j)), pl.BlockSpec((None, LANES, s2), lambda j: (j, 0, 0)),
                  pl.BlockSpec((None, LANES, s2), lambda j: (j, 0, 0)), pl.BlockSpec((1, LANES), lambda j: (0, j)),
                  pl.BlockSpec((None, 1, s2), lambda j: (j, 0, 0))],
        out_specs=[pl.BlockSpec((rows, s2), lambda j: (0, j)), pl.BlockSpec((rows, LANES), lambda j: (0, j)),
                   pl.BlockSpec((rows, LANES), lambda j: (0, j))],
        out_shape=[jax.ShapeDtypeStruct((rows, nb * s2), F32), jax.ShapeDtypeStruct((rows, nb * LANES), F32),
                   jax.ShapeDtypeStruct((rows, nb * LANES), BF16)],
        compiler_params=_params(("parallel",)),
    )(proj, wb, wct, d_skip, abar)


def _s5_bwd(proj, states, y_pre, dyg_a, dyg_b, wb, wct, d_skip, abar):
    rows = proj.shape[0]
    nb = wb.shape[0]
    s2 = 2 * STATE_PER_BATCH
    st = STATE_PER_BATCH
    chunk = _tile(rows, 512, SUBLANES)
    n_tiles = rows // SUBLANES

    def body(u_ref, s_ref, y_ref, ga_ref, gb_ref, wb_ref, wc_ref, d_ref, a_ref,
             du_ref, dwb_ref, dwc_ref, da_ref, dd_ref, ds_ref, dy_ref):
        dy_ref[...] = (ga_ref[...] + gb_ref[...]) * _gelu_grad(y_ref[...])
        dd_ref[...] = jnp.sum(dy_ref[...] * u_ref[...], axis=0, keepdims=True)
        for c0 in range(0, rows, chunk):
            ds_ref[pl.ds(c0, chunk), :] = _dot_nn(dy_ref[pl.ds(c0, chunk), :].astype(BF16), wc_ref[...])
        dwc_ref[...] = _dot_tn(dy_ref[...].astype(BF16), s_ref[...].astype(BF16))
        av = a_ref[...]
        coefs = _scan_coefs(av[:, :st], -av[:, st:], reverse=True)
        row = lax.broadcasted_iota(jnp.int32, (SUBLANES, st), 0)

        def tile(k, carry):
            cr, ci, acc_r, acc_i = carry
            b = n_tiles - 1 - k
            r0 = pl.multiple_of(b * SUBLANES, SUBLANES)
            rp = pl.multiple_of(jnp.maximum(b - 1, 0) * SUBLANES, SUBLANES)
            xr, xi = _scan_tile(ds_ref[pl.ds(r0, SUBLANES), :st], ds_ref[pl.ds(r0, SUBLANES), st:], cr, ci, coefs, True)
            ds_ref[pl.ds(r0, SUBLANES), :st] = xr
            ds_ref[pl.ds(r0, SUBLANES), st:] = xi
            first = jnp.where(b > 0, 1.0, 0.0)
            pr = jnp.where(row == 0, pltpu.roll(s_ref[pl.ds(rp, SUBLANES), :st], 1, 0) * first,
                           pltpu.roll(s_ref[pl.ds(r0, SUBLANES), :st], 1, 0))
            pi = jnp.where(row == 0, pltpu.roll(s_ref[pl.ds(rp, SUBLANES), st:], 1, 0) * first,
                           pltpu.roll(s_ref[pl.ds(r0, SUBLANES), st:], 1, 0))
            acc_r = acc_r + pr * xr + pi * xi
            acc_i = acc_i + pr * xi - pi * xr
            return xr[:1, :], xi[:1, :], acc_r, acc_i

        zero = jnp.zeros((1, st), F32)
        zacc = jnp.zeros((SUBLANES, st), F32)
        _, _, acc_r, acc_i = lax.fori_loop(0, n_tiles, tile, (zero, zero, zacc, zacc))
        da_ref[:, :st] = jnp.sum(acc_r, axis=0, keepdims=True)
        da_ref[:, st:] = jnp.sum(acc_i, axis=0, keepdims=True)
        for c0 in range(0, rows, chunk):
            du_ref[pl.ds(c0, chunk), :] = (_dot_nt(ds_ref[pl.ds(c0, chunk), :].astype(BF16), wb_ref[...])
                                           + d_ref[...] * dy_ref[pl.ds(c0, chunk), :]).astype(du_ref.dtype)
        dwb_ref[...] = _dot_tn(u_ref[...].astype(BF16), ds_ref[...].astype(BF16))

    col = pl.BlockSpec((rows, LANES), lambda j: (0, j))
    return pl.pallas_call(
        body, name="s5_bwd", grid=(nb,),
        in_specs=[col, pl.BlockSpec((rows, s2), lambda j: (0, j)), col, col, col,
                  pl.BlockSpec((None, LANES, s2), lambda j: (j, 0, 0)), pl.BlockSpec((None, LANES, s2), lambda j: (j, 0, 0)),
                  pl.BlockSpec((1, LANES), lambda j: (0, j)), pl.BlockSpec((None, 1, s2), lambda j: (j, 0, 0))],
        out_specs=[col, pl.BlockSpec((None, LANES, s2), lambda j: (j, 0, 0)),
                   pl.BlockSpec((None, LANES, s2), lambda j: (j, 0, 0)), pl.BlockSpec((None, 1, s2), lambda j: (j, 0, 0)),
                   pl.BlockSpec((1, LANES), lambda j: (0, j))],
        out_shape=[jax.ShapeDtypeStruct((rows, nb * LANES), BF16), jax.ShapeDtypeStruct((nb, LANES, s2), F32),
                   jax.ShapeDtypeStruct((nb, LANES, s2), F32), jax.ShapeDtypeStruct((nb, 1, s2), F32),
                   jax.ShapeDtypeStruct((1, nb * LANES), F32)],
        scratch_shapes=[pltpu.VMEM((rows, s2), F32), pltpu.VMEM((rows, LANES), F32)],
        compiler_params=_params(("parallel",)),
    )(proj, states, y_pre, dyg_a, dyg_b, wb, wct, d_skip, abar)


def _glu_norm_fwd(y_pre, z, w, *, tr=256):
    rows, width = y_pre.shape
    tr = _tile(rows, tr, SUBLANES)

    def body(y_ref, z_ref, w_ref, o_ref):
        v = _gelu(y_ref[...]) * jax.nn.sigmoid(z_ref[...])
        o_ref[...] = (v * _rms_rows(v) * w_ref[...]).astype(o_ref.dtype)

    blk = pl.BlockSpec((tr, width), lambda i: (i, 0))
    return pl.pallas_call(
        body, name="glu_norm_fwd", grid=(rows // tr,),
        in_specs=[blk, blk, pl.BlockSpec((1, width), lambda i: (0, 0))], out_specs=blk,
        out_shape=jax.ShapeDtypeStruct((rows, width), BF16), compiler_params=_params(("parallel",)),
    )(y_pre, z, w)


def _glu_norm_bwd(y_pre, z, w, dycat, *, tr=256):
    rows, width = y_pre.shape
    tr = _tile(rows, tr, SUBLANES)

    def body(y_ref, z_ref, w_ref, dy_ref, dz_ref, dg_ref, dw_ref, db_ref):
        yg = _gelu(y_ref[...])
        sg = jax.nn.sigmoid(z_ref[...])
        dv, dwp = _rmsnorm_bwd_rows(yg * sg, w_ref[...], dy_ref[...])
        dz = dv * yg * sg * (1.0 - sg)
        dz_ref[...] = dz.astype(dz_ref.dtype)
        dg_ref[...] = dv * sg
        dw_part = jnp.sum(dwp, axis=0, keepdims=True)
        db_part = jnp.sum(dz, axis=0, keepdims=True)

        @pl.when(pl.program_id(0) == 0)
        def _():
            dw_ref[...] = dw_part
            db_ref[...] = db_part

        @pl.when(pl.program_id(0) > 0)
        def _():
            dw_ref[...] += dw_part
            db_ref[...] += db_part

    blk = pl.BlockSpec((tr, width), lambda i: (i, 0))
    vec = pl.BlockSpec((1, width), lambda i: (0, 0))
    return pl.pallas_call(
        body, name="glu_norm_bwd", grid=(rows // tr,), in_specs=[blk, blk, vec, blk], out_specs=[blk, blk, vec, vec],
        out_shape=[jax.ShapeDtypeStruct((rows, width), BF16), jax.ShapeDtypeStruct((rows, width), F32)]
        + [jax.ShapeDtypeStruct((1, width), F32)] * 2,
        compiler_params=_params(("arbitrary",)),
    )(y_pre, z, w, dycat)


def _rope_tables(pos, freq, sign):
    rows = pos.shape[0]

    def body(p_ref, f_ref, s_ref, cos_ref, sin_ref):
        ang = p_ref[...] * f_ref[...]
        cos_ref[...] = jnp.cos(ang)
        sin_ref[...] = jnp.sin(ang) * s_ref[...]

    return pl.pallas_call(body, name="rope_tables", out_shape=[jax.ShapeDtypeStruct((rows, LANES), F32)] * 2)(pos, freq, sign)


def _rope(x, cos, sin_signed):
    half = QK_ROPE_DIM // 2
    src = lax.broadcasted_iota(jnp.int32, (LANES, LANES), 0)
    dst = lax.broadcasted_iota(jnp.int32, (LANES, LANES), 1)
    swap = jnp.where(jnp.logical_or(jnp.logical_and(dst < half, src == dst + half),
                                    jnp.logical_and(jnp.logical_and(dst >= half, dst < 2 * half), src == dst - half)),
                     1.0, 0.0).astype(F32)
    swapped = _dot_exact(x, swap, ((1,), (0,)))
    return x * cos + swapped * sin_signed


def _attn_prep(q, kv, proj, kpe_col, cos, sin, *, tr=256):
    rows = q.shape[0]
    heads = q.shape[1] // HEAD_SLOT
    tr = _tile(rows, tr, SUBLANES)

    def body(q_ref, kv_ref, kpe_ref, cos_ref, sin_ref, qc_ref, kc_ref, v_ref):
        c, s = cos_ref[...], sin_ref[...]
        kpe = _rope(kpe_ref[...], c, s).astype(BF16)
        for h in range(heads):
            nope = slice(h * HEAD_SLOT, h * HEAD_SLOT + LANES)
            pe = slice(h * HEAD_SLOT + LANES, (h + 1) * HEAD_SLOT)
            qc_ref[:, nope] = q_ref[:, nope].astype(BF16)
            qc_ref[:, pe] = _rope(q_ref[:, pe], c, s).astype(BF16)
            kc_ref[:, nope] = kv_ref[:, nope].astype(BF16)
            kc_ref[:, pe] = kpe
            v_ref[:, h * LANES:(h + 1) * LANES] = kv_ref[:, pe].astype(BF16)

    slots = pl.BlockSpec((tr, heads * HEAD_SLOT), lambda i: (i, 0))
    tab = pl.BlockSpec((tr, LANES), lambda i: (i, 0))
    return pl.pallas_call(
        body, name="attn_prep", grid=(rows // tr,),
        in_specs=[slots, slots, pl.BlockSpec((tr, LANES), lambda i: (i, kpe_col)), tab, tab],
        out_specs=[slots, slots, pl.BlockSpec((tr, heads * LANES), lambda i: (i, 0))],
        out_shape=[jax.ShapeDtypeStruct((rows, heads * HEAD_SLOT), BF16)] * 2
        + [jax.ShapeDtypeStruct((rows, heads * LANES), BF16)],
        compiler_params=_params(("parallel",)),
    )(q, kv, proj, cos, sin)


def _causal(tq, tk):
    return lax.broadcasted_iota(jnp.int32, (tq, tk), 1) <= lax.broadcasted_iota(jnp.int32, (tq, tk), 0)


def _attn_fwd(qc, kc, vb, *, scale, tq=512):
    rows = qc.shape[0]
    heads = qc.shape[1] // HEAD_SLOT
    tq = _tile(rows, tq, SUBLANES)
    tk = tq

    def body(q_ref, k_ref, v_ref, o_ref, lse_ref):
        i = pl.program_id(1)
        q = q_ref[...]

        def step(j, carry, diagonal):
            m, l, acc = carry
            k0 = pl.multiple_of(j * tk, tk)
            s = _dot_nt(q, k_ref[pl.ds(k0, tk), :]) * scale
            if diagonal:
                s = jnp.where(_causal(tq, tk), s, NEG_INF)
            m_new = jnp.maximum(m, jnp.max(s, axis=-1, keepdims=True))
            p = jnp.exp(s - m_new)
            alpha = jnp.exp(m - m_new)
            l = alpha * l + jnp.sum(p, axis=-1, keepdims=True)
            acc = alpha * acc + _dot_nn(p.astype(BF16), v_ref[pl.ds(k0, tk), :])
            return m_new, l, acc

        init = (jnp.full((tq, 1), NEG_INF, F32), jnp.zeros((tq, 1), F32), jnp.zeros((tq, LANES), F32))
        below = lax.fori_loop(0, i, lambda j, carry: step(j, carry, False), init)
        m, l, acc = step(i, below, True)
        o_ref[...] = acc / l
        lse_ref[...] = jnp.broadcast_to(m + jnp.log(l), (tq, LANES))

    return pl.pallas_call(
        body, name="attn_fwd", grid=(heads, rows // tq),
        in_specs=[pl.BlockSpec((tq, HEAD_SLOT), lambda h, i: (i, h)), pl.BlockSpec((rows, HEAD_SLOT), lambda h, i: (0, h)),
                  pl.BlockSpec((rows, LANES), lambda h, i: (0, h))],
        out_specs=[pl.BlockSpec((tq, LANES), lambda h, i: (i, h))] * 2,
        out_shape=[jax.ShapeDtypeStruct((rows, heads * LANES), F32)] * 2,
        compiler_params=_params(("parallel", "parallel")),
    )(qc, kc, vb)


def _attn_bwd(qc, kc, vb, o, do, lse, cos, sin, *, scale, tk=512):
    rows = qc.shape[0]
    heads = qc.shape[1] // HEAD_SLOT
    tk = _tile(rows, tk, SUBLANES)
    tq = tk
    nq = rows // tq

    def body(q_ref, k_ref, v_ref, o_ref, do_ref, lse_ref, cos_ref, sin_ref, dq_ref, dkv_ref, dkpe_ref, dq_acc, delta_ref):
        j = pl.program_id(1)

        @pl.when(j == 0)
        def _():
            dq_acc[...] = jnp.zeros_like(dq_acc)
            for r0 in range(0, rows, tq):
                d = jnp.sum(do_ref[pl.ds(r0, tq), :] * o_ref[pl.ds(r0, tq), :], axis=-1, keepdims=True)
                delta_ref[pl.ds(r0, tq), :] = jnp.broadcast_to(d, (tq, LANES))

        kb, vv = k_ref[...], v_ref[...]

        def step(i, carry, diagonal):
            dk, dv = carry
            q0 = pl.multiple_of(i * tq, tq)
            qb = q_ref[pl.ds(q0, tq), :]
            dob = do_ref[pl.ds(q0, tq), :].astype(BF16)
            s = _dot_nt(qb, kb) * scale
            p = jnp.exp(s - lse_ref[pl.ds(q0, tq), :1])
            if diagonal:
                p = jnp.where(_causal(tq, tk), p, 0.0)
            dv = dv + _dot_tn(p.astype(BF16), dob)
            ds = (p * (_dot_nt(dob, vv) - delta_ref[pl.ds(q0, tq), :1])).astype(BF16)
            dk = dk + _dot_tn(ds, qb)
            dq_acc[pl.ds(q0, tq), :] += _dot_nn(ds, kb)
            return dk, dv

        zero = (jnp.zeros((tk, HEAD_SLOT), F32), jnp.zeros((tk, LANES), F32))
        dk, dv = lax.fori_loop(j + 1, nq, lambda i, carry: step(i, carry, False), step(j, zero, True))
        dkv_ref[:, :LANES] = (dk[:, :LANES] * scale).astype(dkv_ref.dtype)
        dkv_ref[:, LANES:] = dv.astype(dkv_ref.dtype)
        dkpe_ref[...] = dk[:, LANES:] * scale

        @pl.when(j == nq - 1)
        def _():
            for r0 in range(0, rows, tq):
                dq = dq_acc[pl.ds(r0, tq), :] * scale
                dq_ref[pl.ds(r0, tq), :LANES] = dq[:, :LANES].astype(dq_ref.dtype)
                dq_ref[pl.ds(r0, tq), LANES:] = _rope(dq[:, LANES:], cos_ref[pl.ds(r0, tq), :],
                                                      -sin_ref[pl.ds(r0, tq), :]).astype(dq_ref.dtype)

    full_q = pl.BlockSpec((rows, HEAD_SLOT), lambda h, j: (0, h))
    full_v = pl.BlockSpec((rows, LANES), lambda h, j: (0, h))
    tab = pl.BlockSpec((rows, LANES), lambda h, j: (0, 0))
    return pl.pallas_call(
        body, name="attn_bwd", grid=(heads, rows // tk),
        in_specs=[full_q, pl.BlockSpec((tk, HEAD_SLOT), lambda h, j: (j, h)), pl.BlockSpec((tk, LANES), lambda h, j: (j, h)),
                  full_v, full_v, full_v, tab, tab],
        out_specs=[full_q, pl.BlockSpec((tk, HEAD_SLOT), lambda h, j: (j, h)), pl.BlockSpec((tk, LANES), lambda h, j: (j, h))],
        out_shape=[jax.ShapeDtypeStruct((rows, heads * HEAD_SLOT), BF16), jax.ShapeDtypeStruct((rows, heads * HEAD_SLOT), BF16),
                   jax.ShapeDtypeStruct((rows, heads * LANES), F32)],
        scratch_shapes=[pltpu.VMEM((rows, HEAD_SLOT), F32), pltpu.VMEM((rows, LANES), F32)],
        compiler_params=_params(("parallel", "arbitrary")),
    )(qc, kc, vb, o, do, lse, cos, sin)


def _kpe_bwd(dkpe_heads, cos, sin, *, tr=512):
    rows = dkpe_heads.shape[0]
    heads = dkpe_heads.shape[1] // LANES
    tr = _tile(rows, tr, 2 * SUBLANES)

    def body(d_ref, cos_ref, sin_ref, o_ref):
        acc = d_ref[:, :LANES]
        for h in range(1, heads):
            acc = acc + d_ref[:, h * LANES:(h + 1) * LANES]
        o_ref[...] = _rope(acc, cos_ref[...], -sin_ref[...]).astype(o_ref.dtype)

    tab = pl.BlockSpec((tr, LANES), lambda i: (i, 0))
    return pl.pallas_call(
        body, name="kpe_bwd", grid=(rows // tr,),
        in_specs=[pl.BlockSpec((tr, heads * LANES), lambda i: (i, 0)), tab, tab], out_specs=tab,
        out_shape=jax.ShapeDtypeStruct((rows, LANES), BF16), compiler_params=_params(("parallel",)),
    )(dkpe_heads, cos, sin)


CONV_ROWS = 128


def _with_halo(ref, r0, ci, n_chunks, ch, lanes, before, after):
    parts = []
    if before:
        lo = pl.multiple_of(jnp.maximum(r0 - SUBLANES, 0), SUBLANES)
        parts.append(ref[pl.ds(lo, SUBLANES), lanes] * jnp.where(ci > 0, 1.0, 0.0))
    parts.append(ref[pl.ds(r0, ch), lanes])
    if after:
        hi = pl.multiple_of(jnp.minimum(r0 + ch, n_chunks * ch - SUBLANES), SUBLANES)
        parts.append(ref[pl.ds(hi, SUBLANES), lanes] * jnp.where(ci < n_chunks - 1, 1.0, 0.0))
    return jnp.concatenate(parts, axis=0)


def _taps(ext):
    return pltpu.roll(ext, 2, 0)[SUBLANES:], pltpu.roll(ext, 1, 0)[SUBLANES:], ext[SUBLANES:]


def _conv3(taps, w, b):
    return w[0:1, :] * taps[0] + w[1:2, :] * taps[1] + w[2:3, :] * taps[2] + b


def _conv_gate_fwd(a, conv_w, conv_b, *, tc=256):
    rows, f2 = a.shape
    f = f2 // 2
    tc = _tile(f, tc)
    nc = f // tc
    ch = _tile(rows, CONV_ROWS, SUBLANES)
    n_chunks = rows // ch

    def body(ag_ref, av_ref, wg_ref, wv_ref, bg_ref, bv_ref, o_ref):
        for lt in range(tc // LANES):
            lanes = slice(lt * LANES, (lt + 1) * LANES)
            wg, wv, bg, bv = wg_ref[:, lanes], wv_ref[:, lanes], bg_ref[:, lanes], bv_ref[:, lanes]

            def chunk(ci, carry):
                r0 = pl.multiple_of(ci * ch, ch)
                gate = _conv3(_taps(_with_halo(ag_ref, r0, ci, n_chunks, ch, lanes, True, False)), wg, bg)
                val = _conv3(_taps(_with_halo(av_ref, r0, ci, n_chunks, ch, lanes, True, False)), wv, bv)
                o_ref[pl.ds(r0, ch), lanes] = (gate * jax.nn.sigmoid(gate) * val).astype(o_ref.dtype)
                return carry

            lax.fori_loop(0, n_chunks, chunk, 0)

    return pl.pallas_call(
        body, name="conv_gate_fwd", grid=(nc,),
        in_specs=[pl.BlockSpec((rows, tc), lambda j: (0, j)), pl.BlockSpec((rows, tc), lambda j: (0, j + nc)),
                  pl.BlockSpec((SUBLANES, tc), lambda j: (0, j)), pl.BlockSpec((SUBLANES, tc), lambda j: (0, j + nc)),
                  pl.BlockSpec((1, tc), lambda j: (0, j)), pl.BlockSpec((1, tc), lambda j: (0, j + nc))],
        out_specs=pl.BlockSpec((rows, tc), lambda j: (0, j)),
        out_shape=jax.ShapeDtypeStruct((rows, f), BF16), compiler_params=_params(("parallel",)),
    )(a, a, conv_w, conv_w, conv_b, conv_b)


def _conv_gate_bwd(a, conv_w, conv_b, dg, *, tc=256):
    rows, f2 = a.shape
    f = f2 // 2
    tc = _tile(f, tc)
    nc = f // tc
    ch = _tile(rows, CONV_ROWS, SUBLANES)
    n_chunks = rows // ch
    ext_rows = ch + SUBLANES

    def fold(x):
        return jnp.sum(x.reshape(ch // SUBLANES, SUBLANES, LANES), axis=0)

    def body(ag_ref, av_ref, wg_ref, wv_ref, bg_ref, bv_ref, dg_ref, da_ref, dw_ref, db_ref):
        for lt in range(tc // LANES):
            lanes = slice(lt * LANES, (lt + 1) * LANES)
            wg, wv, bg, bv = wg_ref[:, lanes], wv_ref[:, lanes], bg_ref[:, lanes], bv_ref[:, lanes]

            def chunk(ci, acc):
                r0 = pl.multiple_of(ci * ch, ch)
                taps_g = _taps(_with_halo(ag_ref, r0, ci, n_chunks, ch, lanes, True, True))
                taps_v = _taps(_with_halo(av_ref, r0, ci, n_chunks, ch, lanes, True, True))
                dge = _with_halo(dg_ref, r0, ci, n_chunks, ch, lanes, False, True)
                gate, val = _conv3(taps_g, wg, bg), _conv3(taps_v, wv, bv)
                sg = jax.nn.sigmoid(gate)
                d_gate = dge * val * sg * (1.0 + gate * (1.0 - sg))
                d_val = dge * gate * sg
                new = []
                for half, (taps, w, d) in enumerate(((taps_g, wg, d_gate), (taps_v, wv, d_val))):
                    da = (w[2:3, :] * d[:ch] + w[1:2, :] * pltpu.roll(d, ext_rows - 1, 0)[:ch]
                          + w[0:1, :] * pltpu.roll(d, ext_rows - 2, 0)[:ch])
                    da_ref[half, pl.ds(r0, ch), lanes] = da.astype(da_ref.dtype)
                    dc = d[:ch]
                    sums = [fold(dc)] + [fold(dc * t[:ch]) for t in taps]
                    new.append(tuple(x + s for x, s in zip(acc[half], sums)))
                return tuple(new)

            zero = tuple(jnp.zeros((SUBLANES, LANES), F32) for _ in range(4))
            acc = lax.fori_loop(0, n_chunks, chunk, (zero, zero))
            row = lax.broadcasted_iota(jnp.int32, (SUBLANES, LANES), 0)
            for half in range(2):
                db, *taps = (jnp.sum(x, axis=0, keepdims=True) for x in acc[half])
                db_ref[half, :, lanes] = db
                dw = jnp.zeros((SUBLANES, LANES), F32)
                for tap in range(3):
                    dw = jnp.where(row == tap, taps[tap], dw)
                dw_ref[half, :, lanes] = dw

    lo = lambda j: (0, j)
    hi = lambda j: (0, j + nc)
    both = lambda j: (0, 0, j)
    return pl.pallas_call(
        body, name="conv_gate_bwd", grid=(nc,),
        in_specs=[pl.BlockSpec((rows, tc), lo), pl.BlockSpec((rows, tc), hi), pl.BlockSpec((SUBLANES, tc), lo),
                  pl.BlockSpec((SUBLANES, tc), hi), pl.BlockSpec((1, tc), lo), pl.BlockSpec((1, tc), hi),
                  pl.BlockSpec((rows, tc), lo)],
        out_specs=[pl.BlockSpec((2, rows, tc), both), pl.BlockSpec((2, SUBLANES, tc), both), pl.BlockSpec((2, 1, tc), both)],
        out_shape=[jax.ShapeDtypeStruct((2, rows, f), BF16), jax.ShapeDtypeStruct((2, SUBLANES, f), F32),
                   jax.ShapeDtypeStruct((2, 1, f), F32)],
        compiler_params=_params(("parallel",)),
    )(a, a, conv_w, conv_w, conv_b, conv_b, dg)


def _wgrad(a, b, rows, cols, row_sharded, name, **kw):
    return functools.partial(_wgrad_half, a, b, rows, cols, row_sharded, name, **kw)


class _NoExchange:
    def __init__(self, later, ffn):
        self.later, self.ffn = later, ffn

    def mixer_weights(self, after):
        return self.later

    def ffn_weights_arrived(self, after):
        return None

    def ffn_weights(self, after):
        return self.ffn

    def ffn_down_arrived(self, after):
        return None

    def ffn_down_weight(self, after):
        return self.ffn["ffn_w_down"]

    def ffn_grads(self, makers, after):
        self.ffn_makers = makers
        return None

    def ffn_backward_done(self, after):
        return None


def _local_step(x, posf, target, w, hooks):
    rows, d = x.shape
    width = w["ssm_d"].shape[1]
    qr, kvr = w["mla_q_norm_w"].shape[1], w["mla_kv_norm_w"].shape[1]
    heads = w["mla_w_ukv"].shape[1] // HEAD_SLOT
    f2 = w["ffn_conv_b"].shape[1]
    inp = w["w_in"].shape[0]
    scale = (QK_NOPE_DIM + QK_ROPE_DIM) ** -0.5
    g = {}

    hn = _rmsnorm_fwd(x, w["attn_norm_w"], name="attn_norm")
    proj = _matmul(hn, w["w_in"], mode="nt", name="in_proj")

    s5_weights = (w["ssm_lambda_re"], w["ssm_lambda_im"], w["ssm_log_dt"], w["ssm_b_re"], w["ssm_b_im"])
    wb, wct, abar = _s5_bands(*s5_weights, w["ssm_c_re"], w["ssm_c_im"])
    states, y_pre, yg = _s5_fwd(proj, wb, wct, w["ssm_d"], abar)
    later = hooks.mixer_weights(yg)
    z = _matmul(yg, later["ssm_w_glu"], mode="nn", name="glu_proj", bias=w["ssm_b_glu"])
    ys = _glu_norm_fwd(y_pre, z, w["ssm_out_norm_w"])

    q_col, kv_col, kpe_col = width // qr, (width + qr) // kvr, (width + qr + kvr) // LANES
    assert width % qr == 0 and (width + qr) % kvr == 0
    qn = _rmsnorm_fwd(proj, w["mla_q_norm_w"], name="q_norm", width=qr, col=q_col)
    kvn = _rmsnorm_fwd(proj, w["mla_kv_norm_w"], name="kv_norm", width=kvr, col=kv_col)
    q = _matmul(qn, w["mla_w_uq"], mode="nn", name="q_proj")
    kv = _matmul(kvn, w["mla_w_ukv"], mode="nn", name="kv_proj")
    half = QK_ROPE_DIM // 2
    inv_freq = ROPE_THETA ** (-jnp.arange(0, QK_ROPE_DIM, 2, dtype=F32) / QK_ROPE_DIM)
    zeros = jnp.zeros((LANES - QK_ROPE_DIM,), F32)
    freq = jnp.concatenate([inv_freq, inv_freq, zeros]).reshape(1, LANES)
    sign = jnp.concatenate([-jnp.ones((half,), F32), jnp.ones((half,), F32), zeros]).reshape(1, LANES)
    cos, sin = _rope_tables(posf, freq, sign)
    qc, kc, vb = _attn_prep(q, kv, proj, kpe_col, cos, sin)
    o, lse = _attn_fwd(qc, kc, vb, scale=scale, tq=ATTN_BLOCK)
    ym = _rmsnorm_fwd(o, w["mla_out_norm_w"], name="mla_out_norm")
    ycat = jnp.concatenate([ys, ym], axis=1)
    h1 = _matmul(ycat, later["w_out"], mode="nn", name="out_proj", add=x, after=hooks.ffn_weights_arrived(ycat))

    hn2 = _rmsnorm_fwd(h1, w["ffn_norm_w"], name="ffn_norm")
    ffn = hooks.ffn_weights(hn2)
    a = _matmul(hn2, ffn["ffn_w_up"], mode="nn", name="ffn_up", tm=FFN_ROWS)
    started = hooks.ffn_down_arrived(a)
    conv_b = w["ffn_conv_b"] if started is None else w["ffn_conv_b"] + started[:1, :1]
    gated = _conv_gate_fwd(a, ffn["ffn_conv_w"], conv_b)
    w_down = hooks.ffn_down_weight(gated)
    h2 = _matmul(gated, w_down, mode="nn", name="ffn_down", add=h1, tk=2816, tm=FFN_ROWS)
    loss_tile, dh2, dh2_mxu, g["final_norm_w"] = _final_norm_loss(h2, w["final_norm_w"], target)

    dgated = _matmul(dh2_mxu, w_down, mode="nt", name="ffn_down_dx", tm=FFN_ROWS, tn=1408)
    da, dcw, dcb = _conv_gate_bwd(a, ffn["ffn_conv_w"], w["ffn_conv_b"], dgated)
    g["ffn_conv_w"] = jnp.concatenate([dcw[0, :3], dcw[1, :3]], axis=1)
    g["ffn_conv_b"] = jnp.concatenate([dcb[0], dcb[1]], axis=1)
    started = hooks.ffn_grads({
        "ffn_w_up": _wgrad(hn2, da, d, f2, False, "ffn_up_dw", b_split=True, tn=_tile(f2 // N_CHIPS, 1408)),
        "ffn_w_down": _wgrad(gated, dh2_mxu, f2 // 2, d, True, "ffn_down_dw", tm=f2 // 2 // N_CHIPS, tn=512)}, dcb)
    dhn2 = _matmul(da, ffn["ffn_w_up"], mode="nt", name="ffn_up_dx", a_split=True, tk=_tile(f2 // 2, 2816), tm=FFN_ROWS,
                   after=started)
    dh1, dh1_mxu, g["ffn_norm_w"] = _rmsnorm_bwd(h1, w["ffn_norm_w"], dhn2, name="ffn_norm_bwd", add=dh2,
                                                dx_dtypes=(F32, BF16))

    dycat = _matmul(dh1_mxu, later["w_out"], mode="nt", name="out_proj_dx")
    g["w_out"] = _wgrad(ycat, dh1_mxu, 2 * width, d, True, "out_proj_dw")
    started = hooks.ffn_backward_done(dycat)
    mla_out_norm_w, ssm_out_norm_w = w["mla_out_norm_w"], w["ssm_out_norm_w"]
    if started is not None:
        mla_out_norm_w, ssm_out_norm_w = mla_out_norm_w + started[:1, :1], ssm_out_norm_w + started[:1, :1]

    do, g["mla_out_norm_w"] = _rmsnorm_bwd(o, mla_out_norm_w, dycat, name="mla_out_norm_bwd", width=width, dy_col=1)
    dq, dkv, dkpe_heads = _attn_bwd(qc, kc, vb, o, do, lse, cos, sin, scale=scale, tk=ATTN_BLOCK)
    dkpe = _kpe_bwd(dkpe_heads, cos, sin)
    g["mla_w_uq"] = _wgrad(qn, dq, qr, heads * HEAD_SLOT, False, "q_proj_dw")
    dqn = _matmul(dq, w["mla_w_uq"], mode="nt", name="q_proj_dx")
    dcq, g["mla_q_norm_w"] = _rmsnorm_bwd(proj, w["mla_q_norm_w"], dqn, name="q_norm_bwd", width=qr, col=q_col,
                                          dx_dtypes=(BF16,))
    g["mla_w_ukv"] = _wgrad(kvn, dkv, kvr, heads * HEAD_SLOT, False, "kv_proj_dw")
    dkvn = _matmul(dkv, w["mla_w_ukv"], mode="nt", name="kv_proj_dx")
    dckv, g["mla_kv_norm_w"] = _rmsnorm_bwd(proj, w["mla_kv_norm_w"], dkvn, name="kv_norm_bwd", width=kvr, col=kv_col,
                                            dx_dtypes=(BF16,))

    dz, dyg_a, g["ssm_out_norm_w"], g["ssm_b_glu"] = _glu_norm_bwd(y_pre, z, ssm_out_norm_w, dycat)
    dyg_b = _matmul(dz, later["ssm_w_glu"], mode="nt", name="glu_proj_dx")
    g["ssm_w_glu"] = _wgrad(yg, dz, width, width, True, "glu_proj_dw")
    du, dwb, dwct, dabar, g["ssm_d"] = _s5_bwd(proj, states, y_pre, dyg_a, dyg_b, wb, wct, w["ssm_d"], abar)
    (g["ssm_lambda_re"], g["ssm_lambda_im"], g["ssm_log_dt"], g["ssm_b_re"], g["ssm_b_im"], g["ssm_c_re"],
     g["ssm_c_im"]) = _s5_bands_bwd(*s5_weights, dwb, dwct, dabar)

    pad = jnp.zeros((rows, inp - (width + qr + kvr + LANES)), BF16)
    dproj = jnp.concatenate([du, dcq, dckv, dkpe, pad], axis=1)
    g["w_in"] = _wgrad(dproj, hn, inp, d, False, "in_proj_dw")
    dhn = _matmul(dproj, w["w_in"], mode="nn", name="in_proj_dx")
    dx, g["attn_norm_w"] = _rmsnorm_bwd(x, w["attn_norm_w"], dhn, name="attn_norm_bwd", add=dh1)
    return loss_tile, dx, g


ANY = pl.BlockSpec(memory_space=pl.ANY)
MESH = pl.DeviceIdType.MESH


def _mesh_pos():
    return lax.axis_index("x"), lax.axis_index("y"), lax.axis_index("c")


def _other_chips(x, y):
    return [(1 - x, y), (x, 1 - y), (1 - x, 1 - y)]


def _remote(src, dst, send_sems, recv_sems, k, to):
    return pltpu.make_async_remote_copy(src_ref=src, dst_ref=dst, send_sem=send_sems.at[k], recv_sem=recv_sems.at[k],
                                        device_id=to, device_id_type=MESH)


def _place_shard(shard, piece_idx, row_sharded, name, out_dtype=BF16, pieces=N_CHIPS, after=None):
    rs, cs = shard.shape
    tr = _tile(rs, 256, 2 * SUBLANES)
    rb = rs // tr
    extra = [] if after is None else [after]

    def body(p_ref, x_ref, *rest):
        o_ref = rest[-1]
        o_ref[...] = x_ref[...].astype(o_ref.dtype)

    if row_sharded:
        out_shape, out_map = (pieces * rs, cs), (lambda i, p_ref: (p_ref[0] * rb + i, 0))
    else:
        out_shape, out_map = (rs, pieces * cs), (lambda i, p_ref: (i, p_ref[0]))
    return pl.pallas_call(
        body, name=name, out_shape=jax.ShapeDtypeStruct(out_shape, out_dtype),
        grid_spec=pltpu.PrefetchScalarGridSpec(
            num_scalar_prefetch=1, grid=(rb,),
            in_specs=[pl.BlockSpec((tr, cs), lambda i, p_ref: (i, 0))] + [pl.BlockSpec(memory_space=pl.ANY)] * len(extra),
            out_specs=pl.BlockSpec((tr, cs), out_map)),
        compiler_params=_params(("parallel",)),
    )(piece_idx, shard, *extra)


def _gather_weights(placed, name):
    n = len(placed)
    meta = [(row_sharded, direct) for _, row_sharded, direct in placed]
    over_ici, over_d2d = _gather_plans(meta)
    forwarded = [t for t, (_, direct) in enumerate(meta) if not direct]

    def body(*refs):
        outs = refs[n:2 * n]
        send_sems, recv_sems, pass_send_sems, pass_recv_sems = refs[2 * n:]
        first, arrivals = over_ici(outs, send_sems, recv_sems)
        passed, passed_arrivals = over_d2d([outs[t] for t in forwarded], pass_send_sems, pass_recv_sems)
        for cp in first:
            cp.start()
        for t in range(n):
            for j in range(3):
                arrivals[3 * t + j].wait_recv()
                if t in forwarded:
                    passed[3 * forwarded.index(t) + j].start()
        for cp in passed_arrivals:
            cp.wait_recv()
        for cp in first + passed:
            cp.wait_send()

    return pl.pallas_call(
        body, name=name, in_specs=[ANY] * n, out_specs=[ANY] * n,
        out_shape=[jax.ShapeDtypeStruct(arr.shape, arr.dtype) for arr, _, _ in placed],
        input_output_aliases={t: t for t in range(n)},
        scratch_shapes=[pltpu.SemaphoreType.DMA((3 * n,)), pltpu.SemaphoreType.DMA((3 * n,)),
                        pltpu.SemaphoreType.DMA((3 * len(forwarded),)), pltpu.SemaphoreType.DMA((3 * len(forwarded),))],
    )(*[arr for arr, _, _ in placed])


def _gather_plans(meta):
    def window(ref, row_sharded, piece, half):
        r, cc = ref.shape
        if row_sharded:
            rs = r // N_CHIPS
            if half is None:
                return ref.at[pl.ds(piece * rs, rs), :]
            return ref.at[pl.ds(piece * rs + half * (rs // 2), rs // 2), :]
        cs = cc // N_CHIPS
        if half is None:
            return ref.at[:, pl.ds(piece * cs, cs)]
        return ref.at[pl.ds(half * (r // 2), r // 2), pl.ds(piece * cs, cs)]

    def over_ici(refs, send_sems, recv_sems):
        x, y, c = _mesh_pos()
        sends, recvs = [], []
        for t, (row_sharded, direct) in enumerate(meta):
            mine = window(refs[t], row_sharded, 2 * x + y, None if direct else c)
            for j, (px, py) in enumerate(_other_chips(x, y)):
                theirs = window(refs[t], row_sharded, 2 * px + py, None if direct else c)
                sends.append(_remote(mine, mine, send_sems, recv_sems, 3 * t + j, (px, py, c)))
                recvs.append(_remote(theirs, theirs, send_sems, recv_sems, 3 * t + j, (px, py, c)))
        return sends, recvs

    def over_d2d(refs, send_sems, recv_sems):
        x, y, c = _mesh_pos()
        sends, recvs = [], []
        rows = [row_sharded for row_sharded, direct in meta if not direct]
        for t, row_sharded in enumerate(rows):
            for j, (px, py) in enumerate(_other_chips(x, y)):
                got = window(refs[t], row_sharded, 2 * px + py, c)
                other = window(refs[t], row_sharded, 2 * px + py, 1 - c)
                sends.append(_remote(got, got, send_sems, recv_sems, 3 * t + j, (x, y, 1 - c)))
                recvs.append(_remote(other, other, send_sems, recv_sems, 3 * t + j, (x, y, 1 - c)))
        return sends, recvs

    return over_ici, over_d2d


HBM = pl.BlockSpec(memory_space=pltpu.HBM)
SEMAPHORES = pl.BlockSpec(memory_space=pltpu.SEMAPHORE)
DATAFLOW = pltpu.SideEffectType.DATAFLOW_SIDE_EFFECTING


def _start_copies(name, arrays, plan, n_copies, after):
    n = len(arrays)

    def body(*refs):
        sends, _ = plan(refs[:n], refs[n + 1], refs[n + 2])
        for cp in sends:
            cp.start()
        token = refs[2 * n + 3]
        token[...] = jnp.zeros_like(token)

    out = pl.pallas_call(
        body, name=name,
        out_shape=(pltpu.SemaphoreType.DMA((n_copies,)), pltpu.SemaphoreType.DMA((n_copies,)),
                   *[pltpu.HBM(a.shape, a.dtype) for a in arrays], jax.ShapeDtypeStruct((SUBLANES, LANES), F32)),
        in_specs=[HBM] * n + [ANY],
        out_specs=(SEMAPHORES, SEMAPHORES, *[HBM] * n, pl.BlockSpec(memory_space=pltpu.VMEM)),
        input_output_aliases={t: t + 2 for t in range(n)},
        compiler_params=pltpu.CompilerParams(has_side_effects=DATAFLOW),
    )(*[pltpu.with_memory_space_constraint(a, pltpu.HBM) for a in arrays], after)
    return out[0], out[1], list(out[2:2 + n]), out[2 + n]


def _wait_copies(name, started, plan, after):
    send_sems, recv_sems, arrays, _ = started
    n = len(arrays)

    def body(*refs):
        sends, recvs = plan(refs[:n], refs[n], refs[n + 1])
        for cp in sends:
            cp.wait_send()
        for cp in recvs:
            cp.wait_recv()

    out = pl.pallas_call(
        body, name=name, out_shape=[pltpu.HBM(a.shape, a.dtype) for a in arrays],
        in_specs=[HBM] * n + [SEMAPHORES, SEMAPHORES, ANY], out_specs=[HBM] * n,
        input_output_aliases={t: t for t in range(n)},
        compiler_params=pltpu.CompilerParams(has_side_effects=DATAFLOW),
    )(*arrays, send_sems, recv_sems, after)
    return list(out)


def _exchange(name, arrays, plan, n_copies, after=None):
    n = len(arrays)
    extra = [] if after is None else [after]

    def body(*refs):
        outs = refs[n + len(extra):2 * n + len(extra)]
        send_sems, recv_sems = refs[2 * n + len(extra):]
        sends, recvs = plan(outs, send_sems, recv_sems)
        for cp in sends:
            cp.start()
        for cp in recvs:
            cp.wait_recv()
        for cp in sends:
            cp.wait_send()

    return pl.pallas_call(
        body, name=name, in_specs=[ANY] * (n + len(extra)), out_specs=[ANY] * n,
        out_shape=[jax.ShapeDtypeStruct(a.shape, a.dtype) for a in arrays],
        input_output_aliases={t: t for t in range(n)},
        scratch_shapes=[pltpu.SemaphoreType.DMA((n_copies,)), pltpu.SemaphoreType.DMA((n_copies,))],
    )(*arrays, *extra)


def _give_plan(n):
    def plan(refs, send_sems, recv_sems):
        x, y, c = _mesh_pos()
        sends = [_remote(refs[t], refs[n + t], send_sems, recv_sems, t, (x, y, 1 - c)) for t in range(n)]
        return sends, sends

    return plan


def _scatter_plan(n):
    def plan(refs, send_sems, recv_sems):
        x, y, c = _mesh_pos()
        sends = []
        for t in range(n):
            for j, (px, py) in enumerate(_other_chips(x, y)):
                sends.append(_remote(refs[t].at[2 * px + py], refs[n + t].at[j], send_sems, recv_sems, 3 * t + j, (px, py, c)))
        return sends, sends

    return plan


def _scatter_shapes(sums):
    return [jax.ShapeDtypeStruct((3,) + s.shape[1:], s.dtype) for s in sums]


def _join_plan(n):
    def plan(refs, send_sems, recv_sems):
        x, y, c = _mesh_pos()
        sends = [_remote(refs[t].at[c], refs[t].at[c], send_sems, recv_sems, t, (x, y, 1 - c)) for t in range(n)]
        recvs = [_remote(refs[t].at[1 - c], refs[t].at[1 - c], send_sems, recv_sems, t, (x, y, 1 - c)) for t in range(n)]
        return sends, recvs

    return plan


def _join_halves(halves, name, after=None):
    return _exchange(name, halves, _join_plan(len(halves)), len(halves), after=after)


def _add_other_half(g4, got, where, name):
    _, pieces, sr, sc = g4.shape
    tr = _tile(sr, 256, 2 * SUBLANES)

    def body(w_ref, a_ref, b_ref, o_ref):
        o_ref[...] = a_ref[...] + b_ref[...]

    blk = pl.BlockSpec((None, tr, sc), lambda p, i, w_ref: (p, i, 0))
    return pl.pallas_call(
        body, name=name, out_shape=jax.ShapeDtypeStruct((pieces, sr, sc), F32),
        grid_spec=pltpu.PrefetchScalarGridSpec(
            num_scalar_prefetch=1, grid=(pieces, sr // tr),
            in_specs=[pl.BlockSpec((None, None, tr, sc), lambda p, i, w_ref: (w_ref[0], p, i, 0)), blk], out_specs=blk),
        compiler_params=_params(("parallel", "parallel")),
    )(where, g4, got)


def _add_pieces(sums, got_pieces, where, name):
    _, sr, sc = sums.shape
    tr = _tile(sr, 256, 2 * SUBLANES)

    def body(w_ref, a_ref, r_ref, o_ref):
        acc = a_ref[...]
        for j in range(3):
            acc = acc + r_ref[j].astype(F32)
        o_ref[...] = acc

    return pl.pallas_call(
        body, name=name, out_shape=jax.ShapeDtypeStruct((N_CORES, sr, sc), F32),
        grid_spec=pltpu.PrefetchScalarGridSpec(
            num_scalar_prefetch=1, grid=(sr // tr,),
            in_specs=[pl.BlockSpec((None, tr, sc), lambda i, w_ref: (w_ref[1], i, 0)),
                      pl.BlockSpec((3, tr, sc), lambda i, w_ref: (0, i, 0))],
            out_specs=pl.BlockSpec((None, tr, sc), lambda i, w_ref: (w_ref[0], i, 0))),
        compiler_params=_params(("parallel",)),
    )(where, sums, got_pieces)


def _adamw_update(w, g, m, v):
    nm = ADAM_B1 * m + (1.0 - ADAM_B1) * g
    nv = ADAM_B2 * v + (1.0 - ADAM_B2) * (g * g)
    m_hat = nm / (1.0 - ADAM_B1 ** ADAM_STEP)
    v_hat = nv / (1.0 - ADAM_B2 ** ADAM_STEP)
    return -ADAM_LR * (m_hat / (jnp.sqrt(v_hat) + ADAM_EPS) + ADAM_WD * w), nm, nv


def _adamw(w, g, m, v, name, after=None):
    rows, cols = w.shape
    halves = 2 if g.ndim == 3 else 1
    bc = cols // halves
    tr = _tile(rows, max(SUBLANES, (1 << 19) // max(bc, 1) // SUBLANES * SUBLANES), SUBLANES)

    def body(w_ref, g_ref, m_ref, v_ref, *rest):
        d_ref, nm_ref, nv_ref, go_ref = rest[-4:]
        gv = g_ref[...]
        d_ref[...], nm_ref[...], nv_ref[...] = _adamw_update(w_ref[...], gv, m_ref[...], v_ref[...])
        go_ref[...] = gv

    blk = pl.BlockSpec((tr, bc), lambda i, h: (i, h))
    g_blk = pl.BlockSpec((None, tr, bc), lambda i, h: (h, i, 0)) if halves == 2 else blk
    extra = [] if after is None else [after]
    return pl.pallas_call(
        body, name=name, grid=(rows // tr, halves),
        in_specs=[blk, g_blk, blk, blk] + [pl.BlockSpec(memory_space=pl.ANY)] * len(extra), out_specs=[blk] * 4,
        out_shape=[jax.ShapeDtypeStruct((rows, cols), F32)] * 4, compiler_params=_params(("parallel", "parallel")),
    )(w, g, m, v, *extra)


def _adamw_many(ws, gs, ms, vs, name):
    n = len(ws)

    def body(*refs):
        outs = refs[4 * n:]
        for k in range(n):
            w_ref, g_ref, m_ref, v_ref = (refs[j * n + k] for j in range(4))
            outs[k][...], outs[n + k][...], outs[2 * n + k][...] = _adamw_update(w_ref[...], g_ref[...], m_ref[...], v_ref[...])

    out = pl.pallas_call(
        body, name=name, out_shape=[jax.ShapeDtypeStruct(w.shape, F32) for w in ws] * 3,
        compiler_params=pltpu.CompilerParams(vmem_limit_bytes=VMEM_LIMIT_BYTES),
    )(*ws, *gs, *ms, *vs)
    return out[:n], out[n:2 * n], out[2 * n:]


WEIGHTS = ['attn_norm_w', 'w_in', 'ssm_lambda_re', 'ssm_lambda_im', 'ssm_log_dt', 'ssm_b_re', 'ssm_b_im', 'ssm_c_re',
           'ssm_c_im', 'ssm_d', 'ssm_w_glu', 'ssm_b_glu', 'mla_q_norm_w', 'mla_w_uq', 'mla_kv_norm_w', 'mla_w_ukv',
           'ssm_out_norm_w', 'mla_out_norm_w', 'w_out', 'ffn_norm_w', 'ffn_w_up', 'ffn_conv_w', 'ffn_conv_b',
           'ffn_w_down', 'final_norm_w']
SHARDED = {'w_in': False, 'ssm_w_glu': True, 'mla_w_uq': False, 'mla_w_ukv': False, 'w_out': True, 'ffn_w_up': False,
           'ffn_w_down': True}
SMALL = [n for n in WEIGHTS if n not in SHARDED and n != 'ffn_conv_w']
ROPE_PAD = HEAD_SLOT - QK_NOPE_DIM - QK_ROPE_DIM
SMALL_COLS = 8 * LANES


def _pad_heads(w_uq, heads):
    qr = w_uq.shape[0]
    w3 = w_uq.reshape(qr, heads, QK_NOPE_DIM + QK_ROPE_DIM)
    return jnp.concatenate([w3, jnp.zeros((qr, heads, ROPE_PAD), w_uq.dtype)], axis=2).reshape(qr, heads * HEAD_SLOT)


def _unpad_heads(g_uq, heads):
    qr = g_uq.shape[0]
    return g_uq.reshape(qr, heads, HEAD_SLOT)[:, :, :QK_NOPE_DIM + QK_ROPE_DIM].reshape(qr, -1)


FFN = ['ffn_w_up', 'ffn_w_down']
MIXER_LATER = ['ssm_w_glu', 'w_out']
FFN_GATHER = FFN + ['ffn_conv_w']


class _Overlapped:
    def __init__(self, placed_first, first_sharding, where):
        self.where, self.mine, self.other = where, where[:1], 1 - where[:1]
        self.first_ici, self.first_d2d = _gather_plans([(r, False) for r in first_sharding])
        self.first = _start_copies("gather_first_start", placed_first, self.first_ici, 3 * len(placed_first), where)
        self.first_started = self.first[3]

    def start_rest(self, placed_later, placed):
        self.later_ici, self.later_d2d = _gather_plans([(SHARDED[n], False) for n in MIXER_LATER])
        self.later = _start_copies("gather_later_start", placed_later, self.later_ici, 3 * len(placed_later),
                                   self.first_started)
        up, down, taps = placed
        self.up_ici, self.up_d2d = _gather_plans([(SHARDED["ffn_w_up"], False), (False, True)])
        self.up = _start_copies("gather_ffn_up_start", [up, taps], self.up_ici, 6, self.later[3])
        self.down_ici, self.down_d2d = _gather_plans([(SHARDED["ffn_w_down"], False)])
        self.down = _start_copies("gather_ffn_down_start", [down], self.down_ici, 3, self.up[3])
        self.gather_started = self.down[3]
        arrived = _wait_copies("gather_first_wait", self.first, self.first_ici, self.gather_started)
        return _exchange("gather_first_pass", arrived, self.first_d2d, 3 * len(arrived))

    def mixer_weights(self, after):
        arrived = _wait_copies("gather_later_wait", self.later, self.later_ici, after)
        return dict(zip(MIXER_LATER, _exchange("gather_later_pass", arrived, self.later_d2d, 3 * len(arrived))))

    def ffn_weights_arrived(self, after):
        up, self.taps = _wait_copies("gather_ffn_up_wait", self.up, self.up_ici, after)
        self.up_passing = _start_copies("gather_ffn_up_pass_start", [up], self.up_d2d, 3, after)
        return self.up_passing[3]

    def ffn_weights(self, after):
        w_up, = _wait_copies("gather_ffn_up_pass_wait", self.up_passing, self.up_d2d, after)
        return {"ffn_w_up": w_up, "ffn_conv_w": self.taps}

    def ffn_down_arrived(self, after):
        down, = _wait_copies("gather_ffn_down_wait", self.down, self.down_ici, after)
        self.down_passing = _start_copies("gather_ffn_down_pass_start", [down], self.down_d2d, 3, after)
        return self.down_passing[3]

    def ffn_down_weight(self, after):
        return _wait_copies("gather_ffn_down_pass_wait", self.down_passing, self.down_d2d, after)[0]

    def ffn_grads(self, makers, after):
        self.makers = [makers[name] for name in FFN]
        n = len(FFN)
        give = [make(self.other, suffix="_give") for make in self.makers]
        lands = [lax.empty(g.shape, g.dtype) for g in give]
        self.swap = _start_copies("grad_ffn_swap_start", give + lands, _give_plan(n), n, after)
        return self.swap[3]

    def ffn_backward_done(self, after):
        n = len(FFN)
        got = _wait_copies("grad_ffn_swap_wait", self.swap, _give_plan(n), after)[n:]
        kept = [make(self.mine, suffix="_keep", add=got[t], wire=True) for t, make in enumerate(self.makers)]
        self.sums = [k[0] for k in kept]
        wires = [k[1] for k in kept]
        lands = [lax.empty(s.shape, s.dtype) for s in _scatter_shapes(wires)]
        self.scatter = _start_copies("grad_ffn_scatter_start", wires + lands, _scatter_plan(n), 3 * n, after)
        return self.scatter[3]

    def ffn_reduced(self, after):
        n = len(FFN)
        got_pieces = _wait_copies("grad_ffn_scatter_wait", self.scatter, _scatter_plan(n), after)[n:]
        return [_add_pieces(self.sums[t], got_pieces[t], self.where, "grad_add_pieces_" + name) for t, name in enumerate(FFN)]


def _step(args):
    x, positions, target = args["x"][0], args["positions"], args["loss_target"][0]
    rows = x.shape[0]
    p = {n: args[n] for n in WEIGHTS}
    xi, yi, ci = _mesh_pos()
    piece = 2 * xi + yi

    def transposed(a):
        return jnp.swapaxes(a[0], 0, 1)

    def as_stored(n, a):
        return jnp.swapaxes(a, 2, 3) if n in ("ssm_b_re", "ssm_b_im") else a

    w_in = transposed(p["w_in"])
    in_width = w_in.shape[0]
    in_pad = (-in_width) % (2 * LANES)
    heads_here = p["mla_w_uq"].shape[2] // (QK_NOPE_DIM + QK_ROPE_DIM)
    shards = {
        "w_in": jnp.pad(w_in, ((0, in_pad), (0, 0))),
        "ssm_w_glu": p["ssm_w_glu"][0],
        "mla_w_uq": _pad_heads(p["mla_w_uq"][0], heads_here),
        "mla_w_ukv": p["mla_w_ukv"][0],
        "w_out": p["w_out"][0],
        "ffn_w_up": p["ffn_w_up"][0],
        "ffn_w_down": p["ffn_w_down"][0],
    }
    conv_w = jnp.pad(p["ffn_conv_w"][0], ((0, SUBLANES - p["ffn_conv_w"].shape[1]), (0, 0)))
    order = list(SHARDED)
    piece_idx = piece.reshape(1).astype(jnp.int32)
    mixer = [n for n in order if n not in FFN]
    first = [n for n in mixer if n not in MIXER_LATER]
    where = jnp.stack([ci, piece]).astype(jnp.int32)
    placed = {n: _place_shard(shards[n], piece_idx, SHARDED[n], "place_" + n) for n in first}
    hooks = _Overlapped([placed[n] for n in first], [SHARDED[n] for n in first], where)
    for n in order:
        if n not in first:
            placed[n] = _place_shard(shards[n], piece_idx, SHARDED[n], "place_" + n, after=hooks.first_started)
    placed["ffn_conv_w"] = _place_shard(conv_w, piece_idx, False, "place_ffn_conv_w", out_dtype=F32,
                                        after=hooks.first_started)
    w = dict(zip(first, hooks.start_rest([placed[n] for n in MIXER_LATER], [placed[n] for n in FFN_GATHER])))
    groups = p["ssm_lambda_re"].shape[1]
    w.update({
        "attn_norm_w": p["attn_norm_w"] + hooks.gather_started[:1, :1],
        "ssm_lambda_re": p["ssm_lambda_re"][0], "ssm_lambda_im": p["ssm_lambda_im"][0],
        "ssm_log_dt": p["ssm_log_dt"].reshape(groups, 1), "ssm_b_re": as_stored("ssm_b_re", p["ssm_b_re"])[0],
        "ssm_b_im": as_stored("ssm_b_im", p["ssm_b_im"])[0], "ssm_c_re": p["ssm_c_re"][0], "ssm_c_im": p["ssm_c_im"][0],
        "ssm_d": p["ssm_d"], "ssm_b_glu": p["ssm_b_glu"], "mla_q_norm_w": p["mla_q_norm_w"],
        "mla_kv_norm_w": p["mla_kv_norm_w"], "ssm_out_norm_w": p["ssm_out_norm_w"], "mla_out_norm_w": p["mla_out_norm_w"],
        "ffn_norm_w": p["ffn_norm_w"], "ffn_conv_b": p["ffn_conv_b"], "final_norm_w": p["final_norm_w"].reshape(1, -1),
    })

    loss_tile, dx, g = _local_step(x, positions.reshape(rows, 1).astype(F32), target, w, hooks)

    flat = [g[n].reshape(-1) for n in SMALL] + [g["ffn_conv_w"].reshape(-1), loss_tile[0, :1]]
    sizes = [f.shape[0] for f in flat]
    per_block = -(-sum(sizes) // (N_CORES * N_CHIPS * SMALL_COLS))
    small_rows = -(-per_block // (2 * SUBLANES)) * (2 * SUBLANES)
    padded = N_CORES * N_CHIPS * small_rows * SMALL_COLS

    def pack(parts):
        parts = list(parts)
        have = sum(q.shape[0] for q in parts)
        return jnp.concatenate(parts + [jnp.zeros((padded - have,), F32)])

    reduced = mixer + ["small"]
    small = pack(flat).reshape(N_CORES, N_CHIPS, small_rows, SMALL_COLS)
    give = [g[n](hooks.other, suffix="_give") for n in mixer] + [lax.dynamic_index_in_dim(small, 1 - ci, 0, keepdims=False)]
    lands = [lax.empty(a.shape, a.dtype) for a in give]
    give_plan = _give_plan(len(reduced))
    swap = _start_copies("grad_mixer_swap_start", give + lands, give_plan, len(reduced), dx)

    grads, delta, new_m, new_v = {}, {}, {}, {}

    def finish(n, joined, after=None):
        grad = joined if SHARDED[n] else joined.reshape(-1, joined.shape[2])
        if n == "w_in":
            wt, mt, vt = w_in, transposed(args["m_w_in"]), transposed(args["v_w_in"])
            out = _adamw(wt, grad, mt, vt, "adamw_w_in")
            delta[n], new_m[n], new_v[n], grads[n] = (jnp.swapaxes(a, 0, 1)[None] for a in out)
            return
        if n == "mla_w_uq":
            grad = _unpad_heads(grad, heads_here)
        adam(n, grad, after)

    def adam(n, grad, after=None):
        shape = p[n].shape
        out = _adamw(p[n].reshape(shape[1:]), grad, args["m_" + n].reshape(shape[1:]),
                     args["v_" + n].reshape(shape[1:]), "adamw_" + n, after)
        delta[n], new_m[n], new_v[n], grads[n] = (a.reshape(shape) for a in out)

    ffn_halves = hooks.ffn_reduced(swap[3])
    got = _wait_copies("grad_mixer_swap_wait", swap, give_plan, ffn_halves[-1])[len(reduced):]
    join_plan = _join_plan(len(FFN))
    ffn_join = _start_copies("grad_ffn_join_start", ffn_halves, join_plan, len(FFN), got[0])
    kept = [g[n](hooks.mine, suffix="_keep", add=got[t], wire=True) for t, n in enumerate(mixer)]
    small_sum = _add_other_half(small, got[-1], where, "grad_add_half_small")
    sums = [k[0] for k in kept] + [small_sum]
    wires = [k[1] for k in kept] + [small_sum]
    ffn_joined = _wait_copies("grad_ffn_join_wait", ffn_join, join_plan, kept[-1][0])
    lands = [lax.empty(s.shape, s.dtype) for s in _scatter_shapes(wires)]
    scatter_plan = _scatter_plan(len(reduced))
    scatter = _start_copies("grad_mixer_scatter_start", wires + lands, scatter_plan, 3 * len(reduced), ffn_joined[0])
    behind = scatter[3]
    for n, joined in zip(FFN, ffn_joined):
        finish(n, joined, after=behind)
        behind = delta[n]
    got_pieces = _wait_copies("grad_mixer_scatter_wait", scatter, scatter_plan, delta[FFN[-1]])[len(reduced):]
    halves = [_add_pieces(sums[t], got_pieces[t], where, "grad_add_pieces_" + n) for t, n in enumerate(reduced)]
    joined = _join_halves(halves, "grad_join_halves")
    for n, j in zip(mixer, joined):
        finish(n, j)
    eighths = _place_shard(joined[-1].reshape(N_CORES * small_rows, SMALL_COLS), piece_idx, True, "place_small_grads",
                           out_dtype=F32)
    small_sum = _gather_weights([(eighths, True, False)], "gather_small_grads")[0]
    flat_sum = small_sum.reshape(N_CHIPS, N_CORES, small_rows * SMALL_COLS).transpose(1, 0, 2).reshape(-1)
    offs = [0]
    for s in sizes:
        offs.append(offs[-1] + s)
    stored = {n: as_stored(n, p[n]) for n in SMALL}
    for k, n in enumerate(SMALL):
        grads[n] = flat_sum[offs[k]:offs[k + 1]].reshape(stored[n].shape)
    taps, cols_here = p["ffn_conv_w"].shape[1], p["ffn_conv_w"].shape[2]
    conv_full = flat_sum[offs[len(SMALL)]:offs[len(SMALL) + 1]].reshape(taps, N_CHIPS * cols_here)
    adam("ffn_conv_w", lax.dynamic_slice_in_dim(conv_full, piece * cols_here, cols_here, axis=1))
    loss = flat_sum[offs[len(SMALL) + 1]]

    def rank2(a):
        return a.reshape(1, -1) if a.ndim == 1 else a

    d_s, m_s, v_s = _adamw_many([rank2(stored[n]) for n in SMALL], [rank2(grads[n]) for n in SMALL],
                                [rank2(as_stored(n, args["m_" + n])) for n in SMALL],
                                [rank2(as_stored(n, args["v_" + n])) for n in SMALL], "adamw_small")
    for k, n in enumerate(SMALL):
        delta[n], new_m[n], new_v[n], grads[n] = (as_stored(n, a.reshape(stored[n].shape))
                                                  for a in (d_s[k], m_s[k], v_s[k], grads[n]))

    return (loss, dx[None], *[grads[n] for n in WEIGHTS], *[delta[n] for n in WEIGHTS],
            *[new_m[n] for n in WEIGHTS], *[new_v[n] for n in WEIGHTS])


def kernel(x, positions, attn_norm_w, w_in, ssm_lambda_re, ssm_lambda_im, ssm_log_dt, ssm_b_re, ssm_b_im, ssm_c_re, ssm_c_im, ssm_d, ssm_w_glu, ssm_b_glu, mla_q_norm_w, mla_w_uq, mla_kv_norm_w, mla_w_ukv, ssm_out_norm_w, mla_out_norm_w, w_out, ffn_norm_w, ffn_w_up, ffn_conv_w, ffn_conv_b, ffn_w_down, final_norm_w, loss_target, m_attn_norm_w, m_w_in, m_ssm_lambda_re, m_ssm_lambda_im, m_ssm_log_dt, m_ssm_b_re, m_ssm_b_im, m_ssm_c_re, m_ssm_c_im, m_ssm_d, m_ssm_w_glu, m_ssm_b_glu, m_mla_q_norm_w, m_mla_w_uq, m_mla_kv_norm_w, m_mla_w_ukv, m_ssm_out_norm_w, m_mla_out_norm_w, m_w_out, m_ffn_norm_w, m_ffn_w_up, m_ffn_conv_w, m_ffn_conv_b, m_ffn_w_down, m_final_norm_w, v_attn_norm_w, v_w_in, v_ssm_lambda_re, v_ssm_lambda_im, v_ssm_log_dt, v_ssm_b_re, v_ssm_b_im, v_ssm_c_re, v_ssm_c_im, v_ssm_d, v_ssm_w_glu, v_ssm_b_glu, v_mla_q_norm_w, v_mla_w_uq, v_mla_kv_norm_w, v_mla_w_ukv, v_ssm_out_norm_w, v_mla_out_norm_w, v_w_out, v_ffn_norm_w, v_ffn_w_up, v_ffn_conv_w, v_ffn_conv_b, v_ffn_w_down, v_final_norm_w):
    return _step(dict(locals()))
```

```python
import functools
import math

import jax
import jax.numpy as jnp
from jax import lax
from jax.experimental import pallas as pl
from jax.experimental.pallas import tpu as pltpu

F32 = jnp.float32
BF16 = jnp.bfloat16

SSM_GROUP = 16
SSM_STATE = 64
QK_NOPE_DIM = 128
QK_ROPE_DIM = 64
ROPE_THETA = 10000.0
RMS_EPS = 1e-6
ADAM_LR, ADAM_B1, ADAM_B2, ADAM_EPS, ADAM_WD, ADAM_STEP = 0.001, 0.9, 0.999, 1e-08, 0.01, 10

LANES = 128
SUBLANES = 8
VMEM_LIMIT_BYTES = 56 * 1024 * 1024

GROUPS_PER_BATCH = LANES // SSM_GROUP
STATE_PER_BATCH = GROUPS_PER_BATCH * SSM_STATE
HEAD_SLOT = 2 * LANES
NEG_INF = -1e30
ATTN_BLOCK = 512
FFN_ROWS = 1024

N_CHIPS = 4
N_CORES = 2


def _tile(n, pref, align=LANES):
    if n <= pref:
        return n
    t = (pref // align) * align
    while t >= align:
        if n % t == 0:
            return t
        t -= align
    return n


def _params(sem):
    return pltpu.CompilerParams(dimension_semantics=sem, vmem_limit_bytes=VMEM_LIMIT_BYTES)


def _dot(a, b, dims):
    return lax.dot_general(a, b, (dims, ((), ())), preferred_element_type=F32)


def _dot_nn(a, b):
    return _dot(a, b, ((1,), (0,)))


def _dot_nt(a, b):
    return _dot(a, b, ((1,), (1,)))


def _dot_tn(a, b):
    return _dot(a, b, ((0,), (0,)))


def _matmul(a, b, *, mode, name, tm=512, tn=1024, tk=2048, bias=None, add=None, out_dtype=F32,
            a_split=False, b_split=False, after=None):
    if a_split:
        assert mode == "nt"
        a_shape = (a.shape[1], 2 * a.shape[2])
    else:
        a_shape = a.shape
    if b_split:
        assert mode == "tn"
        b_shape = (b.shape[1], 2 * b.shape[2])
    else:
        b_shape = b.shape
    if mode == "nn":
        (m, k), (k2, n) = a_shape, b_shape
    elif mode == "nt":
        (m, k), (n, k2) = a_shape, b_shape
    else:
        (k, m), (k2, n) = a_shape, b_shape
    assert k == k2, (a.shape, b.shape, mode)
    tm, tn, tk = _tile(m, tm, SUBLANES), _tile(n, tn), _tile(k, tk)
    nk = k // tk
    a_spec = {"nn": pl.BlockSpec((tm, tk), lambda i, j, kk: (i, kk)),
              "nt": pl.BlockSpec((tm, tk), lambda i, j, kk: (i, kk)),
              "tn": pl.BlockSpec((tk, tm), lambda i, j, kk: (kk, i))}[mode]
    b_spec = {"nn": pl.BlockSpec((tk, tn), lambda i, j, kk: (kk, j)),
              "nt": pl.BlockSpec((tn, tk), lambda i, j, kk: (j, kk)),
              "tn": pl.BlockSpec((tk, tn), lambda i, j, kk: (kk, j))}[mode]
    if a_split:
        kb = a.shape[2] // tk
        assert a.shape[2] % tk == 0
        a_spec = pl.BlockSpec((None, tm, tk), lambda i, j, kk: (kk // kb, i, kk % kb))
    if b_split:
        nb = b.shape[2] // tn
        assert b.shape[2] % tn == 0
        b_spec = pl.BlockSpec((None, tk, tn), lambda i, j, kk: (j // nb, kk, j % nb))
    dot = {"nn": _dot_nn, "nt": _dot_nt, "tn": _dot_tn}[mode]
    in_specs, operands = [a_spec, b_spec], [a, b]
    if bias is not None:
        in_specs.append(pl.BlockSpec((1, tn), lambda i, j, kk: (0, j)))
        operands.append(bias)
    if add is not None:
        in_specs.append(pl.BlockSpec((tm, tn), lambda i, j, kk: (i, j)))
        operands.append(add)
    if after is not None:
        in_specs.append(pl.BlockSpec(memory_space=pl.ANY))
        operands.append(after)

    def body(*refs):
        a_ref, b_ref = refs[0], refs[1]
        rest = list(refs[2:])
        bias_ref = rest.pop(0) if bias is not None else None
        add_ref = rest.pop(0) if add is not None else None
        if after is not None:
            rest.pop(0)
        o_ref, acc_ref = rest

        def finish(acc):
            if bias_ref is not None:
                acc = acc + bias_ref[...]
            if add_ref is not None:
                acc = acc + add_ref[...]
            o_ref[...] = acc.astype(o_ref.dtype)

        part = dot(a_ref[...].astype(BF16), b_ref[...].astype(BF16))
        if nk == 1:
            finish(part)
        else:
            kk = pl.program_id(2)

            @pl.when(kk == 0)
            def _():
                acc_ref[...] = part

            @pl.when(jnp.logical_and(kk > 0, kk < nk - 1))
            def _():
                acc_ref[...] += part

            @pl.when(kk == nk - 1)
            def _():
                finish(acc_ref[...] + part)

    out_shape = jax.ShapeDtypeStruct((m, n), out_dtype)
    out_spec = pl.BlockSpec((tm, tn), lambda i, j, kk: (i, j))
    acc_shape = (tm, tn) if nk > 1 else (SUBLANES, LANES)
    return pl.pallas_call(
        body, name=name, grid=(m // tm, n // tn, nk), in_specs=in_specs, out_specs=out_spec, out_shape=out_shape,
        scratch_shapes=[pltpu.VMEM(acc_shape, F32)],
        compiler_params=_params(("parallel", "parallel", "arbitrary")),
    )(*operands)


def _wgrad_half(a, b, rows, cols, row_sharded, name, which, *, suffix="", add=None, wire=False, tm=None, tn=None,
                b_split=False, after=None):
    tokens = a.shape[0]
    if row_sharded:
        sr, sc = rows // N_CHIPS, cols // N_CORES
    else:
        sr, sc = rows // N_CORES, cols // N_CHIPS
    tm = _tile(sr, 512) if tm is None else tm
    tn = _tile(sc, 1024) if tn is None else tn
    assert sr % tm == 0 and sc % tn == 0, (rows, cols, tm, tn)
    rb, cb = sr // tm, sc // tn
    if tn >= tm:
        ij, grid = (lambda s, t: (t, s)), (N_CHIPS, cb, rb)
    else:
        ij, grid = (lambda s, t: (s, t)), (N_CHIPS, rb, cb)
    if row_sharded:
        a_tile = lambda p, i, j, h: p * rb + i
        b_tile = lambda p, i, j, h: h[0] * cb + j
    else:
        a_tile = lambda p, i, j, h: h[0] * rb + i
        b_tile = lambda p, i, j, h: p * cb + j
    a_spec = pl.BlockSpec((tokens, tm), lambda p, s, t, h: (0, a_tile(p, *ij(s, t), h)))
    if b_split:
        nbh = b.shape[2] // tn
        assert b.shape[2] % tn == 0
        b_spec = pl.BlockSpec((None, tokens, tn), lambda p, s, t, h: (b_tile(p, *ij(s, t), h) // nbh, 0,
                                                                       b_tile(p, *ij(s, t), h) % nbh))
    else:
        b_spec = pl.BlockSpec((tokens, tn), lambda p, s, t, h: (0, b_tile(p, *ij(s, t), h)))
    out_spec = pl.BlockSpec((None, tm, tn), lambda p, s, t, h: (p, *ij(s, t)))
    in_specs, operands = [a_spec, b_spec], [a, b]
    if add is not None:
        in_specs.append(out_spec)
        operands.append(add)
    if after is not None:
        in_specs.append(pl.BlockSpec(memory_space=pl.ANY))
        operands.append(after)
    out_dtypes = [F32, BF16] if wire else [F32]

    def body(h_ref, a_ref, b_ref, *rest):
        acc = _dot_tn(a_ref[...].astype(BF16), b_ref[...].astype(BF16))
        if add is not None:
            acc = acc + rest[0][...]
        for o_ref in rest[-len(out_dtypes):]:
            o_ref[...] = acc.astype(o_ref.dtype)

    out = pl.pallas_call(
        body, name=name + suffix, out_shape=[jax.ShapeDtypeStruct((N_CHIPS, sr, sc), dt) for dt in out_dtypes],
        grid_spec=pltpu.PrefetchScalarGridSpec(num_scalar_prefetch=1, grid=grid, in_specs=in_specs,
                                               out_specs=[out_spec] * len(out_dtypes)),
        compiler_params=_params(("parallel", "parallel", "parallel")),
    )(which, *operands)
    return tuple(out) if wire else out[0]


def _rms_rows(x):
    return lax.rsqrt(jnp.mean(x * x, axis=-1, keepdims=True) + RMS_EPS)


def _rmsnorm_fwd(x, w, *, name, width=None, col=0, out_dtype=BF16, tr=256):
    rows = x.shape[0]
    width = x.shape[1] if width is None else width
    tr = _tile(rows, tr, SUBLANES)

    def body(x_ref, w_ref, o_ref):
        xv = x_ref[...]
        o_ref[...] = (xv * _rms_rows(xv) * w_ref[...]).astype(o_ref.dtype)

    return pl.pallas_call(
        body, name=name, grid=(rows // tr,),
        in_specs=[pl.BlockSpec((tr, width), lambda i: (i, col)), pl.BlockSpec((1, width), lambda i: (0, 0))],
        out_specs=pl.BlockSpec((tr, width), lambda i: (i, 0)),
        out_shape=jax.ShapeDtypeStruct((rows, width), out_dtype),
        compiler_params=_params(("parallel",)),
    )(x, w)


def _rmsnorm_bwd_rows(xv, w, dy):
    r = _rms_rows(xv)
    n = xv * r
    dn = dy * w
    dx = r * (dn - n * jnp.mean(dn * n, axis=-1, keepdims=True))
    return dx, dy * n


def _rmsnorm_bwd(x, w, dy, *, name, width=None, col=0, dy_col=0, add=None, tr=256, dx_dtypes=(F32,)):
    rows = x.shape[0]
    n_dx = len(dx_dtypes)
    width = x.shape[1] if width is None else width
    tr = _tile(rows, tr, SUBLANES)
    in_specs = [pl.BlockSpec((tr, width), lambda i: (i, col)), pl.BlockSpec((1, width), lambda i: (0, 0)),
                pl.BlockSpec((tr, width), lambda i: (i, dy_col))]
    operands = [x, w, dy]
    if add is not None:
        in_specs.append(pl.BlockSpec((tr, width), lambda i: (i, 0)))
        operands.append(add)

    def body(*refs):
        x_ref, w_ref, dy_ref = refs[:3]
        add_ref = refs[3] if add is not None else None
        dx_refs, dw_ref = refs[-1 - n_dx:-1], refs[-1]
        dx, dwp = _rmsnorm_bwd_rows(x_ref[...], w_ref[...], dy_ref[...])
        if add_ref is not None:
            dx = dx + add_ref[...]
        for dx_ref in dx_refs:
            dx_ref[...] = dx.astype(dx_ref.dtype)
        part = jnp.sum(dwp, axis=0, keepdims=True)

        @pl.when(pl.program_id(0) == 0)
        def _():
            dw_ref[...] = part

        @pl.when(pl.program_id(0) > 0)
        def _():
            dw_ref[...] += part

    return pl.pallas_call(
        body, name=name, grid=(rows // tr,), in_specs=in_specs,
        out_specs=[pl.BlockSpec((tr, width), lambda i: (i, 0))] * n_dx + [pl.BlockSpec((1, width), lambda i: (0, 0))],
        out_shape=[jax.ShapeDtypeStruct((rows, width), dt) for dt in dx_dtypes] + [jax.ShapeDtypeStruct((1, width), F32)],
        compiler_params=_params(("arbitrary",)),
    )(*operands)


def _final_norm_loss(h, w, target, *, tr=256):
    rows, d = h.shape
    tr = _tile(rows, tr, SUBLANES)

    def body(h_ref, w_ref, t_ref, loss_ref, dh_ref, dhb_ref, dw_ref):
        hv, wv = h_ref[...], w_ref[...]
        r = _rms_rows(hv)
        n = hv * r
        err = n * wv - t_ref[...]
        d_out = err * (1.0 / d)
        dn = d_out * wv
        dh = r * (dn - n * jnp.mean(dn * n, axis=-1, keepdims=True))
        dh_ref[...] = dh
        dhb_ref[...] = dh.astype(BF16)
        dw_part = jnp.sum(d_out * n, axis=0, keepdims=True)
        loss_part = jnp.full((SUBLANES, LANES), 0.5 / d, F32) * jnp.sum(err * err)

        @pl.when(pl.program_id(0) == 0)
        def _():
            dw_ref[...] = dw_part
            loss_ref[...] = loss_part

        @pl.when(pl.program_id(0) > 0)
        def _():
            dw_ref[...] += dw_part
            loss_ref[...] += loss_part

    return pl.pallas_call(
        body, name="final_norm_loss", grid=(rows // tr,),
        in_specs=[pl.BlockSpec((tr, d), lambda i: (i, 0)), pl.BlockSpec((1, d), lambda i: (0, 0)),
                  pl.BlockSpec((tr, d), lambda i: (i, 0))],
        out_specs=[pl.BlockSpec((SUBLANES, LANES), lambda i: (0, 0)), pl.BlockSpec((tr, d), lambda i: (i, 0)),
                   pl.BlockSpec((tr, d), lambda i: (i, 0)), pl.BlockSpec((1, d), lambda i: (0, 0))],
        out_shape=[jax.ShapeDtypeStruct((SUBLANES, LANES), F32), jax.ShapeDtypeStruct((rows, d), F32),
                   jax.ShapeDtypeStruct((rows, d), BF16), jax.ShapeDtypeStruct((1, d), F32)],
        compiler_params=_params(("arbitrary",)),
    )(h, w, target)


def _cmul(ar, ai, br, bi):
    return ar * br - ai * bi, ar * bi + ai * br


def _dot_exact(a, b, dims):
    return lax.dot_general(a, b, (dims, ((), ())), preferred_element_type=F32, precision=lax.Precision.HIGHEST)


def _s5_discretize(lr, li, dt):
    mag = jnp.exp(lr * dt)
    th = li * dt
    ar, ai = mag * jnp.cos(th), mag * jnp.sin(th)
    nr, ni = ar - 1.0, ai
    den = lr * lr + li * li
    zr = (nr * lr + ni * li) / den
    zi = (ni * lr - nr * li) / den
    return mag, ar, ai, nr, ni, den, zr, zi


def _band_slices(group):
    j, gi = divmod(group, GROUPS_PER_BATCH)
    rows = slice(gi * SSM_GROUP, (gi + 1) * SSM_GROUP)
    re = slice(gi * SSM_STATE, (gi + 1) * SSM_STATE)
    im = slice(STATE_PER_BATCH + gi * SSM_STATE, STATE_PER_BATCH + (gi + 1) * SSM_STATE)
    return j, rows, re, im


def _s5_bands(lam_re, lam_im, log_dt, b_re, b_im, c_re, c_im):
    g, _ = lam_re.shape
    nb = g // GROUPS_PER_BATCH
    s2 = 2 * STATE_PER_BATCH

    def body(lr_ref, li_ref, ldt_ref, br_ref, bi_ref, cr_ref, ci_ref, wb_ref, wct_ref, a_ref):
        dt = jnp.exp(ldt_ref[...])
        _, ar, ai, _, _, _, zr, zi = _s5_discretize(lr_ref[...], li_ref[...], dt)
        wb_ref[...] = jnp.zeros_like(wb_ref)
        wct_ref[...] = jnp.zeros_like(wct_ref)
        for group in range(g):
            j, rows, re, im = _band_slices(group)
            zr_g, zi_g = zr[group:group + 1, :], zi[group:group + 1, :]
            bre, bim = br_ref[group], bi_ref[group]
            wb_ref[j, rows, re] = (zr_g * bre - zi_g * bim).astype(BF16)
            wb_ref[j, rows, im] = (zr_g * bim + zi_g * bre).astype(BF16)
            wct_ref[j, rows, re] = cr_ref[group].astype(BF16)
            wct_ref[j, rows, im] = (-ci_ref[group]).astype(BF16)
            a_ref[j, :, re] = ar[group:group + 1, :]
            a_ref[j, :, im] = ai[group:group + 1, :]

    return pl.pallas_call(
        body, name="s5_bands",
        out_shape=[jax.ShapeDtypeStruct((nb, LANES, s2), BF16)] * 2 + [jax.ShapeDtypeStruct((nb, 1, s2), F32)],
    )(lam_re, lam_im, log_dt, b_re, b_im, c_re, c_im)


def _s5_bands_bwd(lam_re, lam_im, log_dt, b_re, b_im, dwb, dwct, dabar):
    g, p = lam_re.shape
    gh = b_re.shape[1:]

    def body(lr_ref, li_ref, ldt_ref, br_ref, bi_ref, dwb_ref, dwct_ref, da_ref,
             dlr_ref, dli_ref, dldt_ref, dbre_ref, dbim_ref, dcre_ref, dcim_ref, dzr_ref, dzi_ref, dar_ref, dai_ref):
        lr, li = lr_ref[...], li_ref[...]
        dt = jnp.exp(ldt_ref[...])
        mag, ar, ai, nr, ni, den, zr, zi = _s5_discretize(lr, li, dt)
        for group in range(g):
            j, rows, re, im = _band_slices(group)
            zr_g, zi_g = zr[group:group + 1, :], zi[group:group + 1, :]
            bre, bim = br_ref[group], bi_ref[group]
            dbr, dbi = dwb_ref[j, rows, re], dwb_ref[j, rows, im]
            dbre_ref[group] = zr_g * dbr + zi_g * dbi
            dbim_ref[group] = zr_g * dbi - zi_g * dbr
            dzr_ref[group:group + 1, :] = jnp.sum(bre * dbr + bim * dbi, axis=0, keepdims=True)
            dzi_ref[group:group + 1, :] = jnp.sum(bre * dbi - bim * dbr, axis=0, keepdims=True)
            dcre_ref[group] = dwct_ref[j, rows, re]
            dcim_ref[group] = -dwct_ref[j, rows, im]
            dar_ref[group:group + 1, :] = da_ref[j, :, re]
            dai_ref[group:group + 1, :] = da_ref[j, :, im]
        dzr, dzi = dzr_ref[...], dzi_ref[...]
        inv = 1.0 / den
        d_nr = (dzr * lr - dzi * li) * inv
        d_ni = (dzr * li + dzi * lr) * inv
        d_den = -(dzr * zr + dzi * zi) * inv
        d_lr = (dzr * nr + dzi * ni) * inv + 2.0 * lr * d_den
        d_li = (dzr * ni - dzi * nr) * inv + 2.0 * li * d_den
        t_ar = dar_ref[...] + d_nr
        t_ai = dai_ref[...] + d_ni
        d_lrdt = t_ar * ar + t_ai * ai
        d_th = t_ai * ar - t_ar * ai
        dlr_ref[...] = d_lr + d_lrdt * dt
        dli_ref[...] = d_li + d_th * dt
        dldt_ref[...] = jnp.sum(d_lrdt * lr + d_th * li, axis=1, keepdims=True) * dt

    return pl.pallas_call(
        body, name="s5_bands_bwd",
        out_shape=[jax.ShapeDtypeStruct((g, p), F32)] * 2 + [jax.ShapeDtypeStruct((g, 1), F32)]
        + [jax.ShapeDtypeStruct((g,) + gh, F32)] * 4,
        scratch_shapes=[pltpu.VMEM((g, p), F32)] * 4,
    )(lam_re, lam_im, log_dt, b_re, b_im, dwb, dwct, dabar)


def _powers(ar, ai, count):
    out = [(ar, ai)]
    for _ in range(count - 1):
        out.append(_cmul(out[-1][0], out[-1][1], ar, ai))
    return out


def _scan_coefs(ar, ai, reverse):
    w = ar.shape[-1]
    pw = _powers(ar, ai, SUBLANES)
    row = lax.broadcasted_iota(jnp.int32, (SUBLANES, w), 0)
    steps = []
    d = 1
    while d < SUBLANES:
        keep = (row < SUBLANES - d) if reverse else (row >= d)
        pr, pi = pw[d - 1]
        steps.append((d, jnp.where(keep, pr, 0.0), jnp.where(keep, pi, 0.0)))
        d *= 2
    cr = jnp.zeros((SUBLANES, w), F32)
    ci = jnp.zeros((SUBLANES, w), F32)
    for t in range(SUBLANES):
        pr, pi = pw[SUBLANES - 1 - t] if reverse else pw[t]
        cr = jnp.where(row == t, pr, cr)
        ci = jnp.where(row == t, pi, ci)
    return steps, cr, ci


def _scan_tile(xr, xi, carry_r, carry_i, coefs, reverse):
    steps, cr, ci = coefs
    for d, mr, mi in steps:
        shift = SUBLANES - d if reverse else d
        sr, si = pltpu.roll(xr, shift, 0), pltpu.roll(xi, shift, 0)
        pr, pi = _cmul(mr, mi, sr, si)
        xr, xi = xr + pr, xi + pi
    pr, pi = _cmul(cr, ci, carry_r, carry_i)
    return xr + pr, xi + pi


def _gelu(x):
    c = math.sqrt(2.0 / math.pi)
    return 0.5 * x * (1.0 + jnp.tanh(c * (x + 0.044715 * x * x * x)))


def _gelu_grad(x):
    c = math.sqrt(2.0 / math.pi)
    t = jnp.tanh(c * (x + 0.044715 * x * x * x))
    return 0.5 * (1.0 + t) + 0.5 * x * (1.0 - t * t) * c * (1.0 + 3.0 * 0.044715 * x * x)


def _s5_fwd(proj, wb, wct, d_skip, abar):
    rows = proj.shape[0]
    nb = wb.shape[0]
    s2 = 2 * STATE_PER_BATCH
    st = STATE_PER_BATCH
    chunk = _tile(rows, 512, SUBLANES)

    def body(u_ref, wb_ref, wc_ref, d_ref, a_ref, s_ref, y_ref, yg_ref):
        for c0 in range(0, rows, chunk):
            s_ref[pl.ds(c0, chunk), :] = _dot_nn(u_ref[pl.ds(c0, chunk), :].astype(BF16), wb_ref[...])
        av = a_ref[...]
        coefs = _scan_coefs(av[:, :st], av[:, st:], reverse=False)

        def tile(b, carry):
            r0 = pl.multiple_of(b * SUBLANES, SUBLANES)
            xr, xi = _scan_tile(s_ref[pl.ds(r0, SUBLANES), :st], s_ref[pl.ds(r0, SUBLANES), st:], carry[0], carry[1],
                                coefs, False)
            s_ref[pl.ds(r0, SUBLANES), :st] = xr
            s_ref[pl.ds(r0, SUBLANES), st:] = xi
            return xr[SUBLANES - 1:, :], xi[SUBLANES - 1:, :]

        zero = jnp.zeros((1, st), F32)
        lax.fori_loop(0, rows // SUBLANES, tile, (zero, zero))
        for c0 in range(0, rows, chunk):
            y = _dot_nt(s_ref[pl.ds(c0, chunk), :].astype(BF16), wc_ref[...]) + d_ref[...] * u_ref[pl.ds(c0, chunk), :]
            y_ref[pl.ds(c0, chunk), :] = y
            yg_ref[pl.ds(c0, chunk), :] = _gelu(y).astype(BF16)

    return pl.pallas_call(
        body, name="s5_fwd", grid=(nb,),
        in_specs=[pl.BlockSpec((rows, LANES), lambda j: (0, j)), pl.BlockSpec((None, LANES, s2), lambda j: (j, 0, 0)),
                  pl.BlockSpec((None, LANES, s2), lambda j: (j, 0, 0)), pl.BlockSpec((1, LANES), lambda j: (0, j)),
                  pl.BlockSpec((None, 1, s2), lambda j: (j, 0, 0))],
        out_specs=[pl.BlockSpec((rows, s2), lambda j: (0, j)), pl.BlockSpec((rows, LANES), lambda j: (0, j)),
                   pl.BlockSpec((rows, LANES), lambda j: (0, j))],
        out_shape=[jax.ShapeDtypeStruct((rows, nb * s2), F32), jax.ShapeDtypeStruct((rows, nb * LANES), F32),
                   jax.ShapeDtypeStruct((rows, nb * LANES), BF16)],
        compiler_params=_params(("parallel",)),
    )(proj, wb, wct, d_skip, abar)


def _s5_bwd(proj, states, y_pre, dyg_a, dyg_b, wb, wct, d_skip, abar):
    rows = proj.shape[0]
    nb = wb.shape[0]
    s2 = 2 * STATE_PER_BATCH
    st = STATE_PER_BATCH
    chunk = _tile(rows, 512, SUBLANES)
    n_tiles = rows // SUBLANES

    def body(u_ref, s_ref, y_ref, ga_ref, gb_ref, wb_ref, wc_ref, d_ref, a_ref,
             du_ref, dwb_ref, dwc_ref, da_ref, dd_ref, ds_ref, dy_ref):
        dy_ref[...] = (ga_ref[...] + gb_ref[...]) * _gelu_grad(y_ref[...])
        dd_ref[...] = jnp.sum(dy_ref[...] * u_ref[...], axis=0, keepdims=True)
        for c0 in range(0, rows, chunk):
            ds_ref[pl.ds(c0, chunk), :] = _dot_nn(dy_ref[pl.ds(c0, chunk), :].astype(BF16), wc_ref[...])
        dwc_ref[...] = _dot_tn(dy_ref[...].astype(BF16), s_ref[...].astype(BF16))
        av = a_ref[...]
        coefs = _scan_coefs(av[:, :st], -av[:, st:], reverse=True)
        row = lax.broadcasted_iota(jnp.int32, (SUBLANES, st), 0)

        def tile(k, carry):
            cr, ci, acc_r, acc_i = carry
            b = n_tiles - 1 - k
            r0 = pl.multiple_of(b * SUBLANES, SUBLANES)
            rp = pl.multiple_of(jnp.maximum(b - 1, 0) * SUBLANES, SUBLANES)
            xr, xi = _scan_tile(ds_ref[pl.ds(r0, SUBLANES), :st], ds_ref[pl.ds(r0, SUBLANES), st:], cr, ci, coefs, True)
            ds_ref[pl.ds(r0, SUBLANES), :st] = xr
            ds_ref[pl.ds(r0, SUBLANES), st:] = xi
            first = jnp.where(b > 0, 1.0, 0.0)
            pr = jnp.where(row == 0, pltpu.roll(s_ref[pl.ds(rp, SUBLANES), :st], 1, 0) * first,
                           pltpu.roll(s_ref[pl.ds(r0, SUBLANES), :st], 1, 0))
            pi = jnp.where(row == 0, pltpu.roll(s_ref[pl.ds(rp, SUBLANES), st:], 1, 0) * first,
                           pltpu.roll(s_ref[pl.ds(r0, SUBLANES), st:], 1, 0))
            acc_r = acc_r + pr * xr + pi * xi
            acc_i = acc_i + pr * xi - pi * xr
            return xr[:1, :], xi[:1, :], acc_r, acc_i

        zero = jnp.zeros((1, st), F32)
        zacc = jnp.zeros((SUBLANES, st), F32)
        _, _, acc_r, acc_i = lax.fori_loop(0, n_tiles, tile, (zero, zero, zacc, zacc))
        da_ref[:, :st] = jnp.sum(acc_r, axis=0, keepdims=True)
        da_ref[:, st:] = jnp.sum(acc_i, axis=0, keepdims=True)
        for c0 in range(0, rows, chunk):
            du_ref[pl.ds(c0, chunk), :] = (_dot_nt(ds_ref[pl.ds(c0, chunk), :].astype(BF16), wb_ref[...])
                                           + d_ref[...] * dy_ref[pl.ds(c0, chunk), :]).astype(du_ref.dtype)
        dwb_ref[...] = _dot_tn(u_ref[...].astype(BF16), ds_ref[...].astype(BF16))

    col = pl.BlockSpec((rows, LANES), lambda j: (0, j))
    return pl.pallas_call(
        body, name="s5_bwd", grid=(nb,),
        in_specs=[col, pl.BlockSpec((rows, s2), lambda j: (0, j)), col, col, col,
                  pl.BlockSpec((None, LANES, s2), lambda j: (j, 0, 0)), pl.BlockSpec((None, LANES, s2), lambda j: (j, 0, 0)),
                  pl.BlockSpec((1, LANES), lambda j: (0, j)), pl.BlockSpec((None, 1, s2), lambda j: (j, 0, 0))],
        out_specs=[col, pl.BlockSpec((None, LANES, s2), lambda j: (j, 0, 0)),
                   pl.BlockSpec((None, LANES, s2), lambda j: (j, 0, 0)), pl.BlockSpec((None, 1, s2), lambda j: (j, 0, 0)),
                   pl.BlockSpec((1, LANES), lambda j: (0, j))],
        out_shape=[jax.ShapeDtypeStruct((rows, nb * LANES), BF16), jax.ShapeDtypeStruct((nb, LANES, s2), F32),
                   jax.ShapeDtypeStruct((nb, LANES, s2), F32), jax.ShapeDtypeStruct((nb, 1, s2), F32),
                   jax.ShapeDtypeStruct((1, nb * LANES), F32)],
        scratch_shapes=[pltpu.VMEM((rows, s2), F32), pltpu.VMEM((rows, LANES), F32)],
        compiler_params=_params(("parallel",)),
    )(proj, states, y_pre, dyg_a, dyg_b, wb, wct, d_skip, abar)


def _glu_norm_fwd(y_pre, z, w, *, tr=256):
    rows, width = y_pre.shape
    tr = _tile(rows, tr, SUBLANES)

    def body(y_ref, z_ref, w_ref, o_ref):
        v = _gelu(y_ref[...]) * jax.nn.sigmoid(z_ref[...])
        o_ref[...] = (v * _rms_rows(v) * w_ref[...]).astype(o_ref.dtype)

    blk = pl.BlockSpec((tr, width), lambda i: (i, 0))
    return pl.pallas_call(
        body, name="glu_norm_fwd", grid=(rows // tr,),
        in_specs=[blk, blk, pl.BlockSpec((1, width), lambda i: (0, 0))], out_specs=blk,
        out_shape=jax.ShapeDtypeStruct((rows, width), BF16), compiler_params=_params(("parallel",)),
    )(y_pre, z, w)


def _glu_norm_bwd(y_pre, z, w, dycat, *, tr=256):
    rows, width = y_pre.shape
    tr = _tile(rows, tr, SUBLANES)

    def body(y_ref, z_ref, w_ref, dy_ref, dz_ref, dg_ref, dw_ref, db_ref):
        yg = _gelu(y_ref[...])
        sg = jax.nn.sigmoid(z_ref[...])
        dv, dwp = _rmsnorm_bwd_rows(yg * sg, w_ref[...], dy_ref[...])
        dz = dv * yg * sg * (1.0 - sg)
        dz_ref[...] = dz.astype(dz_ref.dtype)
        dg_ref[...] = dv * sg
        dw_part = jnp.sum(dwp, axis=0, keepdims=True)
        db_part = jnp.sum(dz, axis=0, keepdims=True)

        @pl.when(pl.program_id(0) == 0)
        def _():
            dw_ref[...] = dw_part
            db_ref[...] = db_part

        @pl.when(pl.program_id(0) > 0)
        def _():
            dw_ref[...] += dw_part
            db_ref[...] += db_part

    blk = pl.BlockSpec((tr, width), lambda i: (i, 0))
    vec = pl.BlockSpec((1, width), lambda i: (0, 0))
    return pl.pallas_call(
        body, name="glu_norm_bwd", grid=(rows // tr,), in_specs=[blk, blk, vec, blk], out_specs=[blk, blk, vec, vec],
        out_shape=[jax.ShapeDtypeStruct((rows, width), BF16), jax.ShapeDtypeStruct((rows, width), F32)]
        + [jax.ShapeDtypeStruct((1, width), F32)] * 2,
        compiler_params=_params(("arbitrary",)),
    )(y_pre, z, w, dycat)


def _rope_tables(pos, freq, sign):
    rows = pos.shape[0]

    def body(p_ref, f_ref, s_ref, cos_ref, sin_ref):
        ang = p_ref[...] * f_ref[...]
        cos_ref[...] = jnp.cos(ang)
        sin_ref[...] = jnp.sin(ang) * s_ref[...]

    return pl.pallas_call(body, name="rope_tables", out_shape=[jax.ShapeDtypeStruct((rows, LANES), F32)] * 2)(pos, freq, sign)


def _rope(x, cos, sin_signed):
    half = QK_ROPE_DIM // 2
    src = lax.broadcasted_iota(jnp.int32, (LANES, LANES), 0)
    dst = lax.broadcasted_iota(jnp.int32, (LANES, LANES), 1)
    swap = jnp.where(jnp.logical_or(jnp.logical_and(dst < half, src == dst + half),
                                    jnp.logical_and(jnp.logical_and(dst >= half, dst < 2 * half), src == dst - half)),
                     1.0, 0.0).astype(F32)
    swapped = _dot_exact(x, swap, ((1,), (0,)))
    return x * cos + swapped * sin_signed


def _attn_prep(q, kv, proj, kpe_col, cos, sin, *, tr=256):
    rows = q.shape[0]
    heads = q.shape[1] // HEAD_SLOT
    tr = _tile(rows, tr, SUBLANES)

    def body(q_ref, kv_ref, kpe_ref, cos_ref, sin_ref, qc_ref, kc_ref, v_ref):
        c, s = cos_ref[...], sin_ref[...]
        kpe = _rope(kpe_ref[...], c, s).astype(BF16)
        for h in range(heads):
            nope = slice(h * HEAD_SLOT, h * HEAD_SLOT + LANES)
            pe = slice(h * HEAD_SLOT + LANES, (h + 1) * HEAD_SLOT)
            qc_ref[:, nope] = q_ref[:, nope].astype(BF16)
            qc_ref[:, pe] = _rope(q_ref[:, pe], c, s).astype(BF16)
            kc_ref[:, nope] = kv_ref[:, nope].astype(BF16)
            kc_ref[:, pe] = kpe
            v_ref[:, h * LANES:(h + 1) * LANES] = kv_ref[:, pe].astype(BF16)

    slots = pl.BlockSpec((tr, heads * HEAD_SLOT), lambda i: (i, 0))
    tab = pl.BlockSpec((tr, LANES), lambda i: (i, 0))
    return pl.pallas_call(
        body, name="attn_prep", grid=(rows // tr,),
        in_specs=[slots, slots, pl.BlockSpec((tr, LANES), lambda i: (i, kpe_col)), tab, tab],
        out_specs=[slots, slots, pl.BlockSpec((tr, heads * LANES), lambda i: (i, 0))],
        out_shape=[jax.ShapeDtypeStruct((rows, heads * HEAD_SLOT), BF16)] * 2
        + [jax.ShapeDtypeStruct((rows, heads * LANES), BF16)],
        compiler_params=_params(("parallel",)),
    )(q, kv, proj, cos, sin)


def _causal(tq, tk):
    return lax.broadcasted_iota(jnp.int32, (tq, tk), 1) <= lax.broadcasted_iota(jnp.int32, (tq, tk), 0)


def _attn_fwd(qc, kc, vb, *, scale, tq=512):
    rows = qc.shape[0]
    heads = qc.shape[1] // HEAD_SLOT
    tq = _tile(rows, tq, SUBLANES)
    tk = tq

    def body(q_ref, k_ref, v_ref, o_ref, lse_ref):
        i = pl.program_id(1)
        q = q_ref[...]

        def step(j, carry, diagonal):
            m, l, acc = carry
            k0 = pl.multiple_of(j * tk, tk)
            s = _dot_nt(q, k_ref[pl.ds(k0, tk), :]) * scale
            if diagonal:
                s = jnp.where(_causal(tq, tk), s, NEG_INF)
            m_new = jnp.maximum(m, jnp.max(s, axis=-1, keepdims=True))
            p = jnp.exp(s - m_new)
            alpha = jnp.exp(m - m_new)
            l = alpha * l + jnp.sum(p, axis=-1, keepdims=True)
            acc = alpha * acc + _dot_nn(p.astype(BF16), v_ref[pl.ds(k0, tk), :])
            return m_new, l, acc

        init = (jnp.full((tq, 1), NEG_INF, F32), jnp.zeros((tq, 1), F32), jnp.zeros((tq, LANES), F32))
        below = lax.fori_loop(0, i, lambda j, carry: step(j, carry, False), init)
        m, l, acc = step(i, below, True)
        o_ref[...] = acc / l
        lse_ref[...] = jnp.broadcast_to(m + jnp.log(l), (tq, LANES))

    return pl.pallas_call(
        body, name="attn_fwd", grid=(heads, rows // tq),
        in_specs=[pl.BlockSpec((tq, HEAD_SLOT), lambda h, i: (i, h)), pl.BlockSpec((rows, HEAD_SLOT), lambda h, i: (0, h)),
                  pl.BlockSpec((rows, LANES), lambda h, i: (0, h))],
        out_specs=[pl.BlockSpec((tq, LANES), lambda h, i: (i, h))] * 2,
        out_shape=[jax.ShapeDtypeStruct((rows, heads * LANES), F32)] * 2,
        compiler_params=_params(("parallel", "parallel")),
    )(qc, kc, vb)


def _attn_bwd(qc, kc, vb, o, do, lse, cos, sin, *, scale, tk=512):
    rows = qc.shape[0]
    heads = qc.shape[1] // HEAD_SLOT
    tk = _tile(rows, tk, SUBLANES)
    tq = tk
    nq = rows // tq

    def body(q_ref, k_ref, v_ref, o_ref, do_ref, lse_ref, cos_ref, sin_ref, dq_ref, dkv_ref, dkpe_ref, dq_acc, delta_ref):
        j = pl.program_id(1)

        @pl.when(j == 0)
        def _():
            dq_acc[...] = jnp.zeros_like(dq_acc)
            for r0 in range(0, rows, tq):
                d = jnp.sum(do_ref[pl.ds(r0, tq), :] * o_ref[pl.ds(r0, tq), :], axis=-1, keepdims=True)
                delta_ref[pl.ds(r0, tq), :] = jnp.broadcast_to(d, (tq, LANES))

        kb, vv = k_ref[...], v_ref[...]

        def step(i, carry, diagonal):
            dk, dv = carry
            q0 = pl.multiple_of(i * tq, tq)
            qb = q_ref[pl.ds(q0, tq), :]
            dob = do_ref[pl.ds(q0, tq), :].astype(BF16)
            s = _dot_nt(qb, kb) * scale
            p = jnp.exp(s - lse_ref[pl.ds(q0, tq), :1])
            if diagonal:
                p = jnp.where(_causal(tq, tk), p, 0.0)
            dv = dv + _dot_tn(p.astype(BF16), dob)
            ds = (p * (_dot_nt(dob, vv) - delta_ref[pl.ds(q0, tq), :1])).astype(BF16)
            dk = dk + _dot_tn(ds, qb)
            dq_acc[pl.ds(q0, tq), :] += _dot_nn(ds, kb)
            return dk, dv

        zero = (jnp.zeros((tk, HEAD_SLOT), F32), jnp.zeros((tk, LANES), F32))
        dk, dv = lax.fori_loop(j + 1, nq, lambda i, carry: step(i, carry, False), step(j, zero, True))
        dkv_ref[:, :LANES] = (dk[:, :LANES] * scale).astype(dkv_ref.dtype)
        dkv_ref[:, LANES:] = dv.astype(dkv_ref.dtype)
        dkpe_ref[...] = dk[:, LANES:] * scale

        @pl.when(j == nq - 1)
        def _():
            for r0 in range(0, rows, tq):
                dq = dq_acc[pl.ds(r0, tq), :] * scale
                dq_ref[pl.ds(r0, tq), :LANES] = dq[:, :LANES].astype(dq_ref.dtype)
                dq_ref[pl.ds(r0, tq), LANES:] = _rope(dq[:, LANES:], cos_ref[pl.ds(r0, tq), :],
                                                      -sin_ref[pl.ds(r0, tq), :]).astype(dq_ref.dtype)

    full_q = pl.BlockSpec((rows, HEAD_SLOT), lambda h, j: (0, h))
    full_v = pl.BlockSpec((rows, LANES), lambda h, j: (0, h))
    tab = pl.BlockSpec((rows, LANES), lambda h, j: (0, 0))
    return pl.pallas_call(
        body, name="attn_bwd", grid=(heads, rows // tk),
        in_specs=[full_q, pl.BlockSpec((tk, HEAD_SLOT), lambda h, j: (j, h)), pl.BlockSpec((tk, LANES), lambda h, j: (j, h)),
                  full_v, full_v, full_v, tab, tab],
        out_specs=[full_q, pl.BlockSpec((tk, HEAD_SLOT), lambda h, j: (j, h)), pl.BlockSpec((tk, LANES), lambda h, j: (j, h))],
        out_shape=[jax.ShapeDtypeStruct((rows, heads * HEAD_SLOT), BF16), jax.ShapeDtypeStruct((rows, heads * HEAD_SLOT), BF16),
                   jax.ShapeDtypeStruct((rows, heads * LANES), F32)],
        scratch_shapes=[pltpu.VMEM((rows, HEAD_SLOT), F32), pltpu.VMEM((rows, LANES), F32)],
        compiler_params=_params(("parallel", "arbitrary")),
    )(qc, kc, vb, o, do, lse, cos, sin)


def _kpe_bwd(dkpe_heads, cos, sin, *, tr=512):
    rows = dkpe_heads.shape[0]
    heads = dkpe_heads.shape[1] // LANES
    tr = _tile(rows, tr, 2 * SUBLANES)

    def body(d_ref, cos_ref, sin_ref, o_ref):
        acc = d_ref[:, :LANES]
        for h in range(1, heads):
            acc = acc + d_ref[:, h * LANES:(h + 1) * LANES]
        o_ref[...] = _rope(acc, cos_ref[...], -sin_ref[...]).astype(o_ref.dtype)

    tab = pl.BlockSpec((tr, LANES), lambda i: (i, 0))
    return pl.pallas_call(
        body, name="kpe_bwd", grid=(rows // tr,),
        in_specs=[pl.BlockSpec((tr, heads * LANES), lambda i: (i, 0)), tab, tab], out_specs=tab,
        out_shape=jax.ShapeDtypeStruct((rows, LANES), BF16), compiler_params=_params(("parallel",)),
    )(dkpe_heads, cos, sin)


CONV_ROWS = 128


def _with_halo(ref, r0, ci, n_chunks, ch, lanes, before, after):
    parts = []
    if before:
        lo = pl.multiple_of(jnp.maximum(r0 - SUBLANES, 0), SUBLANES)
        parts.append(ref[pl.ds(lo, SUBLANES), lanes] * jnp.where(ci > 0, 1.0, 0.0))
    parts.append(ref[pl.ds(r0, ch), lanes])
    if after:
        hi = pl.multiple_of(jnp.minimum(r0 + ch, n_chunks * ch - SUBLANES), SUBLANES)
        parts.append(ref[pl.ds(hi, SUBLANES), lanes] * jnp.where(ci < n_chunks - 1, 1.0, 0.0))
    return jnp.concatenate(parts, axis=0)


def _taps(ext):
    return pltpu.roll(ext, 2, 0)[SUBLANES:], pltpu.roll(ext, 1, 0)[SUBLANES:], ext[SUBLANES:]


def _conv3(taps, w, b):
    return w[0:1, :] * taps[0] + w[1:2, :] * taps[1] + w[2:3, :] * taps[2] + b


def _conv_gate_fwd(a, conv_w, conv_b, *, tc=256):
    rows, f2 = a.shape
    f = f2 // 2
    tc = _tile(f, tc)
    nc = f // tc
    ch = _tile(rows, CONV_ROWS, SUBLANES)
    n_chunks = rows // ch

    def body(ag_ref, av_ref, wg_ref, wv_ref, bg_ref, bv_ref, o_ref):
        for lt in range(tc // LANES):
            lanes = slice(lt * LANES, (lt + 1) * LANES)
            wg, wv, bg, bv = wg_ref[:, lanes], wv_ref[:, lanes], bg_ref[:, lanes], bv_ref[:, lanes]

            def chunk(ci, carry):
                r0 = pl.multiple_of(ci * ch, ch)
                gate = _conv3(_taps(_with_halo(ag_ref, r0, ci, n_chunks, ch, lanes, True, False)), wg, bg)
                val = _conv3(_taps(_with_halo(av_ref, r0, ci, n_chunks, ch, lanes, True, False)), wv, bv)
                o_ref[pl.ds(r0, ch), lanes] = (gate * jax.nn.sigmoid(gate) * val).astype(o_ref.dtype)
                return carry

            lax.fori_loop(0, n_chunks, chunk, 0)

    return pl.pallas_call(
        body, name="conv_gate_fwd", grid=(nc,),
        in_specs=[pl.BlockSpec((rows, tc), lambda j: (0, j)), pl.BlockSpec((rows, tc), lambda j: (0, j + nc)),
                  pl.BlockSpec((SUBLANES, tc), lambda j: (0, j)), pl.BlockSpec((SUBLANES, tc), lambda j: (0, j + nc)),
                  pl.BlockSpec((1, tc), lambda j: (0, j)), pl.BlockSpec((1, tc), lambda j: (0, j + nc))],
        out_specs=pl.BlockSpec((rows, tc), lambda j: (0, j)),
        out_shape=jax.ShapeDtypeStruct((rows, f), BF16), compiler_params=_params(("parallel",)),
    )(a, a, conv_w, conv_w, conv_b, conv_b)


def _conv_gate_bwd(a, conv_w, conv_b, dg, *, tc=256):
    rows, f2 = a.shape
    f = f2 // 2
    tc = _tile(f, tc)
    nc = f // tc
    ch = _tile(rows, CONV_ROWS, SUBLANES)
    n_chunks = rows // ch
    ext_rows = ch + SUBLANES

    def fold(x):
        return jnp.sum(x.reshape(ch // SUBLANES, SUBLANES, LANES), axis=0)

    def body(ag_ref, av_ref, wg_ref, wv_ref, bg_ref, bv_ref, dg_ref, da_ref, dw_ref, db_ref):
        for lt in range(tc // LANES):
            lanes = slice(lt * LANES, (lt + 1) * LANES)
            wg, wv, bg, bv = wg_ref[:, lanes], wv_ref[:, lanes], bg_ref[:, lanes], bv_ref[:, lanes]

            def chunk(ci, acc):
                r0 = pl.multiple_of(ci * ch, ch)
                taps_g = _taps(_with_halo(ag_ref, r0, ci, n_chunks, ch, lanes, True, True))
                taps_v = _taps(_with_halo(av_ref, r0, ci, n_chunks, ch, lanes, True, True))
                dge = _with_halo(dg_ref, r0, ci, n_chunks, ch, lanes, False, True)
                gate, val = _conv3(taps_g, wg, bg), _conv3(taps_v, wv, bv)
                sg = jax.nn.sigmoid(gate)
                d_gate = dge * val * sg * (1.0 + gate * (1.0 - sg))
                d_val = dge * gate * sg
                new = []
                for half, (taps, w, d) in enumerate(((taps_g, wg, d_gate), (taps_v, wv, d_val))):
                    da = (w[2:3, :] * d[:ch] + w[1:2, :] * pltpu.roll(d, ext_rows - 1, 0)[:ch]
                          + w[0:1, :] * pltpu.roll(d, ext_rows - 2, 0)[:ch])
                    da_ref[half, pl.ds(r0, ch), lanes] = da.astype(da_ref.dtype)
                    dc = d[:ch]
                    sums = [fold(dc)] + [fold(dc * t[:ch]) for t in taps]
                    new.append(tuple(x + s for x, s in zip(acc[half], sums)))
                return tuple(new)

            zero = tuple(jnp.zeros((SUBLANES, LANES), F32) for _ in range(4))
            acc = lax.fori_loop(0, n_chunks, chunk, (zero, zero))
            row = lax.broadcasted_iota(jnp.int32, (SUBLANES, LANES), 0)
            for half in range(2):
                db, *taps = (jnp.sum(x, axis=0, keepdims=True) for x in acc[half])
                db_ref[half, :, lanes] = db
                dw = jnp.zeros((SUBLANES, LANES), F32)
                for tap in range(3):
                    dw = jnp.where(row == tap, taps[tap], dw)
                dw_ref[half, :, lanes] = dw

    lo = lambda j: (0, j)
    hi = lambda j: (0, j + nc)
    both = lambda j: (0, 0, j)
    return pl.pallas_call(
        body, name="conv_gate_bwd", grid=(nc,),
        in_specs=[pl.BlockSpec((rows, tc), lo), pl.BlockSpec((rows, tc), hi), pl.BlockSpec((SUBLANES, tc), lo),
                  pl.BlockSpec((SUBLANES, tc), hi), pl.BlockSpec((1, tc), lo), pl.BlockSpec((1, tc), hi),
                  pl.BlockSpec((rows, tc), lo)],
        out_specs=[pl.BlockSpec((2, rows, tc), both), pl.BlockSpec((2, SUBLANES, tc), both), pl.BlockSpec((2, 1, tc), both)],
        out_shape=[jax.ShapeDtypeStruct((2, rows, f), BF16), jax.ShapeDtypeStruct((2, SUBLANES, f), F32),
                   jax.ShapeDtypeStruct((2, 1, f), F32)],
        compiler_params=_params(("parallel",)),
    )(a, a, conv_w, conv_w, conv_b, conv_b, dg)


def _wgrad(a, b, rows, cols, row_sharded, name, **kw):
    return functools.partial(_wgrad_half, a, b, rows, cols, row_sharded, name, **kw)


class _NoExchange:
    def __init__(self, later, ffn):
        self.later, self.ffn = later, ffn

    def mixer_weights(self, after):
        return self.later

    def ffn_weights_arrived(self, after):
        return None

    def ffn_weights(self, after):
        return self.ffn

    def ffn_down_arrived(self, after):
        return None

    def ffn_down_weight(self, after):
        return self.ffn["ffn_w_down"]

    def ffn_grads(self, makers, after):
        self.ffn_makers = makers
        return None

    def ffn_backward_done(self, after):
        return None


def _local_step(x, posf, target, w, hooks):
    rows, d = x.shape
    width = w["ssm_d"].shape[1]
    qr, kvr = w["mla_q_norm_w"].shape[1], w["mla_kv_norm_w"].shape[1]
    heads = w["mla_w_ukv"].shape[1] // HEAD_SLOT
    f2 = w["ffn_conv_b"].shape[1]
    inp = w["w_in"].shape[0]
    scale = (QK_NOPE_DIM + QK_ROPE_DIM) ** -0.5
    g = {}

    hn = _rmsnorm_fwd(x, w["attn_norm_w"], name="attn_norm")
    proj = _matmul(hn, w["w_in"], mode="nt", name="in_proj")

    s5_weights = (w["ssm_lambda_re"], w["ssm_lambda_im"], w["ssm_log_dt"], w["ssm_b_re"], w["ssm_b_im"])
    wb, wct, abar = _s5_bands(*s5_weights, w["ssm_c_re"], w["ssm_c_im"])
    states, y_pre, yg = _s5_fwd(proj, wb, wct, w["ssm_d"], abar)
    later = hooks.mixer_weights(yg)
    z = _matmul(yg, later["ssm_w_glu"], mode="nn", name="glu_proj", bias=w["ssm_b_glu"])
    ys = _glu_norm_fwd(y_pre, z, w["ssm_out_norm_w"])

    q_col, kv_col, kpe_col = width // qr, (width + qr) // kvr, (width + qr + kvr) // LANES
    assert width % qr == 0 and (width + qr) % kvr == 0
    qn = _rmsnorm_fwd(proj, w["mla_q_norm_w"], name="q_norm", width=qr, col=q_col)
    kvn = _rmsnorm_fwd(proj, w["mla_kv_norm_w"], name="kv_norm", width=kvr, col=kv_col)
    q = _matmul(qn, w["mla_w_uq"], mode="nn", name="q_proj")
    kv = _matmul(kvn, w["mla_w_ukv"], mode="nn", name="kv_proj")
    half = QK_ROPE_DIM // 2
    inv_freq = ROPE_THETA ** (-jnp.arange(0, QK_ROPE_DIM, 2, dtype=F32) / QK_ROPE_DIM)
    zeros = jnp.zeros((LANES - QK_ROPE_DIM,), F32)
    freq = jnp.concatenate([inv_freq, inv_freq, zeros]).reshape(1, LANES)
    sign = jnp.concatenate([-jnp.ones((half,), F32), jnp.ones((half,), F32), zeros]).reshape(1, LANES)
    cos, sin = _rope_tables(posf, freq, sign)
    qc, kc, vb = _attn_prep(q, kv, proj, kpe_col, cos, sin)
    o, lse = _attn_fwd(qc, kc, vb, scale=scale, tq=ATTN_BLOCK)
    ym = _rmsnorm_fwd(o, w["mla_out_norm_w"], name="mla_out_norm")
    ycat = jnp.concatenate([ys, ym], axis=1)
    h1 = _matmul(ycat, later["w_out"], mode="nn", name="out_proj", add=x, after=hooks.ffn_weights_arrived(ycat))

    hn2 = _rmsnorm_fwd(h1, w["ffn_norm_w"], name="ffn_norm")
    ffn = hooks.ffn_weights(hn2)
    a = _matmul(hn2, ffn["ffn_w_up"], mode="nn", name="ffn_up", tm=FFN_ROWS)
    started = hooks.ffn_down_arrived(a)
    conv_b = w["ffn_conv_b"] if started is None else w["ffn_conv_b"] + started[:1, :1]
    gated = _conv_gate_fwd(a, ffn["ffn_conv_w"], conv_b)
    w_down = hooks.ffn_down_weight(gated)
    h2 = _matmul(gated, w_down, mode="nn", name="ffn_down", add=h1, tk=2816, tm=FFN_ROWS)
    loss_tile, dh2, dh2_mxu, g["final_norm_w"] = _final_norm_loss(h2, w["final_norm_w"], target)

    dgated = _matmul(dh2_mxu, w_down, mode="nt", name="ffn_down_dx", tm=FFN_ROWS)
    da, dcw, dcb = _conv_gate_bwd(a, ffn["ffn_conv_w"], w["ffn_conv_b"], dgated)
    g["ffn_conv_w"] = jnp.concatenate([dcw[0, :3], dcw[1, :3]], axis=1)
    g["ffn_conv_b"] = jnp.concatenate([dcb[0], dcb[1]], axis=1)
    started = hooks.ffn_grads({
        "ffn_w_up": _wgrad(hn2, da, d, f2, False, "ffn_up_dw", b_split=True, tn=_tile(f2 // N_CHIPS, 1408)),
        "ffn_w_down": _wgrad(gated, dh2_mxu, f2 // 2, d, True, "ffn_down_dw", tm=f2 // 2 // N_CHIPS, tn=512)}, dcb)
    dhn2 = _matmul(da, ffn["ffn_w_up"], mode="nt", name="ffn_up_dx", a_split=True, tk=_tile(f2 // 2, 2816), tm=FFN_ROWS,
                   after=started)
    dh1, dh1_mxu, g["ffn_norm_w"] = _rmsnorm_bwd(h1, w["ffn_norm_w"], dhn2, name="ffn_norm_bwd", add=dh2,
                                                dx_dtypes=(F32, BF16))

    dycat = _matmul(dh1_mxu, later["w_out"], mode="nt", name="out_proj_dx")
    g["w_out"] = _wgrad(ycat, dh1_mxu, 2 * width, d, True, "out_proj_dw")
    started = hooks.ffn_backward_done(dycat)
    mla_out_norm_w, ssm_out_norm_w = w["mla_out_norm_w"], w["ssm_out_norm_w"]
    if started is not None:
        mla_out_norm_w, ssm_out_norm_w = mla_out_norm_w + started[:1, :1], ssm_out_norm_w + started[:1, :1]

    do, g["mla_out_norm_w"] = _rmsnorm_bwd(o, mla_out_norm_w, dycat, name="mla_out_norm_bwd", width=width, dy_col=1)
    dq, dkv, dkpe_heads = _attn_bwd(qc, kc, vb, o, do, lse, cos, sin, scale=scale, tk=ATTN_BLOCK)
    dkpe = _kpe_bwd(dkpe_heads, cos, sin)
    g["mla_w_uq"] = _wgrad(qn, dq, qr, heads * HEAD_SLOT, False, "q_proj_dw")
    dqn = _matmul(dq, w["mla_w_uq"], mode="nt", name="q_proj_dx")
    dcq, g["mla_q_norm_w"] = _rmsnorm_bwd(proj, w["mla_q_norm_w"], dqn, name="q_norm_bwd", width=qr, col=q_col,
                                          dx_dtypes=(BF16,))
    g["mla_w_ukv"] = _wgrad(kvn, dkv, kvr, heads * HEAD_SLOT, False, "kv_proj_dw")
    dkvn = _matmul(dkv, w["mla_w_ukv"], mode="nt", name="kv_proj_dx")
    dckv, g["mla_kv_norm_w"] = _rmsnorm_bwd(proj, w["mla_kv_norm_w"], dkvn, name="kv_norm_bwd", width=kvr, col=kv_col,
                                            dx_dtypes=(BF16,))

    dz, dyg_a, g["ssm_out_norm_w"], g["ssm_b_glu"] = _glu_norm_bwd(y_pre, z, ssm_out_norm_w, dycat)
    dyg_b = _matmul(dz, later["ssm_w_glu"], mode="nt", name="glu_proj_dx")
    g["ssm_w_glu"] = _wgrad(yg, dz, width, width, True, "glu_proj_dw")
    du, dwb, dwct, dabar, g["ssm_d"] = _s5_bwd(proj, states, y_pre, dyg_a, dyg_b, wb, wct, w["ssm_d"], abar)
    (g["ssm_lambda_re"], g["ssm_lambda_im"], g["ssm_log_dt"], g["ssm_b_re"], g["ssm_b_im"], g["ssm_c_re"],
     g["ssm_c_im"]) = _s5_bands_bwd(*s5_weights, dwb, dwct, dabar)

    pad = jnp.zeros((rows, inp - (width + qr + kvr + LANES)), BF16)
    dproj = jnp.concatenate([du, dcq, dckv, dkpe, pad], axis=1)
    g["w_in"] = _wgrad(dproj, hn, inp, d, False, "in_proj_dw")
    dhn = _matmul(dproj, w["w_in"], mode="nn", name="in_proj_dx")
    dx, g["attn_norm_w"] = _rmsnorm_bwd(x, w["attn_norm_w"], dhn, name="attn_norm_bwd", add=dh1)
    return loss_tile, dx, g


ANY = pl.BlockSpec(memory_space=pl.ANY)
MESH = pl.DeviceIdType.MESH


def _mesh_pos():
    return lax.axis_index("x"), lax.axis_index("y"), lax.axis_index("c")


def _other_chips(x, y):
    return [(1 - x, y), (x, 1 - y), (1 - x, 1 - y)]


def _remote(src, dst, send_sems, recv_sems, k, to):
    return pltpu.make_async_remote_copy(src_ref=src, dst_ref=dst, send_sem=send_sems.at[k], recv_sem=recv_sems.at[k],
                                        device_id=to, device_id_type=MESH)


def _place_shard(shard, piece_idx, row_sharded, name, out_dtype=BF16, pieces=N_CHIPS, after=None):
    rs, cs = shard.shape
    tr = _tile(rs, 256, 2 * SUBLANES)
    rb = rs // tr
    extra = [] if after is None else [after]

    def body(p_ref, x_ref, *rest):
        o_ref = rest[-1]
        o_ref[...] = x_ref[...].astype(o_ref.dtype)

    if row_sharded:
        out_shape, out_map = (pieces * rs, cs), (lambda i, p_ref: (p_ref[0] * rb + i, 0))
    else:
        out_shape, out_map = (rs, pieces * cs), (lambda i, p_ref: (i, p_ref[0]))
    return pl.pallas_call(
        body, name=name, out_shape=jax.ShapeDtypeStruct(out_shape, out_dtype),
        grid_spec=pltpu.PrefetchScalarGridSpec(
            num_scalar_prefetch=1, grid=(rb,),
            in_specs=[pl.BlockSpec((tr, cs), lambda i, p_ref: (i, 0))] + [pl.BlockSpec(memory_space=pl.ANY)] * len(extra),
            out_specs=pl.BlockSpec((tr, cs), out_map)),
        compiler_params=_params(("parallel",)),
    )(piece_idx, shard, *extra)


def _gather_weights(placed, name):
    n = len(placed)
    meta = [(row_sharded, direct) for _, row_sharded, direct in placed]
    over_ici, over_d2d = _gather_plans(meta)
    forwarded = [t for t, (_, direct) in enumerate(meta) if not direct]

    def body(*refs):
        outs = refs[n:2 * n]
        send_sems, recv_sems, pass_send_sems, pass_recv_sems = refs[2 * n:]
        first, arrivals = over_ici(outs, send_sems, recv_sems)
        passed, passed_arrivals = over_d2d([outs[t] for t in forwarded], pass_send_sems, pass_recv_sems)
        for cp in first:
            cp.start()
        for t in range(n):
            for j in range(3):
                arrivals[3 * t + j].wait_recv()
                if t in forwarded:
                    passed[3 * forwarded.index(t) + j].start()
        for cp in passed_arrivals:
            cp.wait_recv()
        for cp in first + passed:
            cp.wait_send()

    return pl.pallas_call(
        body, name=name, in_specs=[ANY] * n, out_specs=[ANY] * n,
        out_shape=[jax.ShapeDtypeStruct(arr.shape, arr.dtype) for arr, _, _ in placed],
        input_output_aliases={t: t for t in range(n)},
        scratch_shapes=[pltpu.SemaphoreType.DMA((3 * n,)), pltpu.SemaphoreType.DMA((3 * n,)),
                        pltpu.SemaphoreType.DMA((3 * len(forwarded),)), pltpu.SemaphoreType.DMA((3 * len(forwarded),))],
    )(*[arr for arr, _, _ in placed])


def _gather_plans(meta):
    def window(ref, row_sharded, piece, half):
        r, cc = ref.shape
        if row_sharded:
            rs = r // N_CHIPS
            if half is None:
                return ref.at[pl.ds(piece * rs, rs), :]
            return ref.at[pl.ds(piece * rs + half * (rs // 2), rs // 2), :]
        cs = cc // N_CHIPS
        if half is None:
            return ref.at[:, pl.ds(piece * cs, cs)]
        return ref.at[pl.ds(half * (r // 2), r // 2), pl.ds(piece * cs, cs)]

    def over_ici(refs, send_sems, recv_sems):
        x, y, c = _mesh_pos()
        sends, recvs = [], []
        for t, (row_sharded, direct) in enumerate(meta):
            mine = window(refs[t], row_sharded, 2 * x + y, None if direct else c)
            for j, (px, py) in enumerate(_other_chips(x, y)):
                theirs = window(refs[t], row_sharded, 2 * px + py, None if direct else c)
                sends.append(_remote(mine, mine, send_sems, recv_sems, 3 * t + j, (px, py, c)))
                recvs.append(_remote(theirs, theirs, send_sems, recv_sems, 3 * t + j, (px, py, c)))
        return sends, recvs

    def over_d2d(refs, send_sems, recv_sems):
        x, y, c = _mesh_pos()
        sends, recvs = [], []
        rows = [row_sharded for row_sharded, direct in meta if not direct]
        for t, row_sharded in enumerate(rows):
            for j, (px, py) in enumerate(_other_chips(x, y)):
                got = window(refs[t], row_sharded, 2 * px + py, c)
                other = window(refs[t], row_sharded, 2 * px + py, 1 - c)
                sends.append(_remote(got, got, send_sems, recv_sems, 3 * t + j, (x, y, 1 - c)))
                recvs.append(_remote(other, other, send_sems, recv_sems, 3 * t + j, (x, y, 1 - c)))
        return sends, recvs

    return over_ici, over_d2d


HBM = pl.BlockSpec(memory_space=pltpu.HBM)
SEMAPHORES = pl.BlockSpec(memory_space=pltpu.SEMAPHORE)
DATAFLOW = pltpu.SideEffectType.DATAFLOW_SIDE_EFFECTING


def _start_copies(name, arrays, plan, n_copies, after):
    n = len(arrays)

    def body(*refs):
        sends, _ = plan(refs[:n], refs[n + 1], refs[n + 2])
        for cp in sends:
            cp.start()
        token = refs[2 * n + 3]
        token[...] = jnp.zeros_like(token)

    out = pl.pallas_call(
        body, name=name,
        out_shape=(pltpu.SemaphoreType.DMA((n_copies,)), pltpu.SemaphoreType.DMA((n_copies,)),
                   *[pltpu.HBM(a.shape, a.dtype) for a in arrays], jax.ShapeDtypeStruct((SUBLANES, LANES), F32)),
        in_specs=[HBM] * n + [ANY],
        out_specs=(SEMAPHORES, SEMAPHORES, *[HBM] * n, pl.BlockSpec(memory_space=pltpu.VMEM)),
        input_output_aliases={t: t + 2 for t in range(n)},
        compiler_params=pltpu.CompilerParams(has_side_effects=DATAFLOW),
    )(*[pltpu.with_memory_space_constraint(a, pltpu.HBM) for a in arrays], after)
    return out[0], out[1], list(out[2:2 + n]), out[2 + n]


def _wait_copies(name, started, plan, after):
    send_sems, recv_sems, arrays, _ = started
    n = len(arrays)

    def body(*refs):
        sends, recvs = plan(refs[:n], refs[n], refs[n + 1])
        for cp in sends:
            cp.wait_send()
        for cp in recvs:
            cp.wait_recv()

    out = pl.pallas_call(
        body, name=name, out_shape=[pltpu.HBM(a.shape, a.dtype) for a in arrays],
        in_specs=[HBM] * n + [SEMAPHORES, SEMAPHORES, ANY], out_specs=[HBM] * n,
        input_output_aliases={t: t for t in range(n)},
        compiler_params=pltpu.CompilerParams(has_side_effects=DATAFLOW),
    )(*arrays, send_sems, recv_sems, after)
    return list(out)


def _exchange(name, arrays, plan, n_copies, after=None):
    n = len(arrays)
    extra = [] if after is None else [after]

    def body(*refs):
        outs = refs[n + len(extra):2 * n + len(extra)]
        send_sems, recv_sems = refs[2 * n + len(extra):]
        sends, recvs = plan(outs, send_sems, recv_sems)
        for cp in sends:
            cp.start()
        for cp in recvs:
            cp.wait_recv()
        for cp in sends:
            cp.wait_send()

    return pl.pallas_call(
        body, name=name, in_specs=[ANY] * (n + len(extra)), out_specs=[ANY] * n,
        out_shape=[jax.ShapeDtypeStruct(a.shape, a.dtype) for a in arrays],
        input_output_aliases={t: t for t in range(n)},
        scratch_shapes=[pltpu.SemaphoreType.DMA((n_copies,)), pltpu.SemaphoreType.DMA((n_copies,))],
    )(*arrays, *extra)


def _give_plan(n):
    def plan(refs, send_sems, recv_sems):
        x, y, c = _mesh_pos()
        sends = [_remote(refs[t], refs[n + t], send_sems, recv_sems, t, (x, y, 1 - c)) for t in range(n)]
        return sends, sends

    return plan


def _scatter_plan(n):
    def plan(refs, send_sems, recv_sems):
        x, y, c = _mesh_pos()
        sends = []
        for t in range(n):
            for j, (px, py) in enumerate(_other_chips(x, y)):
                sends.append(_remote(refs[t].at[2 * px + py], refs[n + t].at[j], send_sems, recv_sems, 3 * t + j, (px, py, c)))
        return sends, sends

    return plan


def _scatter_shapes(sums):
    return [jax.ShapeDtypeStruct((3,) + s.shape[1:], s.dtype) for s in sums]


def _join_plan(n):
    def plan(refs, send_sems, recv_sems):
        x, y, c = _mesh_pos()
        sends = [_remote(refs[t].at[c], refs[t].at[c], send_sems, recv_sems, t, (x, y, 1 - c)) for t in range(n)]
        recvs = [_remote(refs[t].at[1 - c], refs[t].at[1 - c], send_sems, recv_sems, t, (x, y, 1 - c)) for t in range(n)]
        return sends, recvs

    return plan


def _join_halves(halves, name, after=None):
    return _exchange(name, halves, _join_plan(len(halves)), len(halves), after=after)


def _add_other_half(g4, got, where, name):
    _, pieces, sr, sc = g4.shape
    tr = _tile(sr, 256, 2 * SUBLANES)

    def body(w_ref, a_ref, b_ref, o_ref):
        o_ref[...] = a_ref[...] + b_ref[...]

    blk = pl.BlockSpec((None, tr, sc), lambda p, i, w_ref: (p, i, 0))
    return pl.pallas_call(
        body, name=name, out_shape=jax.ShapeDtypeStruct((pieces, sr, sc), F32),
        grid_spec=pltpu.PrefetchScalarGridSpec(
            num_scalar_prefetch=1, grid=(pieces, sr // tr),
            in_specs=[pl.BlockSpec((None, None, tr, sc), lambda p, i, w_ref: (w_ref[0], p, i, 0)), blk], out_specs=blk),
        compiler_params=_params(("parallel", "parallel")),
    )(where, g4, got)


def _add_pieces(sums, got_pieces, where, name):
    _, sr, sc = sums.shape
    tr = _tile(sr, 256, 2 * SUBLANES)

    def body(w_ref, a_ref, r_ref, o_ref):
        acc = a_ref[...]
        for j in range(3):
            acc = acc + r_ref[j].astype(F32)
        o_ref[...] = acc

    return pl.pallas_call(
        body, name=name, out_shape=jax.ShapeDtypeStruct((N_CORES, sr, sc), F32),
        grid_spec=pltpu.PrefetchScalarGridSpec(
            num_scalar_prefetch=1, grid=(sr // tr,),
            in_specs=[pl.BlockSpec((None, tr, sc), lambda i, w_ref: (w_ref[1], i, 0)),
                      pl.BlockSpec((3, tr, sc), lambda i, w_ref: (0, i, 0))],
            out_specs=pl.BlockSpec((None, tr, sc), lambda i, w_ref: (w_ref[0], i, 0))),
        compiler_params=_params(("parallel",)),
    )(where, sums, got_pieces)


def _adamw_update(w, g, m, v):
    nm = ADAM_B1 * m + (1.0 - ADAM_B1) * g
    nv = ADAM_B2 * v + (1.0 - ADAM_B2) * (g * g)
    m_hat = nm / (1.0 - ADAM_B1 ** ADAM_STEP)
    v_hat = nv / (1.0 - ADAM_B2 ** ADAM_STEP)
    return -ADAM_LR * (m_hat / (jnp.sqrt(v_hat) + ADAM_EPS) + ADAM_WD * w), nm, nv


def _adamw(w, g, m, v, name, after=None):
    rows, cols = w.shape
    halves = 2 if g.ndim == 3 else 1
    bc = cols // halves
    tr = _tile(rows, max(SUBLANES, (1 << 19) // max(bc, 1) // SUBLANES * SUBLANES), SUBLANES)

    def body(w_ref, g_ref, m_ref, v_ref, *rest):
        d_ref, nm_ref, nv_ref, go_ref = rest[-4:]
        gv = g_ref[...]
        d_ref[...], nm_ref[...], nv_ref[...] = _adamw_update(w_ref[...], gv, m_ref[...], v_ref[...])
        go_ref[...] = gv

    blk = pl.BlockSpec((tr, bc), lambda i, h: (i, h))
    g_blk = pl.BlockSpec((None, tr, bc), lambda i, h: (h, i, 0)) if halves == 2 else blk
    extra = [] if after is None else [after]
    return pl.pallas_call(
        body, name=name, grid=(rows // tr, halves),
        in_specs=[blk, g_blk, blk, blk] + [pl.BlockSpec(memory_space=pl.ANY)] * len(extra), out_specs=[blk] * 4,
        out_shape=[jax.ShapeDtypeStruct((rows, cols), F32)] * 4, compiler_params=_params(("parallel", "parallel")),
    )(w, g, m, v, *extra)


def _adamw_many(ws, gs, ms, vs, name):
    n = len(ws)

    def body(*refs):
        outs = refs[4 * n:]
        for k in range(n):
            w_ref, g_ref, m_ref, v_ref = (refs[j * n + k] for j in range(4))
            outs[k][...], outs[n + k][...], outs[2 * n + k][...] = _adamw_update(w_ref[...], g_ref[...], m_ref[...], v_ref[...])

    out = pl.pallas_call(
        body, name=name, out_shape=[jax.ShapeDtypeStruct(w.shape, F32) for w in ws] * 3,
        compiler_params=pltpu.CompilerParams(vmem_limit_bytes=VMEM_LIMIT_BYTES),
    )(*ws, *gs, *ms, *vs)
    return out[:n], out[n:2 * n], out[2 * n:]


WEIGHTS = ['attn_norm_w', 'w_in', 'ssm_lambda_re', 'ssm_lambda_im', 'ssm_log_dt', 'ssm_b_re', 'ssm_b_im', 'ssm_c_re',
           'ssm_c_im', 'ssm_d', 'ssm_w_glu', 'ssm_b_glu', 'mla_q_norm_w', 'mla_w_uq', 'mla_kv_norm_w', 'mla_w_ukv',
           'ssm_out_norm_w', 'mla_out_norm_w', 'w_out', 'ffn_norm_w', 'ffn_w_up', 'ffn_conv_w', 'ffn_conv_b',
           'ffn_w_down', 'final_norm_w']
SHARDED = {'w_in': False, 'ssm_w_glu': True, 'mla_w_uq': False, 'mla_w_ukv': False, 'w_out': True, 'ffn_w_up': False,
           'ffn_w_down': True}
SMALL = [n for n in WEIGHTS if n not in SHARDED and n != 'ffn_conv_w']
ROPE_PAD = HEAD_SLOT - QK_NOPE_DIM - QK_ROPE_DIM
SMALL_COLS = 8 * LANES


def _pad_heads(w_uq, heads):
    qr = w_uq.shape[0]
    w3 = w_uq.reshape(qr, heads, QK_NOPE_DIM + QK_ROPE_DIM)
    return jnp.concatenate([w3, jnp.zeros((qr, heads, ROPE_PAD), w_uq.dtype)], axis=2).reshape(qr, heads * HEAD_SLOT)


def _unpad_heads(g_uq, heads):
    qr = g_uq.shape[0]
    return g_uq.reshape(qr, heads, HEAD_SLOT)[:, :, :QK_NOPE_DIM + QK_ROPE_DIM].reshape(qr, -1)


FFN = ['ffn_w_up', 'ffn_w_down']
MIXER_LATER = ['ssm_w_glu', 'w_out']
MIXER_BIG = ['w_in', 'w_out']
FFN_GATHER = FFN + ['ffn_conv_w']


class _Overlapped:
    def __init__(self, placed_first, first_sharding, where):
        self.where, self.mine, self.other = where, where[:1], 1 - where[:1]
        self.first_ici, self.first_d2d = _gather_plans([(r, False) for r in first_sharding])
        self.first = _start_copies("gather_first_start", placed_first, self.first_ici, 3 * len(placed_first), where)
        self.first_started = self.first[3]

    def start_rest(self, placed_later, placed):
        self.later_ici, self.later_d2d = _gather_plans([(SHARDED[n], False) for n in MIXER_LATER])
        self.later = _start_copies("gather_later_start", placed_later, self.later_ici, 3 * len(placed_later),
                                   self.first_started)
        up, down, taps = placed
        self.up_ici, self.up_d2d = _gather_plans([(SHARDED["ffn_w_up"], False), (False, True)])
        self.up = _start_copies("gather_ffn_up_start", [up, taps], self.up_ici, 6, self.later[3])
        self.down_ici, self.down_d2d = _gather_plans([(SHARDED["ffn_w_down"], False)])
        self.down = _start_copies("gather_ffn_down_start", [down], self.down_ici, 3, self.up[3])
        self.gather_started = self.down[3]
        arrived = _wait_copies("gather_first_wait", self.first, self.first_ici, self.gather_started)
        return _exchange("gather_first_pass", arrived, self.first_d2d, 3 * len(arrived))

    def mixer_weights(self, after):
        arrived = _wait_copies("gather_later_wait", self.later, self.later_ici, after)
        return dict(zip(MIXER_LATER, _exchange("gather_later_pass", arrived, self.later_d2d, 3 * len(arrived))))

    def ffn_weights_arrived(self, after):
        up, self.taps = _wait_copies("gather_ffn_up_wait", self.up, self.up_ici, after)
        self.up_passing = _start_copies("gather_ffn_up_pass_start", [up], self.up_d2d, 3, after)
        return self.up_passing[3]

    def ffn_weights(self, after):
        w_up, = _wait_copies("gather_ffn_up_pass_wait", self.up_passing, self.up_d2d, after)
        return {"ffn_w_up": w_up, "ffn_conv_w": self.taps}

    def ffn_down_arrived(self, after):
        down, = _wait_copies("gather_ffn_down_wait", self.down, self.down_ici, after)
        self.down_passing = _start_copies("gather_ffn_down_pass_start", [down], self.down_d2d, 3, after)
        return self.down_passing[3]

    def ffn_down_weight(self, after):
        return _wait_copies("gather_ffn_down_pass_wait", self.down_passing, self.down_d2d, after)[0]

    def ffn_grads(self, makers, after):
        self.makers = [makers[name] for name in FFN]
        n = len(FFN)
        give = [make(self.other, suffix="_give") for make in self.makers]
        lands = [lax.empty(g.shape, g.dtype) for g in give]
        self.swap = _start_copies("grad_ffn_swap_start", give + lands, _give_plan(n), n, after)
        return self.swap[3]

    def ffn_backward_done(self, after):
        n = len(FFN)
        got = _wait_copies("grad_ffn_swap_wait", self.swap, _give_plan(n), after)[n:]
        kept = [make(self.mine, suffix="_keep", add=got[t], wire=True) for t, make in enumerate(self.makers)]
        self.sums = [k[0] for k in kept]
        wires = [k[1] for k in kept]
        lands = [lax.empty(s.shape, s.dtype) for s in _scatter_shapes(wires)]
        self.scatter = _start_copies("grad_ffn_scatter_start", wires + lands, _scatter_plan(n), 3 * n, after)
        return self.scatter[3]

    def ffn_reduced(self, after):
        n = len(FFN)
        got_pieces = _wait_copies("grad_ffn_scatter_wait", self.scatter, _scatter_plan(n), after)[n:]
        return [_add_pieces(self.sums[t], got_pieces[t], self.where, "grad_add_pieces_" + name) for t, name in enumerate(FFN)]


def _step(args):
    x, positions, target = args["x"][0], args["positions"], args["loss_target"][0]
    rows = x.shape[0]
    p = {n: args[n] for n in WEIGHTS}
    xi, yi, ci = _mesh_pos()
    piece = 2 * xi + yi

    def transposed(a):
        return jnp.swapaxes(a[0], 0, 1)

    def as_stored(n, a):
        return jnp.swapaxes(a, 2, 3) if n in ("ssm_b_re", "ssm_b_im") else a

    w_in = transposed(p["w_in"])
    in_width = w_in.shape[0]
    in_pad = (-in_width) % (2 * LANES)
    heads_here = p["mla_w_uq"].shape[2] // (QK_NOPE_DIM + QK_ROPE_DIM)
    shards = {
        "w_in": jnp.pad(w_in, ((0, in_pad), (0, 0))),
        "ssm_w_glu": p["ssm_w_glu"][0],
        "mla_w_uq": _pad_heads(p["mla_w_uq"][0], heads_here),
        "mla_w_ukv": p["mla_w_ukv"][0],
        "w_out": p["w_out"][0],
        "ffn_w_up": p["ffn_w_up"][0],
        "ffn_w_down": p["ffn_w_down"][0],
    }
    conv_w = jnp.pad(p["ffn_conv_w"][0], ((0, SUBLANES - p["ffn_conv_w"].shape[1]), (0, 0)))
    order = list(SHARDED)
    piece_idx = piece.reshape(1).astype(jnp.int32)
    mixer = [n for n in order if n not in FFN]
    first = [n for n in mixer if n not in MIXER_LATER]
    where = jnp.stack([ci, piece]).astype(jnp.int32)
    placed = {n: _place_shard(shards[n], piece_idx, SHARDED[n], "place_" + n) for n in first}
    hooks = _Overlapped([placed[n] for n in first], [SHARDED[n] for n in first], where)
    for n in order:
        if n not in first:
            placed[n] = _place_shard(shards[n], piece_idx, SHARDED[n], "place_" + n, after=hooks.first_started)
    placed["ffn_conv_w"] = _place_shard(conv_w, piece_idx, False, "place_ffn_conv_w", out_dtype=F32,
                                        after=hooks.first_started)
    w = dict(zip(first, hooks.start_rest([placed[n] for n in MIXER_LATER], [placed[n] for n in FFN_GATHER])))
    groups = p["ssm_lambda_re"].shape[1]
    w.update({
        "attn_norm_w": p["attn_norm_w"] + hooks.gather_started[:1, :1],
        "ssm_lambda_re": p["ssm_lambda_re"][0], "ssm_lambda_im": p["ssm_lambda_im"][0],
        "ssm_log_dt": p["ssm_log_dt"].reshape(groups, 1), "ssm_b_re": as_stored("ssm_b_re", p["ssm_b_re"])[0],
        "ssm_b_im": as_stored("ssm_b_im", p["ssm_b_im"])[0], "ssm_c_re": p["ssm_c_re"][0], "ssm_c_im": p["ssm_c_im"][0],
        "ssm_d": p["ssm_d"], "ssm_b_glu": p["ssm_b_glu"], "mla_q_norm_w": p["mla_q_norm_w"],
        "mla_kv_norm_w": p["mla_kv_norm_w"], "ssm_out_norm_w": p["ssm_out_norm_w"], "mla_out_norm_w": p["mla_out_norm_w"],
        "ffn_norm_w": p["ffn_norm_w"], "ffn_conv_b": p["ffn_conv_b"], "final_norm_w": p["final_norm_w"].reshape(1, -1),
    })

    loss_tile, dx, g = _local_step(x, positions.reshape(rows, 1).astype(F32), target, w, hooks)

    flat = [g[n].reshape(-1) for n in SMALL] + [g["ffn_conv_w"].reshape(-1), loss_tile[0, :1]]
    sizes = [f.shape[0] for f in flat]
    per_block = -(-sum(sizes) // (N_CORES * N_CHIPS * SMALL_COLS))
    small_rows = -(-per_block // (2 * SUBLANES)) * (2 * SUBLANES)
    padded = N_CORES * N_CHIPS * small_rows * SMALL_COLS

    def pack(parts):
        parts = list(parts)
        have = sum(q.shape[0] for q in parts)
        return jnp.concatenate(parts + [jnp.zeros((padded - have,), F32)])

    reduced = mixer + ["small"]
    small = pack(flat).reshape(N_CORES, N_CHIPS, small_rows, SMALL_COLS)
    give = [g[n](hooks.other, suffix="_give") for n in mixer] + [lax.dynamic_index_in_dim(small, 1 - ci, 0, keepdims=False)]
    lands = [lax.empty(a.shape, a.dtype) for a in give]
    give_plan = _give_plan(len(reduced))
    swap = _start_copies("grad_mixer_swap_start", give + lands, give_plan, len(reduced), dx)

    grads, delta, new_m, new_v = {}, {}, {}, {}

    def finish(n, joined, after=None):
        grad = joined if SHARDED[n] else joined.reshape(-1, joined.shape[2])
        if n == "w_in":
            wt, mt, vt = w_in, transposed(args["m_w_in"]), transposed(args["v_w_in"])
            out = _adamw(wt, grad, mt, vt, "adamw_w_in")
            delta[n], new_m[n], new_v[n], grads[n] = (jnp.swapaxes(a, 0, 1)[None] for a in out)
            return
        if n == "mla_w_uq":
            grad = _unpad_heads(grad, heads_here)
        adam(n, grad, after)

    def adam(n, grad, after=None):
        shape = p[n].shape
        out = _adamw(p[n].reshape(shape[1:]), grad, args["m_" + n].reshape(shape[1:]),
                     args["v_" + n].reshape(shape[1:]), "adamw_" + n, after)
        delta[n], new_m[n], new_v[n], grads[n] = (a.reshape(shape) for a in out)

    ffn_halves = hooks.ffn_reduced(swap[3])
    got = _wait_copies("grad_mixer_swap_wait", swap, give_plan, ffn_halves[-1])[len(reduced):]
    join_plan = _join_plan(len(FFN))
    ffn_join = _start_copies("grad_ffn_join_start", ffn_halves, join_plan, len(FFN), got[0])
    big = [t for t, n in enumerate(reduced) if n in MIXER_BIG]
    rest = [t for t in range(len(reduced)) if t not in big]
    sums, wires = {}, {}
    for t in big:
        sums[t], wires[t] = g[reduced[t]](hooks.mine, suffix="_keep", add=got[t], wire=True)
    ffn_joined = _wait_copies("grad_ffn_join_wait", ffn_join, join_plan, sums[big[-1]])

    def scatter_start(name, group, after):
        lands = [lax.empty(s.shape, s.dtype) for s in _scatter_shapes([wires[t] for t in group])]
        return _start_copies(name, [wires[t] for t in group] + lands, _scatter_plan(len(group)), 3 * len(group), after)

    scatter_big = scatter_start("grad_big_scatter_start", big, ffn_joined[0])
    for t in rest[:-1]:
        sums[t], wires[t] = g[reduced[t]](hooks.mine, suffix="_keep", add=got[t], wire=True, after=scatter_big[3])
    sums[rest[-1]] = wires[rest[-1]] = _add_other_half(small, got[-1], where, "grad_add_half_small")
    scatter_rest = scatter_start("grad_rest_scatter_start", rest, sums[rest[0]])
    behind = scatter_rest[3]
    for n, joined in zip(FFN, ffn_joined):
        finish(n, joined, after=behind)
        behind = delta[n]
    got_pieces = dict(zip(big, _wait_copies("grad_big_scatter_wait", scatter_big, _scatter_plan(len(big)),
                                            delta[FFN[-1]])[len(big):]))
    got_pieces.update(zip(rest, _wait_copies("grad_rest_scatter_wait", scatter_rest, _scatter_plan(len(rest)),
                                             got_pieces[big[0]])[len(rest):]))
    halves = [_add_pieces(sums[t], got_pieces[t], where, "grad_add_pieces_" + n) for t, n in enumerate(reduced)]
    joined = _join_halves(halves, "grad_join_halves")
    for n, j in zip(mixer, joined):
        finish(n, j)
    eighths = _place_shard(joined[-1].reshape(N_CORES * small_rows, SMALL_COLS), piece_idx, True, "place_small_grads",
                           out_dtype=F32)
    small_sum = _gather_weights([(eighths, True, False)], "gather_small_grads")[0]
    flat_sum = small_sum.reshape(N_CHIPS, N_CORES, small_rows * SMALL_COLS).transpose(1, 0, 2).reshape(-1)
    offs = [0]
    for s in sizes:
        offs.append(offs[-1] + s)
    stored = {n: as_stored(n, p[n]) for n in SMALL}
    for k, n in enumerate(SMALL):
        grads[n] = flat_sum[offs[k]:offs[k + 1]].reshape(stored[n].shape)
    taps, cols_here = p["ffn_conv_w"].shape[1], p["ffn_conv_w"].shape[2]
    conv_full = flat_sum[offs[len(SMALL)]:offs[len(SMALL) + 1]].reshape(taps, N_CHIPS * cols_here)
    adam("ffn_conv_w", lax.dynamic_slice_in_dim(conv_full, piece * cols_here, cols_here, axis=1))
    loss = flat_sum[offs[len(SMALL) + 1]]

    def rank2(a):
        return a.reshape(1, -1) if a.ndim == 1 else a

    d_s, m_s, v_s = _adamw_many([rank2(stored[n]) for n in SMALL], [rank2(grads[n]) for n in SMALL],
                                [rank2(as_stored(n, args["m_" + n])) for n in SMALL],
                                [rank2(as_stored(n, args["v_" + n])) for n in SMALL], "adamw_small")
    for k, n in enumerate(SMALL):
        delta[n], new_m[n], new_v[n], grads[n] = (as_stored(n, a.reshape(stored[n].shape))
                                                  for a in (d_s[k], m_s[k], v_s[k], grads[n]))

    return (loss, dx[None], *[grads[n] for n in WEIGHTS], *[delta[n] for n in WEIGHTS],
            *[new_m[n] for n in WEIGHTS], *[new_v[n] for n in WEIGHTS])


def kernel(x, positions, attn_norm_w, w_in, ssm_lambda_re, ssm_lambda_im, ssm_log_dt, ssm_b_re, ssm_b_im, ssm_c_re, ssm_c_im, ssm_d, ssm_w_glu, ssm_b_glu, mla_q_norm_w, mla_w_uq, mla_kv_norm_w, mla_w_ukv, ssm_out_norm_w, mla_out_norm_w, w_out, ffn_norm_w, ffn_w_up, ffn_conv_w, ffn_conv_b, ffn_w_down, final_norm_w, loss_target, m_attn_norm_w, m_w_in, m_ssm_lambda_re, m_ssm_lambda_im, m_ssm_log_dt, m_ssm_b_re, m_ssm_b_im, m_ssm_c_re, m_ssm_c_im, m_ssm_d, m_ssm_w_glu, m_ssm_b_glu, m_mla_q_norm_w, m_mla_w_uq, m_mla_kv_norm_w, m_mla_w_ukv, m_ssm_out_norm_w, m_mla_out_norm_w, m_w_out, m_ffn_norm_w, m_ffn_w_up, m_ffn_conv_w, m_ffn_conv_b, m_ffn_w_down, m_final_norm_w, v_attn_norm_w, v_w_in, v_ssm_lambda_re, v_ssm_lambda_im, v_ssm_log_dt, v_ssm_b_re, v_ssm_b_im, v_ssm_c_re, v_ssm_c_im, v_ssm_d, v_ssm_w_glu, v_ssm_b_glu, v_mla_q_norm_w, v_mla_w_uq, v_mla_kv_norm_w, v_mla_w_ukv, v_ssm_out_norm_w, v_mla_out_norm_w, v_w_out, v_ffn_norm_w, v_ffn_w_up, v_ffn_conv_w, v_ffn_conv_b, v_ffn_w_down, v_final_norm_w):
    return _step(dict(locals()))
```

```python
import functools
import math

import jax
import jax.numpy as jnp
from jax import lax
from jax.experimental import pallas as pl
from jax.experimental.pallas import tpu as pltpu

F32 = jnp.float32
BF16 = jnp.bfloat16

SSM_GROUP = 16
SSM_STATE = 64
QK_NOPE_DIM = 128
QK_ROPE_DIM = 64
ROPE_THETA = 10000.0
RMS_EPS = 1e-6
ADAM_LR, ADAM_B1, ADAM_B2, ADAM_EPS, ADAM_WD, ADAM_STEP = 0.001, 0.9, 0.999, 1e-08, 0.01, 10

LANES = 128
SUBLANES = 8
VMEM_LIMIT_BYTES = 56 * 1024 * 1024

GROUPS_PER_BATCH = LANES // SSM_GROUP
STATE_PER_BATCH = GROUPS_PER_BATCH * SSM_STATE
HEAD_SLOT = 2 * LANES
NEG_INF = -1e30
ATTN_BLOCK = 512
FFN_ROWS = 1024

N_CHIPS = 4
N_CORES = 2


def _tile(n, pref, align=LANES):
    if n <= pref:
        return n
    t = (pref // align) * align
    while t >= align:
        if n % t == 0:
            return t
        t -= align
    return n


def _params(sem):
    return pltpu.CompilerParams(dimension_semantics=sem, vmem_limit_bytes=VMEM_LIMIT_BYTES)


def _dot(a, b, dims):
    return lax.dot_general(a, b, (dims, ((), ())), preferred_element_type=F32)


def _dot_nn(a, b):
    return _dot(a, b, ((1,), (0,)))


def _dot_nt(a, b):
    return _dot(a, b, ((1,), (1,)))


def _dot_tn(a, b):
    return _dot(a, b, ((0,), (0,)))


def _matmul(a, b, *, mode, name, tm=512, tn=1024, tk=2048, bias=None, add=None, out_dtype=F32,
            a_split=False, b_split=False, after=None):
    if a_split:
        assert mode == "nt"
        a_shape = (a.shape[1], 2 * a.shape[2])
    else:
        a_shape = a.shape
    if b_split:
        assert mode == "tn"
        b_shape = (b.shape[1], 2 * b.shape[2])
    else:
        b_shape = b.shape
    if mode == "nn":
        (m, k), (k2, n) = a_shape, b_shape
    elif mode == "nt":
        (m, k), (n, k2) = a_shape, b_shape
    else:
        (k, m), (k2, n) = a_shape, b_shape
    assert k == k2, (a.shape, b.shape, mode)
    tm, tn, tk = _tile(m, tm, SUBLANES), _tile(n, tn), _tile(k, tk)
    nk = k // tk
    a_spec = {"nn": pl.BlockSpec((tm, tk), lambda i, j, kk: (i, kk)),
              "nt": pl.BlockSpec((tm, tk), lambda i, j, kk: (i, kk)),
              "tn": pl.BlockSpec((tk, tm), lambda i, j, kk: (kk, i))}[mode]
    b_spec = {"nn": pl.BlockSpec((tk, tn), lambda i, j, kk: (kk, j)),
              "nt": pl.BlockSpec((tn, tk), lambda i, j, kk: (j, kk)),
              "tn": pl.BlockSpec((tk, tn), lambda i, j, kk: (kk, j))}[mode]
    if a_split:
        kb = a.shape[2] // tk
        assert a.shape[2] % tk == 0
        a_spec = pl.BlockSpec((None, tm, tk), lambda i, j, kk: (kk // kb, i, kk % kb))
    if b_split:
        nb = b.shape[2] // tn
        assert b.shape[2] % tn == 0
        b_spec = pl.BlockSpec((None, tk, tn), lambda i, j, kk: (j // nb, kk, j % nb))
    dot = {"nn": _dot_nn, "nt": _dot_nt, "tn": _dot_tn}[mode]
    in_specs, operands = [a_spec, b_spec], [a, b]
    if bias is not None:
        in_specs.append(pl.BlockSpec((1, tn), lambda i, j, kk: (0, j)))
        operands.append(bias)
    if add is not None:
        in_specs.append(pl.BlockSpec((tm, tn), lambda i, j, kk: (i, j)))
        operands.append(add)
    if after is not None:
        in_specs.append(pl.BlockSpec(memory_space=pl.ANY))
        operands.append(after)

    def body(*refs):
        a_ref, b_ref = refs[0], refs[1]
        rest = list(refs[2:])
        bias_ref = rest.pop(0) if bias is not None else None
        add_ref = rest.pop(0) if add is not None else None
        if after is not None:
            rest.pop(0)
        o_ref, acc_ref = rest

        def finish(acc):
            if bias_ref is not None:
                acc = acc + bias_ref[...]
            if add_ref is not None:
                acc = acc + add_ref[...]
            o_ref[...] = acc.astype(o_ref.dtype)

        part = dot(a_ref[...].astype(BF16), b_ref[...].astype(BF16))
        if nk == 1:
            finish(part)
        else:
            kk = pl.program_id(2)

            @pl.when(kk == 0)
            def _():
                acc_ref[...] = part

            @pl.when(jnp.logical_and(kk > 0, kk < nk - 1))
            def _():
                acc_ref[...] += part

            @pl.when(kk == nk - 1)
            def _():
                finish(acc_ref[...] + part)

    out_shape = jax.ShapeDtypeStruct((m, n), out_dtype)
    out_spec = pl.BlockSpec((tm, tn), lambda i, j, kk: (i, j))
    acc_shape = (tm, tn) if nk > 1 else (SUBLANES, LANES)
    return pl.pallas_call(
        body, name=name, grid=(m // tm, n // tn, nk), in_specs=in_specs, out_specs=out_spec, out_shape=out_shape,
        scratch_shapes=[pltpu.VMEM(acc_shape, F32)],
        compiler_params=_params(("parallel", "parallel", "arbitrary")),
    )(*operands)


def _wgrad_half(a, b, rows, cols, row_sharded, name, which, *, suffix="", add=None, wire=False, tm=None, tn=None,
                b_split=False, after=None):
    tokens = a.shape[0]
    if row_sharded:
        sr, sc = rows // N_CHIPS, cols // N_CORES
    else:
        sr, sc = rows // N_CORES, cols // N_CHIPS
    tm = _tile(sr, 512) if tm is None else tm
    tn = _tile(sc, 1024) if tn is None else tn
    assert sr % tm == 0 and sc % tn == 0, (rows, cols, tm, tn)
    rb, cb = sr // tm, sc // tn
    if tn >= tm:
        ij, grid = (lambda s, t: (t, s)), (N_CHIPS, cb, rb)
    else:
        ij, grid = (lambda s, t: (s, t)), (N_CHIPS, rb, cb)
    if row_sharded:
        a_tile = lambda p, i, j, h: p * rb + i
        b_tile = lambda p, i, j, h: h[0] * cb + j
    else:
        a_tile = lambda p, i, j, h: h[0] * rb + i
        b_tile = lambda p, i, j, h: p * cb + j
    a_spec = pl.BlockSpec((tokens, tm), lambda p, s, t, h: (0, a_tile(p, *ij(s, t), h)))
    if b_split:
        nbh = b.shape[2] // tn
        assert b.shape[2] % tn == 0
        b_spec = pl.BlockSpec((None, tokens, tn), lambda p, s, t, h: (b_tile(p, *ij(s, t), h) // nbh, 0,
                                                                       b_tile(p, *ij(s, t), h) % nbh))
    else:
        b_spec = pl.BlockSpec((tokens, tn), lambda p, s, t, h: (0, b_tile(p, *ij(s, t), h)))
    out_spec = pl.BlockSpec((None, tm, tn), lambda p, s, t, h: (p, *ij(s, t)))
    in_specs, operands = [a_spec, b_spec], [a, b]
    if add is not None:
        in_specs.append(out_spec)
        operands.append(add)
    if after is not None:
        in_specs.append(pl.BlockSpec(memory_space=pl.ANY))
        operands.append(after)
    out_dtypes = [F32, BF16] if wire else [F32]

    def body(h_ref, a_ref, b_ref, *rest):
        acc = _dot_tn(a_ref[...].astype(BF16), b_ref[...].astype(BF16))
        if add is not None:
            acc = acc + rest[0][...]
        for o_ref in rest[-len(out_dtypes):]:
            o_ref[...] = acc.astype(o_ref.dtype)

    out = pl.pallas_call(
        body, name=name + suffix, out_shape=[jax.ShapeDtypeStruct((N_CHIPS, sr, sc), dt) for dt in out_dtypes],
        grid_spec=pltpu.PrefetchScalarGridSpec(num_scalar_prefetch=1, grid=grid, in_specs=in_specs,
                                               out_specs=[out_spec] * len(out_dtypes)),
        compiler_params=_params(("parallel", "parallel", "parallel")),
    )(which, *operands)
    return tuple(out) if wire else out[0]


def _rms_rows(x):
    return lax.rsqrt(jnp.mean(x * x, axis=-1, keepdims=True) + RMS_EPS)


def _rmsnorm_fwd(x, w, *, name, width=None, col=0, out_dtype=BF16, tr=256):
    rows = x.shape[0]
    width = x.shape[1] if width is None else width
    tr = _tile(rows, tr, SUBLANES)

    def body(x_ref, w_ref, o_ref):
        xv = x_ref[...]
        o_ref[...] = (xv * _rms_rows(xv) * w_ref[...]).astype(o_ref.dtype)

    return pl.pallas_call(
        body, name=name, grid=(rows // tr,),
        in_specs=[pl.BlockSpec((tr, width), lambda i: (i, col)), pl.BlockSpec((1, width), lambda i: (0, 0))],
        out_specs=pl.BlockSpec((tr, width), lambda i: (i, 0)),
        out_shape=jax.ShapeDtypeStruct((rows, width), out_dtype),
        compiler_params=_params(("parallel",)),
    )(x, w)


def _rmsnorm_bwd_rows(xv, w, dy):
    r = _rms_rows(xv)
    n = xv * r
    dn = dy * w
    dx = r * (dn - n * jnp.mean(dn * n, axis=-1, keepdims=True))
    return dx, dy * n


def _rmsnorm_bwd(x, w, dy, *, name, width=None, col=0, dy_col=0, add=None, tr=256, dx_dtypes=(F32,)):
    rows = x.shape[0]
    n_dx = len(dx_dtypes)
    width = x.shape[1] if width is None else width
    tr = _tile(rows, tr, SUBLANES)
    in_specs = [pl.BlockSpec((tr, width), lambda i: (i, col)), pl.BlockSpec((1, width), lambda i: (0, 0)),
                pl.BlockSpec((tr, width), lambda i: (i, dy_col))]
    operands = [x, w, dy]
    if add is not None:
        in_specs.append(pl.BlockSpec((tr, width), lambda i: (i, 0)))
        operands.append(add)

    def body(*refs):
        x_ref, w_ref, dy_ref = refs[:3]
        add_ref = refs[3] if add is not None else None
        dx_refs, dw_ref = refs[-1 - n_dx:-1], refs[-1]
        dx, dwp = _rmsnorm_bwd_rows(x_ref[...], w_ref[...], dy_ref[...])
        if add_ref is not None:
            dx = dx + add_ref[...]
        for dx_ref in dx_refs:
            dx_ref[...] = dx.astype(dx_ref.dtype)
        part = jnp.sum(dwp, axis=0, keepdims=True)

        @pl.when(pl.program_id(0) == 0)
        def _():
            dw_ref[...] = part

        @pl.when(pl.program_id(0) > 0)
        def _():
            dw_ref[...] += part

    return pl.pallas_call(
        body, name=name, grid=(rows // tr,), in_specs=in_specs,
        out_specs=[pl.BlockSpec((tr, width), lambda i: (i, 0))] * n_dx + [pl.BlockSpec((1, width), lambda i: (0, 0))],
        out_shape=[jax.ShapeDtypeStruct((rows, width), dt) for dt in dx_dtypes] + [jax.ShapeDtypeStruct((1, width), F32)],
        compiler_params=_params(("arbitrary",)),
    )(*operands)


def _final_norm_loss(h, w, target, *, tr=256):
    rows, d = h.shape
    tr = _tile(rows, tr, SUBLANES)

    def body(h_ref, w_ref, t_ref, loss_ref, dh_ref, dhb_ref, dw_ref):
        hv, wv = h_ref[...], w_ref[...]
        r = _rms_rows(hv)
        n = hv * r
        err = n * wv - t_ref[...]
        d_out = err * (1.0 / d)
        dn = d_out * wv
        dh = r * (dn - n * jnp.mean(dn * n, axis=-1, keepdims=True))
        dh_ref[...] = dh
        dhb_ref[...] = dh.astype(BF16)
        dw_part = jnp.sum(d_out * n, axis=0, keepdims=True)
        loss_part = jnp.full((SUBLANES, LANES), 0.5 / d, F32) * jnp.sum(err * err)

        @pl.when(pl.program_id(0) == 0)
        def _():
            dw_ref[...] = dw_part
            loss_ref[...] = loss_part

        @pl.when(pl.program_id(0) > 0)
        def _():
            dw_ref[...] += dw_part
            loss_ref[...] += loss_part

    return pl.pallas_call(
        body, name="final_norm_loss", grid=(rows // tr,),
        in_specs=[pl.BlockSpec((tr, d), lambda i: (i, 0)), pl.BlockSpec((1, d), lambda i: (0, 0)),
                  pl.BlockSpec((tr, d), lambda i: (i, 0))],
        out_specs=[pl.BlockSpec((SUBLANES, LANES), lambda i: (0, 0)), pl.BlockSpec((tr, d), lambda i: (i, 0)),
                   pl.BlockSpec((tr, d), lambda i: (i, 0)), pl.BlockSpec((1, d), lambda i: (0, 0))],
        out_shape=[jax.ShapeDtypeStruct((SUBLANES, LANES), F32), jax.ShapeDtypeStruct((rows, d), F32),
                   jax.ShapeDtypeStruct((rows, d), BF16), jax.ShapeDtypeStruct((1, d), F32)],
        compiler_params=_params(("arbitrary",)),
    )(h, w, target)


def _cmul(ar, ai, br, bi):
    return ar * br - ai * bi, ar * bi + ai * br


def _dot_exact(a, b, dims):
    return lax.dot_general(a, b, (dims, ((), ())), preferred_element_type=F32, precision=lax.Precision.HIGHEST)


def _s5_discretize(lr, li, dt):
    mag = jnp.exp(lr * dt)
    th = li * dt
    ar, ai = mag * jnp.cos(th), mag * jnp.sin(th)
    nr, ni = ar - 1.0, ai
    den = lr * lr + li * li
    zr = (nr * lr + ni * li) / den
    zi = (ni * lr - nr * li) / den
    return mag, ar, ai, nr, ni, den, zr, zi


def _band_slices(group):
    j, gi = divmod(group, GROUPS_PER_BATCH)
    rows = slice(gi * SSM_GROUP, (gi + 1) * SSM_GROUP)
    re = slice(gi * SSM_STATE, (gi + 1) * SSM_STATE)
    im = slice(STATE_PER_BATCH + gi * SSM_STATE, STATE_PER_BATCH + (gi + 1) * SSM_STATE)
    return j, rows, re, im


def _s5_bands(lam_re, lam_im, log_dt, b_re, b_im, c_re, c_im):
    g, _ = lam_re.shape
    nb = g // GROUPS_PER_BATCH
    s2 = 2 * STATE_PER_BATCH

    def body(lr_ref, li_ref, ldt_ref, br_ref, bi_ref, cr_ref, ci_ref, wb_ref, wct_ref, a_ref):
        dt = jnp.exp(ldt_ref[...])
        _, ar, ai, _, _, _, zr, zi = _s5_discretize(lr_ref[...], li_ref[...], dt)
        wb_ref[...] = jnp.zeros_like(wb_ref)
        wct_ref[...] = jnp.zeros_like(wct_ref)
        for group in range(g):
            j, rows, re, im = _band_slices(group)
            zr_g, zi_g = zr[group:group + 1, :], zi[group:group + 1, :]
            bre, bim = br_ref[group], bi_ref[group]
            wb_ref[j, rows, re] = (zr_g * bre - zi_g * bim).astype(BF16)
            wb_ref[j, rows, im] = (zr_g * bim + zi_g * bre).astype(BF16)
            wct_ref[j, rows, re] = cr_ref[group].astype(BF16)
            wct_ref[j, rows, im] = (-ci_ref[group]).astype(BF16)
            a_ref[j, :, re] = ar[group:group + 1, :]
            a_ref[j, :, im] = ai[group:group + 1, :]

    return pl.pallas_call(
        body, name="s5_bands",
        out_shape=[jax.ShapeDtypeStruct((nb, LANES, s2), BF16)] * 2 + [jax.ShapeDtypeStruct((nb, 1, s2), F32)],
    )(lam_re, lam_im, log_dt, b_re, b_im, c_re, c_im)


def _s5_bands_bwd(lam_re, lam_im, log_dt, b_re, b_im, dwb, dwct, dabar):
    g, p = lam_re.shape
    gh = b_re.shape[1:]

    def body(lr_ref, li_ref, ldt_ref, br_ref, bi_ref, dwb_ref, dwct_ref, da_ref,
             dlr_ref, dli_ref, dldt_ref, dbre_ref, dbim_ref, dcre_ref, dcim_ref, dzr_ref, dzi_ref, dar_ref, dai_ref):
        lr, li = lr_ref[...], li_ref[...]
        dt = jnp.exp(ldt_ref[...])
        mag, ar, ai, nr, ni, den, zr, zi = _s5_discretize(lr, li, dt)
        for group in range(g):
            j, rows, re, im = _band_slices(group)
            zr_g, zi_g = zr[group:group + 1, :], zi[group:group + 1, :]
            bre, bim = br_ref[group], bi_ref[group]
            dbr, dbi = dwb_ref[j, rows, re], dwb_ref[j, rows, im]
            dbre_ref[group] = zr_g * dbr + zi_g * dbi
            dbim_ref[group] = zr_g * dbi - zi_g * dbr
            dzr_ref[group:group + 1, :] = jnp.sum(bre * dbr + bim * dbi, axis=0, keepdims=True)
            dzi_ref[group:group + 1, :] = jnp.sum(bre * dbi - bim * dbr, axis=0, keepdims=True)
            dcre_ref[group] = dwct_ref[j, rows, re]
            dcim_ref[group] = -dwct_ref[j, rows, im]
            dar_ref[group:group + 1, :] = da_ref[j, :, re]
            dai_ref[group:group + 1, :] = da_ref[j, :, im]
        dzr, dzi = dzr_ref[...], dzi_ref[...]
        inv = 1.0 / den
        d_nr = (dzr * lr - dzi * li) * inv
        d_ni = (dzr * li + dzi * lr) * inv
        d_den = -(dzr * zr + dzi * zi) * inv
        d_lr = (dzr * nr + dzi * ni) * inv + 2.0 * lr * d_den
        d_li = (dzr * ni - dzi * nr) * inv + 2.0 * li * d_den
        t_ar = dar_ref[...] + d_nr
        t_ai = dai_ref[...] + d_ni
        d_lrdt = t_ar * ar + t_ai * ai
        d_th = t_ai * ar - t_ar * ai
        dlr_ref[...] = d_lr + d_lrdt * dt
        dli_ref[...] = d_li + d_th * dt
        dldt_ref[...] = jnp.sum(d_lrdt * lr + d_th * li, axis=1, keepdims=True) * dt

    return pl.pallas_call(
        body, name="s5_bands_bwd",
        out_shape=[jax.ShapeDtypeStruct((g, p), F32)] * 2 + [jax.ShapeDtypeStruct((g, 1), F32)]
        + [jax.ShapeDtypeStruct((g,) + gh, F32)] * 4,
        scratch_shapes=[pltpu.VMEM((g, p), F32)] * 4,
    )(lam_re, lam_im, log_dt, b_re, b_im, dwb, dwct, dabar)


def _powers(ar, ai, count):
    out = [(ar, ai)]
    for _ in range(count - 1):
        out.append(_cmul(out[-1][0], out[-1][1], ar, ai))
    return out


def _scan_coefs(ar, ai, reverse):
    w = ar.shape[-1]
    pw = _powers(ar, ai, SUBLANES)
    row = lax.broadcasted_iota(jnp.int32, (SUBLANES, w), 0)
    steps = []
    d = 1
    while d < SUBLANES:
        keep = (row < SUBLANES - d) if reverse else (row >= d)
        pr, pi = pw[d - 1]
        steps.append((d, jnp.where(keep, pr, 0.0), jnp.where(keep, pi, 0.0)))
        d *= 2
    cr = jnp.zeros((SUBLANES, w), F32)
    ci = jnp.zeros((SUBLANES, w), F32)
    for t in range(SUBLANES):
        pr, pi = pw[SUBLANES - 1 - t] if reverse else pw[t]
        cr = jnp.where(row == t, pr, cr)
        ci = jnp.where(row == t, pi, ci)
    return steps, cr, ci


def _scan_tile(xr, xi, carry_r, carry_i, coefs, reverse):
    steps, cr, ci = coefs
    for d, mr, mi in steps:
        shift = SUBLANES - d if reverse else d
        sr, si = pltpu.roll(xr, shift, 0), pltpu.roll(xi, shift, 0)
        pr, pi = _cmul(mr, mi, sr, si)
        xr, xi = xr + pr, xi + pi
    pr, pi = _cmul(cr, ci, carry_r, carry_i)
    return xr + pr, xi + pi


def _gelu(x):
    c = math.sqrt(2.0 / math.pi)
    return 0.5 * x * (1.0 + jnp.tanh(c * (x + 0.044715 * x * x * x)))


def _gelu_grad(x):
    c = math.sqrt(2.0 / math.pi)
    t = jnp.tanh(c * (x + 0.044715 * x * x * x))
    return 0.5 * (1.0 + t) + 0.5 * x * (1.0 - t * t) * c * (1.0 + 3.0 * 0.044715 * x * x)


def _s5_fwd(proj, wb, wct, d_skip, abar):
    rows = proj.shape[0]
    nb = wb.shape[0]
    s2 = 2 * STATE_PER_BATCH
    st = STATE_PER_BATCH
    chunk = _tile(rows, 512, SUBLANES)

    def body(u_ref, wb_ref, wc_ref, d_ref, a_ref, s_ref, y_ref, yg_ref):
        for c0 in range(0, rows, chunk):
            s_ref[pl.ds(c0, chunk), :] = _dot_nn(u_ref[pl.ds(c0, chunk), :].astype(BF16), wb_ref[...])
        av = a_ref[...]
        coefs = _scan_coefs(av[:, :st], av[:, st:], reverse=False)

        def tile(b, carry):
            r0 = pl.multiple_of(b * SUBLANES, SUBLANES)
            xr, xi = _scan_tile(s_ref[pl.ds(r0, SUBLANES), :st], s_ref[pl.ds(r0, SUBLANES), st:], carry[0], carry[1],
                                coefs, False)
            s_ref[pl.ds(r0, SUBLANES), :st] = xr
            s_ref[pl.ds(r0, SUBLANES), st:] = xi
            return xr[SUBLANES - 1:, :], xi[SUBLANES - 1:, :]

        zero = jnp.zeros((1, st), F32)
        lax.fori_loop(0, rows // SUBLANES, tile, (zero, zero))
        for c0 in range(0, rows, chunk):
            y = _dot_nt(s_ref[pl.ds(c0, chunk), :].astype(BF16), wc_ref[...]) + d_ref[...] * u_ref[pl.ds(c0, chunk), :]
            y_ref[pl.ds(c0, chunk), :] = y
            yg_ref[pl.ds(c0, chunk), :] = _gelu(y).astype(BF16)

    return pl.pallas_call(
        body, name="s5_fwd", grid=(nb,),
        in_specs=[pl.BlockSpec((rows, LANES), lambda j: (0, j)), pl.BlockSpec((None, LANES, s2), lambda j: (j, 0, 0)),
                  pl.BlockSpec((None, LANES, s2), lambda j: (j, 0, 0)), pl.BlockSpec((1, LANES), lambda j: (0, j)),
                  pl.BlockSpec((None, 1, s2), lambda j: (j, 0, 0))],
        out_specs=[pl.BlockSpec((rows, s2), lambda j: (0, j)), pl.BlockSpec((rows, LANES), lambda j: (0, j)),
                   pl.BlockSpec((rows, LANES), lambda j: (0, j))],
        out_shape=[jax.ShapeDtypeStruct((rows, nb * s2), F32), jax.ShapeDtypeStruct((rows, nb * LANES), F32),
                   jax.ShapeDtypeStruct((rows, nb * LANES), BF16)],
        compiler_params=_params(("parallel",)),
    )(proj, wb, wct, d_skip, abar)


def _s5_bwd(proj, states, y_pre, dyg_a, dyg_b, wb, wct, d_skip, abar):
    rows = proj.shape[0]
    nb = wb.shape[0]
    s2 = 2 * STATE_PER_BATCH
    st = STATE_PER_BATCH
    chunk = _tile(rows, 512, SUBLANES)
    n_tiles = rows // SUBLANES

    def body(u_ref, s_ref, y_ref, ga_ref, gb_ref, wb_ref, wc_ref, d_ref, a_ref,
             du_ref, dwb_ref, dwc_ref, da_ref, dd_ref, ds_ref, dy_ref):
        dy_ref[...] = (ga_ref[...] + gb_ref[...]) * _gelu_grad(y_ref[...])
        dd_ref[...] = jnp.sum(dy_ref[...] * u_ref[...], axis=0, keepdims=True)
        for c0 in range(0, rows, chunk):
            ds_ref[pl.ds(c0, chunk), :] = _dot_nn(dy_ref[pl.ds(c0, chunk), :].astype(BF16), wc_ref[...])
        dwc_ref[...] = _dot_tn(dy_ref[...].astype(BF16), s_ref[...].astype(BF16))
        av = a_ref[...]
        coefs = _scan_coefs(av[:, :st], -av[:, st:], reverse=True)
        row = lax.broadcasted_iota(jnp.int32, (SUBLANES, st), 0)

        def tile(k, carry):
            cr, ci, acc_r, acc_i = carry
            b = n_tiles - 1 - k
            r0 = pl.multiple_of(b * SUBLANES, SUBLANES)
            rp = pl.multiple_of(jnp.maximum(b - 1, 0) * SUBLANES, SUBLANES)
            xr, xi = _scan_tile(ds_ref[pl.ds(r0, SUBLANES), :st], ds_ref[pl.ds(r0, SUBLANES), st:], cr, ci, coefs, True)
            ds_ref[pl.ds(r0, SUBLANES), :st] = xr
            ds_ref[pl.ds(r0, SUBLANES), st:] = xi
            first = jnp.where(b > 0, 1.0, 0.0)
            pr = jnp.where(row == 0, pltpu.roll(s_ref[pl.ds(rp, SUBLANES), :st], 1, 0) * first,
                           pltpu.roll(s_ref[pl.ds(r0, SUBLANES), :st], 1, 0))
            pi = jnp.where(row == 0, pltpu.roll(s_ref[pl.ds(rp, SUBLANES), st:], 1, 0) * first,
                           pltpu.roll(s_ref[pl.ds(r0, SUBLANES), st:], 1, 0))
            acc_r = acc_r + pr * xr + pi * xi
            acc_i = acc_i + pr * xi - pi * xr
            return xr[:1, :], xi[:1, :], acc_r, acc_i

        zero = jnp.zeros((1, st), F32)
        zacc = jnp.zeros((SUBLANES, st), F32)
        _, _, acc_r, acc_i = lax.fori_loop(0, n_tiles, tile, (zero, zero, zacc, zacc))
        da_ref[:, :st] = jnp.sum(acc_r, axis=0, keepdims=True)
        da_ref[:, st:] = jnp.sum(acc_i, axis=0, keepdims=True)
        for c0 in range(0, rows, chunk):
            du_ref[pl.ds(c0, chunk), :] = (_dot_nt(ds_ref[pl.ds(c0, chunk), :].astype(BF16), wb_ref[...])
                                           + d_ref[...] * dy_ref[pl.ds(c0, chunk), :]).astype(du_ref.dtype)
        dwb_ref[...] = _dot_tn(u_ref[...].astype(BF16), ds_ref[...].astype(BF16))

    col = pl.BlockSpec((rows, LANES), lambda j: (0, j))
    return pl.pallas_call(
        body, name="s5_bwd", grid=(nb,),
        in_specs=[col, pl.BlockSpec((rows, s2), lambda j: (0, j)), col, col, col,
                  pl.BlockSpec((None, LANES, s2), lambda j: (j, 0, 0)), pl.BlockSpec((None, LANES, s2), lambda j: (j, 0, 0)),
                  pl.BlockSpec((1, LANES), lambda j: (0, j)), pl.BlockSpec((None, 1, s2), lambda j: (j, 0, 0))],
        out_specs=[col, pl.BlockSpec((None, LANES, s2), lambda j: (j, 0, 0)),
                   pl.BlockSpec((None, LANES, s2), lambda j: (j, 0, 0)), pl.BlockSpec((None, 1, s2), lambda j: (j, 0, 0)),
                   pl.BlockSpec((1, LANES), lambda j: (0, j))],
        out_shape=[jax.ShapeDtypeStruct((rows, nb * LANES), BF16), jax.ShapeDtypeStruct((nb, LANES, s2), F32),
                   jax.ShapeDtypeStruct((nb, LANES, s2), F32), jax.ShapeDtypeStruct((nb, 1, s2), F32),
                   jax.ShapeDtypeStruct((1, nb * LANES), F32)],
        scratch_shapes=[pltpu.VMEM((rows, s2), F32), pltpu.VMEM((rows, LANES), F32)],
        compiler_params=_params(("parallel",)),
    )(proj, states, y_pre, dyg_a, dyg_b, wb, wct, d_skip, abar)


def _glu_norm_fwd(y_pre, z, w, *, tr=256):
    rows, width = y_pre.shape
    tr = _tile(rows, tr, SUBLANES)

    def body(y_ref, z_ref, w_ref, o_ref):
        v = _gelu(y_ref[...]) * jax.nn.sigmoid(z_ref[...])
        o_ref[...] = (v * _rms_rows(v) * w_ref[...]).astype(o_ref.dtype)

    blk = pl.BlockSpec((tr, width), lambda i: (i, 0))
    return pl.pallas_call(
        body, name="glu_norm_fwd", grid=(rows // tr,),
        in_specs=[blk, blk, pl.BlockSpec((1, width), lambda i: (0, 0))], out_specs=blk,
        out_shape=jax.ShapeDtypeStruct((rows, width), BF16), compiler_params=_params(("parallel",)),
    )(y_pre, z, w)


def _glu_norm_bwd(y_pre, z, w, dycat, *, tr=256):
    rows, width = y_pre.shape
    tr = _tile(rows, tr, SUBLANES)

    def body(y_ref, z_ref, w_ref, dy_ref, dz_ref, dg_ref, dw_ref, db_ref):
        yg = _gelu(y_ref[...])
        sg = jax.nn.sigmoid(z_ref[...])
        dv, dwp = _rmsnorm_bwd_rows(yg * sg, w_ref[...], dy_ref[...])
        dz = dv * yg * sg * (1.0 - sg)
        dz_ref[...] = dz.astype(dz_ref.dtype)
        dg_ref[...] = dv * sg
        dw_part = jnp.sum(dwp, axis=0, keepdims=True)
        db_part = jnp.sum(dz, axis=0, keepdims=True)

        @pl.when(pl.program_id(0) == 0)
        def _():
            dw_ref[...] = dw_part
            db_ref[...] = db_part

        @pl.when(pl.program_id(0) > 0)
        def _():
            dw_ref[...] += dw_part
            db_ref[...] += db_part

    blk = pl.BlockSpec((tr, width), lambda i: (i, 0))
    vec = pl.BlockSpec((1, width), lambda i: (0, 0))
    return pl.pallas_call(
        body, name="glu_norm_bwd", grid=(rows // tr,), in_specs=[blk, blk, vec, blk], out_specs=[blk, blk, vec, vec],
        out_shape=[jax.ShapeDtypeStruct((rows, width), BF16), jax.ShapeDtypeStruct((rows, width), F32)]
        + [jax.ShapeDtypeStruct((1, width), F32)] * 2,
        compiler_params=_params(("arbitrary",)),
    )(y_pre, z, w, dycat)


def _rope_tables(pos, freq, sign):
    rows = pos.shape[0]

    def body(p_ref, f_ref, s_ref, cos_ref, sin_ref):
        ang = p_ref[...] * f_ref[...]
        cos_ref[...] = jnp.cos(ang)
        sin_ref[...] = jnp.sin(ang) * s_ref[...]

    return pl.pallas_call(body, name="rope_tables", out_shape=[jax.ShapeDtypeStruct((rows, LANES), F32)] * 2)(pos, freq, sign)


def _rope(x, cos, sin_signed):
    half = QK_ROPE_DIM // 2
    src = lax.broadcasted_iota(jnp.int32, (LANES, LANES), 0)
    dst = lax.broadcasted_iota(jnp.int32, (LANES, LANES), 1)
    swap = jnp.where(jnp.logical_or(jnp.logical_and(dst < half, src == dst + half),
                                    jnp.logical_and(jnp.logical_and(dst >= half, dst < 2 * half), src == dst - half)),
                     1.0, 0.0).astype(F32)
    swapped = _dot_exact(x, swap, ((1,), (0,)))
    return x * cos + swapped * sin_signed


def _attn_prep(q, kv, proj, kpe_col, cos, sin, *, tr=256):
    rows = q.shape[0]
    heads = q.shape[1] // HEAD_SLOT
    tr = _tile(rows, tr, SUBLANES)

    def body(q_ref, kv_ref, kpe_ref, cos_ref, sin_ref, qc_ref, kc_ref, v_ref):
        c, s = cos_ref[...], sin_ref[...]
        kpe = _rope(kpe_ref[...], c, s).astype(BF16)
        for h in range(heads):
            nope = slice(h * HEAD_SLOT, h * HEAD_SLOT + LANES)
            pe = slice(h * HEAD_SLOT + LANES, (h + 1) * HEAD_SLOT)
            qc_ref[:, nope] = q_ref[:, nope].astype(BF16)
            qc_ref[:, pe] = _rope(q_ref[:, pe], c, s).astype(BF16)
            kc_ref[:, nope] = kv_ref[:, nope].astype(BF16)
            kc_ref[:, pe] = kpe
            v_ref[:, h * LANES:(h + 1) * LANES] = kv_ref[:, pe].astype(BF16)

    slots = pl.BlockSpec((tr, heads * HEAD_SLOT), lambda i: (i, 0))
    tab = pl.BlockSpec((tr, LANES), lambda i: (i, 0))
    return pl.pallas_call(
        body, name="attn_prep", grid=(rows // tr,),
        in_specs=[slots, slots, pl.BlockSpec((tr, LANES), lambda i: (i, kpe_col)), tab, tab],
        out_specs=[slots, slots, pl.BlockSpec((tr, heads * LANES), lambda i: (i, 0))],
        out_shape=[jax.ShapeDtypeStruct((rows, heads * HEAD_SLOT), BF16)] * 2
        + [jax.ShapeDtypeStruct((rows, heads * LANES), BF16)],
        compiler_params=_params(("parallel",)),
    )(q, kv, proj, cos, sin)


def _causal(tq, tk):
    return lax.broadcasted_iota(jnp.int32, (tq, tk), 1) <= lax.broadcasted_iota(jnp.int32, (tq, tk), 0)


def _attn_fwd(qc, kc, vb, *, scale, tq=512):
    rows = qc.shape[0]
    heads = qc.shape[1] // HEAD_SLOT
    tq = _tile(rows, tq, SUBLANES)
    tk = tq

    def body(q_ref, k_ref, v_ref, o_ref, lse_ref):
        i = pl.program_id(1)
        q = q_ref[...]

        def step(j, carry, diagonal):
            m, l, acc = carry
            k0 = pl.multiple_of(j * tk, tk)
            s = _dot_nt(q, k_ref[pl.ds(k0, tk), :]) * scale
            if diagonal:
                s = jnp.where(_causal(tq, tk), s, NEG_INF)
            m_new = jnp.maximum(m, jnp.max(s, axis=-1, keepdims=True))
            p = jnp.exp(s - m_new)
            alpha = jnp.exp(m - m_new)
            l = alpha * l + jnp.sum(p, axis=-1, keepdims=True)
            acc = alpha * acc + _dot_nn(p.astype(BF16), v_ref[pl.ds(k0, tk), :])
            return m_new, l, acc

        init = (jnp.full((tq, 1), NEG_INF, F32), jnp.zeros((tq, 1), F32), jnp.zeros((tq, LANES), F32))
        below = lax.fori_loop(0, i, lambda j, carry: step(j, carry, False), init)
        m, l, acc = step(i, below, True)
        o_ref[...] = acc / l
        lse_ref[...] = jnp.broadcast_to(m + jnp.log(l), (tq, LANES))

    return pl.pallas_call(
        body, name="attn_fwd", grid=(heads, rows // tq),
        in_specs=[pl.BlockSpec((tq, HEAD_SLOT), lambda h, i: (i, h)), pl.BlockSpec((rows, HEAD_SLOT), lambda h, i: (0, h)),
                  pl.BlockSpec((rows, LANES), lambda h, i: (0, h))],
        out_specs=[pl.BlockSpec((tq, LANES), lambda h, i: (i, h))] * 2,
        out_shape=[jax.ShapeDtypeStruct((rows, heads * LANES), F32)] * 2,
        compiler_params=_params(("parallel", "parallel")),
    )(qc, kc, vb)


def _attn_bwd(qc, kc, vb, o, do, lse, cos, sin, *, scale, tk=512):
    rows = qc.shape[0]
    heads = qc.shape[1] // HEAD_SLOT
    tk = _tile(rows, tk, SUBLANES)
    tq = tk
    nq = rows // tq

    def body(q_ref, k_ref, v_ref, o_ref, do_ref, lse_ref, cos_ref, sin_ref, dq_ref, dkv_ref, dkpe_ref, dq_acc, delta_ref):
        j = pl.program_id(1)

        @pl.when(j == 0)
        def _():
            dq_acc[...] = jnp.zeros_like(dq_acc)
            for r0 in range(0, rows, tq):
                d = jnp.sum(do_ref[pl.ds(r0, tq), :] * o_ref[pl.ds(r0, tq), :], axis=-1, keepdims=True)
                delta_ref[pl.ds(r0, tq), :] = jnp.broadcast_to(d, (tq, LANES))

        kb, vv = k_ref[...], v_ref[...]

        def step(i, carry, diagonal):
            dk, dv = carry
            q0 = pl.multiple_of(i * tq, tq)
            qb = q_ref[pl.ds(q0, tq), :]
            dob = do_ref[pl.ds(q0, tq), :].astype(BF16)
            s = _dot_nt(qb, kb) * scale
            p = jnp.exp(s - lse_ref[pl.ds(q0, tq), :1])
            if diagonal:
                p = jnp.where(_causal(tq, tk), p, 0.0)
            dv = dv + _dot_tn(p.astype(BF16), dob)
            ds = (p * (_dot_nt(dob, vv) - delta_ref[pl.ds(q0, tq), :1])).astype(BF16)
            dk = dk + _dot_tn(ds, qb)
            dq_acc[pl.ds(q0, tq), :] += _dot_nn(ds, kb)
            return dk, dv

        zero = (jnp.zeros((tk, HEAD_SLOT), F32), jnp.zeros((tk, LANES), F32))
        dk, dv = lax.fori_loop(j + 1, nq, lambda i, carry: step(i, carry, False), step(j, zero, True))
        dkv_ref[:, :LANES] = (dk[:, :LANES] * scale).astype(dkv_ref.dtype)
        dkv_ref[:, LANES:] = dv.astype(dkv_ref.dtype)
        dkpe_ref[...] = dk[:, LANES:] * scale

        @pl.when(j == nq - 1)
        def _():
            for r0 in range(0, rows, tq):
                dq = dq_acc[pl.ds(r0, tq), :] * scale
                dq_ref[pl.ds(r0, tq), :LANES] = dq[:, :LANES].astype(dq_ref.dtype)
                dq_ref[pl.ds(r0, tq), LANES:] = _rope(dq[:, LANES:], cos_ref[pl.ds(r0, tq), :],
                                                      -sin_ref[pl.ds(r0, tq), :]).astype(dq_ref.dtype)

    full_q = pl.BlockSpec((rows, HEAD_SLOT), lambda h, j: (0, h))
    full_v = pl.BlockSpec((rows, LANES), lambda h, j: (0, h))
    tab = pl.BlockSpec((rows, LANES), lambda h, j: (0, 0))
    return pl.pallas_call(
        body, name="attn_bwd", grid=(heads, rows // tk),
        in_specs=[full_q, pl.BlockSpec((tk, HEAD_SLOT), lambda h, j: (j, h)), pl.BlockSpec((tk, LANES), lambda h, j: (j, h)),
                  full_v, full_v, full_v, tab, tab],
        out_specs=[full_q, pl.BlockSpec((tk, HEAD_SLOT), lambda h, j: (j, h)), pl.BlockSpec((tk, LANES), lambda h, j: (j, h))],
        out_shape=[jax.ShapeDtypeStruct((rows, heads * HEAD_SLOT), BF16), jax.ShapeDtypeStruct((rows, heads * HEAD_SLOT), BF16),
                   jax.ShapeDtypeStruct((rows, heads * LANES), F32)],
        scratch_shapes=[pltpu.VMEM((rows, HEAD_SLOT), F32), pltpu.VMEM((rows, LANES), F32)],
        compiler_params=_params(("parallel", "arbitrary")),
    )(qc, kc, vb, o, do, lse, cos, sin)


def _kpe_bwd(dkpe_heads, cos, sin, *, tr=512):
    rows = dkpe_heads.shape[0]
    heads = dkpe_heads.shape[1] // LANES
    tr = _tile(rows, tr, 2 * SUBLANES)

    def body(d_ref, cos_ref, sin_ref, o_ref):
        acc = d_ref[:, :LANES]
        for h in range(1, heads):
            acc = acc + d_ref[:, h * LANES:(h + 1) * LANES]
        o_ref[...] = _rope(acc, cos_ref[...], -sin_ref[...]).astype(o_ref.dtype)

    tab = pl.BlockSpec((tr, LANES), lambda i: (i, 0))
    return pl.pallas_call(
        body, name="kpe_bwd", grid=(rows // tr,),
        in_specs=[pl.BlockSpec((tr, heads * LANES), lambda i: (i, 0)), tab, tab], out_specs=tab,
        out_shape=jax.ShapeDtypeStruct((rows, LANES), BF16), compiler_params=_params(("parallel",)),
    )(dkpe_heads, cos, sin)


CONV_ROWS = 128


def _with_halo(ref, r0, ci, n_chunks, ch, lanes, before, after):
    parts = []
    if before:
        lo = pl.multiple_of(jnp.maximum(r0 - SUBLANES, 0), SUBLANES)
        parts.append(ref[pl.ds(lo, SUBLANES), lanes] * jnp.where(ci > 0, 1.0, 0.0))
    parts.append(ref[pl.ds(r0, ch), lanes])
    if after:
        hi = pl.multiple_of(jnp.minimum(r0 + ch, n_chunks * ch - SUBLANES), SUBLANES)
        parts.append(ref[pl.ds(hi, SUBLANES), lanes] * jnp.where(ci < n_chunks - 1, 1.0, 0.0))
    return jnp.concatenate(parts, axis=0)


def _taps(ext):
    return pltpu.roll(ext, 2, 0)[SUBLANES:], pltpu.roll(ext, 1, 0)[SUBLANES:], ext[SUBLANES:]


def _conv3(taps, w, b):
    return w[0:1, :] * taps[0] + w[1:2, :] * taps[1] + w[2:3, :] * taps[2] + b


def _conv_gate_fwd(a, conv_w, conv_b, *, tc=256):
    rows, f2 = a.shape
    f = f2 // 2
    tc = _tile(f, tc)
    nc = f // tc
    ch = _tile(rows, CONV_ROWS, SUBLANES)
    n_chunks = rows // ch

    def body(ag_ref, av_ref, wg_ref, wv_ref, bg_ref, bv_ref, o_ref):
        for lt in range(tc // LANES):
            lanes = slice(lt * LANES, (lt + 1) * LANES)
            wg, wv, bg, bv = wg_ref[:, lanes], wv_ref[:, lanes], bg_ref[:, lanes], bv_ref[:, lanes]

            def chunk(ci, carry):
                r0 = pl.multiple_of(ci * ch, ch)
                gate = _conv3(_taps(_with_halo(ag_ref, r0, ci, n_chunks, ch, lanes, True, False)), wg, bg)
                val = _conv3(_taps(_with_halo(av_ref, r0, ci, n_chunks, ch, lanes, True, False)), wv, bv)
                o_ref[pl.ds(r0, ch), lanes] = (gate * jax.nn.sigmoid(gate) * val).astype(o_ref.dtype)
                return carry

            lax.fori_loop(0, n_chunks, chunk, 0)

    return pl.pallas_call(
        body, name="conv_gate_fwd", grid=(nc,),
        in_specs=[pl.BlockSpec((rows, tc), lambda j: (0, j)), pl.BlockSpec((rows, tc), lambda j: (0, j + nc)),
                  pl.BlockSpec((SUBLANES, tc), lambda j: (0, j)), pl.BlockSpec((SUBLANES, tc), lambda j: (0, j + nc)),
                  pl.BlockSpec((1, tc), lambda j: (0, j)), pl.BlockSpec((1, tc), lambda j: (0, j + nc))],
        out_specs=pl.BlockSpec((rows, tc), lambda j: (0, j)),
        out_shape=jax.ShapeDtypeStruct((rows, f), BF16), compiler_params=_params(("parallel",)),
    )(a, a, conv_w, conv_w, conv_b, conv_b)


def _conv_gate_bwd(a, conv_w, conv_b, dg, *, tc=256):
    rows, f2 = a.shape
    f = f2 // 2
    tc = _tile(f, tc)
    nc = f // tc
    ch = _tile(rows, CONV_ROWS, SUBLANES)
    n_chunks = rows // ch
    ext_rows = ch + SUBLANES

    def fold(x):
        return jnp.sum(x.reshape(ch // SUBLANES, SUBLANES, LANES), axis=0)

    def body(ag_ref, av_ref, wg_ref, wv_ref, bg_ref, bv_ref, dg_ref, da_ref, dw_ref, db_ref):
        for lt in range(tc // LANES):
            lanes = slice(lt * LANES, (lt + 1) * LANES)
            wg, wv, bg, bv = wg_ref[:, lanes], wv_ref[:, lanes], bg_ref[:, lanes], bv_ref[:, lanes]

            def chunk(ci, acc):
                r0 = pl.multiple_of(ci * ch, ch)
                taps_g = _taps(_with_halo(ag_ref, r0, ci, n_chunks, ch, lanes, True, True))
                taps_v = _taps(_with_halo(av_ref, r0, ci, n_chunks, ch, lanes, True, True))
                dge = _with_halo(dg_ref, r0, ci, n_chunks, ch, lanes, False, True)
                gate, val = _conv3(taps_g, wg, bg), _conv3(taps_v, wv, bv)
                sg = jax.nn.sigmoid(gate)
                d_gate = dge * val * sg * (1.0 + gate * (1.0 - sg))
                d_val = dge * gate * sg
                new = []
                for half, (taps, w, d) in enumerate(((taps_g, wg, d_gate), (taps_v, wv, d_val))):
                    da = (w[2:3, :] * d[:ch] + w[1:2, :] * pltpu.roll(d, ext_rows - 1, 0)[:ch]
                          + w[0:1, :] * pltpu.roll(d, ext_rows - 2, 0)[:ch])
                    da_ref[half, pl.ds(r0, ch), lanes] = da.astype(da_ref.dtype)
                    dc = d[:ch]
                    sums = [fold(dc)] + [fold(dc * t[:ch]) for t in taps]
                    new.append(tuple(x + s for x, s in zip(acc[half], sums)))
                return tuple(new)

            zero = tuple(jnp.zeros((SUBLANES, LANES), F32) for _ in range(4))
            acc = lax.fori_loop(0, n_chunks, chunk, (zero, zero))
            row = lax.broadcasted_iota(jnp.int32, (SUBLANES, LANES), 0)
            for half in range(2):
                db, *taps = (jnp.sum(x, axis=0, keepdims=True) for x in acc[half])
                db_ref[half, :, lanes] = db
                dw = jnp.zeros((SUBLANES, LANES), F32)
                for tap in range(3):
                    dw = jnp.where(row == tap, taps[tap], dw)
                dw_ref[half, :, lanes] = dw

    lo = lambda j: (0, j)
    hi = lambda j: (0, j + nc)
    both = lambda j: (0, 0, j)
    return pl.pallas_call(
        body, name="conv_gate_bwd", grid=(nc,),
        in_specs=[pl.BlockSpec((rows, tc), lo), pl.BlockSpec((rows, tc), hi), pl.BlockSpec((SUBLANES, tc), lo),
                  pl.BlockSpec((SUBLANES, tc), hi), pl.BlockSpec((1, tc), lo), pl.BlockSpec((1, tc), hi),
                  pl.BlockSpec((rows, tc), lo)],
        out_specs=[pl.BlockSpec((2, rows, tc), both), pl.BlockSpec((2, SUBLANES, tc), both), pl.BlockSpec((2, 1, tc), both)],
        out_shape=[jax.ShapeDtypeStruct((2, rows, f), BF16), jax.ShapeDtypeStruct((2, SUBLANES, f), F32),
                   jax.ShapeDtypeStruct((2, 1, f), F32)],
        compiler_params=_params(("parallel",)),
    )(a, a, conv_w, conv_w, conv_b, conv_b, dg)


def _wgrad(a, b, rows, cols, row_sharded, name, **kw):
    return functools.partial(_wgrad_half, a, b, rows, cols, row_sharded, name, **kw)


class _NoExchange:
    def __init__(self, later, ffn):
        self.later, self.ffn = later, ffn

    def mixer_weights(self, after):
        return self.later

    def ffn_weights_arrived(self, after):
        return None

    def ffn_weights(self, after):
        return self.ffn

    def ffn_down_arrived(self, after):
        return None

    def ffn_down_weight(self, after):
        return self.ffn["ffn_w_down"]

    def ffn_grads(self, makers, after):
        self.ffn_makers = makers
        return None

    def ffn_backward_done(self, after):
        return None


def _local_step(x, posf, target, w, hooks):
    rows, d = x.shape
    width = w["ssm_d"].shape[1]
    qr, kvr = w["mla_q_norm_w"].shape[1], w["mla_kv_norm_w"].shape[1]
    heads = w["mla_w_ukv"].shape[1] // HEAD_SLOT
    f2 = w["ffn_conv_b"].shape[1]
    inp = w["w_in"].shape[0]
    scale = (QK_NOPE_DIM + QK_ROPE_DIM) ** -0.5
    g = {}

    hn = _rmsnorm_fwd(x, w["attn_norm_w"], name="attn_norm")
    proj = _matmul(hn, w["w_in"], mode="nt", name="in_proj")

    s5_weights = (w["ssm_lambda_re"], w["ssm_lambda_im"], w["ssm_log_dt"], w["ssm_b_re"], w["ssm_b_im"])
    wb, wct, abar = _s5_bands(*s5_weights, w["ssm_c_re"], w["ssm_c_im"])
    states, y_pre, yg = _s5_fwd(proj, wb, wct, w["ssm_d"], abar)
    later = hooks.mixer_weights(yg)
    z = _matmul(yg, later["ssm_w_glu"], mode="nn", name="glu_proj", bias=w["ssm_b_glu"])
    ys = _glu_norm_fwd(y_pre, z, w["ssm_out_norm_w"])

    q_col, kv_col, kpe_col = width // qr, (width + qr) // kvr, (width + qr + kvr) // LANES
    assert width % qr == 0 and (width + qr) % kvr == 0
    qn = _rmsnorm_fwd(proj, w["mla_q_norm_w"], name="q_norm", width=qr, col=q_col)
    kvn = _rmsnorm_fwd(proj, w["mla_kv_norm_w"], name="kv_norm", width=kvr, col=kv_col)
    q = _matmul(qn, w["mla_w_uq"], mode="nn", name="q_proj")
    kv = _matmul(kvn, w["mla_w_ukv"], mode="nn", name="kv_proj")
    half = QK_ROPE_DIM // 2
    inv_freq = ROPE_THETA ** (-jnp.arange(0, QK_ROPE_DIM, 2, dtype=F32) / QK_ROPE_DIM)
    zeros = jnp.zeros((LANES - QK_ROPE_DIM,), F32)
    freq = jnp.concatenate([inv_freq, inv_freq, zeros]).reshape(1, LANES)
    sign = jnp.concatenate([-jnp.ones((half,), F32), jnp.ones((half,), F32), zeros]).reshape(1, LANES)
    cos, sin = _rope_tables(posf, freq, sign)
    qc, kc, vb = _attn_prep(q, kv, proj, kpe_col, cos, sin)
    o, lse = _attn_fwd(qc, kc, vb, scale=scale, tq=ATTN_BLOCK)
    ym = _rmsnorm_fwd(o, w["mla_out_norm_w"], name="mla_out_norm")
    ycat = jnp.concatenate([ys, ym], axis=1)
    h1 = _matmul(ycat, later["w_out"], mode="nn", name="out_proj", add=x, after=hooks.ffn_weights_arrived(ycat))

    hn2 = _rmsnorm_fwd(h1, w["ffn_norm_w"], name="ffn_norm")
    ffn = hooks.ffn_weights(hn2)
    a = _matmul(hn2, ffn["ffn_w_up"], mode="nn", name="ffn_up", tm=FFN_ROWS)
    started = hooks.ffn_down_arrived(a)
    conv_b = w["ffn_conv_b"] if started is None else w["ffn_conv_b"] + started[:1, :1]
    gated = _conv_gate_fwd(a, ffn["ffn_conv_w"], conv_b)
    w_down = hooks.ffn_down_weight(gated)
    h2 = _matmul(gated, w_down, mode="nn", name="ffn_down", add=h1, tk=2816, tm=FFN_ROWS)
    loss_tile, dh2, dh2_mxu, g["final_norm_w"] = _final_norm_loss(h2, w["final_norm_w"], target)

    dgated = _matmul(dh2_mxu, w_down, mode="nt", name="ffn_down_dx", tm=FFN_ROWS)
    da, dcw, dcb = _conv_gate_bwd(a, ffn["ffn_conv_w"], w["ffn_conv_b"], dgated)
    g["ffn_conv_w"] = jnp.concatenate([dcw[0, :3], dcw[1, :3]], axis=1)
    g["ffn_conv_b"] = jnp.concatenate([dcb[0], dcb[1]], axis=1)
    started = hooks.ffn_grads({
        "ffn_w_up": _wgrad(hn2, da, d, f2, False, "ffn_up_dw", b_split=True, tn=_tile(f2 // N_CHIPS, 1408)),
        "ffn_w_down": _wgrad(gated, dh2_mxu, f2 // 2, d, True, "ffn_down_dw", tm=f2 // 2 // N_CHIPS, tn=512)}, dcb)
    dhn2 = _matmul(da, ffn["ffn_w_up"], mode="nt", name="ffn_up_dx", a_split=True, tk=_tile(f2 // 2, 2816), tm=FFN_ROWS,
                   after=started)
    dh1, dh1_mxu, g["ffn_norm_w"] = _rmsnorm_bwd(h1, w["ffn_norm_w"], dhn2, name="ffn_norm_bwd", add=dh2,
                                                dx_dtypes=(F32, BF16))

    dycat = _matmul(dh1_mxu, later["w_out"], mode="nt", name="out_proj_dx")
    g["w_out"] = _wgrad(ycat, dh1_mxu, 2 * width, d, True, "out_proj_dw")
    started = hooks.ffn_backward_done(dycat)
    mla_out_norm_w, ssm_out_norm_w = w["mla_out_norm_w"], w["ssm_out_norm_w"]
    if started is not None:
        mla_out_norm_w, ssm_out_norm_w = mla_out_norm_w + started[:1, :1], ssm_out_norm_w + started[:1, :1]

    do, g["mla_out_norm_w"] = _rmsnorm_bwd(o, mla_out_norm_w, dycat, name="mla_out_norm_bwd", width=width, dy_col=1)
    dq, dkv, dkpe_heads = _attn_bwd(qc, kc, vb, o, do, lse, cos, sin, scale=scale, tk=ATTN_BLOCK)
    dkpe = _kpe_bwd(dkpe_heads, cos, sin)
    g["mla_w_uq"] = _wgrad(qn, dq, qr, heads * HEAD_SLOT, False, "q_proj_dw")
    dqn = _matmul(dq, w["mla_w_uq"], mode="nt", name="q_proj_dx")
    dcq, g["mla_q_norm_w"] = _rmsnorm_bwd(proj, w["mla_q_norm_w"], dqn, name="q_norm_bwd", width=qr, col=q_col,
                                          dx_dtypes=(BF16,))
    g["mla_w_ukv"] = _wgrad(kvn, dkv, kvr, heads * HEAD_SLOT, False, "kv_proj_dw")
    dkvn = _matmul(dkv, w["mla_w_ukv"], mode="nt", name="kv_proj_dx")
    dckv, g["mla_kv_norm_w"] = _rmsnorm_bwd(proj, w["mla_kv_norm_w"], dkvn, name="kv_norm_bwd", width=kvr, col=kv_col,
                                            dx_dtypes=(BF16,))

    dz, dyg_a, g["ssm_out_norm_w"], g["ssm_b_glu"] = _glu_norm_bwd(y_pre, z, ssm_out_norm_w, dycat)
    dyg_b = _matmul(dz, later["ssm_w_glu"], mode="nt", name="glu_proj_dx")
    g["ssm_w_glu"] = _wgrad(yg, dz, width, width, True, "glu_proj_dw")
    du, dwb, dwct, dabar, g["ssm_d"] = _s5_bwd(proj, states, y_pre, dyg_a, dyg_b, wb, wct, w["ssm_d"], abar)
    (g["ssm_lambda_re"], g["ssm_lambda_im"], g["ssm_log_dt"], g["ssm_b_re"], g["ssm_b_im"], g["ssm_c_re"],
     g["ssm_c_im"]) = _s5_bands_bwd(*s5_weights, dwb, dwct, dabar)

    pad = jnp.zeros((rows, inp - (width + qr + kvr + LANES)), BF16)
    dproj = jnp.concatenate([du, dcq, dckv, dkpe, pad], axis=1)
    g["w_in"] = _wgrad(dproj, hn, inp, d, False, "in_proj_dw")
    dhn = _matmul(dproj, w["w_in"], mode="nn", name="in_proj_dx")
    dx, g["attn_norm_w"] = _rmsnorm_bwd(x, w["attn_norm_w"], dhn, name="attn_norm_bwd", add=dh1)
    return loss_tile, dx, g


ANY = pl.BlockSpec(memory_space=pl.ANY)
MESH = pl.DeviceIdType.MESH


def _mesh_pos():
    return lax.axis_index("x"), lax.axis_index("y"), lax.axis_index("c")


def _other_chips(x, y):
    return [(1 - x, y), (x, 1 - y), (1 - x, 1 - y)]


def _remote(src, dst, send_sems, recv_sems, k, to):
    return pltpu.make_async_remote_copy(src_ref=src, dst_ref=dst, send_sem=send_sems.at[k], recv_sem=recv_sems.at[k],
                                        device_id=to, device_id_type=MESH)


def _place_shard(shard, piece_idx, row_sharded, name, out_dtype=BF16, pieces=N_CHIPS, after=None):
    rs, cs = shard.shape
    tr = _tile(rs, 256, 2 * SUBLANES)
    rb = rs // tr
    extra = [] if after is None else [after]

    def body(p_ref, x_ref, *rest):
        o_ref = rest[-1]
        o_ref[...] = x_ref[...].astype(o_ref.dtype)

    if row_sharded:
        out_shape, out_map = (pieces * rs, cs), (lambda i, p_ref: (p_ref[0] * rb + i, 0))
    else:
        out_shape, out_map = (rs, pieces * cs), (lambda i, p_ref: (i, p_ref[0]))
    return pl.pallas_call(
        body, name=name, out_shape=jax.ShapeDtypeStruct(out_shape, out_dtype),
        grid_spec=pltpu.PrefetchScalarGridSpec(
            num_scalar_prefetch=1, grid=(rb,),
            in_specs=[pl.BlockSpec((tr, cs), lambda i, p_ref: (i, 0))] + [pl.BlockSpec(memory_space=pl.ANY)] * len(extra),
            out_specs=pl.BlockSpec((tr, cs), out_map)),
        compiler_params=_params(("parallel",)),
    )(piece_idx, shard, *extra)


def _gather_weights(placed, name):
    n = len(placed)
    meta = [(row_sharded, direct) for _, row_sharded, direct in placed]
    over_ici, over_d2d = _gather_plans(meta)
    forwarded = [t for t, (_, direct) in enumerate(meta) if not direct]

    def body(*refs):
        outs = refs[n:2 * n]
        send_sems, recv_sems, pass_send_sems, pass_recv_sems = refs[2 * n:]
        first, arrivals = over_ici(outs, send_sems, recv_sems)
        passed, passed_arrivals = over_d2d([outs[t] for t in forwarded], pass_send_sems, pass_recv_sems)
        for cp in first:
            cp.start()
        for t in range(n):
            for j in range(3):
                arrivals[3 * t + j].wait_recv()
                if t in forwarded:
                    passed[3 * forwarded.index(t) + j].start()
        for cp in passed_arrivals:
            cp.wait_recv()
        for cp in first + passed:
            cp.wait_send()

    return pl.pallas_call(
        body, name=name, in_specs=[ANY] * n, out_specs=[ANY] * n,
        out_shape=[jax.ShapeDtypeStruct(arr.shape, arr.dtype) for arr, _, _ in placed],
        input_output_aliases={t: t for t in range(n)},
        scratch_shapes=[pltpu.SemaphoreType.DMA((3 * n,)), pltpu.SemaphoreType.DMA((3 * n,)),
                        pltpu.SemaphoreType.DMA((3 * len(forwarded),)), pltpu.SemaphoreType.DMA((3 * len(forwarded),))],
    )(*[arr for arr, _, _ in placed])


def _gather_plans(meta):
    def window(ref, row_sharded, piece, half):
        r, cc = ref.shape
        if row_sharded:
            rs = r // N_CHIPS
            if half is None:
                return ref.at[pl.ds(piece * rs, rs), :]
            return ref.at[pl.ds(piece * rs + half * (rs // 2), rs // 2), :]
        cs = cc // N_CHIPS
        if half is None:
            return ref.at[:, pl.ds(piece * cs, cs)]
        return ref.at[pl.ds(half * (r // 2), r // 2), pl.ds(piece * cs, cs)]

    def over_ici(refs, send_sems, recv_sems):
        x, y, c = _mesh_pos()
        sends, recvs = [], []
        for t, (row_sharded, direct) in enumerate(meta):
            mine = window(refs[t], row_sharded, 2 * x + y, None if direct else c)
            for j, (px, py) in enumerate(_other_chips(x, y)):
                theirs = window(refs[t], row_sharded, 2 * px + py, None if direct else c)
                sends.append(_remote(mine, mine, send_sems, recv_sems, 3 * t + j, (px, py, c)))
                recvs.append(_remote(theirs, theirs, send_sems, recv_sems, 3 * t + j, (px, py, c)))
        return sends, recvs

    def over_d2d(refs, send_sems, recv_sems):
        x, y, c = _mesh_pos()
        sends, recvs = [], []
        rows = [row_sharded for row_sharded, direct in meta if not direct]
        for t, row_sharded in enumerate(rows):
            for j, (px, py) in enumerate(_other_chips(x, y)):
                got = window(refs[t], row_sharded, 2 * px + py, c)
                other = window(refs[t], row_sharded, 2 * px + py, 1 - c)
                sends.append(_remote(got, got, send_sems, recv_sems, 3 * t + j, (x, y, 1 - c)))
                recvs.append(_remote(other, other, send_sems, recv_sems, 3 * t + j, (x, y, 1 - c)))
        return sends, recvs

    return over_ici, over_d2d


HBM = pl.BlockSpec(memory_space=pltpu.HBM)
SEMAPHORES = pl.BlockSpec(memory_space=pltpu.SEMAPHORE)
DATAFLOW = pltpu.SideEffectType.DATAFLOW_SIDE_EFFECTING


def _start_copies(name, arrays, plan, n_copies, after):
    n = len(arrays)

    def body(*refs):
        sends, _ = plan(refs[:n], refs[n + 1], refs[n + 2])
        for cp in sends:
            cp.start()
        token = refs[2 * n + 3]
        token[...] = jnp.zeros_like(token)

    out = pl.pallas_call(
        body, name=name,
        out_shape=(pltpu.SemaphoreType.DMA((n_copies,)), pltpu.SemaphoreType.DMA((n_copies,)),
                   *[pltpu.HBM(a.shape, a.dtype) for a in arrays], jax.ShapeDtypeStruct((SUBLANES, LANES), F32)),
        in_specs=[HBM] * n + [ANY],
        out_specs=(SEMAPHORES, SEMAPHORES, *[HBM] * n, pl.BlockSpec(memory_space=pltpu.VMEM)),
        input_output_aliases={t: t + 2 for t in range(n)},
        compiler_params=pltpu.CompilerParams(has_side_effects=DATAFLOW),
    )(*[pltpu.with_memory_space_constraint(a, pltpu.HBM) for a in arrays], after)
    return out[0], out[1], list(out[2:2 + n]), out[2 + n]


def _wait_copies(name, started, plan, after):
    send_sems, recv_sems, arrays, _ = started
    n = len(arrays)

    def body(*refs):
        sends, recvs = plan(refs[:n], refs[n], refs[n + 1])
        for cp in sends:
            cp.wait_send()
        for cp in recvs:
            cp.wait_recv()

    out = pl.pallas_call(
        body, name=name, out_shape=[pltpu.HBM(a.shape, a.dtype) for a in arrays],
        in_specs=[HBM] * n + [SEMAPHORES, SEMAPHORES, ANY], out_specs=[HBM] * n,
        input_output_aliases={t: t for t in range(n)},
        compiler_params=pltpu.CompilerParams(has_side_effects=DATAFLOW),
    )(*arrays, send_sems, recv_sems, after)
    return list(out)


def _exchange(name, arrays, plan, n_copies, after=None):
    n = len(arrays)
    extra = [] if after is None else [after]

    def body(*refs):
        outs = refs[n + len(extra):2 * n + len(extra)]
        send_sems, recv_sems = refs[2 * n + len(extra):]
        sends, recvs = plan(outs, send_sems, recv_sems)
        for cp in sends:
            cp.start()
        for cp in recvs:
            cp.wait_recv()
        for cp in sends:
            cp.wait_send()

    return pl.pallas_call(
        body, name=name, in_specs=[ANY] * (n + len(extra)), out_specs=[ANY] * n,
        out_shape=[jax.ShapeDtypeStruct(a.shape, a.dtype) for a in arrays],
        input_output_aliases={t: t for t in range(n)},
        scratch_shapes=[pltpu.SemaphoreType.DMA((n_copies,)), pltpu.SemaphoreType.DMA((n_copies,))],
    )(*arrays, *extra)


def _give_plan(n):
    def plan(refs, send_sems, recv_sems):
        x, y, c = _mesh_pos()
        sends = [_remote(refs[t], refs[n + t], send_sems, recv_sems, t, (x, y, 1 - c)) for t in range(n)]
        return sends, sends

    return plan


def _scatter_plan(n):
    def plan(refs, send_sems, recv_sems):
        x, y, c = _mesh_pos()
        sends = []
        for t in range(n):
            for j, (px, py) in enumerate(_other_chips(x, y)):
                sends.append(_remote(refs[t].at[2 * px + py], refs[n + t].at[j], send_sems, recv_sems, 3 * t + j, (px, py, c)))
        return sends, sends

    return plan


def _scatter_shapes(sums):
    return [jax.ShapeDtypeStruct((3,) + s.shape[1:], s.dtype) for s in sums]


def _join_plan(n):
    def plan(refs, send_sems, recv_sems):
        x, y, c = _mesh_pos()
        sends = [_remote(refs[t].at[c], refs[t].at[c], send_sems, recv_sems, t, (x, y, 1 - c)) for t in range(n)]
        recvs = [_remote(refs[t].at[1 - c], refs[t].at[1 - c], send_sems, recv_sems, t, (x, y, 1 - c)) for t in range(n)]
        return sends, recvs

    return plan


def _join_halves(halves, name, after=None):
    return _exchange(name, halves, _join_plan(len(halves)), len(halves), after=after)


def _add_other_half(g4, got, where, name):
    _, pieces, sr, sc = g4.shape
    tr = _tile(sr, 256, 2 * SUBLANES)

    def body(w_ref, a_ref, b_ref, o_ref):
        o_ref[...] = a_ref[...] + b_ref[...]

    blk = pl.BlockSpec((None, tr, sc), lambda p, i, w_ref: (p, i, 0))
    return pl.pallas_call(
        body, name=name, out_shape=jax.ShapeDtypeStruct((pieces, sr, sc), F32),
        grid_spec=pltpu.PrefetchScalarGridSpec(
            num_scalar_prefetch=1, grid=(pieces, sr // tr),
            in_specs=[pl.BlockSpec((None, None, tr, sc), lambda p, i, w_ref: (w_ref[0], p, i, 0)), blk], out_specs=blk),
        compiler_params=_params(("parallel", "parallel")),
    )(where, g4, got)


def _add_pieces(sums, got_pieces, where, name, after=None):
    _, sr, sc = sums.shape
    tr = _tile(sr, 256, 2 * SUBLANES)
    extra = [] if after is None else [after]

    def body(w_ref, a_ref, r_ref, *rest):
        acc = a_ref[...]
        for j in range(3):
            acc = acc + r_ref[j].astype(F32)
        rest[-1][...] = acc

    return pl.pallas_call(
        body, name=name, out_shape=jax.ShapeDtypeStruct((N_CORES, sr, sc), F32),
        grid_spec=pltpu.PrefetchScalarGridSpec(
            num_scalar_prefetch=1, grid=(sr // tr,),
            in_specs=[pl.BlockSpec((None, tr, sc), lambda i, w_ref: (w_ref[1], i, 0)),
                      pl.BlockSpec((3, tr, sc), lambda i, w_ref: (0, i, 0))] + [pl.BlockSpec(memory_space=pl.ANY)] * len(extra),
            out_specs=pl.BlockSpec((None, tr, sc), lambda i, w_ref: (w_ref[0], i, 0))),
        compiler_params=_params(("parallel",)),
    )(where, sums, got_pieces, *extra)


def _adamw_update(w, g, m, v):
    nm = ADAM_B1 * m + (1.0 - ADAM_B1) * g
    nv = ADAM_B2 * v + (1.0 - ADAM_B2) * (g * g)
    m_hat = nm / (1.0 - ADAM_B1 ** ADAM_STEP)
    v_hat = nv / (1.0 - ADAM_B2 ** ADAM_STEP)
    return -ADAM_LR * (m_hat / (jnp.sqrt(v_hat) + ADAM_EPS) + ADAM_WD * w), nm, nv


def _adamw(w, g, m, v, name, after=None):
    rows, cols = w.shape
    halves = 2 if g.ndim == 3 else 1
    bc = cols // halves
    tr = _tile(rows, max(SUBLANES, (1 << 19) // max(bc, 1) // SUBLANES * SUBLANES), SUBLANES)

    def body(w_ref, g_ref, m_ref, v_ref, *rest):
        d_ref, nm_ref, nv_ref, go_ref = rest[-4:]
        gv = g_ref[...]
        d_ref[...], nm_ref[...], nv_ref[...] = _adamw_update(w_ref[...], gv, m_ref[...], v_ref[...])
        go_ref[...] = gv

    blk = pl.BlockSpec((tr, bc), lambda i, h: (i, h))
    g_blk = pl.BlockSpec((None, tr, bc), lambda i, h: (h, i, 0)) if halves == 2 else blk
    extra = [] if after is None else [after]
    return pl.pallas_call(
        body, name=name, grid=(rows // tr, halves),
        in_specs=[blk, g_blk, blk, blk] + [pl.BlockSpec(memory_space=pl.ANY)] * len(extra), out_specs=[blk] * 4,
        out_shape=[jax.ShapeDtypeStruct((rows, cols), F32)] * 4, compiler_params=_params(("parallel", "parallel")),
    )(w, g, m, v, *extra)


def _adamw_many(ws, gs, ms, vs, name):
    n = len(ws)

    def body(*refs):
        outs = refs[4 * n:]
        for k in range(n):
            w_ref, g_ref, m_ref, v_ref = (refs[j * n + k] for j in range(4))
            outs[k][...], outs[n + k][...], outs[2 * n + k][...] = _adamw_update(w_ref[...], g_ref[...], m_ref[...], v_ref[...])

    out = pl.pallas_call(
        body, name=name, out_shape=[jax.ShapeDtypeStruct(w.shape, F32) for w in ws] * 3,
        compiler_params=pltpu.CompilerParams(vmem_limit_bytes=VMEM_LIMIT_BYTES),
    )(*ws, *gs, *ms, *vs)
    return out[:n], out[n:2 * n], out[2 * n:]


WEIGHTS = ['attn_norm_w', 'w_in', 'ssm_lambda_re', 'ssm_lambda_im', 'ssm_log_dt', 'ssm_b_re', 'ssm_b_im', 'ssm_c_re',
           'ssm_c_im', 'ssm_d', 'ssm_w_glu', 'ssm_b_glu', 'mla_q_norm_w', 'mla_w_uq', 'mla_kv_norm_w', 'mla_w_ukv',
           'ssm_out_norm_w', 'mla_out_norm_w', 'w_out', 'ffn_norm_w', 'ffn_w_up', 'ffn_conv_w', 'ffn_conv_b',
           'ffn_w_down', 'final_norm_w']
SHARDED = {'w_in': False, 'ssm_w_glu': True, 'mla_w_uq': False, 'mla_w_ukv': False, 'w_out': True, 'ffn_w_up': False,
           'ffn_w_down': True}
SMALL = [n for n in WEIGHTS if n not in SHARDED and n != 'ffn_conv_w']
ROPE_PAD = HEAD_SLOT - QK_NOPE_DIM - QK_ROPE_DIM
SMALL_COLS = 8 * LANES


def _pad_heads(w_uq, heads):
    qr = w_uq.shape[0]
    w3 = w_uq.reshape(qr, heads, QK_NOPE_DIM + QK_ROPE_DIM)
    return jnp.concatenate([w3, jnp.zeros((qr, heads, ROPE_PAD), w_uq.dtype)], axis=2).reshape(qr, heads * HEAD_SLOT)


def _unpad_heads(g_uq, heads):
    qr = g_uq.shape[0]
    return g_uq.reshape(qr, heads, HEAD_SLOT)[:, :, :QK_NOPE_DIM + QK_ROPE_DIM].reshape(qr, -1)


FFN = ['ffn_w_up', 'ffn_w_down']
MIXER_LATER = ['ssm_w_glu', 'w_out']
MIXER_BIG = ['w_in', 'w_out']
FFN_GATHER = FFN + ['ffn_conv_w']


class _Overlapped:
    def __init__(self, placed_first, first_sharding, where):
        self.where, self.mine, self.other = where, where[:1], 1 - where[:1]
        self.first_ici, self.first_d2d = _gather_plans([(r, False) for r in first_sharding])
        self.first = _start_copies("gather_first_start", placed_first, self.first_ici, 3 * len(placed_first), where)
        self.first_started = self.first[3]

    def start_rest(self, placed_later, placed):
        self.later_ici, self.later_d2d = _gather_plans([(SHARDED[n], False) for n in MIXER_LATER])
        self.later = _start_copies("gather_later_start", placed_later, self.later_ici, 3 * len(placed_later),
                                   self.first_started)
        up, down, taps = placed
        self.up_ici, self.up_d2d = _gather_plans([(SHARDED["ffn_w_up"], False), (False, True)])
        self.up = _start_copies("gather_ffn_up_start", [up, taps], self.up_ici, 6, self.later[3])
        self.down_ici, self.down_d2d = _gather_plans([(SHARDED["ffn_w_down"], False)])
        self.down = _start_copies("gather_ffn_down_start", [down], self.down_ici, 3, self.up[3])
        self.gather_started = self.down[3]
        arrived = _wait_copies("gather_first_wait", self.first, self.first_ici, self.gather_started)
        return _exchange("gather_first_pass", arrived, self.first_d2d, 3 * len(arrived))

    def mixer_weights(self, after):
        arrived = _wait_copies("gather_later_wait", self.later, self.later_ici, after)
        return dict(zip(MIXER_LATER, _exchange("gather_later_pass", arrived, self.later_d2d, 3 * len(arrived))))

    def ffn_weights_arrived(self, after):
        up, self.taps = _wait_copies("gather_ffn_up_wait", self.up, self.up_ici, after)
        self.up_passing = _start_copies("gather_ffn_up_pass_start", [up], self.up_d2d, 3, after)
        return self.up_passing[3]

    def ffn_weights(self, after):
        w_up, = _wait_copies("gather_ffn_up_pass_wait", self.up_passing, self.up_d2d, after)
        return {"ffn_w_up": w_up, "ffn_conv_w": self.taps}

    def ffn_down_arrived(self, after):
        down, = _wait_copies("gather_ffn_down_wait", self.down, self.down_ici, after)
        self.down_passing = _start_copies("gather_ffn_down_pass_start", [down], self.down_d2d, 3, after)
        return self.down_passing[3]

    def ffn_down_weight(self, after):
        return _wait_copies("gather_ffn_down_pass_wait", self.down_passing, self.down_d2d, after)[0]

    def ffn_grads(self, makers, after):
        self.makers = [makers[name] for name in FFN]
        n = len(FFN)
        give = [make(self.other, suffix="_give") for make in self.makers]
        lands = [lax.empty(g.shape, g.dtype) for g in give]
        self.swap = _start_copies("grad_ffn_swap_start", give + lands, _give_plan(n), n, after)
        return self.swap[3]

    def ffn_backward_done(self, after):
        n = len(FFN)
        got = _wait_copies("grad_ffn_swap_wait", self.swap, _give_plan(n), after)[n:]
        kept = [make(self.mine, suffix="_keep", add=got[t], wire=True) for t, make in enumerate(self.makers)]
        self.sums = [k[0] for k in kept]
        wires = [k[1] for k in kept]
        lands = [lax.empty(s.shape, s.dtype) for s in _scatter_shapes(wires)]
        self.scatter = _start_copies("grad_ffn_scatter_start", wires + lands, _scatter_plan(n), 3 * n, after)
        return self.scatter[3]

    def ffn_reduced(self, after):
        n = len(FFN)
        got_pieces = _wait_copies("grad_ffn_scatter_wait", self.scatter, _scatter_plan(n), after)[n:]
        halves = []
        for t, name in enumerate(FFN):
            halves.append(_add_pieces(self.sums[t], got_pieces[t], self.where, "grad_add_pieces_" + name,
                                      after=halves[-1] if halves else None))
        return halves


def _step(args):
    x, positions, target = args["x"][0], args["positions"], args["loss_target"][0]
    rows = x.shape[0]
    p = {n: args[n] for n in WEIGHTS}
    xi, yi, ci = _mesh_pos()
    piece = 2 * xi + yi

    def transposed(a):
        return jnp.swapaxes(a[0], 0, 1)

    def as_stored(n, a):
        return jnp.swapaxes(a, 2, 3) if n in ("ssm_b_re", "ssm_b_im") else a

    w_in = transposed(p["w_in"])
    in_width = w_in.shape[0]
    in_pad = (-in_width) % (2 * LANES)
    heads_here = p["mla_w_uq"].shape[2] // (QK_NOPE_DIM + QK_ROPE_DIM)
    shards = {
        "w_in": jnp.pad(w_in, ((0, in_pad), (0, 0))),
        "ssm_w_glu": p["ssm_w_glu"][0],
        "mla_w_uq": _pad_heads(p["mla_w_uq"][0], heads_here),
        "mla_w_ukv": p["mla_w_ukv"][0],
        "w_out": p["w_out"][0],
        "ffn_w_up": p["ffn_w_up"][0],
        "ffn_w_down": p["ffn_w_down"][0],
    }
    conv_w = jnp.pad(p["ffn_conv_w"][0], ((0, SUBLANES - p["ffn_conv_w"].shape[1]), (0, 0)))
    order = list(SHARDED)
    piece_idx = piece.reshape(1).astype(jnp.int32)
    mixer = [n for n in order if n not in FFN]
    first = [n for n in mixer if n not in MIXER_LATER]
    where = jnp.stack([ci, piece]).astype(jnp.int32)
    placed = {n: _place_shard(shards[n], piece_idx, SHARDED[n], "place_" + n) for n in first}
    hooks = _Overlapped([placed[n] for n in first], [SHARDED[n] for n in first], where)
    for n in order:
        if n not in first:
            placed[n] = _place_shard(shards[n], piece_idx, SHARDED[n], "place_" + n, after=hooks.first_started)
    placed["ffn_conv_w"] = _place_shard(conv_w, piece_idx, False, "place_ffn_conv_w", out_dtype=F32,
                                        after=hooks.first_started)
    w = dict(zip(first, hooks.start_rest([placed[n] for n in MIXER_LATER], [placed[n] for n in FFN_GATHER])))
    groups = p["ssm_lambda_re"].shape[1]
    w.update({
        "attn_norm_w": p["attn_norm_w"] + hooks.gather_started[:1, :1],
        "ssm_lambda_re": p["ssm_lambda_re"][0], "ssm_lambda_im": p["ssm_lambda_im"][0],
        "ssm_log_dt": p["ssm_log_dt"].reshape(groups, 1), "ssm_b_re": as_stored("ssm_b_re", p["ssm_b_re"])[0],
        "ssm_b_im": as_stored("ssm_b_im", p["ssm_b_im"])[0], "ssm_c_re": p["ssm_c_re"][0], "ssm_c_im": p["ssm_c_im"][0],
        "ssm_d": p["ssm_d"], "ssm_b_glu": p["ssm_b_glu"], "mla_q_norm_w": p["mla_q_norm_w"],
        "mla_kv_norm_w": p["mla_kv_norm_w"], "ssm_out_norm_w": p["ssm_out_norm_w"], "mla_out_norm_w": p["mla_out_norm_w"],
        "ffn_norm_w": p["ffn_norm_w"], "ffn_conv_b": p["ffn_conv_b"], "final_norm_w": p["final_norm_w"].reshape(1, -1),
    })

    loss_tile, dx, g = _local_step(x, positions.reshape(rows, 1).astype(F32), target, w, hooks)

    flat = [g[n].reshape(-1) for n in SMALL] + [g["ffn_conv_w"].reshape(-1), loss_tile[0, :1]]
    sizes = [f.shape[0] for f in flat]
    per_block = -(-sum(sizes) // (N_CORES * N_CHIPS * SMALL_COLS))
    small_rows = -(-per_block // (2 * SUBLANES)) * (2 * SUBLANES)
    padded = N_CORES * N_CHIPS * small_rows * SMALL_COLS

    def pack(parts):
        parts = list(parts)
        have = sum(q.shape[0] for q in parts)
        return jnp.concatenate(parts + [jnp.zeros((padded - have,), F32)])

    reduced = mixer + ["small"]
    small = pack(flat).reshape(N_CORES, N_CHIPS, small_rows, SMALL_COLS)
    give = [g[n](hooks.other, suffix="_give") for n in mixer] + [lax.dynamic_index_in_dim(small, 1 - ci, 0, keepdims=False)]
    lands = [lax.empty(a.shape, a.dtype) for a in give]
    give_plan = _give_plan(len(reduced))
    swap = _start_copies("grad_mixer_swap_start", give + lands, give_plan, len(reduced), dx)

    grads, delta, new_m, new_v = {}, {}, {}, {}

    def finish(n, joined, after=None):
        grad = joined if SHARDED[n] else joined.reshape(-1, joined.shape[2])
        if n == "w_in":
            wt, mt, vt = w_in, transposed(args["m_w_in"]), transposed(args["v_w_in"])
            out = _adamw(wt, grad, mt, vt, "adamw_w_in")
            delta[n], new_m[n], new_v[n], grads[n] = (jnp.swapaxes(a, 0, 1)[None] for a in out)
            return
        if n == "mla_w_uq":
            grad = _unpad_heads(grad, heads_here)
        adam(n, grad, after)

    def adam(n, grad, after=None):
        shape = p[n].shape
        out = _adamw(p[n].reshape(shape[1:]), grad, args["m_" + n].reshape(shape[1:]),
                     args["v_" + n].reshape(shape[1:]), "adamw_" + n, after)
        delta[n], new_m[n], new_v[n], grads[n] = (a.reshape(shape) for a in out)

    ffn_halves = hooks.ffn_reduced(swap[3])
    got = _wait_copies("grad_mixer_swap_wait", swap, give_plan, ffn_halves[-1])[len(reduced):]
    join_plan = _join_plan(len(FFN))
    ffn_join = _start_copies("grad_ffn_join_start", ffn_halves, join_plan, len(FFN), got[0])
    big = [t for t, n in enumerate(reduced) if n in MIXER_BIG]
    rest = [t for t in range(len(reduced)) if t not in big]
    sums, wires = {}, {}
    for t in big:
        sums[t], wires[t] = g[reduced[t]](hooks.mine, suffix="_keep", add=got[t], wire=True)
    ffn_joined = _wait_copies("grad_ffn_join_wait", ffn_join, join_plan, sums[big[-1]])

    def scatter_start(name, group, after):
        lands = [lax.empty(s.shape, s.dtype) for s in _scatter_shapes([wires[t] for t in group])]
        return _start_copies(name, [wires[t] for t in group] + lands, _scatter_plan(len(group)), 3 * len(group), after)

    scatter_big = scatter_start("grad_big_scatter_start", big, ffn_joined[0])
    for t in rest[:-1]:
        sums[t], wires[t] = g[reduced[t]](hooks.mine, suffix="_keep", add=got[t], wire=True, after=scatter_big[3])
    sums[rest[-1]] = wires[rest[-1]] = _add_other_half(small, got[-1], where, "grad_add_half_small")
    scatter_rest = scatter_start("grad_rest_scatter_start", rest, sums[rest[0]])
    behind = scatter_rest[3]
    for n, joined in zip(FFN, ffn_joined):
        finish(n, joined, after=behind)
        behind = delta[n]
    got_pieces = dict(zip(big, _wait_copies("grad_big_scatter_wait", scatter_big, _scatter_plan(len(big)),
                                            delta[FFN[-1]])[len(big):]))
    got_pieces.update(zip(rest, _wait_copies("grad_rest_scatter_wait", scatter_rest, _scatter_plan(len(rest)),
                                             got_pieces[big[0]])[len(rest):]))
    halves = [_add_pieces(sums[t], got_pieces[t], where, "grad_add_pieces_" + n) for t, n in enumerate(reduced)]
    joined = _join_halves(halves, "grad_join_halves")
    for n, j in zip(mixer, joined):
        finish(n, j)
    eighths = _place_shard(joined[-1].reshape(N_CORES * small_rows, SMALL_COLS), piece_idx, True, "place_small_grads",
                           out_dtype=F32)
    small_sum = _gather_weights([(eighths, True, False)], "gather_small_grads")[0]
    flat_sum = small_sum.reshape(N_CHIPS, N_CORES, small_rows * SMALL_COLS).transpose(1, 0, 2).reshape(-1)
    offs = [0]
    for s in sizes:
        offs.append(offs[-1] + s)
    stored = {n: as_stored(n, p[n]) for n in SMALL}
    for k, n in enumerate(SMALL):
        grads[n] = flat_sum[offs[k]:offs[k + 1]].reshape(stored[n].shape)
    taps, cols_here = p["ffn_conv_w"].shape[1], p["ffn_conv_w"].shape[2]
    conv_full = flat_sum[offs[len(SMALL)]:offs[len(SMALL) + 1]].reshape(taps, N_CHIPS * cols_here)
    adam("ffn_conv_w", lax.dynamic_slice_in_dim(conv_full, piece * cols_here, cols_here, axis=1))
    loss = flat_sum[offs[len(SMALL) + 1]]

    def rank2(a):
        return a.reshape(1, -1) if a.ndim == 1 else a

    d_s, m_s, v_s = _adamw_many([rank2(stored[n]) for n in SMALL], [rank2(grads[n]) for n in SMALL],
                                [rank2(as_stored(n, args["m_" + n])) for n in SMALL],
                                [rank2(as_stored(n, args["v_" + n])) for n in SMALL], "adamw_small")
    for k, n in enumerate(SMALL):
        delta[n], new_m[n], new_v[n], grads[n] = (as_stored(n, a.reshape(stored[n].shape))
                                                  for a in (d_s[k], m_s[k], v_s[k], grads[n]))

    return (loss, dx[None], *[grads[n] for n in WEIGHTS], *[delta[n] for n in WEIGHTS],
            *[new_m[n] for n in WEIGHTS], *[new_v[n] for n in WEIGHTS])


def kernel(x, positions, attn_norm_w, w_in, ssm_lambda_re, ssm_lambda_im, ssm_log_dt, ssm_b_re, ssm_b_im, ssm_c_re, ssm_c_im, ssm_d, ssm_w_glu, ssm_b_glu, mla_q_norm_w, mla_w_uq, mla_kv_norm_w, mla_w_ukv, ssm_out_norm_w, mla_out_norm_w, w_out, ffn_norm_w, ffn_w_up, ffn_conv_w, ffn_conv_b, ffn_w_down, final_norm_w, loss_target, m_attn_norm_w, m_w_in, m_ssm_lambda_re, m_ssm_lambda_im, m_ssm_log_dt, m_ssm_b_re, m_ssm_b_im, m_ssm_c_re, m_ssm_c_im, m_ssm_d, m_ssm_w_glu, m_ssm_b_glu, m_mla_q_norm_w, m_mla_w_uq, m_mla_kv_norm_w, m_mla_w_ukv, m_ssm_out_norm_w, m_mla_out_norm_w, m_w_out, m_ffn_norm_w, m_ffn_w_up, m_ffn_conv_w, m_ffn_conv_b, m_ffn_w_down, m_final_norm_w, v_attn_norm_w, v_w_in, v_ssm_lambda_re, v_ssm_lambda_im, v_ssm_log_dt, v_ssm_b_re, v_ssm_b_im, v_ssm_c_re, v_ssm_c_im, v_ssm_d, v_ssm_w_glu, v_ssm_b_glu, v_mla_q_norm_w, v_mla_w_uq, v_mla_kv_norm_w, v_mla_w_ukv, v_ssm_out_norm_w, v_mla_out_norm_w, v_w_out, v_ffn_norm_w, v_ffn_w_up, v_ffn_conv_w, v_ffn_conv_b, v_ffn_w_down, v_final_norm_w):
    return _step(dict(locals()))
```

```python
import functools
import math

import jax
import jax.numpy as jnp
from jax import lax
from jax.experimental import pallas as pl
from jax.experimental.pallas import tpu as pltpu

F32 = jnp.float32
BF16 = jnp.bfloat16

SSM_GROUP = 16
SSM_STATE = 64
QK_NOPE_DIM = 128
QK_ROPE_DIM = 64
ROPE_THETA = 10000.0
RMS_EPS = 1e-6
ADAM_LR, ADAM_B1, ADAM_B2, ADAM_EPS, ADAM_WD, ADAM_STEP = 0.001, 0.9, 0.999, 1e-08, 0.01, 10

LANES = 128
SUBLANES = 8
VMEM_LIMIT_BYTES = 56 * 1024 * 1024

GROUPS_PER_BATCH = LANES // SSM_GROUP
STATE_PER_BATCH = GROUPS_PER_BATCH * SSM_STATE
HEAD_SLOT = 2 * LANES
NEG_INF = -1e30
ATTN_BLOCK = 1024
FFN_ROWS = 1024

N_CHIPS = 4
N_CORES = 2


def _tile(n, pref, align=LANES):
    if n <= pref:
        return n
    t = (pref // align) * align
    while t >= align:
        if n % t == 0:
            return t
        t -= align
    return n


def _params(sem):
    return pltpu.CompilerParams(dimension_semantics=sem, vmem_limit_bytes=VMEM_LIMIT_BYTES)


def _dot(a, b, dims):
    return lax.dot_general(a, b, (dims, ((), ())), preferred_element_type=F32)


def _dot_nn(a, b):
    return _dot(a, b, ((1,), (0,)))


def _dot_nt(a, b):
    return _dot(a, b, ((1,), (1,)))


def _dot_tn(a, b):
    return _dot(a, b, ((0,), (0,)))


def _matmul(a, b, *, mode, name, tm=512, tn=1024, tk=2048, bias=None, add=None, out_dtype=F32,
            a_split=False, b_split=False, after=None):
    if a_split:
        assert mode == "nt"
        a_shape = (a.shape[1], 2 * a.shape[2])
    else:
        a_shape = a.shape
    if b_split:
        assert mode == "tn"
        b_shape = (b.shape[1], 2 * b.shape[2])
    else:
        b_shape = b.shape
    if mode == "nn":
        (m, k), (k2, n) = a_shape, b_shape
    elif mode == "nt":
        (m, k), (n, k2) = a_shape, b_shape
    else:
        (k, m), (k2, n) = a_shape, b_shape
    assert k == k2, (a.shape, b.shape, mode)
    tm, tn, tk = _tile(m, tm, SUBLANES), _tile(n, tn), _tile(k, tk)
    nk = k // tk
    a_spec = {"nn": pl.BlockSpec((tm, tk), lambda i, j, kk: (i, kk)),
              "nt": pl.BlockSpec((tm, tk), lambda i, j, kk: (i, kk)),
              "tn": pl.BlockSpec((tk, tm), lambda i, j, kk: (kk, i))}[mode]
    b_spec = {"nn": pl.BlockSpec((tk, tn), lambda i, j, kk: (kk, j)),
              "nt": pl.BlockSpec((tn, tk), lambda i, j, kk: (j, kk)),
              "tn": pl.BlockSpec((tk, tn), lambda i, j, kk: (kk, j))}[mode]
    if a_split:
        kb = a.shape[2] // tk
        assert a.shape[2] % tk == 0
        a_spec = pl.BlockSpec((None, tm, tk), lambda i, j, kk: (kk // kb, i, kk % kb))
    if b_split:
        nb = b.shape[2] // tn
        assert b.shape[2] % tn == 0
        b_spec = pl.BlockSpec((None, tk, tn), lambda i, j, kk: (j // nb, kk, j % nb))
    dot = {"nn": _dot_nn, "nt": _dot_nt, "tn": _dot_tn}[mode]
    in_specs, operands = [a_spec, b_spec], [a, b]
    if bias is not None:
        in_specs.append(pl.BlockSpec((1, tn), lambda i, j, kk: (0, j)))
        operands.append(bias)
    if add is not None:
        in_specs.append(pl.BlockSpec((tm, tn), lambda i, j, kk: (i, j)))
        operands.append(add)
    if after is not None:
        in_specs.append(pl.BlockSpec(memory_space=pl.ANY))
        operands.append(after)

    def body(*refs):
        a_ref, b_ref = refs[0], refs[1]
        rest = list(refs[2:])
        bias_ref = rest.pop(0) if bias is not None else None
        add_ref = rest.pop(0) if add is not None else None
        if after is not None:
            rest.pop(0)
        o_ref, acc_ref = rest

        def finish(acc):
            if bias_ref is not None:
                acc = acc + bias_ref[...]
            if add_ref is not None:
                acc = acc + add_ref[...]
            o_ref[...] = acc.astype(o_ref.dtype)

        part = dot(a_ref[...].astype(BF16), b_ref[...].astype(BF16))
        if nk == 1:
            finish(part)
        else:
            kk = pl.program_id(2)

            @pl.when(kk == 0)
            def _():
                acc_ref[...] = part

            @pl.when(jnp.logical_and(kk > 0, kk < nk - 1))
            def _():
                acc_ref[...] += part

            @pl.when(kk == nk - 1)
            def _():
                finish(acc_ref[...] + part)

    out_shape = jax.ShapeDtypeStruct((m, n), out_dtype)
    out_spec = pl.BlockSpec((tm, tn), lambda i, j, kk: (i, j))
    acc_shape = (tm, tn) if nk > 1 else (SUBLANES, LANES)
    return pl.pallas_call(
        body, name=name, grid=(m // tm, n // tn, nk), in_specs=in_specs, out_specs=out_spec, out_shape=out_shape,
        scratch_shapes=[pltpu.VMEM(acc_shape, F32)],
        compiler_params=_params(("parallel", "parallel", "arbitrary")),
    )(*operands)


def _wgrad_half(a, b, rows, cols, row_sharded, name, which, *, suffix="", add=None, wire=False, tm=None, tn=None,
                b_split=False, after=None):
    tokens = a.shape[0]
    if row_sharded:
        sr, sc = rows // N_CHIPS, cols // N_CORES
    else:
        sr, sc = rows // N_CORES, cols // N_CHIPS
    tm = _tile(sr, 512) if tm is None else tm
    tn = _tile(sc, 1024) if tn is None else tn
    assert sr % tm == 0 and sc % tn == 0, (rows, cols, tm, tn)
    rb, cb = sr // tm, sc // tn
    if tn >= tm:
        ij, grid = (lambda s, t: (t, s)), (N_CHIPS, cb, rb)
    else:
        ij, grid = (lambda s, t: (s, t)), (N_CHIPS, rb, cb)
    if row_sharded:
        a_tile = lambda p, i, j, h: p * rb + i
        b_tile = lambda p, i, j, h: h[0] * cb + j
    else:
        a_tile = lambda p, i, j, h: h[0] * rb + i
        b_tile = lambda p, i, j, h: p * cb + j
    a_spec = pl.BlockSpec((tokens, tm), lambda p, s, t, h: (0, a_tile(p, *ij(s, t), h)))
    if b_split:
        nbh = b.shape[2] // tn
        assert b.shape[2] % tn == 0
        b_spec = pl.BlockSpec((None, tokens, tn), lambda p, s, t, h: (b_tile(p, *ij(s, t), h) // nbh, 0,
                                                                       b_tile(p, *ij(s, t), h) % nbh))
    else:
        b_spec = pl.BlockSpec((tokens, tn), lambda p, s, t, h: (0, b_tile(p, *ij(s, t), h)))
    out_spec = pl.BlockSpec((None, tm, tn), lambda p, s, t, h: (p, *ij(s, t)))
    in_specs, operands = [a_spec, b_spec], [a, b]
    if add is not None:
        in_specs.append(out_spec)
        operands.append(add)
    if after is not None:
        in_specs.append(pl.BlockSpec(memory_space=pl.ANY))
        operands.append(after)
    out_dtypes = [F32, BF16] if wire else [F32]

    def body(h_ref, a_ref, b_ref, *rest):
        acc = _dot_tn(a_ref[...].astype(BF16), b_ref[...].astype(BF16))
        if add is not None:
            acc = acc + rest[0][...]
        for o_ref in rest[-len(out_dtypes):]:
            o_ref[...] = acc.astype(o_ref.dtype)

    out = pl.pallas_call(
        body, name=name + suffix, out_shape=[jax.ShapeDtypeStruct((N_CHIPS, sr, sc), dt) for dt in out_dtypes],
        grid_spec=pltpu.PrefetchScalarGridSpec(num_scalar_prefetch=1, grid=grid, in_specs=in_specs,
                                               out_specs=[out_spec] * len(out_dtypes)),
        compiler_params=_params(("parallel", "parallel", "parallel")),
    )(which, *operands)
    return tuple(out) if wire else out[0]


def _rms_rows(x):
    return lax.rsqrt(jnp.mean(x * x, axis=-1, keepdims=True) + RMS_EPS)


def _rmsnorm_fwd(x, w, *, name, width=None, col=0, out_dtype=BF16, tr=256):
    rows = x.shape[0]
    width = x.shape[1] if width is None else width
    tr = _tile(rows, tr, SUBLANES)

    def body(x_ref, w_ref, o_ref):
        xv = x_ref[...]
        o_ref[...] = (xv * _rms_rows(xv) * w_ref[...]).astype(o_ref.dtype)

    return pl.pallas_call(
        body, name=name, grid=(rows // tr,),
        in_specs=[pl.BlockSpec((tr, width), lambda i: (i, col)), pl.BlockSpec((1, width), lambda i: (0, 0))],
        out_specs=pl.BlockSpec((tr, width), lambda i: (i, 0)),
        out_shape=jax.ShapeDtypeStruct((rows, width), out_dtype),
        compiler_params=_params(("parallel",)),
    )(x, w)


def _rmsnorm_bwd_rows(xv, w, dy):
    r = _rms_rows(xv)
    n = xv * r
    dn = dy * w
    dx = r * (dn - n * jnp.mean(dn * n, axis=-1, keepdims=True))
    return dx, dy * n


def _rmsnorm_bwd(x, w, dy, *, name, width=None, col=0, dy_col=0, add=None, tr=256, dx_dtypes=(F32,)):
    rows = x.shape[0]
    n_dx = len(dx_dtypes)
    width = x.shape[1] if width is None else width
    tr = _tile(rows, tr, SUBLANES)
    in_specs = [pl.BlockSpec((tr, width), lambda i: (i, col)), pl.BlockSpec((1, width), lambda i: (0, 0)),
                pl.BlockSpec((tr, width), lambda i: (i, dy_col))]
    operands = [x, w, dy]
    if add is not None:
        in_specs.append(pl.BlockSpec((tr, width), lambda i: (i, 0)))
        operands.append(add)

    def body(*refs):
        x_ref, w_ref, dy_ref = refs[:3]
        add_ref = refs[3] if add is not None else None
        dx_refs, dw_ref = refs[-1 - n_dx:-1], refs[-1]
        dx, dwp = _rmsnorm_bwd_rows(x_ref[...], w_ref[...], dy_ref[...])
        if add_ref is not None:
            dx = dx + add_ref[...]
        for dx_ref in dx_refs:
            dx_ref[...] = dx.astype(dx_ref.dtype)
        part = jnp.sum(dwp, axis=0, keepdims=True)

        @pl.when(pl.program_id(0) == 0)
        def _():
            dw_ref[...] = part

        @pl.when(pl.program_id(0) > 0)
        def _():
            dw_ref[...] += part

    return pl.pallas_call(
        body, name=name, grid=(rows // tr,), in_specs=in_specs,
        out_specs=[pl.BlockSpec((tr, width), lambda i: (i, 0))] * n_dx + [pl.BlockSpec((1, width), lambda i: (0, 0))],
        out_shape=[jax.ShapeDtypeStruct((rows, width), dt) for dt in dx_dtypes] + [jax.ShapeDtypeStruct((1, width), F32)],
        compiler_params=_params(("arbitrary",)),
    )(*operands)


def _final_norm_loss(h, w, target, *, tr=256):
    rows, d = h.shape
    tr = _tile(rows, tr, SUBLANES)

    def body(h_ref, w_ref, t_ref, loss_ref, dh_ref, dhb_ref, dw_ref):
        hv, wv = h_ref[...], w_ref[...]
        r = _rms_rows(hv)
        n = hv * r
        err = n * wv - t_ref[...]
        d_out = err * (1.0 / d)
        dn = d_out * wv
        dh = r * (dn - n * jnp.mean(dn * n, axis=-1, keepdims=True))
        dh_ref[...] = dh
        dhb_ref[...] = dh.astype(BF16)
        dw_part = jnp.sum(d_out * n, axis=0, keepdims=True)
        loss_part = jnp.full((SUBLANES, LANES), 0.5 / d, F32) * jnp.sum(err * err)

        @pl.when(pl.program_id(0) == 0)
        def _():
            dw_ref[...] = dw_part
            loss_ref[...] = loss_part

        @pl.when(pl.program_id(0) > 0)
        def _():
            dw_ref[...] += dw_part
            loss_ref[...] += loss_part

    return pl.pallas_call(
        body, name="final_norm_loss", grid=(rows // tr,),
        in_specs=[pl.BlockSpec((tr, d), lambda i: (i, 0)), pl.BlockSpec((1, d), lambda i: (0, 0)),
                  pl.BlockSpec((tr, d), lambda i: (i, 0))],
        out_specs=[pl.BlockSpec((SUBLANES, LANES), lambda i: (0, 0)), pl.BlockSpec((tr, d), lambda i: (i, 0)),
                   pl.BlockSpec((tr, d), lambda i: (i, 0)), pl.BlockSpec((1, d), lambda i: (0, 0))],
        out_shape=[jax.ShapeDtypeStruct((SUBLANES, LANES), F32), jax.ShapeDtypeStruct((rows, d), F32),
                   jax.ShapeDtypeStruct((rows, d), BF16), jax.ShapeDtypeStruct((1, d), F32)],
        compiler_params=_params(("arbitrary",)),
    )(h, w, target)


def _cmul(ar, ai, br, bi):
    return ar * br - ai * bi, ar * bi + ai * br


def _dot_exact(a, b, dims):
    return lax.dot_general(a, b, (dims, ((), ())), preferred_element_type=F32, precision=lax.Precision.HIGHEST)


def _s5_discretize(lr, li, dt):
    mag = jnp.exp(lr * dt)
    th = li * dt
    ar, ai = mag * jnp.cos(th), mag * jnp.sin(th)
    nr, ni = ar - 1.0, ai
    den = lr * lr + li * li
    zr = (nr * lr + ni * li) / den
    zi = (ni * lr - nr * li) / den
    return mag, ar, ai, nr, ni, den, zr, zi


def _band_slices(group):
    j, gi = divmod(group, GROUPS_PER_BATCH)
    rows = slice(gi * SSM_GROUP, (gi + 1) * SSM_GROUP)
    re = slice(gi * SSM_STATE, (gi + 1) * SSM_STATE)
    im = slice(STATE_PER_BATCH + gi * SSM_STATE, STATE_PER_BATCH + (gi + 1) * SSM_STATE)
    return j, rows, re, im


def _s5_bands(lam_re, lam_im, log_dt, b_re, b_im, c_re, c_im):
    g, _ = lam_re.shape
    nb = g // GROUPS_PER_BATCH
    s2 = 2 * STATE_PER_BATCH

    def body(lr_ref, li_ref, ldt_ref, br_ref, bi_ref, cr_ref, ci_ref, wb_ref, wct_ref, a_ref):
        dt = jnp.exp(ldt_ref[...])
        _, ar, ai, _, _, _, zr, zi = _s5_discretize(lr_ref[...], li_ref[...], dt)
        wb_ref[...] = jnp.zeros_like(wb_ref)
        wct_ref[...] = jnp.zeros_like(wct_ref)
        for group in range(g):
            j, rows, re, im = _band_slices(group)
            zr_g, zi_g = zr[group:group + 1, :], zi[group:group + 1, :]
            bre, bim = br_ref[group], bi_ref[group]
            wb_ref[j, rows, re] = (zr_g * bre - zi_g * bim).astype(BF16)
            wb_ref[j, rows, im] = (zr_g * bim + zi_g * bre).astype(BF16)
            wct_ref[j, rows, re] = cr_ref[group].astype(BF16)
            wct_ref[j, rows, im] = (-ci_ref[group]).astype(BF16)
            a_ref[j, :, re] = ar[group:group + 1, :]
            a_ref[j, :, im] = ai[group:group + 1, :]

    return pl.pallas_call(
        body, name="s5_bands",
        out_shape=[jax.ShapeDtypeStruct((nb, LANES, s2), BF16)] * 2 + [jax.ShapeDtypeStruct((nb, 1, s2), F32)],
    )(lam_re, lam_im, log_dt, b_re, b_im, c_re, c_im)


def _s5_bands_bwd(lam_re, lam_im, log_dt, b_re, b_im, dwb, dwct, dabar):
    g, p = lam_re.shape
    gh = b_re.shape[1:]

    def body(lr_ref, li_ref, ldt_ref, br_ref, bi_ref, dwb_ref, dwct_ref, da_ref,
             dlr_ref, dli_ref, dldt_ref, dbre_ref, dbim_ref, dcre_ref, dcim_ref, dzr_ref, dzi_ref, dar_ref, dai_ref):
        lr, li = lr_ref[...], li_ref[...]
        dt = jnp.exp(ldt_ref[...])
        mag, ar, ai, nr, ni, den, zr, zi = _s5_discretize(lr, li, dt)
        for group in range(g):
            j, rows, re, im = _band_slices(group)
            zr_g, zi_g = zr[group:group + 1, :], zi[group:group + 1, :]
            bre, bim = br_ref[group], bi_ref[group]
            dbr, dbi = dwb_ref[j, rows, re], dwb_ref[j, rows, im]
            dbre_ref[group] = zr_g * dbr + zi_g * dbi
            dbim_ref[group] = zr_g * dbi - zi_g * dbr
            dzr_ref[group:group + 1, :] = jnp.sum(bre * dbr + bim * dbi, axis=0, keepdims=True)
            dzi_ref[group:group + 1, :] = jnp.sum(bre * dbi - bim * dbr, axis=0, keepdims=True)
            dcre_ref[group] = dwct_ref[j, rows, re]
            dcim_ref[group] = -dwct_ref[j, rows, im]
            dar_ref[group:group + 1, :] = da_ref[j, :, re]
            dai_ref[group:group + 1, :] = da_ref[j, :, im]
        dzr, dzi = dzr_ref[...], dzi_ref[...]
        inv = 1.0 / den
        d_nr = (dzr * lr - dzi * li) * inv
        d_ni = (dzr * li + dzi * lr) * inv
        d_den = -(dzr * zr + dzi * zi) * inv
        d_lr = (dzr * nr + dzi * ni) * inv + 2.0 * lr * d_den
        d_li = (dzr * ni - dzi * nr) * inv + 2.0 * li * d_den
        t_ar = dar_ref[...] + d_nr
        t_ai = dai_ref[...] + d_ni
        d_lrdt = t_ar * ar + t_ai * ai
        d_th = t_ai * ar - t_ar * ai
        dlr_ref[...] = d_lr + d_lrdt * dt
        dli_ref[...] = d_li + d_th * dt
        dldt_ref[...] = jnp.sum(d_lrdt * lr + d_th * li, axis=1, keepdims=True) * dt

    return pl.pallas_call(
        body, name="s5_bands_bwd",
        out_shape=[jax.ShapeDtypeStruct((g, p), F32)] * 2 + [jax.ShapeDtypeStruct((g, 1), F32)]
        + [jax.ShapeDtypeStruct((g,) + gh, F32)] * 4,
        scratch_shapes=[pltpu.VMEM((g, p), F32)] * 4,
    )(lam_re, lam_im, log_dt, b_re, b_im, dwb, dwct, dabar)


def _powers(ar, ai, count):
    out = [(ar, ai)]
    for _ in range(count - 1):
        out.append(_cmul(out[-1][0], out[-1][1], ar, ai))
    return out


def _scan_coefs(ar, ai, reverse):
    w = ar.shape[-1]
    pw = _powers(ar, ai, SUBLANES)
    row = lax.broadcasted_iota(jnp.int32, (SUBLANES, w), 0)
    steps = []
    d = 1
    while d < SUBLANES:
        keep = (row < SUBLANES - d) if reverse else (row >= d)
        pr, pi = pw[d - 1]
        steps.append((d, jnp.where(keep, pr, 0.0), jnp.where(keep, pi, 0.0)))
        d *= 2
    cr = jnp.zeros((SUBLANES, w), F32)
    ci = jnp.zeros((SUBLANES, w), F32)
    for t in range(SUBLANES):
        pr, pi = pw[SUBLANES - 1 - t] if reverse else pw[t]
        cr = jnp.where(row == t, pr, cr)
        ci = jnp.where(row == t, pi, ci)
    return steps, cr, ci


def _scan_tile(xr, xi, carry_r, carry_i, coefs, reverse):
    steps, cr, ci = coefs
    for d, mr, mi in steps:
        shift = SUBLANES - d if reverse else d
        sr, si = pltpu.roll(xr, shift, 0), pltpu.roll(xi, shift, 0)
        pr, pi = _cmul(mr, mi, sr, si)
        xr, xi = xr + pr, xi + pi
    pr, pi = _cmul(cr, ci, carry_r, carry_i)
    return xr + pr, xi + pi


def _gelu(x):
    c = math.sqrt(2.0 / math.pi)
    return 0.5 * x * (1.0 + jnp.tanh(c * (x + 0.044715 * x * x * x)))


def _gelu_grad(x):
    c = math.sqrt(2.0 / math.pi)
    t = jnp.tanh(c * (x + 0.044715 * x * x * x))
    return 0.5 * (1.0 + t) + 0.5 * x * (1.0 - t * t) * c * (1.0 + 3.0 * 0.044715 * x * x)


def _s5_fwd(proj, wb, wct, d_skip, abar):
    rows = proj.shape[0]
    nb = wb.shape[0]
    s2 = 2 * STATE_PER_BATCH
    st = STATE_PER_BATCH
    chunk = _tile(rows, 512, SUBLANES)

    def body(u_ref, wb_ref, wc_ref, d_ref, a_ref, s_ref, y_ref, yg_ref):
        for c0 in range(0, rows, chunk):
            s_ref[pl.ds(c0, chunk), :] = _dot_nn(u_ref[pl.ds(c0, chunk), :].astype(BF16), wb_ref[...])
        av = a_ref[...]
        coefs = _scan_coefs(av[:, :st], av[:, st:], reverse=False)

        def tile(b, carry):
            r0 = pl.multiple_of(b * SUBLANES, SUBLANES)
            xr, xi = _scan_tile(s_ref[pl.ds(r0, SUBLANES), :st], s_ref[pl.ds(r0, SUBLANES), st:], carry[0], carry[1],
                                coefs, False)
            s_ref[pl.ds(r0, SUBLANES), :st] = xr
            s_ref[pl.ds(r0, SUBLANES), st:] = xi
            return xr[SUBLANES - 1:, :], xi[SUBLANES - 1:, :]

        zero = jnp.zeros((1, st), F32)
        lax.fori_loop(0, rows // SUBLANES, tile, (zero, zero))
        for c0 in range(0, rows, chunk):
            y = _dot_nt(s_ref[pl.ds(c0, chunk), :].astype(BF16), wc_ref[...]) + d_ref[...] * u_ref[pl.ds(c0, chunk), :]
            y_ref[pl.ds(c0, chunk), :] = y
            yg_ref[pl.ds(c0, chunk), :] = _gelu(y).astype(BF16)

    return pl.pallas_call(
        body, name="s5_fwd", grid=(nb,),
        in_specs=[pl.BlockSpec((rows, LANES), lambda j: (0, j)), pl.BlockSpec((None, LANES, s2), lambda j: (j, 0, 0)),
                  pl.BlockSpec((None, LANES, s2), lambda j: (j, 0, 0)), pl.BlockSpec((1, LANES), lambda j: (0, j)),
                  pl.BlockSpec((None, 1, s2), lambda j: (j, 0, 0))],
        out_specs=[pl.BlockSpec((rows, s2), lambda j: (0, j)), pl.BlockSpec((rows, LANES), lambda j: (0, j)),
                   pl.BlockSpec((rows, LANES), lambda j: (0, j))],
        out_shape=[jax.ShapeDtypeStruct((rows, nb * s2), F32), jax.ShapeDtypeStruct((rows, nb * LANES), F32),
                   jax.ShapeDtypeStruct((rows, nb * LANES), BF16)],
        compiler_params=_params(("parallel",)),
    )(proj, wb, wct, d_skip, abar)


def _s5_bwd(proj, states, y_pre, dyg_a, dyg_b, wb, wct, d_skip, abar):
    rows = proj.shape[0]
    nb = wb.shape[0]
    s2 = 2 * STATE_PER_BATCH
    st = STATE_PER_BATCH
    chunk = _tile(rows, 512, SUBLANES)
    n_tiles = rows // SUBLANES

    def body(u_ref, s_ref, y_ref, ga_ref, gb_ref, wb_ref, wc_ref, d_ref, a_ref,
             du_ref, dwb_ref, dwc_ref, da_ref, dd_ref, ds_ref, dy_ref):
        dy_ref[...] = (ga_ref[...] + gb_ref[...]) * _gelu_grad(y_ref[...])
        dd_ref[...] = jnp.sum(dy_ref[...] * u_ref[...], axis=0, keepdims=True)
        for c0 in range(0, rows, chunk):
            ds_ref[pl.ds(c0, chunk), :] = _dot_nn(dy_ref[pl.ds(c0, chunk), :].astype(BF16), wc_ref[...])
        dwc_ref[...] = _dot_tn(dy_ref[...].astype(BF16), s_ref[...].astype(BF16))
        av = a_ref[...]
        coefs = _scan_coefs(av[:, :st], -av[:, st:], reverse=True)
        row = lax.broadcasted_iota(jnp.int32, (SUBLANES, st), 0)

        def tile(k, carry):
            cr, ci, acc_r, acc_i = carry
            b = n_tiles - 1 - k
            r0 = pl.multiple_of(b * SUBLANES, SUBLANES)
            rp = pl.multiple_of(jnp.maximum(b - 1, 0) * SUBLANES, SUBLANES)
            xr, xi = _scan_tile(ds_ref[pl.ds(r0, SUBLANES), :st], ds_ref[pl.ds(r0, SUBLANES), st:], cr, ci, coefs, True)
            ds_ref[pl.ds(r0, SUBLANES), :st] = xr
            ds_ref[pl.ds(r0, SUBLANES), st:] = xi
            first = jnp.where(b > 0, 1.0, 0.0)
            pr = jnp.where(row == 0, pltpu.roll(s_ref[pl.ds(rp, SUBLANES), :st], 1, 0) * first,
                           pltpu.roll(s_ref[pl.ds(r0, SUBLANES), :st], 1, 0))
            pi = jnp.where(row == 0, pltpu.roll(s_ref[pl.ds(rp, SUBLANES), st:], 1, 0) * first,
                           pltpu.roll(s_ref[pl.ds(r0, SUBLANES), st:], 1, 0))
            acc_r = acc_r + pr * xr + pi * xi
            acc_i = acc_i + pr * xi - pi * xr
            return xr[:1, :], xi[:1, :], acc_r, acc_i

        zero = jnp.zeros((1, st), F32)
        zacc = jnp.zeros((SUBLANES, st), F32)
        _, _, acc_r, acc_i = lax.fori_loop(0, n_tiles, tile, (zero, zero, zacc, zacc))
        da_ref[:, :st] = jnp.sum(acc_r, axis=0, keepdims=True)
        da_ref[:, st:] = jnp.sum(acc_i, axis=0, keepdims=True)
        for c0 in range(0, rows, chunk):
            du_ref[pl.ds(c0, chunk), :] = (_dot_nt(ds_ref[pl.ds(c0, chunk), :].astype(BF16), wb_ref[...])
                                           + d_ref[...] * dy_ref[pl.ds(c0, chunk), :]).astype(du_ref.dtype)
        dwb_ref[...] = _dot_tn(u_ref[...].astype(BF16), ds_ref[...].astype(BF16))

    col = pl.BlockSpec((rows, LANES), lambda j: (0, j))
    return pl.pallas_call(
        body, name="s5_bwd", grid=(nb,),
        in_specs=[col, pl.BlockSpec((rows, s2), lambda j: (0, j)), col, col, col,
                  pl.BlockSpec((None, LANES, s2), lambda j: (j, 0, 0)), pl.BlockSpec((None, LANES, s2), lambda j: (j, 0, 0)),
                  pl.BlockSpec((1, LANES), lambda j: (0, j)), pl.BlockSpec((None, 1, s2), lambda j: (j, 0, 0))],
        out_specs=[col, pl.BlockSpec((None, LANES, s2), lambda j: (j, 0, 0)),
                   pl.BlockSpec((None, LANES, s2), lambda j: (j, 0, 0)), pl.BlockSpec((None, 1, s2), lambda j: (j, 0, 0)),
                   pl.BlockSpec((1, LANES), lambda j: (0, j))],
        out_shape=[jax.ShapeDtypeStruct((rows, nb * LANES), BF16), jax.ShapeDtypeStruct((nb, LANES, s2), F32),
                   jax.ShapeDtypeStruct((nb, LANES, s2), F32), jax.ShapeDtypeStruct((nb, 1, s2), F32),
                   jax.ShapeDtypeStruct((1, nb * LANES), F32)],
        scratch_shapes=[pltpu.VMEM((rows, s2), F32), pltpu.VMEM((rows, LANES), F32)],
        compiler_params=_params(("parallel",)),
    )(proj, states, y_pre, dyg_a, dyg_b, wb, wct, d_skip, abar)


def _glu_norm_fwd(y_pre, z, w, *, tr=256):
    rows, width = y_pre.shape
    tr = _tile(rows, tr, SUBLANES)

    def body(y_ref, z_ref, w_ref, o_ref):
        v = _gelu(y_ref[...]) * jax.nn.sigmoid(z_ref[...])
        o_ref[...] = (v * _rms_rows(v) * w_ref[...]).astype(o_ref.dtype)

    blk = pl.BlockSpec((tr, width), lambda i: (i, 0))
    return pl.pallas_call(
        body, name="glu_norm_fwd", grid=(rows // tr,),
        in_specs=[blk, blk, pl.BlockSpec((1, width), lambda i: (0, 0))], out_specs=blk,
        out_shape=jax.ShapeDtypeStruct((rows, width), BF16), compiler_params=_params(("parallel",)),
    )(y_pre, z, w)


def _glu_norm_bwd(y_pre, z, w, dycat, *, tr=256):
    rows, width = y_pre.shape
    tr = _tile(rows, tr, SUBLANES)

    def body(y_ref, z_ref, w_ref, dy_ref, dz_ref, dg_ref, dw_ref, db_ref):
        yg = _gelu(y_ref[...])
        sg = jax.nn.sigmoid(z_ref[...])
        dv, dwp = _rmsnorm_bwd_rows(yg * sg, w_ref[...], dy_ref[...])
        dz = dv * yg * sg * (1.0 - sg)
        dz_ref[...] = dz.astype(dz_ref.dtype)
        dg_ref[...] = dv * sg
        dw_part = jnp.sum(dwp, axis=0, keepdims=True)
        db_part = jnp.sum(dz, axis=0, keepdims=True)

        @pl.when(pl.program_id(0) == 0)
        def _():
            dw_ref[...] = dw_part
            db_ref[...] = db_part

        @pl.when(pl.program_id(0) > 0)
        def _():
            dw_ref[...] += dw_part
            db_ref[...] += db_part

    blk = pl.BlockSpec((tr, width), lambda i: (i, 0))
    vec = pl.BlockSpec((1, width), lambda i: (0, 0))
    return pl.pallas_call(
        body, name="glu_norm_bwd", grid=(rows // tr,), in_specs=[blk, blk, vec, blk], out_specs=[blk, blk, vec, vec],
        out_shape=[jax.ShapeDtypeStruct((rows, width), BF16), jax.ShapeDtypeStruct((rows, width), F32)]
        + [jax.ShapeDtypeStruct((1, width), F32)] * 2,
        compiler_params=_params(("arbitrary",)),
    )(y_pre, z, w, dycat)


def _rope_tables(pos, freq, sign):
    rows = pos.shape[0]

    def body(p_ref, f_ref, s_ref, cos_ref, sin_ref):
        ang = p_ref[...] * f_ref[...]
        cos_ref[...] = jnp.cos(ang)
        sin_ref[...] = jnp.sin(ang) * s_ref[...]

    return pl.pallas_call(body, name="rope_tables", out_shape=[jax.ShapeDtypeStruct((rows, LANES), F32)] * 2)(pos, freq, sign)


def _rope(x, cos, sin_signed):
    half = QK_ROPE_DIM // 2
    src = lax.broadcasted_iota(jnp.int32, (LANES, LANES), 0)
    dst = lax.broadcasted_iota(jnp.int32, (LANES, LANES), 1)
    swap = jnp.where(jnp.logical_or(jnp.logical_and(dst < half, src == dst + half),
                                    jnp.logical_and(jnp.logical_and(dst >= half, dst < 2 * half), src == dst - half)),
                     1.0, 0.0).astype(F32)
    swapped = _dot_exact(x, swap, ((1,), (0,)))
    return x * cos + swapped * sin_signed


def _attn_prep(q, kv, proj, kpe_col, cos, sin, *, tr=256):
    rows = q.shape[0]
    heads = q.shape[1] // HEAD_SLOT
    tr = _tile(rows, tr, SUBLANES)

    def body(q_ref, kv_ref, kpe_ref, cos_ref, sin_ref, qc_ref, kc_ref, v_ref):
        c, s = cos_ref[...], sin_ref[...]
        kpe = _rope(kpe_ref[...], c, s).astype(BF16)
        for h in range(heads):
            nope = slice(h * HEAD_SLOT, h * HEAD_SLOT + LANES)
            pe = slice(h * HEAD_SLOT + LANES, (h + 1) * HEAD_SLOT)
            qc_ref[:, nope] = q_ref[:, nope].astype(BF16)
            qc_ref[:, pe] = _rope(q_ref[:, pe], c, s).astype(BF16)
            kc_ref[:, nope] = kv_ref[:, nope].astype(BF16)
            kc_ref[:, pe] = kpe
            v_ref[:, h * LANES:(h + 1) * LANES] = kv_ref[:, pe].astype(BF16)

    slots = pl.BlockSpec((tr, heads * HEAD_SLOT), lambda i: (i, 0))
    tab = pl.BlockSpec((tr, LANES), lambda i: (i, 0))
    return pl.pallas_call(
        body, name="attn_prep", grid=(rows // tr,),
        in_specs=[slots, slots, pl.BlockSpec((tr, LANES), lambda i: (i, kpe_col)), tab, tab],
        out_specs=[slots, slots, pl.BlockSpec((tr, heads * LANES), lambda i: (i, 0))],
        out_shape=[jax.ShapeDtypeStruct((rows, heads * HEAD_SLOT), BF16)] * 2
        + [jax.ShapeDtypeStruct((rows, heads * LANES), BF16)],
        compiler_params=_params(("parallel",)),
    )(q, kv, proj, cos, sin)


def _causal(tq, tk):
    return lax.broadcasted_iota(jnp.int32, (tq, tk), 1) <= lax.broadcasted_iota(jnp.int32, (tq, tk), 0)


def _attn_fwd(qc, kc, vb, *, scale, tq=512):
    rows = qc.shape[0]
    heads = qc.shape[1] // HEAD_SLOT
    tq = _tile(rows, tq, SUBLANES)
    tk = tq

    def body(q_ref, k_ref, v_ref, o_ref, lse_ref):
        i = pl.program_id(1)
        q = q_ref[...]

        def step(j, carry, diagonal):
            m, l, acc = carry
            k0 = pl.multiple_of(j * tk, tk)
            s = _dot_nt(q, k_ref[pl.ds(k0, tk), :]) * scale
            if diagonal:
                s = jnp.where(_causal(tq, tk), s, NEG_INF)
            m_new = jnp.maximum(m, jnp.max(s, axis=-1, keepdims=True))
            p = jnp.exp(s - m_new)
            alpha = jnp.exp(m - m_new)
            l = alpha * l + jnp.sum(p, axis=-1, keepdims=True)
            acc = alpha * acc + _dot_nn(p.astype(BF16), v_ref[pl.ds(k0, tk), :])
            return m_new, l, acc

        init = (jnp.full((tq, 1), NEG_INF, F32), jnp.zeros((tq, 1), F32), jnp.zeros((tq, LANES), F32))
        below = lax.fori_loop(0, i, lambda j, carry: step(j, carry, False), init)
        m, l, acc = step(i, below, True)
        o_ref[...] = acc / l
        lse_ref[...] = jnp.broadcast_to(m + jnp.log(l), (tq, LANES))

    return pl.pallas_call(
        body, name="attn_fwd", grid=(heads, rows // tq),
        in_specs=[pl.BlockSpec((tq, HEAD_SLOT), lambda h, i: (i, h)), pl.BlockSpec((rows, HEAD_SLOT), lambda h, i: (0, h)),
                  pl.BlockSpec((rows, LANES), lambda h, i: (0, h))],
        out_specs=[pl.BlockSpec((tq, LANES), lambda h, i: (i, h))] * 2,
        out_shape=[jax.ShapeDtypeStruct((rows, heads * LANES), F32)] * 2,
        compiler_params=_params(("parallel", "parallel")),
    )(qc, kc, vb)


def _attn_bwd(qc, kc, vb, o, do, lse, cos, sin, *, scale, tk=512):
    rows = qc.shape[0]
    heads = qc.shape[1] // HEAD_SLOT
    tk = _tile(rows, tk, SUBLANES)
    tq = tk
    nq = rows // tq

    def body(q_ref, k_ref, v_ref, o_ref, do_ref, lse_ref, cos_ref, sin_ref, dq_ref, dkv_ref, dkpe_ref, dq_acc, delta_ref):
        j = pl.program_id(1)

        @pl.when(j == 0)
        def _():
            dq_acc[...] = jnp.zeros_like(dq_acc)
            for r0 in range(0, rows, tq):
                d = jnp.sum(do_ref[pl.ds(r0, tq), :] * o_ref[pl.ds(r0, tq), :], axis=-1, keepdims=True)
                delta_ref[pl.ds(r0, tq), :] = jnp.broadcast_to(d, (tq, LANES))

        kb, vv = k_ref[...], v_ref[...]

        def step(i, carry, diagonal):
            dk, dv = carry
            q0 = pl.multiple_of(i * tq, tq)
            qb = q_ref[pl.ds(q0, tq), :]
            dob = do_ref[pl.ds(q0, tq), :].astype(BF16)
            s = _dot_nt(qb, kb) * scale
            p = jnp.exp(s - lse_ref[pl.ds(q0, tq), :1])
            if diagonal:
                p = jnp.where(_causal(tq, tk), p, 0.0)
            dv = dv + _dot_tn(p.astype(BF16), dob)
            ds = (p * (_dot_nt(dob, vv) - delta_ref[pl.ds(q0, tq), :1])).astype(BF16)
            dk = dk + _dot_tn(ds, qb)
            dq_acc[pl.ds(q0, tq), :] += _dot_nn(ds, kb)
            return dk, dv

        zero = (jnp.zeros((tk, HEAD_SLOT), F32), jnp.zeros((tk, LANES), F32))
        dk, dv = lax.fori_loop(j + 1, nq, lambda i, carry: step(i, carry, False), step(j, zero, True))
        dkv_ref[:, :LANES] = (dk[:, :LANES] * scale).astype(dkv_ref.dtype)
        dkv_ref[:, LANES:] = dv.astype(dkv_ref.dtype)
        dkpe_ref[...] = dk[:, LANES:] * scale

        @pl.when(j == nq - 1)
        def _():
            for r0 in range(0, rows, tq):
                dq = dq_acc[pl.ds(r0, tq), :] * scale
                dq_ref[pl.ds(r0, tq), :LANES] = dq[:, :LANES].astype(dq_ref.dtype)
                dq_ref[pl.ds(r0, tq), LANES:] = _rope(dq[:, LANES:], cos_ref[pl.ds(r0, tq), :],
                                                      -sin_ref[pl.ds(r0, tq), :]).astype(dq_ref.dtype)

    full_q = pl.BlockSpec((rows, HEAD_SLOT), lambda h, j: (0, h))
    full_v = pl.BlockSpec((rows, LANES), lambda h, j: (0, h))
    tab = pl.BlockSpec((rows, LANES), lambda h, j: (0, 0))
    return pl.pallas_call(
        body, name="attn_bwd", grid=(heads, rows // tk),
        in_specs=[full_q, pl.BlockSpec((tk, HEAD_SLOT), lambda h, j: (j, h)), pl.BlockSpec((tk, LANES), lambda h, j: (j, h)),
                  full_v, full_v, full_v, tab, tab],
        out_specs=[full_q, pl.BlockSpec((tk, HEAD_SLOT), lambda h, j: (j, h)), pl.BlockSpec((tk, LANES), lambda h, j: (j, h))],
        out_shape=[jax.ShapeDtypeStruct((rows, heads * HEAD_SLOT), BF16), jax.ShapeDtypeStruct((rows, heads * HEAD_SLOT), BF16),
                   jax.ShapeDtypeStruct((rows, heads * LANES), F32)],
        scratch_shapes=[pltpu.VMEM((rows, HEAD_SLOT), F32), pltpu.VMEM((rows, LANES), F32)],
        compiler_params=_params(("parallel", "arbitrary")),
    )(qc, kc, vb, o, do, lse, cos, sin)


def _kpe_bwd(dkpe_heads, cos, sin, *, tr=512):
    rows = dkpe_heads.shape[0]
    heads = dkpe_heads.shape[1] // LANES
    tr = _tile(rows, tr, 2 * SUBLANES)

    def body(d_ref, cos_ref, sin_ref, o_ref):
        acc = d_ref[:, :LANES]
        for h in range(1, heads):
            acc = acc + d_ref[:, h * LANES:(h + 1) * LANES]
        o_ref[...] = _rope(acc, cos_ref[...], -sin_ref[...]).astype(o_ref.dtype)

    tab = pl.BlockSpec((tr, LANES), lambda i: (i, 0))
    return pl.pallas_call(
        body, name="kpe_bwd", grid=(rows // tr,),
        in_specs=[pl.BlockSpec((tr, heads * LANES), lambda i: (i, 0)), tab, tab], out_specs=tab,
        out_shape=jax.ShapeDtypeStruct((rows, LANES), BF16), compiler_params=_params(("parallel",)),
    )(dkpe_heads, cos, sin)


CONV_ROWS = 128


def _with_halo(ref, r0, ci, n_chunks, ch, lanes, before, after):
    parts = []
    if before:
        lo = pl.multiple_of(jnp.maximum(r0 - SUBLANES, 0), SUBLANES)
        parts.append(ref[pl.ds(lo, SUBLANES), lanes] * jnp.where(ci > 0, 1.0, 0.0))
    parts.append(ref[pl.ds(r0, ch), lanes])
    if after:
        hi = pl.multiple_of(jnp.minimum(r0 + ch, n_chunks * ch - SUBLANES), SUBLANES)
        parts.append(ref[pl.ds(hi, SUBLANES), lanes] * jnp.where(ci < n_chunks - 1, 1.0, 0.0))
    return jnp.concatenate(parts, axis=0)


def _taps(ext):
    return pltpu.roll(ext, 2, 0)[SUBLANES:], pltpu.roll(ext, 1, 0)[SUBLANES:], ext[SUBLANES:]


def _conv3(taps, w, b):
    return w[0:1, :] * taps[0] + w[1:2, :] * taps[1] + w[2:3, :] * taps[2] + b


def _conv_gate_fwd(a, conv_w, conv_b, *, tc=256):
    rows, f2 = a.shape
    f = f2 // 2
    tc = _tile(f, tc)
    nc = f // tc
    ch = _tile(rows, CONV_ROWS, SUBLANES)
    n_chunks = rows // ch

    def body(ag_ref, av_ref, wg_ref, wv_ref, bg_ref, bv_ref, o_ref):
        for lt in range(tc // LANES):
            lanes = slice(lt * LANES, (lt + 1) * LANES)
            wg, wv, bg, bv = wg_ref[:, lanes], wv_ref[:, lanes], bg_ref[:, lanes], bv_ref[:, lanes]

            def chunk(ci, carry):
                r0 = pl.multiple_of(ci * ch, ch)
                gate = _conv3(_taps(_with_halo(ag_ref, r0, ci, n_chunks, ch, lanes, True, False)), wg, bg)
                val = _conv3(_taps(_with_halo(av_ref, r0, ci, n_chunks, ch, lanes, True, False)), wv, bv)
                o_ref[pl.ds(r0, ch), lanes] = (gate * jax.nn.sigmoid(gate) * val).astype(o_ref.dtype)
                return carry

            lax.fori_loop(0, n_chunks, chunk, 0)

    return pl.pallas_call(
        body, name="conv_gate_fwd", grid=(nc,),
        in_specs=[pl.BlockSpec((rows, tc), lambda j: (0, j)), pl.BlockSpec((rows, tc), lambda j: (0, j + nc)),
                  pl.BlockSpec((SUBLANES, tc), lambda j: (0, j)), pl.BlockSpec((SUBLANES, tc), lambda j: (0, j + nc)),
                  pl.BlockSpec((1, tc), lambda j: (0, j)), pl.BlockSpec((1, tc), lambda j: (0, j + nc))],
        out_specs=pl.BlockSpec((rows, tc), lambda j: (0, j)),
        out_shape=jax.ShapeDtypeStruct((rows, f), BF16), compiler_params=_params(("parallel",)),
    )(a, a, conv_w, conv_w, conv_b, conv_b)


def _conv_gate_bwd(a, conv_w, conv_b, dg, *, tc=256):
    rows, f2 = a.shape
    f = f2 // 2
    tc = _tile(f, tc)
    nc = f // tc
    ch = _tile(rows, CONV_ROWS, SUBLANES)
    n_chunks = rows // ch
    ext_rows = ch + SUBLANES

    def fold(x):
        return jnp.sum(x.reshape(ch // SUBLANES, SUBLANES, LANES), axis=0)

    def body(ag_ref, av_ref, wg_ref, wv_ref, bg_ref, bv_ref, dg_ref, da_ref, dw_ref, db_ref):
        for lt in range(tc // LANES):
            lanes = slice(lt * LANES, (lt + 1) * LANES)
            wg, wv, bg, bv = wg_ref[:, lanes], wv_ref[:, lanes], bg_ref[:, lanes], bv_ref[:, lanes]

            def chunk(ci, acc):
                r0 = pl.multiple_of(ci * ch, ch)
                taps_g = _taps(_with_halo(ag_ref, r0, ci, n_chunks, ch, lanes, True, True))
                taps_v = _taps(_with_halo(av_ref, r0, ci, n_chunks, ch, lanes, True, True))
                dge = _with_halo(dg_ref, r0, ci, n_chunks, ch, lanes, False, True)
                gate, val = _conv3(taps_g, wg, bg), _conv3(taps_v, wv, bv)
                sg = jax.nn.sigmoid(gate)
                d_gate = dge * val * sg * (1.0 + gate * (1.0 - sg))
                d_val = dge * gate * sg
                new = []
                for half, (taps, w, d) in enumerate(((taps_g, wg, d_gate), (taps_v, wv, d_val))):
                    da = (w[2:3, :] * d[:ch] + w[1:2, :] * pltpu.roll(d, ext_rows - 1, 0)[:ch]
                          + w[0:1, :] * pltpu.roll(d, ext_rows - 2, 0)[:ch])
                    da_ref[half, pl.ds(r0, ch), lanes] = da.astype(da_ref.dtype)
                    dc = d[:ch]
                    sums = [fold(dc)] + [fold(dc * t[:ch]) for t in taps]
                    new.append(tuple(x + s for x, s in zip(acc[half], sums)))
                return tuple(new)

            zero = tuple(jnp.zeros((SUBLANES, LANES), F32) for _ in range(4))
            acc = lax.fori_loop(0, n_chunks, chunk, (zero, zero))
            row = lax.broadcasted_iota(jnp.int32, (SUBLANES, LANES), 0)
            for half in range(2):
                db, *taps = (jnp.sum(x, axis=0, keepdims=True) for x in acc[half])
                db_ref[half, :, lanes] = db
                dw = jnp.zeros((SUBLANES, LANES), F32)
                for tap in range(3):
                    dw = jnp.where(row == tap, taps[tap], dw)
                dw_ref[half, :, lanes] = dw

    lo = lambda j: (0, j)
    hi = lambda j: (0, j + nc)
    both = lambda j: (0, 0, j)
    return pl.pallas_call(
        body, name="conv_gate_bwd", grid=(nc,),
        in_specs=[pl.BlockSpec((rows, tc), lo), pl.BlockSpec((rows, tc), hi), pl.BlockSpec((SUBLANES, tc), lo),
                  pl.BlockSpec((SUBLANES, tc), hi), pl.BlockSpec((1, tc), lo), pl.BlockSpec((1, tc), hi),
                  pl.BlockSpec((rows, tc), lo)],
        out_specs=[pl.BlockSpec((2, rows, tc), both), pl.BlockSpec((2, SUBLANES, tc), both), pl.BlockSpec((2, 1, tc), both)],
        out_shape=[jax.ShapeDtypeStruct((2, rows, f), BF16), jax.ShapeDtypeStruct((2, SUBLANES, f), F32),
                   jax.ShapeDtypeStruct((2, 1, f), F32)],
        compiler_params=_params(("parallel",)),
    )(a, a, conv_w, conv_w, conv_b, conv_b, dg)


def _wgrad(a, b, rows, cols, row_sharded, name, **kw):
    return functools.partial(_wgrad_half, a, b, rows, cols, row_sharded, name, **kw)


class _NoExchange:
    def __init__(self, later, ffn):
        self.later, self.ffn = later, ffn

    def mixer_weights(self, after):
        return self.later

    def ffn_weights_arrived(self, after):
        return None

    def ffn_weights(self, after):
        return self.ffn

    def ffn_down_arrived(self, after):
        return None

    def ffn_down_weight(self, after):
        return self.ffn["ffn_w_down"]

    def ffn_grads(self, makers, after):
        self.ffn_makers = makers
        return None

    def ffn_backward_done(self, after):
        return None


def _local_step(x, posf, target, w, hooks):
    rows, d = x.shape
    width = w["ssm_d"].shape[1]
    qr, kvr = w["mla_q_norm_w"].shape[1], w["mla_kv_norm_w"].shape[1]
    heads = w["mla_w_ukv"].shape[1] // HEAD_SLOT
    f2 = w["ffn_conv_b"].shape[1]
    inp = w["w_in"].shape[0]
    scale = (QK_NOPE_DIM + QK_ROPE_DIM) ** -0.5
    g = {}

    hn = _rmsnorm_fwd(x, w["attn_norm_w"], name="attn_norm")
    proj = _matmul(hn, w["w_in"], mode="nt", name="in_proj")

    s5_weights = (w["ssm_lambda_re"], w["ssm_lambda_im"], w["ssm_log_dt"], w["ssm_b_re"], w["ssm_b_im"])
    wb, wct, abar = _s5_bands(*s5_weights, w["ssm_c_re"], w["ssm_c_im"])
    states, y_pre, yg = _s5_fwd(proj, wb, wct, w["ssm_d"], abar)
    later = hooks.mixer_weights(yg)
    z = _matmul(yg, later["ssm_w_glu"], mode="nn", name="glu_proj", bias=w["ssm_b_glu"])
    ys = _glu_norm_fwd(y_pre, z, w["ssm_out_norm_w"])

    q_col, kv_col, kpe_col = width // qr, (width + qr) // kvr, (width + qr + kvr) // LANES
    assert width % qr == 0 and (width + qr) % kvr == 0
    qn = _rmsnorm_fwd(proj, w["mla_q_norm_w"], name="q_norm", width=qr, col=q_col)
    kvn = _rmsnorm_fwd(proj, w["mla_kv_norm_w"], name="kv_norm", width=kvr, col=kv_col)
    q = _matmul(qn, w["mla_w_uq"], mode="nn", name="q_proj")
    kv = _matmul(kvn, w["mla_w_ukv"], mode="nn", name="kv_proj")
    half = QK_ROPE_DIM // 2
    inv_freq = ROPE_THETA ** (-jnp.arange(0, QK_ROPE_DIM, 2, dtype=F32) / QK_ROPE_DIM)
    zeros = jnp.zeros((LANES - QK_ROPE_DIM,), F32)
    freq = jnp.concatenate([inv_freq, inv_freq, zeros]).reshape(1, LANES)
    sign = jnp.concatenate([-jnp.ones((half,), F32), jnp.ones((half,), F32), zeros]).reshape(1, LANES)
    cos, sin = _rope_tables(posf, freq, sign)
    qc, kc, vb = _attn_prep(q, kv, proj, kpe_col, cos, sin)
    o, lse = _attn_fwd(qc, kc, vb, scale=scale, tq=ATTN_BLOCK)
    ym = _rmsnorm_fwd(o, w["mla_out_norm_w"], name="mla_out_norm")
    ycat = jnp.concatenate([ys, ym], axis=1)
    h1 = _matmul(ycat, later["w_out"], mode="nn", name="out_proj", add=x, after=hooks.ffn_weights_arrived(ycat))

    hn2 = _rmsnorm_fwd(h1, w["ffn_norm_w"], name="ffn_norm")
    ffn = hooks.ffn_weights(hn2)
    a = _matmul(hn2, ffn["ffn_w_up"], mode="nn", name="ffn_up", tm=FFN_ROWS)
    started = hooks.ffn_down_arrived(a)
    conv_b = w["ffn_conv_b"] if started is None else w["ffn_conv_b"] + started[:1, :1]
    gated = _conv_gate_fwd(a, ffn["ffn_conv_w"], conv_b)
    w_down = hooks.ffn_down_weight(gated)
    h2 = _matmul(gated, w_down, mode="nn", name="ffn_down", add=h1, tk=2816, tm=FFN_ROWS)
    loss_tile, dh2, dh2_mxu, g["final_norm_w"] = _final_norm_loss(h2, w["final_norm_w"], target)

    dgated = _matmul(dh2_mxu, w_down, mode="nt", name="ffn_down_dx", tm=FFN_ROWS)
    da, dcw, dcb = _conv_gate_bwd(a, ffn["ffn_conv_w"], w["ffn_conv_b"], dgated)
    g["ffn_conv_w"] = jnp.concatenate([dcw[0, :3], dcw[1, :3]], axis=1)
    g["ffn_conv_b"] = jnp.concatenate([dcb[0], dcb[1]], axis=1)
    started = hooks.ffn_grads({
        "ffn_w_up": _wgrad(hn2, da, d, f2, False, "ffn_up_dw", b_split=True, tn=_tile(f2 // N_CHIPS, 1408)),
        "ffn_w_down": _wgrad(gated, dh2_mxu, f2 // 2, d, True, "ffn_down_dw", tm=f2 // 2 // N_CHIPS, tn=512)}, dcb)
    dhn2 = _matmul(da, ffn["ffn_w_up"], mode="nt", name="ffn_up_dx", a_split=True, tk=_tile(f2 // 2, 2816), tm=FFN_ROWS,
                   after=started)
    dh1, dh1_mxu, g["ffn_norm_w"] = _rmsnorm_bwd(h1, w["ffn_norm_w"], dhn2, name="ffn_norm_bwd", add=dh2,
                                                dx_dtypes=(F32, BF16))

    dycat = _matmul(dh1_mxu, later["w_out"], mode="nt", name="out_proj_dx")
    g["w_out"] = _wgrad(ycat, dh1_mxu, 2 * width, d, True, "out_proj_dw")
    started = hooks.ffn_backward_done(dycat)
    mla_out_norm_w, ssm_out_norm_w = w["mla_out_norm_w"], w["ssm_out_norm_w"]
    if started is not None:
        mla_out_norm_w, ssm_out_norm_w = mla_out_norm_w + started[:1, :1], ssm_out_norm_w + started[:1, :1]

    do, g["mla_out_norm_w"] = _rmsnorm_bwd(o, mla_out_norm_w, dycat, name="mla_out_norm_bwd", width=width, dy_col=1)
    dq, dkv, dkpe_heads = _attn_bwd(qc, kc, vb, o, do, lse, cos, sin, scale=scale, tk=ATTN_BLOCK)
    dkpe = _kpe_bwd(dkpe_heads, cos, sin)
    g["mla_w_uq"] = _wgrad(qn, dq, qr, heads * HEAD_SLOT, False, "q_proj_dw")
    dqn = _matmul(dq, w["mla_w_uq"], mode="nt", name="q_proj_dx")
    dcq, g["mla_q_norm_w"] = _rmsnorm_bwd(proj, w["mla_q_norm_w"], dqn, name="q_norm_bwd", width=qr, col=q_col,
                                          dx_dtypes=(BF16,))
    g["mla_w_ukv"] = _wgrad(kvn, dkv, kvr, heads * HEAD_SLOT, False, "kv_proj_dw")
    dkvn = _matmul(dkv, w["mla_w_ukv"], mode="nt", name="kv_proj_dx")
    dckv, g["mla_kv_norm_w"] = _rmsnorm_bwd(proj, w["mla_kv_norm_w"], dkvn, name="kv_norm_bwd", width=kvr, col=kv_col,
                                            dx_dtypes=(BF16,))

    dz, dyg_a, g["ssm_out_norm_w"], g["ssm_b_glu"] = _glu_norm_bwd(y_pre, z, ssm_out_norm_w, dycat)
    dyg_b = _matmul(dz, later["ssm_w_glu"], mode="nt", name="glu_proj_dx")
    g["ssm_w_glu"] = _wgrad(yg, dz, width, width, True, "glu_proj_dw")
    du, dwb, dwct, dabar, g["ssm_d"] = _s5_bwd(proj, states, y_pre, dyg_a, dyg_b, wb, wct, w["ssm_d"], abar)
    (g["ssm_lambda_re"], g["ssm_lambda_im"], g["ssm_log_dt"], g["ssm_b_re"], g["ssm_b_im"], g["ssm_c_re"],
     g["ssm_c_im"]) = _s5_bands_bwd(*s5_weights, dwb, dwct, dabar)

    pad = jnp.zeros((rows, inp - (width + qr + kvr + LANES)), BF16)
    dproj = jnp.concatenate([du, dcq, dckv, dkpe, pad], axis=1)
    g["w_in"] = _wgrad(dproj, hn, inp, d, False, "in_proj_dw")
    dhn = _matmul(dproj, w["w_in"], mode="nn", name="in_proj_dx")
    dx, g["attn_norm_w"] = _rmsnorm_bwd(x, w["attn_norm_w"], dhn, name="attn_norm_bwd", add=dh1)
    return loss_tile, dx, g


ANY = pl.BlockSpec(memory_space=pl.ANY)
MESH = pl.DeviceIdType.MESH


def _mesh_pos():
    return lax.axis_index("x"), lax.axis_index("y"), lax.axis_index("c")


def _other_chips(x, y):
    return [(1 - x, y), (x, 1 - y), (1 - x, 1 - y)]


def _remote(src, dst, send_sems, recv_sems, k, to):
    return pltpu.make_async_remote_copy(src_ref=src, dst_ref=dst, send_sem=send_sems.at[k], recv_sem=recv_sems.at[k],
                                        device_id=to, device_id_type=MESH)


def _place_shard(shard, piece_idx, row_sharded, name, out_dtype=BF16, pieces=N_CHIPS, after=None):
    rs, cs = shard.shape
    tr = _tile(rs, 256, 2 * SUBLANES)
    rb = rs // tr
    extra = [] if after is None else [after]

    def body(p_ref, x_ref, *rest):
        o_ref = rest[-1]
        o_ref[...] = x_ref[...].astype(o_ref.dtype)

    if row_sharded:
        out_shape, out_map = (pieces * rs, cs), (lambda i, p_ref: (p_ref[0] * rb + i, 0))
    else:
        out_shape, out_map = (rs, pieces * cs), (lambda i, p_ref: (i, p_ref[0]))
    return pl.pallas_call(
        body, name=name, out_shape=jax.ShapeDtypeStruct(out_shape, out_dtype),
        grid_spec=pltpu.PrefetchScalarGridSpec(
            num_scalar_prefetch=1, grid=(rb,),
            in_specs=[pl.BlockSpec((tr, cs), lambda i, p_ref: (i, 0))] + [pl.BlockSpec(memory_space=pl.ANY)] * len(extra),
            out_specs=pl.BlockSpec((tr, cs), out_map)),
        compiler_params=_params(("parallel",)),
    )(piece_idx, shard, *extra)


def _gather_weights(placed, name):
    n = len(placed)
    meta = [(row_sharded, direct) for _, row_sharded, direct in placed]
    over_ici, over_d2d = _gather_plans(meta)
    forwarded = [t for t, (_, direct) in enumerate(meta) if not direct]

    def body(*refs):
        outs = refs[n:2 * n]
        send_sems, recv_sems, pass_send_sems, pass_recv_sems = refs[2 * n:]
        first, arrivals = over_ici(outs, send_sems, recv_sems)
        passed, passed_arrivals = over_d2d([outs[t] for t in forwarded], pass_send_sems, pass_recv_sems)
        for cp in first:
            cp.start()
        for t in range(n):
            for j in range(3):
                arrivals[3 * t + j].wait_recv()
                if t in forwarded:
                    passed[3 * forwarded.index(t) + j].start()
        for cp in passed_arrivals:
            cp.wait_recv()
        for cp in first + passed:
            cp.wait_send()

    return pl.pallas_call(
        body, name=name, in_specs=[ANY] * n, out_specs=[ANY] * n,
        out_shape=[jax.ShapeDtypeStruct(arr.shape, arr.dtype) for arr, _, _ in placed],
        input_output_aliases={t: t for t in range(n)},
        scratch_shapes=[pltpu.SemaphoreType.DMA((3 * n,)), pltpu.SemaphoreType.DMA((3 * n,)),
                        pltpu.SemaphoreType.DMA((3 * len(forwarded),)), pltpu.SemaphoreType.DMA((3 * len(forwarded),))],
    )(*[arr for arr, _, _ in placed])


def _gather_plans(meta):
    def window(ref, row_sharded, piece, half):
        r, cc = ref.shape
        if row_sharded:
            rs = r // N_CHIPS
            if half is None:
                return ref.at[pl.ds(piece * rs, rs), :]
            return ref.at[pl.ds(piece * rs + half * (rs // 2), rs // 2), :]
        cs = cc // N_CHIPS
        if half is None:
            return ref.at[:, pl.ds(piece * cs, cs)]
        return ref.at[pl.ds(half * (r // 2), r // 2), pl.ds(piece * cs, cs)]

    def over_ici(refs, send_sems, recv_sems):
        x, y, c = _mesh_pos()
        sends, recvs = [], []
        for t, (row_sharded, direct) in enumerate(meta):
            mine = window(refs[t], row_sharded, 2 * x + y, None if direct else c)
            for j, (px, py) in enumerate(_other_chips(x, y)):
                theirs = window(refs[t], row_sharded, 2 * px + py, None if direct else c)
                sends.append(_remote(mine, mine, send_sems, recv_sems, 3 * t + j, (px, py, c)))
                recvs.append(_remote(theirs, theirs, send_sems, recv_sems, 3 * t + j, (px, py, c)))
        return sends, recvs

    def over_d2d(refs, send_sems, recv_sems):
        x, y, c = _mesh_pos()
        sends, recvs = [], []
        rows = [row_sharded for row_sharded, direct in meta if not direct]
        for t, row_sharded in enumerate(rows):
            for j, (px, py) in enumerate(_other_chips(x, y)):
                got = window(refs[t], row_sharded, 2 * px + py, c)
                other = window(refs[t], row_sharded, 2 * px + py, 1 - c)
                sends.append(_remote(got, got, send_sems, recv_sems, 3 * t + j, (x, y, 1 - c)))
                recvs.append(_remote(other, other, send_sems, recv_sems, 3 * t + j, (x, y, 1 - c)))
        return sends, recvs

    return over_ici, over_d2d


HBM = pl.BlockSpec(memory_space=pltpu.HBM)
SEMAPHORES = pl.BlockSpec(memory_space=pltpu.SEMAPHORE)
DATAFLOW = pltpu.SideEffectType.DATAFLOW_SIDE_EFFECTING


def _start_copies(name, arrays, plan, n_copies, after):
    n = len(arrays)

    def body(*refs):
        sends, _ = plan(refs[:n], refs[n + 1], refs[n + 2])
        for cp in sends:
            cp.start()
        token = refs[2 * n + 3]
        token[...] = jnp.zeros_like(token)

    out = pl.pallas_call(
        body, name=name,
        out_shape=(pltpu.SemaphoreType.DMA((n_copies,)), pltpu.SemaphoreType.DMA((n_copies,)),
                   *[pltpu.HBM(a.shape, a.dtype) for a in arrays], jax.ShapeDtypeStruct((SUBLANES, LANES), F32)),
        in_specs=[HBM] * n + [ANY],
        out_specs=(SEMAPHORES, SEMAPHORES, *[HBM] * n, pl.BlockSpec(memory_space=pltpu.VMEM)),
        input_output_aliases={t: t + 2 for t in range(n)},
        compiler_params=pltpu.CompilerParams(has_side_effects=DATAFLOW),
    )(*[pltpu.with_memory_space_constraint(a, pltpu.HBM) for a in arrays], after)
    return out[0], out[1], list(out[2:2 + n]), out[2 + n]


def _wait_copies(name, started, plan, after):
    send_sems, recv_sems, arrays, _ = started
    n = len(arrays)

    def body(*refs):
        sends, recvs = plan(refs[:n], refs[n], refs[n + 1])
        for cp in sends:
            cp.wait_send()
        for cp in recvs:
            cp.wait_recv()

    out = pl.pallas_call(
        body, name=name, out_shape=[pltpu.HBM(a.shape, a.dtype) for a in arrays],
        in_specs=[HBM] * n + [SEMAPHORES, SEMAPHORES, ANY], out_specs=[HBM] * n,
        input_output_aliases={t: t for t in range(n)},
        compiler_params=pltpu.CompilerParams(has_side_effects=DATAFLOW),
    )(*arrays, send_sems, recv_sems, after)
    return list(out)


def _exchange(name, arrays, plan, n_copies, after=None):
    n = len(arrays)
    extra = [] if after is None else [after]

    def body(*refs):
        outs = refs[n + len(extra):2 * n + len(extra)]
        send_sems, recv_sems = refs[2 * n + len(extra):]
        sends, recvs = plan(outs, send_sems, recv_sems)
        for cp in sends:
            cp.start()
        for cp in recvs:
            cp.wait_recv()
        for cp in sends:
            cp.wait_send()

    return pl.pallas_call(
        body, name=name, in_specs=[ANY] * (n + len(extra)), out_specs=[ANY] * n,
        out_shape=[jax.ShapeDtypeStruct(a.shape, a.dtype) for a in arrays],
        input_output_aliases={t: t for t in range(n)},
        scratch_shapes=[pltpu.SemaphoreType.DMA((n_copies,)), pltpu.SemaphoreType.DMA((n_copies,))],
    )(*arrays, *extra)


def _give_plan(n):
    def plan(refs, send_sems, recv_sems):
        x, y, c = _mesh_pos()
        sends = [_remote(refs[t], refs[n + t], send_sems, recv_sems, t, (x, y, 1 - c)) for t in range(n)]
        return sends, sends

    return plan


def _scatter_plan(n):
    def plan(refs, send_sems, recv_sems):
        x, y, c = _mesh_pos()
        sends = []
        for t in range(n):
            for j, (px, py) in enumerate(_other_chips(x, y)):
                sends.append(_remote(refs[t].at[2 * px + py], refs[n + t].at[j], send_sems, recv_sems, 3 * t + j, (px, py, c)))
        return sends, sends

    return plan


def _scatter_shapes(sums):
    return [jax.ShapeDtypeStruct((3,) + s.shape[1:], s.dtype) for s in sums]


def _join_plan(n):
    def plan(refs, send_sems, recv_sems):
        x, y, c = _mesh_pos()
        sends = [_remote(refs[t].at[c], refs[t].at[c], send_sems, recv_sems, t, (x, y, 1 - c)) for t in range(n)]
        recvs = [_remote(refs[t].at[1 - c], refs[t].at[1 - c], send_sems, recv_sems, t, (x, y, 1 - c)) for t in range(n)]
        return sends, recvs

    return plan


def _join_halves(halves, name, after=None):
    return _exchange(name, halves, _join_plan(len(halves)), len(halves), after=after)


def _add_other_half(g4, got, where, name):
    _, pieces, sr, sc = g4.shape
    tr = _tile(sr, 256, 2 * SUBLANES)

    def body(w_ref, a_ref, b_ref, o_ref):
        o_ref[...] = a_ref[...] + b_ref[...]

    blk = pl.BlockSpec((None, tr, sc), lambda p, i, w_ref: (p, i, 0))
    return pl.pallas_call(
        body, name=name, out_shape=jax.ShapeDtypeStruct((pieces, sr, sc), F32),
        grid_spec=pltpu.PrefetchScalarGridSpec(
            num_scalar_prefetch=1, grid=(pieces, sr // tr),
            in_specs=[pl.BlockSpec((None, None, tr, sc), lambda p, i, w_ref: (w_ref[0], p, i, 0)), blk], out_specs=blk),
        compiler_params=_params(("parallel", "parallel")),
    )(where, g4, got)


def _add_pieces(sums, got_pieces, where, name, after=None):
    _, sr, sc = sums.shape
    tr = _tile(sr, 256, 2 * SUBLANES)
    extra = [] if after is None else [after]

    def body(w_ref, a_ref, r_ref, *rest):
        acc = a_ref[...]
        for j in range(3):
            acc = acc + r_ref[j].astype(F32)
        rest[-1][...] = acc

    return pl.pallas_call(
        body, name=name, out_shape=jax.ShapeDtypeStruct((N_CORES, sr, sc), F32),
        grid_spec=pltpu.PrefetchScalarGridSpec(
            num_scalar_prefetch=1, grid=(sr // tr,),
            in_specs=[pl.BlockSpec((None, tr, sc), lambda i, w_ref: (w_ref[1], i, 0)),
                      pl.BlockSpec((3, tr, sc), lambda i, w_ref: (0, i, 0))] + [pl.BlockSpec(memory_space=pl.ANY)] * len(extra),
            out_specs=pl.BlockSpec((None, tr, sc), lambda i, w_ref: (w_ref[0], i, 0))),
        compiler_params=_params(("parallel",)),
    )(where, sums, got_pieces, *extra)


def _adamw_update(w, g, m, v):
    nm = ADAM_B1 * m + (1.0 - ADAM_B1) * g
    nv = ADAM_B2 * v + (1.0 - ADAM_B2) * (g * g)
    m_hat = nm / (1.0 - ADAM_B1 ** ADAM_STEP)
    v_hat = nv / (1.0 - ADAM_B2 ** ADAM_STEP)
    return -ADAM_LR * (m_hat / (jnp.sqrt(v_hat) + ADAM_EPS) + ADAM_WD * w), nm, nv


def _adamw(w, g, m, v, name, after=None):
    rows, cols = w.shape
    halves = 2 if g.ndim == 3 else 1
    bc = cols // halves
    tr = _tile(rows, max(SUBLANES, (1 << 19) // max(bc, 1) // SUBLANES * SUBLANES), SUBLANES)

    def body(w_ref, g_ref, m_ref, v_ref, *rest):
        d_ref, nm_ref, nv_ref, go_ref = rest[-4:]
        gv = g_ref[...]
        d_ref[...], nm_ref[...], nv_ref[...] = _adamw_update(w_ref[...], gv, m_ref[...], v_ref[...])
        go_ref[...] = gv

    blk = pl.BlockSpec((tr, bc), lambda i, h: (i, h))
    g_blk = pl.BlockSpec((None, tr, bc), lambda i, h: (h, i, 0)) if halves == 2 else blk
    extra = [] if after is None else [after]
    return pl.pallas_call(
        body, name=name, grid=(rows // tr, halves),
        in_specs=[blk, g_blk, blk, blk] + [pl.BlockSpec(memory_space=pl.ANY)] * len(extra), out_specs=[blk] * 4,
        out_shape=[jax.ShapeDtypeStruct((rows, cols), F32)] * 4, compiler_params=_params(("parallel", "parallel")),
    )(w, g, m, v, *extra)


def _adamw_many(ws, gs, ms, vs, name):
    n = len(ws)

    def body(*refs):
        outs = refs[4 * n:]
        for k in range(n):
            w_ref, g_ref, m_ref, v_ref = (refs[j * n + k] for j in range(4))
            outs[k][...], outs[n + k][...], outs[2 * n + k][...] = _adamw_update(w_ref[...], g_ref[...], m_ref[...], v_ref[...])

    out = pl.pallas_call(
        body, name=name, out_shape=[jax.ShapeDtypeStruct(w.shape, F32) for w in ws] * 3,
        compiler_params=pltpu.CompilerParams(vmem_limit_bytes=VMEM_LIMIT_BYTES),
    )(*ws, *gs, *ms, *vs)
    return out[:n], out[n:2 * n], out[2 * n:]


WEIGHTS = ['attn_norm_w', 'w_in', 'ssm_lambda_re', 'ssm_lambda_im', 'ssm_log_dt', 'ssm_b_re', 'ssm_b_im', 'ssm_c_re',
           'ssm_c_im', 'ssm_d', 'ssm_w_glu', 'ssm_b_glu', 'mla_q_norm_w', 'mla_w_uq', 'mla_kv_norm_w', 'mla_w_ukv',
           'ssm_out_norm_w', 'mla_out_norm_w', 'w_out', 'ffn_norm_w', 'ffn_w_up', 'ffn_conv_w', 'ffn_conv_b',
           'ffn_w_down', 'final_norm_w']
SHARDED = {'w_in': False, 'ssm_w_glu': True, 'mla_w_uq': False, 'mla_w_ukv': False, 'w_out': True, 'ffn_w_up': False,
           'ffn_w_down': True}
SMALL = [n for n in WEIGHTS if n not in SHARDED and n != 'ffn_conv_w']
ROPE_PAD = HEAD_SLOT - QK_NOPE_DIM - QK_ROPE_DIM
SMALL_COLS = 8 * LANES


def _pad_heads(w_uq, heads):
    qr = w_uq.shape[0]
    w3 = w_uq.reshape(qr, heads, QK_NOPE_DIM + QK_ROPE_DIM)
    return jnp.concatenate([w3, jnp.zeros((qr, heads, ROPE_PAD), w_uq.dtype)], axis=2).reshape(qr, heads * HEAD_SLOT)


def _unpad_heads(g_uq, heads):
    qr = g_uq.shape[0]
    return g_uq.reshape(qr, heads, HEAD_SLOT)[:, :, :QK_NOPE_DIM + QK_ROPE_DIM].reshape(qr, -1)


FFN = ['ffn_w_up', 'ffn_w_down']
MIXER_LATER = ['ssm_w_glu', 'w_out']
MIXER_BIG = ['w_in', 'w_out']
FFN_GATHER = FFN + ['ffn_conv_w']


class _Overlapped:
    def __init__(self, placed_first, first_sharding, where):
        self.where, self.mine, self.other = where, where[:1], 1 - where[:1]
        self.first_ici, self.first_d2d = _gather_plans([(r, False) for r in first_sharding])
        self.first = _start_copies("gather_first_start", placed_first, self.first_ici, 3 * len(placed_first), where)
        self.first_started = self.first[3]

    def start_rest(self, placed_later, placed):
        self.later_ici, self.later_d2d = _gather_plans([(SHARDED[n], False) for n in MIXER_LATER])
        self.later = _start_copies("gather_later_start", placed_later, self.later_ici, 3 * len(placed_later),
                                   self.first_started)
        up, down, taps = placed
        self.up_ici, self.up_d2d = _gather_plans([(SHARDED["ffn_w_up"], False), (False, True)])
        self.up = _start_copies("gather_ffn_up_start", [up, taps], self.up_ici, 6, self.later[3])
        self.down_ici, self.down_d2d = _gather_plans([(SHARDED["ffn_w_down"], False)])
        self.down = _start_copies("gather_ffn_down_start", [down], self.down_ici, 3, self.up[3])
        self.gather_started = self.down[3]
        arrived = _wait_copies("gather_first_wait", self.first, self.first_ici, self.gather_started)
        return _exchange("gather_first_pass", arrived, self.first_d2d, 3 * len(arrived))

    def mixer_weights(self, after):
        arrived = _wait_copies("gather_later_wait", self.later, self.later_ici, after)
        return dict(zip(MIXER_LATER, _exchange("gather_later_pass", arrived, self.later_d2d, 3 * len(arrived))))

    def ffn_weights_arrived(self, after):
        up, self.taps = _wait_copies("gather_ffn_up_wait", self.up, self.up_ici, after)
        self.up_passing = _start_copies("gather_ffn_up_pass_start", [up], self.up_d2d, 3, after)
        return self.up_passing[3]

    def ffn_weights(self, after):
        w_up, = _wait_copies("gather_ffn_up_pass_wait", self.up_passing, self.up_d2d, after)
        return {"ffn_w_up": w_up, "ffn_conv_w": self.taps}

    def ffn_down_arrived(self, after):
        down, = _wait_copies("gather_ffn_down_wait", self.down, self.down_ici, after)
        self.down_passing = _start_copies("gather_ffn_down_pass_start", [down], self.down_d2d, 3, after)
        return self.down_passing[3]

    def ffn_down_weight(self, after):
        return _wait_copies("gather_ffn_down_pass_wait", self.down_passing, self.down_d2d, after)[0]

    def ffn_grads(self, makers, after):
        self.makers = [makers[name] for name in FFN]
        n = len(FFN)
        give = [make(self.other, suffix="_give") for make in self.makers]
        lands = [lax.empty(g.shape, g.dtype) for g in give]
        self.swap = _start_copies("grad_ffn_swap_start", give + lands, _give_plan(n), n, after)
        return self.swap[3]

    def ffn_backward_done(self, after):
        n = len(FFN)
        got = _wait_copies("grad_ffn_swap_wait", self.swap, _give_plan(n), after)[n:]
        kept = [make(self.mine, suffix="_keep", add=got[t], wire=True) for t, make in enumerate(self.makers)]
        self.sums = [k[0] for k in kept]
        wires = [k[1] for k in kept]
        lands = [lax.empty(s.shape, s.dtype) for s in _scatter_shapes(wires)]
        self.scatter = _start_copies("grad_ffn_scatter_start", wires + lands, _scatter_plan(n), 3 * n, after)
        return self.scatter[3]

    def ffn_reduced(self, after):
        n = len(FFN)
        got_pieces = _wait_copies("grad_ffn_scatter_wait", self.scatter, _scatter_plan(n), after)[n:]
        halves = []
        for t, name in enumerate(FFN):
            halves.append(_add_pieces(self.sums[t], got_pieces[t], self.where, "grad_add_pieces_" + name,
                                      after=halves[-1] if halves else None))
        return halves


def _step(args):
    x, positions, target = args["x"][0], args["positions"], args["loss_target"][0]
    rows = x.shape[0]
    p = {n: args[n] for n in WEIGHTS}
    xi, yi, ci = _mesh_pos()
    piece = 2 * xi + yi

    def transposed(a):
        return jnp.swapaxes(a[0], 0, 1)

    def as_stored(n, a):
        return jnp.swapaxes(a, 2, 3) if n in ("ssm_b_re", "ssm_b_im") else a

    w_in = transposed(p["w_in"])
    in_width = w_in.shape[0]
    in_pad = (-in_width) % (2 * LANES)
    heads_here = p["mla_w_uq"].shape[2] // (QK_NOPE_DIM + QK_ROPE_DIM)
    shards = {
        "w_in": jnp.pad(w_in, ((0, in_pad), (0, 0))),
        "ssm_w_glu": p["ssm_w_glu"][0],
        "mla_w_uq": _pad_heads(p["mla_w_uq"][0], heads_here),
        "mla_w_ukv": p["mla_w_ukv"][0],
        "w_out": p["w_out"][0],
        "ffn_w_up": p["ffn_w_up"][0],
        "ffn_w_down": p["ffn_w_down"][0],
    }
    conv_w = jnp.pad(p["ffn_conv_w"][0], ((0, SUBLANES - p["ffn_conv_w"].shape[1]), (0, 0)))
    order = list(SHARDED)
    piece_idx = piece.reshape(1).astype(jnp.int32)
    mixer = [n for n in order if n not in FFN]
    first = [n for n in mixer if n not in MIXER_LATER]
    where = jnp.stack([ci, piece]).astype(jnp.int32)
    placed = {n: _place_shard(shards[n], piece_idx, SHARDED[n], "place_" + n) for n in first}
    hooks = _Overlapped([placed[n] for n in first], [SHARDED[n] for n in first], where)
    for n in order:
        if n not in first:
            placed[n] = _place_shard(shards[n], piece_idx, SHARDED[n], "place_" + n, after=hooks.first_started)
    placed["ffn_conv_w"] = _place_shard(conv_w, piece_idx, False, "place_ffn_conv_w", out_dtype=F32,
                                        after=hooks.first_started)
    w = dict(zip(first, hooks.start_rest([placed[n] for n in MIXER_LATER], [placed[n] for n in FFN_GATHER])))
    groups = p["ssm_lambda_re"].shape[1]
    w.update({
        "attn_norm_w": p["attn_norm_w"] + hooks.gather_started[:1, :1],
        "ssm_lambda_re": p["ssm_lambda_re"][0], "ssm_lambda_im": p["ssm_lambda_im"][0],
        "ssm_log_dt": p["ssm_log_dt"].reshape(groups, 1), "ssm_b_re": as_stored("ssm_b_re", p["ssm_b_re"])[0],
        "ssm_b_im": as_stored("ssm_b_im", p["ssm_b_im"])[0], "ssm_c_re": p["ssm_c_re"][0], "ssm_c_im": p["ssm_c_im"][0],
        "ssm_d": p["ssm_d"], "ssm_b_glu": p["ssm_b_glu"], "mla_q_norm_w": p["mla_q_norm_w"],
        "mla_kv_norm_w": p["mla_kv_norm_w"], "ssm_out_norm_w": p["ssm_out_norm_w"], "mla_out_norm_w": p["mla_out_norm_w"],
        "ffn_norm_w": p["ffn_norm_w"], "ffn_conv_b": p["ffn_conv_b"], "final_norm_w": p["final_norm_w"].reshape(1, -1),
    })

    loss_tile, dx, g = _local_step(x, positions.reshape(rows, 1).astype(F32), target, w, hooks)

    flat = [g[n].reshape(-1) for n in SMALL] + [g["ffn_conv_w"].reshape(-1), loss_tile[0, :1]]
    sizes = [f.shape[0] for f in flat]
    per_block = -(-sum(sizes) // (N_CORES * N_CHIPS * SMALL_COLS))
    small_rows = -(-per_block // (2 * SUBLANES)) * (2 * SUBLANES)
    padded = N_CORES * N_CHIPS * small_rows * SMALL_COLS

    def pack(parts):
        parts = list(parts)
        have = sum(q.shape[0] for q in parts)
        return jnp.concatenate(parts + [jnp.zeros((padded - have,), F32)])

    reduced = mixer + ["small"]
    small = pack(flat).reshape(N_CORES, N_CHIPS, small_rows, SMALL_COLS)
    give = [g[n](hooks.other, suffix="_give") for n in mixer] + [lax.dynamic_index_in_dim(small, 1 - ci, 0, keepdims=False)]
    lands = [lax.empty(a.shape, a.dtype) for a in give]
    give_plan = _give_plan(len(reduced))
    swap = _start_copies("grad_mixer_swap_start", give + lands, give_plan, len(reduced), dx)

    grads, delta, new_m, new_v = {}, {}, {}, {}

    def finish(n, joined, after=None):
        grad = joined if SHARDED[n] else joined.reshape(-1, joined.shape[2])
        if n == "w_in":
            wt, mt, vt = w_in, transposed(args["m_w_in"]), transposed(args["v_w_in"])
            out = _adamw(wt, grad, mt, vt, "adamw_w_in")
            delta[n], new_m[n], new_v[n], grads[n] = (jnp.swapaxes(a, 0, 1)[None] for a in out)
            return
        if n == "mla_w_uq":
            grad = _unpad_heads(grad, heads_here)
        adam(n, grad, after)

    def adam(n, grad, after=None):
        shape = p[n].shape
        out = _adamw(p[n].reshape(shape[1:]), grad, args["m_" + n].reshape(shape[1:]),
                     args["v_" + n].reshape(shape[1:]), "adamw_" + n, after)
        delta[n], new_m[n], new_v[n], grads[n] = (a.reshape(shape) for a in out)

    ffn_halves = hooks.ffn_reduced(swap[3])
    got = _wait_copies("grad_mixer_swap_wait", swap, give_plan, ffn_halves[-1])[len(reduced):]
    join_plan = _join_plan(len(FFN))
    ffn_join = _start_copies("grad_ffn_join_start", ffn_halves, join_plan, len(FFN), got[0])
    big = [t for t, n in enumerate(reduced) if n in MIXER_BIG]
    rest = [t for t in range(len(reduced)) if t not in big]
    sums, wires = {}, {}
    for t in big:
        sums[t], wires[t] = g[reduced[t]](hooks.mine, suffix="_keep", add=got[t], wire=True)
    ffn_joined = _wait_copies("grad_ffn_join_wait", ffn_join, join_plan, sums[big[-1]])

    def scatter_start(name, group, after):
        lands = [lax.empty(s.shape, s.dtype) for s in _scatter_shapes([wires[t] for t in group])]
        return _start_copies(name, [wires[t] for t in group] + lands, _scatter_plan(len(group)), 3 * len(group), after)

    scatter_big = scatter_start("grad_big_scatter_start", big, ffn_joined[0])
    for t in rest[:-1]:
        sums[t], wires[t] = g[reduced[t]](hooks.mine, suffix="_keep", add=got[t], wire=True, after=scatter_big[3])
    sums[rest[-1]] = wires[rest[-1]] = _add_other_half(small, got[-1], where, "grad_add_half_small")
    scatter_rest = scatter_start("grad_rest_scatter_start", rest, sums[rest[0]])
    behind = scatter_rest[3]
    for n, joined in zip(FFN, ffn_joined):
        finish(n, joined, after=behind)
        behind = delta[n]
    got_pieces = dict(zip(big, _wait_copies("grad_big_scatter_wait", scatter_big, _scatter_plan(len(big)),
                                            delta[FFN[-1]])[len(big):]))
    got_pieces.update(zip(rest, _wait_copies("grad_rest_scatter_wait", scatter_rest, _scatter_plan(len(rest)),
                                             got_pieces[big[0]])[len(rest):]))
    halves = [_add_pieces(sums[t], got_pieces[t], where, "grad_add_pieces_" + n) for t, n in enumerate(reduced)]
    joined = _join_halves(halves, "grad_join_halves")
    for n, j in zip(mixer, joined):
        finish(n, j)
    eighths = _place_shard(joined[-1].reshape(N_CORES * small_rows, SMALL_COLS), piece_idx, True, "place_small_grads",
                           out_dtype=F32)
    small_sum = _gather_weights([(eighths, True, False)], "gather_small_grads")[0]
    flat_sum = small_sum.reshape(N_CHIPS, N_CORES, small_rows * SMALL_COLS).transpose(1, 0, 2).reshape(-1)
    offs = [0]
    for s in sizes:
        offs.append(offs[-1] + s)
    stored = {n: as_stored(n, p[n]) for n in SMALL}
    for k, n in enumerate(SMALL):
        grads[n] = flat_sum[offs[k]:offs[k + 1]].reshape(stored[n].shape)
    taps, cols_here = p["ffn_conv_w"].shape[1], p["ffn_conv_w"].shape[2]
    conv_full = flat_sum[offs[len(SMALL)]:offs[len(SMALL) + 1]].reshape(taps, N_CHIPS * cols_here)
    adam("ffn_conv_w", lax.dynamic_slice_in_dim(conv_full, piece * cols_here, cols_here, axis=1))
    loss = flat_sum[offs[len(SMALL) + 1]]

    def rank2(a):
        return a.reshape(1, -1) if a.ndim == 1 else a

    d_s, m_s, v_s = _adamw_many([rank2(stored[n]) for n in SMALL], [rank2(grads[n]) for n in SMALL],
                                [rank2(as_stored(n, args["m_" + n])) for n in SMALL],
                                [rank2(as_stored(n, args["v_" + n])) for n in SMALL], "adamw_small")
    for k, n in enumerate(SMALL):
        delta[n], new_m[n], new_v[n], grads[n] = (as_stored(n, a.reshape(stored[n].shape))
                                                  for a in (d_s[k], m_s[k], v_s[k], grads[n]))

    return (loss, dx[None], *[grads[n] for n in WEIGHTS], *[delta[n] for n in WEIGHTS],
            *[new_m[n] for n in WEIGHTS], *[new_v[n] for n in WEIGHTS])


def kernel(x, positions, attn_norm_w, w_in, ssm_lambda_re, ssm_lambda_im, ssm_log_dt, ssm_b_re, ssm_b_im, ssm_c_re, ssm_c_im, ssm_d, ssm_w_glu, ssm_b_glu, mla_q_norm_w, mla_w_uq, mla_kv_norm_w, mla_w_ukv, ssm_out_norm_w, mla_out_norm_w, w_out, ffn_norm_w, ffn_w_up, ffn_conv_w, ffn_conv_b, ffn_w_down, final_norm_w, loss_target, m_attn_norm_w, m_w_in, m_ssm_lambda_re, m_ssm_lambda_im, m_ssm_log_dt, m_ssm_b_re, m_ssm_b_im, m_ssm_c_re, m_ssm_c_im, m_ssm_d, m_ssm_w_glu, m_ssm_b_glu, m_mla_q_norm_w, m_mla_w_uq, m_mla_kv_norm_w, m_mla_w_ukv, m_ssm_out_norm_w, m_mla_out_norm_w, m_w_out, m_ffn_norm_w, m_ffn_w_up, m_ffn_conv_w, m_ffn_conv_b, m_ffn_w_down, m_final_norm_w, v_attn_norm_w, v_w_in, v_ssm_lambda_re, v_ssm_lambda_im, v_ssm_log_dt, v_ssm_b_re, v_ssm_b_im, v_ssm_c_re, v_ssm_c_im, v_ssm_d, v_ssm_w_glu, v_ssm_b_glu, v_mla_q_norm_w, v_mla_w_uq, v_mla_kv_norm_w, v_mla_w_ukv, v_ssm_out_norm_w, v_mla_out_norm_w, v_w_out, v_ffn_norm_w, v_ffn_w_up, v_ffn_conv_w, v_ffn_conv_b, v_ffn_w_down, v_final_norm_w):
    return _step(dict(locals()))
```

```python
import functools
import math

import jax
import jax.numpy as jnp
from jax import lax
from jax.experimental import pallas as pl
from jax.experimental.pallas import tpu as pltpu

F32 = jnp.float32
BF16 = jnp.bfloat16

SSM_GROUP = 16
SSM_STATE = 64
QK_NOPE_DIM = 128
QK_ROPE_DIM = 64
ROPE_THETA = 10000.0
RMS_EPS = 1e-6
ADAM_LR, ADAM_B1, ADAM_B2, ADAM_EPS, ADAM_WD, ADAM_STEP = 0.001, 0.9, 0.999, 1e-08, 0.01, 10

LANES = 128
SUBLANES = 8
VMEM_LIMIT_BYTES = 56 * 1024 * 1024

GROUPS_PER_BATCH = LANES // SSM_GROUP
STATE_PER_BATCH = GROUPS_PER_BATCH * SSM_STATE
HEAD_SLOT = 2 * LANES
NEG_INF = -1e30
ATTN_BLOCK = 1024
FFN_ROWS = 1024

N_CHIPS = 4
N_CORES = 2


def _tile(n, pref, align=LANES):
    if n <= pref:
        return n
    t = (pref // align) * align
    while t >= align:
        if n % t == 0:
            return t
        t -= align
    return n


def _params(sem):
    return pltpu.CompilerParams(dimension_semantics=sem, vmem_limit_bytes=VMEM_LIMIT_BYTES)


def _dot(a, b, dims):
    return lax.dot_general(a, b, (dims, ((), ())), preferred_element_type=F32)


def _dot_nn(a, b):
    return _dot(a, b, ((1,), (0,)))


def _dot_nt(a, b):
    return _dot(a, b, ((1,), (1,)))


def _dot_tn(a, b):
    return _dot(a, b, ((0,), (0,)))


def _matmul(a, b, *, mode, name, tm=1024, tn=1024, tk=2048, bias=None, add=None, out_dtype=F32,
            a_split=False, b_split=False, after=None):
    if a_split:
        assert mode == "nt"
        a_shape = (a.shape[1], 2 * a.shape[2])
    else:
        a_shape = a.shape
    if b_split:
        assert mode == "tn"
        b_shape = (b.shape[1], 2 * b.shape[2])
    else:
        b_shape = b.shape
    if mode == "nn":
        (m, k), (k2, n) = a_shape, b_shape
    elif mode == "nt":
        (m, k), (n, k2) = a_shape, b_shape
    else:
        (k, m), (k2, n) = a_shape, b_shape
    assert k == k2, (a.shape, b.shape, mode)
    tm, tn, tk = _tile(m, tm, SUBLANES), _tile(n, tn), _tile(k, tk)
    nk = k // tk
    a_spec = {"nn": pl.BlockSpec((tm, tk), lambda i, j, kk: (i, kk)),
              "nt": pl.BlockSpec((tm, tk), lambda i, j, kk: (i, kk)),
              "tn": pl.BlockSpec((tk, tm), lambda i, j, kk: (kk, i))}[mode]
    b_spec = {"nn": pl.BlockSpec((tk, tn), lambda i, j, kk: (kk, j)),
              "nt": pl.BlockSpec((tn, tk), lambda i, j, kk: (j, kk)),
              "tn": pl.BlockSpec((tk, tn), lambda i, j, kk: (kk, j))}[mode]
    if a_split:
        kb = a.shape[2] // tk
        assert a.shape[2] % tk == 0
        a_spec = pl.BlockSpec((None, tm, tk), lambda i, j, kk: (kk // kb, i, kk % kb))
    if b_split:
        nb = b.shape[2] // tn
        assert b.shape[2] % tn == 0
        b_spec = pl.BlockSpec((None, tk, tn), lambda i, j, kk: (j // nb, kk, j % nb))
    dot = {"nn": _dot_nn, "nt": _dot_nt, "tn": _dot_tn}[mode]
    in_specs, operands = [a_spec, b_spec], [a, b]
    if bias is not None:
        in_specs.append(pl.BlockSpec((1, tn), lambda i, j, kk: (0, j)))
        operands.append(bias)
    if add is not None:
        in_specs.append(pl.BlockSpec((tm, tn), lambda i, j, kk: (i, j)))
        operands.append(add)
    if after is not None:
        in_specs.append(pl.BlockSpec(memory_space=pl.ANY))
        operands.append(after)

    def body(*refs):
        a_ref, b_ref = refs[0], refs[1]
        rest = list(refs[2:])
        bias_ref = rest.pop(0) if bias is not None else None
        add_ref = rest.pop(0) if add is not None else None
        if after is not None:
            rest.pop(0)
        o_ref, acc_ref = rest

        def finish(acc):
            if bias_ref is not None:
                acc = acc + bias_ref[...]
            if add_ref is not None:
                acc = acc + add_ref[...]
            o_ref[...] = acc.astype(o_ref.dtype)

        part = dot(a_ref[...].astype(BF16), b_ref[...].astype(BF16))
        if nk == 1:
            finish(part)
        else:
            kk = pl.program_id(2)

            @pl.when(kk == 0)
            def _():
                acc_ref[...] = part

            @pl.when(jnp.logical_and(kk > 0, kk < nk - 1))
            def _():
                acc_ref[...] += part

            @pl.when(kk == nk - 1)
            def _():
                finish(acc_ref[...] + part)

    out_shape = jax.ShapeDtypeStruct((m, n), out_dtype)
    out_spec = pl.BlockSpec((tm, tn), lambda i, j, kk: (i, j))
    acc_shape = (tm, tn) if nk > 1 else (SUBLANES, LANES)
    return pl.pallas_call(
        body, name=name, grid=(m // tm, n // tn, nk), in_specs=in_specs, out_specs=out_spec, out_shape=out_shape,
        scratch_shapes=[pltpu.VMEM(acc_shape, F32)],
        compiler_params=_params(("parallel", "parallel", "arbitrary")),
    )(*operands)


def _wgrad_half(a, b, rows, cols, row_sharded, name, which, *, suffix="", add=None, wire=False, tm=None, tn=None,
                b_split=False, after=None):
    tokens = a.shape[0]
    if row_sharded:
        sr, sc = rows // N_CHIPS, cols // N_CORES
    else:
        sr, sc = rows // N_CORES, cols // N_CHIPS
    tm = _tile(sr, 1024) if tm is None else tm
    tn = _tile(sc, 1024) if tn is None else tn
    assert sr % tm == 0 and sc % tn == 0, (rows, cols, tm, tn)
    rb, cb = sr // tm, sc // tn
    if tn >= tm:
        ij, grid = (lambda s, t: (t, s)), (N_CHIPS, cb, rb)
    else:
        ij, grid = (lambda s, t: (s, t)), (N_CHIPS, rb, cb)
    if row_sharded:
        a_tile = lambda p, i, j, h: p * rb + i
        b_tile = lambda p, i, j, h: h[0] * cb + j
    else:
        a_tile = lambda p, i, j, h: h[0] * rb + i
        b_tile = lambda p, i, j, h: p * cb + j
    a_spec = pl.BlockSpec((tokens, tm), lambda p, s, t, h: (0, a_tile(p, *ij(s, t), h)))
    if b_split:
        nbh = b.shape[2] // tn
        assert b.shape[2] % tn == 0
        b_spec = pl.BlockSpec((None, tokens, tn), lambda p, s, t, h: (b_tile(p, *ij(s, t), h) // nbh, 0,
                                                                       b_tile(p, *ij(s, t), h) % nbh))
    else:
        b_spec = pl.BlockSpec((tokens, tn), lambda p, s, t, h: (0, b_tile(p, *ij(s, t), h)))
    out_spec = pl.BlockSpec((None, tm, tn), lambda p, s, t, h: (p, *ij(s, t)))
    in_specs, operands = [a_spec, b_spec], [a, b]
    if add is not None:
        in_specs.append(out_spec)
        operands.append(add)
    if after is not None:
        in_specs.append(pl.BlockSpec(memory_space=pl.ANY))
        operands.append(after)
    out_dtypes = [F32, BF16] if wire else [F32]

    def body(h_ref, a_ref, b_ref, *rest):
        acc = _dot_tn(a_ref[...].astype(BF16), b_ref[...].astype(BF16))
        if add is not None:
            acc = acc + rest[0][...]
        for o_ref in rest[-len(out_dtypes):]:
            o_ref[...] = acc.astype(o_ref.dtype)

    out = pl.pallas_call(
        body, name=name + suffix, out_shape=[jax.ShapeDtypeStruct((N_CHIPS, sr, sc), dt) for dt in out_dtypes],
        grid_spec=pltpu.PrefetchScalarGridSpec(num_scalar_prefetch=1, grid=grid, in_specs=in_specs,
                                               out_specs=[out_spec] * len(out_dtypes)),
        compiler_params=_params(("parallel", "parallel", "parallel")),
    )(which, *operands)
    return tuple(out) if wire else out[0]


def _rms_rows(x):
    return lax.rsqrt(jnp.mean(x * x, axis=-1, keepdims=True) + RMS_EPS)


def _rmsnorm_fwd(x, w, *, name, width=None, col=0, out_dtype=BF16, tr=512):
    rows = x.shape[0]
    width = x.shape[1] if width is None else width
    tr = _tile(rows, tr, SUBLANES)

    def body(x_ref, w_ref, o_ref):
        xv = x_ref[...]
        o_ref[...] = (xv * _rms_rows(xv) * w_ref[...]).astype(o_ref.dtype)

    return pl.pallas_call(
        body, name=name, grid=(rows // tr,),
        in_specs=[pl.BlockSpec((tr, width), lambda i: (i, col)), pl.BlockSpec((1, width), lambda i: (0, 0))],
        out_specs=pl.BlockSpec((tr, width), lambda i: (i, 0)),
        out_shape=jax.ShapeDtypeStruct((rows, width), out_dtype),
        compiler_params=_params(("parallel",)),
    )(x, w)


def _rmsnorm_bwd_rows(xv, w, dy):
    r = _rms_rows(xv)
    n = xv * r
    dn = dy * w
    dx = r * (dn - n * jnp.mean(dn * n, axis=-1, keepdims=True))
    return dx, dy * n


def _rmsnorm_bwd(x, w, dy, *, name, width=None, col=0, dy_col=0, add=None, tr=512, dx_dtypes=(F32,)):
    rows = x.shape[0]
    n_dx = len(dx_dtypes)
    width = x.shape[1] if width is None else width
    tr = _tile(rows, tr, SUBLANES)
    in_specs = [pl.BlockSpec((tr, width), lambda i: (i, col)), pl.BlockSpec((1, width), lambda i: (0, 0)),
                pl.BlockSpec((tr, width), lambda i: (i, dy_col))]
    operands = [x, w, dy]
    if add is not None:
        in_specs.append(pl.BlockSpec((tr, width), lambda i: (i, 0)))
        operands.append(add)

    def body(*refs):
        x_ref, w_ref, dy_ref = refs[:3]
        add_ref = refs[3] if add is not None else None
        dx_refs, dw_ref = refs[-1 - n_dx:-1], refs[-1]
        dx, dwp = _rmsnorm_bwd_rows(x_ref[...], w_ref[...], dy_ref[...])
        if add_ref is not None:
            dx = dx + add_ref[...]
        for dx_ref in dx_refs:
            dx_ref[...] = dx.astype(dx_ref.dtype)
        part = jnp.sum(dwp, axis=0, keepdims=True)

        @pl.when(pl.program_id(0) == 0)
        def _():
            dw_ref[...] = part

        @pl.when(pl.program_id(0) > 0)
        def _():
            dw_ref[...] += part

    return pl.pallas_call(
        body, name=name, grid=(rows // tr,), in_specs=in_specs,
        out_specs=[pl.BlockSpec((tr, width), lambda i: (i, 0))] * n_dx + [pl.BlockSpec((1, width), lambda i: (0, 0))],
        out_shape=[jax.ShapeDtypeStruct((rows, width), dt) for dt in dx_dtypes] + [jax.ShapeDtypeStruct((1, width), F32)],
        compiler_params=_params(("arbitrary",)),
    )(*operands)


def _final_norm_loss(h, w, target, *, tr=512):
    rows, d = h.shape
    tr = _tile(rows, tr, SUBLANES)

    def body(h_ref, w_ref, t_ref, loss_ref, dh_ref, dhb_ref, dw_ref):
        hv, wv = h_ref[...], w_ref[...]
        r = _rms_rows(hv)
        n = hv * r
        err = n * wv - t_ref[...]
        d_out = err * (1.0 / d)
        dn = d_out * wv
        dh = r * (dn - n * jnp.mean(dn * n, axis=-1, keepdims=True))
        dh_ref[...] = dh
        dhb_ref[...] = dh.astype(BF16)
        dw_part = jnp.sum(d_out * n, axis=0, keepdims=True)
        loss_part = jnp.full((SUBLANES, LANES), 0.5 / d, F32) * jnp.sum(err * err)

        @pl.when(pl.program_id(0) == 0)
        def _():
            dw_ref[...] = dw_part
            loss_ref[...] = loss_part

        @pl.when(pl.program_id(0) > 0)
        def _():
            dw_ref[...] += dw_part
            loss_ref[...] += loss_part

    return pl.pallas_call(
        body, name="final_norm_loss", grid=(rows // tr,),
        in_specs=[pl.BlockSpec((tr, d), lambda i: (i, 0)), pl.BlockSpec((1, d), lambda i: (0, 0)),
                  pl.BlockSpec((tr, d), lambda i: (i, 0))],
        out_specs=[pl.BlockSpec((SUBLANES, LANES), lambda i: (0, 0)), pl.BlockSpec((tr, d), lambda i: (i, 0)),
                   pl.BlockSpec((tr, d), lambda i: (i, 0)), pl.BlockSpec((1, d), lambda i: (0, 0))],
        out_shape=[jax.ShapeDtypeStruct((SUBLANES, LANES), F32), jax.ShapeDtypeStruct((rows, d), F32),
                   jax.ShapeDtypeStruct((rows, d), BF16), jax.ShapeDtypeStruct((1, d), F32)],
        compiler_params=_params(("arbitrary",)),
    )(h, w, target)


def _cmul(ar, ai, br, bi):
    return ar * br - ai * bi, ar * bi + ai * br


def _dot_exact(a, b, dims):
    return lax.dot_general(a, b, (dims, ((), ())), preferred_element_type=F32, precision=lax.Precision.HIGHEST)


def _s5_discretize(lr, li, dt):
    mag = jnp.exp(lr * dt)
    th = li * dt
    ar, ai = mag * jnp.cos(th), mag * jnp.sin(th)
    nr, ni = ar - 1.0, ai
    den = lr * lr + li * li
    zr = (nr * lr + ni * li) / den
    zi = (ni * lr - nr * li) / den
    return mag, ar, ai, nr, ni, den, zr, zi


def _band_slices(group):
    j, gi = divmod(group, GROUPS_PER_BATCH)
    rows = slice(gi * SSM_GROUP, (gi + 1) * SSM_GROUP)
    re = slice(gi * SSM_STATE, (gi + 1) * SSM_STATE)
    im = slice(STATE_PER_BATCH + gi * SSM_STATE, STATE_PER_BATCH + (gi + 1) * SSM_STATE)
    return j, rows, re, im


def _s5_bands(lam_re, lam_im, log_dt, b_re, b_im, c_re, c_im):
    g, _ = lam_re.shape
    nb = g // GROUPS_PER_BATCH
    s2 = 2 * STATE_PER_BATCH

    def body(lr_ref, li_ref, ldt_ref, br_ref, bi_ref, cr_ref, ci_ref, wb_ref, wct_ref, a_ref):
        dt = jnp.exp(ldt_ref[...])
        _, ar, ai, _, _, _, zr, zi = _s5_discretize(lr_ref[...], li_ref[...], dt)
        wb_ref[...] = jnp.zeros_like(wb_ref)
        wct_ref[...] = jnp.zeros_like(wct_ref)
        for group in range(g):
            j, rows, re, im = _band_slices(group)
            zr_g, zi_g = zr[group:group + 1, :], zi[group:group + 1, :]
            bre, bim = br_ref[group], bi_ref[group]
            wb_ref[j, rows, re] = (zr_g * bre - zi_g * bim).astype(BF16)
            wb_ref[j, rows, im] = (zr_g * bim + zi_g * bre).astype(BF16)
            wct_ref[j, rows, re] = cr_ref[group].astype(BF16)
            wct_ref[j, rows, im] = (-ci_ref[group]).astype(BF16)
            a_ref[j, :, re] = ar[group:group + 1, :]
            a_ref[j, :, im] = ai[group:group + 1, :]

    return pl.pallas_call(
        body, name="s5_bands",
        out_shape=[jax.ShapeDtypeStruct((nb, LANES, s2), BF16)] * 2 + [jax.ShapeDtypeStruct((nb, 1, s2), F32)],
    )(lam_re, lam_im, log_dt, b_re, b_im, c_re, c_im)


def _s5_bands_bwd(lam_re, lam_im, log_dt, b_re, b_im, dwb, dwct, dabar):
    g, p = lam_re.shape
    gh = b_re.shape[1:]

    def body(lr_ref, li_ref, ldt_ref, br_ref, bi_ref, dwb_ref, dwct_ref, da_ref,
             dlr_ref, dli_ref, dldt_ref, dbre_ref, dbim_ref, dcre_ref, dcim_ref, dzr_ref, dzi_ref, dar_ref, dai_ref):
        lr, li = lr_ref[...], li_ref[...]
        dt = jnp.exp(ldt_ref[...])
        mag, ar, ai, nr, ni, den, zr, zi = _s5_discretize(lr, li, dt)
        for group in range(g):
            j, rows, re, im = _band_slices(group)
            zr_g, zi_g = zr[group:group + 1, :], zi[group:group + 1, :]
            bre, bim = br_ref[group], bi_ref[group]
            dbr, dbi = dwb_ref[j, rows, re], dwb_ref[j, rows, im]
            dbre_ref[group] = zr_g * dbr + zi_g * dbi
            dbim_ref[group] = zr_g * dbi - zi_g * dbr
            dzr_ref[group:group + 1, :] = jnp.sum(bre * dbr + bim * dbi, axis=0, keepdims=True)
            dzi_ref[group:group + 1, :] = jnp.sum(bre * dbi - bim * dbr, axis=0, keepdims=True)
            dcre_ref[group] = dwct_ref[j, rows, re]
            dcim_ref[group] = -dwct_ref[j, rows, im]
            dar_ref[group:group + 1, :] = da_ref[j, :, re]
            dai_ref[group:group + 1, :] = da_ref[j, :, im]
        dzr, dzi = dzr_ref[...], dzi_ref[...]
        inv = 1.0 / den
        d_nr = (dzr * lr - dzi * li) * inv
        d_ni = (dzr * li + dzi * lr) * inv
        d_den = -(dzr * zr + dzi * zi) * inv
        d_lr = (dzr * nr + dzi * ni) * inv + 2.0 * lr * d_den
        d_li = (dzr * ni - dzi * nr) * inv + 2.0 * li * d_den
        t_ar = dar_ref[...] + d_nr
        t_ai = dai_ref[...] + d_ni
        d_lrdt = t_ar * ar + t_ai * ai
        d_th = t_ai * ar - t_ar * ai
        dlr_ref[...] = d_lr + d_lrdt * dt
        dli_ref[...] = d_li + d_th * dt
        dldt_ref[...] = jnp.sum(d_lrdt * lr + d_th * li, axis=1, keepdims=True) * dt

    return pl.pallas_call(
        body, name="s5_bands_bwd",
        out_shape=[jax.ShapeDtypeStruct((g, p), F32)] * 2 + [jax.ShapeDtypeStruct((g, 1), F32)]
        + [jax.ShapeDtypeStruct((g,) + gh, F32)] * 4,
        scratch_shapes=[pltpu.VMEM((g, p), F32)] * 4,
    )(lam_re, lam_im, log_dt, b_re, b_im, dwb, dwct, dabar)


def _powers(ar, ai, count):
    out = [(ar, ai)]
    for _ in range(count - 1):
        out.append(_cmul(out[-1][0], out[-1][1], ar, ai))
    return out


def _scan_coefs(ar, ai, reverse):
    w = ar.shape[-1]
    pw = _powers(ar, ai, SUBLANES)
    row = lax.broadcasted_iota(jnp.int32, (SUBLANES, w), 0)
    steps = []
    d = 1
    while d < SUBLANES:
        keep = (row < SUBLANES - d) if reverse else (row >= d)
        pr, pi = pw[d - 1]
        steps.append((d, jnp.where(keep, pr, 0.0), jnp.where(keep, pi, 0.0)))
        d *= 2
    cr = jnp.zeros((SUBLANES, w), F32)
    ci = jnp.zeros((SUBLANES, w), F32)
    for t in range(SUBLANES):
        pr, pi = pw[SUBLANES - 1 - t] if reverse else pw[t]
        cr = jnp.where(row == t, pr, cr)
        ci = jnp.where(row == t, pi, ci)
    return steps, cr, ci


def _scan_tile(xr, xi, carry_r, carry_i, coefs, reverse):
    steps, cr, ci = coefs
    for d, mr, mi in steps:
        shift = SUBLANES - d if reverse else d
        sr, si = pltpu.roll(xr, shift, 0), pltpu.roll(xi, shift, 0)
        pr, pi = _cmul(mr, mi, sr, si)
        xr, xi = xr + pr, xi + pi
    pr, pi = _cmul(cr, ci, carry_r, carry_i)
    return xr + pr, xi + pi


def _gelu(x):
    c = math.sqrt(2.0 / math.pi)
    return 0.5 * x * (1.0 + jnp.tanh(c * (x + 0.044715 * x * x * x)))


def _gelu_grad(x):
    c = math.sqrt(2.0 / math.pi)
    t = jnp.tanh(c * (x + 0.044715 * x * x * x))
    return 0.5 * (1.0 + t) + 0.5 * x * (1.0 - t * t) * c * (1.0 + 3.0 * 0.044715 * x * x)


def _s5_fwd(proj, wb, wct, d_skip, abar):
    rows = proj.shape[0]
    nb = wb.shape[0]
    s2 = 2 * STATE_PER_BATCH
    st = STATE_PER_BATCH
    chunk = _tile(rows, 1024, SUBLANES)

    def body(u_ref, wb_ref, wc_ref, d_ref, a_ref, s_ref, y_ref, yg_ref):
        for c0 in range(0, rows, chunk):
            s_ref[pl.ds(c0, chunk), :] = _dot_nn(u_ref[pl.ds(c0, chunk), :].astype(BF16), wb_ref[...])
        av = a_ref[...]
        coefs = _scan_coefs(av[:, :st], av[:, st:], reverse=False)

        def tile(b, carry):
            r0 = pl.multiple_of(b * SUBLANES, SUBLANES)
            xr, xi = _scan_tile(s_ref[pl.ds(r0, SUBLANES), :st], s_ref[pl.ds(r0, SUBLANES), st:], carry[0], carry[1],
                                coefs, False)
            s_ref[pl.ds(r0, SUBLANES), :st] = xr
            s_ref[pl.ds(r0, SUBLANES), st:] = xi
            return xr[SUBLANES - 1:, :], xi[SUBLANES - 1:, :]

        zero = jnp.zeros((1, st), F32)
        lax.fori_loop(0, rows // SUBLANES, tile, (zero, zero))
        for c0 in range(0, rows, chunk):
            y = _dot_nt(s_ref[pl.ds(c0, chunk), :].astype(BF16), wc_ref[...]) + d_ref[...] * u_ref[pl.ds(c0, chunk), :]
            y_ref[pl.ds(c0, chunk), :] = y
            yg_ref[pl.ds(c0, chunk), :] = _gelu(y).astype(BF16)

    return pl.pallas_call(
        body, name="s5_fwd", grid=(nb,),
        in_specs=[pl.BlockSpec((rows, LANES), lambda j: (0, j)), pl.BlockSpec((None, LANES, s2), lambda j: (j, 0, 0)),
                  pl.BlockSpec((None, LANES, s2), lambda j: (j, 0, 0)), pl.BlockSpec((1, LANES), lambda j: (0, j)),
                  pl.BlockSpec((None, 1, s2), lambda j: (j, 0, 0))],
        out_specs=[pl.BlockSpec((rows, s2), lambda j: (0, j)), pl.BlockSpec((rows, LANES), lambda j: (0, j)),
                   pl.BlockSpec((rows, LANES), lambda j: (0, j))],
        out_shape=[jax.ShapeDtypeStruct((rows, nb * s2), F32), jax.ShapeDtypeStruct((rows, nb * LANES), F32),
                   jax.ShapeDtypeStruct((rows, nb * LANES), BF16)],
        compiler_params=_params(("parallel",)),
    )(proj, wb, wct, d_skip, abar)


def _s5_bwd(proj, states, y_pre, dyg_a, dyg_b, wb, wct, d_skip, abar):
    rows = proj.shape[0]
    nb = wb.shape[0]
    s2 = 2 * STATE_PER_BATCH
    st = STATE_PER_BATCH
    chunk = _tile(rows, 1024, SUBLANES)
    n_tiles = rows // SUBLANES

    def body(u_ref, s_ref, y_ref, ga_ref, gb_ref, wb_ref, wc_ref, d_ref, a_ref,
             du_ref, dwb_ref, dwc_ref, da_ref, dd_ref, ds_ref, dy_ref):
        dy_ref[...] = (ga_ref[...] + gb_ref[...]) * _gelu_grad(y_ref[...])
        dd_ref[...] = jnp.sum(dy_ref[...] * u_ref[...], axis=0, keepdims=True)
        for c0 in range(0, rows, chunk):
            ds_ref[pl.ds(c0, chunk), :] = _dot_nn(dy_ref[pl.ds(c0, chunk), :].astype(BF16), wc_ref[...])
        dwc_ref[...] = _dot_tn(dy_ref[...].astype(BF16), s_ref[...].astype(BF16))
        av = a_ref[...]
        coefs = _scan_coefs(av[:, :st], -av[:, st:], reverse=True)
        row = lax.broadcasted_iota(jnp.int32, (SUBLANES, st), 0)

        def tile(k, carry):
            cr, ci, acc_r, acc_i = carry
            b = n_tiles - 1 - k
            r0 = pl.multiple_of(b * SUBLANES, SUBLANES)
            rp = pl.multiple_of(jnp.maximum(b - 1, 0) * SUBLANES, SUBLANES)
            xr, xi = _scan_tile(ds_ref[pl.ds(r0, SUBLANES), :st], ds_ref[pl.ds(r0, SUBLANES), st:], cr, ci, coefs, True)
            ds_ref[pl.ds(r0, SUBLANES), :st] = xr
            ds_ref[pl.ds(r0, SUBLANES), st:] = xi
            first = jnp.where(b > 0, 1.0, 0.0)
            pr = jnp.where(row == 0, pltpu.roll(s_ref[pl.ds(rp, SUBLANES), :st], 1, 0) * first,
                           pltpu.roll(s_ref[pl.ds(r0, SUBLANES), :st], 1, 0))
            pi = jnp.where(row == 0, pltpu.roll(s_ref[pl.ds(rp, SUBLANES), st:], 1, 0) * first,
                           pltpu.roll(s_ref[pl.ds(r0, SUBLANES), st:], 1, 0))
            acc_r = acc_r + pr * xr + pi * xi
            acc_i = acc_i + pr * xi - pi * xr
            return xr[:1, :], xi[:1, :], acc_r, acc_i

        zero = jnp.zeros((1, st), F32)
        zacc = jnp.zeros((SUBLANES, st), F32)
        _, _, acc_r, acc_i = lax.fori_loop(0, n_tiles, tile, (zero, zero, zacc, zacc))
        da_ref[:, :st] = jnp.sum(acc_r, axis=0, keepdims=True)
        da_ref[:, st:] = jnp.sum(acc_i, axis=0, keepdims=True)
        for c0 in range(0, rows, chunk):
            du_ref[pl.ds(c0, chunk), :] = (_dot_nt(ds_ref[pl.ds(c0, chunk), :].astype(BF16), wb_ref[...])
                                           + d_ref[...] * dy_ref[pl.ds(c0, chunk), :]).astype(du_ref.dtype)
        dwb_ref[...] = _dot_tn(u_ref[...].astype(BF16), ds_ref[...].astype(BF16))

    col = pl.BlockSpec((rows, LANES), lambda j: (0, j))
    return pl.pallas_call(
        body, name="s5_bwd", grid=(nb,),
        in_specs=[col, pl.BlockSpec((rows, s2), lambda j: (0, j)), col, col, col,
                  pl.BlockSpec((None, LANES, s2), lambda j: (j, 0, 0)), pl.BlockSpec((None, LANES, s2), lambda j: (j, 0, 0)),
                  pl.BlockSpec((1, LANES), lambda j: (0, j)), pl.BlockSpec((None, 1, s2), lambda j: (j, 0, 0))],
        out_specs=[col, pl.BlockSpec((None, LANES, s2), lambda j: (j, 0, 0)),
                   pl.BlockSpec((None, LANES, s2), lambda j: (j, 0, 0)), pl.BlockSpec((None, 1, s2), lambda j: (j, 0, 0)),
                   pl.BlockSpec((1, LANES), lambda j: (0, j))],
        out_shape=[jax.ShapeDtypeStruct((rows, nb * LANES), BF16), jax.ShapeDtypeStruct((nb, LANES, s2), F32),
                   jax.ShapeDtypeStruct((nb, LANES, s2), F32), jax.ShapeDtypeStruct((nb, 1, s2), F32),
                   jax.ShapeDtypeStruct((1, nb * LANES), F32)],
        scratch_shapes=[pltpu.VMEM((rows, s2), F32), pltpu.VMEM((rows, LANES), F32)],
        compiler_params=_params(("parallel",)),
    )(proj, states, y_pre, dyg_a, dyg_b, wb, wct, d_skip, abar)


def _glu_norm_fwd(y_pre, z, w, *, tr=512):
    rows, width = y_pre.shape
    tr = _tile(rows, tr, SUBLANES)

    def body(y_ref, z_ref, w_ref, o_ref):
        v = _gelu(y_ref[...]) * jax.nn.sigmoid(z_ref[...])
        o_ref[...] = (v * _rms_rows(v) * w_ref[...]).astype(o_ref.dtype)

    blk = pl.BlockSpec((tr, width), lambda i: (i, 0))
    return pl.pallas_call(
        body, name="glu_norm_fwd", grid=(rows // tr,),
        in_specs=[blk, blk, pl.BlockSpec((1, width), lambda i: (0, 0))], out_specs=blk,
        out_shape=jax.ShapeDtypeStruct((rows, width), BF16), compiler_params=_params(("parallel",)),
    )(y_pre, z, w)


def _glu_norm_bwd(y_pre, z, w, dycat, *, tr=512):
    rows, width = y_pre.shape
    tr = _tile(rows, tr, SUBLANES)

    def body(y_ref, z_ref, w_ref, dy_ref, dz_ref, dg_ref, dw_ref, db_ref):
        yg = _gelu(y_ref[...])
        sg = jax.nn.sigmoid(z_ref[...])
        dv, dwp = _rmsnorm_bwd_rows(yg * sg, w_ref[...], dy_ref[...])
        dz = dv * yg * sg * (1.0 - sg)
        dz_ref[...] = dz.astype(dz_ref.dtype)
        dg_ref[...] = dv * sg
        dw_part = jnp.sum(dwp, axis=0, keepdims=True)
        db_part = jnp.sum(dz, axis=0, keepdims=True)

        @pl.when(pl.program_id(0) == 0)
        def _():
            dw_ref[...] = dw_part
            db_ref[...] = db_part

        @pl.when(pl.program_id(0) > 0)
        def _():
            dw_ref[...] += dw_part
            db_ref[...] += db_part

    blk = pl.BlockSpec((tr, width), lambda i: (i, 0))
    vec = pl.BlockSpec((1, width), lambda i: (0, 0))
    return pl.pallas_call(
        body, name="glu_norm_bwd", grid=(rows // tr,), in_specs=[blk, blk, vec, blk], out_specs=[blk, blk, vec, vec],
        out_shape=[jax.ShapeDtypeStruct((rows, width), BF16), jax.ShapeDtypeStruct((rows, width), F32)]
        + [jax.ShapeDtypeStruct((1, width), F32)] * 2,
        compiler_params=_params(("arbitrary",)),
    )(y_pre, z, w, dycat)


def _rope_tables(pos, freq, sign):
    rows = pos.shape[0]

    def body(p_ref, f_ref, s_ref, cos_ref, sin_ref):
        ang = p_ref[...] * f_ref[...]
        cos_ref[...] = jnp.cos(ang)
        sin_ref[...] = jnp.sin(ang) * s_ref[...]

    return pl.pallas_call(body, name="rope_tables", out_shape=[jax.ShapeDtypeStruct((rows, LANES), F32)] * 2)(pos, freq, sign)


def _rope(x, cos, sin_signed):
    half = QK_ROPE_DIM // 2
    src = lax.broadcasted_iota(jnp.int32, (LANES, LANES), 0)
    dst = lax.broadcasted_iota(jnp.int32, (LANES, LANES), 1)
    swap = jnp.where(jnp.logical_or(jnp.logical_and(dst < half, src == dst + half),
                                    jnp.logical_and(jnp.logical_and(dst >= half, dst < 2 * half), src == dst - half)),
                     1.0, 0.0).astype(F32)
    swapped = _dot_exact(x, swap, ((1,), (0,)))
    return x * cos + swapped * sin_signed


def _attn_prep(q, kv, proj, kpe_col, cos, sin, *, tr=512):
    rows = q.shape[0]
    heads = q.shape[1] // HEAD_SLOT
    tr = _tile(rows, tr, SUBLANES)

    def body(q_ref, kv_ref, kpe_ref, cos_ref, sin_ref, qc_ref, kc_ref, v_ref):
        c, s = cos_ref[...], sin_ref[...]
        kpe = _rope(kpe_ref[...], c, s).astype(BF16)
        for h in range(heads):
            nope = slice(h * HEAD_SLOT, h * HEAD_SLOT + LANES)
            pe = slice(h * HEAD_SLOT + LANES, (h + 1) * HEAD_SLOT)
            qc_ref[:, nope] = q_ref[:, nope].astype(BF16)
            qc_ref[:, pe] = _rope(q_ref[:, pe], c, s).astype(BF16)
            kc_ref[:, nope] = kv_ref[:, nope].astype(BF16)
            kc_ref[:, pe] = kpe
            v_ref[:, h * LANES:(h + 1) * LANES] = kv_ref[:, pe].astype(BF16)

    slots = pl.BlockSpec((tr, heads * HEAD_SLOT), lambda i: (i, 0))
    tab = pl.BlockSpec((tr, LANES), lambda i: (i, 0))
    return pl.pallas_call(
        body, name="attn_prep", grid=(rows // tr,),
        in_specs=[slots, slots, pl.BlockSpec((tr, LANES), lambda i: (i, kpe_col)), tab, tab],
        out_specs=[slots, slots, pl.BlockSpec((tr, heads * LANES), lambda i: (i, 0))],
        out_shape=[jax.ShapeDtypeStruct((rows, heads * HEAD_SLOT), BF16)] * 2
        + [jax.ShapeDtypeStruct((rows, heads * LANES), BF16)],
        compiler_params=_params(("parallel",)),
    )(q, kv, proj, cos, sin)


def _causal(tq, tk):
    return lax.broadcasted_iota(jnp.int32, (tq, tk), 1) <= lax.broadcasted_iota(jnp.int32, (tq, tk), 0)


def _attn_fwd(qc, kc, vb, *, scale, tq=512):
    rows = qc.shape[0]
    heads = qc.shape[1] // HEAD_SLOT
    tq = _tile(rows, tq, SUBLANES)
    tk = tq

    def body(q_ref, k_ref, v_ref, o_ref, lse_ref):
        i = pl.program_id(1)
        q = q_ref[...]

        def step(j, carry, diagonal):
            m, l, acc = carry
            k0 = pl.multiple_of(j * tk, tk)
            s = _dot_nt(q, k_ref[pl.ds(k0, tk), :]) * scale
            if diagonal:
                s = jnp.where(_causal(tq, tk), s, NEG_INF)
            m_new = jnp.maximum(m, jnp.max(s, axis=-1, keepdims=True))
            p = jnp.exp(s - m_new)
            alpha = jnp.exp(m - m_new)
            l = alpha * l + jnp.sum(p, axis=-1, keepdims=True)
            acc = alpha * acc + _dot_nn(p.astype(BF16), v_ref[pl.ds(k0, tk), :])
            return m_new, l, acc

        init = (jnp.full((tq, 1), NEG_INF, F32), jnp.zeros((tq, 1), F32), jnp.zeros((tq, LANES), F32))
        below = lax.fori_loop(0, i, lambda j, carry: step(j, carry, False), init)
        m, l, acc = step(i, below, True)
        o_ref[...] = acc / l
        lse_ref[...] = jnp.broadcast_to(m + jnp.log(l), (tq, LANES))

    return pl.pallas_call(
        body, name="attn_fwd", grid=(heads, rows // tq),
        in_specs=[pl.BlockSpec((tq, HEAD_SLOT), lambda h, i: (i, h)), pl.BlockSpec((rows, HEAD_SLOT), lambda h, i: (0, h)),
                  pl.BlockSpec((rows, LANES), lambda h, i: (0, h))],
        out_specs=[pl.BlockSpec((tq, LANES), lambda h, i: (i, h))] * 2,
        out_shape=[jax.ShapeDtypeStruct((rows, heads * LANES), F32)] * 2,
        compiler_params=_params(("parallel", "parallel")),
    )(qc, kc, vb)


def _attn_bwd(qc, kc, vb, o, do, lse, cos, sin, *, scale, tk=512):
    rows = qc.shape[0]
    heads = qc.shape[1] // HEAD_SLOT
    tk = _tile(rows, tk, SUBLANES)
    tq = tk
    nq = rows // tq

    def body(q_ref, k_ref, v_ref, o_ref, do_ref, lse_ref, cos_ref, sin_ref, dq_ref, dkv_ref, dkpe_ref, dq_acc, delta_ref):
        j = pl.program_id(1)

        @pl.when(j == 0)
        def _():
            dq_acc[...] = jnp.zeros_like(dq_acc)
            for r0 in range(0, rows, tq):
                d = jnp.sum(do_ref[pl.ds(r0, tq), :] * o_ref[pl.ds(r0, tq), :], axis=-1, keepdims=True)
                delta_ref[pl.ds(r0, tq), :] = jnp.broadcast_to(d, (tq, LANES))

        kb, vv = k_ref[...], v_ref[...]

        def step(i, carry, diagonal):
            dk, dv = carry
            q0 = pl.multiple_of(i * tq, tq)
            qb = q_ref[pl.ds(q0, tq), :]
            dob = do_ref[pl.ds(q0, tq), :].astype(BF16)
            s = _dot_nt(qb, kb) * scale
            p = jnp.exp(s - lse_ref[pl.ds(q0, tq), :1])
            if diagonal:
                p = jnp.where(_causal(tq, tk), p, 0.0)
            dv = dv + _dot_tn(p.astype(BF16), dob)
            ds = (p * (_dot_nt(dob, vv) - delta_ref[pl.ds(q0, tq), :1])).astype(BF16)
            dk = dk + _dot_tn(ds, qb)
            dq_acc[pl.ds(q0, tq), :] += _dot_nn(ds, kb)
            return dk, dv

        zero = (jnp.zeros((tk, HEAD_SLOT), F32), jnp.zeros((tk, LANES), F32))
        dk, dv = lax.fori_loop(j + 1, nq, lambda i, carry: step(i, carry, False), step(j, zero, True))
        dkv_ref[:, :LANES] = (dk[:, :LANES] * scale).astype(dkv_ref.dtype)
        dkv_ref[:, LANES:] = dv.astype(dkv_ref.dtype)
        dkpe_ref[...] = dk[:, LANES:] * scale

        @pl.when(j == nq - 1)
        def _():
            for r0 in range(0, rows, tq):
                dq = dq_acc[pl.ds(r0, tq), :] * scale
                dq_ref[pl.ds(r0, tq), :LANES] = dq[:, :LANES].astype(dq_ref.dtype)
                dq_ref[pl.ds(r0, tq), LANES:] = _rope(dq[:, LANES:], cos_ref[pl.ds(r0, tq), :],
                                                      -sin_ref[pl.ds(r0, tq), :]).astype(dq_ref.dtype)

    full_q = pl.BlockSpec((rows, HEAD_SLOT), lambda h, j: (0, h))
    full_v = pl.BlockSpec((rows, LANES), lambda h, j: (0, h))
    tab = pl.BlockSpec((rows, LANES), lambda h, j: (0, 0))
    return pl.pallas_call(
        body, name="attn_bwd", grid=(heads, rows // tk),
        in_specs=[full_q, pl.BlockSpec((tk, HEAD_SLOT), lambda h, j: (j, h)), pl.BlockSpec((tk, LANES), lambda h, j: (j, h)),
                  full_v, full_v, full_v, tab, tab],
        out_specs=[full_q, pl.BlockSpec((tk, HEAD_SLOT), lambda h, j: (j, h)), pl.BlockSpec((tk, LANES), lambda h, j: (j, h))],
        out_shape=[jax.ShapeDtypeStruct((rows, heads * HEAD_SLOT), BF16), jax.ShapeDtypeStruct((rows, heads * HEAD_SLOT), BF16),
                   jax.ShapeDtypeStruct((rows, heads * LANES), F32)],
        scratch_shapes=[pltpu.VMEM((rows, HEAD_SLOT), F32), pltpu.VMEM((rows, LANES), F32)],
        compiler_params=_params(("parallel", "arbitrary")),
    )(qc, kc, vb, o, do, lse, cos, sin)


def _kpe_bwd(dkpe_heads, cos, sin, *, tr=512):
    rows = dkpe_heads.shape[0]
    heads = dkpe_heads.shape[1] // LANES
    tr = _tile(rows, tr, 2 * SUBLANES)

    def body(d_ref, cos_ref, sin_ref, o_ref):
        acc = d_ref[:, :LANES]
        for h in range(1, heads):
            acc = acc + d_ref[:, h * LANES:(h + 1) * LANES]
        o_ref[...] = _rope(acc, cos_ref[...], -sin_ref[...]).astype(o_ref.dtype)

    tab = pl.BlockSpec((tr, LANES), lambda i: (i, 0))
    return pl.pallas_call(
        body, name="kpe_bwd", grid=(rows // tr,),
        in_specs=[pl.BlockSpec((tr, heads * LANES), lambda i: (i, 0)), tab, tab], out_specs=tab,
        out_shape=jax.ShapeDtypeStruct((rows, LANES), BF16), compiler_params=_params(("parallel",)),
    )(dkpe_heads, cos, sin)


CONV_ROWS = 512


def _with_halo(ref, r0, ci, n_chunks, ch, lanes, before, after):
    parts = []
    if before:
        lo = pl.multiple_of(jnp.maximum(r0 - SUBLANES, 0), SUBLANES)
        parts.append(ref[pl.ds(lo, SUBLANES), lanes] * jnp.where(ci > 0, 1.0, 0.0))
    parts.append(ref[pl.ds(r0, ch), lanes])
    if after:
        hi = pl.multiple_of(jnp.minimum(r0 + ch, n_chunks * ch - SUBLANES), SUBLANES)
        parts.append(ref[pl.ds(hi, SUBLANES), lanes] * jnp.where(ci < n_chunks - 1, 1.0, 0.0))
    return jnp.concatenate(parts, axis=0)


def _taps(ext):
    return pltpu.roll(ext, 2, 0)[SUBLANES:], pltpu.roll(ext, 1, 0)[SUBLANES:], ext[SUBLANES:]


def _conv3(taps, w, b):
    return w[0:1, :] * taps[0] + w[1:2, :] * taps[1] + w[2:3, :] * taps[2] + b


def _conv_gate_fwd(a, conv_w, conv_b, *, tc=512):
    rows, f2 = a.shape
    f = f2 // 2
    tc = _tile(f, tc)
    nc = f // tc
    ch = _tile(rows, CONV_ROWS, SUBLANES)
    n_chunks = rows // ch

    def body(ag_ref, av_ref, wg_ref, wv_ref, bg_ref, bv_ref, o_ref):
        for lt in range(tc // LANES):
            lanes = slice(lt * LANES, (lt + 1) * LANES)
            wg, wv, bg, bv = wg_ref[:, lanes], wv_ref[:, lanes], bg_ref[:, lanes], bv_ref[:, lanes]

            def chunk(ci, carry):
                r0 = pl.multiple_of(ci * ch, ch)
                gate = _conv3(_taps(_with_halo(ag_ref, r0, ci, n_chunks, ch, lanes, True, False)), wg, bg)
                val = _conv3(_taps(_with_halo(av_ref, r0, ci, n_chunks, ch, lanes, True, False)), wv, bv)
                o_ref[pl.ds(r0, ch), lanes] = (gate * jax.nn.sigmoid(gate) * val).astype(o_ref.dtype)
                return carry

            lax.fori_loop(0, n_chunks, chunk, 0)

    return pl.pallas_call(
        body, name="conv_gate_fwd", grid=(nc,),
        in_specs=[pl.BlockSpec((rows, tc), lambda j: (0, j)), pl.BlockSpec((rows, tc), lambda j: (0, j + nc)),
                  pl.BlockSpec((SUBLANES, tc), lambda j: (0, j)), pl.BlockSpec((SUBLANES, tc), lambda j: (0, j + nc)),
                  pl.BlockSpec((1, tc), lambda j: (0, j)), pl.BlockSpec((1, tc), lambda j: (0, j + nc))],
        out_specs=pl.BlockSpec((rows, tc), lambda j: (0, j)),
        out_shape=jax.ShapeDtypeStruct((rows, f), BF16), compiler_params=_params(("parallel",)),
    )(a, a, conv_w, conv_w, conv_b, conv_b)


def _conv_gate_bwd(a, conv_w, conv_b, dg, *, tc=512):
    rows, f2 = a.shape
    f = f2 // 2
    tc = _tile(f, tc)
    nc = f // tc
    ch = _tile(rows, CONV_ROWS, SUBLANES)
    n_chunks = rows // ch
    ext_rows = ch + SUBLANES

    def fold(x):
        return jnp.sum(x.reshape(ch // SUBLANES, SUBLANES, LANES), axis=0)

    def body(ag_ref, av_ref, wg_ref, wv_ref, bg_ref, bv_ref, dg_ref, da_ref, dw_ref, db_ref):
        for lt in range(tc // LANES):
            lanes = slice(lt * LANES, (lt + 1) * LANES)
            wg, wv, bg, bv = wg_ref[:, lanes], wv_ref[:, lanes], bg_ref[:, lanes], bv_ref[:, lanes]

            def chunk(ci, acc):
                r0 = pl.multiple_of(ci * ch, ch)
                taps_g = _taps(_with_halo(ag_ref, r0, ci, n_chunks, ch, lanes, True, True))
                taps_v = _taps(_with_halo(av_ref, r0, ci, n_chunks, ch, lanes, True, True))
                dge = _with_halo(dg_ref, r0, ci, n_chunks, ch, lanes, False, True)
                gate, val = _conv3(taps_g, wg, bg), _conv3(taps_v, wv, bv)
                sg = jax.nn.sigmoid(gate)
                d_gate = dge * val * sg * (1.0 + gate * (1.0 - sg))
                d_val = dge * gate * sg
                new = []
                for half, (taps, w, d) in enumerate(((taps_g, wg, d_gate), (taps_v, wv, d_val))):
                    da = (w[2:3, :] * d[:ch] + w[1:2, :] * pltpu.roll(d, ext_rows - 1, 0)[:ch]
                          + w[0:1, :] * pltpu.roll(d, ext_rows - 2, 0)[:ch])
                    da_ref[half, pl.ds(r0, ch), lanes] = da.astype(da_ref.dtype)
                    dc = d[:ch]
                    sums = [fold(dc)] + [fold(dc * t[:ch]) for t in taps]
                    new.append(tuple(x + s for x, s in zip(acc[half], sums)))
                return tuple(new)

            zero = tuple(jnp.zeros((SUBLANES, LANES), F32) for _ in range(4))
            acc = lax.fori_loop(0, n_chunks, chunk, (zero, zero))
            row = lax.broadcasted_iota(jnp.int32, (SUBLANES, LANES), 0)
            for half in range(2):
                db, *taps = (jnp.sum(x, axis=0, keepdims=True) for x in acc[half])
                db_ref[half, :, lanes] = db
                dw = jnp.zeros((SUBLANES, LANES), F32)
                for tap in range(3):
                    dw = jnp.where(row == tap, taps[tap], dw)
                dw_ref[half, :, lanes] = dw

    lo = lambda j: (0, j)
    hi = lambda j: (0, j + nc)
    both = lambda j: (0, 0, j)
    return pl.pallas_call(
        body, name="conv_gate_bwd", grid=(nc,),
        in_specs=[pl.BlockSpec((rows, tc), lo), pl.BlockSpec((rows, tc), hi), pl.BlockSpec((SUBLANES, tc), lo),
                  pl.BlockSpec((SUBLANES, tc), hi), pl.BlockSpec((1, tc), lo), pl.BlockSpec((1, tc), hi),
                  pl.BlockSpec((rows, tc), lo)],
        out_specs=[pl.BlockSpec((2, rows, tc), both), pl.BlockSpec((2, SUBLANES, tc), both), pl.BlockSpec((2, 1, tc), both)],
        out_shape=[jax.ShapeDtypeStruct((2, rows, f), BF16), jax.ShapeDtypeStruct((2, SUBLANES, f), F32),
                   jax.ShapeDtypeStruct((2, 1, f), F32)],
        compiler_params=_params(("parallel",)),
    )(a, a, conv_w, conv_w, conv_b, conv_b, dg)


def _wgrad(a, b, rows, cols, row_sharded, name, **kw):
    return functools.partial(_wgrad_half, a, b, rows, cols, row_sharded, name, **kw)


class _NoExchange:
    def __init__(self, later, ffn):
        self.later, self.ffn = later, ffn

    def mixer_weights(self, after):
        return self.later

    def ffn_weights_arrived(self, after):
        return None

    def ffn_weights(self, after):
        return self.ffn

    def ffn_down_arrived(self, after):
        return None

    def ffn_down_weight(self, after):
        return self.ffn["ffn_w_down"]

    def ffn_grads(self, makers, after):
        self.ffn_makers = makers
        return None

    def ffn_backward_done(self, after):
        return None


def _local_step(x, posf, target, w, hooks):
    rows, d = x.shape
    width = w["ssm_d"].shape[1]
    qr, kvr = w["mla_q_norm_w"].shape[1], w["mla_kv_norm_w"].shape[1]
    heads = w["mla_w_ukv"].shape[1] // HEAD_SLOT
    f2 = w["ffn_conv_b"].shape[1]
    inp = w["w_in"].shape[0]
    scale = (QK_NOPE_DIM + QK_ROPE_DIM) ** -0.5
    g = {}

    hn = _rmsnorm_fwd(x, w["attn_norm_w"], name="attn_norm")
    proj = _matmul(hn, w["w_in"], mode="nt", name="in_proj")

    s5_weights = (w["ssm_lambda_re"], w["ssm_lambda_im"], w["ssm_log_dt"], w["ssm_b_re"], w["ssm_b_im"])
    wb, wct, abar = _s5_bands(*s5_weights, w["ssm_c_re"], w["ssm_c_im"])
    states, y_pre, yg = _s5_fwd(proj, wb, wct, w["ssm_d"], abar)
    later = hooks.mixer_weights(yg)
    z = _matmul(yg, later["ssm_w_glu"], mode="nn", name="glu_proj", bias=w["ssm_b_glu"])
    ys = _glu_norm_fwd(y_pre, z, w["ssm_out_norm_w"])

    q_col, kv_col, kpe_col = width // qr, (width + qr) // kvr, (width + qr + kvr) // LANES
    assert width % qr == 0 and (width + qr) % kvr == 0
    qn = _rmsnorm_fwd(proj, w["mla_q_norm_w"], name="q_norm", width=qr, col=q_col)
    kvn = _rmsnorm_fwd(proj, w["mla_kv_norm_w"], name="kv_norm", width=kvr, col=kv_col)
    q = _matmul(qn, w["mla_w_uq"], mode="nn", name="q_proj")
    kv = _matmul(kvn, w["mla_w_ukv"], mode="nn", name="kv_proj")
    half = QK_ROPE_DIM // 2
    inv_freq = ROPE_THETA ** (-jnp.arange(0, QK_ROPE_DIM, 2, dtype=F32) / QK_ROPE_DIM)
    zeros = jnp.zeros((LANES - QK_ROPE_DIM,), F32)
    freq = jnp.concatenate([inv_freq, inv_freq, zeros]).reshape(1, LANES)
    sign = jnp.concatenate([-jnp.ones((half,), F32), jnp.ones((half,), F32), zeros]).reshape(1, LANES)
    cos, sin = _rope_tables(posf, freq, sign)
    qc, kc, vb = _attn_prep(q, kv, proj, kpe_col, cos, sin)
    o, lse = _attn_fwd(qc, kc, vb, scale=scale, tq=ATTN_BLOCK)
    ym = _rmsnorm_fwd(o, w["mla_out_norm_w"], name="mla_out_norm")
    ycat = jnp.concatenate([ys, ym], axis=1)
    h1 = _matmul(ycat, later["w_out"], mode="nn", name="out_proj", add=x, after=hooks.ffn_weights_arrived(ycat))

    hn2 = _rmsnorm_fwd(h1, w["ffn_norm_w"], name="ffn_norm")
    ffn = hooks.ffn_weights(hn2)
    a = _matmul(hn2, ffn["ffn_w_up"], mode="nn", name="ffn_up", tm=FFN_ROWS)
    started = hooks.ffn_down_arrived(a)
    conv_b = w["ffn_conv_b"] if started is None else w["ffn_conv_b"] + started[:1, :1]
    gated = _conv_gate_fwd(a, ffn["ffn_conv_w"], conv_b)
    w_down = hooks.ffn_down_weight(gated)
    h2 = _matmul(gated, w_down, mode="nn", name="ffn_down", add=h1, tk=2816, tm=FFN_ROWS)
    loss_tile, dh2, dh2_mxu, g["final_norm_w"] = _final_norm_loss(h2, w["final_norm_w"], target)

    dgated = _matmul(dh2_mxu, w_down, mode="nt", name="ffn_down_dx", tm=FFN_ROWS)
    da, dcw, dcb = _conv_gate_bwd(a, ffn["ffn_conv_w"], w["ffn_conv_b"], dgated)
    g["ffn_conv_w"] = jnp.concatenate([dcw[0, :3], dcw[1, :3]], axis=1)
    g["ffn_conv_b"] = jnp.concatenate([dcb[0], dcb[1]], axis=1)
    started = hooks.ffn_grads({
        "ffn_w_up": _wgrad(hn2, da, d, f2, False, "ffn_up_dw", b_split=True, tm=FFN_ROWS, tn=_tile(f2 // N_CHIPS, 1408)),
        "ffn_w_down": _wgrad(gated, dh2_mxu, f2 // 2, d, True, "ffn_down_dw", tm=f2 // 2 // N_CHIPS, tn=1024)}, dcb)
    dhn2 = _matmul(da, ffn["ffn_w_up"], mode="nt", name="ffn_up_dx", a_split=True, tk=_tile(f2 // 2, 2816), tm=FFN_ROWS,
                   after=started)
    dh1, dh1_mxu, g["ffn_norm_w"] = _rmsnorm_bwd(h1, w["ffn_norm_w"], dhn2, name="ffn_norm_bwd", add=dh2,
                                                dx_dtypes=(F32, BF16))

    dycat = _matmul(dh1_mxu, later["w_out"], mode="nt", name="out_proj_dx")
    g["w_out"] = _wgrad(ycat, dh1_mxu, 2 * width, d, True, "out_proj_dw")
    started = hooks.ffn_backward_done(dycat)
    mla_out_norm_w, ssm_out_norm_w = w["mla_out_norm_w"], w["ssm_out_norm_w"]
    if started is not None:
        mla_out_norm_w, ssm_out_norm_w = mla_out_norm_w + started[:1, :1], ssm_out_norm_w + started[:1, :1]

    do, g["mla_out_norm_w"] = _rmsnorm_bwd(o, mla_out_norm_w, dycat, name="mla_out_norm_bwd", width=width, dy_col=1)
    dq, dkv, dkpe_heads = _attn_bwd(qc, kc, vb, o, do, lse, cos, sin, scale=scale, tk=ATTN_BLOCK)
    dkpe = _kpe_bwd(dkpe_heads, cos, sin)
    g["mla_w_uq"] = _wgrad(qn, dq, qr, heads * HEAD_SLOT, False, "q_proj_dw")
    dqn = _matmul(dq, w["mla_w_uq"], mode="nt", name="q_proj_dx")
    dcq, g["mla_q_norm_w"] = _rmsnorm_bwd(proj, w["mla_q_norm_w"], dqn, name="q_norm_bwd", width=qr, col=q_col,
                                          dx_dtypes=(BF16,))
    g["mla_w_ukv"] = _wgrad(kvn, dkv, kvr, heads * HEAD_SLOT, False, "kv_proj_dw")
    dkvn = _matmul(dkv, w["mla_w_ukv"], mode="nt", name="kv_proj_dx")
    dckv, g["mla_kv_norm_w"] = _rmsnorm_bwd(proj, w["mla_kv_norm_w"], dkvn, name="kv_norm_bwd", width=kvr, col=kv_col,
                                            dx_dtypes=(BF16,))

    dz, dyg_a, g["ssm_out_norm_w"], g["ssm_b_glu"] = _glu_norm_bwd(y_pre, z, ssm_out_norm_w, dycat)
    dyg_b = _matmul(dz, later["ssm_w_glu"], mode="nt", name="glu_proj_dx")
    g["ssm_w_glu"] = _wgrad(yg, dz, width, width, True, "glu_proj_dw")
    du, dwb, dwct, dabar, g["ssm_d"] = _s5_bwd(proj, states, y_pre, dyg_a, dyg_b, wb, wct, w["ssm_d"], abar)
    (g["ssm_lambda_re"], g["ssm_lambda_im"], g["ssm_log_dt"], g["ssm_b_re"], g["ssm_b_im"], g["ssm_c_re"],
     g["ssm_c_im"]) = _s5_bands_bwd(*s5_weights, dwb, dwct, dabar)

    pad = jnp.zeros((rows, inp - (width + qr + kvr + LANES)), BF16)
    dproj = jnp.concatenate([du, dcq, dckv, dkpe, pad], axis=1)
    g["w_in"] = _wgrad(dproj, hn, inp, d, False, "in_proj_dw")
    dhn = _matmul(dproj, w["w_in"], mode="nn", name="in_proj_dx")
    dx, g["attn_norm_w"] = _rmsnorm_bwd(x, w["attn_norm_w"], dhn, name="attn_norm_bwd", add=dh1)
    return loss_tile, dx, g


ANY = pl.BlockSpec(memory_space=pl.ANY)
MESH = pl.DeviceIdType.MESH


def _mesh_pos():
    return lax.axis_index("x"), lax.axis_index("y"), lax.axis_index("c")


def _other_chips(x, y):
    return [(1 - x, y), (x, 1 - y), (1 - x, 1 - y)]


def _remote(src, dst, send_sems, recv_sems, k, to):
    return pltpu.make_async_remote_copy(src_ref=src, dst_ref=dst, send_sem=send_sems.at[k], recv_sem=recv_sems.at[k],
                                        device_id=to, device_id_type=MESH)


def _place_shard(shard, piece_idx, row_sharded, name, out_dtype=BF16, pieces=N_CHIPS, after=None):
    rs, cs = shard.shape
    tr = _tile(rs, 512, 2 * SUBLANES)
    rb = rs // tr
    extra = [] if after is None else [after]

    def body(p_ref, x_ref, *rest):
        o_ref = rest[-1]
        o_ref[...] = x_ref[...].astype(o_ref.dtype)

    if row_sharded:
        out_shape, out_map = (pieces * rs, cs), (lambda i, p_ref: (p_ref[0] * rb + i, 0))
    else:
        out_shape, out_map = (rs, pieces * cs), (lambda i, p_ref: (i, p_ref[0]))
    return pl.pallas_call(
        body, name=name, out_shape=jax.ShapeDtypeStruct(out_shape, out_dtype),
        grid_spec=pltpu.PrefetchScalarGridSpec(
            num_scalar_prefetch=1, grid=(rb,),
            in_specs=[pl.BlockSpec((tr, cs), lambda i, p_ref: (i, 0))] + [pl.BlockSpec(memory_space=pl.ANY)] * len(extra),
            out_specs=pl.BlockSpec((tr, cs), out_map)),
        compiler_params=_params(("parallel",)),
    )(piece_idx, shard, *extra)


def _gather_weights(placed, name):
    n = len(placed)
    meta = [(row_sharded, direct) for _, row_sharded, direct in placed]
    over_ici, over_d2d = _gather_plans(meta)
    forwarded = [t for t, (_, direct) in enumerate(meta) if not direct]

    def body(*refs):
        outs = refs[n:2 * n]
        send_sems, recv_sems, pass_send_sems, pass_recv_sems = refs[2 * n:]
        first, arrivals = over_ici(outs, send_sems, recv_sems)
        passed, passed_arrivals = over_d2d([outs[t] for t in forwarded], pass_send_sems, pass_recv_sems)
        for cp in first:
            cp.start()
        for t in range(n):
            for j in range(3):
                arrivals[3 * t + j].wait_recv()
                if t in forwarded:
                    passed[3 * forwarded.index(t) + j].start()
        for cp in passed_arrivals:
            cp.wait_recv()
        for cp in first + passed:
            cp.wait_send()

    return pl.pallas_call(
        body, name=name, in_specs=[ANY] * n, out_specs=[ANY] * n,
        out_shape=[jax.ShapeDtypeStruct(arr.shape, arr.dtype) for arr, _, _ in placed],
        input_output_aliases={t: t for t in range(n)},
        scratch_shapes=[pltpu.SemaphoreType.DMA((3 * n,)), pltpu.SemaphoreType.DMA((3 * n,)),
                        pltpu.SemaphoreType.DMA((3 * len(forwarded),)), pltpu.SemaphoreType.DMA((3 * len(forwarded),))],
    )(*[arr for arr, _, _ in placed])


def _gather_plans(meta):
    def window(ref, row_sharded, piece, half):
        r, cc = ref.shape
        if row_sharded:
            rs = r // N_CHIPS
            if half is None:
                return ref.at[pl.ds(piece * rs, rs), :]
            return ref.at[pl.ds(piece * rs + half * (rs // 2), rs // 2), :]
        cs = cc // N_CHIPS
        if half is None:
            return ref.at[:, pl.ds(piece * cs, cs)]
        return ref.at[pl.ds(half * (r // 2), r // 2), pl.ds(piece * cs, cs)]

    def over_ici(refs, send_sems, recv_sems):
        x, y, c = _mesh_pos()
        sends, recvs = [], []
        for t, (row_sharded, direct) in enumerate(meta):
            mine = window(refs[t], row_sharded, 2 * x + y, None if direct else c)
            for j, (px, py) in enumerate(_other_chips(x, y)):
                theirs = window(refs[t], row_sharded, 2 * px + py, None if direct else c)
                sends.append(_remote(mine, mine, send_sems, recv_sems, 3 * t + j, (px, py, c)))
                recvs.append(_remote(theirs, theirs, send_sems, recv_sems, 3 * t + j, (px, py, c)))
        return sends, recvs

    def over_d2d(refs, send_sems, recv_sems):
        x, y, c = _mesh_pos()
        sends, recvs = [], []
        rows = [row_sharded for row_sharded, direct in meta if not direct]
        for t, row_sharded in enumerate(rows):
            for j, (px, py) in enumerate(_other_chips(x, y)):
                got = window(refs[t], row_sharded, 2 * px + py, c)
                other = window(refs[t], row_sharded, 2 * px + py, 1 - c)
                sends.append(_remote(got, got, send_sems, recv_sems, 3 * t + j, (x, y, 1 - c)))
                recvs.append(_remote(other, other, send_sems, recv_sems, 3 * t + j, (x, y, 1 - c)))
        return sends, recvs

    return over_ici, over_d2d


HBM = pl.BlockSpec(memory_space=pltpu.HBM)
SEMAPHORES = pl.BlockSpec(memory_space=pltpu.SEMAPHORE)
DATAFLOW = pltpu.SideEffectType.DATAFLOW_SIDE_EFFECTING


def _start_copies(name, arrays, plan, n_copies, after):
    n = len(arrays)

    def body(*refs):
        sends, _ = plan(refs[:n], refs[n + 1], refs[n + 2])
        for cp in sends:
            cp.start()
        token = refs[2 * n + 3]
        token[...] = jnp.zeros_like(token)

    out = pl.pallas_call(
        body, name=name,
        out_shape=(pltpu.SemaphoreType.DMA((n_copies,)), pltpu.SemaphoreType.DMA((n_copies,)),
                   *[pltpu.HBM(a.shape, a.dtype) for a in arrays], jax.ShapeDtypeStruct((SUBLANES, LANES), F32)),
        in_specs=[HBM] * n + [ANY],
        out_specs=(SEMAPHORES, SEMAPHORES, *[HBM] * n, pl.BlockSpec(memory_space=pltpu.VMEM)),
        input_output_aliases={t: t + 2 for t in range(n)},
        compiler_params=pltpu.CompilerParams(has_side_effects=DATAFLOW),
    )(*[pltpu.with_memory_space_constraint(a, pltpu.HBM) for a in arrays], after)
    return out[0], out[1], list(out[2:2 + n]), out[2 + n]


def _wait_copies(name, started, plan, after):
    send_sems, recv_sems, arrays, _ = started
    n = len(arrays)

    def body(*refs):
        sends, recvs = plan(refs[:n], refs[n], refs[n + 1])
        for cp in sends:
            cp.wait_send()
        for cp in recvs:
            cp.wait_recv()

    out = pl.pallas_call(
        body, name=name, out_shape=[pltpu.HBM(a.shape, a.dtype) for a in arrays],
        in_specs=[HBM] * n + [SEMAPHORES, SEMAPHORES, ANY], out_specs=[HBM] * n,
        input_output_aliases={t: t for t in range(n)},
        compiler_params=pltpu.CompilerParams(has_side_effects=DATAFLOW),
    )(*arrays, send_sems, recv_sems, after)
    return list(out)


def _exchange(name, arrays, plan, n_copies, after=None):
    n = len(arrays)
    extra = [] if after is None else [after]

    def body(*refs):
        outs = refs[n + len(extra):2 * n + len(extra)]
        send_sems, recv_sems = refs[2 * n + len(extra):]
        sends, recvs = plan(outs, send_sems, recv_sems)
        for cp in sends:
            cp.start()
        for cp in recvs:
            cp.wait_recv()
        for cp in sends:
            cp.wait_send()

    return pl.pallas_call(
        body, name=name, in_specs=[ANY] * (n + len(extra)), out_specs=[ANY] * n,
        out_shape=[jax.ShapeDtypeStruct(a.shape, a.dtype) for a in arrays],
        input_output_aliases={t: t for t in range(n)},
        scratch_shapes=[pltpu.SemaphoreType.DMA((n_copies,)), pltpu.SemaphoreType.DMA((n_copies,))],
    )(*arrays, *extra)


def _give_plan(n):
    def plan(refs, send_sems, recv_sems):
        x, y, c = _mesh_pos()
        sends = [_remote(refs[t], refs[n + t], send_sems, recv_sems, t, (x, y, 1 - c)) for t in range(n)]
        return sends, sends

    return plan


def _scatter_plan(n):
    def plan(refs, send_sems, recv_sems):
        x, y, c = _mesh_pos()
        sends = []
        for t in range(n):
            for j, (px, py) in enumerate(_other_chips(x, y)):
                sends.append(_remote(refs[t].at[2 * px + py], refs[n + t].at[j], send_sems, recv_sems, 3 * t + j, (px, py, c)))
        return sends, sends

    return plan


def _scatter_shapes(sums):
    return [jax.ShapeDtypeStruct((3,) + s.shape[1:], s.dtype) for s in sums]


def _join_plan(n):
    def plan(refs, send_sems, recv_sems):
        x, y, c = _mesh_pos()
        sends = [_remote(refs[t].at[c], refs[t].at[c], send_sems, recv_sems, t, (x, y, 1 - c)) for t in range(n)]
        recvs = [_remote(refs[t].at[1 - c], refs[t].at[1 - c], send_sems, recv_sems, t, (x, y, 1 - c)) for t in range(n)]
        return sends, recvs

    return plan


def _join_halves(halves, name, after=None):
    return _exchange(name, halves, _join_plan(len(halves)), len(halves), after=after)


def _add_other_half(g4, got, where, name):
    _, pieces, sr, sc = g4.shape
    tr = _tile(sr, 512, 2 * SUBLANES)

    def body(w_ref, a_ref, b_ref, o_ref):
        o_ref[...] = a_ref[...] + b_ref[...]

    blk = pl.BlockSpec((None, tr, sc), lambda p, i, w_ref: (p, i, 0))
    return pl.pallas_call(
        body, name=name, out_shape=jax.ShapeDtypeStruct((pieces, sr, sc), F32),
        grid_spec=pltpu.PrefetchScalarGridSpec(
            num_scalar_prefetch=1, grid=(pieces, sr // tr),
            in_specs=[pl.BlockSpec((None, None, tr, sc), lambda p, i, w_ref: (w_ref[0], p, i, 0)), blk], out_specs=blk),
        compiler_params=_params(("parallel", "parallel")),
    )(where, g4, got)


def _add_pieces(sums, got_pieces, where, name, after=None):
    _, sr, sc = sums.shape
    tr = _tile(sr, 512, 2 * SUBLANES)
    extra = [] if after is None else [after]

    def body(w_ref, a_ref, r_ref, *rest):
        acc = a_ref[...]
        for j in range(3):
            acc = acc + r_ref[j].astype(F32)
        rest[-1][...] = acc

    return pl.pallas_call(
        body, name=name, out_shape=jax.ShapeDtypeStruct((N_CORES, sr, sc), F32),
        grid_spec=pltpu.PrefetchScalarGridSpec(
            num_scalar_prefetch=1, grid=(sr // tr,),
            in_specs=[pl.BlockSpec((None, tr, sc), lambda i, w_ref: (w_ref[1], i, 0)),
                      pl.BlockSpec((3, tr, sc), lambda i, w_ref: (0, i, 0))] + [pl.BlockSpec(memory_space=pl.ANY)] * len(extra),
            out_specs=pl.BlockSpec((None, tr, sc), lambda i, w_ref: (w_ref[0], i, 0))),
        compiler_params=_params(("parallel",)),
    )(where, sums, got_pieces, *extra)


def _adamw_update(w, g, m, v):
    nm = ADAM_B1 * m + (1.0 - ADAM_B1) * g
    nv = ADAM_B2 * v + (1.0 - ADAM_B2) * (g * g)
    m_hat = nm / (1.0 - ADAM_B1 ** ADAM_STEP)
    v_hat = nv / (1.0 - ADAM_B2 ** ADAM_STEP)
    return -ADAM_LR * (m_hat / (jnp.sqrt(v_hat) + ADAM_EPS) + ADAM_WD * w), nm, nv


def _adamw(w, g, m, v, name, after=None):
    rows, cols = w.shape
    halves = 2 if g.ndim == 3 else 1
    bc = cols // halves
    tr = _tile(rows, max(SUBLANES, (1 << 20) // max(bc, 1) // SUBLANES * SUBLANES), SUBLANES)

    def body(w_ref, g_ref, m_ref, v_ref, *rest):
        d_ref, nm_ref, nv_ref, go_ref = rest[-4:]
        gv = g_ref[...]
        d_ref[...], nm_ref[...], nv_ref[...] = _adamw_update(w_ref[...], gv, m_ref[...], v_ref[...])
        go_ref[...] = gv

    blk = pl.BlockSpec((tr, bc), lambda i, h: (i, h))
    g_blk = pl.BlockSpec((None, tr, bc), lambda i, h: (h, i, 0)) if halves == 2 else blk
    extra = [] if after is None else [after]
    return pl.pallas_call(
        body, name=name, grid=(rows // tr, halves),
        in_specs=[blk, g_blk, blk, blk] + [pl.BlockSpec(memory_space=pl.ANY)] * len(extra), out_specs=[blk] * 4,
        out_shape=[jax.ShapeDtypeStruct((rows, cols), F32)] * 4, compiler_params=_params(("parallel", "parallel")),
    )(w, g, m, v, *extra)


def _adamw_many(ws, gs, ms, vs, name):
    n = len(ws)

    def body(*refs):
        outs = refs[4 * n:]
        for k in range(n):
            w_ref, g_ref, m_ref, v_ref = (refs[j * n + k] for j in range(4))
            outs[k][...], outs[n + k][...], outs[2 * n + k][...] = _adamw_update(w_ref[...], g_ref[...], m_ref[...], v_ref[...])

    out = pl.pallas_call(
        body, name=name, out_shape=[jax.ShapeDtypeStruct(w.shape, F32) for w in ws] * 3,
        compiler_params=pltpu.CompilerParams(vmem_limit_bytes=VMEM_LIMIT_BYTES),
    )(*ws, *gs, *ms, *vs)
    return out[:n], out[n:2 * n], out[2 * n:]


WEIGHTS = ['attn_norm_w', 'w_in', 'ssm_lambda_re', 'ssm_lambda_im', 'ssm_log_dt', 'ssm_b_re', 'ssm_b_im', 'ssm_c_re',
           'ssm_c_im', 'ssm_d', 'ssm_w_glu', 'ssm_b_glu', 'mla_q_norm_w', 'mla_w_uq', 'mla_kv_norm_w', 'mla_w_ukv',
           'ssm_out_norm_w', 'mla_out_norm_w', 'w_out', 'ffn_norm_w', 'ffn_w_up', 'ffn_conv_w', 'ffn_conv_b',
           'ffn_w_down', 'final_norm_w']
SHARDED = {'w_in': False, 'ssm_w_glu': True, 'mla_w_uq': False, 'mla_w_ukv': False, 'w_out': True, 'ffn_w_up': False,
           'ffn_w_down': True}
SMALL = [n for n in WEIGHTS if n not in SHARDED and n != 'ffn_conv_w']
ROPE_PAD = HEAD_SLOT - QK_NOPE_DIM - QK_ROPE_DIM
SMALL_COLS = 8 * LANES


def _pad_heads(w_uq, heads):
    qr = w_uq.shape[0]
    w3 = w_uq.reshape(qr, heads, QK_NOPE_DIM + QK_ROPE_DIM)
    return jnp.concatenate([w3, jnp.zeros((qr, heads, ROPE_PAD), w_uq.dtype)], axis=2).reshape(qr, heads * HEAD_SLOT)


def _unpad_heads(g_uq, heads):
    qr = g_uq.shape[0]
    return g_uq.reshape(qr, heads, HEAD_SLOT)[:, :, :QK_NOPE_DIM + QK_ROPE_DIM].reshape(qr, -1)


FFN = ['ffn_w_up', 'ffn_w_down']
MIXER_LATER = ['ssm_w_glu', 'w_out']
MIXER_BIG = ['w_in', 'w_out']
FFN_GATHER = FFN + ['ffn_conv_w']


class _Overlapped:
    def __init__(self, placed_first, first_sharding, where):
        self.where, self.mine, self.other = where, where[:1], 1 - where[:1]
        self.first_ici, self.first_d2d = _gather_plans([(r, False) for r in first_sharding])
        self.first = _start_copies("gather_first_start", placed_first, self.first_ici, 3 * len(placed_first), where)
        self.first_started = self.first[3]

    def start_rest(self, placed_later, placed):
        self.later_ici, self.later_d2d = _gather_plans([(SHARDED[n], False) for n in MIXER_LATER])
        self.later = _start_copies("gather_later_start", placed_later, self.later_ici, 3 * len(placed_later),
                                   self.first_started)
        up, down, taps = placed
        self.up_ici, self.up_d2d = _gather_plans([(SHARDED["ffn_w_up"], False), (False, True)])
        self.up = _start_copies("gather_ffn_up_start", [up, taps], self.up_ici, 6, self.later[3])
        self.down_ici, self.down_d2d = _gather_plans([(SHARDED["ffn_w_down"], False)])
        self.down = _start_copies("gather_ffn_down_start", [down], self.down_ici, 3, self.up[3])
        self.gather_started = self.down[3]
        arrived = _wait_copies("gather_first_wait", self.first, self.first_ici, self.gather_started)
        return _exchange("gather_first_pass", arrived, self.first_d2d, 3 * len(arrived))

    def mixer_weights(self, after):
        arrived = _wait_copies("gather_later_wait", self.later, self.later_ici, after)
        return dict(zip(MIXER_LATER, _exchange("gather_later_pass", arrived, self.later_d2d, 3 * len(arrived))))

    def ffn_weights_arrived(self, after):
        up, self.taps = _wait_copies("gather_ffn_up_wait", self.up, self.up_ici, after)
        self.up_passing = _start_copies("gather_ffn_up_pass_start", [up], self.up_d2d, 3, after)
        return self.up_passing[3]

    def ffn_weights(self, after):
        w_up, = _wait_copies("gather_ffn_up_pass_wait", self.up_passing, self.up_d2d, after)
        return {"ffn_w_up": w_up, "ffn_conv_w": self.taps}

    def ffn_down_arrived(self, after):
        down, = _wait_copies("gather_ffn_down_wait", self.down, self.down_ici, after)
        self.down_passing = _start_copies("gather_ffn_down_pass_start", [down], self.down_d2d, 3, after)
        return self.down_passing[3]

    def ffn_down_weight(self, after):
        return _wait_copies("gather_ffn_down_pass_wait", self.down_passing, self.down_d2d, after)[0]

    def ffn_grads(self, makers, after):
        self.makers = [makers[name] for name in FFN]
        n = len(FFN)
        give = [make(self.other, suffix="_give") for make in self.makers]
        lands = [lax.empty(g.shape, g.dtype) for g in give]
        self.swap = _start_copies("grad_ffn_swap_start", give + lands, _give_plan(n), n, after)
        return self.swap[3]

    def ffn_backward_done(self, after):
        n = len(FFN)
        got = _wait_copies("grad_ffn_swap_wait", self.swap, _give_plan(n), after)[n:]
        kept = [make(self.mine, suffix="_keep", add=got[t], wire=True) for t, make in enumerate(self.makers)]
        self.sums = [k[0] for k in kept]
        wires = [k[1] for k in kept]
        lands = [lax.empty(s.shape, s.dtype) for s in _scatter_shapes(wires)]
        self.scatter = _start_copies("grad_ffn_scatter_start", wires + lands, _scatter_plan(n), 3 * n, after)
        return self.scatter[3]

    def ffn_reduced(self, after):
        n = len(FFN)
        got_pieces = _wait_copies("grad_ffn_scatter_wait", self.scatter, _scatter_plan(n), after)[n:]
        halves = []
        for t, name in enumerate(FFN):
            halves.append(_add_pieces(self.sums[t], got_pieces[t], self.where, "grad_add_pieces_" + name,
                                      after=halves[-1] if halves else None))
        return halves


def _step(args):
    x, positions, target = args["x"][0], args["positions"], args["loss_target"][0]
    rows = x.shape[0]
    p = {n: args[n] for n in WEIGHTS}
    xi, yi, ci = _mesh_pos()
    piece = 2 * xi + yi

    def transposed(a):
        return jnp.swapaxes(a[0], 0, 1)

    def as_stored(n, a):
        return jnp.swapaxes(a, 2, 3) if n in ("ssm_b_re", "ssm_b_im") else a

    w_in = transposed(p["w_in"])
    in_width = w_in.shape[0]
    in_pad = (-in_width) % (2 * LANES)
    heads_here = p["mla_w_uq"].shape[2] // (QK_NOPE_DIM + QK_ROPE_DIM)
    shards = {
        "w_in": jnp.pad(w_in, ((0, in_pad), (0, 0))),
        "ssm_w_glu": p["ssm_w_glu"][0],
        "mla_w_uq": _pad_heads(p["mla_w_uq"][0], heads_here),
        "mla_w_ukv": p["mla_w_ukv"][0],
        "w_out": p["w_out"][0],
        "ffn_w_up": p["ffn_w_up"][0],
        "ffn_w_down": p["ffn_w_down"][0],
    }
    conv_w = jnp.pad(p["ffn_conv_w"][0], ((0, SUBLANES - p["ffn_conv_w"].shape[1]), (0, 0)))
    order = list(SHARDED)
    piece_idx = piece.reshape(1).astype(jnp.int32)
    mixer = [n for n in order if n not in FFN]
    first = [n for n in mixer if n not in MIXER_LATER]
    where = jnp.stack([ci, piece]).astype(jnp.int32)
    placed = {n: _place_shard(shards[n], piece_idx, SHARDED[n], "place_" + n) for n in first}
    hooks = _Overlapped([placed[n] for n in first], [SHARDED[n] for n in first], where)
    for n in order:
        if n not in first:
            placed[n] = _place_shard(shards[n], piece_idx, SHARDED[n], "place_" + n, after=hooks.first_started)
    placed["ffn_conv_w"] = _place_shard(conv_w, piece_idx, False, "place_ffn_conv_w", out_dtype=F32,
                                        after=hooks.first_started)
    w = dict(zip(first, hooks.start_rest([placed[n] for n in MIXER_LATER], [placed[n] for n in FFN_GATHER])))
    groups = p["ssm_lambda_re"].shape[1]
    w.update({
        "attn_norm_w": p["attn_norm_w"] + hooks.gather_started[:1, :1],
        "ssm_lambda_re": p["ssm_lambda_re"][0], "ssm_lambda_im": p["ssm_lambda_im"][0],
        "ssm_log_dt": p["ssm_log_dt"].reshape(groups, 1), "ssm_b_re": as_stored("ssm_b_re", p["ssm_b_re"])[0],
        "ssm_b_im": as_stored("ssm_b_im", p["ssm_b_im"])[0], "ssm_c_re": p["ssm_c_re"][0], "ssm_c_im": p["ssm_c_im"][0],
        "ssm_d": p["ssm_d"], "ssm_b_glu": p["ssm_b_glu"], "mla_q_norm_w": p["mla_q_norm_w"],
        "mla_kv_norm_w": p["mla_kv_norm_w"], "ssm_out_norm_w": p["ssm_out_norm_w"], "mla_out_norm_w": p["mla_out_norm_w"],
        "ffn_norm_w": p["ffn_norm_w"], "ffn_conv_b": p["ffn_conv_b"], "final_norm_w": p["final_norm_w"].reshape(1, -1),
    })

    loss_tile, dx, g = _local_step(x, positions.reshape(rows, 1).astype(F32), target, w, hooks)

    flat = [g[n].reshape(-1) for n in SMALL] + [g["ffn_conv_w"].reshape(-1), loss_tile[0, :1]]
    sizes = [f.shape[0] for f in flat]
    per_block = -(-sum(sizes) // (N_CORES * N_CHIPS * SMALL_COLS))
    small_rows = -(-per_block // (2 * SUBLANES)) * (2 * SUBLANES)
    padded = N_CORES * N_CHIPS * small_rows * SMALL_COLS

    def pack(parts):
        parts = list(parts)
        have = sum(q.shape[0] for q in parts)
        return jnp.concatenate(parts + [jnp.zeros((padded - have,), F32)])

    reduced = mixer + ["small"]
    small = pack(flat).reshape(N_CORES, N_CHIPS, small_rows, SMALL_COLS)
    give = [g[n](hooks.other, suffix="_give") for n in mixer] + [lax.dynamic_index_in_dim(small, 1 - ci, 0, keepdims=False)]
    lands = [lax.empty(a.shape, a.dtype) for a in give]
    give_plan = _give_plan(len(reduced))
    swap = _start_copies("grad_mixer_swap_start", give + lands, give_plan, len(reduced), dx)

    grads, delta, new_m, new_v = {}, {}, {}, {}

    def finish(n, joined, after=None):
        grad = joined if SHARDED[n] else joined.reshape(-1, joined.shape[2])
        if n == "w_in":
            wt, mt, vt = w_in, transposed(args["m_w_in"]), transposed(args["v_w_in"])
            out = _adamw(wt, grad, mt, vt, "adamw_w_in")
            delta[n], new_m[n], new_v[n], grads[n] = (jnp.swapaxes(a, 0, 1)[None] for a in out)
            return
        if n == "mla_w_uq":
            grad = _unpad_heads(grad, heads_here)
        adam(n, grad, after)

    def adam(n, grad, after=None):
        shape = p[n].shape
        out = _adamw(p[n].reshape(shape[1:]), grad, args["m_" + n].reshape(shape[1:]),
                     args["v_" + n].reshape(shape[1:]), "adamw_" + n, after)
        delta[n], new_m[n], new_v[n], grads[n] = (a.reshape(shape) for a in out)

    ffn_halves = hooks.ffn_reduced(swap[3])
    got = _wait_copies("grad_mixer_swap_wait", swap, give_plan, ffn_halves[-1])[len(reduced):]
    join_plan = _join_plan(len(FFN))
    ffn_join = _start_copies("grad_ffn_join_start", ffn_halves, join_plan, len(FFN), got[0])
    big = [t for t, n in enumerate(reduced) if n in MIXER_BIG]
    rest = [t for t in range(len(reduced)) if t not in big]
    sums, wires = {}, {}
    for t in big:
        sums[t], wires[t] = g[reduced[t]](hooks.mine, suffix="_keep", add=got[t], wire=True)
    ffn_joined = _wait_copies("grad_ffn_join_wait", ffn_join, join_plan, sums[big[-1]])

    def scatter_start(name, group, after):
        lands = [lax.empty(s.shape, s.dtype) for s in _scatter_shapes([wires[t] for t in group])]
        return _start_copies(name, [wires[t] for t in group] + lands, _scatter_plan(len(group)), 3 * len(group), after)

    scatter_big = scatter_start("grad_big_scatter_start", big, ffn_joined[0])
    for t in rest[:-1]:
        sums[t], wires[t] = g[reduced[t]](hooks.mine, suffix="_keep", add=got[t], wire=True, after=scatter_big[3])
    sums[rest[-1]] = wires[rest[-1]] = _add_other_half(small, got[-1], where, "grad_add_half_small")
    scatter_rest = scatter_start("grad_rest_scatter_start", rest, sums[rest[0]])
    behind = scatter_rest[3]
    for n, joined in zip(FFN, ffn_joined):
        finish(n, joined, after=behind)
        behind = delta[n]
    got_pieces = dict(zip(big, _wait_copies("grad_big_scatter_wait", scatter_big, _scatter_plan(len(big)),
                                            delta[FFN[-1]])[len(big):]))
    got_pieces.update(zip(rest, _wait_copies("grad_rest_scatter_wait", scatter_rest, _scatter_plan(len(rest)),
                                             got_pieces[big[0]])[len(rest):]))
    halves = [_add_pieces(sums[t], got_pieces[t], where, "grad_add_pieces_" + n) for t, n in enumerate(reduced)]
    joined = _join_halves(halves, "grad_join_halves")
    for n, j in zip(mixer, joined):
        finish(n, j)
    eighths = _place_shard(joined[-1].reshape(N_CORES * small_rows, SMALL_COLS), piece_idx, True, "place_small_grads",
                           out_dtype=F32)
    small_sum = _gather_weights([(eighths, True, False)], "gather_small_grads")[0]
    flat_sum = small_sum.reshape(N_CHIPS, N_CORES, small_rows * SMALL_COLS).transpose(1, 0, 2).reshape(-1)
    offs = [0]
    for s in sizes:
        offs.append(offs[-1] + s)
    stored = {n: as_stored(n, p[n]) for n in SMALL}
    for k, n in enumerate(SMALL):
        grads[n] = flat_sum[offs[k]:offs[k + 1]].reshape(stored[n].shape)
    taps, cols_here = p["ffn_conv_w"].shape[1], p["ffn_conv_w"].shape[2]
    conv_full = flat_sum[offs[len(SMALL)]:offs[len(SMALL) + 1]].reshape(taps, N_CHIPS * cols_here)
    adam("ffn_conv_w", lax.dynamic_slice_in_dim(conv_full, piece * cols_here, cols_here, axis=1))
    loss = flat_sum[offs[len(SMALL) + 1]]

    def rank2(a):
        return a.reshape(1, -1) if a.ndim == 1 else a

    d_s, m_s, v_s = _adamw_many([rank2(stored[n]) for n in SMALL], [rank2(grads[n]) for n in SMALL],
                                [rank2(as_stored(n, args["m_" + n])) for n in SMALL],
                                [rank2(as_stored(n, args["v_" + n])) for n in SMALL], "adamw_small")
    for k, n in enumerate(SMALL):
        delta[n], new_m[n], new_v[n], grads[n] = (as_stored(n, a.reshape(stored[n].shape))
                                                  for a in (d_s[k], m_s[k], v_s[k], grads[n]))

    return (loss, dx[None], *[grads[n] for n in WEIGHTS], *[delta[n] for n in WEIGHTS],
            *[new_m[n] for n in WEIGHTS], *[new_v[n] for n in WEIGHTS])


def kernel(x, positions, attn_norm_w, w_in, ssm_lambda_re, ssm_lambda_im, ssm_log_dt, ssm_b_re, ssm_b_im, ssm_c_re, ssm_c_im, ssm_d, ssm_w_glu, ssm_b_glu, mla_q_norm_w, mla_w_uq, mla_kv_norm_w, mla_w_ukv, ssm_out_norm_w, mla_out_norm_w, w_out, ffn_norm_w, ffn_w_up, ffn_conv_w, ffn_conv_b, ffn_w_down, final_norm_w, loss_target, m_attn_norm_w, m_w_in, m_ssm_lambda_re, m_ssm_lambda_im, m_ssm_log_dt, m_ssm_b_re, m_ssm_b_im, m_ssm_c_re, m_ssm_c_im, m_ssm_d, m_ssm_w_glu, m_ssm_b_glu, m_mla_q_norm_w, m_mla_w_uq, m_mla_kv_norm_w, m_mla_w_ukv, m_ssm_out_norm_w, m_mla_out_norm_w, m_w_out, m_ffn_norm_w, m_ffn_w_up, m_ffn_conv_w, m_ffn_conv_b, m_ffn_w_down, m_final_norm_w, v_attn_norm_w, v_w_in, v_ssm_lambda_re, v_ssm_lambda_im, v_ssm_log_dt, v_ssm_b_re, v_ssm_b_im, v_ssm_c_re, v_ssm_c_im, v_ssm_d, v_ssm_w_glu, v_ssm_b_glu, v_mla_q_norm_w, v_mla_w_uq, v_mla_kv_norm_w, v_mla_w_ukv, v_ssm_out_norm_w, v_mla_out_norm_w, v_w_out, v_ffn_norm_w, v_ffn_w_up, v_ffn_conv_w, v_ffn_conv_b, v_ffn_w_down, v_final_norm_w):
    return _step(dict(locals()))
```

```python
import functools
import math

import jax
import jax.numpy as jnp
from jax import lax
from jax.experimental import pallas as pl
from jax.experimental.pallas import tpu as pltpu

F32 = jnp.float32
BF16 = jnp.bfloat16

SSM_GROUP = 16
SSM_STATE = 64
QK_NOPE_DIM = 128
QK_ROPE_DIM = 64
ROPE_THETA = 10000.0
RMS_EPS = 1e-6
ADAM_LR, ADAM_B1, ADAM_B2, ADAM_EPS, ADAM_WD, ADAM_STEP = 0.001, 0.9, 0.999, 1e-08, 0.01, 10

LANES = 128
SUBLANES = 8
VMEM_LIMIT_BYTES = 56 * 1024 * 1024

GROUPS_PER_BATCH = LANES // SSM_GROUP
STATE_PER_BATCH = GROUPS_PER_BATCH * SSM_STATE
HEAD_SLOT = 2 * LANES
NEG_INF = -1e30
ATTN_BLOCK = 1024
FFN_ROWS = 1024

N_CHIPS = 4
N_CORES = 2


def _tile(n, pref, align=LANES):
    if n <= pref:
        return n
    t = (pref // align) * align
    while t >= align:
        if n % t == 0:
            return t
        t -= align
    return n


def _params(sem):
    return pltpu.CompilerParams(dimension_semantics=sem, vmem_limit_bytes=VMEM_LIMIT_BYTES)


def _dot(a, b, dims):
    return lax.dot_general(a, b, (dims, ((), ())), preferred_element_type=F32)


def _dot_nn(a, b):
    return _dot(a, b, ((1,), (0,)))


def _dot_nt(a, b):
    return _dot(a, b, ((1,), (1,)))


def _dot_tn(a, b):
    return _dot(a, b, ((0,), (0,)))


def _matmul(a, b, *, mode, name, tm=1024, tn=1024, tk=2048, bias=None, add=None, out_dtype=F32,
            a_split=False, b_split=False, after=None):
    if a_split:
        assert mode == "nt"
        a_shape = (a.shape[1], 2 * a.shape[2])
    else:
        a_shape = a.shape
    if b_split:
        assert mode == "tn"
        b_shape = (b.shape[1], 2 * b.shape[2])
    else:
        b_shape = b.shape
    if mode == "nn":
        (m, k), (k2, n) = a_shape, b_shape
    elif mode == "nt":
        (m, k), (n, k2) = a_shape, b_shape
    else:
        (k, m), (k2, n) = a_shape, b_shape
    assert k == k2, (a.shape, b.shape, mode)
    tm, tn, tk = _tile(m, tm, SUBLANES), _tile(n, tn), _tile(k, tk)
    nk = k // tk
    a_spec = {"nn": pl.BlockSpec((tm, tk), lambda i, j, kk: (i, kk)),
              "nt": pl.BlockSpec((tm, tk), lambda i, j, kk: (i, kk)),
              "tn": pl.BlockSpec((tk, tm), lambda i, j, kk: (kk, i))}[mode]
    b_spec = {"nn": pl.BlockSpec((tk, tn), lambda i, j, kk: (kk, j)),
              "nt": pl.BlockSpec((tn, tk), lambda i, j, kk: (j, kk)),
              "tn": pl.BlockSpec((tk, tn), lambda i, j, kk: (kk, j))}[mode]
    if a_split:
        kb = a.shape[2] // tk
        assert a.shape[2] % tk == 0
        a_spec = pl.BlockSpec((None, tm, tk), lambda i, j, kk: (kk // kb, i, kk % kb))
    if b_split:
        nb = b.shape[2] // tn
        assert b.shape[2] % tn == 0
        b_spec = pl.BlockSpec((None, tk, tn), lambda i, j, kk: (j // nb, kk, j % nb))
    dot = {"nn": _dot_nn, "nt": _dot_nt, "tn": _dot_tn}[mode]
    in_specs, operands = [a_spec, b_spec], [a, b]
    if bias is not None:
        in_specs.append(pl.BlockSpec((1, tn), lambda i, j, kk: (0, j)))
        operands.append(bias)
    if add is not None:
        in_specs.append(pl.BlockSpec((tm, tn), lambda i, j, kk: (i, j)))
        operands.append(add)
    if after is not None:
        in_specs.append(pl.BlockSpec(memory_space=pl.ANY))
        operands.append(after)

    def body(*refs):
        a_ref, b_ref = refs[0], refs[1]
        rest = list(refs[2:])
        bias_ref = rest.pop(0) if bias is not None else None
        add_ref = rest.pop(0) if add is not None else None
        if after is not None:
            rest.pop(0)
        o_ref, acc_ref = rest

        def finish(acc):
            if bias_ref is not None:
                acc = acc + bias_ref[...]
            if add_ref is not None:
                acc = acc + add_ref[...]
            o_ref[...] = acc.astype(o_ref.dtype)

        part = dot(a_ref[...].astype(BF16), b_ref[...].astype(BF16))
        if nk == 1:
            finish(part)
        else:
            kk = pl.program_id(2)

            @pl.when(kk == 0)
            def _():
                acc_ref[...] = part

            @pl.when(jnp.logical_and(kk > 0, kk < nk - 1))
            def _():
                acc_ref[...] += part

            @pl.when(kk == nk - 1)
            def _():
                finish(acc_ref[...] + part)

    out_shape = jax.ShapeDtypeStruct((m, n), out_dtype)
    out_spec = pl.BlockSpec((tm, tn), lambda i, j, kk: (i, j))
    acc_shape = (tm, tn) if nk > 1 else (SUBLANES, LANES)
    return pl.pallas_call(
        body, name=name, grid=(m // tm, n // tn, nk), in_specs=in_specs, out_specs=out_spec, out_shape=out_shape,
        scratch_shapes=[pltpu.VMEM(acc_shape, F32)],
        compiler_params=_params(("parallel", "parallel", "arbitrary")),
    )(*operands)


def _wgrad_half(a, b, rows, cols, row_sharded, name, which, *, suffix="", add=None, wire=False, tm=None, tn=None,
                b_split=False, after=None):
    tokens = a.shape[0]
    if row_sharded:
        sr, sc = rows // N_CHIPS, cols // N_CORES
    else:
        sr, sc = rows // N_CORES, cols // N_CHIPS
    tm = _tile(sr, 1024) if tm is None else tm
    tn = _tile(sc, 1024) if tn is None else tn
    assert sr % tm == 0 and sc % tn == 0, (rows, cols, tm, tn)
    rb, cb = sr // tm, sc // tn
    if tn >= tm:
        ij, grid = (lambda s, t: (t, s)), (N_CHIPS, cb, rb)
    else:
        ij, grid = (lambda s, t: (s, t)), (N_CHIPS, rb, cb)
    if row_sharded:
        a_tile = lambda p, i, j, h: p * rb + i
        b_tile = lambda p, i, j, h: h[0] * cb + j
    else:
        a_tile = lambda p, i, j, h: h[0] * rb + i
        b_tile = lambda p, i, j, h: p * cb + j
    a_spec = pl.BlockSpec((tokens, tm), lambda p, s, t, h: (0, a_tile(p, *ij(s, t), h)))
    if b_split:
        nbh = b.shape[2] // tn
        assert b.shape[2] % tn == 0
        b_spec = pl.BlockSpec((None, tokens, tn), lambda p, s, t, h: (b_tile(p, *ij(s, t), h) // nbh, 0,
                                                                       b_tile(p, *ij(s, t), h) % nbh))
    else:
        b_spec = pl.BlockSpec((tokens, tn), lambda p, s, t, h: (0, b_tile(p, *ij(s, t), h)))
    out_spec = pl.BlockSpec((None, tm, tn), lambda p, s, t, h: (p, *ij(s, t)))
    in_specs, operands = [a_spec, b_spec], [a, b]
    if add is not None:
        in_specs.append(out_spec)
        operands.append(add)
    if after is not None:
        in_specs.append(pl.BlockSpec(memory_space=pl.ANY))
        operands.append(after)
    out_dtypes = [F32, BF16] if wire else [F32]

    def body(h_ref, a_ref, b_ref, *rest):
        acc = _dot_tn(a_ref[...].astype(BF16), b_ref[...].astype(BF16))
        if add is not None:
            acc = acc + rest[0][...]
        for o_ref in rest[-len(out_dtypes):]:
            o_ref[...] = acc.astype(o_ref.dtype)

    out = pl.pallas_call(
        body, name=name + suffix, out_shape=[jax.ShapeDtypeStruct((N_CHIPS, sr, sc), dt) for dt in out_dtypes],
        grid_spec=pltpu.PrefetchScalarGridSpec(num_scalar_prefetch=1, grid=grid, in_specs=in_specs,
                                               out_specs=[out_spec] * len(out_dtypes)),
        compiler_params=_params(("parallel", "parallel", "parallel")),
    )(which, *operands)
    return tuple(out) if wire else out[0]


def _rms_rows(x):
    return lax.rsqrt(jnp.mean(x * x, axis=-1, keepdims=True) + RMS_EPS)


def _rmsnorm_fwd(x, w, *, name, width=None, col=0, out_dtype=BF16, tr=512):
    rows = x.shape[0]
    width = x.shape[1] if width is None else width
    tr = _tile(rows, tr, SUBLANES)

    def body(x_ref, w_ref, o_ref):
        xv = x_ref[...]
        o_ref[...] = (xv * _rms_rows(xv) * w_ref[...]).astype(o_ref.dtype)

    return pl.pallas_call(
        body, name=name, grid=(rows // tr,),
        in_specs=[pl.BlockSpec((tr, width), lambda i: (i, col)), pl.BlockSpec((1, width), lambda i: (0, 0))],
        out_specs=pl.BlockSpec((tr, width), lambda i: (i, 0)),
        out_shape=jax.ShapeDtypeStruct((rows, width), out_dtype),
        compiler_params=_params(("parallel",)),
    )(x, w)


def _rmsnorm_bwd_rows(xv, w, dy):
    r = _rms_rows(xv)
    n = xv * r
    dn = dy * w
    dx = r * (dn - n * jnp.mean(dn * n, axis=-1, keepdims=True))
    return dx, dy * n


def _rmsnorm_bwd(x, w, dy, *, name, width=None, col=0, dy_col=0, add=None, tr=512, dx_dtypes=(F32,)):
    rows = x.shape[0]
    n_dx = len(dx_dtypes)
    width = x.shape[1] if width is None else width
    tr = _tile(rows, tr, SUBLANES)
    in_specs = [pl.BlockSpec((tr, width), lambda i: (i, col)), pl.BlockSpec((1, width), lambda i: (0, 0)),
                pl.BlockSpec((tr, width), lambda i: (i, dy_col))]
    operands = [x, w, dy]
    if add is not None:
        in_specs.append(pl.BlockSpec((tr, width), lambda i: (i, 0)))
        operands.append(add)

    def body(*refs):
        x_ref, w_ref, dy_ref = refs[:3]
        add_ref = refs[3] if add is not None else None
        dx_refs, dw_ref = refs[-1 - n_dx:-1], refs[-1]
        dx, dwp = _rmsnorm_bwd_rows(x_ref[...], w_ref[...], dy_ref[...])
        if add_ref is not None:
            dx = dx + add_ref[...]
        for dx_ref in dx_refs:
            dx_ref[...] = dx.astype(dx_ref.dtype)
        part = jnp.sum(dwp, axis=0, keepdims=True)

        @pl.when(pl.program_id(0) == 0)
        def _():
            dw_ref[...] = part

        @pl.when(pl.program_id(0) > 0)
        def _():
            dw_ref[...] += part

    return pl.pallas_call(
        body, name=name, grid=(rows // tr,), in_specs=in_specs,
        out_specs=[pl.BlockSpec((tr, width), lambda i: (i, 0))] * n_dx + [pl.BlockSpec((1, width), lambda i: (0, 0))],
        out_shape=[jax.ShapeDtypeStruct((rows, width), dt) for dt in dx_dtypes] + [jax.ShapeDtypeStruct((1, width), F32)],
        compiler_params=_params(("arbitrary",)),
    )(*operands)


def _final_norm_loss(h, w, target, *, tr=512):
    rows, d = h.shape
    tr = _tile(rows, tr, SUBLANES)

    def body(h_ref, w_ref, t_ref, loss_ref, dh_ref, dhb_ref, dw_ref):
        hv, wv = h_ref[...], w_ref[...]
        r = _rms_rows(hv)
        n = hv * r
        err = n * wv - t_ref[...]
        d_out = err * (1.0 / d)
        dn = d_out * wv
        dh = r * (dn - n * jnp.mean(dn * n, axis=-1, keepdims=True))
        dh_ref[...] = dh
        dhb_ref[...] = dh.astype(BF16)
        dw_part = jnp.sum(d_out * n, axis=0, keepdims=True)
        loss_part = jnp.full((SUBLANES, LANES), 0.5 / d, F32) * jnp.sum(err * err)

        @pl.when(pl.program_id(0) == 0)
        def _():
            dw_ref[...] = dw_part
            loss_ref[...] = loss_part

        @pl.when(pl.program_id(0) > 0)
        def _():
            dw_ref[...] += dw_part
            loss_ref[...] += loss_part

    return pl.pallas_call(
        body, name="final_norm_loss", grid=(rows // tr,),
        in_specs=[pl.BlockSpec((tr, d), lambda i: (i, 0)), pl.BlockSpec((1, d), lambda i: (0, 0)),
                  pl.BlockSpec((tr, d), lambda i: (i, 0))],
        out_specs=[pl.BlockSpec((SUBLANES, LANES), lambda i: (0, 0)), pl.BlockSpec((tr, d), lambda i: (i, 0)),
                   pl.BlockSpec((tr, d), lambda i: (i, 0)), pl.BlockSpec((1, d), lambda i: (0, 0))],
        out_shape=[jax.ShapeDtypeStruct((SUBLANES, LANES), F32), jax.ShapeDtypeStruct((rows, d), F32),
                   jax.ShapeDtypeStruct((rows, d), BF16), jax.ShapeDtypeStruct((1, d), F32)],
        compiler_params=_params(("arbitrary",)),
    )(h, w, target)


def _cmul(ar, ai, br, bi):
    return ar * br - ai * bi, ar * bi + ai * br


def _dot_exact(a, b, dims):
    return lax.dot_general(a, b, (dims, ((), ())), preferred_element_type=F32, precision=lax.Precision.HIGHEST)


def _s5_discretize(lr, li, dt):
    mag = jnp.exp(lr * dt)
    th = li * dt
    ar, ai = mag * jnp.cos(th), mag * jnp.sin(th)
    nr, ni = ar - 1.0, ai
    den = lr * lr + li * li
    zr = (nr * lr + ni * li) / den
    zi = (ni * lr - nr * li) / den
    return mag, ar, ai, nr, ni, den, zr, zi


def _band_slices(group):
    j, gi = divmod(group, GROUPS_PER_BATCH)
    rows = slice(gi * SSM_GROUP, (gi + 1) * SSM_GROUP)
    re = slice(gi * SSM_STATE, (gi + 1) * SSM_STATE)
    im = slice(STATE_PER_BATCH + gi * SSM_STATE, STATE_PER_BATCH + (gi + 1) * SSM_STATE)
    return j, rows, re, im


def _s5_bands(lam_re, lam_im, log_dt, b_re, b_im, c_re, c_im):
    g, _ = lam_re.shape
    nb = g // GROUPS_PER_BATCH
    s2 = 2 * STATE_PER_BATCH

    def body(lr_ref, li_ref, ldt_ref, br_ref, bi_ref, cr_ref, ci_ref, wb_ref, wct_ref, a_ref):
        dt = jnp.exp(ldt_ref[...])
        _, ar, ai, _, _, _, zr, zi = _s5_discretize(lr_ref[...], li_ref[...], dt)
        wb_ref[...] = jnp.zeros_like(wb_ref)
        wct_ref[...] = jnp.zeros_like(wct_ref)
        for group in range(g):
            j, rows, re, im = _band_slices(group)
            zr_g, zi_g = zr[group:group + 1, :], zi[group:group + 1, :]
            bre, bim = br_ref[group], bi_ref[group]
            wb_ref[j, rows, re] = (zr_g * bre - zi_g * bim).astype(BF16)
            wb_ref[j, rows, im] = (zr_g * bim + zi_g * bre).astype(BF16)
            wct_ref[j, rows, re] = cr_ref[group].astype(BF16)
            wct_ref[j, rows, im] = (-ci_ref[group]).astype(BF16)
            a_ref[j, :, re] = ar[group:group + 1, :]
            a_ref[j, :, im] = ai[group:group + 1, :]

    return pl.pallas_call(
        body, name="s5_bands",
        out_shape=[jax.ShapeDtypeStruct((nb, LANES, s2), BF16)] * 2 + [jax.ShapeDtypeStruct((nb, 1, s2), F32)],
    )(lam_re, lam_im, log_dt, b_re, b_im, c_re, c_im)


def _s5_bands_bwd(lam_re, lam_im, log_dt, b_re, b_im, dwb, dwct, dabar):
    g, p = lam_re.shape
    gh = b_re.shape[1:]

    def body(lr_ref, li_ref, ldt_ref, br_ref, bi_ref, dwb_ref, dwct_ref, da_ref,
             dlr_ref, dli_ref, dldt_ref, dbre_ref, dbim_ref, dcre_ref, dcim_ref, dzr_ref, dzi_ref, dar_ref, dai_ref):
        lr, li = lr_ref[...], li_ref[...]
        dt = jnp.exp(ldt_ref[...])
        mag, ar, ai, nr, ni, den, zr, zi = _s5_discretize(lr, li, dt)
        for group in range(g):
            j, rows, re, im = _band_slices(group)
            zr_g, zi_g = zr[group:group + 1, :], zi[group:group + 1, :]
            bre, bim = br_ref[group], bi_ref[group]
            dbr, dbi = dwb_ref[j, rows, re], dwb_ref[j, rows, im]
            dbre_ref[group] = zr_g * dbr + zi_g * dbi
            dbim_ref[group] = zr_g * dbi - zi_g * dbr
            dzr_ref[group:group + 1, :] = jnp.sum(bre * dbr + bim * dbi, axis=0, keepdims=True)
            dzi_ref[group:group + 1, :] = jnp.sum(bre * dbi - bim * dbr, axis=0, keepdims=True)
            dcre_ref[group] = dwct_ref[j, rows, re]
            dcim_ref[group] = -dwct_ref[j, rows, im]
            dar_ref[group:group + 1, :] = da_ref[j, :, re]
            dai_ref[group:group + 1, :] = da_ref[j, :, im]
        dzr, dzi = dzr_ref[...], dzi_ref[...]
        inv = 1.0 / den
        d_nr = (dzr * lr - dzi * li) * inv
        d_ni = (dzr * li + dzi * lr) * inv
        d_den = -(dzr * zr + dzi * zi) * inv
        d_lr = (dzr * nr + dzi * ni) * inv + 2.0 * lr * d_den
        d_li = (dzr * ni - dzi * nr) * inv + 2.0 * li * d_den
        t_ar = dar_ref[...] + d_nr
        t_ai = dai_ref[...] + d_ni
        d_lrdt = t_ar * ar + t_ai * ai
        d_th = t_ai * ar - t_ar * ai
        dlr_ref[...] = d_lr + d_lrdt * dt
        dli_ref[...] = d_li + d_th * dt
        dldt_ref[...] = jnp.sum(d_lrdt * lr + d_th * li, axis=1, keepdims=True) * dt

    return pl.pallas_call(
        body, name="s5_bands_bwd",
        out_shape=[jax.ShapeDtypeStruct((g, p), F32)] * 2 + [jax.ShapeDtypeStruct((g, 1), F32)]
        + [jax.ShapeDtypeStruct((g,) + gh, F32)] * 4,
        scratch_shapes=[pltpu.VMEM((g, p), F32)] * 4,
    )(lam_re, lam_im, log_dt, b_re, b_im, dwb, dwct, dabar)


def _powers(ar, ai, count):
    out = [(ar, ai)]
    for _ in range(count - 1):
        out.append(_cmul(out[-1][0], out[-1][1], ar, ai))
    return out


def _scan_coefs(ar, ai, reverse):
    w = ar.shape[-1]
    pw = _powers(ar, ai, SUBLANES)
    row = lax.broadcasted_iota(jnp.int32, (SUBLANES, w), 0)
    steps = []
    d = 1
    while d < SUBLANES:
        keep = (row < SUBLANES - d) if reverse else (row >= d)
        pr, pi = pw[d - 1]
        steps.append((d, jnp.where(keep, pr, 0.0), jnp.where(keep, pi, 0.0)))
        d *= 2
    cr = jnp.zeros((SUBLANES, w), F32)
    ci = jnp.zeros((SUBLANES, w), F32)
    for t in range(SUBLANES):
        pr, pi = pw[SUBLANES - 1 - t] if reverse else pw[t]
        cr = jnp.where(row == t, pr, cr)
        ci = jnp.where(row == t, pi, ci)
    return steps, cr, ci


def _scan_tile(xr, xi, carry_r, carry_i, coefs, reverse):
    steps, cr, ci = coefs
    for d, mr, mi in steps:
        shift = SUBLANES - d if reverse else d
        sr, si = pltpu.roll(xr, shift, 0), pltpu.roll(xi, shift, 0)
        pr, pi = _cmul(mr, mi, sr, si)
        xr, xi = xr + pr, xi + pi
    pr, pi = _cmul(cr, ci, carry_r, carry_i)
    return xr + pr, xi + pi


def _gelu(x):
    c = math.sqrt(2.0 / math.pi)
    return 0.5 * x * (1.0 + jnp.tanh(c * (x + 0.044715 * x * x * x)))


def _gelu_grad(x):
    c = math.sqrt(2.0 / math.pi)
    t = jnp.tanh(c * (x + 0.044715 * x * x * x))
    return 0.5 * (1.0 + t) + 0.5 * x * (1.0 - t * t) * c * (1.0 + 3.0 * 0.044715 * x * x)


def _s5_fwd(proj, wb, wct, d_skip, abar):
    rows = proj.shape[0]
    nb = wb.shape[0]
    s2 = 2 * STATE_PER_BATCH
    st = STATE_PER_BATCH
    chunk = _tile(rows, 1024, SUBLANES)

    def body(u_ref, wb_ref, wc_ref, d_ref, a_ref, s_ref, y_ref, yg_ref):
        for c0 in range(0, rows, chunk):
            s_ref[pl.ds(c0, chunk), :] = _dot_nn(u_ref[pl.ds(c0, chunk), :].astype(BF16), wb_ref[...])
        av = a_ref[...]
        coefs = _scan_coefs(av[:, :st], av[:, st:], reverse=False)

        def tile(b, carry):
            r0 = pl.multiple_of(b * SUBLANES, SUBLANES)
            xr, xi = _scan_tile(s_ref[pl.ds(r0, SUBLANES), :st], s_ref[pl.ds(r0, SUBLANES), st:], carry[0], carry[1],
                                coefs, False)
            s_ref[pl.ds(r0, SUBLANES), :st] = xr
            s_ref[pl.ds(r0, SUBLANES), st:] = xi
            return xr[SUBLANES - 1:, :], xi[SUBLANES - 1:, :]

        zero = jnp.zeros((1, st), F32)
        lax.fori_loop(0, rows // SUBLANES, tile, (zero, zero))
        for c0 in range(0, rows, chunk):
            y = _dot_nt(s_ref[pl.ds(c0, chunk), :].astype(BF16), wc_ref[...]) + d_ref[...] * u_ref[pl.ds(c0, chunk), :]
            y_ref[pl.ds(c0, chunk), :] = y
            yg_ref[pl.ds(c0, chunk), :] = _gelu(y).astype(BF16)

    return pl.pallas_call(
        body, name="s5_fwd", grid=(nb,),
        in_specs=[pl.BlockSpec((rows, LANES), lambda j: (0, j)), pl.BlockSpec((None, LANES, s2), lambda j: (j, 0, 0)),
                  pl.BlockSpec((None, LANES, s2), lambda j: (j, 0, 0)), pl.BlockSpec((1, LANES), lambda j: (0, j)),
                  pl.BlockSpec((None, 1, s2), lambda j: (j, 0, 0))],
        out_specs=[pl.BlockSpec((rows, s2), lambda j: (0, j)), pl.BlockSpec((rows, LANES), lambda j: (0, j)),
                   pl.BlockSpec((rows, LANES), lambda j: (0, j))],
        out_shape=[jax.ShapeDtypeStruct((rows, nb * s2), F32), jax.ShapeDtypeStruct((rows, nb * LANES), F32),
                   jax.ShapeDtypeStruct((rows, nb * LANES), BF16)],
        compiler_params=_params(("parallel",)),
    )(proj, wb, wct, d_skip, abar)


def _s5_bwd(proj, states, y_pre, dyg_a, dyg_b, wb, wct, d_skip, abar):
    rows = proj.shape[0]
    nb = wb.shape[0]
    s2 = 2 * STATE_PER_BATCH
    st = STATE_PER_BATCH
    chunk = _tile(rows, 1024, SUBLANES)
    n_tiles = rows // SUBLANES

    def body(u_ref, s_ref, y_ref, ga_ref, gb_ref, wb_ref, wc_ref, d_ref, a_ref,
             du_ref, dwb_ref, dwc_ref, da_ref, dd_ref, ds_ref, dy_ref):
        dy_ref[...] = (ga_ref[...] + gb_ref[...]) * _gelu_grad(y_ref[...])
        dd_ref[...] = jnp.sum(dy_ref[...] * u_ref[...], axis=0, keepdims=True)
        for c0 in range(0, rows, chunk):
            ds_ref[pl.ds(c0, chunk), :] = _dot_nn(dy_ref[pl.ds(c0, chunk), :].astype(BF16), wc_ref[...])
        dwc_ref[...] = _dot_tn(dy_ref[...].astype(BF16), s_ref[...].astype(BF16))
        av = a_ref[...]
        coefs = _scan_coefs(av[:, :st], -av[:, st:], reverse=True)
        row = lax.broadcasted_iota(jnp.int32, (SUBLANES, st), 0)

        def tile(k, carry):
            cr, ci, acc_r, acc_i = carry
            b = n_tiles - 1 - k
            r0 = pl.multiple_of(b * SUBLANES, SUBLANES)
            xr, xi = _scan_tile(ds_ref[pl.ds(r0, SUBLANES), :st], ds_ref[pl.ds(r0, SUBLANES), st:], cr, ci, coefs, True)
            ds_ref[pl.ds(r0, SUBLANES), :st] = xr
            ds_ref[pl.ds(r0, SUBLANES), st:] = xi
            nr = jnp.where(row == SUBLANES - 1, cr, pltpu.roll(xr, SUBLANES - 1, 0))
            ni = jnp.where(row == SUBLANES - 1, ci, pltpu.roll(xi, SUBLANES - 1, 0))
            pr, pi = s_ref[pl.ds(r0, SUBLANES), :st], s_ref[pl.ds(r0, SUBLANES), st:]
            acc_r = acc_r + pr * nr + pi * ni
            acc_i = acc_i + pr * ni - pi * nr
            return xr[:1, :], xi[:1, :], acc_r, acc_i

        zero = jnp.zeros((1, st), F32)
        zacc = jnp.zeros((SUBLANES, st), F32)
        _, _, acc_r, acc_i = lax.fori_loop(0, n_tiles, tile, (zero, zero, zacc, zacc))
        da_ref[:, :st] = jnp.sum(acc_r, axis=0, keepdims=True)
        da_ref[:, st:] = jnp.sum(acc_i, axis=0, keepdims=True)
        for c0 in range(0, rows, chunk):
            du_ref[pl.ds(c0, chunk), :] = (_dot_nt(ds_ref[pl.ds(c0, chunk), :].astype(BF16), wb_ref[...])
                                           + d_ref[...] * dy_ref[pl.ds(c0, chunk), :]).astype(du_ref.dtype)
        dwb_ref[...] = _dot_tn(u_ref[...].astype(BF16), ds_ref[...].astype(BF16))

    col = pl.BlockSpec((rows, LANES), lambda j: (0, j))
    return pl.pallas_call(
        body, name="s5_bwd", grid=(nb,),
        in_specs=[col, pl.BlockSpec((rows, s2), lambda j: (0, j)), col, col, col,
                  pl.BlockSpec((None, LANES, s2), lambda j: (j, 0, 0)), pl.BlockSpec((None, LANES, s2), lambda j: (j, 0, 0)),
                  pl.BlockSpec((1, LANES), lambda j: (0, j)), pl.BlockSpec((None, 1, s2), lambda j: (j, 0, 0))],
        out_specs=[col, pl.BlockSpec((None, LANES, s2), lambda j: (j, 0, 0)),
                   pl.BlockSpec((None, LANES, s2), lambda j: (j, 0, 0)), pl.BlockSpec((None, 1, s2), lambda j: (j, 0, 0)),
                   pl.BlockSpec((1, LANES), lambda j: (0, j))],
        out_shape=[jax.ShapeDtypeStruct((rows, nb * LANES), BF16), jax.ShapeDtypeStruct((nb, LANES, s2), F32),
                   jax.ShapeDtypeStruct((nb, LANES, s2), F32), jax.ShapeDtypeStruct((nb, 1, s2), F32),
                   jax.ShapeDtypeStruct((1, nb * LANES), F32)],
        scratch_shapes=[pltpu.VMEM((rows, s2), F32), pltpu.VMEM((rows, LANES), F32)],
        compiler_params=_params(("parallel",)),
    )(proj, states, y_pre, dyg_a, dyg_b, wb, wct, d_skip, abar)


def _glu_norm_fwd(y_pre, z, w, *, tr=512):
    rows, width = y_pre.shape
    tr = _tile(rows, tr, SUBLANES)

    def body(y_ref, z_ref, w_ref, o_ref):
        v = _gelu(y_ref[...]) * jax.nn.sigmoid(z_ref[...])
        o_ref[...] = (v * _rms_rows(v) * w_ref[...]).astype(o_ref.dtype)

    blk = pl.BlockSpec((tr, width), lambda i: (i, 0))
    return pl.pallas_call(
        body, name="glu_norm_fwd", grid=(rows // tr,),
        in_specs=[blk, blk, pl.BlockSpec((1, width), lambda i: (0, 0))], out_specs=blk,
        out_shape=jax.ShapeDtypeStruct((rows, width), BF16), compiler_params=_params(("parallel",)),
    )(y_pre, z, w)


def _glu_norm_bwd(y_pre, z, w, dycat, *, tr=512):
    rows, width = y_pre.shape
    tr = _tile(rows, tr, SUBLANES)

    def body(y_ref, z_ref, w_ref, dy_ref, dz_ref, dg_ref, dw_ref, db_ref):
        yg = _gelu(y_ref[...])
        sg = jax.nn.sigmoid(z_ref[...])
        dv, dwp = _rmsnorm_bwd_rows(yg * sg, w_ref[...], dy_ref[...])
        dz = dv * yg * sg * (1.0 - sg)
        dz_ref[...] = dz.astype(dz_ref.dtype)
        dg_ref[...] = dv * sg
        dw_part = jnp.sum(dwp, axis=0, keepdims=True)
        db_part = jnp.sum(dz, axis=0, keepdims=True)

        @pl.when(pl.program_id(0) == 0)
        def _():
            dw_ref[...] = dw_part
            db_ref[...] = db_part

        @pl.when(pl.program_id(0) > 0)
        def _():
            dw_ref[...] += dw_part
            db_ref[...] += db_part

    blk = pl.BlockSpec((tr, width), lambda i: (i, 0))
    vec = pl.BlockSpec((1, width), lambda i: (0, 0))
    return pl.pallas_call(
        body, name="glu_norm_bwd", grid=(rows // tr,), in_specs=[blk, blk, vec, blk], out_specs=[blk, blk, vec, vec],
        out_shape=[jax.ShapeDtypeStruct((rows, width), BF16), jax.ShapeDtypeStruct((rows, width), F32)]
        + [jax.ShapeDtypeStruct((1, width), F32)] * 2,
        compiler_params=_params(("arbitrary",)),
    )(y_pre, z, w, dycat)


def _rope_tables(pos, freq, sign):
    rows = pos.shape[0]

    def body(p_ref, f_ref, s_ref, cos_ref, sin_ref):
        ang = p_ref[...] * f_ref[...]
        cos_ref[...] = jnp.cos(ang)
        sin_ref[...] = jnp.sin(ang) * s_ref[...]

    return pl.pallas_call(body, name="rope_tables", out_shape=[jax.ShapeDtypeStruct((rows, LANES), F32)] * 2)(pos, freq, sign)


def _rope(x, cos, sin_signed):
    half = QK_ROPE_DIM // 2
    src = lax.broadcasted_iota(jnp.int32, (LANES, LANES), 0)
    dst = lax.broadcasted_iota(jnp.int32, (LANES, LANES), 1)
    swap = jnp.where(jnp.logical_or(jnp.logical_and(dst < half, src == dst + half),
                                    jnp.logical_and(jnp.logical_and(dst >= half, dst < 2 * half), src == dst - half)),
                     1.0, 0.0).astype(F32)
    swapped = _dot_exact(x, swap, ((1,), (0,)))
    return x * cos + swapped * sin_signed


def _attn_prep(q, kv, proj, kpe_col, cos, sin, *, tr=512):
    rows = q.shape[0]
    heads = q.shape[1] // HEAD_SLOT
    tr = _tile(rows, tr, SUBLANES)

    def body(q_ref, kv_ref, kpe_ref, cos_ref, sin_ref, qc_ref, kc_ref, v_ref):
        c, s = cos_ref[...], sin_ref[...]
        kpe = _rope(kpe_ref[...], c, s).astype(BF16)
        for h in range(heads):
            nope = slice(h * HEAD_SLOT, h * HEAD_SLOT + LANES)
            pe = slice(h * HEAD_SLOT + LANES, (h + 1) * HEAD_SLOT)
            qc_ref[:, nope] = q_ref[:, nope].astype(BF16)
            qc_ref[:, pe] = _rope(q_ref[:, pe], c, s).astype(BF16)
            kc_ref[:, nope] = kv_ref[:, nope].astype(BF16)
            kc_ref[:, pe] = kpe
            v_ref[:, h * LANES:(h + 1) * LANES] = kv_ref[:, pe].astype(BF16)

    slots = pl.BlockSpec((tr, heads * HEAD_SLOT), lambda i: (i, 0))
    tab = pl.BlockSpec((tr, LANES), lambda i: (i, 0))
    return pl.pallas_call(
        body, name="attn_prep", grid=(rows // tr,),
        in_specs=[slots, slots, pl.BlockSpec((tr, LANES), lambda i: (i, kpe_col)), tab, tab],
        out_specs=[slots, slots, pl.BlockSpec((tr, heads * LANES), lambda i: (i, 0))],
        out_shape=[jax.ShapeDtypeStruct((rows, heads * HEAD_SLOT), BF16)] * 2
        + [jax.ShapeDtypeStruct((rows, heads * LANES), BF16)],
        compiler_params=_params(("parallel",)),
    )(q, kv, proj, cos, sin)


def _causal(tq, tk):
    return lax.broadcasted_iota(jnp.int32, (tq, tk), 1) <= lax.broadcasted_iota(jnp.int32, (tq, tk), 0)


def _attn_fwd(qc, kc, vb, *, scale, tq=512):
    rows = qc.shape[0]
    heads = qc.shape[1] // HEAD_SLOT
    tq = _tile(rows, tq, SUBLANES)
    tk = tq

    def body(q_ref, k_ref, v_ref, o_ref, lse_ref):
        i = pl.program_id(1)
        q = q_ref[...]

        def step(j, carry, diagonal):
            m, l, acc = carry
            k0 = pl.multiple_of(j * tk, tk)
            s = _dot_nt(q, k_ref[pl.ds(k0, tk), :]) * scale
            if diagonal:
                s = jnp.where(_causal(tq, tk), s, NEG_INF)
            m_new = jnp.maximum(m, jnp.max(s, axis=-1, keepdims=True))
            p = jnp.exp(s - m_new)
            alpha = jnp.exp(m - m_new)
            l = alpha * l + jnp.sum(p, axis=-1, keepdims=True)
            acc = alpha * acc + _dot_nn(p.astype(BF16), v_ref[pl.ds(k0, tk), :])
            return m_new, l, acc

        init = (jnp.full((tq, 1), NEG_INF, F32), jnp.zeros((tq, 1), F32), jnp.zeros((tq, LANES), F32))
        below = lax.fori_loop(0, i, lambda j, carry: step(j, carry, False), init)
        m, l, acc = step(i, below, True)
        o_ref[...] = acc / l
        lse_ref[...] = jnp.broadcast_to(m + jnp.log(l), (tq, LANES))

    return pl.pallas_call(
        body, name="attn_fwd", grid=(heads, rows // tq),
        in_specs=[pl.BlockSpec((tq, HEAD_SLOT), lambda h, i: (i, h)), pl.BlockSpec((rows, HEAD_SLOT), lambda h, i: (0, h)),
                  pl.BlockSpec((rows, LANES), lambda h, i: (0, h))],
        out_specs=[pl.BlockSpec((tq, LANES), lambda h, i: (i, h))] * 2,
        out_shape=[jax.ShapeDtypeStruct((rows, heads * LANES), F32)] * 2,
        compiler_params=_params(("parallel", "parallel")),
    )(qc, kc, vb)


def _attn_bwd(qc, kc, vb, o, do, lse, cos, sin, *, scale, tk=512):
    rows = qc.shape[0]
    heads = qc.shape[1] // HEAD_SLOT
    tk = _tile(rows, tk, SUBLANES)
    tq = tk
    nq = rows // tq

    def body(q_ref, k_ref, v_ref, o_ref, do_ref, lse_ref, cos_ref, sin_ref, dq_ref, dkv_ref, dkpe_ref, dq_acc, delta_ref):
        j = pl.program_id(1)

        @pl.when(j == 0)
        def _():
            dq_acc[...] = jnp.zeros_like(dq_acc)
            for r0 in range(0, rows, tq):
                d = jnp.sum(do_ref[pl.ds(r0, tq), :] * o_ref[pl.ds(r0, tq), :], axis=-1, keepdims=True)
                delta_ref[pl.ds(r0, tq), :] = jnp.broadcast_to(d, (tq, LANES))

        kb, vv = k_ref[...], v_ref[...]

        def step(i, carry, diagonal):
            dk, dv = carry
            q0 = pl.multiple_of(i * tq, tq)
            qb = q_ref[pl.ds(q0, tq), :]
            dob = do_ref[pl.ds(q0, tq), :].astype(BF16)
            s = _dot_nt(qb, kb) * scale
            p = jnp.exp(s - lse_ref[pl.ds(q0, tq), :1])
            if diagonal:
                p = jnp.where(_causal(tq, tk), p, 0.0)
            dv = dv + _dot_tn(p.astype(BF16), dob)
            ds = (p * (_dot_nt(dob, vv) - delta_ref[pl.ds(q0, tq), :1])).astype(BF16)
            dk = dk + _dot_tn(ds, qb)
            dq_acc[pl.ds(q0, tq), :] += _dot_nn(ds, kb)
            return dk, dv

        zero = (jnp.zeros((tk, HEAD_SLOT), F32), jnp.zeros((tk, LANES), F32))
        dk, dv = lax.fori_loop(j + 1, nq, lambda i, carry: step(i, carry, False), step(j, zero, True))
        dkv_ref[:, :LANES] = (dk[:, :LANES] * scale).astype(dkv_ref.dtype)
        dkv_ref[:, LANES:] = dv.astype(dkv_ref.dtype)
        dkpe_ref[...] = dk[:, LANES:] * scale

        @pl.when(j == nq - 1)
        def _():
            for r0 in range(0, rows, tq):
                dq = dq_acc[pl.ds(r0, tq), :] * scale
                dq_ref[pl.ds(r0, tq), :LANES] = dq[:, :LANES].astype(dq_ref.dtype)
                dq_ref[pl.ds(r0, tq), LANES:] = _rope(dq[:, LANES:], cos_ref[pl.ds(r0, tq), :],
                                                      -sin_ref[pl.ds(r0, tq), :]).astype(dq_ref.dtype)

    full_q = pl.BlockSpec((rows, HEAD_SLOT), lambda h, j: (0, h))
    full_v = pl.BlockSpec((rows, LANES), lambda h, j: (0, h))
    tab = pl.BlockSpec((rows, LANES), lambda h, j: (0, 0))
    return pl.pallas_call(
        body, name="attn_bwd", grid=(heads, rows // tk),
        in_specs=[full_q, pl.BlockSpec((tk, HEAD_SLOT), lambda h, j: (j, h)), pl.BlockSpec((tk, LANES), lambda h, j: (j, h)),
                  full_v, full_v, full_v, tab, tab],
        out_specs=[full_q, pl.BlockSpec((tk, HEAD_SLOT), lambda h, j: (j, h)), pl.BlockSpec((tk, LANES), lambda h, j: (j, h))],
        out_shape=[jax.ShapeDtypeStruct((rows, heads * HEAD_SLOT), BF16), jax.ShapeDtypeStruct((rows, heads * HEAD_SLOT), BF16),
                   jax.ShapeDtypeStruct((rows, heads * LANES), F32)],
        scratch_shapes=[pltpu.VMEM((rows, HEAD_SLOT), F32), pltpu.VMEM((rows, LANES), F32)],
        compiler_params=_params(("parallel", "arbitrary")),
    )(qc, kc, vb, o, do, lse, cos, sin)


def _kpe_bwd(dkpe_heads, cos, sin, *, tr=512):
    rows = dkpe_heads.shape[0]
    heads = dkpe_heads.shape[1] // LANES
    tr = _tile(rows, tr, 2 * SUBLANES)

    def body(d_ref, cos_ref, sin_ref, o_ref):
        acc = d_ref[:, :LANES]
        for h in range(1, heads):
            acc = acc + d_ref[:, h * LANES:(h + 1) * LANES]
        o_ref[...] = _rope(acc, cos_ref[...], -sin_ref[...]).astype(o_ref.dtype)

    tab = pl.BlockSpec((tr, LANES), lambda i: (i, 0))
    return pl.pallas_call(
        body, name="kpe_bwd", grid=(rows // tr,),
        in_specs=[pl.BlockSpec((tr, heads * LANES), lambda i: (i, 0)), tab, tab], out_specs=tab,
        out_shape=jax.ShapeDtypeStruct((rows, LANES), BF16), compiler_params=_params(("parallel",)),
    )(dkpe_heads, cos, sin)


CONV_ROWS = 512


def _with_halo(ref, r0, ci, n_chunks, ch, lanes, before, after):
    parts = []
    if before:
        lo = pl.multiple_of(jnp.maximum(r0 - SUBLANES, 0), SUBLANES)
        parts.append(ref[pl.ds(lo, SUBLANES), lanes] * jnp.where(ci > 0, 1.0, 0.0))
    parts.append(ref[pl.ds(r0, ch), lanes])
    if after:
        hi = pl.multiple_of(jnp.minimum(r0 + ch, n_chunks * ch - SUBLANES), SUBLANES)
        parts.append(ref[pl.ds(hi, SUBLANES), lanes] * jnp.where(ci < n_chunks - 1, 1.0, 0.0))
    return jnp.concatenate(parts, axis=0)


def _taps(ext):
    return pltpu.roll(ext, 2, 0)[SUBLANES:], pltpu.roll(ext, 1, 0)[SUBLANES:], ext[SUBLANES:]


def _conv3(taps, w, b):
    return w[0:1, :] * taps[0] + w[1:2, :] * taps[1] + w[2:3, :] * taps[2] + b


def _conv_gate_fwd(a, conv_w, conv_b, *, tc=512):
    rows, f2 = a.shape
    f = f2 // 2
    tc = _tile(f, tc)
    nc = f // tc
    ch = _tile(rows, CONV_ROWS, SUBLANES)
    n_chunks = rows // ch

    def body(ag_ref, av_ref, wg_ref, wv_ref, bg_ref, bv_ref, o_ref):
        for lt in range(tc // LANES):
            lanes = slice(lt * LANES, (lt + 1) * LANES)
            wg, wv, bg, bv = wg_ref[:, lanes], wv_ref[:, lanes], bg_ref[:, lanes], bv_ref[:, lanes]

            def chunk(ci, carry):
                r0 = pl.multiple_of(ci * ch, ch)
                gate = _conv3(_taps(_with_halo(ag_ref, r0, ci, n_chunks, ch, lanes, True, False)), wg, bg)
                val = _conv3(_taps(_with_halo(av_ref, r0, ci, n_chunks, ch, lanes, True, False)), wv, bv)
                o_ref[pl.ds(r0, ch), lanes] = (gate * jax.nn.sigmoid(gate) * val).astype(o_ref.dtype)
                return carry

            lax.fori_loop(0, n_chunks, chunk, 0)

    return pl.pallas_call(
        body, name="conv_gate_fwd", grid=(nc,),
        in_specs=[pl.BlockSpec((rows, tc), lambda j: (0, j)), pl.BlockSpec((rows, tc), lambda j: (0, j + nc)),
                  pl.BlockSpec((SUBLANES, tc), lambda j: (0, j)), pl.BlockSpec((SUBLANES, tc), lambda j: (0, j + nc)),
                  pl.BlockSpec((1, tc), lambda j: (0, j)), pl.BlockSpec((1, tc), lambda j: (0, j + nc))],
        out_specs=pl.BlockSpec((rows, tc), lambda j: (0, j)),
        out_shape=jax.ShapeDtypeStruct((rows, f), BF16), compiler_params=_params(("parallel",)),
    )(a, a, conv_w, conv_w, conv_b, conv_b)


def _conv_gate_bwd(a, conv_w, conv_b, dg, *, tc=512):
    rows, f2 = a.shape
    f = f2 // 2
    tc = _tile(f, tc)
    nc = f // tc
    ch = _tile(rows, CONV_ROWS, SUBLANES)
    n_chunks = rows // ch
    ext_rows = ch + SUBLANES

    def fold(x):
        return jnp.sum(x.reshape(ch // SUBLANES, SUBLANES, LANES), axis=0)

    def body(ag_ref, av_ref, wg_ref, wv_ref, bg_ref, bv_ref, dg_ref, da_ref, dw_ref, db_ref):
        for lt in range(tc // LANES):
            lanes = slice(lt * LANES, (lt + 1) * LANES)
            wg, wv, bg, bv = wg_ref[:, lanes], wv_ref[:, lanes], bg_ref[:, lanes], bv_ref[:, lanes]

            def chunk(ci, acc):
                r0 = pl.multiple_of(ci * ch, ch)
                taps_g = _taps(_with_halo(ag_ref, r0, ci, n_chunks, ch, lanes, True, True))
                taps_v = _taps(_with_halo(av_ref, r0, ci, n_chunks, ch, lanes, True, True))
                dge = _with_halo(dg_ref, r0, ci, n_chunks, ch, lanes, False, True)
                gate, val = _conv3(taps_g, wg, bg), _conv3(taps_v, wv, bv)
                sg = jax.nn.sigmoid(gate)
                d_gate = dge * val * sg * (1.0 + gate * (1.0 - sg))
                d_val = dge * gate * sg
                new = []
                for half, (taps, w, d) in enumerate(((taps_g, wg, d_gate), (taps_v, wv, d_val))):
                    da = (w[2:3, :] * d[:ch] + w[1:2, :] * pltpu.roll(d, ext_rows - 1, 0)[:ch]
                          + w[0:1, :] * pltpu.roll(d, ext_rows - 2, 0)[:ch])
                    da_ref[half, pl.ds(r0, ch), lanes] = da.astype(da_ref.dtype)
                    dc = d[:ch]
                    sums = [fold(dc)] + [fold(dc * t[:ch]) for t in taps]
                    new.append(tuple(x + s for x, s in zip(acc[half], sums)))
                return tuple(new)

            zero = tuple(jnp.zeros((SUBLANES, LANES), F32) for _ in range(4))
            acc = lax.fori_loop(0, n_chunks, chunk, (zero, zero))
            row = lax.broadcasted_iota(jnp.int32, (SUBLANES, LANES), 0)
            for half in range(2):
                db, *taps = (jnp.sum(x, axis=0, keepdims=True) for x in acc[half])
                db_ref[half, :, lanes] = db
                dw = jnp.zeros((SUBLANES, LANES), F32)
                for tap in range(3):
                    dw = jnp.where(row == tap, taps[tap], dw)
                dw_ref[half, :, lanes] = dw

    lo = lambda j: (0, j)
    hi = lambda j: (0, j + nc)
    both = lambda j: (0, 0, j)
    return pl.pallas_call(
        body, name="conv_gate_bwd", grid=(nc,),
        in_specs=[pl.BlockSpec((rows, tc), lo), pl.BlockSpec((rows, tc), hi), pl.BlockSpec((SUBLANES, tc), lo),
                  pl.BlockSpec((SUBLANES, tc), hi), pl.BlockSpec((1, tc), lo), pl.BlockSpec((1, tc), hi),
                  pl.BlockSpec((rows, tc), lo)],
        out_specs=[pl.BlockSpec((2, rows, tc), both), pl.BlockSpec((2, SUBLANES, tc), both), pl.BlockSpec((2, 1, tc), both)],
        out_shape=[jax.ShapeDtypeStruct((2, rows, f), BF16), jax.ShapeDtypeStruct((2, SUBLANES, f), F32),
                   jax.ShapeDtypeStruct((2, 1, f), F32)],
        compiler_params=_params(("parallel",)),
    )(a, a, conv_w, conv_w, conv_b, conv_b, dg)


def _wgrad(a, b, rows, cols, row_sharded, name, **kw):
    return functools.partial(_wgrad_half, a, b, rows, cols, row_sharded, name, **kw)


class _NoExchange:
    def __init__(self, later, ffn):
        self.later, self.ffn = later, ffn

    def mixer_weights(self, after):
        return self.later

    def ffn_weights_arrived(self, after):
        return None

    def ffn_weights(self, after):
        return self.ffn

    def ffn_down_arrived(self, after):
        return None

    def ffn_down_weight(self, after):
        return self.ffn["ffn_w_down"]

    def ffn_grads(self, makers, after):
        self.ffn_makers = makers
        return None

    def ffn_backward_done(self, after):
        return None


def _local_step(x, posf, target, w, hooks):
    rows, d = x.shape
    width = w["ssm_d"].shape[1]
    qr, kvr = w["mla_q_norm_w"].shape[1], w["mla_kv_norm_w"].shape[1]
    heads = w["mla_w_ukv"].shape[1] // HEAD_SLOT
    f2 = w["ffn_conv_b"].shape[1]
    inp = w["w_in"].shape[0]
    scale = (QK_NOPE_DIM + QK_ROPE_DIM) ** -0.5
    g = {}

    hn = _rmsnorm_fwd(x, w["attn_norm_w"], name="attn_norm")
    proj = _matmul(hn, w["w_in"], mode="nt", name="in_proj")

    s5_weights = (w["ssm_lambda_re"], w["ssm_lambda_im"], w["ssm_log_dt"], w["ssm_b_re"], w["ssm_b_im"])
    wb, wct, abar = _s5_bands(*s5_weights, w["ssm_c_re"], w["ssm_c_im"])
    states, y_pre, yg = _s5_fwd(proj, wb, wct, w["ssm_d"], abar)
    later = hooks.mixer_weights(yg)
    z = _matmul(yg, later["ssm_w_glu"], mode="nn", name="glu_proj", bias=w["ssm_b_glu"])
    ys = _glu_norm_fwd(y_pre, z, w["ssm_out_norm_w"])

    q_col, kv_col, kpe_col = width // qr, (width + qr) // kvr, (width + qr + kvr) // LANES
    assert width % qr == 0 and (width + qr) % kvr == 0
    qn = _rmsnorm_fwd(proj, w["mla_q_norm_w"], name="q_norm", width=qr, col=q_col)
    kvn = _rmsnorm_fwd(proj, w["mla_kv_norm_w"], name="kv_norm", width=kvr, col=kv_col)
    q = _matmul(qn, w["mla_w_uq"], mode="nn", name="q_proj")
    kv = _matmul(kvn, w["mla_w_ukv"], mode="nn", name="kv_proj")
    half = QK_ROPE_DIM // 2
    inv_freq = ROPE_THETA ** (-jnp.arange(0, QK_ROPE_DIM, 2, dtype=F32) / QK_ROPE_DIM)
    zeros = jnp.zeros((LANES - QK_ROPE_DIM,), F32)
    freq = jnp.concatenate([inv_freq, inv_freq, zeros]).reshape(1, LANES)
    sign = jnp.concatenate([-jnp.ones((half,), F32), jnp.ones((half,), F32), zeros]).reshape(1, LANES)
    cos, sin = _rope_tables(posf, freq, sign)
    qc, kc, vb = _attn_prep(q, kv, proj, kpe_col, cos, sin)
    o, lse = _attn_fwd(qc, kc, vb, scale=scale, tq=ATTN_BLOCK)
    ym = _rmsnorm_fwd(o, w["mla_out_norm_w"], name="mla_out_norm")
    ycat = jnp.concatenate([ys, ym], axis=1)
    h1 = _matmul(ycat, later["w_out"], mode="nn", name="out_proj", add=x, after=hooks.ffn_weights_arrived(ycat))

    hn2 = _rmsnorm_fwd(h1, w["ffn_norm_w"], name="ffn_norm")
    ffn = hooks.ffn_weights(hn2)
    a = _matmul(hn2, ffn["ffn_w_up"], mode="nn", name="ffn_up", tm=FFN_ROWS)
    started = hooks.ffn_down_arrived(a)
    conv_b = w["ffn_conv_b"] if started is None else w["ffn_conv_b"] + started[:1, :1]
    gated = _conv_gate_fwd(a, ffn["ffn_conv_w"], conv_b)
    w_down = hooks.ffn_down_weight(gated)
    h2 = _matmul(gated, w_down, mode="nn", name="ffn_down", add=h1, tk=2816, tm=FFN_ROWS)
    loss_tile, dh2, dh2_mxu, g["final_norm_w"] = _final_norm_loss(h2, w["final_norm_w"], target)

    dgated = _matmul(dh2_mxu, w_down, mode="nt", name="ffn_down_dx", tm=FFN_ROWS)
    da, dcw, dcb = _conv_gate_bwd(a, ffn["ffn_conv_w"], w["ffn_conv_b"], dgated)
    g["ffn_conv_w"] = jnp.concatenate([dcw[0, :3], dcw[1, :3]], axis=1)
    g["ffn_conv_b"] = jnp.concatenate([dcb[0], dcb[1]], axis=1)
    started = hooks.ffn_grads({
        "ffn_w_up": _wgrad(hn2, da, d, f2, False, "ffn_up_dw", b_split=True, tm=FFN_ROWS, tn=_tile(f2 // N_CHIPS, 1408)),
        "ffn_w_down": _wgrad(gated, dh2_mxu, f2 // 2, d, True, "ffn_down_dw", tm=f2 // 2 // N_CHIPS, tn=1024)}, dcb)
    dhn2 = _matmul(da, ffn["ffn_w_up"], mode="nt", name="ffn_up_dx", a_split=True, tk=_tile(f2 // 2, 2816), tm=FFN_ROWS,
                   after=started)
    dh1, dh1_mxu, g["ffn_norm_w"] = _rmsnorm_bwd(h1, w["ffn_norm_w"], dhn2, name="ffn_norm_bwd", add=dh2,
                                                dx_dtypes=(F32, BF16))

    dycat = _matmul(dh1_mxu, later["w_out"], mode="nt", name="out_proj_dx")
    g["w_out"] = _wgrad(ycat, dh1_mxu, 2 * width, d, True, "out_proj_dw")
    started = hooks.ffn_backward_done(dycat)
    mla_out_norm_w, ssm_out_norm_w = w["mla_out_norm_w"], w["ssm_out_norm_w"]
    if started is not None:
        mla_out_norm_w, ssm_out_norm_w = mla_out_norm_w + started[:1, :1], ssm_out_norm_w + started[:1, :1]

    do, g["mla_out_norm_w"] = _rmsnorm_bwd(o, mla_out_norm_w, dycat, name="mla_out_norm_bwd", width=width, dy_col=1)
    dq, dkv, dkpe_heads = _attn_bwd(qc, kc, vb, o, do, lse, cos, sin, scale=scale, tk=ATTN_BLOCK)
    dkpe = _kpe_bwd(dkpe_heads, cos, sin)
    g["mla_w_uq"] = _wgrad(qn, dq, qr, heads * HEAD_SLOT, False, "q_proj_dw")
    dqn = _matmul(dq, w["mla_w_uq"], mode="nt", name="q_proj_dx")
    dcq, g["mla_q_norm_w"] = _rmsnorm_bwd(proj, w["mla_q_norm_w"], dqn, name="q_norm_bwd", width=qr, col=q_col,
                                          dx_dtypes=(BF16,))
    g["mla_w_ukv"] = _wgrad(kvn, dkv, kvr, heads * HEAD_SLOT, False, "kv_proj_dw")
    dkvn = _matmul(dkv, w["mla_w_ukv"], mode="nt", name="kv_proj_dx")
    dckv, g["mla_kv_norm_w"] = _rmsnorm_bwd(proj, w["mla_kv_norm_w"], dkvn, name="kv_norm_bwd", width=kvr, col=kv_col,
                                            dx_dtypes=(BF16,))

    dz, dyg_a, g["ssm_out_norm_w"], g["ssm_b_glu"] = _glu_norm_bwd(y_pre, z, ssm_out_norm_w, dycat)
    dyg_b = _matmul(dz, later["ssm_w_glu"], mode="nt", name="glu_proj_dx")
    g["ssm_w_glu"] = _wgrad(yg, dz, width, width, True, "glu_proj_dw")
    du, dwb, dwct, dabar, g["ssm_d"] = _s5_bwd(proj, states, y_pre, dyg_a, dyg_b, wb, wct, w["ssm_d"], abar)
    (g["ssm_lambda_re"], g["ssm_lambda_im"], g["ssm_log_dt"], g["ssm_b_re"], g["ssm_b_im"], g["ssm_c_re"],
     g["ssm_c_im"]) = _s5_bands_bwd(*s5_weights, dwb, dwct, dabar)

    pad = jnp.zeros((rows, inp - (width + qr + kvr + LANES)), BF16)
    dproj = jnp.concatenate([du, dcq, dckv, dkpe, pad], axis=1)
    g["w_in"] = _wgrad(dproj, hn, inp, d, False, "in_proj_dw")
    dhn = _matmul(dproj, w["w_in"], mode="nn", name="in_proj_dx")
    dx, g["attn_norm_w"] = _rmsnorm_bwd(x, w["attn_norm_w"], dhn, name="attn_norm_bwd", add=dh1)
    return loss_tile, dx, g


ANY = pl.BlockSpec(memory_space=pl.ANY)
MESH = pl.DeviceIdType.MESH


def _mesh_pos():
    return lax.axis_index("x"), lax.axis_index("y"), lax.axis_index("c")


def _other_chips(x, y):
    return [(1 - x, y), (x, 1 - y), (1 - x, 1 - y)]


def _remote(src, dst, send_sems, recv_sems, k, to):
    return pltpu.make_async_remote_copy(src_ref=src, dst_ref=dst, send_sem=send_sems.at[k], recv_sem=recv_sems.at[k],
                                        device_id=to, device_id_type=MESH)


def _place_shard(shard, piece_idx, row_sharded, name, out_dtype=BF16, pieces=N_CHIPS, after=None):
    rs, cs = shard.shape
    tr = _tile(rs, 512, 2 * SUBLANES)
    rb = rs // tr
    extra = [] if after is None else [after]

    def body(p_ref, x_ref, *rest):
        o_ref = rest[-1]
        o_ref[...] = x_ref[...].astype(o_ref.dtype)

    if row_sharded:
        out_shape, out_map = (pieces * rs, cs), (lambda i, p_ref: (p_ref[0] * rb + i, 0))
    else:
        out_shape, out_map = (rs, pieces * cs), (lambda i, p_ref: (i, p_ref[0]))
    return pl.pallas_call(
        body, name=name, out_shape=jax.ShapeDtypeStruct(out_shape, out_dtype),
        grid_spec=pltpu.PrefetchScalarGridSpec(
            num_scalar_prefetch=1, grid=(rb,),
            in_specs=[pl.BlockSpec((tr, cs), lambda i, p_ref: (i, 0))] + [pl.BlockSpec(memory_space=pl.ANY)] * len(extra),
            out_specs=pl.BlockSpec((tr, cs), out_map)),
        compiler_params=_params(("parallel",)),
    )(piece_idx, shard, *extra)


def _gather_weights(placed, name):
    n = len(placed)
    meta = [(row_sharded, direct) for _, row_sharded, direct in placed]
    over_ici, over_d2d = _gather_plans(meta)
    forwarded = [t for t, (_, direct) in enumerate(meta) if not direct]

    def body(*refs):
        outs = refs[n:2 * n]
        send_sems, recv_sems, pass_send_sems, pass_recv_sems = refs[2 * n:]
        first, arrivals = over_ici(outs, send_sems, recv_sems)
        passed, passed_arrivals = over_d2d([outs[t] for t in forwarded], pass_send_sems, pass_recv_sems)
        for cp in first:
            cp.start()
        for t in range(n):
            for j in range(3):
                arrivals[3 * t + j].wait_recv()
                if t in forwarded:
                    passed[3 * forwarded.index(t) + j].start()
        for cp in passed_arrivals:
            cp.wait_recv()
        for cp in first + passed:
            cp.wait_send()

    return pl.pallas_call(
        body, name=name, in_specs=[ANY] * n, out_specs=[ANY] * n,
        out_shape=[jax.ShapeDtypeStruct(arr.shape, arr.dtype) for arr, _, _ in placed],
        input_output_aliases={t: t for t in range(n)},
        scratch_shapes=[pltpu.SemaphoreType.DMA((3 * n,)), pltpu.SemaphoreType.DMA((3 * n,)),
                        pltpu.SemaphoreType.DMA((3 * len(forwarded),)), pltpu.SemaphoreType.DMA((3 * len(forwarded),))],
    )(*[arr for arr, _, _ in placed])


def _gather_plans(meta):
    def window(ref, row_sharded, piece, half):
        r, cc = ref.shape
        if row_sharded:
            rs = r // N_CHIPS
            if half is None:
                return ref.at[pl.ds(piece * rs, rs), :]
            return ref.at[pl.ds(piece * rs + half * (rs // 2), rs // 2), :]
        cs = cc // N_CHIPS
        if half is None:
            return ref.at[:, pl.ds(piece * cs, cs)]
        return ref.at[pl.ds(half * (r // 2), r // 2), pl.ds(piece * cs, cs)]

    def over_ici(refs, send_sems, recv_sems):
        x, y, c = _mesh_pos()
        sends, recvs = [], []
        for t, (row_sharded, direct) in enumerate(meta):
            mine = window(refs[t], row_sharded, 2 * x + y, None if direct else c)
            for j, (px, py) in enumerate(_other_chips(x, y)):
                theirs = window(refs[t], row_sharded, 2 * px + py, None if direct else c)
                sends.append(_remote(mine, mine, send_sems, recv_sems, 3 * t + j, (px, py, c)))
                recvs.append(_remote(theirs, theirs, send_sems, recv_sems, 3 * t + j, (px, py, c)))
        return sends, recvs

    def over_d2d(refs, send_sems, recv_sems):
        x, y, c = _mesh_pos()
        sends, recvs = [], []
        rows = [row_sharded for row_sharded, direct in meta if not direct]
        for t, row_sharded in enumerate(rows):
            for j, (px, py) in enumerate(_other_chips(x, y)):
                got = window(refs[t], row_sharded, 2 * px + py, c)
                other = window(refs[t], row_sharded, 2 * px + py, 1 - c)
                sends.append(_remote(got, got, send_sems, recv_sems, 3 * t + j, (x, y, 1 - c)))
                recvs.append(_remote(other, other, send_sems, recv_sems, 3 * t + j, (x, y, 1 - c)))
        return sends, recvs

    return over_ici, over_d2d


HBM = pl.BlockSpec(memory_space=pltpu.HBM)
SEMAPHORES = pl.BlockSpec(memory_space=pltpu.SEMAPHORE)
DATAFLOW = pltpu.SideEffectType.DATAFLOW_SIDE_EFFECTING


def _start_copies(name, arrays, plan, n_copies, after):
    n = len(arrays)

    def body(*refs):
        sends, _ = plan(refs[:n], refs[n + 1], refs[n + 2])
        for cp in sends:
            cp.start()
        token = refs[2 * n + 3]
        token[...] = jnp.zeros_like(token)

    out = pl.pallas_call(
        body, name=name,
        out_shape=(pltpu.SemaphoreType.DMA((n_copies,)), pltpu.SemaphoreType.DMA((n_copies,)),
                   *[pltpu.HBM(a.shape, a.dtype) for a in arrays], jax.ShapeDtypeStruct((SUBLANES, LANES), F32)),
        in_specs=[HBM] * n + [ANY],
        out_specs=(SEMAPHORES, SEMAPHORES, *[HBM] * n, pl.BlockSpec(memory_space=pltpu.VMEM)),
        input_output_aliases={t: t + 2 for t in range(n)},
        compiler_params=pltpu.CompilerParams(has_side_effects=DATAFLOW),
    )(*[pltpu.with_memory_space_constraint(a, pltpu.HBM) for a in arrays], after)
    return out[0], out[1], list(out[2:2 + n]), out[2 + n]


def _wait_copies(name, started, plan, after):
    send_sems, recv_sems, arrays, _ = started
    n = len(arrays)

    def body(*refs):
        sends, recvs = plan(refs[:n], refs[n], refs[n + 1])
        for cp in sends:
            cp.wait_send()
        for cp in recvs:
            cp.wait_recv()

    out = pl.pallas_call(
        body, name=name, out_shape=[pltpu.HBM(a.shape, a.dtype) for a in arrays],
        in_specs=[HBM] * n + [SEMAPHORES, SEMAPHORES, ANY], out_specs=[HBM] * n,
        input_output_aliases={t: t for t in range(n)},
        compiler_params=pltpu.CompilerParams(has_side_effects=DATAFLOW),
    )(*arrays, send_sems, recv_sems, after)
    return list(out)


def _exchange(name, arrays, plan, n_copies, after=None):
    n = len(arrays)
    extra = [] if after is None else [after]

    def body(*refs):
        outs = refs[n + len(extra):2 * n + len(extra)]
        send_sems, recv_sems = refs[2 * n + len(extra):]
        sends, recvs = plan(outs, send_sems, recv_sems)
        for cp in sends:
            cp.start()
        for cp in recvs:
            cp.wait_recv()
        for cp in sends:
            cp.wait_send()

    return pl.pallas_call(
        body, name=name, in_specs=[ANY] * (n + len(extra)), out_specs=[ANY] * n,
        out_shape=[jax.ShapeDtypeStruct(a.shape, a.dtype) for a in arrays],
        input_output_aliases={t: t for t in range(n)},
        scratch_shapes=[pltpu.SemaphoreType.DMA((n_copies,)), pltpu.SemaphoreType.DMA((n_copies,))],
    )(*arrays, *extra)


def _give_plan(n):
    def plan(refs, send_sems, recv_sems):
        x, y, c = _mesh_pos()
        sends = [_remote(refs[t], refs[n + t], send_sems, recv_sems, t, (x, y, 1 - c)) for t in range(n)]
        return sends, sends

    return plan


def _scatter_plan(n):
    def plan(refs, send_sems, recv_sems):
        x, y, c = _mesh_pos()
        sends = []
        for t in range(n):
            for j, (px, py) in enumerate(_other_chips(x, y)):
                sends.append(_remote(refs[t].at[2 * px + py], refs[n + t].at[j], send_sems, recv_sems, 3 * t + j, (px, py, c)))
        return sends, sends

    return plan


def _scatter_shapes(sums):
    return [jax.ShapeDtypeStruct((3,) + s.shape[1:], s.dtype) for s in sums]


def _join_plan(n):
    def plan(refs, send_sems, recv_sems):
        x, y, c = _mesh_pos()
        sends = [_remote(refs[t].at[c], refs[t].at[c], send_sems, recv_sems, t, (x, y, 1 - c)) for t in range(n)]
        recvs = [_remote(refs[t].at[1 - c], refs[t].at[1 - c], send_sems, recv_sems, t, (x, y, 1 - c)) for t in range(n)]
        return sends, recvs

    return plan


def _join_halves(halves, name, after=None):
    return _exchange(name, halves, _join_plan(len(halves)), len(halves), after=after)


def _add_other_half(g4, got, where, name):
    _, pieces, sr, sc = g4.shape
    tr = _tile(sr, 512, 2 * SUBLANES)

    def body(w_ref, a_ref, b_ref, o_ref):
        o_ref[...] = a_ref[...] + b_ref[...]

    blk = pl.BlockSpec((None, tr, sc), lambda p, i, w_ref: (p, i, 0))
    return pl.pallas_call(
        body, name=name, out_shape=jax.ShapeDtypeStruct((pieces, sr, sc), F32),
        grid_spec=pltpu.PrefetchScalarGridSpec(
            num_scalar_prefetch=1, grid=(pieces, sr // tr),
            in_specs=[pl.BlockSpec((None, None, tr, sc), lambda p, i, w_ref: (w_ref[0], p, i, 0)), blk], out_specs=blk),
        compiler_params=_params(("parallel", "parallel")),
    )(where, g4, got)


def _add_pieces(sums, got_pieces, where, name, after=None):
    _, sr, sc = sums.shape
    tr = _tile(sr, 512, 2 * SUBLANES)
    extra = [] if after is None else [after]

    def body(w_ref, a_ref, r_ref, *rest):
        acc = a_ref[...]
        for j in range(3):
            acc = acc + r_ref[j].astype(F32)
        rest[-1][...] = acc

    return pl.pallas_call(
        body, name=name, out_shape=jax.ShapeDtypeStruct((N_CORES, sr, sc), F32),
        grid_spec=pltpu.PrefetchScalarGridSpec(
            num_scalar_prefetch=1, grid=(sr // tr,),
            in_specs=[pl.BlockSpec((None, tr, sc), lambda i, w_ref: (w_ref[1], i, 0)),
                      pl.BlockSpec((3, tr, sc), lambda i, w_ref: (0, i, 0))] + [pl.BlockSpec(memory_space=pl.ANY)] * len(extra),
            out_specs=pl.BlockSpec((None, tr, sc), lambda i, w_ref: (w_ref[0], i, 0))),
        compiler_params=_params(("parallel",)),
    )(where, sums, got_pieces, *extra)


def _adamw_update(w, g, m, v):
    nm = ADAM_B1 * m + (1.0 - ADAM_B1) * g
    nv = ADAM_B2 * v + (1.0 - ADAM_B2) * (g * g)
    m_hat = nm / (1.0 - ADAM_B1 ** ADAM_STEP)
    v_hat = nv / (1.0 - ADAM_B2 ** ADAM_STEP)
    return -ADAM_LR * (m_hat / (jnp.sqrt(v_hat) + ADAM_EPS) + ADAM_WD * w), nm, nv


def _adamw(w, g, m, v, name, after=None):
    rows, cols = w.shape
    halves = 2 if g.ndim == 3 else 1
    bc = cols // halves
    tr = _tile(rows, max(SUBLANES, (1 << 20) // max(bc, 1) // SUBLANES * SUBLANES), SUBLANES)

    def body(w_ref, g_ref, m_ref, v_ref, *rest):
        d_ref, nm_ref, nv_ref, go_ref = rest[-4:]
        gv = g_ref[...]
        d_ref[...], nm_ref[...], nv_ref[...] = _adamw_update(w_ref[...], gv, m_ref[...], v_ref[...])
        go_ref[...] = gv

    blk = pl.BlockSpec((tr, bc), lambda i, h: (i, h))
    g_blk = pl.BlockSpec((None, tr, bc), lambda i, h: (h, i, 0)) if halves == 2 else blk
    extra = [] if after is None else [after]
    return pl.pallas_call(
        body, name=name, grid=(rows // tr, halves),
        in_specs=[blk, g_blk, blk, blk] + [pl.BlockSpec(memory_space=pl.ANY)] * len(extra), out_specs=[blk] * 4,
        out_shape=[jax.ShapeDtypeStruct((rows, cols), F32)] * 4, compiler_params=_params(("parallel", "parallel")),
    )(w, g, m, v, *extra)


def _adamw_many(ws, gs, ms, vs, name):
    n = len(ws)

    def body(*refs):
        outs = refs[4 * n:]
        for k in range(n):
            w_ref, g_ref, m_ref, v_ref = (refs[j * n + k] for j in range(4))
            outs[k][...], outs[n + k][...], outs[2 * n + k][...] = _adamw_update(w_ref[...], g_ref[...], m_ref[...], v_ref[...])

    out = pl.pallas_call(
        body, name=name, out_shape=[jax.ShapeDtypeStruct(w.shape, F32) for w in ws] * 3,
        compiler_params=pltpu.CompilerParams(vmem_limit_bytes=VMEM_LIMIT_BYTES),
    )(*ws, *gs, *ms, *vs)
    return out[:n], out[n:2 * n], out[2 * n:]


WEIGHTS = ['attn_norm_w', 'w_in', 'ssm_lambda_re', 'ssm_lambda_im', 'ssm_log_dt', 'ssm_b_re', 'ssm_b_im', 'ssm_c_re',
           'ssm_c_im', 'ssm_d', 'ssm_w_glu', 'ssm_b_glu', 'mla_q_norm_w', 'mla_w_uq', 'mla_kv_norm_w', 'mla_w_ukv',
           'ssm_out_norm_w', 'mla_out_norm_w', 'w_out', 'ffn_norm_w', 'ffn_w_up', 'ffn_conv_w', 'ffn_conv_b',
           'ffn_w_down', 'final_norm_w']
SHARDED = {'w_in': False, 'ssm_w_glu': True, 'mla_w_uq': False, 'mla_w_ukv': False, 'w_out': True, 'ffn_w_up': False,
           'ffn_w_down': True}
SMALL = [n for n in WEIGHTS if n not in SHARDED and n != 'ffn_conv_w']
ROPE_PAD = HEAD_SLOT - QK_NOPE_DIM - QK_ROPE_DIM
SMALL_COLS = 8 * LANES


def _pad_heads(w_uq, heads):
    qr = w_uq.shape[0]
    w3 = w_uq.reshape(qr, heads, QK_NOPE_DIM + QK_ROPE_DIM)
    return jnp.concatenate([w3, jnp.zeros((qr, heads, ROPE_PAD), w_uq.dtype)], axis=2).reshape(qr, heads * HEAD_SLOT)


def _unpad_heads(g_uq, heads):
    qr = g_uq.shape[0]
    return g_uq.reshape(qr, heads, HEAD_SLOT)[:, :, :QK_NOPE_DIM + QK_ROPE_DIM].reshape(qr, -1)


FFN = ['ffn_w_up', 'ffn_w_down']
MIXER_LATER = ['ssm_w_glu', 'w_out']
MIXER_BIG = ['w_in', 'w_out']
FFN_GATHER = FFN + ['ffn_conv_w']


class _Overlapped:
    def __init__(self, placed_first, first_sharding, where):
        self.where, self.mine, self.other = where, where[:1], 1 - where[:1]
        self.first_ici, self.first_d2d = _gather_plans([(r, False) for r in first_sharding])
        self.first = _start_copies("gather_first_start", placed_first, self.first_ici, 3 * len(placed_first), where)
        self.first_started = self.first[3]

    def start_rest(self, placed_later, placed):
        self.later_ici, self.later_d2d = _gather_plans([(SHARDED[n], False) for n in MIXER_LATER])
        self.later = _start_copies("gather_later_start", placed_later, self.later_ici, 3 * len(placed_later),
                                   self.first_started)
        up, down, taps = placed
        self.up_ici, self.up_d2d = _gather_plans([(SHARDED["ffn_w_up"], False), (False, True)])
        self.up = _start_copies("gather_ffn_up_start", [up, taps], self.up_ici, 6, self.later[3])
        self.down_ici, self.down_d2d = _gather_plans([(SHARDED["ffn_w_down"], False)])
        self.down = _start_copies("gather_ffn_down_start", [down], self.down_ici, 3, self.up[3])
        self.gather_started = self.down[3]
        arrived = _wait_copies("gather_first_wait", self.first, self.first_ici, self.gather_started)
        return _exchange("gather_first_pass", arrived, self.first_d2d, 3 * len(arrived))

    def mixer_weights(self, after):
        arrived = _wait_copies("gather_later_wait", self.later, self.later_ici, after)
        return dict(zip(MIXER_LATER, _exchange("gather_later_pass", arrived, self.later_d2d, 3 * len(arrived))))

    def ffn_weights_arrived(self, after):
        up, self.taps = _wait_copies("gather_ffn_up_wait", self.up, self.up_ici, after)
        self.up_passing = _start_copies("gather_ffn_up_pass_start", [up], self.up_d2d, 3, after)
        return self.up_passing[3]

    def ffn_weights(self, after):
        w_up, = _wait_copies("gather_ffn_up_pass_wait", self.up_passing, self.up_d2d, after)
        return {"ffn_w_up": w_up, "ffn_conv_w": self.taps}

    def ffn_down_arrived(self, after):
        down, = _wait_copies("gather_ffn_down_wait", self.down, self.down_ici, after)
        self.down_passing = _start_copies("gather_ffn_down_pass_start", [down], self.down_d2d, 3, after)
        return self.down_passing[3]

    def ffn_down_weight(self, after):
        return _wait_copies("gather_ffn_down_pass_wait", self.down_passing, self.down_d2d, after)[0]

    def ffn_grads(self, makers, after):
        self.makers = [makers[name] for name in FFN]
        n = len(FFN)
        give = [make(self.other, suffix="_give") for make in self.makers]
        lands = [lax.empty(g.shape, g.dtype) for g in give]
        self.swap = _start_copies("grad_ffn_swap_start", give + lands, _give_plan(n), n, after)
        return self.swap[3]

    def ffn_backward_done(self, after):
        n = len(FFN)
        got = _wait_copies("grad_ffn_swap_wait", self.swap, _give_plan(n), after)[n:]
        kept = [make(self.mine, suffix="_keep", add=got[t], wire=True) for t, make in enumerate(self.makers)]
        self.sums = [k[0] for k in kept]
        wires = [k[1] for k in kept]
        lands = [lax.empty(s.shape, s.dtype) for s in _scatter_shapes(wires)]
        self.scatter = _start_copies("grad_ffn_scatter_start", wires + lands, _scatter_plan(n), 3 * n, after)
        return self.scatter[3]

    def ffn_reduced(self, after):
        n = len(FFN)
        got_pieces = _wait_copies("grad_ffn_scatter_wait", self.scatter, _scatter_plan(n), after)[n:]
        halves = []
        for t, name in enumerate(FFN):
            halves.append(_add_pieces(self.sums[t], got_pieces[t], self.where, "grad_add_pieces_" + name,
                                      after=halves[-1] if halves else None))
        return halves


def _step(args):
    x, positions, target = args["x"][0], args["positions"], args["loss_target"][0]
    rows = x.shape[0]
    p = {n: args[n] for n in WEIGHTS}
    xi, yi, ci = _mesh_pos()
    piece = 2 * xi + yi

    def transposed(a):
        return jnp.swapaxes(a[0], 0, 1)

    def as_stored(n, a):
        return jnp.swapaxes(a, 2, 3) if n in ("ssm_b_re", "ssm_b_im") else a

    w_in = transposed(p["w_in"])
    in_width = w_in.shape[0]
    in_pad = (-in_width) % (2 * LANES)
    heads_here = p["mla_w_uq"].shape[2] // (QK_NOPE_DIM + QK_ROPE_DIM)
    shards = {
        "w_in": jnp.pad(w_in, ((0, in_pad), (0, 0))),
        "ssm_w_glu": p["ssm_w_glu"][0],
        "mla_w_uq": _pad_heads(p["mla_w_uq"][0], heads_here),
        "mla_w_ukv": p["mla_w_ukv"][0],
        "w_out": p["w_out"][0],
        "ffn_w_up": p["ffn_w_up"][0],
        "ffn_w_down": p["ffn_w_down"][0],
    }
    conv_w = jnp.pad(p["ffn_conv_w"][0], ((0, SUBLANES - p["ffn_conv_w"].shape[1]), (0, 0)))
    order = list(SHARDED)
    piece_idx = piece.reshape(1).astype(jnp.int32)
    mixer = [n for n in order if n not in FFN]
    first = [n for n in mixer if n not in MIXER_LATER]
    where = jnp.stack([ci, piece]).astype(jnp.int32)
    placed = {n: _place_shard(shards[n], piece_idx, SHARDED[n], "place_" + n) for n in first}
    hooks = _Overlapped([placed[n] for n in first], [SHARDED[n] for n in first], where)
    for n in order:
        if n not in first:
            placed[n] = _place_shard(shards[n], piece_idx, SHARDED[n], "place_" + n, after=hooks.first_started)
    placed["ffn_conv_w"] = _place_shard(conv_w, piece_idx, False, "place_ffn_conv_w", out_dtype=F32,
                                        after=hooks.first_started)
    w = dict(zip(first, hooks.start_rest([placed[n] for n in MIXER_LATER], [placed[n] for n in FFN_GATHER])))
    groups = p["ssm_lambda_re"].shape[1]
    w.update({
        "attn_norm_w": p["attn_norm_w"] + hooks.gather_started[:1, :1],
        "ssm_lambda_re": p["ssm_lambda_re"][0], "ssm_lambda_im": p["ssm_lambda_im"][0],
        "ssm_log_dt": p["ssm_log_dt"].reshape(groups, 1), "ssm_b_re": as_stored("ssm_b_re", p["ssm_b_re"])[0],
        "ssm_b_im": as_stored("ssm_b_im", p["ssm_b_im"])[0], "ssm_c_re": p["ssm_c_re"][0], "ssm_c_im": p["ssm_c_im"][0],
        "ssm_d": p["ssm_d"], "ssm_b_glu": p["ssm_b_glu"], "mla_q_norm_w": p["mla_q_norm_w"],
        "mla_kv_norm_w": p["mla_kv_norm_w"], "ssm_out_norm_w": p["ssm_out_norm_w"], "mla_out_norm_w": p["mla_out_norm_w"],
        "ffn_norm_w": p["ffn_norm_w"], "ffn_conv_b": p["ffn_conv_b"], "final_norm_w": p["final_norm_w"].reshape(1, -1),
    })

    loss_tile, dx, g = _local_step(x, positions.reshape(rows, 1).astype(F32), target, w, hooks)

    flat = [g[n].reshape(-1) for n in SMALL] + [g["ffn_conv_w"].reshape(-1), loss_tile[0, :1]]
    sizes = [f.shape[0] for f in flat]
    per_block = -(-sum(sizes) // (N_CORES * N_CHIPS * SMALL_COLS))
    small_rows = -(-per_block // (2 * SUBLANES)) * (2 * SUBLANES)
    padded = N_CORES * N_CHIPS * small_rows * SMALL_COLS

    def pack(parts):
        parts = list(parts)
        have = sum(q.shape[0] for q in parts)
        return jnp.concatenate(parts + [jnp.zeros((padded - have,), F32)])

    reduced = mixer + ["small"]
    small = pack(flat).reshape(N_CORES, N_CHIPS, small_rows, SMALL_COLS)
    give = [g[n](hooks.other, suffix="_give") for n in mixer] + [lax.dynamic_index_in_dim(small, 1 - ci, 0, keepdims=False)]
    lands = [lax.empty(a.shape, a.dtype) for a in give]
    give_plan = _give_plan(len(reduced))
    swap = _start_copies("grad_mixer_swap_start", give + lands, give_plan, len(reduced), dx)

    grads, delta, new_m, new_v = {}, {}, {}, {}

    def finish(n, joined, after=None):
        grad = joined if SHARDED[n] else joined.reshape(-1, joined.shape[2])
        if n == "w_in":
            wt, mt, vt = w_in, transposed(args["m_w_in"]), transposed(args["v_w_in"])
            out = _adamw(wt, grad, mt, vt, "adamw_w_in")
            delta[n], new_m[n], new_v[n], grads[n] = (jnp.swapaxes(a, 0, 1)[None] for a in out)
            return
        if n == "mla_w_uq":
            grad = _unpad_heads(grad, heads_here)
        adam(n, grad, after)

    def adam(n, grad, after=None):
        shape = p[n].shape
        out = _adamw(p[n].reshape(shape[1:]), grad, args["m_" + n].reshape(shape[1:]),
                     args["v_" + n].reshape(shape[1:]), "adamw_" + n, after)
        delta[n], new_m[n], new_v[n], grads[n] = (a.reshape(shape) for a in out)

    ffn_halves = hooks.ffn_reduced(swap[3])
    got = _wait_copies("grad_mixer_swap_wait", swap, give_plan, ffn_halves[-1])[len(reduced):]
    join_plan = _join_plan(len(FFN))
    ffn_join = _start_copies("grad_ffn_join_start", ffn_halves, join_plan, len(FFN), got[0])
    big = [t for t, n in enumerate(reduced) if n in MIXER_BIG]
    rest = [t for t in range(len(reduced)) if t not in big]
    sums, wires = {}, {}
    for t in big:
        sums[t], wires[t] = g[reduced[t]](hooks.mine, suffix="_keep", add=got[t], wire=True)
    ffn_joined = _wait_copies("grad_ffn_join_wait", ffn_join, join_plan, sums[big[-1]])

    def scatter_start(name, group, after):
        lands = [lax.empty(s.shape, s.dtype) for s in _scatter_shapes([wires[t] for t in group])]
        return _start_copies(name, [wires[t] for t in group] + lands, _scatter_plan(len(group)), 3 * len(group), after)

    scatter_big = scatter_start("grad_big_scatter_start", big, ffn_joined[0])
    for t in rest[:-1]:
        sums[t], wires[t] = g[reduced[t]](hooks.mine, suffix="_keep", add=got[t], wire=True, after=scatter_big[3])
    sums[rest[-1]] = wires[rest[-1]] = _add_other_half(small, got[-1], where, "grad_add_half_small")
    scatter_rest = scatter_start("grad_rest_scatter_start", rest, sums[rest[0]])
    behind = scatter_rest[3]
    for n, joined in zip(FFN, ffn_joined):
        finish(n, joined, after=behind)
        behind = delta[n]
    got_pieces = dict(zip(big, _wait_copies("grad_big_scatter_wait", scatter_big, _scatter_plan(len(big)),
                                            delta[FFN[-1]])[len(big):]))
    got_pieces.update(zip(rest, _wait_copies("grad_rest_scatter_wait", scatter_rest, _scatter_plan(len(rest)),
                                             got_pieces[big[0]])[len(rest):]))
    halves = [_add_pieces(sums[t], got_pieces[t], where, "grad_add_pieces_" + n) for t, n in enumerate(reduced)]
    joined = _join_halves(halves, "grad_join_halves")
    for n, j in zip(mixer, joined):
        finish(n, j)
    eighths = _place_shard(joined[-1].reshape(N_CORES * small_rows, SMALL_COLS), piece_idx, True, "place_small_grads",
                           out_dtype=F32)
    small_sum = _gather_weights([(eighths, True, False)], "gather_small_grads")[0]
    flat_sum = small_sum.reshape(N_CHIPS, N_CORES, small_rows * SMALL_COLS).transpose(1, 0, 2).reshape(-1)
    offs = [0]
    for s in sizes:
        offs.append(offs[-1] + s)
    stored = {n: as_stored(n, p[n]) for n in SMALL}
    for k, n in enumerate(SMALL):
        grads[n] = flat_sum[offs[k]:offs[k + 1]].reshape(stored[n].shape)
    taps, cols_here = p["ffn_conv_w"].shape[1], p["ffn_conv_w"].shape[2]
    conv_full = flat_sum[offs[len(SMALL)]:offs[len(SMALL) + 1]].reshape(taps, N_CHIPS * cols_here)
    adam("ffn_conv_w", lax.dynamic_slice_in_dim(conv_full, piece * cols_here, cols_here, axis=1))
    loss = flat_sum[offs[len(SMALL) + 1]]

    def rank2(a):
        return a.reshape(1, -1) if a.ndim == 1 else a

    d_s, m_s, v_s = _adamw_many([rank2(stored[n]) for n in SMALL], [rank2(grads[n]) for n in SMALL],
                                [rank2(as_stored(n, args["m_" + n])) for n in SMALL],
                                [rank2(as_stored(n, args["v_" + n])) for n in SMALL], "adamw_small")
    for k, n in enumerate(SMALL):
        delta[n], new_m[n], new_v[n], grads[n] = (as_stored(n, a.reshape(stored[n].shape))
                                                  for a in (d_s[k], m_s[k], v_s[k], grads[n]))

    return (loss, dx[None], *[grads[n] for n in WEIGHTS], *[delta[n] for n in WEIGHTS],
            *[new_m[n] for n in WEIGHTS], *[new_v[n] for n in WEIGHTS])


def kernel(x, positions, attn_norm_w, w_in, ssm_lambda_re, ssm_lambda_im, ssm_log_dt, ssm_b_re, ssm_b_im, ssm_c_re, ssm_c_im, ssm_d, ssm_w_glu, ssm_b_glu, mla_q_norm_w, mla_w_uq, mla_kv_norm_w, mla_w_ukv, ssm_out_norm_w, mla_out_norm_w, w_out, ffn_norm_w, ffn_w_up, ffn_conv_w, ffn_conv_b, ffn_w_down, final_norm_w, loss_target, m_attn_norm_w, m_w_in, m_ssm_lambda_re, m_ssm_lambda_im, m_ssm_log_dt, m_ssm_b_re, m_ssm_b_im, m_ssm_c_re, m_ssm_c_im, m_ssm_d, m_ssm_w_glu, m_ssm_b_glu, m_mla_q_norm_w, m_mla_w_uq, m_mla_kv_norm_w, m_mla_w_ukv, m_ssm_out_norm_w, m_mla_out_norm_w, m_w_out, m_ffn_norm_w, m_ffn_w_up, m_ffn_conv_w, m_ffn_conv_b, m_ffn_w_down, m_final_norm_w, v_attn_norm_w, v_w_in, v_ssm_lambda_re, v_ssm_lambda_im, v_ssm_log_dt, v_ssm_b_re, v_ssm_b_im, v_ssm_c_re, v_ssm_c_im, v_ssm_d, v_ssm_w_glu, v_ssm_b_glu, v_mla_q_norm_w, v_mla_w_uq, v_mla_kv_norm_w, v_mla_w_ukv, v_ssm_out_norm_w, v_mla_out_norm_w, v_w_out, v_ffn_norm_w, v_ffn_w_up, v_ffn_conv_w, v_ffn_conv_b, v_ffn_w_down, v_final_norm_w):
    return _step(dict(locals()))
```

```python
import functools
import math

import jax
import jax.numpy as jnp
from jax import lax
from jax.experimental import pallas as pl
from jax.experimental.pallas import tpu as pltpu

F32 = jnp.float32
BF16 = jnp.bfloat16

SSM_GROUP = 16
SSM_STATE = 64
QK_NOPE_DIM = 128
QK_ROPE_DIM = 64
ROPE_THETA = 10000.0
RMS_EPS = 1e-6
ADAM_LR, ADAM_B1, ADAM_B2, ADAM_EPS, ADAM_WD, ADAM_STEP = 0.001, 0.9, 0.999, 1e-08, 0.01, 10

LANES = 128
SUBLANES = 8
VMEM_LIMIT_BYTES = 56 * 1024 * 1024

GROUPS_PER_BATCH = LANES // SSM_GROUP
STATE_PER_BATCH = GROUPS_PER_BATCH * SSM_STATE
HEAD_SLOT = 2 * LANES
NEG_INF = -1e30
ATTN_BLOCK = 1024
FFN_ROWS = 1024

N_CHIPS = 4
N_CORES = 2


def _tile(n, pref, align=LANES):
    if n <= pref:
        return n
    t = (pref // align) * align
    while t >= align:
        if n % t == 0:
            return t
        t -= align
    return n


def _params(sem):
    return pltpu.CompilerParams(dimension_semantics=sem, vmem_limit_bytes=VMEM_LIMIT_BYTES)


def _dot(a, b, dims):
    return lax.dot_general(a, b, (dims, ((), ())), preferred_element_type=F32)


def _dot_nn(a, b):
    return _dot(a, b, ((1,), (0,)))


def _dot_nt(a, b):
    return _dot(a, b, ((1,), (1,)))


def _dot_tn(a, b):
    return _dot(a, b, ((0,), (0,)))


def _matmul(a, b, *, mode, name, tm=1024, tn=1024, tk=2048, bias=None, add=None, out_dtype=F32,
            a_split=False, b_split=False, after=None):
    if a_split:
        assert mode == "nt"
        a_shape = (a.shape[1], 2 * a.shape[2])
    else:
        a_shape = a.shape
    if b_split:
        assert mode == "tn"
        b_shape = (b.shape[1], 2 * b.shape[2])
    else:
        b_shape = b.shape
    if mode == "nn":
        (m, k), (k2, n) = a_shape, b_shape
    elif mode == "nt":
        (m, k), (n, k2) = a_shape, b_shape
    else:
        (k, m), (k2, n) = a_shape, b_shape
    assert k == k2, (a.shape, b.shape, mode)
    tm, tn, tk = _tile(m, tm, SUBLANES), _tile(n, tn), _tile(k, tk)
    nk = k // tk
    a_spec = {"nn": pl.BlockSpec((tm, tk), lambda i, j, kk: (i, kk)),
              "nt": pl.BlockSpec((tm, tk), lambda i, j, kk: (i, kk)),
              "tn": pl.BlockSpec((tk, tm), lambda i, j, kk: (kk, i))}[mode]
    b_spec = {"nn": pl.BlockSpec((tk, tn), lambda i, j, kk: (kk, j)),
              "nt": pl.BlockSpec((tn, tk), lambda i, j, kk: (j, kk)),
              "tn": pl.BlockSpec((tk, tn), lambda i, j, kk: (kk, j))}[mode]
    if a_split:
        kb = a.shape[2] // tk
        assert a.shape[2] % tk == 0
        a_spec = pl.BlockSpec((None, tm, tk), lambda i, j, kk: (kk // kb, i, kk % kb))
    if b_split:
        nb = b.shape[2] // tn
        assert b.shape[2] % tn == 0
        b_spec = pl.BlockSpec((None, tk, tn), lambda i, j, kk: (j // nb, kk, j % nb))
    dot = {"nn": _dot_nn, "nt": _dot_nt, "tn": _dot_tn}[mode]
    in_specs, operands = [a_spec, b_spec], [a, b]
    if bias is not None:
        in_specs.append(pl.BlockSpec((1, tn), lambda i, j, kk: (0, j)))
        operands.append(bias)
    if add is not None:
        in_specs.append(pl.BlockSpec((tm, tn), lambda i, j, kk: (i, j)))
        operands.append(add)
    if after is not None:
        in_specs.append(pl.BlockSpec(memory_space=pl.ANY))
        operands.append(after)

    def body(*refs):
        a_ref, b_ref = refs[0], refs[1]
        rest = list(refs[2:])
        bias_ref = rest.pop(0) if bias is not None else None
        add_ref = rest.pop(0) if add is not None else None
        if after is not None:
            rest.pop(0)
        o_ref, acc_ref = rest

        def finish(acc):
            if bias_ref is not None:
                acc = acc + bias_ref[...]
            if add_ref is not None:
                acc = acc + add_ref[...]
            o_ref[...] = acc.astype(o_ref.dtype)

        part = dot(a_ref[...].astype(BF16), b_ref[...].astype(BF16))
        if nk == 1:
            finish(part)
        else:
            kk = pl.program_id(2)

            @pl.when(kk == 0)
            def _():
                acc_ref[...] = part

            @pl.when(jnp.logical_and(kk > 0, kk < nk - 1))
            def _():
                acc_ref[...] += part

            @pl.when(kk == nk - 1)
            def _():
                finish(acc_ref[...] + part)

    out_shape = jax.ShapeDtypeStruct((m, n), out_dtype)
    out_spec = pl.BlockSpec((tm, tn), lambda i, j, kk: (i, j))
    acc_shape = (tm, tn) if nk > 1 else (SUBLANES, LANES)
    return pl.pallas_call(
        body, name=name, grid=(m // tm, n // tn, nk), in_specs=in_specs, out_specs=out_spec, out_shape=out_shape,
        scratch_shapes=[pltpu.VMEM(acc_shape, F32)],
        compiler_params=_params(("parallel", "parallel", "arbitrary")),
    )(*operands)


def _wgrad_half(a, b, rows, cols, row_sharded, name, which, *, suffix="", add=None, wire=False, tm=None, tn=None,
                b_split=False, after=None):
    tokens = a.shape[0]
    if row_sharded:
        sr, sc = rows // N_CHIPS, cols // N_CORES
    else:
        sr, sc = rows // N_CORES, cols // N_CHIPS
    tm = _tile(sr, 1024) if tm is None else tm
    tn = _tile(sc, 1024) if tn is None else tn
    assert sr % tm == 0 and sc % tn == 0, (rows, cols, tm, tn)
    rb, cb = sr // tm, sc // tn
    if tn >= tm:
        ij, grid = (lambda s, t: (t, s)), (N_CHIPS, cb, rb)
    else:
        ij, grid = (lambda s, t: (s, t)), (N_CHIPS, rb, cb)
    if row_sharded:
        a_tile = lambda p, i, j, h: p * rb + i
        b_tile = lambda p, i, j, h: h[0] * cb + j
    else:
        a_tile = lambda p, i, j, h: h[0] * rb + i
        b_tile = lambda p, i, j, h: p * cb + j
    a_spec = pl.BlockSpec((tokens, tm), lambda p, s, t, h: (0, a_tile(p, *ij(s, t), h)))
    if b_split:
        nbh = b.shape[2] // tn
        assert b.shape[2] % tn == 0
        b_spec = pl.BlockSpec((None, tokens, tn), lambda p, s, t, h: (b_tile(p, *ij(s, t), h) // nbh, 0,
                                                                       b_tile(p, *ij(s, t), h) % nbh))
    else:
        b_spec = pl.BlockSpec((tokens, tn), lambda p, s, t, h: (0, b_tile(p, *ij(s, t), h)))
    out_spec = pl.BlockSpec((None, tm, tn), lambda p, s, t, h: (p, *ij(s, t)))
    in_specs, operands = [a_spec, b_spec], [a, b]
    if add is not None:
        in_specs.append(out_spec)
        operands.append(add)
    if after is not None:
        in_specs.append(pl.BlockSpec(memory_space=pl.ANY))
        operands.append(after)
    out_dtypes = [F32, BF16] if wire else [F32]

    def body(h_ref, a_ref, b_ref, *rest):
        acc = _dot_tn(a_ref[...].astype(BF16), b_ref[...].astype(BF16))
        if add is not None:
            acc = acc + rest[0][...]
        for o_ref in rest[-len(out_dtypes):]:
            o_ref[...] = acc.astype(o_ref.dtype)

    out = pl.pallas_call(
        body, name=name + suffix, out_shape=[jax.ShapeDtypeStruct((N_CHIPS, sr, sc), dt) for dt in out_dtypes],
        grid_spec=pltpu.PrefetchScalarGridSpec(num_scalar_prefetch=1, grid=grid, in_specs=in_specs,
                                               out_specs=[out_spec] * len(out_dtypes)),
        compiler_params=_params(("parallel", "parallel", "parallel")),
    )(which, *operands)
    return tuple(out) if wire else out[0]


def _rms_rows(x):
    return lax.rsqrt(jnp.mean(x * x, axis=-1, keepdims=True) + RMS_EPS)


def _rmsnorm_fwd(x, w, *, name, width=None, col=0, out_dtype=BF16, tr=512):
    rows = x.shape[0]
    width = x.shape[1] if width is None else width
    tr = _tile(rows, tr, SUBLANES)

    def body(x_ref, w_ref, o_ref):
        xv = x_ref[...]
        o_ref[...] = (xv * _rms_rows(xv) * w_ref[...]).astype(o_ref.dtype)

    return pl.pallas_call(
        body, name=name, grid=(rows // tr,),
        in_specs=[pl.BlockSpec((tr, width), lambda i: (i, col)), pl.BlockSpec((1, width), lambda i: (0, 0))],
        out_specs=pl.BlockSpec((tr, width), lambda i: (i, 0)),
        out_shape=jax.ShapeDtypeStruct((rows, width), out_dtype),
        compiler_params=_params(("parallel",)),
    )(x, w)


def _rmsnorm_bwd_rows(xv, w, dy):
    r = _rms_rows(xv)
    n = xv * r
    dn = dy * w
    dx = r * (dn - n * jnp.mean(dn * n, axis=-1, keepdims=True))
    return dx, dy * n


def _rmsnorm_bwd(x, w, dy, *, name, width=None, col=0, dy_col=0, add=None, tr=512, dx_dtypes=(F32,)):
    rows = x.shape[0]
    n_dx = len(dx_dtypes)
    width = x.shape[1] if width is None else width
    tr = _tile(rows, tr, SUBLANES)
    in_specs = [pl.BlockSpec((tr, width), lambda i: (i, col)), pl.BlockSpec((1, width), lambda i: (0, 0)),
                pl.BlockSpec((tr, width), lambda i: (i, dy_col))]
    operands = [x, w, dy]
    if add is not None:
        in_specs.append(pl.BlockSpec((tr, width), lambda i: (i, 0)))
        operands.append(add)

    def body(*refs):
        x_ref, w_ref, dy_ref = refs[:3]
        add_ref = refs[3] if add is not None else None
        dx_refs, dw_ref = refs[-1 - n_dx:-1], refs[-1]
        dx, dwp = _rmsnorm_bwd_rows(x_ref[...], w_ref[...], dy_ref[...])
        if add_ref is not None:
            dx = dx + add_ref[...]
        for dx_ref in dx_refs:
            dx_ref[...] = dx.astype(dx_ref.dtype)
        part = jnp.sum(dwp, axis=0, keepdims=True)

        @pl.when(pl.program_id(0) == 0)
        def _():
            dw_ref[...] = part

        @pl.when(pl.program_id(0) > 0)
        def _():
            dw_ref[...] += part

    return pl.pallas_call(
        body, name=name, grid=(rows // tr,), in_specs=in_specs,
        out_specs=[pl.BlockSpec((tr, width), lambda i: (i, 0))] * n_dx + [pl.BlockSpec((1, width), lambda i: (0, 0))],
        out_shape=[jax.ShapeDtypeStruct((rows, width), dt) for dt in dx_dtypes] + [jax.ShapeDtypeStruct((1, width), F32)],
        compiler_params=_params(("arbitrary",)),
    )(*operands)


def _final_norm_loss(h, w, target, *, tr=512):
    rows, d = h.shape
    tr = _tile(rows, tr, SUBLANES)

    def body(h_ref, w_ref, t_ref, loss_ref, dh_ref, dhb_ref, dw_ref):
        hv, wv = h_ref[...], w_ref[...]
        r = _rms_rows(hv)
        n = hv * r
        err = n * wv - t_ref[...]
        d_out = err * (1.0 / d)
        dn = d_out * wv
        dh = r * (dn - n * jnp.mean(dn * n, axis=-1, keepdims=True))
        dh_ref[...] = dh
        dhb_ref[...] = dh.astype(BF16)
        dw_part = jnp.sum(d_out * n, axis=0, keepdims=True)
        loss_part = jnp.full((SUBLANES, LANES), 0.5 / d, F32) * jnp.sum(err * err)

        @pl.when(pl.program_id(0) == 0)
        def _():
            dw_ref[...] = dw_part
            loss_ref[...] = loss_part

        @pl.when(pl.program_id(0) > 0)
        def _():
            dw_ref[...] += dw_part
            loss_ref[...] += loss_part

    return pl.pallas_call(
        body, name="final_norm_loss", grid=(rows // tr,),
        in_specs=[pl.BlockSpec((tr, d), lambda i: (i, 0)), pl.BlockSpec((1, d), lambda i: (0, 0)),
                  pl.BlockSpec((tr, d), lambda i: (i, 0))],
        out_specs=[pl.BlockSpec((SUBLANES, LANES), lambda i: (0, 0)), pl.BlockSpec((tr, d), lambda i: (i, 0)),
                   pl.BlockSpec((tr, d), lambda i: (i, 0)), pl.BlockSpec((1, d), lambda i: (0, 0))],
        out_shape=[jax.ShapeDtypeStruct((SUBLANES, LANES), F32), jax.ShapeDtypeStruct((rows, d), F32),
                   jax.ShapeDtypeStruct((rows, d), BF16), jax.ShapeDtypeStruct((1, d), F32)],
        compiler_params=_params(("arbitrary",)),
    )(h, w, target)


def _cmul(ar, ai, br, bi):
    return ar * br - ai * bi, ar * bi + ai * br


def _dot_exact(a, b, dims):
    return lax.dot_general(a, b, (dims, ((), ())), preferred_element_type=F32, precision=lax.Precision.HIGHEST)


def _s5_discretize(lr, li, dt):
    mag = jnp.exp(lr * dt)
    th = li * dt
    ar, ai = mag * jnp.cos(th), mag * jnp.sin(th)
    nr, ni = ar - 1.0, ai
    den = lr * lr + li * li
    zr = (nr * lr + ni * li) / den
    zi = (ni * lr - nr * li) / den
    return mag, ar, ai, nr, ni, den, zr, zi


def _band_slices(group):
    j, gi = divmod(group, GROUPS_PER_BATCH)
    rows = slice(gi * SSM_GROUP, (gi + 1) * SSM_GROUP)
    re = slice(gi * SSM_STATE, (gi + 1) * SSM_STATE)
    im = slice(STATE_PER_BATCH + gi * SSM_STATE, STATE_PER_BATCH + (gi + 1) * SSM_STATE)
    return j, rows, re, im


def _s5_bands(lam_re, lam_im, log_dt, b_re, b_im, c_re, c_im):
    g, _ = lam_re.shape
    nb = g // GROUPS_PER_BATCH
    s2 = 2 * STATE_PER_BATCH

    def body(lr_ref, li_ref, ldt_ref, br_ref, bi_ref, cr_ref, ci_ref, wb_ref, wct_ref, a_ref):
        dt = jnp.exp(ldt_ref[...])
        _, ar, ai, _, _, _, zr, zi = _s5_discretize(lr_ref[...], li_ref[...], dt)
        wb_ref[...] = jnp.zeros_like(wb_ref)
        wct_ref[...] = jnp.zeros_like(wct_ref)
        for group in range(g):
            j, rows, re, im = _band_slices(group)
            zr_g, zi_g = zr[group:group + 1, :], zi[group:group + 1, :]
            bre, bim = br_ref[group], bi_ref[group]
            wb_ref[j, rows, re] = (zr_g * bre - zi_g * bim).astype(BF16)
            wb_ref[j, rows, im] = (zr_g * bim + zi_g * bre).astype(BF16)
            wct_ref[j, rows, re] = cr_ref[group].astype(BF16)
            wct_ref[j, rows, im] = (-ci_ref[group]).astype(BF16)
            a_ref[j, :, re] = ar[group:group + 1, :]
            a_ref[j, :, im] = ai[group:group + 1, :]

    return pl.pallas_call(
        body, name="s5_bands",
        out_shape=[jax.ShapeDtypeStruct((nb, LANES, s2), BF16)] * 2 + [jax.ShapeDtypeStruct((nb, 1, s2), F32)],
    )(lam_re, lam_im, log_dt, b_re, b_im, c_re, c_im)


def _s5_bands_bwd(lam_re, lam_im, log_dt, b_re, b_im, dwb, dwct, dabar):
    g, p = lam_re.shape
    gh = b_re.shape[1:]

    def body(lr_ref, li_ref, ldt_ref, br_ref, bi_ref, dwb_ref, dwct_ref, da_ref,
             dlr_ref, dli_ref, dldt_ref, dbre_ref, dbim_ref, dcre_ref, dcim_ref, dzr_ref, dzi_ref, dar_ref, dai_ref):
        lr, li = lr_ref[...], li_ref[...]
        dt = jnp.exp(ldt_ref[...])
        mag, ar, ai, nr, ni, den, zr, zi = _s5_discretize(lr, li, dt)
        for group in range(g):
            j, rows, re, im = _band_slices(group)
            zr_g, zi_g = zr[group:group + 1, :], zi[group:group + 1, :]
            bre, bim = br_ref[group], bi_ref[group]
            dbr, dbi = dwb_ref[j, rows, re], dwb_ref[j, rows, im]
            dbre_ref[group] = zr_g * dbr + zi_g * dbi
            dbim_ref[group] = zr_g * dbi - zi_g * dbr
            dzr_ref[group:group + 1, :] = jnp.sum(bre * dbr + bim * dbi, axis=0, keepdims=True)
            dzi_ref[group:group + 1, :] = jnp.sum(bre * dbi - bim * dbr, axis=0, keepdims=True)
            dcre_ref[group] = dwct_ref[j, rows, re]
            dcim_ref[group] = -dwct_ref[j, rows, im]
            dar_ref[group:group + 1, :] = da_ref[j, :, re]
            dai_ref[group:group + 1, :] = da_ref[j, :, im]
        dzr, dzi = dzr_ref[...], dzi_ref[...]
        inv = 1.0 / den
        d_nr = (dzr * lr - dzi * li) * inv
        d_ni = (dzr * li + dzi * lr) * inv
        d_den = -(dzr * zr + dzi * zi) * inv
        d_lr = (dzr * nr + dzi * ni) * inv + 2.0 * lr * d_den
        d_li = (dzr * ni - dzi * nr) * inv + 2.0 * li * d_den
        t_ar = dar_ref[...] + d_nr
        t_ai = dai_ref[...] + d_ni
        d_lrdt = t_ar * ar + t_ai * ai
        d_th = t_ai * ar - t_ar * ai
        dlr_ref[...] = d_lr + d_lrdt * dt
        dli_ref[...] = d_li + d_th * dt
        dldt_ref[...] = jnp.sum(d_lrdt * lr + d_th * li, axis=1, keepdims=True) * dt

    return pl.pallas_call(
        body, name="s5_bands_bwd",
        out_shape=[jax.ShapeDtypeStruct((g, p), F32)] * 2 + [jax.ShapeDtypeStruct((g, 1), F32)]
        + [jax.ShapeDtypeStruct((g,) + gh, F32)] * 4,
        scratch_shapes=[pltpu.VMEM((g, p), F32)] * 4,
    )(lam_re, lam_im, log_dt, b_re, b_im, dwb, dwct, dabar)


def _powers(ar, ai, count):
    out = [(ar, ai)]
    for _ in range(count - 1):
        out.append(_cmul(out[-1][0], out[-1][1], ar, ai))
    return out


def _scan_coefs(ar, ai, reverse):
    w = ar.shape[-1]
    pw = _powers(ar, ai, SUBLANES)
    row = lax.broadcasted_iota(jnp.int32, (SUBLANES, w), 0)
    steps = []
    d = 1
    while d < SUBLANES:
        keep = (row < SUBLANES - d) if reverse else (row >= d)
        pr, pi = pw[d - 1]
        steps.append((d, jnp.where(keep, pr, 0.0), jnp.where(keep, pi, 0.0)))
        d *= 2
    cr = jnp.zeros((SUBLANES, w), F32)
    ci = jnp.zeros((SUBLANES, w), F32)
    for t in range(SUBLANES):
        pr, pi = pw[SUBLANES - 1 - t] if reverse else pw[t]
        cr = jnp.where(row == t, pr, cr)
        ci = jnp.where(row == t, pi, ci)
    return steps, cr, ci


def _scan_local(xr, xi, coefs, reverse):
    for d, mr, mi in coefs[0]:
        shift = SUBLANES - d if reverse else d
        sr, si = pltpu.roll(xr, shift, 0), pltpu.roll(xi, shift, 0)
        pr, pi = _cmul(mr, mi, sr, si)
        xr, xi = xr + pr, xi + pi
    return xr, xi


def _scan_carry(xr, xi, carry_r, carry_i, coefs):
    _, cr, ci = coefs
    pr, pi = _cmul(cr, ci, carry_r, carry_i)
    return xr + pr, xi + pi


SCAN_TILES = 4


def _gelu(x):
    c = math.sqrt(2.0 / math.pi)
    return 0.5 * x * (1.0 + jnp.tanh(c * (x + 0.044715 * x * x * x)))


def _gelu_grad(x):
    c = math.sqrt(2.0 / math.pi)
    t = jnp.tanh(c * (x + 0.044715 * x * x * x))
    return 0.5 * (1.0 + t) + 0.5 * x * (1.0 - t * t) * c * (1.0 + 3.0 * 0.044715 * x * x)


def _s5_fwd(proj, wb, wct, d_skip, abar):
    rows = proj.shape[0]
    nb = wb.shape[0]
    s2 = 2 * STATE_PER_BATCH
    st = STATE_PER_BATCH
    chunk = _tile(rows, 1024, SUBLANES)
    assert rows % (SCAN_TILES * SUBLANES) == 0

    def body(u_ref, wb_ref, wc_ref, d_ref, a_ref, s_ref, y_ref, yg_ref):
        for c0 in range(0, rows, chunk):
            s_ref[pl.ds(c0, chunk), :] = _dot_nn(u_ref[pl.ds(c0, chunk), :].astype(BF16), wb_ref[...])
        av = a_ref[...]
        coefs = _scan_coefs(av[:, :st], av[:, st:], reverse=False)

        def tiles(b, carry):
            starts = [pl.multiple_of((b * SCAN_TILES + t) * SUBLANES, SUBLANES) for t in range(SCAN_TILES)]
            local = [_scan_local(s_ref[pl.ds(r0, SUBLANES), :st], s_ref[pl.ds(r0, SUBLANES), st:], coefs, False)
                     for r0 in starts]
            for r0, (lr, li) in zip(starts, local):
                xr, xi = _scan_carry(lr, li, carry[0], carry[1], coefs)
                s_ref[pl.ds(r0, SUBLANES), :st] = xr
                s_ref[pl.ds(r0, SUBLANES), st:] = xi
                carry = xr[SUBLANES - 1:, :], xi[SUBLANES - 1:, :]
            return carry

        zero = jnp.zeros((1, st), F32)
        lax.fori_loop(0, rows // (SCAN_TILES * SUBLANES), tiles, (zero, zero))
        for c0 in range(0, rows, chunk):
            y = _dot_nt(s_ref[pl.ds(c0, chunk), :].astype(BF16), wc_ref[...]) + d_ref[...] * u_ref[pl.ds(c0, chunk), :]
            y_ref[pl.ds(c0, chunk), :] = y
            yg_ref[pl.ds(c0, chunk), :] = _gelu(y).astype(BF16)

    return pl.pallas_call(
        body, name="s5_fwd", grid=(nb,),
        in_specs=[pl.BlockSpec((rows, LANES), lambda j: (0, j)), pl.BlockSpec((None, LANES, s2), lambda j: (j, 0, 0)),
                  pl.BlockSpec((None, LANES, s2), lambda j: (j, 0, 0)), pl.BlockSpec((1, LANES), lambda j: (0, j)),
                  pl.BlockSpec((None, 1, s2), lambda j: (j, 0, 0))],
        out_specs=[pl.BlockSpec((rows, s2), lambda j: (0, j)), pl.BlockSpec((rows, LANES), lambda j: (0, j)),
                   pl.BlockSpec((rows, LANES), lambda j: (0, j))],
        out_shape=[jax.ShapeDtypeStruct((rows, nb * s2), F32), jax.ShapeDtypeStruct((rows, nb * LANES), F32),
                   jax.ShapeDtypeStruct((rows, nb * LANES), BF16)],
        compiler_params=_params(("parallel",)),
    )(proj, wb, wct, d_skip, abar)


def _s5_bwd(proj, states, y_pre, dyg_a, dyg_b, wb, wct, d_skip, abar):
    rows = proj.shape[0]
    nb = wb.shape[0]
    s2 = 2 * STATE_PER_BATCH
    st = STATE_PER_BATCH
    chunk = _tile(rows, 1024, SUBLANES)
    assert rows % (SCAN_TILES * SUBLANES) == 0
    n_tiles = rows // SUBLANES

    def body(u_ref, s_ref, y_ref, ga_ref, gb_ref, wb_ref, wc_ref, d_ref, a_ref,
             du_ref, dwb_ref, dwc_ref, da_ref, dd_ref, ds_ref, dy_ref):
        dy_ref[...] = (ga_ref[...] + gb_ref[...]) * _gelu_grad(y_ref[...])
        dd_ref[...] = jnp.sum(dy_ref[...] * u_ref[...], axis=0, keepdims=True)
        for c0 in range(0, rows, chunk):
            ds_ref[pl.ds(c0, chunk), :] = _dot_nn(dy_ref[pl.ds(c0, chunk), :].astype(BF16), wc_ref[...])
        dwc_ref[...] = _dot_tn(dy_ref[...].astype(BF16), s_ref[...].astype(BF16))
        av = a_ref[...]
        coefs = _scan_coefs(av[:, :st], -av[:, st:], reverse=True)
        row = lax.broadcasted_iota(jnp.int32, (SUBLANES, st), 0)

        def tiles(k, carry):
            cr, ci, acc_r, acc_i = carry
            starts = [pl.multiple_of((n_tiles - 1 - k * SCAN_TILES - t) * SUBLANES, SUBLANES) for t in range(SCAN_TILES)]
            local = [_scan_local(ds_ref[pl.ds(r0, SUBLANES), :st], ds_ref[pl.ds(r0, SUBLANES), st:], coefs, True)
                     for r0 in starts]
            states = [(s_ref[pl.ds(r0, SUBLANES), :st], s_ref[pl.ds(r0, SUBLANES), st:]) for r0 in starts]
            for r0, (lr, li), (pr, pi) in zip(starts, local, states):
                xr, xi = _scan_carry(lr, li, cr, ci, coefs)
                ds_ref[pl.ds(r0, SUBLANES), :st] = xr
                ds_ref[pl.ds(r0, SUBLANES), st:] = xi
                nr = jnp.where(row == SUBLANES - 1, cr, pltpu.roll(xr, SUBLANES - 1, 0))
                ni = jnp.where(row == SUBLANES - 1, ci, pltpu.roll(xi, SUBLANES - 1, 0))
                acc_r = acc_r + pr * nr + pi * ni
                acc_i = acc_i + pr * ni - pi * nr
                cr, ci = xr[:1, :], xi[:1, :]
            return cr, ci, acc_r, acc_i

        zero = jnp.zeros((1, st), F32)
        zacc = jnp.zeros((SUBLANES, st), F32)
        _, _, acc_r, acc_i = lax.fori_loop(0, n_tiles // SCAN_TILES, tiles, (zero, zero, zacc, zacc))
        da_ref[:, :st] = jnp.sum(acc_r, axis=0, keepdims=True)
        da_ref[:, st:] = jnp.sum(acc_i, axis=0, keepdims=True)
        for c0 in range(0, rows, chunk):
            du_ref[pl.ds(c0, chunk), :] = (_dot_nt(ds_ref[pl.ds(c0, chunk), :].astype(BF16), wb_ref[...])
                                           + d_ref[...] * dy_ref[pl.ds(c0, chunk), :]).astype(du_ref.dtype)
        dwb_ref[...] = _dot_tn(u_ref[...].astype(BF16), ds_ref[...].astype(BF16))

    col = pl.BlockSpec((rows, LANES), lambda j: (0, j))
    return pl.pallas_call(
        body, name="s5_bwd", grid=(nb,),
        in_specs=[col, pl.BlockSpec((rows, s2), lambda j: (0, j)), col, col, col,
                  pl.BlockSpec((None, LANES, s2), lambda j: (j, 0, 0)), pl.BlockSpec((None, LANES, s2), lambda j: (j, 0, 0)),
                  pl.BlockSpec((1, LANES), lambda j: (0, j)), pl.BlockSpec((None, 1, s2), lambda j: (j, 0, 0))],
        out_specs=[col, pl.BlockSpec((None, LANES, s2), lambda j: (j, 0, 0)),
                   pl.BlockSpec((None, LANES, s2), lambda j: (j, 0, 0)), pl.BlockSpec((None, 1, s2), lambda j: (j, 0, 0)),
                   pl.BlockSpec((1, LANES), lambda j: (0, j))],
        out_shape=[jax.ShapeDtypeStruct((rows, nb * LANES), BF16), jax.ShapeDtypeStruct((nb, LANES, s2), F32),
                   jax.ShapeDtypeStruct((nb, LANES, s2), F32), jax.ShapeDtypeStruct((nb, 1, s2), F32),
                   jax.ShapeDtypeStruct((1, nb * LANES), F32)],
        scratch_shapes=[pltpu.VMEM((rows, s2), F32), pltpu.VMEM((rows, LANES), F32)],
        compiler_params=_params(("parallel",)),
    )(proj, states, y_pre, dyg_a, dyg_b, wb, wct, d_skip, abar)


def _glu_norm_fwd(y_pre, z, w, *, tr=512):
    rows, width = y_pre.shape
    tr = _tile(rows, tr, SUBLANES)

    def body(y_ref, z_ref, w_ref, o_ref):
        v = _gelu(y_ref[...]) * jax.nn.sigmoid(z_ref[...])
        o_ref[...] = (v * _rms_rows(v) * w_ref[...]).astype(o_ref.dtype)

    blk = pl.BlockSpec((tr, width), lambda i: (i, 0))
    return pl.pallas_call(
        body, name="glu_norm_fwd", grid=(rows // tr,),
        in_specs=[blk, blk, pl.BlockSpec((1, width), lambda i: (0, 0))], out_specs=blk,
        out_shape=jax.ShapeDtypeStruct((rows, width), BF16), compiler_params=_params(("parallel",)),
    )(y_pre, z, w)


def _glu_norm_bwd(y_pre, z, w, dycat, *, tr=512):
    rows, width = y_pre.shape
    tr = _tile(rows, tr, SUBLANES)

    def body(y_ref, z_ref, w_ref, dy_ref, dz_ref, dg_ref, dw_ref, db_ref):
        yg = _gelu(y_ref[...])
        sg = jax.nn.sigmoid(z_ref[...])
        dv, dwp = _rmsnorm_bwd_rows(yg * sg, w_ref[...], dy_ref[...])
        dz = dv * yg * sg * (1.0 - sg)
        dz_ref[...] = dz.astype(dz_ref.dtype)
        dg_ref[...] = dv * sg
        dw_part = jnp.sum(dwp, axis=0, keepdims=True)
        db_part = jnp.sum(dz, axis=0, keepdims=True)

        @pl.when(pl.program_id(0) == 0)
        def _():
            dw_ref[...] = dw_part
            db_ref[...] = db_part

        @pl.when(pl.program_id(0) > 0)
        def _():
            dw_ref[...] += dw_part
            db_ref[...] += db_part

    blk = pl.BlockSpec((tr, width), lambda i: (i, 0))
    vec = pl.BlockSpec((1, width), lambda i: (0, 0))
    return pl.pallas_call(
        body, name="glu_norm_bwd", grid=(rows // tr,), in_specs=[blk, blk, vec, blk], out_specs=[blk, blk, vec, vec],
        out_shape=[jax.ShapeDtypeStruct((rows, width), BF16), jax.ShapeDtypeStruct((rows, width), F32)]
        + [jax.ShapeDtypeStruct((1, width), F32)] * 2,
        compiler_params=_params(("arbitrary",)),
    )(y_pre, z, w, dycat)


def _rope_tables(pos, freq, sign):
    rows = pos.shape[0]

    def body(p_ref, f_ref, s_ref, cos_ref, sin_ref):
        ang = p_ref[...] * f_ref[...]
        cos_ref[...] = jnp.cos(ang)
        sin_ref[...] = jnp.sin(ang) * s_ref[...]

    return pl.pallas_call(body, name="rope_tables", out_shape=[jax.ShapeDtypeStruct((rows, LANES), F32)] * 2)(pos, freq, sign)


def _rope(x, cos, sin_signed):
    half = QK_ROPE_DIM // 2
    src = lax.broadcasted_iota(jnp.int32, (LANES, LANES), 0)
    dst = lax.broadcasted_iota(jnp.int32, (LANES, LANES), 1)
    swap = jnp.where(jnp.logical_or(jnp.logical_and(dst < half, src == dst + half),
                                    jnp.logical_and(jnp.logical_and(dst >= half, dst < 2 * half), src == dst - half)),
                     1.0, 0.0).astype(F32)
    swapped = _dot_exact(x, swap, ((1,), (0,)))
    return x * cos + swapped * sin_signed


def _attn_prep(q, kv, proj, kpe_col, cos, sin, *, tr=512):
    rows = q.shape[0]
    heads = q.shape[1] // HEAD_SLOT
    tr = _tile(rows, tr, SUBLANES)

    def body(q_ref, kv_ref, kpe_ref, cos_ref, sin_ref, qc_ref, kc_ref, v_ref):
        c, s = cos_ref[...], sin_ref[...]
        kpe = _rope(kpe_ref[...], c, s).astype(BF16)
        for h in range(heads):
            nope = slice(h * HEAD_SLOT, h * HEAD_SLOT + LANES)
            pe = slice(h * HEAD_SLOT + LANES, (h + 1) * HEAD_SLOT)
            qc_ref[:, nope] = q_ref[:, nope].astype(BF16)
            qc_ref[:, pe] = _rope(q_ref[:, pe], c, s).astype(BF16)
            kc_ref[:, nope] = kv_ref[:, nope].astype(BF16)
            kc_ref[:, pe] = kpe
            v_ref[:, h * LANES:(h + 1) * LANES] = kv_ref[:, pe].astype(BF16)

    slots = pl.BlockSpec((tr, heads * HEAD_SLOT), lambda i: (i, 0))
    tab = pl.BlockSpec((tr, LANES), lambda i: (i, 0))
    return pl.pallas_call(
        body, name="attn_prep", grid=(rows // tr,),
        in_specs=[slots, slots, pl.BlockSpec((tr, LANES), lambda i: (i, kpe_col)), tab, tab],
        out_specs=[slots, slots, pl.BlockSpec((tr, heads * LANES), lambda i: (i, 0))],
        out_shape=[jax.ShapeDtypeStruct((rows, heads * HEAD_SLOT), BF16)] * 2
        + [jax.ShapeDtypeStruct((rows, heads * LANES), BF16)],
        compiler_params=_params(("parallel",)),
    )(q, kv, proj, cos, sin)


def _causal(tq, tk):
    return lax.broadcasted_iota(jnp.int32, (tq, tk), 1) <= lax.broadcasted_iota(jnp.int32, (tq, tk), 0)


def _attn_fwd(qc, kc, vb, *, scale, tq=512):
    rows = qc.shape[0]
    heads = qc.shape[1] // HEAD_SLOT
    tq = _tile(rows, tq, SUBLANES)
    tk = tq

    def body(q_ref, k_ref, v_ref, o_ref, lse_ref):
        i = pl.program_id(1)
        q = q_ref[...]

        def step(j, carry, diagonal):
            m, l, acc = carry
            k0 = pl.multiple_of(j * tk, tk)
            s = _dot_nt(q, k_ref[pl.ds(k0, tk), :]) * scale
            if diagonal:
                s = jnp.where(_causal(tq, tk), s, NEG_INF)
            m_new = jnp.maximum(m, jnp.max(s, axis=-1, keepdims=True))
            p = jnp.exp(s - m_new)
            alpha = jnp.exp(m - m_new)
            l = alpha * l + jnp.sum(p, axis=-1, keepdims=True)
            acc = alpha * acc + _dot_nn(p.astype(BF16), v_ref[pl.ds(k0, tk), :])
            return m_new, l, acc

        init = (jnp.full((tq, 1), NEG_INF, F32), jnp.zeros((tq, 1), F32), jnp.zeros((tq, LANES), F32))
        below = lax.fori_loop(0, i, lambda j, carry: step(j, carry, False), init)
        m, l, acc = step(i, below, True)
        o_ref[...] = acc / l
        lse_ref[...] = jnp.broadcast_to(m + jnp.log(l), (tq, LANES))

    return pl.pallas_call(
        body, name="attn_fwd", grid=(heads, rows // tq),
        in_specs=[pl.BlockSpec((tq, HEAD_SLOT), lambda h, i: (i, h)), pl.BlockSpec((rows, HEAD_SLOT), lambda h, i: (0, h)),
                  pl.BlockSpec((rows, LANES), lambda h, i: (0, h))],
        out_specs=[pl.BlockSpec((tq, LANES), lambda h, i: (i, h))] * 2,
        out_shape=[jax.ShapeDtypeStruct((rows, heads * LANES), F32)] * 2,
        compiler_params=_params(("parallel", "parallel")),
    )(qc, kc, vb)


def _attn_bwd(qc, kc, vb, o, do, lse, cos, sin, *, scale, tk=512):
    rows = qc.shape[0]
    heads = qc.shape[1] // HEAD_SLOT
    tk = _tile(rows, tk, SUBLANES)
    tq = tk
    nq = rows // tq

    def body(q_ref, k_ref, v_ref, o_ref, do_ref, lse_ref, cos_ref, sin_ref, dq_ref, dkv_ref, dkpe_ref, dq_acc, delta_ref):
        j = pl.program_id(1)

        @pl.when(j == 0)
        def _():
            dq_acc[...] = jnp.zeros_like(dq_acc)
            for r0 in range(0, rows, tq):
                d = jnp.sum(do_ref[pl.ds(r0, tq), :] * o_ref[pl.ds(r0, tq), :], axis=-1, keepdims=True)
                delta_ref[pl.ds(r0, tq), :] = jnp.broadcast_to(d, (tq, LANES))

        kb, vv = k_ref[...], v_ref[...]

        def step(i, carry, diagonal):
            dk, dv = carry
            q0 = pl.multiple_of(i * tq, tq)
            qb = q_ref[pl.ds(q0, tq), :]
            dob = do_ref[pl.ds(q0, tq), :].astype(BF16)
            s = _dot_nt(qb, kb) * scale
            p = jnp.exp(s - lse_ref[pl.ds(q0, tq), :1])
            if diagonal:
                p = jnp.where(_causal(tq, tk), p, 0.0)
            dv = dv + _dot_tn(p.astype(BF16), dob)
            ds = (p * (_dot_nt(dob, vv) - delta_ref[pl.ds(q0, tq), :1])).astype(BF16)
            dk = dk + _dot_tn(ds, qb)
            dq_acc[pl.ds(q0, tq), :] += _dot_nn(ds, kb)
            return dk, dv

        zero = (jnp.zeros((tk, HEAD_SLOT), F32), jnp.zeros((tk, LANES), F32))
        dk, dv = lax.fori_loop(j + 1, nq, lambda i, carry: step(i, carry, False), step(j, zero, True))
        dkv_ref[:, :LANES] = (dk[:, :LANES] * scale).astype(dkv_ref.dtype)
        dkv_ref[:, LANES:] = dv.astype(dkv_ref.dtype)
        dkpe_ref[...] = dk[:, LANES:] * scale

        @pl.when(j == nq - 1)
        def _():
            for r0 in range(0, rows, tq):
                dq = dq_acc[pl.ds(r0, tq), :] * scale
                dq_ref[pl.ds(r0, tq), :LANES] = dq[:, :LANES].astype(dq_ref.dtype)
                dq_ref[pl.ds(r0, tq), LANES:] = _rope(dq[:, LANES:], cos_ref[pl.ds(r0, tq), :],
                                                      -sin_ref[pl.ds(r0, tq), :]).astype(dq_ref.dtype)

    full_q = pl.BlockSpec((rows, HEAD_SLOT), lambda h, j: (0, h))
    full_v = pl.BlockSpec((rows, LANES), lambda h, j: (0, h))
    tab = pl.BlockSpec((rows, LANES), lambda h, j: (0, 0))
    return pl.pallas_call(
        body, name="attn_bwd", grid=(heads, rows // tk),
        in_specs=[full_q, pl.BlockSpec((tk, HEAD_SLOT), lambda h, j: (j, h)), pl.BlockSpec((tk, LANES), lambda h, j: (j, h)),
                  full_v, full_v, full_v, tab, tab],
        out_specs=[full_q, pl.BlockSpec((tk, HEAD_SLOT), lambda h, j: (j, h)), pl.BlockSpec((tk, LANES), lambda h, j: (j, h))],
        out_shape=[jax.ShapeDtypeStruct((rows, heads * HEAD_SLOT), BF16), jax.ShapeDtypeStruct((rows, heads * HEAD_SLOT), BF16),
                   jax.ShapeDtypeStruct((rows, heads * LANES), F32)],
        scratch_shapes=[pltpu.VMEM((rows, HEAD_SLOT), F32), pltpu.VMEM((rows, LANES), F32)],
        compiler_params=_params(("parallel", "arbitrary")),
    )(qc, kc, vb, o, do, lse, cos, sin)


def _kpe_bwd(dkpe_heads, cos, sin, *, tr=512):
    rows = dkpe_heads.shape[0]
    heads = dkpe_heads.shape[1] // LANES
    tr = _tile(rows, tr, 2 * SUBLANES)

    def body(d_ref, cos_ref, sin_ref, o_ref):
        acc = d_ref[:, :LANES]
        for h in range(1, heads):
            acc = acc + d_ref[:, h * LANES:(h + 1) * LANES]
        o_ref[...] = _rope(acc, cos_ref[...], -sin_ref[...]).astype(o_ref.dtype)

    tab = pl.BlockSpec((tr, LANES), lambda i: (i, 0))
    return pl.pallas_call(
        body, name="kpe_bwd", grid=(rows // tr,),
        in_specs=[pl.BlockSpec((tr, heads * LANES), lambda i: (i, 0)), tab, tab], out_specs=tab,
        out_shape=jax.ShapeDtypeStruct((rows, LANES), BF16), compiler_params=_params(("parallel",)),
    )(dkpe_heads, cos, sin)


CONV_ROWS = 512


def _with_halo(ref, r0, ci, n_chunks, ch, lanes, before, after):
    parts = []
    if before:
        lo = pl.multiple_of(jnp.maximum(r0 - SUBLANES, 0), SUBLANES)
        parts.append(ref[pl.ds(lo, SUBLANES), lanes] * jnp.where(ci > 0, 1.0, 0.0))
    parts.append(ref[pl.ds(r0, ch), lanes])
    if after:
        hi = pl.multiple_of(jnp.minimum(r0 + ch, n_chunks * ch - SUBLANES), SUBLANES)
        parts.append(ref[pl.ds(hi, SUBLANES), lanes] * jnp.where(ci < n_chunks - 1, 1.0, 0.0))
    return jnp.concatenate(parts, axis=0)


def _taps(ext):
    return pltpu.roll(ext, 2, 0)[SUBLANES:], pltpu.roll(ext, 1, 0)[SUBLANES:], ext[SUBLANES:]


def _conv3(taps, w, b):
    return w[0:1, :] * taps[0] + w[1:2, :] * taps[1] + w[2:3, :] * taps[2] + b


def _conv_gate_fwd(a, conv_w, conv_b, *, tc=512):
    rows, f2 = a.shape
    f = f2 // 2
    tc = _tile(f, tc)
    nc = f // tc
    ch = _tile(rows, CONV_ROWS, SUBLANES)
    n_chunks = rows // ch

    def body(ag_ref, av_ref, wg_ref, wv_ref, bg_ref, bv_ref, o_ref):
        for lt in range(tc // LANES):
            lanes = slice(lt * LANES, (lt + 1) * LANES)
            wg, wv, bg, bv = wg_ref[:, lanes], wv_ref[:, lanes], bg_ref[:, lanes], bv_ref[:, lanes]

            def chunk(ci, carry):
                r0 = pl.multiple_of(ci * ch, ch)
                gate = _conv3(_taps(_with_halo(ag_ref, r0, ci, n_chunks, ch, lanes, True, False)), wg, bg)
                val = _conv3(_taps(_with_halo(av_ref, r0, ci, n_chunks, ch, lanes, True, False)), wv, bv)
                o_ref[pl.ds(r0, ch), lanes] = (gate * jax.nn.sigmoid(gate) * val).astype(o_ref.dtype)
                return carry

            lax.fori_loop(0, n_chunks, chunk, 0)

    return pl.pallas_call(
        body, name="conv_gate_fwd", grid=(nc,),
        in_specs=[pl.BlockSpec((rows, tc), lambda j: (0, j)), pl.BlockSpec((rows, tc), lambda j: (0, j + nc)),
                  pl.BlockSpec((SUBLANES, tc), lambda j: (0, j)), pl.BlockSpec((SUBLANES, tc), lambda j: (0, j + nc)),
                  pl.BlockSpec((1, tc), lambda j: (0, j)), pl.BlockSpec((1, tc), lambda j: (0, j + nc))],
        out_specs=pl.BlockSpec((rows, tc), lambda j: (0, j)),
        out_shape=jax.ShapeDtypeStruct((rows, f), BF16), compiler_params=_params(("parallel",)),
    )(a, a, conv_w, conv_w, conv_b, conv_b)


def _conv_gate_bwd(a, conv_w, conv_b, dg, *, tc=512):
    rows, f2 = a.shape
    f = f2 // 2
    tc = _tile(f, tc)
    nc = f // tc
    ch = _tile(rows, CONV_ROWS, SUBLANES)
    n_chunks = rows // ch
    ext_rows = ch + SUBLANES

    def fold(x):
        return jnp.sum(x.reshape(ch // SUBLANES, SUBLANES, LANES), axis=0)

    def body(ag_ref, av_ref, wg_ref, wv_ref, bg_ref, bv_ref, dg_ref, da_ref, dw_ref, db_ref):
        for lt in range(tc // LANES):
            lanes = slice(lt * LANES, (lt + 1) * LANES)
            wg, wv, bg, bv = wg_ref[:, lanes], wv_ref[:, lanes], bg_ref[:, lanes], bv_ref[:, lanes]

            def chunk(ci, acc):
                r0 = pl.multiple_of(ci * ch, ch)
                taps_g = _taps(_with_halo(ag_ref, r0, ci, n_chunks, ch, lanes, True, True))
                taps_v = _taps(_with_halo(av_ref, r0, ci, n_chunks, ch, lanes, True, True))
                dge = _with_halo(dg_ref, r0, ci, n_chunks, ch, lanes, False, True)
                gate, val = _conv3(taps_g, wg, bg), _conv3(taps_v, wv, bv)
                sg = jax.nn.sigmoid(gate)
                d_gate = dge * val * sg * (1.0 + gate * (1.0 - sg))
                d_val = dge * gate * sg
                new = []
                for half, (taps, w, d) in enumerate(((taps_g, wg, d_gate), (taps_v, wv, d_val))):
                    da = (w[2:3, :] * d[:ch] + w[1:2, :] * pltpu.roll(d, ext_rows - 1, 0)[:ch]
                          + w[0:1, :] * pltpu.roll(d, ext_rows - 2, 0)[:ch])
                    da_ref[half, pl.ds(r0, ch), lanes] = da.astype(da_ref.dtype)
                    dc = d[:ch]
                    sums = [fold(dc)] + [fold(dc * t[:ch]) for t in taps]
                    new.append(tuple(x + s for x, s in zip(acc[half], sums)))
                return tuple(new)

            zero = tuple(jnp.zeros((SUBLANES, LANES), F32) for _ in range(4))
            acc = lax.fori_loop(0, n_chunks, chunk, (zero, zero))
            row = lax.broadcasted_iota(jnp.int32, (SUBLANES, LANES), 0)
            for half in range(2):
                db, *taps = (jnp.sum(x, axis=0, keepdims=True) for x in acc[half])
                db_ref[half, :, lanes] = db
                dw = jnp.zeros((SUBLANES, LANES), F32)
                for tap in range(3):
                    dw = jnp.where(row == tap, taps[tap], dw)
                dw_ref[half, :, lanes] = dw

    lo = lambda j: (0, j)
    hi = lambda j: (0, j + nc)
    both = lambda j: (0, 0, j)
    return pl.pallas_call(
        body, name="conv_gate_bwd", grid=(nc,),
        in_specs=[pl.BlockSpec((rows, tc), lo), pl.BlockSpec((rows, tc), hi), pl.BlockSpec((SUBLANES, tc), lo),
                  pl.BlockSpec((SUBLANES, tc), hi), pl.BlockSpec((1, tc), lo), pl.BlockSpec((1, tc), hi),
                  pl.BlockSpec((rows, tc), lo)],
        out_specs=[pl.BlockSpec((2, rows, tc), both), pl.BlockSpec((2, SUBLANES, tc), both), pl.BlockSpec((2, 1, tc), both)],
        out_shape=[jax.ShapeDtypeStruct((2, rows, f), BF16), jax.ShapeDtypeStruct((2, SUBLANES, f), F32),
                   jax.ShapeDtypeStruct((2, 1, f), F32)],
        compiler_params=_params(("parallel",)),
    )(a, a, conv_w, conv_w, conv_b, conv_b, dg)


def _wgrad(a, b, rows, cols, row_sharded, name, **kw):
    return functools.partial(_wgrad_half, a, b, rows, cols, row_sharded, name, **kw)


class _NoExchange:
    def __init__(self, later, ffn):
        self.later, self.ffn = later, ffn

    def mixer_weights(self, after):
        return self.later

    def ffn_weights_arrived(self, after):
        return None

    def ffn_weights(self, after):
        return self.ffn

    def ffn_down_arrived(self, after):
        return None

    def ffn_down_weight(self, after):
        return self.ffn["ffn_w_down"]

    def ffn_grads(self, makers, after):
        self.ffn_makers = makers
        return None

    def ffn_backward_done(self, after):
        return None


def _local_step(x, posf, target, w, hooks):
    rows, d = x.shape
    width = w["ssm_d"].shape[1]
    qr, kvr = w["mla_q_norm_w"].shape[1], w["mla_kv_norm_w"].shape[1]
    heads = w["mla_w_ukv"].shape[1] // HEAD_SLOT
    f2 = w["ffn_conv_b"].shape[1]
    inp = w["w_in"].shape[0]
    scale = (QK_NOPE_DIM + QK_ROPE_DIM) ** -0.5
    g = {}

    hn = _rmsnorm_fwd(x, w["attn_norm_w"], name="attn_norm")
    proj = _matmul(hn, w["w_in"], mode="nt", name="in_proj")

    s5_weights = (w["ssm_lambda_re"], w["ssm_lambda_im"], w["ssm_log_dt"], w["ssm_b_re"], w["ssm_b_im"])
    wb, wct, abar = _s5_bands(*s5_weights, w["ssm_c_re"], w["ssm_c_im"])
    states, y_pre, yg = _s5_fwd(proj, wb, wct, w["ssm_d"], abar)
    later = hooks.mixer_weights(yg)
    z = _matmul(yg, later["ssm_w_glu"], mode="nn", name="glu_proj", bias=w["ssm_b_glu"])
    ys = _glu_norm_fwd(y_pre, z, w["ssm_out_norm_w"])

    q_col, kv_col, kpe_col = width // qr, (width + qr) // kvr, (width + qr + kvr) // LANES
    assert width % qr == 0 and (width + qr) % kvr == 0
    qn = _rmsnorm_fwd(proj, w["mla_q_norm_w"], name="q_norm", width=qr, col=q_col)
    kvn = _rmsnorm_fwd(proj, w["mla_kv_norm_w"], name="kv_norm", width=kvr, col=kv_col)
    q = _matmul(qn, w["mla_w_uq"], mode="nn", name="q_proj")
    kv = _matmul(kvn, w["mla_w_ukv"], mode="nn", name="kv_proj")
    half = QK_ROPE_DIM // 2
    inv_freq = ROPE_THETA ** (-jnp.arange(0, QK_ROPE_DIM, 2, dtype=F32) / QK_ROPE_DIM)
    zeros = jnp.zeros((LANES - QK_ROPE_DIM,), F32)
    freq = jnp.concatenate([inv_freq, inv_freq, zeros]).reshape(1, LANES)
    sign = jnp.concatenate([-jnp.ones((half,), F32), jnp.ones((half,), F32), zeros]).reshape(1, LANES)
    cos, sin = _rope_tables(posf, freq, sign)
    qc, kc, vb = _attn_prep(q, kv, proj, kpe_col, cos, sin)
    o, lse = _attn_fwd(qc, kc, vb, scale=scale, tq=ATTN_BLOCK)
    ym = _rmsnorm_fwd(o, w["mla_out_norm_w"], name="mla_out_norm")
    ycat = jnp.concatenate([ys, ym], axis=1)
    h1 = _matmul(ycat, later["w_out"], mode="nn", name="out_proj", add=x, after=hooks.ffn_weights_arrived(ycat))

    hn2 = _rmsnorm_fwd(h1, w["ffn_norm_w"], name="ffn_norm")
    ffn = hooks.ffn_weights(hn2)
    a = _matmul(hn2, ffn["ffn_w_up"], mode="nn", name="ffn_up", tm=FFN_ROWS)
    started = hooks.ffn_down_arrived(a)
    conv_b = w["ffn_conv_b"] if started is None else w["ffn_conv_b"] + started[:1, :1]
    gated = _conv_gate_fwd(a, ffn["ffn_conv_w"], conv_b)
    w_down = hooks.ffn_down_weight(gated)
    h2 = _matmul(gated, w_down, mode="nn", name="ffn_down", add=h1, tk=2816, tm=FFN_ROWS)
    loss_tile, dh2, dh2_mxu, g["final_norm_w"] = _final_norm_loss(h2, w["final_norm_w"], target)

    dgated = _matmul(dh2_mxu, w_down, mode="nt", name="ffn_down_dx", tm=FFN_ROWS)
    da, dcw, dcb = _conv_gate_bwd(a, ffn["ffn_conv_w"], w["ffn_conv_b"], dgated)
    g["ffn_conv_w"] = jnp.concatenate([dcw[0, :3], dcw[1, :3]], axis=1)
    g["ffn_conv_b"] = jnp.concatenate([dcb[0], dcb[1]], axis=1)
    started = hooks.ffn_grads({
        "ffn_w_up": _wgrad(hn2, da, d, f2, False, "ffn_up_dw", b_split=True, tm=FFN_ROWS, tn=_tile(f2 // N_CHIPS, 1408)),
        "ffn_w_down": _wgrad(gated, dh2_mxu, f2 // 2, d, True, "ffn_down_dw", tm=f2 // 2 // N_CHIPS, tn=1024)}, dcb)
    dhn2 = _matmul(da, ffn["ffn_w_up"], mode="nt", name="ffn_up_dx", a_split=True, tk=_tile(f2 // 2, 2816), tm=FFN_ROWS,
                   after=started)
    dh1, dh1_mxu, g["ffn_norm_w"] = _rmsnorm_bwd(h1, w["ffn_norm_w"], dhn2, name="ffn_norm_bwd", add=dh2,
                                                dx_dtypes=(F32, BF16))

    dycat = _matmul(dh1_mxu, later["w_out"], mode="nt", name="out_proj_dx")
    g["w_out"] = _wgrad(ycat, dh1_mxu, 2 * width, d, True, "out_proj_dw")
    started = hooks.ffn_backward_done(dycat)
    mla_out_norm_w, ssm_out_norm_w = w["mla_out_norm_w"], w["ssm_out_norm_w"]
    if started is not None:
        mla_out_norm_w, ssm_out_norm_w = mla_out_norm_w + started[:1, :1], ssm_out_norm_w + started[:1, :1]

    do, g["mla_out_norm_w"] = _rmsnorm_bwd(o, mla_out_norm_w, dycat, name="mla_out_norm_bwd", width=width, dy_col=1)
    dq, dkv, dkpe_heads = _attn_bwd(qc, kc, vb, o, do, lse, cos, sin, scale=scale, tk=ATTN_BLOCK)
    dkpe = _kpe_bwd(dkpe_heads, cos, sin)
    g["mla_w_uq"] = _wgrad(qn, dq, qr, heads * HEAD_SLOT, False, "q_proj_dw")
    dqn = _matmul(dq, w["mla_w_uq"], mode="nt", name="q_proj_dx")
    dcq, g["mla_q_norm_w"] = _rmsnorm_bwd(proj, w["mla_q_norm_w"], dqn, name="q_norm_bwd", width=qr, col=q_col,
                                          dx_dtypes=(BF16,))
    g["mla_w_ukv"] = _wgrad(kvn, dkv, kvr, heads * HEAD_SLOT, False, "kv_proj_dw")
    dkvn = _matmul(dkv, w["mla_w_ukv"], mode="nt", name="kv_proj_dx")
    dckv, g["mla_kv_norm_w"] = _rmsnorm_bwd(proj, w["mla_kv_norm_w"], dkvn, name="kv_norm_bwd", width=kvr, col=kv_col,
                                            dx_dtypes=(BF16,))

    dz, dyg_a, g["ssm_out_norm_w"], g["ssm_b_glu"] = _glu_norm_bwd(y_pre, z, ssm_out_norm_w, dycat)
    dyg_b = _matmul(dz, later["ssm_w_glu"], mode="nt", name="glu_proj_dx")
    g["ssm_w_glu"] = _wgrad(yg, dz, width, width, True, "glu_proj_dw")
    du, dwb, dwct, dabar, g["ssm_d"] = _s5_bwd(proj, states, y_pre, dyg_a, dyg_b, wb, wct, w["ssm_d"], abar)
    (g["ssm_lambda_re"], g["ssm_lambda_im"], g["ssm_log_dt"], g["ssm_b_re"], g["ssm_b_im"], g["ssm_c_re"],
     g["ssm_c_im"]) = _s5_bands_bwd(*s5_weights, dwb, dwct, dabar)

    pad = jnp.zeros((rows, inp - (width + qr + kvr + LANES)), BF16)
    dproj = jnp.concatenate([du, dcq, dckv, dkpe, pad], axis=1)
    g["w_in"] = _wgrad(dproj, hn, inp, d, False, "in_proj_dw")
    dhn = _matmul(dproj, w["w_in"], mode="nn", name="in_proj_dx")
    dx, g["attn_norm_w"] = _rmsnorm_bwd(x, w["attn_norm_w"], dhn, name="attn_norm_bwd", add=dh1)
    return loss_tile, dx, g


ANY = pl.BlockSpec(memory_space=pl.ANY)
MESH = pl.DeviceIdType.MESH


def _mesh_pos():
    return lax.axis_index("x"), lax.axis_index("y"), lax.axis_index("c")


def _other_chips(x, y):
    return [(1 - x, y), (x, 1 - y), (1 - x, 1 - y)]


def _remote(src, dst, send_sems, recv_sems, k, to):
    return pltpu.make_async_remote_copy(src_ref=src, dst_ref=dst, send_sem=send_sems.at[k], recv_sem=recv_sems.at[k],
                                        device_id=to, device_id_type=MESH)


def _place_shard(shard, piece_idx, row_sharded, name, out_dtype=BF16, pieces=N_CHIPS, after=None):
    rs, cs = shard.shape
    tr = _tile(rs, 512, 2 * SUBLANES)
    rb = rs // tr
    extra = [] if after is None else [after]

    def body(p_ref, x_ref, *rest):
        o_ref = rest[-1]
        o_ref[...] = x_ref[...].astype(o_ref.dtype)

    if row_sharded:
        out_shape, out_map = (pieces * rs, cs), (lambda i, p_ref: (p_ref[0] * rb + i, 0))
    else:
        out_shape, out_map = (rs, pieces * cs), (lambda i, p_ref: (i, p_ref[0]))
    return pl.pallas_call(
        body, name=name, out_shape=jax.ShapeDtypeStruct(out_shape, out_dtype),
        grid_spec=pltpu.PrefetchScalarGridSpec(
            num_scalar_prefetch=1, grid=(rb,),
            in_specs=[pl.BlockSpec((tr, cs), lambda i, p_ref: (i, 0))] + [pl.BlockSpec(memory_space=pl.ANY)] * len(extra),
            out_specs=pl.BlockSpec((tr, cs), out_map)),
        compiler_params=_params(("parallel",)),
    )(piece_idx, shard, *extra)


def _gather_weights(placed, name):
    n = len(placed)
    meta = [(row_sharded, direct) for _, row_sharded, direct in placed]
    over_ici, over_d2d = _gather_plans(meta)
    forwarded = [t for t, (_, direct) in enumerate(meta) if not direct]

    def body(*refs):
        outs = refs[n:2 * n]
        send_sems, recv_sems, pass_send_sems, pass_recv_sems = refs[2 * n:]
        first, arrivals = over_ici(outs, send_sems, recv_sems)
        passed, passed_arrivals = over_d2d([outs[t] for t in forwarded], pass_send_sems, pass_recv_sems)
        for cp in first:
            cp.start()
        for t in range(n):
            for j in range(3):
                arrivals[3 * t + j].wait_recv()
                if t in forwarded:
                    passed[3 * forwarded.index(t) + j].start()
        for cp in passed_arrivals:
            cp.wait_recv()
        for cp in first + passed:
            cp.wait_send()

    return pl.pallas_call(
        body, name=name, in_specs=[ANY] * n, out_specs=[ANY] * n,
        out_shape=[jax.ShapeDtypeStruct(arr.shape, arr.dtype) for arr, _, _ in placed],
        input_output_aliases={t: t for t in range(n)},
        scratch_shapes=[pltpu.SemaphoreType.DMA((3 * n,)), pltpu.SemaphoreType.DMA((3 * n,)),
                        pltpu.SemaphoreType.DMA((3 * len(forwarded),)), pltpu.SemaphoreType.DMA((3 * len(forwarded),))],
    )(*[arr for arr, _, _ in placed])


def _gather_plans(meta):
    def window(ref, row_sharded, piece, half):
        r, cc = ref.shape
        if row_sharded:
            rs = r // N_CHIPS
            if half is None:
                return ref.at[pl.ds(piece * rs, rs), :]
            return ref.at[pl.ds(piece * rs + half * (rs // 2), rs // 2), :]
        cs = cc // N_CHIPS
        if half is None:
            return ref.at[:, pl.ds(piece * cs, cs)]
        return ref.at[pl.ds(half * (r // 2), r // 2), pl.ds(piece * cs, cs)]

    def over_ici(refs, send_sems, recv_sems):
        x, y, c = _mesh_pos()
        sends, recvs = [], []
        for t, (row_sharded, direct) in enumerate(meta):
            mine = window(refs[t], row_sharded, 2 * x + y, None if direct else c)
            for j, (px, py) in enumerate(_other_chips(x, y)):
                theirs = window(refs[t], row_sharded, 2 * px + py, None if direct else c)
                sends.append(_remote(mine, mine, send_sems, recv_sems, 3 * t + j, (px, py, c)))
                recvs.append(_remote(theirs, theirs, send_sems, recv_sems, 3 * t + j, (px, py, c)))
        return sends, recvs

    def over_d2d(refs, send_sems, recv_sems):
        x, y, c = _mesh_pos()
        sends, recvs = [], []
        rows = [row_sharded for row_sharded, direct in meta if not direct]
        for t, row_sharded in enumerate(rows):
            for j, (px, py) in enumerate(_other_chips(x, y)):
                got = window(refs[t], row_sharded, 2 * px + py, c)
                other = window(refs[t], row_sharded, 2 * px + py, 1 - c)
                sends.append(_remote(got, got, send_sems, recv_sems, 3 * t + j, (x, y, 1 - c)))
                recvs.append(_remote(other, other, send_sems, recv_sems, 3 * t + j, (x, y, 1 - c)))
        return sends, recvs

    return over_ici, over_d2d


HBM = pl.BlockSpec(memory_space=pltpu.HBM)
SEMAPHORES = pl.BlockSpec(memory_space=pltpu.SEMAPHORE)
DATAFLOW = pltpu.SideEffectType.DATAFLOW_SIDE_EFFECTING


def _start_copies(name, arrays, plan, n_copies, after):
    n = len(arrays)

    def body(*refs):
        sends, _ = plan(refs[:n], refs[n + 1], refs[n + 2])
        for cp in sends:
            cp.start()
        token = refs[2 * n + 3]
        token[...] = jnp.zeros_like(token)

    out = pl.pallas_call(
        body, name=name,
        out_shape=(pltpu.SemaphoreType.DMA((n_copies,)), pltpu.SemaphoreType.DMA((n_copies,)),
                   *[pltpu.HBM(a.shape, a.dtype) for a in arrays], jax.ShapeDtypeStruct((SUBLANES, LANES), F32)),
        in_specs=[HBM] * n + [ANY],
        out_specs=(SEMAPHORES, SEMAPHORES, *[HBM] * n, pl.BlockSpec(memory_space=pltpu.VMEM)),
        input_output_aliases={t: t + 2 for t in range(n)},
        compiler_params=pltpu.CompilerParams(has_side_effects=DATAFLOW),
    )(*[pltpu.with_memory_space_constraint(a, pltpu.HBM) for a in arrays], after)
    return out[0], out[1], list(out[2:2 + n]), out[2 + n]


def _wait_copies(name, started, plan, after):
    send_sems, recv_sems, arrays, _ = started
    n = len(arrays)

    def body(*refs):
        sends, recvs = plan(refs[:n], refs[n], refs[n + 1])
        for cp in sends:
            cp.wait_send()
        for cp in recvs:
            cp.wait_recv()

    out = pl.pallas_call(
        body, name=name, out_shape=[pltpu.HBM(a.shape, a.dtype) for a in arrays],
        in_specs=[HBM] * n + [SEMAPHORES, SEMAPHORES, ANY], out_specs=[HBM] * n,
        input_output_aliases={t: t for t in range(n)},
        compiler_params=pltpu.CompilerParams(has_side_effects=DATAFLOW),
    )(*arrays, send_sems, recv_sems, after)
    return list(out)


def _exchange(name, arrays, plan, n_copies, after=None):
    n = len(arrays)
    extra = [] if after is None else [after]

    def body(*refs):
        outs = refs[n + len(extra):2 * n + len(extra)]
        send_sems, recv_sems = refs[2 * n + len(extra):]
        sends, recvs = plan(outs, send_sems, recv_sems)
        for cp in sends:
            cp.start()
        for cp in recvs:
            cp.wait_recv()
        for cp in sends:
            cp.wait_send()

    return pl.pallas_call(
        body, name=name, in_specs=[ANY] * (n + len(extra)), out_specs=[ANY] * n,
        out_shape=[jax.ShapeDtypeStruct(a.shape, a.dtype) for a in arrays],
        input_output_aliases={t: t for t in range(n)},
        scratch_shapes=[pltpu.SemaphoreType.DMA((n_copies,)), pltpu.SemaphoreType.DMA((n_copies,))],
    )(*arrays, *extra)


def _give_plan(n):
    def plan(refs, send_sems, recv_sems):
        x, y, c = _mesh_pos()
        sends = [_remote(refs[t], refs[n + t], send_sems, recv_sems, t, (x, y, 1 - c)) for t in range(n)]
        return sends, sends

    return plan


def _scatter_plan(n):
    def plan(refs, send_sems, recv_sems):
        x, y, c = _mesh_pos()
        sends = []
        for t in range(n):
            for j, (px, py) in enumerate(_other_chips(x, y)):
                sends.append(_remote(refs[t].at[2 * px + py], refs[n + t].at[j], send_sems, recv_sems, 3 * t + j, (px, py, c)))
        return sends, sends

    return plan


def _scatter_shapes(sums):
    return [jax.ShapeDtypeStruct((3,) + s.shape[1:], s.dtype) for s in sums]


def _join_plan(n):
    def plan(refs, send_sems, recv_sems):
        x, y, c = _mesh_pos()
        sends = [_remote(refs[t].at[c], refs[t].at[c], send_sems, recv_sems, t, (x, y, 1 - c)) for t in range(n)]
        recvs = [_remote(refs[t].at[1 - c], refs[t].at[1 - c], send_sems, recv_sems, t, (x, y, 1 - c)) for t in range(n)]
        return sends, recvs

    return plan


def _join_halves(halves, name, after=None):
    return _exchange(name, halves, _join_plan(len(halves)), len(halves), after=after)


def _add_other_half(g4, got, where, name):
    _, pieces, sr, sc = g4.shape
    tr = _tile(sr, 512, 2 * SUBLANES)

    def body(w_ref, a_ref, b_ref, o_ref):
        o_ref[...] = a_ref[...] + b_ref[...]

    blk = pl.BlockSpec((None, tr, sc), lambda p, i, w_ref: (p, i, 0))
    return pl.pallas_call(
        body, name=name, out_shape=jax.ShapeDtypeStruct((pieces, sr, sc), F32),
        grid_spec=pltpu.PrefetchScalarGridSpec(
            num_scalar_prefetch=1, grid=(pieces, sr // tr),
            in_specs=[pl.BlockSpec((None, None, tr, sc), lambda p, i, w_ref: (w_ref[0], p, i, 0)), blk], out_specs=blk),
        compiler_params=_params(("parallel", "parallel")),
    )(where, g4, got)


def _add_pieces(sums, got_pieces, where, name, after=None):
    _, sr, sc = sums.shape
    tr = _tile(sr, 512, 2 * SUBLANES)
    extra = [] if after is None else [after]

    def body(w_ref, a_ref, r_ref, *rest):
        acc = a_ref[...]
        for j in range(3):
            acc = acc + r_ref[j].astype(F32)
        rest[-1][...] = acc

    return pl.pallas_call(
        body, name=name, out_shape=jax.ShapeDtypeStruct((N_CORES, sr, sc), F32),
        grid_spec=pltpu.PrefetchScalarGridSpec(
            num_scalar_prefetch=1, grid=(sr // tr,),
            in_specs=[pl.BlockSpec((None, tr, sc), lambda i, w_ref: (w_ref[1], i, 0)),
                      pl.BlockSpec((3, tr, sc), lambda i, w_ref: (0, i, 0))] + [pl.BlockSpec(memory_space=pl.ANY)] * len(extra),
            out_specs=pl.BlockSpec((None, tr, sc), lambda i, w_ref: (w_ref[0], i, 0))),
        compiler_params=_params(("parallel",)),
    )(where, sums, got_pieces, *extra)


def _adamw_update(w, g, m, v):
    nm = ADAM_B1 * m + (1.0 - ADAM_B1) * g
    nv = ADAM_B2 * v + (1.0 - ADAM_B2) * (g * g)
    m_hat = nm / (1.0 - ADAM_B1 ** ADAM_STEP)
    v_hat = nv / (1.0 - ADAM_B2 ** ADAM_STEP)
    return -ADAM_LR * (m_hat / (jnp.sqrt(v_hat) + ADAM_EPS) + ADAM_WD * w), nm, nv


def _adamw(w, g, m, v, name, after=None):
    rows, cols = w.shape
    halves = 2 if g.ndim == 3 else 1
    bc = cols // halves
    tr = _tile(rows, max(SUBLANES, (1 << 20) // max(bc, 1) // SUBLANES * SUBLANES), SUBLANES)

    def body(w_ref, g_ref, m_ref, v_ref, *rest):
        d_ref, nm_ref, nv_ref, go_ref = rest[-4:]
        gv = g_ref[...]
        d_ref[...], nm_ref[...], nv_ref[...] = _adamw_update(w_ref[...], gv, m_ref[...], v_ref[...])
        go_ref[...] = gv

    blk = pl.BlockSpec((tr, bc), lambda i, h: (i, h))
    g_blk = pl.BlockSpec((None, tr, bc), lambda i, h: (h, i, 0)) if halves == 2 else blk
    extra = [] if after is None else [after]
    return pl.pallas_call(
        body, name=name, grid=(rows // tr, halves),
        in_specs=[blk, g_blk, blk, blk] + [pl.BlockSpec(memory_space=pl.ANY)] * len(extra), out_specs=[blk] * 4,
        out_shape=[jax.ShapeDtypeStruct((rows, cols), F32)] * 4, compiler_params=_params(("parallel", "parallel")),
    )(w, g, m, v, *extra)


def _adamw_many(ws, gs, ms, vs, name):
    n = len(ws)

    def body(*refs):
        outs = refs[4 * n:]
        for k in range(n):
            w_ref, g_ref, m_ref, v_ref = (refs[j * n + k] for j in range(4))
            outs[k][...], outs[n + k][...], outs[2 * n + k][...] = _adamw_update(w_ref[...], g_ref[...], m_ref[...], v_ref[...])

    out = pl.pallas_call(
        body, name=name, out_shape=[jax.ShapeDtypeStruct(w.shape, F32) for w in ws] * 3,
        compiler_params=pltpu.CompilerParams(vmem_limit_bytes=VMEM_LIMIT_BYTES),
    )(*ws, *gs, *ms, *vs)
    return out[:n], out[n:2 * n], out[2 * n:]


WEIGHTS = ['attn_norm_w', 'w_in', 'ssm_lambda_re', 'ssm_lambda_im', 'ssm_log_dt', 'ssm_b_re', 'ssm_b_im', 'ssm_c_re',
           'ssm_c_im', 'ssm_d', 'ssm_w_glu', 'ssm_b_glu', 'mla_q_norm_w', 'mla_w_uq', 'mla_kv_norm_w', 'mla_w_ukv',
           'ssm_out_norm_w', 'mla_out_norm_w', 'w_out', 'ffn_norm_w', 'ffn_w_up', 'ffn_conv_w', 'ffn_conv_b',
           'ffn_w_down', 'final_norm_w']
SHARDED = {'w_in': False, 'ssm_w_glu': True, 'mla_w_uq': False, 'mla_w_ukv': False, 'w_out': True, 'ffn_w_up': False,
           'ffn_w_down': True}
SMALL = [n for n in WEIGHTS if n not in SHARDED and n != 'ffn_conv_w']
ROPE_PAD = HEAD_SLOT - QK_NOPE_DIM - QK_ROPE_DIM
SMALL_COLS = 8 * LANES


def _pad_heads(w_uq, heads):
    qr = w_uq.shape[0]
    w3 = w_uq.reshape(qr, heads, QK_NOPE_DIM + QK_ROPE_DIM)
    return jnp.concatenate([w3, jnp.zeros((qr, heads, ROPE_PAD), w_uq.dtype)], axis=2).reshape(qr, heads * HEAD_SLOT)


def _unpad_heads(g_uq, heads):
    qr = g_uq.shape[0]
    return g_uq.reshape(qr, heads, HEAD_SLOT)[:, :, :QK_NOPE_DIM + QK_ROPE_DIM].reshape(qr, -1)


FFN = ['ffn_w_up', 'ffn_w_down']
MIXER_LATER = ['ssm_w_glu', 'w_out']
MIXER_BIG = ['w_in', 'w_out']
FFN_GATHER = FFN + ['ffn_conv_w']


class _Overlapped:
    def __init__(self, placed_first, first_sharding, where):
        self.where, self.mine, self.other = where, where[:1], 1 - where[:1]
        self.first_ici, self.first_d2d = _gather_plans([(r, False) for r in first_sharding])
        self.first = _start_copies("gather_first_start", placed_first, self.first_ici, 3 * len(placed_first), where)
        self.first_started = self.first[3]

    def start_rest(self, placed_later, placed):
        self.later_ici, self.later_d2d = _gather_plans([(SHARDED[n], False) for n in MIXER_LATER])
        self.later = _start_copies("gather_later_start", placed_later, self.later_ici, 3 * len(placed_later),
                                   self.first_started)
        up, down, taps = placed
        self.up_ici, self.up_d2d = _gather_plans([(SHARDED["ffn_w_up"], False), (False, True)])
        self.up = _start_copies("gather_ffn_up_start", [up, taps], self.up_ici, 6, self.later[3])
        self.down_ici, self.down_d2d = _gather_plans([(SHARDED["ffn_w_down"], False)])
        self.down = _start_copies("gather_ffn_down_start", [down], self.down_ici, 3, self.up[3])
        self.gather_started = self.down[3]
        arrived = _wait_copies("gather_first_wait", self.first, self.first_ici, self.gather_started)
        return _exchange("gather_first_pass", arrived, self.first_d2d, 3 * len(arrived))

    def mixer_weights(self, after):
        arrived = _wait_copies("gather_later_wait", self.later, self.later_ici, after)
        return dict(zip(MIXER_LATER, _exchange("gather_later_pass", arrived, self.later_d2d, 3 * len(arrived))))

    def ffn_weights_arrived(self, after):
        up, self.taps = _wait_copies("gather_ffn_up_wait", self.up, self.up_ici, after)
        self.up_passing = _start_copies("gather_ffn_up_pass_start", [up], self.up_d2d, 3, after)
        return self.up_passing[3]

    def ffn_weights(self, after):
        w_up, = _wait_copies("gather_ffn_up_pass_wait", self.up_passing, self.up_d2d, after)
        return {"ffn_w_up": w_up, "ffn_conv_w": self.taps}

    def ffn_down_arrived(self, after):
        down, = _wait_copies("gather_ffn_down_wait", self.down, self.down_ici, after)
        self.down_passing = _start_copies("gather_ffn_down_pass_start", [down], self.down_d2d, 3, after)
        return self.down_passing[3]

    def ffn_down_weight(self, after):
        return _wait_copies("gather_ffn_down_pass_wait", self.down_passing, self.down_d2d, after)[0]

    def ffn_grads(self, makers, after):
        self.makers = [makers[name] for name in FFN]
        n = len(FFN)
        give = [make(self.other, suffix="_give") for make in self.makers]
        lands = [lax.empty(g.shape, g.dtype) for g in give]
        self.swap = _start_copies("grad_ffn_swap_start", give + lands, _give_plan(n), n, after)
        return self.swap[3]

    def ffn_backward_done(self, after):
        n = len(FFN)
        got = _wait_copies("grad_ffn_swap_wait", self.swap, _give_plan(n), after)[n:]
        kept = [make(self.mine, suffix="_keep", add=got[t], wire=True) for t, make in enumerate(self.makers)]
        self.sums = [k[0] for k in kept]
        wires = [k[1] for k in kept]
        lands = [lax.empty(s.shape, s.dtype) for s in _scatter_shapes(wires)]
        self.scatter = _start_copies("grad_ffn_scatter_start", wires + lands, _scatter_plan(n), 3 * n, after)
        return self.scatter[3]

    def ffn_reduced(self, after):
        n = len(FFN)
        got_pieces = _wait_copies("grad_ffn_scatter_wait", self.scatter, _scatter_plan(n), after)[n:]
        halves = []
        for t, name in enumerate(FFN):
            halves.append(_add_pieces(self.sums[t], got_pieces[t], self.where, "grad_add_pieces_" + name,
                                      after=halves[-1] if halves else None))
        return halves


def _step(args):
    x, positions, target = args["x"][0], args["positions"], args["loss_target"][0]
    rows = x.shape[0]
    p = {n: args[n] for n in WEIGHTS}
    xi, yi, ci = _mesh_pos()
    piece = 2 * xi + yi

    def transposed(a):
        return jnp.swapaxes(a[0], 0, 1)

    def as_stored(n, a):
        return jnp.swapaxes(a, 2, 3) if n in ("ssm_b_re", "ssm_b_im") else a

    w_in = transposed(p["w_in"])
    in_width = w_in.shape[0]
    in_pad = (-in_width) % (2 * LANES)
    heads_here = p["mla_w_uq"].shape[2] // (QK_NOPE_DIM + QK_ROPE_DIM)
    shards = {
        "w_in": jnp.pad(w_in, ((0, in_pad), (0, 0))),
        "ssm_w_glu": p["ssm_w_glu"][0],
        "mla_w_uq": _pad_heads(p["mla_w_uq"][0], heads_here),
        "mla_w_ukv": p["mla_w_ukv"][0],
        "w_out": p["w_out"][0],
        "ffn_w_up": p["ffn_w_up"][0],
        "ffn_w_down": p["ffn_w_down"][0],
    }
    conv_w = jnp.pad(p["ffn_conv_w"][0], ((0, SUBLANES - p["ffn_conv_w"].shape[1]), (0, 0)))
    order = list(SHARDED)
    piece_idx = piece.reshape(1).astype(jnp.int32)
    mixer = [n for n in order if n not in FFN]
    first = [n for n in mixer if n not in MIXER_LATER]
    where = jnp.stack([ci, piece]).astype(jnp.int32)
    placed = {n: _place_shard(shards[n], piece_idx, SHARDED[n], "place_" + n) for n in first}
    hooks = _Overlapped([placed[n] for n in first], [SHARDED[n] for n in first], where)
    for n in order:
        if n not in first:
            placed[n] = _place_shard(shards[n], piece_idx, SHARDED[n], "place_" + n, after=hooks.first_started)
    placed["ffn_conv_w"] = _place_shard(conv_w, piece_idx, False, "place_ffn_conv_w", out_dtype=F32,
                                        after=hooks.first_started)
    w = dict(zip(first, hooks.start_rest([placed[n] for n in MIXER_LATER], [placed[n] for n in FFN_GATHER])))
    groups = p["ssm_lambda_re"].shape[1]
    w.update({
        "attn_norm_w": p["attn_norm_w"] + hooks.gather_started[:1, :1],
        "ssm_lambda_re": p["ssm_lambda_re"][0], "ssm_lambda_im": p["ssm_lambda_im"][0],
        "ssm_log_dt": p["ssm_log_dt"].reshape(groups, 1), "ssm_b_re": as_stored("ssm_b_re", p["ssm_b_re"])[0],
        "ssm_b_im": as_stored("ssm_b_im", p["ssm_b_im"])[0], "ssm_c_re": p["ssm_c_re"][0], "ssm_c_im": p["ssm_c_im"][0],
        "ssm_d": p["ssm_d"], "ssm_b_glu": p["ssm_b_glu"], "mla_q_norm_w": p["mla_q_norm_w"],
        "mla_kv_norm_w": p["mla_kv_norm_w"], "ssm_out_norm_w": p["ssm_out_norm_w"], "mla_out_norm_w": p["mla_out_norm_w"],
        "ffn_norm_w": p["ffn_norm_w"], "ffn_conv_b": p["ffn_conv_b"], "final_norm_w": p["final_norm_w"].reshape(1, -1),
    })

    loss_tile, dx, g = _local_step(x, positions.reshape(rows, 1).astype(F32), target, w, hooks)

    flat = [g[n].reshape(-1) for n in SMALL] + [g["ffn_conv_w"].reshape(-1), loss_tile[0, :1]]
    sizes = [f.shape[0] for f in flat]
    per_block = -(-sum(sizes) // (N_CORES * N_CHIPS * SMALL_COLS))
    small_rows = -(-per_block // (2 * SUBLANES)) * (2 * SUBLANES)
    padded = N_CORES * N_CHIPS * small_rows * SMALL_COLS

    def pack(parts):
        parts = list(parts)
        have = sum(q.shape[0] for q in parts)
        return jnp.concatenate(parts + [jnp.zeros((padded - have,), F32)])

    reduced = mixer + ["small"]
    small = pack(flat).reshape(N_CORES, N_CHIPS, small_rows, SMALL_COLS)
    give = [g[n](hooks.other, suffix="_give") for n in mixer] + [lax.dynamic_index_in_dim(small, 1 - ci, 0, keepdims=False)]
    lands = [lax.empty(a.shape, a.dtype) for a in give]
    give_plan = _give_plan(len(reduced))
    swap = _start_copies("grad_mixer_swap_start", give + lands, give_plan, len(reduced), dx)

    grads, delta, new_m, new_v = {}, {}, {}, {}

    def finish(n, joined, after=None):
        grad = joined if SHARDED[n] else joined.reshape(-1, joined.shape[2])
        if n == "w_in":
            wt, mt, vt = w_in, transposed(args["m_w_in"]), transposed(args["v_w_in"])
            out = _adamw(wt, grad, mt, vt, "adamw_w_in")
            delta[n], new_m[n], new_v[n], grads[n] = (jnp.swapaxes(a, 0, 1)[None] for a in out)
            return
        if n == "mla_w_uq":
            grad = _unpad_heads(grad, heads_here)
        adam(n, grad, after)

    def adam(n, grad, after=None):
        shape = p[n].shape
        out = _adamw(p[n].reshape(shape[1:]), grad, args["m_" + n].reshape(shape[1:]),
                     args["v_" + n].reshape(shape[1:]), "adamw_" + n, after)
        delta[n], new_m[n], new_v[n], grads[n] = (a.reshape(shape) for a in out)

    ffn_halves = hooks.ffn_reduced(swap[3])
    got = _wait_copies("grad_mixer_swap_wait", swap, give_plan, ffn_halves[-1])[len(reduced):]
    join_plan = _join_plan(len(FFN))
    ffn_join = _start_copies("grad_ffn_join_start", ffn_halves, join_plan, len(FFN), got[0])
    big = [t for t, n in enumerate(reduced) if n in MIXER_BIG]
    rest = [t for t in range(len(reduced)) if t not in big]
    sums, wires = {}, {}
    for t in big:
        sums[t], wires[t] = g[reduced[t]](hooks.mine, suffix="_keep", add=got[t], wire=True)
    ffn_joined = _wait_copies("grad_ffn_join_wait", ffn_join, join_plan, sums[big[-1]])

    def scatter_start(name, group, after):
        lands = [lax.empty(s.shape, s.dtype) for s in _scatter_shapes([wires[t] for t in group])]
        return _start_copies(name, [wires[t] for t in group] + lands, _scatter_plan(len(group)), 3 * len(group), after)

    scatter_big = scatter_start("grad_big_scatter_start", big, ffn_joined[0])
    for t in rest[:-1]:
        sums[t], wires[t] = g[reduced[t]](hooks.mine, suffix="_keep", add=got[t], wire=True, after=scatter_big[3])
    sums[rest[-1]] = wires[rest[-1]] = _add_other_half(small, got[-1], where, "grad_add_half_small")
    scatter_rest = scatter_start("grad_rest_scatter_start", rest, sums[rest[0]])
    behind = scatter_rest[3]
    for n, joined in zip(FFN, ffn_joined):
        finish(n, joined, after=behind)
        behind = delta[n]
    got_pieces = dict(zip(big, _wait_copies("grad_big_scatter_wait", scatter_big, _scatter_plan(len(big)),
                                            delta[FFN[-1]])[len(big):]))
    got_pieces.update(zip(rest, _wait_copies("grad_rest_scatter_wait", scatter_rest, _scatter_plan(len(rest)),
                                             got_pieces[big[0]])[len(rest):]))
    halves = [_add_pieces(sums[t], got_pieces[t], where, "grad_add_pieces_" + n) for t, n in enumerate(reduced)]
    joined = _join_halves(halves, "grad_join_halves")
    for n, j in zip(mixer, joined):
        finish(n, j)
    eighths = _place_shard(joined[-1].reshape(N_CORES * small_rows, SMALL_COLS), piece_idx, True, "place_small_grads",
                           out_dtype=F32)
    small_sum = _gather_weights([(eighths, True, False)], "gather_small_grads")[0]
    flat_sum = small_sum.reshape(N_CHIPS, N_CORES, small_rows * SMALL_COLS).transpose(1, 0, 2).reshape(-1)
    offs = [0]
    for s in sizes:
        offs.append(offs[-1] + s)
    stored = {n: as_stored(n, p[n]) for n in SMALL}
    for k, n in enumerate(SMALL):
        grads[n] = flat_sum[offs[k]:offs[k + 1]].reshape(stored[n].shape)
    taps, cols_here = p["ffn_conv_w"].shape[1], p["ffn_conv_w"].shape[2]
    conv_full = flat_sum[offs[len(SMALL)]:offs[len(SMALL) + 1]].reshape(taps, N_CHIPS * cols_here)
    adam("ffn_conv_w", lax.dynamic_slice_in_dim(conv_full, piece * cols_here, cols_here, axis=1))
    loss = flat_sum[offs[len(SMALL) + 1]]

    def rank2(a):
        return a.reshape(1, -1) if a.ndim == 1 else a

    d_s, m_s, v_s = _adamw_many([rank2(stored[n]) for n in SMALL], [rank2(grads[n]) for n in SMALL],
                                [rank2(as_stored(n, args["m_" + n])) for n in SMALL],
                                [rank2(as_stored(n, args["v_" + n])) for n in SMALL], "adamw_small")
    for k, n in enumerate(SMALL):
        delta[n], new_m[n], new_v[n], grads[n] = (as_stored(n, a.reshape(stored[n].shape))
                                                  for a in (d_s[k], m_s[k], v_s[k], grads[n]))

    return (loss, dx[None], *[grads[n] for n in WEIGHTS], *[delta[n] for n in WEIGHTS],
            *[new_m[n] for n in WEIGHTS], *[new_v[n] for n in WEIGHTS])


def kernel(x, positions, attn_norm_w, w_in, ssm_lambda_re, ssm_lambda_im, ssm_log_dt, ssm_b_re, ssm_b_im, ssm_c_re, ssm_c_im, ssm_d, ssm_w_glu, ssm_b_glu, mla_q_norm_w, mla_w_uq, mla_kv_norm_w, mla_w_ukv, ssm_out_norm_w, mla_out_norm_w, w_out, ffn_norm_w, ffn_w_up, ffn_conv_w, ffn_conv_b, ffn_w_down, final_norm_w, loss_target, m_attn_norm_w, m_w_in, m_ssm_lambda_re, m_ssm_lambda_im, m_ssm_log_dt, m_ssm_b_re, m_ssm_b_im, m_ssm_c_re, m_ssm_c_im, m_ssm_d, m_ssm_w_glu, m_ssm_b_glu, m_mla_q_norm_w, m_mla_w_uq, m_mla_kv_norm_w, m_mla_w_ukv, m_ssm_out_norm_w, m_mla_out_norm_w, m_w_out, m_ffn_norm_w, m_ffn_w_up, m_ffn_conv_w, m_ffn_conv_b, m_ffn_w_down, m_final_norm_w, v_attn_norm_w, v_w_in, v_ssm_lambda_re, v_ssm_lambda_im, v_ssm_log_dt, v_ssm_b_re, v_ssm_b_im, v_ssm_c_re, v_ssm_c_im, v_ssm_d, v_ssm_w_glu, v_ssm_b_glu, v_mla_q_norm_w, v_mla_w_uq, v_mla_kv_norm_w, v_mla_w_ukv, v_ssm_out_norm_w, v_mla_out_norm_w, v_w_out, v_ffn_norm_w, v_ffn_w_up, v_ffn_conv_w, v_ffn_conv_b, v_ffn_w_down, v_final_norm_w):
    return _step(dict(locals()))
```
